```python
import math
import jax, jax.numpy as jnp
from jax import lax
import numpy as np

D_MODEL = 1024
BATCH = 16
SEQ = 2048
DEPTH = 1

CTX_LEN = 256
GRID_W = 64
DN_HEADS = 8
DN_HEAD_DIM = 128
DN_WIDTH = DN_HEADS * DN_HEAD_DIM
DN_CONV = 4
CHUNK = 64
LRU_WIDTH = 1024
LRU_BLOCKS = 16
LRU_BLOCK_DIM = LRU_WIDTH // LRU_BLOCKS
LRU_CONV = 4
LRU_C = 8.0
N_DIRS = 2
N_BRANCH = 2
D_FF = 4 * D_MODEL
FFN_CONV = 3
EPS = 1e-6
IN_SIZES = (DN_WIDTH, DN_WIDTH, DN_WIDTH, DN_WIDTH, N_DIRS * DN_HEADS, N_DIRS * DN_HEADS,
            LRU_WIDTH, LRU_WIDTH, N_BRANCH * D_MODEL)
D_IN = sum(IN_SIZES)

kernel_name = "hybrid_deltanet_rglru_convffn_dit"


def rmsnorm(u, gain):
    u32 = u.astype(jnp.float32)
    y = u32 * lax.rsqrt(jnp.mean(u32 * u32, axis=-1, keepdims=True) + EPS)
    return (y * gain.astype(jnp.float32)).astype(u.dtype)


def l2norm(u):
    return u * lax.rsqrt(jnp.sum(u * u, axis=-1, keepdims=True) + EPS)


def modulate(u, shift_c, scale_c, shift_x, scale_x, n_ctx):
    return jnp.concatenate([u[:, :n_ctx] * (1 + scale_c) + shift_c,
                            u[:, n_ctx:] * (1 + scale_x[:, None]) + shift_x[:, None]], axis=1)


def gate_rows(v, gate_c, gate_x, n_ctx):
    return jnp.concatenate([v[:, :n_ctx] * gate_c, v[:, n_ctx:] * gate_x[:, None]], axis=1)


def dwconv1d(u, w, pad_left):
    K, C = w.shape
    return lax.conv_general_dilated(u, w[:, None, :].astype(u.dtype), (1,), [(pad_left, K - 1 - pad_left)],
                                    dimension_numbers=('NWC', 'WIO', 'NWC'), feature_group_count=C)


def seg_dwconv1d(u, w, n_ctx, pad_left):
    return jnp.concatenate([dwconv1d(u[:, :n_ctx], w, pad_left), dwconv1d(u[:, n_ctx:], w, pad_left)], axis=1)


def ffn_dwconv(u, w, n_ctx, rows):
    B, T, C = u.shape
    lat = lax.conv_general_dilated(u[:, n_ctx:].reshape(B, rows, GRID_W, C), w[:, :, None, :].astype(u.dtype),
                                   (1, 1), 'SAME', dimension_numbers=('NHWC', 'HWIO', 'NHWC'),
                                   feature_group_count=C).reshape(B, T - n_ctx, C)
    if n_ctx == 0:
        return lat
    ctx_part = dwconv1d(u[:, :n_ctx], w[FFN_CONV // 2], FFN_CONV // 2)
    return jnp.concatenate([ctx_part, lat], axis=1)


def seg_flip(u, n_ctx):
    return jnp.concatenate([jnp.flip(u[:, :n_ctx], axis=1), jnp.flip(u[:, n_ctx:], axis=1)], axis=1)


def to_dirs(u_fwd, u_bwd, n_ctx):
    return jnp.stack([u_fwd, seg_flip(u_bwd, n_ctx)])


def from_dirs(o, n_ctx):
    return o[0] + seg_flip(o[1], n_ctx)


def delta_rule_chunked(q, k, v, g, beta):
    *lead, T, dk = q.shape
    dv = v.shape[-1]
    n = T // CHUNK
    q = q.reshape(*lead, n, CHUNK, dk)
    k = k.reshape(*lead, n, CHUNK, dk)
    v = v.reshape(*lead, n, CHUNK, dv)
    beta = beta.reshape(*lead, n, CHUNK)
    g = jnp.cumsum(g.reshape(*lead, n, CHUNK), axis=-1)
    incl = jnp.tril(jnp.ones((CHUNK, CHUNK), bool))
    strict = jnp.tril(jnp.ones((CHUNK, CHUNK), bool), -1)
    decay = jnp.exp(jnp.where(incl, g[..., :, None] - g[..., None, :], -jnp.inf))
    k_beta = k * beta[..., None]
    lower = jnp.where(strict, jnp.einsum('...ik,...jk->...ij', k_beta, k) * decay, 0.0)
    eye = jnp.eye(CHUNK, dtype=q.dtype)
    t_inv = lax.linalg.triangular_solve(eye + lower, jnp.broadcast_to(eye, lower.shape),
                                        left_side=True, lower=True, unit_diagonal=True)
    w = t_inv @ (k_beta * jnp.exp(g)[..., None])
    u = t_inv @ (v * beta[..., None])
    attn = jnp.where(incl, jnp.einsum('...ik,...jk->...ij', q, k) * decay, 0.0)

    def step(state, xs):
        qc, kc, uc, wc, ac, gc = xs
        v_new = uc - wc @ state
        out = (qc * jnp.exp(gc)[..., None]) @ state + ac @ v_new
        g_last = gc[..., -1:]
        state = state * jnp.exp(g_last)[..., None] + jnp.einsum(
            '...ck,...cv->...kv', kc * jnp.exp(g_last - gc)[..., None], v_new)
        return state, out

    xs = tuple(jnp.moveaxis(t, len(lead), 0) for t in (q, k, u, w, attn, g))
    state0 = jnp.zeros((*lead, dk, dv), q.dtype)
    _, out = lax.scan(step, state0, xs)
    return jnp.moveaxis(out, 0, len(lead)).reshape(*lead, T, dv)


def gated_deltanet(q, k, v, z, a, b, w_conv, a_log, dt_bias, onorm, n_ctx):
    B, T, _ = q.shape
    out_dtype = q.dtype
    f32 = jnp.float32
    qkv = jax.nn.silu(seg_dwconv1d(jnp.concatenate([q, k, v], axis=-1), w_conv, n_ctx, DN_CONV // 2))
    q, k, v = jnp.split(qkv.astype(f32), 3, axis=-1)
    heads = lambda t: t.reshape(B, T, DN_HEADS, DN_HEAD_DIM)
    q = l2norm(heads(q)) * DN_HEAD_DIM ** -0.5
    k = l2norm(heads(k))
    v = heads(v)
    g = -jnp.exp(a_log.astype(f32)) * jax.nn.softplus(a.astype(f32) + dt_bias.astype(f32))
    beta = jax.nn.sigmoid(b.astype(f32))
    tth = lambda t: jnp.moveaxis(t, 2, 3)
    o = delta_rule_chunked(tth(to_dirs(q, q, n_ctx)), tth(to_dirs(k, k, n_ctx)), tth(to_dirs(v, v, n_ctx)),
                           tth(to_dirs(g[:, :, 0], g[:, :, 1], n_ctx)),
                           tth(to_dirs(beta[:, :, 0], beta[:, :, 1], n_ctx)))
    o = from_dirs(jnp.moveaxis(o, 3, 2), n_ctx)
    o = rmsnorm(o, onorm) * jax.nn.silu(heads(z).astype(f32))
    return o.reshape(B, T, DN_WIDTH).astype(out_dtype)


def rg_lru(xb, yb, w_conv, b_conv, w_rg, b_rg, w_ig, b_ig, lam, n_ctx):
    B, T, W = xb.shape
    out_dtype = xb.dtype
    f32 = jnp.float32
    xc = (seg_dwconv1d(xb, w_conv, n_ctx, LRU_CONV // 2) + b_conv).astype(f32)
    xd = to_dirs(xc, xc, n_ctx)
    blocks = xd.reshape(N_DIRS, B, T, LRU_BLOCKS, LRU_BLOCK_DIM)

    def gate(w, bias):
        y = jnp.einsum('dbthi,dhij->dbthj', blocks, w.astype(f32)).reshape(N_DIRS, B, T, W)
        return jax.nn.sigmoid(y + bias.astype(f32)[:, None, None])

    r = gate(w_rg, b_rg)
    i = gate(w_ig, b_ig)
    log_a = -LRU_C * r * jax.nn.softplus(-lam.astype(f32))[:, None, None]
    a = jnp.exp(log_a)
    inp = jnp.sqrt(-jnp.expm1(2.0 * log_a)) * (i * xd)
    _, h = lax.associative_scan(lambda e, l: (e[0] * l[0], l[0] * e[1] + l[1]), (a, inp), axis=2)
    h = from_dirs(h, n_ctx)
    return (h * jax.nn.gelu(yb.astype(f32))).astype(out_dtype)


def _fwd_setup_inputs(seed: int = 0) -> dict:
    key = jax.random.key(seed)
    ks = jax.random.split(key, 32)
    f32 = jnp.float32
    L = DEPTH
    nrm = lambda k, shape, scale: jax.random.normal(k, shape, f32) * scale
    x = nrm(ks[0], (BATCH, SEQ, D_MODEL), 1.0)
    c = nrm(ks[1], (BATCH, D_MODEL), 1.0)
    ctx = nrm(ks[2], (BATCH, CTX_LEN, D_MODEL), 1.0)
    c_ctx = nrm(ks[3], (D_MODEL,), 1.0)
    w_ada = nrm(ks[4], (L, D_MODEL, 6 * D_MODEL), 0.5 * D_MODEL ** -0.5)
    b_ada = nrm(ks[5], (L, 6 * D_MODEL), 0.02)
    g_pre_mix = 1.0 + nrm(ks[6], (L, D_MODEL), 0.05)
    g_post_mix = 1.0 + nrm(ks[7], (L, D_MODEL), 0.05)
    g_pre_ffn = 1.0 + nrm(ks[8], (L, D_MODEL), 0.05)
    g_post_ffn = 1.0 + nrm(ks[9], (L, D_MODEL), 0.05)
    w_in = nrm(ks[10], (L, D_MODEL, D_IN), D_MODEL ** -0.5)
    b_merge = nrm(ks[11], (L, N_BRANCH * D_MODEL), 0.02)
    dn_conv = nrm(ks[12], (L, DN_CONV, 3 * DN_WIDTH), DN_CONV ** -0.5)
    dn_a_log = jnp.log(jax.random.uniform(ks[13], (L, N_DIRS, DN_HEADS), f32, 1.0, 16.0))
    dt = jnp.exp(jax.random.uniform(ks[14], (L, N_DIRS, DN_HEADS), f32, math.log(1e-3), math.log(1e-1)))
    dn_dt_bias = dt + jnp.log(-jnp.expm1(-dt))
    dn_onorm = 1.0 + nrm(ks[15], (L, DN_HEAD_DIM), 0.05)
    lru_conv = nrm(ks[16], (L, LRU_CONV, LRU_WIDTH), LRU_CONV ** -0.5)
    lru_conv_b = nrm(ks[17], (L, LRU_WIDTH), 0.02)
    lru_w_rg = nrm(ks[18], (L, N_DIRS, LRU_BLOCKS, LRU_BLOCK_DIM, LRU_BLOCK_DIM), LRU_BLOCK_DIM ** -0.5)
    lru_b_rg = nrm(ks[19], (L, N_DIRS, LRU_WIDTH), 0.02)
    lru_w_ig = nrm(ks[20], (L, N_DIRS, LRU_BLOCKS, LRU_BLOCK_DIM, LRU_BLOCK_DIM), LRU_BLOCK_DIM ** -0.5)
    lru_b_ig = nrm(ks[21], (L, N_DIRS, LRU_WIDTH), 0.02)
    a0 = jax.random.uniform(ks[22], (L, N_DIRS, LRU_WIDTH), f32, 0.9, 0.999)
    s = a0 ** (1.0 / LRU_C)
    lru_lambda = jnp.log(s) - jnp.log1p(-s)
    w_branch_dn = nrm(ks[23], (L, DN_WIDTH, D_MODEL), DN_WIDTH ** -0.5)
    w_branch_lru = nrm(ks[24], (L, LRU_WIDTH, D_MODEL), LRU_WIDTH ** -0.5)
    w_out = nrm(ks[25], (L, D_MODEL, D_MODEL), D_MODEL ** -0.5)
    w_up = nrm(ks[26], (L, D_MODEL, 2 * D_FF), D_MODEL ** -0.5)
    ffn_dw = nrm(ks[27], (L, FFN_CONV, FFN_CONV, D_FF), 1.0 / FFN_CONV)
    ffn_dw_b = nrm(ks[28], (L, D_FF), 0.02)
    w_down = nrm(ks[29], (L, D_FF, D_MODEL), D_FF ** -0.5)
    return {"x": x, "c": c, "ctx": ctx, "c_ctx": c_ctx, "w_ada": w_ada, "b_ada": b_ada,
            "g_pre_mix": g_pre_mix, "g_post_mix": g_post_mix, "g_pre_ffn": g_pre_ffn, "g_post_ffn": g_post_ffn,
            "w_in": w_in, "b_merge": b_merge, "dn_conv": dn_conv, "dn_a_log": dn_a_log,
            "dn_dt_bias": dn_dt_bias, "dn_onorm": dn_onorm, "lru_conv": lru_conv, "lru_conv_b": lru_conv_b,
            "lru_w_rg": lru_w_rg, "lru_b_rg": lru_b_rg, "lru_w_ig": lru_w_ig, "lru_b_ig": lru_b_ig,
            "lru_lambda": lru_lambda, "w_branch_dn": w_branch_dn, "w_branch_lru": w_branch_lru,
            "w_out": w_out, "w_up": w_up, "ffn_dw": ffn_dw, "ffn_dw_b": ffn_dw_b, "w_down": w_down}


def _fwd_reference(x, c, ctx, c_ctx, w_ada, b_ada, g_pre_mix, g_post_mix, g_pre_ffn, g_post_ffn, w_in, b_merge,
              dn_conv, dn_a_log, dn_dt_bias, dn_onorm, lru_conv, lru_conv_b, lru_w_rg, lru_b_rg, lru_w_ig,
              lru_b_ig, lru_lambda, w_branch_dn, w_branch_lru, w_out, w_up, ffn_dw, ffn_dw_b, w_down):
    n_ctx = ctx.shape[1]
    rows = x.shape[1] // GRID_W
    B = x.shape[0]
    split_at = [int(s) for s in np.cumsum(IN_SIZES)[:-1]]
    h = jnp.concatenate([ctx, x], axis=1)
    silu_c = jax.nn.silu(c)
    silu_cc = jax.nn.silu(c_ctx)
    for l in range(DEPTH):
        last = l == DEPTH - 1
        m_x = jnp.split(silu_c @ w_ada[l] + b_ada[l], 6, axis=-1)
        m_c = jnp.split(silu_cc @ w_ada[l] + b_ada[l], 6, axis=-1)
        u = modulate(rmsnorm(h, g_pre_mix[l]), m_c[0], m_c[1], m_x[0], m_x[1], n_ctx)
        T = u.shape[1]
        q, k, v, z, a, b, xl, yl, mg = jnp.split(u @ w_in[l], split_at, axis=-1)
        y_dn = gated_deltanet(q, k, v, z, a.reshape(B, T, N_DIRS, DN_HEADS), b.reshape(B, T, N_DIRS, DN_HEADS),
                              dn_conv[l], dn_a_log[l], dn_dt_bias[l], dn_onorm[l], n_ctx)
        y_lru = rg_lru(xl, yl, lru_conv[l], lru_conv_b[l], lru_w_rg[l], lru_b_rg[l], lru_w_ig[l], lru_b_ig[l],
                       lru_lambda[l], n_ctx)
        drop = n_ctx if last else 0
        n_cur = n_ctx - drop
        h, y_dn, y_lru, mg = h[:, drop:], y_dn[:, drop:], y_lru[:, drop:], mg[:, drop:]
        g_dn, g_lru = jnp.split(jax.nn.sigmoid(mg + b_merge[l]), N_BRANCH, axis=-1)
        mix = (g_dn * (y_dn @ w_branch_dn[l]) + g_lru * (y_lru @ w_branch_lru[l])) @ w_out[l]
        h = h + gate_rows(rmsnorm(mix, g_post_mix[l]), m_c[2], m_x[2], n_cur)
        u = modulate(rmsnorm(h, g_pre_ffn[l]), m_c[3], m_c[4], m_x[3], m_x[4], n_cur)
        f_gate, f_val = jnp.split(u @ w_up[l], 2, axis=-1)
        f_gate = ffn_dwconv(f_gate, ffn_dw[l], n_cur, rows) + ffn_dw_b[l]
        f = jax.nn.gelu(f_gate) * f_val
        h = h + gate_rows(rmsnorm(f @ w_down[l], g_post_ffn[l]), m_c[5], m_x[5], n_cur)
    return h


import jax as _jax
import jax.numpy as _jnp

TWIN_FORMAT = 'train_step'
FWD_PARAMS = ['x', 'c', 'ctx', 'c_ctx', 'w_ada', 'b_ada', 'g_pre_mix', 'g_post_mix', 'g_pre_ffn', 'g_post_ffn', 'w_in', 'b_merge', 'dn_conv', 'dn_a_log', 'dn_dt_bias', 'dn_onorm', 'lru_conv', 'lru_conv_b', 'lru_w_rg', 'lru_b_rg', 'lru_w_ig', 'lru_b_ig', 'lru_lambda', 'w_branch_dn', 'w_branch_lru', 'w_out', 'w_up', 'ffn_dw', 'ffn_dw_b', 'w_down']
TWIN_WEIGHTS = ['c_ctx', 'w_ada', 'b_ada', 'g_pre_mix', 'g_post_mix', 'g_pre_ffn', 'g_post_ffn', 'w_in', 'b_merge', 'dn_conv', 'dn_a_log', 'dn_dt_bias', 'dn_onorm', 'lru_conv', 'lru_conv_b', 'lru_w_rg', 'lru_b_rg', 'lru_w_ig', 'lru_b_ig', 'lru_lambda', 'w_branch_dn', 'w_branch_lru', 'w_out', 'w_up', 'ffn_dw', 'ffn_dw_b', 'w_down']
TWIN_DIFF_INPUT = 'x'
TWIN_INPUTS = ['x', 'c', 'ctx', 'c_ctx', 'w_ada', 'b_ada', 'g_pre_mix', 'g_post_mix', 'g_pre_ffn', 'g_post_ffn', 'w_in', 'b_merge', 'dn_conv', 'dn_a_log', 'dn_dt_bias', 'dn_onorm', 'lru_conv', 'lru_conv_b', 'lru_w_rg', 'lru_b_rg', 'lru_w_ig', 'lru_b_ig', 'lru_lambda', 'w_branch_dn', 'w_branch_lru', 'w_out', 'w_up', 'ffn_dw', 'ffn_dw_b', 'w_down', 'loss_target', 'm_c_ctx', 'm_w_ada', 'm_b_ada', 'm_g_pre_mix', 'm_g_post_mix', 'm_g_pre_ffn', 'm_g_post_ffn', 'm_w_in', 'm_b_merge', 'm_dn_conv', 'm_dn_a_log', 'm_dn_dt_bias', 'm_dn_onorm', 'm_lru_conv', 'm_lru_conv_b', 'm_lru_w_rg', 'm_lru_b_rg', 'm_lru_w_ig', 'm_lru_b_ig', 'm_lru_lambda', 'm_w_branch_dn', 'm_w_branch_lru', 'm_w_out', 'm_w_up', 'm_ffn_dw', 'm_ffn_dw_b', 'm_w_down', 'v_c_ctx', 'v_w_ada', 'v_b_ada', 'v_g_pre_mix', 'v_g_post_mix', 'v_g_pre_ffn', 'v_g_post_ffn', 'v_w_in', 'v_b_merge', 'v_dn_conv', 'v_dn_a_log', 'v_dn_dt_bias', 'v_dn_onorm', 'v_lru_conv', 'v_lru_conv_b', 'v_lru_w_rg', 'v_lru_b_rg', 'v_lru_w_ig', 'v_lru_b_ig', 'v_lru_lambda', 'v_w_branch_dn', 'v_w_branch_lru', 'v_w_out', 'v_w_up', 'v_ffn_dw', 'v_ffn_dw_b', 'v_w_down']
TWIN_OUTPUTS = ['loss', 'grad_x', 'grad_c_ctx', 'grad_w_ada', 'grad_b_ada', 'grad_g_pre_mix', 'grad_g_post_mix', 'grad_g_pre_ffn', 'grad_g_post_ffn', 'grad_w_in', 'grad_b_merge', 'grad_dn_conv', 'grad_dn_a_log', 'grad_dn_dt_bias', 'grad_dn_onorm', 'grad_lru_conv', 'grad_lru_conv_b', 'grad_lru_w_rg', 'grad_lru_b_rg', 'grad_lru_w_ig', 'grad_lru_b_ig', 'grad_lru_lambda', 'grad_w_branch_dn', 'grad_w_branch_lru', 'grad_w_out', 'grad_w_up', 'grad_ffn_dw', 'grad_ffn_dw_b', 'grad_w_down', 'delta_c_ctx', 'delta_w_ada', 'delta_b_ada', 'delta_g_pre_mix', 'delta_g_post_mix', 'delta_g_pre_ffn', 'delta_g_post_ffn', 'delta_w_in', 'delta_b_merge', 'delta_dn_conv', 'delta_dn_a_log', 'delta_dn_dt_bias', 'delta_dn_onorm', 'delta_lru_conv', 'delta_lru_conv_b', 'delta_lru_w_rg', 'delta_lru_b_rg', 'delta_lru_w_ig', 'delta_lru_b_ig', 'delta_lru_lambda', 'delta_w_branch_dn', 'delta_w_branch_lru', 'delta_w_out', 'delta_w_up', 'delta_ffn_dw', 'delta_ffn_dw_b', 'delta_w_down', 'new_m_c_ctx', 'new_m_w_ada', 'new_m_b_ada', 'new_m_g_pre_mix', 'new_m_g_post_mix', 'new_m_g_pre_ffn', 'new_m_g_post_ffn', 'new_m_w_in', 'new_m_b_merge', 'new_m_dn_conv', 'new_m_dn_a_log', 'new_m_dn_dt_bias', 'new_m_dn_onorm', 'new_m_lru_conv', 'new_m_lru_conv_b', 'new_m_lru_w_rg', 'new_m_lru_b_rg', 'new_m_lru_w_ig', 'new_m_lru_b_ig', 'new_m_lru_lambda', 'new_m_w_branch_dn', 'new_m_w_branch_lru', 'new_m_w_out', 'new_m_w_up', 'new_m_ffn_dw', 'new_m_ffn_dw_b', 'new_m_w_down', 'new_v_c_ctx', 'new_v_w_ada', 'new_v_b_ada', 'new_v_g_pre_mix', 'new_v_g_post_mix', 'new_v_g_pre_ffn', 'new_v_g_post_ffn', 'new_v_w_in', 'new_v_b_merge', 'new_v_dn_conv', 'new_v_dn_a_log', 'new_v_dn_dt_bias', 'new_v_dn_onorm', 'new_v_lru_conv', 'new_v_lru_conv_b', 'new_v_lru_w_rg', 'new_v_lru_b_rg', 'new_v_lru_w_ig', 'new_v_lru_b_ig', 'new_v_lru_lambda', 'new_v_w_branch_dn', 'new_v_w_branch_lru', 'new_v_w_out', 'new_v_w_up', 'new_v_ffn_dw', 'new_v_ffn_dw_b', 'new_v_w_down']
TWIN_LEAF_KINDS = {'loss': 'loss', 'grad_x': 'grad_x', 'grad_c_ctx': 'grad_w', 'grad_w_ada': 'grad_w', 'grad_b_ada': 'grad_w', 'grad_g_pre_mix': 'grad_w', 'grad_g_post_mix': 'grad_w', 'grad_g_pre_ffn': 'grad_w', 'grad_g_post_ffn': 'grad_w', 'grad_w_in': 'grad_w', 'grad_b_merge': 'grad_w', 'grad_dn_conv': 'grad_w', 'grad_dn_a_log': 'grad_w', 'grad_dn_dt_bias': 'grad_w', 'grad_dn_onorm': 'grad_w', 'grad_lru_conv': 'grad_w', 'grad_lru_conv_b': 'grad_w', 'grad_lru_w_rg': 'grad_w', 'grad_lru_b_rg': 'grad_w', 'grad_lru_w_ig': 'grad_w', 'grad_lru_b_ig': 'grad_w', 'grad_lru_lambda': 'grad_w', 'grad_w_branch_dn': 'grad_w', 'grad_w_branch_lru': 'grad_w', 'grad_w_out': 'grad_w', 'grad_w_up': 'grad_w', 'grad_ffn_dw': 'grad_w', 'grad_ffn_dw_b': 'grad_w', 'grad_w_down': 'grad_w', 'delta_c_ctx': 'delta_w', 'delta_w_ada': 'delta_w', 'delta_b_ada': 'delta_w', 'delta_g_pre_mix': 'delta_w', 'delta_g_post_mix': 'delta_w', 'delta_g_pre_ffn': 'delta_w', 'delta_g_post_ffn': 'delta_w', 'delta_w_in': 'delta_w', 'delta_b_merge': 'delta_w', 'delta_dn_conv': 'delta_w', 'delta_dn_a_log': 'delta_w', 'delta_dn_dt_bias': 'delta_w', 'delta_dn_onorm': 'delta_w', 'delta_lru_conv': 'delta_w', 'delta_lru_conv_b': 'delta_w', 'delta_lru_w_rg': 'delta_w', 'delta_lru_b_rg': 'delta_w', 'delta_lru_w_ig': 'delta_w', 'delta_lru_b_ig': 'delta_w', 'delta_lru_lambda': 'delta_w', 'delta_w_branch_dn': 'delta_w', 'delta_w_branch_lru': 'delta_w', 'delta_w_out': 'delta_w', 'delta_w_up': 'delta_w', 'delta_ffn_dw': 'delta_w', 'delta_ffn_dw_b': 'delta_w', 'delta_w_down': 'delta_w', 'new_m_c_ctx': 'new_m', 'new_m_w_ada': 'new_m', 'new_m_b_ada': 'new_m', 'new_m_g_pre_mix': 'new_m', 'new_m_g_post_mix': 'new_m', 'new_m_g_pre_ffn': 'new_m', 'new_m_g_post_ffn': 'new_m', 'new_m_w_in': 'new_m', 'new_m_b_merge': 'new_m', 'new_m_dn_conv': 'new_m', 'new_m_dn_a_log': 'new_m', 'new_m_dn_dt_bias': 'new_m', 'new_m_dn_onorm': 'new_m', 'new_m_lru_conv': 'new_m', 'new_m_lru_conv_b': 'new_m', 'new_m_lru_w_rg': 'new_m', 'new_m_lru_b_rg': 'new_m', 'new_m_lru_w_ig': 'new_m', 'new_m_lru_b_ig': 'new_m', 'new_m_lru_lambda': 'new_m', 'new_m_w_branch_dn': 'new_m', 'new_m_w_branch_lru': 'new_m', 'new_m_w_out': 'new_m', 'new_m_w_up': 'new_m', 'new_m_ffn_dw': 'new_m', 'new_m_ffn_dw_b': 'new_m', 'new_m_w_down': 'new_m', 'new_v_c_ctx': 'new_v', 'new_v_w_ada': 'new_v', 'new_v_b_ada': 'new_v', 'new_v_g_pre_mix': 'new_v', 'new_v_g_post_mix': 'new_v', 'new_v_g_pre_ffn': 'new_v', 'new_v_g_post_ffn': 'new_v', 'new_v_w_in': 'new_v', 'new_v_b_merge': 'new_v', 'new_v_dn_conv': 'new_v', 'new_v_dn_a_log': 'new_v', 'new_v_dn_dt_bias': 'new_v', 'new_v_dn_onorm': 'new_v', 'new_v_lru_conv': 'new_v', 'new_v_lru_conv_b': 'new_v', 'new_v_lru_w_rg': 'new_v', 'new_v_lru_b_rg': 'new_v', 'new_v_lru_w_ig': 'new_v', 'new_v_lru_b_ig': 'new_v', 'new_v_lru_lambda': 'new_v', 'new_v_w_branch_dn': 'new_v', 'new_v_w_branch_lru': 'new_v', 'new_v_w_out': 'new_v', 'new_v_w_up': 'new_v', 'new_v_ffn_dw': 'new_v', 'new_v_ffn_dw_b': 'new_v', 'new_v_w_down': 'new_v'}


def _forward(args):
    return _fwd_reference(*[args[k] for k in FWD_PARAMS])


def _output_shape():
    out = _jax.eval_shape(lambda: _forward(_fwd_setup_inputs(0)))
    return out.shape, out.dtype

N_MICROBATCH = 1
ADAM_LR = 0.001
ADAM_B1 = 0.9
ADAM_B2 = 0.999
ADAM_EPS = 1e-08
ADAM_WD = 0.01
ADAM_STEP = 10
PER_EXAMPLE_BATCH_AXIS = {'x': 0, 'c': 0, 'ctx': 0, 'loss_target': 0}
SHARED_INPUTS = []
_WEIGHT_DTYPES = {'c_ctx': _jnp.float32, 'w_ada': _jnp.float32, 'b_ada': _jnp.float32, 'g_pre_mix': _jnp.float32, 'g_post_mix': _jnp.float32, 'g_pre_ffn': _jnp.float32, 'g_post_ffn': _jnp.float32, 'w_in': _jnp.float32, 'b_merge': _jnp.float32, 'dn_conv': _jnp.float32, 'dn_a_log': _jnp.float32, 'dn_dt_bias': _jnp.float32, 'dn_onorm': _jnp.float32, 'lru_conv': _jnp.float32, 'lru_conv_b': _jnp.float32, 'lru_w_rg': _jnp.float32, 'lru_b_rg': _jnp.float32, 'lru_w_ig': _jnp.float32, 'lru_b_ig': _jnp.float32, 'lru_lambda': _jnp.float32, 'w_branch_dn': _jnp.float32, 'w_branch_lru': _jnp.float32, 'w_out': _jnp.float32, 'w_up': _jnp.float32, 'ffn_dw': _jnp.float32, 'ffn_dw_b': _jnp.float32, 'w_down': _jnp.float32}
MOMENT_SCALE = {'c_ctx': 4.294370e-02, 'w_ada': 1.789025e+00, 'b_ada': 3.274365e+00, 'g_pre_mix': 1.142342e-01, 'g_post_mix': 3.757789e+00, 'g_pre_ffn': 1.341274e-01, 'g_post_ffn': 3.859444e+00, 'w_in': 1.135974e-01, 'b_merge': 9.957739e-02, 'dn_conv': 2.487251e-02, 'dn_a_log': 6.930043e-02, 'dn_dt_bias': 6.668776e-02, 'dn_onorm': 1.234971e-01, 'lru_conv': 3.596159e-01, 'lru_conv_b': 1.064977e+00, 'lru_w_rg': 2.358485e-02, 'lru_b_rg': 2.458455e-02, 'lru_w_ig': 4.562430e-02, 'lru_b_ig': 6.957254e-02, 'lru_lambda': 6.241693e-02, 'w_branch_dn': 5.038189e-02, 'w_branch_lru': 3.502413e-01, 'w_out': 3.546821e-01, 'w_up': 5.409421e-02, 'ffn_dw': 5.022494e-02, 'ffn_dw_b': 7.171216e-02, 'w_down': 1.250447e-01}


def _to_microbatches(a, axis):
    t = _jnp.moveaxis(a, axis, 0)
    t = t.reshape((N_MICROBATCH, t.shape[0] // N_MICROBATCH) + t.shape[1:])
    return _jnp.moveaxis(t, 1, axis + 1)


def setup_inputs(seed: int = 0) -> dict:
    inp = _fwd_setup_inputs(seed)
    key = _jax.random.fold_in(_jax.random.key(seed), 7919)
    shape, _ = _output_shape()
    out = dict(inp)
    out["loss_target"] = _jax.random.normal(_jax.random.fold_in(key, 0), shape, _jnp.float32)
    for i, name in enumerate(TWIN_WEIGHTS):
        w = inp[name].astype(_jnp.float32)
        if MOMENT_SCALE is None:
            s = _jnp.sqrt(_jnp.mean(_jnp.square(w)) + 1e-30)
        else:
            s = MOMENT_SCALE[name]
        km, kv = _jax.random.split(_jax.random.fold_in(key, i + 1))
        out[name] = w
        out["m_" + name] = s * _jax.random.normal(km, w.shape, _jnp.float32)
        out["v_" + name] = (s * s) * _jax.random.uniform(kv, w.shape, _jnp.float32, 0.5, 1.5)
    if N_MICROBATCH > 1:
        for name, axis in PER_EXAMPLE_BATCH_AXIS.items():
            out[name] = _to_microbatches(out[name], axis)
    return {'x': out['x'], 'c': out['c'], 'ctx': out['ctx'], 'c_ctx': out['c_ctx'], 'w_ada': out['w_ada'], 'b_ada': out['b_ada'], 'g_pre_mix': out['g_pre_mix'], 'g_post_mix': out['g_post_mix'], 'g_pre_ffn': out['g_pre_ffn'], 'g_post_ffn': out['g_post_ffn'], 'w_in': out['w_in'], 'b_merge': out['b_merge'], 'dn_conv': out['dn_conv'], 'dn_a_log': out['dn_a_log'], 'dn_dt_bias': out['dn_dt_bias'], 'dn_onorm': out['dn_onorm'], 'lru_conv': out['lru_conv'], 'lru_conv_b': out['lru_conv_b'], 'lru_w_rg': out['lru_w_rg'], 'lru_b_rg': out['lru_b_rg'], 'lru_w_ig': out['lru_w_ig'], 'lru_b_ig': out['lru_b_ig'], 'lru_lambda': out['lru_lambda'], 'w_branch_dn': out['w_branch_dn'], 'w_branch_lru': out['w_branch_lru'], 'w_out': out['w_out'], 'w_up': out['w_up'], 'ffn_dw': out['ffn_dw'], 'ffn_dw_b': out['ffn_dw_b'], 'w_down': out['w_down'], 'loss_target': out['loss_target'], 'm_c_ctx': out['m_c_ctx'], 'm_w_ada': out['m_w_ada'], 'm_b_ada': out['m_b_ada'], 'm_g_pre_mix': out['m_g_pre_mix'], 'm_g_post_mix': out['m_g_post_mix'], 'm_g_pre_ffn': out['m_g_pre_ffn'], 'm_g_post_ffn': out['m_g_post_ffn'], 'm_w_in': out['m_w_in'], 'm_b_merge': out['m_b_merge'], 'm_dn_conv': out['m_dn_conv'], 'm_dn_a_log': out['m_dn_a_log'], 'm_dn_dt_bias': out['m_dn_dt_bias'], 'm_dn_onorm': out['m_dn_onorm'], 'm_lru_conv': out['m_lru_conv'], 'm_lru_conv_b': out['m_lru_conv_b'], 'm_lru_w_rg': out['m_lru_w_rg'], 'm_lru_b_rg': out['m_lru_b_rg'], 'm_lru_w_ig': out['m_lru_w_ig'], 'm_lru_b_ig': out['m_lru_b_ig'], 'm_lru_lambda': out['m_lru_lambda'], 'm_w_branch_dn': out['m_w_branch_dn'], 'm_w_branch_lru': out['m_w_branch_lru'], 'm_w_out': out['m_w_out'], 'm_w_up': out['m_w_up'], 'm_ffn_dw': out['m_ffn_dw'], 'm_ffn_dw_b': out['m_ffn_dw_b'], 'm_w_down': out['m_w_down'], 'v_c_ctx': out['v_c_ctx'], 'v_w_ada': out['v_w_ada'], 'v_b_ada': out['v_b_ada'], 'v_g_pre_mix': out['v_g_pre_mix'], 'v_g_post_mix': out['v_g_post_mix'], 'v_g_pre_ffn': out['v_g_pre_ffn'], 'v_g_post_ffn': out['v_g_post_ffn'], 'v_w_in': out['v_w_in'], 'v_b_merge': out['v_b_merge'], 'v_dn_conv': out['v_dn_conv'], 'v_dn_a_log': out['v_dn_a_log'], 'v_dn_dt_bias': out['v_dn_dt_bias'], 'v_dn_onorm': out['v_dn_onorm'], 'v_lru_conv': out['v_lru_conv'], 'v_lru_conv_b': out['v_lru_conv_b'], 'v_lru_w_rg': out['v_lru_w_rg'], 'v_lru_b_rg': out['v_lru_b_rg'], 'v_lru_w_ig': out['v_lru_w_ig'], 'v_lru_b_ig': out['v_lru_b_ig'], 'v_lru_lambda': out['v_lru_lambda'], 'v_w_branch_dn': out['v_w_branch_dn'], 'v_w_branch_lru': out['v_w_branch_lru'], 'v_w_out': out['v_w_out'], 'v_w_up': out['v_w_up'], 'v_ffn_dw': out['v_ffn_dw'], 'v_ffn_dw_b': out['v_ffn_dw_b'], 'v_w_down': out['v_w_down']}


def _loss(weights, diff, rest, loss_target):
    with _jax.named_scope("forward"):
        args = {**rest, TWIN_DIFF_INPUT: diff, **{k: w.astype(_WEIGHT_DTYPES[k]) for k, w in weights.items()}}
        y = _forward(args)
    with _jax.named_scope("loss_head"):
        err = _jnp.square(y.astype(_jnp.float32) - loss_target)
        return 0.5 * _jnp.sum(_jnp.mean(err, axis=-1)) if err.ndim else 0.5 * err


def _adamw(w, g, m, v):
    m = ADAM_B1 * m + (1.0 - ADAM_B1) * g
    v = ADAM_B2 * v + (1.0 - ADAM_B2) * _jnp.square(g)
    m_hat = m / (1.0 - ADAM_B1 ** ADAM_STEP)
    v_hat = v / (1.0 - ADAM_B2 ** ADAM_STEP)
    delta = -ADAM_LR * (m_hat / (_jnp.sqrt(v_hat) + ADAM_EPS) + ADAM_WD * w)
    return delta, m, v


def reference(x, c, ctx, c_ctx, w_ada, b_ada, g_pre_mix, g_post_mix, g_pre_ffn, g_post_ffn, w_in, b_merge, dn_conv, dn_a_log, dn_dt_bias, dn_onorm, lru_conv, lru_conv_b, lru_w_rg, lru_b_rg, lru_w_ig, lru_b_ig, lru_lambda, w_branch_dn, w_branch_lru, w_out, w_up, ffn_dw, ffn_dw_b, w_down, loss_target, m_c_ctx, m_w_ada, m_b_ada, m_g_pre_mix, m_g_post_mix, m_g_pre_ffn, m_g_post_ffn, m_w_in, m_b_merge, m_dn_conv, m_dn_a_log, m_dn_dt_bias, m_dn_onorm, m_lru_conv, m_lru_conv_b, m_lru_w_rg, m_lru_b_rg, m_lru_w_ig, m_lru_b_ig, m_lru_lambda, m_w_branch_dn, m_w_branch_lru, m_w_out, m_w_up, m_ffn_dw, m_ffn_dw_b, m_w_down, v_c_ctx, v_w_ada, v_b_ada, v_g_pre_mix, v_g_post_mix, v_g_pre_ffn, v_g_post_ffn, v_w_in, v_b_merge, v_dn_conv, v_dn_a_log, v_dn_dt_bias, v_dn_onorm, v_lru_conv, v_lru_conv_b, v_lru_w_rg, v_lru_b_rg, v_lru_w_ig, v_lru_b_ig, v_lru_lambda, v_w_branch_dn, v_w_branch_lru, v_w_out, v_w_up, v_ffn_dw, v_ffn_dw_b, v_w_down):
    given = dict(x=x, c=c, ctx=ctx, c_ctx=c_ctx, w_ada=w_ada, b_ada=b_ada, g_pre_mix=g_pre_mix, g_post_mix=g_post_mix, g_pre_ffn=g_pre_ffn, g_post_ffn=g_post_ffn, w_in=w_in, b_merge=b_merge, dn_conv=dn_conv, dn_a_log=dn_a_log, dn_dt_bias=dn_dt_bias, dn_onorm=dn_onorm, lru_conv=lru_conv, lru_conv_b=lru_conv_b, lru_w_rg=lru_w_rg, lru_b_rg=lru_b_rg, lru_w_ig=lru_w_ig, lru_b_ig=lru_b_ig, lru_lambda=lru_lambda, w_branch_dn=w_branch_dn, w_branch_lru=w_branch_lru, w_out=w_out, w_up=w_up, ffn_dw=ffn_dw, ffn_dw_b=ffn_dw_b, w_down=w_down, loss_target=loss_target, m_c_ctx=m_c_ctx, m_w_ada=m_w_ada, m_b_ada=m_b_ada, m_g_pre_mix=m_g_pre_mix, m_g_post_mix=m_g_post_mix, m_g_pre_ffn=m_g_pre_ffn, m_g_post_ffn=m_g_post_ffn, m_w_in=m_w_in, m_b_merge=m_b_merge, m_dn_conv=m_dn_conv, m_dn_a_log=m_dn_a_log, m_dn_dt_bias=m_dn_dt_bias, m_dn_onorm=m_dn_onorm, m_lru_conv=m_lru_conv, m_lru_conv_b=m_lru_conv_b, m_lru_w_rg=m_lru_w_rg, m_lru_b_rg=m_lru_b_rg, m_lru_w_ig=m_lru_w_ig, m_lru_b_ig=m_lru_b_ig, m_lru_lambda=m_lru_lambda, m_w_branch_dn=m_w_branch_dn, m_w_branch_lru=m_w_branch_lru, m_w_out=m_w_out, m_w_up=m_w_up, m_ffn_dw=m_ffn_dw, m_ffn_dw_b=m_ffn_dw_b, m_w_down=m_w_down, v_c_ctx=v_c_ctx, v_w_ada=v_w_ada, v_b_ada=v_b_ada, v_g_pre_mix=v_g_pre_mix, v_g_post_mix=v_g_post_mix, v_g_pre_ffn=v_g_pre_ffn, v_g_post_ffn=v_g_post_ffn, v_w_in=v_w_in, v_b_merge=v_b_merge, v_dn_conv=v_dn_conv, v_dn_a_log=v_dn_a_log, v_dn_dt_bias=v_dn_dt_bias, v_dn_onorm=v_dn_onorm, v_lru_conv=v_lru_conv, v_lru_conv_b=v_lru_conv_b, v_lru_w_rg=v_lru_w_rg, v_lru_b_rg=v_lru_b_rg, v_lru_w_ig=v_lru_w_ig, v_lru_b_ig=v_lru_b_ig, v_lru_lambda=v_lru_lambda, v_w_branch_dn=v_w_branch_dn, v_w_branch_lru=v_w_branch_lru, v_w_out=v_w_out, v_w_up=v_w_up, v_ffn_dw=v_ffn_dw, v_ffn_dw_b=v_ffn_dw_b, v_w_down=v_w_down)
    weights = {n: given[n] for n in TWIN_WEIGHTS}
    shared = {n: given[n] for n in SHARED_INPUTS}
    per_example = {n: given[n] for n in ['x', 'c', 'ctx']}
    grad_fn = _jax.value_and_grad(_loss, argnums=(0, 1))

    def one_microbatch(ex, loss_target):
        ex = dict(ex)
        diff = ex.pop(TWIN_DIFF_INPUT)
        return grad_fn(weights, diff, {**shared, **ex}, loss_target)

    if N_MICROBATCH == 1:
        loss, (grad_w, grad_x) = one_microbatch(per_example, given["loss_target"])
    else:
        def body(carry, xs):
            loss_sum, grad_sum = carry
            l_k, (gw_k, gx_k) = one_microbatch(xs[0], xs[1])
            with _jax.named_scope("update"):
                return (loss_sum + l_k, _jax.tree.map(_jnp.add, grad_sum, gw_k)), gx_k

        init = (_jnp.zeros((), _jnp.float32), _jax.tree.map(_jnp.zeros_like, weights))
        (loss, grad_w), grad_x = _jax.lax.scan(body, init, (per_example, given["loss_target"]))
    with _jax.named_scope("update"):
        delta_w, new_m, new_v = {}, {}, {}
        for n in TWIN_WEIGHTS:
            delta_w[n], new_m[n], new_v[n] = _adamw(weights[n], grad_w[n], given["m_" + n], given["v_" + n])
    return (loss, grad_x, *[grad_w[n] for n in TWIN_WEIGHTS], *[delta_w[n] for n in TWIN_WEIGHTS],
            *[new_m[n] for n in TWIN_WEIGHTS], *[new_v[n] for n in TWIN_WEIGHTS])
```

```python
import functools
import math

import jax
import jax.numpy as jnp
from jax import lax
from jax.experimental import pallas as pl
from jax.experimental.pallas import tpu as pltpu

F32 = jnp.float32
BF16 = jnp.bfloat16
EPS = 1e-6
GRID_W = 64
CHUNK = 64
LRU_C = 8.0
LANES = 128
SUBLANES = 8
VMEM_LIMIT = 56 * 1024 * 1024
ADAM_LR, ADAM_B1, ADAM_B2, ADAM_EPS, ADAM_WD, ADAM_STEP = 0.001, 0.9, 0.999, 1e-08, 0.01, 10
MESH = pl.DeviceIdType.MESH


def _tile(n, target, mult=LANES):
    best = None
    for t in range(mult, min(n, target) + 1, mult):
        if n % t == 0:
            best = t
    return best if best is not None else n


def _params(sem=None, **kw):
    return pltpu.CompilerParams(dimension_semantics=sem, vmem_limit_bytes=VMEM_LIMIT, **kw)


def _sigmoid(x):
    return 1.0 / (1.0 + jnp.exp(-x))


def _silu(x):
    return x * _sigmoid(x)


def _softplus(x):
    return jnp.maximum(x, 0.0) + jnp.log(1.0 + jnp.exp(-jnp.abs(x)))


def _gelu(x):
    return 0.5 * x * (1.0 + jnp.tanh(math.sqrt(2.0 / math.pi) * (x + 0.044715 * x * x * x)))


def _rmsn(u, gain):
    return u * lax.rsqrt(jnp.mean(u * u, axis=-1, keepdims=True) + EPS) * gain


def _matmul(a, b, *, ta=False, tb=False, add=None, out_dtype=F32, name, tm=1024, tn=1024, tk=512):
    (K, M) = a.shape if ta else a.shape[::-1]
    N = b.shape[0] if tb else b.shape[1]
    assert (b.shape[1] if tb else b.shape[0]) == K, (a.shape, b.shape, ta, tb)
    tm, tn, tk = _tile(M, tm), _tile(N, tn), _tile(K, tk)
    nk = K // tk
    dims = (((0 if ta else 1,), (1 if tb else 0,)), ((), ()))

    def body(a_ref, b_ref, *rest):
        (c_ref, o_ref, acc_ref) = rest if add is not None else (None, *rest)
        k = pl.program_id(2)

        @pl.when(k == 0)
        def _():
            acc_ref[...] = jnp.zeros_like(acc_ref) if c_ref is None else c_ref[...]

        acc_ref[...] += lax.dot_general(a_ref[...].astype(BF16), b_ref[...].astype(BF16), dims,
                                        preferred_element_type=F32)

        @pl.when(k == nk - 1)
        def _():
            o_ref[...] = acc_ref[...].astype(out_dtype)

    a_spec = pl.BlockSpec((tk, tm), lambda i, j, k: (k, i)) if ta else pl.BlockSpec((tm, tk), lambda i, j, k: (i, k))
    b_spec = pl.BlockSpec((tn, tk), lambda i, j, k: (j, k)) if tb else pl.BlockSpec((tk, tn), lambda i, j, k: (k, j))
    o_spec = pl.BlockSpec((tm, tn), lambda i, j, k: (i, j))
    return pl.pallas_call(
        body, name=name, grid=(M // tm, N // tn, nk),
        in_specs=[a_spec, b_spec] + ([o_spec] if add is not None else []),
        out_specs=o_spec,
        out_shape=jax.ShapeDtypeStruct((M, N), out_dtype),
        scratch_shapes=[pltpu.VMEM((tm, tn), F32)],
        compiler_params=_params(("parallel", "parallel", "arbitrary")),
    )(*((a, b) + ((add,) if add is not None else ())))


def _premix_math(h, gain, shift, scale):
    return _rmsn(h, gain) * (1.0 + scale) + shift


def _premix_fwd(h, gain, tab, *, nc, tm):
    B, T, D = h.shape
    nt, nct = T // tm, nc // tm

    def body(h_ref, g_ref, tab_ref, u_ref):
        tabv = tab_ref[0, 0]
        u_ref[...] = _premix_math(h_ref[0], g_ref[...], tabv[0:1], tabv[1:2]).astype(BF16)

    return pl.pallas_call(
        body, name="premix_fwd", grid=(B, nt),
        in_specs=[pl.BlockSpec((1, tm, D), lambda b, t: (b, t, 0)),
                  pl.BlockSpec((1, D), lambda b, t: (0, 0)),
                  pl.BlockSpec((1, 1, 8, D), lambda b, t: (b, jnp.where(t < nct, 0, 1), 0, 0))],
        out_specs=pl.BlockSpec((tm, D), lambda b, t: (b * nt + t, 0)),
        out_shape=jax.ShapeDtypeStruct((B * T, D), BF16),
        compiler_params=_params(("parallel", "parallel")),
    )(h, gain, tab)


def _premix_bwd(h, gain, tab, du, dres, *, nc, tm):
    B, T, D = h.shape
    nt, nct = T // tm, nc // tm
    N = T - nc

    def body(h_ref, g_ref, tab_ref, du_ref, dres_ref, dx_ref, sums_ref):
        t = pl.program_id(1)
        tabv = tab_ref[0, 0]
        _, vjp = jax.vjp(_premix_math, h_ref[0], g_ref[...], tabv[0:1], tabv[1:2])
        dh, dgain, dshift, dscale = vjp(du_ref[...].astype(F32))

        @pl.when((t == 0) | (t == nct))
        def _():
            sums_ref[...] = jnp.zeros_like(sums_ref)

        sums_ref[0, 0, 0:1, :] += dshift
        sums_ref[0, 0, 1:2, :] += dscale
        sums_ref[0, 0, 2:3, :] += dgain

        @pl.when(t >= nct)
        def _():
            dx_ref[0] = dres_ref[...] + dh

    lat = lambda b, t: jnp.maximum(t - nct, 0)
    return pl.pallas_call(
        body, name="premix_bwd", grid=(B, nt),
        in_specs=[pl.BlockSpec((1, tm, D), lambda b, t: (b, t, 0)),
                  pl.BlockSpec((1, D), lambda b, t: (0, 0)),
                  pl.BlockSpec((1, 1, 8, D), lambda b, t: (b, jnp.where(t < nct, 0, 1), 0, 0)),
                  pl.BlockSpec((tm, D), lambda b, t: (b * nt + t, 0)),
                  pl.BlockSpec((tm, D), lambda b, t: (b * (nt - nct) + lat(b, t), 0))],
        out_specs=[pl.BlockSpec((1, tm, D), lambda b, t: (b, lat(b, t), 0)),
                   pl.BlockSpec((1, 1, 8, D), lambda b, t: (b, jnp.where(t < nct, 0, 1), 0, 0))],
        out_shape=[jax.ShapeDtypeStruct((B, N, D), F32), jax.ShapeDtypeStruct((B, 2, 8, D), F32)],
        compiler_params=_params(("parallel", "arbitrary")),
    )(h, gain, tab, du, dres)


def _merge_math(mgd, mgl, yd, yl, bd, bl):
    return _sigmoid(mgd + bd) * yd + _sigmoid(mgl + bl) * yl


def _merge_fwd(p, ydn, ylru, b_merge, *, B, T, nc, D, col0, tm):
    N = T - nc
    ntl, nt, nct, cb = N // tm, T // tm, nc // tm, col0 // D

    def body(mgd_ref, mgl_ref, yd_ref, yl_ref, bm_ref, o_ref):
        o_ref[...] = _merge_math(mgd_ref[...], mgl_ref[...], yd_ref[...], yl_ref[...],
                                 bm_ref[:, 0:D], bm_ref[:, D:2 * D]).astype(BF16)

    prow = lambda b, t: b * nt + nct + t
    return pl.pallas_call(
        body, name="merge_fwd", grid=(B, ntl),
        in_specs=[pl.BlockSpec((tm, D), lambda b, t: (prow(b, t), cb)),
                  pl.BlockSpec((tm, D), lambda b, t: (prow(b, t), cb + 1)),
                  pl.BlockSpec((tm, D), lambda b, t: (b * ntl + t, 0)),
                  pl.BlockSpec((tm, D), lambda b, t: (b * ntl + t, 0)),
                  pl.BlockSpec((1, 2 * D), lambda b, t: (0, 0))],
        out_specs=pl.BlockSpec((tm, D), lambda b, t: (b * ntl + t, 0)),
        out_shape=jax.ShapeDtypeStruct((B * N, D), BF16),
        compiler_params=_params(("parallel", "parallel")),
    )(p, p, ydn, ylru, b_merge)


def _merge_bwd(p, ydn, ylru, b_merge, dmix, dp, *, B, T, nc, D, col0, tm):
    N = T - nc
    ntl, nt, nct, cb = N // tm, T // tm, nc // tm, col0 // D
    assert col0 % (2 * D) == 0

    def body(mgd_ref, mgl_ref, yd_ref, yl_ref, bm_ref, dm_ref, dp_any, dyd_ref, dyl_ref, dp_ref, sums_ref):
        _, vjp = jax.vjp(_merge_math, mgd_ref[...], mgl_ref[...], yd_ref[...], yl_ref[...],
                         bm_ref[:, 0:D], bm_ref[:, D:2 * D])
        dmgd, dmgl, dyd, dyl, dbd, dbl = vjp(dm_ref[...])
        dyd_ref[...] = dyd.astype(BF16)
        dyl_ref[...] = dyl.astype(BF16)
        dp_ref[:, 0:D] = dmgd.astype(BF16)
        dp_ref[:, D:2 * D] = dmgl.astype(BF16)

        @pl.when((pl.program_id(0) == 0) & (pl.program_id(1) == 0))
        def _():
            sums_ref[...] = jnp.zeros_like(sums_ref)

        sums_ref[0:1, 0:D] += dbd
        sums_ref[0:1, D:2 * D] += dbl

    prow = lambda b, t: b * nt + nct + t
    row = pl.BlockSpec((tm, D), lambda b, t: (b * ntl + t, 0))
    return pl.pallas_call(
        body, name="merge_bwd", grid=(B, ntl),
        in_specs=[pl.BlockSpec((tm, D), lambda b, t: (prow(b, t), cb)),
                  pl.BlockSpec((tm, D), lambda b, t: (prow(b, t), cb + 1)),
                  row, row, pl.BlockSpec((1, 2 * D), lambda b, t: (0, 0)), row,
                  pl.BlockSpec(memory_space=pl.ANY)],
        out_specs=[row, row,
                   pl.BlockSpec((tm, 2 * D), lambda b, t: (prow(b, t), cb // 2)),
                   pl.BlockSpec((8, 2 * D), lambda b, t: (0, 0))],
        out_shape=[jax.ShapeDtypeStruct((B * N, D), BF16), jax.ShapeDtypeStruct((B * N, D), BF16),
                   jax.ShapeDtypeStruct(dp.shape, dp.dtype), jax.ShapeDtypeStruct((8, 2 * D), F32)],
        input_output_aliases={6: 2},
        compiler_params=_params(("arbitrary", "arbitrary")),
    )(p, p, ydn, ylru, b_merge, dmix, dp)


def _post_math(x, mix, g1, gate, g2, sh, sc):
    h1 = x + _rmsn(mix, g1) * gate
    return h1, _rmsn(h1, g2) * (1.0 + sc) + sh


def _post_fwd(x, mix, gains, vecs, *, tm):
    B, N, D = x.shape
    ntl = N // tm

    def body(x_ref, mix_ref, g_ref, v_ref, h1_ref, u2_ref):
        v = v_ref[0]
        h1, u2 = _post_math(x_ref[0], mix_ref[...], g_ref[0:1], v[0:1], g_ref[1:2], v[1:2], v[2:3])
        h1_ref[...] = h1
        u2_ref[...] = u2.astype(BF16)

    row = pl.BlockSpec((tm, D), lambda b, t: (b * ntl + t, 0))
    return pl.pallas_call(
        body, name="post_fwd", grid=(B, ntl),
        in_specs=[pl.BlockSpec((1, tm, D), lambda b, t: (b, t, 0)), row,
                  pl.BlockSpec((8, D), lambda b, t: (0, 0)), pl.BlockSpec((1, 8, D), lambda b, t: (b, 0, 0))],
        out_specs=[row, row],
        out_shape=[jax.ShapeDtypeStruct((B * N, D), F32), jax.ShapeDtypeStruct((B * N, D), BF16)],
        compiler_params=_params(("parallel", "parallel")),
    )(x, mix, gains, vecs)


def _post_bwd(x, mix, gains, vecs, dh1, du2, *, tm):
    B, N, D = x.shape
    ntl = N // tm

    def body(x_ref, mix_ref, g_ref, v_ref, dh1_ref, du2_ref, dx_ref, dmix_ref, sums_ref):
        v = v_ref[0]
        _, vjp = jax.vjp(_post_math, x_ref[0], mix_ref[...], g_ref[0:1], v[0:1], g_ref[1:2], v[1:2], v[2:3])
        dx, dmix, dg1, dgate, dg2, dsh, dsc = vjp((dh1_ref[...], du2_ref[...]))
        dx_ref[...] = dx
        dmix_ref[...] = dmix.astype(BF16)

        @pl.when(pl.program_id(1) == 0)
        def _():
            sums_ref[...] = jnp.zeros_like(sums_ref)

        sums_ref[0, 0:1, :] += dgate
        sums_ref[0, 1:2, :] += dsh
        sums_ref[0, 2:3, :] += dsc
        sums_ref[0, 3:4, :] += dg1
        sums_ref[0, 4:5, :] += dg2

    row = pl.BlockSpec((tm, D), lambda b, t: (b * ntl + t, 0))
    return pl.pallas_call(
        body, name="post_bwd", grid=(B, ntl),
        in_specs=[pl.BlockSpec((1, tm, D), lambda b, t: (b, t, 0)), row,
                  pl.BlockSpec((8, D), lambda b, t: (0, 0)), pl.BlockSpec((1, 8, D), lambda b, t: (b, 0, 0)), row, row],
        out_specs=[row, row, pl.BlockSpec((1, 8, D), lambda b, t: (b, 0, 0))],
        out_shape=[jax.ShapeDtypeStruct((B * N, D), F32), jax.ShapeDtypeStruct((B * N, D), BF16),
                   jax.ShapeDtypeStruct((B, 8, D), F32)],
        compiler_params=_params(("parallel", "arbitrary")),
    )(x, mix, gains, vecs, dh1, du2)


def _final_math(dn, g4, gate5):
    return _rmsn(dn, g4) * gate5


def _final(h1, dn, target, gains, vecs, *, tm):
    B, N, D = target.shape
    ntl = N // tm

    def body(h1_ref, dn_ref, t_ref, g_ref, v_ref, ddn_ref, dout_ref, sums_ref):
        v = v_ref[0]
        y, vjp = jax.vjp(_final_math, dn_ref[...], g_ref[2:3], v[3:4])
        err = h1_ref[...] + y - t_ref[0]
        dout = err * (1.0 / D)
        ddn, dg4, dgate5 = vjp(dout)
        ddn_ref[...] = ddn.astype(BF16)
        dout_ref[...] = dout

        @pl.when(pl.program_id(1) == 0)
        def _():
            sums_ref[...] = jnp.zeros_like(sums_ref)

        sums_ref[0, 0:1, :] += dgate5
        sums_ref[0, 1:2, :] += dg4
        sums_ref[0, 2:3, :] += jnp.sum(err * err, axis=0, keepdims=True) * (0.5 / D)

    row = pl.BlockSpec((tm, D), lambda b, t: (b * ntl + t, 0))
    return pl.pallas_call(
        body, name="final", grid=(B, ntl),
        in_specs=[row, row, pl.BlockSpec((1, tm, D), lambda b, t: (b, t, 0)),
                  pl.BlockSpec((8, D), lambda b, t: (0, 0)), pl.BlockSpec((1, 8, D), lambda b, t: (b, 0, 0))],
        out_specs=[row, row, pl.BlockSpec((1, 8, D), lambda b, t: (b, 0, 0))],
        out_shape=[jax.ShapeDtypeStruct((B * N, D), BF16), jax.ShapeDtypeStruct((B * N, D), F32),
                   jax.ShapeDtypeStruct((B, 8, D), F32)],
        compiler_params=_params(("parallel", "arbitrary")),
    )(h1, dn, target, gains, vecs)


def _shift(x, s):
    s = s % x.shape[0]
    return x if s == 0 else pltpu.roll(x, s, 0)


def _seg_taps(T, nc, width, pad_left):
    t = lax.broadcasted_iota(jnp.int32, (T, 1), 0)
    pos = jnp.where(t < nc, t, t - nc)
    seg = jnp.where(t < nc, nc, T - nc)
    taps = []
    for k in range(width):
        src = pos + (k - pad_left)
        taps.append((pad_left - k, (src >= 0) & (src < seg)))
    return taps


def _grid_taps(N):
    t = lax.broadcasted_iota(jnp.int32, (N, 1), 0)
    wcol = t % GRID_W
    taps = []
    for dr in (-1, 0, 1):
        for dw in (-1, 0, 1):
            off = dr * GRID_W + dw
            ok = (wcol + dw >= 0) & (wcol + dw < GRID_W) & (t + dr * GRID_W >= 0) & (t + dr * GRID_W < N)
            taps.append((-off, ok))
    return taps


def _conv_fwd(x, w, taps):
    y = jnp.zeros_like(x)
    for k, (s, m) in enumerate(taps):
        y = y + w[k:k + 1] * jnp.where(m, _shift(x, s), 0.0)
    return y


def _conv_bwd(x, w, taps, dy):
    dx = jnp.zeros_like(x)
    dws = []
    for k, (s, m) in enumerate(taps):
        dym = jnp.where(m, dy, 0.0)
        dx = dx + w[k:k + 1] * _shift(dym, -s)
        dws.append(jnp.sum(dym * _shift(x, s), axis=0, keepdims=True))
    return dx, jnp.concatenate(dws, axis=0)


def _ffn_act_fwd(F, w9, bias, *, B, N, DFF, tc):
    nj = DFF // tc

    def body(fg_ref, fv_ref, w_ref, b_ref, o_ref):
        fg = _conv_fwd(fg_ref[...], w_ref[...], _grid_taps(N)) + b_ref[...]
        o_ref[...] = (_gelu(fg) * fv_ref[...]).astype(BF16)

    return pl.pallas_call(
        body, name="ffn_act_fwd", grid=(B, nj),
        in_specs=[pl.BlockSpec((N, tc), lambda b, j: (b, j)), pl.BlockSpec((N, tc), lambda b, j: (b, nj + j)),
                  pl.BlockSpec((9, tc), lambda b, j: (0, j)), pl.BlockSpec((1, tc), lambda b, j: (0, j))],
        out_specs=pl.BlockSpec((N, tc), lambda b, j: (b, j)),
        out_shape=jax.ShapeDtypeStruct((B * N, DFF), BF16),
        compiler_params=_params(("parallel", "parallel")),
    )(F, F, w9, bias)


def _ffn_act_bwd(F, w9, bias, df, *, B, N, DFF, tc):
    nj = DFF // tc

    def body(fg_ref, fv_ref, w_ref, b_ref, df_ref, dfg_ref, dfv_ref, dwb_ref):
        taps = _grid_taps(N)
        x = fg_ref[...]
        fg, vjp = jax.vjp(lambda a: _gelu(a), _conv_fwd(x, w_ref[...], taps) + b_ref[...])
        dfl = df_ref[...]
        dfv_ref[...] = (dfl * fg).astype(BF16)
        (dpre,) = vjp(dfl * fv_ref[...])
        dx, dw = _conv_bwd(x, w_ref[...], taps, dpre)
        dfg_ref[...] = dx.astype(BF16)

        @pl.when(pl.program_id(1) == 0)
        def _():
            dwb_ref[...] = jnp.zeros_like(dwb_ref)

        dwb_ref[0:9, :] += dw
        dwb_ref[9:10, :] += jnp.sum(dpre, axis=0, keepdims=True)

    col = pl.BlockSpec((N, tc), lambda j, b: (b, j))
    return pl.pallas_call(
        body, name="ffn_act_bwd", grid=(nj, B),
        in_specs=[col, pl.BlockSpec((N, tc), lambda j, b: (b, nj + j)),
                  pl.BlockSpec((9, tc), lambda j, b: (0, j)), pl.BlockSpec((1, tc), lambda j, b: (0, j)), col],
        out_specs=[col, col, pl.BlockSpec((16, tc), lambda j, b: (0, j))],
        out_shape=[jax.ShapeDtypeStruct((B * N, DFF), BF16), jax.ShapeDtypeStruct((B * N, DFF), BF16),
                   jax.ShapeDtypeStruct((16, DFF), F32)],
        compiler_params=_params(("parallel", "arbitrary")),
    )(F, F, w9, bias, df)


def _dnprep_math(y, is_qk, scale):
    s = _silu(y)
    n = s * lax.rsqrt(jnp.sum(s * s, axis=-1, keepdims=True) + EPS) * scale
    return jnp.where(is_qk, n, s)


def _dnprep_fwd(p, cw, *, B, T, nc, H, HD):
    def body(x_ref, w_ref, o_ref):
        j = pl.program_id(1)
        y = _conv_fwd(x_ref[...], w_ref[...], _seg_taps(T, nc, 4, 2))
        o_ref[...] = _dnprep_math(y, j < 2 * H, jnp.where(j < H, HD ** -0.5, 1.0))

    return pl.pallas_call(
        body, name="dnprep_fwd", grid=(B, 3 * H),
        in_specs=[pl.BlockSpec((T, HD), lambda b, j: (b, j)), pl.BlockSpec((4, HD), lambda b, j: (0, j))],
        out_specs=pl.BlockSpec((T, HD), lambda b, j: (b, j)),
        out_shape=jax.ShapeDtypeStruct((B * T, 3 * H * HD), F32),
        compiler_params=_params(("parallel", "parallel")),
    )(p, cw)


def _dnprep_bwd(p, cw, dqkv, dp, *, B, T, nc, H, HD):
    def body(x_ref, w_ref, d_ref, dp_any, dp_ref, dcw_ref):
        j = pl.program_id(0)
        taps = _seg_taps(T, nc, 4, 2)
        x = x_ref[...]
        y = _conv_fwd(x, w_ref[...], taps)
        is_qk, scale = j < 2 * H, jnp.where(j < H, HD ** -0.5, 1.0)
        _, vjp = jax.vjp(lambda a: _dnprep_math(a, is_qk, scale), y)
        (dy,) = vjp(d_ref[0])
        dx, dw = _conv_bwd(x, w_ref[...], taps, dy)
        dp_ref[...] = dx.astype(BF16)

        @pl.when(pl.program_id(1) == 0)
        def _():
            dcw_ref[...] = jnp.zeros_like(dcw_ref)

        dcw_ref[0:4, :] += dw

    col = pl.BlockSpec((T, HD), lambda j, b: (b, j))
    return pl.pallas_call(
        body, name="dnprep_bwd", grid=(3 * H, B),
        in_specs=[col, pl.BlockSpec((4, HD), lambda j, b: (0, j)),
                  pl.BlockSpec((1, T, HD), lambda j, b: (j // H, b, j % H)), pl.BlockSpec(memory_space=pl.ANY)],
        out_specs=[col, pl.BlockSpec((8, HD), lambda j, b: (0, j))],
        out_shape=[jax.ShapeDtypeStruct(dp.shape, dp.dtype), jax.ShapeDtypeStruct((8, 3 * H * HD), F32)],
        input_output_aliases={3: 0},
        compiler_params=_params(("parallel", "arbitrary")),
    )(p, cw, dqkv, dp)


def _gb_math(ab, alog, dtb, H):
    lane = lax.broadcasted_iota(jnp.int32, ab.shape, 1)
    g = -jnp.exp(alog) * _softplus(ab + dtb)
    return jnp.where(lane < 2 * H, g, jnp.where(lane < 4 * H, _sigmoid(ab), 0.0))


def _gb_fwd(p, prm, *, rows, col0, H, tm):
    def body(x_ref, prm_ref, o_ref):
        o_ref[...] = _gb_math(x_ref[...], prm_ref[0:1], prm_ref[1:2], H)

    return pl.pallas_call(
        body, name="gb_fwd", grid=(rows // tm,),
        in_specs=[pl.BlockSpec((tm, LANES), lambda t: (t, col0 // LANES)), pl.BlockSpec((8, LANES), lambda t: (0, 0))],
        out_specs=pl.BlockSpec((tm, LANES), lambda t: (t, 0)),
        out_shape=jax.ShapeDtypeStruct((rows, LANES), F32),
        compiler_params=_params(("parallel",)),
    )(p, prm)


def _gb_bwd(p, prm, dgb, dp, *, rows, col0, H, tm):
    def body(x_ref, prm_ref, d_ref, dp_any, dp_ref, dprm_ref):
        _, vjp = jax.vjp(lambda a, b, c: _gb_math(a, b, c, H), x_ref[...], prm_ref[0:1], prm_ref[1:2])
        dab, dalog, ddtb = vjp(d_ref[...])
        dp_ref[...] = dab.astype(BF16)

        @pl.when(pl.program_id(0) == 0)
        def _():
            dprm_ref[...] = jnp.zeros_like(dprm_ref)

        dprm_ref[0:1, :] += dalog
        dprm_ref[1:2, :] += ddtb

    blk = pl.BlockSpec((tm, LANES), lambda t: (t, col0 // LANES))
    return pl.pallas_call(
        body, name="gb_bwd", grid=(rows // tm,),
        in_specs=[blk, pl.BlockSpec((8, LANES), lambda t: (0, 0)), pl.BlockSpec((tm, LANES), lambda t: (t, 0)),
                  pl.BlockSpec(memory_space=pl.ANY)],
        out_specs=[blk, pl.BlockSpec((8, LANES), lambda t: (0, 0))],
        out_shape=[jax.ShapeDtypeStruct(dp.shape, dp.dtype), jax.ShapeDtypeStruct((8, LANES), F32)],
        input_output_aliases={3: 0},
        compiler_params=_params(("arbitrary",)),
    )(p, prm, dgb, dp)


def _lru_scan(a_ref, b_ref, h_ref, hp_ref, segs):
    C = a_ref.shape[1]
    row = lax.broadcasted_iota(jnp.int32, (SUBLANES, C), 0)
    carry = jnp.zeros((1, C), F32)
    for start, rows, reverse in segs:
        nb = rows // SUBLANES

        def blk(i, carry, start=start, nb=nb, reverse=reverse):
            r0 = pl.multiple_of(start + (nb - 1 - i if reverse else i) * SUBLANES, SUBLANES)
            A = a_ref[pl.ds(r0, SUBLANES), :]
            Bv = b_ref[pl.ds(r0, SUBLANES), :]
            for s in (1, 2, 4):
                sh = SUBLANES - s if reverse else s
                m = (row < SUBLANES - s) if reverse else (row >= s)
                Bv = jnp.where(m, A * pltpu.roll(Bv, sh, 0) + Bv, Bv)
                A = jnp.where(m, A * pltpu.roll(A, sh, 0), A)
            Hv = Bv + A * carry
            h_ref[pl.ds(r0, SUBLANES), :] = Hv
            if hp_ref is not None:
                if reverse:
                    hp = jnp.where(row < SUBLANES - 1, pltpu.roll(Hv, SUBLANES - 1, 0), carry)
                else:
                    hp = jnp.where(row >= 1, pltpu.roll(Hv, 1, 0), carry)
                hp_ref[pl.ds(r0, SUBLANES), :] = hp
            return Hv[0:1] if reverse else Hv[SUBLANES - 1:SUBLANES]

        carry = lax.fori_loop(0, nb, blk, carry)


def _lru_orders(T, nc, d):
    if d == 0:
        return [(0, T, False)], [(0, T, True)]
    return [(0, nc, True), (nc, T - nc, True)], [(nc, T - nc, False), (0, nc, False)]


def _bdot(a, b, dims=(((1,), (0,)), ((), ()))):
    return lax.dot_general(a.astype(BF16), b.astype(BF16), dims, preferred_element_type=F32)


_NT = (((1,), (1,)), ((), ()))
_TN = (((0,), (0,)), ((), ()))


def _blockdiag(w, C):
    nd, nb, bd, _ = w.shape
    per = C // bd
    out = jnp.einsum('dnpij,pq->dnpiqj', w.reshape(nd, nb // per, per, bd, bd), jnp.eye(per, dtype=w.dtype))
    return out.reshape(nd, nb // per, C, C)


def _blockdiag_extract(dw, bd):
    nd, nj, C, _ = dw.shape
    per = C // bd
    out = jnp.einsum('dnpiqj,pq->dnpij', dw.reshape(nd, nj, per, bd, per, bd), jnp.eye(per, dtype=dw.dtype))
    return out.reshape(nd, nj * per, bd, bd)


def _lru_fwd(p, cw, lv, wr, wi, *, B, T, nc, LW, col0, C):
    N = T - nc
    nj = LW // C

    def body(x_ref, cw_ref, lv_ref, wr_ref, wi_ref, o_ref, a_s, b_s, h_s, acc_s):
        lv_ = lv_ref[...]
        xc = _conv_fwd(x_ref[:, 0:C], cw_ref[...], _seg_taps(T, nc, 4, 2)) + lv_[0:1]
        for d in (0, 1):
            r = _sigmoid(_bdot(xc, wr_ref[d, 0]) + lv_[1 + d:2 + d])
            i = _sigmoid(_bdot(xc, wi_ref[d, 0]) + lv_[3 + d:4 + d])
            la = -LRU_C * r * _softplus(-lv_[5 + d:6 + d])
            a_s[...] = jnp.exp(la)
            b_s[...] = jnp.sqrt(1.0 - jnp.exp(2.0 * la)) * i * xc
            _lru_scan(a_s, b_s, h_s, None, _lru_orders(T, nc, d)[0])
            if d == 0:
                acc_s[...] = h_s[...]
            else:
                acc_s[...] += h_s[...]
        o_ref[...] = (acc_s[nc:, :] * _gelu(x_ref[nc:, C:2 * C])).astype(BF16)

    return pl.pallas_call(
        body, name="lru_fwd", grid=(B, nj),
        in_specs=[pl.BlockSpec((T, 2 * C), lambda b, j: (b, col0 // (2 * C) + j)),
                  pl.BlockSpec((4, C), lambda b, j: (0, j)), pl.BlockSpec((8, C), lambda b, j: (0, j)),
                  pl.BlockSpec((2, 1, C, C), lambda b, j: (0, j, 0, 0)), pl.BlockSpec((2, 1, C, C), lambda b, j: (0, j, 0, 0))],
        out_specs=pl.BlockSpec((N, C), lambda b, j: (b, j)),
        out_shape=jax.ShapeDtypeStruct((B * N, LW), BF16),
        scratch_shapes=[pltpu.VMEM((T, C), F32)] * 4,
        compiler_params=_params(("parallel", "parallel")),
    )(p, cw, lv, wr, wi)


def _lru_bwd(p, cw, lv, wr, wi, dy, dp, *, B, T, nc, LW, col0, C):
    N = T - nc
    nj = LW // C

    def body(x_ref, cw_ref, lv_ref, wr_ref, wi_ref, dy_ref, dp_any, dp_ref, dcw_ref, dlv_ref, dwr_ref, dwi_ref,
             a_s, b_s, h_s, hp_s, mu_s, mup_s, dh_s, dxc_s, hsum_s):
        taps = _seg_taps(T, nc, 4, 2)
        lv_ = lv_ref[...]
        xl = x_ref[:, 0:C]
        xc = _conv_fwd(xl, cw_ref[...], taps) + lv_[0:1]
        gel, gelu_vjp = jax.vjp(_gelu, x_ref[nc:, C:2 * C])
        dh_s[0:nc, :] = jnp.zeros((nc, C), F32)
        dh_s[nc:, :] = dy_ref[...] * gel
        dxc_s[...] = jnp.zeros_like(dxc_s)

        @pl.when(pl.program_id(1) == 0)
        def _():
            dcw_ref[...] = jnp.zeros_like(dcw_ref)
            dlv_ref[...] = jnp.zeros_like(dlv_ref)
            dwr_ref[...] = jnp.zeros_like(dwr_ref)
            dwi_ref[...] = jnp.zeros_like(dwi_ref)

        for d in (0, 1):
            fwd_order, adj_order = _lru_orders(T, nc, d)
            lam = lv_[5 + d:6 + d]
            r = _sigmoid(_bdot(xc, wr_ref[d, 0]) + lv_[1 + d:2 + d])
            i = _sigmoid(_bdot(xc, wi_ref[d, 0]) + lv_[3 + d:4 + d])
            sp = _softplus(-lam)
            la = -LRU_C * r * sp
            a = jnp.exp(la)
            e2 = jnp.exp(2.0 * la)
            mult = jnp.sqrt(1.0 - e2)
            a_s[...] = a
            b_s[...] = mult * i * xc
            _lru_scan(a_s, b_s, h_s, hp_s, fwd_order)
            if d == 0:
                hsum_s[...] = h_s[...]
            else:
                hsum_s[...] += h_s[...]
            b_s[...] = a * dh_s[...]
            _lru_scan(a_s, b_s, mu_s, mup_s, adj_order)
            dinp = dh_s[...] + mup_s[...]
            da = dinp * hp_s[...]
            dmult = dinp * i * xc
            di = dinp * mult * xc
            dla = da * a - dmult * e2 / mult
            dpre_r = (dla * (-LRU_C * sp)) * r * (1.0 - r)
            dpre_i = di * i * (1.0 - i)
            dsp = jnp.sum(dla * (-LRU_C * r), axis=0, keepdims=True)
            dxc_s[...] += dinp * mult * i + _bdot(dpre_r, wr_ref[d, 0], _NT) + _bdot(dpre_i, wi_ref[d, 0], _NT)
            dwr_ref[d, 0] += _bdot(xc, dpre_r, _TN)
            dwi_ref[d, 0] += _bdot(xc, dpre_i, _TN)
            dlv_ref[1 + d:2 + d, :] += jnp.sum(dpre_r, axis=0, keepdims=True)
            dlv_ref[3 + d:4 + d, :] += jnp.sum(dpre_i, axis=0, keepdims=True)
            dlv_ref[5 + d:6 + d, :] += -dsp * _sigmoid(-lam)

        dxc = dxc_s[...]
        dxl, dw = _conv_bwd(xl, cw_ref[...], taps, dxc)
        dcw_ref[0:4, :] += dw
        dlv_ref[0:1, :] += jnp.sum(dxc, axis=0, keepdims=True)
        dp_ref[:, 0:C] = dxl.astype(BF16)
        (dyl,) = gelu_vjp(dy_ref[...] * hsum_s[nc:, :])
        dp_ref[0:nc, C:2 * C] = jnp.zeros((nc, C), BF16)
        dp_ref[nc:, C:2 * C] = dyl.astype(BF16)

    xblk = pl.BlockSpec((T, 2 * C), lambda j, b: (b, col0 // (2 * C) + j))
    wblk = pl.BlockSpec((2, 1, C, C), lambda j, b: (0, j, 0, 0))
    vblk = pl.BlockSpec((8, C), lambda j, b: (0, j))
    return pl.pallas_call(
        body, name="lru_bwd", grid=(nj, B),
        in_specs=[xblk, pl.BlockSpec((4, C), lambda j, b: (0, j)), vblk, wblk, wblk,
                  pl.BlockSpec((N, C), lambda j, b: (b, j)), pl.BlockSpec(memory_space=pl.ANY)],
        out_specs=[xblk, vblk, vblk, wblk, wblk],
        out_shape=[jax.ShapeDtypeStruct(dp.shape, dp.dtype), jax.ShapeDtypeStruct((8, LW), F32),
                   jax.ShapeDtypeStruct((8, LW), F32), jax.ShapeDtypeStruct((2, nj, C, C), F32),
                   jax.ShapeDtypeStruct((2, nj, C, C), F32)],
        scratch_shapes=[pltpu.VMEM((T, C), F32)] * 9,
        input_output_aliases={6: 0},
        compiler_params=_params(("parallel", "arbitrary")),
    )(p, cw, lv, wr, wi, dy, dp)


def _chunk_masks(upper):
    i = lax.broadcasted_iota(jnp.int32, (CHUNK, CHUNK), 0)
    j = lax.broadcasted_iota(jnp.int32, (CHUNK, CHUNK), 1)
    return i == j, (j >= i) if upper else (j <= i), (j > i) if upper else (j < i)


def _col2row(c, eye):
    return jnp.sum(jnp.where(eye, c, 0.0), axis=0, keepdims=True)


def _row2col(r, eye):
    return jnp.sum(jnp.where(eye, r, 0.0), axis=1, keepdims=True)


def _rowsum(x):
    return jnp.sum(x, axis=1, keepdims=True)


def _dot3(a, b):
    ah, bh = a.astype(BF16), b.astype(BF16)
    al, bl = (a - ah.astype(F32)).astype(BF16), (b - bh.astype(F32)).astype(BF16)
    dot = functools.partial(jnp.dot, preferred_element_type=F32)
    return dot(ah, bh) + (dot(ah, bl) + dot(al, bh))


def _unit_tri_inverse(L, eye):
    X = -L
    R = jnp.where(eye, 1.0, 0.0) + X
    Xp = X
    for _ in range(int(math.log2(CHUNK)) - 1):
        Xp = _dot3(Xp, Xp)
        R = R + _dot3(R, Xp)
    return R


def _delta_chunk_common(q, k, v, gcol, bcol, upper):
    eye, incl, strict = _chunk_masks(upper)
    gc = _rowsum(jnp.where(incl, _col2row(gcol, eye), 0.0))
    D = jnp.where(incl, jnp.exp(jnp.minimum(gc - _col2row(gc, eye), 0.0)), 0.0)
    kb = k * bcol
    A = _bdot(kb, k, _NT)
    L = jnp.where(strict, A * D, 0.0)
    eg = jnp.exp(gc)
    gl = jnp.sum(gcol, axis=0, keepdims=True)
    P = _bdot(q, k, _NT)
    attn = jnp.where(incl, P * D, 0.0)
    return dict(eye=eye, incl=incl, strict=strict, gc=gc, D=D, kb=kb, A=A, L=L, eg=eg, gl=gl, egl=jnp.exp(gl),
                attn=attn, kbe=kb * eg, vb=v * bcol, qe=q * eg, kd=k * jnp.exp(gl - gc))


def _delta_chunk_fwd(q, k, v, gcol, bcol, S, upper):
    c = _delta_chunk_common(q, k, v, gcol, bcol, upper)
    Tm = _unit_tri_inverse(c["L"], c["eye"])
    w = _bdot(Tm, c["kbe"])
    u = _bdot(Tm, c["vb"])
    vn = u - _bdot(w, S)
    o = _bdot(c["qe"], S) + _bdot(c["attn"], vn)
    S2 = S * c["egl"] + _bdot(c["kd"], vn, _TN)
    return o, S2, Tm, vn


def _delta_chunk_bwd(q, k, v, gcol, bcol, S, Tm, vn, do, dS2, upper):
    c = _delta_chunk_common(q, k, v, gcol, bcol, upper)
    eye, incl, strict, D, eg, egl = c["eye"], c["incl"], c["strict"], c["D"], c["eg"], c["egl"]
    kb, kbe, vb, qe, kd, attn = c["kb"], c["kbe"], c["vb"], c["qe"], c["kd"], c["attn"]
    w = _bdot(Tm, kbe)
    dvn = _bdot(kd, dS2) + _bdot(attn, do, _TN)
    dkd = _bdot(vn, dS2, _NT)
    dgl = jnp.sum(_rowsum(dS2 * S), axis=0, keepdims=True) * egl
    dqe = _bdot(do, S, _NT)
    dattn = jnp.where(incl, _bdot(do, vn, _NT), 0.0)
    dw = -_bdot(dvn, S, _NT)
    dS = dS2 * egl + _bdot(qe, do, _TN) - _bdot(w, dvn, _TN)
    r = _rowsum(dkd * kd)
    dk = dkd * jnp.exp(c["gl"] - c["gc"])
    dgl = dgl + jnp.sum(r, axis=0, keepdims=True)
    dgc = _rowsum(dqe * qe) - r
    dq = dqe * eg + _bdot(dattn * D, k)
    dk = dk + _bdot(dattn * D, q, _TN)
    E = dattn * attn
    dTm = _bdot(dvn, vb, _NT) + _bdot(dw, kbe, _NT)
    dvb = _bdot(Tm, dvn, _TN)
    dv = dvb * bcol
    dbeta = _rowsum(dvb * v)
    dkbe = _bdot(Tm, dw, _TN)
    dkb = dkbe * eg
    dgc = dgc + _rowsum(dkbe * kbe)
    dL = jnp.where(strict, -_bdot(Tm, _bdot(dTm, Tm, _NT), _TN), 0.0)
    dA = dL * D
    E = E + dL * c["L"]
    dkb = dkb + _bdot(dA, k)
    dk = dk + _bdot(dA, kb, _TN) + dkb * bcol
    dbeta = dbeta + _rowsum(dkb * k)
    dgc = dgc + _rowsum(E) - _row2col(jnp.sum(E, axis=0, keepdims=True), eye)
    dg = _row2col(jnp.sum(jnp.where(incl, dgc, 0.0), axis=0, keepdims=True), eye) + dgl
    return dq, dk, dv, dg, dbeta, dS


def _delta_orders(T, nc, d):
    n, ncc = T // CHUNK, nc // CHUNK
    return [(0, n, False)] if d == 0 else [(0, ncc, True), (ncc, n - ncc, True)]


def _dn_out_math(o, onorm, z):
    return _rmsn(o, onorm) * _silu(z)


def _delta_fwd(qkv, gb, p, onorm, *, B, T, nc, H, HD):
    N = T - nc

    def body(q_ref, k_ref, v_ref, gb_ref, z_ref, on_ref, y_ref, o_s):
        h = pl.program_id(1)
        lane = lax.broadcasted_iota(jnp.int32, (CHUNK, LANES), 1)
        for d in (0, 1):
            S = jnp.zeros((HD, HD), F32)
            for first, count, desc in _delta_orders(T, nc, d):
                def chunk(i, S, d=d, first=first, count=count, desc=desc):
                    rows = pl.ds(pl.multiple_of((first + (count - 1 - i if desc else i)) * CHUNK, CHUNK), CHUNK)
                    gbb = gb_ref[rows, :]
                    gcol = _rowsum(jnp.where(lane == d * H + h, gbb, 0.0))
                    bcol = _rowsum(jnp.where(lane == 2 * H + d * H + h, gbb, 0.0))
                    o, S2, _, _ = _delta_chunk_fwd(q_ref[rows, :], k_ref[rows, :], v_ref[rows, :], gcol, bcol, S, d == 1)
                    if d == 0:
                        o_s[rows, :] = o
                    else:
                        o_s[rows, :] += o
                    return S2

                S = lax.fori_loop(0, count, chunk, S)
        y_ref[...] = _dn_out_math(o_s[nc:, :], on_ref[...], z_ref[nc:, :]).astype(BF16)

    col = lambda off: pl.BlockSpec((T, HD), lambda b, h: (b, off + h))
    return pl.pallas_call(
        body, name="delta_fwd", grid=(B, H),
        in_specs=[col(0), col(H), col(2 * H), pl.BlockSpec((T, LANES), lambda b, h: (b, 0)), col(3 * H),
                  pl.BlockSpec((1, HD), lambda b, h: (0, 0))],
        out_specs=pl.BlockSpec((N, HD), lambda b, h: (b, h)),
        out_shape=jax.ShapeDtypeStruct((B * N, H * HD), BF16),
        scratch_shapes=[pltpu.VMEM((T, HD), F32)],
        compiler_params=_params(("parallel", "parallel")),
    )(qkv, qkv, qkv, gb, p, onorm)


def _delta_bwd(qkv, gb, p, onorm, dy, dp, *, B, T, nc, H, HD):
    N = T - nc
    n = T // CHUNK

    def body(q_ref, k_ref, v_ref, gb_ref, z_ref, on_ref, dy_ref, dp_any, dqkv_ref, dgb_ref, dp_ref, don_ref,
             o_s, do_s, vn_s, S_s, Tm_s):
        h = pl.program_id(1)
        lane = lax.broadcasted_iota(jnp.int32, (CHUNK, LANES), 1)

        def cols(rows, d):
            gbb = gb_ref[rows, :]
            return (_rowsum(jnp.where(lane == d * H + h, gbb, 0.0)),
                    _rowsum(jnp.where(lane == 2 * H + d * H + h, gbb, 0.0)))

        def rows_of(first, count, desc, i):
            c = first + (count - 1 - i if desc else i)
            return c, pl.ds(pl.multiple_of(c * CHUNK, CHUNK), CHUNK)

        for d in (0, 1):
            S = jnp.zeros((HD, HD), F32)
            for first, count, desc in _delta_orders(T, nc, d):
                def chunk(i, S, d=d, first=first, count=count, desc=desc):
                    c, rows = rows_of(first, count, desc, i)
                    gcol, bcol = cols(rows, d)
                    o, S2, Tm, vn = _delta_chunk_fwd(q_ref[rows, :], k_ref[rows, :], v_ref[rows, :], gcol, bcol, S, d == 1)
                    S_s[d * n + c] = S
                    Tm_s[d * n + c] = Tm
                    vn_s[d, rows, :] = vn
                    if d == 0:
                        o_s[rows, :] = o
                    else:
                        o_s[rows, :] += o
                    return S2

                S = lax.fori_loop(0, count, chunk, S)

        _, vjp = jax.vjp(_dn_out_math, o_s[nc:, :], on_ref[...], z_ref[nc:, :])
        do, don, dz = vjp(dy_ref[...])
        do_s[0:nc, :] = jnp.zeros((nc, HD), F32)
        do_s[nc:, :] = do
        dp_ref[0:nc, :] = jnp.zeros((nc, HD), BF16)
        dp_ref[nc:, :] = dz.astype(BF16)

        @pl.when(h == 0)
        def _():
            don_ref[...] = jnp.zeros_like(don_ref)
            dgb_ref[...] = jnp.zeros_like(dgb_ref)

        don_ref[0, 0:1, :] += don

        for d in (0, 1):
            dS = jnp.zeros((HD, HD), F32)
            for first, count, desc in reversed(_delta_orders(T, nc, d)):
                def chunk(i, dS, d=d, first=first, count=count, desc=desc):
                    c, rows = rows_of(first, count, not desc, i)
                    gcol, bcol = cols(rows, d)
                    dq, dk, dv, dg, dbeta, dS1 = _delta_chunk_bwd(
                        q_ref[rows, :], k_ref[rows, :], v_ref[rows, :], gcol, bcol, S_s[d * n + c], Tm_s[d * n + c],
                        vn_s[d, rows, :], do_s[rows, :], dS, d == 1)
                    if d == 0:
                        dqkv_ref[0, rows, :] = dq
                        dqkv_ref[1, rows, :] = dk
                        dqkv_ref[2, rows, :] = dv
                    else:
                        dqkv_ref[0, rows, :] += dq
                        dqkv_ref[1, rows, :] += dk
                        dqkv_ref[2, rows, :] += dv
                    dgb_ref[rows, :] += (jnp.where(lane == d * H + h, dg, 0.0)
                                         + jnp.where(lane == 2 * H + d * H + h, dbeta, 0.0))
                    return dS1

                dS = lax.fori_loop(0, count, chunk, dS)

    col = lambda off: pl.BlockSpec((T, HD), lambda b, h: (b, off + h))
    return pl.pallas_call(
        body, name="delta_bwd", grid=(B, H),
        in_specs=[col(0), col(H), col(2 * H), pl.BlockSpec((T, LANES), lambda b, h: (b, 0)), col(3 * H),
                  pl.BlockSpec((1, HD), lambda b, h: (0, 0)), pl.BlockSpec((N, HD), lambda b, h: (b, h)),
                  pl.BlockSpec(memory_space=pl.ANY)],
        out_specs=[pl.BlockSpec((3, T, HD), lambda b, h: (0, b, h)), pl.BlockSpec((T, LANES), lambda b, h: (b, 0)),
                   col(3 * H), pl.BlockSpec((1, 8, HD), lambda b, h: (b, 0, 0))],
        out_shape=[jax.ShapeDtypeStruct((3, B * T, H * HD), F32), jax.ShapeDtypeStruct((B * T, LANES), F32),
                   jax.ShapeDtypeStruct(dp.shape, dp.dtype), jax.ShapeDtypeStruct((B, 8, HD), F32)],
        scratch_shapes=[pltpu.VMEM((T, HD), F32), pltpu.VMEM((T, HD), F32), pltpu.VMEM((2, T, HD), F32),
                        pltpu.VMEM((2 * n, HD, HD), F32), pltpu.VMEM((2 * n, CHUNK, CHUNK), F32)],
        input_output_aliases={7: 2},
        compiler_params=_params(("parallel", "arbitrary")),
    )(qkv, qkv, qkv, gb, p, onorm, dy, dp)


def _rowwise(fn, ins, out_dtypes, *, name, tm=256, mult=16):
    R, W = ins[0].shape
    tm = _tile(R, tm, mult)

    def body(*refs):
        outs = fn(*[r[...] for r in refs[:len(ins)]])
        for o_ref, o in zip(refs[len(ins):], outs):
            o_ref[...] = o.astype(o_ref.dtype)

    spec = pl.BlockSpec((tm, W), lambda i: (i, 0))
    return pl.pallas_call(
        body, name=name, grid=(R // tm,), in_specs=[spec] * len(ins), out_specs=[spec] * len(out_dtypes),
        out_shape=[jax.ShapeDtypeStruct((R, W), dt) for dt in out_dtypes],
        compiler_params=_params(("parallel",)),
    )(*ins)


def _sum_lead(x, *, name, tm=256, mult=16):
    S, R, W = x.shape
    tm = _tile(R, tm, mult)

    def body(*refs):
        acc = refs[0][0].astype(F32)
        for r in refs[1:S]:
            acc = acc + r[0].astype(F32)
        refs[S][...] = acc

    return pl.pallas_call(
        body, name=name, grid=(R // tm,),
        in_specs=[pl.BlockSpec((1, tm, W), functools.partial(lambda s, i: (s, i, 0), s)) for s in range(S)],
        out_specs=pl.BlockSpec((tm, W), lambda i: (i, 0)),
        out_shape=jax.ShapeDtypeStruct((R, W), F32),
        compiler_params=_params(("parallel",)),
    )(*([x] * S))


def _adamw_math(w, g, m, v):
    m = ADAM_B1 * m + (1.0 - ADAM_B1) * g
    v = ADAM_B2 * v + (1.0 - ADAM_B2) * (g * g)
    m_hat = m / (1.0 - ADAM_B1 ** ADAM_STEP)
    v_hat = v / (1.0 - ADAM_B2 ** ADAM_STEP)
    return -ADAM_LR * (m_hat / (jnp.sqrt(v_hat) + ADAM_EPS) + ADAM_WD * w), m, v


def _adamw(w, g, m, v, *, name):
    return _rowwise(_adamw_math, [w, g, m, v], [F32, F32, F32], name=name, tm=128, mult=SUBLANES)


def _me():
    return lax.axis_index("x"), lax.axis_index("y"), lax.axis_index("c")


def _allgather_small(v):
    R, W = v.shape

    def body(x_ref, out_ref, send_sems, recv_sems, local_sem):
        x, y, c = _me()
        me, sibling = (x, y, c), (x, y, 1 - c)
        chips = [(1 - x, y), (x, 1 - y), (1 - x, 1 - y)]

        def slot(px, py, pc):
            return out_ref.at[4 * px + 2 * py + pc]

        def copy(k, block, to, src=None):
            return pltpu.make_async_remote_copy(
                src_ref=slot(*block) if src is None else src, dst_ref=slot(*block),
                send_sem=send_sems.at[k], recv_sem=recv_sems.at[k], device_id=to, device_id_type=MESH)

        mine = pltpu.make_async_copy(x_ref, slot(*me), local_sem)
        mine.start()
        first = [copy(0, me, sibling, src=x_ref)]
        first += [copy(1 + j, me, (*chip, c), src=x_ref) for j, chip in enumerate(chips)]
        for cp in first:
            cp.start()
        passed = [copy(4 + j, (*chip, c), sibling) for j, chip in enumerate(chips)]
        for j, chip in enumerate(chips):
            copy(1 + j, (*chip, c), me).wait_recv()
            passed[j].start()
        copy(0, sibling, me).wait_recv()
        for j, chip in enumerate(chips):
            copy(4 + j, (*chip, 1 - c), me).wait_recv()
        for cp in first + passed:
            cp.wait_send()
        mine.wait()

    return pl.pallas_call(
        body, name="allgather_small", out_shape=jax.ShapeDtypeStruct((8, R, W), v.dtype),
        in_specs=[pl.BlockSpec(memory_space=pltpu.VMEM)], out_specs=pl.BlockSpec(memory_space=pltpu.VMEM),
        scratch_shapes=[pltpu.SemaphoreType.DMA((7,)), pltpu.SemaphoreType.DMA((7,)), pltpu.SemaphoreType.DMA],
        compiler_params=_params(),
    )(v)


def _allgather_big(pool):
    PR, W = pool.shape[0] // 2, pool.shape[1]

    def body(x_ref, out_ref, send_sems, recv_sems, local_sem):
        x, y, c = _me()
        me, sibling = (x, y, c), (x, y, 1 - c)
        chips = [(1 - x, y), (x, 1 - y), (1 - x, 1 - y)]
        own = x_ref.at[pl.ds(c * PR, PR), :]

        def slot(px, py, pc):
            return out_ref.at[4 * px + 2 * py + pc]

        def copy(k, block, to, src=None):
            return pltpu.make_async_remote_copy(
                src_ref=slot(*block) if src is None else src, dst_ref=slot(*block),
                send_sem=send_sems.at[k], recv_sem=recv_sems.at[k], device_id=to, device_id_type=MESH)

        mine = pltpu.make_async_copy(own, slot(*me), local_sem)
        mine.start()
        first = [copy(0, me, sibling, src=own)]
        first += [copy(1 + j, me, (*chip, c), src=own) for j, chip in enumerate(chips)]
        for cp in first:
            cp.start()
        passed = [copy(4 + j, (*chip, c), sibling) for j, chip in enumerate(chips)]
        for j, chip in enumerate(chips):
            copy(1 + j, (*chip, c), me).wait_recv()
            passed[j].start()
        copy(0, sibling, me).wait_recv()
        for j, chip in enumerate(chips):
            copy(4 + j, (*chip, 1 - c), me).wait_recv()
        for cp in first + passed:
            cp.wait_send()
        mine.wait()

    return pl.pallas_call(
        body, name="allgather_big", out_shape=jax.ShapeDtypeStruct((8, PR, W), pool.dtype),
        in_specs=[pl.BlockSpec(memory_space=pl.ANY)], out_specs=pl.BlockSpec(memory_space=pl.ANY),
        scratch_shapes=[pltpu.SemaphoreType.DMA((7,)), pltpu.SemaphoreType.DMA((7,)), pltpu.SemaphoreType.DMA],
        compiler_params=_params(),
    )(pool)


def _sibling_swap(v, *, name):
    def body(x_ref, out_ref, send_sem, recv_sem):
        x, y, c = _me()
        cp = pltpu.make_async_remote_copy(src_ref=x_ref, dst_ref=out_ref, send_sem=send_sem, recv_sem=recv_sem,
                                          device_id=(x, y, 1 - c), device_id_type=MESH)
        cp.start()
        cp.wait()

    return pl.pallas_call(
        body, name=name, out_shape=jax.ShapeDtypeStruct(v.shape, v.dtype),
        in_specs=[pl.BlockSpec(memory_space=pl.ANY)], out_specs=pl.BlockSpec(memory_space=pl.ANY),
        scratch_shapes=[pltpu.SemaphoreType.DMA, pltpu.SemaphoreType.DMA],
        compiler_params=_params(),
    )(v)


def _chip_exchange(v):
    def body(x_ref, out_ref, send_sems, recv_sems, local_sem):
        x, y, c = _me()
        s_me = 2 * x + y
        chips = [(1 - x, y), (x, 1 - y), (1 - x, 1 - y)]
        mine = pltpu.make_async_copy(x_ref.at[s_me], out_ref.at[s_me], local_sem)
        mine.start()
        sends = []
        for k, (px, py) in enumerate(chips):
            cp = pltpu.make_async_remote_copy(
                src_ref=x_ref.at[2 * px + py], dst_ref=out_ref.at[s_me], send_sem=send_sems.at[k],
                recv_sem=recv_sems.at[k], device_id=(px, py, c), device_id_type=MESH)
            cp.start()
            sends.append(cp)
        for k, (px, py) in enumerate(chips):
            pltpu.make_async_remote_copy(
                src_ref=x_ref.at[s_me], dst_ref=out_ref.at[2 * px + py], send_sem=send_sems.at[k],
                recv_sem=recv_sems.at[k], device_id=(px, py, c), device_id_type=MESH).wait_recv()
        for cp in sends:
            cp.wait_send()
        mine.wait()

    return pl.pallas_call(
        body, name="chip_exchange", out_shape=jax.ShapeDtypeStruct(v.shape, v.dtype),
        in_specs=[pl.BlockSpec(memory_space=pl.ANY)], out_specs=pl.BlockSpec(memory_space=pl.ANY),
        scratch_shapes=[pltpu.SemaphoreType.DMA((3,)), pltpu.SemaphoreType.DMA((3,)), pltpu.SemaphoreType.DMA],
        compiler_params=_params(),
    )(v)


def _layout(sizes, width, part_mult, total_mult):
    offs, rows, r = [], [], 0
    for n in sizes:
        k = -(-n // width)
        offs.append(r)
        rows.append(k)
        r += -(-k // part_mult) * part_mult
    return offs, rows, -(-r // total_mult) * total_mult


def _pack(arrs, width, part_mult, total_mult, dtype, lead=()):
    nl = len(lead)
    sizes = [math.prod(a.shape[nl:]) for a in arrs]
    offs, rows, total = _layout(sizes, width, part_mult, total_mult)
    parts, r = [], 0
    for a, n, o, k in zip(arrs, sizes, offs, rows):
        if o > r:
            parts.append(jnp.zeros((*lead, o - r, width), dtype))
        flat = a.reshape(*lead, n).astype(dtype)
        if k * width > n:
            flat = jnp.concatenate([flat, jnp.zeros((*lead, k * width - n), dtype)], axis=-1)
        parts.append(flat.reshape(*lead, k, width))
        r = o + k
    if total > r:
        parts.append(jnp.zeros((*lead, total - r, width), dtype))
    return jnp.concatenate(parts, axis=nl)


def _unpack(pool, shapes, width, part_mult, total_mult):
    lead = pool.shape[:-2]
    sizes = [math.prod(s) for s in shapes]
    offs, rows, _ = _layout(sizes, width, part_mult, total_mult)
    out = []
    for s, n, o, k in zip(shapes, sizes, offs, rows):
        flat = lax.slice_in_dim(pool, o, o + k, axis=len(lead)).reshape(*lead, k * width)
        out.append(lax.slice_in_dim(flat, 0, n, axis=len(lead)).reshape(*lead, *s))
    return out


_WEIGHTS = ("c_ctx", "w_ada", "b_ada", "g_pre_mix", "g_post_mix", "g_pre_ffn", "g_post_ffn", "w_in", "b_merge",
            "dn_conv", "dn_a_log", "dn_dt_bias", "dn_onorm", "lru_conv", "lru_conv_b", "lru_w_rg", "lru_b_rg",
            "lru_w_ig", "lru_b_ig", "lru_lambda", "w_branch_dn", "w_branch_lru", "w_out", "w_up", "ffn_dw",
            "ffn_dw_b", "w_down")
_BIG = {"w_ada": True, "w_in": True, "w_branch_dn": False, "w_branch_lru": False, "w_out": False, "w_up": True,
        "w_down": False}
_SMALL_SHARDED = ("dn_conv", "lru_conv", "lru_b_rg", "lru_b_ig", "lru_lambda", "ffn_dw")
_NCHIP = 4
_POOL_PART, _POOL_TOTAL = 16, 32
_FLAT_PART = 8


def _to_chip_shards(g, by_cols):
    if by_cols:
        return g.reshape(g.shape[0], _NCHIP, g.shape[1] // _NCHIP).transpose(1, 0, 2)
    return g.reshape(_NCHIP, g.shape[0] // _NCHIP, g.shape[1])


def _from_chip_shards(s, by_cols):
    if by_cols:
        return s.transpose(1, 0, 2).reshape(s.shape[1], _NCHIP * s.shape[2])
    return s.reshape(_NCHIP * s.shape[1], s.shape[2])


def _dsilu(x):
    s = _sigmoid(x)
    return s * (1.0 + x * (1.0 - s))


def kernel(x, c, ctx, c_ctx, w_ada, b_ada, g_pre_mix, g_post_mix, g_pre_ffn, g_post_ffn, w_in, b_merge, dn_conv, dn_a_log, dn_dt_bias, dn_onorm, lru_conv, lru_conv_b, lru_w_rg, lru_b_rg, lru_w_ig, lru_b_ig, lru_lambda, w_branch_dn, w_branch_lru, w_out, w_up, ffn_dw, ffn_dw_b, w_down, loss_target, m_c_ctx, m_w_ada, m_b_ada, m_g_pre_mix, m_g_post_mix, m_g_pre_ffn, m_g_post_ffn, m_w_in, m_b_merge, m_dn_conv, m_dn_a_log, m_dn_dt_bias, m_dn_onorm, m_lru_conv, m_lru_conv_b, m_lru_w_rg, m_lru_b_rg, m_lru_w_ig, m_lru_b_ig, m_lru_lambda, m_w_branch_dn, m_w_branch_lru, m_w_out, m_w_up, m_ffn_dw, m_ffn_dw_b, m_w_down, v_c_ctx, v_w_ada, v_b_ada, v_g_pre_mix, v_g_post_mix, v_g_pre_ffn, v_g_post_ffn, v_w_in, v_b_merge, v_dn_conv, v_dn_a_log, v_dn_dt_bias, v_dn_onorm, v_lru_conv, v_lru_conv_b, v_lru_w_rg, v_lru_b_rg, v_lru_w_ig, v_lru_b_ig, v_lru_lambda, v_w_branch_dn, v_w_branch_lru, v_w_out, v_w_up, v_ffn_dw, v_ffn_dw_b, v_w_down):
    W = dict(zip(_WEIGHTS, (c_ctx, w_ada, b_ada, g_pre_mix, g_post_mix, g_pre_ffn, g_post_ffn, w_in, b_merge, dn_conv,
                            dn_a_log, dn_dt_bias, dn_onorm, lru_conv, lru_conv_b, lru_w_rg, lru_b_rg, lru_w_ig, lru_b_ig,
                            lru_lambda, w_branch_dn, w_branch_lru, w_out, w_up, ffn_dw, ffn_dw_b, w_down)))
    Mo = dict(zip(_WEIGHTS, (m_c_ctx, m_w_ada, m_b_ada, m_g_pre_mix, m_g_post_mix, m_g_pre_ffn, m_g_post_ffn, m_w_in,
                             m_b_merge, m_dn_conv, m_dn_a_log, m_dn_dt_bias, m_dn_onorm, m_lru_conv, m_lru_conv_b,
                             m_lru_w_rg, m_lru_b_rg, m_lru_w_ig, m_lru_b_ig, m_lru_lambda, m_w_branch_dn,
                             m_w_branch_lru, m_w_out, m_w_up, m_ffn_dw, m_ffn_dw_b, m_w_down)))
    Vo = dict(zip(_WEIGHTS, (v_c_ctx, v_w_ada, v_b_ada, v_g_pre_mix, v_g_post_mix, v_g_pre_ffn, v_g_post_ffn, v_w_in,
                             v_b_merge, v_dn_conv, v_dn_a_log, v_dn_dt_bias, v_dn_onorm, v_lru_conv, v_lru_conv_b,
                             v_lru_w_rg, v_lru_b_rg, v_lru_w_ig, v_lru_b_ig, v_lru_lambda, v_w_branch_dn,
                             v_w_branch_lru, v_w_out, v_w_up, v_ffn_dw, v_ffn_dw_b, v_w_down)))
    B, N, D = x.shape
    NC = ctx.shape[1]
    T = NC + N
    H, HD = dn_a_log.shape[-1], dn_onorm.shape[-1]
    DNW = H * HD
    LW, LBD = lru_conv_b.shape[-1], lru_w_rg.shape[-1]
    DFF = ffn_dw_b.shape[-1]
    LC = LANES
    x_i, y_i, c_i = _me()
    s_me = 2 * x_i + y_i
    tm = _tile(math.gcd(NC, N), 256, 16)

    big_local = [W[n][0] for n in _BIG]
    big_shapes = [a.shape for a in big_local]
    pool = _pack(big_local, D, _POOL_PART, _POOL_TOTAL, BF16)
    gathered = _allgather_big(pool)
    gathered = gathered.reshape(_NCHIP, pool.shape[0], D)
    full = {n: _from_chip_shards(s, _BIG[n])
            for n, s in zip(_BIG, _unpack(gathered, big_shapes, D, _POOL_PART, _POOL_TOTAL))}

    small_local = [W[n][0].reshape(-1, W[n].shape[-1]) for n in _SMALL_SHARDED]
    small_shapes = [a.shape for a in small_local]
    spack = _pack(small_local, LANES, _FLAT_PART, _FLAT_PART, F32)
    sgath = _allgather_small(spack)[0::2]
    sfull = {n: _from_chip_shards(s, True)
             for n, s in zip(_SMALL_SHARDED, _unpack(sgath, small_shapes, LANES, _FLAT_PART, _FLAT_PART))}

    o_a = 4 * DNW
    o_xl = o_a + 4 * H
    o_mg = o_xl + 2 * LW
    wi_ = full["w_in"]
    nj = LW // LC
    lru_cols = jnp.stack([wi_[:, o_xl:o_xl + LW].reshape(D, nj, LC), wi_[:, o_xl + LW:o_mg].reshape(D, nj, LC)],
                         axis=2).reshape(D, 2 * LW)
    wp = jnp.concatenate([wi_[:, :o_a], lru_cols, wi_[:, o_mg:], wi_[:, o_a:o_xl],
                          jnp.zeros((D, LANES - 4 * H), BF16)], axis=1)
    p_lru, p_mg, p_ab = 4 * DNW, 4 * DNW + 2 * LW, 4 * DNW + 2 * LW + 2 * D
    PW = p_ab + LANES

    MR = LANES
    cond = jnp.concatenate([c, c_ctx[None], jnp.zeros((MR - B - 1, D), F32)], axis=0)
    silu_rows = _rowwise(lambda a: (_silu(a),), [cond], [F32], name="cond_silu")[0]
    mod = _matmul(silu_rows, full["w_ada"], name="ada_fwd") + b_ada
    mx = mod[:B].reshape(B, 6, D)
    mc = mod[B].reshape(6, D)
    zero = jnp.zeros((B, D), F32)
    tab = jnp.stack([jnp.stack([jnp.broadcast_to(mc[0], (B, D)), jnp.broadcast_to(mc[1], (B, D))] + [zero] * 6, axis=1),
                     jnp.stack([mx[:, 0], mx[:, 1]] + [zero] * 6, axis=1)], axis=1)
    vecs = jnp.stack([mx[:, 2], mx[:, 3], mx[:, 4], mx[:, 5]] + [zero] * 4, axis=1)
    gains = jnp.concatenate([g_post_mix, g_pre_ffn, g_post_ffn, jnp.zeros((5, D), F32)], axis=0)

    h = jnp.concatenate([ctx, x], axis=1)
    u = _premix_fwd(h, g_pre_mix, tab, nc=NC, tm=tm)
    p = _matmul(u, wp, name="in_fwd")
    dkw = dict(B=B, T=T, nc=NC, H=H, HD=HD)
    qkv = _dnprep_fwd(p, sfull["dn_conv"], **dkw)
    prm = jnp.concatenate([
        jnp.concatenate([dn_a_log.reshape(1, 2 * H), jnp.zeros((1, LANES - 2 * H), F32)], axis=1),
        jnp.concatenate([dn_dt_bias.reshape(1, 2 * H), jnp.zeros((1, LANES - 2 * H), F32)], axis=1),
        jnp.zeros((6, LANES), F32)], axis=0)
    gtm = _tile(B * T, 512, 16)
    gb = _gb_fwd(p, prm, rows=B * T, col0=p_ab, H=H, tm=gtm)
    y_dn = _delta_fwd(qkv, gb, p, dn_onorm, **dkw)
    lv = jnp.concatenate([lru_conv_b, sfull["lru_b_rg"], sfull["lru_b_ig"], sfull["lru_lambda"], jnp.zeros((1, LW), F32)], axis=0)
    wr = _blockdiag(lru_w_rg[0], LC).astype(BF16)
    wi = _blockdiag(lru_w_ig[0], LC).astype(BF16)
    lkw = dict(B=B, T=T, nc=NC, LW=LW, col0=p_lru, C=LC)
    y_lru = _lru_fwd(p, sfull["lru_conv"], lv, wr, wi, **lkw)
    Ydn = _matmul(y_dn, full["w_branch_dn"], name="bdn_fwd")
    Ylru = _matmul(y_lru, full["w_branch_lru"], name="blru_fwd")
    mkw = dict(B=B, T=T, nc=NC, D=D, col0=p_mg, tm=tm)
    mixin = _merge_fwd(p, Ydn, Ylru, b_merge, **mkw)
    mix = _matmul(mixin, full["w_out"], name="out_fwd")
    h1, u2 = _post_fwd(x, mix, gains, vecs, tm=tm)
    F = _matmul(u2, full["w_up"], name="up_fwd")
    w9 = sfull["ffn_dw"]
    ftc = _tile(DFF, 256)
    f = _ffn_act_fwd(F, w9, ffn_dw_b, B=B, N=N, DFF=DFF, tc=ftc)
    dn = _matmul(f, full["w_down"], name="down_fwd")
    ddn, dout, sums_f = _final(h1, dn, loss_target, gains, vecs, tm=tm)

    G = {}
    df = _matmul(ddn, full["w_down"], tb=True, name="down_bwd_x")
    G["w_down"] = _matmul(f, ddn, ta=True, name="down_bwd_w")
    dFg, dFv, dwb = _ffn_act_bwd(F, w9, ffn_dw_b, df, B=B, N=N, DFF=DFF, tc=ftc)
    du2 = _matmul(dFg, full["w_up"][:, :DFF], tb=True, name="up_bwd_xg")
    du2 = _matmul(dFv, full["w_up"][:, DFF:], tb=True, add=du2, name="up_bwd_xv")
    G["w_up"] = jnp.concatenate([_matmul(u2, dFg, ta=True, name="up_bwd_wg"), _matmul(u2, dFv, ta=True, name="up_bwd_wv")], axis=1)
    dx1, dmix, sums_p = _post_bwd(x, mix, gains, vecs, dout, du2, tm=tm)
    dmixin = _matmul(dmix, full["w_out"], tb=True, name="out_bwd_x")
    G["w_out"] = _matmul(mixin, dmix, ta=True, name="out_bwd_w")
    dp = jnp.zeros((B * T, PW), BF16)
    dYdn, dYlru, dp, sums_m = _merge_bwd(p, Ydn, Ylru, b_merge, dmixin, dp, **mkw)
    dy_dn = _matmul(dYdn, full["w_branch_dn"], tb=True, name="bdn_bwd_x")
    G["w_branch_dn"] = _matmul(y_dn, dYdn, ta=True, name="bdn_bwd_w")
    dy_lru = _matmul(dYlru, full["w_branch_lru"], tb=True, name="blru_bwd_x")
    G["w_branch_lru"] = _matmul(y_lru, dYlru, ta=True, name="blru_bwd_w")
    dp, dcw_l, dlv, dwr, dwi = _lru_bwd(p, sfull["lru_conv"], lv, wr, wi, dy_lru, dp, **lkw)
    dqkv, dgb, dp, don = _delta_bwd(qkv, gb, p, dn_onorm, dy_dn, dp, **dkw)
    dp, dprm = _gb_bwd(p, prm, dgb, dp, rows=B * T, col0=p_ab, H=H, tm=gtm)
    dp, dcw_d = _dnprep_bwd(p, sfull["dn_conv"], dqkv, dp, **dkw)
    dU = _matmul(dp, wp, tb=True, name="in_bwd_x")
    dwp = _matmul(u, dp, ta=True, name="in_bwd_w")
    grad_x, sums_pm = _premix_bwd(h, g_pre_mix, tab, dU, dx1, nc=NC, tm=tm)
    dlru = dwp[:, p_lru:p_mg].reshape(D, nj, 2, LC)
    G["w_in"] = jnp.concatenate([dwp[:, :o_a], dwp[:, p_ab:p_ab + 4 * H], dlru[:, :, 0].reshape(D, LW),
                                 dlru[:, :, 1].reshape(D, LW), dwp[:, p_mg:p_ab]], axis=1)

    dmod_x = jnp.stack([sums_pm[:, 1, 0], sums_pm[:, 1, 1], sums_p[:, 0], sums_p[:, 1], sums_p[:, 2], sums_f[:, 0]],
                       axis=1).reshape(B, 6 * D)
    dmod_c = jnp.concatenate([sums_pm[:, 0, 0].sum(0), sums_pm[:, 0, 1].sum(0), jnp.zeros((4 * D,), F32)])[None]
    dmod = jnp.concatenate([dmod_x, dmod_c, jnp.zeros((MR - B - 1, 6 * D), F32)], axis=0)
    G["w_ada"] = _matmul(silu_rows, dmod, ta=True, name="ada_bwd_w")
    dsilu = _matmul(dmod, full["w_ada"], tb=True, name="ada_bwd_x")

    g_small = {
        "c_ctx": dsilu[B] * _dsilu(c_ctx),
        "b_ada": dmod[:B + 1].sum(0)[None],
        "g_pre_mix": sums_pm[:, :, 2].sum((0, 1))[None],
        "g_post_mix": sums_p[:, 3].sum(0)[None],
        "g_pre_ffn": sums_p[:, 4].sum(0)[None],
        "g_post_ffn": sums_f[:, 1].sum(0)[None],
        "b_merge": sums_m[0:1],
        "dn_conv": dcw_d[0:4][None],
        "dn_a_log": dprm[0, :2 * H].reshape(1, 2, H),
        "dn_dt_bias": dprm[1, :2 * H].reshape(1, 2, H),
        "dn_onorm": don[:, 0].sum(0)[None],
        "lru_conv": dcw_l[0:4][None],
        "lru_conv_b": dlv[0:1],
        "lru_w_rg": _blockdiag_extract(dwr, LBD)[None],
        "lru_b_rg": dlv[1:3][None],
        "lru_w_ig": _blockdiag_extract(dwi, LBD)[None],
        "lru_b_ig": dlv[3:5][None],
        "lru_lambda": dlv[5:7][None],
        "ffn_dw": dwb[0:9].reshape(1, 3, 3, DFF),
        "ffn_dw_b": dwb[9:10],
    }
    small_names = tuple(n for n in _WEIGHTS if n not in _BIG)
    loss_part = sums_f[:, 2].sum().reshape(1)
    gs_list = [g_small[n] for n in small_names] + [loss_part]
    gs_shapes = [a.shape for a in gs_list]
    gpack = _pack(gs_list, LANES, _FLAT_PART, _FLAT_PART, F32)
    gsum = _sum_lead(_allgather_small(gpack), name="small_sum", tm=512, mult=SUBLANES)
    gs_red = dict(zip(small_names + ("loss",), _unpack(gsum, gs_shapes, LANES, _FLAT_PART, _FLAT_PART)))
    loss = gs_red["loss"][0]

    gp = _pack([_to_chip_shards(G[n], _BIG[n]) for n in _BIG], D, _POOL_PART, _POOL_TOTAL, F32, lead=(_NCHIP,))
    R = gp.shape[1]
    PR = R // 2
    gp = gp.reshape(_NCHIP, 2, PR, D)
    own = lax.dynamic_index_in_dim(gp, c_i, axis=1, keepdims=False)
    other = lax.dynamic_index_in_dim(gp, 1 - c_i, axis=1, keepdims=False).astype(BF16)
    from_sibling = _sibling_swap(other, name="rs_sibling")
    chip_sum = _rowwise(lambda a, b: (a + b.astype(F32),), [own.reshape(_NCHIP * PR, D), from_sibling.reshape(_NCHIP * PR, D)],
                        [BF16], name="rs_add")[0].reshape(_NCHIP, PR, D)
    half = _sum_lead(_chip_exchange(chip_sum), name="rs_sum")
    sib_half = _sibling_swap(half, name="rs_gather")
    red_pool = jnp.where(c_i == 0, jnp.concatenate([half, sib_half], axis=0), jnp.concatenate([sib_half, half], axis=0))
    g_big = dict(zip(_BIG, _unpack(red_pool, big_shapes, D, _POOL_PART, _POOL_TOTAL)))

    grads, deltas, new_m, new_v = {}, {}, {}, {}
    for n in _BIG:
        shp = W[n].shape
        grads[n] = g_big[n].reshape(shp)
        d_, m_, v_ = _adamw(W[n][0], g_big[n], Mo[n][0], Vo[n][0], name="adamw_" + n)
        deltas[n], new_m[n], new_v[n] = d_.reshape(shp), m_.reshape(shp), v_.reshape(shp)
    for n in small_names:
        g = gs_red[n]
        if n in _SMALL_SHARDED:
            k = W[n].shape[-1]
            g = lax.dynamic_slice_in_dim(g, s_me * k, k, axis=g.ndim - 1)
        grads[n] = g.reshape(W[n].shape)
    sm_shapes = [W[n].shape for n in small_names]
    pk = lambda d: _pack([d[n] for n in small_names], LANES, _FLAT_PART, _FLAT_PART, F32)
    d_, m_, v_ = _adamw(pk(W), pk(grads), pk(Mo), pk(Vo), name="adamw_small")
    for dst, pool_ in ((deltas, d_), (new_m, m_), (new_v, v_)):
        dst.update(zip(small_names, _unpack(pool_, sm_shapes, LANES, _FLAT_PART, _FLAT_PART)))
    return (loss, grad_x, *[grads[n] for n in _WEIGHTS], *[deltas[n] for n in _WEIGHTS],
            *[new_m[n] for n in _WEIGHTS], *[new_v[n] for n in _WEIGHTS])
```

```python
import functools
import math

import jax
import jax.numpy as jnp
from jax import lax
from jax.experimental import pallas as pl
from jax.experimental.pallas import tpu as pltpu

F32 = jnp.float32
BF16 = jnp.bfloat16
EPS = 1e-6
GRID_W = 64
CHUNK = 64
LRU_C = 8.0
LANES = 128
SUBLANES = 8
VMEM_LIMIT = 56 * 1024 * 1024
ADAM_LR, ADAM_B1, ADAM_B2, ADAM_EPS, ADAM_WD, ADAM_STEP = 0.001, 0.9, 0.999, 1e-08, 0.01, 10
MESH = pl.DeviceIdType.MESH


def _tile(n, target, mult=LANES):
    best = None
    for t in range(mult, min(n, target) + 1, mult):
        if n % t == 0:
            best = t
    return best if best is not None else n


def _params(sem=None, **kw):
    return pltpu.CompilerParams(dimension_semantics=sem, vmem_limit_bytes=VMEM_LIMIT, **kw)


def _sigmoid(x):
    return 1.0 / (1.0 + jnp.exp(-x))


def _silu(x):
    return x * _sigmoid(x)


def _softplus(x):
    return jnp.maximum(x, 0.0) + jnp.log(1.0 + jnp.exp(-jnp.abs(x)))


def _gelu(x):
    return 0.5 * x * (1.0 + jnp.tanh(math.sqrt(2.0 / math.pi) * (x + 0.044715 * x * x * x)))


def _rmsn(u, gain):
    return u * lax.rsqrt(jnp.mean(u * u, axis=-1, keepdims=True) + EPS) * gain


def _matmul(a, b, *, ta=False, tb=False, add=None, out_dtype=F32, name, tm=1024, tn=1024, tk=512):
    (K, M) = a.shape if ta else a.shape[::-1]
    N = b.shape[0] if tb else b.shape[1]
    assert (b.shape[1] if tb else b.shape[0]) == K, (a.shape, b.shape, ta, tb)
    tm, tn, tk = _tile(M, tm), _tile(N, tn), _tile(K, tk)
    nk = K // tk
    dims = (((0 if ta else 1,), (1 if tb else 0,)), ((), ()))

    def body(a_ref, b_ref, *rest):
        (c_ref, o_ref, acc_ref) = rest if add is not None else (None, *rest)
        k = pl.program_id(2)

        @pl.when(k == 0)
        def _():
            acc_ref[...] = jnp.zeros_like(acc_ref) if c_ref is None else c_ref[...]

        acc_ref[...] += lax.dot_general(a_ref[...].astype(BF16), b_ref[...].astype(BF16), dims,
                                        preferred_element_type=F32)

        @pl.when(k == nk - 1)
        def _():
            o_ref[...] = acc_ref[...].astype(out_dtype)

    a_spec = pl.BlockSpec((tk, tm), lambda i, j, k: (k, i)) if ta else pl.BlockSpec((tm, tk), lambda i, j, k: (i, k))
    b_spec = pl.BlockSpec((tn, tk), lambda i, j, k: (j, k)) if tb else pl.BlockSpec((tk, tn), lambda i, j, k: (k, j))
    o_spec = pl.BlockSpec((tm, tn), lambda i, j, k: (i, j))
    return pl.pallas_call(
        body, name=name, grid=(M // tm, N // tn, nk),
        in_specs=[a_spec, b_spec] + ([o_spec] if add is not None else []),
        out_specs=o_spec,
        out_shape=jax.ShapeDtypeStruct((M, N), out_dtype),
        scratch_shapes=[pltpu.VMEM((tm, tn), F32)],
        compiler_params=_params(("parallel", "parallel", "arbitrary")),
    )(*((a, b) + ((add,) if add is not None else ())))


def _premix_math(h, gain, shift, scale):
    return _rmsn(h, gain) * (1.0 + scale) + shift


def _premix_fwd(h, gain, tab, *, nc, tm):
    B, T, D = h.shape
    nt, nct = T // tm, nc // tm

    def body(h_ref, g_ref, tab_ref, u_ref):
        tabv = tab_ref[0, 0]
        u_ref[...] = _premix_math(h_ref[0], g_ref[...], tabv[0:1], tabv[1:2]).astype(BF16)

    return pl.pallas_call(
        body, name="premix_fwd", grid=(B, nt),
        in_specs=[pl.BlockSpec((1, tm, D), lambda b, t: (b, t, 0)),
                  pl.BlockSpec((1, D), lambda b, t: (0, 0)),
                  pl.BlockSpec((1, 1, 8, D), lambda b, t: (b, jnp.where(t < nct, 0, 1), 0, 0))],
        out_specs=pl.BlockSpec((tm, D), lambda b, t: (b * nt + t, 0)),
        out_shape=jax.ShapeDtypeStruct((B * T, D), BF16),
        compiler_params=_params(("parallel", "parallel")),
    )(h, gain, tab)


def _premix_bwd(h, gain, tab, du, dres, *, nc, tm):
    B, T, D = h.shape
    nt, nct = T // tm, nc // tm
    N = T - nc

    def body(h_ref, g_ref, tab_ref, du_ref, dres_ref, dx_ref, sums_ref):
        t = pl.program_id(1)
        tabv = tab_ref[0, 0]
        _, vjp = jax.vjp(_premix_math, h_ref[0], g_ref[...], tabv[0:1], tabv[1:2])
        dh, dgain, dshift, dscale = vjp(du_ref[...].astype(F32))

        @pl.when((t == 0) | (t == nct))
        def _():
            sums_ref[...] = jnp.zeros_like(sums_ref)

        sums_ref[0, 0, 0:1, :] += dshift
        sums_ref[0, 0, 1:2, :] += dscale
        sums_ref[0, 0, 2:3, :] += dgain

        @pl.when(t >= nct)
        def _():
            dx_ref[0] = dres_ref[...] + dh

    lat = lambda b, t: jnp.maximum(t - nct, 0)
    return pl.pallas_call(
        body, name="premix_bwd", grid=(B, nt),
        in_specs=[pl.BlockSpec((1, tm, D), lambda b, t: (b, t, 0)),
                  pl.BlockSpec((1, D), lambda b, t: (0, 0)),
                  pl.BlockSpec((1, 1, 8, D), lambda b, t: (b, jnp.where(t < nct, 0, 1), 0, 0)),
                  pl.BlockSpec((tm, D), lambda b, t: (b * nt + t, 0)),
                  pl.BlockSpec((tm, D), lambda b, t: (b * (nt - nct) + lat(b, t), 0))],
        out_specs=[pl.BlockSpec((1, tm, D), lambda b, t: (b, lat(b, t), 0)),
                   pl.BlockSpec((1, 1, 8, D), lambda b, t: (b, jnp.where(t < nct, 0, 1), 0, 0))],
        out_shape=[jax.ShapeDtypeStruct((B, N, D), F32), jax.ShapeDtypeStruct((B, 2, 8, D), F32)],
        compiler_params=_params(("parallel", "arbitrary")),
    )(h, gain, tab, du, dres)


def _merge_math(mgd, mgl, yd, yl, bd, bl):
    return _sigmoid(mgd + bd) * yd + _sigmoid(mgl + bl) * yl


def _merge_fwd(p, ydn, ylru, b_merge, *, B, T, nc, D, col0, tm):
    N = T - nc
    ntl, nt, nct, cb = N // tm, T // tm, nc // tm, col0 // D

    def body(mgd_ref, mgl_ref, yd_ref, yl_ref, bm_ref, o_ref):
        o_ref[...] = _merge_math(mgd_ref[...], mgl_ref[...], yd_ref[...], yl_ref[...],
                                 bm_ref[:, 0:D], bm_ref[:, D:2 * D]).astype(BF16)

    prow = lambda b, t: b * nt + nct + t
    return pl.pallas_call(
        body, name="merge_fwd", grid=(B, ntl),
        in_specs=[pl.BlockSpec((tm, D), lambda b, t: (prow(b, t), cb)),
                  pl.BlockSpec((tm, D), lambda b, t: (prow(b, t), cb + 1)),
                  pl.BlockSpec((tm, D), lambda b, t: (b * ntl + t, 0)),
                  pl.BlockSpec((tm, D), lambda b, t: (b * ntl + t, 0)),
                  pl.BlockSpec((1, 2 * D), lambda b, t: (0, 0))],
        out_specs=pl.BlockSpec((tm, D), lambda b, t: (b * ntl + t, 0)),
        out_shape=jax.ShapeDtypeStruct((B * N, D), BF16),
        compiler_params=_params(("parallel", "parallel")),
    )(p, p, ydn, ylru, b_merge)


def _merge_bwd(p, ydn, ylru, b_merge, dmix, dp, *, B, T, nc, D, col0, tm):
    N = T - nc
    ntl, nt, nct, cb = N // tm, T // tm, nc // tm, col0 // D
    assert col0 % (2 * D) == 0

    def body(mgd_ref, mgl_ref, yd_ref, yl_ref, bm_ref, dm_ref, dp_any, dyd_ref, dyl_ref, dp_ref, sums_ref):
        _, vjp = jax.vjp(_merge_math, mgd_ref[...], mgl_ref[...], yd_ref[...], yl_ref[...],
                         bm_ref[:, 0:D], bm_ref[:, D:2 * D])
        dmgd, dmgl, dyd, dyl, dbd, dbl = vjp(dm_ref[...])
        dyd_ref[...] = dyd.astype(BF16)
        dyl_ref[...] = dyl.astype(BF16)
        dp_ref[:, 0:D] = dmgd.astype(BF16)
        dp_ref[:, D:2 * D] = dmgl.astype(BF16)

        @pl.when((pl.program_id(0) == 0) & (pl.program_id(1) == 0))
        def _():
            sums_ref[...] = jnp.zeros_like(sums_ref)

        sums_ref[0:1, 0:D] += dbd
        sums_ref[0:1, D:2 * D] += dbl

    prow = lambda b, t: b * nt + nct + t
    row = pl.BlockSpec((tm, D), lambda b, t: (b * ntl + t, 0))
    return pl.pallas_call(
        body, name="merge_bwd", grid=(B, ntl),
        in_specs=[pl.BlockSpec((tm, D), lambda b, t: (prow(b, t), cb)),
                  pl.BlockSpec((tm, D), lambda b, t: (prow(b, t), cb + 1)),
                  row, row, pl.BlockSpec((1, 2 * D), lambda b, t: (0, 0)), row,
                  pl.BlockSpec(memory_space=pl.ANY)],
        out_specs=[row, row,
                   pl.BlockSpec((tm, 2 * D), lambda b, t: (prow(b, t), cb // 2)),
                   pl.BlockSpec((8, 2 * D), lambda b, t: (0, 0))],
        out_shape=[jax.ShapeDtypeStruct((B * N, D), BF16), jax.ShapeDtypeStruct((B * N, D), BF16),
                   jax.ShapeDtypeStruct(dp.shape, dp.dtype), jax.ShapeDtypeStruct((8, 2 * D), F32)],
        input_output_aliases={6: 2},
        compiler_params=_params(("arbitrary", "arbitrary")),
    )(p, p, ydn, ylru, b_merge, dmix, dp)


def _post_math(x, mix, g1, gate, g2, sh, sc):
    h1 = x + _rmsn(mix, g1) * gate
    return h1, _rmsn(h1, g2) * (1.0 + sc) + sh


def _post_fwd(x, mix, gains, vecs, *, tm):
    B, N, D = x.shape
    ntl = N // tm

    def body(x_ref, mix_ref, g_ref, v_ref, h1_ref, u2_ref):
        v = v_ref[0]
        h1, u2 = _post_math(x_ref[0], mix_ref[...], g_ref[0:1], v[0:1], g_ref[1:2], v[1:2], v[2:3])
        h1_ref[...] = h1
        u2_ref[...] = u2.astype(BF16)

    row = pl.BlockSpec((tm, D), lambda b, t: (b * ntl + t, 0))
    return pl.pallas_call(
        body, name="post_fwd", grid=(B, ntl),
        in_specs=[pl.BlockSpec((1, tm, D), lambda b, t: (b, t, 0)), row,
                  pl.BlockSpec((8, D), lambda b, t: (0, 0)), pl.BlockSpec((1, 8, D), lambda b, t: (b, 0, 0))],
        out_specs=[row, row],
        out_shape=[jax.ShapeDtypeStruct((B * N, D), F32), jax.ShapeDtypeStruct((B * N, D), BF16)],
        compiler_params=_params(("parallel", "parallel")),
    )(x, mix, gains, vecs)


def _post_bwd(x, mix, gains, vecs, dh1, du2, *, tm):
    B, N, D = x.shape
    ntl = N // tm

    def body(x_ref, mix_ref, g_ref, v_ref, dh1_ref, du2_ref, dx_ref, dmix_ref, sums_ref):
        v = v_ref[0]
        _, vjp = jax.vjp(_post_math, x_ref[0], mix_ref[...], g_ref[0:1], v[0:1], g_ref[1:2], v[1:2], v[2:3])
        dx, dmix, dg1, dgate, dg2, dsh, dsc = vjp((dh1_ref[...], du2_ref[...]))
        dx_ref[...] = dx
        dmix_ref[...] = dmix.astype(BF16)

        @pl.when(pl.program_id(1) == 0)
        def _():
            sums_ref[...] = jnp.zeros_like(sums_ref)

        sums_ref[0, 0:1, :] += dgate
        sums_ref[0, 1:2, :] += dsh
        sums_ref[0, 2:3, :] += dsc
        sums_ref[0, 3:4, :] += dg1
        sums_ref[0, 4:5, :] += dg2

    row = pl.BlockSpec((tm, D), lambda b, t: (b * ntl + t, 0))
    return pl.pallas_call(
        body, name="post_bwd", grid=(B, ntl),
        in_specs=[pl.BlockSpec((1, tm, D), lambda b, t: (b, t, 0)), row,
                  pl.BlockSpec((8, D), lambda b, t: (0, 0)), pl.BlockSpec((1, 8, D), lambda b, t: (b, 0, 0)), row, row],
        out_specs=[row, row, pl.BlockSpec((1, 8, D), lambda b, t: (b, 0, 0))],
        out_shape=[jax.ShapeDtypeStruct((B * N, D), F32), jax.ShapeDtypeStruct((B * N, D), BF16),
                   jax.ShapeDtypeStruct((B, 8, D), F32)],
        compiler_params=_params(("parallel", "arbitrary")),
    )(x, mix, gains, vecs, dh1, du2)


def _final_math(dn, g4, gate5):
    return _rmsn(dn, g4) * gate5


def _final(h1, dn, target, gains, vecs, *, tm):
    B, N, D = target.shape
    ntl = N // tm

    def body(h1_ref, dn_ref, t_ref, g_ref, v_ref, ddn_ref, dout_ref, sums_ref):
        v = v_ref[0]
        y, vjp = jax.vjp(_final_math, dn_ref[...], g_ref[2:3], v[3:4])
        err = h1_ref[...] + y - t_ref[0]
        dout = err * (1.0 / D)
        ddn, dg4, dgate5 = vjp(dout)
        ddn_ref[...] = ddn.astype(BF16)
        dout_ref[...] = dout

        @pl.when(pl.program_id(1) == 0)
        def _():
            sums_ref[...] = jnp.zeros_like(sums_ref)

        sums_ref[0, 0:1, :] += dgate5
        sums_ref[0, 1:2, :] += dg4
        sums_ref[0, 2:3, :] += jnp.sum(err * err, axis=0, keepdims=True) * (0.5 / D)

    row = pl.BlockSpec((tm, D), lambda b, t: (b * ntl + t, 0))
    return pl.pallas_call(
        body, name="final", grid=(B, ntl),
        in_specs=[row, row, pl.BlockSpec((1, tm, D), lambda b, t: (b, t, 0)),
                  pl.BlockSpec((8, D), lambda b, t: (0, 0)), pl.BlockSpec((1, 8, D), lambda b, t: (b, 0, 0))],
        out_specs=[row, row, pl.BlockSpec((1, 8, D), lambda b, t: (b, 0, 0))],
        out_shape=[jax.ShapeDtypeStruct((B * N, D), BF16), jax.ShapeDtypeStruct((B * N, D), F32),
                   jax.ShapeDtypeStruct((B, 8, D), F32)],
        compiler_params=_params(("parallel", "arbitrary")),
    )(h1, dn, target, gains, vecs)


def _shift(x, s):
    s = s % x.shape[0]
    return x if s == 0 else pltpu.roll(x, s, 0)


def _seg_taps(T, nc, width, pad_left):
    t = lax.broadcasted_iota(jnp.int32, (T, 1), 0)
    pos = jnp.where(t < nc, t, t - nc)
    seg = jnp.where(t < nc, nc, T - nc)
    taps = []
    for k in range(width):
        src = pos + (k - pad_left)
        taps.append((pad_left - k, (src >= 0) & (src < seg)))
    return taps


def _grid_taps(N):
    t = lax.broadcasted_iota(jnp.int32, (N, 1), 0)
    wcol = t % GRID_W
    taps = []
    for dr in (-1, 0, 1):
        for dw in (-1, 0, 1):
            off = dr * GRID_W + dw
            ok = (wcol + dw >= 0) & (wcol + dw < GRID_W) & (t + dr * GRID_W >= 0) & (t + dr * GRID_W < N)
            taps.append((-off, ok))
    return taps


def _conv_fwd(x, w, taps):
    y = jnp.zeros_like(x)
    for k, (s, m) in enumerate(taps):
        y = y + w[k:k + 1] * jnp.where(m, _shift(x, s), 0.0)
    return y


def _conv_bwd(x, w, taps, dy):
    dx = jnp.zeros_like(x)
    dws = []
    for k, (s, m) in enumerate(taps):
        dym = jnp.where(m, dy, 0.0)
        dx = dx + w[k:k + 1] * _shift(dym, -s)
        dws.append(jnp.sum(dym * _shift(x, s), axis=0, keepdims=True))
    return dx, jnp.concatenate(dws, axis=0)


def _ffn_act_fwd(F, w9, bias, *, B, N, DFF, tc):
    nj = DFF // tc

    def body(fg_ref, fv_ref, w_ref, b_ref, o_ref):
        fg = _conv_fwd(fg_ref[...], w_ref[...], _grid_taps(N)) + b_ref[...]
        o_ref[...] = (_gelu(fg) * fv_ref[...]).astype(BF16)

    return pl.pallas_call(
        body, name="ffn_act_fwd", grid=(B, nj),
        in_specs=[pl.BlockSpec((N, tc), lambda b, j: (b, j)), pl.BlockSpec((N, tc), lambda b, j: (b, nj + j)),
                  pl.BlockSpec((9, tc), lambda b, j: (0, j)), pl.BlockSpec((1, tc), lambda b, j: (0, j))],
        out_specs=pl.BlockSpec((N, tc), lambda b, j: (b, j)),
        out_shape=jax.ShapeDtypeStruct((B * N, DFF), BF16),
        compiler_params=_params(("parallel", "parallel")),
    )(F, F, w9, bias)


def _ffn_act_bwd(F, w9, bias, df, *, B, N, DFF, tc):
    nj = DFF // tc

    def body(fg_ref, fv_ref, w_ref, b_ref, df_ref, dfg_ref, dfv_ref, dwb_ref):
        taps = _grid_taps(N)
        x = fg_ref[...]
        fg, vjp = jax.vjp(lambda a: _gelu(a), _conv_fwd(x, w_ref[...], taps) + b_ref[...])
        dfl = df_ref[...]
        dfv_ref[...] = (dfl * fg).astype(BF16)
        (dpre,) = vjp(dfl * fv_ref[...])
        dx, dw = _conv_bwd(x, w_ref[...], taps, dpre)
        dfg_ref[...] = dx.astype(BF16)

        @pl.when(pl.program_id(1) == 0)
        def _():
            dwb_ref[...] = jnp.zeros_like(dwb_ref)

        dwb_ref[0:9, :] += dw
        dwb_ref[9:10, :] += jnp.sum(dpre, axis=0, keepdims=True)

    col = pl.BlockSpec((N, tc), lambda j, b: (b, j))
    return pl.pallas_call(
        body, name="ffn_act_bwd", grid=(nj, B),
        in_specs=[col, pl.BlockSpec((N, tc), lambda j, b: (b, nj + j)),
                  pl.BlockSpec((9, tc), lambda j, b: (0, j)), pl.BlockSpec((1, tc), lambda j, b: (0, j)), col],
        out_specs=[col, col, pl.BlockSpec((16, tc), lambda j, b: (0, j))],
        out_shape=[jax.ShapeDtypeStruct((B * N, DFF), BF16), jax.ShapeDtypeStruct((B * N, DFF), BF16),
                   jax.ShapeDtypeStruct((16, DFF), F32)],
        compiler_params=_params(("parallel", "arbitrary")),
    )(F, F, w9, bias, df)


def _dnprep_math(y, is_qk, scale):
    s = _silu(y)
    n = s * lax.rsqrt(jnp.sum(s * s, axis=-1, keepdims=True) + EPS) * scale
    return jnp.where(is_qk, n, s)


def _dnprep_fwd(p, cw, *, B, T, nc, H, HD):
    def body(x_ref, w_ref, o_ref):
        j = pl.program_id(1)
        y = _conv_fwd(x_ref[...], w_ref[...], _seg_taps(T, nc, 4, 2))
        o_ref[...] = _dnprep_math(y, j < 2 * H, jnp.where(j < H, HD ** -0.5, 1.0))

    return pl.pallas_call(
        body, name="dnprep_fwd", grid=(B, 3 * H),
        in_specs=[pl.BlockSpec((T, HD), lambda b, j: (b, j)), pl.BlockSpec((4, HD), lambda b, j: (0, j))],
        out_specs=pl.BlockSpec((T, HD), lambda b, j: (b, j)),
        out_shape=jax.ShapeDtypeStruct((B * T, 3 * H * HD), F32),
        compiler_params=_params(("parallel", "parallel")),
    )(p, cw)


def _dnprep_bwd(p, cw, dqkv, dp, *, B, T, nc, H, HD):
    def body(x_ref, w_ref, d_ref, dp_any, dp_ref, dcw_ref):
        j = pl.program_id(0)
        taps = _seg_taps(T, nc, 4, 2)
        x = x_ref[...]
        y = _conv_fwd(x, w_ref[...], taps)
        is_qk, scale = j < 2 * H, jnp.where(j < H, HD ** -0.5, 1.0)
        _, vjp = jax.vjp(lambda a: _dnprep_math(a, is_qk, scale), y)
        (dy,) = vjp(d_ref[0])
        dx, dw = _conv_bwd(x, w_ref[...], taps, dy)
        dp_ref[...] = dx.astype(BF16)

        @pl.when(pl.program_id(1) == 0)
        def _():
            dcw_ref[...] = jnp.zeros_like(dcw_ref)

        dcw_ref[0:4, :] += dw

    col = pl.BlockSpec((T, HD), lambda j, b: (b, j))
    return pl.pallas_call(
        body, name="dnprep_bwd", grid=(3 * H, B),
        in_specs=[col, pl.BlockSpec((4, HD), lambda j, b: (0, j)),
                  pl.BlockSpec((1, T, HD), lambda j, b: (j // H, b, j % H)), pl.BlockSpec(memory_space=pl.ANY)],
        out_specs=[col, pl.BlockSpec((8, HD), lambda j, b: (0, j))],
        out_shape=[jax.ShapeDtypeStruct(dp.shape, dp.dtype), jax.ShapeDtypeStruct((8, 3 * H * HD), F32)],
        input_output_aliases={3: 0},
        compiler_params=_params(("parallel", "arbitrary")),
    )(p, cw, dqkv, dp)


def _gb_math(ab, alog, dtb, H):
    lane = lax.broadcasted_iota(jnp.int32, ab.shape, 1)
    g = -jnp.exp(alog) * _softplus(ab + dtb)
    return jnp.where(lane < 2 * H, g, jnp.where(lane < 4 * H, _sigmoid(ab), 0.0))


def _gb_fwd(p, prm, *, rows, col0, H, tm):
    def body(x_ref, prm_ref, o_ref):
        o_ref[...] = _gb_math(x_ref[...], prm_ref[0:1], prm_ref[1:2], H)

    return pl.pallas_call(
        body, name="gb_fwd", grid=(rows // tm,),
        in_specs=[pl.BlockSpec((tm, LANES), lambda t: (t, col0 // LANES)), pl.BlockSpec((8, LANES), lambda t: (0, 0))],
        out_specs=pl.BlockSpec((tm, LANES), lambda t: (t, 0)),
        out_shape=jax.ShapeDtypeStruct((rows, LANES), F32),
        compiler_params=_params(("parallel",)),
    )(p, prm)


def _gb_bwd(p, prm, dgb, dp, *, rows, col0, H, tm):
    def body(x_ref, prm_ref, d_ref, dp_any, dp_ref, dprm_ref):
        _, vjp = jax.vjp(lambda a, b, c: _gb_math(a, b, c, H), x_ref[...], prm_ref[0:1], prm_ref[1:2])
        dab, dalog, ddtb = vjp(d_ref[...])
        dp_ref[...] = dab.astype(BF16)

        @pl.when(pl.program_id(0) == 0)
        def _():
            dprm_ref[...] = jnp.zeros_like(dprm_ref)

        dprm_ref[0:1, :] += dalog
        dprm_ref[1:2, :] += ddtb

    blk = pl.BlockSpec((tm, LANES), lambda t: (t, col0 // LANES))
    return pl.pallas_call(
        body, name="gb_bwd", grid=(rows // tm,),
        in_specs=[blk, pl.BlockSpec((8, LANES), lambda t: (0, 0)), pl.BlockSpec((tm, LANES), lambda t: (t, 0)),
                  pl.BlockSpec(memory_space=pl.ANY)],
        out_specs=[blk, pl.BlockSpec((8, LANES), lambda t: (0, 0))],
        out_shape=[jax.ShapeDtypeStruct(dp.shape, dp.dtype), jax.ShapeDtypeStruct((8, LANES), F32)],
        input_output_aliases={3: 0},
        compiler_params=_params(("arbitrary",)),
    )(p, prm, dgb, dp)


def _lru_scan(a_ref, b_ref, h_ref, hp_ref, segs):
    C = a_ref.shape[1]
    row = lax.broadcasted_iota(jnp.int32, (SUBLANES, C), 0)
    carry = jnp.zeros((1, C), F32)
    for start, rows, reverse in segs:
        nb = rows // SUBLANES

        def blk(i, carry, start=start, nb=nb, reverse=reverse):
            r0 = pl.multiple_of(start + (nb - 1 - i if reverse else i) * SUBLANES, SUBLANES)
            A = a_ref[pl.ds(r0, SUBLANES), :]
            Bv = b_ref[pl.ds(r0, SUBLANES), :]
            for s in (1, 2, 4):
                sh = SUBLANES - s if reverse else s
                m = (row < SUBLANES - s) if reverse else (row >= s)
                Bv = jnp.where(m, A * pltpu.roll(Bv, sh, 0) + Bv, Bv)
                A = jnp.where(m, A * pltpu.roll(A, sh, 0), A)
            Hv = Bv + A * carry
            h_ref[pl.ds(r0, SUBLANES), :] = Hv
            if hp_ref is not None:
                if reverse:
                    hp = jnp.where(row < SUBLANES - 1, pltpu.roll(Hv, SUBLANES - 1, 0), carry)
                else:
                    hp = jnp.where(row >= 1, pltpu.roll(Hv, 1, 0), carry)
                hp_ref[pl.ds(r0, SUBLANES), :] = hp
            return Hv[0:1] if reverse else Hv[SUBLANES - 1:SUBLANES]

        carry = lax.fori_loop(0, nb, blk, carry)


def _lru_orders(T, nc, d):
    if d == 0:
        return [(0, T, False)], [(0, T, True)]
    return [(0, nc, True), (nc, T - nc, True)], [(nc, T - nc, False), (0, nc, False)]


def _bdot(a, b, dims=(((1,), (0,)), ((), ()))):
    return lax.dot_general(a.astype(BF16), b.astype(BF16), dims, preferred_element_type=F32)


_NT = (((1,), (1,)), ((), ()))
_TN = (((0,), (0,)), ((), ()))


def _blockdiag(w, C):
    nd, nb, bd, _ = w.shape
    per = C // bd
    out = jnp.einsum('dnpij,pq->dnpiqj', w.reshape(nd, nb // per, per, bd, bd), jnp.eye(per, dtype=w.dtype))
    return out.reshape(nd, nb // per, C, C)


def _blockdiag_extract(dw, bd):
    nd, nj, C, _ = dw.shape
    per = C // bd
    out = jnp.einsum('dnpiqj,pq->dnpij', dw.reshape(nd, nj, per, bd, per, bd), jnp.eye(per, dtype=dw.dtype))
    return out.reshape(nd, nj * per, bd, bd)


def _lru_fwd(p, cw, lv, wr, wi, *, B, T, nc, LW, col0, C):
    N = T - nc
    nj = LW // C

    def body(x_ref, cw_ref, lv_ref, wr_ref, wi_ref, o_ref, a_s, b_s, h_s, acc_s):
        lv_ = lv_ref[...]
        xc = _conv_fwd(x_ref[:, 0:C], cw_ref[...], _seg_taps(T, nc, 4, 2)) + lv_[0:1]
        for d in (0, 1):
            r = _sigmoid(_bdot(xc, wr_ref[d, 0]) + lv_[1 + d:2 + d])
            i = _sigmoid(_bdot(xc, wi_ref[d, 0]) + lv_[3 + d:4 + d])
            la = -LRU_C * r * _softplus(-lv_[5 + d:6 + d])
            a_s[...] = jnp.exp(la)
            b_s[...] = jnp.sqrt(1.0 - jnp.exp(2.0 * la)) * i * xc
            _lru_scan(a_s, b_s, h_s, None, _lru_orders(T, nc, d)[0])
            if d == 0:
                acc_s[...] = h_s[...]
            else:
                acc_s[...] += h_s[...]
        o_ref[...] = (acc_s[nc:, :] * _gelu(x_ref[nc:, C:2 * C])).astype(BF16)

    return pl.pallas_call(
        body, name="lru_fwd", grid=(B, nj),
        in_specs=[pl.BlockSpec((T, 2 * C), lambda b, j: (b, col0 // (2 * C) + j)),
                  pl.BlockSpec((4, C), lambda b, j: (0, j)), pl.BlockSpec((8, C), lambda b, j: (0, j)),
                  pl.BlockSpec((2, 1, C, C), lambda b, j: (0, j, 0, 0)), pl.BlockSpec((2, 1, C, C), lambda b, j: (0, j, 0, 0))],
        out_specs=pl.BlockSpec((N, C), lambda b, j: (b, j)),
        out_shape=jax.ShapeDtypeStruct((B * N, LW), BF16),
        scratch_shapes=[pltpu.VMEM((T, C), F32)] * 4,
        compiler_params=_params(("parallel", "parallel")),
    )(p, cw, lv, wr, wi)


def _lru_bwd(p, cw, lv, wr, wi, dy, dp, *, B, T, nc, LW, col0, C):
    N = T - nc
    nj = LW // C

    def body(x_ref, cw_ref, lv_ref, wr_ref, wi_ref, dy_ref, dp_any, dp_ref, dcw_ref, dlv_ref, dwr_ref, dwi_ref,
             a_s, b_s, h_s, hp_s, mu_s, mup_s, dh_s, dxc_s, hsum_s):
        taps = _seg_taps(T, nc, 4, 2)
        lv_ = lv_ref[...]
        xl = x_ref[:, 0:C]
        xc = _conv_fwd(xl, cw_ref[...], taps) + lv_[0:1]
        gel, gelu_vjp = jax.vjp(_gelu, x_ref[nc:, C:2 * C])
        dh_s[0:nc, :] = jnp.zeros((nc, C), F32)
        dh_s[nc:, :] = dy_ref[...] * gel
        dxc_s[...] = jnp.zeros_like(dxc_s)

        @pl.when(pl.program_id(1) == 0)
        def _():
            dcw_ref[...] = jnp.zeros_like(dcw_ref)
            dlv_ref[...] = jnp.zeros_like(dlv_ref)
            dwr_ref[...] = jnp.zeros_like(dwr_ref)
            dwi_ref[...] = jnp.zeros_like(dwi_ref)

        for d in (0, 1):
            fwd_order, adj_order = _lru_orders(T, nc, d)
            lam = lv_[5 + d:6 + d]
            r = _sigmoid(_bdot(xc, wr_ref[d, 0]) + lv_[1 + d:2 + d])
            i = _sigmoid(_bdot(xc, wi_ref[d, 0]) + lv_[3 + d:4 + d])
            sp = _softplus(-lam)
            la = -LRU_C * r * sp
            a = jnp.exp(la)
            e2 = jnp.exp(2.0 * la)
            mult = jnp.sqrt(1.0 - e2)
            a_s[...] = a
            b_s[...] = mult * i * xc
            _lru_scan(a_s, b_s, h_s, hp_s, fwd_order)
            if d == 0:
                hsum_s[...] = h_s[...]
            else:
                hsum_s[...] += h_s[...]
            b_s[...] = a * dh_s[...]
            _lru_scan(a_s, b_s, mu_s, mup_s, adj_order)
            dinp = dh_s[...] + mup_s[...]
            da = dinp * hp_s[...]
            dmult = dinp * i * xc
            di = dinp * mult * xc
            dla = da * a - dmult * e2 / mult
            dpre_r = (dla * (-LRU_C * sp)) * r * (1.0 - r)
            dpre_i = di * i * (1.0 - i)
            dsp = jnp.sum(dla * (-LRU_C * r), axis=0, keepdims=True)
            dxc_s[...] += dinp * mult * i + _bdot(dpre_r, wr_ref[d, 0], _NT) + _bdot(dpre_i, wi_ref[d, 0], _NT)
            dwr_ref[d, 0] += _bdot(xc, dpre_r, _TN)
            dwi_ref[d, 0] += _bdot(xc, dpre_i, _TN)
            dlv_ref[1 + d:2 + d, :] += jnp.sum(dpre_r, axis=0, keepdims=True)
            dlv_ref[3 + d:4 + d, :] += jnp.sum(dpre_i, axis=0, keepdims=True)
            dlv_ref[5 + d:6 + d, :] += -dsp * _sigmoid(-lam)

        dxc = dxc_s[...]
        dxl, dw = _conv_bwd(xl, cw_ref[...], taps, dxc)
        dcw_ref[0:4, :] += dw
        dlv_ref[0:1, :] += jnp.sum(dxc, axis=0, keepdims=True)
        dp_ref[:, 0:C] = dxl.astype(BF16)
        (dyl,) = gelu_vjp(dy_ref[...] * hsum_s[nc:, :])
        dp_ref[0:nc, C:2 * C] = jnp.zeros((nc, C), BF16)
        dp_ref[nc:, C:2 * C] = dyl.astype(BF16)

    xblk = pl.BlockSpec((T, 2 * C), lambda j, b: (b, col0 // (2 * C) + j))
    wblk = pl.BlockSpec((2, 1, C, C), lambda j, b: (0, j, 0, 0))
    vblk = pl.BlockSpec((8, C), lambda j, b: (0, j))
    return pl.pallas_call(
        body, name="lru_bwd", grid=(nj, B),
        in_specs=[xblk, pl.BlockSpec((4, C), lambda j, b: (0, j)), vblk, wblk, wblk,
                  pl.BlockSpec((N, C), lambda j, b: (b, j)), pl.BlockSpec(memory_space=pl.ANY)],
        out_specs=[xblk, vblk, vblk, wblk, wblk],
        out_shape=[jax.ShapeDtypeStruct(dp.shape, dp.dtype), jax.ShapeDtypeStruct((8, LW), F32),
                   jax.ShapeDtypeStruct((8, LW), F32), jax.ShapeDtypeStruct((2, nj, C, C), F32),
                   jax.ShapeDtypeStruct((2, nj, C, C), F32)],
        scratch_shapes=[pltpu.VMEM((T, C), F32)] * 9,
        input_output_aliases={6: 0},
        compiler_params=_params(("parallel", "arbitrary")),
    )(p, cw, lv, wr, wi, dy, dp)


def _chunk_masks(upper):
    i = lax.broadcasted_iota(jnp.int32, (CHUNK, CHUNK), 0)
    j = lax.broadcasted_iota(jnp.int32, (CHUNK, CHUNK), 1)
    return i == j, (j >= i) if upper else (j <= i), (j > i) if upper else (j < i)


def _col2row(c, eye):
    return jnp.sum(jnp.where(eye, c, 0.0), axis=0, keepdims=True)


def _row2col(r, eye):
    return jnp.sum(jnp.where(eye, r, 0.0), axis=1, keepdims=True)


def _rowsum(x):
    return jnp.sum(x, axis=1, keepdims=True)


def _dot3(a, b):
    ah, bh = a.astype(BF16), b.astype(BF16)
    al, bl = (a - ah.astype(F32)).astype(BF16), (b - bh.astype(F32)).astype(BF16)
    dot = functools.partial(jnp.dot, preferred_element_type=F32)
    return dot(ah, bh) + (dot(ah, bl) + dot(al, bh))


def _unit_tri_inverse(L, eye):
    X = -L
    R = jnp.where(eye, 1.0, 0.0) + X
    Xp = X
    for _ in range(int(math.log2(CHUNK)) - 1):
        Xp = _dot3(Xp, Xp)
        R = R + _dot3(R, Xp)
    return R


def _delta_chunk_common(q, k, v, gcol, bcol, upper):
    eye, incl, strict = _chunk_masks(upper)
    gc = _rowsum(jnp.where(incl, _col2row(gcol, eye), 0.0))
    D = jnp.where(incl, jnp.exp(jnp.minimum(gc - _col2row(gc, eye), 0.0)), 0.0)
    kb = k * bcol
    A = _bdot(kb, k, _NT)
    L = jnp.where(strict, A * D, 0.0)
    eg = jnp.exp(gc)
    gl = jnp.sum(gcol, axis=0, keepdims=True)
    P = _bdot(q, k, _NT)
    attn = jnp.where(incl, P * D, 0.0)
    return dict(eye=eye, incl=incl, strict=strict, gc=gc, D=D, kb=kb, A=A, L=L, eg=eg, gl=gl, egl=jnp.exp(gl),
                attn=attn, kbe=kb * eg, vb=v * bcol, qe=q * eg, kd=k * jnp.exp(gl - gc))


def _delta_chunk_pre(q, k, v, gcol, bcol, upper):
    c = _delta_chunk_common(q, k, v, gcol, bcol, upper)
    Tm = _unit_tri_inverse(c["L"], c["eye"])
    w = _bdot(Tm, c["kbe"])
    u = _bdot(Tm, c["vb"])
    return (Tm, _bdot(c["kd"], w, _TN), _bdot(c["kd"], u, _TN), c["qe"] - _bdot(c["attn"], w),
            _bdot(c["attn"], u), c["egl"])


def _delta_chunk_bwd(q, k, v, gcol, bcol, S, Tm, do, dS2, upper):
    c = _delta_chunk_common(q, k, v, gcol, bcol, upper)
    eye, incl, strict, D, eg, egl = c["eye"], c["incl"], c["strict"], c["D"], c["eg"], c["egl"]
    kb, kbe, vb, qe, kd, attn = c["kb"], c["kbe"], c["vb"], c["qe"], c["kd"], c["attn"]
    w = _bdot(Tm, kbe)
    vn = _bdot(Tm, vb) - _bdot(w, S)
    dvn = _bdot(kd, dS2) + _bdot(attn, do, _TN)
    dkd = _bdot(vn, dS2, _NT)
    dgl = jnp.sum(_rowsum(dS2 * S), axis=0, keepdims=True) * egl
    dqe = _bdot(do, S, _NT)
    dattn = jnp.where(incl, _bdot(do, vn, _NT), 0.0)
    dw = -_bdot(dvn, S, _NT)
    r = _rowsum(dkd * kd)
    dk = dkd * jnp.exp(c["gl"] - c["gc"])
    dgl = dgl + jnp.sum(r, axis=0, keepdims=True)
    dgc = _rowsum(dqe * qe) - r
    dq = dqe * eg + _bdot(dattn * D, k)
    dk = dk + _bdot(dattn * D, q, _TN)
    E = dattn * attn
    dTm = _bdot(dvn, vb, _NT) + _bdot(dw, kbe, _NT)
    dvb = _bdot(Tm, dvn, _TN)
    dv = dvb * bcol
    dbeta = _rowsum(dvb * v)
    dkbe = _bdot(Tm, dw, _TN)
    dkb = dkbe * eg
    dgc = dgc + _rowsum(dkbe * kbe)
    dL = jnp.where(strict, -_bdot(Tm, _bdot(dTm, Tm, _NT), _TN), 0.0)
    dA = dL * D
    E = E + dL * c["L"]
    dkb = dkb + _bdot(dA, k)
    dk = dk + _bdot(dA, kb, _TN) + dkb * bcol
    dbeta = dbeta + _rowsum(dkb * k)
    dgc = dgc + _rowsum(E) - _row2col(jnp.sum(E, axis=0, keepdims=True), eye)
    dg = _row2col(jnp.sum(jnp.where(incl, dgc, 0.0), axis=0, keepdims=True), eye) + dgl
    return dq, dk, dv, dg, dbeta


def _delta_chunk_at(T, nc, d, i):
    n, ncc = T // CHUNK, nc // CHUNK
    return i if d == 0 else jnp.where(i < ncc, ncc - 1 - i, n - 1 - (i - ncc))


def _dn_out_math(o, onorm, z):
    return _rmsn(o, onorm) * _silu(z)


def _delta_fwd(qkv, gb, p, onorm, *, B, T, nc, H, HD):
    N = T - nc
    n = T // CHUNK

    def body(q_ref, k_ref, v_ref, gb_ref, z_ref, on_ref, y_ref, o_ref, K_s, N_s, Qp_s, O0_s, eg_s, o_s):
        h = pl.program_id(1)
        lane = lax.broadcasted_iota(jnp.int32, (CHUNK, LANES), 1)

        def pre(c, carry):
            rows = pl.ds(pl.multiple_of(c * CHUNK, CHUNK), CHUNK)
            gbb = gb_ref[rows, :]
            for d in (0, 1):
                gcol = _rowsum(jnp.where(lane == d * H + h, gbb, 0.0))
                bcol = _rowsum(jnp.where(lane == 2 * H + d * H + h, gbb, 0.0))
                _, K, Nn, Qp, O0, egl = _delta_chunk_pre(q_ref[rows, :], k_ref[rows, :], v_ref[rows, :], gcol, bcol, d == 1)
                K_s[d * n + c] = K.astype(BF16)
                N_s[d * n + c] = Nn
                Qp_s[d, rows, :] = Qp.astype(BF16)
                O0_s[d, rows, :] = O0
                eg_s[d * n + c] = jnp.broadcast_to(egl, (SUBLANES, HD))
            return carry

        lax.fori_loop(0, n, pre, 0)

        def step(i, Ss):
            out = []
            for d in (0, 1):
                c = _delta_chunk_at(T, nc, d, i)
                rows = pl.ds(pl.multiple_of(c * CHUNK, CHUNK), CHUNK)
                Sb = Ss[d].astype(BF16)
                o_s[d, rows, :] = jnp.dot(Qp_s[d, rows, :], Sb, preferred_element_type=F32) + O0_s[d, rows, :]
                out.append(eg_s[d * n + c][0:1] * Ss[d] + N_s[d * n + c]
                           - jnp.dot(K_s[d * n + c], Sb, preferred_element_type=F32))
            return tuple(out)

        lax.fori_loop(0, n, step, (jnp.zeros((HD, HD), F32), jnp.zeros((HD, HD), F32)))
        o = o_s[0, nc:, :] + o_s[1, nc:, :]
        o_ref[...] = o
        y_ref[...] = _dn_out_math(o, on_ref[...], z_ref[nc:, :]).astype(BF16)

    col = lambda off: pl.BlockSpec((T, HD), lambda b, h: (b, off + h))
    lat = pl.BlockSpec((N, HD), lambda b, h: (b, h))
    return pl.pallas_call(
        body, name="delta_fwd", grid=(B, H),
        in_specs=[col(0), col(H), col(2 * H), pl.BlockSpec((T, LANES), lambda b, h: (b, 0)), col(3 * H),
                  pl.BlockSpec((1, HD), lambda b, h: (0, 0))],
        out_specs=[lat, lat],
        out_shape=[jax.ShapeDtypeStruct((B * N, H * HD), BF16), jax.ShapeDtypeStruct((B * N, H * HD), F32)],
        scratch_shapes=[pltpu.VMEM((2 * n, HD, HD), BF16), pltpu.VMEM((2 * n, HD, HD), F32),
                        pltpu.VMEM((2, T, HD), BF16), pltpu.VMEM((2, T, HD), F32),
                        pltpu.VMEM((2 * n, SUBLANES, HD), F32), pltpu.VMEM((2, T, HD), F32)],
        compiler_params=_params(("parallel", "parallel")),
    )(qkv, qkv, qkv, gb, p, onorm)


def _delta_bwd(qkv, gb, p, onorm, o, dy, dp, *, B, T, nc, H, HD):
    N = T - nc
    n = T // CHUNK

    def body(q_ref, k_ref, v_ref, gb_ref, z_ref, on_ref, o_ref, dy_ref, dp_any, dqkv_ref, dgb_ref, dp_ref, don_ref,
             do_s, Tm_s, K_s, N_s, R_s, eg_s, S_s, dS_s):
        h = pl.program_id(1)
        lane = lax.broadcasted_iota(jnp.int32, (CHUNK, LANES), 1)

        def cols(rows, d):
            gbb = gb_ref[rows, :]
            return (_rowsum(jnp.where(lane == d * H + h, gbb, 0.0)),
                    _rowsum(jnp.where(lane == 2 * H + d * H + h, gbb, 0.0)))

        _, vjp = jax.vjp(_dn_out_math, o_ref[...], on_ref[...], z_ref[nc:, :])
        do, don, dz = vjp(dy_ref[...])
        do_s[0:nc, :] = jnp.zeros((nc, HD), F32)
        do_s[nc:, :] = do
        dp_ref[0:nc, :] = jnp.zeros((nc, HD), BF16)
        dp_ref[nc:, :] = dz.astype(BF16)

        @pl.when(h == 0)
        def _():
            don_ref[...] = jnp.zeros_like(don_ref)
            dgb_ref[...] = jnp.zeros_like(dgb_ref)

        don_ref[0, 0:1, :] += don

        for d in (0, 1):
            def pre(c, carry, d=d):
                rows = pl.ds(pl.multiple_of(c * CHUNK, CHUNK), CHUNK)
                gcol, bcol = cols(rows, d)
                Tm, K, Nn, Qp, _, egl = _delta_chunk_pre(q_ref[rows, :], k_ref[rows, :], v_ref[rows, :], gcol, bcol, d == 1)
                Tm_s[c] = Tm
                K_s[c] = K.astype(BF16)
                N_s[c] = Nn
                R_s[c] = _bdot(Qp, do_s[rows, :], _TN)
                eg_s[c] = jnp.broadcast_to(egl, (SUBLANES, HD))
                return carry

            lax.fori_loop(0, n, pre, 0)

            def fwd_step(i, S, d=d):
                c = _delta_chunk_at(T, nc, d, i)
                S_s[c] = S
                return eg_s[c][0:1] * S + N_s[c] - jnp.dot(K_s[c], S.astype(BF16), preferred_element_type=F32)

            lax.fori_loop(0, n, fwd_step, jnp.zeros((HD, HD), F32))

            def bwd_step(i, dS, d=d):
                c = _delta_chunk_at(T, nc, d, n - 1 - i)
                dS_s[c] = dS
                return (eg_s[c][0:1] * dS + R_s[c]
                        - lax.dot_general(K_s[c], dS.astype(BF16), _TN, preferred_element_type=F32))

            lax.fori_loop(0, n, bwd_step, jnp.zeros((HD, HD), F32))

            def grads(c, carry, d=d):
                rows = pl.ds(pl.multiple_of(c * CHUNK, CHUNK), CHUNK)
                gcol, bcol = cols(rows, d)
                dq, dk, dv, dg, dbeta = _delta_chunk_bwd(q_ref[rows, :], k_ref[rows, :], v_ref[rows, :], gcol, bcol,
                                                         S_s[c], Tm_s[c], do_s[rows, :], dS_s[c], d == 1)
                if d == 0:
                    dqkv_ref[0, rows, :] = dq
                    dqkv_ref[1, rows, :] = dk
                    dqkv_ref[2, rows, :] = dv
                else:
                    dqkv_ref[0, rows, :] += dq
                    dqkv_ref[1, rows, :] += dk
                    dqkv_ref[2, rows, :] += dv
                dgb_ref[rows, :] += (jnp.where(lane == d * H + h, dg, 0.0)
                                     + jnp.where(lane == 2 * H + d * H + h, dbeta, 0.0))
                return carry

            lax.fori_loop(0, n, grads, 0)

    col = lambda off: pl.BlockSpec((T, HD), lambda b, h: (b, off + h))
    lat = pl.BlockSpec((N, HD), lambda b, h: (b, h))
    return pl.pallas_call(
        body, name="delta_bwd", grid=(B, H),
        in_specs=[col(0), col(H), col(2 * H), pl.BlockSpec((T, LANES), lambda b, h: (b, 0)), col(3 * H),
                  pl.BlockSpec((1, HD), lambda b, h: (0, 0)), lat, lat, pl.BlockSpec(memory_space=pl.ANY)],
        out_specs=[pl.BlockSpec((3, T, HD), lambda b, h: (0, b, h)), pl.BlockSpec((T, LANES), lambda b, h: (b, 0)),
                   col(3 * H), pl.BlockSpec((1, 8, HD), lambda b, h: (b, 0, 0))],
        out_shape=[jax.ShapeDtypeStruct((3, B * T, H * HD), F32), jax.ShapeDtypeStruct((B * T, LANES), F32),
                   jax.ShapeDtypeStruct(dp.shape, dp.dtype), jax.ShapeDtypeStruct((B, 8, HD), F32)],
        scratch_shapes=[pltpu.VMEM((T, HD), F32), pltpu.VMEM((n, CHUNK, CHUNK), F32), pltpu.VMEM((n, HD, HD), BF16),
                        pltpu.VMEM((n, HD, HD), F32), pltpu.VMEM((n, HD, HD), F32), pltpu.VMEM((n, SUBLANES, HD), F32),
                        pltpu.VMEM((n, HD, HD), F32), pltpu.VMEM((n, HD, HD), F32)],
        input_output_aliases={8: 2},
        compiler_params=_params(("parallel", "arbitrary")),
    )(qkv, qkv, qkv, gb, p, onorm, o, dy, dp)


def _rowwise(fn, ins, out_dtypes, *, name, tm=256, mult=16):
    R, W = ins[0].shape
    tm = _tile(R, tm, mult)

    def body(*refs):
        outs = fn(*[r[...] for r in refs[:len(ins)]])
        for o_ref, o in zip(refs[len(ins):], outs):
            o_ref[...] = o.astype(o_ref.dtype)

    spec = pl.BlockSpec((tm, W), lambda i: (i, 0))
    return pl.pallas_call(
        body, name=name, grid=(R // tm,), in_specs=[spec] * len(ins), out_specs=[spec] * len(out_dtypes),
        out_shape=[jax.ShapeDtypeStruct((R, W), dt) for dt in out_dtypes],
        compiler_params=_params(("parallel",)),
    )(*ins)


def _sum_lead(x, *, name, tm=256, mult=16):
    S, R, W = x.shape
    tm = _tile(R, tm, mult)

    def body(*refs):
        acc = refs[0][0].astype(F32)
        for r in refs[1:S]:
            acc = acc + r[0].astype(F32)
        refs[S][...] = acc

    return pl.pallas_call(
        body, name=name, grid=(R // tm,),
        in_specs=[pl.BlockSpec((1, tm, W), functools.partial(lambda s, i: (s, i, 0), s)) for s in range(S)],
        out_specs=pl.BlockSpec((tm, W), lambda i: (i, 0)),
        out_shape=jax.ShapeDtypeStruct((R, W), F32),
        compiler_params=_params(("parallel",)),
    )(*([x] * S))


def _adamw_math(w, g, m, v):
    m = ADAM_B1 * m + (1.0 - ADAM_B1) * g
    v = ADAM_B2 * v + (1.0 - ADAM_B2) * (g * g)
    m_hat = m / (1.0 - ADAM_B1 ** ADAM_STEP)
    v_hat = v / (1.0 - ADAM_B2 ** ADAM_STEP)
    return -ADAM_LR * (m_hat / (jnp.sqrt(v_hat) + ADAM_EPS) + ADAM_WD * w), m, v


def _adamw(w, g, m, v, *, name):
    return _rowwise(_adamw_math, [w, g, m, v], [F32, F32, F32], name=name, tm=128, mult=SUBLANES)


def _me():
    return lax.axis_index("x"), lax.axis_index("y"), lax.axis_index("c")


def _allgather_small(v):
    R, W = v.shape

    def body(x_ref, out_ref, send_sems, recv_sems, local_sem):
        x, y, c = _me()
        me, sibling = (x, y, c), (x, y, 1 - c)
        chips = [(1 - x, y), (x, 1 - y), (1 - x, 1 - y)]

        def slot(px, py, pc):
            return out_ref.at[4 * px + 2 * py + pc]

        def copy(k, block, to, src=None):
            return pltpu.make_async_remote_copy(
                src_ref=slot(*block) if src is None else src, dst_ref=slot(*block),
                send_sem=send_sems.at[k], recv_sem=recv_sems.at[k], device_id=to, device_id_type=MESH)

        mine = pltpu.make_async_copy(x_ref, slot(*me), local_sem)
        mine.start()
        first = [copy(0, me, sibling, src=x_ref)]
        first += [copy(1 + j, me, (*chip, c), src=x_ref) for j, chip in enumerate(chips)]
        for cp in first:
            cp.start()
        passed = [copy(4 + j, (*chip, c), sibling) for j, chip in enumerate(chips)]
        for j, chip in enumerate(chips):
            copy(1 + j, (*chip, c), me).wait_recv()
            passed[j].start()
        copy(0, sibling, me).wait_recv()
        for j, chip in enumerate(chips):
            copy(4 + j, (*chip, 1 - c), me).wait_recv()
        for cp in first + passed:
            cp.wait_send()
        mine.wait()

    return pl.pallas_call(
        body, name="allgather_small", out_shape=jax.ShapeDtypeStruct((8, R, W), v.dtype),
        in_specs=[pl.BlockSpec(memory_space=pltpu.VMEM)], out_specs=pl.BlockSpec(memory_space=pltpu.VMEM),
        scratch_shapes=[pltpu.SemaphoreType.DMA((7,)), pltpu.SemaphoreType.DMA((7,)), pltpu.SemaphoreType.DMA],
        compiler_params=_params(),
    )(v)


def _allgather_big(pool):
    PR, W = pool.shape[0] // 2, pool.shape[1]

    def body(x_ref, out_ref, send_sems, recv_sems, local_sem):
        x, y, c = _me()
        me, sibling = (x, y, c), (x, y, 1 - c)
        chips = [(1 - x, y), (x, 1 - y), (1 - x, 1 - y)]
        own = x_ref.at[pl.ds(c * PR, PR), :]

        def slot(px, py, pc):
            return out_ref.at[4 * px + 2 * py + pc]

        def copy(k, block, to, src=None):
            return pltpu.make_async_remote_copy(
                src_ref=slot(*block) if src is None else src, dst_ref=slot(*block),
                send_sem=send_sems.at[k], recv_sem=recv_sems.at[k], device_id=to, device_id_type=MESH)

        mine = pltpu.make_async_copy(own, slot(*me), local_sem)
        mine.start()
        first = [copy(0, me, sibling, src=own)]
        first += [copy(1 + j, me, (*chip, c), src=own) for j, chip in enumerate(chips)]
        for cp in first:
            cp.start()
        passed = [copy(4 + j, (*chip, c), sibling) for j, chip in enumerate(chips)]
        for j, chip in enumerate(chips):
            copy(1 + j, (*chip, c), me).wait_recv()
            passed[j].start()
        copy(0, sibling, me).wait_recv()
        for j, chip in enumerate(chips):
            copy(4 + j, (*chip, 1 - c), me).wait_recv()
        for cp in first + passed:
            cp.wait_send()
        mine.wait()

    return pl.pallas_call(
        body, name="allgather_big", out_shape=jax.ShapeDtypeStruct((8, PR, W), pool.dtype),
        in_specs=[pl.BlockSpec(memory_space=pl.ANY)], out_specs=pl.BlockSpec(memory_space=pl.ANY),
        scratch_shapes=[pltpu.SemaphoreType.DMA((7,)), pltpu.SemaphoreType.DMA((7,)), pltpu.SemaphoreType.DMA],
        compiler_params=_params(),
    )(pool)


def _sibling_swap(v, *, name):
    def body(x_ref, out_ref, send_sem, recv_sem):
        x, y, c = _me()
        cp = pltpu.make_async_remote_copy(src_ref=x_ref, dst_ref=out_ref, send_sem=send_sem, recv_sem=recv_sem,
                                          device_id=(x, y, 1 - c), device_id_type=MESH)
        cp.start()
        cp.wait()

    return pl.pallas_call(
        body, name=name, out_shape=jax.ShapeDtypeStruct(v.shape, v.dtype),
        in_specs=[pl.BlockSpec(memory_space=pl.ANY)], out_specs=pl.BlockSpec(memory_space=pl.ANY),
        scratch_shapes=[pltpu.SemaphoreType.DMA, pltpu.SemaphoreType.DMA],
        compiler_params=_params(),
    )(v)


def _chip_exchange(v):
    def body(x_ref, out_ref, send_sems, recv_sems, local_sem):
        x, y, c = _me()
        s_me = 2 * x + y
        chips = [(1 - x, y), (x, 1 - y), (1 - x, 1 - y)]
        mine = pltpu.make_async_copy(x_ref.at[s_me], out_ref.at[s_me], local_sem)
        mine.start()
        sends = []
        for k, (px, py) in enumerate(chips):
            cp = pltpu.make_async_remote_copy(
                src_ref=x_ref.at[2 * px + py], dst_ref=out_ref.at[s_me], send_sem=send_sems.at[k],
                recv_sem=recv_sems.at[k], device_id=(px, py, c), device_id_type=MESH)
            cp.start()
            sends.append(cp)
        for k, (px, py) in enumerate(chips):
            pltpu.make_async_remote_copy(
                src_ref=x_ref.at[s_me], dst_ref=out_ref.at[2 * px + py], send_sem=send_sems.at[k],
                recv_sem=recv_sems.at[k], device_id=(px, py, c), device_id_type=MESH).wait_recv()
        for cp in sends:
            cp.wait_send()
        mine.wait()

    return pl.pallas_call(
        body, name="chip_exchange", out_shape=jax.ShapeDtypeStruct(v.shape, v.dtype),
        in_specs=[pl.BlockSpec(memory_space=pl.ANY)], out_specs=pl.BlockSpec(memory_space=pl.ANY),
        scratch_shapes=[pltpu.SemaphoreType.DMA((3,)), pltpu.SemaphoreType.DMA((3,)), pltpu.SemaphoreType.DMA],
        compiler_params=_params(),
    )(v)


def _layout(sizes, width, part_mult, total_mult):
    offs, rows, r = [], [], 0
    for n in sizes:
        k = -(-n // width)
        offs.append(r)
        rows.append(k)
        r += -(-k // part_mult) * part_mult
    return offs, rows, -(-r // total_mult) * total_mult


def _pack(arrs, width, part_mult, total_mult, dtype, lead=()):
    nl = len(lead)
    sizes = [math.prod(a.shape[nl:]) for a in arrs]
    offs, rows, total = _layout(sizes, width, part_mult, total_mult)
    parts, r = [], 0
    for a, n, o, k in zip(arrs, sizes, offs, rows):
        if o > r:
            parts.append(jnp.zeros((*lead, o - r, width), dtype))
        flat = a.reshape(*lead, n).astype(dtype)
        if k * width > n:
            flat = jnp.concatenate([flat, jnp.zeros((*lead, k * width - n), dtype)], axis=-1)
        parts.append(flat.reshape(*lead, k, width))
        r = o + k
    if total > r:
        parts.append(jnp.zeros((*lead, total - r, width), dtype))
    return jnp.concatenate(parts, axis=nl)


def _unpack(pool, shapes, width, part_mult, total_mult):
    lead = pool.shape[:-2]
    sizes = [math.prod(s) for s in shapes]
    offs, rows, _ = _layout(sizes, width, part_mult, total_mult)
    out = []
    for s, n, o, k in zip(shapes, sizes, offs, rows):
        flat = lax.slice_in_dim(pool, o, o + k, axis=len(lead)).reshape(*lead, k * width)
        out.append(lax.slice_in_dim(flat, 0, n, axis=len(lead)).reshape(*lead, *s))
    return out


_WEIGHTS = ("c_ctx", "w_ada", "b_ada", "g_pre_mix", "g_post_mix", "g_pre_ffn", "g_post_ffn", "w_in", "b_merge",
            "dn_conv", "dn_a_log", "dn_dt_bias", "dn_onorm", "lru_conv", "lru_conv_b", "lru_w_rg", "lru_b_rg",
            "lru_w_ig", "lru_b_ig", "lru_lambda", "w_branch_dn", "w_branch_lru", "w_out", "w_up", "ffn_dw",
            "ffn_dw_b", "w_down")
_BIG = {"w_ada": True, "w_in": True, "w_branch_dn": False, "w_branch_lru": False, "w_out": False, "w_up": True,
        "w_down": False}
_SMALL_SHARDED = ("dn_conv", "lru_conv", "lru_b_rg", "lru_b_ig", "lru_lambda", "ffn_dw")
_NCHIP = 4
_POOL_PART, _POOL_TOTAL = 16, 32
_FLAT_PART = 8


def _to_chip_shards(g, by_cols):
    if by_cols:
        return g.reshape(g.shape[0], _NCHIP, g.shape[1] // _NCHIP).transpose(1, 0, 2)
    return g.reshape(_NCHIP, g.shape[0] // _NCHIP, g.shape[1])


def _from_chip_shards(s, by_cols):
    if by_cols:
        return s.transpose(1, 0, 2).reshape(s.shape[1], _NCHIP * s.shape[2])
    return s.reshape(_NCHIP * s.shape[1], s.shape[2])


def _dsilu(x):
    s = _sigmoid(x)
    return s * (1.0 + x * (1.0 - s))


def kernel(x, c, ctx, c_ctx, w_ada, b_ada, g_pre_mix, g_post_mix, g_pre_ffn, g_post_ffn, w_in, b_merge, dn_conv, dn_a_log, dn_dt_bias, dn_onorm, lru_conv, lru_conv_b, lru_w_rg, lru_b_rg, lru_w_ig, lru_b_ig, lru_lambda, w_branch_dn, w_branch_lru, w_out, w_up, ffn_dw, ffn_dw_b, w_down, loss_target, m_c_ctx, m_w_ada, m_b_ada, m_g_pre_mix, m_g_post_mix, m_g_pre_ffn, m_g_post_ffn, m_w_in, m_b_merge, m_dn_conv, m_dn_a_log, m_dn_dt_bias, m_dn_onorm, m_lru_conv, m_lru_conv_b, m_lru_w_rg, m_lru_b_rg, m_lru_w_ig, m_lru_b_ig, m_lru_lambda, m_w_branch_dn, m_w_branch_lru, m_w_out, m_w_up, m_ffn_dw, m_ffn_dw_b, m_w_down, v_c_ctx, v_w_ada, v_b_ada, v_g_pre_mix, v_g_post_mix, v_g_pre_ffn, v_g_post_ffn, v_w_in, v_b_merge, v_dn_conv, v_dn_a_log, v_dn_dt_bias, v_dn_onorm, v_lru_conv, v_lru_conv_b, v_lru_w_rg, v_lru_b_rg, v_lru_w_ig, v_lru_b_ig, v_lru_lambda, v_w_branch_dn, v_w_branch_lru, v_w_out, v_w_up, v_ffn_dw, v_ffn_dw_b, v_w_down):
    W = dict(zip(_WEIGHTS, (c_ctx, w_ada, b_ada, g_pre_mix, g_post_mix, g_pre_ffn, g_post_ffn, w_in, b_merge, dn_conv,
                            dn_a_log, dn_dt_bias, dn_onorm, lru_conv, lru_conv_b, lru_w_rg, lru_b_rg, lru_w_ig, lru_b_ig,
                            lru_lambda, w_branch_dn, w_branch_lru, w_out, w_up, ffn_dw, ffn_dw_b, w_down)))
    Mo = dict(zip(_WEIGHTS, (m_c_ctx, m_w_ada, m_b_ada, m_g_pre_mix, m_g_post_mix, m_g_pre_ffn, m_g_post_ffn, m_w_in,
                             m_b_merge, m_dn_conv, m_dn_a_log, m_dn_dt_bias, m_dn_onorm, m_lru_conv, m_lru_conv_b,
                             m_lru_w_rg, m_lru_b_rg, m_lru_w_ig, m_lru_b_ig, m_lru_lambda, m_w_branch_dn,
                             m_w_branch_lru, m_w_out, m_w_up, m_ffn_dw, m_ffn_dw_b, m_w_down)))
    Vo = dict(zip(_WEIGHTS, (v_c_ctx, v_w_ada, v_b_ada, v_g_pre_mix, v_g_post_mix, v_g_pre_ffn, v_g_post_ffn, v_w_in,
                             v_b_merge, v_dn_conv, v_dn_a_log, v_dn_dt_bias, v_dn_onorm, v_lru_conv, v_lru_conv_b,
                             v_lru_w_rg, v_lru_b_rg, v_lru_w_ig, v_lru_b_ig, v_lru_lambda, v_w_branch_dn,
                             v_w_branch_lru, v_w_out, v_w_up, v_ffn_dw, v_ffn_dw_b, v_w_down)))
    B, N, D = x.shape
    NC = ctx.shape[1]
    T = NC + N
    H, HD = dn_a_log.shape[-1], dn_onorm.shape[-1]
    DNW = H * HD
    LW, LBD = lru_conv_b.shape[-1], lru_w_rg.shape[-1]
    DFF = ffn_dw_b.shape[-1]
    LC = LANES
    x_i, y_i, c_i = _me()
    s_me = 2 * x_i + y_i
    tm = _tile(math.gcd(NC, N), 256, 16)

    big_local = [W[n][0] for n in _BIG]
    big_shapes = [a.shape for a in big_local]
    pool = _pack(big_local, D, _POOL_PART, _POOL_TOTAL, BF16)
    gathered = _allgather_big(pool)
    gathered = gathered.reshape(_NCHIP, pool.shape[0], D)
    full = {n: _from_chip_shards(s, _BIG[n])
            for n, s in zip(_BIG, _unpack(gathered, big_shapes, D, _POOL_PART, _POOL_TOTAL))}

    small_local = [W[n][0].reshape(-1, W[n].shape[-1]) for n in _SMALL_SHARDED]
    small_shapes = [a.shape for a in small_local]
    spack = _pack(small_local, LANES, _FLAT_PART, _FLAT_PART, F32)
    sgath = _allgather_small(spack)[0::2]
    sfull = {n: _from_chip_shards(s, True)
             for n, s in zip(_SMALL_SHARDED, _unpack(sgath, small_shapes, LANES, _FLAT_PART, _FLAT_PART))}

    o_a = 4 * DNW
    o_xl = o_a + 4 * H
    o_mg = o_xl + 2 * LW
    wi_ = full["w_in"]
    nj = LW // LC
    lru_cols = jnp.stack([wi_[:, o_xl:o_xl + LW].reshape(D, nj, LC), wi_[:, o_xl + LW:o_mg].reshape(D, nj, LC)],
                         axis=2).reshape(D, 2 * LW)
    wp = jnp.concatenate([wi_[:, :o_a], lru_cols, wi_[:, o_mg:], wi_[:, o_a:o_xl],
                          jnp.zeros((D, LANES - 4 * H), BF16)], axis=1)
    p_lru, p_mg, p_ab = 4 * DNW, 4 * DNW + 2 * LW, 4 * DNW + 2 * LW + 2 * D
    PW = p_ab + LANES

    MR = LANES
    cond = jnp.concatenate([c, c_ctx[None], jnp.zeros((MR - B - 1, D), F32)], axis=0)
    silu_rows = _rowwise(lambda a: (_silu(a),), [cond], [F32], name="cond_silu")[0]
    mod = _matmul(silu_rows, full["w_ada"], name="ada_fwd") + b_ada
    mx = mod[:B].reshape(B, 6, D)
    mc = mod[B].reshape(6, D)
    zero = jnp.zeros((B, D), F32)
    tab = jnp.stack([jnp.stack([jnp.broadcast_to(mc[0], (B, D)), jnp.broadcast_to(mc[1], (B, D))] + [zero] * 6, axis=1),
                     jnp.stack([mx[:, 0], mx[:, 1]] + [zero] * 6, axis=1)], axis=1)
    vecs = jnp.stack([mx[:, 2], mx[:, 3], mx[:, 4], mx[:, 5]] + [zero] * 4, axis=1)
    gains = jnp.concatenate([g_post_mix, g_pre_ffn, g_post_ffn, jnp.zeros((5, D), F32)], axis=0)

    h = jnp.concatenate([ctx, x], axis=1)
    u = _premix_fwd(h, g_pre_mix, tab, nc=NC, tm=tm)
    p = _matmul(u, wp, name="in_fwd")
    dkw = dict(B=B, T=T, nc=NC, H=H, HD=HD)
    qkv = _dnprep_fwd(p, sfull["dn_conv"], **dkw)
    prm = jnp.concatenate([
        jnp.concatenate([dn_a_log.reshape(1, 2 * H), jnp.zeros((1, LANES - 2 * H), F32)], axis=1),
        jnp.concatenate([dn_dt_bias.reshape(1, 2 * H), jnp.zeros((1, LANES - 2 * H), F32)], axis=1),
        jnp.zeros((6, LANES), F32)], axis=0)
    gtm = _tile(B * T, 512, 16)
    gb = _gb_fwd(p, prm, rows=B * T, col0=p_ab, H=H, tm=gtm)
    y_dn, o_dn = _delta_fwd(qkv, gb, p, dn_onorm, **dkw)
    lv = jnp.concatenate([lru_conv_b, sfull["lru_b_rg"], sfull["lru_b_ig"], sfull["lru_lambda"], jnp.zeros((1, LW), F32)], axis=0)
    wr = _blockdiag(lru_w_rg[0], LC).astype(BF16)
    wi = _blockdiag(lru_w_ig[0], LC).astype(BF16)
    lkw = dict(B=B, T=T, nc=NC, LW=LW, col0=p_lru, C=LC)
    y_lru = _lru_fwd(p, sfull["lru_conv"], lv, wr, wi, **lkw)
    Ydn = _matmul(y_dn, full["w_branch_dn"], name="bdn_fwd")
    Ylru = _matmul(y_lru, full["w_branch_lru"], name="blru_fwd")
    mkw = dict(B=B, T=T, nc=NC, D=D, col0=p_mg, tm=tm)
    mixin = _merge_fwd(p, Ydn, Ylru, b_merge, **mkw)
    mix = _matmul(mixin, full["w_out"], name="out_fwd")
    h1, u2 = _post_fwd(x, mix, gains, vecs, tm=tm)
    F = _matmul(u2, full["w_up"], name="up_fwd")
    w9 = sfull["ffn_dw"]
    ftc = _tile(DFF, 256)
    f = _ffn_act_fwd(F, w9, ffn_dw_b, B=B, N=N, DFF=DFF, tc=ftc)
    dn = _matmul(f, full["w_down"], name="down_fwd")
    ddn, dout, sums_f = _final(h1, dn, loss_target, gains, vecs, tm=tm)

    G = {}
    df = _matmul(ddn, full["w_down"], tb=True, name="down_bwd_x")
    G["w_down"] = _matmul(f, ddn, ta=True, name="down_bwd_w")
    dFg, dFv, dwb = _ffn_act_bwd(F, w9, ffn_dw_b, df, B=B, N=N, DFF=DFF, tc=ftc)
    du2 = _matmul(dFg, full["w_up"][:, :DFF], tb=True, name="up_bwd_xg")
    du2 = _matmul(dFv, full["w_up"][:, DFF:], tb=True, add=du2, name="up_bwd_xv")
    G["w_up"] = jnp.concatenate([_matmul(u2, dFg, ta=True, name="up_bwd_wg"), _matmul(u2, dFv, ta=True, name="up_bwd_wv")], axis=1)
    dx1, dmix, sums_p = _post_bwd(x, mix, gains, vecs, dout, du2, tm=tm)
    dmixin = _matmul(dmix, full["w_out"], tb=True, name="out_bwd_x")
    G["w_out"] = _matmul(mixin, dmix, ta=True, name="out_bwd_w")
    dp = jnp.zeros((B * T, PW), BF16)
    dYdn, dYlru, dp, sums_m = _merge_bwd(p, Ydn, Ylru, b_merge, dmixin, dp, **mkw)
    dy_dn = _matmul(dYdn, full["w_branch_dn"], tb=True, name="bdn_bwd_x")
    G["w_branch_dn"] = _matmul(y_dn, dYdn, ta=True, name="bdn_bwd_w")
    dy_lru = _matmul(dYlru, full["w_branch_lru"], tb=True, name="blru_bwd_x")
    G["w_branch_lru"] = _matmul(y_lru, dYlru, ta=True, name="blru_bwd_w")
    dp, dcw_l, dlv, dwr, dwi = _lru_bwd(p, sfull["lru_conv"], lv, wr, wi, dy_lru, dp, **lkw)
    dqkv, dgb, dp, don = _delta_bwd(qkv, gb, p, dn_onorm, o_dn, dy_dn, dp, **dkw)
    dp, dprm = _gb_bwd(p, prm, dgb, dp, rows=B * T, col0=p_ab, H=H, tm=gtm)
    dp, dcw_d = _dnprep_bwd(p, sfull["dn_conv"], dqkv, dp, **dkw)
    dU = _matmul(dp, wp, tb=True, name="in_bwd_x")
    dwp = _matmul(u, dp, ta=True, name="in_bwd_w")
    grad_x, sums_pm = _premix_bwd(h, g_pre_mix, tab, dU, dx1, nc=NC, tm=tm)
    dlru = dwp[:, p_lru:p_mg].reshape(D, nj, 2, LC)
    G["w_in"] = jnp.concatenate([dwp[:, :o_a], dwp[:, p_ab:p_ab + 4 * H], dlru[:, :, 0].reshape(D, LW),
                                 dlru[:, :, 1].reshape(D, LW), dwp[:, p_mg:p_ab]], axis=1)

    dmod_x = jnp.stack([sums_pm[:, 1, 0], sums_pm[:, 1, 1], sums_p[:, 0], sums_p[:, 1], sums_p[:, 2], sums_f[:, 0]],
                       axis=1).reshape(B, 6 * D)
    dmod_c = jnp.concatenate([sums_pm[:, 0, 0].sum(0), sums_pm[:, 0, 1].sum(0), jnp.zeros((4 * D,), F32)])[None]
    dmod = jnp.concatenate([dmod_x, dmod_c, jnp.zeros((MR - B - 1, 6 * D), F32)], axis=0)
    G["w_ada"] = _matmul(silu_rows, dmod, ta=True, name="ada_bwd_w")
    dsilu = _matmul(dmod, full["w_ada"], tb=True, name="ada_bwd_x")

    g_small = {
        "c_ctx": dsilu[B] * _dsilu(c_ctx),
        "b_ada": dmod[:B + 1].sum(0)[None],
        "g_pre_mix": sums_pm[:, :, 2].sum((0, 1))[None],
        "g_post_mix": sums_p[:, 3].sum(0)[None],
        "g_pre_ffn": sums_p[:, 4].sum(0)[None],
        "g_post_ffn": sums_f[:, 1].sum(0)[None],
        "b_merge": sums_m[0:1],
        "dn_conv": dcw_d[0:4][None],
        "dn_a_log": dprm[0, :2 * H].reshape(1, 2, H),
        "dn_dt_bias": dprm[1, :2 * H].reshape(1, 2, H),
        "dn_onorm": don[:, 0].sum(0)[None],
        "lru_conv": dcw_l[0:4][None],
        "lru_conv_b": dlv[0:1],
        "lru_w_rg": _blockdiag_extract(dwr, LBD)[None],
        "lru_b_rg": dlv[1:3][None],
        "lru_w_ig": _blockdiag_extract(dwi, LBD)[None],
        "lru_b_ig": dlv[3:5][None],
        "lru_lambda": dlv[5:7][None],
        "ffn_dw": dwb[0:9].reshape(1, 3, 3, DFF),
        "ffn_dw_b": dwb[9:10],
    }
    small_names = tuple(n for n in _WEIGHTS if n not in _BIG)
    loss_part = sums_f[:, 2].sum().reshape(1)
    gs_list = [g_small[n] for n in small_names] + [loss_part]
    gs_shapes = [a.shape for a in gs_list]
    gpack = _pack(gs_list, LANES, _FLAT_PART, _FLAT_PART, F32)
    gsum = _sum_lead(_allgather_small(gpack), name="small_sum", tm=512, mult=SUBLANES)
    gs_red = dict(zip(small_names + ("loss",), _unpack(gsum, gs_shapes, LANES, _FLAT_PART, _FLAT_PART)))
    loss = gs_red["loss"][0]

    gp = _pack([_to_chip_shards(G[n], _BIG[n]) for n in _BIG], D, _POOL_PART, _POOL_TOTAL, F32, lead=(_NCHIP,))
    R = gp.shape[1]
    PR = R // 2
    gp = gp.reshape(_NCHIP, 2, PR, D)
    own = lax.dynamic_index_in_dim(gp, c_i, axis=1, keepdims=False)
    other = lax.dynamic_index_in_dim(gp, 1 - c_i, axis=1, keepdims=False).astype(BF16)
    from_sibling = _sibling_swap(other, name="rs_sibling")
    chip_sum = _rowwise(lambda a, b: (a + b.astype(F32),), [own.reshape(_NCHIP * PR, D), from_sibling.reshape(_NCHIP * PR, D)],
                        [BF16], name="rs_add")[0].reshape(_NCHIP, PR, D)
    half = _sum_lead(_chip_exchange(chip_sum), name="rs_sum")
    sib_half = _sibling_swap(half, name="rs_gather")
    red_pool = jnp.where(c_i == 0, jnp.concatenate([half, sib_half], axis=0), jnp.concatenate([sib_half, half], axis=0))
    g_big = dict(zip(_BIG, _unpack(red_pool, big_shapes, D, _POOL_PART, _POOL_TOTAL)))

    grads, deltas, new_m, new_v = {}, {}, {}, {}
    for n in _BIG:
        shp = W[n].shape
        grads[n] = g_big[n].reshape(shp)
        d_, m_, v_ = _adamw(W[n][0], g_big[n], Mo[n][0], Vo[n][0], name="adamw_" + n)
        deltas[n], new_m[n], new_v[n] = d_.reshape(shp), m_.reshape(shp), v_.reshape(shp)
    for n in small_names:
        g = gs_red[n]
        if n in _SMALL_SHARDED:
            k = W[n].shape[-1]
            g = lax.dynamic_slice_in_dim(g, s_me * k, k, axis=g.ndim - 1)
        grads[n] = g.reshape(W[n].shape)
    sm_shapes = [W[n].shape for n in small_names]
    pk = lambda d: _pack([d[n] for n in small_names], LANES, _FLAT_PART, _FLAT_PART, F32)
    d_, m_, v_ = _adamw(pk(W), pk(grads), pk(Mo), pk(Vo), name="adamw_small")
    for dst, pool_ in ((deltas, d_), (new_m, m_), (new_v, v_)):
        dst.update(zip(small_names, _unpack(pool_, sm_shapes, LANES, _FLAT_PART, _FLAT_PART)))
    return (loss, grad_x, *[grads[n] for n in _WEIGHTS], *[deltas[n] for n in _WEIGHTS],
            *[new_m[n] for n in _WEIGHTS], *[new_v[n] for n in _WEIGHTS])
```

```python
import functools
import math

import jax
import jax.numpy as jnp
from jax import lax
from jax.experimental import pallas as pl
from jax.experimental.pallas import tpu as pltpu

F32 = jnp.float32
BF16 = jnp.bfloat16
EPS = 1e-6
GRID_W = 64
CHUNK = 64
LRU_C = 8.0
LANES = 128
SUBLANES = 8
VMEM_LIMIT = 56 * 1024 * 1024
ADAM_LR, ADAM_B1, ADAM_B2, ADAM_EPS, ADAM_WD, ADAM_STEP = 0.001, 0.9, 0.999, 1e-08, 0.01, 10
MESH = pl.DeviceIdType.MESH


def _tile(n, target, mult=LANES):
    best = None
    for t in range(mult, min(n, target) + 1, mult):
        if n % t == 0:
            best = t
    return best if best is not None else n


def _params(sem=None, **kw):
    return pltpu.CompilerParams(dimension_semantics=sem, vmem_limit_bytes=VMEM_LIMIT, **kw)


def _sigmoid(x):
    return 1.0 / (1.0 + jnp.exp(-x))


def _silu(x):
    return x * _sigmoid(x)


def _softplus(x):
    return jnp.maximum(x, 0.0) + jnp.log(1.0 + jnp.exp(-jnp.abs(x)))


def _gelu(x):
    return 0.5 * x * (1.0 + jnp.tanh(math.sqrt(2.0 / math.pi) * (x + 0.044715 * x * x * x)))


def _rmsn(u, gain):
    return u * lax.rsqrt(jnp.mean(u * u, axis=-1, keepdims=True) + EPS) * gain


_MM_VMEM = 40 * 1024 * 1024


def _matmul(a, b, *, ta=False, tb=False, add=None, out_dtype=F32, name, tm=1024, tn=2048, tk=1024):
    (K, M) = a.shape if ta else a.shape[::-1]
    N = b.shape[0] if tb else b.shape[1]
    assert (b.shape[1] if tb else b.shape[0]) == K, (a.shape, b.shape, ta, tb)
    tm, tk = _tile(M, tm), _tile(K, tk)
    osz = jnp.dtype(out_dtype).itemsize + (4 if add is not None else 0)
    while True:
        tn_ = _tile(N, tn)
        need = 2 * (tm * tk * a.dtype.itemsize + tk * tn_ * b.dtype.itemsize + tm * tn_ * osz) + 4 * tm * tn_
        if need <= _MM_VMEM or tn <= LANES:
            break
        tn //= 2
    tn = tn_
    nk = K // tk
    dims = (((0 if ta else 1,), (1 if tb else 0,)), ((), ()))

    def body(a_ref, b_ref, *rest):
        (c_ref, o_ref, acc_ref) = rest if add is not None else (None, *rest)
        k = pl.program_id(2)

        @pl.when(k == 0)
        def _():
            acc_ref[...] = jnp.zeros_like(acc_ref) if c_ref is None else c_ref[...]

        acc_ref[...] += lax.dot_general(a_ref[...].astype(BF16), b_ref[...].astype(BF16), dims,
                                        preferred_element_type=F32)

        @pl.when(k == nk - 1)
        def _():
            o_ref[...] = acc_ref[...].astype(out_dtype)

    a_spec = pl.BlockSpec((tk, tm), lambda i, j, k: (k, i)) if ta else pl.BlockSpec((tm, tk), lambda i, j, k: (i, k))
    b_spec = pl.BlockSpec((tn, tk), lambda i, j, k: (j, k)) if tb else pl.BlockSpec((tk, tn), lambda i, j, k: (k, j))
    o_spec = pl.BlockSpec((tm, tn), lambda i, j, k: (i, j))
    return pl.pallas_call(
        body, name=name, grid=(M // tm, N // tn, nk),
        in_specs=[a_spec, b_spec] + ([o_spec] if add is not None else []),
        out_specs=o_spec,
        out_shape=jax.ShapeDtypeStruct((M, N), out_dtype),
        scratch_shapes=[pltpu.VMEM((tm, tn), F32)],
        compiler_params=_params(("parallel", "parallel", "arbitrary")),
    )(*((a, b) + ((add,) if add is not None else ())))


def _premix_math(h, gain, shift, scale):
    return _rmsn(h, gain) * (1.0 + scale) + shift


def _premix_fwd(h, gain, tab, *, nc, tm):
    B, T, D = h.shape
    nt, nct = T // tm, nc // tm

    def body(h_ref, g_ref, tab_ref, u_ref):
        tabv = tab_ref[0, 0]
        u_ref[...] = _premix_math(h_ref[0], g_ref[...], tabv[0:1], tabv[1:2]).astype(BF16)

    return pl.pallas_call(
        body, name="premix_fwd", grid=(B, nt),
        in_specs=[pl.BlockSpec((1, tm, D), lambda b, t: (b, t, 0)),
                  pl.BlockSpec((1, D), lambda b, t: (0, 0)),
                  pl.BlockSpec((1, 1, 8, D), lambda b, t: (b, jnp.where(t < nct, 0, 1), 0, 0))],
        out_specs=pl.BlockSpec((tm, D), lambda b, t: (b * nt + t, 0)),
        out_shape=jax.ShapeDtypeStruct((B * T, D), BF16),
        compiler_params=_params(("parallel", "parallel")),
    )(h, gain, tab)


def _premix_bwd(h, gain, tab, du, dres, *, nc, tm):
    B, T, D = h.shape
    nt, nct = T // tm, nc // tm
    N = T - nc

    def body(h_ref, g_ref, tab_ref, du_ref, dres_ref, dx_ref, sums_ref):
        t = pl.program_id(1)
        tabv = tab_ref[0, 0]
        _, vjp = jax.vjp(_premix_math, h_ref[0], g_ref[...], tabv[0:1], tabv[1:2])
        dh, dgain, dshift, dscale = vjp(du_ref[...].astype(F32))

        @pl.when((t == 0) | (t == nct))
        def _():
            sums_ref[...] = jnp.zeros_like(sums_ref)

        sums_ref[0, 0, 0:1, :] += dshift
        sums_ref[0, 0, 1:2, :] += dscale
        sums_ref[0, 0, 2:3, :] += dgain

        @pl.when(t >= nct)
        def _():
            dx_ref[0] = dres_ref[...] + dh

    lat = lambda b, t: jnp.maximum(t - nct, 0)
    return pl.pallas_call(
        body, name="premix_bwd", grid=(B, nt),
        in_specs=[pl.BlockSpec((1, tm, D), lambda b, t: (b, t, 0)),
                  pl.BlockSpec((1, D), lambda b, t: (0, 0)),
                  pl.BlockSpec((1, 1, 8, D), lambda b, t: (b, jnp.where(t < nct, 0, 1), 0, 0)),
                  pl.BlockSpec((tm, D), lambda b, t: (b * nt + t, 0)),
                  pl.BlockSpec((tm, D), lambda b, t: (b * (nt - nct) + lat(b, t), 0))],
        out_specs=[pl.BlockSpec((1, tm, D), lambda b, t: (b, lat(b, t), 0)),
                   pl.BlockSpec((1, 1, 8, D), lambda b, t: (b, jnp.where(t < nct, 0, 1), 0, 0))],
        out_shape=[jax.ShapeDtypeStruct((B, N, D), F32), jax.ShapeDtypeStruct((B, 2, 8, D), F32)],
        compiler_params=_params(("parallel", "arbitrary")),
    )(h, gain, tab, du, dres)


def _merge_math(mgd, mgl, yd, yl, bd, bl):
    return _sigmoid(mgd + bd) * yd + _sigmoid(mgl + bl) * yl


def _merge_fwd(p, ydn, ylru, b_merge, *, B, T, nc, D, col0, tm):
    N = T - nc
    ntl, nt, nct, cb = N // tm, T // tm, nc // tm, col0 // D

    def body(mgd_ref, mgl_ref, yd_ref, yl_ref, bm_ref, o_ref):
        o_ref[...] = _merge_math(mgd_ref[...], mgl_ref[...], yd_ref[...], yl_ref[...],
                                 bm_ref[:, 0:D], bm_ref[:, D:2 * D]).astype(BF16)

    prow = lambda b, t: b * nt + nct + t
    return pl.pallas_call(
        body, name="merge_fwd", grid=(B, ntl),
        in_specs=[pl.BlockSpec((tm, D), lambda b, t: (prow(b, t), cb)),
                  pl.BlockSpec((tm, D), lambda b, t: (prow(b, t), cb + 1)),
                  pl.BlockSpec((tm, D), lambda b, t: (b * ntl + t, 0)),
                  pl.BlockSpec((tm, D), lambda b, t: (b * ntl + t, 0)),
                  pl.BlockSpec((1, 2 * D), lambda b, t: (0, 0))],
        out_specs=pl.BlockSpec((tm, D), lambda b, t: (b * ntl + t, 0)),
        out_shape=jax.ShapeDtypeStruct((B * N, D), BF16),
        compiler_params=_params(("parallel", "parallel")),
    )(p, p, ydn, ylru, b_merge)


def _merge_bwd(p, ydn, ylru, b_merge, dmix, dp, *, B, T, nc, D, col0, tm):
    N = T - nc
    ntl, nt, nct, cb = N // tm, T // tm, nc // tm, col0 // D
    assert col0 % (2 * D) == 0

    def body(mgd_ref, mgl_ref, yd_ref, yl_ref, bm_ref, dm_ref, dp_any, dyd_ref, dyl_ref, dp_ref, sums_ref):
        _, vjp = jax.vjp(_merge_math, mgd_ref[...], mgl_ref[...], yd_ref[...], yl_ref[...],
                         bm_ref[:, 0:D], bm_ref[:, D:2 * D])
        dmgd, dmgl, dyd, dyl, dbd, dbl = vjp(dm_ref[...])
        dyd_ref[...] = dyd.astype(BF16)
        dyl_ref[...] = dyl.astype(BF16)
        dp_ref[:, 0:D] = dmgd.astype(BF16)
        dp_ref[:, D:2 * D] = dmgl.astype(BF16)

        @pl.when((pl.program_id(0) == 0) & (pl.program_id(1) == 0))
        def _():
            sums_ref[...] = jnp.zeros_like(sums_ref)

        sums_ref[0:1, 0:D] += dbd
        sums_ref[0:1, D:2 * D] += dbl

    prow = lambda b, t: b * nt + nct + t
    row = pl.BlockSpec((tm, D), lambda b, t: (b * ntl + t, 0))
    return pl.pallas_call(
        body, name="merge_bwd", grid=(B, ntl),
        in_specs=[pl.BlockSpec((tm, D), lambda b, t: (prow(b, t), cb)),
                  pl.BlockSpec((tm, D), lambda b, t: (prow(b, t), cb + 1)),
                  row, row, pl.BlockSpec((1, 2 * D), lambda b, t: (0, 0)), row,
                  pl.BlockSpec(memory_space=pl.ANY)],
        out_specs=[row, row,
                   pl.BlockSpec((tm, 2 * D), lambda b, t: (prow(b, t), cb // 2)),
                   pl.BlockSpec((8, 2 * D), lambda b, t: (0, 0))],
        out_shape=[jax.ShapeDtypeStruct((B * N, D), BF16), jax.ShapeDtypeStruct((B * N, D), BF16),
                   jax.ShapeDtypeStruct(dp.shape, dp.dtype), jax.ShapeDtypeStruct((8, 2 * D), F32)],
        input_output_aliases={6: 2},
        compiler_params=_params(("arbitrary", "arbitrary")),
    )(p, p, ydn, ylru, b_merge, dmix, dp)


def _post_math(x, mix, g1, gate, g2, sh, sc):
    h1 = x + _rmsn(mix, g1) * gate
    return h1, _rmsn(h1, g2) * (1.0 + sc) + sh


def _post_fwd(x, mix, gains, vecs, *, tm):
    B, N, D = x.shape
    ntl = N // tm

    def body(x_ref, mix_ref, g_ref, v_ref, h1_ref, u2_ref):
        v = v_ref[0]
        h1, u2 = _post_math(x_ref[0], mix_ref[...], g_ref[0:1], v[0:1], g_ref[1:2], v[1:2], v[2:3])
        h1_ref[...] = h1
        u2_ref[...] = u2.astype(BF16)

    row = pl.BlockSpec((tm, D), lambda b, t: (b * ntl + t, 0))
    return pl.pallas_call(
        body, name="post_fwd", grid=(B, ntl),
        in_specs=[pl.BlockSpec((1, tm, D), lambda b, t: (b, t, 0)), row,
                  pl.BlockSpec((8, D), lambda b, t: (0, 0)), pl.BlockSpec((1, 8, D), lambda b, t: (b, 0, 0))],
        out_specs=[row, row],
        out_shape=[jax.ShapeDtypeStruct((B * N, D), F32), jax.ShapeDtypeStruct((B * N, D), BF16)],
        compiler_params=_params(("parallel", "parallel")),
    )(x, mix, gains, vecs)


def _post_bwd(x, mix, gains, vecs, dh1, du2, *, tm):
    B, N, D = x.shape
    ntl = N // tm

    def body(x_ref, mix_ref, g_ref, v_ref, dh1_ref, du2_ref, dx_ref, dmix_ref, sums_ref):
        v = v_ref[0]
        _, vjp = jax.vjp(_post_math, x_ref[0], mix_ref[...], g_ref[0:1], v[0:1], g_ref[1:2], v[1:2], v[2:3])
        dx, dmix, dg1, dgate, dg2, dsh, dsc = vjp((dh1_ref[...], du2_ref[...]))
        dx_ref[...] = dx
        dmix_ref[...] = dmix.astype(BF16)

        @pl.when(pl.program_id(1) == 0)
        def _():
            sums_ref[...] = jnp.zeros_like(sums_ref)

        sums_ref[0, 0:1, :] += dgate
        sums_ref[0, 1:2, :] += dsh
        sums_ref[0, 2:3, :] += dsc
        sums_ref[0, 3:4, :] += dg1
        sums_ref[0, 4:5, :] += dg2

    row = pl.BlockSpec((tm, D), lambda b, t: (b * ntl + t, 0))
    return pl.pallas_call(
        body, name="post_bwd", grid=(B, ntl),
        in_specs=[pl.BlockSpec((1, tm, D), lambda b, t: (b, t, 0)), row,
                  pl.BlockSpec((8, D), lambda b, t: (0, 0)), pl.BlockSpec((1, 8, D), lambda b, t: (b, 0, 0)), row, row],
        out_specs=[row, row, pl.BlockSpec((1, 8, D), lambda b, t: (b, 0, 0))],
        out_shape=[jax.ShapeDtypeStruct((B * N, D), F32), jax.ShapeDtypeStruct((B * N, D), BF16),
                   jax.ShapeDtypeStruct((B, 8, D), F32)],
        compiler_params=_params(("parallel", "arbitrary")),
    )(x, mix, gains, vecs, dh1, du2)


def _final_math(dn, g4, gate5):
    return _rmsn(dn, g4) * gate5


def _final(h1, dn, target, gains, vecs, *, tm):
    B, N, D = target.shape
    ntl = N // tm

    def body(h1_ref, dn_ref, t_ref, g_ref, v_ref, ddn_ref, dout_ref, sums_ref):
        v = v_ref[0]
        y, vjp = jax.vjp(_final_math, dn_ref[...], g_ref[2:3], v[3:4])
        err = h1_ref[...] + y - t_ref[0]
        dout = err * (1.0 / D)
        ddn, dg4, dgate5 = vjp(dout)
        ddn_ref[...] = ddn.astype(BF16)
        dout_ref[...] = dout

        @pl.when(pl.program_id(1) == 0)
        def _():
            sums_ref[...] = jnp.zeros_like(sums_ref)

        sums_ref[0, 0:1, :] += dgate5
        sums_ref[0, 1:2, :] += dg4
        sums_ref[0, 2:3, :] += jnp.sum(err * err, axis=0, keepdims=True) * (0.5 / D)

    row = pl.BlockSpec((tm, D), lambda b, t: (b * ntl + t, 0))
    return pl.pallas_call(
        body, name="final", grid=(B, ntl),
        in_specs=[row, row, pl.BlockSpec((1, tm, D), lambda b, t: (b, t, 0)),
                  pl.BlockSpec((8, D), lambda b, t: (0, 0)), pl.BlockSpec((1, 8, D), lambda b, t: (b, 0, 0))],
        out_specs=[row, row, pl.BlockSpec((1, 8, D), lambda b, t: (b, 0, 0))],
        out_shape=[jax.ShapeDtypeStruct((B * N, D), BF16), jax.ShapeDtypeStruct((B * N, D), F32),
                   jax.ShapeDtypeStruct((B, 8, D), F32)],
        compiler_params=_params(("parallel", "arbitrary")),
    )(h1, dn, target, gains, vecs)


def _shift(x, s):
    s = s % x.shape[0]
    return x if s == 0 else pltpu.roll(x, s, 0)


def _seg_taps(T, nc, width, pad_left):
    t = lax.broadcasted_iota(jnp.int32, (T, 1), 0)
    pos = jnp.where(t < nc, t, t - nc)
    seg = jnp.where(t < nc, nc, T - nc)
    taps = []
    for k in range(width):
        src = pos + (k - pad_left)
        taps.append((pad_left - k, (src >= 0) & (src < seg)))
    return taps


def _grid_taps(N):
    t = lax.broadcasted_iota(jnp.int32, (N, 1), 0)
    wcol = t % GRID_W
    taps = []
    for dr in (-1, 0, 1):
        for dw in (-1, 0, 1):
            off = dr * GRID_W + dw
            ok = (wcol + dw >= 0) & (wcol + dw < GRID_W) & (t + dr * GRID_W >= 0) & (t + dr * GRID_W < N)
            taps.append((-off, ok))
    return taps


def _conv_fwd(x, w, taps):
    y = jnp.zeros_like(x)
    for k, (s, m) in enumerate(taps):
        y = y + w[k:k + 1] * jnp.where(m, _shift(x, s), 0.0)
    return y


def _conv_bwd(x, w, taps, dy):
    dx = jnp.zeros_like(x)
    dws = []
    for k, (s, m) in enumerate(taps):
        dym = jnp.where(m, dy, 0.0)
        dx = dx + w[k:k + 1] * _shift(dym, -s)
        dws.append(jnp.sum(dym * _shift(x, s), axis=0, keepdims=True))
    return dx, jnp.concatenate(dws, axis=0)


def _ffn_act_fwd(F, w9, bias, *, B, N, DFF, tc):
    nj = DFF // tc

    def body(fg_ref, fv_ref, w_ref, b_ref, o_ref):
        fg = _conv_fwd(fg_ref[...], w_ref[...], _grid_taps(N)) + b_ref[...]
        o_ref[...] = (_gelu(fg) * fv_ref[...]).astype(BF16)

    return pl.pallas_call(
        body, name="ffn_act_fwd", grid=(B, nj),
        in_specs=[pl.BlockSpec((N, tc), lambda b, j: (b, j)), pl.BlockSpec((N, tc), lambda b, j: (b, nj + j)),
                  pl.BlockSpec((9, tc), lambda b, j: (0, j)), pl.BlockSpec((1, tc), lambda b, j: (0, j))],
        out_specs=pl.BlockSpec((N, tc), lambda b, j: (b, j)),
        out_shape=jax.ShapeDtypeStruct((B * N, DFF), BF16),
        compiler_params=_params(("parallel", "parallel")),
    )(F, F, w9, bias)


def _ffn_act_bwd(F, w9, bias, df, *, B, N, DFF, tc):
    nj = DFF // tc

    def body(fg_ref, fv_ref, w_ref, b_ref, df_ref, dfg_ref, dfv_ref, dwb_ref):
        taps = _grid_taps(N)
        x = fg_ref[...]
        fg, vjp = jax.vjp(lambda a: _gelu(a), _conv_fwd(x, w_ref[...], taps) + b_ref[...])
        dfl = df_ref[...]
        dfv_ref[...] = (dfl * fg).astype(BF16)
        (dpre,) = vjp(dfl * fv_ref[...])
        dx, dw = _conv_bwd(x, w_ref[...], taps, dpre)
        dfg_ref[...] = dx.astype(BF16)

        @pl.when(pl.program_id(1) == 0)
        def _():
            dwb_ref[...] = jnp.zeros_like(dwb_ref)

        dwb_ref[0:9, :] += dw
        dwb_ref[9:10, :] += jnp.sum(dpre, axis=0, keepdims=True)

    col = pl.BlockSpec((N, tc), lambda j, b: (b, j))
    return pl.pallas_call(
        body, name="ffn_act_bwd", grid=(nj, B),
        in_specs=[col, pl.BlockSpec((N, tc), lambda j, b: (b, nj + j)),
                  pl.BlockSpec((9, tc), lambda j, b: (0, j)), pl.BlockSpec((1, tc), lambda j, b: (0, j)), col],
        out_specs=[col, col, pl.BlockSpec((16, tc), lambda j, b: (0, j))],
        out_shape=[jax.ShapeDtypeStruct((B * N, DFF), BF16), jax.ShapeDtypeStruct((B * N, DFF), BF16),
                   jax.ShapeDtypeStruct((16, DFF), F32)],
        compiler_params=_params(("parallel", "arbitrary")),
    )(F, F, w9, bias, df)


def _dnprep_math(y, is_qk, scale):
    s = _silu(y)
    n = s * lax.rsqrt(jnp.sum(s * s, axis=-1, keepdims=True) + EPS) * scale
    return jnp.where(is_qk, n, s)


def _dnprep_fwd(p, cw, *, B, T, nc, H, HD):
    def body(x_ref, w_ref, o_ref):
        j = pl.program_id(1)
        y = _conv_fwd(x_ref[...], w_ref[...], _seg_taps(T, nc, 4, 2))
        o_ref[...] = _dnprep_math(y, j < 2 * H, jnp.where(j < H, HD ** -0.5, 1.0))

    return pl.pallas_call(
        body, name="dnprep_fwd", grid=(B, 3 * H),
        in_specs=[pl.BlockSpec((T, HD), lambda b, j: (b, j)), pl.BlockSpec((4, HD), lambda b, j: (0, j))],
        out_specs=pl.BlockSpec((T, HD), lambda b, j: (b, j)),
        out_shape=jax.ShapeDtypeStruct((B * T, 3 * H * HD), F32),
        compiler_params=_params(("parallel", "parallel")),
    )(p, cw)


def _dnprep_bwd(p, cw, dqkv, dp, *, B, T, nc, H, HD):
    def body(x_ref, w_ref, d_ref, dp_any, dp_ref, dcw_ref):
        j = pl.program_id(0)
        taps = _seg_taps(T, nc, 4, 2)
        x = x_ref[...]
        y = _conv_fwd(x, w_ref[...], taps)
        is_qk, scale = j < 2 * H, jnp.where(j < H, HD ** -0.5, 1.0)
        _, vjp = jax.vjp(lambda a: _dnprep_math(a, is_qk, scale), y)
        (dy,) = vjp(d_ref[0])
        dx, dw = _conv_bwd(x, w_ref[...], taps, dy)
        dp_ref[...] = dx.astype(BF16)

        @pl.when(pl.program_id(1) == 0)
        def _():
            dcw_ref[...] = jnp.zeros_like(dcw_ref)

        dcw_ref[0:4, :] += dw

    col = pl.BlockSpec((T, HD), lambda j, b: (b, j))
    return pl.pallas_call(
        body, name="dnprep_bwd", grid=(3 * H, B),
        in_specs=[col, pl.BlockSpec((4, HD), lambda j, b: (0, j)),
                  pl.BlockSpec((1, T, HD), lambda j, b: (j // H, b, j % H)), pl.BlockSpec(memory_space=pl.ANY)],
        out_specs=[col, pl.BlockSpec((8, HD), lambda j, b: (0, j))],
        out_shape=[jax.ShapeDtypeStruct(dp.shape, dp.dtype), jax.ShapeDtypeStruct((8, 3 * H * HD), F32)],
        input_output_aliases={3: 0},
        compiler_params=_params(("parallel", "arbitrary")),
    )(p, cw, dqkv, dp)


def _gb_math(ab, alog, dtb, H):
    lane = lax.broadcasted_iota(jnp.int32, ab.shape, 1)
    g = -jnp.exp(alog) * _softplus(ab + dtb)
    return jnp.where(lane < 2 * H, g, jnp.where(lane < 4 * H, _sigmoid(ab), 0.0))


def _gb_fwd(p, prm, *, rows, col0, H, tm):
    def body(x_ref, prm_ref, o_ref):
        o_ref[...] = _gb_math(x_ref[...], prm_ref[0:1], prm_ref[1:2], H)

    return pl.pallas_call(
        body, name="gb_fwd", grid=(rows // tm,),
        in_specs=[pl.BlockSpec((tm, LANES), lambda t: (t, col0 // LANES)), pl.BlockSpec((8, LANES), lambda t: (0, 0))],
        out_specs=pl.BlockSpec((tm, LANES), lambda t: (t, 0)),
        out_shape=jax.ShapeDtypeStruct((rows, LANES), F32),
        compiler_params=_params(("parallel",)),
    )(p, prm)


def _gb_bwd(p, prm, dgb, dp, *, rows, col0, H, tm):
    def body(x_ref, prm_ref, d_ref, dp_any, dp_ref, dprm_ref):
        _, vjp = jax.vjp(lambda a, b, c: _gb_math(a, b, c, H), x_ref[...], prm_ref[0:1], prm_ref[1:2])
        dab, dalog, ddtb = vjp(d_ref[...])
        dp_ref[...] = dab.astype(BF16)

        @pl.when(pl.program_id(0) == 0)
        def _():
            dprm_ref[...] = jnp.zeros_like(dprm_ref)

        dprm_ref[0:1, :] += dalog
        dprm_ref[1:2, :] += ddtb

    blk = pl.BlockSpec((tm, LANES), lambda t: (t, col0 // LANES))
    return pl.pallas_call(
        body, name="gb_bwd", grid=(rows // tm,),
        in_specs=[blk, pl.BlockSpec((8, LANES), lambda t: (0, 0)), pl.BlockSpec((tm, LANES), lambda t: (t, 0)),
                  pl.BlockSpec(memory_space=pl.ANY)],
        out_specs=[blk, pl.BlockSpec((8, LANES), lambda t: (0, 0))],
        out_shape=[jax.ShapeDtypeStruct(dp.shape, dp.dtype), jax.ShapeDtypeStruct((8, LANES), F32)],
        input_output_aliases={3: 0},
        compiler_params=_params(("arbitrary",)),
    )(p, prm, dgb, dp)


def _lru_scan(a_ref, b_ref, h_ref, hp_ref, segs):
    C = a_ref.shape[1]
    row = lax.broadcasted_iota(jnp.int32, (SUBLANES, C), 0)
    carry = jnp.zeros((1, C), F32)
    for start, rows, reverse in segs:
        nb = rows // SUBLANES

        def blk(i, carry, start=start, nb=nb, reverse=reverse):
            r0 = pl.multiple_of(start + (nb - 1 - i if reverse else i) * SUBLANES, SUBLANES)
            A = a_ref[pl.ds(r0, SUBLANES), :]
            Bv = b_ref[pl.ds(r0, SUBLANES), :]
            for s in (1, 2, 4):
                sh = SUBLANES - s if reverse else s
                m = (row < SUBLANES - s) if reverse else (row >= s)
                Bv = jnp.where(m, A * pltpu.roll(Bv, sh, 0) + Bv, Bv)
                A = jnp.where(m, A * pltpu.roll(A, sh, 0), A)
            Hv = Bv + A * carry
            h_ref[pl.ds(r0, SUBLANES), :] = Hv
            if hp_ref is not None:
                if reverse:
                    hp = jnp.where(row < SUBLANES - 1, pltpu.roll(Hv, SUBLANES - 1, 0), carry)
                else:
                    hp = jnp.where(row >= 1, pltpu.roll(Hv, 1, 0), carry)
                hp_ref[pl.ds(r0, SUBLANES), :] = hp
            return Hv[0:1] if reverse else Hv[SUBLANES - 1:SUBLANES]

        carry = lax.fori_loop(0, nb, blk, carry)


def _lru_orders(T, nc, d):
    if d == 0:
        return [(0, T, False)], [(0, T, True)]
    return [(0, nc, True), (nc, T - nc, True)], [(nc, T - nc, False), (0, nc, False)]


def _bdot(a, b, dims=(((1,), (0,)), ((), ()))):
    return lax.dot_general(a.astype(BF16), b.astype(BF16), dims, preferred_element_type=F32)


_NT = (((1,), (1,)), ((), ()))
_TN = (((0,), (0,)), ((), ()))


def _blockdiag(w, C):
    nd, nb, bd, _ = w.shape
    per = C // bd
    out = jnp.einsum('dnpij,pq->dnpiqj', w.reshape(nd, nb // per, per, bd, bd), jnp.eye(per, dtype=w.dtype))
    return out.reshape(nd, nb // per, C, C)


def _blockdiag_extract(dw, bd):
    nd, nj, C, _ = dw.shape
    per = C // bd
    out = jnp.einsum('dnpiqj,pq->dnpij', dw.reshape(nd, nj, per, bd, per, bd), jnp.eye(per, dtype=dw.dtype))
    return out.reshape(nd, nj * per, bd, bd)


def _lru_fwd(p, cw, lv, wr, wi, *, B, T, nc, LW, col0, C):
    N = T - nc
    nj = LW // C

    def body(x_ref, cw_ref, lv_ref, wr_ref, wi_ref, o_ref, a_s, b_s, h_s, acc_s):
        lv_ = lv_ref[...]
        xc = _conv_fwd(x_ref[:, 0:C], cw_ref[...], _seg_taps(T, nc, 4, 2)) + lv_[0:1]
        for d in (0, 1):
            r = _sigmoid(_bdot(xc, wr_ref[d, 0]) + lv_[1 + d:2 + d])
            i = _sigmoid(_bdot(xc, wi_ref[d, 0]) + lv_[3 + d:4 + d])
            la = -LRU_C * r * _softplus(-lv_[5 + d:6 + d])
            a_s[...] = jnp.exp(la)
            b_s[...] = jnp.sqrt(1.0 - jnp.exp(2.0 * la)) * i * xc
            _lru_scan(a_s, b_s, h_s, None, _lru_orders(T, nc, d)[0])
            if d == 0:
                acc_s[...] = h_s[...]
            else:
                acc_s[...] += h_s[...]
        o_ref[...] = (acc_s[nc:, :] * _gelu(x_ref[nc:, C:2 * C])).astype(BF16)

    return pl.pallas_call(
        body, name="lru_fwd", grid=(B, nj),
        in_specs=[pl.BlockSpec((T, 2 * C), lambda b, j: (b, col0 // (2 * C) + j)),
                  pl.BlockSpec((4, C), lambda b, j: (0, j)), pl.BlockSpec((8, C), lambda b, j: (0, j)),
                  pl.BlockSpec((2, 1, C, C), lambda b, j: (0, j, 0, 0)), pl.BlockSpec((2, 1, C, C), lambda b, j: (0, j, 0, 0))],
        out_specs=pl.BlockSpec((N, C), lambda b, j: (b, j)),
        out_shape=jax.ShapeDtypeStruct((B * N, LW), BF16),
        scratch_shapes=[pltpu.VMEM((T, C), F32)] * 4,
        compiler_params=_params(("parallel", "parallel")),
    )(p, cw, lv, wr, wi)


def _lru_bwd(p, cw, lv, wr, wi, dy, dp, *, B, T, nc, LW, col0, C):
    N = T - nc
    nj = LW // C

    def body(x_ref, cw_ref, lv_ref, wr_ref, wi_ref, dy_ref, dp_any, dp_ref, dcw_ref, dlv_ref, dwr_ref, dwi_ref,
             a_s, b_s, h_s, hp_s, mu_s, mup_s, dh_s, dxc_s, hsum_s):
        taps = _seg_taps(T, nc, 4, 2)
        lv_ = lv_ref[...]
        xl = x_ref[:, 0:C]
        xc = _conv_fwd(xl, cw_ref[...], taps) + lv_[0:1]
        gel, gelu_vjp = jax.vjp(_gelu, x_ref[nc:, C:2 * C])
        dh_s[0:nc, :] = jnp.zeros((nc, C), F32)
        dh_s[nc:, :] = dy_ref[...] * gel
        dxc_s[...] = jnp.zeros_like(dxc_s)

        @pl.when(pl.program_id(1) == 0)
        def _():
            dcw_ref[...] = jnp.zeros_like(dcw_ref)
            dlv_ref[...] = jnp.zeros_like(dlv_ref)
            dwr_ref[...] = jnp.zeros_like(dwr_ref)
            dwi_ref[...] = jnp.zeros_like(dwi_ref)

        for d in (0, 1):
            fwd_order, adj_order = _lru_orders(T, nc, d)
            lam = lv_[5 + d:6 + d]
            r = _sigmoid(_bdot(xc, wr_ref[d, 0]) + lv_[1 + d:2 + d])
            i = _sigmoid(_bdot(xc, wi_ref[d, 0]) + lv_[3 + d:4 + d])
            sp = _softplus(-lam)
            la = -LRU_C * r * sp
            a = jnp.exp(la)
            e2 = jnp.exp(2.0 * la)
            mult = jnp.sqrt(1.0 - e2)
            a_s[...] = a
            b_s[...] = mult * i * xc
            _lru_scan(a_s, b_s, h_s, hp_s, fwd_order)
            if d == 0:
                hsum_s[...] = h_s[...]
            else:
                hsum_s[...] += h_s[...]
            b_s[...] = a * dh_s[...]
            _lru_scan(a_s, b_s, mu_s, mup_s, adj_order)
            dinp = dh_s[...] + mup_s[...]
            da = dinp * hp_s[...]
            dmult = dinp * i * xc
            di = dinp * mult * xc
            dla = da * a - dmult * e2 / mult
            dpre_r = (dla * (-LRU_C * sp)) * r * (1.0 - r)
            dpre_i = di * i * (1.0 - i)
            dsp = jnp.sum(dla * (-LRU_C * r), axis=0, keepdims=True)
            dxc_s[...] += dinp * mult * i + _bdot(dpre_r, wr_ref[d, 0], _NT) + _bdot(dpre_i, wi_ref[d, 0], _NT)
            dwr_ref[d, 0] += _bdot(xc, dpre_r, _TN)
            dwi_ref[d, 0] += _bdot(xc, dpre_i, _TN)
            dlv_ref[1 + d:2 + d, :] += jnp.sum(dpre_r, axis=0, keepdims=True)
            dlv_ref[3 + d:4 + d, :] += jnp.sum(dpre_i, axis=0, keepdims=True)
            dlv_ref[5 + d:6 + d, :] += -dsp * _sigmoid(-lam)

        dxc = dxc_s[...]
        dxl, dw = _conv_bwd(xl, cw_ref[...], taps, dxc)
        dcw_ref[0:4, :] += dw
        dlv_ref[0:1, :] += jnp.sum(dxc, axis=0, keepdims=True)
        dp_ref[:, 0:C] = dxl.astype(BF16)
        (dyl,) = gelu_vjp(dy_ref[...] * hsum_s[nc:, :])
        dp_ref[0:nc, C:2 * C] = jnp.zeros((nc, C), BF16)
        dp_ref[nc:, C:2 * C] = dyl.astype(BF16)

    xblk = pl.BlockSpec((T, 2 * C), lambda j, b: (b, col0 // (2 * C) + j))
    wblk = pl.BlockSpec((2, 1, C, C), lambda j, b: (0, j, 0, 0))
    vblk = pl.BlockSpec((8, C), lambda j, b: (0, j))
    return pl.pallas_call(
        body, name="lru_bwd", grid=(nj, B),
        in_specs=[xblk, pl.BlockSpec((4, C), lambda j, b: (0, j)), vblk, wblk, wblk,
                  pl.BlockSpec((N, C), lambda j, b: (b, j)), pl.BlockSpec(memory_space=pl.ANY)],
        out_specs=[xblk, vblk, vblk, wblk, wblk],
        out_shape=[jax.ShapeDtypeStruct(dp.shape, dp.dtype), jax.ShapeDtypeStruct((8, LW), F32),
                   jax.ShapeDtypeStruct((8, LW), F32), jax.ShapeDtypeStruct((2, nj, C, C), F32),
                   jax.ShapeDtypeStruct((2, nj, C, C), F32)],
        scratch_shapes=[pltpu.VMEM((T, C), F32)] * 9,
        input_output_aliases={6: 0},
        compiler_params=_params(("parallel", "arbitrary")),
    )(p, cw, lv, wr, wi, dy, dp)


def _chunk_masks(upper):
    i = lax.broadcasted_iota(jnp.int32, (CHUNK, CHUNK), 0)
    j = lax.broadcasted_iota(jnp.int32, (CHUNK, CHUNK), 1)
    return i == j, (j >= i) if upper else (j <= i), (j > i) if upper else (j < i)


def _col2row(c, eye):
    return jnp.sum(jnp.where(eye, c, 0.0), axis=0, keepdims=True)


def _row2col(r, eye):
    return jnp.sum(jnp.where(eye, r, 0.0), axis=1, keepdims=True)


def _rowsum(x):
    return jnp.sum(x, axis=1, keepdims=True)


def _unit_tri_inverses(Ls):
    G = len(Ls)
    W = G * CHUNK
    blk = (lax.broadcasted_iota(jnp.int32, (W, W), 0) // CHUNK) == (lax.broadcasted_iota(jnp.int32, (W, W), 1) // CHUNK)

    def pdot(a, b):
        bd = jnp.where(blk, jnp.tile(b.astype(BF16), (G, 1)), jnp.zeros((), BF16))
        return jnp.dot(a.astype(BF16), bd, preferred_element_type=F32)

    Xp = -(Ls[0] if G == 1 else jnp.concatenate(Ls, axis=1))
    Rm = Xp
    for _ in range(int(math.log2(CHUNK)) - 1):
        Xp = pdot(Xp, Xp)
        Rm = Rm + Xp + pdot(Rm, Xp)
    eye = _chunk_masks(False)[0]
    return [jnp.where(eye, 1.0, 0.0) + Rm[:, g * CHUNK:(g + 1) * CHUNK] for g in range(G)]


def _delta_chunk_common(q, k, v, gcol, bcol, upper):
    eye, incl, strict = _chunk_masks(upper)
    gc = _rowsum(jnp.where(incl, _col2row(gcol, eye), 0.0))
    D = jnp.where(incl, jnp.exp(jnp.minimum(gc - _col2row(gc, eye), 0.0)), 0.0)
    kb = k * bcol
    AP = _bdot(jnp.concatenate([kb, q], axis=0), k, _NT)
    A = AP[:CHUNK]
    L = jnp.where(strict, A * D, 0.0)
    eg = jnp.exp(gc)
    gl = jnp.sum(gcol, axis=0, keepdims=True)
    attn = jnp.where(incl, AP[CHUNK:] * D, 0.0)
    return dict(eye=eye, incl=incl, strict=strict, gc=gc, D=D, kb=kb, A=A, L=L, eg=eg, gl=gl, egl=jnp.exp(gl),
                attn=attn, kbe=kb * eg, vb=v * bcol, qe=q * eg, kd=k * jnp.exp(gl - gc))


def _delta_group_pre(chunks, upper):
    cs = [_delta_chunk_common(*ch, upper) for ch in chunks]
    out = []
    for c, Tm in zip(cs, _unit_tri_inverses([c["L"] for c in cs])):
        dk = c["kbe"].shape[1]
        wu = _bdot(Tm, jnp.concatenate([c["kbe"], c["vb"]], axis=1))
        KN = _bdot(c["kd"], wu, _TN)
        QO = _bdot(c["attn"], wu)
        out.append((Tm, KN[:, :dk], KN[:, dk:], c["qe"] - QO[:, :dk], QO[:, dk:], c["egl"]))
    return out


def _delta_chunk_bwd(q, k, v, gcol, bcol, S, Tm, do, dS2, upper):
    c = _delta_chunk_common(q, k, v, gcol, bcol, upper)
    eye, incl, strict, D, eg, egl = c["eye"], c["incl"], c["strict"], c["D"], c["eg"], c["egl"]
    kb, kbe, vb, qe, kd, attn = c["kb"], c["kbe"], c["vb"], c["qe"], c["kd"], c["attn"]
    dkk = kbe.shape[1]
    wu = _bdot(Tm, jnp.concatenate([kbe, vb], axis=1))
    w = wu[:, :dkk]
    vn = wu[:, dkk:] - _bdot(w, S)
    dvn = _bdot(kd, dS2) + _bdot(attn, do, _TN)
    dkd = _bdot(vn, dS2, _NT)
    dgl = jnp.sum(_rowsum(dS2 * S), axis=0, keepdims=True) * egl
    dqa = _bdot(do, jnp.concatenate([S, vn], axis=0), _NT)
    dqe = dqa[:, :dkk]
    dattn = jnp.where(incl, dqa[:, dkk:], 0.0)
    dw = -_bdot(dvn, S, _NT)
    r = _rowsum(dkd * kd)
    dk = dkd * jnp.exp(c["gl"] - c["gc"])
    dgl = dgl + jnp.sum(r, axis=0, keepdims=True)
    dgc = _rowsum(dqe * qe) - r
    dq = dqe * eg + _bdot(dattn * D, k)
    dk = dk + _bdot(dattn * D, q, _TN)
    E = dattn * attn
    dvw = jnp.concatenate([dvn, dw], axis=1)
    dTm = _bdot(dvw, jnp.concatenate([vb, kbe], axis=1), _NT)
    dvk = _bdot(Tm, dvw, _TN)
    dvb = dvk[:, :dvn.shape[1]]
    dv = dvb * bcol
    dbeta = _rowsum(dvb * v)
    dkbe = dvk[:, dvn.shape[1]:]
    dkb = dkbe * eg
    dgc = dgc + _rowsum(dkbe * kbe)
    dL = jnp.where(strict, -_bdot(Tm, _bdot(dTm, Tm, _NT), _TN), 0.0)
    dA = dL * D
    E = E + dL * c["L"]
    dkb = dkb + _bdot(dA, k)
    dk = dk + _bdot(dA, kb, _TN) + dkb * bcol
    dbeta = dbeta + _rowsum(dkb * k)
    dgc = dgc + _rowsum(E) - _row2col(jnp.sum(E, axis=0, keepdims=True), eye)
    dg = _row2col(jnp.sum(jnp.where(incl, dgc, 0.0), axis=0, keepdims=True), eye) + dgl
    return dq, dk, dv, dg, dbeta


def _delta_group(n):
    return max(g for g in (4, 3, 2, 1) if n % g == 0)


def _delta_chunk_at(T, nc, d, i):
    n, ncc = T // CHUNK, nc // CHUNK
    return i if d == 0 else jnp.where(i < ncc, ncc - 1 - i, n - 1 - (i - ncc))


def _dn_out_math(o, onorm, z):
    return _rmsn(o, onorm) * _silu(z)


def _delta_fwd(qkv, gb, p, onorm, *, B, T, nc, H, HD):
    N = T - nc
    n = T // CHUNK
    G = _delta_group(n)

    def body(q_ref, k_ref, v_ref, gb_ref, z_ref, on_ref, y_ref, o_ref, K_s, N_s, Qp_s, O0_s, eg_s, o_s):
        h = pl.program_id(1)
        lane = lax.broadcasted_iota(jnp.int32, (CHUNK, LANES), 1)

        def pre(g, carry):
            cs = [g * G + i for i in range(G)]
            rows = [pl.ds(pl.multiple_of(c * CHUNK, CHUNK), CHUNK) for c in cs]
            for d in (0, 1):
                chunks = []
                for r in rows:
                    gbb = gb_ref[r, :]
                    chunks.append((q_ref[r, :], k_ref[r, :], v_ref[r, :],
                                   _rowsum(jnp.where(lane == d * H + h, gbb, 0.0)),
                                   _rowsum(jnp.where(lane == 2 * H + d * H + h, gbb, 0.0))))
                for c, r, (_, K, Nn, Qp, O0, egl) in zip(cs, rows, _delta_group_pre(chunks, d == 1)):
                    K_s[d * n + c] = K.astype(BF16)
                    N_s[d * n + c] = Nn
                    Qp_s[d, r, :] = Qp.astype(BF16)
                    O0_s[d, r, :] = O0
                    eg_s[d * n + c] = jnp.broadcast_to(egl, (SUBLANES, HD))
            return carry

        lax.fori_loop(0, n // G, pre, 0)

        def step(i, Ss):
            out = []
            for d in (0, 1):
                c = _delta_chunk_at(T, nc, d, i)
                rows = pl.ds(pl.multiple_of(c * CHUNK, CHUNK), CHUNK)
                Sb = Ss[d].astype(BF16)
                o_s[d, rows, :] = jnp.dot(Qp_s[d, rows, :], Sb, preferred_element_type=F32) + O0_s[d, rows, :]
                out.append(eg_s[d * n + c][0:1] * Ss[d] + N_s[d * n + c]
                           - jnp.dot(K_s[d * n + c], Sb, preferred_element_type=F32))
            return tuple(out)

        lax.fori_loop(0, n, step, (jnp.zeros((HD, HD), F32), jnp.zeros((HD, HD), F32)))
        o = o_s[0, nc:, :] + o_s[1, nc:, :]
        o_ref[...] = o
        y_ref[...] = _dn_out_math(o, on_ref[...], z_ref[nc:, :]).astype(BF16)

    col = lambda off: pl.BlockSpec((T, HD), lambda b, h: (b, off + h))
    lat = pl.BlockSpec((N, HD), lambda b, h: (b, h))
    return pl.pallas_call(
        body, name="delta_fwd", grid=(B, H),
        in_specs=[col(0), col(H), col(2 * H), pl.BlockSpec((T, LANES), lambda b, h: (b, 0)), col(3 * H),
                  pl.BlockSpec((1, HD), lambda b, h: (0, 0))],
        out_specs=[lat, lat],
        out_shape=[jax.ShapeDtypeStruct((B * N, H * HD), BF16), jax.ShapeDtypeStruct((B * N, H * HD), F32)],
        scratch_shapes=[pltpu.VMEM((2 * n, HD, HD), BF16), pltpu.VMEM((2 * n, HD, HD), F32),
                        pltpu.VMEM((2, T, HD), BF16), pltpu.VMEM((2, T, HD), F32),
                        pltpu.VMEM((2 * n, SUBLANES, HD), F32), pltpu.VMEM((2, T, HD), F32)],
        compiler_params=_params(("parallel", "parallel")),
    )(qkv, qkv, qkv, gb, p, onorm)


def _delta_bwd(qkv, gb, p, onorm, o, dy, dp, *, B, T, nc, H, HD):
    N = T - nc
    n = T // CHUNK
    G = _delta_group(n)

    def body(q_ref, k_ref, v_ref, gb_ref, z_ref, on_ref, o_ref, dy_ref, dp_any, dqkv_ref, dgb_ref, dp_ref, don_ref,
             do_s, Tm_s, K_s, N_s, R_s, eg_s, S_s, dS_s):
        h = pl.program_id(1)
        lane = lax.broadcasted_iota(jnp.int32, (CHUNK, LANES), 1)

        def cols(rows, d):
            gbb = gb_ref[rows, :]
            return (_rowsum(jnp.where(lane == d * H + h, gbb, 0.0)),
                    _rowsum(jnp.where(lane == 2 * H + d * H + h, gbb, 0.0)))

        _, vjp = jax.vjp(_dn_out_math, o_ref[...], on_ref[...], z_ref[nc:, :])
        do, don, dz = vjp(dy_ref[...])
        do_s[0:nc, :] = jnp.zeros((nc, HD), F32)
        do_s[nc:, :] = do
        dp_ref[0:nc, :] = jnp.zeros((nc, HD), BF16)
        dp_ref[nc:, :] = dz.astype(BF16)

        @pl.when(h == 0)
        def _():
            don_ref[...] = jnp.zeros_like(don_ref)
            dgb_ref[...] = jnp.zeros_like(dgb_ref)

        don_ref[0, 0:1, :] += don

        for d in (0, 1):
            def pre(g, carry, d=d):
                cs = [g * G + i for i in range(G)]
                rows = [pl.ds(pl.multiple_of(c * CHUNK, CHUNK), CHUNK) for c in cs]
                chunks = [(q_ref[r, :], k_ref[r, :], v_ref[r, :], *cols(r, d)) for r in rows]
                for c, r, (Tm, K, Nn, Qp, _, egl) in zip(cs, rows, _delta_group_pre(chunks, d == 1)):
                    Tm_s[c] = Tm
                    K_s[c] = K.astype(BF16)
                    N_s[c] = Nn
                    R_s[c] = _bdot(Qp, do_s[r, :], _TN)
                    eg_s[c] = jnp.broadcast_to(egl, (SUBLANES, HD))
                return carry

            lax.fori_loop(0, n // G, pre, 0)

            def fwd_step(i, S, d=d):
                c = _delta_chunk_at(T, nc, d, i)
                S_s[c] = S
                return eg_s[c][0:1] * S + N_s[c] - jnp.dot(K_s[c], S.astype(BF16), preferred_element_type=F32)

            lax.fori_loop(0, n, fwd_step, jnp.zeros((HD, HD), F32))

            def bwd_step(i, dS, d=d):
                c = _delta_chunk_at(T, nc, d, n - 1 - i)
                dS_s[c] = dS
                return (eg_s[c][0:1] * dS + R_s[c]
                        - lax.dot_general(K_s[c], dS.astype(BF16), _TN, preferred_element_type=F32))

            lax.fori_loop(0, n, bwd_step, jnp.zeros((HD, HD), F32))

            def grads(c, carry, d=d):
                rows = pl.ds(pl.multiple_of(c * CHUNK, CHUNK), CHUNK)
                gcol, bcol = cols(rows, d)
                dq, dk, dv, dg, dbeta = _delta_chunk_bwd(q_ref[rows, :], k_ref[rows, :], v_ref[rows, :], gcol, bcol,
                                                         S_s[c], Tm_s[c], do_s[rows, :], dS_s[c], d == 1)
                if d == 0:
                    dqkv_ref[0, rows, :] = dq
                    dqkv_ref[1, rows, :] = dk
                    dqkv_ref[2, rows, :] = dv
                else:
                    dqkv_ref[0, rows, :] += dq
                    dqkv_ref[1, rows, :] += dk
                    dqkv_ref[2, rows, :] += dv
                dgb_ref[rows, :] += (jnp.where(lane == d * H + h, dg, 0.0)
                                     + jnp.where(lane == 2 * H + d * H + h, dbeta, 0.0))
                return carry

            lax.fori_loop(0, n, grads, 0)

    col = lambda off: pl.BlockSpec((T, HD), lambda b, h: (b, off + h))
    lat = pl.BlockSpec((N, HD), lambda b, h: (b, h))
    return pl.pallas_call(
        body, name="delta_bwd", grid=(B, H),
        in_specs=[col(0), col(H), col(2 * H), pl.BlockSpec((T, LANES), lambda b, h: (b, 0)), col(3 * H),
                  pl.BlockSpec((1, HD), lambda b, h: (0, 0)), lat, lat, pl.BlockSpec(memory_space=pl.ANY)],
        out_specs=[pl.BlockSpec((3, T, HD), lambda b, h: (0, b, h)), pl.BlockSpec((T, LANES), lambda b, h: (b, 0)),
                   col(3 * H), pl.BlockSpec((1, 8, HD), lambda b, h: (b, 0, 0))],
        out_shape=[jax.ShapeDtypeStruct((3, B * T, H * HD), F32), jax.ShapeDtypeStruct((B * T, LANES), F32),
                   jax.ShapeDtypeStruct(dp.shape, dp.dtype), jax.ShapeDtypeStruct((B, 8, HD), F32)],
        scratch_shapes=[pltpu.VMEM((T, HD), F32), pltpu.VMEM((n, CHUNK, CHUNK), F32), pltpu.VMEM((n, HD, HD), BF16),
                        pltpu.VMEM((n, HD, HD), F32), pltpu.VMEM((n, HD, HD), F32), pltpu.VMEM((n, SUBLANES, HD), F32),
                        pltpu.VMEM((n, HD, HD), F32), pltpu.VMEM((n, HD, HD), F32)],
        input_output_aliases={8: 2},
        compiler_params=_params(("parallel", "arbitrary")),
    )(qkv, qkv, qkv, gb, p, onorm, o, dy, dp)


def _rowwise(fn, ins, out_dtypes, *, name, tm=256, mult=16):
    R, W = ins[0].shape
    tm = _tile(R, tm, mult)

    def body(*refs):
        outs = fn(*[r[...] for r in refs[:len(ins)]])
        for o_ref, o in zip(refs[len(ins):], outs):
            o_ref[...] = o.astype(o_ref.dtype)

    spec = pl.BlockSpec((tm, W), lambda i: (i, 0))
    return pl.pallas_call(
        body, name=name, grid=(R // tm,), in_specs=[spec] * len(ins), out_specs=[spec] * len(out_dtypes),
        out_shape=[jax.ShapeDtypeStruct((R, W), dt) for dt in out_dtypes],
        compiler_params=_params(("parallel",)),
    )(*ins)


def _sum_lead(x, *, name, tm=256, mult=16):
    S, R, W = x.shape
    tm = _tile(R, tm, mult)

    def body(*refs):
        acc = refs[0][0].astype(F32)
        for r in refs[1:S]:
            acc = acc + r[0].astype(F32)
        refs[S][...] = acc

    return pl.pallas_call(
        body, name=name, grid=(R // tm,),
        in_specs=[pl.BlockSpec((1, tm, W), functools.partial(lambda s, i: (s, i, 0), s)) for s in range(S)],
        out_specs=pl.BlockSpec((tm, W), lambda i: (i, 0)),
        out_shape=jax.ShapeDtypeStruct((R, W), F32),
        compiler_params=_params(("parallel",)),
    )(*([x] * S))


def _adamw_math(w, g, m, v):
    m = ADAM_B1 * m + (1.0 - ADAM_B1) * g
    v = ADAM_B2 * v + (1.0 - ADAM_B2) * (g * g)
    m_hat = m / (1.0 - ADAM_B1 ** ADAM_STEP)
    v_hat = v / (1.0 - ADAM_B2 ** ADAM_STEP)
    return -ADAM_LR * (m_hat / (jnp.sqrt(v_hat) + ADAM_EPS) + ADAM_WD * w), m, v


def _adamw(w, g, m, v, *, name):
    return _rowwise(_adamw_math, [w, g, m, v], [F32, F32, F32], name=name, tm=128, mult=SUBLANES)


def _me():
    return lax.axis_index("x"), lax.axis_index("y"), lax.axis_index("c")


def _allgather_small(v):
    R, W = v.shape

    def body(x_ref, out_ref, send_sems, recv_sems, local_sem):
        x, y, c = _me()
        me, sibling = (x, y, c), (x, y, 1 - c)
        chips = [(1 - x, y), (x, 1 - y), (1 - x, 1 - y)]

        def slot(px, py, pc):
            return out_ref.at[4 * px + 2 * py + pc]

        def copy(k, block, to, src=None):
            return pltpu.make_async_remote_copy(
                src_ref=slot(*block) if src is None else src, dst_ref=slot(*block),
                send_sem=send_sems.at[k], recv_sem=recv_sems.at[k], device_id=to, device_id_type=MESH)

        mine = pltpu.make_async_copy(x_ref, slot(*me), local_sem)
        mine.start()
        first = [copy(0, me, sibling, src=x_ref)]
        first += [copy(1 + j, me, (*chip, c), src=x_ref) for j, chip in enumerate(chips)]
        for cp in first:
            cp.start()
        passed = [copy(4 + j, (*chip, c), sibling) for j, chip in enumerate(chips)]
        for j, chip in enumerate(chips):
            copy(1 + j, (*chip, c), me).wait_recv()
            passed[j].start()
        copy(0, sibling, me).wait_recv()
        for j, chip in enumerate(chips):
            copy(4 + j, (*chip, 1 - c), me).wait_recv()
        for cp in first + passed:
            cp.wait_send()
        mine.wait()

    return pl.pallas_call(
        body, name="allgather_small", out_shape=jax.ShapeDtypeStruct((8, R, W), v.dtype),
        in_specs=[pl.BlockSpec(memory_space=pltpu.VMEM)], out_specs=pl.BlockSpec(memory_space=pltpu.VMEM),
        scratch_shapes=[pltpu.SemaphoreType.DMA((7,)), pltpu.SemaphoreType.DMA((7,)), pltpu.SemaphoreType.DMA],
        compiler_params=_params(),
    )(v)


def _allgather_big(pool):
    PR, W = pool.shape[0] // 2, pool.shape[1]

    def body(x_ref, out_ref, send_sems, recv_sems, local_sem):
        x, y, c = _me()
        me, sibling = (x, y, c), (x, y, 1 - c)
        chips = [(1 - x, y), (x, 1 - y), (1 - x, 1 - y)]
        own = x_ref.at[pl.ds(c * PR, PR), :]

        def slot(px, py, pc):
            return out_ref.at[4 * px + 2 * py + pc]

        def copy(k, block, to, src=None):
            return pltpu.make_async_remote_copy(
                src_ref=slot(*block) if src is None else src, dst_ref=slot(*block),
                send_sem=send_sems.at[k], recv_sem=recv_sems.at[k], device_id=to, device_id_type=MESH)

        mine = pltpu.make_async_copy(own, slot(*me), local_sem)
        mine.start()
        first = [copy(0, me, sibling, src=own)]
        first += [copy(1 + j, me, (*chip, c), src=own) for j, chip in enumerate(chips)]
        for cp in first:
            cp.start()
        passed = [copy(4 + j, (*chip, c), sibling) for j, chip in enumerate(chips)]
        for j, chip in enumerate(chips):
            copy(1 + j, (*chip, c), me).wait_recv()
            passed[j].start()
        copy(0, sibling, me).wait_recv()
        for j, chip in enumerate(chips):
            copy(4 + j, (*chip, 1 - c), me).wait_recv()
        for cp in first + passed:
            cp.wait_send()
        mine.wait()

    return pl.pallas_call(
        body, name="allgather_big", out_shape=jax.ShapeDtypeStruct((8, PR, W), pool.dtype),
        in_specs=[pl.BlockSpec(memory_space=pl.ANY)], out_specs=pl.BlockSpec(memory_space=pl.ANY),
        scratch_shapes=[pltpu.SemaphoreType.DMA((7,)), pltpu.SemaphoreType.DMA((7,)), pltpu.SemaphoreType.DMA],
        compiler_params=_params(),
    )(pool)


def _sibling_swap(v, *, name):
    def body(x_ref, out_ref, send_sem, recv_sem):
        x, y, c = _me()
        cp = pltpu.make_async_remote_copy(src_ref=x_ref, dst_ref=out_ref, send_sem=send_sem, recv_sem=recv_sem,
                                          device_id=(x, y, 1 - c), device_id_type=MESH)
        cp.start()
        cp.wait()

    return pl.pallas_call(
        body, name=name, out_shape=jax.ShapeDtypeStruct(v.shape, v.dtype),
        in_specs=[pl.BlockSpec(memory_space=pl.ANY)], out_specs=pl.BlockSpec(memory_space=pl.ANY),
        scratch_shapes=[pltpu.SemaphoreType.DMA, pltpu.SemaphoreType.DMA],
        compiler_params=_params(),
    )(v)


def _chip_exchange(v):
    def body(x_ref, out_ref, send_sems, recv_sems, local_sem):
        x, y, c = _me()
        s_me = 2 * x + y
        chips = [(1 - x, y), (x, 1 - y), (1 - x, 1 - y)]
        mine = pltpu.make_async_copy(x_ref.at[s_me], out_ref.at[s_me], local_sem)
        mine.start()
        sends = []
        for k, (px, py) in enumerate(chips):
            cp = pltpu.make_async_remote_copy(
                src_ref=x_ref.at[2 * px + py], dst_ref=out_ref.at[s_me], send_sem=send_sems.at[k],
                recv_sem=recv_sems.at[k], device_id=(px, py, c), device_id_type=MESH)
            cp.start()
            sends.append(cp)
        for k, (px, py) in enumerate(chips):
            pltpu.make_async_remote_copy(
                src_ref=x_ref.at[s_me], dst_ref=out_ref.at[2 * px + py], send_sem=send_sems.at[k],
                recv_sem=recv_sems.at[k], device_id=(px, py, c), device_id_type=MESH).wait_recv()
        for cp in sends:
            cp.wait_send()
        mine.wait()

    return pl.pallas_call(
        body, name="chip_exchange", out_shape=jax.ShapeDtypeStruct(v.shape, v.dtype),
        in_specs=[pl.BlockSpec(memory_space=pl.ANY)], out_specs=pl.BlockSpec(memory_space=pl.ANY),
        scratch_shapes=[pltpu.SemaphoreType.DMA((3,)), pltpu.SemaphoreType.DMA((3,)), pltpu.SemaphoreType.DMA],
        compiler_params=_params(),
    )(v)


def _layout(sizes, width, part_mult, total_mult):
    offs, rows, r = [], [], 0
    for n in sizes:
        k = -(-n // width)
        offs.append(r)
        rows.append(k)
        r += -(-k // part_mult) * part_mult
    return offs, rows, -(-r // total_mult) * total_mult


def _pack(arrs, width, part_mult, total_mult, dtype, lead=()):
    nl = len(lead)
    sizes = [math.prod(a.shape[nl:]) for a in arrs]
    offs, rows, total = _layout(sizes, width, part_mult, total_mult)
    parts, r = [], 0
    for a, n, o, k in zip(arrs, sizes, offs, rows):
        if o > r:
            parts.append(jnp.zeros((*lead, o - r, width), dtype))
        flat = a.reshape(*lead, n).astype(dtype)
        if k * width > n:
            flat = jnp.concatenate([flat, jnp.zeros((*lead, k * width - n), dtype)], axis=-1)
        parts.append(flat.reshape(*lead, k, width))
        r = o + k
    if total > r:
        parts.append(jnp.zeros((*lead, total - r, width), dtype))
    return jnp.concatenate(parts, axis=nl)


def _unpack(pool, shapes, width, part_mult, total_mult):
    lead = pool.shape[:-2]
    sizes = [math.prod(s) for s in shapes]
    offs, rows, _ = _layout(sizes, width, part_mult, total_mult)
    out = []
    for s, n, o, k in zip(shapes, sizes, offs, rows):
        flat = lax.slice_in_dim(pool, o, o + k, axis=len(lead)).reshape(*lead, k * width)
        out.append(lax.slice_in_dim(flat, 0, n, axis=len(lead)).reshape(*lead, *s))
    return out


_WEIGHTS = ("c_ctx", "w_ada", "b_ada", "g_pre_mix", "g_post_mix", "g_pre_ffn", "g_post_ffn", "w_in", "b_merge",
            "dn_conv", "dn_a_log", "dn_dt_bias", "dn_onorm", "lru_conv", "lru_conv_b", "lru_w_rg", "lru_b_rg",
            "lru_w_ig", "lru_b_ig", "lru_lambda", "w_branch_dn", "w_branch_lru", "w_out", "w_up", "ffn_dw",
            "ffn_dw_b", "w_down")
_BIG = {"w_ada": True, "w_in": True, "w_branch_dn": False, "w_branch_lru": False, "w_out": False, "w_up": True,
        "w_down": False}
_SMALL_SHARDED = ("dn_conv", "lru_conv", "lru_b_rg", "lru_b_ig", "lru_lambda", "ffn_dw")
_NCHIP = 4
_POOL_PART, _POOL_TOTAL = 16, 32
_FLAT_PART = 8


def _to_chip_shards(g, by_cols):
    if by_cols:
        return g.reshape(g.shape[0], _NCHIP, g.shape[1] // _NCHIP).transpose(1, 0, 2)
    return g.reshape(_NCHIP, g.shape[0] // _NCHIP, g.shape[1])


def _from_chip_shards(s, by_cols):
    if by_cols:
        return s.transpose(1, 0, 2).reshape(s.shape[1], _NCHIP * s.shape[2])
    return s.reshape(_NCHIP * s.shape[1], s.shape[2])


def _dsilu(x):
    s = _sigmoid(x)
    return s * (1.0 + x * (1.0 - s))


def kernel(x, c, ctx, c_ctx, w_ada, b_ada, g_pre_mix, g_post_mix, g_pre_ffn, g_post_ffn, w_in, b_merge, dn_conv, dn_a_log, dn_dt_bias, dn_onorm, lru_conv, lru_conv_b, lru_w_rg, lru_b_rg, lru_w_ig, lru_b_ig, lru_lambda, w_branch_dn, w_branch_lru, w_out, w_up, ffn_dw, ffn_dw_b, w_down, loss_target, m_c_ctx, m_w_ada, m_b_ada, m_g_pre_mix, m_g_post_mix, m_g_pre_ffn, m_g_post_ffn, m_w_in, m_b_merge, m_dn_conv, m_dn_a_log, m_dn_dt_bias, m_dn_onorm, m_lru_conv, m_lru_conv_b, m_lru_w_rg, m_lru_b_rg, m_lru_w_ig, m_lru_b_ig, m_lru_lambda, m_w_branch_dn, m_w_branch_lru, m_w_out, m_w_up, m_ffn_dw, m_ffn_dw_b, m_w_down, v_c_ctx, v_w_ada, v_b_ada, v_g_pre_mix, v_g_post_mix, v_g_pre_ffn, v_g_post_ffn, v_w_in, v_b_merge, v_dn_conv, v_dn_a_log, v_dn_dt_bias, v_dn_onorm, v_lru_conv, v_lru_conv_b, v_lru_w_rg, v_lru_b_rg, v_lru_w_ig, v_lru_b_ig, v_lru_lambda, v_w_branch_dn, v_w_branch_lru, v_w_out, v_w_up, v_ffn_dw, v_ffn_dw_b, v_w_down):
    W = dict(zip(_WEIGHTS, (c_ctx, w_ada, b_ada, g_pre_mix, g_post_mix, g_pre_ffn, g_post_ffn, w_in, b_merge, dn_conv,
                            dn_a_log, dn_dt_bias, dn_onorm, lru_conv, lru_conv_b, lru_w_rg, lru_b_rg, lru_w_ig, lru_b_ig,
                            lru_lambda, w_branch_dn, w_branch_lru, w_out, w_up, ffn_dw, ffn_dw_b, w_down)))
    Mo = dict(zip(_WEIGHTS, (m_c_ctx, m_w_ada, m_b_ada, m_g_pre_mix, m_g_post_mix, m_g_pre_ffn, m_g_post_ffn, m_w_in,
                             m_b_merge, m_dn_conv, m_dn_a_log, m_dn_dt_bias, m_dn_onorm, m_lru_conv, m_lru_conv_b,
                             m_lru_w_rg, m_lru_b_rg, m_lru_w_ig, m_lru_b_ig, m_lru_lambda, m_w_branch_dn,
                             m_w_branch_lru, m_w_out, m_w_up, m_ffn_dw, m_ffn_dw_b, m_w_down)))
    Vo = dict(zip(_WEIGHTS, (v_c_ctx, v_w_ada, v_b_ada, v_g_pre_mix, v_g_post_mix, v_g_pre_ffn, v_g_post_ffn, v_w_in,
                             v_b_merge, v_dn_conv, v_dn_a_log, v_dn_dt_bias, v_dn_onorm, v_lru_conv, v_lru_conv_b,
                             v_lru_w_rg, v_lru_b_rg, v_lru_w_ig, v_lru_b_ig, v_lru_lambda, v_w_branch_dn,
                             v_w_branch_lru, v_w_out, v_w_up, v_ffn_dw, v_ffn_dw_b, v_w_down)))
    B, N, D = x.shape
    NC = ctx.shape[1]
    T = NC + N
    H, HD = dn_a_log.shape[-1], dn_onorm.shape[-1]
    DNW = H * HD
    LW, LBD = lru_conv_b.shape[-1], lru_w_rg.shape[-1]
    DFF = ffn_dw_b.shape[-1]
    LC = LANES
    x_i, y_i, c_i = _me()
    s_me = 2 * x_i + y_i
    tm = _tile(math.gcd(NC, N), 256, 16)

    big_local = [W[n][0] for n in _BIG]
    big_shapes = [a.shape for a in big_local]
    pool = _pack(big_local, D, _POOL_PART, _POOL_TOTAL, BF16)
    gathered = _allgather_big(pool)
    gathered = gathered.reshape(_NCHIP, pool.shape[0], D)
    full = {n: _from_chip_shards(s, _BIG[n])
            for n, s in zip(_BIG, _unpack(gathered, big_shapes, D, _POOL_PART, _POOL_TOTAL))}

    small_local = [W[n][0].reshape(-1, W[n].shape[-1]) for n in _SMALL_SHARDED]
    small_shapes = [a.shape for a in small_local]
    spack = _pack(small_local, LANES, _FLAT_PART, _FLAT_PART, F32)
    sgath = _allgather_small(spack)[0::2]
    sfull = {n: _from_chip_shards(s, True)
             for n, s in zip(_SMALL_SHARDED, _unpack(sgath, small_shapes, LANES, _FLAT_PART, _FLAT_PART))}

    o_a = 4 * DNW
    o_xl = o_a + 4 * H
    o_mg = o_xl + 2 * LW
    wi_ = full["w_in"]
    nj = LW // LC
    lru_cols = jnp.stack([wi_[:, o_xl:o_xl + LW].reshape(D, nj, LC), wi_[:, o_xl + LW:o_mg].reshape(D, nj, LC)],
                         axis=2).reshape(D, 2 * LW)
    wp = jnp.concatenate([wi_[:, :o_a], lru_cols, wi_[:, o_mg:], wi_[:, o_a:o_xl],
                          jnp.zeros((D, LANES - 4 * H), BF16)], axis=1)
    p_lru, p_mg, p_ab = 4 * DNW, 4 * DNW + 2 * LW, 4 * DNW + 2 * LW + 2 * D
    PW = p_ab + LANES

    MR = LANES
    cond = jnp.concatenate([c, c_ctx[None], jnp.zeros((MR - B - 1, D), F32)], axis=0)
    silu_rows = _rowwise(lambda a: (_silu(a),), [cond], [F32], name="cond_silu")[0]
    mod = _matmul(silu_rows, full["w_ada"], name="ada_fwd") + b_ada
    mx = mod[:B].reshape(B, 6, D)
    mc = mod[B].reshape(6, D)
    zero = jnp.zeros((B, D), F32)
    tab = jnp.stack([jnp.stack([jnp.broadcast_to(mc[0], (B, D)), jnp.broadcast_to(mc[1], (B, D))] + [zero] * 6, axis=1),
                     jnp.stack([mx[:, 0], mx[:, 1]] + [zero] * 6, axis=1)], axis=1)
    vecs = jnp.stack([mx[:, 2], mx[:, 3], mx[:, 4], mx[:, 5]] + [zero] * 4, axis=1)
    gains = jnp.concatenate([g_post_mix, g_pre_ffn, g_post_ffn, jnp.zeros((5, D), F32)], axis=0)

    h = jnp.concatenate([ctx, x], axis=1)
    u = _premix_fwd(h, g_pre_mix, tab, nc=NC, tm=tm)
    p = _matmul(u, wp, name="in_fwd")
    dkw = dict(B=B, T=T, nc=NC, H=H, HD=HD)
    qkv = _dnprep_fwd(p, sfull["dn_conv"], **dkw)
    prm = jnp.concatenate([
        jnp.concatenate([dn_a_log.reshape(1, 2 * H), jnp.zeros((1, LANES - 2 * H), F32)], axis=1),
        jnp.concatenate([dn_dt_bias.reshape(1, 2 * H), jnp.zeros((1, LANES - 2 * H), F32)], axis=1),
        jnp.zeros((6, LANES), F32)], axis=0)
    gtm = _tile(B * T, 512, 16)
    gb = _gb_fwd(p, prm, rows=B * T, col0=p_ab, H=H, tm=gtm)
    y_dn, o_dn = _delta_fwd(qkv, gb, p, dn_onorm, **dkw)
    lv = jnp.concatenate([lru_conv_b, sfull["lru_b_rg"], sfull["lru_b_ig"], sfull["lru_lambda"], jnp.zeros((1, LW), F32)], axis=0)
    wr = _blockdiag(lru_w_rg[0], LC).astype(BF16)
    wi = _blockdiag(lru_w_ig[0], LC).astype(BF16)
    lkw = dict(B=B, T=T, nc=NC, LW=LW, col0=p_lru, C=LC)
    y_lru = _lru_fwd(p, sfull["lru_conv"], lv, wr, wi, **lkw)
    Ydn = _matmul(y_dn, full["w_branch_dn"], name="bdn_fwd")
    Ylru = _matmul(y_lru, full["w_branch_lru"], name="blru_fwd")
    mkw = dict(B=B, T=T, nc=NC, D=D, col0=p_mg, tm=tm)
    mixin = _merge_fwd(p, Ydn, Ylru, b_merge, **mkw)
    mix = _matmul(mixin, full["w_out"], name="out_fwd")
    h1, u2 = _post_fwd(x, mix, gains, vecs, tm=tm)
    F = _matmul(u2, full["w_up"], name="up_fwd")
    w9 = sfull["ffn_dw"]
    ftc = _tile(DFF, 256)
    f = _ffn_act_fwd(F, w9, ffn_dw_b, B=B, N=N, DFF=DFF, tc=ftc)
    dn = _matmul(f, full["w_down"], name="down_fwd")
    ddn, dout, sums_f = _final(h1, dn, loss_target, gains, vecs, tm=tm)

    G = {}
    df = _matmul(ddn, full["w_down"], tb=True, name="down_bwd_x")
    G["w_down"] = _matmul(f, ddn, ta=True, name="down_bwd_w")
    dFg, dFv, dwb = _ffn_act_bwd(F, w9, ffn_dw_b, df, B=B, N=N, DFF=DFF, tc=ftc)
    du2 = _matmul(dFg, full["w_up"][:, :DFF], tb=True, name="up_bwd_xg")
    du2 = _matmul(dFv, full["w_up"][:, DFF:], tb=True, add=du2, name="up_bwd_xv")
    G["w_up"] = jnp.concatenate([_matmul(u2, dFg, ta=True, name="up_bwd_wg"), _matmul(u2, dFv, ta=True, name="up_bwd_wv")], axis=1)
    dx1, dmix, sums_p = _post_bwd(x, mix, gains, vecs, dout, du2, tm=tm)
    dmixin = _matmul(dmix, full["w_out"], tb=True, name="out_bwd_x")
    G["w_out"] = _matmul(mixin, dmix, ta=True, name="out_bwd_w")
    dp = jnp.zeros((B * T, PW), BF16)
    dYdn, dYlru, dp, sums_m = _merge_bwd(p, Ydn, Ylru, b_merge, dmixin, dp, **mkw)
    dy_dn = _matmul(dYdn, full["w_branch_dn"], tb=True, name="bdn_bwd_x")
    G["w_branch_dn"] = _matmul(y_dn, dYdn, ta=True, name="bdn_bwd_w")
    dy_lru = _matmul(dYlru, full["w_branch_lru"], tb=True, name="blru_bwd_x")
    G["w_branch_lru"] = _matmul(y_lru, dYlru, ta=True, name="blru_bwd_w")
    dp, dcw_l, dlv, dwr, dwi = _lru_bwd(p, sfull["lru_conv"], lv, wr, wi, dy_lru, dp, **lkw)
    dqkv, dgb, dp, don = _delta_bwd(qkv, gb, p, dn_onorm, o_dn, dy_dn, dp, **dkw)
    dp, dprm = _gb_bwd(p, prm, dgb, dp, rows=B * T, col0=p_ab, H=H, tm=gtm)
    dp, dcw_d = _dnprep_bwd(p, sfull["dn_conv"], dqkv, dp, **dkw)
    dU = _matmul(dp, wp, tb=True, name="in_bwd_x")
    dwp = _matmul(u, dp, ta=True, name="in_bwd_w")
    grad_x, sums_pm = _premix_bwd(h, g_pre_mix, tab, dU, dx1, nc=NC, tm=tm)
    dlru = dwp[:, p_lru:p_mg].reshape(D, nj, 2, LC)
    G["w_in"] = jnp.concatenate([dwp[:, :o_a], dwp[:, p_ab:p_ab + 4 * H], dlru[:, :, 0].reshape(D, LW),
                                 dlru[:, :, 1].reshape(D, LW), dwp[:, p_mg:p_ab]], axis=1)

    dmod_x = jnp.stack([sums_pm[:, 1, 0], sums_pm[:, 1, 1], sums_p[:, 0], sums_p[:, 1], sums_p[:, 2], sums_f[:, 0]],
                       axis=1).reshape(B, 6 * D)
    dmod_c = jnp.concatenate([sums_pm[:, 0, 0].sum(0), sums_pm[:, 0, 1].sum(0), jnp.zeros((4 * D,), F32)])[None]
    dmod = jnp.concatenate([dmod_x, dmod_c, jnp.zeros((MR - B - 1, 6 * D), F32)], axis=0)
    G["w_ada"] = _matmul(silu_rows, dmod, ta=True, name="ada_bwd_w")
    dsilu = _matmul(dmod, full["w_ada"], tb=True, name="ada_bwd_x")

    g_small = {
        "c_ctx": dsilu[B] * _dsilu(c_ctx),
        "b_ada": dmod[:B + 1].sum(0)[None],
        "g_pre_mix": sums_pm[:, :, 2].sum((0, 1))[None],
        "g_post_mix": sums_p[:, 3].sum(0)[None],
        "g_pre_ffn": sums_p[:, 4].sum(0)[None],
        "g_post_ffn": sums_f[:, 1].sum(0)[None],
        "b_merge": sums_m[0:1],
        "dn_conv": dcw_d[0:4][None],
        "dn_a_log": dprm[0, :2 * H].reshape(1, 2, H),
        "dn_dt_bias": dprm[1, :2 * H].reshape(1, 2, H),
        "dn_onorm": don[:, 0].sum(0)[None],
        "lru_conv": dcw_l[0:4][None],
        "lru_conv_b": dlv[0:1],
        "lru_w_rg": _blockdiag_extract(dwr, LBD)[None],
        "lru_b_rg": dlv[1:3][None],
        "lru_w_ig": _blockdiag_extract(dwi, LBD)[None],
        "lru_b_ig": dlv[3:5][None],
        "lru_lambda": dlv[5:7][None],
        "ffn_dw": dwb[0:9].reshape(1, 3, 3, DFF),
        "ffn_dw_b": dwb[9:10],
    }
    small_names = tuple(n for n in _WEIGHTS if n not in _BIG)
    loss_part = sums_f[:, 2].sum().reshape(1)
    gs_list = [g_small[n] for n in small_names] + [loss_part]
    gs_shapes = [a.shape for a in gs_list]
    gpack = _pack(gs_list, LANES, _FLAT_PART, _FLAT_PART, F32)
    gsum = _sum_lead(_allgather_small(gpack), name="small_sum", tm=512, mult=SUBLANES)
    gs_red = dict(zip(small_names + ("loss",), _unpack(gsum, gs_shapes, LANES, _FLAT_PART, _FLAT_PART)))
    loss = gs_red["loss"][0]

    gp = _pack([_to_chip_shards(G[n], _BIG[n]) for n in _BIG], D, _POOL_PART, _POOL_TOTAL, F32, lead=(_NCHIP,))
    R = gp.shape[1]
    PR = R // 2
    gp = gp.reshape(_NCHIP, 2, PR, D)
    own = lax.dynamic_index_in_dim(gp, c_i, axis=1, keepdims=False)
    other = lax.dynamic_index_in_dim(gp, 1 - c_i, axis=1, keepdims=False).astype(BF16)
    from_sibling = _sibling_swap(other, name="rs_sibling")
    chip_sum = _rowwise(lambda a, b: (a + b.astype(F32),), [own.reshape(_NCHIP * PR, D), from_sibling.reshape(_NCHIP * PR, D)],
                        [BF16], name="rs_add")[0].reshape(_NCHIP, PR, D)
    half = _sum_lead(_chip_exchange(chip_sum), name="rs_sum")
    sib_half = _sibling_swap(half, name="rs_gather")
    red_pool = jnp.where(c_i == 0, jnp.concatenate([half, sib_half], axis=0), jnp.concatenate([sib_half, half], axis=0))
    g_big = dict(zip(_BIG, _unpack(red_pool, big_shapes, D, _POOL_PART, _POOL_TOTAL)))

    grads, deltas, new_m, new_v = {}, {}, {}, {}
    for n in _BIG:
        shp = W[n].shape
        grads[n] = g_big[n].reshape(shp)
        d_, m_, v_ = _adamw(W[n][0], g_big[n], Mo[n][0], Vo[n][0], name="adamw_" + n)
        deltas[n], new_m[n], new_v[n] = d_.reshape(shp), m_.reshape(shp), v_.reshape(shp)
    for n in small_names:
        g = gs_red[n]
        if n in _SMALL_SHARDED:
            k = W[n].shape[-1]
            g = lax.dynamic_slice_in_dim(g, s_me * k, k, axis=g.ndim - 1)
        grads[n] = g.reshape(W[n].shape)
    sm_shapes = [W[n].shape for n in small_names]
    pk = lambda d: _pack([d[n] for n in small_names], LANES, _FLAT_PART, _FLAT_PART, F32)
    d_, m_, v_ = _adamw(pk(W), pk(grads), pk(Mo), pk(Vo), name="adamw_small")
    for dst, pool_ in ((deltas, d_), (new_m, m_), (new_v, v_)):
        dst.update(zip(small_names, _unpack(pool_, sm_shapes, LANES, _FLAT_PART, _FLAT_PART)))
    return (loss, grad_x, *[grads[n] for n in _WEIGHTS], *[deltas[n] for n in _WEIGHTS],
            *[new_m[n] for n in _WEIGHTS], *[new_v[n] for n in _WEIGHTS])
```

```python
import functools
import math

import jax
import jax.numpy as jnp
from jax import lax
from jax.experimental import pallas as pl
from jax.experimental.pallas import tpu as pltpu

F32 = jnp.float32
BF16 = jnp.bfloat16
EPS = 1e-6
GRID_W = 64
CHUNK = 64
LRU_C = 8.0
LANES = 128
SUBLANES = 8
VMEM_LIMIT = 56 * 1024 * 1024
ADAM_LR, ADAM_B1, ADAM_B2, ADAM_EPS, ADAM_WD, ADAM_STEP = 0.001, 0.9, 0.999, 1e-08, 0.01, 10
MESH = pl.DeviceIdType.MESH


def _tile(n, target, mult=LANES):
    best = None
    for t in range(mult, min(n, target) + 1, mult):
        if n % t == 0:
            best = t
    return best if best is not None else n


def _params(sem=None, **kw):
    return pltpu.CompilerParams(dimension_semantics=sem, vmem_limit_bytes=VMEM_LIMIT, **kw)


def _sigmoid(x):
    return 1.0 / (1.0 + jnp.exp(-x))


def _silu(x):
    return x * _sigmoid(x)


def _softplus(x):
    return jnp.maximum(x, 0.0) + jnp.log(1.0 + jnp.exp(-jnp.abs(x)))


def _gelu(x):
    return 0.5 * x * (1.0 + jnp.tanh(math.sqrt(2.0 / math.pi) * (x + 0.044715 * x * x * x)))


def _rmsn(u, gain):
    return u * lax.rsqrt(jnp.mean(u * u, axis=-1, keepdims=True) + EPS) * gain


_MM_VMEM = 40 * 1024 * 1024


def _matmul(a, b, *, ta=False, tb=False, add=None, b_shards=None, out_shards=None, out_dtype=F32, name,
            tm=1024, tn=2048, tk=1024):
    (K, M) = a.shape if ta else a.shape[::-1]
    if b_shards is not None:
        s0, ns = b_shards
        bsh = (b.shape[1], ns * b.shape[2])
        nsh = b.shape[2]
    else:
        bsh = b.shape
    N = bsh[0] if tb else bsh[1]
    assert (bsh[1] if tb else bsh[0]) == K, (a.shape, b.shape, ta, tb)
    tm = _tile(M, tm)
    tk = _tile(nsh if (b_shards is not None and tb) else K, tk)
    nlim = nsh if (b_shards is not None and not tb) else (N // out_shards if out_shards else N)
    osz = jnp.dtype(out_dtype).itemsize + (4 if add is not None else 0)
    while True:
        tn_ = _tile(nlim, tn)
        need = 2 * (tm * tk * a.dtype.itemsize + tk * tn_ * b.dtype.itemsize + tm * tn_ * osz) + 4 * tm * tn_
        if need <= _MM_VMEM or tn <= LANES:
            break
        tn //= 2
    tn = tn_
    nk = K // tk
    dims = (((0 if ta else 1,), (1 if tb else 0,)), ((), ()))

    def body(a_ref, b_ref, *rest):
        (c_ref, o_ref, acc_ref) = rest if add is not None else (None, *rest)
        k = pl.program_id(2)

        @pl.when(k == 0)
        def _():
            acc_ref[...] = jnp.zeros_like(acc_ref) if c_ref is None else c_ref[...]

        bv = b_ref[0] if b_shards is not None else b_ref[...]
        acc_ref[...] += lax.dot_general(a_ref[...].astype(BF16), bv.astype(BF16), dims, preferred_element_type=F32)

        @pl.when(k == nk - 1)
        def _():
            if out_shards:
                o_ref[0] = acc_ref[...].astype(out_dtype)
            else:
                o_ref[...] = acc_ref[...].astype(out_dtype)

    a_spec = pl.BlockSpec((tk, tm), lambda i, j, k: (k, i)) if ta else pl.BlockSpec((tm, tk), lambda i, j, k: (i, k))
    if b_shards is None:
        b_spec = pl.BlockSpec((tn, tk), lambda i, j, k: (j, k)) if tb else pl.BlockSpec((tk, tn), lambda i, j, k: (k, j))
    elif tb:
        per = nsh // tk
        b_spec = pl.BlockSpec((1, tn, tk), lambda i, j, k: (s0 + k // per, j, k % per))
    else:
        per = nsh // tn
        b_spec = pl.BlockSpec((1, tk, tn), lambda i, j, k: (s0 + j // per, k, j % per))
    o_spec = pl.BlockSpec((tm, tn), lambda i, j, k: (i, j))
    if out_shards:
        oper = N // out_shards // tn
        out_spec = pl.BlockSpec((1, tm, tn), lambda i, j, k: (j // oper, i, j % oper))
        out_shape = jax.ShapeDtypeStruct((out_shards, M, N // out_shards), out_dtype)
    else:
        out_spec, out_shape = o_spec, jax.ShapeDtypeStruct((M, N), out_dtype)
    return pl.pallas_call(
        body, name=name, grid=(M // tm, N // tn, nk),
        in_specs=[a_spec, b_spec] + ([o_spec] if add is not None else []),
        out_specs=out_spec, out_shape=out_shape,
        scratch_shapes=[pltpu.VMEM((tm, tn), F32)],
        compiler_params=_params(("parallel", "parallel", "arbitrary")),
    )(*((a, b) + ((add,) if add is not None else ())))


def _premix_math(h, gain, shift, scale):
    return _rmsn(h, gain) * (1.0 + scale) + shift


def _premix_fwd(h, gain, tab, *, nc, tm):
    B, T, D = h.shape
    nt, nct = T // tm, nc // tm

    def body(h_ref, g_ref, tab_ref, u_ref):
        tabv = tab_ref[0, 0]
        u_ref[...] = _premix_math(h_ref[0], g_ref[...], tabv[0:1], tabv[1:2]).astype(BF16)

    return pl.pallas_call(
        body, name="premix_fwd", grid=(B, nt),
        in_specs=[pl.BlockSpec((1, tm, D), lambda b, t: (b, t, 0)),
                  pl.BlockSpec((1, D), lambda b, t: (0, 0)),
                  pl.BlockSpec((1, 1, 8, D), lambda b, t: (b, jnp.where(t < nct, 0, 1), 0, 0))],
        out_specs=pl.BlockSpec((tm, D), lambda b, t: (b * nt + t, 0)),
        out_shape=jax.ShapeDtypeStruct((B * T, D), BF16),
        compiler_params=_params(("parallel", "parallel")),
    )(h, gain, tab)


def _premix_bwd(h, gain, tab, du, dres, *, nc, tm):
    B, T, D = h.shape
    nt, nct = T // tm, nc // tm
    N = T - nc

    def body(h_ref, g_ref, tab_ref, du_ref, dres_ref, dx_ref, sums_ref):
        t = pl.program_id(1)
        tabv = tab_ref[0, 0]
        _, vjp = jax.vjp(_premix_math, h_ref[0], g_ref[...], tabv[0:1], tabv[1:2])
        dh, dgain, dshift, dscale = vjp(du_ref[...].astype(F32))

        @pl.when((t == 0) | (t == nct))
        def _():
            sums_ref[...] = jnp.zeros_like(sums_ref)

        sums_ref[0, 0, 0:1, :] += dshift
        sums_ref[0, 0, 1:2, :] += dscale
        sums_ref[0, 0, 2:3, :] += dgain

        @pl.when(t >= nct)
        def _():
            dx_ref[0] = dres_ref[...] + dh

    lat = lambda b, t: jnp.maximum(t - nct, 0)
    return pl.pallas_call(
        body, name="premix_bwd", grid=(B, nt),
        in_specs=[pl.BlockSpec((1, tm, D), lambda b, t: (b, t, 0)),
                  pl.BlockSpec((1, D), lambda b, t: (0, 0)),
                  pl.BlockSpec((1, 1, 8, D), lambda b, t: (b, jnp.where(t < nct, 0, 1), 0, 0)),
                  pl.BlockSpec((tm, D), lambda b, t: (b * nt + t, 0)),
                  pl.BlockSpec((tm, D), lambda b, t: (b * (nt - nct) + lat(b, t), 0))],
        out_specs=[pl.BlockSpec((1, tm, D), lambda b, t: (b, lat(b, t), 0)),
                   pl.BlockSpec((1, 1, 8, D), lambda b, t: (b, jnp.where(t < nct, 0, 1), 0, 0))],
        out_shape=[jax.ShapeDtypeStruct((B, N, D), F32), jax.ShapeDtypeStruct((B, 2, 8, D), F32)],
        compiler_params=_params(("parallel", "arbitrary")),
    )(h, gain, tab, du, dres)


def _merge_math(mgd, mgl, yd, yl, bd, bl):
    return _sigmoid(mgd + bd) * yd + _sigmoid(mgl + bl) * yl


def _merge_fwd(p, ydn, ylru, b_merge, *, B, T, nc, D, col0, tm):
    N = T - nc
    ntl, nt, nct, cb = N // tm, T // tm, nc // tm, col0 // D

    def body(mgd_ref, mgl_ref, yd_ref, yl_ref, bm_ref, o_ref):
        o_ref[...] = _merge_math(mgd_ref[...], mgl_ref[...], yd_ref[...], yl_ref[...],
                                 bm_ref[:, 0:D], bm_ref[:, D:2 * D]).astype(BF16)

    prow = lambda b, t: b * nt + nct + t
    return pl.pallas_call(
        body, name="merge_fwd", grid=(B, ntl),
        in_specs=[pl.BlockSpec((tm, D), lambda b, t: (prow(b, t), cb)),
                  pl.BlockSpec((tm, D), lambda b, t: (prow(b, t), cb + 1)),
                  pl.BlockSpec((tm, D), lambda b, t: (b * ntl + t, 0)),
                  pl.BlockSpec((tm, D), lambda b, t: (b * ntl + t, 0)),
                  pl.BlockSpec((1, 2 * D), lambda b, t: (0, 0))],
        out_specs=pl.BlockSpec((tm, D), lambda b, t: (b * ntl + t, 0)),
        out_shape=jax.ShapeDtypeStruct((B * N, D), BF16),
        compiler_params=_params(("parallel", "parallel")),
    )(p, p, ydn, ylru, b_merge)


def _merge_bwd(p, ydn, ylru, b_merge, dmix, dp, *, B, T, nc, D, col0, tm):
    N = T - nc
    ntl, nt, nct, cb = N // tm, T // tm, nc // tm, col0 // D
    assert col0 % (2 * D) == 0

    def body(mgd_ref, mgl_ref, yd_ref, yl_ref, bm_ref, dm_ref, dp_any, dyd_ref, dyl_ref, dp_ref, sums_ref):
        _, vjp = jax.vjp(_merge_math, mgd_ref[...], mgl_ref[...], yd_ref[...], yl_ref[...],
                         bm_ref[:, 0:D], bm_ref[:, D:2 * D])
        dmgd, dmgl, dyd, dyl, dbd, dbl = vjp(dm_ref[...])
        dyd_ref[...] = dyd.astype(BF16)
        dyl_ref[...] = dyl.astype(BF16)
        dp_ref[:, 0:D] = dmgd.astype(BF16)
        dp_ref[:, D:2 * D] = dmgl.astype(BF16)

        @pl.when((pl.program_id(0) == 0) & (pl.program_id(1) == 0))
        def _():
            sums_ref[...] = jnp.zeros_like(sums_ref)

        sums_ref[0:1, 0:D] += dbd
        sums_ref[0:1, D:2 * D] += dbl

    prow = lambda b, t: b * nt + nct + t
    row = pl.BlockSpec((tm, D), lambda b, t: (b * ntl + t, 0))
    return pl.pallas_call(
        body, name="merge_bwd", grid=(B, ntl),
        in_specs=[pl.BlockSpec((tm, D), lambda b, t: (prow(b, t), cb)),
                  pl.BlockSpec((tm, D), lambda b, t: (prow(b, t), cb + 1)),
                  row, row, pl.BlockSpec((1, 2 * D), lambda b, t: (0, 0)), row,
                  pl.BlockSpec(memory_space=pl.ANY)],
        out_specs=[row, row,
                   pl.BlockSpec((tm, 2 * D), lambda b, t: (prow(b, t), cb // 2)),
                   pl.BlockSpec((8, 2 * D), lambda b, t: (0, 0))],
        out_shape=[jax.ShapeDtypeStruct((B * N, D), BF16), jax.ShapeDtypeStruct((B * N, D), BF16),
                   jax.ShapeDtypeStruct(dp.shape, dp.dtype), jax.ShapeDtypeStruct((8, 2 * D), F32)],
        input_output_aliases={6: 2},
        compiler_params=_params(("arbitrary", "arbitrary")),
    )(p, p, ydn, ylru, b_merge, dmix, dp)


def _post_math(x, mix, g1, gate, g2, sh, sc):
    h1 = x + _rmsn(mix, g1) * gate
    return h1, _rmsn(h1, g2) * (1.0 + sc) + sh


def _post_fwd(x, mix, gains, vecs, *, tm):
    B, N, D = x.shape
    ntl = N // tm

    def body(x_ref, mix_ref, g_ref, v_ref, h1_ref, u2_ref):
        v = v_ref[0]
        h1, u2 = _post_math(x_ref[0], mix_ref[...], g_ref[0:1], v[0:1], g_ref[1:2], v[1:2], v[2:3])
        h1_ref[...] = h1
        u2_ref[...] = u2.astype(BF16)

    row = pl.BlockSpec((tm, D), lambda b, t: (b * ntl + t, 0))
    return pl.pallas_call(
        body, name="post_fwd", grid=(B, ntl),
        in_specs=[pl.BlockSpec((1, tm, D), lambda b, t: (b, t, 0)), row,
                  pl.BlockSpec((8, D), lambda b, t: (0, 0)), pl.BlockSpec((1, 8, D), lambda b, t: (b, 0, 0))],
        out_specs=[row, row],
        out_shape=[jax.ShapeDtypeStruct((B * N, D), F32), jax.ShapeDtypeStruct((B * N, D), BF16)],
        compiler_params=_params(("parallel", "parallel")),
    )(x, mix, gains, vecs)


def _post_bwd(x, mix, gains, vecs, dh1, du2, *, tm):
    B, N, D = x.shape
    ntl = N // tm

    def body(x_ref, mix_ref, g_ref, v_ref, dh1_ref, du2_ref, dx_ref, dmix_ref, sums_ref):
        v = v_ref[0]
        _, vjp = jax.vjp(_post_math, x_ref[0], mix_ref[...], g_ref[0:1], v[0:1], g_ref[1:2], v[1:2], v[2:3])
        dx, dmix, dg1, dgate, dg2, dsh, dsc = vjp((dh1_ref[...], du2_ref[...]))
        dx_ref[...] = dx
        dmix_ref[...] = dmix.astype(BF16)

        @pl.when(pl.program_id(1) == 0)
        def _():
            sums_ref[...] = jnp.zeros_like(sums_ref)

        sums_ref[0, 0:1, :] += dgate
        sums_ref[0, 1:2, :] += dsh
        sums_ref[0, 2:3, :] += dsc
        sums_ref[0, 3:4, :] += dg1
        sums_ref[0, 4:5, :] += dg2

    row = pl.BlockSpec((tm, D), lambda b, t: (b * ntl + t, 0))
    return pl.pallas_call(
        body, name="post_bwd", grid=(B, ntl),
        in_specs=[pl.BlockSpec((1, tm, D), lambda b, t: (b, t, 0)), row,
                  pl.BlockSpec((8, D), lambda b, t: (0, 0)), pl.BlockSpec((1, 8, D), lambda b, t: (b, 0, 0)), row, row],
        out_specs=[row, row, pl.BlockSpec((1, 8, D), lambda b, t: (b, 0, 0))],
        out_shape=[jax.ShapeDtypeStruct((B * N, D), F32), jax.ShapeDtypeStruct((B * N, D), BF16),
                   jax.ShapeDtypeStruct((B, 8, D), F32)],
        compiler_params=_params(("parallel", "arbitrary")),
    )(x, mix, gains, vecs, dh1, du2)


def _final_math(dn, g4, gate5):
    return _rmsn(dn, g4) * gate5


def _final(h1, dn, target, gains, vecs, *, tm):
    B, N, D = target.shape
    ntl = N // tm

    def body(h1_ref, dn_ref, t_ref, g_ref, v_ref, ddn_ref, dout_ref, sums_ref):
        v = v_ref[0]
        y, vjp = jax.vjp(_final_math, dn_ref[...], g_ref[2:3], v[3:4])
        err = h1_ref[...] + y - t_ref[0]
        dout = err * (1.0 / D)
        ddn, dg4, dgate5 = vjp(dout)
        ddn_ref[...] = ddn.astype(BF16)
        dout_ref[...] = dout

        @pl.when(pl.program_id(1) == 0)
        def _():
            sums_ref[...] = jnp.zeros_like(sums_ref)

        sums_ref[0, 0:1, :] += dgate5
        sums_ref[0, 1:2, :] += dg4
        sums_ref[0, 2:3, :] += jnp.sum(err * err, axis=0, keepdims=True) * (0.5 / D)

    row = pl.BlockSpec((tm, D), lambda b, t: (b * ntl + t, 0))
    return pl.pallas_call(
        body, name="final", grid=(B, ntl),
        in_specs=[row, row, pl.BlockSpec((1, tm, D), lambda b, t: (b, t, 0)),
                  pl.BlockSpec((8, D), lambda b, t: (0, 0)), pl.BlockSpec((1, 8, D), lambda b, t: (b, 0, 0))],
        out_specs=[row, row, pl.BlockSpec((1, 8, D), lambda b, t: (b, 0, 0))],
        out_shape=[jax.ShapeDtypeStruct((B * N, D), BF16), jax.ShapeDtypeStruct((B * N, D), F32),
                   jax.ShapeDtypeStruct((B, 8, D), F32)],
        compiler_params=_params(("parallel", "arbitrary")),
    )(h1, dn, target, gains, vecs)


def _shift(x, s):
    s = s % x.shape[0]
    return x if s == 0 else pltpu.roll(x, s, 0)


def _seg_taps(T, nc, width, pad_left):
    t = lax.broadcasted_iota(jnp.int32, (T, 1), 0)
    pos = jnp.where(t < nc, t, t - nc)
    seg = jnp.where(t < nc, nc, T - nc)
    taps = []
    for k in range(width):
        src = pos + (k - pad_left)
        taps.append((pad_left - k, (src >= 0) & (src < seg)))
    return taps


def _grid_taps(N):
    t = lax.broadcasted_iota(jnp.int32, (N, 1), 0)
    wcol = t % GRID_W
    taps = []
    for dr in (-1, 0, 1):
        for dw in (-1, 0, 1):
            off = dr * GRID_W + dw
            ok = (wcol + dw >= 0) & (wcol + dw < GRID_W) & (t + dr * GRID_W >= 0) & (t + dr * GRID_W < N)
            taps.append((-off, ok))
    return taps


def _conv_fwd(x, w, taps):
    y = jnp.zeros_like(x)
    for k, (s, m) in enumerate(taps):
        y = y + w[k:k + 1] * jnp.where(m, _shift(x, s), 0.0)
    return y


def _conv_bwd(x, w, taps, dy):
    dx = jnp.zeros_like(x)
    dws = []
    for k, (s, m) in enumerate(taps):
        dym = jnp.where(m, dy, 0.0)
        dx = dx + w[k:k + 1] * _shift(dym, -s)
        dws.append(jnp.sum(dym * _shift(x, s), axis=0, keepdims=True))
    return dx, jnp.concatenate(dws, axis=0)


def _ffn_act_fwd(F, w9, bias, *, B, N, DFF, tc):
    nj = DFF // tc

    def body(fg_ref, fv_ref, w_ref, b_ref, o_ref):
        fg = _conv_fwd(fg_ref[...], w_ref[...], _grid_taps(N)) + b_ref[...]
        o_ref[...] = (_gelu(fg) * fv_ref[...]).astype(BF16)

    return pl.pallas_call(
        body, name="ffn_act_fwd", grid=(B, nj),
        in_specs=[pl.BlockSpec((N, tc), lambda b, j: (b, j)), pl.BlockSpec((N, tc), lambda b, j: (b, nj + j)),
                  pl.BlockSpec((9, tc), lambda b, j: (0, j)), pl.BlockSpec((1, tc), lambda b, j: (0, j))],
        out_specs=pl.BlockSpec((N, tc), lambda b, j: (b, j)),
        out_shape=jax.ShapeDtypeStruct((B * N, DFF), BF16),
        compiler_params=_params(("parallel", "parallel")),
    )(F, F, w9, bias)


def _ffn_act_bwd(F, w9, bias, df, *, B, N, DFF, tc):
    nj = DFF // tc

    def body(fg_ref, fv_ref, w_ref, b_ref, df_ref, dfg_ref, dfv_ref, dwb_ref):
        taps = _grid_taps(N)
        x = fg_ref[...]
        fg, vjp = jax.vjp(lambda a: _gelu(a), _conv_fwd(x, w_ref[...], taps) + b_ref[...])
        dfl = df_ref[...]
        dfv_ref[...] = (dfl * fg).astype(BF16)
        (dpre,) = vjp(dfl * fv_ref[...])
        dx, dw = _conv_bwd(x, w_ref[...], taps, dpre)
        dfg_ref[...] = dx.astype(BF16)

        @pl.when(pl.program_id(1) == 0)
        def _():
            dwb_ref[...] = jnp.zeros_like(dwb_ref)

        dwb_ref[0:9, :] += dw
        dwb_ref[9:10, :] += jnp.sum(dpre, axis=0, keepdims=True)

    col = pl.BlockSpec((N, tc), lambda j, b: (b, j))
    return pl.pallas_call(
        body, name="ffn_act_bwd", grid=(nj, B),
        in_specs=[col, pl.BlockSpec((N, tc), lambda j, b: (b, nj + j)),
                  pl.BlockSpec((9, tc), lambda j, b: (0, j)), pl.BlockSpec((1, tc), lambda j, b: (0, j)), col],
        out_specs=[col, col, pl.BlockSpec((16, tc), lambda j, b: (0, j))],
        out_shape=[jax.ShapeDtypeStruct((B * N, DFF), BF16), jax.ShapeDtypeStruct((B * N, DFF), BF16),
                   jax.ShapeDtypeStruct((16, DFF), F32)],
        compiler_params=_params(("parallel", "arbitrary")),
    )(F, F, w9, bias, df)


def _dnprep_math(y, is_qk, scale):
    s = _silu(y)
    n = s * lax.rsqrt(jnp.sum(s * s, axis=-1, keepdims=True) + EPS) * scale
    return jnp.where(is_qk, n, s)


def _dnprep_fwd(p, cw, *, B, T, nc, H, HD):
    def body(x_ref, w_ref, o_ref):
        j = pl.program_id(1)
        y = _conv_fwd(x_ref[...], w_ref[...], _seg_taps(T, nc, 4, 2))
        o_ref[...] = _dnprep_math(y, j < 2 * H, jnp.where(j < H, HD ** -0.5, 1.0))

    return pl.pallas_call(
        body, name="dnprep_fwd", grid=(B, 3 * H),
        in_specs=[pl.BlockSpec((T, HD), lambda b, j: (b, j)), pl.BlockSpec((4, HD), lambda b, j: (0, j))],
        out_specs=pl.BlockSpec((T, HD), lambda b, j: (b, j)),
        out_shape=jax.ShapeDtypeStruct((B * T, 3 * H * HD), F32),
        compiler_params=_params(("parallel", "parallel")),
    )(p, cw)


def _dnprep_bwd(p, cw, dqkv, dp, *, B, T, nc, H, HD):
    def body(x_ref, w_ref, d_ref, dp_any, dp_ref, dcw_ref):
        j = pl.program_id(0)
        taps = _seg_taps(T, nc, 4, 2)
        x = x_ref[...]
        y = _conv_fwd(x, w_ref[...], taps)
        is_qk, scale = j < 2 * H, jnp.where(j < H, HD ** -0.5, 1.0)
        _, vjp = jax.vjp(lambda a: _dnprep_math(a, is_qk, scale), y)
        (dy,) = vjp(d_ref[0])
        dx, dw = _conv_bwd(x, w_ref[...], taps, dy)
        dp_ref[...] = dx.astype(BF16)

        @pl.when(pl.program_id(1) == 0)
        def _():
            dcw_ref[...] = jnp.zeros_like(dcw_ref)

        dcw_ref[0:4, :] += dw

    col = pl.BlockSpec((T, HD), lambda j, b: (b, j))
    return pl.pallas_call(
        body, name="dnprep_bwd", grid=(3 * H, B),
        in_specs=[col, pl.BlockSpec((4, HD), lambda j, b: (0, j)),
                  pl.BlockSpec((1, T, HD), lambda j, b: (j // H, b, j % H)), pl.BlockSpec(memory_space=pl.ANY)],
        out_specs=[col, pl.BlockSpec((8, HD), lambda j, b: (0, j))],
        out_shape=[jax.ShapeDtypeStruct(dp.shape, dp.dtype), jax.ShapeDtypeStruct((8, 3 * H * HD), F32)],
        input_output_aliases={3: 0},
        compiler_params=_params(("parallel", "arbitrary")),
    )(p, cw, dqkv, dp)


def _gb_math(ab, alog, dtb, H):
    lane = lax.broadcasted_iota(jnp.int32, ab.shape, 1)
    g = -jnp.exp(alog) * _softplus(ab + dtb)
    return jnp.where(lane < 2 * H, g, jnp.where(lane < 4 * H, _sigmoid(ab), 0.0))


def _gb_fwd(p, prm, *, rows, col0, H, tm):
    def body(x_ref, prm_ref, o_ref):
        o_ref[...] = _gb_math(x_ref[...], prm_ref[0:1], prm_ref[1:2], H)

    return pl.pallas_call(
        body, name="gb_fwd", grid=(rows // tm,),
        in_specs=[pl.BlockSpec((tm, LANES), lambda t: (t, col0 // LANES)), pl.BlockSpec((8, LANES), lambda t: (0, 0))],
        out_specs=pl.BlockSpec((tm, LANES), lambda t: (t, 0)),
        out_shape=jax.ShapeDtypeStruct((rows, LANES), F32),
        compiler_params=_params(("parallel",)),
    )(p, prm)


def _gb_bwd(p, prm, dgb, dp, *, rows, col0, H, tm):
    def body(x_ref, prm_ref, d_ref, dp_any, dp_ref, dprm_ref):
        _, vjp = jax.vjp(lambda a, b, c: _gb_math(a, b, c, H), x_ref[...], prm_ref[0:1], prm_ref[1:2])
        dab, dalog, ddtb = vjp(d_ref[...])
        dp_ref[...] = dab.astype(BF16)

        @pl.when(pl.program_id(0) == 0)
        def _():
            dprm_ref[...] = jnp.zeros_like(dprm_ref)

        dprm_ref[0:1, :] += dalog
        dprm_ref[1:2, :] += ddtb

    blk = pl.BlockSpec((tm, LANES), lambda t: (t, col0 // LANES))
    return pl.pallas_call(
        body, name="gb_bwd", grid=(rows // tm,),
        in_specs=[blk, pl.BlockSpec((8, LANES), lambda t: (0, 0)), pl.BlockSpec((tm, LANES), lambda t: (t, 0)),
                  pl.BlockSpec(memory_space=pl.ANY)],
        out_specs=[blk, pl.BlockSpec((8, LANES), lambda t: (0, 0))],
        out_shape=[jax.ShapeDtypeStruct(dp.shape, dp.dtype), jax.ShapeDtypeStruct((8, LANES), F32)],
        input_output_aliases={3: 0},
        compiler_params=_params(("arbitrary",)),
    )(p, prm, dgb, dp)


def _lru_scan(a_ref, b_ref, h_ref, hp_ref, segs):
    C = a_ref.shape[1]
    row = lax.broadcasted_iota(jnp.int32, (SUBLANES, C), 0)
    carry = jnp.zeros((1, C), F32)
    for start, rows, reverse in segs:
        nb = rows // SUBLANES

        def blk(i, carry, start=start, nb=nb, reverse=reverse):
            r0 = pl.multiple_of(start + (nb - 1 - i if reverse else i) * SUBLANES, SUBLANES)
            A = a_ref[pl.ds(r0, SUBLANES), :]
            Bv = b_ref[pl.ds(r0, SUBLANES), :]
            for s in (1, 2, 4):
                sh = SUBLANES - s if reverse else s
                m = (row < SUBLANES - s) if reverse else (row >= s)
                Bv = jnp.where(m, A * pltpu.roll(Bv, sh, 0) + Bv, Bv)
                A = jnp.where(m, A * pltpu.roll(A, sh, 0), A)
            Hv = Bv + A * carry
            h_ref[pl.ds(r0, SUBLANES), :] = Hv
            if hp_ref is not None:
                if reverse:
                    hp = jnp.where(row < SUBLANES - 1, pltpu.roll(Hv, SUBLANES - 1, 0), carry)
                else:
                    hp = jnp.where(row >= 1, pltpu.roll(Hv, 1, 0), carry)
                hp_ref[pl.ds(r0, SUBLANES), :] = hp
            return Hv[0:1] if reverse else Hv[SUBLANES - 1:SUBLANES]

        carry = lax.fori_loop(0, nb, blk, carry)


def _lru_orders(T, nc, d):
    if d == 0:
        return [(0, T, False)], [(0, T, True)]
    return [(0, nc, True), (nc, T - nc, True)], [(nc, T - nc, False), (0, nc, False)]


def _bdot(a, b, dims=(((1,), (0,)), ((), ()))):
    return lax.dot_general(a.astype(BF16), b.astype(BF16), dims, preferred_element_type=F32)


_NT = (((1,), (1,)), ((), ()))
_TN = (((0,), (0,)), ((), ()))


def _blockdiag(w, C):
    nd, nb, bd, _ = w.shape
    per = C // bd
    out = jnp.einsum('dnpij,pq->dnpiqj', w.reshape(nd, nb // per, per, bd, bd), jnp.eye(per, dtype=w.dtype))
    return out.reshape(nd, nb // per, C, C)


def _blockdiag_extract(dw, bd):
    nd, nj, C, _ = dw.shape
    per = C // bd
    out = jnp.einsum('dnpiqj,pq->dnpij', dw.reshape(nd, nj, per, bd, per, bd), jnp.eye(per, dtype=dw.dtype))
    return out.reshape(nd, nj * per, bd, bd)


def _lru_fwd(p, cw, lv, wr, wi, *, B, T, nc, LW, col0, C):
    N = T - nc
    nj = LW // C

    def body(x_ref, cw_ref, lv_ref, wr_ref, wi_ref, o_ref, a_s, b_s, h_s, acc_s):
        lv_ = lv_ref[...]
        xc = _conv_fwd(x_ref[:, 0:C], cw_ref[...], _seg_taps(T, nc, 4, 2)) + lv_[0:1]
        for d in (0, 1):
            r = _sigmoid(_bdot(xc, wr_ref[d, 0]) + lv_[1 + d:2 + d])
            i = _sigmoid(_bdot(xc, wi_ref[d, 0]) + lv_[3 + d:4 + d])
            la = -LRU_C * r * _softplus(-lv_[5 + d:6 + d])
            a_s[...] = jnp.exp(la)
            b_s[...] = jnp.sqrt(1.0 - jnp.exp(2.0 * la)) * i * xc
            _lru_scan(a_s, b_s, h_s, None, _lru_orders(T, nc, d)[0])
            if d == 0:
                acc_s[...] = h_s[...]
            else:
                acc_s[...] += h_s[...]
        o_ref[...] = (acc_s[nc:, :] * _gelu(x_ref[nc:, C:2 * C])).astype(BF16)

    return pl.pallas_call(
        body, name="lru_fwd", grid=(B, nj),
        in_specs=[pl.BlockSpec((T, 2 * C), lambda b, j: (b, col0 // (2 * C) + j)),
                  pl.BlockSpec((4, C), lambda b, j: (0, j)), pl.BlockSpec((8, C), lambda b, j: (0, j)),
                  pl.BlockSpec((2, 1, C, C), lambda b, j: (0, j, 0, 0)), pl.BlockSpec((2, 1, C, C), lambda b, j: (0, j, 0, 0))],
        out_specs=pl.BlockSpec((N, C), lambda b, j: (b, j)),
        out_shape=jax.ShapeDtypeStruct((B * N, LW), BF16),
        scratch_shapes=[pltpu.VMEM((T, C), F32)] * 4,
        compiler_params=_params(("parallel", "parallel")),
    )(p, cw, lv, wr, wi)


def _lru_bwd(p, cw, lv, wr, wi, dy, dp, *, B, T, nc, LW, col0, C):
    N = T - nc
    nj = LW // C

    def body(x_ref, cw_ref, lv_ref, wr_ref, wi_ref, dy_ref, dp_any, dp_ref, dcw_ref, dlv_ref, dwr_ref, dwi_ref,
             a_s, b_s, h_s, hp_s, mu_s, mup_s, dh_s, dxc_s, hsum_s):
        taps = _seg_taps(T, nc, 4, 2)
        lv_ = lv_ref[...]
        xl = x_ref[:, 0:C]
        xc = _conv_fwd(xl, cw_ref[...], taps) + lv_[0:1]
        gel, gelu_vjp = jax.vjp(_gelu, x_ref[nc:, C:2 * C])
        dh_s[0:nc, :] = jnp.zeros((nc, C), F32)
        dh_s[nc:, :] = dy_ref[...] * gel
        dxc_s[...] = jnp.zeros_like(dxc_s)

        @pl.when(pl.program_id(1) == 0)
        def _():
            dcw_ref[...] = jnp.zeros_like(dcw_ref)
            dlv_ref[...] = jnp.zeros_like(dlv_ref)
            dwr_ref[...] = jnp.zeros_like(dwr_ref)
            dwi_ref[...] = jnp.zeros_like(dwi_ref)

        for d in (0, 1):
            fwd_order, adj_order = _lru_orders(T, nc, d)
            lam = lv_[5 + d:6 + d]
            r = _sigmoid(_bdot(xc, wr_ref[d, 0]) + lv_[1 + d:2 + d])
            i = _sigmoid(_bdot(xc, wi_ref[d, 0]) + lv_[3 + d:4 + d])
            sp = _softplus(-lam)
            la = -LRU_C * r * sp
            a = jnp.exp(la)
            e2 = jnp.exp(2.0 * la)
            mult = jnp.sqrt(1.0 - e2)
            a_s[...] = a
            b_s[...] = mult * i * xc
            _lru_scan(a_s, b_s, h_s, hp_s, fwd_order)
            if d == 0:
                hsum_s[...] = h_s[...]
            else:
                hsum_s[...] += h_s[...]
            b_s[...] = a * dh_s[...]
            _lru_scan(a_s, b_s, mu_s, mup_s, adj_order)
            dinp = dh_s[...] + mup_s[...]
            da = dinp * hp_s[...]
            dmult = dinp * i * xc
            di = dinp * mult * xc
            dla = da * a - dmult * e2 / mult
            dpre_r = (dla * (-LRU_C * sp)) * r * (1.0 - r)
            dpre_i = di * i * (1.0 - i)
            dsp = jnp.sum(dla * (-LRU_C * r), axis=0, keepdims=True)
            dxc_s[...] += dinp * mult * i + _bdot(dpre_r, wr_ref[d, 0], _NT) + _bdot(dpre_i, wi_ref[d, 0], _NT)
            dwr_ref[d, 0] += _bdot(xc, dpre_r, _TN)
            dwi_ref[d, 0] += _bdot(xc, dpre_i, _TN)
            dlv_ref[1 + d:2 + d, :] += jnp.sum(dpre_r, axis=0, keepdims=True)
            dlv_ref[3 + d:4 + d, :] += jnp.sum(dpre_i, axis=0, keepdims=True)
            dlv_ref[5 + d:6 + d, :] += -dsp * _sigmoid(-lam)

        dxc = dxc_s[...]
        dxl, dw = _conv_bwd(xl, cw_ref[...], taps, dxc)
        dcw_ref[0:4, :] += dw
        dlv_ref[0:1, :] += jnp.sum(dxc, axis=0, keepdims=True)
        dp_ref[:, 0:C] = dxl.astype(BF16)
        (dyl,) = gelu_vjp(dy_ref[...] * hsum_s[nc:, :])
        dp_ref[0:nc, C:2 * C] = jnp.zeros((nc, C), BF16)
        dp_ref[nc:, C:2 * C] = dyl.astype(BF16)

    xblk = pl.BlockSpec((T, 2 * C), lambda j, b: (b, col0 // (2 * C) + j))
    wblk = pl.BlockSpec((2, 1, C, C), lambda j, b: (0, j, 0, 0))
    vblk = pl.BlockSpec((8, C), lambda j, b: (0, j))
    return pl.pallas_call(
        body, name="lru_bwd", grid=(nj, B),
        in_specs=[xblk, pl.BlockSpec((4, C), lambda j, b: (0, j)), vblk, wblk, wblk,
                  pl.BlockSpec((N, C), lambda j, b: (b, j)), pl.BlockSpec(memory_space=pl.ANY)],
        out_specs=[xblk, vblk, vblk, wblk, wblk],
        out_shape=[jax.ShapeDtypeStruct(dp.shape, dp.dtype), jax.ShapeDtypeStruct((8, LW), F32),
                   jax.ShapeDtypeStruct((8, LW), F32), jax.ShapeDtypeStruct((2, nj, C, C), F32),
                   jax.ShapeDtypeStruct((2, nj, C, C), F32)],
        scratch_shapes=[pltpu.VMEM((T, C), F32)] * 9,
        input_output_aliases={6: 0},
        compiler_params=_params(("parallel", "arbitrary")),
    )(p, cw, lv, wr, wi, dy, dp)


def _chunk_masks(upper):
    i = lax.broadcasted_iota(jnp.int32, (CHUNK, CHUNK), 0)
    j = lax.broadcasted_iota(jnp.int32, (CHUNK, CHUNK), 1)
    return i == j, (j >= i) if upper else (j <= i), (j > i) if upper else (j < i)


def _col2row(c, eye):
    return jnp.sum(jnp.where(eye, c, 0.0), axis=0, keepdims=True)


def _row2col(r, eye):
    return jnp.sum(jnp.where(eye, r, 0.0), axis=1, keepdims=True)


def _rowsum(x):
    return jnp.sum(x, axis=1, keepdims=True)


def _unit_tri_inverses(Ls):
    G = len(Ls)
    W = G * CHUNK
    blk = (lax.broadcasted_iota(jnp.int32, (W, W), 0) // CHUNK) == (lax.broadcasted_iota(jnp.int32, (W, W), 1) // CHUNK)

    def pdot(a, b):
        bd = jnp.where(blk, jnp.tile(b.astype(BF16), (G, 1)), jnp.zeros((), BF16))
        return jnp.dot(a.astype(BF16), bd, preferred_element_type=F32)

    Xp = -(Ls[0] if G == 1 else jnp.concatenate(Ls, axis=1))
    Rm = Xp
    for _ in range(int(math.log2(CHUNK)) - 1):
        Xp = pdot(Xp, Xp)
        Rm = Rm + Xp + pdot(Rm, Xp)
    eye = _chunk_masks(False)[0]
    return [jnp.where(eye, 1.0, 0.0) + Rm[:, g * CHUNK:(g + 1) * CHUNK] for g in range(G)]


def _delta_chunk_common(q, k, v, gcol, bcol, upper):
    eye, incl, strict = _chunk_masks(upper)
    gc = _rowsum(jnp.where(incl, _col2row(gcol, eye), 0.0))
    D = jnp.where(incl, jnp.exp(jnp.minimum(gc - _col2row(gc, eye), 0.0)), 0.0)
    kb = k * bcol
    AP = _bdot(jnp.concatenate([kb, q], axis=0), k, _NT)
    A = AP[:CHUNK]
    L = jnp.where(strict, A * D, 0.0)
    eg = jnp.exp(gc)
    gl = jnp.sum(gcol, axis=0, keepdims=True)
    attn = jnp.where(incl, AP[CHUNK:] * D, 0.0)
    return dict(eye=eye, incl=incl, strict=strict, gc=gc, D=D, kb=kb, A=A, L=L, eg=eg, gl=gl, egl=jnp.exp(gl),
                attn=attn, kbe=kb * eg, vb=v * bcol, qe=q * eg, kd=k * jnp.exp(gl - gc))


def _delta_group_pre(chunks, upper):
    cs = [_delta_chunk_common(*ch, upper) for ch in chunks]
    out = []
    for c, Tm in zip(cs, _unit_tri_inverses([c["L"] for c in cs])):
        dk = c["kbe"].shape[1]
        wu = _bdot(Tm, jnp.concatenate([c["kbe"], c["vb"]], axis=1))
        KN = _bdot(c["kd"], wu, _TN)
        QO = _bdot(c["attn"], wu)
        out.append((Tm, KN[:, :dk], KN[:, dk:], c["qe"] - QO[:, :dk], QO[:, dk:], c["egl"]))
    return out


def _delta_chunk_bwd(q, k, v, gcol, bcol, S, Tm, do, dS2, upper):
    c = _delta_chunk_common(q, k, v, gcol, bcol, upper)
    eye, incl, strict, D, eg, egl = c["eye"], c["incl"], c["strict"], c["D"], c["eg"], c["egl"]
    kb, kbe, vb, qe, kd, attn = c["kb"], c["kbe"], c["vb"], c["qe"], c["kd"], c["attn"]
    dkk = kbe.shape[1]
    wu = _bdot(Tm, jnp.concatenate([kbe, vb], axis=1))
    w = wu[:, :dkk]
    vn = wu[:, dkk:] - _bdot(w, S)
    dvn = _bdot(kd, dS2) + _bdot(attn, do, _TN)
    dkd = _bdot(vn, dS2, _NT)
    dgl = jnp.sum(_rowsum(dS2 * S), axis=0, keepdims=True) * egl
    dqa = _bdot(do, jnp.concatenate([S, vn], axis=0), _NT)
    dqe = dqa[:, :dkk]
    dattn = jnp.where(incl, dqa[:, dkk:], 0.0)
    dw = -_bdot(dvn, S, _NT)
    r = _rowsum(dkd * kd)
    dk = dkd * jnp.exp(c["gl"] - c["gc"])
    dgl = dgl + jnp.sum(r, axis=0, keepdims=True)
    dgc = _rowsum(dqe * qe) - r
    dq = dqe * eg + _bdot(dattn * D, k)
    dk = dk + _bdot(dattn * D, q, _TN)
    E = dattn * attn
    dvw = jnp.concatenate([dvn, dw], axis=1)
    dTm = _bdot(dvw, jnp.concatenate([vb, kbe], axis=1), _NT)
    dvk = _bdot(Tm, dvw, _TN)
    dvb = dvk[:, :dvn.shape[1]]
    dv = dvb * bcol
    dbeta = _rowsum(dvb * v)
    dkbe = dvk[:, dvn.shape[1]:]
    dkb = dkbe * eg
    dgc = dgc + _rowsum(dkbe * kbe)
    dL = jnp.where(strict, -_bdot(Tm, _bdot(dTm, Tm, _NT), _TN), 0.0)
    dA = dL * D
    E = E + dL * c["L"]
    dkb = dkb + _bdot(dA, k)
    dk = dk + _bdot(dA, kb, _TN) + dkb * bcol
    dbeta = dbeta + _rowsum(dkb * k)
    dgc = dgc + _rowsum(E) - _row2col(jnp.sum(E, axis=0, keepdims=True), eye)
    dg = _row2col(jnp.sum(jnp.where(incl, dgc, 0.0), axis=0, keepdims=True), eye) + dgl
    return dq, dk, dv, dg, dbeta


def _delta_group(n):
    return max(g for g in (4, 3, 2, 1) if n % g == 0)


def _delta_chunk_at(T, nc, d, i):
    n, ncc = T // CHUNK, nc // CHUNK
    return i if d == 0 else jnp.where(i < ncc, ncc - 1 - i, n - 1 - (i - ncc))


def _dn_out_math(o, onorm, z):
    return _rmsn(o, onorm) * _silu(z)


def _delta_fwd(qkv, gb, p, onorm, *, B, T, nc, H, HD):
    N = T - nc
    n = T // CHUNK
    G = _delta_group(n)

    def body(q_ref, k_ref, v_ref, gb_ref, z_ref, on_ref, y_ref, o_ref, K_s, N_s, Qp_s, O0_s, eg_s, o_s):
        h = pl.program_id(1)
        lane = lax.broadcasted_iota(jnp.int32, (CHUNK, LANES), 1)

        def pre(g, carry):
            cs = [g * G + i for i in range(G)]
            rows = [pl.ds(pl.multiple_of(c * CHUNK, CHUNK), CHUNK) for c in cs]
            for d in (0, 1):
                chunks = []
                for r in rows:
                    gbb = gb_ref[r, :]
                    chunks.append((q_ref[r, :], k_ref[r, :], v_ref[r, :],
                                   _rowsum(jnp.where(lane == d * H + h, gbb, 0.0)),
                                   _rowsum(jnp.where(lane == 2 * H + d * H + h, gbb, 0.0))))
                for c, r, (_, K, Nn, Qp, O0, egl) in zip(cs, rows, _delta_group_pre(chunks, d == 1)):
                    K_s[d * n + c] = K.astype(BF16)
                    N_s[d * n + c] = Nn
                    Qp_s[d, r, :] = Qp.astype(BF16)
                    O0_s[d, r, :] = O0
                    eg_s[d * n + c] = jnp.broadcast_to(egl, (SUBLANES, HD))
            return carry

        lax.fori_loop(0, n // G, pre, 0)

        def step(i, Ss):
            out = []
            for d in (0, 1):
                c = _delta_chunk_at(T, nc, d, i)
                rows = pl.ds(pl.multiple_of(c * CHUNK, CHUNK), CHUNK)
                Sb = Ss[d].astype(BF16)
                o_s[d, rows, :] = jnp.dot(Qp_s[d, rows, :], Sb, preferred_element_type=F32) + O0_s[d, rows, :]
                out.append(eg_s[d * n + c][0:1] * Ss[d] + N_s[d * n + c]
                           - jnp.dot(K_s[d * n + c], Sb, preferred_element_type=F32))
            return tuple(out)

        lax.fori_loop(0, n, step, (jnp.zeros((HD, HD), F32), jnp.zeros((HD, HD), F32)))
        o = o_s[0, nc:, :] + o_s[1, nc:, :]
        o_ref[...] = o
        y_ref[...] = _dn_out_math(o, on_ref[...], z_ref[nc:, :]).astype(BF16)

    col = lambda off: pl.BlockSpec((T, HD), lambda b, h: (b, off + h))
    lat = pl.BlockSpec((N, HD), lambda b, h: (b, h))
    return pl.pallas_call(
        body, name="delta_fwd", grid=(B, H),
        in_specs=[col(0), col(H), col(2 * H), pl.BlockSpec((T, LANES), lambda b, h: (b, 0)), col(3 * H),
                  pl.BlockSpec((1, HD), lambda b, h: (0, 0))],
        out_specs=[lat, lat],
        out_shape=[jax.ShapeDtypeStruct((B * N, H * HD), BF16), jax.ShapeDtypeStruct((B * N, H * HD), F32)],
        scratch_shapes=[pltpu.VMEM((2 * n, HD, HD), BF16), pltpu.VMEM((2 * n, HD, HD), F32),
                        pltpu.VMEM((2, T, HD), BF16), pltpu.VMEM((2, T, HD), F32),
                        pltpu.VMEM((2 * n, SUBLANES, HD), F32), pltpu.VMEM((2, T, HD), F32)],
        compiler_params=_params(("parallel", "parallel")),
    )(qkv, qkv, qkv, gb, p, onorm)


def _delta_bwd(qkv, gb, p, onorm, o, dy, dp, *, B, T, nc, H, HD):
    N = T - nc
    n = T // CHUNK
    G = _delta_group(n)

    def body(q_ref, k_ref, v_ref, gb_ref, z_ref, on_ref, o_ref, dy_ref, dp_any, dqkv_ref, dgb_ref, dp_ref, don_ref,
             do_s, Tm_s, K_s, N_s, R_s, eg_s, S_s, dS_s):
        h = pl.program_id(1)
        lane = lax.broadcasted_iota(jnp.int32, (CHUNK, LANES), 1)

        def cols(rows, d):
            gbb = gb_ref[rows, :]
            return (_rowsum(jnp.where(lane == d * H + h, gbb, 0.0)),
                    _rowsum(jnp.where(lane == 2 * H + d * H + h, gbb, 0.0)))

        _, vjp = jax.vjp(_dn_out_math, o_ref[...], on_ref[...], z_ref[nc:, :])
        do, don, dz = vjp(dy_ref[...])
        do_s[0:nc, :] = jnp.zeros((nc, HD), F32)
        do_s[nc:, :] = do
        dp_ref[0:nc, :] = jnp.zeros((nc, HD), BF16)
        dp_ref[nc:, :] = dz.astype(BF16)

        @pl.when(h == 0)
        def _():
            don_ref[...] = jnp.zeros_like(don_ref)
            dgb_ref[...] = jnp.zeros_like(dgb_ref)

        don_ref[0, 0:1, :] += don

        for d in (0, 1):
            def pre(g, carry, d=d):
                cs = [g * G + i for i in range(G)]
                rows = [pl.ds(pl.multiple_of(c * CHUNK, CHUNK), CHUNK) for c in cs]
                chunks = [(q_ref[r, :], k_ref[r, :], v_ref[r, :], *cols(r, d)) for r in rows]
                for c, r, (Tm, K, Nn, Qp, _, egl) in zip(cs, rows, _delta_group_pre(chunks, d == 1)):
                    Tm_s[c] = Tm
                    K_s[c] = K.astype(BF16)
                    N_s[c] = Nn
                    R_s[c] = _bdot(Qp, do_s[r, :], _TN)
                    eg_s[c] = jnp.broadcast_to(egl, (SUBLANES, HD))
                return carry

            lax.fori_loop(0, n // G, pre, 0)

            def fwd_step(i, S, d=d):
                c = _delta_chunk_at(T, nc, d, i)
                S_s[c] = S
                return eg_s[c][0:1] * S + N_s[c] - jnp.dot(K_s[c], S.astype(BF16), preferred_element_type=F32)

            lax.fori_loop(0, n, fwd_step, jnp.zeros((HD, HD), F32))

            def bwd_step(i, dS, d=d):
                c = _delta_chunk_at(T, nc, d, n - 1 - i)
                dS_s[c] = dS
                return (eg_s[c][0:1] * dS + R_s[c]
                        - lax.dot_general(K_s[c], dS.astype(BF16), _TN, preferred_element_type=F32))

            lax.fori_loop(0, n, bwd_step, jnp.zeros((HD, HD), F32))

            def grads(c, carry, d=d):
                rows = pl.ds(pl.multiple_of(c * CHUNK, CHUNK), CHUNK)
                gcol, bcol = cols(rows, d)
                dq, dk, dv, dg, dbeta = _delta_chunk_bwd(q_ref[rows, :], k_ref[rows, :], v_ref[rows, :], gcol, bcol,
                                                         S_s[c], Tm_s[c], do_s[rows, :], dS_s[c], d == 1)
                if d == 0:
                    dqkv_ref[0, rows, :] = dq
                    dqkv_ref[1, rows, :] = dk
                    dqkv_ref[2, rows, :] = dv
                else:
                    dqkv_ref[0, rows, :] += dq
                    dqkv_ref[1, rows, :] += dk
                    dqkv_ref[2, rows, :] += dv
                dgb_ref[rows, :] += (jnp.where(lane == d * H + h, dg, 0.0)
                                     + jnp.where(lane == 2 * H + d * H + h, dbeta, 0.0))
                return carry

            lax.fori_loop(0, n, grads, 0)

    col = lambda off: pl.BlockSpec((T, HD), lambda b, h: (b, off + h))
    lat = pl.BlockSpec((N, HD), lambda b, h: (b, h))
    return pl.pallas_call(
        body, name="delta_bwd", grid=(B, H),
        in_specs=[col(0), col(H), col(2 * H), pl.BlockSpec((T, LANES), lambda b, h: (b, 0)), col(3 * H),
                  pl.BlockSpec((1, HD), lambda b, h: (0, 0)), lat, lat, pl.BlockSpec(memory_space=pl.ANY)],
        out_specs=[pl.BlockSpec((3, T, HD), lambda b, h: (0, b, h)), pl.BlockSpec((T, LANES), lambda b, h: (b, 0)),
                   col(3 * H), pl.BlockSpec((1, 8, HD), lambda b, h: (b, 0, 0))],
        out_shape=[jax.ShapeDtypeStruct((3, B * T, H * HD), F32), jax.ShapeDtypeStruct((B * T, LANES), F32),
                   jax.ShapeDtypeStruct(dp.shape, dp.dtype), jax.ShapeDtypeStruct((B, 8, HD), F32)],
        scratch_shapes=[pltpu.VMEM((T, HD), F32), pltpu.VMEM((n, CHUNK, CHUNK), F32), pltpu.VMEM((n, HD, HD), BF16),
                        pltpu.VMEM((n, HD, HD), F32), pltpu.VMEM((n, HD, HD), F32), pltpu.VMEM((n, SUBLANES, HD), F32),
                        pltpu.VMEM((n, HD, HD), F32), pltpu.VMEM((n, HD, HD), F32)],
        input_output_aliases={8: 2},
        compiler_params=_params(("parallel", "arbitrary")),
    )(qkv, qkv, qkv, gb, p, onorm, o, dy, dp)


def _rowwise(fn, ins, out_dtypes, *, name, tm=256, mult=16):
    R, W = ins[0].shape
    tm = _tile(R, tm, mult)

    def body(*refs):
        outs = fn(*[r[...] for r in refs[:len(ins)]])
        for o_ref, o in zip(refs[len(ins):], outs):
            o_ref[...] = o.astype(o_ref.dtype)

    spec = pl.BlockSpec((tm, W), lambda i: (i, 0))
    return pl.pallas_call(
        body, name=name, grid=(R // tm,), in_specs=[spec] * len(ins), out_specs=[spec] * len(out_dtypes),
        out_shape=[jax.ShapeDtypeStruct((R, W), dt) for dt in out_dtypes],
        compiler_params=_params(("parallel",)),
    )(*ins)


def _sum_lead(x, *, name, tm=256, mult=16):
    S, R, W = x.shape
    tm = _tile(R, tm, mult)

    def body(*refs):
        acc = refs[0][0].astype(F32)
        for r in refs[1:S]:
            acc = acc + r[0].astype(F32)
        refs[S][...] = acc

    return pl.pallas_call(
        body, name=name, grid=(R // tm,),
        in_specs=[pl.BlockSpec((1, tm, W), functools.partial(lambda s, i: (s, i, 0), s)) for s in range(S)],
        out_specs=pl.BlockSpec((tm, W), lambda i: (i, 0)),
        out_shape=jax.ShapeDtypeStruct((R, W), F32),
        compiler_params=_params(("parallel",)),
    )(*([x] * S))


def _adamw_math(w, g, m, v):
    m = ADAM_B1 * m + (1.0 - ADAM_B1) * g
    v = ADAM_B2 * v + (1.0 - ADAM_B2) * (g * g)
    m_hat = m / (1.0 - ADAM_B1 ** ADAM_STEP)
    v_hat = v / (1.0 - ADAM_B2 ** ADAM_STEP)
    return -ADAM_LR * (m_hat / (jnp.sqrt(v_hat) + ADAM_EPS) + ADAM_WD * w), m, v


def _adamw(w, g, m, v, *, name):
    tm = max(SUBLANES, (256 * 1024) // w.shape[1] // SUBLANES * SUBLANES)
    return _rowwise(_adamw_math, [w, g, m, v], [F32, F32, F32], name=name, tm=tm, mult=SUBLANES)


def _me():
    return lax.axis_index("x"), lax.axis_index("y"), lax.axis_index("c")


def _allgather_small(v):
    R, W = v.shape

    def body(x_ref, out_ref, send_sems, recv_sems, local_sem):
        x, y, c = _me()
        me, sibling = (x, y, c), (x, y, 1 - c)
        chips = [(1 - x, y), (x, 1 - y), (1 - x, 1 - y)]

        def slot(px, py, pc):
            return out_ref.at[4 * px + 2 * py + pc]

        def copy(k, block, to, src=None):
            return pltpu.make_async_remote_copy(
                src_ref=slot(*block) if src is None else src, dst_ref=slot(*block),
                send_sem=send_sems.at[k], recv_sem=recv_sems.at[k], device_id=to, device_id_type=MESH)

        mine = pltpu.make_async_copy(x_ref, slot(*me), local_sem)
        mine.start()
        first = [copy(0, me, sibling, src=x_ref)]
        first += [copy(1 + j, me, (*chip, c), src=x_ref) for j, chip in enumerate(chips)]
        for cp in first:
            cp.start()
        passed = [copy(4 + j, (*chip, c), sibling) for j, chip in enumerate(chips)]
        for j, chip in enumerate(chips):
            copy(1 + j, (*chip, c), me).wait_recv()
            passed[j].start()
        copy(0, sibling, me).wait_recv()
        for j, chip in enumerate(chips):
            copy(4 + j, (*chip, 1 - c), me).wait_recv()
        for cp in first + passed:
            cp.wait_send()
        mine.wait()

    return pl.pallas_call(
        body, name="allgather_small", out_shape=jax.ShapeDtypeStruct((8, R, W), v.dtype),
        in_specs=[pl.BlockSpec(memory_space=pltpu.VMEM)], out_specs=pl.BlockSpec(memory_space=pltpu.VMEM),
        scratch_shapes=[pltpu.SemaphoreType.DMA((7,)), pltpu.SemaphoreType.DMA((7,)), pltpu.SemaphoreType.DMA],
        compiler_params=_params(),
    )(v)


_ANY = pl.BlockSpec(memory_space=pl.ANY)


def _allgather_halves(shards, *, name):
    nw = len(shards)

    def body(*refs):
        x_refs, out_refs = refs[:nw], refs[nw:2 * nw]
        send_sems, recv_sems, local_sems = refs[2 * nw:]
        x, y, c = _me()
        me, sibling = (x, y, c), (x, y, 1 - c)
        chips = [(1 - x, y), (x, 1 - y), (1 - x, 1 - y)]

        def slot(w, px, py, pc):
            return out_refs[w].at[4 * px + 2 * py + pc]

        def copy(w, k, block, to, src=None):
            return pltpu.make_async_remote_copy(
                src_ref=slot(w, *block) if src is None else src, dst_ref=slot(w, *block),
                send_sem=send_sems.at[w, k], recv_sem=recv_sems.at[w, k], device_id=to, device_id_type=MESH)

        started, local = [], []
        for w in range(nw):
            half = shards[w].shape[0] // 2
            own = x_refs[w].at[pl.ds(c * half, half), :]
            mine = pltpu.make_async_copy(own, slot(w, *me), local_sems.at[w])
            mine.start()
            first = [copy(w, 0, me, sibling, src=own)]
            first += [copy(w, 1 + j, me, (*chip, c), src=own) for j, chip in enumerate(chips)]
            for cp in first:
                cp.start()
            started += first
            local.append(mine)
        for w in range(nw):
            for j, chip in enumerate(chips):
                copy(w, 1 + j, (*chip, c), me).wait_recv()
                fwd = copy(w, 4 + j, (*chip, c), sibling)
                fwd.start()
                started.append(fwd)
        for w in range(nw):
            copy(w, 0, sibling, me).wait_recv()
            for j, chip in enumerate(chips):
                copy(w, 4 + j, (*chip, 1 - c), me).wait_recv()
        for cp in started:
            cp.wait_send()
        for cp in local:
            cp.wait()

    return pl.pallas_call(
        body, name=name,
        out_shape=[jax.ShapeDtypeStruct((8, s.shape[0] // 2, s.shape[1]), s.dtype) for s in shards],
        in_specs=[_ANY] * nw, out_specs=[_ANY] * nw,
        scratch_shapes=[pltpu.SemaphoreType.DMA((nw, 7)), pltpu.SemaphoreType.DMA((nw, 7)), pltpu.SemaphoreType.DMA((nw,))],
        compiler_params=_params(),
    )(*shards)


def _sibling_send_halves(arrs, *, name):
    nw = len(arrs)

    def body(*refs):
        x_refs, out_refs, send_sems, recv_sems = refs[:nw], refs[nw:2 * nw], refs[2 * nw], refs[2 * nw + 1]
        x, y, c = _me()
        cps = []
        for w in range(nw):
            half = arrs[w].shape[1] // 2
            cp = pltpu.make_async_remote_copy(
                src_ref=x_refs[w].at[:, pl.ds((1 - c) * half, half), :], dst_ref=out_refs[w],
                send_sem=send_sems.at[w], recv_sem=recv_sems.at[w], device_id=(x, y, 1 - c), device_id_type=MESH)
            cp.start()
            cps.append(cp)
        for cp in cps:
            cp.wait()

    return pl.pallas_call(
        body, name=name,
        out_shape=[jax.ShapeDtypeStruct((a.shape[0], a.shape[1] // 2, a.shape[2]), a.dtype) for a in arrs],
        in_specs=[_ANY] * nw, out_specs=[_ANY] * nw,
        scratch_shapes=[pltpu.SemaphoreType.DMA((nw,)), pltpu.SemaphoreType.DMA((nw,))],
        compiler_params=_params(),
    )(*arrs)


def _sibling_gather(halves, *, name):
    nw = len(halves)

    def body(*refs):
        x_refs, out_refs = refs[:nw], refs[nw:2 * nw]
        send_sems, recv_sems, local_sems = refs[2 * nw:]
        x, y, c = _me()
        cps = []
        for w in range(nw):
            mine = pltpu.make_async_copy(x_refs[w], out_refs[w].at[c], local_sems.at[w])
            mine.start()
            cp = pltpu.make_async_remote_copy(
                src_ref=x_refs[w], dst_ref=out_refs[w].at[c], send_sem=send_sems.at[w], recv_sem=recv_sems.at[w],
                device_id=(x, y, 1 - c), device_id_type=MESH)
            cp.start()
            cps += [cp, mine]
        for cp in cps:
            cp.wait()

    return pl.pallas_call(
        body, name=name, out_shape=[jax.ShapeDtypeStruct((2, *h.shape), h.dtype) for h in halves],
        in_specs=[_ANY] * nw, out_specs=[_ANY] * nw,
        scratch_shapes=[pltpu.SemaphoreType.DMA((nw,)), pltpu.SemaphoreType.DMA((nw,)), pltpu.SemaphoreType.DMA((nw,))],
        compiler_params=_params(),
    )(*halves)


def _chip_exchange(arrs, *, name):
    nw = len(arrs)

    def body(*refs):
        x_refs, out_refs = refs[:nw], refs[nw:2 * nw]
        send_sems, recv_sems, local_sems = refs[2 * nw:]
        x, y, c = _me()
        s_me = 2 * x + y
        chips = [(1 - x, y), (x, 1 - y), (1 - x, 1 - y)]
        started, local = [], []
        for w in range(nw):
            mine = pltpu.make_async_copy(x_refs[w].at[s_me], out_refs[w].at[s_me], local_sems.at[w])
            mine.start()
            local.append(mine)
            for k, (px, py) in enumerate(chips):
                cp = pltpu.make_async_remote_copy(
                    src_ref=x_refs[w].at[2 * px + py], dst_ref=out_refs[w].at[s_me], send_sem=send_sems.at[w, k],
                    recv_sem=recv_sems.at[w, k], device_id=(px, py, c), device_id_type=MESH)
                cp.start()
                started.append(cp)
        for w in range(nw):
            for k, (px, py) in enumerate(chips):
                pltpu.make_async_remote_copy(
                    src_ref=x_refs[w].at[s_me], dst_ref=out_refs[w].at[2 * px + py], send_sem=send_sems.at[w, k],
                    recv_sem=recv_sems.at[w, k], device_id=(px, py, c), device_id_type=MESH).wait_recv()
        for cp in started:
            cp.wait_send()
        for cp in local:
            cp.wait()

    return pl.pallas_call(
        body, name=name, out_shape=[jax.ShapeDtypeStruct(a.shape, a.dtype) for a in arrs],
        in_specs=[_ANY] * nw, out_specs=[_ANY] * nw,
        scratch_shapes=[pltpu.SemaphoreType.DMA((nw, 3)), pltpu.SemaphoreType.DMA((nw, 3)), pltpu.SemaphoreType.DMA((nw,))],
        compiler_params=_params(),
    )(*arrs)


def _half_add(g, recv, c_arr, *, name):
    S, r, w = g.shape
    h = r // 2
    tm = _tile(h, 512, 16)
    nb = h // tm

    def body(c_ref, g_ref, r_ref, o_ref):
        o_ref[...] = (g_ref[...] + r_ref[...]).astype(BF16)

    return pl.pallas_call(
        body, name=name,
        grid_spec=pltpu.PrefetchScalarGridSpec(
            num_scalar_prefetch=1, grid=(S, nb),
            in_specs=[pl.BlockSpec((1, tm, w), lambda s, i, c_ref: (s, c_ref[0] * nb + i, 0)),
                      pl.BlockSpec((1, tm, w), lambda s, i, c_ref: (s, i, 0))],
            out_specs=pl.BlockSpec((1, tm, w), lambda s, i, c_ref: (s, i, 0))),
        out_shape=jax.ShapeDtypeStruct((S, h, w), BF16),
        compiler_params=_params(("parallel", "parallel")),
    )(c_arr, g, recv)


def _layout(sizes, width, part_mult, total_mult):
    offs, rows, r = [], [], 0
    for n in sizes:
        k = -(-n // width)
        offs.append(r)
        rows.append(k)
        r += -(-k // part_mult) * part_mult
    return offs, rows, -(-r // total_mult) * total_mult


def _pack(arrs, width, part_mult, total_mult, dtype, lead=()):
    nl = len(lead)
    sizes = [math.prod(a.shape[nl:]) for a in arrs]
    offs, rows, total = _layout(sizes, width, part_mult, total_mult)
    parts, r = [], 0
    for a, n, o, k in zip(arrs, sizes, offs, rows):
        if o > r:
            parts.append(jnp.zeros((*lead, o - r, width), dtype))
        flat = a.reshape(*lead, n).astype(dtype)
        if k * width > n:
            flat = jnp.concatenate([flat, jnp.zeros((*lead, k * width - n), dtype)], axis=-1)
        parts.append(flat.reshape(*lead, k, width))
        r = o + k
    if total > r:
        parts.append(jnp.zeros((*lead, total - r, width), dtype))
    return jnp.concatenate(parts, axis=nl)


def _unpack(pool, shapes, width, part_mult, total_mult):
    lead = pool.shape[:-2]
    sizes = [math.prod(s) for s in shapes]
    offs, rows, _ = _layout(sizes, width, part_mult, total_mult)
    out = []
    for s, n, o, k in zip(shapes, sizes, offs, rows):
        flat = lax.slice_in_dim(pool, o, o + k, axis=len(lead)).reshape(*lead, k * width)
        out.append(lax.slice_in_dim(flat, 0, n, axis=len(lead)).reshape(*lead, *s))
    return out


_WEIGHTS = ("c_ctx", "w_ada", "b_ada", "g_pre_mix", "g_post_mix", "g_pre_ffn", "g_post_ffn", "w_in", "b_merge",
            "dn_conv", "dn_a_log", "dn_dt_bias", "dn_onorm", "lru_conv", "lru_conv_b", "lru_w_rg", "lru_b_rg",
            "lru_w_ig", "lru_b_ig", "lru_lambda", "w_branch_dn", "w_branch_lru", "w_out", "w_up", "ffn_dw",
            "ffn_dw_b", "w_down")
_BIG = {"w_ada": True, "w_in": True, "w_branch_dn": False, "w_branch_lru": False, "w_out": False, "w_up": True,
        "w_down": False}
_SMALL_SHARDED = ("dn_conv", "lru_conv", "lru_b_rg", "lru_b_ig", "lru_lambda", "ffn_dw")
_NCHIP = 4
_FLAT_PART = 8


def _to_chip_shards(g, by_cols):
    if by_cols:
        return g.reshape(g.shape[0], _NCHIP, g.shape[1] // _NCHIP).transpose(1, 0, 2)
    return g.reshape(_NCHIP, g.shape[0] // _NCHIP, g.shape[1])


def _from_chip_shards(s, by_cols):
    if by_cols:
        return s.transpose(1, 0, 2).reshape(s.shape[1], _NCHIP * s.shape[2])
    return s.reshape(_NCHIP * s.shape[1], s.shape[2])


def _dsilu(x):
    s = _sigmoid(x)
    return s * (1.0 + x * (1.0 - s))


def kernel(x, c, ctx, c_ctx, w_ada, b_ada, g_pre_mix, g_post_mix, g_pre_ffn, g_post_ffn, w_in, b_merge, dn_conv, dn_a_log, dn_dt_bias, dn_onorm, lru_conv, lru_conv_b, lru_w_rg, lru_b_rg, lru_w_ig, lru_b_ig, lru_lambda, w_branch_dn, w_branch_lru, w_out, w_up, ffn_dw, ffn_dw_b, w_down, loss_target, m_c_ctx, m_w_ada, m_b_ada, m_g_pre_mix, m_g_post_mix, m_g_pre_ffn, m_g_post_ffn, m_w_in, m_b_merge, m_dn_conv, m_dn_a_log, m_dn_dt_bias, m_dn_onorm, m_lru_conv, m_lru_conv_b, m_lru_w_rg, m_lru_b_rg, m_lru_w_ig, m_lru_b_ig, m_lru_lambda, m_w_branch_dn, m_w_branch_lru, m_w_out, m_w_up, m_ffn_dw, m_ffn_dw_b, m_w_down, v_c_ctx, v_w_ada, v_b_ada, v_g_pre_mix, v_g_post_mix, v_g_pre_ffn, v_g_post_ffn, v_w_in, v_b_merge, v_dn_conv, v_dn_a_log, v_dn_dt_bias, v_dn_onorm, v_lru_conv, v_lru_conv_b, v_lru_w_rg, v_lru_b_rg, v_lru_w_ig, v_lru_b_ig, v_lru_lambda, v_w_branch_dn, v_w_branch_lru, v_w_out, v_w_up, v_ffn_dw, v_ffn_dw_b, v_w_down):
    W = dict(zip(_WEIGHTS, (c_ctx, w_ada, b_ada, g_pre_mix, g_post_mix, g_pre_ffn, g_post_ffn, w_in, b_merge, dn_conv,
                            dn_a_log, dn_dt_bias, dn_onorm, lru_conv, lru_conv_b, lru_w_rg, lru_b_rg, lru_w_ig, lru_b_ig,
                            lru_lambda, w_branch_dn, w_branch_lru, w_out, w_up, ffn_dw, ffn_dw_b, w_down)))
    Mo = dict(zip(_WEIGHTS, (m_c_ctx, m_w_ada, m_b_ada, m_g_pre_mix, m_g_post_mix, m_g_pre_ffn, m_g_post_ffn, m_w_in,
                             m_b_merge, m_dn_conv, m_dn_a_log, m_dn_dt_bias, m_dn_onorm, m_lru_conv, m_lru_conv_b,
                             m_lru_w_rg, m_lru_b_rg, m_lru_w_ig, m_lru_b_ig, m_lru_lambda, m_w_branch_dn,
                             m_w_branch_lru, m_w_out, m_w_up, m_ffn_dw, m_ffn_dw_b, m_w_down)))
    Vo = dict(zip(_WEIGHTS, (v_c_ctx, v_w_ada, v_b_ada, v_g_pre_mix, v_g_post_mix, v_g_pre_ffn, v_g_post_ffn, v_w_in,
                             v_b_merge, v_dn_conv, v_dn_a_log, v_dn_dt_bias, v_dn_onorm, v_lru_conv, v_lru_conv_b,
                             v_lru_w_rg, v_lru_b_rg, v_lru_w_ig, v_lru_b_ig, v_lru_lambda, v_w_branch_dn,
                             v_w_branch_lru, v_w_out, v_w_up, v_ffn_dw, v_ffn_dw_b, v_w_down)))
    B, N, D = x.shape
    NC = ctx.shape[1]
    T = NC + N
    H, HD = dn_a_log.shape[-1], dn_onorm.shape[-1]
    DNW = H * HD
    LW, LBD = lru_conv_b.shape[-1], lru_w_rg.shape[-1]
    DFF = ffn_dw_b.shape[-1]
    LC = LANES
    x_i, y_i, c_i = _me()
    s_me = 2 * x_i + y_i
    tm = _tile(math.gcd(NC, N), 256, 16)

    gathered = _allgather_halves([W[n][0].astype(BF16) for n in _BIG], name="allgather_big")
    full = {}
    for n, g in zip(_BIG, gathered):
        r, w_ = W[n].shape[1:]
        full[n] = g.reshape(_NCHIP, r, w_) if _BIG[n] else g.reshape(_NCHIP * r, w_)

    small_local = [W[n][0].reshape(-1, W[n].shape[-1]) for n in _SMALL_SHARDED]
    small_shapes = [a.shape for a in small_local]
    spack = _pack(small_local, LANES, _FLAT_PART, _FLAT_PART, F32)
    sgath = _allgather_small(spack)[0::2]
    sfull = {n: _from_chip_shards(s, True)
             for n, s in zip(_SMALL_SHARDED, _unpack(sgath, small_shapes, LANES, _FLAT_PART, _FLAT_PART))}

    o_a = 4 * DNW
    o_xl = o_a + 4 * H
    o_mg = o_xl + 2 * LW
    wi_ = _from_chip_shards(full["w_in"], True)
    nj = LW // LC
    lru_cols = jnp.stack([wi_[:, o_xl:o_xl + LW].reshape(D, nj, LC), wi_[:, o_xl + LW:o_mg].reshape(D, nj, LC)],
                         axis=2).reshape(D, 2 * LW)
    wp = jnp.concatenate([wi_[:, :o_a], lru_cols, wi_[:, o_mg:], wi_[:, o_a:o_xl],
                          jnp.zeros((D, LANES - 4 * H), BF16)], axis=1)
    p_lru, p_mg, p_ab = 4 * DNW, 4 * DNW + 2 * LW, 4 * DNW + 2 * LW + 2 * D
    PW = p_ab + LANES

    MR = LANES
    cond = jnp.concatenate([c, c_ctx[None], jnp.zeros((MR - B - 1, D), F32)], axis=0)
    silu_rows = _rowwise(lambda a: (_silu(a),), [cond], [F32], name="cond_silu")[0]
    mod = _matmul(silu_rows, full["w_ada"], b_shards=(0, _NCHIP), name="ada_fwd") + b_ada
    mx = mod[:B].reshape(B, 6, D)
    mc = mod[B].reshape(6, D)
    zero = jnp.zeros((B, D), F32)
    tab = jnp.stack([jnp.stack([jnp.broadcast_to(mc[0], (B, D)), jnp.broadcast_to(mc[1], (B, D))] + [zero] * 6, axis=1),
                     jnp.stack([mx[:, 0], mx[:, 1]] + [zero] * 6, axis=1)], axis=1)
    vecs = jnp.stack([mx[:, 2], mx[:, 3], mx[:, 4], mx[:, 5]] + [zero] * 4, axis=1)
    gains = jnp.concatenate([g_post_mix, g_pre_ffn, g_post_ffn, jnp.zeros((5, D), F32)], axis=0)

    h = jnp.concatenate([ctx, x], axis=1)
    u = _premix_fwd(h, g_pre_mix, tab, nc=NC, tm=tm)
    p = _matmul(u, wp, name="in_fwd")
    dkw = dict(B=B, T=T, nc=NC, H=H, HD=HD)
    qkv = _dnprep_fwd(p, sfull["dn_conv"], **dkw)
    prm = jnp.concatenate([
        jnp.concatenate([dn_a_log.reshape(1, 2 * H), jnp.zeros((1, LANES - 2 * H), F32)], axis=1),
        jnp.concatenate([dn_dt_bias.reshape(1, 2 * H), jnp.zeros((1, LANES - 2 * H), F32)], axis=1),
        jnp.zeros((6, LANES), F32)], axis=0)
    gtm = _tile(B * T, 512, 16)
    gb = _gb_fwd(p, prm, rows=B * T, col0=p_ab, H=H, tm=gtm)
    y_dn, o_dn = _delta_fwd(qkv, gb, p, dn_onorm, **dkw)
    lv = jnp.concatenate([lru_conv_b, sfull["lru_b_rg"], sfull["lru_b_ig"], sfull["lru_lambda"], jnp.zeros((1, LW), F32)], axis=0)
    wr = _blockdiag(lru_w_rg[0], LC).astype(BF16)
    wi = _blockdiag(lru_w_ig[0], LC).astype(BF16)
    lkw = dict(B=B, T=T, nc=NC, LW=LW, col0=p_lru, C=LC)
    y_lru = _lru_fwd(p, sfull["lru_conv"], lv, wr, wi, **lkw)
    Ydn = _matmul(y_dn, full["w_branch_dn"], name="bdn_fwd")
    Ylru = _matmul(y_lru, full["w_branch_lru"], name="blru_fwd")
    mkw = dict(B=B, T=T, nc=NC, D=D, col0=p_mg, tm=tm)
    mixin = _merge_fwd(p, Ydn, Ylru, b_merge, **mkw)
    mix = _matmul(mixin, full["w_out"], name="out_fwd")
    h1, u2 = _post_fwd(x, mix, gains, vecs, tm=tm)
    F = _matmul(u2, full["w_up"], b_shards=(0, _NCHIP), name="up_fwd")
    w9 = sfull["ffn_dw"]
    ftc = _tile(DFF, 256)
    f = _ffn_act_fwd(F, w9, ffn_dw_b, B=B, N=N, DFF=DFF, tc=ftc)
    dn = _matmul(f, full["w_down"], name="down_fwd")
    ddn, dout, sums_f = _final(h1, dn, loss_target, gains, vecs, tm=tm)

    G = {}
    df = _matmul(ddn, full["w_down"], tb=True, name="down_bwd_x")
    G["w_down"] = _matmul(f, ddn, ta=True, name="down_bwd_w")
    dFg, dFv, dwb = _ffn_act_bwd(F, w9, ffn_dw_b, df, B=B, N=N, DFF=DFF, tc=ftc)
    hs = _NCHIP // 2
    du2 = _matmul(dFg, full["w_up"], tb=True, b_shards=(0, hs), name="up_bwd_xg")
    du2 = _matmul(dFv, full["w_up"], tb=True, b_shards=(hs, hs), add=du2, name="up_bwd_xv")
    G["w_up"] = jnp.concatenate([_matmul(u2, dFg, ta=True, out_shards=hs, name="up_bwd_wg"),
                                 _matmul(u2, dFv, ta=True, out_shards=hs, name="up_bwd_wv")], axis=0)
    dx1, dmix, sums_p = _post_bwd(x, mix, gains, vecs, dout, du2, tm=tm)
    dmixin = _matmul(dmix, full["w_out"], tb=True, name="out_bwd_x")
    G["w_out"] = _matmul(mixin, dmix, ta=True, name="out_bwd_w")
    dp = jnp.zeros((B * T, PW), BF16)
    dYdn, dYlru, dp, sums_m = _merge_bwd(p, Ydn, Ylru, b_merge, dmixin, dp, **mkw)
    dy_dn = _matmul(dYdn, full["w_branch_dn"], tb=True, name="bdn_bwd_x")
    G["w_branch_dn"] = _matmul(y_dn, dYdn, ta=True, name="bdn_bwd_w")
    dy_lru = _matmul(dYlru, full["w_branch_lru"], tb=True, name="blru_bwd_x")
    G["w_branch_lru"] = _matmul(y_lru, dYlru, ta=True, name="blru_bwd_w")
    dp, dcw_l, dlv, dwr, dwi = _lru_bwd(p, sfull["lru_conv"], lv, wr, wi, dy_lru, dp, **lkw)
    dqkv, dgb, dp, don = _delta_bwd(qkv, gb, p, dn_onorm, o_dn, dy_dn, dp, **dkw)
    dp, dprm = _gb_bwd(p, prm, dgb, dp, rows=B * T, col0=p_ab, H=H, tm=gtm)
    dp, dcw_d = _dnprep_bwd(p, sfull["dn_conv"], dqkv, dp, **dkw)
    dU = _matmul(dp, wp, tb=True, name="in_bwd_x")
    dwp = _matmul(u, dp, ta=True, name="in_bwd_w")
    grad_x, sums_pm = _premix_bwd(h, g_pre_mix, tab, dU, dx1, nc=NC, tm=tm)
    dlru = dwp[:, p_lru:p_mg].reshape(D, nj, 2, LC)
    G["w_in"] = _to_chip_shards(jnp.concatenate([dwp[:, :o_a], dwp[:, p_ab:p_ab + 4 * H], dlru[:, :, 0].reshape(D, LW),
                                                 dlru[:, :, 1].reshape(D, LW), dwp[:, p_mg:p_ab]], axis=1), True)

    dmod_x = jnp.stack([sums_pm[:, 1, 0], sums_pm[:, 1, 1], sums_p[:, 0], sums_p[:, 1], sums_p[:, 2], sums_f[:, 0]],
                       axis=1).reshape(B, 6 * D)
    dmod_c = jnp.concatenate([sums_pm[:, 0, 0].sum(0), sums_pm[:, 0, 1].sum(0), jnp.zeros((4 * D,), F32)])[None]
    dmod = jnp.concatenate([dmod_x, dmod_c, jnp.zeros((MR - B - 1, 6 * D), F32)], axis=0)
    G["w_ada"] = _matmul(silu_rows, dmod, ta=True, out_shards=_NCHIP, name="ada_bwd_w")
    dsilu = _matmul(dmod, full["w_ada"], tb=True, b_shards=(0, _NCHIP), name="ada_bwd_x")

    g_small = {
        "c_ctx": dsilu[B] * _dsilu(c_ctx),
        "b_ada": dmod[:B + 1].sum(0)[None],
        "g_pre_mix": sums_pm[:, :, 2].sum((0, 1))[None],
        "g_post_mix": sums_p[:, 3].sum(0)[None],
        "g_pre_ffn": sums_p[:, 4].sum(0)[None],
        "g_post_ffn": sums_f[:, 1].sum(0)[None],
        "b_merge": sums_m[0:1],
        "dn_conv": dcw_d[0:4][None],
        "dn_a_log": dprm[0, :2 * H].reshape(1, 2, H),
        "dn_dt_bias": dprm[1, :2 * H].reshape(1, 2, H),
        "dn_onorm": don[:, 0].sum(0)[None],
        "lru_conv": dcw_l[0:4][None],
        "lru_conv_b": dlv[0:1],
        "lru_w_rg": _blockdiag_extract(dwr, LBD)[None],
        "lru_b_rg": dlv[1:3][None],
        "lru_w_ig": _blockdiag_extract(dwi, LBD)[None],
        "lru_b_ig": dlv[3:5][None],
        "lru_lambda": dlv[5:7][None],
        "ffn_dw": dwb[0:9].reshape(1, 3, 3, DFF),
        "ffn_dw_b": dwb[9:10],
    }
    small_names = tuple(n for n in _WEIGHTS if n not in _BIG)
    loss_part = sums_f[:, 2].sum().reshape(1)
    gs_list = [g_small[n] for n in small_names] + [loss_part]
    gs_shapes = [a.shape for a in gs_list]
    gpack = _pack(gs_list, LANES, _FLAT_PART, _FLAT_PART, F32)
    gsum = _sum_lead(_allgather_small(gpack), name="small_sum", tm=512, mult=SUBLANES)
    gs_red = dict(zip(small_names + ("loss",), _unpack(gsum, gs_shapes, LANES, _FLAT_PART, _FLAT_PART)))
    loss = gs_red["loss"][0]

    slabs = [G[n] if _BIG[n] else G[n].reshape(_NCHIP, G[n].shape[0] // _NCHIP, G[n].shape[1]) for n in _BIG]
    c_arr = c_i.astype(jnp.int32).reshape(1)
    from_sibling = _sibling_send_halves(slabs, name="rs_sibling")
    chip_sums = [_half_add(g, r, c_arr, name="rs_add_" + n) for n, g, r in zip(_BIG, slabs, from_sibling)]
    landed = _chip_exchange(chip_sums, name="chip_exchange")
    halves = [_sum_lead(e, name="rs_sum_" + n, tm=512) for n, e in zip(_BIG, landed)]
    g_big = {n: g.reshape(W[n].shape[1:]) for n, g in zip(_BIG, _sibling_gather(halves, name="rs_gather"))}

    grads, deltas, new_m, new_v = {}, {}, {}, {}
    for n in _BIG:
        shp = W[n].shape
        grads[n] = g_big[n].reshape(shp)
        d_, m_, v_ = _adamw(W[n][0], g_big[n], Mo[n][0], Vo[n][0], name="adamw_" + n)
        deltas[n], new_m[n], new_v[n] = d_.reshape(shp), m_.reshape(shp), v_.reshape(shp)
    for n in small_names:
        g = gs_red[n]
        if n in _SMALL_SHARDED:
            k = W[n].shape[-1]
            g = lax.dynamic_slice_in_dim(g, s_me * k, k, axis=g.ndim - 1)
        grads[n] = g.reshape(W[n].shape)
    sm_shapes = [W[n].shape for n in small_names]
    pk = lambda d: _pack([d[n] for n in small_names], LANES, _FLAT_PART, _FLAT_PART, F32)
    d_, m_, v_ = _adamw(pk(W), pk(grads), pk(Mo), pk(Vo), name="adamw_small")
    for dst, pool_ in ((deltas, d_), (new_m, m_), (new_v, v_)):
        dst.update(zip(small_names, _unpack(pool_, sm_shapes, LANES, _FLAT_PART, _FLAT_PART)))
    return (loss, grad_x, *[grads[n] for n in _WEIGHTS], *[deltas[n] for n in _WEIGHTS],
            *[new_m[n] for n in _WEIGHTS], *[new_v[n] for n in _WEIGHTS])
```

```python
import functools
import math

import jax
import jax.numpy as jnp
from jax import lax
from jax.experimental import pallas as pl
from jax.experimental.pallas import tpu as pltpu

F32 = jnp.float32
BF16 = jnp.bfloat16
EPS = 1e-6
GRID_W = 64
CHUNK = 64
LRU_C = 8.0
LANES = 128
SUBLANES = 8
VMEM_LIMIT = 56 * 1024 * 1024
ADAM_LR, ADAM_B1, ADAM_B2, ADAM_EPS, ADAM_WD, ADAM_STEP = 0.001, 0.9, 0.999, 1e-08, 0.01, 10
MESH = pl.DeviceIdType.MESH


def _tile(n, target, mult=LANES):
    best = None
    for t in range(mult, min(n, target) + 1, mult):
        if n % t == 0:
            best = t
    return best if best is not None else n


def _params(sem=None, **kw):
    return pltpu.CompilerParams(dimension_semantics=sem, vmem_limit_bytes=VMEM_LIMIT, **kw)


def _sigmoid(x):
    return 1.0 / (1.0 + jnp.exp(-x))


def _silu(x):
    return x * _sigmoid(x)


def _softplus(x):
    return jnp.maximum(x, 0.0) + jnp.log(1.0 + jnp.exp(-jnp.abs(x)))


def _gelu(x):
    return 0.5 * x * (1.0 + jnp.tanh(math.sqrt(2.0 / math.pi) * (x + 0.044715 * x * x * x)))


def _rmsn(u, gain):
    return u * lax.rsqrt(jnp.mean(u * u, axis=-1, keepdims=True) + EPS) * gain


_MM_VMEM = 40 * 1024 * 1024


def _matmul(a, b, *, ta=False, tb=False, add=None, b_shards=None, out_shards=None, out_dtype=F32, name,
            tm=1024, tn=2048, tk=1024):
    (K, M) = a.shape if ta else a.shape[::-1]
    if b_shards is not None:
        s0, ns = b_shards
        bsh = (b.shape[1], ns * b.shape[2])
        nsh = b.shape[2]
    else:
        bsh = b.shape
    N = bsh[0] if tb else bsh[1]
    assert (bsh[1] if tb else bsh[0]) == K, (a.shape, b.shape, ta, tb)
    tm = _tile(M, tm)
    tk = _tile(nsh if (b_shards is not None and tb) else K, tk)
    nlim = nsh if (b_shards is not None and not tb) else (N // out_shards if out_shards else N)
    osz = jnp.dtype(out_dtype).itemsize + (4 if add is not None else 0)
    while True:
        tn_ = _tile(nlim, tn)
        need = 2 * (tm * tk * a.dtype.itemsize + tk * tn_ * b.dtype.itemsize + tm * tn_ * osz) + 4 * tm * tn_
        if need <= _MM_VMEM or tn <= LANES:
            break
        tn //= 2
    tn = tn_
    nk = K // tk
    dims = (((0 if ta else 1,), (1 if tb else 0,)), ((), ()))

    def body(a_ref, b_ref, *rest):
        (c_ref, o_ref, acc_ref) = rest if add is not None else (None, *rest)
        k = pl.program_id(2)

        @pl.when(k == 0)
        def _():
            acc_ref[...] = jnp.zeros_like(acc_ref) if c_ref is None else c_ref[...]

        bv = b_ref[0] if b_shards is not None else b_ref[...]
        acc_ref[...] += lax.dot_general(a_ref[...].astype(BF16), bv.astype(BF16), dims, preferred_element_type=F32)

        @pl.when(k == nk - 1)
        def _():
            if out_shards:
                o_ref[0] = acc_ref[...].astype(out_dtype)
            else:
                o_ref[...] = acc_ref[...].astype(out_dtype)

    a_spec = pl.BlockSpec((tk, tm), lambda i, j, k: (k, i)) if ta else pl.BlockSpec((tm, tk), lambda i, j, k: (i, k))
    if b_shards is None:
        b_spec = pl.BlockSpec((tn, tk), lambda i, j, k: (j, k)) if tb else pl.BlockSpec((tk, tn), lambda i, j, k: (k, j))
    elif tb:
        per = nsh // tk
        b_spec = pl.BlockSpec((1, tn, tk), lambda i, j, k: (s0 + k // per, j, k % per))
    else:
        per = nsh // tn
        b_spec = pl.BlockSpec((1, tk, tn), lambda i, j, k: (s0 + j // per, k, j % per))
    o_spec = pl.BlockSpec((tm, tn), lambda i, j, k: (i, j))
    if out_shards:
        oper = N // out_shards // tn
        out_spec = pl.BlockSpec((1, tm, tn), lambda i, j, k: (j // oper, i, j % oper))
        out_shape = jax.ShapeDtypeStruct((out_shards, M, N // out_shards), out_dtype)
    else:
        out_spec, out_shape = o_spec, jax.ShapeDtypeStruct((M, N), out_dtype)
    return pl.pallas_call(
        body, name=name, grid=(M // tm, N // tn, nk),
        in_specs=[a_spec, b_spec] + ([o_spec] if add is not None else []),
        out_specs=out_spec, out_shape=out_shape,
        scratch_shapes=[pltpu.VMEM((tm, tn), F32)],
        compiler_params=_params(("parallel", "parallel", "arbitrary")),
    )(*((a, b) + ((add,) if add is not None else ())))


def _premix_math(h, gain, shift, scale):
    return _rmsn(h, gain) * (1.0 + scale) + shift


def _premix_fwd(h, gain, tab, *, nc, tm):
    B, T, D = h.shape
    nt, nct = T // tm, nc // tm

    def body(h_ref, g_ref, tab_ref, u_ref):
        tabv = tab_ref[0, 0]
        u_ref[...] = _premix_math(h_ref[0], g_ref[...], tabv[0:1], tabv[1:2]).astype(BF16)

    return pl.pallas_call(
        body, name="premix_fwd", grid=(B, nt),
        in_specs=[pl.BlockSpec((1, tm, D), lambda b, t: (b, t, 0)),
                  pl.BlockSpec((1, D), lambda b, t: (0, 0)),
                  pl.BlockSpec((1, 1, 8, D), lambda b, t: (b, jnp.where(t < nct, 0, 1), 0, 0))],
        out_specs=pl.BlockSpec((tm, D), lambda b, t: (b * nt + t, 0)),
        out_shape=jax.ShapeDtypeStruct((B * T, D), BF16),
        compiler_params=_params(("parallel", "parallel")),
    )(h, gain, tab)


def _premix_bwd(h, gain, tab, du, dres, *, nc, tm):
    B, T, D = h.shape
    nt, nct = T // tm, nc // tm
    N = T - nc

    def body(h_ref, g_ref, tab_ref, du_ref, dres_ref, dx_ref, sums_ref):
        t = pl.program_id(1)
        tabv = tab_ref[0, 0]
        _, vjp = jax.vjp(_premix_math, h_ref[0], g_ref[...], tabv[0:1], tabv[1:2])
        dh, dgain, dshift, dscale = vjp(du_ref[...].astype(F32))

        @pl.when((t == 0) | (t == nct))
        def _():
            sums_ref[...] = jnp.zeros_like(sums_ref)

        sums_ref[0, 0, 0:1, :] += dshift
        sums_ref[0, 0, 1:2, :] += dscale
        sums_ref[0, 0, 2:3, :] += dgain

        @pl.when(t >= nct)
        def _():
            dx_ref[0] = dres_ref[...] + dh

    lat = lambda b, t: jnp.maximum(t - nct, 0)
    return pl.pallas_call(
        body, name="premix_bwd", grid=(B, nt),
        in_specs=[pl.BlockSpec((1, tm, D), lambda b, t: (b, t, 0)),
                  pl.BlockSpec((1, D), lambda b, t: (0, 0)),
                  pl.BlockSpec((1, 1, 8, D), lambda b, t: (b, jnp.where(t < nct, 0, 1), 0, 0)),
                  pl.BlockSpec((tm, D), lambda b, t: (b * nt + t, 0)),
                  pl.BlockSpec((tm, D), lambda b, t: (b * (nt - nct) + lat(b, t), 0))],
        out_specs=[pl.BlockSpec((1, tm, D), lambda b, t: (b, lat(b, t), 0)),
                   pl.BlockSpec((1, 1, 8, D), lambda b, t: (b, jnp.where(t < nct, 0, 1), 0, 0))],
        out_shape=[jax.ShapeDtypeStruct((B, N, D), F32), jax.ShapeDtypeStruct((B, 2, 8, D), F32)],
        compiler_params=_params(("parallel", "arbitrary")),
    )(h, gain, tab, du, dres)


def _merge_math(mgd, mgl, yd, yl, bd, bl):
    return _sigmoid(mgd + bd) * yd + _sigmoid(mgl + bl) * yl


def _merge_fwd(p, ydn, ylru, b_merge, *, B, T, nc, D, col0, tm):
    N = T - nc
    ntl, nt, nct, cb = N // tm, T // tm, nc // tm, col0 // D

    def body(mgd_ref, mgl_ref, yd_ref, yl_ref, bm_ref, o_ref):
        o_ref[...] = _merge_math(mgd_ref[...], mgl_ref[...], yd_ref[...], yl_ref[...],
                                 bm_ref[:, 0:D], bm_ref[:, D:2 * D]).astype(BF16)

    prow = lambda b, t: b * nt + nct + t
    return pl.pallas_call(
        body, name="merge_fwd", grid=(B, ntl),
        in_specs=[pl.BlockSpec((tm, D), lambda b, t: (prow(b, t), cb)),
                  pl.BlockSpec((tm, D), lambda b, t: (prow(b, t), cb + 1)),
                  pl.BlockSpec((tm, D), lambda b, t: (b * ntl + t, 0)),
                  pl.BlockSpec((tm, D), lambda b, t: (b * ntl + t, 0)),
                  pl.BlockSpec((1, 2 * D), lambda b, t: (0, 0))],
        out_specs=pl.BlockSpec((tm, D), lambda b, t: (b * ntl + t, 0)),
        out_shape=jax.ShapeDtypeStruct((B * N, D), BF16),
        compiler_params=_params(("parallel", "parallel")),
    )(p, p, ydn, ylru, b_merge)


def _merge_bwd(p, ydn, ylru, b_merge, dmix, dp, *, B, T, nc, D, col0, tm):
    N = T - nc
    ntl, nt, nct, cb = N // tm, T // tm, nc // tm, col0 // D
    assert col0 % (2 * D) == 0

    def body(mgd_ref, mgl_ref, yd_ref, yl_ref, bm_ref, dm_ref, dp_any, dyd_ref, dyl_ref, dp_ref, sums_ref):
        _, vjp = jax.vjp(_merge_math, mgd_ref[...], mgl_ref[...], yd_ref[...], yl_ref[...],
                         bm_ref[:, 0:D], bm_ref[:, D:2 * D])
        dmgd, dmgl, dyd, dyl, dbd, dbl = vjp(dm_ref[...])
        dyd_ref[...] = dyd.astype(BF16)
        dyl_ref[...] = dyl.astype(BF16)
        dp_ref[:, 0:D] = dmgd.astype(BF16)
        dp_ref[:, D:2 * D] = dmgl.astype(BF16)

        @pl.when((pl.program_id(0) == 0) & (pl.program_id(1) == 0))
        def _():
            sums_ref[...] = jnp.zeros_like(sums_ref)

        sums_ref[0:1, 0:D] += dbd
        sums_ref[0:1, D:2 * D] += dbl

    prow = lambda b, t: b * nt + nct + t
    row = pl.BlockSpec((tm, D), lambda b, t: (b * ntl + t, 0))
    return pl.pallas_call(
        body, name="merge_bwd", grid=(B, ntl),
        in_specs=[pl.BlockSpec((tm, D), lambda b, t: (prow(b, t), cb)),
                  pl.BlockSpec((tm, D), lambda b, t: (prow(b, t), cb + 1)),
                  row, row, pl.BlockSpec((1, 2 * D), lambda b, t: (0, 0)), row,
                  pl.BlockSpec(memory_space=pl.ANY)],
        out_specs=[row, row,
                   pl.BlockSpec((tm, 2 * D), lambda b, t: (prow(b, t), cb // 2)),
                   pl.BlockSpec((8, 2 * D), lambda b, t: (0, 0))],
        out_shape=[jax.ShapeDtypeStruct((B * N, D), BF16), jax.ShapeDtypeStruct((B * N, D), BF16),
                   jax.ShapeDtypeStruct(dp.shape, dp.dtype), jax.ShapeDtypeStruct((8, 2 * D), F32)],
        input_output_aliases={6: 2},
        compiler_params=_params(("arbitrary", "arbitrary")),
    )(p, p, ydn, ylru, b_merge, dmix, dp)


def _post_math(x, mix, g1, gate, g2, sh, sc):
    h1 = x + _rmsn(mix, g1) * gate
    return h1, _rmsn(h1, g2) * (1.0 + sc) + sh


def _post_fwd(x, mix, gains, vecs, *, tm):
    B, N, D = x.shape
    ntl = N // tm

    def body(x_ref, mix_ref, g_ref, v_ref, h1_ref, u2_ref):
        v = v_ref[0]
        h1, u2 = _post_math(x_ref[0], mix_ref[...], g_ref[0:1], v[0:1], g_ref[1:2], v[1:2], v[2:3])
        h1_ref[...] = h1
        u2_ref[...] = u2.astype(BF16)

    row = pl.BlockSpec((tm, D), lambda b, t: (b * ntl + t, 0))
    return pl.pallas_call(
        body, name="post_fwd", grid=(B, ntl),
        in_specs=[pl.BlockSpec((1, tm, D), lambda b, t: (b, t, 0)), row,
                  pl.BlockSpec((8, D), lambda b, t: (0, 0)), pl.BlockSpec((1, 8, D), lambda b, t: (b, 0, 0))],
        out_specs=[row, row],
        out_shape=[jax.ShapeDtypeStruct((B * N, D), F32), jax.ShapeDtypeStruct((B * N, D), BF16)],
        compiler_params=_params(("parallel", "parallel")),
    )(x, mix, gains, vecs)


def _post_bwd(x, mix, gains, vecs, dh1, du2, *, tm):
    B, N, D = x.shape
    ntl = N // tm

    def body(x_ref, mix_ref, g_ref, v_ref, dh1_ref, du2_ref, dx_ref, dmix_ref, sums_ref):
        v = v_ref[0]
        _, vjp = jax.vjp(_post_math, x_ref[0], mix_ref[...], g_ref[0:1], v[0:1], g_ref[1:2], v[1:2], v[2:3])
        dx, dmix, dg1, dgate, dg2, dsh, dsc = vjp((dh1_ref[...], du2_ref[...]))
        dx_ref[...] = dx
        dmix_ref[...] = dmix.astype(BF16)

        @pl.when(pl.program_id(1) == 0)
        def _():
            sums_ref[...] = jnp.zeros_like(sums_ref)

        sums_ref[0, 0:1, :] += dgate
        sums_ref[0, 1:2, :] += dsh
        sums_ref[0, 2:3, :] += dsc
        sums_ref[0, 3:4, :] += dg1
        sums_ref[0, 4:5, :] += dg2

    row = pl.BlockSpec((tm, D), lambda b, t: (b * ntl + t, 0))
    return pl.pallas_call(
        body, name="post_bwd", grid=(B, ntl),
        in_specs=[pl.BlockSpec((1, tm, D), lambda b, t: (b, t, 0)), row,
                  pl.BlockSpec((8, D), lambda b, t: (0, 0)), pl.BlockSpec((1, 8, D), lambda b, t: (b, 0, 0)), row, row],
        out_specs=[row, row, pl.BlockSpec((1, 8, D), lambda b, t: (b, 0, 0))],
        out_shape=[jax.ShapeDtypeStruct((B * N, D), F32), jax.ShapeDtypeStruct((B * N, D), BF16),
                   jax.ShapeDtypeStruct((B, 8, D), F32)],
        compiler_params=_params(("parallel", "arbitrary")),
    )(x, mix, gains, vecs, dh1, du2)


def _final_math(dn, g4, gate5):
    return _rmsn(dn, g4) * gate5


def _final(h1, dn, target, gains, vecs, *, tm):
    B, N, D = target.shape
    ntl = N // tm

    def body(h1_ref, dn_ref, t_ref, g_ref, v_ref, ddn_ref, dout_ref, sums_ref):
        v = v_ref[0]
        y, vjp = jax.vjp(_final_math, dn_ref[...], g_ref[2:3], v[3:4])
        err = h1_ref[...] + y - t_ref[0]
        dout = err * (1.0 / D)
        ddn, dg4, dgate5 = vjp(dout)
        ddn_ref[...] = ddn.astype(BF16)
        dout_ref[...] = dout

        @pl.when(pl.program_id(1) == 0)
        def _():
            sums_ref[...] = jnp.zeros_like(sums_ref)

        sums_ref[0, 0:1, :] += dgate5
        sums_ref[0, 1:2, :] += dg4
        sums_ref[0, 2:3, :] += jnp.sum(err * err, axis=0, keepdims=True) * (0.5 / D)

    row = pl.BlockSpec((tm, D), lambda b, t: (b * ntl + t, 0))
    return pl.pallas_call(
        body, name="final", grid=(B, ntl),
        in_specs=[row, row, pl.BlockSpec((1, tm, D), lambda b, t: (b, t, 0)),
                  pl.BlockSpec((8, D), lambda b, t: (0, 0)), pl.BlockSpec((1, 8, D), lambda b, t: (b, 0, 0))],
        out_specs=[row, row, pl.BlockSpec((1, 8, D), lambda b, t: (b, 0, 0))],
        out_shape=[jax.ShapeDtypeStruct((B * N, D), BF16), jax.ShapeDtypeStruct((B * N, D), F32),
                   jax.ShapeDtypeStruct((B, 8, D), F32)],
        compiler_params=_params(("parallel", "arbitrary")),
    )(h1, dn, target, gains, vecs)


def _shift(x, s):
    s = s % x.shape[0]
    return x if s == 0 else pltpu.roll(x, s, 0)


def _seg_taps(T, nc, width, pad_left):
    t = lax.broadcasted_iota(jnp.int32, (T, 1), 0)
    pos = jnp.where(t < nc, t, t - nc)
    seg = jnp.where(t < nc, nc, T - nc)
    taps = []
    for k in range(width):
        src = pos + (k - pad_left)
        taps.append((pad_left - k, (src >= 0) & (src < seg)))
    return taps


def _grid_taps(N):
    t = lax.broadcasted_iota(jnp.int32, (N, 1), 0)
    wcol = t % GRID_W
    taps = []
    for dr in (-1, 0, 1):
        for dw in (-1, 0, 1):
            off = dr * GRID_W + dw
            ok = (wcol + dw >= 0) & (wcol + dw < GRID_W) & (t + dr * GRID_W >= 0) & (t + dr * GRID_W < N)
            taps.append((-off, ok))
    return taps


def _conv_fwd(x, w, taps):
    y = jnp.zeros_like(x)
    for k, (s, m) in enumerate(taps):
        y = y + w[k:k + 1] * jnp.where(m, _shift(x, s), 0.0)
    return y


def _conv_bwd(x, w, taps, dy):
    dx = jnp.zeros_like(x)
    dws = []
    for k, (s, m) in enumerate(taps):
        dym = jnp.where(m, dy, 0.0)
        dx = dx + w[k:k + 1] * _shift(dym, -s)
        dws.append(jnp.sum(dym * _shift(x, s), axis=0, keepdims=True))
    return dx, jnp.concatenate(dws, axis=0)


def _ffn_act_fwd(F, w9, bias, *, B, N, DFF, tc):
    nj = DFF // tc

    def body(fg_ref, fv_ref, w_ref, b_ref, o_ref):
        fg = _conv_fwd(fg_ref[...], w_ref[...], _grid_taps(N)) + b_ref[...]
        o_ref[...] = (_gelu(fg) * fv_ref[...]).astype(BF16)

    return pl.pallas_call(
        body, name="ffn_act_fwd", grid=(B, nj),
        in_specs=[pl.BlockSpec((N, tc), lambda b, j: (b, j)), pl.BlockSpec((N, tc), lambda b, j: (b, nj + j)),
                  pl.BlockSpec((9, tc), lambda b, j: (0, j)), pl.BlockSpec((1, tc), lambda b, j: (0, j))],
        out_specs=pl.BlockSpec((N, tc), lambda b, j: (b, j)),
        out_shape=jax.ShapeDtypeStruct((B * N, DFF), BF16),
        compiler_params=_params(("parallel", "parallel")),
    )(F, F, w9, bias)


def _ffn_act_bwd(F, w9, bias, df, *, B, N, DFF, tc):
    nj = DFF // tc

    def body(fg_ref, fv_ref, w_ref, b_ref, df_ref, dfg_ref, dfv_ref, dwb_ref):
        taps = _grid_taps(N)
        x = fg_ref[...]
        fg, vjp = jax.vjp(lambda a: _gelu(a), _conv_fwd(x, w_ref[...], taps) + b_ref[...])
        dfl = df_ref[...]
        dfv_ref[...] = (dfl * fg).astype(BF16)
        (dpre,) = vjp(dfl * fv_ref[...])
        dx, dw = _conv_bwd(x, w_ref[...], taps, dpre)
        dfg_ref[...] = dx.astype(BF16)

        @pl.when(pl.program_id(1) == 0)
        def _():
            dwb_ref[...] = jnp.zeros_like(dwb_ref)

        dwb_ref[0:9, :] += dw
        dwb_ref[9:10, :] += jnp.sum(dpre, axis=0, keepdims=True)

    col = pl.BlockSpec((N, tc), lambda j, b: (b, j))
    return pl.pallas_call(
        body, name="ffn_act_bwd", grid=(nj, B),
        in_specs=[col, pl.BlockSpec((N, tc), lambda j, b: (b, nj + j)),
                  pl.BlockSpec((9, tc), lambda j, b: (0, j)), pl.BlockSpec((1, tc), lambda j, b: (0, j)), col],
        out_specs=[col, col, pl.BlockSpec((16, tc), lambda j, b: (0, j))],
        out_shape=[jax.ShapeDtypeStruct((B * N, DFF), BF16), jax.ShapeDtypeStruct((B * N, DFF), BF16),
                   jax.ShapeDtypeStruct((16, DFF), F32)],
        compiler_params=_params(("parallel", "arbitrary")),
    )(F, F, w9, bias, df)


def _dnprep_math(y, is_qk, scale):
    s = _silu(y)
    n = s * lax.rsqrt(jnp.sum(s * s, axis=-1, keepdims=True) + EPS) * scale
    return jnp.where(is_qk, n, s)


def _dnprep_fwd(p, cw, *, B, T, nc, H, HD):
    def body(x_ref, w_ref, o_ref):
        j = pl.program_id(1)
        y = _conv_fwd(x_ref[...], w_ref[...], _seg_taps(T, nc, 4, 2))
        o_ref[...] = _dnprep_math(y, j < 2 * H, jnp.where(j < H, HD ** -0.5, 1.0))

    return pl.pallas_call(
        body, name="dnprep_fwd", grid=(B, 3 * H),
        in_specs=[pl.BlockSpec((T, HD), lambda b, j: (b, j)), pl.BlockSpec((4, HD), lambda b, j: (0, j))],
        out_specs=pl.BlockSpec((T, HD), lambda b, j: (b, j)),
        out_shape=jax.ShapeDtypeStruct((B * T, 3 * H * HD), F32),
        compiler_params=_params(("parallel", "parallel")),
    )(p, cw)


def _dnprep_bwd(p, cw, dqkv, dp, *, B, T, nc, H, HD):
    def body(x_ref, w_ref, d_ref, dp_any, dp_ref, dcw_ref):
        j = pl.program_id(0)
        taps = _seg_taps(T, nc, 4, 2)
        x = x_ref[...]
        y = _conv_fwd(x, w_ref[...], taps)
        is_qk, scale = j < 2 * H, jnp.where(j < H, HD ** -0.5, 1.0)
        _, vjp = jax.vjp(lambda a: _dnprep_math(a, is_qk, scale), y)
        (dy,) = vjp(d_ref[0])
        dx, dw = _conv_bwd(x, w_ref[...], taps, dy)
        dp_ref[...] = dx.astype(BF16)

        @pl.when(pl.program_id(1) == 0)
        def _():
            dcw_ref[...] = jnp.zeros_like(dcw_ref)

        dcw_ref[0:4, :] += dw

    col = pl.BlockSpec((T, HD), lambda j, b: (b, j))
    return pl.pallas_call(
        body, name="dnprep_bwd", grid=(3 * H, B),
        in_specs=[col, pl.BlockSpec((4, HD), lambda j, b: (0, j)),
                  pl.BlockSpec((1, T, HD), lambda j, b: (j // H, b, j % H)), pl.BlockSpec(memory_space=pl.ANY)],
        out_specs=[col, pl.BlockSpec((8, HD), lambda j, b: (0, j))],
        out_shape=[jax.ShapeDtypeStruct(dp.shape, dp.dtype), jax.ShapeDtypeStruct((8, 3 * H * HD), F32)],
        input_output_aliases={3: 0},
        compiler_params=_params(("parallel", "arbitrary")),
    )(p, cw, dqkv, dp)


def _gb_math(ab, alog, dtb, H):
    lane = lax.broadcasted_iota(jnp.int32, ab.shape, 1)
    g = -jnp.exp(alog) * _softplus(ab + dtb)
    return jnp.where(lane < 2 * H, g, jnp.where(lane < 4 * H, _sigmoid(ab), 0.0))


def _gb_fwd(p, prm, *, rows, col0, H, tm):
    def body(x_ref, prm_ref, o_ref):
        o_ref[...] = _gb_math(x_ref[...], prm_ref[0:1], prm_ref[1:2], H)

    return pl.pallas_call(
        body, name="gb_fwd", grid=(rows // tm,),
        in_specs=[pl.BlockSpec((tm, LANES), lambda t: (t, col0 // LANES)), pl.BlockSpec((8, LANES), lambda t: (0, 0))],
        out_specs=pl.BlockSpec((tm, LANES), lambda t: (t, 0)),
        out_shape=jax.ShapeDtypeStruct((rows, LANES), F32),
        compiler_params=_params(("parallel",)),
    )(p, prm)


def _gb_bwd(p, prm, dgb, dp, *, rows, col0, H, tm):
    def body(x_ref, prm_ref, d_ref, dp_any, dp_ref, dprm_ref):
        _, vjp = jax.vjp(lambda a, b, c: _gb_math(a, b, c, H), x_ref[...], prm_ref[0:1], prm_ref[1:2])
        dab, dalog, ddtb = vjp(d_ref[...])
        dp_ref[...] = dab.astype(BF16)

        @pl.when(pl.program_id(0) == 0)
        def _():
            dprm_ref[...] = jnp.zeros_like(dprm_ref)

        dprm_ref[0:1, :] += dalog
        dprm_ref[1:2, :] += ddtb

    blk = pl.BlockSpec((tm, LANES), lambda t: (t, col0 // LANES))
    return pl.pallas_call(
        body, name="gb_bwd", grid=(rows // tm,),
        in_specs=[blk, pl.BlockSpec((8, LANES), lambda t: (0, 0)), pl.BlockSpec((tm, LANES), lambda t: (t, 0)),
                  pl.BlockSpec(memory_space=pl.ANY)],
        out_specs=[blk, pl.BlockSpec((8, LANES), lambda t: (0, 0))],
        out_shape=[jax.ShapeDtypeStruct(dp.shape, dp.dtype), jax.ShapeDtypeStruct((8, LANES), F32)],
        input_output_aliases={3: 0},
        compiler_params=_params(("arbitrary",)),
    )(p, prm, dgb, dp)


def _lru_scan(a_ref, b_ref, h_ref, hp_ref, segs):
    C = a_ref.shape[1]
    row = lax.broadcasted_iota(jnp.int32, (SUBLANES, C), 0)
    carry = jnp.zeros((1, C), F32)
    for start, rows, reverse in segs:
        nb = rows // SUBLANES

        def blk(i, carry, start=start, nb=nb, reverse=reverse):
            r0 = pl.multiple_of(start + (nb - 1 - i if reverse else i) * SUBLANES, SUBLANES)
            A = a_ref[pl.ds(r0, SUBLANES), :]
            Bv = b_ref[pl.ds(r0, SUBLANES), :]
            for s in (1, 2, 4):
                sh = SUBLANES - s if reverse else s
                m = (row < SUBLANES - s) if reverse else (row >= s)
                Bv = jnp.where(m, A * pltpu.roll(Bv, sh, 0) + Bv, Bv)
                A = jnp.where(m, A * pltpu.roll(A, sh, 0), A)
            Hv = Bv + A * carry
            h_ref[pl.ds(r0, SUBLANES), :] = Hv
            if hp_ref is not None:
                if reverse:
                    hp = jnp.where(row < SUBLANES - 1, pltpu.roll(Hv, SUBLANES - 1, 0), carry)
                else:
                    hp = jnp.where(row >= 1, pltpu.roll(Hv, 1, 0), carry)
                hp_ref[pl.ds(r0, SUBLANES), :] = hp
            return Hv[0:1] if reverse else Hv[SUBLANES - 1:SUBLANES]

        carry = lax.fori_loop(0, nb, blk, carry)


def _lru_orders(T, nc, d):
    if d == 0:
        return [(0, T, False)], [(0, T, True)]
    return [(0, nc, True), (nc, T - nc, True)], [(nc, T - nc, False), (0, nc, False)]


def _bdot(a, b, dims=(((1,), (0,)), ((), ()))):
    return lax.dot_general(a.astype(BF16), b.astype(BF16), dims, preferred_element_type=F32)


_NT = (((1,), (1,)), ((), ()))
_TN = (((0,), (0,)), ((), ()))


def _blockdiag(w, C):
    nd, nb, bd, _ = w.shape
    per = C // bd
    out = jnp.einsum('dnpij,pq->dnpiqj', w.reshape(nd, nb // per, per, bd, bd), jnp.eye(per, dtype=w.dtype))
    return out.reshape(nd, nb // per, C, C)


def _blockdiag_extract(dw, bd):
    nd, nj, C, _ = dw.shape
    per = C // bd
    out = jnp.einsum('dnpiqj,pq->dnpij', dw.reshape(nd, nj, per, bd, per, bd), jnp.eye(per, dtype=dw.dtype))
    return out.reshape(nd, nj * per, bd, bd)


def _lru_fwd(p, cw, lv, wr, wi, *, B, T, nc, LW, col0, C):
    N = T - nc
    nj = LW // C

    def body(x_ref, cw_ref, lv_ref, wr_ref, wi_ref, o_ref, a_s, b_s, h_s, acc_s):
        lv_ = lv_ref[...]
        xc = _conv_fwd(x_ref[:, 0:C], cw_ref[...], _seg_taps(T, nc, 4, 2)) + lv_[0:1]
        for d in (0, 1):
            r = _sigmoid(_bdot(xc, wr_ref[d, 0]) + lv_[1 + d:2 + d])
            i = _sigmoid(_bdot(xc, wi_ref[d, 0]) + lv_[3 + d:4 + d])
            la = -LRU_C * r * _softplus(-lv_[5 + d:6 + d])
            a_s[...] = jnp.exp(la)
            b_s[...] = jnp.sqrt(1.0 - jnp.exp(2.0 * la)) * i * xc
            _lru_scan(a_s, b_s, h_s, None, _lru_orders(T, nc, d)[0])
            if d == 0:
                acc_s[...] = h_s[...]
            else:
                acc_s[...] += h_s[...]
        o_ref[...] = (acc_s[nc:, :] * _gelu(x_ref[nc:, C:2 * C])).astype(BF16)

    return pl.pallas_call(
        body, name="lru_fwd", grid=(B, nj),
        in_specs=[pl.BlockSpec((T, 2 * C), lambda b, j: (b, col0 // (2 * C) + j)),
                  pl.BlockSpec((4, C), lambda b, j: (0, j)), pl.BlockSpec((8, C), lambda b, j: (0, j)),
                  pl.BlockSpec((2, 1, C, C), lambda b, j: (0, j, 0, 0)), pl.BlockSpec((2, 1, C, C), lambda b, j: (0, j, 0, 0))],
        out_specs=pl.BlockSpec((N, C), lambda b, j: (b, j)),
        out_shape=jax.ShapeDtypeStruct((B * N, LW), BF16),
        scratch_shapes=[pltpu.VMEM((T, C), F32)] * 4,
        compiler_params=_params(("parallel", "parallel")),
    )(p, cw, lv, wr, wi)


def _lru_bwd(p, cw, lv, wr, wi, dy, dp, *, B, T, nc, LW, col0, C):
    N = T - nc
    nj = LW // C

    def body(x_ref, cw_ref, lv_ref, wr_ref, wi_ref, dy_ref, dp_any, dp_ref, dcw_ref, dlv_ref, dwr_ref, dwi_ref,
             a_s, b_s, h_s, hp_s, mu_s, mup_s, dh_s, dxc_s, hsum_s):
        taps = _seg_taps(T, nc, 4, 2)
        lv_ = lv_ref[...]
        xl = x_ref[:, 0:C]
        xc = _conv_fwd(xl, cw_ref[...], taps) + lv_[0:1]
        gel, gelu_vjp = jax.vjp(_gelu, x_ref[nc:, C:2 * C])
        dh_s[0:nc, :] = jnp.zeros((nc, C), F32)
        dh_s[nc:, :] = dy_ref[...] * gel
        dxc_s[...] = jnp.zeros_like(dxc_s)

        @pl.when(pl.program_id(1) == 0)
        def _():
            dcw_ref[...] = jnp.zeros_like(dcw_ref)
            dlv_ref[...] = jnp.zeros_like(dlv_ref)
            dwr_ref[...] = jnp.zeros_like(dwr_ref)
            dwi_ref[...] = jnp.zeros_like(dwi_ref)

        for d in (0, 1):
            fwd_order, adj_order = _lru_orders(T, nc, d)
            lam = lv_[5 + d:6 + d]
            r = _sigmoid(_bdot(xc, wr_ref[d, 0]) + lv_[1 + d:2 + d])
            i = _sigmoid(_bdot(xc, wi_ref[d, 0]) + lv_[3 + d:4 + d])
            sp = _softplus(-lam)
            la = -LRU_C * r * sp
            a = jnp.exp(la)
            e2 = jnp.exp(2.0 * la)
            mult = jnp.sqrt(1.0 - e2)
            a_s[...] = a
            b_s[...] = mult * i * xc
            _lru_scan(a_s, b_s, h_s, hp_s, fwd_order)
            if d == 0:
                hsum_s[...] = h_s[...]
            else:
                hsum_s[...] += h_s[...]
            b_s[...] = a * dh_s[...]
            _lru_scan(a_s, b_s, mu_s, mup_s, adj_order)
            dinp = dh_s[...] + mup_s[...]
            da = dinp * hp_s[...]
            dmult = dinp * i * xc
            di = dinp * mult * xc
            dla = da * a - dmult * e2 / mult
            dpre_r = (dla * (-LRU_C * sp)) * r * (1.0 - r)
            dpre_i = di * i * (1.0 - i)
            dsp = jnp.sum(dla * (-LRU_C * r), axis=0, keepdims=True)
            dxc_s[...] += dinp * mult * i + _bdot(dpre_r, wr_ref[d, 0], _NT) + _bdot(dpre_i, wi_ref[d, 0], _NT)
            dwr_ref[d, 0] += _bdot(xc, dpre_r, _TN)
            dwi_ref[d, 0] += _bdot(xc, dpre_i, _TN)
            dlv_ref[1 + d:2 + d, :] += jnp.sum(dpre_r, axis=0, keepdims=True)
            dlv_ref[3 + d:4 + d, :] += jnp.sum(dpre_i, axis=0, keepdims=True)
            dlv_ref[5 + d:6 + d, :] += -dsp * _sigmoid(-lam)

        dxc = dxc_s[...]
        dxl, dw = _conv_bwd(xl, cw_ref[...], taps, dxc)
        dcw_ref[0:4, :] += dw
        dlv_ref[0:1, :] += jnp.sum(dxc, axis=0, keepdims=True)
        dp_ref[:, 0:C] = dxl.astype(BF16)
        (dyl,) = gelu_vjp(dy_ref[...] * hsum_s[nc:, :])
        dp_ref[0:nc, C:2 * C] = jnp.zeros((nc, C), BF16)
        dp_ref[nc:, C:2 * C] = dyl.astype(BF16)

    xblk = pl.BlockSpec((T, 2 * C), lambda j, b: (b, col0 // (2 * C) + j))
    wblk = pl.BlockSpec((2, 1, C, C), lambda j, b: (0, j, 0, 0))
    vblk = pl.BlockSpec((8, C), lambda j, b: (0, j))
    return pl.pallas_call(
        body, name="lru_bwd", grid=(nj, B),
        in_specs=[xblk, pl.BlockSpec((4, C), lambda j, b: (0, j)), vblk, wblk, wblk,
                  pl.BlockSpec((N, C), lambda j, b: (b, j)), pl.BlockSpec(memory_space=pl.ANY)],
        out_specs=[xblk, vblk, vblk, wblk, wblk],
        out_shape=[jax.ShapeDtypeStruct(dp.shape, dp.dtype), jax.ShapeDtypeStruct((8, LW), F32),
                   jax.ShapeDtypeStruct((8, LW), F32), jax.ShapeDtypeStruct((2, nj, C, C), F32),
                   jax.ShapeDtypeStruct((2, nj, C, C), F32)],
        scratch_shapes=[pltpu.VMEM((T, C), F32)] * 9,
        input_output_aliases={6: 0},
        compiler_params=_params(("parallel", "arbitrary")),
    )(p, cw, lv, wr, wi, dy, dp)


def _chunk_masks(upper):
    i = lax.broadcasted_iota(jnp.int32, (CHUNK, CHUNK), 0)
    j = lax.broadcasted_iota(jnp.int32, (CHUNK, CHUNK), 1)
    return i == j, (j >= i) if upper else (j <= i), (j > i) if upper else (j < i)


def _col2row(c, eye):
    return jnp.sum(jnp.where(eye, c, 0.0), axis=0, keepdims=True)


def _row2col(r, eye):
    return jnp.sum(jnp.where(eye, r, 0.0), axis=1, keepdims=True)


def _rowsum(x):
    return jnp.sum(x, axis=1, keepdims=True)


def _unit_tri_inverses(Ls):
    G = len(Ls)
    W = G * CHUNK
    blk = (lax.broadcasted_iota(jnp.int32, (W, W), 0) // CHUNK) == (lax.broadcasted_iota(jnp.int32, (W, W), 1) // CHUNK)

    def pdot(a, b):
        bd = jnp.where(blk, jnp.tile(b.astype(BF16), (G, 1)), jnp.zeros((), BF16))
        return jnp.dot(a.astype(BF16), bd, preferred_element_type=F32)

    Xp = -(Ls[0] if G == 1 else jnp.concatenate(Ls, axis=1))
    Rm = Xp
    for _ in range(int(math.log2(CHUNK)) - 1):
        Xp = pdot(Xp, Xp)
        Rm = Rm + Xp + pdot(Rm, Xp)
    eye = _chunk_masks(False)[0]
    return [jnp.where(eye, 1.0, 0.0) + Rm[:, g * CHUNK:(g + 1) * CHUNK] for g in range(G)]


def _delta_chunk_common(q, k, v, gcol, bcol, upper):
    eye, incl, strict = _chunk_masks(upper)
    gc = _rowsum(jnp.where(incl, _col2row(gcol, eye), 0.0))
    D = jnp.where(incl, jnp.exp(jnp.minimum(gc - _col2row(gc, eye), 0.0)), 0.0)
    kb = k * bcol
    AP = _bdot(jnp.concatenate([kb, q], axis=0), k, _NT)
    A = AP[:CHUNK]
    L = jnp.where(strict, A * D, 0.0)
    eg = jnp.exp(gc)
    gl = jnp.sum(gcol, axis=0, keepdims=True)
    attn = jnp.where(incl, AP[CHUNK:] * D, 0.0)
    return dict(eye=eye, incl=incl, strict=strict, gc=gc, D=D, kb=kb, A=A, L=L, eg=eg, gl=gl, egl=jnp.exp(gl),
                attn=attn, kbe=kb * eg, vb=v * bcol, qe=q * eg, kd=k * jnp.exp(gl - gc))


def _delta_group_pre(chunks, upper):
    cs = [_delta_chunk_common(*ch, upper) for ch in chunks]
    out = []
    for c, Tm in zip(cs, _unit_tri_inverses([c["L"] for c in cs])):
        dk = c["kbe"].shape[1]
        wu = _bdot(Tm, jnp.concatenate([c["kbe"], c["vb"]], axis=1))
        KN = _bdot(c["kd"], wu, _TN)
        QO = _bdot(c["attn"], wu)
        out.append((Tm, KN[:, :dk], KN[:, dk:], c["qe"] - QO[:, :dk], QO[:, dk:], c["egl"]))
    return out


def _delta_chunk_bwd(q, k, v, gcol, bcol, S, Tm, do, dS2, upper):
    c = _delta_chunk_common(q, k, v, gcol, bcol, upper)
    eye, incl, strict, D, eg, egl = c["eye"], c["incl"], c["strict"], c["D"], c["eg"], c["egl"]
    kb, kbe, vb, qe, kd, attn = c["kb"], c["kbe"], c["vb"], c["qe"], c["kd"], c["attn"]
    dkk = kbe.shape[1]
    wu = _bdot(Tm, jnp.concatenate([kbe, vb], axis=1))
    w = wu[:, :dkk]
    vn = wu[:, dkk:] - _bdot(w, S)
    dvn = _bdot(kd, dS2) + _bdot(attn, do, _TN)
    dkd = _bdot(vn, dS2, _NT)
    dgl = jnp.sum(_rowsum(dS2 * S), axis=0, keepdims=True) * egl
    dqa = _bdot(do, jnp.concatenate([S, vn], axis=0), _NT)
    dqe = dqa[:, :dkk]
    dattn = jnp.where(incl, dqa[:, dkk:], 0.0)
    dw = -_bdot(dvn, S, _NT)
    r = _rowsum(dkd * kd)
    dk = dkd * jnp.exp(c["gl"] - c["gc"])
    dgl = dgl + jnp.sum(r, axis=0, keepdims=True)
    dgc = _rowsum(dqe * qe) - r
    dq = dqe * eg + _bdot(dattn * D, k)
    dk = dk + _bdot(dattn * D, q, _TN)
    E = dattn * attn
    dvw = jnp.concatenate([dvn, dw], axis=1)
    dTm = _bdot(dvw, jnp.concatenate([vb, kbe], axis=1), _NT)
    dvk = _bdot(Tm, dvw, _TN)
    dvb = dvk[:, :dvn.shape[1]]
    dv = dvb * bcol
    dbeta = _rowsum(dvb * v)
    dkbe = dvk[:, dvn.shape[1]:]
    dkb = dkbe * eg
    dgc = dgc + _rowsum(dkbe * kbe)
    dL = jnp.where(strict, -_bdot(Tm, _bdot(dTm, Tm, _NT), _TN), 0.0)
    dA = dL * D
    E = E + dL * c["L"]
    dkb = dkb + _bdot(dA, k)
    dk = dk + _bdot(dA, kb, _TN) + dkb * bcol
    dbeta = dbeta + _rowsum(dkb * k)
    dgc = dgc + _rowsum(E) - _row2col(jnp.sum(E, axis=0, keepdims=True), eye)
    dg = _row2col(jnp.sum(jnp.where(incl, dgc, 0.0), axis=0, keepdims=True), eye) + dgl
    return dq, dk, dv, dg, dbeta


def _delta_group(n):
    return max(g for g in (4, 3, 2, 1) if n % g == 0)


def _delta_chunk_at(T, nc, d, i):
    n, ncc = T // CHUNK, nc // CHUNK
    return i if d == 0 else jnp.where(i < ncc, ncc - 1 - i, n - 1 - (i - ncc))


def _dn_out_math(o, onorm, z):
    return _rmsn(o, onorm) * _silu(z)


def _delta_fwd(qkv, gb, p, onorm, *, B, T, nc, H, HD):
    N = T - nc
    n = T // CHUNK
    G = _delta_group(n)

    def body(q_ref, k_ref, v_ref, gb_ref, z_ref, on_ref, y_ref, o_ref, K_s, N_s, Qp_s, O0_s, eg_s, o_s):
        h = pl.program_id(1)
        lane = lax.broadcasted_iota(jnp.int32, (CHUNK, LANES), 1)

        def pre(g, carry):
            cs = [g * G + i for i in range(G)]
            rows = [pl.ds(pl.multiple_of(c * CHUNK, CHUNK), CHUNK) for c in cs]
            for d in (0, 1):
                chunks = []
                for r in rows:
                    gbb = gb_ref[r, :]
                    chunks.append((q_ref[r, :], k_ref[r, :], v_ref[r, :],
                                   _rowsum(jnp.where(lane == d * H + h, gbb, 0.0)),
                                   _rowsum(jnp.where(lane == 2 * H + d * H + h, gbb, 0.0))))
                for c, r, (_, K, Nn, Qp, O0, egl) in zip(cs, rows, _delta_group_pre(chunks, d == 1)):
                    K_s[d * n + c] = K.astype(BF16)
                    N_s[d * n + c] = Nn
                    Qp_s[d, r, :] = Qp.astype(BF16)
                    O0_s[d, r, :] = O0
                    eg_s[d * n + c] = jnp.broadcast_to(egl, (SUBLANES, HD))
            return carry

        lax.fori_loop(0, n // G, pre, 0)

        def step(i, Ss):
            out = []
            for d in (0, 1):
                c = _delta_chunk_at(T, nc, d, i)
                rows = pl.ds(pl.multiple_of(c * CHUNK, CHUNK), CHUNK)
                Sb = Ss[d].astype(BF16)
                o_s[d, rows, :] = jnp.dot(Qp_s[d, rows, :], Sb, preferred_element_type=F32) + O0_s[d, rows, :]
                out.append(eg_s[d * n + c][0:1] * Ss[d] + N_s[d * n + c]
                           - jnp.dot(K_s[d * n + c], Sb, preferred_element_type=F32))
            return tuple(out)

        lax.fori_loop(0, n, step, (jnp.zeros((HD, HD), F32), jnp.zeros((HD, HD), F32)))
        o = o_s[0, nc:, :] + o_s[1, nc:, :]
        o_ref[...] = o
        y_ref[...] = _dn_out_math(o, on_ref[...], z_ref[nc:, :]).astype(BF16)

    col = lambda off: pl.BlockSpec((T, HD), lambda b, h: (b, off + h))
    lat = pl.BlockSpec((N, HD), lambda b, h: (b, h))
    return pl.pallas_call(
        body, name="delta_fwd", grid=(B, H),
        in_specs=[col(0), col(H), col(2 * H), pl.BlockSpec((T, LANES), lambda b, h: (b, 0)), col(3 * H),
                  pl.BlockSpec((1, HD), lambda b, h: (0, 0))],
        out_specs=[lat, lat],
        out_shape=[jax.ShapeDtypeStruct((B * N, H * HD), BF16), jax.ShapeDtypeStruct((B * N, H * HD), F32)],
        scratch_shapes=[pltpu.VMEM((2 * n, HD, HD), BF16), pltpu.VMEM((2 * n, HD, HD), F32),
                        pltpu.VMEM((2, T, HD), BF16), pltpu.VMEM((2, T, HD), F32),
                        pltpu.VMEM((2 * n, SUBLANES, HD), F32), pltpu.VMEM((2, T, HD), F32)],
        compiler_params=_params(("parallel", "parallel")),
    )(qkv, qkv, qkv, gb, p, onorm)


def _delta_bwd(qkv, gb, p, onorm, o, dy, dp, *, B, T, nc, H, HD):
    N = T - nc
    n = T // CHUNK
    G = _delta_group(n)

    def body(q_ref, k_ref, v_ref, gb_ref, z_ref, on_ref, o_ref, dy_ref, dp_any, dqkv_ref, dgb_ref, dp_ref, don_ref,
             do_s, Tm_s, K_s, N_s, R_s, eg_s, S_s, dS_s):
        h = pl.program_id(1)
        lane = lax.broadcasted_iota(jnp.int32, (CHUNK, LANES), 1)

        def cols(rows, d):
            gbb = gb_ref[rows, :]
            return (_rowsum(jnp.where(lane == d * H + h, gbb, 0.0)),
                    _rowsum(jnp.where(lane == 2 * H + d * H + h, gbb, 0.0)))

        _, vjp = jax.vjp(_dn_out_math, o_ref[...], on_ref[...], z_ref[nc:, :])
        do, don, dz = vjp(dy_ref[...])
        do_s[0:nc, :] = jnp.zeros((nc, HD), F32)
        do_s[nc:, :] = do
        dp_ref[0:nc, :] = jnp.zeros((nc, HD), BF16)
        dp_ref[nc:, :] = dz.astype(BF16)

        @pl.when(h == 0)
        def _():
            don_ref[...] = jnp.zeros_like(don_ref)
            dgb_ref[...] = jnp.zeros_like(dgb_ref)

        don_ref[0, 0:1, :] += don

        for d in (0, 1):
            def pre(g, carry, d=d):
                cs = [g * G + i for i in range(G)]
                rows = [pl.ds(pl.multiple_of(c * CHUNK, CHUNK), CHUNK) for c in cs]
                chunks = [(q_ref[r, :], k_ref[r, :], v_ref[r, :], *cols(r, d)) for r in rows]
                for c, r, (Tm, K, Nn, Qp, _, egl) in zip(cs, rows, _delta_group_pre(chunks, d == 1)):
                    Tm_s[c] = Tm
                    K_s[c] = K.astype(BF16)
                    N_s[c] = Nn
                    R_s[c] = _bdot(Qp, do_s[r, :], _TN)
                    eg_s[c] = jnp.broadcast_to(egl, (SUBLANES, HD))
                return carry

            lax.fori_loop(0, n // G, pre, 0)

            def fwd_step(i, S, d=d):
                c = _delta_chunk_at(T, nc, d, i)
                S_s[c] = S
                return eg_s[c][0:1] * S + N_s[c] - jnp.dot(K_s[c], S.astype(BF16), preferred_element_type=F32)

            lax.fori_loop(0, n, fwd_step, jnp.zeros((HD, HD), F32))

            def bwd_step(i, dS, d=d):
                c = _delta_chunk_at(T, nc, d, n - 1 - i)
                dS_s[c] = dS
                return (eg_s[c][0:1] * dS + R_s[c]
                        - lax.dot_general(K_s[c], dS.astype(BF16), _TN, preferred_element_type=F32))

            lax.fori_loop(0, n, bwd_step, jnp.zeros((HD, HD), F32))

            def grads(c, carry, d=d):
                rows = pl.ds(pl.multiple_of(c * CHUNK, CHUNK), CHUNK)
                gcol, bcol = cols(rows, d)
                dq, dk, dv, dg, dbeta = _delta_chunk_bwd(q_ref[rows, :], k_ref[rows, :], v_ref[rows, :], gcol, bcol,
                                                         S_s[c], Tm_s[c], do_s[rows, :], dS_s[c], d == 1)
                if d == 0:
                    dqkv_ref[0, rows, :] = dq
                    dqkv_ref[1, rows, :] = dk
                    dqkv_ref[2, rows, :] = dv
                else:
                    dqkv_ref[0, rows, :] += dq
                    dqkv_ref[1, rows, :] += dk
                    dqkv_ref[2, rows, :] += dv
                dgb_ref[rows, :] += (jnp.where(lane == d * H + h, dg, 0.0)
                                     + jnp.where(lane == 2 * H + d * H + h, dbeta, 0.0))
                return carry

            lax.fori_loop(0, n, grads, 0)

    col = lambda off: pl.BlockSpec((T, HD), lambda b, h: (b, off + h))
    lat = pl.BlockSpec((N, HD), lambda b, h: (b, h))
    return pl.pallas_call(
        body, name="delta_bwd", grid=(B, H),
        in_specs=[col(0), col(H), col(2 * H), pl.BlockSpec((T, LANES), lambda b, h: (b, 0)), col(3 * H),
                  pl.BlockSpec((1, HD), lambda b, h: (0, 0)), lat, lat, pl.BlockSpec(memory_space=pl.ANY)],
        out_specs=[pl.BlockSpec((3, T, HD), lambda b, h: (0, b, h)), pl.BlockSpec((T, LANES), lambda b, h: (b, 0)),
                   col(3 * H), pl.BlockSpec((1, 8, HD), lambda b, h: (b, 0, 0))],
        out_shape=[jax.ShapeDtypeStruct((3, B * T, H * HD), F32), jax.ShapeDtypeStruct((B * T, LANES), F32),
                   jax.ShapeDtypeStruct(dp.shape, dp.dtype), jax.ShapeDtypeStruct((B, 8, HD), F32)],
        scratch_shapes=[pltpu.VMEM((T, HD), F32), pltpu.VMEM((n, CHUNK, CHUNK), F32), pltpu.VMEM((n, HD, HD), BF16),
                        pltpu.VMEM((n, HD, HD), F32), pltpu.VMEM((n, HD, HD), F32), pltpu.VMEM((n, SUBLANES, HD), F32),
                        pltpu.VMEM((n, HD, HD), F32), pltpu.VMEM((n, HD, HD), F32)],
        input_output_aliases={8: 2},
        compiler_params=_params(("parallel", "arbitrary")),
    )(qkv, qkv, qkv, gb, p, onorm, o, dy, dp)


def _rowwise(fn, ins, out_dtypes, *, name, tm=256, mult=16):
    R, W = ins[0].shape
    tm = _tile(R, tm, mult)

    def body(*refs):
        outs = fn(*[r[...] for r in refs[:len(ins)]])
        for o_ref, o in zip(refs[len(ins):], outs):
            o_ref[...] = o.astype(o_ref.dtype)

    spec = pl.BlockSpec((tm, W), lambda i: (i, 0))
    return pl.pallas_call(
        body, name=name, grid=(R // tm,), in_specs=[spec] * len(ins), out_specs=[spec] * len(out_dtypes),
        out_shape=[jax.ShapeDtypeStruct((R, W), dt) for dt in out_dtypes],
        compiler_params=_params(("parallel",)),
    )(*ins)


def _sum_lead(x, *, name, tm=256, mult=16):
    S, R, W = x.shape
    tm = _tile(R, tm, mult)

    def body(*refs):
        acc = refs[0][0].astype(F32)
        for r in refs[1:S]:
            acc = acc + r[0].astype(F32)
        refs[S][...] = acc

    return pl.pallas_call(
        body, name=name, grid=(R // tm,),
        in_specs=[pl.BlockSpec((1, tm, W), functools.partial(lambda s, i: (s, i, 0), s)) for s in range(S)],
        out_specs=pl.BlockSpec((tm, W), lambda i: (i, 0)),
        out_shape=jax.ShapeDtypeStruct((R, W), F32),
        compiler_params=_params(("parallel",)),
    )(*([x] * S))


def _adamw_math(w, g, m, v):
    m = ADAM_B1 * m + (1.0 - ADAM_B1) * g
    v = ADAM_B2 * v + (1.0 - ADAM_B2) * (g * g)
    m_hat = m / (1.0 - ADAM_B1 ** ADAM_STEP)
    v_hat = v / (1.0 - ADAM_B2 ** ADAM_STEP)
    return -ADAM_LR * (m_hat / (jnp.sqrt(v_hat) + ADAM_EPS) + ADAM_WD * w), m, v


def _adamw(w, g, m, v, *, name):
    tm = max(SUBLANES, (256 * 1024) // w.shape[1] // SUBLANES * SUBLANES)
    return _rowwise(_adamw_math, [w, g, m, v], [F32, F32, F32], name=name, tm=tm, mult=SUBLANES)


def _me():
    return lax.axis_index("x"), lax.axis_index("y"), lax.axis_index("c")


def _allgather_small(v):
    R, W = v.shape

    def body(x_ref, out_ref, send_sems, recv_sems, local_sem):
        x, y, c = _me()
        me, sibling = (x, y, c), (x, y, 1 - c)
        chips = [(1 - x, y), (x, 1 - y), (1 - x, 1 - y)]

        def slot(px, py, pc):
            return out_ref.at[4 * px + 2 * py + pc]

        def copy(k, block, to, src=None):
            return pltpu.make_async_remote_copy(
                src_ref=slot(*block) if src is None else src, dst_ref=slot(*block),
                send_sem=send_sems.at[k], recv_sem=recv_sems.at[k], device_id=to, device_id_type=MESH)

        mine = pltpu.make_async_copy(x_ref, slot(*me), local_sem)
        mine.start()
        first = [copy(0, me, sibling, src=x_ref)]
        first += [copy(1 + j, me, (*chip, c), src=x_ref) for j, chip in enumerate(chips)]
        for cp in first:
            cp.start()
        passed = [copy(4 + j, (*chip, c), sibling) for j, chip in enumerate(chips)]
        for j, chip in enumerate(chips):
            copy(1 + j, (*chip, c), me).wait_recv()
            passed[j].start()
        copy(0, sibling, me).wait_recv()
        for j, chip in enumerate(chips):
            copy(4 + j, (*chip, 1 - c), me).wait_recv()
        for cp in first + passed:
            cp.wait_send()
        mine.wait()

    return pl.pallas_call(
        body, name="allgather_small", out_shape=jax.ShapeDtypeStruct((8, R, W), v.dtype),
        in_specs=[pl.BlockSpec(memory_space=pltpu.VMEM)], out_specs=pl.BlockSpec(memory_space=pltpu.VMEM),
        scratch_shapes=[pltpu.SemaphoreType.DMA((7,)), pltpu.SemaphoreType.DMA((7,)), pltpu.SemaphoreType.DMA],
        compiler_params=_params(),
    )(v)


_ANY = pl.BlockSpec(memory_space=pl.ANY)


def _allgather_halves(shards, *, name):
    nw = len(shards)

    def body(*refs):
        x_refs, out_refs = refs[:nw], refs[nw:2 * nw]
        send_sems, recv_sems, local_sems = refs[2 * nw:]
        x, y, c = _me()
        me, sibling = (x, y, c), (x, y, 1 - c)
        chips = [(1 - x, y), (x, 1 - y), (1 - x, 1 - y)]

        def slot(w, px, py, pc):
            return out_refs[w].at[4 * px + 2 * py + pc]

        def copy(w, k, block, to, src=None):
            return pltpu.make_async_remote_copy(
                src_ref=slot(w, *block) if src is None else src, dst_ref=slot(w, *block),
                send_sem=send_sems.at[w, k], recv_sem=recv_sems.at[w, k], device_id=to, device_id_type=MESH)

        started, local = [], []
        for w in range(nw):
            half = shards[w].shape[0] // 2
            own = x_refs[w].at[pl.ds(c * half, half), :]
            mine = pltpu.make_async_copy(own, slot(w, *me), local_sems.at[w])
            mine.start()
            first = [copy(w, 0, me, sibling, src=own)]
            first += [copy(w, 1 + j, me, (*chip, c), src=own) for j, chip in enumerate(chips)]
            for cp in first:
                cp.start()
            started += first
            local.append(mine)
        for w in range(nw):
            for j, chip in enumerate(chips):
                copy(w, 1 + j, (*chip, c), me).wait_recv()
                fwd = copy(w, 4 + j, (*chip, c), sibling)
                fwd.start()
                started.append(fwd)
        for w in range(nw):
            copy(w, 0, sibling, me).wait_recv()
            for j, chip in enumerate(chips):
                copy(w, 4 + j, (*chip, 1 - c), me).wait_recv()
        for cp in started:
            cp.wait_send()
        for cp in local:
            cp.wait()

    return pl.pallas_call(
        body, name=name,
        out_shape=[jax.ShapeDtypeStruct((8, s.shape[0] // 2, s.shape[1]), s.dtype) for s in shards],
        in_specs=[_ANY] * nw, out_specs=[_ANY] * nw,
        scratch_shapes=[pltpu.SemaphoreType.DMA((nw, 7)), pltpu.SemaphoreType.DMA((nw, 7)), pltpu.SemaphoreType.DMA((nw,))],
        compiler_params=_params(),
    )(*shards)


def _sibling_send_halves(arrs, *, name):
    nw = len(arrs)

    def body(*refs):
        x_refs, out_refs, send_sems, recv_sems = refs[:nw], refs[nw:2 * nw], refs[2 * nw], refs[2 * nw + 1]
        x, y, c = _me()
        cps = []
        for w in range(nw):
            half = arrs[w].shape[1] // 2
            cp = pltpu.make_async_remote_copy(
                src_ref=x_refs[w].at[:, pl.ds((1 - c) * half, half), :], dst_ref=out_refs[w],
                send_sem=send_sems.at[w], recv_sem=recv_sems.at[w], device_id=(x, y, 1 - c), device_id_type=MESH)
            cp.start()
            cps.append(cp)
        for cp in cps:
            cp.wait()

    return pl.pallas_call(
        body, name=name,
        out_shape=[jax.ShapeDtypeStruct((a.shape[0], a.shape[1] // 2, a.shape[2]), a.dtype) for a in arrs],
        in_specs=[_ANY] * nw, out_specs=[_ANY] * nw,
        scratch_shapes=[pltpu.SemaphoreType.DMA((nw,)), pltpu.SemaphoreType.DMA((nw,))],
        compiler_params=_params(),
    )(*arrs)


def _sibling_swap(arrs, *, name):
    nw = len(arrs)

    def body(*refs):
        x_refs, out_refs, send_sems, recv_sems = refs[:nw], refs[nw:2 * nw], refs[2 * nw], refs[2 * nw + 1]
        x, y, c = _me()
        cps = []
        for w in range(nw):
            cp = pltpu.make_async_remote_copy(
                src_ref=x_refs[w], dst_ref=out_refs[w], send_sem=send_sems.at[w], recv_sem=recv_sems.at[w],
                device_id=(x, y, 1 - c), device_id_type=MESH)
            cp.start()
            cps.append(cp)
        for cp in cps:
            cp.wait()

    return pl.pallas_call(
        body, name=name, out_shape=[jax.ShapeDtypeStruct(a.shape, a.dtype) for a in arrs],
        in_specs=[_ANY] * nw, out_specs=[_ANY] * nw,
        scratch_shapes=[pltpu.SemaphoreType.DMA((nw,)), pltpu.SemaphoreType.DMA((nw,))],
        compiler_params=_params(),
    )(*arrs)


def _adamw_halves(w, own, sib, m, v, c_arr, *, name):
    r, cols = w.shape
    h = r // 2
    tm = _tile(h, max(SUBLANES, (192 * 1024) // cols // SUBLANES * SUBLANES), SUBLANES)
    nb = h // tm

    def body(c_ref, w_ref, own_ref, sib_ref, m_ref, v_ref, g_out, d_out, m_out, v_out):
        g = jnp.where(pl.program_id(0) == c_ref[0], own_ref[...], sib_ref[...])
        g_out[...] = g
        d_out[...], m_out[...], v_out[...] = _adamw_math(w_ref[...], g, m_ref[...], v_ref[...])

    full = pl.BlockSpec((tm, cols), lambda hh, i, c_ref: (hh * nb + i, 0))
    half = pl.BlockSpec((tm, cols), lambda hh, i, c_ref: (i, 0))
    return pl.pallas_call(
        body, name=name,
        grid_spec=pltpu.PrefetchScalarGridSpec(num_scalar_prefetch=1, grid=(2, nb),
                                               in_specs=[full, half, half, full, full], out_specs=[full] * 4),
        out_shape=[jax.ShapeDtypeStruct((r, cols), F32)] * 4,
        compiler_params=_params(("parallel", "parallel")),
    )(c_arr, w, own, sib, m, v)


def _chip_exchange(arrs, *, name):
    nw = len(arrs)

    def body(*refs):
        x_refs, out_refs = refs[:nw], refs[nw:2 * nw]
        send_sems, recv_sems, local_sems = refs[2 * nw:]
        x, y, c = _me()
        s_me = 2 * x + y
        chips = [(1 - x, y), (x, 1 - y), (1 - x, 1 - y)]
        started, local = [], []
        for w in range(nw):
            mine = pltpu.make_async_copy(x_refs[w].at[s_me], out_refs[w].at[s_me], local_sems.at[w])
            mine.start()
            local.append(mine)
            for k, (px, py) in enumerate(chips):
                cp = pltpu.make_async_remote_copy(
                    src_ref=x_refs[w].at[2 * px + py], dst_ref=out_refs[w].at[s_me], send_sem=send_sems.at[w, k],
                    recv_sem=recv_sems.at[w, k], device_id=(px, py, c), device_id_type=MESH)
                cp.start()
                started.append(cp)
        for w in range(nw):
            for k, (px, py) in enumerate(chips):
                pltpu.make_async_remote_copy(
                    src_ref=x_refs[w].at[s_me], dst_ref=out_refs[w].at[2 * px + py], send_sem=send_sems.at[w, k],
                    recv_sem=recv_sems.at[w, k], device_id=(px, py, c), device_id_type=MESH).wait_recv()
        for cp in started:
            cp.wait_send()
        for cp in local:
            cp.wait()

    return pl.pallas_call(
        body, name=name, out_shape=[jax.ShapeDtypeStruct(a.shape, a.dtype) for a in arrs],
        in_specs=[_ANY] * nw, out_specs=[_ANY] * nw,
        scratch_shapes=[pltpu.SemaphoreType.DMA((nw, 3)), pltpu.SemaphoreType.DMA((nw, 3)), pltpu.SemaphoreType.DMA((nw,))],
        compiler_params=_params(),
    )(*arrs)


def _half_add(g, recv, c_arr, *, name):
    S, r, w = g.shape
    h = r // 2
    tm = _tile(h, 512, 16)
    nb = h // tm

    def body(c_ref, g_ref, r_ref, o_ref):
        o_ref[...] = (g_ref[...] + r_ref[...]).astype(BF16)

    return pl.pallas_call(
        body, name=name,
        grid_spec=pltpu.PrefetchScalarGridSpec(
            num_scalar_prefetch=1, grid=(S, nb),
            in_specs=[pl.BlockSpec((1, tm, w), lambda s, i, c_ref: (s, c_ref[0] * nb + i, 0)),
                      pl.BlockSpec((1, tm, w), lambda s, i, c_ref: (s, i, 0))],
            out_specs=pl.BlockSpec((1, tm, w), lambda s, i, c_ref: (s, i, 0))),
        out_shape=jax.ShapeDtypeStruct((S, h, w), BF16),
        compiler_params=_params(("parallel", "parallel")),
    )(c_arr, g, recv)


def _layout(sizes, width, part_mult, total_mult):
    offs, rows, r = [], [], 0
    for n in sizes:
        k = -(-n // width)
        offs.append(r)
        rows.append(k)
        r += -(-k // part_mult) * part_mult
    return offs, rows, -(-r // total_mult) * total_mult


def _pack(arrs, width, part_mult, total_mult, dtype, lead=()):
    nl = len(lead)
    sizes = [math.prod(a.shape[nl:]) for a in arrs]
    offs, rows, total = _layout(sizes, width, part_mult, total_mult)
    parts, r = [], 0
    for a, n, o, k in zip(arrs, sizes, offs, rows):
        kp = -(-k // part_mult) * part_mult
        flat = a.reshape(*lead, n).astype(dtype)
        if kp * width > n:
            flat = jnp.pad(flat, [(0, 0)] * nl + [(0, kp * width - n)])
        parts.append(flat.reshape(*lead, kp, width))
        r = o + kp
    if total > r:
        parts.append(jnp.zeros((*lead, total - r, width), dtype))
    return jnp.concatenate(parts, axis=nl)


def _unpack(pool, shapes, width, part_mult, total_mult):
    lead = pool.shape[:-2]
    sizes = [math.prod(s) for s in shapes]
    offs, rows, _ = _layout(sizes, width, part_mult, total_mult)
    out = []
    for s, n, o, k in zip(shapes, sizes, offs, rows):
        flat = lax.slice_in_dim(pool, o, o + k, axis=len(lead)).reshape(*lead, k * width)
        out.append(lax.slice_in_dim(flat, 0, n, axis=len(lead)).reshape(*lead, *s))
    return out


_WEIGHTS = ("c_ctx", "w_ada", "b_ada", "g_pre_mix", "g_post_mix", "g_pre_ffn", "g_post_ffn", "w_in", "b_merge",
            "dn_conv", "dn_a_log", "dn_dt_bias", "dn_onorm", "lru_conv", "lru_conv_b", "lru_w_rg", "lru_b_rg",
            "lru_w_ig", "lru_b_ig", "lru_lambda", "w_branch_dn", "w_branch_lru", "w_out", "w_up", "ffn_dw",
            "ffn_dw_b", "w_down")
_BIG = {"w_ada": True, "w_in": True, "w_branch_dn": False, "w_branch_lru": False, "w_out": False, "w_up": True,
        "w_down": False}
_SMALL_SHARDED = ("dn_conv", "lru_conv", "lru_b_rg", "lru_b_ig", "lru_lambda", "ffn_dw")
_NCHIP = 4
_FLAT_PART = 8
_FLAT_TOTAL = 256


def _to_chip_shards(g, by_cols):
    if by_cols:
        return g.reshape(g.shape[0], _NCHIP, g.shape[1] // _NCHIP).transpose(1, 0, 2)
    return g.reshape(_NCHIP, g.shape[0] // _NCHIP, g.shape[1])


def _from_chip_shards(s, by_cols):
    if by_cols:
        return s.transpose(1, 0, 2).reshape(s.shape[1], _NCHIP * s.shape[2])
    return s.reshape(_NCHIP * s.shape[1], s.shape[2])


def _dsilu(x):
    s = _sigmoid(x)
    return s * (1.0 + x * (1.0 - s))


def kernel(x, c, ctx, c_ctx, w_ada, b_ada, g_pre_mix, g_post_mix, g_pre_ffn, g_post_ffn, w_in, b_merge, dn_conv, dn_a_log, dn_dt_bias, dn_onorm, lru_conv, lru_conv_b, lru_w_rg, lru_b_rg, lru_w_ig, lru_b_ig, lru_lambda, w_branch_dn, w_branch_lru, w_out, w_up, ffn_dw, ffn_dw_b, w_down, loss_target, m_c_ctx, m_w_ada, m_b_ada, m_g_pre_mix, m_g_post_mix, m_g_pre_ffn, m_g_post_ffn, m_w_in, m_b_merge, m_dn_conv, m_dn_a_log, m_dn_dt_bias, m_dn_onorm, m_lru_conv, m_lru_conv_b, m_lru_w_rg, m_lru_b_rg, m_lru_w_ig, m_lru_b_ig, m_lru_lambda, m_w_branch_dn, m_w_branch_lru, m_w_out, m_w_up, m_ffn_dw, m_ffn_dw_b, m_w_down, v_c_ctx, v_w_ada, v_b_ada, v_g_pre_mix, v_g_post_mix, v_g_pre_ffn, v_g_post_ffn, v_w_in, v_b_merge, v_dn_conv, v_dn_a_log, v_dn_dt_bias, v_dn_onorm, v_lru_conv, v_lru_conv_b, v_lru_w_rg, v_lru_b_rg, v_lru_w_ig, v_lru_b_ig, v_lru_lambda, v_w_branch_dn, v_w_branch_lru, v_w_out, v_w_up, v_ffn_dw, v_ffn_dw_b, v_w_down):
    W = dict(zip(_WEIGHTS, (c_ctx, w_ada, b_ada, g_pre_mix, g_post_mix, g_pre_ffn, g_post_ffn, w_in, b_merge, dn_conv,
                            dn_a_log, dn_dt_bias, dn_onorm, lru_conv, lru_conv_b, lru_w_rg, lru_b_rg, lru_w_ig, lru_b_ig,
                            lru_lambda, w_branch_dn, w_branch_lru, w_out, w_up, ffn_dw, ffn_dw_b, w_down)))
    Mo = dict(zip(_WEIGHTS, (m_c_ctx, m_w_ada, m_b_ada, m_g_pre_mix, m_g_post_mix, m_g_pre_ffn, m_g_post_ffn, m_w_in,
                             m_b_merge, m_dn_conv, m_dn_a_log, m_dn_dt_bias, m_dn_onorm, m_lru_conv, m_lru_conv_b,
                             m_lru_w_rg, m_lru_b_rg, m_lru_w_ig, m_lru_b_ig, m_lru_lambda, m_w_branch_dn,
                             m_w_branch_lru, m_w_out, m_w_up, m_ffn_dw, m_ffn_dw_b, m_w_down)))
    Vo = dict(zip(_WEIGHTS, (v_c_ctx, v_w_ada, v_b_ada, v_g_pre_mix, v_g_post_mix, v_g_pre_ffn, v_g_post_ffn, v_w_in,
                             v_b_merge, v_dn_conv, v_dn_a_log, v_dn_dt_bias, v_dn_onorm, v_lru_conv, v_lru_conv_b,
                             v_lru_w_rg, v_lru_b_rg, v_lru_w_ig, v_lru_b_ig, v_lru_lambda, v_w_branch_dn,
                             v_w_branch_lru, v_w_out, v_w_up, v_ffn_dw, v_ffn_dw_b, v_w_down)))
    B, N, D = x.shape
    NC = ctx.shape[1]
    T = NC + N
    H, HD = dn_a_log.shape[-1], dn_onorm.shape[-1]
    DNW = H * HD
    LW, LBD = lru_conv_b.shape[-1], lru_w_rg.shape[-1]
    DFF = ffn_dw_b.shape[-1]
    LC = LANES
    x_i, y_i, c_i = _me()
    s_me = 2 * x_i + y_i
    tm = _tile(math.gcd(NC, N), 256, 16)

    gathered = _allgather_halves([W[n][0].astype(BF16) for n in _BIG], name="allgather_big")
    full = {}
    for n, g in zip(_BIG, gathered):
        r, w_ = W[n].shape[1:]
        full[n] = g.reshape(_NCHIP, r, w_) if _BIG[n] else g.reshape(_NCHIP * r, w_)

    small_local = [W[n][0].reshape(-1, W[n].shape[-1]) for n in _SMALL_SHARDED]
    small_shapes = [a.shape for a in small_local]
    spack = _pack(small_local, LANES, _FLAT_PART, _FLAT_PART, F32)
    sgath = _allgather_small(spack)[0::2]
    sfull = {n: _from_chip_shards(s, True)
             for n, s in zip(_SMALL_SHARDED, _unpack(sgath, small_shapes, LANES, _FLAT_PART, _FLAT_PART))}

    o_a = 4 * DNW
    o_xl = o_a + 4 * H
    o_mg = o_xl + 2 * LW
    wi_ = _from_chip_shards(full["w_in"], True)
    nj = LW // LC
    lru_cols = jnp.stack([wi_[:, o_xl:o_xl + LW].reshape(D, nj, LC), wi_[:, o_xl + LW:o_mg].reshape(D, nj, LC)],
                         axis=2).reshape(D, 2 * LW)
    wp = jnp.concatenate([wi_[:, :o_a], lru_cols, wi_[:, o_mg:], wi_[:, o_a:o_xl],
                          jnp.zeros((D, LANES - 4 * H), BF16)], axis=1)
    p_lru, p_mg, p_ab = 4 * DNW, 4 * DNW + 2 * LW, 4 * DNW + 2 * LW + 2 * D
    PW = p_ab + LANES

    MR = LANES
    cond = jnp.concatenate([c, c_ctx[None], jnp.zeros((MR - B - 1, D), F32)], axis=0)
    silu_rows = _rowwise(lambda a: (_silu(a),), [cond], [F32], name="cond_silu")[0]
    mod = _matmul(silu_rows, full["w_ada"], b_shards=(0, _NCHIP), name="ada_fwd") + b_ada
    mx = mod[:B].reshape(B, 6, D)
    mc = mod[B].reshape(6, D)
    zero = jnp.zeros((B, D), F32)
    tab = jnp.stack([jnp.stack([jnp.broadcast_to(mc[0], (B, D)), jnp.broadcast_to(mc[1], (B, D))] + [zero] * 6, axis=1),
                     jnp.stack([mx[:, 0], mx[:, 1]] + [zero] * 6, axis=1)], axis=1)
    vecs = jnp.stack([mx[:, 2], mx[:, 3], mx[:, 4], mx[:, 5]] + [zero] * 4, axis=1)
    gains = jnp.concatenate([g_post_mix, g_pre_ffn, g_post_ffn, jnp.zeros((5, D), F32)], axis=0)

    h = jnp.concatenate([ctx, x], axis=1)
    u = _premix_fwd(h, g_pre_mix, tab, nc=NC, tm=tm)
    p = _matmul(u, wp, name="in_fwd")
    dkw = dict(B=B, T=T, nc=NC, H=H, HD=HD)
    qkv = _dnprep_fwd(p, sfull["dn_conv"], **dkw)
    prm = jnp.concatenate([
        jnp.concatenate([dn_a_log.reshape(1, 2 * H), jnp.zeros((1, LANES - 2 * H), F32)], axis=1),
        jnp.concatenate([dn_dt_bias.reshape(1, 2 * H), jnp.zeros((1, LANES - 2 * H), F32)], axis=1),
        jnp.zeros((6, LANES), F32)], axis=0)
    gtm = _tile(B * T, 512, 16)
    gb = _gb_fwd(p, prm, rows=B * T, col0=p_ab, H=H, tm=gtm)
    y_dn, o_dn = _delta_fwd(qkv, gb, p, dn_onorm, **dkw)
    lv = jnp.concatenate([lru_conv_b, sfull["lru_b_rg"], sfull["lru_b_ig"], sfull["lru_lambda"], jnp.zeros((1, LW), F32)], axis=0)
    wr = _blockdiag(lru_w_rg[0], LC).astype(BF16)
    wi = _blockdiag(lru_w_ig[0], LC).astype(BF16)
    lkw = dict(B=B, T=T, nc=NC, LW=LW, col0=p_lru, C=LC)
    y_lru = _lru_fwd(p, sfull["lru_conv"], lv, wr, wi, **lkw)
    Ydn = _matmul(y_dn, full["w_branch_dn"], name="bdn_fwd")
    Ylru = _matmul(y_lru, full["w_branch_lru"], name="blru_fwd")
    mkw = dict(B=B, T=T, nc=NC, D=D, col0=p_mg, tm=tm)
    mixin = _merge_fwd(p, Ydn, Ylru, b_merge, **mkw)
    mix = _matmul(mixin, full["w_out"], name="out_fwd")
    h1, u2 = _post_fwd(x, mix, gains, vecs, tm=tm)
    F = _matmul(u2, full["w_up"], b_shards=(0, _NCHIP), name="up_fwd")
    w9 = sfull["ffn_dw"]
    ftc = _tile(DFF, 256)
    f = _ffn_act_fwd(F, w9, ffn_dw_b, B=B, N=N, DFF=DFF, tc=ftc)
    dn = _matmul(f, full["w_down"], name="down_fwd")
    ddn, dout, sums_f = _final(h1, dn, loss_target, gains, vecs, tm=tm)

    G = {}
    df = _matmul(ddn, full["w_down"], tb=True, name="down_bwd_x")
    G["w_down"] = _matmul(f, ddn, ta=True, name="down_bwd_w")
    dFg, dFv, dwb = _ffn_act_bwd(F, w9, ffn_dw_b, df, B=B, N=N, DFF=DFF, tc=ftc)
    hs = _NCHIP // 2
    du2 = _matmul(dFg, full["w_up"], tb=True, b_shards=(0, hs), name="up_bwd_xg")
    du2 = _matmul(dFv, full["w_up"], tb=True, b_shards=(hs, hs), add=du2, name="up_bwd_xv")
    G["w_up"] = jnp.concatenate([_matmul(u2, dFg, ta=True, out_shards=hs, name="up_bwd_wg"),
                                 _matmul(u2, dFv, ta=True, out_shards=hs, name="up_bwd_wv")], axis=0)
    dx1, dmix, sums_p = _post_bwd(x, mix, gains, vecs, dout, du2, tm=tm)
    dmixin = _matmul(dmix, full["w_out"], tb=True, name="out_bwd_x")
    G["w_out"] = _matmul(mixin, dmix, ta=True, name="out_bwd_w")
    dp = jnp.zeros((B * T, PW), BF16)
    dYdn, dYlru, dp, sums_m = _merge_bwd(p, Ydn, Ylru, b_merge, dmixin, dp, **mkw)
    dy_dn = _matmul(dYdn, full["w_branch_dn"], tb=True, name="bdn_bwd_x")
    G["w_branch_dn"] = _matmul(y_dn, dYdn, ta=True, name="bdn_bwd_w")
    dy_lru = _matmul(dYlru, full["w_branch_lru"], tb=True, name="blru_bwd_x")
    G["w_branch_lru"] = _matmul(y_lru, dYlru, ta=True, name="blru_bwd_w")
    dp, dcw_l, dlv, dwr, dwi = _lru_bwd(p, sfull["lru_conv"], lv, wr, wi, dy_lru, dp, **lkw)
    dqkv, dgb, dp, don = _delta_bwd(qkv, gb, p, dn_onorm, o_dn, dy_dn, dp, **dkw)
    dp, dprm = _gb_bwd(p, prm, dgb, dp, rows=B * T, col0=p_ab, H=H, tm=gtm)
    dp, dcw_d = _dnprep_bwd(p, sfull["dn_conv"], dqkv, dp, **dkw)
    dU = _matmul(dp, wp, tb=True, name="in_bwd_x")
    dwp = _matmul(u, dp, ta=True, name="in_bwd_w")
    grad_x, sums_pm = _premix_bwd(h, g_pre_mix, tab, dU, dx1, nc=NC, tm=tm)
    dlru = dwp[:, p_lru:p_mg].reshape(D, nj, 2, LC)
    G["w_in"] = _to_chip_shards(jnp.concatenate([dwp[:, :o_a], dwp[:, p_ab:p_ab + 4 * H], dlru[:, :, 0].reshape(D, LW),
                                                 dlru[:, :, 1].reshape(D, LW), dwp[:, p_mg:p_ab]], axis=1), True)

    dmod_x = jnp.stack([sums_pm[:, 1, 0], sums_pm[:, 1, 1], sums_p[:, 0], sums_p[:, 1], sums_p[:, 2], sums_f[:, 0]],
                       axis=1).reshape(B, 6 * D)
    dmod_c = jnp.concatenate([sums_pm[:, 0, 0].sum(0), sums_pm[:, 0, 1].sum(0), jnp.zeros((4 * D,), F32)])[None]
    dmod = jnp.concatenate([dmod_x, dmod_c, jnp.zeros((MR - B - 1, 6 * D), F32)], axis=0)
    G["w_ada"] = _matmul(silu_rows, dmod, ta=True, out_shards=_NCHIP, name="ada_bwd_w")
    dsilu = _matmul(dmod, full["w_ada"], tb=True, b_shards=(0, _NCHIP), name="ada_bwd_x")

    g_small = {
        "c_ctx": dsilu[B] * _dsilu(c_ctx),
        "b_ada": dmod[:B + 1].sum(0)[None],
        "g_pre_mix": sums_pm[:, :, 2].sum((0, 1))[None],
        "g_post_mix": sums_p[:, 3].sum(0)[None],
        "g_pre_ffn": sums_p[:, 4].sum(0)[None],
        "g_post_ffn": sums_f[:, 1].sum(0)[None],
        "b_merge": sums_m[0:1],
        "dn_conv": dcw_d[0:4][None],
        "dn_a_log": dprm[0, :2 * H].reshape(1, 2, H),
        "dn_dt_bias": dprm[1, :2 * H].reshape(1, 2, H),
        "dn_onorm": don[:, 0].sum(0)[None],
        "lru_conv": dcw_l[0:4][None],
        "lru_conv_b": dlv[0:1],
        "lru_w_rg": _blockdiag_extract(dwr, LBD)[None],
        "lru_b_rg": dlv[1:3][None],
        "lru_w_ig": _blockdiag_extract(dwi, LBD)[None],
        "lru_b_ig": dlv[3:5][None],
        "lru_lambda": dlv[5:7][None],
        "ffn_dw": dwb[0:9].reshape(1, 3, 3, DFF),
        "ffn_dw_b": dwb[9:10],
    }
    small_names = tuple(n for n in _WEIGHTS if n not in _BIG)
    loss_part = sums_f[:, 2].sum().reshape(1)
    gs_list = [g_small[n] for n in small_names] + [loss_part]
    gs_shapes = [a.shape for a in gs_list]
    gpack = _pack(gs_list, LANES, _FLAT_PART, _FLAT_TOTAL, F32)
    gsum = _sum_lead(_allgather_small(gpack), name="small_sum", tm=512, mult=SUBLANES)
    gs_red = dict(zip(small_names + ("loss",), _unpack(gsum, gs_shapes, LANES, _FLAT_PART, _FLAT_TOTAL)))
    loss = gs_red["loss"][0]

    slabs = [G[n] if _BIG[n] else G[n].reshape(_NCHIP, G[n].shape[0] // _NCHIP, G[n].shape[1]) for n in _BIG]
    c_arr = c_i.astype(jnp.int32).reshape(1)
    from_sibling = _sibling_send_halves(slabs, name="rs_sibling")
    chip_sums = [_half_add(g, r, c_arr, name="rs_add_" + n) for n, g, r in zip(_BIG, slabs, from_sibling)]
    landed = _chip_exchange(chip_sums, name="chip_exchange")
    halves = [_sum_lead(e, name="rs_sum_" + n, tm=512) for n, e in zip(_BIG, landed)]
    sib_halves = _sibling_swap(halves, name="rs_gather")

    grads, deltas, new_m, new_v = {}, {}, {}, {}
    for n, own, sib in zip(_BIG, halves, sib_halves):
        shp = W[n].shape
        outs = _adamw_halves(W[n][0], own, sib, Mo[n][0], Vo[n][0], c_arr, name="adamw_" + n)
        grads[n], deltas[n], new_m[n], new_v[n] = (o.reshape(shp) for o in outs)
    for n in small_names:
        g = gs_red[n]
        if n in _SMALL_SHARDED:
            k = W[n].shape[-1]
            g = lax.dynamic_slice_in_dim(g, s_me * k, k, axis=g.ndim - 1)
        grads[n] = g.reshape(W[n].shape)
    sm_shapes = [W[n].shape for n in small_names]
    pk = lambda d: _pack([d[n] for n in small_names], LANES, _FLAT_PART, _FLAT_TOTAL, F32)
    d_, m_, v_ = _adamw(pk(W), pk(grads), pk(Mo), pk(Vo), name="adamw_small")
    for dst, pool_ in ((deltas, d_), (new_m, m_), (new_v, v_)):
        dst.update(zip(small_names, _unpack(pool_, sm_shapes, LANES, _FLAT_PART, _FLAT_TOTAL)))
    return (loss, grad_x, *[grads[n] for n in _WEIGHTS], *[deltas[n] for n in _WEIGHTS],
            *[new_m[n] for n in _WEIGHTS], *[new_v[n] for n in _WEIGHTS])
```

```python
import functools
import math

import jax
import jax.numpy as jnp
from jax import lax
from jax.experimental import pallas as pl
from jax.experimental.pallas import tpu as pltpu

F32 = jnp.float32
BF16 = jnp.bfloat16
EPS = 1e-6
GRID_W = 64
CHUNK = 64
LRU_C = 8.0
LANES = 128
SUBLANES = 8
VMEM_LIMIT = 56 * 1024 * 1024
ADAM_LR, ADAM_B1, ADAM_B2, ADAM_EPS, ADAM_WD, ADAM_STEP = 0.001, 0.9, 0.999, 1e-08, 0.01, 10
MESH = pl.DeviceIdType.MESH


def _tile(n, target, mult=LANES):
    best = None
    for t in range(mult, min(n, target) + 1, mult):
        if n % t == 0:
            best = t
    return best if best is not None else n


def _params(sem=None, **kw):
    return pltpu.CompilerParams(dimension_semantics=sem, vmem_limit_bytes=VMEM_LIMIT, **kw)


def _sigmoid(x):
    return 1.0 / (1.0 + jnp.exp(-x))


def _silu(x):
    return x * _sigmoid(x)


def _softplus(x):
    return jnp.maximum(x, 0.0) + jnp.log(1.0 + jnp.exp(-jnp.abs(x)))


def _gelu(x):
    return 0.5 * x * (1.0 + jnp.tanh(math.sqrt(2.0 / math.pi) * (x + 0.044715 * x * x * x)))


def _rmsn(u, gain):
    return u * lax.rsqrt(jnp.mean(u * u, axis=-1, keepdims=True) + EPS) * gain


_MM_VMEM = 40 * 1024 * 1024


def _matmul(a, b, *, ta=False, tb=False, add=None, b_shards=None, out_shards=None, out_dtype=F32, name,
            tm=1024, tn=2048, tk=1024):
    (K, M) = a.shape if ta else a.shape[::-1]
    if b_shards is not None:
        s0, ns = b_shards
        bsh = (b.shape[1], ns * b.shape[2])
        nsh = b.shape[2]
    else:
        bsh = b.shape
    N = bsh[0] if tb else bsh[1]
    assert (bsh[1] if tb else bsh[0]) == K, (a.shape, b.shape, ta, tb)
    tm = _tile(M, tm)
    tk = _tile(nsh if (b_shards is not None and tb) else K, tk)
    nlim = nsh if (b_shards is not None and not tb) else (N // out_shards if out_shards else N)
    osz = jnp.dtype(out_dtype).itemsize + (4 if add is not None else 0)
    while True:
        tn_ = _tile(nlim, tn)
        need = 2 * (tm * tk * a.dtype.itemsize + tk * tn_ * b.dtype.itemsize + tm * tn_ * osz) + 4 * tm * tn_
        if need <= _MM_VMEM or tn <= LANES:
            break
        tn //= 2
    tn = tn_
    nk = K // tk
    dims = (((0 if ta else 1,), (1 if tb else 0,)), ((), ()))

    def body(a_ref, b_ref, *rest):
        (c_ref, o_ref, acc_ref) = rest if add is not None else (None, *rest)
        k = pl.program_id(2)

        @pl.when(k == 0)
        def _():
            acc_ref[...] = jnp.zeros_like(acc_ref) if c_ref is None else c_ref[...]

        bv = b_ref[0] if b_shards is not None else b_ref[...]
        acc_ref[...] += lax.dot_general(a_ref[...].astype(BF16), bv.astype(BF16), dims, preferred_element_type=F32)

        @pl.when(k == nk - 1)
        def _():
            if out_shards:
                o_ref[0] = acc_ref[...].astype(out_dtype)
            else:
                o_ref[...] = acc_ref[...].astype(out_dtype)

    a_spec = pl.BlockSpec((tk, tm), lambda i, j, k: (k, i)) if ta else pl.BlockSpec((tm, tk), lambda i, j, k: (i, k))
    if b_shards is None:
        b_spec = pl.BlockSpec((tn, tk), lambda i, j, k: (j, k)) if tb else pl.BlockSpec((tk, tn), lambda i, j, k: (k, j))
    elif tb:
        per = nsh // tk
        b_spec = pl.BlockSpec((1, tn, tk), lambda i, j, k: (s0 + k // per, j, k % per))
    else:
        per = nsh // tn
        b_spec = pl.BlockSpec((1, tk, tn), lambda i, j, k: (s0 + j // per, k, j % per))
    o_spec = pl.BlockSpec((tm, tn), lambda i, j, k: (i, j))
    if out_shards:
        oper = N // out_shards // tn
        out_spec = pl.BlockSpec((1, tm, tn), lambda i, j, k: (j // oper, i, j % oper))
        out_shape = jax.ShapeDtypeStruct((out_shards, M, N // out_shards), out_dtype)
    else:
        out_spec, out_shape = o_spec, jax.ShapeDtypeStruct((M, N), out_dtype)
    return pl.pallas_call(
        body, name=name, grid=(M // tm, N // tn, nk),
        in_specs=[a_spec, b_spec] + ([o_spec] if add is not None else []),
        out_specs=out_spec, out_shape=out_shape,
        scratch_shapes=[pltpu.VMEM((tm, tn), F32)],
        compiler_params=_params(("parallel", "parallel", "arbitrary")),
    )(*((a, b) + ((add,) if add is not None else ())))


def _premix_math(h, gain, shift, scale):
    return _rmsn(h, gain) * (1.0 + scale) + shift


def _premix_fwd(h, gain, tab, *, nc, tm):
    B, T, D = h.shape
    nt, nct = T // tm, nc // tm

    def body(h_ref, g_ref, tab_ref, u_ref):
        tabv = tab_ref[0, 0]
        u_ref[...] = _premix_math(h_ref[0], g_ref[...], tabv[0:1], tabv[1:2]).astype(BF16)

    return pl.pallas_call(
        body, name="premix_fwd", grid=(B, nt),
        in_specs=[pl.BlockSpec((1, tm, D), lambda b, t: (b, t, 0)),
                  pl.BlockSpec((1, D), lambda b, t: (0, 0)),
                  pl.BlockSpec((1, 1, 8, D), lambda b, t: (b, jnp.where(t < nct, 0, 1), 0, 0))],
        out_specs=pl.BlockSpec((tm, D), lambda b, t: (b * nt + t, 0)),
        out_shape=jax.ShapeDtypeStruct((B * T, D), BF16),
        compiler_params=_params(("parallel", "parallel")),
    )(h, gain, tab)


def _premix_bwd(h, gain, tab, du, dres, *, nc, tm):
    B, T, D = h.shape
    nt, nct = T // tm, nc // tm
    N = T - nc

    def body(h_ref, g_ref, tab_ref, du_ref, dres_ref, dx_ref, sums_ref):
        t = pl.program_id(1)
        tabv = tab_ref[0, 0]
        _, vjp = jax.vjp(_premix_math, h_ref[0], g_ref[...], tabv[0:1], tabv[1:2])
        dh, dgain, dshift, dscale = vjp(du_ref[...].astype(F32))

        @pl.when((t == 0) | (t == nct))
        def _():
            sums_ref[...] = jnp.zeros_like(sums_ref)

        sums_ref[0, 0, 0:1, :] += dshift
        sums_ref[0, 0, 1:2, :] += dscale
        sums_ref[0, 0, 2:3, :] += dgain

        @pl.when(t >= nct)
        def _():
            dx_ref[0] = dres_ref[...] + dh

    lat = lambda b, t: jnp.maximum(t - nct, 0)
    return pl.pallas_call(
        body, name="premix_bwd", grid=(B, nt),
        in_specs=[pl.BlockSpec((1, tm, D), lambda b, t: (b, t, 0)),
                  pl.BlockSpec((1, D), lambda b, t: (0, 0)),
                  pl.BlockSpec((1, 1, 8, D), lambda b, t: (b, jnp.where(t < nct, 0, 1), 0, 0)),
                  pl.BlockSpec((tm, D), lambda b, t: (b * nt + t, 0)),
                  pl.BlockSpec((tm, D), lambda b, t: (b * (nt - nct) + lat(b, t), 0))],
        out_specs=[pl.BlockSpec((1, tm, D), lambda b, t: (b, lat(b, t), 0)),
                   pl.BlockSpec((1, 1, 8, D), lambda b, t: (b, jnp.where(t < nct, 0, 1), 0, 0))],
        out_shape=[jax.ShapeDtypeStruct((B, N, D), F32), jax.ShapeDtypeStruct((B, 2, 8, D), F32)],
        compiler_params=_params(("parallel", "arbitrary")),
    )(h, gain, tab, du, dres)


def _merge_math(mgd, mgl, yd, yl, bd, bl):
    return _sigmoid(mgd + bd) * yd + _sigmoid(mgl + bl) * yl


def _merge_fwd(p, ydn, ylru, b_merge, *, B, T, nc, D, col0, tm):
    N = T - nc
    ntl, nt, nct, cb = N // tm, T // tm, nc // tm, col0 // D

    def body(mgd_ref, mgl_ref, yd_ref, yl_ref, bm_ref, o_ref):
        o_ref[...] = _merge_math(mgd_ref[...], mgl_ref[...], yd_ref[...], yl_ref[...],
                                 bm_ref[:, 0:D], bm_ref[:, D:2 * D]).astype(BF16)

    prow = lambda b, t: b * nt + nct + t
    return pl.pallas_call(
        body, name="merge_fwd", grid=(B, ntl),
        in_specs=[pl.BlockSpec((tm, D), lambda b, t: (prow(b, t), cb)),
                  pl.BlockSpec((tm, D), lambda b, t: (prow(b, t), cb + 1)),
                  pl.BlockSpec((tm, D), lambda b, t: (b * ntl + t, 0)),
                  pl.BlockSpec((tm, D), lambda b, t: (b * ntl + t, 0)),
                  pl.BlockSpec((1, 2 * D), lambda b, t: (0, 0))],
        out_specs=pl.BlockSpec((tm, D), lambda b, t: (b * ntl + t, 0)),
        out_shape=jax.ShapeDtypeStruct((B * N, D), BF16),
        compiler_params=_params(("parallel", "parallel")),
    )(p, p, ydn, ylru, b_merge)


def _merge_bwd(p, ydn, ylru, b_merge, dmix, dp, *, B, T, nc, D, col0, tm):
    N = T - nc
    ntl, nt, nct, cb = N // tm, T // tm, nc // tm, col0 // D
    assert col0 % (2 * D) == 0

    def body(mgd_ref, mgl_ref, yd_ref, yl_ref, bm_ref, dm_ref, dp_any, dyd_ref, dyl_ref, dp_ref, sums_ref):
        _, vjp = jax.vjp(_merge_math, mgd_ref[...], mgl_ref[...], yd_ref[...], yl_ref[...],
                         bm_ref[:, 0:D], bm_ref[:, D:2 * D])
        dmgd, dmgl, dyd, dyl, dbd, dbl = vjp(dm_ref[...])
        dyd_ref[...] = dyd.astype(BF16)
        dyl_ref[...] = dyl.astype(BF16)
        dp_ref[:, 0:D] = dmgd.astype(BF16)
        dp_ref[:, D:2 * D] = dmgl.astype(BF16)

        @pl.when((pl.program_id(0) == 0) & (pl.program_id(1) == 0))
        def _():
            sums_ref[...] = jnp.zeros_like(sums_ref)

        sums_ref[0:1, 0:D] += dbd
        sums_ref[0:1, D:2 * D] += dbl

    prow = lambda b, t: b * nt + nct + t
    row = pl.BlockSpec((tm, D), lambda b, t: (b * ntl + t, 0))
    return pl.pallas_call(
        body, name="merge_bwd", grid=(B, ntl),
        in_specs=[pl.BlockSpec((tm, D), lambda b, t: (prow(b, t), cb)),
                  pl.BlockSpec((tm, D), lambda b, t: (prow(b, t), cb + 1)),
                  row, row, pl.BlockSpec((1, 2 * D), lambda b, t: (0, 0)), row,
                  pl.BlockSpec(memory_space=pl.ANY)],
        out_specs=[row, row,
                   pl.BlockSpec((tm, 2 * D), lambda b, t: (prow(b, t), cb // 2)),
                   pl.BlockSpec((8, 2 * D), lambda b, t: (0, 0))],
        out_shape=[jax.ShapeDtypeStruct((B * N, D), BF16), jax.ShapeDtypeStruct((B * N, D), BF16),
                   jax.ShapeDtypeStruct(dp.shape, dp.dtype), jax.ShapeDtypeStruct((8, 2 * D), F32)],
        input_output_aliases={6: 2},
        compiler_params=_params(("arbitrary", "arbitrary")),
    )(p, p, ydn, ylru, b_merge, dmix, dp)


def _post_math(x, mix, g1, gate, g2, sh, sc):
    h1 = x + _rmsn(mix, g1) * gate
    return h1, _rmsn(h1, g2) * (1.0 + sc) + sh


def _post_fwd(x, mix, gains, vecs, *, tm):
    B, N, D = x.shape
    ntl = N // tm

    def body(x_ref, mix_ref, g_ref, v_ref, h1_ref, u2_ref):
        v = v_ref[0]
        h1, u2 = _post_math(x_ref[0], mix_ref[...], g_ref[0:1], v[0:1], g_ref[1:2], v[1:2], v[2:3])
        h1_ref[...] = h1
        u2_ref[...] = u2.astype(BF16)

    row = pl.BlockSpec((tm, D), lambda b, t: (b * ntl + t, 0))
    return pl.pallas_call(
        body, name="post_fwd", grid=(B, ntl),
        in_specs=[pl.BlockSpec((1, tm, D), lambda b, t: (b, t, 0)), row,
                  pl.BlockSpec((8, D), lambda b, t: (0, 0)), pl.BlockSpec((1, 8, D), lambda b, t: (b, 0, 0))],
        out_specs=[row, row],
        out_shape=[jax.ShapeDtypeStruct((B * N, D), F32), jax.ShapeDtypeStruct((B * N, D), BF16)],
        compiler_params=_params(("parallel", "parallel")),
    )(x, mix, gains, vecs)


def _post_bwd(x, mix, gains, vecs, dh1, du2, *, tm):
    B, N, D = x.shape
    ntl = N // tm

    def body(x_ref, mix_ref, g_ref, v_ref, dh1_ref, du2_ref, dx_ref, dmix_ref, sums_ref):
        v = v_ref[0]
        _, vjp = jax.vjp(_post_math, x_ref[0], mix_ref[...], g_ref[0:1], v[0:1], g_ref[1:2], v[1:2], v[2:3])
        dx, dmix, dg1, dgate, dg2, dsh, dsc = vjp((dh1_ref[...], du2_ref[...]))
        dx_ref[...] = dx
        dmix_ref[...] = dmix.astype(BF16)

        @pl.when(pl.program_id(1) == 0)
        def _():
            sums_ref[...] = jnp.zeros_like(sums_ref)

        sums_ref[0, 0:1, :] += dgate
        sums_ref[0, 1:2, :] += dsh
        sums_ref[0, 2:3, :] += dsc
        sums_ref[0, 3:4, :] += dg1
        sums_ref[0, 4:5, :] += dg2

    row = pl.BlockSpec((tm, D), lambda b, t: (b * ntl + t, 0))
    return pl.pallas_call(
        body, name="post_bwd", grid=(B, ntl),
        in_specs=[pl.BlockSpec((1, tm, D), lambda b, t: (b, t, 0)), row,
                  pl.BlockSpec((8, D), lambda b, t: (0, 0)), pl.BlockSpec((1, 8, D), lambda b, t: (b, 0, 0)), row, row],
        out_specs=[row, row, pl.BlockSpec((1, 8, D), lambda b, t: (b, 0, 0))],
        out_shape=[jax.ShapeDtypeStruct((B * N, D), F32), jax.ShapeDtypeStruct((B * N, D), BF16),
                   jax.ShapeDtypeStruct((B, 8, D), F32)],
        compiler_params=_params(("parallel", "arbitrary")),
    )(x, mix, gains, vecs, dh1, du2)


def _final_math(dn, g4, gate5):
    return _rmsn(dn, g4) * gate5


def _final(h1, dn, target, gains, vecs, *, tm):
    B, N, D = target.shape
    ntl = N // tm

    def body(h1_ref, dn_ref, t_ref, g_ref, v_ref, ddn_ref, dout_ref, sums_ref):
        v = v_ref[0]
        y, vjp = jax.vjp(_final_math, dn_ref[...], g_ref[2:3], v[3:4])
        err = h1_ref[...] + y - t_ref[0]
        dout = err * (1.0 / D)
        ddn, dg4, dgate5 = vjp(dout)
        ddn_ref[...] = ddn.astype(BF16)
        dout_ref[...] = dout

        @pl.when(pl.program_id(1) == 0)
        def _():
            sums_ref[...] = jnp.zeros_like(sums_ref)

        sums_ref[0, 0:1, :] += dgate5
        sums_ref[0, 1:2, :] += dg4
        sums_ref[0, 2:3, :] += jnp.sum(err * err, axis=0, keepdims=True) * (0.5 / D)

    row = pl.BlockSpec((tm, D), lambda b, t: (b * ntl + t, 0))
    return pl.pallas_call(
        body, name="final", grid=(B, ntl),
        in_specs=[row, row, pl.BlockSpec((1, tm, D), lambda b, t: (b, t, 0)),
                  pl.BlockSpec((8, D), lambda b, t: (0, 0)), pl.BlockSpec((1, 8, D), lambda b, t: (b, 0, 0))],
        out_specs=[row, row, pl.BlockSpec((1, 8, D), lambda b, t: (b, 0, 0))],
        out_shape=[jax.ShapeDtypeStruct((B * N, D), BF16), jax.ShapeDtypeStruct((B * N, D), F32),
                   jax.ShapeDtypeStruct((B, 8, D), F32)],
        compiler_params=_params(("parallel", "arbitrary")),
    )(h1, dn, target, gains, vecs)


def _shift(x, s):
    s = s % x.shape[0]
    return x if s == 0 else pltpu.roll(x, s, 0)


def _seg_taps(T, nc, width, pad_left):
    t = lax.broadcasted_iota(jnp.int32, (T, 1), 0)
    pos = jnp.where(t < nc, t, t - nc)
    seg = jnp.where(t < nc, nc, T - nc)
    taps = []
    for k in range(width):
        src = pos + (k - pad_left)
        taps.append((pad_left - k, (src >= 0) & (src < seg)))
    return taps


def _grid_taps(N):
    t = lax.broadcasted_iota(jnp.int32, (N, 1), 0)
    wcol = t % GRID_W
    taps = []
    for dr in (-1, 0, 1):
        for dw in (-1, 0, 1):
            off = dr * GRID_W + dw
            ok = (wcol + dw >= 0) & (wcol + dw < GRID_W) & (t + dr * GRID_W >= 0) & (t + dr * GRID_W < N)
            taps.append((-off, ok))
    return taps


def _conv_fwd(x, w, taps):
    y = jnp.zeros_like(x)
    for k, (s, m) in enumerate(taps):
        y = y + w[k:k + 1] * jnp.where(m, _shift(x, s), 0.0)
    return y


def _conv_bwd(x, w, taps, dy):
    dx = jnp.zeros_like(x)
    dws = []
    for k, (s, m) in enumerate(taps):
        dym = jnp.where(m, dy, 0.0)
        dx = dx + w[k:k + 1] * _shift(dym, -s)
        dws.append(jnp.sum(dym * _shift(x, s), axis=0, keepdims=True))
    return dx, jnp.concatenate(dws, axis=0)


def _ffn_act_fwd(F, w9, bias, *, B, N, DFF, tc):
    nj = DFF // tc

    def body(fg_ref, fv_ref, w_ref, b_ref, o_ref):
        fg = _conv_fwd(fg_ref[...], w_ref[...], _grid_taps(N)) + b_ref[...]
        o_ref[...] = (_gelu(fg) * fv_ref[...]).astype(BF16)

    return pl.pallas_call(
        body, name="ffn_act_fwd", grid=(B, nj),
        in_specs=[pl.BlockSpec((N, tc), lambda b, j: (b, j)), pl.BlockSpec((N, tc), lambda b, j: (b, nj + j)),
                  pl.BlockSpec((9, tc), lambda b, j: (0, j)), pl.BlockSpec((1, tc), lambda b, j: (0, j))],
        out_specs=pl.BlockSpec((N, tc), lambda b, j: (b, j)),
        out_shape=jax.ShapeDtypeStruct((B * N, DFF), BF16),
        compiler_params=_params(("parallel", "parallel")),
    )(F, F, w9, bias)


def _ffn_act_bwd(F, w9, bias, df, *, B, N, DFF, tc):
    nj = DFF // tc

    def body(fg_ref, fv_ref, w_ref, b_ref, df_ref, dfg_ref, dfv_ref, dwb_ref):
        taps = _grid_taps(N)
        x = fg_ref[...]
        fg, vjp = jax.vjp(lambda a: _gelu(a), _conv_fwd(x, w_ref[...], taps) + b_ref[...])
        dfl = df_ref[...]
        dfv_ref[...] = (dfl * fg).astype(BF16)
        (dpre,) = vjp(dfl * fv_ref[...])
        dx, dw = _conv_bwd(x, w_ref[...], taps, dpre)
        dfg_ref[...] = dx.astype(BF16)

        @pl.when(pl.program_id(1) == 0)
        def _():
            dwb_ref[...] = jnp.zeros_like(dwb_ref)

        dwb_ref[0:9, :] += dw
        dwb_ref[9:10, :] += jnp.sum(dpre, axis=0, keepdims=True)

    col = pl.BlockSpec((N, tc), lambda j, b: (b, j))
    return pl.pallas_call(
        body, name="ffn_act_bwd", grid=(nj, B),
        in_specs=[col, pl.BlockSpec((N, tc), lambda j, b: (b, nj + j)),
                  pl.BlockSpec((9, tc), lambda j, b: (0, j)), pl.BlockSpec((1, tc), lambda j, b: (0, j)), col],
        out_specs=[col, col, pl.BlockSpec((16, tc), lambda j, b: (0, j))],
        out_shape=[jax.ShapeDtypeStruct((B * N, DFF), BF16), jax.ShapeDtypeStruct((B * N, DFF), BF16),
                   jax.ShapeDtypeStruct((16, DFF), F32)],
        compiler_params=_params(("parallel", "arbitrary")),
    )(F, F, w9, bias, df)


def _dnprep_math(y, is_qk, scale):
    s = _silu(y)
    n = s * lax.rsqrt(jnp.sum(s * s, axis=-1, keepdims=True) + EPS) * scale
    return jnp.where(is_qk, n, s)


def _dnprep_fwd(p, cw, *, B, T, nc, H, HD):
    def body(x_ref, w_ref, o_ref):
        j = pl.program_id(1)
        y = _conv_fwd(x_ref[...], w_ref[...], _seg_taps(T, nc, 4, 2))
        o_ref[...] = _dnprep_math(y, j < 2 * H, jnp.where(j < H, HD ** -0.5, 1.0))

    return pl.pallas_call(
        body, name="dnprep_fwd", grid=(B, 3 * H),
        in_specs=[pl.BlockSpec((T, HD), lambda b, j: (b, j)), pl.BlockSpec((4, HD), lambda b, j: (0, j))],
        out_specs=pl.BlockSpec((T, HD), lambda b, j: (b, j)),
        out_shape=jax.ShapeDtypeStruct((B * T, 3 * H * HD), F32),
        compiler_params=_params(("parallel", "parallel")),
    )(p, cw)


def _dnprep_bwd(p, cw, dqkv, dp, *, B, T, nc, H, HD):
    def body(x_ref, w_ref, d_ref, dp_any, dp_ref, dcw_ref):
        j = pl.program_id(0)
        taps = _seg_taps(T, nc, 4, 2)
        x = x_ref[...]
        y = _conv_fwd(x, w_ref[...], taps)
        is_qk, scale = j < 2 * H, jnp.where(j < H, HD ** -0.5, 1.0)
        _, vjp = jax.vjp(lambda a: _dnprep_math(a, is_qk, scale), y)
        (dy,) = vjp(d_ref[0])
        dx, dw = _conv_bwd(x, w_ref[...], taps, dy)
        dp_ref[...] = dx.astype(BF16)

        @pl.when(pl.program_id(1) == 0)
        def _():
            dcw_ref[...] = jnp.zeros_like(dcw_ref)

        dcw_ref[0:4, :] += dw

    col = pl.BlockSpec((T, HD), lambda j, b: (b, j))
    return pl.pallas_call(
        body, name="dnprep_bwd", grid=(3 * H, B),
        in_specs=[col, pl.BlockSpec((4, HD), lambda j, b: (0, j)),
                  pl.BlockSpec((1, T, HD), lambda j, b: (j // H, b, j % H)), pl.BlockSpec(memory_space=pl.ANY)],
        out_specs=[col, pl.BlockSpec((8, HD), lambda j, b: (0, j))],
        out_shape=[jax.ShapeDtypeStruct(dp.shape, dp.dtype), jax.ShapeDtypeStruct((8, 3 * H * HD), F32)],
        input_output_aliases={3: 0},
        compiler_params=_params(("parallel", "arbitrary")),
    )(p, cw, dqkv, dp)


def _gb_math(ab, alog, dtb, H):
    lane = lax.broadcasted_iota(jnp.int32, ab.shape, 1)
    g = -jnp.exp(alog) * _softplus(ab + dtb)
    return jnp.where(lane < 2 * H, g, jnp.where(lane < 4 * H, _sigmoid(ab), 0.0))


def _gb_fwd(p, prm, *, rows, col0, H, tm):
    def body(x_ref, prm_ref, o_ref):
        o_ref[...] = _gb_math(x_ref[...], prm_ref[0:1], prm_ref[1:2], H)

    return pl.pallas_call(
        body, name="gb_fwd", grid=(rows // tm,),
        in_specs=[pl.BlockSpec((tm, LANES), lambda t: (t, col0 // LANES)), pl.BlockSpec((8, LANES), lambda t: (0, 0))],
        out_specs=pl.BlockSpec((tm, LANES), lambda t: (t, 0)),
        out_shape=jax.ShapeDtypeStruct((rows, LANES), F32),
        compiler_params=_params(("parallel",)),
    )(p, prm)


def _gb_bwd(p, prm, dgb, dp, *, rows, col0, H, tm):
    def body(x_ref, prm_ref, d_ref, dp_any, dp_ref, dprm_ref):
        _, vjp = jax.vjp(lambda a, b, c: _gb_math(a, b, c, H), x_ref[...], prm_ref[0:1], prm_ref[1:2])
        dab, dalog, ddtb = vjp(d_ref[...])
        dp_ref[...] = dab.astype(BF16)

        @pl.when(pl.program_id(0) == 0)
        def _():
            dprm_ref[...] = jnp.zeros_like(dprm_ref)

        dprm_ref[0:1, :] += dalog
        dprm_ref[1:2, :] += ddtb

    blk = pl.BlockSpec((tm, LANES), lambda t: (t, col0 // LANES))
    return pl.pallas_call(
        body, name="gb_bwd", grid=(rows // tm,),
        in_specs=[blk, pl.BlockSpec((8, LANES), lambda t: (0, 0)), pl.BlockSpec((tm, LANES), lambda t: (t, 0)),
                  pl.BlockSpec(memory_space=pl.ANY)],
        out_specs=[blk, pl.BlockSpec((8, LANES), lambda t: (0, 0))],
        out_shape=[jax.ShapeDtypeStruct(dp.shape, dp.dtype), jax.ShapeDtypeStruct((8, LANES), F32)],
        input_output_aliases={3: 0},
        compiler_params=_params(("arbitrary",)),
    )(p, prm, dgb, dp)


def _lru_scans(scans):
    C = scans[0][0].shape[1]
    row = lax.broadcasted_iota(jnp.int32, (SUBLANES, C), 0)
    carries = tuple(jnp.zeros((1, C), F32) for _ in scans)
    for si in range(len(scans[0][4])):
        nb = scans[0][4][si][1] // SUBLANES
        assert all(sc[4][si][1] // SUBLANES == nb for sc in scans)

        def blk(i, carries, si=si, nb=nb):
            out = []
            for (a_ref, b_ref, h_ref, hp_ref, segs), carry in zip(scans, carries):
                start, _, reverse = segs[si]
                r0 = pl.multiple_of(start + (nb - 1 - i if reverse else i) * SUBLANES, SUBLANES)
                A = a_ref[pl.ds(r0, SUBLANES), :]
                Bv = b_ref[pl.ds(r0, SUBLANES), :]
                for s in (1, 2, 4):
                    sh = SUBLANES - s if reverse else s
                    m = (row < SUBLANES - s) if reverse else (row >= s)
                    Bv = jnp.where(m, A * pltpu.roll(Bv, sh, 0) + Bv, Bv)
                    A = jnp.where(m, A * pltpu.roll(A, sh, 0), A)
                Hv = Bv + A * carry
                h_ref[pl.ds(r0, SUBLANES), :] = Hv
                if hp_ref is not None:
                    if reverse:
                        hp = jnp.where(row < SUBLANES - 1, pltpu.roll(Hv, SUBLANES - 1, 0), carry)
                    else:
                        hp = jnp.where(row >= 1, pltpu.roll(Hv, 1, 0), carry)
                    hp_ref[pl.ds(r0, SUBLANES), :] = hp
                out.append(Hv[0:1] if reverse else Hv[SUBLANES - 1:SUBLANES])
            return tuple(out)

        carries = lax.fori_loop(0, nb, blk, carries)


def _lru_orders(T, nc, d):
    N = T - nc
    if d == 0:
        return [(0, nc, False), (nc, N, False)], [(nc, N, True), (0, nc, True)]
    return [(0, nc, True), (nc, N, True)], [(nc, N, False), (0, nc, False)]


def _bdot(a, b, dims=(((1,), (0,)), ((), ()))):
    return lax.dot_general(a.astype(BF16), b.astype(BF16), dims, preferred_element_type=F32)


_NT = (((1,), (1,)), ((), ()))
_TN = (((0,), (0,)), ((), ()))


def _blockdiag(w, C):
    nd, nb, bd, _ = w.shape
    per = C // bd
    out = jnp.einsum('dnpij,pq->dnpiqj', w.reshape(nd, nb // per, per, bd, bd), jnp.eye(per, dtype=w.dtype))
    return out.reshape(nd, nb // per, C, C)


def _blockdiag_extract(dw, bd):
    nd, nj, C, _ = dw.shape
    per = C // bd
    out = jnp.einsum('dnpiqj,pq->dnpij', dw.reshape(nd, nj, per, bd, per, bd), jnp.eye(per, dtype=dw.dtype))
    return out.reshape(nd, nj * per, bd, bd)


def _lru_fwd(p, cw, lv, wr, wi, *, B, T, nc, LW, col0, C):
    N = T - nc
    nj = LW // C

    def body(x_ref, cw_ref, lv_ref, wr_ref, wi_ref, o_ref, a_s, b_s, h_s):
        lv_ = lv_ref[...]
        xc = _conv_fwd(x_ref[:, 0:C], cw_ref[...], _seg_taps(T, nc, 4, 2)) + lv_[0:1]
        for d in (0, 1):
            r = _sigmoid(_bdot(xc, wr_ref[d, 0]) + lv_[1 + d:2 + d])
            i = _sigmoid(_bdot(xc, wi_ref[d, 0]) + lv_[3 + d:4 + d])
            la = -LRU_C * r * _softplus(-lv_[5 + d:6 + d])
            a_s[d] = jnp.exp(la)
            b_s[d] = jnp.sqrt(1.0 - jnp.exp(2.0 * la)) * i * xc
        _lru_scans([(a_s.at[d], b_s.at[d], h_s.at[d], None, _lru_orders(T, nc, d)[0]) for d in (0, 1)])
        o_ref[...] = ((h_s[0, nc:, :] + h_s[1, nc:, :]) * _gelu(x_ref[nc:, C:2 * C])).astype(BF16)

    return pl.pallas_call(
        body, name="lru_fwd", grid=(B, nj),
        in_specs=[pl.BlockSpec((T, 2 * C), lambda b, j: (b, col0 // (2 * C) + j)),
                  pl.BlockSpec((4, C), lambda b, j: (0, j)), pl.BlockSpec((8, C), lambda b, j: (0, j)),
                  pl.BlockSpec((2, 1, C, C), lambda b, j: (0, j, 0, 0)), pl.BlockSpec((2, 1, C, C), lambda b, j: (0, j, 0, 0))],
        out_specs=pl.BlockSpec((N, C), lambda b, j: (b, j)),
        out_shape=jax.ShapeDtypeStruct((B * N, LW), BF16),
        scratch_shapes=[pltpu.VMEM((2, T, C), F32)] * 3,
        compiler_params=_params(("parallel", "parallel")),
    )(p, cw, lv, wr, wi)


def _lru_bwd(p, cw, lv, wr, wi, dy, dp, *, B, T, nc, LW, col0, C):
    N = T - nc
    nj = LW // C

    def body(x_ref, cw_ref, lv_ref, wr_ref, wi_ref, dy_ref, dp_any, dp_ref, dcw_ref, dlv_ref, dwr_ref, dwi_ref,
             a_s, b_s, h_s, hp_s, mu_s, mup_s, dh_s, dxc_s):
        taps = _seg_taps(T, nc, 4, 2)
        lv_ = lv_ref[...]
        xl = x_ref[:, 0:C]
        xc = _conv_fwd(xl, cw_ref[...], taps) + lv_[0:1]
        gel, gelu_vjp = jax.vjp(_gelu, x_ref[nc:, C:2 * C])
        dh_s[0:nc, :] = jnp.zeros((nc, C), F32)
        dh_s[nc:, :] = dy_ref[...] * gel
        dxc_s[...] = jnp.zeros_like(dxc_s)

        @pl.when(pl.program_id(1) == 0)
        def _():
            dcw_ref[...] = jnp.zeros_like(dcw_ref)
            dlv_ref[...] = jnp.zeros_like(dlv_ref)
            dwr_ref[...] = jnp.zeros_like(dwr_ref)
            dwi_ref[...] = jnp.zeros_like(dwi_ref)

        def gates(d):
            lam = lv_[5 + d:6 + d]
            r = _sigmoid(_bdot(xc, wr_ref[d, 0]) + lv_[1 + d:2 + d])
            i = _sigmoid(_bdot(xc, wi_ref[d, 0]) + lv_[3 + d:4 + d])
            sp = _softplus(-lam)
            la = -LRU_C * r * sp
            e2 = jnp.exp(2.0 * la)
            return lam, r, i, sp, la, e2, jnp.sqrt(1.0 - e2)

        for d in (0, 1):
            _, _, i, _, la, _, mult = gates(d)
            a_s[d] = jnp.exp(la)
            b_s[d] = mult * i * xc
        _lru_scans([(a_s.at[d], b_s.at[d], h_s.at[d], hp_s.at[d], _lru_orders(T, nc, d)[0]) for d in (0, 1)])
        for d in (0, 1):
            b_s[d] = a_s[d] * dh_s[...]
        _lru_scans([(a_s.at[d], b_s.at[d], mu_s.at[d], mup_s.at[d], _lru_orders(T, nc, d)[1]) for d in (0, 1)])

        for d in (0, 1):
            lam, r, i, sp, la, e2, mult = gates(d)
            a = a_s[d]
            dinp = dh_s[...] + mup_s[d]
            da = dinp * hp_s[d]
            dmult = dinp * i * xc
            di = dinp * mult * xc
            dla = da * a - dmult * e2 / mult
            dpre_r = (dla * (-LRU_C * sp)) * r * (1.0 - r)
            dpre_i = di * i * (1.0 - i)
            dsp = jnp.sum(dla * (-LRU_C * r), axis=0, keepdims=True)
            dxc_s[...] += dinp * mult * i + _bdot(dpre_r, wr_ref[d, 0], _NT) + _bdot(dpre_i, wi_ref[d, 0], _NT)
            dwr_ref[d, 0] += _bdot(xc, dpre_r, _TN)
            dwi_ref[d, 0] += _bdot(xc, dpre_i, _TN)
            dlv_ref[1 + d:2 + d, :] += jnp.sum(dpre_r, axis=0, keepdims=True)
            dlv_ref[3 + d:4 + d, :] += jnp.sum(dpre_i, axis=0, keepdims=True)
            dlv_ref[5 + d:6 + d, :] += -dsp * _sigmoid(-lam)

        dxc = dxc_s[...]
        dxl, dw = _conv_bwd(xl, cw_ref[...], taps, dxc)
        dcw_ref[0:4, :] += dw
        dlv_ref[0:1, :] += jnp.sum(dxc, axis=0, keepdims=True)
        dp_ref[:, 0:C] = dxl.astype(BF16)
        (dyl,) = gelu_vjp(dy_ref[...] * (h_s[0, nc:, :] + h_s[1, nc:, :]))
        dp_ref[0:nc, C:2 * C] = jnp.zeros((nc, C), BF16)
        dp_ref[nc:, C:2 * C] = dyl.astype(BF16)

    xblk = pl.BlockSpec((T, 2 * C), lambda j, b: (b, col0 // (2 * C) + j))
    wblk = pl.BlockSpec((2, 1, C, C), lambda j, b: (0, j, 0, 0))
    vblk = pl.BlockSpec((8, C), lambda j, b: (0, j))
    return pl.pallas_call(
        body, name="lru_bwd", grid=(nj, B),
        in_specs=[xblk, pl.BlockSpec((4, C), lambda j, b: (0, j)), vblk, wblk, wblk,
                  pl.BlockSpec((N, C), lambda j, b: (b, j)), pl.BlockSpec(memory_space=pl.ANY)],
        out_specs=[xblk, vblk, vblk, wblk, wblk],
        out_shape=[jax.ShapeDtypeStruct(dp.shape, dp.dtype), jax.ShapeDtypeStruct((8, LW), F32),
                   jax.ShapeDtypeStruct((8, LW), F32), jax.ShapeDtypeStruct((2, nj, C, C), F32),
                   jax.ShapeDtypeStruct((2, nj, C, C), F32)],
        scratch_shapes=[pltpu.VMEM((2, T, C), F32)] * 6 + [pltpu.VMEM((T, C), F32)] * 2,
        input_output_aliases={6: 0},
        compiler_params=_params(("parallel", "arbitrary")),
    )(p, cw, lv, wr, wi, dy, dp)


def _chunk_masks(upper):
    i = lax.broadcasted_iota(jnp.int32, (CHUNK, CHUNK), 0)
    j = lax.broadcasted_iota(jnp.int32, (CHUNK, CHUNK), 1)
    ahead = jnp.where(upper, j - i, i - j)
    return i == j, ahead >= 0, ahead > 0


def _col2row(c, eye):
    return jnp.sum(jnp.where(eye, c, 0.0), axis=0, keepdims=True)


def _row2col(r, eye):
    return jnp.sum(jnp.where(eye, r, 0.0), axis=1, keepdims=True)


def _rowsum(x):
    return jnp.sum(x, axis=1, keepdims=True)


def _unit_tri_inverses(Ls):
    G = len(Ls)
    W = G * CHUNK
    blk = (lax.broadcasted_iota(jnp.int32, (W, W), 0) // CHUNK) == (lax.broadcasted_iota(jnp.int32, (W, W), 1) // CHUNK)

    def pdot(a, b):
        bd = jnp.where(blk, jnp.tile(b.astype(BF16), (G, 1)), jnp.zeros((), BF16))
        return jnp.dot(a.astype(BF16), bd, preferred_element_type=F32)

    Xp = -(Ls[0] if G == 1 else jnp.concatenate(Ls, axis=1))
    Rm = Xp
    for _ in range(int(math.log2(CHUNK)) - 1):
        Xp = pdot(Xp, Xp)
        Rm = Rm + Xp + pdot(Rm, Xp)
    eye = _chunk_masks(False)[0]
    return [jnp.where(eye, 1.0, 0.0) + Rm[:, g * CHUNK:(g + 1) * CHUNK] for g in range(G)]


def _delta_chunk_common(q, k, v, gcol, bcol, upper):
    eye, incl, strict = _chunk_masks(upper)
    gc = _rowsum(jnp.where(incl, _col2row(gcol, eye), 0.0))
    D = jnp.where(incl, jnp.exp(jnp.minimum(gc - _col2row(gc, eye), 0.0)), 0.0)
    kb = k * bcol
    AP = _bdot(jnp.concatenate([kb, q], axis=0), k, _NT)
    A = AP[:CHUNK]
    L = jnp.where(strict, A * D, 0.0)
    eg = jnp.exp(gc)
    gl = jnp.sum(gcol, axis=0, keepdims=True)
    attn = jnp.where(incl, AP[CHUNK:] * D, 0.0)
    return dict(eye=eye, incl=incl, strict=strict, gc=gc, D=D, kb=kb, A=A, L=L, eg=eg, gl=gl, egl=jnp.exp(gl),
                attn=attn, kbe=kb * eg, vb=v * bcol, qe=q * eg, kd=k * jnp.exp(gl - gc))


def _delta_group_pre(chunks, upper):
    cs = [_delta_chunk_common(*ch, upper) for ch in chunks]
    out = []
    for c, Tm in zip(cs, _unit_tri_inverses([c["L"] for c in cs])):
        dk = c["kbe"].shape[1]
        wu = _bdot(Tm, jnp.concatenate([c["kbe"], c["vb"]], axis=1))
        KN = _bdot(c["kd"], wu, _TN)
        QO = _bdot(c["attn"], wu)
        out.append((Tm, KN[:, :dk], KN[:, dk:], c["qe"] - QO[:, :dk], QO[:, dk:], c["egl"]))
    return out


def _delta_chunk_bwd(q, k, v, gcol, bcol, S, Tm, do, dS2, upper):
    c = _delta_chunk_common(q, k, v, gcol, bcol, upper)
    eye, incl, strict, D, eg, egl = c["eye"], c["incl"], c["strict"], c["D"], c["eg"], c["egl"]
    kb, kbe, vb, qe, kd, attn = c["kb"], c["kbe"], c["vb"], c["qe"], c["kd"], c["attn"]
    dkk = kbe.shape[1]
    wu = _bdot(Tm, jnp.concatenate([kbe, vb], axis=1))
    w = wu[:, :dkk]
    vn = wu[:, dkk:] - _bdot(w, S)
    dvn = _bdot(kd, dS2) + _bdot(attn, do, _TN)
    dkd = _bdot(vn, dS2, _NT)
    dgl = jnp.sum(_rowsum(dS2 * S), axis=0, keepdims=True) * egl
    dqa = _bdot(do, jnp.concatenate([S, vn], axis=0), _NT)
    dqe = dqa[:, :dkk]
    dattn = jnp.where(incl, dqa[:, dkk:], 0.0)
    dw = -_bdot(dvn, S, _NT)
    r = _rowsum(dkd * kd)
    dk = dkd * jnp.exp(c["gl"] - c["gc"])
    dgl = dgl + jnp.sum(r, axis=0, keepdims=True)
    dgc = _rowsum(dqe * qe) - r
    E = dattn * attn
    dvw = jnp.concatenate([dvn, dw], axis=1)
    dTm = _bdot(dvw, jnp.concatenate([vb, kbe], axis=1), _NT)
    dvk = _bdot(Tm, dvw, _TN)
    dvb = dvk[:, :dvn.shape[1]]
    dv = dvb * bcol
    dbeta = _rowsum(dvb * v)
    dkbe = dvk[:, dvn.shape[1]:]
    dkb = dkbe * eg
    dgc = dgc + _rowsum(dkbe * kbe)
    dL = jnp.where(strict, -_bdot(Tm, _bdot(dTm, Tm, _NT), _TN), 0.0)
    dA = dL * D
    E = E + dL * c["L"]
    PA = jnp.concatenate([dattn * D, dA], axis=0)
    PAk = _bdot(PA, k)
    dq = dqe * eg + PAk[:CHUNK]
    dkb = dkb + PAk[CHUNK:]
    dk = dk + _bdot(PA, jnp.concatenate([q, kb], axis=0), _TN) + dkb * bcol
    dbeta = dbeta + _rowsum(dkb * k)
    dgc = dgc + _rowsum(E) - _row2col(jnp.sum(E, axis=0, keepdims=True), eye)
    dg = _row2col(jnp.sum(jnp.where(incl, dgc, 0.0), axis=0, keepdims=True), eye) + dgl
    return dq, dk, dv, dg, dbeta


def _delta_group(n):
    return max(g for g in (4, 3, 2, 1) if n % g == 0)


def _delta_chunk_at(T, nc, d, i):
    n, ncc = T // CHUNK, nc // CHUNK
    desc = jnp.where(i < ncc, ncc - 1 - i, n - 1 - (i - ncc))
    if isinstance(d, int):
        return i if d == 0 else desc
    return jnp.where(d == 0, i, desc)


def _dn_out_math(o, onorm, z):
    return _rmsn(o, onorm) * _silu(z)


def _delta_fwd(qkv, gb, p, onorm, *, B, T, nc, H, HD):
    N = T - nc
    n = T // CHUNK
    G = _delta_group(n)

    def body(q_ref, k_ref, v_ref, gb_ref, z_ref, on_ref, y_ref, o_ref, Tm_ref, K_ref, S_ref, Qp_ref, eg_ref,
             N_s, O0_s, o_s):
        h = pl.program_id(1)
        lane = lax.broadcasted_iota(jnp.int32, (CHUNK, LANES), 1)

        def pre(g, carry):
            cs = [g * G + i for i in range(G)]
            rows = [pl.ds(pl.multiple_of(c * CHUNK, CHUNK), CHUNK) for c in cs]
            for d in (0, 1):
                chunks = []
                for r in rows:
                    gbb = gb_ref[r, :]
                    chunks.append((q_ref[r, :], k_ref[r, :], v_ref[r, :],
                                   _rowsum(jnp.where(lane == d * H + h, gbb, 0.0)),
                                   _rowsum(jnp.where(lane == 2 * H + d * H + h, gbb, 0.0))))
                for c, r, (Tm, K, Nn, Qp, O0, egl) in zip(cs, rows, _delta_group_pre(chunks, d == 1)):
                    Tm_ref[0, d * n + c] = Tm
                    K_ref[0, d * n + c] = K.astype(BF16)
                    N_s[d * n + c] = Nn
                    Qp_ref[0, d, r, :] = Qp.astype(BF16)
                    O0_s[d, r, :] = O0
                    eg_ref[0, d * n + c] = jnp.broadcast_to(egl, (SUBLANES, HD))
            return carry

        lax.fori_loop(0, n // G, pre, 0)

        def step(i, Ss):
            out = []
            for d in (0, 1):
                c = _delta_chunk_at(T, nc, d, i)
                rows = pl.ds(pl.multiple_of(c * CHUNK, CHUNK), CHUNK)
                S_ref[0, d * n + c] = Ss[d]
                Sb = Ss[d].astype(BF16)
                o_s[d, rows, :] = jnp.dot(Qp_ref[0, d, rows, :], Sb, preferred_element_type=F32) + O0_s[d, rows, :]
                out.append(eg_ref[0, d * n + c][0:1] * Ss[d] + N_s[d * n + c]
                           - jnp.dot(K_ref[0, d * n + c], Sb, preferred_element_type=F32))
            return tuple(out)

        lax.fori_loop(0, n, step, (jnp.zeros((HD, HD), F32), jnp.zeros((HD, HD), F32)))
        o = o_s[0, nc:, :] + o_s[1, nc:, :]
        o_ref[...] = o
        y_ref[...] = _dn_out_math(o, on_ref[...], z_ref[nc:, :]).astype(BF16)

    col = lambda off: pl.BlockSpec((T, HD), lambda b, h: (b, off + h))
    lat = pl.BlockSpec((N, HD), lambda b, h: (b, h))
    per = lambda *blk: pl.BlockSpec((1, *blk), lambda b, h: (b * H + h, 0, 0, 0))
    return pl.pallas_call(
        body, name="delta_fwd", grid=(B, H),
        in_specs=[col(0), col(H), col(2 * H), pl.BlockSpec((T, LANES), lambda b, h: (b, 0)), col(3 * H),
                  pl.BlockSpec((1, HD), lambda b, h: (0, 0))],
        out_specs=[lat, lat, per(2 * n, CHUNK, CHUNK), per(2 * n, HD, HD), per(2 * n, HD, HD), per(2, T, HD),
                   per(2 * n, SUBLANES, HD)],
        out_shape=[jax.ShapeDtypeStruct((B * N, H * HD), BF16), jax.ShapeDtypeStruct((B * N, H * HD), F32),
                   jax.ShapeDtypeStruct((B * H, 2 * n, CHUNK, CHUNK), F32),
                   jax.ShapeDtypeStruct((B * H, 2 * n, HD, HD), BF16), jax.ShapeDtypeStruct((B * H, 2 * n, HD, HD), F32),
                   jax.ShapeDtypeStruct((B * H, 2, T, HD), BF16), jax.ShapeDtypeStruct((B * H, 2 * n, SUBLANES, HD), F32)],
        scratch_shapes=[pltpu.VMEM((2 * n, HD, HD), F32), pltpu.VMEM((2, T, HD), F32), pltpu.VMEM((2, T, HD), F32)],
        compiler_params=_params(("parallel", "parallel")),
    )(qkv, qkv, qkv, gb, p, onorm)


def _delta_bwd(qkv, gb, p, onorm, o, res, dy, dp, *, B, T, nc, H, HD):
    N = T - nc
    n = T // CHUNK

    def body(q_ref, k_ref, v_ref, gb_ref, z_ref, on_ref, o_ref, dy_ref, Tm_ref, K_ref, S_ref, Qp_ref, eg_ref, dp_any,
             dqkv_ref, dgb_ref, dp_ref, don_ref, do_s, R_s, dS_s):
        h, d = pl.program_id(1), pl.program_id(2)
        lane = lax.broadcasted_iota(jnp.int32, (CHUNK, LANES), 1)

        @pl.when(d == 0)
        def _():
            _, vjp = jax.vjp(_dn_out_math, o_ref[...], on_ref[...], z_ref[nc:, :])
            do, don, dz = vjp(dy_ref[...])
            do_s[0:nc, :] = jnp.zeros((nc, HD), F32)
            do_s[nc:, :] = do
            dp_ref[0:nc, :] = jnp.zeros((nc, HD), BF16)
            dp_ref[nc:, :] = dz.astype(BF16)
            dqkv_ref[...] = jnp.zeros_like(dqkv_ref)

            @pl.when(h == 0)
            def _():
                don_ref[...] = jnp.zeros_like(don_ref)
                dgb_ref[...] = jnp.zeros_like(dgb_ref)

            don_ref[0, 0:1, :] += don

        def r_of(c, carry):
            rows = pl.ds(pl.multiple_of(c * CHUNK, CHUNK), CHUNK)
            R_s[c] = lax.dot_general(Qp_ref[0, 0, rows, :], do_s[rows, :].astype(BF16), _TN, preferred_element_type=F32)
            return carry

        lax.fori_loop(0, n, r_of, 0)

        def bwd_step(i, dS):
            c = _delta_chunk_at(T, nc, d, n - 1 - i)
            dS_s[c] = dS
            return (eg_ref[0, c][0:1] * dS + R_s[c]
                    - lax.dot_general(K_ref[0, c], dS.astype(BF16), _TN, preferred_element_type=F32))

        lax.fori_loop(0, n, bwd_step, jnp.zeros((HD, HD), F32))

        def grads(c, carry):
            rows = pl.ds(pl.multiple_of(c * CHUNK, CHUNK), CHUNK)
            gbb = gb_ref[rows, :]
            gcol = _rowsum(jnp.where(lane == d * H + h, gbb, 0.0))
            bcol = _rowsum(jnp.where(lane == 2 * H + d * H + h, gbb, 0.0))
            dq, dk, dv, dg, dbeta = _delta_chunk_bwd(q_ref[rows, :], k_ref[rows, :], v_ref[rows, :], gcol, bcol,
                                                     S_ref[0, c], Tm_ref[0, c], do_s[rows, :], dS_s[c], d == 1)
            dqkv_ref[0, rows, :] += dq
            dqkv_ref[1, rows, :] += dk
            dqkv_ref[2, rows, :] += dv
            dgb_ref[rows, :] += (jnp.where(lane == d * H + h, dg, 0.0)
                                 + jnp.where(lane == 2 * H + d * H + h, dbeta, 0.0))
            return carry

        lax.fori_loop(0, n, grads, 0)

    col = lambda off: pl.BlockSpec((T, HD), lambda b, h, d: (b, off + h))
    lat = pl.BlockSpec((N, HD), lambda b, h, d: (b, h))
    per = lambda *blk: pl.BlockSpec((1, *blk), lambda b, h, d: (b * H + h, d, 0, 0))
    return pl.pallas_call(
        body, name="delta_bwd", grid=(B, H, 2),
        in_specs=[col(0), col(H), col(2 * H), pl.BlockSpec((T, LANES), lambda b, h, d: (b, 0)), col(3 * H),
                  pl.BlockSpec((1, HD), lambda b, h, d: (0, 0)), lat, lat,
                  per(n, CHUNK, CHUNK), per(n, HD, HD), per(n, HD, HD), per(1, T, HD), per(n, SUBLANES, HD),
                  pl.BlockSpec(memory_space=pl.ANY)],
        out_specs=[pl.BlockSpec((3, T, HD), lambda b, h, d: (0, b, h)), pl.BlockSpec((T, LANES), lambda b, h, d: (b, 0)),
                   col(3 * H), pl.BlockSpec((1, 8, HD), lambda b, h, d: (b, 0, 0))],
        out_shape=[jax.ShapeDtypeStruct((3, B * T, H * HD), F32), jax.ShapeDtypeStruct((B * T, LANES), F32),
                   jax.ShapeDtypeStruct(dp.shape, dp.dtype), jax.ShapeDtypeStruct((B, 8, HD), F32)],
        scratch_shapes=[pltpu.VMEM((T, HD), F32), pltpu.VMEM((n, HD, HD), F32), pltpu.VMEM((n, HD, HD), F32)],
        input_output_aliases={13: 2},
        compiler_params=_params(("parallel", "arbitrary", "arbitrary")),
    )(qkv, qkv, qkv, gb, p, onorm, o, dy, *res, dp)


def _rowwise(fn, ins, out_dtypes, *, name, tm=256, mult=16):
    R, W = ins[0].shape
    tm = _tile(R, tm, mult)

    def body(*refs):
        outs = fn(*[r[...] for r in refs[:len(ins)]])
        for o_ref, o in zip(refs[len(ins):], outs):
            o_ref[...] = o.astype(o_ref.dtype)

    spec = pl.BlockSpec((tm, W), lambda i: (i, 0))
    return pl.pallas_call(
        body, name=name, grid=(R // tm,), in_specs=[spec] * len(ins), out_specs=[spec] * len(out_dtypes),
        out_shape=[jax.ShapeDtypeStruct((R, W), dt) for dt in out_dtypes],
        compiler_params=_params(("parallel",)),
    )(*ins)


def _sum_lead(x, *, name, tm=256, mult=16):
    S, R, W = x.shape
    tm = _tile(R, tm, mult)

    def body(*refs):
        acc = refs[0][0].astype(F32)
        for r in refs[1:S]:
            acc = acc + r[0].astype(F32)
        refs[S][...] = acc

    return pl.pallas_call(
        body, name=name, grid=(R // tm,),
        in_specs=[pl.BlockSpec((1, tm, W), functools.partial(lambda s, i: (s, i, 0), s)) for s in range(S)],
        out_specs=pl.BlockSpec((tm, W), lambda i: (i, 0)),
        out_shape=jax.ShapeDtypeStruct((R, W), F32),
        compiler_params=_params(("parallel",)),
    )(*([x] * S))


def _adamw_math(w, g, m, v):
    m = ADAM_B1 * m + (1.0 - ADAM_B1) * g
    v = ADAM_B2 * v + (1.0 - ADAM_B2) * (g * g)
    m_hat = m / (1.0 - ADAM_B1 ** ADAM_STEP)
    v_hat = v / (1.0 - ADAM_B2 ** ADAM_STEP)
    return -ADAM_LR * (m_hat / (jnp.sqrt(v_hat) + ADAM_EPS) + ADAM_WD * w), m, v


def _adamw(w, g, m, v, *, name):
    tm = max(SUBLANES, (256 * 1024) // w.shape[1] // SUBLANES * SUBLANES)
    return _rowwise(_adamw_math, [w, g, m, v], [F32, F32, F32], name=name, tm=tm, mult=SUBLANES)


def _me():
    return lax.axis_index("x"), lax.axis_index("y"), lax.axis_index("c")


def _allgather_small(v):
    R, W = v.shape

    def body(x_ref, out_ref, send_sems, recv_sems, local_sem):
        x, y, c = _me()
        me, sibling = (x, y, c), (x, y, 1 - c)
        chips = [(1 - x, y), (x, 1 - y), (1 - x, 1 - y)]

        def slot(px, py, pc):
            return out_ref.at[4 * px + 2 * py + pc]

        def copy(k, block, to, src=None):
            return pltpu.make_async_remote_copy(
                src_ref=slot(*block) if src is None else src, dst_ref=slot(*block),
                send_sem=send_sems.at[k], recv_sem=recv_sems.at[k], device_id=to, device_id_type=MESH)

        mine = pltpu.make_async_copy(x_ref, slot(*me), local_sem)
        mine.start()
        first = [copy(0, me, sibling, src=x_ref)]
        first += [copy(1 + j, me, (*chip, c), src=x_ref) for j, chip in enumerate(chips)]
        for cp in first:
            cp.start()
        passed = [copy(4 + j, (*chip, c), sibling) for j, chip in enumerate(chips)]
        for j, chip in enumerate(chips):
            copy(1 + j, (*chip, c), me).wait_recv()
            passed[j].start()
        copy(0, sibling, me).wait_recv()
        for j, chip in enumerate(chips):
            copy(4 + j, (*chip, 1 - c), me).wait_recv()
        for cp in first + passed:
            cp.wait_send()
        mine.wait()

    return pl.pallas_call(
        body, name="allgather_small", out_shape=jax.ShapeDtypeStruct((8, R, W), v.dtype),
        in_specs=[pl.BlockSpec(memory_space=pltpu.VMEM)], out_specs=pl.BlockSpec(memory_space=pltpu.VMEM),
        scratch_shapes=[pltpu.SemaphoreType.DMA((7,)), pltpu.SemaphoreType.DMA((7,)), pltpu.SemaphoreType.DMA],
        compiler_params=_params(),
    )(v)


_ANY = pl.BlockSpec(memory_space=pl.ANY)


def _allgather_halves(shards, *, name):
    nw = len(shards)

    def body(*refs):
        x_refs, out_refs = refs[:nw], refs[nw:2 * nw]
        send_sems, recv_sems, local_sems = refs[2 * nw:]
        x, y, c = _me()
        me, sibling = (x, y, c), (x, y, 1 - c)
        chips = [(1 - x, y), (x, 1 - y), (1 - x, 1 - y)]

        def slot(w, px, py, pc):
            return out_refs[w].at[4 * px + 2 * py + pc]

        def copy(w, k, block, to, src=None):
            return pltpu.make_async_remote_copy(
                src_ref=slot(w, *block) if src is None else src, dst_ref=slot(w, *block),
                send_sem=send_sems.at[w, k], recv_sem=recv_sems.at[w, k], device_id=to, device_id_type=MESH)

        started, local = [], []
        for w in range(nw):
            half = shards[w].shape[0] // 2
            own = x_refs[w].at[pl.ds(c * half, half), :]
            mine = pltpu.make_async_copy(own, slot(w, *me), local_sems.at[w])
            mine.start()
            first = [copy(w, 0, me, sibling, src=own)]
            first += [copy(w, 1 + j, me, (*chip, c), src=own) for j, chip in enumerate(chips)]
            for cp in first:
                cp.start()
            started += first
            local.append(mine)
        for w in range(nw):
            for j, chip in enumerate(chips):
                copy(w, 1 + j, (*chip, c), me).wait_recv()
                fwd = copy(w, 4 + j, (*chip, c), sibling)
                fwd.start()
                started.append(fwd)
        for w in range(nw):
            copy(w, 0, sibling, me).wait_recv()
            for j, chip in enumerate(chips):
                copy(w, 4 + j, (*chip, 1 - c), me).wait_recv()
        for cp in started:
            cp.wait_send()
        for cp in local:
            cp.wait()

    return pl.pallas_call(
        body, name=name,
        out_shape=[jax.ShapeDtypeStruct((8, s.shape[0] // 2, s.shape[1]), s.dtype) for s in shards],
        in_specs=[_ANY] * nw, out_specs=[_ANY] * nw,
        scratch_shapes=[pltpu.SemaphoreType.DMA((nw, 7)), pltpu.SemaphoreType.DMA((nw, 7)), pltpu.SemaphoreType.DMA((nw,))],
        compiler_params=_params(),
    )(*shards)


def _sibling_send_halves(arrs, *, name):
    nw = len(arrs)

    def body(*refs):
        x_refs, out_refs, send_sems, recv_sems = refs[:nw], refs[nw:2 * nw], refs[2 * nw], refs[2 * nw + 1]
        x, y, c = _me()
        cps = []
        for w in range(nw):
            half = arrs[w].shape[1] // 2
            cp = pltpu.make_async_remote_copy(
                src_ref=x_refs[w].at[:, pl.ds((1 - c) * half, half), :], dst_ref=out_refs[w],
                send_sem=send_sems.at[w], recv_sem=recv_sems.at[w], device_id=(x, y, 1 - c), device_id_type=MESH)
            cp.start()
            cps.append(cp)
        for cp in cps:
            cp.wait()

    return pl.pallas_call(
        body, name=name,
        out_shape=[jax.ShapeDtypeStruct((a.shape[0], a.shape[1] // 2, a.shape[2]), a.dtype) for a in arrs],
        in_specs=[_ANY] * nw, out_specs=[_ANY] * nw,
        scratch_shapes=[pltpu.SemaphoreType.DMA((nw,)), pltpu.SemaphoreType.DMA((nw,))],
        compiler_params=_params(),
    )(*arrs)


def _sibling_swap(arrs, *, name):
    nw = len(arrs)

    def body(*refs):
        x_refs, out_refs, send_sems, recv_sems = refs[:nw], refs[nw:2 * nw], refs[2 * nw], refs[2 * nw + 1]
        x, y, c = _me()
        cps = []
        for w in range(nw):
            cp = pltpu.make_async_remote_copy(
                src_ref=x_refs[w], dst_ref=out_refs[w], send_sem=send_sems.at[w], recv_sem=recv_sems.at[w],
                device_id=(x, y, 1 - c), device_id_type=MESH)
            cp.start()
            cps.append(cp)
        for cp in cps:
            cp.wait()

    return pl.pallas_call(
        body, name=name, out_shape=[jax.ShapeDtypeStruct(a.shape, a.dtype) for a in arrs],
        in_specs=[_ANY] * nw, out_specs=[_ANY] * nw,
        scratch_shapes=[pltpu.SemaphoreType.DMA((nw,)), pltpu.SemaphoreType.DMA((nw,))],
        compiler_params=_params(),
    )(*arrs)


def _adamw_halves(w, own, sib, m, v, c_arr, *, name):
    r, cols = w.shape
    h = r // 2
    tm = _tile(h, max(SUBLANES, (192 * 1024) // cols // SUBLANES * SUBLANES), SUBLANES)
    nb = h // tm

    def body(c_ref, w_ref, own_ref, sib_ref, m_ref, v_ref, g_out, d_out, m_out, v_out):
        g = jnp.where(pl.program_id(0) == c_ref[0], own_ref[...], sib_ref[...])
        g_out[...] = g
        d_out[...], m_out[...], v_out[...] = _adamw_math(w_ref[...], g, m_ref[...], v_ref[...])

    full = pl.BlockSpec((tm, cols), lambda hh, i, c_ref: (hh * nb + i, 0))
    half = pl.BlockSpec((tm, cols), lambda hh, i, c_ref: (i, 0))
    return pl.pallas_call(
        body, name=name,
        grid_spec=pltpu.PrefetchScalarGridSpec(num_scalar_prefetch=1, grid=(2, nb),
                                               in_specs=[full, half, half, full, full], out_specs=[full] * 4),
        out_shape=[jax.ShapeDtypeStruct((r, cols), F32)] * 4,
        compiler_params=_params(("parallel", "parallel")),
    )(c_arr, w, own, sib, m, v)


def _chip_exchange(arrs, *, name):
    nw = len(arrs)

    def body(*refs):
        x_refs, out_refs = refs[:nw], refs[nw:2 * nw]
        send_sems, recv_sems, local_sems = refs[2 * nw:]
        x, y, c = _me()
        s_me = 2 * x + y
        chips = [(1 - x, y), (x, 1 - y), (1 - x, 1 - y)]
        started, local = [], []
        for w in range(nw):
            mine = pltpu.make_async_copy(x_refs[w].at[s_me], out_refs[w].at[s_me], local_sems.at[w])
            mine.start()
            local.append(mine)
            for k, (px, py) in enumerate(chips):
                cp = pltpu.make_async_remote_copy(
                    src_ref=x_refs[w].at[2 * px + py], dst_ref=out_refs[w].at[s_me], send_sem=send_sems.at[w, k],
                    recv_sem=recv_sems.at[w, k], device_id=(px, py, c), device_id_type=MESH)
                cp.start()
                started.append(cp)
        for w in range(nw):
            for k, (px, py) in enumerate(chips):
                pltpu.make_async_remote_copy(
                    src_ref=x_refs[w].at[s_me], dst_ref=out_refs[w].at[2 * px + py], send_sem=send_sems.at[w, k],
                    recv_sem=recv_sems.at[w, k], device_id=(px, py, c), device_id_type=MESH).wait_recv()
        for cp in started:
            cp.wait_send()
        for cp in local:
            cp.wait()

    return pl.pallas_call(
        body, name=name, out_shape=[jax.ShapeDtypeStruct(a.shape, a.dtype) for a in arrs],
        in_specs=[_ANY] * nw, out_specs=[_ANY] * nw,
        scratch_shapes=[pltpu.SemaphoreType.DMA((nw, 3)), pltpu.SemaphoreType.DMA((nw, 3)), pltpu.SemaphoreType.DMA((nw,))],
        compiler_params=_params(),
    )(*arrs)


def _half_add(g, recv, c_arr, *, name):
    S, r, w = g.shape
    h = r // 2
    tm = _tile(h, 512, 16)
    nb = h // tm

    def body(c_ref, g_ref, r_ref, o_ref):
        o_ref[...] = (g_ref[...] + r_ref[...]).astype(BF16)

    return pl.pallas_call(
        body, name=name,
        grid_spec=pltpu.PrefetchScalarGridSpec(
            num_scalar_prefetch=1, grid=(S, nb),
            in_specs=[pl.BlockSpec((1, tm, w), lambda s, i, c_ref: (s, c_ref[0] * nb + i, 0)),
                      pl.BlockSpec((1, tm, w), lambda s, i, c_ref: (s, i, 0))],
            out_specs=pl.BlockSpec((1, tm, w), lambda s, i, c_ref: (s, i, 0))),
        out_shape=jax.ShapeDtypeStruct((S, h, w), BF16),
        compiler_params=_params(("parallel", "parallel")),
    )(c_arr, g, recv)


def _layout(sizes, width, part_mult, total_mult):
    offs, rows, r = [], [], 0
    for n in sizes:
        k = -(-n // width)
        offs.append(r)
        rows.append(k)
        r += -(-k // part_mult) * part_mult
    return offs, rows, -(-r // total_mult) * total_mult


def _pack(arrs, width, part_mult, total_mult, dtype, lead=()):
    nl = len(lead)
    sizes = [math.prod(a.shape[nl:]) for a in arrs]
    offs, rows, total = _layout(sizes, width, part_mult, total_mult)
    parts, r = [], 0
    for a, n, o, k in zip(arrs, sizes, offs, rows):
        kp = -(-k // part_mult) * part_mult
        flat = a.reshape(*lead, n).astype(dtype)
        if kp * width > n:
            flat = jnp.pad(flat, [(0, 0)] * nl + [(0, kp * width - n)])
        parts.append(flat.reshape(*lead, kp, width))
        r = o + kp
    if total > r:
        parts.append(jnp.zeros((*lead, total - r, width), dtype))
    return jnp.concatenate(parts, axis=nl)


def _unpack(pool, shapes, width, part_mult, total_mult):
    lead = pool.shape[:-2]
    sizes = [math.prod(s) for s in shapes]
    offs, rows, _ = _layout(sizes, width, part_mult, total_mult)
    out = []
    for s, n, o, k in zip(shapes, sizes, offs, rows):
        flat = lax.slice_in_dim(pool, o, o + k, axis=len(lead)).reshape(*lead, k * width)
        out.append(lax.slice_in_dim(flat, 0, n, axis=len(lead)).reshape(*lead, *s))
    return out


_WEIGHTS = ("c_ctx", "w_ada", "b_ada", "g_pre_mix", "g_post_mix", "g_pre_ffn", "g_post_ffn", "w_in", "b_merge",
            "dn_conv", "dn_a_log", "dn_dt_bias", "dn_onorm", "lru_conv", "lru_conv_b", "lru_w_rg", "lru_b_rg",
            "lru_w_ig", "lru_b_ig", "lru_lambda", "w_branch_dn", "w_branch_lru", "w_out", "w_up", "ffn_dw",
            "ffn_dw_b", "w_down")
_BIG = {"w_ada": True, "w_in": True, "w_branch_dn": False, "w_branch_lru": False, "w_out": False, "w_up": True,
        "w_down": False}
_SMALL_SHARDED = ("dn_conv", "lru_conv", "lru_b_rg", "lru_b_ig", "lru_lambda", "ffn_dw")
_NCHIP = 4
_FLAT_PART = 8
_FLAT_TOTAL = 256


def _to_chip_shards(g, by_cols):
    if by_cols:
        return g.reshape(g.shape[0], _NCHIP, g.shape[1] // _NCHIP).transpose(1, 0, 2)
    return g.reshape(_NCHIP, g.shape[0] // _NCHIP, g.shape[1])


def _from_chip_shards(s, by_cols):
    if by_cols:
        return s.transpose(1, 0, 2).reshape(s.shape[1], _NCHIP * s.shape[2])
    return s.reshape(_NCHIP * s.shape[1], s.shape[2])


def _dsilu(x):
    s = _sigmoid(x)
    return s * (1.0 + x * (1.0 - s))


def kernel(x, c, ctx, c_ctx, w_ada, b_ada, g_pre_mix, g_post_mix, g_pre_ffn, g_post_ffn, w_in, b_merge, dn_conv, dn_a_log, dn_dt_bias, dn_onorm, lru_conv, lru_conv_b, lru_w_rg, lru_b_rg, lru_w_ig, lru_b_ig, lru_lambda, w_branch_dn, w_branch_lru, w_out, w_up, ffn_dw, ffn_dw_b, w_down, loss_target, m_c_ctx, m_w_ada, m_b_ada, m_g_pre_mix, m_g_post_mix, m_g_pre_ffn, m_g_post_ffn, m_w_in, m_b_merge, m_dn_conv, m_dn_a_log, m_dn_dt_bias, m_dn_onorm, m_lru_conv, m_lru_conv_b, m_lru_w_rg, m_lru_b_rg, m_lru_w_ig, m_lru_b_ig, m_lru_lambda, m_w_branch_dn, m_w_branch_lru, m_w_out, m_w_up, m_ffn_dw, m_ffn_dw_b, m_w_down, v_c_ctx, v_w_ada, v_b_ada, v_g_pre_mix, v_g_post_mix, v_g_pre_ffn, v_g_post_ffn, v_w_in, v_b_merge, v_dn_conv, v_dn_a_log, v_dn_dt_bias, v_dn_onorm, v_lru_conv, v_lru_conv_b, v_lru_w_rg, v_lru_b_rg, v_lru_w_ig, v_lru_b_ig, v_lru_lambda, v_w_branch_dn, v_w_branch_lru, v_w_out, v_w_up, v_ffn_dw, v_ffn_dw_b, v_w_down):
    W = dict(zip(_WEIGHTS, (c_ctx, w_ada, b_ada, g_pre_mix, g_post_mix, g_pre_ffn, g_post_ffn, w_in, b_merge, dn_conv,
                            dn_a_log, dn_dt_bias, dn_onorm, lru_conv, lru_conv_b, lru_w_rg, lru_b_rg, lru_w_ig, lru_b_ig,
                            lru_lambda, w_branch_dn, w_branch_lru, w_out, w_up, ffn_dw, ffn_dw_b, w_down)))
    Mo = dict(zip(_WEIGHTS, (m_c_ctx, m_w_ada, m_b_ada, m_g_pre_mix, m_g_post_mix, m_g_pre_ffn, m_g_post_ffn, m_w_in,
                             m_b_merge, m_dn_conv, m_dn_a_log, m_dn_dt_bias, m_dn_onorm, m_lru_conv, m_lru_conv_b,
                             m_lru_w_rg, m_lru_b_rg, m_lru_w_ig, m_lru_b_ig, m_lru_lambda, m_w_branch_dn,
                             m_w_branch_lru, m_w_out, m_w_up, m_ffn_dw, m_ffn_dw_b, m_w_down)))
    Vo = dict(zip(_WEIGHTS, (v_c_ctx, v_w_ada, v_b_ada, v_g_pre_mix, v_g_post_mix, v_g_pre_ffn, v_g_post_ffn, v_w_in,
                             v_b_merge, v_dn_conv, v_dn_a_log, v_dn_dt_bias, v_dn_onorm, v_lru_conv, v_lru_conv_b,
                             v_lru_w_rg, v_lru_b_rg, v_lru_w_ig, v_lru_b_ig, v_lru_lambda, v_w_branch_dn,
                             v_w_branch_lru, v_w_out, v_w_up, v_ffn_dw, v_ffn_dw_b, v_w_down)))
    B, N, D = x.shape
    NC = ctx.shape[1]
    T = NC + N
    H, HD = dn_a_log.shape[-1], dn_onorm.shape[-1]
    DNW = H * HD
    LW, LBD = lru_conv_b.shape[-1], lru_w_rg.shape[-1]
    DFF = ffn_dw_b.shape[-1]
    LC = LANES
    x_i, y_i, c_i = _me()
    s_me = 2 * x_i + y_i
    tm = _tile(math.gcd(NC, N), 256, 16)

    gathered = _allgather_halves([W[n][0].astype(BF16) for n in _BIG], name="allgather_big")
    full = {}
    for n, g in zip(_BIG, gathered):
        r, w_ = W[n].shape[1:]
        full[n] = g.reshape(_NCHIP, r, w_) if _BIG[n] else g.reshape(_NCHIP * r, w_)

    small_local = [W[n][0].reshape(-1, W[n].shape[-1]) for n in _SMALL_SHARDED]
    small_shapes = [a.shape for a in small_local]
    spack = _pack(small_local, LANES, _FLAT_PART, _FLAT_PART, F32)
    sgath = _allgather_small(spack)[0::2]
    sfull = {n: _from_chip_shards(s, True)
             for n, s in zip(_SMALL_SHARDED, _unpack(sgath, small_shapes, LANES, _FLAT_PART, _FLAT_PART))}

    o_a = 4 * DNW
    o_xl = o_a + 4 * H
    o_mg = o_xl + 2 * LW
    wi_ = _from_chip_shards(full["w_in"], True)
    nj = LW // LC
    lru_cols = jnp.stack([wi_[:, o_xl:o_xl + LW].reshape(D, nj, LC), wi_[:, o_xl + LW:o_mg].reshape(D, nj, LC)],
                         axis=2).reshape(D, 2 * LW)
    wp = jnp.concatenate([wi_[:, :o_a], lru_cols, wi_[:, o_mg:], wi_[:, o_a:o_xl],
                          jnp.zeros((D, LANES - 4 * H), BF16)], axis=1)
    p_lru, p_mg, p_ab = 4 * DNW, 4 * DNW + 2 * LW, 4 * DNW + 2 * LW + 2 * D
    PW = p_ab + LANES

    MR = LANES
    cond = jnp.concatenate([c, c_ctx[None], jnp.zeros((MR - B - 1, D), F32)], axis=0)
    silu_rows = _rowwise(lambda a: (_silu(a),), [cond], [F32], name="cond_silu")[0]
    mod = _matmul(silu_rows, full["w_ada"], b_shards=(0, _NCHIP), name="ada_fwd") + b_ada
    mx = mod[:B].reshape(B, 6, D)
    mc = mod[B].reshape(6, D)
    zero = jnp.zeros((B, D), F32)
    tab = jnp.stack([jnp.stack([jnp.broadcast_to(mc[0], (B, D)), jnp.broadcast_to(mc[1], (B, D))] + [zero] * 6, axis=1),
                     jnp.stack([mx[:, 0], mx[:, 1]] + [zero] * 6, axis=1)], axis=1)
    vecs = jnp.stack([mx[:, 2], mx[:, 3], mx[:, 4], mx[:, 5]] + [zero] * 4, axis=1)
    gains = jnp.concatenate([g_post_mix, g_pre_ffn, g_post_ffn, jnp.zeros((5, D), F32)], axis=0)

    h = jnp.concatenate([ctx, x], axis=1)
    u = _premix_fwd(h, g_pre_mix, tab, nc=NC, tm=tm)
    p = _matmul(u, wp, name="in_fwd")
    dkw = dict(B=B, T=T, nc=NC, H=H, HD=HD)
    qkv = _dnprep_fwd(p, sfull["dn_conv"], **dkw)
    prm = jnp.concatenate([
        jnp.concatenate([dn_a_log.reshape(1, 2 * H), jnp.zeros((1, LANES - 2 * H), F32)], axis=1),
        jnp.concatenate([dn_dt_bias.reshape(1, 2 * H), jnp.zeros((1, LANES - 2 * H), F32)], axis=1),
        jnp.zeros((6, LANES), F32)], axis=0)
    gtm = _tile(B * T, 512, 16)
    gb = _gb_fwd(p, prm, rows=B * T, col0=p_ab, H=H, tm=gtm)
    y_dn, o_dn, *dn_res = _delta_fwd(qkv, gb, p, dn_onorm, **dkw)
    lv = jnp.concatenate([lru_conv_b, sfull["lru_b_rg"], sfull["lru_b_ig"], sfull["lru_lambda"], jnp.zeros((1, LW), F32)], axis=0)
    wr = _blockdiag(lru_w_rg[0], LC).astype(BF16)
    wi = _blockdiag(lru_w_ig[0], LC).astype(BF16)
    lkw = dict(B=B, T=T, nc=NC, LW=LW, col0=p_lru, C=LC)
    y_lru = _lru_fwd(p, sfull["lru_conv"], lv, wr, wi, **lkw)
    Ydn = _matmul(y_dn, full["w_branch_dn"], name="bdn_fwd")
    Ylru = _matmul(y_lru, full["w_branch_lru"], name="blru_fwd")
    mkw = dict(B=B, T=T, nc=NC, D=D, col0=p_mg, tm=tm)
    mixin = _merge_fwd(p, Ydn, Ylru, b_merge, **mkw)
    mix = _matmul(mixin, full["w_out"], name="out_fwd")
    h1, u2 = _post_fwd(x, mix, gains, vecs, tm=tm)
    F = _matmul(u2, full["w_up"], b_shards=(0, _NCHIP), name="up_fwd")
    w9 = sfull["ffn_dw"]
    ftc = _tile(DFF, 256)
    f = _ffn_act_fwd(F, w9, ffn_dw_b, B=B, N=N, DFF=DFF, tc=ftc)
    dn = _matmul(f, full["w_down"], name="down_fwd")
    ddn, dout, sums_f = _final(h1, dn, loss_target, gains, vecs, tm=tm)

    G = {}
    df = _matmul(ddn, full["w_down"], tb=True, name="down_bwd_x")
    G["w_down"] = _matmul(f, ddn, ta=True, name="down_bwd_w")
    dFg, dFv, dwb = _ffn_act_bwd(F, w9, ffn_dw_b, df, B=B, N=N, DFF=DFF, tc=ftc)
    hs = _NCHIP // 2
    du2 = _matmul(dFg, full["w_up"], tb=True, b_shards=(0, hs), name="up_bwd_xg")
    du2 = _matmul(dFv, full["w_up"], tb=True, b_shards=(hs, hs), add=du2, name="up_bwd_xv")
    G["w_up"] = jnp.concatenate([_matmul(u2, dFg, ta=True, out_shards=hs, name="up_bwd_wg"),
                                 _matmul(u2, dFv, ta=True, out_shards=hs, name="up_bwd_wv")], axis=0)
    dx1, dmix, sums_p = _post_bwd(x, mix, gains, vecs, dout, du2, tm=tm)
    dmixin = _matmul(dmix, full["w_out"], tb=True, name="out_bwd_x")
    G["w_out"] = _matmul(mixin, dmix, ta=True, name="out_bwd_w")
    dp = jnp.zeros((B * T, PW), BF16)
    dYdn, dYlru, dp, sums_m = _merge_bwd(p, Ydn, Ylru, b_merge, dmixin, dp, **mkw)
    dy_dn = _matmul(dYdn, full["w_branch_dn"], tb=True, name="bdn_bwd_x")
    G["w_branch_dn"] = _matmul(y_dn, dYdn, ta=True, name="bdn_bwd_w")
    dy_lru = _matmul(dYlru, full["w_branch_lru"], tb=True, name="blru_bwd_x")
    G["w_branch_lru"] = _matmul(y_lru, dYlru, ta=True, name="blru_bwd_w")
    dp, dcw_l, dlv, dwr, dwi = _lru_bwd(p, sfull["lru_conv"], lv, wr, wi, dy_lru, dp, **lkw)
    dqkv, dgb, dp, don = _delta_bwd(qkv, gb, p, dn_onorm, o_dn, dn_res, dy_dn, dp, **dkw)
    dp, dprm = _gb_bwd(p, prm, dgb, dp, rows=B * T, col0=p_ab, H=H, tm=gtm)
    dp, dcw_d = _dnprep_bwd(p, sfull["dn_conv"], dqkv, dp, **dkw)
    dU = _matmul(dp, wp, tb=True, name="in_bwd_x")
    dwp = _matmul(u, dp, ta=True, name="in_bwd_w")
    grad_x, sums_pm = _premix_bwd(h, g_pre_mix, tab, dU, dx1, nc=NC, tm=tm)
    dlru = dwp[:, p_lru:p_mg].reshape(D, nj, 2, LC)
    G["w_in"] = _to_chip_shards(jnp.concatenate([dwp[:, :o_a], dwp[:, p_ab:p_ab + 4 * H], dlru[:, :, 0].reshape(D, LW),
                                                 dlru[:, :, 1].reshape(D, LW), dwp[:, p_mg:p_ab]], axis=1), True)

    dmod_x = jnp.stack([sums_pm[:, 1, 0], sums_pm[:, 1, 1], sums_p[:, 0], sums_p[:, 1], sums_p[:, 2], sums_f[:, 0]],
                       axis=1).reshape(B, 6 * D)
    dmod_c = jnp.concatenate([sums_pm[:, 0, 0].sum(0), sums_pm[:, 0, 1].sum(0), jnp.zeros((4 * D,), F32)])[None]
    dmod = jnp.concatenate([dmod_x, dmod_c, jnp.zeros((MR - B - 1, 6 * D), F32)], axis=0)
    G["w_ada"] = _matmul(silu_rows, dmod, ta=True, out_shards=_NCHIP, name="ada_bwd_w")
    dsilu = _matmul(dmod, full["w_ada"], tb=True, b_shards=(0, _NCHIP), name="ada_bwd_x")

    g_small = {
        "c_ctx": dsilu[B] * _dsilu(c_ctx),
        "b_ada": dmod[:B + 1].sum(0)[None],
        "g_pre_mix": sums_pm[:, :, 2].sum((0, 1))[None],
        "g_post_mix": sums_p[:, 3].sum(0)[None],
        "g_pre_ffn": sums_p[:, 4].sum(0)[None],
        "g_post_ffn": sums_f[:, 1].sum(0)[None],
        "b_merge": sums_m[0:1],
        "dn_conv": dcw_d[0:4][None],
        "dn_a_log": dprm[0, :2 * H].reshape(1, 2, H),
        "dn_dt_bias": dprm[1, :2 * H].reshape(1, 2, H),
        "dn_onorm": don[:, 0].sum(0)[None],
        "lru_conv": dcw_l[0:4][None],
        "lru_conv_b": dlv[0:1],
        "lru_w_rg": _blockdiag_extract(dwr, LBD)[None],
        "lru_b_rg": dlv[1:3][None],
        "lru_w_ig": _blockdiag_extract(dwi, LBD)[None],
        "lru_b_ig": dlv[3:5][None],
        "lru_lambda": dlv[5:7][None],
        "ffn_dw": dwb[0:9].reshape(1, 3, 3, DFF),
        "ffn_dw_b": dwb[9:10],
    }
    small_names = tuple(n for n in _WEIGHTS if n not in _BIG)
    loss_part = sums_f[:, 2].sum().reshape(1)
    gs_list = [g_small[n] for n in small_names] + [loss_part]
    gs_shapes = [a.shape for a in gs_list]
    gpack = _pack(gs_list, LANES, _FLAT_PART, _FLAT_TOTAL, F32)
    gsum = _sum_lead(_allgather_small(gpack), name="small_sum", tm=512, mult=SUBLANES)
    gs_red = dict(zip(small_names + ("loss",), _unpack(gsum, gs_shapes, LANES, _FLAT_PART, _FLAT_TOTAL)))
    loss = gs_red["loss"][0]

    slabs = [G[n] if _BIG[n] else G[n].reshape(_NCHIP, G[n].shape[0] // _NCHIP, G[n].shape[1]) for n in _BIG]
    c_arr = c_i.astype(jnp.int32).reshape(1)
    from_sibling = _sibling_send_halves(slabs, name="rs_sibling")
    chip_sums = [_half_add(g, r, c_arr, name="rs_add_" + n) for n, g, r in zip(_BIG, slabs, from_sibling)]
    landed = _chip_exchange(chip_sums, name="chip_exchange")
    halves = [_sum_lead(e, name="rs_sum_" + n, tm=512) for n, e in zip(_BIG, landed)]
    sib_halves = _sibling_swap(halves, name="rs_gather")

    grads, deltas, new_m, new_v = {}, {}, {}, {}
    for n, own, sib in zip(_BIG, halves, sib_halves):
        shp = W[n].shape
        outs = _adamw_halves(W[n][0], own, sib, Mo[n][0], Vo[n][0], c_arr, name="adamw_" + n)
        grads[n], deltas[n], new_m[n], new_v[n] = (o.reshape(shp) for o in outs)
    for n in small_names:
        g = gs_red[n]
        if n in _SMALL_SHARDED:
            k = W[n].shape[-1]
            g = lax.dynamic_slice_in_dim(g, s_me * k, k, axis=g.ndim - 1)
        grads[n] = g.reshape(W[n].shape)
    sm_shapes = [W[n].shape for n in small_names]
    pk = lambda d: _pack([d[n] for n in small_names], LANES, _FLAT_PART, _FLAT_TOTAL, F32)
    d_, m_, v_ = _adamw(pk(W), pk(grads), pk(Mo), pk(Vo), name="adamw_small")
    for dst, pool_ in ((deltas, d_), (new_m, m_), (new_v, v_)):
        dst.update(zip(small_names, _unpack(pool_, sm_shapes, LANES, _FLAT_PART, _FLAT_TOTAL)))
    return (loss, grad_x, *[grads[n] for n in _WEIGHTS], *[deltas[n] for n in _WEIGHTS],
            *[new_m[n] for n in _WEIGHTS], *[new_v[n] for n in _WEIGHTS])
```

```python
import functools
import math

import jax
import jax.numpy as jnp
from jax import lax
from jax.experimental import pallas as pl
from jax.experimental.pallas import tpu as pltpu

F32 = jnp.float32
BF16 = jnp.bfloat16
EPS = 1e-6
GRID_W = 64
CHUNK = 128
LRU_C = 8.0
LANES = 128
SUBLANES = 8
VMEM_LIMIT = 56 * 1024 * 1024
ADAM_LR, ADAM_B1, ADAM_B2, ADAM_EPS, ADAM_WD, ADAM_STEP = 0.001, 0.9, 0.999, 1e-08, 0.01, 10
MESH = pl.DeviceIdType.MESH


def _tile(n, target, mult=LANES):
    best = None
    for t in range(mult, min(n, target) + 1, mult):
        if n % t == 0:
            best = t
    return best if best is not None else n


def _params(sem=None, **kw):
    return pltpu.CompilerParams(dimension_semantics=sem, vmem_limit_bytes=VMEM_LIMIT, **kw)


def _sigmoid(x):
    return 1.0 / (1.0 + jnp.exp(-x))


def _silu(x):
    return x * _sigmoid(x)


def _softplus(x):
    return jnp.maximum(x, 0.0) + jnp.log(1.0 + jnp.exp(-jnp.abs(x)))


def _gelu(x):
    return 0.5 * x * (1.0 + jnp.tanh(math.sqrt(2.0 / math.pi) * (x + 0.044715 * x * x * x)))


def _rmsn(u, gain):
    return u * lax.rsqrt(jnp.mean(u * u, axis=-1, keepdims=True) + EPS) * gain


_MM_VMEM = 40 * 1024 * 1024


def _matmul(a, b, *, ta=False, tb=False, add=None, b_shards=None, out_shards=None, out_dtype=F32, name,
            tm=1024, tn=2048, tk=1024):
    (K, M) = a.shape if ta else a.shape[::-1]
    if b_shards is not None:
        s0, ns = b_shards
        bsh = (b.shape[1], ns * b.shape[2])
        nsh = b.shape[2]
    else:
        bsh = b.shape
    N = bsh[0] if tb else bsh[1]
    assert (bsh[1] if tb else bsh[0]) == K, (a.shape, b.shape, ta, tb)
    tm = _tile(M, tm)
    tk = _tile(nsh if (b_shards is not None and tb) else K, tk)
    nlim = nsh if (b_shards is not None and not tb) else (N // out_shards if out_shards else N)
    osz = jnp.dtype(out_dtype).itemsize + (4 if add is not None else 0)
    while True:
        tn_ = _tile(nlim, tn)
        need = 2 * (tm * tk * a.dtype.itemsize + tk * tn_ * b.dtype.itemsize + tm * tn_ * osz) + 4 * tm * tn_
        if need <= _MM_VMEM or tn <= LANES:
            break
        tn //= 2
    tn = tn_
    nk = K // tk
    dims = (((0 if ta else 1,), (1 if tb else 0,)), ((), ()))

    def body(a_ref, b_ref, *rest):
        (c_ref, o_ref, acc_ref) = rest if add is not None else (None, *rest)
        k = pl.program_id(2)

        @pl.when(k == 0)
        def _():
            acc_ref[...] = jnp.zeros_like(acc_ref) if c_ref is None else c_ref[...]

        bv = b_ref[0] if b_shards is not None else b_ref[...]
        acc_ref[...] += lax.dot_general(a_ref[...].astype(BF16), bv.astype(BF16), dims, preferred_element_type=F32)

        @pl.when(k == nk - 1)
        def _():
            if out_shards:
                o_ref[0] = acc_ref[...].astype(out_dtype)
            else:
                o_ref[...] = acc_ref[...].astype(out_dtype)

    a_spec = pl.BlockSpec((tk, tm), lambda i, j, k: (k, i)) if ta else pl.BlockSpec((tm, tk), lambda i, j, k: (i, k))
    if b_shards is None:
        b_spec = pl.BlockSpec((tn, tk), lambda i, j, k: (j, k)) if tb else pl.BlockSpec((tk, tn), lambda i, j, k: (k, j))
    elif tb:
        per = nsh // tk
        b_spec = pl.BlockSpec((1, tn, tk), lambda i, j, k: (s0 + k // per, j, k % per))
    else:
        per = nsh // tn
        b_spec = pl.BlockSpec((1, tk, tn), lambda i, j, k: (s0 + j // per, k, j % per))
    o_spec = pl.BlockSpec((tm, tn), lambda i, j, k: (i, j))
    if out_shards:
        oper = N // out_shards // tn
        out_spec = pl.BlockSpec((1, tm, tn), lambda i, j, k: (j // oper, i, j % oper))
        out_shape = jax.ShapeDtypeStruct((out_shards, M, N // out_shards), out_dtype)
    else:
        out_spec, out_shape = o_spec, jax.ShapeDtypeStruct((M, N), out_dtype)
    return pl.pallas_call(
        body, name=name, grid=(M // tm, N // tn, nk),
        in_specs=[a_spec, b_spec] + ([o_spec] if add is not None else []),
        out_specs=out_spec, out_shape=out_shape,
        scratch_shapes=[pltpu.VMEM((tm, tn), F32)],
        compiler_params=_params(("parallel", "parallel", "arbitrary")),
    )(*((a, b) + ((add,) if add is not None else ())))


def _premix_math(h, gain, shift, scale):
    return _rmsn(h, gain) * (1.0 + scale) + shift


def _premix_fwd(h, gain, tab, *, nc, tm):
    B, T, D = h.shape
    nt, nct = T // tm, nc // tm

    def body(h_ref, g_ref, tab_ref, u_ref):
        tabv = tab_ref[0, 0]
        u_ref[...] = _premix_math(h_ref[0], g_ref[...], tabv[0:1], tabv[1:2]).astype(BF16)

    return pl.pallas_call(
        body, name="premix_fwd", grid=(B, nt),
        in_specs=[pl.BlockSpec((1, tm, D), lambda b, t: (b, t, 0)),
                  pl.BlockSpec((1, D), lambda b, t: (0, 0)),
                  pl.BlockSpec((1, 1, 8, D), lambda b, t: (b, jnp.where(t < nct, 0, 1), 0, 0))],
        out_specs=pl.BlockSpec((tm, D), lambda b, t: (b * nt + t, 0)),
        out_shape=jax.ShapeDtypeStruct((B * T, D), BF16),
        compiler_params=_params(("parallel", "parallel")),
    )(h, gain, tab)


def _premix_bwd(h, gain, tab, du, dres, *, nc, tm):
    B, T, D = h.shape
    nt, nct = T // tm, nc // tm
    N = T - nc

    def body(h_ref, g_ref, tab_ref, du_ref, dres_ref, dx_ref, sums_ref):
        t = pl.program_id(1)
        tabv = tab_ref[0, 0]
        _, vjp = jax.vjp(_premix_math, h_ref[0], g_ref[...], tabv[0:1], tabv[1:2])
        dh, dgain, dshift, dscale = vjp(du_ref[...].astype(F32))

        @pl.when((t == 0) | (t == nct))
        def _():
            sums_ref[...] = jnp.zeros_like(sums_ref)

        sums_ref[0, 0, 0:1, :] += dshift
        sums_ref[0, 0, 1:2, :] += dscale
        sums_ref[0, 0, 2:3, :] += dgain

        @pl.when(t >= nct)
        def _():
            dx_ref[0] = dres_ref[...] + dh

    lat = lambda b, t: jnp.maximum(t - nct, 0)
    return pl.pallas_call(
        body, name="premix_bwd", grid=(B, nt),
        in_specs=[pl.BlockSpec((1, tm, D), lambda b, t: (b, t, 0)),
                  pl.BlockSpec((1, D), lambda b, t: (0, 0)),
                  pl.BlockSpec((1, 1, 8, D), lambda b, t: (b, jnp.where(t < nct, 0, 1), 0, 0)),
                  pl.BlockSpec((tm, D), lambda b, t: (b * nt + t, 0)),
                  pl.BlockSpec((tm, D), lambda b, t: (b * (nt - nct) + lat(b, t), 0))],
        out_specs=[pl.BlockSpec((1, tm, D), lambda b, t: (b, lat(b, t), 0)),
                   pl.BlockSpec((1, 1, 8, D), lambda b, t: (b, jnp.where(t < nct, 0, 1), 0, 0))],
        out_shape=[jax.ShapeDtypeStruct((B, N, D), F32), jax.ShapeDtypeStruct((B, 2, 8, D), F32)],
        compiler_params=_params(("parallel", "arbitrary")),
    )(h, gain, tab, du, dres)


def _merge_math(mgd, mgl, yd, yl, bd, bl):
    return _sigmoid(mgd + bd) * yd + _sigmoid(mgl + bl) * yl


def _merge_fwd(p, ydn, ylru, b_merge, *, B, T, nc, D, col0, tm):
    N = T - nc
    ntl, nt, nct, cb = N // tm, T // tm, nc // tm, col0 // D

    def body(mgd_ref, mgl_ref, yd_ref, yl_ref, bm_ref, o_ref):
        o_ref[...] = _merge_math(mgd_ref[...], mgl_ref[...], yd_ref[...], yl_ref[...],
                                 bm_ref[:, 0:D], bm_ref[:, D:2 * D]).astype(BF16)

    prow = lambda b, t: b * nt + nct + t
    return pl.pallas_call(
        body, name="merge_fwd", grid=(B, ntl),
        in_specs=[pl.BlockSpec((tm, D), lambda b, t: (prow(b, t), cb)),
                  pl.BlockSpec((tm, D), lambda b, t: (prow(b, t), cb + 1)),
                  pl.BlockSpec((tm, D), lambda b, t: (b * ntl + t, 0)),
                  pl.BlockSpec((tm, D), lambda b, t: (b * ntl + t, 0)),
                  pl.BlockSpec((1, 2 * D), lambda b, t: (0, 0))],
        out_specs=pl.BlockSpec((tm, D), lambda b, t: (b * ntl + t, 0)),
        out_shape=jax.ShapeDtypeStruct((B * N, D), BF16),
        compiler_params=_params(("parallel", "parallel")),
    )(p, p, ydn, ylru, b_merge)


def _merge_bwd(p, ydn, ylru, b_merge, dmix, dp, *, B, T, nc, D, col0, tm):
    N = T - nc
    ntl, nt, nct, cb = N // tm, T // tm, nc // tm, col0 // D
    assert col0 % (2 * D) == 0

    def body(mgd_ref, mgl_ref, yd_ref, yl_ref, bm_ref, dm_ref, dp_any, dyd_ref, dyl_ref, dp_ref, sums_ref):
        _, vjp = jax.vjp(_merge_math, mgd_ref[...], mgl_ref[...], yd_ref[...], yl_ref[...],
                         bm_ref[:, 0:D], bm_ref[:, D:2 * D])
        dmgd, dmgl, dyd, dyl, dbd, dbl = vjp(dm_ref[...])
        dyd_ref[...] = dyd.astype(BF16)
        dyl_ref[...] = dyl.astype(BF16)
        dp_ref[:, 0:D] = dmgd.astype(BF16)
        dp_ref[:, D:2 * D] = dmgl.astype(BF16)

        @pl.when((pl.program_id(0) == 0) & (pl.program_id(1) == 0))
        def _():
            sums_ref[...] = jnp.zeros_like(sums_ref)

        sums_ref[0:1, 0:D] += dbd
        sums_ref[0:1, D:2 * D] += dbl

    prow = lambda b, t: b * nt + nct + t
    row = pl.BlockSpec((tm, D), lambda b, t: (b * ntl + t, 0))
    return pl.pallas_call(
        body, name="merge_bwd", grid=(B, ntl),
        in_specs=[pl.BlockSpec((tm, D), lambda b, t: (prow(b, t), cb)),
                  pl.BlockSpec((tm, D), lambda b, t: (prow(b, t), cb + 1)),
                  row, row, pl.BlockSpec((1, 2 * D), lambda b, t: (0, 0)), row,
                  pl.BlockSpec(memory_space=pl.ANY)],
        out_specs=[row, row,
                   pl.BlockSpec((tm, 2 * D), lambda b, t: (prow(b, t), cb // 2)),
                   pl.BlockSpec((8, 2 * D), lambda b, t: (0, 0))],
        out_shape=[jax.ShapeDtypeStruct((B * N, D), BF16), jax.ShapeDtypeStruct((B * N, D), BF16),
                   jax.ShapeDtypeStruct(dp.shape, dp.dtype), jax.ShapeDtypeStruct((8, 2 * D), F32)],
        input_output_aliases={6: 2},
        compiler_params=_params(("arbitrary", "arbitrary")),
    )(p, p, ydn, ylru, b_merge, dmix, dp)


def _post_math(x, mix, g1, gate, g2, sh, sc):
    h1 = x + _rmsn(mix, g1) * gate
    return h1, _rmsn(h1, g2) * (1.0 + sc) + sh


def _post_fwd(x, mix, gains, vecs, *, tm):
    B, N, D = x.shape
    ntl = N // tm

    def body(x_ref, mix_ref, g_ref, v_ref, h1_ref, u2_ref):
        v = v_ref[0]
        h1, u2 = _post_math(x_ref[0], mix_ref[...], g_ref[0:1], v[0:1], g_ref[1:2], v[1:2], v[2:3])
        h1_ref[...] = h1
        u2_ref[...] = u2.astype(BF16)

    row = pl.BlockSpec((tm, D), lambda b, t: (b * ntl + t, 0))
    return pl.pallas_call(
        body, name="post_fwd", grid=(B, ntl),
        in_specs=[pl.BlockSpec((1, tm, D), lambda b, t: (b, t, 0)), row,
                  pl.BlockSpec((8, D), lambda b, t: (0, 0)), pl.BlockSpec((1, 8, D), lambda b, t: (b, 0, 0))],
        out_specs=[row, row],
        out_shape=[jax.ShapeDtypeStruct((B * N, D), F32), jax.ShapeDtypeStruct((B * N, D), BF16)],
        compiler_params=_params(("parallel", "parallel")),
    )(x, mix, gains, vecs)


def _post_bwd(x, mix, gains, vecs, dh1, du2, *, tm):
    B, N, D = x.shape
    ntl = N // tm

    def body(x_ref, mix_ref, g_ref, v_ref, dh1_ref, du2_ref, dx_ref, dmix_ref, sums_ref):
        v = v_ref[0]
        _, vjp = jax.vjp(_post_math, x_ref[0], mix_ref[...], g_ref[0:1], v[0:1], g_ref[1:2], v[1:2], v[2:3])
        dx, dmix, dg1, dgate, dg2, dsh, dsc = vjp((dh1_ref[...], du2_ref[...]))
        dx_ref[...] = dx
        dmix_ref[...] = dmix.astype(BF16)

        @pl.when(pl.program_id(1) == 0)
        def _():
            sums_ref[...] = jnp.zeros_like(sums_ref)

        sums_ref[0, 0:1, :] += dgate
        sums_ref[0, 1:2, :] += dsh
        sums_ref[0, 2:3, :] += dsc
        sums_ref[0, 3:4, :] += dg1
        sums_ref[0, 4:5, :] += dg2

    row = pl.BlockSpec((tm, D), lambda b, t: (b * ntl + t, 0))
    return pl.pallas_call(
        body, name="post_bwd", grid=(B, ntl),
        in_specs=[pl.BlockSpec((1, tm, D), lambda b, t: (b, t, 0)), row,
                  pl.BlockSpec((8, D), lambda b, t: (0, 0)), pl.BlockSpec((1, 8, D), lambda b, t: (b, 0, 0)), row, row],
        out_specs=[row, row, pl.BlockSpec((1, 8, D), lambda b, t: (b, 0, 0))],
        out_shape=[jax.ShapeDtypeStruct((B * N, D), F32), jax.ShapeDtypeStruct((B * N, D), BF16),
                   jax.ShapeDtypeStruct((B, 8, D), F32)],
        compiler_params=_params(("parallel", "arbitrary")),
    )(x, mix, gains, vecs, dh1, du2)


def _final_math(dn, g4, gate5):
    return _rmsn(dn, g4) * gate5


def _final(h1, dn, target, gains, vecs, *, tm):
    B, N, D = target.shape
    ntl = N // tm

    def body(h1_ref, dn_ref, t_ref, g_ref, v_ref, ddn_ref, dout_ref, sums_ref):
        v = v_ref[0]
        y, vjp = jax.vjp(_final_math, dn_ref[...], g_ref[2:3], v[3:4])
        err = h1_ref[...] + y - t_ref[0]
        dout = err * (1.0 / D)
        ddn, dg4, dgate5 = vjp(dout)
        ddn_ref[...] = ddn.astype(BF16)
        dout_ref[...] = dout

        @pl.when(pl.program_id(1) == 0)
        def _():
            sums_ref[...] = jnp.zeros_like(sums_ref)

        sums_ref[0, 0:1, :] += dgate5
        sums_ref[0, 1:2, :] += dg4
        sums_ref[0, 2:3, :] += jnp.sum(err * err, axis=0, keepdims=True) * (0.5 / D)

    row = pl.BlockSpec((tm, D), lambda b, t: (b * ntl + t, 0))
    return pl.pallas_call(
        body, name="final", grid=(B, ntl),
        in_specs=[row, row, pl.BlockSpec((1, tm, D), lambda b, t: (b, t, 0)),
                  pl.BlockSpec((8, D), lambda b, t: (0, 0)), pl.BlockSpec((1, 8, D), lambda b, t: (b, 0, 0))],
        out_specs=[row, row, pl.BlockSpec((1, 8, D), lambda b, t: (b, 0, 0))],
        out_shape=[jax.ShapeDtypeStruct((B * N, D), BF16), jax.ShapeDtypeStruct((B * N, D), F32),
                   jax.ShapeDtypeStruct((B, 8, D), F32)],
        compiler_params=_params(("parallel", "arbitrary")),
    )(h1, dn, target, gains, vecs)


def _shift(x, s):
    s = s % x.shape[0]
    return x if s == 0 else pltpu.roll(x, s, 0)


def _seg_taps(T, nc, width, pad_left):
    t = lax.broadcasted_iota(jnp.int32, (T, 1), 0)
    pos = jnp.where(t < nc, t, t - nc)
    seg = jnp.where(t < nc, nc, T - nc)
    taps = []
    for k in range(width):
        src = pos + (k - pad_left)
        taps.append((pad_left - k, (src >= 0) & (src < seg)))
    return taps


def _grid_taps(N):
    t = lax.broadcasted_iota(jnp.int32, (N, 1), 0)
    wcol = t % GRID_W
    taps = []
    for dr in (-1, 0, 1):
        for dw in (-1, 0, 1):
            off = dr * GRID_W + dw
            ok = (wcol + dw >= 0) & (wcol + dw < GRID_W) & (t + dr * GRID_W >= 0) & (t + dr * GRID_W < N)
            taps.append((-off, ok))
    return taps


def _conv_fwd(x, w, taps):
    y = jnp.zeros_like(x)
    for k, (s, m) in enumerate(taps):
        y = y + w[k:k + 1] * jnp.where(m, _shift(x, s), 0.0)
    return y


def _conv_bwd(x, w, taps, dy):
    dx = jnp.zeros_like(x)
    dws = []
    for k, (s, m) in enumerate(taps):
        dym = jnp.where(m, dy, 0.0)
        dx = dx + w[k:k + 1] * _shift(dym, -s)
        dws.append(jnp.sum(dym * _shift(x, s), axis=0, keepdims=True))
    return dx, jnp.concatenate(dws, axis=0)


def _ffn_act_fwd(F, w9, bias, *, B, N, DFF, tc):
    nj = DFF // tc

    def body(fg_ref, fv_ref, w_ref, b_ref, o_ref):
        fg = _conv_fwd(fg_ref[...], w_ref[...], _grid_taps(N)) + b_ref[...]
        o_ref[...] = (_gelu(fg) * fv_ref[...]).astype(BF16)

    return pl.pallas_call(
        body, name="ffn_act_fwd", grid=(B, nj),
        in_specs=[pl.BlockSpec((N, tc), lambda b, j: (b, j)), pl.BlockSpec((N, tc), lambda b, j: (b, nj + j)),
                  pl.BlockSpec((9, tc), lambda b, j: (0, j)), pl.BlockSpec((1, tc), lambda b, j: (0, j))],
        out_specs=pl.BlockSpec((N, tc), lambda b, j: (b, j)),
        out_shape=jax.ShapeDtypeStruct((B * N, DFF), BF16),
        compiler_params=_params(("parallel", "parallel")),
    )(F, F, w9, bias)


def _ffn_act_bwd(F, w9, bias, df, *, B, N, DFF, tc):
    nj = DFF // tc

    def body(fg_ref, fv_ref, w_ref, b_ref, df_ref, dfg_ref, dfv_ref, dwb_ref):
        taps = _grid_taps(N)
        x = fg_ref[...]
        fg, vjp = jax.vjp(lambda a: _gelu(a), _conv_fwd(x, w_ref[...], taps) + b_ref[...])
        dfl = df_ref[...]
        dfv_ref[...] = (dfl * fg).astype(BF16)
        (dpre,) = vjp(dfl * fv_ref[...])
        dx, dw = _conv_bwd(x, w_ref[...], taps, dpre)
        dfg_ref[...] = dx.astype(BF16)

        @pl.when(pl.program_id(1) == 0)
        def _():
            dwb_ref[...] = jnp.zeros_like(dwb_ref)

        dwb_ref[0:9, :] += dw
        dwb_ref[9:10, :] += jnp.sum(dpre, axis=0, keepdims=True)

    col = pl.BlockSpec((N, tc), lambda j, b: (b, j))
    return pl.pallas_call(
        body, name="ffn_act_bwd", grid=(nj, B),
        in_specs=[col, pl.BlockSpec((N, tc), lambda j, b: (b, nj + j)),
                  pl.BlockSpec((9, tc), lambda j, b: (0, j)), pl.BlockSpec((1, tc), lambda j, b: (0, j)), col],
        out_specs=[col, col, pl.BlockSpec((16, tc), lambda j, b: (0, j))],
        out_shape=[jax.ShapeDtypeStruct((B * N, DFF), BF16), jax.ShapeDtypeStruct((B * N, DFF), BF16),
                   jax.ShapeDtypeStruct((16, DFF), F32)],
        compiler_params=_params(("parallel", "arbitrary")),
    )(F, F, w9, bias, df)


def _dnprep_math(y, is_qk, scale):
    s = _silu(y)
    n = s * lax.rsqrt(jnp.sum(s * s, axis=-1, keepdims=True) + EPS) * scale
    return jnp.where(is_qk, n, s)


def _dnprep_fwd(p, cw, *, B, T, nc, H, HD):
    def body(x_ref, w_ref, o_ref):
        j = pl.program_id(1)
        y = _conv_fwd(x_ref[...], w_ref[...], _seg_taps(T, nc, 4, 2))
        o_ref[...] = _dnprep_math(y, j < 2 * H, jnp.where(j < H, HD ** -0.5, 1.0))

    return pl.pallas_call(
        body, name="dnprep_fwd", grid=(B, 3 * H),
        in_specs=[pl.BlockSpec((T, HD), lambda b, j: (b, j)), pl.BlockSpec((4, HD), lambda b, j: (0, j))],
        out_specs=pl.BlockSpec((T, HD), lambda b, j: (b, j)),
        out_shape=jax.ShapeDtypeStruct((B * T, 3 * H * HD), F32),
        compiler_params=_params(("parallel", "parallel")),
    )(p, cw)


def _dnprep_bwd(p, cw, dqkv, dp, *, B, T, nc, H, HD):
    def body(x_ref, w_ref, d_ref, dp_any, dp_ref, dcw_ref):
        j = pl.program_id(0)
        taps = _seg_taps(T, nc, 4, 2)
        x = x_ref[...]
        y = _conv_fwd(x, w_ref[...], taps)
        is_qk, scale = j < 2 * H, jnp.where(j < H, HD ** -0.5, 1.0)
        _, vjp = jax.vjp(lambda a: _dnprep_math(a, is_qk, scale), y)
        (dy,) = vjp(d_ref[0])
        dx, dw = _conv_bwd(x, w_ref[...], taps, dy)
        dp_ref[...] = dx.astype(BF16)

        @pl.when(pl.program_id(1) == 0)
        def _():
            dcw_ref[...] = jnp.zeros_like(dcw_ref)

        dcw_ref[0:4, :] += dw

    col = pl.BlockSpec((T, HD), lambda j, b: (b, j))
    return pl.pallas_call(
        body, name="dnprep_bwd", grid=(3 * H, B),
        in_specs=[col, pl.BlockSpec((4, HD), lambda j, b: (0, j)),
                  pl.BlockSpec((1, T, HD), lambda j, b: (j // H, b, j % H)), pl.BlockSpec(memory_space=pl.ANY)],
        out_specs=[col, pl.BlockSpec((8, HD), lambda j, b: (0, j))],
        out_shape=[jax.ShapeDtypeStruct(dp.shape, dp.dtype), jax.ShapeDtypeStruct((8, 3 * H * HD), F32)],
        input_output_aliases={3: 0},
        compiler_params=_params(("parallel", "arbitrary")),
    )(p, cw, dqkv, dp)


def _gb_math(ab, alog, dtb, H):
    lane = lax.broadcasted_iota(jnp.int32, ab.shape, 1)
    g = -jnp.exp(alog) * _softplus(ab + dtb)
    return jnp.where(lane < 2 * H, g, jnp.where(lane < 4 * H, _sigmoid(ab), 0.0))


def _gb_fwd(p, prm, *, rows, col0, H, tm):
    def body(x_ref, prm_ref, o_ref):
        o_ref[...] = _gb_math(x_ref[...], prm_ref[0:1], prm_ref[1:2], H)

    return pl.pallas_call(
        body, name="gb_fwd", grid=(rows // tm,),
        in_specs=[pl.BlockSpec((tm, LANES), lambda t: (t, col0 // LANES)), pl.BlockSpec((8, LANES), lambda t: (0, 0))],
        out_specs=pl.BlockSpec((tm, LANES), lambda t: (t, 0)),
        out_shape=jax.ShapeDtypeStruct((rows, LANES), F32),
        compiler_params=_params(("parallel",)),
    )(p, prm)


def _gb_bwd(p, prm, dgb, dp, *, rows, col0, H, tm):
    def body(x_ref, prm_ref, d_ref, dp_any, dp_ref, dprm_ref):
        _, vjp = jax.vjp(lambda a, b, c: _gb_math(a, b, c, H), x_ref[...], prm_ref[0:1], prm_ref[1:2])
        dab, dalog, ddtb = vjp(d_ref[...])
        dp_ref[...] = dab.astype(BF16)

        @pl.when(pl.program_id(0) == 0)
        def _():
            dprm_ref[...] = jnp.zeros_like(dprm_ref)

        dprm_ref[0:1, :] += dalog
        dprm_ref[1:2, :] += ddtb

    blk = pl.BlockSpec((tm, LANES), lambda t: (t, col0 // LANES))
    return pl.pallas_call(
        body, name="gb_bwd", grid=(rows // tm,),
        in_specs=[blk, pl.BlockSpec((8, LANES), lambda t: (0, 0)), pl.BlockSpec((tm, LANES), lambda t: (t, 0)),
                  pl.BlockSpec(memory_space=pl.ANY)],
        out_specs=[blk, pl.BlockSpec((8, LANES), lambda t: (0, 0))],
        out_shape=[jax.ShapeDtypeStruct(dp.shape, dp.dtype), jax.ShapeDtypeStruct((8, LANES), F32)],
        input_output_aliases={3: 0},
        compiler_params=_params(("arbitrary",)),
    )(p, prm, dgb, dp)


def _lru_scans(scans):
    C = scans[0][0].shape[1]
    row = lax.broadcasted_iota(jnp.int32, (SUBLANES, C), 0)
    carries = tuple(jnp.zeros((1, C), F32) for _ in scans)
    for si in range(len(scans[0][4])):
        nb = scans[0][4][si][1] // SUBLANES
        assert all(sc[4][si][1] // SUBLANES == nb for sc in scans)

        def blk(i, carries, si=si, nb=nb):
            out = []
            for (a_ref, b_ref, h_ref, hp_ref, segs), carry in zip(scans, carries):
                start, _, reverse = segs[si]
                r0 = pl.multiple_of(start + (nb - 1 - i if reverse else i) * SUBLANES, SUBLANES)
                A = a_ref[pl.ds(r0, SUBLANES), :]
                Bv = b_ref[pl.ds(r0, SUBLANES), :]
                for s in (1, 2, 4):
                    sh = SUBLANES - s if reverse else s
                    m = (row < SUBLANES - s) if reverse else (row >= s)
                    Bv = jnp.where(m, A * pltpu.roll(Bv, sh, 0) + Bv, Bv)
                    A = jnp.where(m, A * pltpu.roll(A, sh, 0), A)
                Hv = Bv + A * carry
                h_ref[pl.ds(r0, SUBLANES), :] = Hv
                if hp_ref is not None:
                    if reverse:
                        hp = jnp.where(row < SUBLANES - 1, pltpu.roll(Hv, SUBLANES - 1, 0), carry)
                    else:
                        hp = jnp.where(row >= 1, pltpu.roll(Hv, 1, 0), carry)
                    hp_ref[pl.ds(r0, SUBLANES), :] = hp
                out.append(Hv[0:1] if reverse else Hv[SUBLANES - 1:SUBLANES])
            return tuple(out)

        carries = lax.fori_loop(0, nb, blk, carries)


def _lru_orders(T, nc, d):
    N = T - nc
    if d == 0:
        return [(0, nc, False), (nc, N, False)], [(nc, N, True), (0, nc, True)]
    return [(0, nc, True), (nc, N, True)], [(nc, N, False), (0, nc, False)]


def _bdot(a, b, dims=(((1,), (0,)), ((), ()))):
    return lax.dot_general(a.astype(BF16), b.astype(BF16), dims, preferred_element_type=F32)


_NT = (((1,), (1,)), ((), ()))
_TN = (((0,), (0,)), ((), ()))


def _blockdiag(w, C):
    nd, nb, bd, _ = w.shape
    per = C // bd
    out = jnp.einsum('dnpij,pq->dnpiqj', w.reshape(nd, nb // per, per, bd, bd), jnp.eye(per, dtype=w.dtype))
    return out.reshape(nd, nb // per, C, C)


def _blockdiag_extract(dw, bd):
    nd, nj, C, _ = dw.shape
    per = C // bd
    out = jnp.einsum('dnpiqj,pq->dnpij', dw.reshape(nd, nj, per, bd, per, bd), jnp.eye(per, dtype=dw.dtype))
    return out.reshape(nd, nj * per, bd, bd)


def _lru_fwd(p, cw, lv, wr, wi, *, B, T, nc, LW, col0, C):
    N = T - nc
    nj = LW // C

    def body(x_ref, cw_ref, lv_ref, wr_ref, wi_ref, o_ref, a_s, b_s, h_s):
        lv_ = lv_ref[...]
        xc = _conv_fwd(x_ref[:, 0:C], cw_ref[...], _seg_taps(T, nc, 4, 2)) + lv_[0:1]
        for d in (0, 1):
            r = _sigmoid(_bdot(xc, wr_ref[d, 0]) + lv_[1 + d:2 + d])
            i = _sigmoid(_bdot(xc, wi_ref[d, 0]) + lv_[3 + d:4 + d])
            la = -LRU_C * r * _softplus(-lv_[5 + d:6 + d])
            a_s[d] = jnp.exp(la)
            b_s[d] = jnp.sqrt(1.0 - jnp.exp(2.0 * la)) * i * xc
        _lru_scans([(a_s.at[d], b_s.at[d], h_s.at[d], None, _lru_orders(T, nc, d)[0]) for d in (0, 1)])
        o_ref[...] = ((h_s[0, nc:, :] + h_s[1, nc:, :]) * _gelu(x_ref[nc:, C:2 * C])).astype(BF16)

    return pl.pallas_call(
        body, name="lru_fwd", grid=(B, nj),
        in_specs=[pl.BlockSpec((T, 2 * C), lambda b, j: (b, col0 // (2 * C) + j)),
                  pl.BlockSpec((4, C), lambda b, j: (0, j)), pl.BlockSpec((8, C), lambda b, j: (0, j)),
                  pl.BlockSpec((2, 1, C, C), lambda b, j: (0, j, 0, 0)), pl.BlockSpec((2, 1, C, C), lambda b, j: (0, j, 0, 0))],
        out_specs=pl.BlockSpec((N, C), lambda b, j: (b, j)),
        out_shape=jax.ShapeDtypeStruct((B * N, LW), BF16),
        scratch_shapes=[pltpu.VMEM((2, T, C), F32)] * 3,
        compiler_params=_params(("parallel", "parallel")),
    )(p, cw, lv, wr, wi)


def _lru_bwd(p, cw, lv, wr, wi, dy, dp, *, B, T, nc, LW, col0, C):
    N = T - nc
    nj = LW // C

    def body(x_ref, cw_ref, lv_ref, wr_ref, wi_ref, dy_ref, dp_any, dp_ref, dcw_ref, dlv_ref, dwr_ref, dwi_ref,
             a_s, b_s, h_s, hp_s, mu_s, mup_s, dh_s, dxc_s):
        taps = _seg_taps(T, nc, 4, 2)
        lv_ = lv_ref[...]
        xl = x_ref[:, 0:C]
        xc = _conv_fwd(xl, cw_ref[...], taps) + lv_[0:1]
        gel, gelu_vjp = jax.vjp(_gelu, x_ref[nc:, C:2 * C])
        dh_s[0:nc, :] = jnp.zeros((nc, C), F32)
        dh_s[nc:, :] = dy_ref[...] * gel
        dxc_s[...] = jnp.zeros_like(dxc_s)

        @pl.when(pl.program_id(1) == 0)
        def _():
            dcw_ref[...] = jnp.zeros_like(dcw_ref)
            dlv_ref[...] = jnp.zeros_like(dlv_ref)
            dwr_ref[...] = jnp.zeros_like(dwr_ref)
            dwi_ref[...] = jnp.zeros_like(dwi_ref)

        def gates(d):
            lam = lv_[5 + d:6 + d]
            r = _sigmoid(_bdot(xc, wr_ref[d, 0]) + lv_[1 + d:2 + d])
            i = _sigmoid(_bdot(xc, wi_ref[d, 0]) + lv_[3 + d:4 + d])
            sp = _softplus(-lam)
            la = -LRU_C * r * sp
            e2 = jnp.exp(2.0 * la)
            return lam, r, i, sp, la, e2, jnp.sqrt(1.0 - e2)

        for d in (0, 1):
            _, _, i, _, la, _, mult = gates(d)
            a_s[d] = jnp.exp(la)
            b_s[d] = mult * i * xc
        _lru_scans([(a_s.at[d], b_s.at[d], h_s.at[d], hp_s.at[d], _lru_orders(T, nc, d)[0]) for d in (0, 1)])
        for d in (0, 1):
            b_s[d] = a_s[d] * dh_s[...]
        _lru_scans([(a_s.at[d], b_s.at[d], mu_s.at[d], mup_s.at[d], _lru_orders(T, nc, d)[1]) for d in (0, 1)])

        for d in (0, 1):
            lam, r, i, sp, la, e2, mult = gates(d)
            a = a_s[d]
            dinp = dh_s[...] + mup_s[d]
            da = dinp * hp_s[d]
            dmult = dinp * i * xc
            di = dinp * mult * xc
            dla = da * a - dmult * e2 / mult
            dpre_r = (dla * (-LRU_C * sp)) * r * (1.0 - r)
            dpre_i = di * i * (1.0 - i)
            dsp = jnp.sum(dla * (-LRU_C * r), axis=0, keepdims=True)
            dxc_s[...] += dinp * mult * i + _bdot(dpre_r, wr_ref[d, 0], _NT) + _bdot(dpre_i, wi_ref[d, 0], _NT)
            dwr_ref[d, 0] += _bdot(xc, dpre_r, _TN)
            dwi_ref[d, 0] += _bdot(xc, dpre_i, _TN)
            dlv_ref[1 + d:2 + d, :] += jnp.sum(dpre_r, axis=0, keepdims=True)
            dlv_ref[3 + d:4 + d, :] += jnp.sum(dpre_i, axis=0, keepdims=True)
            dlv_ref[5 + d:6 + d, :] += -dsp * _sigmoid(-lam)

        dxc = dxc_s[...]
        dxl, dw = _conv_bwd(xl, cw_ref[...], taps, dxc)
        dcw_ref[0:4, :] += dw
        dlv_ref[0:1, :] += jnp.sum(dxc, axis=0, keepdims=True)
        dp_ref[:, 0:C] = dxl.astype(BF16)
        (dyl,) = gelu_vjp(dy_ref[...] * (h_s[0, nc:, :] + h_s[1, nc:, :]))
        dp_ref[0:nc, C:2 * C] = jnp.zeros((nc, C), BF16)
        dp_ref[nc:, C:2 * C] = dyl.astype(BF16)

    xblk = pl.BlockSpec((T, 2 * C), lambda j, b: (b, col0 // (2 * C) + j))
    wblk = pl.BlockSpec((2, 1, C, C), lambda j, b: (0, j, 0, 0))
    vblk = pl.BlockSpec((8, C), lambda j, b: (0, j))
    return pl.pallas_call(
        body, name="lru_bwd", grid=(nj, B),
        in_specs=[xblk, pl.BlockSpec((4, C), lambda j, b: (0, j)), vblk, wblk, wblk,
                  pl.BlockSpec((N, C), lambda j, b: (b, j)), pl.BlockSpec(memory_space=pl.ANY)],
        out_specs=[xblk, vblk, vblk, wblk, wblk],
        out_shape=[jax.ShapeDtypeStruct(dp.shape, dp.dtype), jax.ShapeDtypeStruct((8, LW), F32),
                   jax.ShapeDtypeStruct((8, LW), F32), jax.ShapeDtypeStruct((2, nj, C, C), F32),
                   jax.ShapeDtypeStruct((2, nj, C, C), F32)],
        scratch_shapes=[pltpu.VMEM((2, T, C), F32)] * 6 + [pltpu.VMEM((T, C), F32)] * 2,
        input_output_aliases={6: 0},
        compiler_params=_params(("parallel", "arbitrary")),
    )(p, cw, lv, wr, wi, dy, dp)


def _chunk_masks(upper):
    i = lax.broadcasted_iota(jnp.int32, (CHUNK, CHUNK), 0)
    j = lax.broadcasted_iota(jnp.int32, (CHUNK, CHUNK), 1)
    ahead = jnp.where(upper, j - i, i - j)
    return i == j, ahead >= 0, ahead > 0


def _col2row(c, eye):
    return jnp.sum(jnp.where(eye, c, 0.0), axis=0, keepdims=True)


def _row2col(r, eye):
    return jnp.sum(jnp.where(eye, r, 0.0), axis=1, keepdims=True)


def _rowsum(x):
    return jnp.sum(x, axis=1, keepdims=True)


_INV_BASE = 8


def _unit_tri_inverses(Ls):
    G = len(Ls)
    W = G * CHUNK
    blk = (lax.broadcasted_iota(jnp.int32, (W, W), 0) // CHUNK) == (lax.broadcasted_iota(jnp.int32, (W, W), 1) // CHUNK)
    ri = lax.broadcasted_iota(jnp.int32, (CHUNK, W), 0)
    ci = lax.broadcasted_iota(jnp.int32, (CHUNK, W), 1) % CHUNK

    def bd(b):
        return jnp.where(blk, jnp.tile(b, (G, 1)), jnp.zeros((), BF16))

    def pdot(a, b):
        return jnp.dot(a.astype(BF16), bd(b.astype(BF16)), preferred_element_type=F32)

    Lc = Ls[0] if G == 1 else jnp.concatenate(Ls, axis=1)
    s = _INV_BASE
    Xp = -jnp.where(ri // s == ci // s, Lc, 0.0)
    Rm = Xp
    for _ in range(int(math.log2(s)) - 1):
        Xp = pdot(Xp, Xp)
        Rm = Rm + Xp + pdot(Rm, Xp)
    while s < CHUNK:
        E = jnp.where((ri // (2 * s) == ci // (2 * s)) & (ri // s != ci // s), Lc, 0.0)
        DE = E + pdot(Rm, E)
        Rm = Rm - (DE + pdot(DE, Rm))
        s *= 2
    eye = _chunk_masks(False)[0]
    return [jnp.where(eye, 1.0, 0.0) + Rm[:, g * CHUNK:(g + 1) * CHUNK] for g in range(G)]


def _delta_chunk_common(q, k, v, gcol, bcol, upper):
    eye, incl, strict = _chunk_masks(upper)
    gc = _rowsum(jnp.where(incl, _col2row(gcol, eye), 0.0))
    D = jnp.where(incl, jnp.exp(jnp.minimum(gc - _col2row(gc, eye), 0.0)), 0.0)
    kb = k * bcol
    AP = _bdot(jnp.concatenate([kb, q], axis=0), k, _NT)
    A = AP[:CHUNK]
    L = jnp.where(strict, A * D, 0.0)
    eg = jnp.exp(gc)
    gl = jnp.sum(gcol, axis=0, keepdims=True)
    attn = jnp.where(incl, AP[CHUNK:] * D, 0.0)
    return dict(eye=eye, incl=incl, strict=strict, gc=gc, D=D, kb=kb, A=A, L=L, eg=eg, gl=gl, egl=jnp.exp(gl),
                attn=attn, kbe=kb * eg, vb=v * bcol, qe=q * eg, kd=k * jnp.exp(gl - gc))


def _delta_group_pre(chunks, upper):
    cs = [_delta_chunk_common(*ch, upper) for ch in chunks]
    out = []
    for c, Tm in zip(cs, _unit_tri_inverses([c["L"] for c in cs])):
        dk = c["kbe"].shape[1]
        wu = _bdot(Tm, jnp.concatenate([c["kbe"], c["vb"]], axis=1))
        KN = _bdot(c["kd"], wu, _TN)
        QO = _bdot(c["attn"], wu)
        out.append((Tm, KN[:, :dk], KN[:, dk:], c["qe"] - QO[:, :dk], QO[:, dk:], c["egl"]))
    return out


def _delta_chunk_bwd(q, k, v, gcol, bcol, S, Tm, do, dS2, upper):
    c = _delta_chunk_common(q, k, v, gcol, bcol, upper)
    eye, incl, strict, D, eg, egl = c["eye"], c["incl"], c["strict"], c["D"], c["eg"], c["egl"]
    kb, kbe, vb, qe, kd, attn = c["kb"], c["kbe"], c["vb"], c["qe"], c["kd"], c["attn"]
    dkk = kbe.shape[1]
    wu = _bdot(Tm, jnp.concatenate([kbe, vb], axis=1))
    w = wu[:, :dkk]
    vn = wu[:, dkk:] - _bdot(w, S)
    dvn = _bdot(kd, dS2) + _bdot(attn, do, _TN)
    dkd = _bdot(vn, dS2, _NT)
    dgl = jnp.sum(_rowsum(dS2 * S), axis=0, keepdims=True) * egl
    dqa = _bdot(do, jnp.concatenate([S, vn], axis=0), _NT)
    dqe = dqa[:, :dkk]
    dattn = jnp.where(incl, dqa[:, dkk:], 0.0)
    dw = -_bdot(dvn, S, _NT)
    r = _rowsum(dkd * kd)
    dk = dkd * jnp.exp(c["gl"] - c["gc"])
    dgl = dgl + jnp.sum(r, axis=0, keepdims=True)
    dgc = _rowsum(dqe * qe) - r
    E = dattn * attn
    dvw = jnp.concatenate([dvn, dw], axis=1)
    dTm = _bdot(dvw, jnp.concatenate([vb, kbe], axis=1), _NT)
    dvk = _bdot(Tm, dvw, _TN)
    dvb = dvk[:, :dvn.shape[1]]
    dv = dvb * bcol
    dbeta = _rowsum(dvb * v)
    dkbe = dvk[:, dvn.shape[1]:]
    dkb = dkbe * eg
    dgc = dgc + _rowsum(dkbe * kbe)
    dL = jnp.where(strict, -_bdot(Tm, _bdot(dTm, Tm, _NT), _TN), 0.0)
    dA = dL * D
    E = E + dL * c["L"]
    PA = jnp.concatenate([dattn * D, dA], axis=0)
    PAk = _bdot(PA, k)
    dq = dqe * eg + PAk[:CHUNK]
    dkb = dkb + PAk[CHUNK:]
    dk = dk + _bdot(PA, jnp.concatenate([q, kb], axis=0), _TN) + dkb * bcol
    dbeta = dbeta + _rowsum(dkb * k)
    dgc = dgc + _rowsum(E) - _row2col(jnp.sum(E, axis=0, keepdims=True), eye)
    dg = _row2col(jnp.sum(jnp.where(incl, dgc, 0.0), axis=0, keepdims=True), eye) + dgl
    return dq, dk, dv, dg, dbeta


def _delta_group(n):
    return max(g for g in range(1, 2 * LANES // CHUNK + 1) if n % g == 0)


def _delta_chunk_at(T, nc, d, i):
    n, ncc = T // CHUNK, nc // CHUNK
    desc = jnp.where(i < ncc, ncc - 1 - i, n - 1 - (i - ncc))
    if isinstance(d, int):
        return i if d == 0 else desc
    return jnp.where(d == 0, i, desc)


def _dn_out_math(o, onorm, z):
    return _rmsn(o, onorm) * _silu(z)


def _delta_fwd(qkv, gb, p, onorm, *, B, T, nc, H, HD):
    N = T - nc
    n = T // CHUNK
    G = _delta_group(n)

    def body(q_ref, k_ref, v_ref, gb_ref, z_ref, on_ref, y_ref, o_ref, Tm_ref, K_ref, S_ref, Qp_ref, eg_ref,
             N_s, O0_s, o_s):
        h = pl.program_id(1)
        lane = lax.broadcasted_iota(jnp.int32, (CHUNK, LANES), 1)

        def pre(g, carry):
            cs = [g * G + i for i in range(G)]
            rows = [pl.ds(pl.multiple_of(c * CHUNK, CHUNK), CHUNK) for c in cs]
            for d in (0, 1):
                chunks = []
                for r in rows:
                    gbb = gb_ref[r, :]
                    chunks.append((q_ref[r, :], k_ref[r, :], v_ref[r, :],
                                   _rowsum(jnp.where(lane == d * H + h, gbb, 0.0)),
                                   _rowsum(jnp.where(lane == 2 * H + d * H + h, gbb, 0.0))))
                for c, r, (Tm, K, Nn, Qp, O0, egl) in zip(cs, rows, _delta_group_pre(chunks, d == 1)):
                    Tm_ref[0, d * n + c] = Tm
                    K_ref[0, d * n + c] = K.astype(BF16)
                    N_s[d * n + c] = Nn
                    Qp_ref[0, d, r, :] = Qp.astype(BF16)
                    O0_s[d, r, :] = O0
                    eg_ref[0, d * n + c] = jnp.broadcast_to(egl, (SUBLANES, HD))
            return carry

        lax.fori_loop(0, n // G, pre, 0)

        def step(i, Ss):
            out = []
            for d in (0, 1):
                c = _delta_chunk_at(T, nc, d, i)
                rows = pl.ds(pl.multiple_of(c * CHUNK, CHUNK), CHUNK)
                S_ref[0, d * n + c] = Ss[d]
                Sb = Ss[d].astype(BF16)
                o_s[d, rows, :] = jnp.dot(Qp_ref[0, d, rows, :], Sb, preferred_element_type=F32) + O0_s[d, rows, :]
                out.append(eg_ref[0, d * n + c][0:1] * Ss[d] + N_s[d * n + c]
                           - jnp.dot(K_ref[0, d * n + c], Sb, preferred_element_type=F32))
            return tuple(out)

        lax.fori_loop(0, n, step, (jnp.zeros((HD, HD), F32), jnp.zeros((HD, HD), F32)))
        o = o_s[0, nc:, :] + o_s[1, nc:, :]
        o_ref[...] = o
        y_ref[...] = _dn_out_math(o, on_ref[...], z_ref[nc:, :]).astype(BF16)

    col = lambda off: pl.BlockSpec((T, HD), lambda b, h: (b, off + h))
    lat = pl.BlockSpec((N, HD), lambda b, h: (b, h))
    per = lambda *blk: pl.BlockSpec((1, *blk), lambda b, h: (b * H + h, 0, 0, 0))
    return pl.pallas_call(
        body, name="delta_fwd", grid=(B, H),
        in_specs=[col(0), col(H), col(2 * H), pl.BlockSpec((T, LANES), lambda b, h: (b, 0)), col(3 * H),
                  pl.BlockSpec((1, HD), lambda b, h: (0, 0))],
        out_specs=[lat, lat, per(2 * n, CHUNK, CHUNK), per(2 * n, HD, HD), per(2 * n, HD, HD), per(2, T, HD),
                   per(2 * n, SUBLANES, HD)],
        out_shape=[jax.ShapeDtypeStruct((B * N, H * HD), BF16), jax.ShapeDtypeStruct((B * N, H * HD), F32),
                   jax.ShapeDtypeStruct((B * H, 2 * n, CHUNK, CHUNK), F32),
                   jax.ShapeDtypeStruct((B * H, 2 * n, HD, HD), BF16), jax.ShapeDtypeStruct((B * H, 2 * n, HD, HD), F32),
                   jax.ShapeDtypeStruct((B * H, 2, T, HD), BF16), jax.ShapeDtypeStruct((B * H, 2 * n, SUBLANES, HD), F32)],
        scratch_shapes=[pltpu.VMEM((2 * n, HD, HD), F32), pltpu.VMEM((2, T, HD), F32), pltpu.VMEM((2, T, HD), F32)],
        compiler_params=_params(("parallel", "parallel")),
    )(qkv, qkv, qkv, gb, p, onorm)


def _delta_bwd(qkv, gb, p, onorm, o, res, dy, dp, *, B, T, nc, H, HD):
    N = T - nc
    n = T // CHUNK

    def body(q_ref, k_ref, v_ref, gb_ref, z_ref, on_ref, o_ref, dy_ref, Tm_ref, K_ref, S_ref, Qp_ref, eg_ref, dp_any,
             dqkv_ref, dgb_ref, dp_ref, don_ref, do_s, R_s, dS_s):
        h, d = pl.program_id(1), pl.program_id(2)
        lane = lax.broadcasted_iota(jnp.int32, (CHUNK, LANES), 1)

        @pl.when(d == 0)
        def _():
            _, vjp = jax.vjp(_dn_out_math, o_ref[...], on_ref[...], z_ref[nc:, :])
            do, don, dz = vjp(dy_ref[...])
            do_s[0:nc, :] = jnp.zeros((nc, HD), F32)
            do_s[nc:, :] = do
            dp_ref[0:nc, :] = jnp.zeros((nc, HD), BF16)
            dp_ref[nc:, :] = dz.astype(BF16)
            dqkv_ref[...] = jnp.zeros_like(dqkv_ref)

            @pl.when(h == 0)
            def _():
                don_ref[...] = jnp.zeros_like(don_ref)
                dgb_ref[...] = jnp.zeros_like(dgb_ref)

            don_ref[0, 0:1, :] += don

        def r_of(c, carry):
            rows = pl.ds(pl.multiple_of(c * CHUNK, CHUNK), CHUNK)
            R_s[c] = lax.dot_general(Qp_ref[0, 0, rows, :], do_s[rows, :].astype(BF16), _TN, preferred_element_type=F32)
            return carry

        lax.fori_loop(0, n, r_of, 0)

        def bwd_step(i, dS):
            c = _delta_chunk_at(T, nc, d, n - 1 - i)
            dS_s[c] = dS
            return (eg_ref[0, c][0:1] * dS + R_s[c]
                    - lax.dot_general(K_ref[0, c], dS.astype(BF16), _TN, preferred_element_type=F32))

        lax.fori_loop(0, n, bwd_step, jnp.zeros((HD, HD), F32))

        def grads(c, carry):
            rows = pl.ds(pl.multiple_of(c * CHUNK, CHUNK), CHUNK)
            gbb = gb_ref[rows, :]
            gcol = _rowsum(jnp.where(lane == d * H + h, gbb, 0.0))
            bcol = _rowsum(jnp.where(lane == 2 * H + d * H + h, gbb, 0.0))
            dq, dk, dv, dg, dbeta = _delta_chunk_bwd(q_ref[rows, :], k_ref[rows, :], v_ref[rows, :], gcol, bcol,
                                                     S_ref[0, c], Tm_ref[0, c], do_s[rows, :], dS_s[c], d == 1)
            dqkv_ref[0, rows, :] += dq
            dqkv_ref[1, rows, :] += dk
            dqkv_ref[2, rows, :] += dv
            dgb_ref[rows, :] += (jnp.where(lane == d * H + h, dg, 0.0)
                                 + jnp.where(lane == 2 * H + d * H + h, dbeta, 0.0))
            return carry

        lax.fori_loop(0, n, grads, 0)

    col = lambda off: pl.BlockSpec((T, HD), lambda b, h, d: (b, off + h))
    lat = pl.BlockSpec((N, HD), lambda b, h, d: (b, h))
    per = lambda *blk: pl.BlockSpec((1, *blk), lambda b, h, d: (b * H + h, d, 0, 0))
    return pl.pallas_call(
        body, name="delta_bwd", grid=(B, H, 2),
        in_specs=[col(0), col(H), col(2 * H), pl.BlockSpec((T, LANES), lambda b, h, d: (b, 0)), col(3 * H),
                  pl.BlockSpec((1, HD), lambda b, h, d: (0, 0)), lat, lat,
                  per(n, CHUNK, CHUNK), per(n, HD, HD), per(n, HD, HD), per(1, T, HD), per(n, SUBLANES, HD),
                  pl.BlockSpec(memory_space=pl.ANY)],
        out_specs=[pl.BlockSpec((3, T, HD), lambda b, h, d: (0, b, h)), pl.BlockSpec((T, LANES), lambda b, h, d: (b, 0)),
                   col(3 * H), pl.BlockSpec((1, 8, HD), lambda b, h, d: (b, 0, 0))],
        out_shape=[jax.ShapeDtypeStruct((3, B * T, H * HD), F32), jax.ShapeDtypeStruct((B * T, LANES), F32),
                   jax.ShapeDtypeStruct(dp.shape, dp.dtype), jax.ShapeDtypeStruct((B, 8, HD), F32)],
        scratch_shapes=[pltpu.VMEM((T, HD), F32), pltpu.VMEM((n, HD, HD), F32), pltpu.VMEM((n, HD, HD), F32)],
        input_output_aliases={13: 2},
        compiler_params=_params(("parallel", "arbitrary", "arbitrary")),
    )(qkv, qkv, qkv, gb, p, onorm, o, dy, *res, dp)


def _rowwise(fn, ins, out_dtypes, *, name, tm=256, mult=16):
    R, W = ins[0].shape
    tm = _tile(R, tm, mult)

    def body(*refs):
        outs = fn(*[r[...] for r in refs[:len(ins)]])
        for o_ref, o in zip(refs[len(ins):], outs):
            o_ref[...] = o.astype(o_ref.dtype)

    spec = pl.BlockSpec((tm, W), lambda i: (i, 0))
    return pl.pallas_call(
        body, name=name, grid=(R // tm,), in_specs=[spec] * len(ins), out_specs=[spec] * len(out_dtypes),
        out_shape=[jax.ShapeDtypeStruct((R, W), dt) for dt in out_dtypes],
        compiler_params=_params(("parallel",)),
    )(*ins)


def _sum_lead(x, *, name, tm=256, mult=16):
    S, R, W = x.shape
    tm = _tile(R, tm, mult)

    def body(*refs):
        acc = refs[0][0].astype(F32)
        for r in refs[1:S]:
            acc = acc + r[0].astype(F32)
        refs[S][...] = acc

    return pl.pallas_call(
        body, name=name, grid=(R // tm,),
        in_specs=[pl.BlockSpec((1, tm, W), functools.partial(lambda s, i: (s, i, 0), s)) for s in range(S)],
        out_specs=pl.BlockSpec((tm, W), lambda i: (i, 0)),
        out_shape=jax.ShapeDtypeStruct((R, W), F32),
        compiler_params=_params(("parallel",)),
    )(*([x] * S))


def _adamw_math(w, g, m, v):
    m = ADAM_B1 * m + (1.0 - ADAM_B1) * g
    v = ADAM_B2 * v + (1.0 - ADAM_B2) * (g * g)
    m_hat = m / (1.0 - ADAM_B1 ** ADAM_STEP)
    v_hat = v / (1.0 - ADAM_B2 ** ADAM_STEP)
    return -ADAM_LR * (m_hat / (jnp.sqrt(v_hat) + ADAM_EPS) + ADAM_WD * w), m, v


def _adamw(w, g, m, v, *, name):
    tm = max(SUBLANES, (256 * 1024) // w.shape[1] // SUBLANES * SUBLANES)
    return _rowwise(_adamw_math, [w, g, m, v], [F32, F32, F32], name=name, tm=tm, mult=SUBLANES)


def _me():
    return lax.axis_index("x"), lax.axis_index("y"), lax.axis_index("c")


def _allgather_small(v):
    R, W = v.shape

    def body(x_ref, out_ref, send_sems, recv_sems, local_sem):
        x, y, c = _me()
        me, sibling = (x, y, c), (x, y, 1 - c)
        chips = [(1 - x, y), (x, 1 - y), (1 - x, 1 - y)]

        def slot(px, py, pc):
            return out_ref.at[4 * px + 2 * py + pc]

        def copy(k, block, to, src=None):
            return pltpu.make_async_remote_copy(
                src_ref=slot(*block) if src is None else src, dst_ref=slot(*block),
                send_sem=send_sems.at[k], recv_sem=recv_sems.at[k], device_id=to, device_id_type=MESH)

        mine = pltpu.make_async_copy(x_ref, slot(*me), local_sem)
        mine.start()
        first = [copy(0, me, sibling, src=x_ref)]
        first += [copy(1 + j, me, (*chip, c), src=x_ref) for j, chip in enumerate(chips)]
        for cp in first:
            cp.start()
        passed = [copy(4 + j, (*chip, c), sibling) for j, chip in enumerate(chips)]
        for j, chip in enumerate(chips):
            copy(1 + j, (*chip, c), me).wait_recv()
            passed[j].start()
        copy(0, sibling, me).wait_recv()
        for j, chip in enumerate(chips):
            copy(4 + j, (*chip, 1 - c), me).wait_recv()
        for cp in first + passed:
            cp.wait_send()
        mine.wait()

    return pl.pallas_call(
        body, name="allgather_small", out_shape=jax.ShapeDtypeStruct((8, R, W), v.dtype),
        in_specs=[pl.BlockSpec(memory_space=pltpu.VMEM)], out_specs=pl.BlockSpec(memory_space=pltpu.VMEM),
        scratch_shapes=[pltpu.SemaphoreType.DMA((7,)), pltpu.SemaphoreType.DMA((7,)), pltpu.SemaphoreType.DMA],
        compiler_params=_params(),
    )(v)


_ANY = pl.BlockSpec(memory_space=pl.ANY)


def _allgather_halves(shards, *, name):
    nw = len(shards)

    def body(*refs):
        x_refs, out_refs = refs[:nw], refs[nw:2 * nw]
        send_sems, recv_sems, local_sems = refs[2 * nw:]
        x, y, c = _me()
        me, sibling = (x, y, c), (x, y, 1 - c)
        chips = [(1 - x, y), (x, 1 - y), (1 - x, 1 - y)]

        def slot(w, px, py, pc):
            return out_refs[w].at[4 * px + 2 * py + pc]

        def copy(w, k, block, to, src=None):
            return pltpu.make_async_remote_copy(
                src_ref=slot(w, *block) if src is None else src, dst_ref=slot(w, *block),
                send_sem=send_sems.at[w, k], recv_sem=recv_sems.at[w, k], device_id=to, device_id_type=MESH)

        started, local = [], []
        for w in range(nw):
            half = shards[w].shape[0] // 2
            own = x_refs[w].at[pl.ds(c * half, half), :]
            mine = pltpu.make_async_copy(own, slot(w, *me), local_sems.at[w])
            mine.start()
            first = [copy(w, 0, me, sibling, src=own)]
            first += [copy(w, 1 + j, me, (*chip, c), src=own) for j, chip in enumerate(chips)]
            for cp in first:
                cp.start()
            started += first
            local.append(mine)
        for w in range(nw):
            for j, chip in enumerate(chips):
                copy(w, 1 + j, (*chip, c), me).wait_recv()
                fwd = copy(w, 4 + j, (*chip, c), sibling)
                fwd.start()
                started.append(fwd)
        for w in range(nw):
            copy(w, 0, sibling, me).wait_recv()
            for j, chip in enumerate(chips):
                copy(w, 4 + j, (*chip, 1 - c), me).wait_recv()
        for cp in started:
            cp.wait_send()
        for cp in local:
            cp.wait()

    return pl.pallas_call(
        body, name=name,
        out_shape=[jax.ShapeDtypeStruct((8, s.shape[0] // 2, s.shape[1]), s.dtype) for s in shards],
        in_specs=[_ANY] * nw, out_specs=[_ANY] * nw,
        scratch_shapes=[pltpu.SemaphoreType.DMA((nw, 7)), pltpu.SemaphoreType.DMA((nw, 7)), pltpu.SemaphoreType.DMA((nw,))],
        compiler_params=_params(),
    )(*shards)


def _sibling_send_halves(arrs, *, name):
    nw = len(arrs)

    def body(*refs):
        x_refs, out_refs, send_sems, recv_sems = refs[:nw], refs[nw:2 * nw], refs[2 * nw], refs[2 * nw + 1]
        x, y, c = _me()
        cps = []
        for w in range(nw):
            half = arrs[w].shape[1] // 2
            cp = pltpu.make_async_remote_copy(
                src_ref=x_refs[w].at[:, pl.ds((1 - c) * half, half), :], dst_ref=out_refs[w],
                send_sem=send_sems.at[w], recv_sem=recv_sems.at[w], device_id=(x, y, 1 - c), device_id_type=MESH)
            cp.start()
            cps.append(cp)
        for cp in cps:
            cp.wait()

    return pl.pallas_call(
        body, name=name,
        out_shape=[jax.ShapeDtypeStruct((a.shape[0], a.shape[1] // 2, a.shape[2]), a.dtype) for a in arrs],
        in_specs=[_ANY] * nw, out_specs=[_ANY] * nw,
        scratch_shapes=[pltpu.SemaphoreType.DMA((nw,)), pltpu.SemaphoreType.DMA((nw,))],
        compiler_params=_params(),
    )(*arrs)


def _sibling_swap(arrs, *, name):
    nw = len(arrs)

    def body(*refs):
        x_refs, out_refs, send_sems, recv_sems = refs[:nw], refs[nw:2 * nw], refs[2 * nw], refs[2 * nw + 1]
        x, y, c = _me()
        cps = []
        for w in range(nw):
            cp = pltpu.make_async_remote_copy(
                src_ref=x_refs[w], dst_ref=out_refs[w], send_sem=send_sems.at[w], recv_sem=recv_sems.at[w],
                device_id=(x, y, 1 - c), device_id_type=MESH)
            cp.start()
            cps.append(cp)
        for cp in cps:
            cp.wait()

    return pl.pallas_call(
        body, name=name, out_shape=[jax.ShapeDtypeStruct(a.shape, a.dtype) for a in arrs],
        in_specs=[_ANY] * nw, out_specs=[_ANY] * nw,
        scratch_shapes=[pltpu.SemaphoreType.DMA((nw,)), pltpu.SemaphoreType.DMA((nw,))],
        compiler_params=_params(),
    )(*arrs)


def _adamw_halves(w, own, sib, m, v, c_arr, *, name):
    r, cols = w.shape
    h = r // 2
    tm = _tile(h, max(SUBLANES, (192 * 1024) // cols // SUBLANES * SUBLANES), SUBLANES)
    nb = h // tm

    def body(c_ref, w_ref, own_ref, sib_ref, m_ref, v_ref, g_out, d_out, m_out, v_out):
        g = jnp.where(pl.program_id(0) == c_ref[0], own_ref[...], sib_ref[...])
        g_out[...] = g
        d_out[...], m_out[...], v_out[...] = _adamw_math(w_ref[...], g, m_ref[...], v_ref[...])

    full = pl.BlockSpec((tm, cols), lambda hh, i, c_ref: (hh * nb + i, 0))
    half = pl.BlockSpec((tm, cols), lambda hh, i, c_ref: (i, 0))
    return pl.pallas_call(
        body, name=name,
        grid_spec=pltpu.PrefetchScalarGridSpec(num_scalar_prefetch=1, grid=(2, nb),
                                               in_specs=[full, half, half, full, full], out_specs=[full] * 4),
        out_shape=[jax.ShapeDtypeStruct((r, cols), F32)] * 4,
        compiler_params=_params(("parallel", "parallel")),
    )(c_arr, w, own, sib, m, v)


def _chip_exchange(arrs, *, name):
    nw = len(arrs)

    def body(*refs):
        x_refs, out_refs = refs[:nw], refs[nw:2 * nw]
        send_sems, recv_sems, local_sems = refs[2 * nw:]
        x, y, c = _me()
        s_me = 2 * x + y
        chips = [(1 - x, y), (x, 1 - y), (1 - x, 1 - y)]
        started, local = [], []
        for w in range(nw):
            mine = pltpu.make_async_copy(x_refs[w].at[s_me], out_refs[w].at[s_me], local_sems.at[w])
            mine.start()
            local.append(mine)
            for k, (px, py) in enumerate(chips):
                cp = pltpu.make_async_remote_copy(
                    src_ref=x_refs[w].at[2 * px + py], dst_ref=out_refs[w].at[s_me], send_sem=send_sems.at[w, k],
                    recv_sem=recv_sems.at[w, k], device_id=(px, py, c), device_id_type=MESH)
                cp.start()
                started.append(cp)
        for w in range(nw):
            for k, (px, py) in enumerate(chips):
                pltpu.make_async_remote_copy(
                    src_ref=x_refs[w].at[s_me], dst_ref=out_refs[w].at[2 * px + py], send_sem=send_sems.at[w, k],
                    recv_sem=recv_sems.at[w, k], device_id=(px, py, c), device_id_type=MESH).wait_recv()
        for cp in started:
            cp.wait_send()
        for cp in local:
            cp.wait()

    return pl.pallas_call(
        body, name=name, out_shape=[jax.ShapeDtypeStruct(a.shape, a.dtype) for a in arrs],
        in_specs=[_ANY] * nw, out_specs=[_ANY] * nw,
        scratch_shapes=[pltpu.SemaphoreType.DMA((nw, 3)), pltpu.SemaphoreType.DMA((nw, 3)), pltpu.SemaphoreType.DMA((nw,))],
        compiler_params=_params(),
    )(*arrs)


def _half_add(g, recv, c_arr, *, name):
    S, r, w = g.shape
    h = r // 2
    tm = _tile(h, 512, 16)
    nb = h // tm

    def body(c_ref, g_ref, r_ref, o_ref):
        o_ref[...] = (g_ref[...] + r_ref[...]).astype(BF16)

    return pl.pallas_call(
        body, name=name,
        grid_spec=pltpu.PrefetchScalarGridSpec(
            num_scalar_prefetch=1, grid=(S, nb),
            in_specs=[pl.BlockSpec((1, tm, w), lambda s, i, c_ref: (s, c_ref[0] * nb + i, 0)),
                      pl.BlockSpec((1, tm, w), lambda s, i, c_ref: (s, i, 0))],
            out_specs=pl.BlockSpec((1, tm, w), lambda s, i, c_ref: (s, i, 0))),
        out_shape=jax.ShapeDtypeStruct((S, h, w), BF16),
        compiler_params=_params(("parallel", "parallel")),
    )(c_arr, g, recv)


def _layout(sizes, width, part_mult, total_mult):
    offs, rows, r = [], [], 0
    for n in sizes:
        k = -(-n // width)
        offs.append(r)
        rows.append(k)
        r += -(-k // part_mult) * part_mult
    return offs, rows, -(-r // total_mult) * total_mult


def _pack(arrs, width, part_mult, total_mult, dtype, lead=()):
    nl = len(lead)
    sizes = [math.prod(a.shape[nl:]) for a in arrs]
    offs, rows, total = _layout(sizes, width, part_mult, total_mult)
    parts, r = [], 0
    for a, n, o, k in zip(arrs, sizes, offs, rows):
        kp = -(-k // part_mult) * part_mult
        flat = a.reshape(*lead, n).astype(dtype)
        if kp * width > n:
            flat = jnp.pad(flat, [(0, 0)] * nl + [(0, kp * width - n)])
        parts.append(flat.reshape(*lead, kp, width))
        r = o + kp
    if total > r:
        parts.append(jnp.zeros((*lead, total - r, width), dtype))
    return jnp.concatenate(parts, axis=nl)


def _unpack(pool, shapes, width, part_mult, total_mult):
    lead = pool.shape[:-2]
    sizes = [math.prod(s) for s in shapes]
    offs, rows, _ = _layout(sizes, width, part_mult, total_mult)
    out = []
    for s, n, o, k in zip(shapes, sizes, offs, rows):
        flat = lax.slice_in_dim(pool, o, o + k, axis=len(lead)).reshape(*lead, k * width)
        out.append(lax.slice_in_dim(flat, 0, n, axis=len(lead)).reshape(*lead, *s))
    return out


_WEIGHTS = ("c_ctx", "w_ada", "b_ada", "g_pre_mix", "g_post_mix", "g_pre_ffn", "g_post_ffn", "w_in", "b_merge",
            "dn_conv", "dn_a_log", "dn_dt_bias", "dn_onorm", "lru_conv", "lru_conv_b", "lru_w_rg", "lru_b_rg",
            "lru_w_ig", "lru_b_ig", "lru_lambda", "w_branch_dn", "w_branch_lru", "w_out", "w_up", "ffn_dw",
            "ffn_dw_b", "w_down")
_BIG = {"w_ada": True, "w_in": True, "w_branch_dn": False, "w_branch_lru": False, "w_out": False, "w_up": True,
        "w_down": False}
_SMALL_SHARDED = ("dn_conv", "lru_conv", "lru_b_rg", "lru_b_ig", "lru_lambda", "ffn_dw")
_NCHIP = 4
_FLAT_PART = 8
_FLAT_TOTAL = 256


def _to_chip_shards(g, by_cols):
    if by_cols:
        return g.reshape(g.shape[0], _NCHIP, g.shape[1] // _NCHIP).transpose(1, 0, 2)
    return g.reshape(_NCHIP, g.shape[0] // _NCHIP, g.shape[1])


def _from_chip_shards(s, by_cols):
    if by_cols:
        return s.transpose(1, 0, 2).reshape(s.shape[1], _NCHIP * s.shape[2])
    return s.reshape(_NCHIP * s.shape[1], s.shape[2])


def _dsilu(x):
    s = _sigmoid(x)
    return s * (1.0 + x * (1.0 - s))


def kernel(x, c, ctx, c_ctx, w_ada, b_ada, g_pre_mix, g_post_mix, g_pre_ffn, g_post_ffn, w_in, b_merge, dn_conv, dn_a_log, dn_dt_bias, dn_onorm, lru_conv, lru_conv_b, lru_w_rg, lru_b_rg, lru_w_ig, lru_b_ig, lru_lambda, w_branch_dn, w_branch_lru, w_out, w_up, ffn_dw, ffn_dw_b, w_down, loss_target, m_c_ctx, m_w_ada, m_b_ada, m_g_pre_mix, m_g_post_mix, m_g_pre_ffn, m_g_post_ffn, m_w_in, m_b_merge, m_dn_conv, m_dn_a_log, m_dn_dt_bias, m_dn_onorm, m_lru_conv, m_lru_conv_b, m_lru_w_rg, m_lru_b_rg, m_lru_w_ig, m_lru_b_ig, m_lru_lambda, m_w_branch_dn, m_w_branch_lru, m_w_out, m_w_up, m_ffn_dw, m_ffn_dw_b, m_w_down, v_c_ctx, v_w_ada, v_b_ada, v_g_pre_mix, v_g_post_mix, v_g_pre_ffn, v_g_post_ffn, v_w_in, v_b_merge, v_dn_conv, v_dn_a_log, v_dn_dt_bias, v_dn_onorm, v_lru_conv, v_lru_conv_b, v_lru_w_rg, v_lru_b_rg, v_lru_w_ig, v_lru_b_ig, v_lru_lambda, v_w_branch_dn, v_w_branch_lru, v_w_out, v_w_up, v_ffn_dw, v_ffn_dw_b, v_w_down):
    W = dict(zip(_WEIGHTS, (c_ctx, w_ada, b_ada, g_pre_mix, g_post_mix, g_pre_ffn, g_post_ffn, w_in, b_merge, dn_conv,
                            dn_a_log, dn_dt_bias, dn_onorm, lru_conv, lru_conv_b, lru_w_rg, lru_b_rg, lru_w_ig, lru_b_ig,
                            lru_lambda, w_branch_dn, w_branch_lru, w_out, w_up, ffn_dw, ffn_dw_b, w_down)))
    Mo = dict(zip(_WEIGHTS, (m_c_ctx, m_w_ada, m_b_ada, m_g_pre_mix, m_g_post_mix, m_g_pre_ffn, m_g_post_ffn, m_w_in,
                             m_b_merge, m_dn_conv, m_dn_a_log, m_dn_dt_bias, m_dn_onorm, m_lru_conv, m_lru_conv_b,
                             m_lru_w_rg, m_lru_b_rg, m_lru_w_ig, m_lru_b_ig, m_lru_lambda, m_w_branch_dn,
                             m_w_branch_lru, m_w_out, m_w_up, m_ffn_dw, m_ffn_dw_b, m_w_down)))
    Vo = dict(zip(_WEIGHTS, (v_c_ctx, v_w_ada, v_b_ada, v_g_pre_mix, v_g_post_mix, v_g_pre_ffn, v_g_post_ffn, v_w_in,
                             v_b_merge, v_dn_conv, v_dn_a_log, v_dn_dt_bias, v_dn_onorm, v_lru_conv, v_lru_conv_b,
                             v_lru_w_rg, v_lru_b_rg, v_lru_w_ig, v_lru_b_ig, v_lru_lambda, v_w_branch_dn,
                             v_w_branch_lru, v_w_out, v_w_up, v_ffn_dw, v_ffn_dw_b, v_w_down)))
    B, N, D = x.shape
    NC = ctx.shape[1]
    T = NC + N
    H, HD = dn_a_log.shape[-1], dn_onorm.shape[-1]
    DNW = H * HD
    LW, LBD = lru_conv_b.shape[-1], lru_w_rg.shape[-1]
    DFF = ffn_dw_b.shape[-1]
    LC = LANES
    x_i, y_i, c_i = _me()
    s_me = 2 * x_i + y_i
    tm = _tile(math.gcd(NC, N), 256, 16)

    gathered = _allgather_halves([W[n][0].astype(BF16) for n in _BIG], name="allgather_big")
    full = {}
    for n, g in zip(_BIG, gathered):
        r, w_ = W[n].shape[1:]
        full[n] = g.reshape(_NCHIP, r, w_) if _BIG[n] else g.reshape(_NCHIP * r, w_)

    small_local = [W[n][0].reshape(-1, W[n].shape[-1]) for n in _SMALL_SHARDED]
    small_shapes = [a.shape for a in small_local]
    spack = _pack(small_local, LANES, _FLAT_PART, _FLAT_PART, F32)
    sgath = _allgather_small(spack)[0::2]
    sfull = {n: _from_chip_shards(s, True)
             for n, s in zip(_SMALL_SHARDED, _unpack(sgath, small_shapes, LANES, _FLAT_PART, _FLAT_PART))}

    o_a = 4 * DNW
    o_xl = o_a + 4 * H
    o_mg = o_xl + 2 * LW
    wi_ = _from_chip_shards(full["w_in"], True)
    nj = LW // LC
    lru_cols = jnp.stack([wi_[:, o_xl:o_xl + LW].reshape(D, nj, LC), wi_[:, o_xl + LW:o_mg].reshape(D, nj, LC)],
                         axis=2).reshape(D, 2 * LW)
    wp = jnp.concatenate([wi_[:, :o_a], lru_cols, wi_[:, o_mg:], wi_[:, o_a:o_xl],
                          jnp.zeros((D, LANES - 4 * H), BF16)], axis=1)
    p_lru, p_mg, p_ab = 4 * DNW, 4 * DNW + 2 * LW, 4 * DNW + 2 * LW + 2 * D
    PW = p_ab + LANES

    MR = LANES
    cond = jnp.concatenate([c, c_ctx[None], jnp.zeros((MR - B - 1, D), F32)], axis=0)
    silu_rows = _rowwise(lambda a: (_silu(a),), [cond], [F32], name="cond_silu")[0]
    mod = _matmul(silu_rows, full["w_ada"], b_shards=(0, _NCHIP), name="ada_fwd") + b_ada
    mx = mod[:B].reshape(B, 6, D)
    mc = mod[B].reshape(6, D)
    zero = jnp.zeros((B, D), F32)
    tab = jnp.stack([jnp.stack([jnp.broadcast_to(mc[0], (B, D)), jnp.broadcast_to(mc[1], (B, D))] + [zero] * 6, axis=1),
                     jnp.stack([mx[:, 0], mx[:, 1]] + [zero] * 6, axis=1)], axis=1)
    vecs = jnp.stack([mx[:, 2], mx[:, 3], mx[:, 4], mx[:, 5]] + [zero] * 4, axis=1)
    gains = jnp.concatenate([g_post_mix, g_pre_ffn, g_post_ffn, jnp.zeros((5, D), F32)], axis=0)

    h = jnp.concatenate([ctx, x], axis=1)
    u = _premix_fwd(h, g_pre_mix, tab, nc=NC, tm=tm)
    p = _matmul(u, wp, name="in_fwd")
    dkw = dict(B=B, T=T, nc=NC, H=H, HD=HD)
    qkv = _dnprep_fwd(p, sfull["dn_conv"], **dkw)
    prm = jnp.concatenate([
        jnp.concatenate([dn_a_log.reshape(1, 2 * H), jnp.zeros((1, LANES - 2 * H), F32)], axis=1),
        jnp.concatenate([dn_dt_bias.reshape(1, 2 * H), jnp.zeros((1, LANES - 2 * H), F32)], axis=1),
        jnp.zeros((6, LANES), F32)], axis=0)
    gtm = _tile(B * T, 512, 16)
    gb = _gb_fwd(p, prm, rows=B * T, col0=p_ab, H=H, tm=gtm)
    y_dn, o_dn, *dn_res = _delta_fwd(qkv, gb, p, dn_onorm, **dkw)
    lv = jnp.concatenate([lru_conv_b, sfull["lru_b_rg"], sfull["lru_b_ig"], sfull["lru_lambda"], jnp.zeros((1, LW), F32)], axis=0)
    wr = _blockdiag(lru_w_rg[0], LC).astype(BF16)
    wi = _blockdiag(lru_w_ig[0], LC).astype(BF16)
    lkw = dict(B=B, T=T, nc=NC, LW=LW, col0=p_lru, C=LC)
    y_lru = _lru_fwd(p, sfull["lru_conv"], lv, wr, wi, **lkw)
    Ydn = _matmul(y_dn, full["w_branch_dn"], name="bdn_fwd")
    Ylru = _matmul(y_lru, full["w_branch_lru"], name="blru_fwd")
    mkw = dict(B=B, T=T, nc=NC, D=D, col0=p_mg, tm=tm)
    mixin = _merge_fwd(p, Ydn, Ylru, b_merge, **mkw)
    mix = _matmul(mixin, full["w_out"], name="out_fwd")
    h1, u2 = _post_fwd(x, mix, gains, vecs, tm=tm)
    F = _matmul(u2, full["w_up"], b_shards=(0, _NCHIP), name="up_fwd")
    w9 = sfull["ffn_dw"]
    ftc = _tile(DFF, 256)
    f = _ffn_act_fwd(F, w9, ffn_dw_b, B=B, N=N, DFF=DFF, tc=ftc)
    dn = _matmul(f, full["w_down"], name="down_fwd")
    ddn, dout, sums_f = _final(h1, dn, loss_target, gains, vecs, tm=tm)

    G = {}
    df = _matmul(ddn, full["w_down"], tb=True, name="down_bwd_x")
    G["w_down"] = _matmul(f, ddn, ta=True, name="down_bwd_w")
    dFg, dFv, dwb = _ffn_act_bwd(F, w9, ffn_dw_b, df, B=B, N=N, DFF=DFF, tc=ftc)
    hs = _NCHIP // 2
    du2 = _matmul(dFg, full["w_up"], tb=True, b_shards=(0, hs), name="up_bwd_xg")
    du2 = _matmul(dFv, full["w_up"], tb=True, b_shards=(hs, hs), add=du2, name="up_bwd_xv")
    G["w_up"] = jnp.concatenate([_matmul(u2, dFg, ta=True, out_shards=hs, name="up_bwd_wg"),
                                 _matmul(u2, dFv, ta=True, out_shards=hs, name="up_bwd_wv")], axis=0)
    dx1, dmix, sums_p = _post_bwd(x, mix, gains, vecs, dout, du2, tm=tm)
    dmixin = _matmul(dmix, full["w_out"], tb=True, name="out_bwd_x")
    G["w_out"] = _matmul(mixin, dmix, ta=True, name="out_bwd_w")
    dp = jnp.zeros((B * T, PW), BF16)
    dYdn, dYlru, dp, sums_m = _merge_bwd(p, Ydn, Ylru, b_merge, dmixin, dp, **mkw)
    dy_dn = _matmul(dYdn, full["w_branch_dn"], tb=True, name="bdn_bwd_x")
    G["w_branch_dn"] = _matmul(y_dn, dYdn, ta=True, name="bdn_bwd_w")
    dy_lru = _matmul(dYlru, full["w_branch_lru"], tb=True, name="blru_bwd_x")
    G["w_branch_lru"] = _matmul(y_lru, dYlru, ta=True, name="blru_bwd_w")
    dp, dcw_l, dlv, dwr, dwi = _lru_bwd(p, sfull["lru_conv"], lv, wr, wi, dy_lru, dp, **lkw)
    dqkv, dgb, dp, don = _delta_bwd(qkv, gb, p, dn_onorm, o_dn, dn_res, dy_dn, dp, **dkw)
    dp, dprm = _gb_bwd(p, prm, dgb, dp, rows=B * T, col0=p_ab, H=H, tm=gtm)
    dp, dcw_d = _dnprep_bwd(p, sfull["dn_conv"], dqkv, dp, **dkw)
    dU = _matmul(dp, wp, tb=True, name="in_bwd_x")
    dwp = _matmul(u, dp, ta=True, name="in_bwd_w")
    grad_x, sums_pm = _premix_bwd(h, g_pre_mix, tab, dU, dx1, nc=NC, tm=tm)
    dlru = dwp[:, p_lru:p_mg].reshape(D, nj, 2, LC)
    G["w_in"] = _to_chip_shards(jnp.concatenate([dwp[:, :o_a], dwp[:, p_ab:p_ab + 4 * H], dlru[:, :, 0].reshape(D, LW),
                                                 dlru[:, :, 1].reshape(D, LW), dwp[:, p_mg:p_ab]], axis=1), True)

    dmod_x = jnp.stack([sums_pm[:, 1, 0], sums_pm[:, 1, 1], sums_p[:, 0], sums_p[:, 1], sums_p[:, 2], sums_f[:, 0]],
                       axis=1).reshape(B, 6 * D)
    dmod_c = jnp.concatenate([sums_pm[:, 0, 0].sum(0), sums_pm[:, 0, 1].sum(0), jnp.zeros((4 * D,), F32)])[None]
    dmod = jnp.concatenate([dmod_x, dmod_c, jnp.zeros((MR - B - 1, 6 * D), F32)], axis=0)
    G["w_ada"] = _matmul(silu_rows, dmod, ta=True, out_shards=_NCHIP, name="ada_bwd_w")
    dsilu = _matmul(dmod, full["w_ada"], tb=True, b_shards=(0, _NCHIP), name="ada_bwd_x")

    g_small = {
        "c_ctx": dsilu[B] * _dsilu(c_ctx),
        "b_ada": dmod[:B + 1].sum(0)[None],
        "g_pre_mix": sums_pm[:, :, 2].sum((0, 1))[None],
        "g_post_mix": sums_p[:, 3].sum(0)[None],
        "g_pre_ffn": sums_p[:, 4].sum(0)[None],
        "g_post_ffn": sums_f[:, 1].sum(0)[None],
        "b_merge": sums_m[0:1],
        "dn_conv": dcw_d[0:4][None],
        "dn_a_log": dprm[0, :2 * H].reshape(1, 2, H),
        "dn_dt_bias": dprm[1, :2 * H].reshape(1, 2, H),
        "dn_onorm": don[:, 0].sum(0)[None],
        "lru_conv": dcw_l[0:4][None],
        "lru_conv_b": dlv[0:1],
        "lru_w_rg": _blockdiag_extract(dwr, LBD)[None],
        "lru_b_rg": dlv[1:3][None],
        "lru_w_ig": _blockdiag_extract(dwi, LBD)[None],
        "lru_b_ig": dlv[3:5][None],
        "lru_lambda": dlv[5:7][None],
        "ffn_dw": dwb[0:9].reshape(1, 3, 3, DFF),
        "ffn_dw_b": dwb[9:10],
    }
    small_names = tuple(n for n in _WEIGHTS if n not in _BIG)
    loss_part = sums_f[:, 2].sum().reshape(1)
    gs_list = [g_small[n] for n in small_names] + [loss_part]
    gs_shapes = [a.shape for a in gs_list]
    gpack = _pack(gs_list, LANES, _FLAT_PART, _FLAT_TOTAL, F32)
    gsum = _sum_lead(_allgather_small(gpack), name="small_sum", tm=512, mult=SUBLANES)
    gs_red = dict(zip(small_names + ("loss",), _unpack(gsum, gs_shapes, LANES, _FLAT_PART, _FLAT_TOTAL)))
    loss = gs_red["loss"][0]

    slabs = [G[n] if _BIG[n] else G[n].reshape(_NCHIP, G[n].shape[0] // _NCHIP, G[n].shape[1]) for n in _BIG]
    c_arr = c_i.astype(jnp.int32).reshape(1)
    from_sibling = _sibling_send_halves(slabs, name="rs_sibling")
    chip_sums = [_half_add(g, r, c_arr, name="rs_add_" + n) for n, g, r in zip(_BIG, slabs, from_sibling)]
    landed = _chip_exchange(chip_sums, name="chip_exchange")
    halves = [_sum_lead(e, name="rs_sum_" + n, tm=512) for n, e in zip(_BIG, landed)]
    sib_halves = _sibling_swap(halves, name="rs_gather")

    grads, deltas, new_m, new_v = {}, {}, {}, {}
    for n, own, sib in zip(_BIG, halves, sib_halves):
        shp = W[n].shape
        outs = _adamw_halves(W[n][0], own, sib, Mo[n][0], Vo[n][0], c_arr, name="adamw_" + n)
        grads[n], deltas[n], new_m[n], new_v[n] = (o.reshape(shp) for o in outs)
    for n in small_names:
        g = gs_red[n]
        if n in _SMALL_SHARDED:
            k = W[n].shape[-1]
            g = lax.dynamic_slice_in_dim(g, s_me * k, k, axis=g.ndim - 1)
        grads[n] = g.reshape(W[n].shape)
    sm_shapes = [W[n].shape for n in small_names]
    pk = lambda d: _pack([d[n] for n in small_names], LANES, _FLAT_PART, _FLAT_TOTAL, F32)
    d_, m_, v_ = _adamw(pk(W), pk(grads), pk(Mo), pk(Vo), name="adamw_small")
    for dst, pool_ in ((deltas, d_), (new_m, m_), (new_v, v_)):
        dst.update(zip(small_names, _unpack(pool_, sm_shapes, LANES, _FLAT_PART, _FLAT_TOTAL)))
    return (loss, grad_x, *[grads[n] for n in _WEIGHTS], *[deltas[n] for n in _WEIGHTS],
            *[new_m[n] for n in _WEIGHTS], *[new_v[n] for n in _WEIGHTS])
```

```python
import functools
import math

import jax
import jax.numpy as jnp
from jax import lax
from jax.experimental import pallas as pl
from jax.experimental.pallas import tpu as pltpu

F32 = jnp.float32
BF16 = jnp.bfloat16
EPS = 1e-6
GRID_W = 64
CHUNK = 128
LRU_C = 8.0
LANES = 128
SUBLANES = 8
VMEM_LIMIT = 56 * 1024 * 1024
ADAM_LR, ADAM_B1, ADAM_B2, ADAM_EPS, ADAM_WD, ADAM_STEP = 0.001, 0.9, 0.999, 1e-08, 0.01, 10
MESH = pl.DeviceIdType.MESH


def _tile(n, target, mult=LANES):
    best = None
    for t in range(mult, min(n, target) + 1, mult):
        if n % t == 0:
            best = t
    return best if best is not None else n


def _params(sem=None, **kw):
    return pltpu.CompilerParams(dimension_semantics=sem, vmem_limit_bytes=VMEM_LIMIT, **kw)


def _sigmoid(x):
    return 1.0 / (1.0 + jnp.exp(-x))


def _silu(x):
    return x * _sigmoid(x)


def _softplus(x):
    return jnp.maximum(x, 0.0) + jnp.log(1.0 + jnp.exp(-jnp.abs(x)))


def _gelu(x):
    return 0.5 * x * (1.0 + jnp.tanh(math.sqrt(2.0 / math.pi) * (x + 0.044715 * x * x * x)))


def _rmsn(u, gain):
    return u * lax.rsqrt(jnp.mean(u * u, axis=-1, keepdims=True) + EPS) * gain


_MM_VMEM = 40 * 1024 * 1024


def _matmul(a, b, *, ta=False, tb=False, add=None, b_shards=None, out_shards=None, out_dtype=F32, name,
            tm=1024, tn=2048, tk=1024):
    (K, M) = a.shape if ta else a.shape[::-1]
    if b_shards is not None:
        s0, ns = b_shards
        bsh = (b.shape[1], ns * b.shape[2])
        nsh = b.shape[2]
    else:
        bsh = b.shape
    N = bsh[0] if tb else bsh[1]
    assert (bsh[1] if tb else bsh[0]) == K, (a.shape, b.shape, ta, tb)
    tm = _tile(M, tm)
    tk = _tile(nsh if (b_shards is not None and tb) else K, tk)
    nlim = nsh if (b_shards is not None and not tb) else (N // out_shards if out_shards else N)
    osz = jnp.dtype(out_dtype).itemsize + (4 if add is not None else 0)
    while True:
        tn_ = _tile(nlim, tn)
        need = 2 * (tm * tk * a.dtype.itemsize + tk * tn_ * b.dtype.itemsize + tm * tn_ * osz) + 4 * tm * tn_
        if need <= _MM_VMEM or tn <= LANES:
            break
        tn //= 2
    tn = tn_
    nk = K // tk
    dims = (((0 if ta else 1,), (1 if tb else 0,)), ((), ()))

    def body(a_ref, b_ref, *rest):
        (c_ref, o_ref, acc_ref) = rest if add is not None else (None, *rest)
        k = pl.program_id(2)

        @pl.when(k == 0)
        def _():
            acc_ref[...] = jnp.zeros_like(acc_ref) if c_ref is None else c_ref[...]

        bv = b_ref[0] if b_shards is not None else b_ref[...]
        acc_ref[...] += lax.dot_general(a_ref[...].astype(BF16), bv.astype(BF16), dims, preferred_element_type=F32)

        @pl.when(k == nk - 1)
        def _():
            if out_shards:
                o_ref[0] = acc_ref[...].astype(out_dtype)
            else:
                o_ref[...] = acc_ref[...].astype(out_dtype)

    a_spec = pl.BlockSpec((tk, tm), lambda i, j, k: (k, i)) if ta else pl.BlockSpec((tm, tk), lambda i, j, k: (i, k))
    if b_shards is None:
        b_spec = pl.BlockSpec((tn, tk), lambda i, j, k: (j, k)) if tb else pl.BlockSpec((tk, tn), lambda i, j, k: (k, j))
    elif tb:
        per = nsh // tk
        b_spec = pl.BlockSpec((1, tn, tk), lambda i, j, k: (s0 + k // per, j, k % per))
    else:
        per = nsh // tn
        b_spec = pl.BlockSpec((1, tk, tn), lambda i, j, k: (s0 + j // per, k, j % per))
    o_spec = pl.BlockSpec((tm, tn), lambda i, j, k: (i, j))
    if out_shards:
        oper = N // out_shards // tn
        out_spec = pl.BlockSpec((1, tm, tn), lambda i, j, k: (j // oper, i, j % oper))
        out_shape = jax.ShapeDtypeStruct((out_shards, M, N // out_shards), out_dtype)
    else:
        out_spec, out_shape = o_spec, jax.ShapeDtypeStruct((M, N), out_dtype)
    return pl.pallas_call(
        body, name=name, grid=(M // tm, N // tn, nk),
        in_specs=[a_spec, b_spec] + ([o_spec] if add is not None else []),
        out_specs=out_spec, out_shape=out_shape,
        scratch_shapes=[pltpu.VMEM((tm, tn), F32)],
        compiler_params=_params(("parallel", "parallel", "arbitrary")),
    )(*((a, b) + ((add,) if add is not None else ())))


def _premix_math(h, gain, shift, scale):
    return _rmsn(h, gain) * (1.0 + scale) + shift


def _premix_fwd(h, gain, tab, *, nc, tm):
    B, T, D = h.shape
    nt, nct = T // tm, nc // tm

    def body(h_ref, g_ref, tab_ref, u_ref):
        tabv = tab_ref[0, 0]
        u_ref[...] = _premix_math(h_ref[0], g_ref[...], tabv[0:1], tabv[1:2]).astype(BF16)

    return pl.pallas_call(
        body, name="premix_fwd", grid=(B, nt),
        in_specs=[pl.BlockSpec((1, tm, D), lambda b, t: (b, t, 0)),
                  pl.BlockSpec((1, D), lambda b, t: (0, 0)),
                  pl.BlockSpec((1, 1, 8, D), lambda b, t: (b, jnp.where(t < nct, 0, 1), 0, 0))],
        out_specs=pl.BlockSpec((tm, D), lambda b, t: (b * nt + t, 0)),
        out_shape=jax.ShapeDtypeStruct((B * T, D), BF16),
        compiler_params=_params(("parallel", "parallel")),
    )(h, gain, tab)


def _premix_bwd(h, gain, tab, du, dres, *, nc, tm):
    B, T, D = h.shape
    nt, nct = T // tm, nc // tm
    N = T - nc

    def body(h_ref, g_ref, tab_ref, du_ref, dres_ref, dx_ref, sums_ref):
        t = pl.program_id(1)
        tabv = tab_ref[0, 0]
        _, vjp = jax.vjp(_premix_math, h_ref[0], g_ref[...], tabv[0:1], tabv[1:2])
        dh, dgain, dshift, dscale = vjp(du_ref[...].astype(F32))

        @pl.when((t == 0) | (t == nct))
        def _():
            sums_ref[...] = jnp.zeros_like(sums_ref)

        sums_ref[0, 0, 0:1, :] += dshift
        sums_ref[0, 0, 1:2, :] += dscale
        sums_ref[0, 0, 2:3, :] += dgain

        @pl.when(t >= nct)
        def _():
            dx_ref[0] = dres_ref[...] + dh

    lat = lambda b, t: jnp.maximum(t - nct, 0)
    return pl.pallas_call(
        body, name="premix_bwd", grid=(B, nt),
        in_specs=[pl.BlockSpec((1, tm, D), lambda b, t: (b, t, 0)),
                  pl.BlockSpec((1, D), lambda b, t: (0, 0)),
                  pl.BlockSpec((1, 1, 8, D), lambda b, t: (b, jnp.where(t < nct, 0, 1), 0, 0)),
                  pl.BlockSpec((tm, D), lambda b, t: (b * nt + t, 0)),
                  pl.BlockSpec((tm, D), lambda b, t: (b * (nt - nct) + lat(b, t), 0))],
        out_specs=[pl.BlockSpec((1, tm, D), lambda b, t: (b, lat(b, t), 0)),
                   pl.BlockSpec((1, 1, 8, D), lambda b, t: (b, jnp.where(t < nct, 0, 1), 0, 0))],
        out_shape=[jax.ShapeDtypeStruct((B, N, D), F32), jax.ShapeDtypeStruct((B, 2, 8, D), F32)],
        compiler_params=_params(("parallel", "arbitrary")),
    )(h, gain, tab, du, dres)


def _merge_math(mgd, mgl, yd, yl, bd, bl):
    return _sigmoid(mgd + bd) * yd + _sigmoid(mgl + bl) * yl


def _merge_fwd(p, ydn, ylru, b_merge, *, B, T, nc, D, col0, tm):
    N = T - nc
    ntl, nt, nct, cb = N // tm, T // tm, nc // tm, col0 // D

    def body(mgd_ref, mgl_ref, yd_ref, yl_ref, bm_ref, o_ref):
        o_ref[...] = _merge_math(mgd_ref[...], mgl_ref[...], yd_ref[...], yl_ref[...],
                                 bm_ref[:, 0:D], bm_ref[:, D:2 * D]).astype(BF16)

    prow = lambda b, t: b * nt + nct + t
    return pl.pallas_call(
        body, name="merge_fwd", grid=(B, ntl),
        in_specs=[pl.BlockSpec((tm, D), lambda b, t: (prow(b, t), cb)),
                  pl.BlockSpec((tm, D), lambda b, t: (prow(b, t), cb + 1)),
                  pl.BlockSpec((tm, D), lambda b, t: (b * ntl + t, 0)),
                  pl.BlockSpec((tm, D), lambda b, t: (b * ntl + t, 0)),
                  pl.BlockSpec((1, 2 * D), lambda b, t: (0, 0))],
        out_specs=pl.BlockSpec((tm, D), lambda b, t: (b * ntl + t, 0)),
        out_shape=jax.ShapeDtypeStruct((B * N, D), BF16),
        compiler_params=_params(("parallel", "parallel")),
    )(p, p, ydn, ylru, b_merge)


def _merge_bwd(p, ydn, ylru, b_merge, dmix, dp, *, B, T, nc, D, col0, tm):
    N = T - nc
    ntl, nt, nct, cb = N // tm, T // tm, nc // tm, col0 // D
    assert col0 % (2 * D) == 0

    def body(mgd_ref, mgl_ref, yd_ref, yl_ref, bm_ref, dm_ref, dp_any, dyd_ref, dyl_ref, dp_ref, sums_ref):
        _, vjp = jax.vjp(_merge_math, mgd_ref[...], mgl_ref[...], yd_ref[...], yl_ref[...],
                         bm_ref[:, 0:D], bm_ref[:, D:2 * D])
        dmgd, dmgl, dyd, dyl, dbd, dbl = vjp(dm_ref[...])
        dyd_ref[...] = dyd.astype(BF16)
        dyl_ref[...] = dyl.astype(BF16)
        dp_ref[:, 0:D] = dmgd.astype(BF16)
        dp_ref[:, D:2 * D] = dmgl.astype(BF16)

        @pl.when((pl.program_id(0) == 0) & (pl.program_id(1) == 0))
        def _():
            sums_ref[...] = jnp.zeros_like(sums_ref)

        sums_ref[0:1, 0:D] += dbd
        sums_ref[0:1, D:2 * D] += dbl

    prow = lambda b, t: b * nt + nct + t
    row = pl.BlockSpec((tm, D), lambda b, t: (b * ntl + t, 0))
    return pl.pallas_call(
        body, name="merge_bwd", grid=(B, ntl),
        in_specs=[pl.BlockSpec((tm, D), lambda b, t: (prow(b, t), cb)),
                  pl.BlockSpec((tm, D), lambda b, t: (prow(b, t), cb + 1)),
                  row, row, pl.BlockSpec((1, 2 * D), lambda b, t: (0, 0)), row,
                  pl.BlockSpec(memory_space=pl.ANY)],
        out_specs=[row, row,
                   pl.BlockSpec((tm, 2 * D), lambda b, t: (prow(b, t), cb // 2)),
                   pl.BlockSpec((8, 2 * D), lambda b, t: (0, 0))],
        out_shape=[jax.ShapeDtypeStruct((B * N, D), BF16), jax.ShapeDtypeStruct((B * N, D), BF16),
                   jax.ShapeDtypeStruct(dp.shape, dp.dtype), jax.ShapeDtypeStruct((8, 2 * D), F32)],
        input_output_aliases={6: 2},
        compiler_params=_params(("arbitrary", "arbitrary")),
    )(p, p, ydn, ylru, b_merge, dmix, dp)


def _post_math(x, mix, g1, gate, g2, sh, sc):
    h1 = x + _rmsn(mix, g1) * gate
    return h1, _rmsn(h1, g2) * (1.0 + sc) + sh


def _post_fwd(x, mix, gains, vecs, *, tm):
    B, N, D = x.shape
    ntl = N // tm

    def body(x_ref, mix_ref, g_ref, v_ref, h1_ref, u2_ref):
        v = v_ref[0]
        h1, u2 = _post_math(x_ref[0], mix_ref[...], g_ref[0:1], v[0:1], g_ref[1:2], v[1:2], v[2:3])
        h1_ref[...] = h1
        u2_ref[...] = u2.astype(BF16)

    row = pl.BlockSpec((tm, D), lambda b, t: (b * ntl + t, 0))
    return pl.pallas_call(
        body, name="post_fwd", grid=(B, ntl),
        in_specs=[pl.BlockSpec((1, tm, D), lambda b, t: (b, t, 0)), row,
                  pl.BlockSpec((8, D), lambda b, t: (0, 0)), pl.BlockSpec((1, 8, D), lambda b, t: (b, 0, 0))],
        out_specs=[row, row],
        out_shape=[jax.ShapeDtypeStruct((B * N, D), F32), jax.ShapeDtypeStruct((B * N, D), BF16)],
        compiler_params=_params(("parallel", "parallel")),
    )(x, mix, gains, vecs)


def _post_bwd(x, mix, gains, vecs, dh1, du2, *, tm):
    B, N, D = x.shape
    ntl = N // tm

    def body(x_ref, mix_ref, g_ref, v_ref, dh1_ref, du2_ref, dx_ref, dmix_ref, sums_ref):
        v = v_ref[0]
        _, vjp = jax.vjp(_post_math, x_ref[0], mix_ref[...], g_ref[0:1], v[0:1], g_ref[1:2], v[1:2], v[2:3])
        dx, dmix, dg1, dgate, dg2, dsh, dsc = vjp((dh1_ref[...], du2_ref[...]))
        dx_ref[...] = dx
        dmix_ref[...] = dmix.astype(BF16)

        @pl.when(pl.program_id(1) == 0)
        def _():
            sums_ref[...] = jnp.zeros_like(sums_ref)

        sums_ref[0, 0:1, :] += dgate
        sums_ref[0, 1:2, :] += dsh
        sums_ref[0, 2:3, :] += dsc
        sums_ref[0, 3:4, :] += dg1
        sums_ref[0, 4:5, :] += dg2

    row = pl.BlockSpec((tm, D), lambda b, t: (b * ntl + t, 0))
    return pl.pallas_call(
        body, name="post_bwd", grid=(B, ntl),
        in_specs=[pl.BlockSpec((1, tm, D), lambda b, t: (b, t, 0)), row,
                  pl.BlockSpec((8, D), lambda b, t: (0, 0)), pl.BlockSpec((1, 8, D), lambda b, t: (b, 0, 0)), row, row],
        out_specs=[row, row, pl.BlockSpec((1, 8, D), lambda b, t: (b, 0, 0))],
        out_shape=[jax.ShapeDtypeStruct((B * N, D), F32), jax.ShapeDtypeStruct((B * N, D), BF16),
                   jax.ShapeDtypeStruct((B, 8, D), F32)],
        compiler_params=_params(("parallel", "arbitrary")),
    )(x, mix, gains, vecs, dh1, du2)


def _final_math(dn, g4, gate5):
    return _rmsn(dn, g4) * gate5


def _final(h1, dn, target, gains, vecs, *, tm):
    B, N, D = target.shape
    ntl = N // tm

    def body(h1_ref, dn_ref, t_ref, g_ref, v_ref, ddn_ref, dout_ref, sums_ref):
        v = v_ref[0]
        y, vjp = jax.vjp(_final_math, dn_ref[...], g_ref[2:3], v[3:4])
        err = h1_ref[...] + y - t_ref[0]
        dout = err * (1.0 / D)
        ddn, dg4, dgate5 = vjp(dout)
        ddn_ref[...] = ddn.astype(BF16)
        dout_ref[...] = dout

        @pl.when(pl.program_id(1) == 0)
        def _():
            sums_ref[...] = jnp.zeros_like(sums_ref)

        sums_ref[0, 0:1, :] += dgate5
        sums_ref[0, 1:2, :] += dg4
        sums_ref[0, 2:3, :] += jnp.sum(err * err, axis=0, keepdims=True) * (0.5 / D)

    row = pl.BlockSpec((tm, D), lambda b, t: (b * ntl + t, 0))
    return pl.pallas_call(
        body, name="final", grid=(B, ntl),
        in_specs=[row, row, pl.BlockSpec((1, tm, D), lambda b, t: (b, t, 0)),
                  pl.BlockSpec((8, D), lambda b, t: (0, 0)), pl.BlockSpec((1, 8, D), lambda b, t: (b, 0, 0))],
        out_specs=[row, row, pl.BlockSpec((1, 8, D), lambda b, t: (b, 0, 0))],
        out_shape=[jax.ShapeDtypeStruct((B * N, D), BF16), jax.ShapeDtypeStruct((B * N, D), F32),
                   jax.ShapeDtypeStruct((B, 8, D), F32)],
        compiler_params=_params(("parallel", "arbitrary")),
    )(h1, dn, target, gains, vecs)


def _shift(x, s):
    s = s % x.shape[0]
    return x if s == 0 else pltpu.roll(x, s, 0)


def _seg_taps(T, nc, width, pad_left):
    t = lax.broadcasted_iota(jnp.int32, (T, 1), 0)
    pos = jnp.where(t < nc, t, t - nc)
    seg = jnp.where(t < nc, nc, T - nc)
    taps = []
    for k in range(width):
        src = pos + (k - pad_left)
        taps.append((pad_left - k, (src >= 0) & (src < seg)))
    return taps


def _grid_taps(N):
    t = lax.broadcasted_iota(jnp.int32, (N, 1), 0)
    wcol = t % GRID_W
    taps = []
    for dr in (-1, 0, 1):
        for dw in (-1, 0, 1):
            off = dr * GRID_W + dw
            ok = (wcol + dw >= 0) & (wcol + dw < GRID_W) & (t + dr * GRID_W >= 0) & (t + dr * GRID_W < N)
            taps.append((-off, ok))
    return taps


def _conv_fwd(x, w, taps):
    y = jnp.zeros_like(x)
    for k, (s, m) in enumerate(taps):
        y = y + w[k:k + 1] * jnp.where(m, _shift(x, s), 0.0)
    return y


def _conv_bwd(x, w, taps, dy):
    dx = jnp.zeros_like(x)
    dws = []
    for k, (s, m) in enumerate(taps):
        dym = jnp.where(m, dy, 0.0)
        dx = dx + w[k:k + 1] * _shift(dym, -s)
        dws.append(jnp.sum(dym * _shift(x, s), axis=0, keepdims=True))
    return dx, jnp.concatenate(dws, axis=0)


def _ffn_act_fwd(F, w9, bias, *, B, N, DFF, tc):
    nj = DFF // tc

    def body(fg_ref, fv_ref, w_ref, b_ref, o_ref):
        fg = _conv_fwd(fg_ref[...], w_ref[...], _grid_taps(N)) + b_ref[...]
        o_ref[...] = (_gelu(fg) * fv_ref[...]).astype(BF16)

    return pl.pallas_call(
        body, name="ffn_act_fwd", grid=(B, nj),
        in_specs=[pl.BlockSpec((N, tc), lambda b, j: (b, j)), pl.BlockSpec((N, tc), lambda b, j: (b, nj + j)),
                  pl.BlockSpec((9, tc), lambda b, j: (0, j)), pl.BlockSpec((1, tc), lambda b, j: (0, j))],
        out_specs=pl.BlockSpec((N, tc), lambda b, j: (b, j)),
        out_shape=jax.ShapeDtypeStruct((B * N, DFF), BF16),
        compiler_params=_params(("parallel", "parallel")),
    )(F, F, w9, bias)


def _ffn_act_bwd(F, w9, bias, df, *, B, N, DFF, tc):
    nj = DFF // tc

    def body(fg_ref, fv_ref, w_ref, b_ref, df_ref, dfg_ref, dfv_ref, dwb_ref):
        taps = _grid_taps(N)
        x = fg_ref[...]
        fg, vjp = jax.vjp(lambda a: _gelu(a), _conv_fwd(x, w_ref[...], taps) + b_ref[...])
        dfl = df_ref[...]
        dfv_ref[...] = (dfl * fg).astype(BF16)
        (dpre,) = vjp(dfl * fv_ref[...])
        dx, dw = _conv_bwd(x, w_ref[...], taps, dpre)
        dfg_ref[...] = dx.astype(BF16)

        @pl.when(pl.program_id(1) == 0)
        def _():
            dwb_ref[...] = jnp.zeros_like(dwb_ref)

        dwb_ref[0:9, :] += dw
        dwb_ref[9:10, :] += jnp.sum(dpre, axis=0, keepdims=True)

    col = pl.BlockSpec((N, tc), lambda j, b: (b, j))
    return pl.pallas_call(
        body, name="ffn_act_bwd", grid=(nj, B),
        in_specs=[col, pl.BlockSpec((N, tc), lambda j, b: (b, nj + j)),
                  pl.BlockSpec((9, tc), lambda j, b: (0, j)), pl.BlockSpec((1, tc), lambda j, b: (0, j)), col],
        out_specs=[col, col, pl.BlockSpec((16, tc), lambda j, b: (0, j))],
        out_shape=[jax.ShapeDtypeStruct((B * N, DFF), BF16), jax.ShapeDtypeStruct((B * N, DFF), BF16),
                   jax.ShapeDtypeStruct((16, DFF), F32)],
        compiler_params=_params(("parallel", "arbitrary")),
    )(F, F, w9, bias, df)


def _dnprep_math(y, is_qk, scale):
    s = _silu(y)
    n = s * lax.rsqrt(jnp.sum(s * s, axis=-1, keepdims=True) + EPS) * scale
    return jnp.where(is_qk, n, s)


def _dnprep_fwd(p, cw, *, B, T, nc, H, HD):
    def body(x_ref, w_ref, o_ref):
        j = pl.program_id(1)
        y = _conv_fwd(x_ref[...], w_ref[...], _seg_taps(T, nc, 4, 2))
        o_ref[...] = _dnprep_math(y, j < 2 * H, jnp.where(j < H, HD ** -0.5, 1.0))

    return pl.pallas_call(
        body, name="dnprep_fwd", grid=(B, 3 * H),
        in_specs=[pl.BlockSpec((T, HD), lambda b, j: (b, j)), pl.BlockSpec((4, HD), lambda b, j: (0, j))],
        out_specs=pl.BlockSpec((T, HD), lambda b, j: (b, j)),
        out_shape=jax.ShapeDtypeStruct((B * T, 3 * H * HD), F32),
        compiler_params=_params(("parallel", "parallel")),
    )(p, cw)


def _dnprep_bwd(p, cw, dqkv, dp, *, B, T, nc, H, HD):
    def body(x_ref, w_ref, d_ref, dp_any, dp_ref, dcw_ref):
        j = pl.program_id(0)
        taps = _seg_taps(T, nc, 4, 2)
        x = x_ref[...]
        y = _conv_fwd(x, w_ref[...], taps)
        is_qk, scale = j < 2 * H, jnp.where(j < H, HD ** -0.5, 1.0)
        _, vjp = jax.vjp(lambda a: _dnprep_math(a, is_qk, scale), y)
        (dy,) = vjp(d_ref[0])
        dx, dw = _conv_bwd(x, w_ref[...], taps, dy)
        dp_ref[...] = dx.astype(BF16)

        @pl.when(pl.program_id(1) == 0)
        def _():
            dcw_ref[...] = jnp.zeros_like(dcw_ref)

        dcw_ref[0:4, :] += dw

    col = pl.BlockSpec((T, HD), lambda j, b: (b, j))
    return pl.pallas_call(
        body, name="dnprep_bwd", grid=(3 * H, B),
        in_specs=[col, pl.BlockSpec((4, HD), lambda j, b: (0, j)),
                  pl.BlockSpec((1, T, HD), lambda j, b: (j // H, b, j % H)), pl.BlockSpec(memory_space=pl.ANY)],
        out_specs=[col, pl.BlockSpec((8, HD), lambda j, b: (0, j))],
        out_shape=[jax.ShapeDtypeStruct(dp.shape, dp.dtype), jax.ShapeDtypeStruct((8, 3 * H * HD), F32)],
        input_output_aliases={3: 0},
        compiler_params=_params(("parallel", "arbitrary")),
    )(p, cw, dqkv, dp)


def _gb_math(ab, alog, dtb, H):
    lane = lax.broadcasted_iota(jnp.int32, ab.shape, 1)
    g = -jnp.exp(alog) * _softplus(ab + dtb)
    return jnp.where(lane < 2 * H, g, jnp.where(lane < 4 * H, _sigmoid(ab), 0.0))


def _gb_fwd(p, prm, *, rows, col0, H, tm):
    def body(x_ref, prm_ref, o_ref):
        o_ref[...] = _gb_math(x_ref[...], prm_ref[0:1], prm_ref[1:2], H)

    return pl.pallas_call(
        body, name="gb_fwd", grid=(rows // tm,),
        in_specs=[pl.BlockSpec((tm, LANES), lambda t: (t, col0 // LANES)), pl.BlockSpec((8, LANES), lambda t: (0, 0))],
        out_specs=pl.BlockSpec((tm, LANES), lambda t: (t, 0)),
        out_shape=jax.ShapeDtypeStruct((rows, LANES), F32),
        compiler_params=_params(("parallel",)),
    )(p, prm)


def _gb_bwd(p, prm, dgb, dp, *, rows, col0, H, tm):
    def body(x_ref, prm_ref, d_ref, dp_any, dp_ref, dprm_ref):
        _, vjp = jax.vjp(lambda a, b, c: _gb_math(a, b, c, H), x_ref[...], prm_ref[0:1], prm_ref[1:2])
        dab, dalog, ddtb = vjp(d_ref[...])
        dp_ref[...] = dab.astype(BF16)

        @pl.when(pl.program_id(0) == 0)
        def _():
            dprm_ref[...] = jnp.zeros_like(dprm_ref)

        dprm_ref[0:1, :] += dalog
        dprm_ref[1:2, :] += ddtb

    blk = pl.BlockSpec((tm, LANES), lambda t: (t, col0 // LANES))
    return pl.pallas_call(
        body, name="gb_bwd", grid=(rows // tm,),
        in_specs=[blk, pl.BlockSpec((8, LANES), lambda t: (0, 0)), pl.BlockSpec((tm, LANES), lambda t: (t, 0)),
                  pl.BlockSpec(memory_space=pl.ANY)],
        out_specs=[blk, pl.BlockSpec((8, LANES), lambda t: (0, 0))],
        out_shape=[jax.ShapeDtypeStruct(dp.shape, dp.dtype), jax.ShapeDtypeStruct((8, LANES), F32)],
        input_output_aliases={3: 0},
        compiler_params=_params(("arbitrary",)),
    )(p, prm, dgb, dp)


def _lru_scans(scans):
    C = scans[0][0].shape[1]
    row = lax.broadcasted_iota(jnp.int32, (SUBLANES, C), 0)
    carries = tuple(jnp.zeros((1, C), F32) for _ in scans)
    for si in range(len(scans[0][4])):
        nb = scans[0][4][si][1] // SUBLANES
        assert all(sc[4][si][1] // SUBLANES == nb for sc in scans)

        def blk(i, carries, si=si, nb=nb):
            out = []
            for (a_ref, b_ref, h_ref, hp_ref, segs), carry in zip(scans, carries):
                start, _, reverse = segs[si]
                r0 = pl.multiple_of(start + (nb - 1 - i if reverse else i) * SUBLANES, SUBLANES)
                A = a_ref[pl.ds(r0, SUBLANES), :]
                Bv = b_ref[pl.ds(r0, SUBLANES), :]
                for s in (1, 2, 4):
                    sh = SUBLANES - s if reverse else s
                    m = (row < SUBLANES - s) if reverse else (row >= s)
                    Bv = jnp.where(m, A * pltpu.roll(Bv, sh, 0) + Bv, Bv)
                    A = jnp.where(m, A * pltpu.roll(A, sh, 0), A)
                Hv = Bv + A * carry
                h_ref[pl.ds(r0, SUBLANES), :] = Hv
                if hp_ref is not None:
                    if reverse:
                        hp = jnp.where(row < SUBLANES - 1, pltpu.roll(Hv, SUBLANES - 1, 0), carry)
                    else:
                        hp = jnp.where(row >= 1, pltpu.roll(Hv, 1, 0), carry)
                    hp_ref[pl.ds(r0, SUBLANES), :] = hp
                out.append(Hv[0:1] if reverse else Hv[SUBLANES - 1:SUBLANES])
            return tuple(out)

        carries = lax.fori_loop(0, nb, blk, carries)


def _lru_orders(T, nc, d):
    N = T - nc
    if d == 0:
        return [(0, nc, False), (nc, N, False)], [(nc, N, True), (0, nc, True)]
    return [(0, nc, True), (nc, N, True)], [(nc, N, False), (0, nc, False)]


def _bdot(a, b, dims=(((1,), (0,)), ((), ()))):
    return lax.dot_general(a.astype(BF16), b.astype(BF16), dims, preferred_element_type=F32)


_NT = (((1,), (1,)), ((), ()))
_TN = (((0,), (0,)), ((), ()))


def _blockdiag(w, C):
    nd, nb, bd, _ = w.shape
    per = C // bd
    out = jnp.einsum('dnpij,pq->dnpiqj', w.reshape(nd, nb // per, per, bd, bd), jnp.eye(per, dtype=w.dtype))
    return out.reshape(nd, nb // per, C, C)


def _blockdiag_extract(dw, bd):
    nd, nj, C, _ = dw.shape
    per = C // bd
    out = jnp.einsum('dnpiqj,pq->dnpij', dw.reshape(nd, nj, per, bd, per, bd), jnp.eye(per, dtype=dw.dtype))
    return out.reshape(nd, nj * per, bd, bd)


def _lru_fwd(p, cw, lv, wr, wi, *, B, T, nc, LW, col0, C):
    N = T - nc
    nj = LW // C

    def body(x_ref, cw_ref, lv_ref, wr_ref, wi_ref, o_ref, a_s, b_s, h_s):
        lv_ = lv_ref[...]
        xc = _conv_fwd(x_ref[:, 0:C], cw_ref[...], _seg_taps(T, nc, 4, 2)) + lv_[0:1]
        for d in (0, 1):
            r = _sigmoid(_bdot(xc, wr_ref[d, 0]) + lv_[1 + d:2 + d])
            i = _sigmoid(_bdot(xc, wi_ref[d, 0]) + lv_[3 + d:4 + d])
            la = -LRU_C * r * _softplus(-lv_[5 + d:6 + d])
            a_s[d] = jnp.exp(la)
            b_s[d] = jnp.sqrt(1.0 - jnp.exp(2.0 * la)) * i * xc
        _lru_scans([(a_s.at[d], b_s.at[d], h_s.at[d], None, _lru_orders(T, nc, d)[0]) for d in (0, 1)])
        o_ref[...] = ((h_s[0, nc:, :] + h_s[1, nc:, :]) * _gelu(x_ref[nc:, C:2 * C])).astype(BF16)

    return pl.pallas_call(
        body, name="lru_fwd", grid=(B, nj),
        in_specs=[pl.BlockSpec((T, 2 * C), lambda b, j: (b, col0 // (2 * C) + j)),
                  pl.BlockSpec((4, C), lambda b, j: (0, j)), pl.BlockSpec((8, C), lambda b, j: (0, j)),
                  pl.BlockSpec((2, 1, C, C), lambda b, j: (0, j, 0, 0)), pl.BlockSpec((2, 1, C, C), lambda b, j: (0, j, 0, 0))],
        out_specs=pl.BlockSpec((N, C), lambda b, j: (b, j)),
        out_shape=jax.ShapeDtypeStruct((B * N, LW), BF16),
        scratch_shapes=[pltpu.VMEM((2, T, C), F32)] * 3,
        compiler_params=_params(("parallel", "parallel")),
    )(p, cw, lv, wr, wi)


def _lru_bwd(p, cw, lv, wr, wi, dy, dp, *, B, T, nc, LW, col0, C):
    N = T - nc
    nj = LW // C

    def body(x_ref, cw_ref, lv_ref, wr_ref, wi_ref, dy_ref, dp_any, dp_ref, dcw_ref, dlv_ref, dwr_ref, dwi_ref,
             a_s, b_s, h_s, hp_s, mu_s, mup_s, dh_s, dxc_s):
        taps = _seg_taps(T, nc, 4, 2)
        lv_ = lv_ref[...]
        xl = x_ref[:, 0:C]
        xc = _conv_fwd(xl, cw_ref[...], taps) + lv_[0:1]
        gel, gelu_vjp = jax.vjp(_gelu, x_ref[nc:, C:2 * C])
        dh_s[0:nc, :] = jnp.zeros((nc, C), F32)
        dh_s[nc:, :] = dy_ref[...] * gel
        dxc_s[...] = jnp.zeros_like(dxc_s)

        @pl.when(pl.program_id(1) == 0)
        def _():
            dcw_ref[...] = jnp.zeros_like(dcw_ref)
            dlv_ref[...] = jnp.zeros_like(dlv_ref)
            dwr_ref[...] = jnp.zeros_like(dwr_ref)
            dwi_ref[...] = jnp.zeros_like(dwi_ref)

        def gates(d):
            lam = lv_[5 + d:6 + d]
            r = _sigmoid(_bdot(xc, wr_ref[d, 0]) + lv_[1 + d:2 + d])
            i = _sigmoid(_bdot(xc, wi_ref[d, 0]) + lv_[3 + d:4 + d])
            sp = _softplus(-lam)
            la = -LRU_C * r * sp
            e2 = jnp.exp(2.0 * la)
            return lam, r, i, sp, la, e2, jnp.sqrt(1.0 - e2)

        for d in (0, 1):
            _, _, i, _, la, _, mult = gates(d)
            a_s[d] = jnp.exp(la)
            b_s[d] = mult * i * xc
        _lru_scans([(a_s.at[d], b_s.at[d], h_s.at[d], hp_s.at[d], _lru_orders(T, nc, d)[0]) for d in (0, 1)])
        for d in (0, 1):
            b_s[d] = a_s[d] * dh_s[...]
        _lru_scans([(a_s.at[d], b_s.at[d], mu_s.at[d], mup_s.at[d], _lru_orders(T, nc, d)[1]) for d in (0, 1)])

        for d in (0, 1):
            lam, r, i, sp, la, e2, mult = gates(d)
            a = a_s[d]
            dinp = dh_s[...] + mup_s[d]
            da = dinp * hp_s[d]
            dmult = dinp * i * xc
            di = dinp * mult * xc
            dla = da * a - dmult * e2 / mult
            dpre_r = (dla * (-LRU_C * sp)) * r * (1.0 - r)
            dpre_i = di * i * (1.0 - i)
            dsp = jnp.sum(dla * (-LRU_C * r), axis=0, keepdims=True)
            dxc_s[...] += dinp * mult * i + _bdot(dpre_r, wr_ref[d, 0], _NT) + _bdot(dpre_i, wi_ref[d, 0], _NT)
            dwr_ref[d, 0] += _bdot(xc, dpre_r, _TN)
            dwi_ref[d, 0] += _bdot(xc, dpre_i, _TN)
            dlv_ref[1 + d:2 + d, :] += jnp.sum(dpre_r, axis=0, keepdims=True)
            dlv_ref[3 + d:4 + d, :] += jnp.sum(dpre_i, axis=0, keepdims=True)
            dlv_ref[5 + d:6 + d, :] += -dsp * _sigmoid(-lam)

        dxc = dxc_s[...]
        dxl, dw = _conv_bwd(xl, cw_ref[...], taps, dxc)
        dcw_ref[0:4, :] += dw
        dlv_ref[0:1, :] += jnp.sum(dxc, axis=0, keepdims=True)
        dp_ref[:, 0:C] = dxl.astype(BF16)
        (dyl,) = gelu_vjp(dy_ref[...] * (h_s[0, nc:, :] + h_s[1, nc:, :]))
        dp_ref[0:nc, C:2 * C] = jnp.zeros((nc, C), BF16)
        dp_ref[nc:, C:2 * C] = dyl.astype(BF16)

    xblk = pl.BlockSpec((T, 2 * C), lambda j, b: (b, col0 // (2 * C) + j))
    wblk = pl.BlockSpec((2, 1, C, C), lambda j, b: (0, j, 0, 0))
    vblk = pl.BlockSpec((8, C), lambda j, b: (0, j))
    return pl.pallas_call(
        body, name="lru_bwd", grid=(nj, B),
        in_specs=[xblk, pl.BlockSpec((4, C), lambda j, b: (0, j)), vblk, wblk, wblk,
                  pl.BlockSpec((N, C), lambda j, b: (b, j)), pl.BlockSpec(memory_space=pl.ANY)],
        out_specs=[xblk, vblk, vblk, wblk, wblk],
        out_shape=[jax.ShapeDtypeStruct(dp.shape, dp.dtype), jax.ShapeDtypeStruct((8, LW), F32),
                   jax.ShapeDtypeStruct((8, LW), F32), jax.ShapeDtypeStruct((2, nj, C, C), F32),
                   jax.ShapeDtypeStruct((2, nj, C, C), F32)],
        scratch_shapes=[pltpu.VMEM((2, T, C), F32)] * 6 + [pltpu.VMEM((T, C), F32)] * 2,
        input_output_aliases={6: 0},
        compiler_params=_params(("parallel", "arbitrary")),
    )(p, cw, lv, wr, wi, dy, dp)


def _chunk_masks(upper):
    i = lax.broadcasted_iota(jnp.int32, (CHUNK, CHUNK), 0)
    j = lax.broadcasted_iota(jnp.int32, (CHUNK, CHUNK), 1)
    ahead = jnp.where(upper, j - i, i - j)
    return i == j, ahead >= 0, ahead > 0


def _col2row(c, eye):
    return jnp.sum(jnp.where(eye, c, 0.0), axis=0, keepdims=True)


def _row2col(r, eye):
    return jnp.sum(jnp.where(eye, r, 0.0), axis=1, keepdims=True)


def _rowsum(x):
    return jnp.sum(x, axis=1, keepdims=True)


_INV_BASE = 8


def _unit_tri_inverses(Ls):
    G = len(Ls)
    W = G * CHUNK
    blk = (lax.broadcasted_iota(jnp.int32, (W, W), 0) // CHUNK) == (lax.broadcasted_iota(jnp.int32, (W, W), 1) // CHUNK)
    ri = lax.broadcasted_iota(jnp.int32, (CHUNK, W), 0)
    ci = lax.broadcasted_iota(jnp.int32, (CHUNK, W), 1) % CHUNK

    def bd(b):
        return jnp.where(blk, jnp.tile(b, (G, 1)), jnp.zeros((), BF16))

    def pdot(a, b):
        return jnp.dot(a.astype(BF16), bd(b.astype(BF16)), preferred_element_type=F32)

    Lc = Ls[0] if G == 1 else jnp.concatenate(Ls, axis=1)
    s = _INV_BASE
    Xp = -jnp.where(ri // s == ci // s, Lc, 0.0)
    Rm = Xp
    for _ in range(int(math.log2(s)) - 1):
        Xp = pdot(Xp, Xp)
        Rm = Rm + Xp + pdot(Rm, Xp)
    while s < CHUNK:
        E = jnp.where((ri // (2 * s) == ci // (2 * s)) & (ri // s != ci // s), Lc, 0.0)
        DE = E + pdot(Rm, E)
        Rm = Rm - (DE + pdot(DE, Rm))
        s *= 2
    eye = _chunk_masks(False)[0]
    return [jnp.where(eye, 1.0, 0.0) + Rm[:, g * CHUNK:(g + 1) * CHUNK] for g in range(G)]


def _delta_chunk_common(q, k, v, gcol, bcol, upper):
    eye, incl, strict = _chunk_masks(upper)
    gc = _rowsum(jnp.where(incl, _col2row(gcol, eye), 0.0))
    D = jnp.where(incl, jnp.exp(jnp.minimum(gc - _col2row(gc, eye), 0.0)), 0.0)
    kb = k * bcol
    AP = _bdot(jnp.concatenate([kb, q], axis=0), k, _NT)
    A = AP[:CHUNK]
    L = jnp.where(strict, A * D, 0.0)
    eg = jnp.exp(gc)
    gl = jnp.sum(gcol, axis=0, keepdims=True)
    attn = jnp.where(incl, AP[CHUNK:] * D, 0.0)
    return dict(eye=eye, incl=incl, strict=strict, gc=gc, D=D, kb=kb, A=A, L=L, eg=eg, gl=gl, egl=jnp.exp(gl),
                attn=attn, kbe=kb * eg, vb=v * bcol, qe=q * eg, kd=k * jnp.exp(gl - gc))


def _delta_group_pre(chunks, upper):
    cs = [_delta_chunk_common(*ch, upper) for ch in chunks]
    out = []
    for c, Tm in zip(cs, _unit_tri_inverses([c["L"] for c in cs])):
        dk = c["kbe"].shape[1]
        wu = _bdot(Tm, jnp.concatenate([c["kbe"], c["vb"]], axis=1))
        KN = _bdot(c["kd"], wu, _TN)
        QO = _bdot(c["attn"], wu)
        out.append((Tm, KN[:, :dk], KN[:, dk:], c["qe"] - QO[:, :dk], QO[:, dk:], c["egl"]))
    return out


def _delta_chunk_bwd(q, k, v, gcol, bcol, S, Tm, do, dS2, upper):
    c = _delta_chunk_common(q, k, v, gcol, bcol, upper)
    eye, incl, strict, D, eg, egl = c["eye"], c["incl"], c["strict"], c["D"], c["eg"], c["egl"]
    kb, kbe, vb, qe, kd, attn = c["kb"], c["kbe"], c["vb"], c["qe"], c["kd"], c["attn"]
    dkk = kbe.shape[1]
    wu = _bdot(Tm, jnp.concatenate([kbe, vb], axis=1))
    w = wu[:, :dkk]
    vn = wu[:, dkk:] - _bdot(w, S)
    dvn = _bdot(kd, dS2) + _bdot(attn, do, _TN)
    dkd = _bdot(vn, dS2, _NT)
    dgl = jnp.sum(_rowsum(dS2 * S), axis=0, keepdims=True) * egl
    dqa = _bdot(do, jnp.concatenate([S, vn], axis=0), _NT)
    dqe = dqa[:, :dkk]
    dattn = jnp.where(incl, dqa[:, dkk:], 0.0)
    dw = -_bdot(dvn, S, _NT)
    r = _rowsum(dkd * kd)
    dk = dkd * jnp.exp(c["gl"] - c["gc"])
    dgl = dgl + jnp.sum(r, axis=0, keepdims=True)
    dgc = _rowsum(dqe * qe) - r
    E = dattn * attn
    dvw = jnp.concatenate([dvn, dw], axis=1)
    dTm = _bdot(dvw, jnp.concatenate([vb, kbe], axis=1), _NT)
    dvk = _bdot(Tm, dvw, _TN)
    dvb = dvk[:, :dvn.shape[1]]
    dv = dvb * bcol
    dbeta = _rowsum(dvb * v)
    dkbe = dvk[:, dvn.shape[1]:]
    dkb = dkbe * eg
    dgc = dgc + _rowsum(dkbe * kbe)
    dL = jnp.where(strict, -_bdot(Tm, _bdot(dTm, Tm, _NT), _TN), 0.0)
    dA = dL * D
    E = E + dL * c["L"]
    PA = jnp.concatenate([dattn * D, dA], axis=0)
    PAk = _bdot(PA, k)
    dq = dqe * eg + PAk[:CHUNK]
    dkb = dkb + PAk[CHUNK:]
    dk = dk + _bdot(PA, jnp.concatenate([q, kb], axis=0), _TN) + dkb * bcol
    dbeta = dbeta + _rowsum(dkb * k)
    dgc = dgc + _rowsum(E) - _row2col(jnp.sum(E, axis=0, keepdims=True), eye)
    dg = _row2col(jnp.sum(jnp.where(incl, dgc, 0.0), axis=0, keepdims=True), eye) + dgl
    return dq, dk, dv, dg, dbeta


def _delta_group(n):
    return max(g for g in range(1, 2 * LANES // CHUNK + 1) if n % g == 0)


def _delta_chunk_at(T, nc, d, i):
    n, ncc = T // CHUNK, nc // CHUNK
    desc = jnp.where(i < ncc, ncc - 1 - i, n - 1 - (i - ncc))
    if isinstance(d, int):
        return i if d == 0 else desc
    return jnp.where(d == 0, i, desc)


def _dn_out_math(o, onorm, z):
    return _rmsn(o, onorm) * _silu(z)


def _delta_fwd(qkv, gb, p, onorm, *, B, T, nc, H, HD):
    N = T - nc
    n = T // CHUNK
    G = _delta_group(n)

    def body(q_ref, k_ref, v_ref, gb_ref, z_ref, on_ref, y_ref, o_ref, Tm_ref, K_ref, S_ref, Qp_ref, eg_ref,
             N_s, O0_s, o_s):
        h = pl.program_id(1)
        lane = lax.broadcasted_iota(jnp.int32, (CHUNK, LANES), 1)

        def pre(g, carry):
            cs = [g * G + i for i in range(G)]
            rows = [pl.ds(pl.multiple_of(c * CHUNK, CHUNK), CHUNK) for c in cs]
            for d in (0, 1):
                chunks = []
                for r in rows:
                    gbb = gb_ref[r, :]
                    chunks.append((q_ref[r, :], k_ref[r, :], v_ref[r, :],
                                   _rowsum(jnp.where(lane == d * H + h, gbb, 0.0)),
                                   _rowsum(jnp.where(lane == 2 * H + d * H + h, gbb, 0.0))))
                for c, r, (Tm, K, Nn, Qp, O0, egl) in zip(cs, rows, _delta_group_pre(chunks, d == 1)):
                    Tm_ref[0, d * n + c] = Tm
                    K_ref[0, d * n + c] = K.astype(BF16)
                    N_s[d * n + c] = Nn
                    Qp_ref[0, d, r, :] = Qp.astype(BF16)
                    O0_s[d, r, :] = O0
                    eg_ref[0, d * n + c] = jnp.broadcast_to(egl, (SUBLANES, HD))
            return carry

        lax.fori_loop(0, n // G, pre, 0)

        def step(i, Ss):
            out = []
            for d in (0, 1):
                c = _delta_chunk_at(T, nc, d, i)
                rows = pl.ds(pl.multiple_of(c * CHUNK, CHUNK), CHUNK)
                S_ref[0, d * n + c] = Ss[d]
                Sb = Ss[d].astype(BF16)
                o_s[d, rows, :] = jnp.dot(Qp_ref[0, d, rows, :], Sb, preferred_element_type=F32) + O0_s[d, rows, :]
                out.append(eg_ref[0, d * n + c][0:1] * Ss[d] + N_s[d * n + c]
                           - jnp.dot(K_ref[0, d * n + c], Sb, preferred_element_type=F32))
            return tuple(out)

        lax.fori_loop(0, n, step, (jnp.zeros((HD, HD), F32), jnp.zeros((HD, HD), F32)))
        o = o_s[0, nc:, :] + o_s[1, nc:, :]
        o_ref[...] = o
        y_ref[...] = _dn_out_math(o, on_ref[...], z_ref[nc:, :]).astype(BF16)

    col = lambda off: pl.BlockSpec((T, HD), lambda b, h: (b, off + h))
    lat = pl.BlockSpec((N, HD), lambda b, h: (b, h))
    per = lambda *blk: pl.BlockSpec((1, *blk), lambda b, h: (b * H + h, 0, 0, 0))
    return pl.pallas_call(
        body, name="delta_fwd", grid=(B, H),
        in_specs=[col(0), col(H), col(2 * H), pl.BlockSpec((T, LANES), lambda b, h: (b, 0)), col(3 * H),
                  pl.BlockSpec((1, HD), lambda b, h: (0, 0))],
        out_specs=[lat, lat, per(2 * n, CHUNK, CHUNK), per(2 * n, HD, HD), per(2 * n, HD, HD), per(2, T, HD),
                   per(2 * n, SUBLANES, HD)],
        out_shape=[jax.ShapeDtypeStruct((B * N, H * HD), BF16), jax.ShapeDtypeStruct((B * N, H * HD), F32),
                   jax.ShapeDtypeStruct((B * H, 2 * n, CHUNK, CHUNK), F32),
                   jax.ShapeDtypeStruct((B * H, 2 * n, HD, HD), BF16), jax.ShapeDtypeStruct((B * H, 2 * n, HD, HD), F32),
                   jax.ShapeDtypeStruct((B * H, 2, T, HD), BF16), jax.ShapeDtypeStruct((B * H, 2 * n, SUBLANES, HD), F32)],
        scratch_shapes=[pltpu.VMEM((2 * n, HD, HD), F32), pltpu.VMEM((2, T, HD), F32), pltpu.VMEM((2, T, HD), F32)],
        compiler_params=_params(("parallel", "parallel")),
    )(qkv, qkv, qkv, gb, p, onorm)


def _delta_bwd(qkv, gb, p, onorm, o, res, dy, dp, *, B, T, nc, H, HD):
    N = T - nc
    n = T // CHUNK

    def body(q_ref, k_ref, v_ref, gb_ref, z_ref, on_ref, o_ref, dy_ref, Tm_ref, K_ref, S_ref, Qp_ref, eg_ref, dp_any,
             dqkv_ref, dgb_ref, dp_ref, don_ref, do_s, R_s, dS_s):
        h, d = pl.program_id(1), pl.program_id(2)
        lane = lax.broadcasted_iota(jnp.int32, (CHUNK, LANES), 1)

        @pl.when(d == 0)
        def _():
            _, vjp = jax.vjp(_dn_out_math, o_ref[...], on_ref[...], z_ref[nc:, :])
            do, don, dz = vjp(dy_ref[...])
            do_s[0:nc, :] = jnp.zeros((nc, HD), F32)
            do_s[nc:, :] = do
            dp_ref[0:nc, :] = jnp.zeros((nc, HD), BF16)
            dp_ref[nc:, :] = dz.astype(BF16)
            dqkv_ref[...] = jnp.zeros_like(dqkv_ref)

            @pl.when(h == 0)
            def _():
                don_ref[...] = jnp.zeros_like(don_ref)
                dgb_ref[...] = jnp.zeros_like(dgb_ref)

            don_ref[0, 0:1, :] += don

        def r_of(c, carry):
            rows = pl.ds(pl.multiple_of(c * CHUNK, CHUNK), CHUNK)
            R_s[c] = lax.dot_general(Qp_ref[0, 0, rows, :], do_s[rows, :].astype(BF16), _TN, preferred_element_type=F32)
            return carry

        lax.fori_loop(0, n, r_of, 0)

        def bwd_step(i, dS):
            c = _delta_chunk_at(T, nc, d, n - 1 - i)
            dS_s[c] = dS
            return (eg_ref[0, c][0:1] * dS + R_s[c]
                    - lax.dot_general(K_ref[0, c], dS.astype(BF16), _TN, preferred_element_type=F32))

        lax.fori_loop(0, n, bwd_step, jnp.zeros((HD, HD), F32))

        def grads(c, carry):
            rows = pl.ds(pl.multiple_of(c * CHUNK, CHUNK), CHUNK)
            gbb = gb_ref[rows, :]
            gcol = _rowsum(jnp.where(lane == d * H + h, gbb, 0.0))
            bcol = _rowsum(jnp.where(lane == 2 * H + d * H + h, gbb, 0.0))
            dq, dk, dv, dg, dbeta = _delta_chunk_bwd(q_ref[rows, :], k_ref[rows, :], v_ref[rows, :], gcol, bcol,
                                                     S_ref[0, c], Tm_ref[0, c], do_s[rows, :], dS_s[c], d == 1)
            dqkv_ref[0, rows, :] += dq
            dqkv_ref[1, rows, :] += dk
            dqkv_ref[2, rows, :] += dv
            dgb_ref[rows, :] += (jnp.where(lane == d * H + h, dg, 0.0)
                                 + jnp.where(lane == 2 * H + d * H + h, dbeta, 0.0))
            return carry

        lax.fori_loop(0, n, grads, 0)

    col = lambda off: pl.BlockSpec((T, HD), lambda b, h, d: (b, off + h))
    lat = pl.BlockSpec((N, HD), lambda b, h, d: (b, h))
    per = lambda *blk: pl.BlockSpec((1, *blk), lambda b, h, d: (b * H + h, d, 0, 0))
    return pl.pallas_call(
        body, name="delta_bwd", grid=(B, H, 2),
        in_specs=[col(0), col(H), col(2 * H), pl.BlockSpec((T, LANES), lambda b, h, d: (b, 0)), col(3 * H),
                  pl.BlockSpec((1, HD), lambda b, h, d: (0, 0)), lat, lat,
                  per(n, CHUNK, CHUNK), per(n, HD, HD), per(n, HD, HD), per(1, T, HD), per(n, SUBLANES, HD),
                  pl.BlockSpec(memory_space=pl.ANY)],
        out_specs=[pl.BlockSpec((3, T, HD), lambda b, h, d: (0, b, h)), pl.BlockSpec((T, LANES), lambda b, h, d: (b, 0)),
                   col(3 * H), pl.BlockSpec((1, 8, HD), lambda b, h, d: (b, 0, 0))],
        out_shape=[jax.ShapeDtypeStruct((3, B * T, H * HD), F32), jax.ShapeDtypeStruct((B * T, LANES), F32),
                   jax.ShapeDtypeStruct(dp.shape, dp.dtype), jax.ShapeDtypeStruct((B, 8, HD), F32)],
        scratch_shapes=[pltpu.VMEM((T, HD), F32), pltpu.VMEM((n, HD, HD), F32), pltpu.VMEM((n, HD, HD), F32)],
        input_output_aliases={13: 2},
        compiler_params=_params(("parallel", "arbitrary", "arbitrary")),
    )(qkv, qkv, qkv, gb, p, onorm, o, dy, *res, dp)


def _rowwise(fn, ins, out_dtypes, *, name, tm=256, mult=16):
    R, W = ins[0].shape
    tm = _tile(R, tm, mult)

    def body(*refs):
        outs = fn(*[r[...] for r in refs[:len(ins)]])
        for o_ref, o in zip(refs[len(ins):], outs):
            o_ref[...] = o.astype(o_ref.dtype)

    spec = pl.BlockSpec((tm, W), lambda i: (i, 0))
    return pl.pallas_call(
        body, name=name, grid=(R // tm,), in_specs=[spec] * len(ins), out_specs=[spec] * len(out_dtypes),
        out_shape=[jax.ShapeDtypeStruct((R, W), dt) for dt in out_dtypes],
        compiler_params=_params(("parallel",)),
    )(*ins)


def _sum_lead(x, *, name, tm=256, mult=16):
    S, R, W = x.shape
    tm = _tile(R, tm, mult)

    def body(*refs):
        acc = refs[0][0].astype(F32)
        for r in refs[1:S]:
            acc = acc + r[0].astype(F32)
        refs[S][...] = acc

    return pl.pallas_call(
        body, name=name, grid=(R // tm,),
        in_specs=[pl.BlockSpec((1, tm, W), functools.partial(lambda s, i: (s, i, 0), s)) for s in range(S)],
        out_specs=pl.BlockSpec((tm, W), lambda i: (i, 0)),
        out_shape=jax.ShapeDtypeStruct((R, W), F32),
        compiler_params=_params(("parallel",)),
    )(*([x] * S))


def _adamw_math(w, g, m, v):
    m = ADAM_B1 * m + (1.0 - ADAM_B1) * g
    v = ADAM_B2 * v + (1.0 - ADAM_B2) * (g * g)
    m_hat = m / (1.0 - ADAM_B1 ** ADAM_STEP)
    v_hat = v / (1.0 - ADAM_B2 ** ADAM_STEP)
    return -ADAM_LR * (m_hat / (jnp.sqrt(v_hat) + ADAM_EPS) + ADAM_WD * w), m, v


def _adamw(w, g, m, v, *, name):
    tm = max(SUBLANES, (256 * 1024) // w.shape[1] // SUBLANES * SUBLANES)
    return _rowwise(_adamw_math, [w, g, m, v], [F32, F32, F32], name=name, tm=tm, mult=SUBLANES)


def _me():
    return lax.axis_index("x"), lax.axis_index("y"), lax.axis_index("c")


def _allgather_small(v):
    R, W = v.shape

    def body(x_ref, out_ref, send_sems, recv_sems, local_sem):
        x, y, c = _me()
        me, sibling = (x, y, c), (x, y, 1 - c)
        chips = [(1 - x, y), (x, 1 - y), (1 - x, 1 - y)]

        def slot(px, py, pc):
            return out_ref.at[4 * px + 2 * py + pc]

        def copy(k, block, to, src=None):
            return pltpu.make_async_remote_copy(
                src_ref=slot(*block) if src is None else src, dst_ref=slot(*block),
                send_sem=send_sems.at[k], recv_sem=recv_sems.at[k], device_id=to, device_id_type=MESH)

        mine = pltpu.make_async_copy(x_ref, slot(*me), local_sem)
        mine.start()
        first = [copy(0, me, sibling, src=x_ref)]
        first += [copy(1 + j, me, (*chip, c), src=x_ref) for j, chip in enumerate(chips)]
        for cp in first:
            cp.start()
        passed = [copy(4 + j, (*chip, c), sibling) for j, chip in enumerate(chips)]
        for j, chip in enumerate(chips):
            copy(1 + j, (*chip, c), me).wait_recv()
            passed[j].start()
        copy(0, sibling, me).wait_recv()
        for j, chip in enumerate(chips):
            copy(4 + j, (*chip, 1 - c), me).wait_recv()
        for cp in first + passed:
            cp.wait_send()
        mine.wait()

    return pl.pallas_call(
        body, name="allgather_small", out_shape=jax.ShapeDtypeStruct((8, R, W), v.dtype),
        in_specs=[pl.BlockSpec(memory_space=pltpu.VMEM)], out_specs=pl.BlockSpec(memory_space=pltpu.VMEM),
        scratch_shapes=[pltpu.SemaphoreType.DMA((7,)), pltpu.SemaphoreType.DMA((7,)), pltpu.SemaphoreType.DMA],
        compiler_params=_params(),
    )(v)


_ANY = pl.BlockSpec(memory_space=pl.ANY)


def _allgather_halves(shards, *, name):
    nw = len(shards)

    def body(*refs):
        x_refs, out_refs = refs[:nw], refs[nw:2 * nw]
        send_sems, recv_sems, local_sems = refs[2 * nw:]
        x, y, c = _me()
        me, sibling = (x, y, c), (x, y, 1 - c)
        chips = [(1 - x, y), (x, 1 - y), (1 - x, 1 - y)]

        def slot(w, px, py, pc):
            return out_refs[w].at[4 * px + 2 * py + pc]

        def copy(w, k, block, to, src=None):
            return pltpu.make_async_remote_copy(
                src_ref=slot(w, *block) if src is None else src, dst_ref=slot(w, *block),
                send_sem=send_sems.at[w, k], recv_sem=recv_sems.at[w, k], device_id=to, device_id_type=MESH)

        started, local = [], []
        for w in range(nw):
            half = shards[w].shape[0] // 2
            own = x_refs[w].at[pl.ds(c * half, half), :]
            mine = pltpu.make_async_copy(own, slot(w, *me), local_sems.at[w])
            mine.start()
            first = [copy(w, 0, me, sibling, src=own)]
            first += [copy(w, 1 + j, me, (*chip, c), src=own) for j, chip in enumerate(chips)]
            for cp in first:
                cp.start()
            started += first
            local.append(mine)
        for w in range(nw):
            for j, chip in enumerate(chips):
                copy(w, 1 + j, (*chip, c), me).wait_recv()
                fwd = copy(w, 4 + j, (*chip, c), sibling)
                fwd.start()
                started.append(fwd)
        for w in range(nw):
            copy(w, 0, sibling, me).wait_recv()
            for j, chip in enumerate(chips):
                copy(w, 4 + j, (*chip, 1 - c), me).wait_recv()
        for cp in started:
            cp.wait_send()
        for cp in local:
            cp.wait()

    return pl.pallas_call(
        body, name=name,
        out_shape=[jax.ShapeDtypeStruct((8, s.shape[0] // 2, s.shape[1]), s.dtype) for s in shards],
        in_specs=[_ANY] * nw, out_specs=[_ANY] * nw,
        scratch_shapes=[pltpu.SemaphoreType.DMA((nw, 7)), pltpu.SemaphoreType.DMA((nw, 7)), pltpu.SemaphoreType.DMA((nw,))],
        compiler_params=_params(),
    )(*shards)


def _sibling_send_halves(arrs, *, name):
    nw = len(arrs)

    def body(*refs):
        x_refs, out_refs, send_sems, recv_sems = refs[:nw], refs[nw:2 * nw], refs[2 * nw], refs[2 * nw + 1]
        x, y, c = _me()
        cps = []
        for w in range(nw):
            half = arrs[w].shape[1] // 2
            cp = pltpu.make_async_remote_copy(
                src_ref=x_refs[w].at[:, pl.ds((1 - c) * half, half), :], dst_ref=out_refs[w],
                send_sem=send_sems.at[w], recv_sem=recv_sems.at[w], device_id=(x, y, 1 - c), device_id_type=MESH)
            cp.start()
            cps.append(cp)
        for cp in cps:
            cp.wait()

    return pl.pallas_call(
        body, name=name,
        out_shape=[jax.ShapeDtypeStruct((a.shape[0], a.shape[1] // 2, a.shape[2]), a.dtype) for a in arrs],
        in_specs=[_ANY] * nw, out_specs=[_ANY] * nw,
        scratch_shapes=[pltpu.SemaphoreType.DMA((nw,)), pltpu.SemaphoreType.DMA((nw,))],
        compiler_params=_params(),
    )(*arrs)


def _sibling_swap(arrs, *, name):
    nw = len(arrs)

    def body(*refs):
        x_refs, out_refs, send_sems, recv_sems = refs[:nw], refs[nw:2 * nw], refs[2 * nw], refs[2 * nw + 1]
        x, y, c = _me()
        cps = []
        for w in range(nw):
            cp = pltpu.make_async_remote_copy(
                src_ref=x_refs[w], dst_ref=out_refs[w], send_sem=send_sems.at[w], recv_sem=recv_sems.at[w],
                device_id=(x, y, 1 - c), device_id_type=MESH)
            cp.start()
            cps.append(cp)
        for cp in cps:
            cp.wait()

    return pl.pallas_call(
        body, name=name, out_shape=[jax.ShapeDtypeStruct(a.shape, a.dtype) for a in arrs],
        in_specs=[_ANY] * nw, out_specs=[_ANY] * nw,
        scratch_shapes=[pltpu.SemaphoreType.DMA((nw,)), pltpu.SemaphoreType.DMA((nw,))],
        compiler_params=_params(),
    )(*arrs)


def _adamw_halves(w, own, sib, m, v, c_arr, *, name):
    r, cols = w.shape
    h = r // 2
    tm = _tile(h, max(SUBLANES, (192 * 1024) // cols // SUBLANES * SUBLANES), SUBLANES)
    nb = h // tm

    def body(c_ref, w_ref, own_ref, sib_ref, m_ref, v_ref, g_out, d_out, m_out, v_out):
        g = jnp.where(pl.program_id(0) == c_ref[0], own_ref[...], sib_ref[...])
        g_out[...] = g
        d_out[...], m_out[...], v_out[...] = _adamw_math(w_ref[...], g, m_ref[...], v_ref[...])

    full = pl.BlockSpec((tm, cols), lambda hh, i, c_ref: (hh * nb + i, 0))
    half = pl.BlockSpec((tm, cols), lambda hh, i, c_ref: (i, 0))
    return pl.pallas_call(
        body, name=name,
        grid_spec=pltpu.PrefetchScalarGridSpec(num_scalar_prefetch=1, grid=(2, nb),
                                               in_specs=[full, half, half, full, full], out_specs=[full] * 4),
        out_shape=[jax.ShapeDtypeStruct((r, cols), F32)] * 4,
        compiler_params=_params(("parallel", "parallel")),
    )(c_arr, w, own, sib, m, v)


def _chip_exchange(arrs, *, name):
    nw = len(arrs)

    def body(*refs):
        x_refs, out_refs = refs[:nw], refs[nw:2 * nw]
        send_sems, recv_sems = refs[2 * nw:]
        x, y, c = _me()
        s_me = 2 * x + y
        chips = [(1 - x, y), (x, 1 - y), (1 - x, 1 - y)]
        started = []
        for w in range(nw):
            for k, (px, py) in enumerate(chips):
                cp = pltpu.make_async_remote_copy(
                    src_ref=x_refs[w].at[2 * px + py], dst_ref=out_refs[w].at[s_me], send_sem=send_sems.at[w, k],
                    recv_sem=recv_sems.at[w, k], device_id=(px, py, c), device_id_type=MESH)
                cp.start()
                started.append(cp)
        for w in range(nw):
            for k, (px, py) in enumerate(chips):
                pltpu.make_async_remote_copy(
                    src_ref=x_refs[w].at[s_me], dst_ref=out_refs[w].at[2 * px + py], send_sem=send_sems.at[w, k],
                    recv_sem=recv_sems.at[w, k], device_id=(px, py, c), device_id_type=MESH).wait_recv()
        for cp in started:
            cp.wait_send()

    return pl.pallas_call(
        body, name=name, out_shape=[jax.ShapeDtypeStruct(a.shape, a.dtype) for a in arrs],
        in_specs=[_ANY] * nw, out_specs=[_ANY] * nw,
        scratch_shapes=[pltpu.SemaphoreType.DMA((nw, 3)), pltpu.SemaphoreType.DMA((nw, 3))],
        compiler_params=_params(),
    )(*arrs)


_HBM = pl.BlockSpec(memory_space=pltpu.HBM)
_SEM = pl.BlockSpec(memory_space=pltpu.SEMAPHORE)
_DATAFLOW = pltpu.SideEffectType.DATAFLOW_SIDE_EFFECTING


def _chip_exchange_start(arrs, *, name):
    nw = len(arrs)

    def body(*refs):
        x_refs, land_refs, send_sems, recv_sems = refs[:nw], refs[nw:2 * nw], refs[2 * nw], refs[2 * nw + 1]
        token = refs[-1]
        x, y, c = _me()
        s_me = 2 * x + y
        for w in range(nw):
            for k, (px, py) in enumerate([(1 - x, y), (x, 1 - y), (1 - x, 1 - y)]):
                pltpu.make_async_remote_copy(
                    src_ref=x_refs[w].at[2 * px + py], dst_ref=land_refs[w].at[s_me], send_sem=send_sems.at[3 * w + k],
                    recv_sem=recv_sems.at[3 * w + k], device_id=(px, py, c), device_id_type=MESH).start()
        token[...] = jnp.zeros_like(token)

    hbm = [pltpu.HBM(a.shape, a.dtype) for a in arrs]
    outs = pl.pallas_call(
        body, name=name,
        out_shape=(pltpu.SemaphoreType.DMA((3 * nw,)), pltpu.SemaphoreType.DMA((3 * nw,)), *hbm, *hbm,
                   jax.ShapeDtypeStruct((SUBLANES, LANES), F32)),
        in_specs=[_HBM] * (2 * nw), out_specs=(_SEM, _SEM, *([_HBM] * (2 * nw)), pl.BlockSpec(memory_space=pltpu.VMEM)),
        input_output_aliases={i: 2 + i for i in range(2 * nw)},
        compiler_params=pltpu.CompilerParams(has_side_effects=_DATAFLOW),
    )(*[pltpu.with_memory_space_constraint(a, pltpu.HBM) for a in arrs],
      *[pltpu.with_memory_space_constraint(lax.empty(a.shape, a.dtype), pltpu.HBM) for a in arrs])
    return outs[0], outs[1], list(outs[2:2 + nw]), list(outs[2 + nw:2 + 2 * nw]), outs[-1]


def _chip_exchange_wait(send_sems, recv_sems, srcs, lands, after, *, name):
    nw = len(srcs)

    def body(*refs):
        x_refs, land_refs, send_sems, recv_sems = refs[:nw], refs[nw:2 * nw], refs[2 * nw], refs[2 * nw + 1]
        x, y, c = _me()
        for w in range(nw):
            for k, (px, py) in enumerate([(1 - x, y), (x, 1 - y), (1 - x, 1 - y)]):
                cp = pltpu.make_async_remote_copy(
                    src_ref=x_refs[w].at[2 * px + py], dst_ref=land_refs[w].at[2 * px + py], send_sem=send_sems.at[3 * w + k],
                    recv_sem=recv_sems.at[3 * w + k], device_id=(px, py, c), device_id_type=MESH)
                cp.wait_send()
                cp.wait_recv()

    hbm = [pltpu.HBM(a.shape, a.dtype) for a in srcs]
    outs = pl.pallas_call(
        body, name=name, out_shape=(*hbm, *hbm),
        in_specs=[_HBM] * (2 * nw) + [_SEM, _SEM, _ANY], out_specs=tuple([_HBM] * (2 * nw)),
        input_output_aliases={i: i for i in range(2 * nw)},
        compiler_params=pltpu.CompilerParams(has_side_effects=_DATAFLOW),
    )(*srcs, *lands, send_sems, recv_sems, after)
    return list(outs[:nw]), list(outs[nw:])


def _sum_slabs(landed, own_src, s_arr, *, name, tm=512):
    S, h, w = landed.shape
    tm = _tile(h, tm, 16)

    def body(s_ref, *refs):
        own = refs[S][0].astype(F32)
        acc = None
        for s in range(S):
            term = jnp.where(s_ref[0] == s, own, refs[s][0].astype(F32))
            acc = term if acc is None else acc + term
        refs[S + 1][...] = acc

    def slab(s):
        return pl.BlockSpec((1, tm, w), lambda i, s_ref: (jnp.where(s_ref[0] == s, (s + 1) % S, s), i, 0))

    return pl.pallas_call(
        body, name=name,
        grid_spec=pltpu.PrefetchScalarGridSpec(
            num_scalar_prefetch=1, grid=(h // tm,),
            in_specs=[slab(s) for s in range(S)] + [pl.BlockSpec((1, tm, w), lambda i, s_ref: (s_ref[0], i, 0))],
            out_specs=pl.BlockSpec((tm, w), lambda i, s_ref: (i, 0))),
        out_shape=jax.ShapeDtypeStruct((h, w), F32),
        compiler_params=_params(("parallel",)),
    )(s_arr, *([landed] * S), own_src)


def _half_add(g, recv, c_arr, *, name):
    S, r, w = g.shape
    h = r // 2
    tm = _tile(h, 512, 16)
    nb = h // tm

    def body(c_ref, g_ref, r_ref, o_ref):
        o_ref[...] = (g_ref[...] + r_ref[...]).astype(BF16)

    return pl.pallas_call(
        body, name=name,
        grid_spec=pltpu.PrefetchScalarGridSpec(
            num_scalar_prefetch=1, grid=(S, nb),
            in_specs=[pl.BlockSpec((1, tm, w), lambda s, i, c_ref: (s, c_ref[0] * nb + i, 0)),
                      pl.BlockSpec((1, tm, w), lambda s, i, c_ref: (s, i, 0))],
            out_specs=pl.BlockSpec((1, tm, w), lambda s, i, c_ref: (s, i, 0))),
        out_shape=jax.ShapeDtypeStruct((S, h, w), BF16),
        compiler_params=_params(("parallel", "parallel")),
    )(c_arr, g, recv)


def _layout(sizes, width, part_mult, total_mult):
    offs, rows, r = [], [], 0
    for n in sizes:
        k = -(-n // width)
        offs.append(r)
        rows.append(k)
        r += -(-k // part_mult) * part_mult
    return offs, rows, -(-r // total_mult) * total_mult


def _pack(arrs, width, part_mult, total_mult, dtype, lead=()):
    nl = len(lead)
    sizes = [math.prod(a.shape[nl:]) for a in arrs]
    offs, rows, total = _layout(sizes, width, part_mult, total_mult)
    parts, r = [], 0
    for a, n, o, k in zip(arrs, sizes, offs, rows):
        kp = -(-k // part_mult) * part_mult
        flat = a.reshape(*lead, n).astype(dtype)
        if kp * width > n:
            flat = jnp.pad(flat, [(0, 0)] * nl + [(0, kp * width - n)])
        parts.append(flat.reshape(*lead, kp, width))
        r = o + kp
    if total > r:
        parts.append(jnp.zeros((*lead, total - r, width), dtype))
    return jnp.concatenate(parts, axis=nl)


def _unpack(pool, shapes, width, part_mult, total_mult):
    lead = pool.shape[:-2]
    sizes = [math.prod(s) for s in shapes]
    offs, rows, _ = _layout(sizes, width, part_mult, total_mult)
    out = []
    for s, n, o, k in zip(shapes, sizes, offs, rows):
        flat = lax.slice_in_dim(pool, o, o + k, axis=len(lead)).reshape(*lead, k * width)
        out.append(lax.slice_in_dim(flat, 0, n, axis=len(lead)).reshape(*lead, *s))
    return out


_WEIGHTS = ("c_ctx", "w_ada", "b_ada", "g_pre_mix", "g_post_mix", "g_pre_ffn", "g_post_ffn", "w_in", "b_merge",
            "dn_conv", "dn_a_log", "dn_dt_bias", "dn_onorm", "lru_conv", "lru_conv_b", "lru_w_rg", "lru_b_rg",
            "lru_w_ig", "lru_b_ig", "lru_lambda", "w_branch_dn", "w_branch_lru", "w_out", "w_up", "ffn_dw",
            "ffn_dw_b", "w_down")
_BIG = {"w_ada": True, "w_in": True, "w_branch_dn": False, "w_branch_lru": False, "w_out": False, "w_up": True,
        "w_down": False}
_SMALL_SHARDED = ("dn_conv", "lru_conv", "lru_b_rg", "lru_b_ig", "lru_lambda", "ffn_dw")
_NCHIP = 4
_FLAT_PART = 8
_FLAT_TOTAL = 256


def _to_chip_shards(g, by_cols):
    if by_cols:
        return g.reshape(g.shape[0], _NCHIP, g.shape[1] // _NCHIP).transpose(1, 0, 2)
    return g.reshape(_NCHIP, g.shape[0] // _NCHIP, g.shape[1])


def _from_chip_shards(s, by_cols):
    if by_cols:
        return s.transpose(1, 0, 2).reshape(s.shape[1], _NCHIP * s.shape[2])
    return s.reshape(_NCHIP * s.shape[1], s.shape[2])


def _dsilu(x):
    s = _sigmoid(x)
    return s * (1.0 + x * (1.0 - s))


def kernel(x, c, ctx, c_ctx, w_ada, b_ada, g_pre_mix, g_post_mix, g_pre_ffn, g_post_ffn, w_in, b_merge, dn_conv, dn_a_log, dn_dt_bias, dn_onorm, lru_conv, lru_conv_b, lru_w_rg, lru_b_rg, lru_w_ig, lru_b_ig, lru_lambda, w_branch_dn, w_branch_lru, w_out, w_up, ffn_dw, ffn_dw_b, w_down, loss_target, m_c_ctx, m_w_ada, m_b_ada, m_g_pre_mix, m_g_post_mix, m_g_pre_ffn, m_g_post_ffn, m_w_in, m_b_merge, m_dn_conv, m_dn_a_log, m_dn_dt_bias, m_dn_onorm, m_lru_conv, m_lru_conv_b, m_lru_w_rg, m_lru_b_rg, m_lru_w_ig, m_lru_b_ig, m_lru_lambda, m_w_branch_dn, m_w_branch_lru, m_w_out, m_w_up, m_ffn_dw, m_ffn_dw_b, m_w_down, v_c_ctx, v_w_ada, v_b_ada, v_g_pre_mix, v_g_post_mix, v_g_pre_ffn, v_g_post_ffn, v_w_in, v_b_merge, v_dn_conv, v_dn_a_log, v_dn_dt_bias, v_dn_onorm, v_lru_conv, v_lru_conv_b, v_lru_w_rg, v_lru_b_rg, v_lru_w_ig, v_lru_b_ig, v_lru_lambda, v_w_branch_dn, v_w_branch_lru, v_w_out, v_w_up, v_ffn_dw, v_ffn_dw_b, v_w_down):
    W = dict(zip(_WEIGHTS, (c_ctx, w_ada, b_ada, g_pre_mix, g_post_mix, g_pre_ffn, g_post_ffn, w_in, b_merge, dn_conv,
                            dn_a_log, dn_dt_bias, dn_onorm, lru_conv, lru_conv_b, lru_w_rg, lru_b_rg, lru_w_ig, lru_b_ig,
                            lru_lambda, w_branch_dn, w_branch_lru, w_out, w_up, ffn_dw, ffn_dw_b, w_down)))
    Mo = dict(zip(_WEIGHTS, (m_c_ctx, m_w_ada, m_b_ada, m_g_pre_mix, m_g_post_mix, m_g_pre_ffn, m_g_post_ffn, m_w_in,
                             m_b_merge, m_dn_conv, m_dn_a_log, m_dn_dt_bias, m_dn_onorm, m_lru_conv, m_lru_conv_b,
                             m_lru_w_rg, m_lru_b_rg, m_lru_w_ig, m_lru_b_ig, m_lru_lambda, m_w_branch_dn,
                             m_w_branch_lru, m_w_out, m_w_up, m_ffn_dw, m_ffn_dw_b, m_w_down)))
    Vo = dict(zip(_WEIGHTS, (v_c_ctx, v_w_ada, v_b_ada, v_g_pre_mix, v_g_post_mix, v_g_pre_ffn, v_g_post_ffn, v_w_in,
                             v_b_merge, v_dn_conv, v_dn_a_log, v_dn_dt_bias, v_dn_onorm, v_lru_conv, v_lru_conv_b,
                             v_lru_w_rg, v_lru_b_rg, v_lru_w_ig, v_lru_b_ig, v_lru_lambda, v_w_branch_dn,
                             v_w_branch_lru, v_w_out, v_w_up, v_ffn_dw, v_ffn_dw_b, v_w_down)))
    B, N, D = x.shape
    NC = ctx.shape[1]
    T = NC + N
    H, HD = dn_a_log.shape[-1], dn_onorm.shape[-1]
    DNW = H * HD
    LW, LBD = lru_conv_b.shape[-1], lru_w_rg.shape[-1]
    DFF = ffn_dw_b.shape[-1]
    LC = LANES
    x_i, y_i, c_i = _me()
    s_me = 2 * x_i + y_i
    tm = _tile(math.gcd(NC, N), 256, 16)

    gathered = _allgather_halves([W[n][0].astype(BF16) for n in _BIG], name="allgather_big")
    full = {}
    for n, g in zip(_BIG, gathered):
        r, w_ = W[n].shape[1:]
        full[n] = g.reshape(_NCHIP, r, w_) if _BIG[n] else g.reshape(_NCHIP * r, w_)

    small_local = [W[n][0].reshape(-1, W[n].shape[-1]) for n in _SMALL_SHARDED]
    small_shapes = [a.shape for a in small_local]
    spack = _pack(small_local, LANES, _FLAT_PART, _FLAT_PART, F32)
    sgath = _allgather_small(spack)[0::2]
    sfull = {n: _from_chip_shards(s, True)
             for n, s in zip(_SMALL_SHARDED, _unpack(sgath, small_shapes, LANES, _FLAT_PART, _FLAT_PART))}

    o_a = 4 * DNW
    o_xl = o_a + 4 * H
    o_mg = o_xl + 2 * LW
    wi_ = _from_chip_shards(full["w_in"], True)
    nj = LW // LC
    lru_cols = jnp.stack([wi_[:, o_xl:o_xl + LW].reshape(D, nj, LC), wi_[:, o_xl + LW:o_mg].reshape(D, nj, LC)],
                         axis=2).reshape(D, 2 * LW)
    wp = jnp.concatenate([wi_[:, :o_a], lru_cols, wi_[:, o_mg:], wi_[:, o_a:o_xl],
                          jnp.zeros((D, LANES - 4 * H), BF16)], axis=1)
    p_lru, p_mg, p_ab = 4 * DNW, 4 * DNW + 2 * LW, 4 * DNW + 2 * LW + 2 * D
    PW = p_ab + LANES

    MR = LANES
    cond = jnp.concatenate([c, c_ctx[None], jnp.zeros((MR - B - 1, D), F32)], axis=0)
    silu_rows = _rowwise(lambda a: (_silu(a),), [cond], [F32], name="cond_silu")[0]
    mod = _matmul(silu_rows, full["w_ada"], b_shards=(0, _NCHIP), name="ada_fwd") + b_ada
    mx = mod[:B].reshape(B, 6, D)
    mc = mod[B].reshape(6, D)
    zero = jnp.zeros((B, D), F32)
    tab = jnp.stack([jnp.stack([jnp.broadcast_to(mc[0], (B, D)), jnp.broadcast_to(mc[1], (B, D))] + [zero] * 6, axis=1),
                     jnp.stack([mx[:, 0], mx[:, 1]] + [zero] * 6, axis=1)], axis=1)
    vecs = jnp.stack([mx[:, 2], mx[:, 3], mx[:, 4], mx[:, 5]] + [zero] * 4, axis=1)
    gains = jnp.concatenate([g_post_mix, g_pre_ffn, g_post_ffn, jnp.zeros((5, D), F32)], axis=0)

    h = jnp.concatenate([ctx, x], axis=1)
    u = _premix_fwd(h, g_pre_mix, tab, nc=NC, tm=tm)
    p = _matmul(u, wp, name="in_fwd")
    dkw = dict(B=B, T=T, nc=NC, H=H, HD=HD)
    qkv = _dnprep_fwd(p, sfull["dn_conv"], **dkw)
    prm = jnp.concatenate([
        jnp.concatenate([dn_a_log.reshape(1, 2 * H), jnp.zeros((1, LANES - 2 * H), F32)], axis=1),
        jnp.concatenate([dn_dt_bias.reshape(1, 2 * H), jnp.zeros((1, LANES - 2 * H), F32)], axis=1),
        jnp.zeros((6, LANES), F32)], axis=0)
    gtm = _tile(B * T, 512, 16)
    gb = _gb_fwd(p, prm, rows=B * T, col0=p_ab, H=H, tm=gtm)
    y_dn, o_dn, *dn_res = _delta_fwd(qkv, gb, p, dn_onorm, **dkw)
    lv = jnp.concatenate([lru_conv_b, sfull["lru_b_rg"], sfull["lru_b_ig"], sfull["lru_lambda"], jnp.zeros((1, LW), F32)], axis=0)
    wr = _blockdiag(lru_w_rg[0], LC).astype(BF16)
    wi = _blockdiag(lru_w_ig[0], LC).astype(BF16)
    lkw = dict(B=B, T=T, nc=NC, LW=LW, col0=p_lru, C=LC)
    y_lru = _lru_fwd(p, sfull["lru_conv"], lv, wr, wi, **lkw)
    Ydn = _matmul(y_dn, full["w_branch_dn"], name="bdn_fwd")
    Ylru = _matmul(y_lru, full["w_branch_lru"], name="blru_fwd")
    mkw = dict(B=B, T=T, nc=NC, D=D, col0=p_mg, tm=tm)
    mixin = _merge_fwd(p, Ydn, Ylru, b_merge, **mkw)
    mix = _matmul(mixin, full["w_out"], name="out_fwd")
    h1, u2 = _post_fwd(x, mix, gains, vecs, tm=tm)
    F = _matmul(u2, full["w_up"], b_shards=(0, _NCHIP), name="up_fwd")
    w9 = sfull["ffn_dw"]
    ftc = _tile(DFF, 256)
    f = _ffn_act_fwd(F, w9, ffn_dw_b, B=B, N=N, DFF=DFF, tc=ftc)
    dn = _matmul(f, full["w_down"], name="down_fwd")
    ddn, dout, sums_f = _final(h1, dn, loss_target, gains, vecs, tm=tm)

    G = {}
    df = _matmul(ddn, full["w_down"], tb=True, name="down_bwd_x")
    G["w_down"] = _matmul(f, ddn, ta=True, name="down_bwd_w")
    dFg, dFv, dwb = _ffn_act_bwd(F, w9, ffn_dw_b, df, B=B, N=N, DFF=DFF, tc=ftc)
    hs = _NCHIP // 2
    du2 = _matmul(dFg, full["w_up"], tb=True, b_shards=(0, hs), name="up_bwd_xg")
    du2 = _matmul(dFv, full["w_up"], tb=True, b_shards=(hs, hs), add=du2, name="up_bwd_xv")
    G["w_up"] = jnp.concatenate([_matmul(u2, dFg, ta=True, out_shards=hs, name="up_bwd_wg"),
                                 _matmul(u2, dFv, ta=True, out_shards=hs, name="up_bwd_wv")], axis=0)
    dx1, dmix, sums_p = _post_bwd(x, mix, gains, vecs, dout, du2, tm=tm)
    dmixin = _matmul(dmix, full["w_out"], tb=True, name="out_bwd_x")
    G["w_out"] = _matmul(mixin, dmix, ta=True, name="out_bwd_w")
    dp = jnp.zeros((B * T, PW), BF16)
    dYdn, dYlru, dp, sums_m = _merge_bwd(p, Ydn, Ylru, b_merge, dmixin, dp, **mkw)
    dy_dn = _matmul(dYdn, full["w_branch_dn"], tb=True, name="bdn_bwd_x")
    G["w_branch_dn"] = _matmul(y_dn, dYdn, ta=True, name="bdn_bwd_w")
    dy_lru = _matmul(dYlru, full["w_branch_lru"], tb=True, name="blru_bwd_x")
    G["w_branch_lru"] = _matmul(y_lru, dYlru, ta=True, name="blru_bwd_w")

    c_arr = c_i.astype(jnp.int32).reshape(1)
    s_arr = s_me.astype(jnp.int32).reshape(1)

    def chip_sums(names, tag):
        slabs = [G[n] if _BIG[n] else G[n].reshape(_NCHIP, G[n].shape[0] // _NCHIP, G[n].shape[1]) for n in names]
        from_sibling = _sibling_send_halves(slabs, name="rs_sibling_" + tag)
        return [_half_add(g, r, c_arr, name="rs_add_" + n) for n, g, r in zip(names, slabs, from_sibling)]

    early = tuple(n for n in _BIG if n in G)
    late = tuple(n for n in _BIG if n not in G)
    cx_send, cx_recv, cx_src, cx_land, cx_token = _chip_exchange_start(chip_sums(early, "early"), name="cx_start")
    dp, dcw_l, dlv, dwr, dwi = _lru_bwd(p, sfull["lru_conv"], lv + cx_token[0, 0], wr, wi, dy_lru, dp, **lkw)
    dqkv, dgb, dp, don = _delta_bwd(qkv, gb, p, dn_onorm, o_dn, dn_res, dy_dn, dp, **dkw)
    dp, dprm = _gb_bwd(p, prm, dgb, dp, rows=B * T, col0=p_ab, H=H, tm=gtm)
    dp, dcw_d = _dnprep_bwd(p, sfull["dn_conv"], dqkv, dp, **dkw)
    dU = _matmul(dp, wp, tb=True, name="in_bwd_x")
    dwp = _matmul(u, dp, ta=True, name="in_bwd_w")
    grad_x, sums_pm = _premix_bwd(h, g_pre_mix, tab, dU, dx1, nc=NC, tm=tm)
    dlru = dwp[:, p_lru:p_mg].reshape(D, nj, 2, LC)
    G["w_in"] = _to_chip_shards(jnp.concatenate([dwp[:, :o_a], dwp[:, p_ab:p_ab + 4 * H], dlru[:, :, 0].reshape(D, LW),
                                                 dlru[:, :, 1].reshape(D, LW), dwp[:, p_mg:p_ab]], axis=1), True)

    dmod_x = jnp.stack([sums_pm[:, 1, 0], sums_pm[:, 1, 1], sums_p[:, 0], sums_p[:, 1], sums_p[:, 2], sums_f[:, 0]],
                       axis=1).reshape(B, 6 * D)
    dmod_c = jnp.concatenate([sums_pm[:, 0, 0].sum(0), sums_pm[:, 0, 1].sum(0), jnp.zeros((4 * D,), F32)])[None]
    dmod = jnp.concatenate([dmod_x, dmod_c, jnp.zeros((MR - B - 1, 6 * D), F32)], axis=0)
    G["w_ada"] = _matmul(silu_rows, dmod, ta=True, out_shards=_NCHIP, name="ada_bwd_w")
    dsilu = _matmul(dmod, full["w_ada"], tb=True, b_shards=(0, _NCHIP), name="ada_bwd_x")

    g_small = {
        "c_ctx": dsilu[B] * _dsilu(c_ctx),
        "b_ada": dmod[:B + 1].sum(0)[None],
        "g_pre_mix": sums_pm[:, :, 2].sum((0, 1))[None],
        "g_post_mix": sums_p[:, 3].sum(0)[None],
        "g_pre_ffn": sums_p[:, 4].sum(0)[None],
        "g_post_ffn": sums_f[:, 1].sum(0)[None],
        "b_merge": sums_m[0:1],
        "dn_conv": dcw_d[0:4][None],
        "dn_a_log": dprm[0, :2 * H].reshape(1, 2, H),
        "dn_dt_bias": dprm[1, :2 * H].reshape(1, 2, H),
        "dn_onorm": don[:, 0].sum(0)[None],
        "lru_conv": dcw_l[0:4][None],
        "lru_conv_b": dlv[0:1],
        "lru_w_rg": _blockdiag_extract(dwr, LBD)[None],
        "lru_b_rg": dlv[1:3][None],
        "lru_w_ig": _blockdiag_extract(dwi, LBD)[None],
        "lru_b_ig": dlv[3:5][None],
        "lru_lambda": dlv[5:7][None],
        "ffn_dw": dwb[0:9].reshape(1, 3, 3, DFF),
        "ffn_dw_b": dwb[9:10],
    }
    small_names = tuple(n for n in _WEIGHTS if n not in _BIG)
    loss_part = sums_f[:, 2].sum().reshape(1)
    gs_list = [g_small[n] for n in small_names] + [loss_part]
    gs_shapes = [a.shape for a in gs_list]
    gpack = _pack(gs_list, LANES, _FLAT_PART, _FLAT_TOTAL, F32)
    gsum = _sum_lead(_allgather_small(gpack), name="small_sum", tm=512, mult=SUBLANES)
    gs_red = dict(zip(small_names + ("loss",), _unpack(gsum, gs_shapes, LANES, _FLAT_PART, _FLAT_TOTAL)))
    loss = gs_red["loss"][0]

    cx_src, cx_land = _chip_exchange_wait(cx_send, cx_recv, cx_src, cx_land, dsilu, name="cx_wait")
    late_sums = chip_sums(late, "late")
    late_land = _chip_exchange(late_sums, name="chip_exchange")
    half = {n: _sum_slabs(l, src, s_arr, name="rs_sum_" + n)
            for n, l, src in zip(early + late, list(cx_land) + list(late_land), list(cx_src) + list(late_sums))}
    halves = [half[n] for n in _BIG]
    sib_halves = _sibling_swap(halves, name="rs_gather")

    grads, deltas, new_m, new_v = {}, {}, {}, {}
    for n, own, sib in zip(_BIG, halves, sib_halves):
        shp = W[n].shape
        outs = _adamw_halves(W[n][0], own, sib, Mo[n][0], Vo[n][0], c_arr, name="adamw_" + n)
        grads[n], deltas[n], new_m[n], new_v[n] = (o.reshape(shp) for o in outs)
    for n in small_names:
        g = gs_red[n]
        if n in _SMALL_SHARDED:
            k = W[n].shape[-1]
            g = lax.dynamic_slice_in_dim(g, s_me * k, k, axis=g.ndim - 1)
        grads[n] = g.reshape(W[n].shape)
    sm_shapes = [W[n].shape for n in small_names]
    pk = lambda d: _pack([d[n] for n in small_names], LANES, _FLAT_PART, _FLAT_TOTAL, F32)
    d_, m_, v_ = _adamw(pk(W), pk(grads), pk(Mo), pk(Vo), name="adamw_small")
    for dst, pool_ in ((deltas, d_), (new_m, m_), (new_v, v_)):
        dst.update(zip(small_names, _unpack(pool_, sm_shapes, LANES, _FLAT_PART, _FLAT_TOTAL)))
    return (loss, grad_x, *[grads[n] for n in _WEIGHTS], *[deltas[n] for n in _WEIGHTS],
            *[new_m[n] for n in _WEIGHTS], *[new_v[n] for n in _WEIGHTS])
```

```python
import functools
import math

import jax
import jax.numpy as jnp
from jax import lax
from jax.experimental import pallas as pl
from jax.experimental.pallas import tpu as pltpu

F32 = jnp.float32
BF16 = jnp.bfloat16
EPS = 1e-6
GRID_W = 64
CHUNK = 128
LRU_C = 8.0
LANES = 128
SUBLANES = 8
VMEM_LIMIT = 56 * 1024 * 1024
ADAM_LR, ADAM_B1, ADAM_B2, ADAM_EPS, ADAM_WD, ADAM_STEP = 0.001, 0.9, 0.999, 1e-08, 0.01, 10
MESH = pl.DeviceIdType.MESH


def _tile(n, target, mult=LANES):
    best = None
    for t in range(mult, min(n, target) + 1, mult):
        if n % t == 0:
            best = t
    return best if best is not None else n


def _params(sem=None, **kw):
    return pltpu.CompilerParams(dimension_semantics=sem, vmem_limit_bytes=VMEM_LIMIT, **kw)


def _sigmoid(x):
    return 1.0 / (1.0 + jnp.exp(-x))


def _silu(x):
    return x * _sigmoid(x)


def _softplus(x):
    return jnp.maximum(x, 0.0) + jnp.log(1.0 + jnp.exp(-jnp.abs(x)))


def _gelu(x):
    return 0.5 * x * (1.0 + jnp.tanh(math.sqrt(2.0 / math.pi) * (x + 0.044715 * x * x * x)))


def _rmsn(u, gain):
    return u * lax.rsqrt(jnp.mean(u * u, axis=-1, keepdims=True) + EPS) * gain


_MM_VMEM = 40 * 1024 * 1024


def _matmul(a, b, *, ta=False, tb=False, add=None, b_shards=None, out_shards=None, out_dtype=F32, name,
            tm=1024, tn=2048, tk=1024):
    (K, M) = a.shape if ta else a.shape[::-1]
    if b_shards is not None:
        s0, ns = b_shards
        bsh = (b.shape[1], ns * b.shape[2])
        nsh = b.shape[2]
    else:
        bsh = b.shape
    N = bsh[0] if tb else bsh[1]
    assert (bsh[1] if tb else bsh[0]) == K, (a.shape, b.shape, ta, tb)
    tm = _tile(M, tm)
    tk = _tile(nsh if (b_shards is not None and tb) else K, tk)
    nlim = nsh if (b_shards is not None and not tb) else (N // out_shards if out_shards else N)
    osz = jnp.dtype(out_dtype).itemsize + (4 if add is not None else 0)
    while True:
        tn_ = _tile(nlim, tn)
        need = 2 * (tm * tk * a.dtype.itemsize + tk * tn_ * b.dtype.itemsize + tm * tn_ * osz) + 4 * tm * tn_
        if need <= _MM_VMEM or tn <= LANES:
            break
        tn //= 2
    tn = tn_
    nk = K // tk
    dims = (((0 if ta else 1,), (1 if tb else 0,)), ((), ()))

    def body(a_ref, b_ref, *rest):
        (c_ref, o_ref, acc_ref) = rest if add is not None else (None, *rest)
        k = pl.program_id(2)

        @pl.when(k == 0)
        def _():
            acc_ref[...] = jnp.zeros_like(acc_ref) if c_ref is None else c_ref[...]

        bv = b_ref[0] if b_shards is not None else b_ref[...]
        acc_ref[...] += lax.dot_general(a_ref[...].astype(BF16), bv.astype(BF16), dims, preferred_element_type=F32)

        @pl.when(k == nk - 1)
        def _():
            if out_shards:
                o_ref[0] = acc_ref[...].astype(out_dtype)
            else:
                o_ref[...] = acc_ref[...].astype(out_dtype)

    a_spec = pl.BlockSpec((tk, tm), lambda i, j, k: (k, i)) if ta else pl.BlockSpec((tm, tk), lambda i, j, k: (i, k))
    if b_shards is None:
        b_spec = pl.BlockSpec((tn, tk), lambda i, j, k: (j, k)) if tb else pl.BlockSpec((tk, tn), lambda i, j, k: (k, j))
    elif tb:
        per = nsh // tk
        b_spec = pl.BlockSpec((1, tn, tk), lambda i, j, k: (s0 + k // per, j, k % per))
    else:
        per = nsh // tn
        b_spec = pl.BlockSpec((1, tk, tn), lambda i, j, k: (s0 + j // per, k, j % per))
    o_spec = pl.BlockSpec((tm, tn), lambda i, j, k: (i, j))
    if out_shards:
        oper = N // out_shards // tn
        out_spec = pl.BlockSpec((1, tm, tn), lambda i, j, k: (j // oper, i, j % oper))
        out_shape = jax.ShapeDtypeStruct((out_shards, M, N // out_shards), out_dtype)
    else:
        out_spec, out_shape = o_spec, jax.ShapeDtypeStruct((M, N), out_dtype)
    return pl.pallas_call(
        body, name=name, grid=(M // tm, N // tn, nk),
        in_specs=[a_spec, b_spec] + ([o_spec] if add is not None else []),
        out_specs=out_spec, out_shape=out_shape,
        scratch_shapes=[pltpu.VMEM((tm, tn), F32)],
        compiler_params=_params(("parallel", "parallel", "arbitrary")),
    )(*((a, b) + ((add,) if add is not None else ())))


def _premix_math(h, gain, shift, scale):
    return _rmsn(h, gain) * (1.0 + scale) + shift


def _premix_fwd(h, gain, tab, *, nc, tm):
    B, T, D = h.shape
    nt, nct = T // tm, nc // tm

    def body(h_ref, g_ref, tab_ref, u_ref):
        tabv = tab_ref[0, 0]
        u_ref[...] = _premix_math(h_ref[0], g_ref[...], tabv[0:1], tabv[1:2]).astype(BF16)

    return pl.pallas_call(
        body, name="premix_fwd", grid=(B, nt),
        in_specs=[pl.BlockSpec((1, tm, D), lambda b, t: (b, t, 0)),
                  pl.BlockSpec((1, D), lambda b, t: (0, 0)),
                  pl.BlockSpec((1, 1, 8, D), lambda b, t: (b, jnp.where(t < nct, 0, 1), 0, 0))],
        out_specs=pl.BlockSpec((tm, D), lambda b, t: (b * nt + t, 0)),
        out_shape=jax.ShapeDtypeStruct((B * T, D), BF16),
        compiler_params=_params(("parallel", "parallel")),
    )(h, gain, tab)


def _premix_bwd(h, gain, tab, du, dres, *, nc, tm):
    B, T, D = h.shape
    nt, nct = T // tm, nc // tm
    N = T - nc

    def body(h_ref, g_ref, tab_ref, du_ref, dres_ref, dx_ref, sums_ref):
        t = pl.program_id(1)
        tabv = tab_ref[0, 0]
        _, vjp = jax.vjp(_premix_math, h_ref[0], g_ref[...], tabv[0:1], tabv[1:2])
        dh, dgain, dshift, dscale = vjp(du_ref[...].astype(F32))

        @pl.when((t == 0) | (t == nct))
        def _():
            sums_ref[...] = jnp.zeros_like(sums_ref)

        sums_ref[0, 0, 0:1, :] += dshift
        sums_ref[0, 0, 1:2, :] += dscale
        sums_ref[0, 0, 2:3, :] += dgain

        @pl.when(t >= nct)
        def _():
            dx_ref[0] = dres_ref[...] + dh

    lat = lambda b, t: jnp.maximum(t - nct, 0)
    return pl.pallas_call(
        body, name="premix_bwd", grid=(B, nt),
        in_specs=[pl.BlockSpec((1, tm, D), lambda b, t: (b, t, 0)),
                  pl.BlockSpec((1, D), lambda b, t: (0, 0)),
                  pl.BlockSpec((1, 1, 8, D), lambda b, t: (b, jnp.where(t < nct, 0, 1), 0, 0)),
                  pl.BlockSpec((tm, D), lambda b, t: (b * nt + t, 0)),
                  pl.BlockSpec((tm, D), lambda b, t: (b * (nt - nct) + lat(b, t), 0))],
        out_specs=[pl.BlockSpec((1, tm, D), lambda b, t: (b, lat(b, t), 0)),
                   pl.BlockSpec((1, 1, 8, D), lambda b, t: (b, jnp.where(t < nct, 0, 1), 0, 0))],
        out_shape=[jax.ShapeDtypeStruct((B, N, D), F32), jax.ShapeDtypeStruct((B, 2, 8, D), F32)],
        compiler_params=_params(("parallel", "arbitrary")),
    )(h, gain, tab, du, dres)


def _merge_math(mgd, mgl, yd, yl, bd, bl):
    return _sigmoid(mgd + bd) * yd + _sigmoid(mgl + bl) * yl


def _merge_fwd(p, ydn, ylru, b_merge, *, B, T, nc, D, col0, tm):
    N = T - nc
    ntl, nt, nct, cb = N // tm, T // tm, nc // tm, col0 // D

    def body(mgd_ref, mgl_ref, yd_ref, yl_ref, bm_ref, o_ref):
        o_ref[...] = _merge_math(mgd_ref[...], mgl_ref[...], yd_ref[...], yl_ref[...],
                                 bm_ref[:, 0:D], bm_ref[:, D:2 * D]).astype(BF16)

    prow = lambda b, t: b * nt + nct + t
    return pl.pallas_call(
        body, name="merge_fwd", grid=(B, ntl),
        in_specs=[pl.BlockSpec((tm, D), lambda b, t: (prow(b, t), cb)),
                  pl.BlockSpec((tm, D), lambda b, t: (prow(b, t), cb + 1)),
                  pl.BlockSpec((tm, D), lambda b, t: (b * ntl + t, 0)),
                  pl.BlockSpec((tm, D), lambda b, t: (b * ntl + t, 0)),
                  pl.BlockSpec((1, 2 * D), lambda b, t: (0, 0))],
        out_specs=pl.BlockSpec((tm, D), lambda b, t: (b * ntl + t, 0)),
        out_shape=jax.ShapeDtypeStruct((B * N, D), BF16),
        compiler_params=_params(("parallel", "parallel")),
    )(p, p, ydn, ylru, b_merge)


def _merge_bwd(p, ydn, ylru, b_merge, dmix, dp, *, B, T, nc, D, col0, tm):
    N = T - nc
    ntl, nt, nct, cb = N // tm, T // tm, nc // tm, col0 // D
    assert col0 % (2 * D) == 0

    def body(mgd_ref, mgl_ref, yd_ref, yl_ref, bm_ref, dm_ref, dp_any, dyd_ref, dyl_ref, dp_ref, sums_ref):
        _, vjp = jax.vjp(_merge_math, mgd_ref[...], mgl_ref[...], yd_ref[...], yl_ref[...],
                         bm_ref[:, 0:D], bm_ref[:, D:2 * D])
        dmgd, dmgl, dyd, dyl, dbd, dbl = vjp(dm_ref[...])
        dyd_ref[...] = dyd.astype(BF16)
        dyl_ref[...] = dyl.astype(BF16)
        dp_ref[:, 0:D] = dmgd.astype(BF16)
        dp_ref[:, D:2 * D] = dmgl.astype(BF16)

        @pl.when((pl.program_id(0) == 0) & (pl.program_id(1) == 0))
        def _():
            sums_ref[...] = jnp.zeros_like(sums_ref)

        sums_ref[0:1, 0:D] += dbd
        sums_ref[0:1, D:2 * D] += dbl

    prow = lambda b, t: b * nt + nct + t
    row = pl.BlockSpec((tm, D), lambda b, t: (b * ntl + t, 0))
    return pl.pallas_call(
        body, name="merge_bwd", grid=(B, ntl),
        in_specs=[pl.BlockSpec((tm, D), lambda b, t: (prow(b, t), cb)),
                  pl.BlockSpec((tm, D), lambda b, t: (prow(b, t), cb + 1)),
                  row, row, pl.BlockSpec((1, 2 * D), lambda b, t: (0, 0)), row,
                  pl.BlockSpec(memory_space=pl.ANY)],
        out_specs=[row, row,
                   pl.BlockSpec((tm, 2 * D), lambda b, t: (prow(b, t), cb // 2)),
                   pl.BlockSpec((8, 2 * D), lambda b, t: (0, 0))],
        out_shape=[jax.ShapeDtypeStruct((B * N, D), BF16), jax.ShapeDtypeStruct((B * N, D), BF16),
                   jax.ShapeDtypeStruct(dp.shape, dp.dtype), jax.ShapeDtypeStruct((8, 2 * D), F32)],
        input_output_aliases={6: 2},
        compiler_params=_params(("arbitrary", "arbitrary")),
    )(p, p, ydn, ylru, b_merge, dmix, dp)


def _post_math(x, mix, g1, gate, g2, sh, sc):
    h1 = x + _rmsn(mix, g1) * gate
    return h1, _rmsn(h1, g2) * (1.0 + sc) + sh


def _post_fwd(x, mix, gains, vecs, *, tm):
    B, N, D = x.shape
    ntl = N // tm

    def body(x_ref, mix_ref, g_ref, v_ref, h1_ref, u2_ref):
        v = v_ref[0]
        h1, u2 = _post_math(x_ref[0], mix_ref[...], g_ref[0:1], v[0:1], g_ref[1:2], v[1:2], v[2:3])
        h1_ref[...] = h1
        u2_ref[...] = u2.astype(BF16)

    row = pl.BlockSpec((tm, D), lambda b, t: (b * ntl + t, 0))
    return pl.pallas_call(
        body, name="post_fwd", grid=(B, ntl),
        in_specs=[pl.BlockSpec((1, tm, D), lambda b, t: (b, t, 0)), row,
                  pl.BlockSpec((8, D), lambda b, t: (0, 0)), pl.BlockSpec((1, 8, D), lambda b, t: (b, 0, 0))],
        out_specs=[row, row],
        out_shape=[jax.ShapeDtypeStruct((B * N, D), F32), jax.ShapeDtypeStruct((B * N, D), BF16)],
        compiler_params=_params(("parallel", "parallel")),
    )(x, mix, gains, vecs)


def _post_bwd(x, mix, gains, vecs, dh1, du2, *, tm):
    B, N, D = x.shape
    ntl = N // tm

    def body(x_ref, mix_ref, g_ref, v_ref, dh1_ref, du2_ref, dx_ref, dmix_ref, sums_ref):
        v = v_ref[0]
        _, vjp = jax.vjp(_post_math, x_ref[0], mix_ref[...], g_ref[0:1], v[0:1], g_ref[1:2], v[1:2], v[2:3])
        dx, dmix, dg1, dgate, dg2, dsh, dsc = vjp((dh1_ref[...], du2_ref[...]))
        dx_ref[...] = dx
        dmix_ref[...] = dmix.astype(BF16)

        @pl.when(pl.program_id(1) == 0)
        def _():
            sums_ref[...] = jnp.zeros_like(sums_ref)

        sums_ref[0, 0:1, :] += dgate
        sums_ref[0, 1:2, :] += dsh
        sums_ref[0, 2:3, :] += dsc
        sums_ref[0, 3:4, :] += dg1
        sums_ref[0, 4:5, :] += dg2

    row = pl.BlockSpec((tm, D), lambda b, t: (b * ntl + t, 0))
    return pl.pallas_call(
        body, name="post_bwd", grid=(B, ntl),
        in_specs=[pl.BlockSpec((1, tm, D), lambda b, t: (b, t, 0)), row,
                  pl.BlockSpec((8, D), lambda b, t: (0, 0)), pl.BlockSpec((1, 8, D), lambda b, t: (b, 0, 0)), row, row],
        out_specs=[row, row, pl.BlockSpec((1, 8, D), lambda b, t: (b, 0, 0))],
        out_shape=[jax.ShapeDtypeStruct((B * N, D), F32), jax.ShapeDtypeStruct((B * N, D), BF16),
                   jax.ShapeDtypeStruct((B, 8, D), F32)],
        compiler_params=_params(("parallel", "arbitrary")),
    )(x, mix, gains, vecs, dh1, du2)


def _final_math(dn, g4, gate5):
    return _rmsn(dn, g4) * gate5


def _final(h1, dn, target, gains, vecs, *, tm):
    B, N, D = target.shape
    ntl = N // tm

    def body(h1_ref, dn_ref, t_ref, g_ref, v_ref, ddn_ref, dout_ref, sums_ref):
        v = v_ref[0]
        y, vjp = jax.vjp(_final_math, dn_ref[...], g_ref[2:3], v[3:4])
        err = h1_ref[...] + y - t_ref[0]
        dout = err * (1.0 / D)
        ddn, dg4, dgate5 = vjp(dout)
        ddn_ref[...] = ddn.astype(BF16)
        dout_ref[...] = dout

        @pl.when(pl.program_id(1) == 0)
        def _():
            sums_ref[...] = jnp.zeros_like(sums_ref)

        sums_ref[0, 0:1, :] += dgate5
        sums_ref[0, 1:2, :] += dg4
        sums_ref[0, 2:3, :] += jnp.sum(err * err, axis=0, keepdims=True) * (0.5 / D)

    row = pl.BlockSpec((tm, D), lambda b, t: (b * ntl + t, 0))
    return pl.pallas_call(
        body, name="final", grid=(B, ntl),
        in_specs=[row, row, pl.BlockSpec((1, tm, D), lambda b, t: (b, t, 0)),
                  pl.BlockSpec((8, D), lambda b, t: (0, 0)), pl.BlockSpec((1, 8, D), lambda b, t: (b, 0, 0))],
        out_specs=[row, row, pl.BlockSpec((1, 8, D), lambda b, t: (b, 0, 0))],
        out_shape=[jax.ShapeDtypeStruct((B * N, D), BF16), jax.ShapeDtypeStruct((B * N, D), F32),
                   jax.ShapeDtypeStruct((B, 8, D), F32)],
        compiler_params=_params(("parallel", "arbitrary")),
    )(h1, dn, target, gains, vecs)


def _shift(x, s):
    s = s % x.shape[0]
    return x if s == 0 else pltpu.roll(x, s, 0)


def _seg_taps(T, nc, width, pad_left):
    t = lax.broadcasted_iota(jnp.int32, (T, 1), 0)
    pos = jnp.where(t < nc, t, t - nc)
    seg = jnp.where(t < nc, nc, T - nc)
    taps = []
    for k in range(width):
        src = pos + (k - pad_left)
        taps.append((pad_left - k, (src >= 0) & (src < seg)))
    return taps


def _grid_taps(N):
    t = lax.broadcasted_iota(jnp.int32, (N, 1), 0)
    wcol = t % GRID_W
    taps = []
    for dr in (-1, 0, 1):
        for dw in (-1, 0, 1):
            off = dr * GRID_W + dw
            ok = (wcol + dw >= 0) & (wcol + dw < GRID_W) & (t + dr * GRID_W >= 0) & (t + dr * GRID_W < N)
            taps.append((-off, ok))
    return taps


def _conv_fwd(x, w, taps):
    y = jnp.zeros_like(x)
    for k, (s, m) in enumerate(taps):
        y = y + w[k:k + 1] * jnp.where(m, _shift(x, s), 0.0)
    return y


def _conv_bwd(x, w, taps, dy):
    dx = jnp.zeros_like(x)
    dws = []
    for k, (s, m) in enumerate(taps):
        dym = jnp.where(m, dy, 0.0)
        dx = dx + w[k:k + 1] * _shift(dym, -s)
        dws.append(jnp.sum(dym * _shift(x, s), axis=0, keepdims=True))
    return dx, jnp.concatenate(dws, axis=0)


def _ffn_act_fwd(F, w9, bias, *, B, N, DFF, tc):
    nj = DFF // tc

    def body(fg_ref, fv_ref, w_ref, b_ref, o_ref):
        fg = _conv_fwd(fg_ref[...], w_ref[...], _grid_taps(N)) + b_ref[...]
        o_ref[...] = (_gelu(fg) * fv_ref[...]).astype(BF16)

    return pl.pallas_call(
        body, name="ffn_act_fwd", grid=(B, nj),
        in_specs=[pl.BlockSpec((N, tc), lambda b, j: (b, j)), pl.BlockSpec((N, tc), lambda b, j: (b, nj + j)),
                  pl.BlockSpec((9, tc), lambda b, j: (0, j)), pl.BlockSpec((1, tc), lambda b, j: (0, j))],
        out_specs=pl.BlockSpec((N, tc), lambda b, j: (b, j)),
        out_shape=jax.ShapeDtypeStruct((B * N, DFF), BF16),
        compiler_params=_params(("parallel", "parallel")),
    )(F, F, w9, bias)


def _ffn_act_bwd(F, w9, bias, df, *, B, N, DFF, tc):
    nj = DFF // tc

    def body(fg_ref, fv_ref, w_ref, b_ref, df_ref, dfg_ref, dfv_ref, dwb_ref):
        taps = _grid_taps(N)
        x = fg_ref[...]
        fg, vjp = jax.vjp(lambda a: _gelu(a), _conv_fwd(x, w_ref[...], taps) + b_ref[...])
        dfl = df_ref[...]
        dfv_ref[...] = (dfl * fg).astype(BF16)
        (dpre,) = vjp(dfl * fv_ref[...])
        dx, dw = _conv_bwd(x, w_ref[...], taps, dpre)
        dfg_ref[...] = dx.astype(BF16)

        @pl.when(pl.program_id(1) == 0)
        def _():
            dwb_ref[...] = jnp.zeros_like(dwb_ref)

        dwb_ref[0:9, :] += dw
        dwb_ref[9:10, :] += jnp.sum(dpre, axis=0, keepdims=True)

    col = pl.BlockSpec((N, tc), lambda j, b: (b, j))
    return pl.pallas_call(
        body, name="ffn_act_bwd", grid=(nj, B),
        in_specs=[col, pl.BlockSpec((N, tc), lambda j, b: (b, nj + j)),
                  pl.BlockSpec((9, tc), lambda j, b: (0, j)), pl.BlockSpec((1, tc), lambda j, b: (0, j)), col],
        out_specs=[col, col, pl.BlockSpec((16, tc), lambda j, b: (0, j))],
        out_shape=[jax.ShapeDtypeStruct((B * N, DFF), BF16), jax.ShapeDtypeStruct((B * N, DFF), BF16),
                   jax.ShapeDtypeStruct((16, DFF), F32)],
        compiler_params=_params(("parallel", "arbitrary")),
    )(F, F, w9, bias, df)


def _dnprep_math(y, is_qk, scale):
    s = _silu(y)
    n = s * lax.rsqrt(jnp.sum(s * s, axis=-1, keepdims=True) + EPS) * scale
    return jnp.where(is_qk, n, s)


def _dnprep_fwd(p, cw, *, B, T, nc, H, HD):
    def body(x_ref, w_ref, o_ref):
        j = pl.program_id(1)
        y = _conv_fwd(x_ref[...], w_ref[...], _seg_taps(T, nc, 4, 2))
        o_ref[...] = _dnprep_math(y, j < 2 * H, jnp.where(j < H, HD ** -0.5, 1.0))

    return pl.pallas_call(
        body, name="dnprep_fwd", grid=(B, 3 * H),
        in_specs=[pl.BlockSpec((T, HD), lambda b, j: (b, j)), pl.BlockSpec((4, HD), lambda b, j: (0, j))],
        out_specs=pl.BlockSpec((T, HD), lambda b, j: (b, j)),
        out_shape=jax.ShapeDtypeStruct((B * T, 3 * H * HD), F32),
        compiler_params=_params(("parallel", "parallel")),
    )(p, cw)


def _dnprep_bwd(p, cw, dqkv, dp, *, B, T, nc, H, HD):
    def body(x_ref, w_ref, d_ref, dp_any, dp_ref, dcw_ref):
        j = pl.program_id(0)
        taps = _seg_taps(T, nc, 4, 2)
        x = x_ref[...]
        y = _conv_fwd(x, w_ref[...], taps)
        is_qk, scale = j < 2 * H, jnp.where(j < H, HD ** -0.5, 1.0)
        _, vjp = jax.vjp(lambda a: _dnprep_math(a, is_qk, scale), y)
        (dy,) = vjp(d_ref[0])
        dx, dw = _conv_bwd(x, w_ref[...], taps, dy)
        dp_ref[...] = dx.astype(BF16)

        @pl.when(pl.program_id(1) == 0)
        def _():
            dcw_ref[...] = jnp.zeros_like(dcw_ref)

        dcw_ref[0:4, :] += dw

    col = pl.BlockSpec((T, HD), lambda j, b: (b, j))
    return pl.pallas_call(
        body, name="dnprep_bwd", grid=(3 * H, B),
        in_specs=[col, pl.BlockSpec((4, HD), lambda j, b: (0, j)),
                  pl.BlockSpec((1, T, HD), lambda j, b: (j // H, b, j % H)), pl.BlockSpec(memory_space=pl.ANY)],
        out_specs=[col, pl.BlockSpec((8, HD), lambda j, b: (0, j))],
        out_shape=[jax.ShapeDtypeStruct(dp.shape, dp.dtype), jax.ShapeDtypeStruct((8, 3 * H * HD), F32)],
        input_output_aliases={3: 0},
        compiler_params=_params(("parallel", "arbitrary")),
    )(p, cw, dqkv, dp)


def _gb_math(ab, alog, dtb, H):
    lane = lax.broadcasted_iota(jnp.int32, ab.shape, 1)
    g = -jnp.exp(alog) * _softplus(ab + dtb)
    return jnp.where(lane < 2 * H, g, jnp.where(lane < 4 * H, _sigmoid(ab), 0.0))


def _gb_fwd(p, prm, *, rows, col0, H, tm):
    def body(x_ref, prm_ref, o_ref):
        o_ref[...] = _gb_math(x_ref[...], prm_ref[0:1], prm_ref[1:2], H)

    return pl.pallas_call(
        body, name="gb_fwd", grid=(rows // tm,),
        in_specs=[pl.BlockSpec((tm, LANES), lambda t: (t, col0 // LANES)), pl.BlockSpec((8, LANES), lambda t: (0, 0))],
        out_specs=pl.BlockSpec((tm, LANES), lambda t: (t, 0)),
        out_shape=jax.ShapeDtypeStruct((rows, LANES), F32),
        compiler_params=_params(("parallel",)),
    )(p, prm)


def _gb_bwd(p, prm, dgb, dp, *, rows, col0, H, tm):
    def body(x_ref, prm_ref, d_ref, dp_any, dp_ref, dprm_ref):
        _, vjp = jax.vjp(lambda a, b, c: _gb_math(a, b, c, H), x_ref[...], prm_ref[0:1], prm_ref[1:2])
        dab, dalog, ddtb = vjp(d_ref[...])
        dp_ref[...] = dab.astype(BF16)

        @pl.when(pl.program_id(0) == 0)
        def _():
            dprm_ref[...] = jnp.zeros_like(dprm_ref)

        dprm_ref[0:1, :] += dalog
        dprm_ref[1:2, :] += ddtb

    blk = pl.BlockSpec((tm, LANES), lambda t: (t, col0 // LANES))
    return pl.pallas_call(
        body, name="gb_bwd", grid=(rows // tm,),
        in_specs=[blk, pl.BlockSpec((8, LANES), lambda t: (0, 0)), pl.BlockSpec((tm, LANES), lambda t: (t, 0)),
                  pl.BlockSpec(memory_space=pl.ANY)],
        out_specs=[blk, pl.BlockSpec((8, LANES), lambda t: (0, 0))],
        out_shape=[jax.ShapeDtypeStruct(dp.shape, dp.dtype), jax.ShapeDtypeStruct((8, LANES), F32)],
        input_output_aliases={3: 0},
        compiler_params=_params(("arbitrary",)),
    )(p, prm, dgb, dp)


def _lru_scans(scans):
    C = scans[0][0].shape[1]
    row = lax.broadcasted_iota(jnp.int32, (SUBLANES, C), 0)
    carries = tuple(jnp.zeros((1, C), F32) for _ in scans)
    for si in range(len(scans[0][4])):
        nb = scans[0][4][si][1] // SUBLANES
        assert all(sc[4][si][1] // SUBLANES == nb for sc in scans)

        def blk(i, carries, si=si, nb=nb):
            out = []
            for (a_ref, b_ref, h_ref, hp_ref, segs), carry in zip(scans, carries):
                start, _, reverse = segs[si]
                r0 = pl.multiple_of(start + (nb - 1 - i if reverse else i) * SUBLANES, SUBLANES)
                A = a_ref[pl.ds(r0, SUBLANES), :]
                Bv = b_ref[pl.ds(r0, SUBLANES), :]
                for s in (1, 2, 4):
                    sh = SUBLANES - s if reverse else s
                    m = (row < SUBLANES - s) if reverse else (row >= s)
                    Bv = jnp.where(m, A * pltpu.roll(Bv, sh, 0) + Bv, Bv)
                    A = jnp.where(m, A * pltpu.roll(A, sh, 0), A)
                Hv = Bv + A * carry
                h_ref[pl.ds(r0, SUBLANES), :] = Hv
                if hp_ref is not None:
                    if reverse:
                        hp = jnp.where(row < SUBLANES - 1, pltpu.roll(Hv, SUBLANES - 1, 0), carry)
                    else:
                        hp = jnp.where(row >= 1, pltpu.roll(Hv, 1, 0), carry)
                    hp_ref[pl.ds(r0, SUBLANES), :] = hp
                out.append(Hv[0:1] if reverse else Hv[SUBLANES - 1:SUBLANES])
            return tuple(out)

        carries = lax.fori_loop(0, nb, blk, carries)


def _lru_orders(T, nc, d):
    N = T - nc
    if d == 0:
        return [(0, nc, False), (nc, N, False)], [(nc, N, True), (0, nc, True)]
    return [(0, nc, True), (nc, N, True)], [(nc, N, False), (0, nc, False)]


def _bdot(a, b, dims=(((1,), (0,)), ((), ()))):
    return lax.dot_general(a.astype(BF16), b.astype(BF16), dims, preferred_element_type=F32)


_NT = (((1,), (1,)), ((), ()))
_TN = (((0,), (0,)), ((), ()))


def _blockdiag(w, C):
    nd, nb, bd, _ = w.shape
    per = C // bd
    out = jnp.einsum('dnpij,pq->dnpiqj', w.reshape(nd, nb // per, per, bd, bd), jnp.eye(per, dtype=w.dtype))
    return out.reshape(nd, nb // per, C, C)


def _blockdiag_extract(dw, bd):
    nd, nj, C, _ = dw.shape
    per = C // bd
    out = jnp.einsum('dnpiqj,pq->dnpij', dw.reshape(nd, nj, per, bd, per, bd), jnp.eye(per, dtype=dw.dtype))
    return out.reshape(nd, nj * per, bd, bd)


def _lru_fwd(p, cw, lv, wr, wi, *, B, T, nc, LW, col0, C):
    N = T - nc
    nj = LW // C

    def body(x_ref, cw_ref, lv_ref, wr_ref, wi_ref, o_ref, a_s, b_s, h_s):
        lv_ = lv_ref[...]
        xc = _conv_fwd(x_ref[:, 0:C], cw_ref[...], _seg_taps(T, nc, 4, 2)) + lv_[0:1]
        for d in (0, 1):
            r = _sigmoid(_bdot(xc, wr_ref[d, 0]) + lv_[1 + d:2 + d])
            i = _sigmoid(_bdot(xc, wi_ref[d, 0]) + lv_[3 + d:4 + d])
            la = -LRU_C * r * _softplus(-lv_[5 + d:6 + d])
            a_s[d] = jnp.exp(la)
            b_s[d] = jnp.sqrt(1.0 - jnp.exp(2.0 * la)) * i * xc
        _lru_scans([(a_s.at[d], b_s.at[d], h_s.at[d], None, _lru_orders(T, nc, d)[0]) for d in (0, 1)])
        o_ref[...] = ((h_s[0, nc:, :] + h_s[1, nc:, :]) * _gelu(x_ref[nc:, C:2 * C])).astype(BF16)

    return pl.pallas_call(
        body, name="lru_fwd", grid=(B, nj),
        in_specs=[pl.BlockSpec((T, 2 * C), lambda b, j: (b, col0 // (2 * C) + j)),
                  pl.BlockSpec((4, C), lambda b, j: (0, j)), pl.BlockSpec((8, C), lambda b, j: (0, j)),
                  pl.BlockSpec((2, 1, C, C), lambda b, j: (0, j, 0, 0)), pl.BlockSpec((2, 1, C, C), lambda b, j: (0, j, 0, 0))],
        out_specs=pl.BlockSpec((N, C), lambda b, j: (b, j)),
        out_shape=jax.ShapeDtypeStruct((B * N, LW), BF16),
        scratch_shapes=[pltpu.VMEM((2, T, C), F32)] * 3,
        compiler_params=_params(("parallel", "parallel")),
    )(p, cw, lv, wr, wi)


def _lru_bwd(p, cw, lv, wr, wi, dy, dp, *, B, T, nc, LW, col0, C):
    N = T - nc
    nj = LW // C

    def body(x_ref, cw_ref, lv_ref, wr_ref, wi_ref, dy_ref, dp_any, dp_ref, dcw_ref, dlv_ref, dwr_ref, dwi_ref,
             a_s, b_s, h_s, hp_s, mu_s, mup_s, dh_s, dxc_s):
        taps = _seg_taps(T, nc, 4, 2)
        lv_ = lv_ref[...]
        xl = x_ref[:, 0:C]
        xc = _conv_fwd(xl, cw_ref[...], taps) + lv_[0:1]
        gel, gelu_vjp = jax.vjp(_gelu, x_ref[nc:, C:2 * C])
        dh_s[0:nc, :] = jnp.zeros((nc, C), F32)
        dh_s[nc:, :] = dy_ref[...] * gel
        dxc_s[...] = jnp.zeros_like(dxc_s)

        @pl.when(pl.program_id(1) == 0)
        def _():
            dcw_ref[...] = jnp.zeros_like(dcw_ref)
            dlv_ref[...] = jnp.zeros_like(dlv_ref)
            dwr_ref[...] = jnp.zeros_like(dwr_ref)
            dwi_ref[...] = jnp.zeros_like(dwi_ref)

        def gates(d):
            lam = lv_[5 + d:6 + d]
            r = _sigmoid(_bdot(xc, wr_ref[d, 0]) + lv_[1 + d:2 + d])
            i = _sigmoid(_bdot(xc, wi_ref[d, 0]) + lv_[3 + d:4 + d])
            sp = _softplus(-lam)
            la = -LRU_C * r * sp
            e2 = jnp.exp(2.0 * la)
            return lam, r, i, sp, la, e2, jnp.sqrt(1.0 - e2)

        for d in (0, 1):
            _, _, i, _, la, _, mult = gates(d)
            a_s[d] = jnp.exp(la)
            b_s[d] = mult * i * xc
        _lru_scans([(a_s.at[d], b_s.at[d], h_s.at[d], hp_s.at[d], _lru_orders(T, nc, d)[0]) for d in (0, 1)])
        for d in (0, 1):
            b_s[d] = a_s[d] * dh_s[...]
        _lru_scans([(a_s.at[d], b_s.at[d], mu_s.at[d], mup_s.at[d], _lru_orders(T, nc, d)[1]) for d in (0, 1)])

        for d in (0, 1):
            lam, r, i, sp, la, e2, mult = gates(d)
            a = a_s[d]
            dinp = dh_s[...] + mup_s[d]
            da = dinp * hp_s[d]
            dmult = dinp * i * xc
            di = dinp * mult * xc
            dla = da * a - dmult * e2 / mult
            dpre_r = (dla * (-LRU_C * sp)) * r * (1.0 - r)
            dpre_i = di * i * (1.0 - i)
            dsp = jnp.sum(dla * (-LRU_C * r), axis=0, keepdims=True)
            dxc_s[...] += dinp * mult * i + _bdot(dpre_r, wr_ref[d, 0], _NT) + _bdot(dpre_i, wi_ref[d, 0], _NT)
            dwr_ref[d, 0] += _bdot(xc, dpre_r, _TN)
            dwi_ref[d, 0] += _bdot(xc, dpre_i, _TN)
            dlv_ref[1 + d:2 + d, :] += jnp.sum(dpre_r, axis=0, keepdims=True)
            dlv_ref[3 + d:4 + d, :] += jnp.sum(dpre_i, axis=0, keepdims=True)
            dlv_ref[5 + d:6 + d, :] += -dsp * _sigmoid(-lam)

        dxc = dxc_s[...]
        dxl, dw = _conv_bwd(xl, cw_ref[...], taps, dxc)
        dcw_ref[0:4, :] += dw
        dlv_ref[0:1, :] += jnp.sum(dxc, axis=0, keepdims=True)
        dp_ref[:, 0:C] = dxl.astype(BF16)
        (dyl,) = gelu_vjp(dy_ref[...] * (h_s[0, nc:, :] + h_s[1, nc:, :]))
        dp_ref[0:nc, C:2 * C] = jnp.zeros((nc, C), BF16)
        dp_ref[nc:, C:2 * C] = dyl.astype(BF16)

    xblk = pl.BlockSpec((T, 2 * C), lambda j, b: (b, col0 // (2 * C) + j))
    wblk = pl.BlockSpec((2, 1, C, C), lambda j, b: (0, j, 0, 0))
    vblk = pl.BlockSpec((8, C), lambda j, b: (0, j))
    return pl.pallas_call(
        body, name="lru_bwd", grid=(nj, B),
        in_specs=[xblk, pl.BlockSpec((4, C), lambda j, b: (0, j)), vblk, wblk, wblk,
                  pl.BlockSpec((N, C), lambda j, b: (b, j)), pl.BlockSpec(memory_space=pl.ANY)],
        out_specs=[xblk, vblk, vblk, wblk, wblk],
        out_shape=[jax.ShapeDtypeStruct(dp.shape, dp.dtype), jax.ShapeDtypeStruct((8, LW), F32),
                   jax.ShapeDtypeStruct((8, LW), F32), jax.ShapeDtypeStruct((2, nj, C, C), F32),
                   jax.ShapeDtypeStruct((2, nj, C, C), F32)],
        scratch_shapes=[pltpu.VMEM((2, T, C), F32)] * 6 + [pltpu.VMEM((T, C), F32)] * 2,
        input_output_aliases={6: 0},
        compiler_params=_params(("parallel", "arbitrary")),
    )(p, cw, lv, wr, wi, dy, dp)


def _chunk_masks(upper):
    i = lax.broadcasted_iota(jnp.int32, (CHUNK, CHUNK), 0)
    j = lax.broadcasted_iota(jnp.int32, (CHUNK, CHUNK), 1)
    ahead = jnp.where(upper, j - i, i - j)
    return i == j, ahead >= 0, ahead > 0


def _col2row(c, eye):
    return jnp.sum(jnp.where(eye, c, 0.0), axis=0, keepdims=True)


def _row2col(r, eye):
    return jnp.sum(jnp.where(eye, r, 0.0), axis=1, keepdims=True)


def _rowsum(x):
    return jnp.sum(x, axis=1, keepdims=True)


_INV_BASE = 8


def _unit_tri_inverses(Ls):
    G = len(Ls)
    W = G * CHUNK
    blk = (lax.broadcasted_iota(jnp.int32, (W, W), 0) // CHUNK) == (lax.broadcasted_iota(jnp.int32, (W, W), 1) // CHUNK)
    ri = lax.broadcasted_iota(jnp.int32, (CHUNK, W), 0)
    ci = lax.broadcasted_iota(jnp.int32, (CHUNK, W), 1) % CHUNK

    def bd(b):
        return jnp.where(blk, jnp.tile(b, (G, 1)), jnp.zeros((), BF16))

    def pdot(a, b):
        return jnp.dot(a.astype(BF16), bd(b.astype(BF16)), preferred_element_type=F32)

    Lc = Ls[0] if G == 1 else jnp.concatenate(Ls, axis=1)
    s = _INV_BASE
    Xp = -jnp.where(ri // s == ci // s, Lc, 0.0)
    Rm = Xp
    for _ in range(int(math.log2(s)) - 1):
        Xp = pdot(Xp, Xp)
        Rm = Rm + Xp + pdot(Rm, Xp)
    while s < CHUNK:
        E = jnp.where((ri // (2 * s) == ci // (2 * s)) & (ri // s != ci // s), Lc, 0.0)
        DE = E + pdot(Rm, E)
        Rm = Rm - (DE + pdot(DE, Rm))
        s *= 2
    eye = _chunk_masks(False)[0]
    return [jnp.where(eye, 1.0, 0.0) + Rm[:, g * CHUNK:(g + 1) * CHUNK] for g in range(G)]


def _delta_chunk_common(q, k, v, gcol, bcol, upper):
    eye, incl, strict = _chunk_masks(upper)
    gc = _rowsum(jnp.where(incl, _col2row(gcol, eye), 0.0))
    D = jnp.where(incl, jnp.exp(jnp.minimum(gc - _col2row(gc, eye), 0.0)), 0.0)
    kb = k * bcol
    AP = _bdot(jnp.concatenate([kb, q], axis=0), k, _NT)
    A = AP[:CHUNK]
    L = jnp.where(strict, A * D, 0.0)
    eg = jnp.exp(gc)
    gl = jnp.sum(gcol, axis=0, keepdims=True)
    attn = jnp.where(incl, AP[CHUNK:] * D, 0.0)
    return dict(eye=eye, incl=incl, strict=strict, gc=gc, D=D, kb=kb, A=A, L=L, eg=eg, gl=gl, egl=jnp.exp(gl),
                attn=attn, kbe=kb * eg, vb=v * bcol, qe=q * eg, kd=k * jnp.exp(gl - gc))


def _delta_group_pre(chunks, upper):
    cs = [_delta_chunk_common(*ch, upper) for ch in chunks]
    out = []
    for c, Tm in zip(cs, _unit_tri_inverses([c["L"] for c in cs])):
        dk = c["kbe"].shape[1]
        wu = _bdot(Tm, jnp.concatenate([c["kbe"], c["vb"]], axis=1))
        KN = _bdot(c["kd"], wu, _TN)
        QO = _bdot(c["attn"], wu)
        out.append((Tm, KN[:, :dk], KN[:, dk:], c["qe"] - QO[:, :dk], QO[:, dk:], c["egl"]))
    return out


def _delta_chunk_bwd(q, k, v, gcol, bcol, S, Tm, do, dS2, upper):
    c = _delta_chunk_common(q, k, v, gcol, bcol, upper)
    eye, incl, strict, D, eg, egl = c["eye"], c["incl"], c["strict"], c["D"], c["eg"], c["egl"]
    kb, kbe, vb, qe, kd, attn = c["kb"], c["kbe"], c["vb"], c["qe"], c["kd"], c["attn"]
    dkk = kbe.shape[1]
    wu = _bdot(Tm, jnp.concatenate([kbe, vb], axis=1))
    w = wu[:, :dkk]
    vn = wu[:, dkk:] - _bdot(w, S)
    dvn = _bdot(kd, dS2) + _bdot(attn, do, _TN)
    dkd = _bdot(vn, dS2, _NT)
    dgl = jnp.sum(_rowsum(dS2 * S), axis=0, keepdims=True) * egl
    dqa = _bdot(do, jnp.concatenate([S, vn], axis=0), _NT)
    dqe = dqa[:, :dkk]
    dattn = jnp.where(incl, dqa[:, dkk:], 0.0)
    dw = -_bdot(dvn, S, _NT)
    r = _rowsum(dkd * kd)
    dk = dkd * jnp.exp(c["gl"] - c["gc"])
    dgl = dgl + jnp.sum(r, axis=0, keepdims=True)
    dgc = _rowsum(dqe * qe) - r
    E = dattn * attn
    dvw = jnp.concatenate([dvn, dw], axis=1)
    dTm = _bdot(dvw, jnp.concatenate([vb, kbe], axis=1), _NT)
    dvk = _bdot(Tm, dvw, _TN)
    dvb = dvk[:, :dvn.shape[1]]
    dv = dvb * bcol
    dbeta = _rowsum(dvb * v)
    dkbe = dvk[:, dvn.shape[1]:]
    dkb = dkbe * eg
    dgc = dgc + _rowsum(dkbe * kbe)
    dL = jnp.where(strict, -_bdot(Tm, _bdot(dTm, Tm, _NT), _TN), 0.0)
    dA = dL * D
    E = E + dL * c["L"]
    PA = jnp.concatenate([dattn * D, dA], axis=0)
    PAk = _bdot(PA, k)
    dq = dqe * eg + PAk[:CHUNK]
    dkb = dkb + PAk[CHUNK:]
    dk = dk + _bdot(PA, jnp.concatenate([q, kb], axis=0), _TN) + dkb * bcol
    dbeta = dbeta + _rowsum(dkb * k)
    dgc = dgc + _rowsum(E) - _row2col(jnp.sum(E, axis=0, keepdims=True), eye)
    dg = _row2col(jnp.sum(jnp.where(incl, dgc, 0.0), axis=0, keepdims=True), eye) + dgl
    return dq, dk, dv, dg, dbeta


def _delta_group(n):
    return max(g for g in range(1, 2 * LANES // CHUNK + 1) if n % g == 0)


def _delta_chunk_at(T, nc, d, i):
    n, ncc = T // CHUNK, nc // CHUNK
    desc = jnp.where(i < ncc, ncc - 1 - i, n - 1 - (i - ncc))
    if isinstance(d, int):
        return i if d == 0 else desc
    return jnp.where(d == 0, i, desc)


def _dn_out_math(o, onorm, z):
    return _rmsn(o, onorm) * _silu(z)


def _delta_fwd(qkv, gb, p, onorm, *, B, T, nc, H, HD):
    N = T - nc
    n = T // CHUNK
    G = _delta_group(n)

    def body(q_ref, k_ref, v_ref, gb_ref, z_ref, on_ref, y_ref, o_ref, Tm_ref, K_ref, S_ref, Qp_ref, eg_ref,
             N_s, O0_s, o_s):
        h = pl.program_id(1)
        lane = lax.broadcasted_iota(jnp.int32, (CHUNK, LANES), 1)

        def pre(g, carry):
            cs = [g * G + i for i in range(G)]
            rows = [pl.ds(pl.multiple_of(c * CHUNK, CHUNK), CHUNK) for c in cs]
            for d in (0, 1):
                chunks = []
                for r in rows:
                    gbb = gb_ref[r, :]
                    chunks.append((q_ref[r, :], k_ref[r, :], v_ref[r, :],
                                   _rowsum(jnp.where(lane == d * H + h, gbb, 0.0)),
                                   _rowsum(jnp.where(lane == 2 * H + d * H + h, gbb, 0.0))))
                for c, r, (Tm, K, Nn, Qp, O0, egl) in zip(cs, rows, _delta_group_pre(chunks, d == 1)):
                    Tm_ref[0, d * n + c] = Tm
                    K_ref[0, d * n + c] = K.astype(BF16)
                    N_s[d * n + c] = Nn
                    Qp_ref[0, d, r, :] = Qp.astype(BF16)
                    O0_s[d, r, :] = O0
                    eg_ref[0, d * n + c] = jnp.broadcast_to(egl, (SUBLANES, HD))
            return carry

        lax.fori_loop(0, n // G, pre, 0)

        def step(i, Ss):
            out = []
            for d in (0, 1):
                c = _delta_chunk_at(T, nc, d, i)
                rows = pl.ds(pl.multiple_of(c * CHUNK, CHUNK), CHUNK)
                S_ref[0, d * n + c] = Ss[d]
                Sb = Ss[d].astype(BF16)
                o_s[d, rows, :] = jnp.dot(Qp_ref[0, d, rows, :], Sb, preferred_element_type=F32) + O0_s[d, rows, :]
                out.append(eg_ref[0, d * n + c][0:1] * Ss[d] + N_s[d * n + c]
                           - jnp.dot(K_ref[0, d * n + c], Sb, preferred_element_type=F32))
            return tuple(out)

        lax.fori_loop(0, n, step, (jnp.zeros((HD, HD), F32), jnp.zeros((HD, HD), F32)))
        o = o_s[0, nc:, :] + o_s[1, nc:, :]
        o_ref[...] = o
        y_ref[...] = _dn_out_math(o, on_ref[...], z_ref[nc:, :]).astype(BF16)

    col = lambda off: pl.BlockSpec((T, HD), lambda b, h: (b, off + h))
    lat = pl.BlockSpec((N, HD), lambda b, h: (b, h))
    per = lambda *blk: pl.BlockSpec((1, *blk), lambda b, h: (b * H + h, 0, 0, 0))
    return pl.pallas_call(
        body, name="delta_fwd", grid=(B, H),
        in_specs=[col(0), col(H), col(2 * H), pl.BlockSpec((T, LANES), lambda b, h: (b, 0)), col(3 * H),
                  pl.BlockSpec((1, HD), lambda b, h: (0, 0))],
        out_specs=[lat, lat, per(2 * n, CHUNK, CHUNK), per(2 * n, HD, HD), per(2 * n, HD, HD), per(2, T, HD),
                   per(2 * n, SUBLANES, HD)],
        out_shape=[jax.ShapeDtypeStruct((B * N, H * HD), BF16), jax.ShapeDtypeStruct((B * N, H * HD), F32),
                   jax.ShapeDtypeStruct((B * H, 2 * n, CHUNK, CHUNK), F32),
                   jax.ShapeDtypeStruct((B * H, 2 * n, HD, HD), BF16), jax.ShapeDtypeStruct((B * H, 2 * n, HD, HD), F32),
                   jax.ShapeDtypeStruct((B * H, 2, T, HD), BF16), jax.ShapeDtypeStruct((B * H, 2 * n, SUBLANES, HD), F32)],
        scratch_shapes=[pltpu.VMEM((2 * n, HD, HD), F32), pltpu.VMEM((2, T, HD), F32), pltpu.VMEM((2, T, HD), F32)],
        compiler_params=_params(("parallel", "parallel")),
    )(qkv, qkv, qkv, gb, p, onorm)


def _delta_bwd(qkv, gb, p, onorm, o, res, dy, dp, *, B, T, nc, H, HD):
    N = T - nc
    n = T // CHUNK

    def body(q_ref, k_ref, v_ref, gb_ref, z_ref, on_ref, o_ref, dy_ref, Tm_ref, K_ref, S_ref, Qp_ref, eg_ref, dp_any,
             dqkv_ref, dgb_ref, dp_ref, don_ref, do_s, R_s, dS_s):
        h, d = pl.program_id(1), pl.program_id(2)
        lane = lax.broadcasted_iota(jnp.int32, (CHUNK, LANES), 1)

        @pl.when(d == 0)
        def _():
            _, vjp = jax.vjp(_dn_out_math, o_ref[...], on_ref[...], z_ref[nc:, :])
            do, don, dz = vjp(dy_ref[...])
            do_s[0:nc, :] = jnp.zeros((nc, HD), F32)
            do_s[nc:, :] = do
            dp_ref[0:nc, :] = jnp.zeros((nc, HD), BF16)
            dp_ref[nc:, :] = dz.astype(BF16)
            dqkv_ref[...] = jnp.zeros_like(dqkv_ref)

            @pl.when(h == 0)
            def _():
                don_ref[...] = jnp.zeros_like(don_ref)
                dgb_ref[...] = jnp.zeros_like(dgb_ref)

            don_ref[0, 0:1, :] += don

        def r_of(c, carry):
            rows = pl.ds(pl.multiple_of(c * CHUNK, CHUNK), CHUNK)
            R_s[c] = lax.dot_general(Qp_ref[0, 0, rows, :], do_s[rows, :].astype(BF16), _TN, preferred_element_type=F32)
            return carry

        lax.fori_loop(0, n, r_of, 0)

        def bwd_step(i, dS):
            c = _delta_chunk_at(T, nc, d, n - 1 - i)
            dS_s[c] = dS
            return (eg_ref[0, c][0:1] * dS + R_s[c]
                    - lax.dot_general(K_ref[0, c], dS.astype(BF16), _TN, preferred_element_type=F32))

        lax.fori_loop(0, n, bwd_step, jnp.zeros((HD, HD), F32))

        def grads(c, carry):
            rows = pl.ds(pl.multiple_of(c * CHUNK, CHUNK), CHUNK)
            gbb = gb_ref[rows, :]
            gcol = _rowsum(jnp.where(lane == d * H + h, gbb, 0.0))
            bcol = _rowsum(jnp.where(lane == 2 * H + d * H + h, gbb, 0.0))
            dq, dk, dv, dg, dbeta = _delta_chunk_bwd(q_ref[rows, :], k_ref[rows, :], v_ref[rows, :], gcol, bcol,
                                                     S_ref[0, c], Tm_ref[0, c], do_s[rows, :], dS_s[c], d == 1)
            dqkv_ref[0, rows, :] += dq
            dqkv_ref[1, rows, :] += dk
            dqkv_ref[2, rows, :] += dv
            dgb_ref[rows, :] += (jnp.where(lane == d * H + h, dg, 0.0)
                                 + jnp.where(lane == 2 * H + d * H + h, dbeta, 0.0))
            return carry

        lax.fori_loop(0, n, grads, 0)

    col = lambda off: pl.BlockSpec((T, HD), lambda b, h, d: (b, off + h))
    lat = pl.BlockSpec((N, HD), lambda b, h, d: (b, h))
    per = lambda *blk: pl.BlockSpec((1, *blk), lambda b, h, d: (b * H + h, d, 0, 0))
    return pl.pallas_call(
        body, name="delta_bwd", grid=(B, H, 2),
        in_specs=[col(0), col(H), col(2 * H), pl.BlockSpec((T, LANES), lambda b, h, d: (b, 0)), col(3 * H),
                  pl.BlockSpec((1, HD), lambda b, h, d: (0, 0)), lat, lat,
                  per(n, CHUNK, CHUNK), per(n, HD, HD), per(n, HD, HD), per(1, T, HD), per(n, SUBLANES, HD),
                  pl.BlockSpec(memory_space=pl.ANY)],
        out_specs=[pl.BlockSpec((3, T, HD), lambda b, h, d: (0, b, h)), pl.BlockSpec((T, LANES), lambda b, h, d: (b, 0)),
                   col(3 * H), pl.BlockSpec((1, 8, HD), lambda b, h, d: (b, 0, 0))],
        out_shape=[jax.ShapeDtypeStruct((3, B * T, H * HD), F32), jax.ShapeDtypeStruct((B * T, LANES), F32),
                   jax.ShapeDtypeStruct(dp.shape, dp.dtype), jax.ShapeDtypeStruct((B, 8, HD), F32)],
        scratch_shapes=[pltpu.VMEM((T, HD), F32), pltpu.VMEM((n, HD, HD), F32), pltpu.VMEM((n, HD, HD), F32)],
        input_output_aliases={13: 2},
        compiler_params=_params(("parallel", "arbitrary", "arbitrary")),
    )(qkv, qkv, qkv, gb, p, onorm, o, dy, *res, dp)


def _rowwise(fn, ins, out_dtypes, *, name, tm=256, mult=16):
    R, W = ins[0].shape
    tm = _tile(R, tm, mult)

    def body(*refs):
        outs = fn(*[r[...] for r in refs[:len(ins)]])
        for o_ref, o in zip(refs[len(ins):], outs):
            o_ref[...] = o.astype(o_ref.dtype)

    spec = pl.BlockSpec((tm, W), lambda i: (i, 0))
    return pl.pallas_call(
        body, name=name, grid=(R // tm,), in_specs=[spec] * len(ins), out_specs=[spec] * len(out_dtypes),
        out_shape=[jax.ShapeDtypeStruct((R, W), dt) for dt in out_dtypes],
        compiler_params=_params(("parallel",)),
    )(*ins)


def _sum_lead(x, *, name, tm=256, mult=16):
    S, R, W = x.shape
    tm = _tile(R, tm, mult)

    def body(*refs):
        acc = refs[0][0].astype(F32)
        for r in refs[1:S]:
            acc = acc + r[0].astype(F32)
        refs[S][...] = acc

    return pl.pallas_call(
        body, name=name, grid=(R // tm,),
        in_specs=[pl.BlockSpec((1, tm, W), functools.partial(lambda s, i: (s, i, 0), s)) for s in range(S)],
        out_specs=pl.BlockSpec((tm, W), lambda i: (i, 0)),
        out_shape=jax.ShapeDtypeStruct((R, W), F32),
        compiler_params=_params(("parallel",)),
    )(*([x] * S))


def _adamw_math(w, g, m, v):
    m = ADAM_B1 * m + (1.0 - ADAM_B1) * g
    v = ADAM_B2 * v + (1.0 - ADAM_B2) * (g * g)
    m_hat = m / (1.0 - ADAM_B1 ** ADAM_STEP)
    v_hat = v / (1.0 - ADAM_B2 ** ADAM_STEP)
    return -ADAM_LR * (m_hat / (jnp.sqrt(v_hat) + ADAM_EPS) + ADAM_WD * w), m, v


def _adamw(w, g, m, v, *, name):
    tm = max(SUBLANES, (256 * 1024) // w.shape[1] // SUBLANES * SUBLANES)
    return _rowwise(_adamw_math, [w, g, m, v], [F32, F32, F32], name=name, tm=tm, mult=SUBLANES)


def _me():
    return lax.axis_index("x"), lax.axis_index("y"), lax.axis_index("c")


def _allgather_small(v):
    R, W = v.shape

    def body(x_ref, out_ref, send_sems, recv_sems, local_sem):
        x, y, c = _me()
        me, sibling = (x, y, c), (x, y, 1 - c)
        chips = [(1 - x, y), (x, 1 - y), (1 - x, 1 - y)]

        def slot(px, py, pc):
            return out_ref.at[4 * px + 2 * py + pc]

        def copy(k, block, to, src=None):
            return pltpu.make_async_remote_copy(
                src_ref=slot(*block) if src is None else src, dst_ref=slot(*block),
                send_sem=send_sems.at[k], recv_sem=recv_sems.at[k], device_id=to, device_id_type=MESH)

        mine = pltpu.make_async_copy(x_ref, slot(*me), local_sem)
        mine.start()
        first = [copy(0, me, sibling, src=x_ref)]
        first += [copy(1 + j, me, (*chip, c), src=x_ref) for j, chip in enumerate(chips)]
        for cp in first:
            cp.start()
        passed = [copy(4 + j, (*chip, c), sibling) for j, chip in enumerate(chips)]
        for j, chip in enumerate(chips):
            copy(1 + j, (*chip, c), me).wait_recv()
            passed[j].start()
        copy(0, sibling, me).wait_recv()
        for j, chip in enumerate(chips):
            copy(4 + j, (*chip, 1 - c), me).wait_recv()
        for cp in first + passed:
            cp.wait_send()
        mine.wait()

    return pl.pallas_call(
        body, name="allgather_small", out_shape=jax.ShapeDtypeStruct((8, R, W), v.dtype),
        in_specs=[pl.BlockSpec(memory_space=pltpu.VMEM)], out_specs=pl.BlockSpec(memory_space=pltpu.VMEM),
        scratch_shapes=[pltpu.SemaphoreType.DMA((7,)), pltpu.SemaphoreType.DMA((7,)), pltpu.SemaphoreType.DMA],
        compiler_params=_params(),
    )(v)


_ANY = pl.BlockSpec(memory_space=pl.ANY)


def _allgather_halves(shards, *, name):
    nw = len(shards)

    def body(*refs):
        x_refs, out_refs = refs[:nw], refs[nw:2 * nw]
        send_sems, recv_sems, local_sems = refs[2 * nw:]
        x, y, c = _me()
        me, sibling = (x, y, c), (x, y, 1 - c)
        chips = [(1 - x, y), (x, 1 - y), (1 - x, 1 - y)]

        def slot(w, px, py, pc):
            return out_refs[w].at[4 * px + 2 * py + pc]

        def copy(w, k, block, to, src=None):
            return pltpu.make_async_remote_copy(
                src_ref=slot(w, *block) if src is None else src, dst_ref=slot(w, *block),
                send_sem=send_sems.at[w, k], recv_sem=recv_sems.at[w, k], device_id=to, device_id_type=MESH)

        started, local = [], []
        for w in range(nw):
            half = shards[w].shape[0] // 2
            own = x_refs[w].at[pl.ds(c * half, half), :]
            mine = pltpu.make_async_copy(own, slot(w, *me), local_sems.at[w])
            mine.start()
            first = [copy(w, 0, me, sibling, src=own)]
            first += [copy(w, 1 + j, me, (*chip, c), src=own) for j, chip in enumerate(chips)]
            for cp in first:
                cp.start()
            started += first
            local.append(mine)
        for w in range(nw):
            for j, chip in enumerate(chips):
                copy(w, 1 + j, (*chip, c), me).wait_recv()
                fwd = copy(w, 4 + j, (*chip, c), sibling)
                fwd.start()
                started.append(fwd)
        for w in range(nw):
            copy(w, 0, sibling, me).wait_recv()
            for j, chip in enumerate(chips):
                copy(w, 4 + j, (*chip, 1 - c), me).wait_recv()
        for cp in started:
            cp.wait_send()
        for cp in local:
            cp.wait()

    return pl.pallas_call(
        body, name=name,
        out_shape=[jax.ShapeDtypeStruct((8, s.shape[0] // 2, s.shape[1]), s.dtype) for s in shards],
        in_specs=[_ANY] * nw, out_specs=[_ANY] * nw,
        scratch_shapes=[pltpu.SemaphoreType.DMA((nw, 7)), pltpu.SemaphoreType.DMA((nw, 7)), pltpu.SemaphoreType.DMA((nw,))],
        compiler_params=_params(),
    )(*shards)


def _sibling_send_halves(arrs, *, name):
    nw = len(arrs)

    def body(*refs):
        x_refs, out_refs, send_sems, recv_sems = refs[:nw], refs[nw:2 * nw], refs[2 * nw], refs[2 * nw + 1]
        x, y, c = _me()
        cps = []
        for w in range(nw):
            half = arrs[w].shape[1] // 2
            cp = pltpu.make_async_remote_copy(
                src_ref=x_refs[w].at[:, pl.ds((1 - c) * half, half), :], dst_ref=out_refs[w],
                send_sem=send_sems.at[w], recv_sem=recv_sems.at[w], device_id=(x, y, 1 - c), device_id_type=MESH)
            cp.start()
            cps.append(cp)
        for cp in cps:
            cp.wait()

    return pl.pallas_call(
        body, name=name,
        out_shape=[jax.ShapeDtypeStruct((a.shape[0], a.shape[1] // 2, a.shape[2]), a.dtype) for a in arrs],
        in_specs=[_ANY] * nw, out_specs=[_ANY] * nw,
        scratch_shapes=[pltpu.SemaphoreType.DMA((nw,)), pltpu.SemaphoreType.DMA((nw,))],
        compiler_params=_params(),
    )(*arrs)


def _sibling_swap(arrs, *, name):
    nw = len(arrs)

    def body(*refs):
        x_refs, out_refs, send_sems, recv_sems = refs[:nw], refs[nw:2 * nw], refs[2 * nw], refs[2 * nw + 1]
        x, y, c = _me()
        cps = []
        for w in range(nw):
            cp = pltpu.make_async_remote_copy(
                src_ref=x_refs[w], dst_ref=out_refs[w], send_sem=send_sems.at[w], recv_sem=recv_sems.at[w],
                device_id=(x, y, 1 - c), device_id_type=MESH)
            cp.start()
            cps.append(cp)
        for cp in cps:
            cp.wait()

    return pl.pallas_call(
        body, name=name, out_shape=[jax.ShapeDtypeStruct(a.shape, a.dtype) for a in arrs],
        in_specs=[_ANY] * nw, out_specs=[_ANY] * nw,
        scratch_shapes=[pltpu.SemaphoreType.DMA((nw,)), pltpu.SemaphoreType.DMA((nw,))],
        compiler_params=_params(),
    )(*arrs)


def _adamw_halves(w, own, sib, m, v, c_arr, *, name):
    r, cols = w.shape
    h = r // 2
    tm = _tile(h, max(SUBLANES, (192 * 1024) // cols // SUBLANES * SUBLANES), SUBLANES)
    nb = h // tm

    def body(c_ref, w_ref, own_ref, sib_ref, m_ref, v_ref, g_out, d_out, m_out, v_out):
        g = jnp.where(pl.program_id(0) == c_ref[0], own_ref[...], sib_ref[...])
        g_out[...] = g
        d_out[...], m_out[...], v_out[...] = _adamw_math(w_ref[...], g, m_ref[...], v_ref[...])

    full = pl.BlockSpec((tm, cols), lambda hh, i, c_ref: (hh * nb + i, 0))
    half = pl.BlockSpec((tm, cols), lambda hh, i, c_ref: (i, 0))
    return pl.pallas_call(
        body, name=name,
        grid_spec=pltpu.PrefetchScalarGridSpec(num_scalar_prefetch=1, grid=(2, nb),
                                               in_specs=[full, half, half, full, full], out_specs=[full] * 4),
        out_shape=[jax.ShapeDtypeStruct((r, cols), F32)] * 4,
        compiler_params=_params(("parallel", "parallel")),
    )(c_arr, w, own, sib, m, v)


def _chip_exchange(arrs, *, name):
    nw = len(arrs)

    def body(*refs):
        x_refs, out_refs = refs[:nw], refs[nw:2 * nw]
        send_sems, recv_sems = refs[2 * nw:]
        x, y, c = _me()
        s_me = 2 * x + y
        chips = [(1 - x, y), (x, 1 - y), (1 - x, 1 - y)]
        started = []
        for w in range(nw):
            for k, (px, py) in enumerate(chips):
                cp = pltpu.make_async_remote_copy(
                    src_ref=x_refs[w].at[2 * px + py], dst_ref=out_refs[w].at[s_me], send_sem=send_sems.at[w, k],
                    recv_sem=recv_sems.at[w, k], device_id=(px, py, c), device_id_type=MESH)
                cp.start()
                started.append(cp)
        for w in range(nw):
            for k, (px, py) in enumerate(chips):
                pltpu.make_async_remote_copy(
                    src_ref=x_refs[w].at[s_me], dst_ref=out_refs[w].at[2 * px + py], send_sem=send_sems.at[w, k],
                    recv_sem=recv_sems.at[w, k], device_id=(px, py, c), device_id_type=MESH).wait_recv()
        for cp in started:
            cp.wait_send()

    return pl.pallas_call(
        body, name=name, out_shape=[jax.ShapeDtypeStruct(a.shape, a.dtype) for a in arrs],
        in_specs=[_ANY] * nw, out_specs=[_ANY] * nw,
        scratch_shapes=[pltpu.SemaphoreType.DMA((nw, 3)), pltpu.SemaphoreType.DMA((nw, 3))],
        compiler_params=_params(),
    )(*arrs)


_HBM = pl.BlockSpec(memory_space=pltpu.HBM)
_SEM = pl.BlockSpec(memory_space=pltpu.SEMAPHORE)
_DATAFLOW = pltpu.SideEffectType.DATAFLOW_SIDE_EFFECTING


def _chip_exchange_start(arrs, *, name):
    nw = len(arrs)

    def body(*refs):
        x_refs, land_refs, send_sems, recv_sems = refs[:nw], refs[nw:2 * nw], refs[2 * nw], refs[2 * nw + 1]
        token = refs[-1]
        x, y, c = _me()
        s_me = 2 * x + y
        for w in range(nw):
            for k, (px, py) in enumerate([(1 - x, y), (x, 1 - y), (1 - x, 1 - y)]):
                pltpu.make_async_remote_copy(
                    src_ref=x_refs[w].at[2 * px + py], dst_ref=land_refs[w].at[s_me], send_sem=send_sems.at[3 * w + k],
                    recv_sem=recv_sems.at[3 * w + k], device_id=(px, py, c), device_id_type=MESH).start()
        token[...] = jnp.zeros_like(token)

    hbm = [pltpu.HBM(a.shape, a.dtype) for a in arrs]
    outs = pl.pallas_call(
        body, name=name,
        out_shape=(pltpu.SemaphoreType.DMA((3 * nw,)), pltpu.SemaphoreType.DMA((3 * nw,)), *hbm, *hbm,
                   jax.ShapeDtypeStruct((SUBLANES, LANES), F32)),
        in_specs=[_HBM] * (2 * nw), out_specs=(_SEM, _SEM, *([_HBM] * (2 * nw)), pl.BlockSpec(memory_space=pltpu.VMEM)),
        input_output_aliases={i: 2 + i for i in range(2 * nw)},
        compiler_params=pltpu.CompilerParams(has_side_effects=_DATAFLOW),
    )(*[pltpu.with_memory_space_constraint(a, pltpu.HBM) for a in arrs],
      *[pltpu.with_memory_space_constraint(lax.empty(a.shape, a.dtype), pltpu.HBM) for a in arrs])
    return outs[0], outs[1], list(outs[2:2 + nw]), list(outs[2 + nw:2 + 2 * nw]), outs[-1]


def _allgather_start(shards, *, name):
    nw = len(shards)

    def body(*refs):
        x_refs, land_refs, send_sems, recv_sems = refs[:nw], refs[nw:2 * nw], refs[2 * nw], refs[2 * nw + 1]
        token = refs[-1]
        x, y, c = _me()
        me = 4 * x + 2 * y + c
        for w in range(nw):
            half = shards[w].shape[0] // 2
            own = x_refs[w].at[pl.ds(c * half, half), :]
            for k, to in enumerate([(x, y, 1 - c), (1 - x, y, c), (x, 1 - y, c), (1 - x, 1 - y, c)]):
                pltpu.make_async_remote_copy(
                    src_ref=own, dst_ref=land_refs[w].at[me], send_sem=send_sems.at[4 * w + k],
                    recv_sem=recv_sems.at[4 * w + k], device_id=to, device_id_type=MESH).start()
        token[...] = jnp.zeros_like(token)

    lands = [pltpu.HBM((8, s.shape[0] // 2, s.shape[1]), s.dtype) for s in shards]
    outs = pl.pallas_call(
        body, name=name,
        out_shape=(pltpu.SemaphoreType.DMA((4 * nw,)), pltpu.SemaphoreType.DMA((4 * nw,)),
                   *[pltpu.HBM(s.shape, s.dtype) for s in shards], *lands, jax.ShapeDtypeStruct((SUBLANES, LANES), F32)),
        in_specs=[_HBM] * (2 * nw), out_specs=(_SEM, _SEM, *([_HBM] * (2 * nw)), pl.BlockSpec(memory_space=pltpu.VMEM)),
        input_output_aliases={i: 2 + i for i in range(2 * nw)},
        compiler_params=pltpu.CompilerParams(has_side_effects=_DATAFLOW),
    )(*[pltpu.with_memory_space_constraint(s, pltpu.HBM) for s in shards],
      *[pltpu.with_memory_space_constraint(lax.empty(l.shape, l.dtype), pltpu.HBM) for l in lands])
    return outs[0], outs[1], list(outs[2:2 + nw]), list(outs[2 + nw:2 + 2 * nw]), outs[-1]


def _allgather_wait(send_sems, recv_sems, srcs, lands, after, *, name):
    nw = len(srcs)

    def body(*refs):
        x_refs, land_refs, send_sems, recv_sems = refs[:nw], refs[nw:2 * nw], refs[2 * nw], refs[2 * nw + 1]
        x, y, c = _me()
        for w in range(nw):
            half = srcs[w].shape[0] // 2
            own = x_refs[w].at[pl.ds(c * half, half), :]
            for k, (px, py, pc) in enumerate([(x, y, 1 - c), (1 - x, y, c), (x, 1 - y, c), (1 - x, 1 - y, c)]):
                cp = pltpu.make_async_remote_copy(
                    src_ref=own, dst_ref=land_refs[w].at[4 * px + 2 * py + pc], send_sem=send_sems.at[4 * w + k],
                    recv_sem=recv_sems.at[4 * w + k], device_id=(px, py, pc), device_id_type=MESH)
                cp.wait_send()
                cp.wait_recv()

    outs = pl.pallas_call(
        body, name=name,
        out_shape=(*[pltpu.HBM(a.shape, a.dtype) for a in srcs], *[pltpu.HBM(a.shape, a.dtype) for a in lands]),
        in_specs=[_HBM] * (2 * nw) + [_SEM, _SEM, _ANY], out_specs=tuple([_HBM] * (2 * nw)),
        input_output_aliases={i: i for i in range(2 * nw)},
        compiler_params=pltpu.CompilerParams(has_side_effects=_DATAFLOW),
    )(*srcs, *lands, send_sems, recv_sems, after)
    return list(outs[:nw]), list(outs[nw:])


def _pass_to_sibling(lands, *, name):
    nw = len(lands)

    def body(*refs):
        x_refs, out_refs, send_sems, recv_sems = refs[:nw], refs[nw:2 * nw], refs[2 * nw], refs[2 * nw + 1]
        x, y, c = _me()
        chips = [(1 - x, y), (x, 1 - y), (1 - x, 1 - y)]
        cps = []
        for w in range(nw):
            for k, (px, py) in enumerate(chips):
                cp = pltpu.make_async_remote_copy(
                    src_ref=x_refs[w].at[4 * px + 2 * py + c], dst_ref=out_refs[w].at[4 * px + 2 * py + c],
                    send_sem=send_sems.at[3 * w + k], recv_sem=recv_sems.at[3 * w + k], device_id=(x, y, 1 - c),
                    device_id_type=MESH)
                cp.start()
                cps.append(cp)
        for w in range(nw):
            for k, (px, py) in enumerate(chips):
                pltpu.make_async_remote_copy(
                    src_ref=x_refs[w].at[4 * px + 2 * py + c], dst_ref=out_refs[w].at[4 * px + 2 * py + 1 - c],
                    send_sem=send_sems.at[3 * w + k], recv_sem=recv_sems.at[3 * w + k], device_id=(x, y, 1 - c),
                    device_id_type=MESH).wait_recv()
        for cp in cps:
            cp.wait_send()

    return pl.pallas_call(
        body, name=name, out_shape=[jax.ShapeDtypeStruct(a.shape, a.dtype) for a in lands],
        in_specs=[_ANY] * nw, out_specs=[_ANY] * nw, input_output_aliases={i: i for i in range(nw)},
        scratch_shapes=[pltpu.SemaphoreType.DMA((3 * nw,)), pltpu.SemaphoreType.DMA((3 * nw,))],
        compiler_params=_params(),
    )(*lands)


def _chip_exchange_wait(send_sems, recv_sems, srcs, lands, after, *, name):
    nw = len(srcs)

    def body(*refs):
        x_refs, land_refs, send_sems, recv_sems = refs[:nw], refs[nw:2 * nw], refs[2 * nw], refs[2 * nw + 1]
        x, y, c = _me()
        for w in range(nw):
            for k, (px, py) in enumerate([(1 - x, y), (x, 1 - y), (1 - x, 1 - y)]):
                cp = pltpu.make_async_remote_copy(
                    src_ref=x_refs[w].at[2 * px + py], dst_ref=land_refs[w].at[2 * px + py], send_sem=send_sems.at[3 * w + k],
                    recv_sem=recv_sems.at[3 * w + k], device_id=(px, py, c), device_id_type=MESH)
                cp.wait_send()
                cp.wait_recv()

    hbm = [pltpu.HBM(a.shape, a.dtype) for a in srcs]
    outs = pl.pallas_call(
        body, name=name, out_shape=(*hbm, *hbm),
        in_specs=[_HBM] * (2 * nw) + [_SEM, _SEM, _ANY], out_specs=tuple([_HBM] * (2 * nw)),
        input_output_aliases={i: i for i in range(2 * nw)},
        compiler_params=pltpu.CompilerParams(has_side_effects=_DATAFLOW),
    )(*srcs, *lands, send_sems, recv_sems, after)
    return list(outs[:nw]), list(outs[nw:])


def _sum_slabs(landed, own_src, s_arr, *, name, tm=512):
    S, h, w = landed.shape
    tm = _tile(h, tm, 16)

    def body(s_ref, *refs):
        own = refs[S][0].astype(F32)
        acc = None
        for s in range(S):
            term = jnp.where(s_ref[0] == s, own, refs[s][0].astype(F32))
            acc = term if acc is None else acc + term
        refs[S + 1][...] = acc

    def slab(s):
        return pl.BlockSpec((1, tm, w), lambda i, s_ref: (jnp.where(s_ref[0] == s, (s + 1) % S, s), i, 0))

    return pl.pallas_call(
        body, name=name,
        grid_spec=pltpu.PrefetchScalarGridSpec(
            num_scalar_prefetch=1, grid=(h // tm,),
            in_specs=[slab(s) for s in range(S)] + [pl.BlockSpec((1, tm, w), lambda i, s_ref: (s_ref[0], i, 0))],
            out_specs=pl.BlockSpec((tm, w), lambda i, s_ref: (i, 0))),
        out_shape=jax.ShapeDtypeStruct((h, w), F32),
        compiler_params=_params(("parallel",)),
    )(s_arr, *([landed] * S), own_src)


def _half_add(g, recv, c_arr, *, name):
    S, r, w = g.shape
    h = r // 2
    tm = _tile(h, 512, 16)
    nb = h // tm

    def body(c_ref, g_ref, r_ref, o_ref):
        o_ref[...] = (g_ref[...] + r_ref[...]).astype(BF16)

    return pl.pallas_call(
        body, name=name,
        grid_spec=pltpu.PrefetchScalarGridSpec(
            num_scalar_prefetch=1, grid=(S, nb),
            in_specs=[pl.BlockSpec((1, tm, w), lambda s, i, c_ref: (s, c_ref[0] * nb + i, 0)),
                      pl.BlockSpec((1, tm, w), lambda s, i, c_ref: (s, i, 0))],
            out_specs=pl.BlockSpec((1, tm, w), lambda s, i, c_ref: (s, i, 0))),
        out_shape=jax.ShapeDtypeStruct((S, h, w), BF16),
        compiler_params=_params(("parallel", "parallel")),
    )(c_arr, g, recv)


def _layout(sizes, width, part_mult, total_mult):
    offs, rows, r = [], [], 0
    for n in sizes:
        k = -(-n // width)
        offs.append(r)
        rows.append(k)
        r += -(-k // part_mult) * part_mult
    return offs, rows, -(-r // total_mult) * total_mult


def _pack(arrs, width, part_mult, total_mult, dtype, lead=()):
    nl = len(lead)
    sizes = [math.prod(a.shape[nl:]) for a in arrs]
    offs, rows, total = _layout(sizes, width, part_mult, total_mult)
    parts, r = [], 0
    for a, n, o, k in zip(arrs, sizes, offs, rows):
        kp = -(-k // part_mult) * part_mult
        flat = a.reshape(*lead, n).astype(dtype)
        if kp * width > n:
            flat = jnp.pad(flat, [(0, 0)] * nl + [(0, kp * width - n)])
        parts.append(flat.reshape(*lead, kp, width))
        r = o + kp
    if total > r:
        parts.append(jnp.zeros((*lead, total - r, width), dtype))
    return jnp.concatenate(parts, axis=nl)


def _unpack(pool, shapes, width, part_mult, total_mult):
    lead = pool.shape[:-2]
    sizes = [math.prod(s) for s in shapes]
    offs, rows, _ = _layout(sizes, width, part_mult, total_mult)
    out = []
    for s, n, o, k in zip(shapes, sizes, offs, rows):
        flat = lax.slice_in_dim(pool, o, o + k, axis=len(lead)).reshape(*lead, k * width)
        out.append(lax.slice_in_dim(flat, 0, n, axis=len(lead)).reshape(*lead, *s))
    return out


_WEIGHTS = ("c_ctx", "w_ada", "b_ada", "g_pre_mix", "g_post_mix", "g_pre_ffn", "g_post_ffn", "w_in", "b_merge",
            "dn_conv", "dn_a_log", "dn_dt_bias", "dn_onorm", "lru_conv", "lru_conv_b", "lru_w_rg", "lru_b_rg",
            "lru_w_ig", "lru_b_ig", "lru_lambda", "w_branch_dn", "w_branch_lru", "w_out", "w_up", "ffn_dw",
            "ffn_dw_b", "w_down")
_BIG = {"w_ada": True, "w_in": True, "w_branch_dn": False, "w_branch_lru": False, "w_out": False, "w_up": True,
        "w_down": False}
_SMALL_SHARDED = ("dn_conv", "lru_conv", "lru_b_rg", "lru_b_ig", "lru_lambda", "ffn_dw")
_NCHIP = 4
_FLAT_PART = 8
_FLAT_TOTAL = 256


def _to_chip_shards(g, by_cols):
    if by_cols:
        return g.reshape(g.shape[0], _NCHIP, g.shape[1] // _NCHIP).transpose(1, 0, 2)
    return g.reshape(_NCHIP, g.shape[0] // _NCHIP, g.shape[1])


def _from_chip_shards(s, by_cols):
    if by_cols:
        return s.transpose(1, 0, 2).reshape(s.shape[1], _NCHIP * s.shape[2])
    return s.reshape(_NCHIP * s.shape[1], s.shape[2])


def _dsilu(x):
    s = _sigmoid(x)
    return s * (1.0 + x * (1.0 - s))


def kernel(x, c, ctx, c_ctx, w_ada, b_ada, g_pre_mix, g_post_mix, g_pre_ffn, g_post_ffn, w_in, b_merge, dn_conv, dn_a_log, dn_dt_bias, dn_onorm, lru_conv, lru_conv_b, lru_w_rg, lru_b_rg, lru_w_ig, lru_b_ig, lru_lambda, w_branch_dn, w_branch_lru, w_out, w_up, ffn_dw, ffn_dw_b, w_down, loss_target, m_c_ctx, m_w_ada, m_b_ada, m_g_pre_mix, m_g_post_mix, m_g_pre_ffn, m_g_post_ffn, m_w_in, m_b_merge, m_dn_conv, m_dn_a_log, m_dn_dt_bias, m_dn_onorm, m_lru_conv, m_lru_conv_b, m_lru_w_rg, m_lru_b_rg, m_lru_w_ig, m_lru_b_ig, m_lru_lambda, m_w_branch_dn, m_w_branch_lru, m_w_out, m_w_up, m_ffn_dw, m_ffn_dw_b, m_w_down, v_c_ctx, v_w_ada, v_b_ada, v_g_pre_mix, v_g_post_mix, v_g_pre_ffn, v_g_post_ffn, v_w_in, v_b_merge, v_dn_conv, v_dn_a_log, v_dn_dt_bias, v_dn_onorm, v_lru_conv, v_lru_conv_b, v_lru_w_rg, v_lru_b_rg, v_lru_w_ig, v_lru_b_ig, v_lru_lambda, v_w_branch_dn, v_w_branch_lru, v_w_out, v_w_up, v_ffn_dw, v_ffn_dw_b, v_w_down):
    W = dict(zip(_WEIGHTS, (c_ctx, w_ada, b_ada, g_pre_mix, g_post_mix, g_pre_ffn, g_post_ffn, w_in, b_merge, dn_conv,
                            dn_a_log, dn_dt_bias, dn_onorm, lru_conv, lru_conv_b, lru_w_rg, lru_b_rg, lru_w_ig, lru_b_ig,
                            lru_lambda, w_branch_dn, w_branch_lru, w_out, w_up, ffn_dw, ffn_dw_b, w_down)))
    Mo = dict(zip(_WEIGHTS, (m_c_ctx, m_w_ada, m_b_ada, m_g_pre_mix, m_g_post_mix, m_g_pre_ffn, m_g_post_ffn, m_w_in,
                             m_b_merge, m_dn_conv, m_dn_a_log, m_dn_dt_bias, m_dn_onorm, m_lru_conv, m_lru_conv_b,
                             m_lru_w_rg, m_lru_b_rg, m_lru_w_ig, m_lru_b_ig, m_lru_lambda, m_w_branch_dn,
                             m_w_branch_lru, m_w_out, m_w_up, m_ffn_dw, m_ffn_dw_b, m_w_down)))
    Vo = dict(zip(_WEIGHTS, (v_c_ctx, v_w_ada, v_b_ada, v_g_pre_mix, v_g_post_mix, v_g_pre_ffn, v_g_post_ffn, v_w_in,
                             v_b_merge, v_dn_conv, v_dn_a_log, v_dn_dt_bias, v_dn_onorm, v_lru_conv, v_lru_conv_b,
                             v_lru_w_rg, v_lru_b_rg, v_lru_w_ig, v_lru_b_ig, v_lru_lambda, v_w_branch_dn,
                             v_w_branch_lru, v_w_out, v_w_up, v_ffn_dw, v_ffn_dw_b, v_w_down)))
    B, N, D = x.shape
    NC = ctx.shape[1]
    T = NC + N
    H, HD = dn_a_log.shape[-1], dn_onorm.shape[-1]
    DNW = H * HD
    LW, LBD = lru_conv_b.shape[-1], lru_w_rg.shape[-1]
    DFF = ffn_dw_b.shape[-1]
    LC = LANES
    x_i, y_i, c_i = _me()
    s_me = 2 * x_i + y_i
    tm = _tile(math.gcd(NC, N), 256, 16)

    def whole(n, g):
        r, w_ = W[n].shape[1:]
        return g.reshape(_NCHIP, r, w_) if _BIG[n] else g.reshape(_NCHIP * r, w_)

    first = ("w_ada", "w_in")
    later = tuple(n for n in _BIG if n not in first)
    shard16 = {n: W[n][0].astype(BF16) for n in _BIG}
    full = {n: whole(n, g) for n, g in zip(first, _allgather_halves([shard16[n] for n in first], name="allgather_first"))}
    ag_send, ag_recv, ag_src, ag_land, ag_token = _allgather_start([shard16[n] for n in later], name="ag_start")

    small_local = [W[n][0].reshape(-1, W[n].shape[-1]) for n in _SMALL_SHARDED]
    small_shapes = [a.shape for a in small_local]
    spack = _pack(small_local, LANES, _FLAT_PART, _FLAT_PART, F32)
    sgath = _allgather_small(spack)[0::2]
    sfull = {n: _from_chip_shards(s, True)
             for n, s in zip(_SMALL_SHARDED, _unpack(sgath, small_shapes, LANES, _FLAT_PART, _FLAT_PART))}

    o_a = 4 * DNW
    o_xl = o_a + 4 * H
    o_mg = o_xl + 2 * LW
    wi_ = _from_chip_shards(full["w_in"], True)
    nj = LW // LC
    lru_cols = jnp.stack([wi_[:, o_xl:o_xl + LW].reshape(D, nj, LC), wi_[:, o_xl + LW:o_mg].reshape(D, nj, LC)],
                         axis=2).reshape(D, 2 * LW)
    wp = jnp.concatenate([wi_[:, :o_a], lru_cols, wi_[:, o_mg:], wi_[:, o_a:o_xl],
                          jnp.zeros((D, LANES - 4 * H), BF16)], axis=1)
    p_lru, p_mg, p_ab = 4 * DNW, 4 * DNW + 2 * LW, 4 * DNW + 2 * LW + 2 * D
    PW = p_ab + LANES

    MR = LANES
    cond = jnp.concatenate([c, c_ctx[None], jnp.zeros((MR - B - 1, D), F32)], axis=0)
    silu_rows = _rowwise(lambda a: (_silu(a),), [cond], [F32], name="cond_silu")[0]
    mod = _matmul(silu_rows, full["w_ada"], b_shards=(0, _NCHIP), name="ada_fwd") + b_ada + ag_token[0, 0]
    mx = mod[:B].reshape(B, 6, D)
    mc = mod[B].reshape(6, D)
    zero = jnp.zeros((B, D), F32)
    tab = jnp.stack([jnp.stack([jnp.broadcast_to(mc[0], (B, D)), jnp.broadcast_to(mc[1], (B, D))] + [zero] * 6, axis=1),
                     jnp.stack([mx[:, 0], mx[:, 1]] + [zero] * 6, axis=1)], axis=1)
    vecs = jnp.stack([mx[:, 2], mx[:, 3], mx[:, 4], mx[:, 5]] + [zero] * 4, axis=1)
    gains = jnp.concatenate([g_post_mix, g_pre_ffn, g_post_ffn, jnp.zeros((5, D), F32)], axis=0)

    h = jnp.concatenate([ctx, x], axis=1)
    u = _premix_fwd(h, g_pre_mix, tab, nc=NC, tm=tm)
    p = _matmul(u, wp, name="in_fwd")
    dkw = dict(B=B, T=T, nc=NC, H=H, HD=HD)
    qkv = _dnprep_fwd(p, sfull["dn_conv"], **dkw)
    prm = jnp.concatenate([
        jnp.concatenate([dn_a_log.reshape(1, 2 * H), jnp.zeros((1, LANES - 2 * H), F32)], axis=1),
        jnp.concatenate([dn_dt_bias.reshape(1, 2 * H), jnp.zeros((1, LANES - 2 * H), F32)], axis=1),
        jnp.zeros((6, LANES), F32)], axis=0)
    gtm = _tile(B * T, 512, 16)
    gb = _gb_fwd(p, prm, rows=B * T, col0=p_ab, H=H, tm=gtm)
    y_dn, o_dn, *dn_res = _delta_fwd(qkv, gb, p, dn_onorm, **dkw)
    lv = jnp.concatenate([lru_conv_b, sfull["lru_b_rg"], sfull["lru_b_ig"], sfull["lru_lambda"], jnp.zeros((1, LW), F32)], axis=0)
    wr = _blockdiag(lru_w_rg[0], LC).astype(BF16)
    wi = _blockdiag(lru_w_ig[0], LC).astype(BF16)
    lkw = dict(B=B, T=T, nc=NC, LW=LW, col0=p_lru, C=LC)
    y_lru = _lru_fwd(p, sfull["lru_conv"], lv, wr, wi, **lkw)
    ag_src, ag_land = _allgather_wait(ag_send, ag_recv, ag_src, ag_land, y_lru, name="ag_wait")
    me_piece = 4 * x_i + 2 * y_i + c_i
    for n, src, land in zip(later, ag_src, _pass_to_sibling(ag_land, name="ag_pass")):
        own = lax.dynamic_slice_in_dim(src, c_i * (src.shape[0] // 2), src.shape[0] // 2, axis=0)
        full[n] = whole(n, lax.dynamic_update_index_in_dim(land, own, me_piece, axis=0))
    Ydn = _matmul(y_dn, full["w_branch_dn"], name="bdn_fwd")
    Ylru = _matmul(y_lru, full["w_branch_lru"], name="blru_fwd")
    mkw = dict(B=B, T=T, nc=NC, D=D, col0=p_mg, tm=tm)
    mixin = _merge_fwd(p, Ydn, Ylru, b_merge, **mkw)
    mix = _matmul(mixin, full["w_out"], name="out_fwd")
    h1, u2 = _post_fwd(x, mix, gains, vecs, tm=tm)
    F = _matmul(u2, full["w_up"], b_shards=(0, _NCHIP), name="up_fwd")
    w9 = sfull["ffn_dw"]
    ftc = _tile(DFF, 256)
    f = _ffn_act_fwd(F, w9, ffn_dw_b, B=B, N=N, DFF=DFF, tc=ftc)
    dn = _matmul(f, full["w_down"], name="down_fwd")
    ddn, dout, sums_f = _final(h1, dn, loss_target, gains, vecs, tm=tm)

    G = {}
    df = _matmul(ddn, full["w_down"], tb=True, name="down_bwd_x")
    G["w_down"] = _matmul(f, ddn, ta=True, name="down_bwd_w")
    dFg, dFv, dwb = _ffn_act_bwd(F, w9, ffn_dw_b, df, B=B, N=N, DFF=DFF, tc=ftc)
    hs = _NCHIP // 2
    du2 = _matmul(dFg, full["w_up"], tb=True, b_shards=(0, hs), name="up_bwd_xg")
    du2 = _matmul(dFv, full["w_up"], tb=True, b_shards=(hs, hs), add=du2, name="up_bwd_xv")
    G["w_up"] = jnp.concatenate([_matmul(u2, dFg, ta=True, out_shards=hs, name="up_bwd_wg"),
                                 _matmul(u2, dFv, ta=True, out_shards=hs, name="up_bwd_wv")], axis=0)
    dx1, dmix, sums_p = _post_bwd(x, mix, gains, vecs, dout, du2, tm=tm)
    dmixin = _matmul(dmix, full["w_out"], tb=True, name="out_bwd_x")
    G["w_out"] = _matmul(mixin, dmix, ta=True, name="out_bwd_w")
    dp = jnp.zeros((B * T, PW), BF16)
    dYdn, dYlru, dp, sums_m = _merge_bwd(p, Ydn, Ylru, b_merge, dmixin, dp, **mkw)
    dy_dn = _matmul(dYdn, full["w_branch_dn"], tb=True, name="bdn_bwd_x")
    G["w_branch_dn"] = _matmul(y_dn, dYdn, ta=True, name="bdn_bwd_w")
    dy_lru = _matmul(dYlru, full["w_branch_lru"], tb=True, name="blru_bwd_x")
    G["w_branch_lru"] = _matmul(y_lru, dYlru, ta=True, name="blru_bwd_w")

    c_arr = c_i.astype(jnp.int32).reshape(1)
    s_arr = s_me.astype(jnp.int32).reshape(1)

    def chip_sums(names, tag):
        slabs = [G[n] if _BIG[n] else G[n].reshape(_NCHIP, G[n].shape[0] // _NCHIP, G[n].shape[1]) for n in names]
        from_sibling = _sibling_send_halves(slabs, name="rs_sibling_" + tag)
        return [_half_add(g, r, c_arr, name="rs_add_" + n) for n, g, r in zip(names, slabs, from_sibling)]

    early = tuple(n for n in _BIG if n in G)
    late = tuple(n for n in _BIG if n not in G)
    cx_send, cx_recv, cx_src, cx_land, cx_token = _chip_exchange_start(chip_sums(early, "early"), name="cx_start")
    dp, dcw_l, dlv, dwr, dwi = _lru_bwd(p, sfull["lru_conv"], lv + cx_token[0, 0], wr, wi, dy_lru, dp, **lkw)
    dqkv, dgb, dp, don = _delta_bwd(qkv, gb, p, dn_onorm, o_dn, dn_res, dy_dn, dp, **dkw)
    dp, dprm = _gb_bwd(p, prm, dgb, dp, rows=B * T, col0=p_ab, H=H, tm=gtm)
    dp, dcw_d = _dnprep_bwd(p, sfull["dn_conv"], dqkv, dp, **dkw)
    dU = _matmul(dp, wp, tb=True, name="in_bwd_x")
    dwp = _matmul(u, dp, ta=True, name="in_bwd_w")
    grad_x, sums_pm = _premix_bwd(h, g_pre_mix, tab, dU, dx1, nc=NC, tm=tm)
    dlru = dwp[:, p_lru:p_mg].reshape(D, nj, 2, LC)
    G["w_in"] = _to_chip_shards(jnp.concatenate([dwp[:, :o_a], dwp[:, p_ab:p_ab + 4 * H], dlru[:, :, 0].reshape(D, LW),
                                                 dlru[:, :, 1].reshape(D, LW), dwp[:, p_mg:p_ab]], axis=1), True)

    dmod_x = jnp.stack([sums_pm[:, 1, 0], sums_pm[:, 1, 1], sums_p[:, 0], sums_p[:, 1], sums_p[:, 2], sums_f[:, 0]],
                       axis=1).reshape(B, 6 * D)
    dmod_c = jnp.concatenate([sums_pm[:, 0, 0].sum(0), sums_pm[:, 0, 1].sum(0), jnp.zeros((4 * D,), F32)])[None]
    dmod = jnp.concatenate([dmod_x, dmod_c, jnp.zeros((MR - B - 1, 6 * D), F32)], axis=0)
    G["w_ada"] = _matmul(silu_rows, dmod, ta=True, out_shards=_NCHIP, name="ada_bwd_w")
    dsilu = _matmul(dmod, full["w_ada"], tb=True, b_shards=(0, _NCHIP), name="ada_bwd_x")

    g_small = {
        "c_ctx": dsilu[B] * _dsilu(c_ctx),
        "b_ada": dmod[:B + 1].sum(0)[None],
        "g_pre_mix": sums_pm[:, :, 2].sum((0, 1))[None],
        "g_post_mix": sums_p[:, 3].sum(0)[None],
        "g_pre_ffn": sums_p[:, 4].sum(0)[None],
        "g_post_ffn": sums_f[:, 1].sum(0)[None],
        "b_merge": sums_m[0:1],
        "dn_conv": dcw_d[0:4][None],
        "dn_a_log": dprm[0, :2 * H].reshape(1, 2, H),
        "dn_dt_bias": dprm[1, :2 * H].reshape(1, 2, H),
        "dn_onorm": don[:, 0].sum(0)[None],
        "lru_conv": dcw_l[0:4][None],
        "lru_conv_b": dlv[0:1],
        "lru_w_rg": _blockdiag_extract(dwr, LBD)[None],
        "lru_b_rg": dlv[1:3][None],
        "lru_w_ig": _blockdiag_extract(dwi, LBD)[None],
        "lru_b_ig": dlv[3:5][None],
        "lru_lambda": dlv[5:7][None],
        "ffn_dw": dwb[0:9].reshape(1, 3, 3, DFF),
        "ffn_dw_b": dwb[9:10],
    }
    small_names = tuple(n for n in _WEIGHTS if n not in _BIG)
    loss_part = sums_f[:, 2].sum().reshape(1)
    gs_list = [g_small[n] for n in small_names] + [loss_part]
    gs_shapes = [a.shape for a in gs_list]
    gpack = _pack(gs_list, LANES, _FLAT_PART, _FLAT_TOTAL, F32)
    gsum = _sum_lead(_allgather_small(gpack), name="small_sum", tm=512, mult=SUBLANES)
    gs_red = dict(zip(small_names + ("loss",), _unpack(gsum, gs_shapes, LANES, _FLAT_PART, _FLAT_TOTAL)))
    loss = gs_red["loss"][0]

    cx_src, cx_land = _chip_exchange_wait(cx_send, cx_recv, cx_src, cx_land, dsilu, name="cx_wait")
    late_sums = chip_sums(late, "late")
    late_land = _chip_exchange(late_sums, name="chip_exchange")
    half = {n: _sum_slabs(l, src, s_arr, name="rs_sum_" + n)
            for n, l, src in zip(early + late, list(cx_land) + list(late_land), list(cx_src) + list(late_sums))}
    halves = [half[n] for n in _BIG]
    sib_halves = _sibling_swap(halves, name="rs_gather")

    grads, deltas, new_m, new_v = {}, {}, {}, {}
    for n, own, sib in zip(_BIG, halves, sib_halves):
        shp = W[n].shape
        outs = _adamw_halves(W[n][0], own, sib, Mo[n][0], Vo[n][0], c_arr, name="adamw_" + n)
        grads[n], deltas[n], new_m[n], new_v[n] = (o.reshape(shp) for o in outs)
    for n in small_names:
        g = gs_red[n]
        if n in _SMALL_SHARDED:
            k = W[n].shape[-1]
            g = lax.dynamic_slice_in_dim(g, s_me * k, k, axis=g.ndim - 1)
        grads[n] = g.reshape(W[n].shape)
    sm_shapes = [W[n].shape for n in small_names]
    pk = lambda d: _pack([d[n] for n in small_names], LANES, _FLAT_PART, _FLAT_TOTAL, F32)
    d_, m_, v_ = _adamw(pk(W), pk(grads), pk(Mo), pk(Vo), name="adamw_small")
    for dst, pool_ in ((deltas, d_), (new_m, m_), (new_v, v_)):
        dst.update(zip(small_names, _unpack(pool_, sm_shapes, LANES, _FLAT_PART, _FLAT_TOTAL)))
    return (loss, grad_x, *[grads[n] for n in _WEIGHTS], *[deltas[n] for n in _WEIGHTS],
            *[new_m[n] for n in _WEIGHTS], *[new_v[n] for n in _WEIGHTS])
```

```python
import functools
import math

import jax
import jax.numpy as jnp
from jax import lax
from jax.experimental import pallas as pl
from jax.experimental.pallas import tpu as pltpu

F32 = jnp.float32
BF16 = jnp.bfloat16
EPS = 1e-6
GRID_W = 64
CHUNK = 128
LRU_C = 8.0
LANES = 128
SUBLANES = 8
VMEM_LIMIT = 56 * 1024 * 1024
ADAM_LR, ADAM_B1, ADAM_B2, ADAM_EPS, ADAM_WD, ADAM_STEP = 0.001, 0.9, 0.999, 1e-08, 0.01, 10
MESH = pl.DeviceIdType.MESH


def _tile(n, target, mult=LANES):
    best = None
    for t in range(mult, min(n, target) + 1, mult):
        if n % t == 0:
            best = t
    return best if best is not None else n


def _params(sem=None, **kw):
    return pltpu.CompilerParams(dimension_semantics=sem, vmem_limit_bytes=VMEM_LIMIT, **kw)


def _sigmoid(x):
    return 1.0 / (1.0 + jnp.exp(-x))


def _silu(x):
    return x * _sigmoid(x)


def _softplus(x):
    return jnp.maximum(x, 0.0) + jnp.log(1.0 + jnp.exp(-jnp.abs(x)))


def _gelu(x):
    return 0.5 * x * (1.0 + jnp.tanh(math.sqrt(2.0 / math.pi) * (x + 0.044715 * x * x * x)))


def _rmsn(u, gain):
    return u * lax.rsqrt(jnp.mean(u * u, axis=-1, keepdims=True) + EPS) * gain


_MM_VMEM = 40 * 1024 * 1024


def _matmul(a, b, *, ta=False, tb=False, add=None, b_shards=None, out_shards=None, out_dtype=F32, name,
            tm=1024, tn=2048, tk=1024):
    (K, M) = a.shape if ta else a.shape[::-1]
    if b_shards is not None:
        s0, ns = b_shards
        bsh = (b.shape[1], ns * b.shape[2])
        nsh = b.shape[2]
    else:
        bsh = b.shape
    N = bsh[0] if tb else bsh[1]
    assert (bsh[1] if tb else bsh[0]) == K, (a.shape, b.shape, ta, tb)
    tm = _tile(M, tm)
    tk = _tile(nsh if (b_shards is not None and tb) else K, tk)
    nlim = nsh if (b_shards is not None and not tb) else (N // out_shards if out_shards else N)
    osz = jnp.dtype(out_dtype).itemsize + (4 if add is not None else 0)
    while True:
        tn_ = _tile(nlim, tn)
        need = 2 * (tm * tk * a.dtype.itemsize + tk * tn_ * b.dtype.itemsize + tm * tn_ * osz) + 4 * tm * tn_
        if need <= _MM_VMEM or tn <= LANES:
            break
        tn //= 2
    tn = tn_
    nk = K // tk
    dims = (((0 if ta else 1,), (1 if tb else 0,)), ((), ()))

    def body(a_ref, b_ref, *rest):
        (c_ref, o_ref, acc_ref) = rest if add is not None else (None, *rest)
        k = pl.program_id(2)

        @pl.when(k == 0)
        def _():
            acc_ref[...] = jnp.zeros_like(acc_ref) if c_ref is None else c_ref[...]

        bv = b_ref[0] if b_shards is not None else b_ref[...]
        acc_ref[...] += lax.dot_general(a_ref[...].astype(BF16), bv.astype(BF16), dims, preferred_element_type=F32)

        @pl.when(k == nk - 1)
        def _():
            if out_shards:
                o_ref[0] = acc_ref[...].astype(out_dtype)
            else:
                o_ref[...] = acc_ref[...].astype(out_dtype)

    a_spec = pl.BlockSpec((tk, tm), lambda i, j, k: (k, i)) if ta else pl.BlockSpec((tm, tk), lambda i, j, k: (i, k))
    if b_shards is None:
        b_spec = pl.BlockSpec((tn, tk), lambda i, j, k: (j, k)) if tb else pl.BlockSpec((tk, tn), lambda i, j, k: (k, j))
    elif tb:
        per = nsh // tk
        b_spec = pl.BlockSpec((1, tn, tk), lambda i, j, k: (s0 + k // per, j, k % per))
    else:
        per = nsh // tn
        b_spec = pl.BlockSpec((1, tk, tn), lambda i, j, k: (s0 + j // per, k, j % per))
    o_spec = pl.BlockSpec((tm, tn), lambda i, j, k: (i, j))
    if out_shards:
        oper = N // out_shards // tn
        out_spec = pl.BlockSpec((1, tm, tn), lambda i, j, k: (j // oper, i, j % oper))
        out_shape = jax.ShapeDtypeStruct((out_shards, M, N // out_shards), out_dtype)
    else:
        out_spec, out_shape = o_spec, jax.ShapeDtypeStruct((M, N), out_dtype)
    return pl.pallas_call(
        body, name=name, grid=(M // tm, N // tn, nk),
        in_specs=[a_spec, b_spec] + ([o_spec] if add is not None else []),
        out_specs=out_spec, out_shape=out_shape,
        scratch_shapes=[pltpu.VMEM((tm, tn), F32)],
        compiler_params=_params(("parallel", "parallel", "arbitrary")),
    )(*((a, b) + ((add,) if add is not None else ())))


def _premix_math(h, gain, shift, scale):
    return _rmsn(h, gain) * (1.0 + scale) + shift


def _premix_fwd(h, gain, tab, *, nc, tm):
    B, T, D = h.shape
    nt, nct = T // tm, nc // tm

    def body(h_ref, g_ref, tab_ref, u_ref):
        tabv = tab_ref[0, 0]
        u_ref[...] = _premix_math(h_ref[0], g_ref[...], tabv[0:1], tabv[1:2]).astype(BF16)

    return pl.pallas_call(
        body, name="premix_fwd", grid=(B, nt),
        in_specs=[pl.BlockSpec((1, tm, D), lambda b, t: (b, t, 0)),
                  pl.BlockSpec((1, D), lambda b, t: (0, 0)),
                  pl.BlockSpec((1, 1, 8, D), lambda b, t: (b, jnp.where(t < nct, 0, 1), 0, 0))],
        out_specs=pl.BlockSpec((tm, D), lambda b, t: (b * nt + t, 0)),
        out_shape=jax.ShapeDtypeStruct((B * T, D), BF16),
        compiler_params=_params(("parallel", "parallel")),
    )(h, gain, tab)


def _premix_bwd(h, gain, tab, du, dres, *, nc, tm):
    B, T, D = h.shape
    nt, nct = T // tm, nc // tm
    N = T - nc

    def body(h_ref, g_ref, tab_ref, du_ref, dres_ref, dx_ref, sums_ref):
        t = pl.program_id(1)
        tabv = tab_ref[0, 0]
        _, vjp = jax.vjp(_premix_math, h_ref[0], g_ref[...], tabv[0:1], tabv[1:2])
        dh, dgain, dshift, dscale = vjp(du_ref[...].astype(F32))

        @pl.when((t == 0) | (t == nct))
        def _():
            sums_ref[...] = jnp.zeros_like(sums_ref)

        sums_ref[0, 0, 0:1, :] += dshift
        sums_ref[0, 0, 1:2, :] += dscale
        sums_ref[0, 0, 2:3, :] += dgain

        @pl.when(t >= nct)
        def _():
            dx_ref[0] = dres_ref[...] + dh

    lat = lambda b, t: jnp.maximum(t - nct, 0)
    return pl.pallas_call(
        body, name="premix_bwd", grid=(B, nt),
        in_specs=[pl.BlockSpec((1, tm, D), lambda b, t: (b, t, 0)),
                  pl.BlockSpec((1, D), lambda b, t: (0, 0)),
                  pl.BlockSpec((1, 1, 8, D), lambda b, t: (b, jnp.where(t < nct, 0, 1), 0, 0)),
                  pl.BlockSpec((tm, D), lambda b, t: (b * nt + t, 0)),
                  pl.BlockSpec((tm, D), lambda b, t: (b * (nt - nct) + lat(b, t), 0))],
        out_specs=[pl.BlockSpec((1, tm, D), lambda b, t: (b, lat(b, t), 0)),
                   pl.BlockSpec((1, 1, 8, D), lambda b, t: (b, jnp.where(t < nct, 0, 1), 0, 0))],
        out_shape=[jax.ShapeDtypeStruct((B, N, D), F32), jax.ShapeDtypeStruct((B, 2, 8, D), F32)],
        compiler_params=_params(("parallel", "arbitrary")),
    )(h, gain, tab, du, dres)


def _merge_math(mgd, mgl, yd, yl, bd, bl):
    return _sigmoid(mgd + bd) * yd + _sigmoid(mgl + bl) * yl


def _merge_fwd(p, ydn, ylru, b_merge, *, B, T, nc, D, col0, tm):
    N = T - nc
    ntl, nt, nct, cb = N // tm, T // tm, nc // tm, col0 // D

    def body(mgd_ref, mgl_ref, yd_ref, yl_ref, bm_ref, o_ref):
        o_ref[...] = _merge_math(mgd_ref[...], mgl_ref[...], yd_ref[...], yl_ref[...],
                                 bm_ref[:, 0:D], bm_ref[:, D:2 * D]).astype(BF16)

    prow = lambda b, t: b * nt + nct + t
    return pl.pallas_call(
        body, name="merge_fwd", grid=(B, ntl),
        in_specs=[pl.BlockSpec((tm, D), lambda b, t: (prow(b, t), cb)),
                  pl.BlockSpec((tm, D), lambda b, t: (prow(b, t), cb + 1)),
                  pl.BlockSpec((tm, D), lambda b, t: (b * ntl + t, 0)),
                  pl.BlockSpec((tm, D), lambda b, t: (b * ntl + t, 0)),
                  pl.BlockSpec((1, 2 * D), lambda b, t: (0, 0))],
        out_specs=pl.BlockSpec((tm, D), lambda b, t: (b * ntl + t, 0)),
        out_shape=jax.ShapeDtypeStruct((B * N, D), BF16),
        compiler_params=_params(("parallel", "parallel")),
    )(p, p, ydn, ylru, b_merge)


def _merge_bwd(p, ydn, ylru, b_merge, dmix, dp, *, B, T, nc, D, col0, tm):
    N = T - nc
    ntl, nt, nct, cb = N // tm, T // tm, nc // tm, col0 // D
    assert col0 % (2 * D) == 0

    def body(mgd_ref, mgl_ref, yd_ref, yl_ref, bm_ref, dm_ref, dp_any, dyd_ref, dyl_ref, dp_ref, sums_ref):
        _, vjp = jax.vjp(_merge_math, mgd_ref[...], mgl_ref[...], yd_ref[...], yl_ref[...],
                         bm_ref[:, 0:D], bm_ref[:, D:2 * D])
        dmgd, dmgl, dyd, dyl, dbd, dbl = vjp(dm_ref[...])
        dyd_ref[...] = dyd.astype(BF16)
        dyl_ref[...] = dyl.astype(BF16)
        dp_ref[:, 0:D] = dmgd.astype(BF16)
        dp_ref[:, D:2 * D] = dmgl.astype(BF16)

        @pl.when((pl.program_id(0) == 0) & (pl.program_id(1) == 0))
        def _():
            sums_ref[...] = jnp.zeros_like(sums_ref)

        sums_ref[0:1, 0:D] += dbd
        sums_ref[0:1, D:2 * D] += dbl

    prow = lambda b, t: b * nt + nct + t
    row = pl.BlockSpec((tm, D), lambda b, t: (b * ntl + t, 0))
    return pl.pallas_call(
        body, name="merge_bwd", grid=(B, ntl),
        in_specs=[pl.BlockSpec((tm, D), lambda b, t: (prow(b, t), cb)),
                  pl.BlockSpec((tm, D), lambda b, t: (prow(b, t), cb + 1)),
                  row, row, pl.BlockSpec((1, 2 * D), lambda b, t: (0, 0)), row,
                  pl.BlockSpec(memory_space=pl.ANY)],
        out_specs=[row, row,
                   pl.BlockSpec((tm, 2 * D), lambda b, t: (prow(b, t), cb // 2)),
                   pl.BlockSpec((8, 2 * D), lambda b, t: (0, 0))],
        out_shape=[jax.ShapeDtypeStruct((B * N, D), BF16), jax.ShapeDtypeStruct((B * N, D), BF16),
                   jax.ShapeDtypeStruct(dp.shape, dp.dtype), jax.ShapeDtypeStruct((8, 2 * D), F32)],
        input_output_aliases={6: 2},
        compiler_params=_params(("arbitrary", "arbitrary")),
    )(p, p, ydn, ylru, b_merge, dmix, dp)


def _post_math(x, mix, g1, gate, g2, sh, sc):
    h1 = x + _rmsn(mix, g1) * gate
    return h1, _rmsn(h1, g2) * (1.0 + sc) + sh


def _post_fwd(x, mix, gains, vecs, *, tm):
    B, N, D = x.shape
    ntl = N // tm

    def body(x_ref, mix_ref, g_ref, v_ref, h1_ref, u2_ref):
        v = v_ref[0]
        h1, u2 = _post_math(x_ref[0], mix_ref[...], g_ref[0:1], v[0:1], g_ref[1:2], v[1:2], v[2:3])
        h1_ref[...] = h1
        u2_ref[...] = u2.astype(BF16)

    row = pl.BlockSpec((tm, D), lambda b, t: (b * ntl + t, 0))
    return pl.pallas_call(
        body, name="post_fwd", grid=(B, ntl),
        in_specs=[pl.BlockSpec((1, tm, D), lambda b, t: (b, t, 0)), row,
                  pl.BlockSpec((8, D), lambda b, t: (0, 0)), pl.BlockSpec((1, 8, D), lambda b, t: (b, 0, 0))],
        out_specs=[row, row],
        out_shape=[jax.ShapeDtypeStruct((B * N, D), F32), jax.ShapeDtypeStruct((B * N, D), BF16)],
        compiler_params=_params(("parallel", "parallel")),
    )(x, mix, gains, vecs)


def _post_bwd(x, mix, gains, vecs, dh1, du2, *, tm):
    B, N, D = x.shape
    ntl = N // tm

    def body(x_ref, mix_ref, g_ref, v_ref, dh1_ref, du2_ref, dx_ref, dmix_ref, sums_ref):
        v = v_ref[0]
        _, vjp = jax.vjp(_post_math, x_ref[0], mix_ref[...], g_ref[0:1], v[0:1], g_ref[1:2], v[1:2], v[2:3])
        dx, dmix, dg1, dgate, dg2, dsh, dsc = vjp((dh1_ref[...], du2_ref[...]))
        dx_ref[...] = dx
        dmix_ref[...] = dmix.astype(BF16)

        @pl.when(pl.program_id(1) == 0)
        def _():
            sums_ref[...] = jnp.zeros_like(sums_ref)

        sums_ref[0, 0:1, :] += dgate
        sums_ref[0, 1:2, :] += dsh
        sums_ref[0, 2:3, :] += dsc
        sums_ref[0, 3:4, :] += dg1
        sums_ref[0, 4:5, :] += dg2

    row = pl.BlockSpec((tm, D), lambda b, t: (b * ntl + t, 0))
    return pl.pallas_call(
        body, name="post_bwd", grid=(B, ntl),
        in_specs=[pl.BlockSpec((1, tm, D), lambda b, t: (b, t, 0)), row,
                  pl.BlockSpec((8, D), lambda b, t: (0, 0)), pl.BlockSpec((1, 8, D), lambda b, t: (b, 0, 0)), row, row],
        out_specs=[row, row, pl.BlockSpec((1, 8, D), lambda b, t: (b, 0, 0))],
        out_shape=[jax.ShapeDtypeStruct((B * N, D), F32), jax.ShapeDtypeStruct((B * N, D), BF16),
                   jax.ShapeDtypeStruct((B, 8, D), F32)],
        compiler_params=_params(("parallel", "arbitrary")),
    )(x, mix, gains, vecs, dh1, du2)


def _final_math(dn, g4, gate5):
    return _rmsn(dn, g4) * gate5


def _final(h1, dn, target, gains, vecs, *, tm):
    B, N, D = target.shape
    ntl = N // tm

    def body(h1_ref, dn_ref, t_ref, g_ref, v_ref, ddn_ref, dout_ref, sums_ref):
        v = v_ref[0]
        y, vjp = jax.vjp(_final_math, dn_ref[...], g_ref[2:3], v[3:4])
        err = h1_ref[...] + y - t_ref[0]
        dout = err * (1.0 / D)
        ddn, dg4, dgate5 = vjp(dout)
        ddn_ref[...] = ddn.astype(BF16)
        dout_ref[...] = dout

        @pl.when(pl.program_id(1) == 0)
        def _():
            sums_ref[...] = jnp.zeros_like(sums_ref)

        sums_ref[0, 0:1, :] += dgate5
        sums_ref[0, 1:2, :] += dg4
        sums_ref[0, 2:3, :] += jnp.sum(err * err, axis=0, keepdims=True) * (0.5 / D)

    row = pl.BlockSpec((tm, D), lambda b, t: (b * ntl + t, 0))
    return pl.pallas_call(
        body, name="final", grid=(B, ntl),
        in_specs=[row, row, pl.BlockSpec((1, tm, D), lambda b, t: (b, t, 0)),
                  pl.BlockSpec((8, D), lambda b, t: (0, 0)), pl.BlockSpec((1, 8, D), lambda b, t: (b, 0, 0))],
        out_specs=[row, row, pl.BlockSpec((1, 8, D), lambda b, t: (b, 0, 0))],
        out_shape=[jax.ShapeDtypeStruct((B * N, D), BF16), jax.ShapeDtypeStruct((B * N, D), F32),
                   jax.ShapeDtypeStruct((B, 8, D), F32)],
        compiler_params=_params(("parallel", "arbitrary")),
    )(h1, dn, target, gains, vecs)


def _shift(x, s):
    s = s % x.shape[0]
    return x if s == 0 else pltpu.roll(x, s, 0)


def _seg_taps(T, nc, width, pad_left):
    t = lax.broadcasted_iota(jnp.int32, (T, 1), 0)
    pos = jnp.where(t < nc, t, t - nc)
    seg = jnp.where(t < nc, nc, T - nc)
    taps = []
    for k in range(width):
        src = pos + (k - pad_left)
        taps.append((pad_left - k, (src >= 0) & (src < seg)))
    return taps


def _grid_taps(N):
    t = lax.broadcasted_iota(jnp.int32, (N, 1), 0)
    wcol = t % GRID_W
    taps = []
    for dr in (-1, 0, 1):
        for dw in (-1, 0, 1):
            off = dr * GRID_W + dw
            ok = (wcol + dw >= 0) & (wcol + dw < GRID_W) & (t + dr * GRID_W >= 0) & (t + dr * GRID_W < N)
            taps.append((-off, ok))
    return taps


def _conv_fwd(x, w, taps):
    y = jnp.zeros_like(x)
    for k, (s, m) in enumerate(taps):
        y = y + w[k:k + 1] * jnp.where(m, _shift(x, s), 0.0)
    return y


def _conv_bwd(x, w, taps, dy):
    dx = jnp.zeros_like(x)
    dws = []
    for k, (s, m) in enumerate(taps):
        dym = jnp.where(m, dy, 0.0)
        dx = dx + w[k:k + 1] * _shift(dym, -s)
        dws.append(jnp.sum(dym * _shift(x, s), axis=0, keepdims=True))
    return dx, jnp.concatenate(dws, axis=0)


def _ffn_act_fwd(F, w9, bias, *, B, N, DFF, tc):
    nj = DFF // tc

    def body(fg_ref, fv_ref, w_ref, b_ref, o_ref):
        fg = _conv_fwd(fg_ref[...], w_ref[...], _grid_taps(N)) + b_ref[...]
        o_ref[...] = (_gelu(fg) * fv_ref[...]).astype(BF16)

    return pl.pallas_call(
        body, name="ffn_act_fwd", grid=(B, nj),
        in_specs=[pl.BlockSpec((N, tc), lambda b, j: (b, j)), pl.BlockSpec((N, tc), lambda b, j: (b, nj + j)),
                  pl.BlockSpec((9, tc), lambda b, j: (0, j)), pl.BlockSpec((1, tc), lambda b, j: (0, j))],
        out_specs=pl.BlockSpec((N, tc), lambda b, j: (b, j)),
        out_shape=jax.ShapeDtypeStruct((B * N, DFF), BF16),
        compiler_params=_params(("parallel", "parallel")),
    )(F, F, w9, bias)


def _ffn_act_bwd(F, w9, bias, df, *, B, N, DFF, tc):
    nj = DFF // tc

    def body(fg_ref, fv_ref, w_ref, b_ref, df_ref, dfg_ref, dfv_ref, dwb_ref):
        taps = _grid_taps(N)
        x = fg_ref[...]
        fg, vjp = jax.vjp(lambda a: _gelu(a), _conv_fwd(x, w_ref[...], taps) + b_ref[...])
        dfl = df_ref[...]
        dfv_ref[...] = (dfl * fg).astype(BF16)
        (dpre,) = vjp(dfl * fv_ref[...])
        dx, dw = _conv_bwd(x, w_ref[...], taps, dpre)
        dfg_ref[...] = dx.astype(BF16)

        @pl.when(pl.program_id(1) == 0)
        def _():
            dwb_ref[...] = jnp.zeros_like(dwb_ref)

        dwb_ref[0:9, :] += dw
        dwb_ref[9:10, :] += jnp.sum(dpre, axis=0, keepdims=True)

    col = pl.BlockSpec((N, tc), lambda j, b: (b, j))
    return pl.pallas_call(
        body, name="ffn_act_bwd", grid=(nj, B),
        in_specs=[col, pl.BlockSpec((N, tc), lambda j, b: (b, nj + j)),
                  pl.BlockSpec((9, tc), lambda j, b: (0, j)), pl.BlockSpec((1, tc), lambda j, b: (0, j)), col],
        out_specs=[col, col, pl.BlockSpec((16, tc), lambda j, b: (0, j))],
        out_shape=[jax.ShapeDtypeStruct((B * N, DFF), BF16), jax.ShapeDtypeStruct((B * N, DFF), BF16),
                   jax.ShapeDtypeStruct((16, DFF), F32)],
        compiler_params=_params(("parallel", "arbitrary")),
    )(F, F, w9, bias, df)


def _dnprep_math(y, is_qk, scale):
    s = _silu(y)
    n = s * lax.rsqrt(jnp.sum(s * s, axis=-1, keepdims=True) + EPS) * scale
    return jnp.where(is_qk, n, s)


def _dnprep_fwd(p, cw, *, B, T, nc, H, HD):
    def body(x_ref, w_ref, o_ref):
        j = pl.program_id(1)
        y = _conv_fwd(x_ref[...], w_ref[...], _seg_taps(T, nc, 4, 2))
        o_ref[...] = _dnprep_math(y, j < 2 * H, jnp.where(j < H, HD ** -0.5, 1.0))

    return pl.pallas_call(
        body, name="dnprep_fwd", grid=(B, 3 * H),
        in_specs=[pl.BlockSpec((T, HD), lambda b, j: (b, j)), pl.BlockSpec((4, HD), lambda b, j: (0, j))],
        out_specs=pl.BlockSpec((T, HD), lambda b, j: (b, j)),
        out_shape=jax.ShapeDtypeStruct((B * T, 3 * H * HD), F32),
        compiler_params=_params(("parallel", "parallel")),
    )(p, cw)


def _dnprep_bwd(p, cw, dqkv, dp, *, B, T, nc, H, HD):
    def body(x_ref, w_ref, d_ref, dp_any, dp_ref, dcw_ref):
        j = pl.program_id(0)
        taps = _seg_taps(T, nc, 4, 2)
        x = x_ref[...]
        y = _conv_fwd(x, w_ref[...], taps)
        is_qk, scale = j < 2 * H, jnp.where(j < H, HD ** -0.5, 1.0)
        _, vjp = jax.vjp(lambda a: _dnprep_math(a, is_qk, scale), y)
        (dy,) = vjp(d_ref[0])
        dx, dw = _conv_bwd(x, w_ref[...], taps, dy)
        dp_ref[...] = dx.astype(BF16)

        @pl.when(pl.program_id(1) == 0)
        def _():
            dcw_ref[...] = jnp.zeros_like(dcw_ref)

        dcw_ref[0:4, :] += dw

    col = pl.BlockSpec((T, HD), lambda j, b: (b, j))
    return pl.pallas_call(
        body, name="dnprep_bwd", grid=(3 * H, B),
        in_specs=[col, pl.BlockSpec((4, HD), lambda j, b: (0, j)),
                  pl.BlockSpec((1, T, HD), lambda j, b: (j // H, b, j % H)), pl.BlockSpec(memory_space=pl.ANY)],
        out_specs=[col, pl.BlockSpec((8, HD), lambda j, b: (0, j))],
        out_shape=[jax.ShapeDtypeStruct(dp.shape, dp.dtype), jax.ShapeDtypeStruct((8, 3 * H * HD), F32)],
        input_output_aliases={3: 0},
        compiler_params=_params(("parallel", "arbitrary")),
    )(p, cw, dqkv, dp)


def _gb_math(ab, alog, dtb, H):
    lane = lax.broadcasted_iota(jnp.int32, ab.shape, 1)
    g = -jnp.exp(alog) * _softplus(ab + dtb)
    return jnp.where(lane < 2 * H, g, jnp.where(lane < 4 * H, _sigmoid(ab), 0.0))


def _gb_fwd(p, prm, *, rows, col0, H, tm):
    def body(x_ref, prm_ref, o_ref):
        o_ref[...] = _gb_math(x_ref[...], prm_ref[0:1], prm_ref[1:2], H)

    return pl.pallas_call(
        body, name="gb_fwd", grid=(rows // tm,),
        in_specs=[pl.BlockSpec((tm, LANES), lambda t: (t, col0 // LANES)), pl.BlockSpec((8, LANES), lambda t: (0, 0))],
        out_specs=pl.BlockSpec((tm, LANES), lambda t: (t, 0)),
        out_shape=jax.ShapeDtypeStruct((rows, LANES), F32),
        compiler_params=_params(("parallel",)),
    )(p, prm)


def _gb_bwd(p, prm, dgb, dp, *, rows, col0, H, tm):
    def body(x_ref, prm_ref, d_ref, dp_any, dp_ref, dprm_ref):
        _, vjp = jax.vjp(lambda a, b, c: _gb_math(a, b, c, H), x_ref[...], prm_ref[0:1], prm_ref[1:2])
        dab, dalog, ddtb = vjp(d_ref[...])
        dp_ref[...] = dab.astype(BF16)

        @pl.when(pl.program_id(0) == 0)
        def _():
            dprm_ref[...] = jnp.zeros_like(dprm_ref)

        dprm_ref[0:1, :] += dalog
        dprm_ref[1:2, :] += ddtb

    blk = pl.BlockSpec((tm, LANES), lambda t: (t, col0 // LANES))
    return pl.pallas_call(
        body, name="gb_bwd", grid=(rows // tm,),
        in_specs=[blk, pl.BlockSpec((8, LANES), lambda t: (0, 0)), pl.BlockSpec((tm, LANES), lambda t: (t, 0)),
                  pl.BlockSpec(memory_space=pl.ANY)],
        out_specs=[blk, pl.BlockSpec((8, LANES), lambda t: (0, 0))],
        out_shape=[jax.ShapeDtypeStruct(dp.shape, dp.dtype), jax.ShapeDtypeStruct((8, LANES), F32)],
        input_output_aliases={3: 0},
        compiler_params=_params(("arbitrary",)),
    )(p, prm, dgb, dp)


def _lru_scans(scans):
    C = scans[0][0].shape[1]
    row = lax.broadcasted_iota(jnp.int32, (SUBLANES, C), 0)
    carries = tuple(jnp.zeros((1, C), F32) for _ in scans)
    for si in range(len(scans[0][4])):
        nb = scans[0][4][si][1] // SUBLANES
        assert all(sc[4][si][1] // SUBLANES == nb for sc in scans)

        def blk(i, carries, si=si, nb=nb):
            out = []
            for (a_ref, b_ref, h_ref, hp_ref, segs), carry in zip(scans, carries):
                start, _, reverse = segs[si]
                r0 = pl.multiple_of(start + (nb - 1 - i if reverse else i) * SUBLANES, SUBLANES)
                A = a_ref[pl.ds(r0, SUBLANES), :]
                Bv = b_ref[pl.ds(r0, SUBLANES), :]
                for s in (1, 2, 4):
                    sh = SUBLANES - s if reverse else s
                    m = (row < SUBLANES - s) if reverse else (row >= s)
                    Bv = jnp.where(m, A * pltpu.roll(Bv, sh, 0) + Bv, Bv)
                    A = jnp.where(m, A * pltpu.roll(A, sh, 0), A)
                Hv = Bv + A * carry
                h_ref[pl.ds(r0, SUBLANES), :] = Hv
                if hp_ref is not None:
                    if reverse:
                        hp = jnp.where(row < SUBLANES - 1, pltpu.roll(Hv, SUBLANES - 1, 0), carry)
                    else:
                        hp = jnp.where(row >= 1, pltpu.roll(Hv, 1, 0), carry)
                    hp_ref[pl.ds(r0, SUBLANES), :] = hp
                out.append(Hv[0:1] if reverse else Hv[SUBLANES - 1:SUBLANES])
            return tuple(out)

        carries = lax.fori_loop(0, nb, blk, carries)


def _lru_orders(T, nc, d):
    N = T - nc
    if d == 0:
        return [(0, nc, False), (nc, N, False)], [(nc, N, True), (0, nc, True)]
    return [(0, nc, True), (nc, N, True)], [(nc, N, False), (0, nc, False)]


def _bdot(a, b, dims=(((1,), (0,)), ((), ()))):
    return lax.dot_general(a.astype(BF16), b.astype(BF16), dims, preferred_element_type=F32)


_NT = (((1,), (1,)), ((), ()))
_TN = (((0,), (0,)), ((), ()))


def _blockdiag(w, C):
    nd, nb, bd, _ = w.shape
    per = C // bd
    out = jnp.einsum('dnpij,pq->dnpiqj', w.reshape(nd, nb // per, per, bd, bd), jnp.eye(per, dtype=w.dtype))
    return out.reshape(nd, nb // per, C, C)


def _blockdiag_extract(dw, bd):
    nd, nj, C, _ = dw.shape
    per = C // bd
    out = jnp.einsum('dnpiqj,pq->dnpij', dw.reshape(nd, nj, per, bd, per, bd), jnp.eye(per, dtype=dw.dtype))
    return out.reshape(nd, nj * per, bd, bd)


def _lru_fwd(p, cw, lv, wr, wi, *, B, T, nc, LW, col0, C):
    N = T - nc
    nj = LW // C

    def body(x_ref, cw_ref, lv_ref, wr_ref, wi_ref, o_ref, a_s, b_s, h_s):
        lv_ = lv_ref[...]
        xc = _conv_fwd(x_ref[:, 0:C], cw_ref[...], _seg_taps(T, nc, 4, 2)) + lv_[0:1]
        for d in (0, 1):
            r = _sigmoid(_bdot(xc, wr_ref[d, 0]) + lv_[1 + d:2 + d])
            i = _sigmoid(_bdot(xc, wi_ref[d, 0]) + lv_[3 + d:4 + d])
            la = -LRU_C * r * _softplus(-lv_[5 + d:6 + d])
            a_s[d] = jnp.exp(la)
            b_s[d] = jnp.sqrt(1.0 - jnp.exp(2.0 * la)) * i * xc
        _lru_scans([(a_s.at[d], b_s.at[d], h_s.at[d], None, _lru_orders(T, nc, d)[0]) for d in (0, 1)])
        o_ref[...] = ((h_s[0, nc:, :] + h_s[1, nc:, :]) * _gelu(x_ref[nc:, C:2 * C])).astype(BF16)

    return pl.pallas_call(
        body, name="lru_fwd", grid=(B, nj),
        in_specs=[pl.BlockSpec((T, 2 * C), lambda b, j: (b, col0 // (2 * C) + j)),
                  pl.BlockSpec((4, C), lambda b, j: (0, j)), pl.BlockSpec((8, C), lambda b, j: (0, j)),
                  pl.BlockSpec((2, 1, C, C), lambda b, j: (0, j, 0, 0)), pl.BlockSpec((2, 1, C, C), lambda b, j: (0, j, 0, 0))],
        out_specs=pl.BlockSpec((N, C), lambda b, j: (b, j)),
        out_shape=jax.ShapeDtypeStruct((B * N, LW), BF16),
        scratch_shapes=[pltpu.VMEM((2, T, C), F32)] * 3,
        compiler_params=_params(("parallel", "parallel")),
    )(p, cw, lv, wr, wi)


def _lru_bwd(p, cw, lv, wr, wi, dy, dp, *, B, T, nc, LW, col0, C):
    N = T - nc
    nj = LW // C

    def body(x_ref, cw_ref, lv_ref, wr_ref, wi_ref, dy_ref, dp_any, dp_ref, dcw_ref, dlv_ref, dwr_ref, dwi_ref,
             a_s, b_s, h_s, hp_s, mu_s, mup_s, dh_s, dxc_s):
        taps = _seg_taps(T, nc, 4, 2)
        lv_ = lv_ref[...]
        xl = x_ref[:, 0:C]
        xc = _conv_fwd(xl, cw_ref[...], taps) + lv_[0:1]
        gel, gelu_vjp = jax.vjp(_gelu, x_ref[nc:, C:2 * C])
        dh_s[0:nc, :] = jnp.zeros((nc, C), F32)
        dh_s[nc:, :] = dy_ref[...] * gel
        dxc_s[...] = jnp.zeros_like(dxc_s)

        @pl.when(pl.program_id(1) == 0)
        def _():
            dcw_ref[...] = jnp.zeros_like(dcw_ref)
            dlv_ref[...] = jnp.zeros_like(dlv_ref)
            dwr_ref[...] = jnp.zeros_like(dwr_ref)
            dwi_ref[...] = jnp.zeros_like(dwi_ref)

        def gates(d):
            lam = lv_[5 + d:6 + d]
            r = _sigmoid(_bdot(xc, wr_ref[d, 0]) + lv_[1 + d:2 + d])
            i = _sigmoid(_bdot(xc, wi_ref[d, 0]) + lv_[3 + d:4 + d])
            sp = _softplus(-lam)
            la = -LRU_C * r * sp
            e2 = jnp.exp(2.0 * la)
            return lam, r, i, sp, la, e2, jnp.sqrt(1.0 - e2)

        for d in (0, 1):
            _, _, i, _, la, _, mult = gates(d)
            a_s[d] = jnp.exp(la)
            b_s[d] = mult * i * xc
        _lru_scans([(a_s.at[d], b_s.at[d], h_s.at[d], hp_s.at[d], _lru_orders(T, nc, d)[0]) for d in (0, 1)])
        for d in (0, 1):
            b_s[d] = a_s[d] * dh_s[...]
        _lru_scans([(a_s.at[d], b_s.at[d], mu_s.at[d], mup_s.at[d], _lru_orders(T, nc, d)[1]) for d in (0, 1)])

        for d in (0, 1):
            lam, r, i, sp, la, e2, mult = gates(d)
            a = a_s[d]
            dinp = dh_s[...] + mup_s[d]
            da = dinp * hp_s[d]
            dmult = dinp * i * xc
            di = dinp * mult * xc
            dla = da * a - dmult * e2 / mult
            dpre_r = (dla * (-LRU_C * sp)) * r * (1.0 - r)
            dpre_i = di * i * (1.0 - i)
            dsp = jnp.sum(dla * (-LRU_C * r), axis=0, keepdims=True)
            dxc_s[...] += dinp * mult * i + _bdot(dpre_r, wr_ref[d, 0], _NT) + _bdot(dpre_i, wi_ref[d, 0], _NT)
            dwr_ref[d, 0] += _bdot(xc, dpre_r, _TN)
            dwi_ref[d, 0] += _bdot(xc, dpre_i, _TN)
            dlv_ref[1 + d:2 + d, :] += jnp.sum(dpre_r, axis=0, keepdims=True)
            dlv_ref[3 + d:4 + d, :] += jnp.sum(dpre_i, axis=0, keepdims=True)
            dlv_ref[5 + d:6 + d, :] += -dsp * _sigmoid(-lam)

        dxc = dxc_s[...]
        dxl, dw = _conv_bwd(xl, cw_ref[...], taps, dxc)
        dcw_ref[0:4, :] += dw
        dlv_ref[0:1, :] += jnp.sum(dxc, axis=0, keepdims=True)
        dp_ref[:, 0:C] = dxl.astype(BF16)
        (dyl,) = gelu_vjp(dy_ref[...] * (h_s[0, nc:, :] + h_s[1, nc:, :]))
        dp_ref[0:nc, C:2 * C] = jnp.zeros((nc, C), BF16)
        dp_ref[nc:, C:2 * C] = dyl.astype(BF16)

    xblk = pl.BlockSpec((T, 2 * C), lambda j, b: (b, col0 // (2 * C) + j))
    wblk = pl.BlockSpec((2, 1, C, C), lambda j, b: (0, j, 0, 0))
    vblk = pl.BlockSpec((8, C), lambda j, b: (0, j))
    return pl.pallas_call(
        body, name="lru_bwd", grid=(nj, B),
        in_specs=[xblk, pl.BlockSpec((4, C), lambda j, b: (0, j)), vblk, wblk, wblk,
                  pl.BlockSpec((N, C), lambda j, b: (b, j)), pl.BlockSpec(memory_space=pl.ANY)],
        out_specs=[xblk, vblk, vblk, wblk, wblk],
        out_shape=[jax.ShapeDtypeStruct(dp.shape, dp.dtype), jax.ShapeDtypeStruct((8, LW), F32),
                   jax.ShapeDtypeStruct((8, LW), F32), jax.ShapeDtypeStruct((2, nj, C, C), F32),
                   jax.ShapeDtypeStruct((2, nj, C, C), F32)],
        scratch_shapes=[pltpu.VMEM((2, T, C), F32)] * 6 + [pltpu.VMEM((T, C), F32)] * 2,
        input_output_aliases={6: 0},
        compiler_params=_params(("parallel", "arbitrary")),
    )(p, cw, lv, wr, wi, dy, dp)


def _chunk_masks(upper):
    i = lax.broadcasted_iota(jnp.int32, (CHUNK, CHUNK), 0)
    j = lax.broadcasted_iota(jnp.int32, (CHUNK, CHUNK), 1)
    ahead = jnp.where(upper, j - i, i - j)
    return i == j, ahead >= 0, ahead > 0


def _col2row(c, eye):
    return jnp.sum(jnp.where(eye, c, 0.0), axis=0, keepdims=True)


def _row2col(r, eye):
    return jnp.sum(jnp.where(eye, r, 0.0), axis=1, keepdims=True)


def _rowsum(x):
    return jnp.sum(x, axis=1, keepdims=True)


_INV_BASE = 8


def _unit_tri_inverses(Ls):
    G = len(Ls)
    W = G * CHUNK
    blk = (lax.broadcasted_iota(jnp.int32, (W, W), 0) // CHUNK) == (lax.broadcasted_iota(jnp.int32, (W, W), 1) // CHUNK)
    ri = lax.broadcasted_iota(jnp.int32, (CHUNK, W), 0)
    ci = lax.broadcasted_iota(jnp.int32, (CHUNK, W), 1) % CHUNK

    def bd(b):
        return jnp.where(blk, jnp.tile(b, (G, 1)), jnp.zeros((), BF16))

    def pdot(a, b):
        return jnp.dot(a.astype(BF16), bd(b.astype(BF16)), preferred_element_type=F32)

    Lc = Ls[0] if G == 1 else jnp.concatenate(Ls, axis=1)
    s = _INV_BASE
    Xp = -jnp.where(ri // s == ci // s, Lc, 0.0)
    Rm = Xp
    for _ in range(int(math.log2(s)) - 1):
        Xp = pdot(Xp, Xp)
        Rm = Rm + Xp + pdot(Rm, Xp)
    while s < CHUNK:
        E = jnp.where((ri // (2 * s) == ci // (2 * s)) & (ri // s != ci // s), Lc, 0.0)
        DE = E + pdot(Rm, E)
        Rm = Rm - (DE + pdot(DE, Rm))
        s *= 2
    eye = _chunk_masks(False)[0]
    return [jnp.where(eye, 1.0, 0.0) + Rm[:, g * CHUNK:(g + 1) * CHUNK] for g in range(G)]


def _delta_chunk_common(q, k, v, gcol, bcol, upper):
    eye, incl, strict = _chunk_masks(upper)
    gc = _rowsum(jnp.where(incl, _col2row(gcol, eye), 0.0))
    D = jnp.where(incl, jnp.exp(jnp.minimum(gc - _col2row(gc, eye), 0.0)), 0.0)
    kb = k * bcol
    AP = _bdot(jnp.concatenate([kb, q], axis=0), k, _NT)
    A = AP[:CHUNK]
    L = jnp.where(strict, A * D, 0.0)
    eg = jnp.exp(gc)
    gl = jnp.sum(gcol, axis=0, keepdims=True)
    attn = jnp.where(incl, AP[CHUNK:] * D, 0.0)
    return dict(eye=eye, incl=incl, strict=strict, gc=gc, D=D, kb=kb, A=A, L=L, eg=eg, gl=gl, egl=jnp.exp(gl),
                attn=attn, kbe=kb * eg, vb=v * bcol, qe=q * eg, kd=k * jnp.exp(gl - gc))


def _delta_group_pre(chunks, upper):
    cs = [_delta_chunk_common(*ch, upper) for ch in chunks]
    out = []
    for c, Tm in zip(cs, _unit_tri_inverses([c["L"] for c in cs])):
        dk = c["kbe"].shape[1]
        wu = _bdot(Tm, jnp.concatenate([c["kbe"], c["vb"]], axis=1))
        KN = _bdot(c["kd"], wu, _TN)
        QO = _bdot(c["attn"], wu)
        out.append((Tm, KN[:, :dk], KN[:, dk:], c["qe"] - QO[:, :dk], QO[:, dk:], c["egl"]))
    return out


def _delta_chunk_bwd(q, k, v, gcol, bcol, S, Tm, do, dS2, upper):
    c = _delta_chunk_common(q, k, v, gcol, bcol, upper)
    eye, incl, strict, D, eg, egl = c["eye"], c["incl"], c["strict"], c["D"], c["eg"], c["egl"]
    kb, kbe, vb, qe, kd, attn = c["kb"], c["kbe"], c["vb"], c["qe"], c["kd"], c["attn"]
    dkk = kbe.shape[1]
    wu = _bdot(Tm, jnp.concatenate([kbe, vb], axis=1))
    w = wu[:, :dkk]
    vn = wu[:, dkk:] - _bdot(w, S)
    dvn = _bdot(kd, dS2) + _bdot(attn, do, _TN)
    dkd = _bdot(vn, dS2, _NT)
    dgl = jnp.sum(_rowsum(dS2 * S), axis=0, keepdims=True) * egl
    dqa = _bdot(do, jnp.concatenate([S, vn], axis=0), _NT)
    dqe = dqa[:, :dkk]
    dattn = jnp.where(incl, dqa[:, dkk:], 0.0)
    dw = -_bdot(dvn, S, _NT)
    r = _rowsum(dkd * kd)
    dk = dkd * jnp.exp(c["gl"] - c["gc"])
    dgl = dgl + jnp.sum(r, axis=0, keepdims=True)
    dgc = _rowsum(dqe * qe) - r
    E = dattn * attn
    dvw = jnp.concatenate([dvn, dw], axis=1)
    dTm = _bdot(dvw, jnp.concatenate([vb, kbe], axis=1), _NT)
    dvk = _bdot(Tm, dvw, _TN)
    dvb = dvk[:, :dvn.shape[1]]
    dv = dvb * bcol
    dbeta = _rowsum(dvb * v)
    dkbe = dvk[:, dvn.shape[1]:]
    dkb = dkbe * eg
    dgc = dgc + _rowsum(dkbe * kbe)
    dL = jnp.where(strict, -_bdot(Tm, _bdot(dTm, Tm, _NT), _TN), 0.0)
    dA = dL * D
    E = E + dL * c["L"]
    PA = jnp.concatenate([dattn * D, dA], axis=0)
    PAk = _bdot(PA, k)
    dq = dqe * eg + PAk[:CHUNK]
    dkb = dkb + PAk[CHUNK:]
    dk = dk + _bdot(PA, jnp.concatenate([q, kb], axis=0), _TN) + dkb * bcol
    dbeta = dbeta + _rowsum(dkb * k)
    dgc = dgc + _rowsum(E) - _row2col(jnp.sum(E, axis=0, keepdims=True), eye)
    dg = _row2col(jnp.sum(jnp.where(incl, dgc, 0.0), axis=0, keepdims=True), eye) + dgl
    return dq, dk, dv, dg, dbeta


def _delta_group(n):
    return max(g for g in range(1, 2 * LANES // CHUNK + 1) if n % g == 0)


def _delta_chunk_at(T, nc, d, i):
    n, ncc = T // CHUNK, nc // CHUNK
    desc = jnp.where(i < ncc, ncc - 1 - i, n - 1 - (i - ncc))
    if isinstance(d, int):
        return i if d == 0 else desc
    return jnp.where(d == 0, i, desc)


def _dn_out_math(o, onorm, z):
    return _rmsn(o, onorm) * _silu(z)


def _delta_fwd(qkv, gb, p, onorm, *, B, T, nc, H, HD):
    N = T - nc
    n = T // CHUNK
    G = _delta_group(n)

    def body(q_ref, k_ref, v_ref, gb_ref, z_ref, on_ref, y_ref, o_ref, Tm_ref, K_ref, S_ref, Qp_ref, eg_ref,
             N_s, O0_s, o_s):
        h = pl.program_id(1)
        lane = lax.broadcasted_iota(jnp.int32, (CHUNK, LANES), 1)

        def pre(g, carry):
            cs = [g * G + i for i in range(G)]
            rows = [pl.ds(pl.multiple_of(c * CHUNK, CHUNK), CHUNK) for c in cs]
            for d in (0, 1):
                chunks = []
                for r in rows:
                    gbb = gb_ref[r, :]
                    chunks.append((q_ref[r, :], k_ref[r, :], v_ref[r, :],
                                   _rowsum(jnp.where(lane == d * H + h, gbb, 0.0)),
                                   _rowsum(jnp.where(lane == 2 * H + d * H + h, gbb, 0.0))))
                for c, r, (Tm, K, Nn, Qp, O0, egl) in zip(cs, rows, _delta_group_pre(chunks, d == 1)):
                    Tm_ref[0, d * n + c] = Tm
                    K_ref[0, d * n + c] = K.astype(BF16)
                    N_s[d * n + c] = Nn
                    Qp_ref[0, d, r, :] = Qp.astype(BF16)
                    O0_s[d, r, :] = O0
                    eg_ref[0, d * n + c] = jnp.broadcast_to(egl, (SUBLANES, HD))
            return carry

        lax.fori_loop(0, n // G, pre, 0)

        def step(i, Ss):
            out = []
            for d in (0, 1):
                c = _delta_chunk_at(T, nc, d, i)
                rows = pl.ds(pl.multiple_of(c * CHUNK, CHUNK), CHUNK)
                S_ref[0, d * n + c] = Ss[d]
                Sb = Ss[d].astype(BF16)
                o_s[d, rows, :] = jnp.dot(Qp_ref[0, d, rows, :], Sb, preferred_element_type=F32) + O0_s[d, rows, :]
                out.append(eg_ref[0, d * n + c][0:1] * Ss[d] + N_s[d * n + c]
                           - jnp.dot(K_ref[0, d * n + c], Sb, preferred_element_type=F32))
            return tuple(out)

        lax.fori_loop(0, n, step, (jnp.zeros((HD, HD), F32), jnp.zeros((HD, HD), F32)))
        o = o_s[0, nc:, :] + o_s[1, nc:, :]
        o_ref[...] = o
        y_ref[...] = _dn_out_math(o, on_ref[...], z_ref[nc:, :]).astype(BF16)

    col = lambda off: pl.BlockSpec((T, HD), lambda b, h: (b, off + h))
    lat = pl.BlockSpec((N, HD), lambda b, h: (b, h))
    per = lambda *blk: pl.BlockSpec((1, *blk), lambda b, h: (b * H + h, 0, 0, 0))
    return pl.pallas_call(
        body, name="delta_fwd", grid=(B, H),
        in_specs=[col(0), col(H), col(2 * H), pl.BlockSpec((T, LANES), lambda b, h: (b, 0)), col(3 * H),
                  pl.BlockSpec((1, HD), lambda b, h: (0, 0))],
        out_specs=[lat, lat, per(2 * n, CHUNK, CHUNK), per(2 * n, HD, HD), per(2 * n, HD, HD), per(2, T, HD),
                   per(2 * n, SUBLANES, HD)],
        out_shape=[jax.ShapeDtypeStruct((B * N, H * HD), BF16), jax.ShapeDtypeStruct((B * N, H * HD), F32),
                   jax.ShapeDtypeStruct((B * H, 2 * n, CHUNK, CHUNK), F32),
                   jax.ShapeDtypeStruct((B * H, 2 * n, HD, HD), BF16), jax.ShapeDtypeStruct((B * H, 2 * n, HD, HD), F32),
                   jax.ShapeDtypeStruct((B * H, 2, T, HD), BF16), jax.ShapeDtypeStruct((B * H, 2 * n, SUBLANES, HD), F32)],
        scratch_shapes=[pltpu.VMEM((2 * n, HD, HD), F32), pltpu.VMEM((2, T, HD), F32), pltpu.VMEM((2, T, HD), F32)],
        compiler_params=_params(("parallel", "parallel")),
    )(qkv, qkv, qkv, gb, p, onorm)


def _delta_bwd(qkv, gb, p, onorm, o, res, dy, dp, *, B, T, nc, H, HD):
    N = T - nc
    n = T // CHUNK

    def body(q_ref, k_ref, v_ref, gb_ref, z_ref, on_ref, o_ref, dy_ref, Tm_ref, K_ref, S_ref, Qp_ref, eg_ref, dp_any,
             dqkv_ref, dgb_ref, dp_ref, don_ref, do_s, R_s, dS_s):
        h, d = pl.program_id(1), pl.program_id(2)
        lane = lax.broadcasted_iota(jnp.int32, (CHUNK, LANES), 1)

        @pl.when(d == 0)
        def _():
            _, vjp = jax.vjp(_dn_out_math, o_ref[...], on_ref[...], z_ref[nc:, :])
            do, don, dz = vjp(dy_ref[...])
            do_s[0:nc, :] = jnp.zeros((nc, HD), F32)
            do_s[nc:, :] = do
            dp_ref[0:nc, :] = jnp.zeros((nc, HD), BF16)
            dp_ref[nc:, :] = dz.astype(BF16)
            dqkv_ref[...] = jnp.zeros_like(dqkv_ref)

            @pl.when(h == 0)
            def _():
                don_ref[...] = jnp.zeros_like(don_ref)
                dgb_ref[...] = jnp.zeros_like(dgb_ref)

            don_ref[0, 0:1, :] += don

        def r_of(c, carry):
            rows = pl.ds(pl.multiple_of(c * CHUNK, CHUNK), CHUNK)
            R_s[c] = lax.dot_general(Qp_ref[0, 0, rows, :], do_s[rows, :].astype(BF16), _TN, preferred_element_type=F32)
            return carry

        lax.fori_loop(0, n, r_of, 0)

        def bwd_step(i, dS):
            c = _delta_chunk_at(T, nc, d, n - 1 - i)
            dS_s[c] = dS
            return (eg_ref[0, c][0:1] * dS + R_s[c]
                    - lax.dot_general(K_ref[0, c], dS.astype(BF16), _TN, preferred_element_type=F32))

        lax.fori_loop(0, n, bwd_step, jnp.zeros((HD, HD), F32))

        def grads(c, carry):
            rows = pl.ds(pl.multiple_of(c * CHUNK, CHUNK), CHUNK)
            gbb = gb_ref[rows, :]
            gcol = _rowsum(jnp.where(lane == d * H + h, gbb, 0.0))
            bcol = _rowsum(jnp.where(lane == 2 * H + d * H + h, gbb, 0.0))
            dq, dk, dv, dg, dbeta = _delta_chunk_bwd(q_ref[rows, :], k_ref[rows, :], v_ref[rows, :], gcol, bcol,
                                                     S_ref[0, c], Tm_ref[0, c], do_s[rows, :], dS_s[c], d == 1)
            dqkv_ref[0, rows, :] += dq
            dqkv_ref[1, rows, :] += dk
            dqkv_ref[2, rows, :] += dv
            dgb_ref[rows, :] += (jnp.where(lane == d * H + h, dg, 0.0)
                                 + jnp.where(lane == 2 * H + d * H + h, dbeta, 0.0))
            return carry

        lax.fori_loop(0, n, grads, 0)

    col = lambda off: pl.BlockSpec((T, HD), lambda b, h, d: (b, off + h))
    lat = pl.BlockSpec((N, HD), lambda b, h, d: (b, h))
    per = lambda *blk: pl.BlockSpec((1, *blk), lambda b, h, d: (b * H + h, d, 0, 0))
    return pl.pallas_call(
        body, name="delta_bwd", grid=(B, H, 2),
        in_specs=[col(0), col(H), col(2 * H), pl.BlockSpec((T, LANES), lambda b, h, d: (b, 0)), col(3 * H),
                  pl.BlockSpec((1, HD), lambda b, h, d: (0, 0)), lat, lat,
                  per(n, CHUNK, CHUNK), per(n, HD, HD), per(n, HD, HD), per(1, T, HD), per(n, SUBLANES, HD),
                  pl.BlockSpec(memory_space=pl.ANY)],
        out_specs=[pl.BlockSpec((3, T, HD), lambda b, h, d: (0, b, h)), pl.BlockSpec((T, LANES), lambda b, h, d: (b, 0)),
                   col(3 * H), pl.BlockSpec((1, 8, HD), lambda b, h, d: (b, 0, 0))],
        out_shape=[jax.ShapeDtypeStruct((3, B * T, H * HD), F32), jax.ShapeDtypeStruct((B * T, LANES), F32),
                   jax.ShapeDtypeStruct(dp.shape, dp.dtype), jax.ShapeDtypeStruct((B, 8, HD), F32)],
        scratch_shapes=[pltpu.VMEM((T, HD), F32), pltpu.VMEM((n, HD, HD), F32), pltpu.VMEM((n, HD, HD), F32)],
        input_output_aliases={13: 2},
        compiler_params=_params(("parallel", "arbitrary", "arbitrary")),
    )(qkv, qkv, qkv, gb, p, onorm, o, dy, *res, dp)


def _rowwise(fn, ins, out_dtypes, *, name, tm=256, mult=16):
    R, W = ins[0].shape
    tm = _tile(R, tm, mult)

    def body(*refs):
        outs = fn(*[r[...] for r in refs[:len(ins)]])
        for o_ref, o in zip(refs[len(ins):], outs):
            o_ref[...] = o.astype(o_ref.dtype)

    spec = pl.BlockSpec((tm, W), lambda i: (i, 0))
    return pl.pallas_call(
        body, name=name, grid=(R // tm,), in_specs=[spec] * len(ins), out_specs=[spec] * len(out_dtypes),
        out_shape=[jax.ShapeDtypeStruct((R, W), dt) for dt in out_dtypes],
        compiler_params=_params(("parallel",)),
    )(*ins)


def _sum_lead(x, *, name, tm=256, mult=16):
    S, R, W = x.shape
    tm = _tile(R, tm, mult)

    def body(*refs):
        acc = refs[0][0].astype(F32)
        for r in refs[1:S]:
            acc = acc + r[0].astype(F32)
        refs[S][...] = acc

    return pl.pallas_call(
        body, name=name, grid=(R // tm,),
        in_specs=[pl.BlockSpec((1, tm, W), functools.partial(lambda s, i: (s, i, 0), s)) for s in range(S)],
        out_specs=pl.BlockSpec((tm, W), lambda i: (i, 0)),
        out_shape=jax.ShapeDtypeStruct((R, W), F32),
        compiler_params=_params(("parallel",)),
    )(*([x] * S))


def _adamw_math(w, g, m, v):
    m = ADAM_B1 * m + (1.0 - ADAM_B1) * g
    v = ADAM_B2 * v + (1.0 - ADAM_B2) * (g * g)
    m_hat = m / (1.0 - ADAM_B1 ** ADAM_STEP)
    v_hat = v / (1.0 - ADAM_B2 ** ADAM_STEP)
    return -ADAM_LR * (m_hat / (jnp.sqrt(v_hat) + ADAM_EPS) + ADAM_WD * w), m, v


def _adamw(w, g, m, v, *, name):
    tm = max(SUBLANES, (256 * 1024) // w.shape[1] // SUBLANES * SUBLANES)
    return _rowwise(_adamw_math, [w, g, m, v], [F32, F32, F32], name=name, tm=tm, mult=SUBLANES)


def _me():
    return lax.axis_index("x"), lax.axis_index("y"), lax.axis_index("c")


def _allgather_small(v):
    R, W = v.shape

    def body(x_ref, out_ref, send_sems, recv_sems, local_sem):
        x, y, c = _me()
        me, sibling = (x, y, c), (x, y, 1 - c)
        chips = [(1 - x, y), (x, 1 - y), (1 - x, 1 - y)]

        def slot(px, py, pc):
            return out_ref.at[4 * px + 2 * py + pc]

        def copy(k, block, to, src=None):
            return pltpu.make_async_remote_copy(
                src_ref=slot(*block) if src is None else src, dst_ref=slot(*block),
                send_sem=send_sems.at[k], recv_sem=recv_sems.at[k], device_id=to, device_id_type=MESH)

        mine = pltpu.make_async_copy(x_ref, slot(*me), local_sem)
        mine.start()
        first = [copy(0, me, sibling, src=x_ref)]
        first += [copy(1 + j, me, (*chip, c), src=x_ref) for j, chip in enumerate(chips)]
        for cp in first:
            cp.start()
        passed = [copy(4 + j, (*chip, c), sibling) for j, chip in enumerate(chips)]
        for j, chip in enumerate(chips):
            copy(1 + j, (*chip, c), me).wait_recv()
            passed[j].start()
        copy(0, sibling, me).wait_recv()
        for j, chip in enumerate(chips):
            copy(4 + j, (*chip, 1 - c), me).wait_recv()
        for cp in first + passed:
            cp.wait_send()
        mine.wait()

    return pl.pallas_call(
        body, name="allgather_small", out_shape=jax.ShapeDtypeStruct((8, R, W), v.dtype),
        in_specs=[pl.BlockSpec(memory_space=pltpu.VMEM)], out_specs=pl.BlockSpec(memory_space=pltpu.VMEM),
        scratch_shapes=[pltpu.SemaphoreType.DMA((7,)), pltpu.SemaphoreType.DMA((7,)), pltpu.SemaphoreType.DMA],
        compiler_params=_params(),
    )(v)


_ANY = pl.BlockSpec(memory_space=pl.ANY)


def _allgather_halves(shards, *, name):
    nw = len(shards)

    def body(*refs):
        x_refs, out_refs = refs[:nw], refs[nw:2 * nw]
        send_sems, recv_sems, local_sems = refs[2 * nw:]
        x, y, c = _me()
        me, sibling = (x, y, c), (x, y, 1 - c)
        chips = [(1 - x, y), (x, 1 - y), (1 - x, 1 - y)]

        def slot(w, px, py, pc):
            return out_refs[w].at[4 * px + 2 * py + pc]

        def copy(w, k, block, to, src=None):
            return pltpu.make_async_remote_copy(
                src_ref=slot(w, *block) if src is None else src, dst_ref=slot(w, *block),
                send_sem=send_sems.at[w, k], recv_sem=recv_sems.at[w, k], device_id=to, device_id_type=MESH)

        started, local = [], []
        for w in range(nw):
            half = shards[w].shape[0] // 2
            own = x_refs[w].at[pl.ds(c * half, half), :]
            mine = pltpu.make_async_copy(own, slot(w, *me), local_sems.at[w])
            mine.start()
            first = [copy(w, 0, me, sibling, src=own)]
            first += [copy(w, 1 + j, me, (*chip, c), src=own) for j, chip in enumerate(chips)]
            for cp in first:
                cp.start()
            started += first
            local.append(mine)
        for w in range(nw):
            for j, chip in enumerate(chips):
                copy(w, 1 + j, (*chip, c), me).wait_recv()
                fwd = copy(w, 4 + j, (*chip, c), sibling)
                fwd.start()
                started.append(fwd)
        for w in range(nw):
            copy(w, 0, sibling, me).wait_recv()
            for j, chip in enumerate(chips):
                copy(w, 4 + j, (*chip, 1 - c), me).wait_recv()
        for cp in started:
            cp.wait_send()
        for cp in local:
            cp.wait()

    return pl.pallas_call(
        body, name=name,
        out_shape=[jax.ShapeDtypeStruct((8, s.shape[0] // 2, s.shape[1]), s.dtype) for s in shards],
        in_specs=[_ANY] * nw, out_specs=[_ANY] * nw,
        scratch_shapes=[pltpu.SemaphoreType.DMA((nw, 7)), pltpu.SemaphoreType.DMA((nw, 7)), pltpu.SemaphoreType.DMA((nw,))],
        compiler_params=_params(),
    )(*shards)


def _sibling_send_halves(arrs, *, name):
    nw = len(arrs)

    def body(*refs):
        x_refs, out_refs, send_sems, recv_sems = refs[:nw], refs[nw:2 * nw], refs[2 * nw], refs[2 * nw + 1]
        x, y, c = _me()
        cps = []
        for w in range(nw):
            half = arrs[w].shape[1] // 2
            cp = pltpu.make_async_remote_copy(
                src_ref=x_refs[w].at[:, pl.ds((1 - c) * half, half), :], dst_ref=out_refs[w],
                send_sem=send_sems.at[w], recv_sem=recv_sems.at[w], device_id=(x, y, 1 - c), device_id_type=MESH)
            cp.start()
            cps.append(cp)
        for cp in cps:
            cp.wait()

    return pl.pallas_call(
        body, name=name,
        out_shape=[jax.ShapeDtypeStruct((a.shape[0], a.shape[1] // 2, a.shape[2]), a.dtype) for a in arrs],
        in_specs=[_ANY] * nw, out_specs=[_ANY] * nw,
        scratch_shapes=[pltpu.SemaphoreType.DMA((nw,)), pltpu.SemaphoreType.DMA((nw,))],
        compiler_params=_params(),
    )(*arrs)


def _sibling_swap(arrs, *, name):
    nw = len(arrs)

    def body(*refs):
        x_refs, out_refs, send_sems, recv_sems = refs[:nw], refs[nw:2 * nw], refs[2 * nw], refs[2 * nw + 1]
        x, y, c = _me()
        cps = []
        for w in range(nw):
            cp = pltpu.make_async_remote_copy(
                src_ref=x_refs[w], dst_ref=out_refs[w], send_sem=send_sems.at[w], recv_sem=recv_sems.at[w],
                device_id=(x, y, 1 - c), device_id_type=MESH)
            cp.start()
            cps.append(cp)
        for cp in cps:
            cp.wait()

    return pl.pallas_call(
        body, name=name, out_shape=[jax.ShapeDtypeStruct(a.shape, a.dtype) for a in arrs],
        in_specs=[_ANY] * nw, out_specs=[_ANY] * nw,
        scratch_shapes=[pltpu.SemaphoreType.DMA((nw,)), pltpu.SemaphoreType.DMA((nw,))],
        compiler_params=_params(),
    )(*arrs)


def _adamw_halves(w, own, sib, m, v, c_arr, *, name):
    r, cols = w.shape
    h = r // 2
    tm = _tile(h, max(SUBLANES, (192 * 1024) // cols // SUBLANES * SUBLANES), SUBLANES)
    nb = h // tm

    def body(c_ref, w_ref, own_ref, sib_ref, m_ref, v_ref, g_out, d_out, m_out, v_out):
        g = jnp.where(pl.program_id(0) == c_ref[0], own_ref[...], sib_ref[...])
        g_out[...] = g
        d_out[...], m_out[...], v_out[...] = _adamw_math(w_ref[...], g, m_ref[...], v_ref[...])

    full = pl.BlockSpec((tm, cols), lambda hh, i, c_ref: (hh * nb + i, 0))
    half = pl.BlockSpec((tm, cols), lambda hh, i, c_ref: (i, 0))
    return pl.pallas_call(
        body, name=name,
        grid_spec=pltpu.PrefetchScalarGridSpec(num_scalar_prefetch=1, grid=(2, nb),
                                               in_specs=[full, half, half, full, full], out_specs=[full] * 4),
        out_shape=[jax.ShapeDtypeStruct((r, cols), F32)] * 4,
        compiler_params=_params(("parallel", "parallel")),
    )(c_arr, w, own, sib, m, v)


def _chip_exchange(arrs, *, name):
    nw = len(arrs)

    def body(*refs):
        x_refs, out_refs = refs[:nw], refs[nw:2 * nw]
        send_sems, recv_sems = refs[2 * nw:]
        x, y, c = _me()
        s_me = 2 * x + y
        chips = [(1 - x, y), (x, 1 - y), (1 - x, 1 - y)]
        started = []
        for w in range(nw):
            for k, (px, py) in enumerate(chips):
                cp = pltpu.make_async_remote_copy(
                    src_ref=x_refs[w].at[2 * px + py], dst_ref=out_refs[w].at[s_me], send_sem=send_sems.at[w, k],
                    recv_sem=recv_sems.at[w, k], device_id=(px, py, c), device_id_type=MESH)
                cp.start()
                started.append(cp)
        for w in range(nw):
            for k, (px, py) in enumerate(chips):
                pltpu.make_async_remote_copy(
                    src_ref=x_refs[w].at[s_me], dst_ref=out_refs[w].at[2 * px + py], send_sem=send_sems.at[w, k],
                    recv_sem=recv_sems.at[w, k], device_id=(px, py, c), device_id_type=MESH).wait_recv()
        for cp in started:
            cp.wait_send()

    return pl.pallas_call(
        body, name=name, out_shape=[jax.ShapeDtypeStruct(a.shape, a.dtype) for a in arrs],
        in_specs=[_ANY] * nw, out_specs=[_ANY] * nw,
        scratch_shapes=[pltpu.SemaphoreType.DMA((nw, 3)), pltpu.SemaphoreType.DMA((nw, 3))],
        compiler_params=_params(),
    )(*arrs)


_HBM = pl.BlockSpec(memory_space=pltpu.HBM)
_SEM = pl.BlockSpec(memory_space=pltpu.SEMAPHORE)
_DATAFLOW = pltpu.SideEffectType.DATAFLOW_SIDE_EFFECTING


def _chip_exchange_start(arrs, *, name):
    nw = len(arrs)

    def body(*refs):
        x_refs, land_refs, send_sems, recv_sems = refs[:nw], refs[nw:2 * nw], refs[2 * nw], refs[2 * nw + 1]
        token = refs[-1]
        x, y, c = _me()
        s_me = 2 * x + y
        for w in range(nw):
            for k, (px, py) in enumerate([(1 - x, y), (x, 1 - y), (1 - x, 1 - y)]):
                pltpu.make_async_remote_copy(
                    src_ref=x_refs[w].at[2 * px + py], dst_ref=land_refs[w].at[s_me], send_sem=send_sems.at[3 * w + k],
                    recv_sem=recv_sems.at[3 * w + k], device_id=(px, py, c), device_id_type=MESH).start()
        token[...] = jnp.zeros_like(token)

    hbm = [pltpu.HBM(a.shape, a.dtype) for a in arrs]
    outs = pl.pallas_call(
        body, name=name,
        out_shape=(pltpu.SemaphoreType.DMA((3 * nw,)), pltpu.SemaphoreType.DMA((3 * nw,)), *hbm, *hbm,
                   jax.ShapeDtypeStruct((SUBLANES, LANES), F32)),
        in_specs=[_HBM] * (2 * nw), out_specs=(_SEM, _SEM, *([_HBM] * (2 * nw)), pl.BlockSpec(memory_space=pltpu.VMEM)),
        input_output_aliases={i: 2 + i for i in range(2 * nw)},
        compiler_params=pltpu.CompilerParams(has_side_effects=_DATAFLOW),
    )(*[pltpu.with_memory_space_constraint(a, pltpu.HBM) for a in arrs],
      *[pltpu.with_memory_space_constraint(lax.empty(a.shape, a.dtype), pltpu.HBM) for a in arrs])
    return outs[0], outs[1], list(outs[2:2 + nw]), list(outs[2 + nw:2 + 2 * nw]), outs[-1]


def _allgather_start(shards, *, name):
    nw = len(shards)

    def body(*refs):
        x_refs, land_refs, send_sems, recv_sems = refs[:nw], refs[nw:2 * nw], refs[2 * nw], refs[2 * nw + 1]
        token = refs[-1]
        x, y, c = _me()
        me = 4 * x + 2 * y + c
        for w in range(nw):
            half = shards[w].shape[0] // 2
            own = x_refs[w].at[pl.ds(c * half, half), :]
            for k, to in enumerate([(x, y, 1 - c), (1 - x, y, c), (x, 1 - y, c), (1 - x, 1 - y, c)]):
                pltpu.make_async_remote_copy(
                    src_ref=own, dst_ref=land_refs[w].at[me], send_sem=send_sems.at[4 * w + k],
                    recv_sem=recv_sems.at[4 * w + k], device_id=to, device_id_type=MESH).start()
        token[...] = jnp.zeros_like(token)

    lands = [pltpu.HBM((8, s.shape[0] // 2, s.shape[1]), s.dtype) for s in shards]
    outs = pl.pallas_call(
        body, name=name,
        out_shape=(pltpu.SemaphoreType.DMA((4 * nw,)), pltpu.SemaphoreType.DMA((4 * nw,)),
                   *[pltpu.HBM(s.shape, s.dtype) for s in shards], *lands, jax.ShapeDtypeStruct((SUBLANES, LANES), F32)),
        in_specs=[_HBM] * (2 * nw), out_specs=(_SEM, _SEM, *([_HBM] * (2 * nw)), pl.BlockSpec(memory_space=pltpu.VMEM)),
        input_output_aliases={i: 2 + i for i in range(2 * nw)},
        compiler_params=pltpu.CompilerParams(has_side_effects=_DATAFLOW),
    )(*[pltpu.with_memory_space_constraint(s, pltpu.HBM) for s in shards],
      *[pltpu.with_memory_space_constraint(lax.empty(l.shape, l.dtype), pltpu.HBM) for l in lands])
    return outs[0], outs[1], list(outs[2:2 + nw]), list(outs[2 + nw:2 + 2 * nw]), outs[-1]


def _allgather_wait(send_sems, recv_sems, srcs, lands, after, *, name):
    nw = len(srcs)

    def body(*refs):
        x_refs, land_refs, send_sems, recv_sems = refs[:nw], refs[nw:2 * nw], refs[2 * nw], refs[2 * nw + 1]
        x, y, c = _me()
        for w in range(nw):
            half = srcs[w].shape[0] // 2
            own = x_refs[w].at[pl.ds(c * half, half), :]
            for k, (px, py, pc) in enumerate([(x, y, 1 - c), (1 - x, y, c), (x, 1 - y, c), (1 - x, 1 - y, c)]):
                cp = pltpu.make_async_remote_copy(
                    src_ref=own, dst_ref=land_refs[w].at[4 * px + 2 * py + pc], send_sem=send_sems.at[4 * w + k],
                    recv_sem=recv_sems.at[4 * w + k], device_id=(px, py, pc), device_id_type=MESH)
                cp.wait_send()
                cp.wait_recv()

    outs = pl.pallas_call(
        body, name=name,
        out_shape=(*[pltpu.HBM(a.shape, a.dtype) for a in srcs], *[pltpu.HBM(a.shape, a.dtype) for a in lands]),
        in_specs=[_HBM] * (2 * nw) + [_SEM, _SEM, _ANY], out_specs=tuple([_HBM] * (2 * nw)),
        input_output_aliases={i: i for i in range(2 * nw)},
        compiler_params=pltpu.CompilerParams(has_side_effects=_DATAFLOW),
    )(*srcs, *lands, send_sems, recv_sems, after)
    return list(outs[:nw]), list(outs[nw:])


def _pass_to_sibling(lands, *, name):
    nw = len(lands)

    def body(*refs):
        x_refs, out_refs, send_sems, recv_sems = refs[:nw], refs[nw:2 * nw], refs[2 * nw], refs[2 * nw + 1]
        x, y, c = _me()
        chips = [(1 - x, y), (x, 1 - y), (1 - x, 1 - y)]
        cps = []
        for w in range(nw):
            for k, (px, py) in enumerate(chips):
                cp = pltpu.make_async_remote_copy(
                    src_ref=x_refs[w].at[4 * px + 2 * py + c], dst_ref=out_refs[w].at[4 * px + 2 * py + c],
                    send_sem=send_sems.at[3 * w + k], recv_sem=recv_sems.at[3 * w + k], device_id=(x, y, 1 - c),
                    device_id_type=MESH)
                cp.start()
                cps.append(cp)
        for w in range(nw):
            for k, (px, py) in enumerate(chips):
                pltpu.make_async_remote_copy(
                    src_ref=x_refs[w].at[4 * px + 2 * py + c], dst_ref=out_refs[w].at[4 * px + 2 * py + 1 - c],
                    send_sem=send_sems.at[3 * w + k], recv_sem=recv_sems.at[3 * w + k], device_id=(x, y, 1 - c),
                    device_id_type=MESH).wait_recv()
        for cp in cps:
            cp.wait_send()

    return pl.pallas_call(
        body, name=name, out_shape=[jax.ShapeDtypeStruct(a.shape, a.dtype) for a in lands],
        in_specs=[_ANY] * nw, out_specs=[_ANY] * nw, input_output_aliases={i: i for i in range(nw)},
        scratch_shapes=[pltpu.SemaphoreType.DMA((3 * nw,)), pltpu.SemaphoreType.DMA((3 * nw,))],
        compiler_params=_params(),
    )(*lands)


def _chip_exchange_wait(send_sems, recv_sems, srcs, lands, after, *, name):
    nw = len(srcs)

    def body(*refs):
        x_refs, land_refs, send_sems, recv_sems = refs[:nw], refs[nw:2 * nw], refs[2 * nw], refs[2 * nw + 1]
        x, y, c = _me()
        for w in range(nw):
            for k, (px, py) in enumerate([(1 - x, y), (x, 1 - y), (1 - x, 1 - y)]):
                cp = pltpu.make_async_remote_copy(
                    src_ref=x_refs[w].at[2 * px + py], dst_ref=land_refs[w].at[2 * px + py], send_sem=send_sems.at[3 * w + k],
                    recv_sem=recv_sems.at[3 * w + k], device_id=(px, py, c), device_id_type=MESH)
                cp.wait_send()
                cp.wait_recv()

    hbm = [pltpu.HBM(a.shape, a.dtype) for a in srcs]
    outs = pl.pallas_call(
        body, name=name, out_shape=(*hbm, *hbm),
        in_specs=[_HBM] * (2 * nw) + [_SEM, _SEM, _ANY], out_specs=tuple([_HBM] * (2 * nw)),
        input_output_aliases={i: i for i in range(2 * nw)},
        compiler_params=pltpu.CompilerParams(has_side_effects=_DATAFLOW),
    )(*srcs, *lands, send_sems, recv_sems, after)
    return list(outs[:nw]), list(outs[nw:])


def _sum_slabs(landed, own_src, s_arr, *, name, tm=512):
    S, h, w = landed.shape
    tm = _tile(h, tm, 16)

    def body(s_ref, *refs):
        own = refs[S][0].astype(F32)
        acc = None
        for s in range(S):
            term = jnp.where(s_ref[0] == s, own, refs[s][0].astype(F32))
            acc = term if acc is None else acc + term
        refs[S + 1][...] = acc

    def slab(s):
        return pl.BlockSpec((1, tm, w), lambda i, s_ref: (jnp.where(s_ref[0] == s, (s + 1) % S, s), i, 0))

    return pl.pallas_call(
        body, name=name,
        grid_spec=pltpu.PrefetchScalarGridSpec(
            num_scalar_prefetch=1, grid=(h // tm,),
            in_specs=[slab(s) for s in range(S)] + [pl.BlockSpec((1, tm, w), lambda i, s_ref: (s_ref[0], i, 0))],
            out_specs=pl.BlockSpec((tm, w), lambda i, s_ref: (i, 0))),
        out_shape=jax.ShapeDtypeStruct((h, w), F32),
        compiler_params=_params(("parallel",)),
    )(s_arr, *([landed] * S), own_src)


def _half_add(g, recv, c_arr, *, name):
    S, r, w = g.shape
    h = r // 2
    tm = _tile(h, 512, 16)
    nb = h // tm

    def body(c_ref, g_ref, r_ref, o_ref):
        o_ref[...] = (g_ref[...] + r_ref[...]).astype(BF16)

    return pl.pallas_call(
        body, name=name,
        grid_spec=pltpu.PrefetchScalarGridSpec(
            num_scalar_prefetch=1, grid=(S, nb),
            in_specs=[pl.BlockSpec((1, tm, w), lambda s, i, c_ref: (s, c_ref[0] * nb + i, 0)),
                      pl.BlockSpec((1, tm, w), lambda s, i, c_ref: (s, i, 0))],
            out_specs=pl.BlockSpec((1, tm, w), lambda s, i, c_ref: (s, i, 0))),
        out_shape=jax.ShapeDtypeStruct((S, h, w), BF16),
        compiler_params=_params(("parallel", "parallel")),
    )(c_arr, g, recv)


def _layout(sizes, width, part_mult, total_mult):
    offs, rows, r = [], [], 0
    for n in sizes:
        k = -(-n // width)
        offs.append(r)
        rows.append(k)
        r += -(-k // part_mult) * part_mult
    return offs, rows, -(-r // total_mult) * total_mult


def _pack(arrs, width, part_mult, total_mult, dtype, lead=()):
    nl = len(lead)
    sizes = [math.prod(a.shape[nl:]) for a in arrs]
    offs, rows, total = _layout(sizes, width, part_mult, total_mult)
    parts, r = [], 0
    for a, n, o, k in zip(arrs, sizes, offs, rows):
        kp = -(-k // part_mult) * part_mult
        flat = a.reshape(*lead, n).astype(dtype)
        if kp * width > n:
            flat = jnp.pad(flat, [(0, 0)] * nl + [(0, kp * width - n)])
        parts.append(flat.reshape(*lead, kp, width))
        r = o + kp
    if total > r:
        parts.append(jnp.zeros((*lead, total - r, width), dtype))
    return jnp.concatenate(parts, axis=nl)


def _unpack(pool, shapes, width, part_mult, total_mult):
    lead = pool.shape[:-2]
    sizes = [math.prod(s) for s in shapes]
    offs, rows, _ = _layout(sizes, width, part_mult, total_mult)
    out = []
    for s, n, o, k in zip(shapes, sizes, offs, rows):
        flat = lax.slice_in_dim(pool, o, o + k, axis=len(lead)).reshape(*lead, k * width)
        out.append(lax.slice_in_dim(flat, 0, n, axis=len(lead)).reshape(*lead, *s))
    return out


_WEIGHTS = ("c_ctx", "w_ada", "b_ada", "g_pre_mix", "g_post_mix", "g_pre_ffn", "g_post_ffn", "w_in", "b_merge",
            "dn_conv", "dn_a_log", "dn_dt_bias", "dn_onorm", "lru_conv", "lru_conv_b", "lru_w_rg", "lru_b_rg",
            "lru_w_ig", "lru_b_ig", "lru_lambda", "w_branch_dn", "w_branch_lru", "w_out", "w_up", "ffn_dw",
            "ffn_dw_b", "w_down")
_BIG = {"w_ada": True, "w_in": True, "w_branch_dn": False, "w_branch_lru": False, "w_out": False, "w_up": True,
        "w_down": False}
_SMALL_SHARDED = ("dn_conv", "lru_conv", "lru_b_rg", "lru_b_ig", "lru_lambda", "ffn_dw")
_NCHIP = 4
_FLAT_PART = 8
_FLAT_TOTAL = 256


def _to_chip_shards(g, by_cols):
    if by_cols:
        return g.reshape(g.shape[0], _NCHIP, g.shape[1] // _NCHIP).transpose(1, 0, 2)
    return g.reshape(_NCHIP, g.shape[0] // _NCHIP, g.shape[1])


def _from_chip_shards(s, by_cols):
    if by_cols:
        return s.transpose(1, 0, 2).reshape(s.shape[1], _NCHIP * s.shape[2])
    return s.reshape(_NCHIP * s.shape[1], s.shape[2])


def _dsilu(x):
    s = _sigmoid(x)
    return s * (1.0 + x * (1.0 - s))


def kernel(x, c, ctx, c_ctx, w_ada, b_ada, g_pre_mix, g_post_mix, g_pre_ffn, g_post_ffn, w_in, b_merge, dn_conv, dn_a_log, dn_dt_bias, dn_onorm, lru_conv, lru_conv_b, lru_w_rg, lru_b_rg, lru_w_ig, lru_b_ig, lru_lambda, w_branch_dn, w_branch_lru, w_out, w_up, ffn_dw, ffn_dw_b, w_down, loss_target, m_c_ctx, m_w_ada, m_b_ada, m_g_pre_mix, m_g_post_mix, m_g_pre_ffn, m_g_post_ffn, m_w_in, m_b_merge, m_dn_conv, m_dn_a_log, m_dn_dt_bias, m_dn_onorm, m_lru_conv, m_lru_conv_b, m_lru_w_rg, m_lru_b_rg, m_lru_w_ig, m_lru_b_ig, m_lru_lambda, m_w_branch_dn, m_w_branch_lru, m_w_out, m_w_up, m_ffn_dw, m_ffn_dw_b, m_w_down, v_c_ctx, v_w_ada, v_b_ada, v_g_pre_mix, v_g_post_mix, v_g_pre_ffn, v_g_post_ffn, v_w_in, v_b_merge, v_dn_conv, v_dn_a_log, v_dn_dt_bias, v_dn_onorm, v_lru_conv, v_lru_conv_b, v_lru_w_rg, v_lru_b_rg, v_lru_w_ig, v_lru_b_ig, v_lru_lambda, v_w_branch_dn, v_w_branch_lru, v_w_out, v_w_up, v_ffn_dw, v_ffn_dw_b, v_w_down):
    W = dict(zip(_WEIGHTS, (c_ctx, w_ada, b_ada, g_pre_mix, g_post_mix, g_pre_ffn, g_post_ffn, w_in, b_merge, dn_conv,
                            dn_a_log, dn_dt_bias, dn_onorm, lru_conv, lru_conv_b, lru_w_rg, lru_b_rg, lru_w_ig, lru_b_ig,
                            lru_lambda, w_branch_dn, w_branch_lru, w_out, w_up, ffn_dw, ffn_dw_b, w_down)))
    Mo = dict(zip(_WEIGHTS, (m_c_ctx, m_w_ada, m_b_ada, m_g_pre_mix, m_g_post_mix, m_g_pre_ffn, m_g_post_ffn, m_w_in,
                             m_b_merge, m_dn_conv, m_dn_a_log, m_dn_dt_bias, m_dn_onorm, m_lru_conv, m_lru_conv_b,
                             m_lru_w_rg, m_lru_b_rg, m_lru_w_ig, m_lru_b_ig, m_lru_lambda, m_w_branch_dn,
                             m_w_branch_lru, m_w_out, m_w_up, m_ffn_dw, m_ffn_dw_b, m_w_down)))
    Vo = dict(zip(_WEIGHTS, (v_c_ctx, v_w_ada, v_b_ada, v_g_pre_mix, v_g_post_mix, v_g_pre_ffn, v_g_post_ffn, v_w_in,
                             v_b_merge, v_dn_conv, v_dn_a_log, v_dn_dt_bias, v_dn_onorm, v_lru_conv, v_lru_conv_b,
                             v_lru_w_rg, v_lru_b_rg, v_lru_w_ig, v_lru_b_ig, v_lru_lambda, v_w_branch_dn,
                             v_w_branch_lru, v_w_out, v_w_up, v_ffn_dw, v_ffn_dw_b, v_w_down)))
    B, N, D = x.shape
    NC = ctx.shape[1]
    T = NC + N
    H, HD = dn_a_log.shape[-1], dn_onorm.shape[-1]
    DNW = H * HD
    LW, LBD = lru_conv_b.shape[-1], lru_w_rg.shape[-1]
    DFF = ffn_dw_b.shape[-1]
    LC = LANES
    x_i, y_i, c_i = _me()
    s_me = 2 * x_i + y_i
    tm = _tile(math.gcd(NC, N), 256, 16)

    def whole(n, g):
        r, w_ = W[n].shape[1:]
        return g.reshape(_NCHIP, r, w_) if _BIG[n] else g.reshape(_NCHIP * r, w_)

    first = ("w_ada", "w_in")
    later = tuple(n for n in _BIG if n not in first)
    shard16 = {n: W[n][0].astype(BF16) for n in _BIG}
    full = {n: whole(n, g) for n, g in zip(first, _allgather_halves([shard16[n] for n in first], name="allgather_first"))}

    small_local = [W[n][0].reshape(-1, W[n].shape[-1]) for n in _SMALL_SHARDED]
    small_shapes = [a.shape for a in small_local]
    spack = _pack(small_local, LANES, _FLAT_PART, _FLAT_PART, F32)
    sgath = _allgather_small(spack)[0::2]
    sfull = {n: _from_chip_shards(s, True)
             for n, s in zip(_SMALL_SHARDED, _unpack(sgath, small_shapes, LANES, _FLAT_PART, _FLAT_PART))}

    later16, sgath = lax.optimization_barrier(([shard16[n] for n in later], sgath))
    ag_send, ag_recv, ag_src, ag_land, ag_token = _allgather_start(later16, name="ag_start")

    o_a = 4 * DNW
    o_xl = o_a + 4 * H
    o_mg = o_xl + 2 * LW
    wi_ = _from_chip_shards(full["w_in"], True)
    nj = LW // LC
    lru_cols = jnp.stack([wi_[:, o_xl:o_xl + LW].reshape(D, nj, LC), wi_[:, o_xl + LW:o_mg].reshape(D, nj, LC)],
                         axis=2).reshape(D, 2 * LW)
    wp = jnp.concatenate([wi_[:, :o_a], lru_cols, wi_[:, o_mg:], wi_[:, o_a:o_xl],
                          jnp.zeros((D, LANES - 4 * H), BF16)], axis=1)
    p_lru, p_mg, p_ab = 4 * DNW, 4 * DNW + 2 * LW, 4 * DNW + 2 * LW + 2 * D
    PW = p_ab + LANES

    MR = LANES
    cond = jnp.concatenate([c, c_ctx[None], jnp.zeros((MR - B - 1, D), F32)], axis=0)
    silu_rows = _rowwise(lambda a: (_silu(a),), [cond], [F32], name="cond_silu")[0]
    mod = _matmul(silu_rows, full["w_ada"], b_shards=(0, _NCHIP), name="ada_fwd") + b_ada + ag_token[0, 0]
    mx = mod[:B].reshape(B, 6, D)
    mc = mod[B].reshape(6, D)
    zero = jnp.zeros((B, D), F32)
    tab = jnp.stack([jnp.stack([jnp.broadcast_to(mc[0], (B, D)), jnp.broadcast_to(mc[1], (B, D))] + [zero] * 6, axis=1),
                     jnp.stack([mx[:, 0], mx[:, 1]] + [zero] * 6, axis=1)], axis=1)
    vecs = jnp.stack([mx[:, 2], mx[:, 3], mx[:, 4], mx[:, 5]] + [zero] * 4, axis=1)
    gains = jnp.concatenate([g_post_mix, g_pre_ffn, g_post_ffn, jnp.zeros((5, D), F32)], axis=0)

    h = jnp.concatenate([ctx, x], axis=1)
    u = _premix_fwd(h, g_pre_mix, tab, nc=NC, tm=tm)
    p = _matmul(u, wp, name="in_fwd")
    dkw = dict(B=B, T=T, nc=NC, H=H, HD=HD)
    qkv = _dnprep_fwd(p, sfull["dn_conv"], **dkw)
    prm = jnp.concatenate([
        jnp.concatenate([dn_a_log.reshape(1, 2 * H), jnp.zeros((1, LANES - 2 * H), F32)], axis=1),
        jnp.concatenate([dn_dt_bias.reshape(1, 2 * H), jnp.zeros((1, LANES - 2 * H), F32)], axis=1),
        jnp.zeros((6, LANES), F32)], axis=0)
    gtm = _tile(B * T, 512, 16)
    gb = _gb_fwd(p, prm, rows=B * T, col0=p_ab, H=H, tm=gtm)
    y_dn, o_dn, *dn_res = _delta_fwd(qkv, gb, p, dn_onorm, **dkw)
    lv = jnp.concatenate([lru_conv_b, sfull["lru_b_rg"], sfull["lru_b_ig"], sfull["lru_lambda"], jnp.zeros((1, LW), F32)], axis=0)
    wr = _blockdiag(lru_w_rg[0], LC).astype(BF16)
    wi = _blockdiag(lru_w_ig[0], LC).astype(BF16)
    lkw = dict(B=B, T=T, nc=NC, LW=LW, col0=p_lru, C=LC)
    y_lru = _lru_fwd(p, sfull["lru_conv"], lv, wr, wi, **lkw)
    ag_src, ag_land = _allgather_wait(ag_send, ag_recv, ag_src, ag_land, y_lru, name="ag_wait")
    me_piece = 4 * x_i + 2 * y_i + c_i
    for n, src, land in zip(later, ag_src, _pass_to_sibling(ag_land, name="ag_pass")):
        own = lax.dynamic_slice_in_dim(src, c_i * (src.shape[0] // 2), src.shape[0] // 2, axis=0)
        full[n] = whole(n, lax.dynamic_update_index_in_dim(land, own, me_piece, axis=0))
    Ydn = _matmul(y_dn, full["w_branch_dn"], name="bdn_fwd")
    Ylru = _matmul(y_lru, full["w_branch_lru"], name="blru_fwd")
    mkw = dict(B=B, T=T, nc=NC, D=D, col0=p_mg, tm=tm)
    mixin = _merge_fwd(p, Ydn, Ylru, b_merge, **mkw)
    mix = _matmul(mixin, full["w_out"], name="out_fwd")
    h1, u2 = _post_fwd(x, mix, gains, vecs, tm=tm)
    F = _matmul(u2, full["w_up"], b_shards=(0, _NCHIP), name="up_fwd")
    w9 = sfull["ffn_dw"]
    ftc = _tile(DFF, 256)
    f = _ffn_act_fwd(F, w9, ffn_dw_b, B=B, N=N, DFF=DFF, tc=ftc)
    dn = _matmul(f, full["w_down"], name="down_fwd")
    ddn, dout, sums_f = _final(h1, dn, loss_target, gains, vecs, tm=tm)

    G = {}
    df = _matmul(ddn, full["w_down"], tb=True, name="down_bwd_x")
    G["w_down"] = _matmul(f, ddn, ta=True, name="down_bwd_w")
    dFg, dFv, dwb = _ffn_act_bwd(F, w9, ffn_dw_b, df, B=B, N=N, DFF=DFF, tc=ftc)
    hs = _NCHIP // 2
    du2 = _matmul(dFg, full["w_up"], tb=True, b_shards=(0, hs), name="up_bwd_xg")
    du2 = _matmul(dFv, full["w_up"], tb=True, b_shards=(hs, hs), add=du2, name="up_bwd_xv")
    G["w_up"] = jnp.concatenate([_matmul(u2, dFg, ta=True, out_shards=hs, name="up_bwd_wg"),
                                 _matmul(u2, dFv, ta=True, out_shards=hs, name="up_bwd_wv")], axis=0)
    dx1, dmix, sums_p = _post_bwd(x, mix, gains, vecs, dout, du2, tm=tm)
    dmixin = _matmul(dmix, full["w_out"], tb=True, name="out_bwd_x")
    G["w_out"] = _matmul(mixin, dmix, ta=True, name="out_bwd_w")
    dp = jnp.zeros((B * T, PW), BF16)
    dYdn, dYlru, dp, sums_m = _merge_bwd(p, Ydn, Ylru, b_merge, dmixin, dp, **mkw)
    dy_dn = _matmul(dYdn, full["w_branch_dn"], tb=True, name="bdn_bwd_x")
    G["w_branch_dn"] = _matmul(y_dn, dYdn, ta=True, name="bdn_bwd_w")
    dy_lru = _matmul(dYlru, full["w_branch_lru"], tb=True, name="blru_bwd_x")
    G["w_branch_lru"] = _matmul(y_lru, dYlru, ta=True, name="blru_bwd_w")

    c_arr = c_i.astype(jnp.int32).reshape(1)
    s_arr = s_me.astype(jnp.int32).reshape(1)

    def chip_sums(names, tag):
        slabs = [G[n] if _BIG[n] else G[n].reshape(_NCHIP, G[n].shape[0] // _NCHIP, G[n].shape[1]) for n in names]
        from_sibling = _sibling_send_halves(slabs, name="rs_sibling_" + tag)
        return [_half_add(g, r, c_arr, name="rs_add_" + n) for n, g, r in zip(names, slabs, from_sibling)]

    early = tuple(n for n in _BIG if n in G)
    late = tuple(n for n in _BIG if n not in G)
    cx_send, cx_recv, cx_src, cx_land, cx_token = _chip_exchange_start(chip_sums(early, "early"), name="cx_start")
    dp, dcw_l, dlv, dwr, dwi = _lru_bwd(p, sfull["lru_conv"], lv + cx_token[0, 0], wr, wi, dy_lru, dp, **lkw)
    dqkv, dgb, dp, don = _delta_bwd(qkv, gb, p, dn_onorm, o_dn, dn_res, dy_dn, dp, **dkw)
    dp, dprm = _gb_bwd(p, prm, dgb, dp, rows=B * T, col0=p_ab, H=H, tm=gtm)
    dp, dcw_d = _dnprep_bwd(p, sfull["dn_conv"], dqkv, dp, **dkw)
    dU = _matmul(dp, wp, tb=True, name="in_bwd_x")
    dwp = _matmul(u, dp, ta=True, name="in_bwd_w")
    grad_x, sums_pm = _premix_bwd(h, g_pre_mix, tab, dU, dx1, nc=NC, tm=tm)
    dlru = dwp[:, p_lru:p_mg].reshape(D, nj, 2, LC)
    G["w_in"] = _to_chip_shards(jnp.concatenate([dwp[:, :o_a], dwp[:, p_ab:p_ab + 4 * H], dlru[:, :, 0].reshape(D, LW),
                                                 dlru[:, :, 1].reshape(D, LW), dwp[:, p_mg:p_ab]], axis=1), True)

    dmod_x = jnp.stack([sums_pm[:, 1, 0], sums_pm[:, 1, 1], sums_p[:, 0], sums_p[:, 1], sums_p[:, 2], sums_f[:, 0]],
                       axis=1).reshape(B, 6 * D)
    dmod_c = jnp.concatenate([sums_pm[:, 0, 0].sum(0), sums_pm[:, 0, 1].sum(0), jnp.zeros((4 * D,), F32)])[None]
    dmod = jnp.concatenate([dmod_x, dmod_c, jnp.zeros((MR - B - 1, 6 * D), F32)], axis=0)
    G["w_ada"] = _matmul(silu_rows, dmod, ta=True, out_shards=_NCHIP, name="ada_bwd_w")
    dsilu = _matmul(dmod, full["w_ada"], tb=True, b_shards=(0, _NCHIP), name="ada_bwd_x")

    g_small = {
        "c_ctx": dsilu[B] * _dsilu(c_ctx),
        "b_ada": dmod[:B + 1].sum(0)[None],
        "g_pre_mix": sums_pm[:, :, 2].sum((0, 1))[None],
        "g_post_mix": sums_p[:, 3].sum(0)[None],
        "g_pre_ffn": sums_p[:, 4].sum(0)[None],
        "g_post_ffn": sums_f[:, 1].sum(0)[None],
        "b_merge": sums_m[0:1],
        "dn_conv": dcw_d[0:4][None],
        "dn_a_log": dprm[0, :2 * H].reshape(1, 2, H),
        "dn_dt_bias": dprm[1, :2 * H].reshape(1, 2, H),
        "dn_onorm": don[:, 0].sum(0)[None],
        "lru_conv": dcw_l[0:4][None],
        "lru_conv_b": dlv[0:1],
        "lru_w_rg": _blockdiag_extract(dwr, LBD)[None],
        "lru_b_rg": dlv[1:3][None],
        "lru_w_ig": _blockdiag_extract(dwi, LBD)[None],
        "lru_b_ig": dlv[3:5][None],
        "lru_lambda": dlv[5:7][None],
        "ffn_dw": dwb[0:9].reshape(1, 3, 3, DFF),
        "ffn_dw_b": dwb[9:10],
    }
    small_names = tuple(n for n in _WEIGHTS if n not in _BIG)
    loss_part = sums_f[:, 2].sum().reshape(1)
    gs_list = [g_small[n] for n in small_names] + [loss_part]
    gs_shapes = [a.shape for a in gs_list]
    gpack = _pack(gs_list, LANES, _FLAT_PART, _FLAT_TOTAL, F32)
    gsum = _sum_lead(_allgather_small(gpack), name="small_sum", tm=512, mult=SUBLANES)
    gs_red = dict(zip(small_names + ("loss",), _unpack(gsum, gs_shapes, LANES, _FLAT_PART, _FLAT_TOTAL)))
    loss = gs_red["loss"][0]

    cx_src, cx_land = _chip_exchange_wait(cx_send, cx_recv, cx_src, cx_land, dsilu, name="cx_wait")
    late_sums = chip_sums(late, "late")
    late_land = _chip_exchange(late_sums, name="chip_exchange")
    half = {n: _sum_slabs(l, src, s_arr, name="rs_sum_" + n)
            for n, l, src in zip(early + late, list(cx_land) + list(late_land), list(cx_src) + list(late_sums))}
    halves = [half[n] for n in _BIG]
    sib_halves = _sibling_swap(halves, name="rs_gather")

    grads, deltas, new_m, new_v = {}, {}, {}, {}
    for n, own, sib in zip(_BIG, halves, sib_halves):
        shp = W[n].shape
        outs = _adamw_halves(W[n][0], own, sib, Mo[n][0], Vo[n][0], c_arr, name="adamw_" + n)
        grads[n], deltas[n], new_m[n], new_v[n] = (o.reshape(shp) for o in outs)
    for n in small_names:
        g = gs_red[n]
        if n in _SMALL_SHARDED:
            k = W[n].shape[-1]
            g = lax.dynamic_slice_in_dim(g, s_me * k, k, axis=g.ndim - 1)
        grads[n] = g.reshape(W[n].shape)
    sm_shapes = [W[n].shape for n in small_names]
    pk = lambda d: _pack([d[n] for n in small_names], LANES, _FLAT_PART, _FLAT_TOTAL, F32)
    d_, m_, v_ = _adamw(pk(W), pk(grads), pk(Mo), pk(Vo), name="adamw_small")
    for dst, pool_ in ((deltas, d_), (new_m, m_), (new_v, v_)):
        dst.update(zip(small_names, _unpack(pool_, sm_shapes, LANES, _FLAT_PART, _FLAT_TOTAL)))
    return (loss, grad_x, *[grads[n] for n in _WEIGHTS], *[deltas[n] for n in _WEIGHTS],
            *[new_m[n] for n in _WEIGHTS], *[new_v[n] for n in _WEIGHTS])
```

```python
import functools
import math

import jax
import jax.numpy as jnp
from jax import lax
from jax.experimental import pallas as pl
from jax.experimental.pallas import tpu as pltpu

F32 = jnp.float32
BF16 = jnp.bfloat16
EPS = 1e-6
GRID_W = 64
CHUNK = 256
LRU_C = 8.0
LANES = 128
SUBLANES = 8
VMEM_LIMIT = 56 * 1024 * 1024
ADAM_LR, ADAM_B1, ADAM_B2, ADAM_EPS, ADAM_WD, ADAM_STEP = 0.001, 0.9, 0.999, 1e-08, 0.01, 10
MESH = pl.DeviceIdType.MESH


def _tile(n, target, mult=LANES):
    best = None
    for t in range(mult, min(n, target) + 1, mult):
        if n % t == 0:
            best = t
    return best if best is not None else n


def _params(sem=None, **kw):
    return pltpu.CompilerParams(dimension_semantics=sem, vmem_limit_bytes=VMEM_LIMIT, **kw)


def _sigmoid(x):
    return 1.0 / (1.0 + jnp.exp(-x))


def _silu(x):
    return x * _sigmoid(x)


def _softplus(x):
    return jnp.maximum(x, 0.0) + jnp.log(1.0 + jnp.exp(-jnp.abs(x)))


def _gelu(x):
    return 0.5 * x * (1.0 + jnp.tanh(math.sqrt(2.0 / math.pi) * (x + 0.044715 * x * x * x)))


def _rmsn(u, gain):
    return u * lax.rsqrt(jnp.mean(u * u, axis=-1, keepdims=True) + EPS) * gain


_MM_VMEM = 40 * 1024 * 1024


def _matmul(a, b, *, ta=False, tb=False, add=None, b_shards=None, out_shards=None, out_dtype=F32, name,
            tm=1024, tn=2048, tk=1024):
    (K, M) = a.shape if ta else a.shape[::-1]
    if b_shards is not None:
        s0, ns = b_shards
        bsh = (b.shape[1], ns * b.shape[2])
        nsh = b.shape[2]
    else:
        bsh = b.shape
    N = bsh[0] if tb else bsh[1]
    assert (bsh[1] if tb else bsh[0]) == K, (a.shape, b.shape, ta, tb)
    tm = _tile(M, tm)
    tk = _tile(nsh if (b_shards is not None and tb) else K, tk)
    nlim = nsh if (b_shards is not None and not tb) else (N // out_shards if out_shards else N)
    osz = jnp.dtype(out_dtype).itemsize + (4 if add is not None else 0)
    while True:
        tn_ = _tile(nlim, tn)
        need = 2 * (tm * tk * a.dtype.itemsize + tk * tn_ * b.dtype.itemsize + tm * tn_ * osz) + 4 * tm * tn_
        if need <= _MM_VMEM or tn <= LANES:
            break
        tn //= 2
    tn = tn_
    nk = K // tk
    dims = (((0 if ta else 1,), (1 if tb else 0,)), ((), ()))

    def body(a_ref, b_ref, *rest):
        (c_ref, o_ref, acc_ref) = rest if add is not None else (None, *rest)
        k = pl.program_id(2)

        @pl.when(k == 0)
        def _():
            acc_ref[...] = jnp.zeros_like(acc_ref) if c_ref is None else c_ref[...]

        bv = b_ref[0] if b_shards is not None else b_ref[...]
        acc_ref[...] += lax.dot_general(a_ref[...].astype(BF16), bv.astype(BF16), dims, preferred_element_type=F32)

        @pl.when(k == nk - 1)
        def _():
            if out_shards:
                o_ref[0] = acc_ref[...].astype(out_dtype)
            else:
                o_ref[...] = acc_ref[...].astype(out_dtype)

    a_spec = pl.BlockSpec((tk, tm), lambda i, j, k: (k, i)) if ta else pl.BlockSpec((tm, tk), lambda i, j, k: (i, k))
    if b_shards is None:
        b_spec = pl.BlockSpec((tn, tk), lambda i, j, k: (j, k)) if tb else pl.BlockSpec((tk, tn), lambda i, j, k: (k, j))
    elif tb:
        per = nsh // tk
        b_spec = pl.BlockSpec((1, tn, tk), lambda i, j, k: (s0 + k // per, j, k % per))
    else:
        per = nsh // tn
        b_spec = pl.BlockSpec((1, tk, tn), lambda i, j, k: (s0 + j // per, k, j % per))
    o_spec = pl.BlockSpec((tm, tn), lambda i, j, k: (i, j))
    if out_shards:
        oper = N // out_shards // tn
        out_spec = pl.BlockSpec((1, tm, tn), lambda i, j, k: (j // oper, i, j % oper))
        out_shape = jax.ShapeDtypeStruct((out_shards, M, N // out_shards), out_dtype)
    else:
        out_spec, out_shape = o_spec, jax.ShapeDtypeStruct((M, N), out_dtype)
    return pl.pallas_call(
        body, name=name, grid=(M // tm, N // tn, nk),
        in_specs=[a_spec, b_spec] + ([o_spec] if add is not None else []),
        out_specs=out_spec, out_shape=out_shape,
        scratch_shapes=[pltpu.VMEM((tm, tn), F32)],
        compiler_params=_params(("parallel", "parallel", "arbitrary")),
    )(*((a, b) + ((add,) if add is not None else ())))


def _premix_math(h, gain, shift, scale):
    return _rmsn(h, gain) * (1.0 + scale) + shift


def _premix_fwd(h, gain, tab, *, nc, tm):
    B, T, D = h.shape
    nt, nct = T // tm, nc // tm

    def body(h_ref, g_ref, tab_ref, u_ref):
        tabv = tab_ref[0, 0]
        u_ref[...] = _premix_math(h_ref[0], g_ref[...], tabv[0:1], tabv[1:2]).astype(BF16)

    return pl.pallas_call(
        body, name="premix_fwd", grid=(B, nt),
        in_specs=[pl.BlockSpec((1, tm, D), lambda b, t: (b, t, 0)),
                  pl.BlockSpec((1, D), lambda b, t: (0, 0)),
                  pl.BlockSpec((1, 1, 8, D), lambda b, t: (b, jnp.where(t < nct, 0, 1), 0, 0))],
        out_specs=pl.BlockSpec((tm, D), lambda b, t: (b * nt + t, 0)),
        out_shape=jax.ShapeDtypeStruct((B * T, D), BF16),
        compiler_params=_params(("parallel", "parallel")),
    )(h, gain, tab)


def _premix_bwd(h, gain, tab, du, dres, *, nc, tm):
    B, T, D = h.shape
    nt, nct = T // tm, nc // tm
    N = T - nc

    def body(h_ref, g_ref, tab_ref, du_ref, dres_ref, dx_ref, sums_ref):
        t = pl.program_id(1)
        tabv = tab_ref[0, 0]
        _, vjp = jax.vjp(_premix_math, h_ref[0], g_ref[...], tabv[0:1], tabv[1:2])
        dh, dgain, dshift, dscale = vjp(du_ref[...].astype(F32))

        @pl.when((t == 0) | (t == nct))
        def _():
            sums_ref[...] = jnp.zeros_like(sums_ref)

        sums_ref[0, 0, 0:1, :] += dshift
        sums_ref[0, 0, 1:2, :] += dscale
        sums_ref[0, 0, 2:3, :] += dgain

        @pl.when(t >= nct)
        def _():
            dx_ref[0] = dres_ref[...] + dh

    lat = lambda b, t: jnp.maximum(t - nct, 0)
    return pl.pallas_call(
        body, name="premix_bwd", grid=(B, nt),
        in_specs=[pl.BlockSpec((1, tm, D), lambda b, t: (b, t, 0)),
                  pl.BlockSpec((1, D), lambda b, t: (0, 0)),
                  pl.BlockSpec((1, 1, 8, D), lambda b, t: (b, jnp.where(t < nct, 0, 1), 0, 0)),
                  pl.BlockSpec((tm, D), lambda b, t: (b * nt + t, 0)),
                  pl.BlockSpec((tm, D), lambda b, t: (b * (nt - nct) + lat(b, t), 0))],
        out_specs=[pl.BlockSpec((1, tm, D), lambda b, t: (b, lat(b, t), 0)),
                   pl.BlockSpec((1, 1, 8, D), lambda b, t: (b, jnp.where(t < nct, 0, 1), 0, 0))],
        out_shape=[jax.ShapeDtypeStruct((B, N, D), F32), jax.ShapeDtypeStruct((B, 2, 8, D), F32)],
        compiler_params=_params(("parallel", "arbitrary")),
    )(h, gain, tab, du, dres)


def _merge_math(mgd, mgl, yd, yl, bd, bl):
    return _sigmoid(mgd + bd) * yd + _sigmoid(mgl + bl) * yl


def _merge_fwd(p, ydn, ylru, b_merge, *, B, T, nc, D, col0, tm):
    N = T - nc
    ntl, nt, nct, cb = N // tm, T // tm, nc // tm, col0 // D

    def body(mgd_ref, mgl_ref, yd_ref, yl_ref, bm_ref, o_ref):
        o_ref[...] = _merge_math(mgd_ref[...], mgl_ref[...], yd_ref[...], yl_ref[...],
                                 bm_ref[:, 0:D], bm_ref[:, D:2 * D]).astype(BF16)

    prow = lambda b, t: b * nt + nct + t
    return pl.pallas_call(
        body, name="merge_fwd", grid=(B, ntl),
        in_specs=[pl.BlockSpec((tm, D), lambda b, t: (prow(b, t), cb)),
                  pl.BlockSpec((tm, D), lambda b, t: (prow(b, t), cb + 1)),
                  pl.BlockSpec((tm, D), lambda b, t: (b * ntl + t, 0)),
                  pl.BlockSpec((tm, D), lambda b, t: (b * ntl + t, 0)),
                  pl.BlockSpec((1, 2 * D), lambda b, t: (0, 0))],
        out_specs=pl.BlockSpec((tm, D), lambda b, t: (b * ntl + t, 0)),
        out_shape=jax.ShapeDtypeStruct((B * N, D), BF16),
        compiler_params=_params(("parallel", "parallel")),
    )(p, p, ydn, ylru, b_merge)


def _merge_bwd(p, ydn, ylru, b_merge, dmix, dp, *, B, T, nc, D, col0, tm):
    N = T - nc
    ntl, nt, nct, cb = N // tm, T // tm, nc // tm, col0 // D
    assert col0 % (2 * D) == 0

    def body(mgd_ref, mgl_ref, yd_ref, yl_ref, bm_ref, dm_ref, dp_any, dyd_ref, dyl_ref, dp_ref, sums_ref):
        _, vjp = jax.vjp(_merge_math, mgd_ref[...], mgl_ref[...], yd_ref[...], yl_ref[...],
                         bm_ref[:, 0:D], bm_ref[:, D:2 * D])
        dmgd, dmgl, dyd, dyl, dbd, dbl = vjp(dm_ref[...])
        dyd_ref[...] = dyd.astype(BF16)
        dyl_ref[...] = dyl.astype(BF16)
        dp_ref[:, 0:D] = dmgd.astype(BF16)
        dp_ref[:, D:2 * D] = dmgl.astype(BF16)

        @pl.when((pl.program_id(0) == 0) & (pl.program_id(1) == 0))
        def _():
            sums_ref[...] = jnp.zeros_like(sums_ref)

        sums_ref[0:1, 0:D] += dbd
        sums_ref[0:1, D:2 * D] += dbl

    prow = lambda b, t: b * nt + nct + t
    row = pl.BlockSpec((tm, D), lambda b, t: (b * ntl + t, 0))
    return pl.pallas_call(
        body, name="merge_bwd", grid=(B, ntl),
        in_specs=[pl.BlockSpec((tm, D), lambda b, t: (prow(b, t), cb)),
                  pl.BlockSpec((tm, D), lambda b, t: (prow(b, t), cb + 1)),
                  row, row, pl.BlockSpec((1, 2 * D), lambda b, t: (0, 0)), row,
                  pl.BlockSpec(memory_space=pl.ANY)],
        out_specs=[row, row,
                   pl.BlockSpec((tm, 2 * D), lambda b, t: (prow(b, t), cb // 2)),
                   pl.BlockSpec((8, 2 * D), lambda b, t: (0, 0))],
        out_shape=[jax.ShapeDtypeStruct((B * N, D), BF16), jax.ShapeDtypeStruct((B * N, D), BF16),
                   jax.ShapeDtypeStruct(dp.shape, dp.dtype), jax.ShapeDtypeStruct((8, 2 * D), F32)],
        input_output_aliases={6: 2},
        compiler_params=_params(("arbitrary", "arbitrary")),
    )(p, p, ydn, ylru, b_merge, dmix, dp)


def _post_math(x, mix, g1, gate, g2, sh, sc):
    h1 = x + _rmsn(mix, g1) * gate
    return h1, _rmsn(h1, g2) * (1.0 + sc) + sh


def _post_fwd(x, mix, gains, vecs, *, tm):
    B, N, D = x.shape
    ntl = N // tm

    def body(x_ref, mix_ref, g_ref, v_ref, h1_ref, u2_ref):
        v = v_ref[0]
        h1, u2 = _post_math(x_ref[0], mix_ref[...], g_ref[0:1], v[0:1], g_ref[1:2], v[1:2], v[2:3])
        h1_ref[...] = h1
        u2_ref[...] = u2.astype(BF16)

    row = pl.BlockSpec((tm, D), lambda b, t: (b * ntl + t, 0))
    return pl.pallas_call(
        body, name="post_fwd", grid=(B, ntl),
        in_specs=[pl.BlockSpec((1, tm, D), lambda b, t: (b, t, 0)), row,
                  pl.BlockSpec((8, D), lambda b, t: (0, 0)), pl.BlockSpec((1, 8, D), lambda b, t: (b, 0, 0))],
        out_specs=[row, row],
        out_shape=[jax.ShapeDtypeStruct((B * N, D), F32), jax.ShapeDtypeStruct((B * N, D), BF16)],
        compiler_params=_params(("parallel", "parallel")),
    )(x, mix, gains, vecs)


def _post_bwd(x, mix, gains, vecs, dh1, du2, *, tm):
    B, N, D = x.shape
    ntl = N // tm

    def body(x_ref, mix_ref, g_ref, v_ref, dh1_ref, du2_ref, dx_ref, dmix_ref, sums_ref):
        v = v_ref[0]
        _, vjp = jax.vjp(_post_math, x_ref[0], mix_ref[...], g_ref[0:1], v[0:1], g_ref[1:2], v[1:2], v[2:3])
        dx, dmix, dg1, dgate, dg2, dsh, dsc = vjp((dh1_ref[...], du2_ref[...]))
        dx_ref[...] = dx
        dmix_ref[...] = dmix.astype(BF16)

        @pl.when(pl.program_id(1) == 0)
        def _():
            sums_ref[...] = jnp.zeros_like(sums_ref)

        sums_ref[0, 0:1, :] += dgate
        sums_ref[0, 1:2, :] += dsh
        sums_ref[0, 2:3, :] += dsc
        sums_ref[0, 3:4, :] += dg1
        sums_ref[0, 4:5, :] += dg2

    row = pl.BlockSpec((tm, D), lambda b, t: (b * ntl + t, 0))
    return pl.pallas_call(
        body, name="post_bwd", grid=(B, ntl),
        in_specs=[pl.BlockSpec((1, tm, D), lambda b, t: (b, t, 0)), row,
                  pl.BlockSpec((8, D), lambda b, t: (0, 0)), pl.BlockSpec((1, 8, D), lambda b, t: (b, 0, 0)), row, row],
        out_specs=[row, row, pl.BlockSpec((1, 8, D), lambda b, t: (b, 0, 0))],
        out_shape=[jax.ShapeDtypeStruct((B * N, D), F32), jax.ShapeDtypeStruct((B * N, D), BF16),
                   jax.ShapeDtypeStruct((B, 8, D), F32)],
        compiler_params=_params(("parallel", "arbitrary")),
    )(x, mix, gains, vecs, dh1, du2)


def _final_math(dn, g4, gate5):
    return _rmsn(dn, g4) * gate5


def _final(h1, dn, target, gains, vecs, *, tm):
    B, N, D = target.shape
    ntl = N // tm

    def body(h1_ref, dn_ref, t_ref, g_ref, v_ref, ddn_ref, dout_ref, sums_ref):
        v = v_ref[0]
        y, vjp = jax.vjp(_final_math, dn_ref[...], g_ref[2:3], v[3:4])
        err = h1_ref[...] + y - t_ref[0]
        dout = err * (1.0 / D)
        ddn, dg4, dgate5 = vjp(dout)
        ddn_ref[...] = ddn.astype(BF16)
        dout_ref[...] = dout

        @pl.when(pl.program_id(1) == 0)
        def _():
            sums_ref[...] = jnp.zeros_like(sums_ref)

        sums_ref[0, 0:1, :] += dgate5
        sums_ref[0, 1:2, :] += dg4
        sums_ref[0, 2:3, :] += jnp.sum(err * err, axis=0, keepdims=True) * (0.5 / D)

    row = pl.BlockSpec((tm, D), lambda b, t: (b * ntl + t, 0))
    return pl.pallas_call(
        body, name="final", grid=(B, ntl),
        in_specs=[row, row, pl.BlockSpec((1, tm, D), lambda b, t: (b, t, 0)),
                  pl.BlockSpec((8, D), lambda b, t: (0, 0)), pl.BlockSpec((1, 8, D), lambda b, t: (b, 0, 0))],
        out_specs=[row, row, pl.BlockSpec((1, 8, D), lambda b, t: (b, 0, 0))],
        out_shape=[jax.ShapeDtypeStruct((B * N, D), BF16), jax.ShapeDtypeStruct((B * N, D), F32),
                   jax.ShapeDtypeStruct((B, 8, D), F32)],
        compiler_params=_params(("parallel", "arbitrary")),
    )(h1, dn, target, gains, vecs)


def _shift(x, s):
    s = s % x.shape[0]
    return x if s == 0 else pltpu.roll(x, s, 0)


def _seg_taps(T, nc, width, pad_left):
    t = lax.broadcasted_iota(jnp.int32, (T, 1), 0)
    pos = jnp.where(t < nc, t, t - nc)
    seg = jnp.where(t < nc, nc, T - nc)
    taps = []
    for k in range(width):
        src = pos + (k - pad_left)
        taps.append((pad_left - k, (src >= 0) & (src < seg)))
    return taps


def _grid_taps(N):
    t = lax.broadcasted_iota(jnp.int32, (N, 1), 0)
    wcol = t % GRID_W
    taps = []
    for dr in (-1, 0, 1):
        for dw in (-1, 0, 1):
            off = dr * GRID_W + dw
            ok = (wcol + dw >= 0) & (wcol + dw < GRID_W) & (t + dr * GRID_W >= 0) & (t + dr * GRID_W < N)
            taps.append((-off, ok))
    return taps


def _conv_fwd(x, w, taps):
    y = jnp.zeros_like(x)
    for k, (s, m) in enumerate(taps):
        y = y + w[k:k + 1] * jnp.where(m, _shift(x, s), 0.0)
    return y


def _conv_bwd(x, w, taps, dy):
    dx = jnp.zeros_like(x)
    dws = []
    for k, (s, m) in enumerate(taps):
        dym = jnp.where(m, dy, 0.0)
        dx = dx + w[k:k + 1] * _shift(dym, -s)
        dws.append(jnp.sum(dym * _shift(x, s), axis=0, keepdims=True))
    return dx, jnp.concatenate(dws, axis=0)


def _ffn_act_fwd(F, w9, bias, *, B, N, DFF, tc):
    nj = DFF // tc

    def body(fg_ref, fv_ref, w_ref, b_ref, o_ref):
        fg = _conv_fwd(fg_ref[...], w_ref[...], _grid_taps(N)) + b_ref[...]
        o_ref[...] = (_gelu(fg) * fv_ref[...]).astype(BF16)

    return pl.pallas_call(
        body, name="ffn_act_fwd", grid=(B, nj),
        in_specs=[pl.BlockSpec((N, tc), lambda b, j: (b, j)), pl.BlockSpec((N, tc), lambda b, j: (b, nj + j)),
                  pl.BlockSpec((9, tc), lambda b, j: (0, j)), pl.BlockSpec((1, tc), lambda b, j: (0, j))],
        out_specs=pl.BlockSpec((N, tc), lambda b, j: (b, j)),
        out_shape=jax.ShapeDtypeStruct((B * N, DFF), BF16),
        compiler_params=_params(("parallel", "parallel")),
    )(F, F, w9, bias)


def _ffn_act_bwd(F, w9, bias, df, *, B, N, DFF, tc):
    nj = DFF // tc

    def body(fg_ref, fv_ref, w_ref, b_ref, df_ref, dfg_ref, dfv_ref, dwb_ref):
        taps = _grid_taps(N)
        x = fg_ref[...]
        fg, vjp = jax.vjp(lambda a: _gelu(a), _conv_fwd(x, w_ref[...], taps) + b_ref[...])
        dfl = df_ref[...]
        dfv_ref[...] = (dfl * fg).astype(BF16)
        (dpre,) = vjp(dfl * fv_ref[...])
        dx, dw = _conv_bwd(x, w_ref[...], taps, dpre)
        dfg_ref[...] = dx.astype(BF16)

        @pl.when(pl.program_id(1) == 0)
        def _():
            dwb_ref[...] = jnp.zeros_like(dwb_ref)

        dwb_ref[0:9, :] += dw
        dwb_ref[9:10, :] += jnp.sum(dpre, axis=0, keepdims=True)

    col = pl.BlockSpec((N, tc), lambda j, b: (b, j))
    return pl.pallas_call(
        body, name="ffn_act_bwd", grid=(nj, B),
        in_specs=[col, pl.BlockSpec((N, tc), lambda j, b: (b, nj + j)),
                  pl.BlockSpec((9, tc), lambda j, b: (0, j)), pl.BlockSpec((1, tc), lambda j, b: (0, j)), col],
        out_specs=[col, col, pl.BlockSpec((16, tc), lambda j, b: (0, j))],
        out_shape=[jax.ShapeDtypeStruct((B * N, DFF), BF16), jax.ShapeDtypeStruct((B * N, DFF), BF16),
                   jax.ShapeDtypeStruct((16, DFF), F32)],
        compiler_params=_params(("parallel", "arbitrary")),
    )(F, F, w9, bias, df)


def _dnprep_math(y, is_qk, scale):
    s = _silu(y)
    n = s * lax.rsqrt(jnp.sum(s * s, axis=-1, keepdims=True) + EPS) * scale
    return jnp.where(is_qk, n, s)


def _dnprep_fwd(p, cw, *, B, T, nc, H, HD):
    def body(x_ref, w_ref, o_ref):
        j = pl.program_id(1)
        y = _conv_fwd(x_ref[...], w_ref[...], _seg_taps(T, nc, 4, 2))
        o_ref[...] = _dnprep_math(y, j < 2 * H, jnp.where(j < H, HD ** -0.5, 1.0))

    return pl.pallas_call(
        body, name="dnprep_fwd", grid=(B, 3 * H),
        in_specs=[pl.BlockSpec((T, HD), lambda b, j: (b, j)), pl.BlockSpec((4, HD), lambda b, j: (0, j))],
        out_specs=pl.BlockSpec((T, HD), lambda b, j: (b, j)),
        out_shape=jax.ShapeDtypeStruct((B * T, 3 * H * HD), F32),
        compiler_params=_params(("parallel", "parallel")),
    )(p, cw)


def _dnprep_bwd(p, cw, dqkv, dp, *, B, T, nc, H, HD):
    def body(x_ref, w_ref, d_ref, dp_any, dp_ref, dcw_ref):
        j = pl.program_id(0)
        taps = _seg_taps(T, nc, 4, 2)
        x = x_ref[...]
        y = _conv_fwd(x, w_ref[...], taps)
        is_qk, scale = j < 2 * H, jnp.where(j < H, HD ** -0.5, 1.0)
        _, vjp = jax.vjp(lambda a: _dnprep_math(a, is_qk, scale), y)
        (dy,) = vjp(d_ref[0])
        dx, dw = _conv_bwd(x, w_ref[...], taps, dy)
        dp_ref[...] = dx.astype(BF16)

        @pl.when(pl.program_id(1) == 0)
        def _():
            dcw_ref[...] = jnp.zeros_like(dcw_ref)

        dcw_ref[0:4, :] += dw

    col = pl.BlockSpec((T, HD), lambda j, b: (b, j))
    return pl.pallas_call(
        body, name="dnprep_bwd", grid=(3 * H, B),
        in_specs=[col, pl.BlockSpec((4, HD), lambda j, b: (0, j)),
                  pl.BlockSpec((1, T, HD), lambda j, b: (j // H, b, j % H)), pl.BlockSpec(memory_space=pl.ANY)],
        out_specs=[col, pl.BlockSpec((8, HD), lambda j, b: (0, j))],
        out_shape=[jax.ShapeDtypeStruct(dp.shape, dp.dtype), jax.ShapeDtypeStruct((8, 3 * H * HD), F32)],
        input_output_aliases={3: 0},
        compiler_params=_params(("parallel", "arbitrary")),
    )(p, cw, dqkv, dp)


def _gb_math(ab, alog, dtb, H):
    lane = lax.broadcasted_iota(jnp.int32, ab.shape, 1)
    g = -jnp.exp(alog) * _softplus(ab + dtb)
    return jnp.where(lane < 2 * H, g, jnp.where(lane < 4 * H, _sigmoid(ab), 0.0))


def _gb_fwd(p, prm, *, rows, col0, H, tm):
    def body(x_ref, prm_ref, o_ref):
        o_ref[...] = _gb_math(x_ref[...], prm_ref[0:1], prm_ref[1:2], H)

    return pl.pallas_call(
        body, name="gb_fwd", grid=(rows // tm,),
        in_specs=[pl.BlockSpec((tm, LANES), lambda t: (t, col0 // LANES)), pl.BlockSpec((8, LANES), lambda t: (0, 0))],
        out_specs=pl.BlockSpec((tm, LANES), lambda t: (t, 0)),
        out_shape=jax.ShapeDtypeStruct((rows, LANES), F32),
        compiler_params=_params(("parallel",)),
    )(p, prm)


def _gb_bwd(p, prm, dgb, dp, *, rows, col0, H, tm):
    def body(x_ref, prm_ref, d_ref, dp_any, dp_ref, dprm_ref):
        _, vjp = jax.vjp(lambda a, b, c: _gb_math(a, b, c, H), x_ref[...], prm_ref[0:1], prm_ref[1:2])
        dab, dalog, ddtb = vjp(d_ref[...])
        dp_ref[...] = dab.astype(BF16)

        @pl.when(pl.program_id(0) == 0)
        def _():
            dprm_ref[...] = jnp.zeros_like(dprm_ref)

        dprm_ref[0:1, :] += dalog
        dprm_ref[1:2, :] += ddtb

    blk = pl.BlockSpec((tm, LANES), lambda t: (t, col0 // LANES))
    return pl.pallas_call(
        body, name="gb_bwd", grid=(rows // tm,),
        in_specs=[blk, pl.BlockSpec((8, LANES), lambda t: (0, 0)), pl.BlockSpec((tm, LANES), lambda t: (t, 0)),
                  pl.BlockSpec(memory_space=pl.ANY)],
        out_specs=[blk, pl.BlockSpec((8, LANES), lambda t: (0, 0))],
        out_shape=[jax.ShapeDtypeStruct(dp.shape, dp.dtype), jax.ShapeDtypeStruct((8, LANES), F32)],
        input_output_aliases={3: 0},
        compiler_params=_params(("arbitrary",)),
    )(p, prm, dgb, dp)


def _lru_scans(scans):
    C = scans[0][0].shape[1]
    row = lax.broadcasted_iota(jnp.int32, (SUBLANES, C), 0)
    carries = tuple(jnp.zeros((1, C), F32) for _ in scans)
    for si in range(len(scans[0][4])):
        nb = scans[0][4][si][1] // SUBLANES
        assert all(sc[4][si][1] // SUBLANES == nb for sc in scans)

        def blk(i, carries, si=si, nb=nb):
            out = []
            for (a_ref, b_ref, h_ref, hp_ref, segs), carry in zip(scans, carries):
                start, _, reverse = segs[si]
                r0 = pl.multiple_of(start + (nb - 1 - i if reverse else i) * SUBLANES, SUBLANES)
                A = a_ref[pl.ds(r0, SUBLANES), :]
                Bv = b_ref[pl.ds(r0, SUBLANES), :]
                for s in (1, 2, 4):
                    sh = SUBLANES - s if reverse else s
                    m = (row < SUBLANES - s) if reverse else (row >= s)
                    Bv = jnp.where(m, A * pltpu.roll(Bv, sh, 0) + Bv, Bv)
                    A = jnp.where(m, A * pltpu.roll(A, sh, 0), A)
                Hv = Bv + A * carry
                h_ref[pl.ds(r0, SUBLANES), :] = Hv
                if hp_ref is not None:
                    if reverse:
                        hp = jnp.where(row < SUBLANES - 1, pltpu.roll(Hv, SUBLANES - 1, 0), carry)
                    else:
                        hp = jnp.where(row >= 1, pltpu.roll(Hv, 1, 0), carry)
                    hp_ref[pl.ds(r0, SUBLANES), :] = hp
                out.append(Hv[0:1] if reverse else Hv[SUBLANES - 1:SUBLANES])
            return tuple(out)

        carries = lax.fori_loop(0, nb, blk, carries)


def _lru_orders(T, nc, d):
    N = T - nc
    if d == 0:
        return [(0, nc, False), (nc, N, False)], [(nc, N, True), (0, nc, True)]
    return [(0, nc, True), (nc, N, True)], [(nc, N, False), (0, nc, False)]


def _bdot(a, b, dims=(((1,), (0,)), ((), ()))):
    return lax.dot_general(a.astype(BF16), b.astype(BF16), dims, preferred_element_type=F32)


_NT = (((1,), (1,)), ((), ()))
_TN = (((0,), (0,)), ((), ()))


def _blockdiag(w, C):
    nd, nb, bd, _ = w.shape
    per = C // bd
    out = jnp.einsum('dnpij,pq->dnpiqj', w.reshape(nd, nb // per, per, bd, bd), jnp.eye(per, dtype=w.dtype))
    return out.reshape(nd, nb // per, C, C)


def _blockdiag_extract(dw, bd):
    nd, nj, C, _ = dw.shape
    per = C // bd
    out = jnp.einsum('dnpiqj,pq->dnpij', dw.reshape(nd, nj, per, bd, per, bd), jnp.eye(per, dtype=dw.dtype))
    return out.reshape(nd, nj * per, bd, bd)


def _lru_fwd(p, cw, lv, wr, wi, *, B, T, nc, LW, col0, C):
    N = T - nc
    nj = LW // C

    def body(x_ref, cw_ref, lv_ref, wr_ref, wi_ref, o_ref, a_s, b_s, h_s):
        lv_ = lv_ref[...]
        xc = _conv_fwd(x_ref[:, 0:C], cw_ref[...], _seg_taps(T, nc, 4, 2)) + lv_[0:1]
        for d in (0, 1):
            r = _sigmoid(_bdot(xc, wr_ref[d, 0]) + lv_[1 + d:2 + d])
            i = _sigmoid(_bdot(xc, wi_ref[d, 0]) + lv_[3 + d:4 + d])
            la = -LRU_C * r * _softplus(-lv_[5 + d:6 + d])
            a_s[d] = jnp.exp(la)
            b_s[d] = jnp.sqrt(1.0 - jnp.exp(2.0 * la)) * i * xc
        _lru_scans([(a_s.at[d], b_s.at[d], h_s.at[d], None, _lru_orders(T, nc, d)[0]) for d in (0, 1)])
        o_ref[...] = ((h_s[0, nc:, :] + h_s[1, nc:, :]) * _gelu(x_ref[nc:, C:2 * C])).astype(BF16)

    return pl.pallas_call(
        body, name="lru_fwd", grid=(B, nj),
        in_specs=[pl.BlockSpec((T, 2 * C), lambda b, j: (b, col0 // (2 * C) + j)),
                  pl.BlockSpec((4, C), lambda b, j: (0, j)), pl.BlockSpec((8, C), lambda b, j: (0, j)),
                  pl.BlockSpec((2, 1, C, C), lambda b, j: (0, j, 0, 0)), pl.BlockSpec((2, 1, C, C), lambda b, j: (0, j, 0, 0))],
        out_specs=pl.BlockSpec((N, C), lambda b, j: (b, j)),
        out_shape=jax.ShapeDtypeStruct((B * N, LW), BF16),
        scratch_shapes=[pltpu.VMEM((2, T, C), F32)] * 3,
        compiler_params=_params(("parallel", "parallel")),
    )(p, cw, lv, wr, wi)


def _lru_bwd(p, cw, lv, wr, wi, dy, dp, *, B, T, nc, LW, col0, C):
    N = T - nc
    nj = LW // C

    def body(x_ref, cw_ref, lv_ref, wr_ref, wi_ref, dy_ref, dp_any, dp_ref, dcw_ref, dlv_ref, dwr_ref, dwi_ref,
             a_s, b_s, h_s, hp_s, mu_s, mup_s, dh_s, dxc_s):
        taps = _seg_taps(T, nc, 4, 2)
        lv_ = lv_ref[...]
        xl = x_ref[:, 0:C]
        xc = _conv_fwd(xl, cw_ref[...], taps) + lv_[0:1]
        gel, gelu_vjp = jax.vjp(_gelu, x_ref[nc:, C:2 * C])
        dh_s[0:nc, :] = jnp.zeros((nc, C), F32)
        dh_s[nc:, :] = dy_ref[...] * gel
        dxc_s[...] = jnp.zeros_like(dxc_s)

        @pl.when(pl.program_id(1) == 0)
        def _():
            dcw_ref[...] = jnp.zeros_like(dcw_ref)
            dlv_ref[...] = jnp.zeros_like(dlv_ref)
            dwr_ref[...] = jnp.zeros_like(dwr_ref)
            dwi_ref[...] = jnp.zeros_like(dwi_ref)

        def gates(d):
            lam = lv_[5 + d:6 + d]
            r = _sigmoid(_bdot(xc, wr_ref[d, 0]) + lv_[1 + d:2 + d])
            i = _sigmoid(_bdot(xc, wi_ref[d, 0]) + lv_[3 + d:4 + d])
            sp = _softplus(-lam)
            la = -LRU_C * r * sp
            e2 = jnp.exp(2.0 * la)
            return lam, r, i, sp, la, e2, jnp.sqrt(1.0 - e2)

        for d in (0, 1):
            _, _, i, _, la, _, mult = gates(d)
            a_s[d] = jnp.exp(la)
            b_s[d] = mult * i * xc
        _lru_scans([(a_s.at[d], b_s.at[d], h_s.at[d], hp_s.at[d], _lru_orders(T, nc, d)[0]) for d in (0, 1)])
        for d in (0, 1):
            b_s[d] = a_s[d] * dh_s[...]
        _lru_scans([(a_s.at[d], b_s.at[d], mu_s.at[d], mup_s.at[d], _lru_orders(T, nc, d)[1]) for d in (0, 1)])

        for d in (0, 1):
            lam, r, i, sp, la, e2, mult = gates(d)
            a = a_s[d]
            dinp = dh_s[...] + mup_s[d]
            da = dinp * hp_s[d]
            dmult = dinp * i * xc
            di = dinp * mult * xc
            dla = da * a - dmult * e2 / mult
            dpre_r = (dla * (-LRU_C * sp)) * r * (1.0 - r)
            dpre_i = di * i * (1.0 - i)
            dsp = jnp.sum(dla * (-LRU_C * r), axis=0, keepdims=True)
            dxc_s[...] += dinp * mult * i + _bdot(dpre_r, wr_ref[d, 0], _NT) + _bdot(dpre_i, wi_ref[d, 0], _NT)
            dwr_ref[d, 0] += _bdot(xc, dpre_r, _TN)
            dwi_ref[d, 0] += _bdot(xc, dpre_i, _TN)
            dlv_ref[1 + d:2 + d, :] += jnp.sum(dpre_r, axis=0, keepdims=True)
            dlv_ref[3 + d:4 + d, :] += jnp.sum(dpre_i, axis=0, keepdims=True)
            dlv_ref[5 + d:6 + d, :] += -dsp * _sigmoid(-lam)

        dxc = dxc_s[...]
        dxl, dw = _conv_bwd(xl, cw_ref[...], taps, dxc)
        dcw_ref[0:4, :] += dw
        dlv_ref[0:1, :] += jnp.sum(dxc, axis=0, keepdims=True)
        dp_ref[:, 0:C] = dxl.astype(BF16)
        (dyl,) = gelu_vjp(dy_ref[...] * (h_s[0, nc:, :] + h_s[1, nc:, :]))
        dp_ref[0:nc, C:2 * C] = jnp.zeros((nc, C), BF16)
        dp_ref[nc:, C:2 * C] = dyl.astype(BF16)

    xblk = pl.BlockSpec((T, 2 * C), lambda j, b: (b, col0 // (2 * C) + j))
    wblk = pl.BlockSpec((2, 1, C, C), lambda j, b: (0, j, 0, 0))
    vblk = pl.BlockSpec((8, C), lambda j, b: (0, j))
    return pl.pallas_call(
        body, name="lru_bwd", grid=(nj, B),
        in_specs=[xblk, pl.BlockSpec((4, C), lambda j, b: (0, j)), vblk, wblk, wblk,
                  pl.BlockSpec((N, C), lambda j, b: (b, j)), pl.BlockSpec(memory_space=pl.ANY)],
        out_specs=[xblk, vblk, vblk, wblk, wblk],
        out_shape=[jax.ShapeDtypeStruct(dp.shape, dp.dtype), jax.ShapeDtypeStruct((8, LW), F32),
                   jax.ShapeDtypeStruct((8, LW), F32), jax.ShapeDtypeStruct((2, nj, C, C), F32),
                   jax.ShapeDtypeStruct((2, nj, C, C), F32)],
        scratch_shapes=[pltpu.VMEM((2, T, C), F32)] * 6 + [pltpu.VMEM((T, C), F32)] * 2,
        input_output_aliases={6: 0},
        compiler_params=_params(("parallel", "arbitrary")),
    )(p, cw, lv, wr, wi, dy, dp)


def _chunk_masks(upper):
    i = lax.broadcasted_iota(jnp.int32, (CHUNK, CHUNK), 0)
    j = lax.broadcasted_iota(jnp.int32, (CHUNK, CHUNK), 1)
    ahead = jnp.where(upper, j - i, i - j)
    return i == j, ahead >= 0, ahead > 0


def _col2row(c, eye):
    return jnp.sum(jnp.where(eye, c, 0.0), axis=0, keepdims=True)


def _row2col(r, eye):
    return jnp.sum(jnp.where(eye, r, 0.0), axis=1, keepdims=True)


def _rowsum(x):
    return jnp.sum(x, axis=1, keepdims=True)


_INV_BASE = 8


def _unit_tri_inverses(Ls):
    G = len(Ls)
    W = G * CHUNK
    blk = (lax.broadcasted_iota(jnp.int32, (W, W), 0) // CHUNK) == (lax.broadcasted_iota(jnp.int32, (W, W), 1) // CHUNK)
    ri = lax.broadcasted_iota(jnp.int32, (CHUNK, W), 0)
    ci = lax.broadcasted_iota(jnp.int32, (CHUNK, W), 1) % CHUNK

    def bd(b):
        return jnp.where(blk, jnp.tile(b, (G, 1)), jnp.zeros((), BF16))

    def pdot(a, b):
        return jnp.dot(a.astype(BF16), bd(b.astype(BF16)), preferred_element_type=F32)

    Lc = Ls[0] if G == 1 else jnp.concatenate(Ls, axis=1)
    s = _INV_BASE
    Xp = -jnp.where(ri // s == ci // s, Lc, 0.0)
    Rm = Xp
    for _ in range(int(math.log2(s)) - 1):
        Xp = pdot(Xp, Xp)
        Rm = Rm + Xp + pdot(Rm, Xp)
    while s < CHUNK:
        E = jnp.where((ri // (2 * s) == ci // (2 * s)) & (ri // s != ci // s), Lc, 0.0)
        DE = E + pdot(Rm, E)
        Rm = Rm - (DE + pdot(DE, Rm))
        s *= 2
    eye = _chunk_masks(False)[0]
    return [jnp.where(eye, 1.0, 0.0) + Rm[:, g * CHUNK:(g + 1) * CHUNK] for g in range(G)]


def _delta_chunk_common(q, k, v, gcol, bcol, upper):
    eye, incl, strict = _chunk_masks(upper)
    gc = _rowsum(jnp.where(incl, _col2row(gcol, eye), 0.0))
    D = jnp.where(incl, jnp.exp(jnp.minimum(gc - _col2row(gc, eye), 0.0)), 0.0)
    kb = k * bcol
    AP = _bdot(jnp.concatenate([kb, q], axis=0), k, _NT)
    A = AP[:CHUNK]
    L = jnp.where(strict, A * D, 0.0)
    eg = jnp.exp(gc)
    gl = jnp.sum(gcol, axis=0, keepdims=True)
    attn = jnp.where(incl, AP[CHUNK:] * D, 0.0)
    return dict(eye=eye, incl=incl, strict=strict, gc=gc, D=D, kb=kb, A=A, L=L, eg=eg, gl=gl, egl=jnp.exp(gl),
                attn=attn, kbe=kb * eg, vb=v * bcol, qe=q * eg, kd=k * jnp.exp(gl - gc))


def _delta_group_pre(chunks, upper):
    cs = [_delta_chunk_common(*ch, upper) for ch in chunks]
    out = []
    for c, Tm in zip(cs, _unit_tri_inverses([c["L"] for c in cs])):
        dk = c["kbe"].shape[1]
        wu = _bdot(Tm, jnp.concatenate([c["kbe"], c["vb"]], axis=1))
        KN = _bdot(c["kd"], wu, _TN)
        QO = _bdot(c["attn"], wu)
        out.append((Tm, KN[:, :dk], KN[:, dk:], c["qe"] - QO[:, :dk], QO[:, dk:], c["egl"]))
    return out


def _delta_chunk_bwd(q, k, v, gcol, bcol, S, Tm, do, dS2, upper):
    c = _delta_chunk_common(q, k, v, gcol, bcol, upper)
    eye, incl, strict, D, eg, egl = c["eye"], c["incl"], c["strict"], c["D"], c["eg"], c["egl"]
    kb, kbe, vb, qe, kd, attn = c["kb"], c["kbe"], c["vb"], c["qe"], c["kd"], c["attn"]
    dkk = kbe.shape[1]
    wu = _bdot(Tm, jnp.concatenate([kbe, vb], axis=1))
    w = wu[:, :dkk]
    vn = wu[:, dkk:] - _bdot(w, S)
    dvn = _bdot(kd, dS2) + _bdot(attn, do, _TN)
    dkd = _bdot(vn, dS2, _NT)
    dgl = jnp.sum(_rowsum(dS2 * S), axis=0, keepdims=True) * egl
    dqa = _bdot(do, jnp.concatenate([S, vn], axis=0), _NT)
    dqe = dqa[:, :dkk]
    dattn = jnp.where(incl, dqa[:, dkk:], 0.0)
    dw = -_bdot(dvn, S, _NT)
    r = _rowsum(dkd * kd)
    dk = dkd * jnp.exp(c["gl"] - c["gc"])
    dgl = dgl + jnp.sum(r, axis=0, keepdims=True)
    dgc = _rowsum(dqe * qe) - r
    E = dattn * attn
    dvw = jnp.concatenate([dvn, dw], axis=1)
    dTm = _bdot(dvw, jnp.concatenate([vb, kbe], axis=1), _NT)
    dvk = _bdot(Tm, dvw, _TN)
    dvb = dvk[:, :dvn.shape[1]]
    dv = dvb * bcol
    dbeta = _rowsum(dvb * v)
    dkbe = dvk[:, dvn.shape[1]:]
    dkb = dkbe * eg
    dgc = dgc + _rowsum(dkbe * kbe)
    dL = jnp.where(strict, -_bdot(Tm, _bdot(dTm, Tm, _NT), _TN), 0.0)
    dA = dL * D
    E = E + dL * c["L"]
    PA = jnp.concatenate([dattn * D, dA], axis=0)
    PAk = _bdot(PA, k)
    dq = dqe * eg + PAk[:CHUNK]
    dkb = dkb + PAk[CHUNK:]
    dk = dk + _bdot(PA, jnp.concatenate([q, kb], axis=0), _TN) + dkb * bcol
    dbeta = dbeta + _rowsum(dkb * k)
    dgc = dgc + _rowsum(E) - _row2col(jnp.sum(E, axis=0, keepdims=True), eye)
    dg = _row2col(jnp.sum(jnp.where(incl, dgc, 0.0), axis=0, keepdims=True), eye) + dgl
    return dq, dk, dv, dg, dbeta


def _delta_group(n):
    return max(g for g in range(1, 2 * LANES // CHUNK + 1) if n % g == 0)


def _delta_chunk_at(T, nc, d, i):
    n, ncc = T // CHUNK, nc // CHUNK
    desc = jnp.where(i < ncc, ncc - 1 - i, n - 1 - (i - ncc))
    if isinstance(d, int):
        return i if d == 0 else desc
    return jnp.where(d == 0, i, desc)


def _dn_out_math(o, onorm, z):
    return _rmsn(o, onorm) * _silu(z)


def _delta_fwd(qkv, gb, p, onorm, *, B, T, nc, H, HD):
    N = T - nc
    n = T // CHUNK
    G = _delta_group(n)

    def body(q_ref, k_ref, v_ref, gb_ref, z_ref, on_ref, y_ref, o_ref, Tm_ref, K_ref, S_ref, Qp_ref, eg_ref,
             N_s, O0_s, o_s):
        h = pl.program_id(1)
        lane = lax.broadcasted_iota(jnp.int32, (CHUNK, LANES), 1)

        def pre(g, carry):
            cs = [g * G + i for i in range(G)]
            rows = [pl.ds(pl.multiple_of(c * CHUNK, CHUNK), CHUNK) for c in cs]
            for d in (0, 1):
                chunks = []
                for r in rows:
                    gbb = gb_ref[r, :]
                    chunks.append((q_ref[r, :], k_ref[r, :], v_ref[r, :],
                                   _rowsum(jnp.where(lane == d * H + h, gbb, 0.0)),
                                   _rowsum(jnp.where(lane == 2 * H + d * H + h, gbb, 0.0))))
                for c, r, (Tm, K, Nn, Qp, O0, egl) in zip(cs, rows, _delta_group_pre(chunks, d == 1)):
                    Tm_ref[0, d * n + c] = Tm
                    K_ref[0, d * n + c] = K.astype(BF16)
                    N_s[d * n + c] = Nn
                    Qp_ref[0, d, r, :] = Qp.astype(BF16)
                    O0_s[d, r, :] = O0
                    eg_ref[0, d * n + c] = jnp.broadcast_to(egl, (SUBLANES, HD))
            return carry

        lax.fori_loop(0, n // G, pre, 0)

        def step(i, Ss):
            out = []
            for d in (0, 1):
                c = _delta_chunk_at(T, nc, d, i)
                rows = pl.ds(pl.multiple_of(c * CHUNK, CHUNK), CHUNK)
                S_ref[0, d * n + c] = Ss[d]
                Sb = Ss[d].astype(BF16)
                o_s[d, rows, :] = jnp.dot(Qp_ref[0, d, rows, :], Sb, preferred_element_type=F32) + O0_s[d, rows, :]
                out.append(eg_ref[0, d * n + c][0:1] * Ss[d] + N_s[d * n + c]
                           - jnp.dot(K_ref[0, d * n + c], Sb, preferred_element_type=F32))
            return tuple(out)

        lax.fori_loop(0, n, step, (jnp.zeros((HD, HD), F32), jnp.zeros((HD, HD), F32)))
        o = o_s[0, nc:, :] + o_s[1, nc:, :]
        o_ref[...] = o
        y_ref[...] = _dn_out_math(o, on_ref[...], z_ref[nc:, :]).astype(BF16)

    col = lambda off: pl.BlockSpec((T, HD), lambda b, h: (b, off + h))
    lat = pl.BlockSpec((N, HD), lambda b, h: (b, h))
    per = lambda *blk: pl.BlockSpec((1, *blk), lambda b, h: (b * H + h, 0, 0, 0))
    return pl.pallas_call(
        body, name="delta_fwd", grid=(B, H),
        in_specs=[col(0), col(H), col(2 * H), pl.BlockSpec((T, LANES), lambda b, h: (b, 0)), col(3 * H),
                  pl.BlockSpec((1, HD), lambda b, h: (0, 0))],
        out_specs=[lat, lat, per(2 * n, CHUNK, CHUNK), per(2 * n, HD, HD), per(2 * n, HD, HD), per(2, T, HD),
                   per(2 * n, SUBLANES, HD)],
        out_shape=[jax.ShapeDtypeStruct((B * N, H * HD), BF16), jax.ShapeDtypeStruct((B * N, H * HD), F32),
                   jax.ShapeDtypeStruct((B * H, 2 * n, CHUNK, CHUNK), F32),
                   jax.ShapeDtypeStruct((B * H, 2 * n, HD, HD), BF16), jax.ShapeDtypeStruct((B * H, 2 * n, HD, HD), F32),
                   jax.ShapeDtypeStruct((B * H, 2, T, HD), BF16), jax.ShapeDtypeStruct((B * H, 2 * n, SUBLANES, HD), F32)],
        scratch_shapes=[pltpu.VMEM((2 * n, HD, HD), F32), pltpu.VMEM((2, T, HD), F32), pltpu.VMEM((2, T, HD), F32)],
        compiler_params=_params(("parallel", "parallel")),
    )(qkv, qkv, qkv, gb, p, onorm)


def _delta_bwd(qkv, gb, p, onorm, o, res, dy, dp, *, B, T, nc, H, HD):
    N = T - nc
    n = T // CHUNK

    def body(q_ref, k_ref, v_ref, gb_ref, z_ref, on_ref, o_ref, dy_ref, Tm_ref, K_ref, S_ref, Qp_ref, eg_ref, dp_any,
             dqkv_ref, dgb_ref, dp_ref, don_ref, do_s, R_s, dS_s):
        h, d = pl.program_id(1), pl.program_id(2)
        lane = lax.broadcasted_iota(jnp.int32, (CHUNK, LANES), 1)

        @pl.when(d == 0)
        def _():
            _, vjp = jax.vjp(_dn_out_math, o_ref[...], on_ref[...], z_ref[nc:, :])
            do, don, dz = vjp(dy_ref[...])
            do_s[0:nc, :] = jnp.zeros((nc, HD), F32)
            do_s[nc:, :] = do
            dp_ref[0:nc, :] = jnp.zeros((nc, HD), BF16)
            dp_ref[nc:, :] = dz.astype(BF16)
            dqkv_ref[...] = jnp.zeros_like(dqkv_ref)

            @pl.when(h == 0)
            def _():
                don_ref[...] = jnp.zeros_like(don_ref)
                dgb_ref[...] = jnp.zeros_like(dgb_ref)

            don_ref[0, 0:1, :] += don

        def r_of(c, carry):
            rows = pl.ds(pl.multiple_of(c * CHUNK, CHUNK), CHUNK)
            R_s[c] = lax.dot_general(Qp_ref[0, 0, rows, :], do_s[rows, :].astype(BF16), _TN, preferred_element_type=F32)
            return carry

        lax.fori_loop(0, n, r_of, 0)

        def bwd_step(i, dS):
            c = _delta_chunk_at(T, nc, d, n - 1 - i)
            dS_s[c] = dS
            return (eg_ref[0, c][0:1] * dS + R_s[c]
                    - lax.dot_general(K_ref[0, c], dS.astype(BF16), _TN, preferred_element_type=F32))

        lax.fori_loop(0, n, bwd_step, jnp.zeros((HD, HD), F32))

        def grads(c, carry):
            rows = pl.ds(pl.multiple_of(c * CHUNK, CHUNK), CHUNK)
            gbb = gb_ref[rows, :]
            gcol = _rowsum(jnp.where(lane == d * H + h, gbb, 0.0))
            bcol = _rowsum(jnp.where(lane == 2 * H + d * H + h, gbb, 0.0))
            dq, dk, dv, dg, dbeta = _delta_chunk_bwd(q_ref[rows, :], k_ref[rows, :], v_ref[rows, :], gcol, bcol,
                                                     S_ref[0, c], Tm_ref[0, c], do_s[rows, :], dS_s[c], d == 1)
            dqkv_ref[0, rows, :] += dq
            dqkv_ref[1, rows, :] += dk
            dqkv_ref[2, rows, :] += dv
            dgb_ref[rows, :] += (jnp.where(lane == d * H + h, dg, 0.0)
                                 + jnp.where(lane == 2 * H + d * H + h, dbeta, 0.0))
            return carry

        lax.fori_loop(0, n, grads, 0)

    col = lambda off: pl.BlockSpec((T, HD), lambda b, h, d: (b, off + h))
    lat = pl.BlockSpec((N, HD), lambda b, h, d: (b, h))
    per = lambda *blk: pl.BlockSpec((1, *blk), lambda b, h, d: (b * H + h, d, 0, 0))
    return pl.pallas_call(
        body, name="delta_bwd", grid=(B, H, 2),
        in_specs=[col(0), col(H), col(2 * H), pl.BlockSpec((T, LANES), lambda b, h, d: (b, 0)), col(3 * H),
                  pl.BlockSpec((1, HD), lambda b, h, d: (0, 0)), lat, lat,
                  per(n, CHUNK, CHUNK), per(n, HD, HD), per(n, HD, HD), per(1, T, HD), per(n, SUBLANES, HD),
                  pl.BlockSpec(memory_space=pl.ANY)],
        out_specs=[pl.BlockSpec((3, T, HD), lambda b, h, d: (0, b, h)), pl.BlockSpec((T, LANES), lambda b, h, d: (b, 0)),
                   col(3 * H), pl.BlockSpec((1, 8, HD), lambda b, h, d: (b, 0, 0))],
        out_shape=[jax.ShapeDtypeStruct((3, B * T, H * HD), F32), jax.ShapeDtypeStruct((B * T, LANES), F32),
                   jax.ShapeDtypeStruct(dp.shape, dp.dtype), jax.ShapeDtypeStruct((B, 8, HD), F32)],
        scratch_shapes=[pltpu.VMEM((T, HD), F32), pltpu.VMEM((n, HD, HD), F32), pltpu.VMEM((n, HD, HD), F32)],
        input_output_aliases={13: 2},
        compiler_params=_params(("parallel", "arbitrary", "arbitrary")),
    )(qkv, qkv, qkv, gb, p, onorm, o, dy, *res, dp)


def _rowwise(fn, ins, out_dtypes, *, name, tm=256, mult=16):
    R, W = ins[0].shape
    tm = _tile(R, tm, mult)

    def body(*refs):
        outs = fn(*[r[...] for r in refs[:len(ins)]])
        for o_ref, o in zip(refs[len(ins):], outs):
            o_ref[...] = o.astype(o_ref.dtype)

    spec = pl.BlockSpec((tm, W), lambda i: (i, 0))
    return pl.pallas_call(
        body, name=name, grid=(R // tm,), in_specs=[spec] * len(ins), out_specs=[spec] * len(out_dtypes),
        out_shape=[jax.ShapeDtypeStruct((R, W), dt) for dt in out_dtypes],
        compiler_params=_params(("parallel",)),
    )(*ins)


def _sum_lead(x, *, name, tm=256, mult=16):
    S, R, W = x.shape
    tm = _tile(R, tm, mult)

    def body(*refs):
        acc = refs[0][0].astype(F32)
        for r in refs[1:S]:
            acc = acc + r[0].astype(F32)
        refs[S][...] = acc

    return pl.pallas_call(
        body, name=name, grid=(R // tm,),
        in_specs=[pl.BlockSpec((1, tm, W), functools.partial(lambda s, i: (s, i, 0), s)) for s in range(S)],
        out_specs=pl.BlockSpec((tm, W), lambda i: (i, 0)),
        out_shape=jax.ShapeDtypeStruct((R, W), F32),
        compiler_params=_params(("parallel",)),
    )(*([x] * S))


def _adamw_math(w, g, m, v):
    m = ADAM_B1 * m + (1.0 - ADAM_B1) * g
    v = ADAM_B2 * v + (1.0 - ADAM_B2) * (g * g)
    m_hat = m / (1.0 - ADAM_B1 ** ADAM_STEP)
    v_hat = v / (1.0 - ADAM_B2 ** ADAM_STEP)
    return -ADAM_LR * (m_hat / (jnp.sqrt(v_hat) + ADAM_EPS) + ADAM_WD * w), m, v


def _adamw(w, g, m, v, *, name):
    tm = max(SUBLANES, (256 * 1024) // w.shape[1] // SUBLANES * SUBLANES)
    return _rowwise(_adamw_math, [w, g, m, v], [F32, F32, F32], name=name, tm=tm, mult=SUBLANES)


def _me():
    return lax.axis_index("x"), lax.axis_index("y"), lax.axis_index("c")


def _allgather_small(v):
    R, W = v.shape

    def body(x_ref, out_ref, send_sems, recv_sems, local_sem):
        x, y, c = _me()
        me, sibling = (x, y, c), (x, y, 1 - c)
        chips = [(1 - x, y), (x, 1 - y), (1 - x, 1 - y)]

        def slot(px, py, pc):
            return out_ref.at[4 * px + 2 * py + pc]

        def copy(k, block, to, src=None):
            return pltpu.make_async_remote_copy(
                src_ref=slot(*block) if src is None else src, dst_ref=slot(*block),
                send_sem=send_sems.at[k], recv_sem=recv_sems.at[k], device_id=to, device_id_type=MESH)

        mine = pltpu.make_async_copy(x_ref, slot(*me), local_sem)
        mine.start()
        first = [copy(0, me, sibling, src=x_ref)]
        first += [copy(1 + j, me, (*chip, c), src=x_ref) for j, chip in enumerate(chips)]
        for cp in first:
            cp.start()
        passed = [copy(4 + j, (*chip, c), sibling) for j, chip in enumerate(chips)]
        for j, chip in enumerate(chips):
            copy(1 + j, (*chip, c), me).wait_recv()
            passed[j].start()
        copy(0, sibling, me).wait_recv()
        for j, chip in enumerate(chips):
            copy(4 + j, (*chip, 1 - c), me).wait_recv()
        for cp in first + passed:
            cp.wait_send()
        mine.wait()

    return pl.pallas_call(
        body, name="allgather_small", out_shape=jax.ShapeDtypeStruct((8, R, W), v.dtype),
        in_specs=[pl.BlockSpec(memory_space=pltpu.VMEM)], out_specs=pl.BlockSpec(memory_space=pltpu.VMEM),
        scratch_shapes=[pltpu.SemaphoreType.DMA((7,)), pltpu.SemaphoreType.DMA((7,)), pltpu.SemaphoreType.DMA],
        compiler_params=_params(),
    )(v)


_ANY = pl.BlockSpec(memory_space=pl.ANY)


def _allgather_halves(shards, *, name):
    nw = len(shards)

    def body(*refs):
        x_refs, out_refs = refs[:nw], refs[nw:2 * nw]
        send_sems, recv_sems, local_sems = refs[2 * nw:]
        x, y, c = _me()
        me, sibling = (x, y, c), (x, y, 1 - c)
        chips = [(1 - x, y), (x, 1 - y), (1 - x, 1 - y)]

        def slot(w, px, py, pc):
            return out_refs[w].at[4 * px + 2 * py + pc]

        def copy(w, k, block, to, src=None):
            return pltpu.make_async_remote_copy(
                src_ref=slot(w, *block) if src is None else src, dst_ref=slot(w, *block),
                send_sem=send_sems.at[w, k], recv_sem=recv_sems.at[w, k], device_id=to, device_id_type=MESH)

        started, local = [], []
        for w in range(nw):
            half = shards[w].shape[0] // 2
            own = x_refs[w].at[pl.ds(c * half, half), :]
            mine = pltpu.make_async_copy(own, slot(w, *me), local_sems.at[w])
            mine.start()
            first = [copy(w, 0, me, sibling, src=own)]
            first += [copy(w, 1 + j, me, (*chip, c), src=own) for j, chip in enumerate(chips)]
            for cp in first:
                cp.start()
            started += first
            local.append(mine)
        for w in range(nw):
            for j, chip in enumerate(chips):
                copy(w, 1 + j, (*chip, c), me).wait_recv()
                fwd = copy(w, 4 + j, (*chip, c), sibling)
                fwd.start()
                started.append(fwd)
        for w in range(nw):
            copy(w, 0, sibling, me).wait_recv()
            for j, chip in enumerate(chips):
                copy(w, 4 + j, (*chip, 1 - c), me).wait_recv()
        for cp in started:
            cp.wait_send()
        for cp in local:
            cp.wait()

    return pl.pallas_call(
        body, name=name,
        out_shape=[jax.ShapeDtypeStruct((8, s.shape[0] // 2, s.shape[1]), s.dtype) for s in shards],
        in_specs=[_ANY] * nw, out_specs=[_ANY] * nw,
        scratch_shapes=[pltpu.SemaphoreType.DMA((nw, 7)), pltpu.SemaphoreType.DMA((nw, 7)), pltpu.SemaphoreType.DMA((nw,))],
        compiler_params=_params(),
    )(*shards)


def _sibling_send_halves(arrs, *, name):
    nw = len(arrs)

    def body(*refs):
        x_refs, out_refs, send_sems, recv_sems = refs[:nw], refs[nw:2 * nw], refs[2 * nw], refs[2 * nw + 1]
        x, y, c = _me()
        cps = []
        for w in range(nw):
            half = arrs[w].shape[1] // 2
            cp = pltpu.make_async_remote_copy(
                src_ref=x_refs[w].at[:, pl.ds((1 - c) * half, half), :], dst_ref=out_refs[w],
                send_sem=send_sems.at[w], recv_sem=recv_sems.at[w], device_id=(x, y, 1 - c), device_id_type=MESH)
            cp.start()
            cps.append(cp)
        for cp in cps:
            cp.wait()

    return pl.pallas_call(
        body, name=name,
        out_shape=[jax.ShapeDtypeStruct((a.shape[0], a.shape[1] // 2, a.shape[2]), a.dtype) for a in arrs],
        in_specs=[_ANY] * nw, out_specs=[_ANY] * nw,
        scratch_shapes=[pltpu.SemaphoreType.DMA((nw,)), pltpu.SemaphoreType.DMA((nw,))],
        compiler_params=_params(),
    )(*arrs)


def _sibling_swap(arrs, *, name):
    nw = len(arrs)

    def body(*refs):
        x_refs, out_refs, send_sems, recv_sems = refs[:nw], refs[nw:2 * nw], refs[2 * nw], refs[2 * nw + 1]
        x, y, c = _me()
        cps = []
        for w in range(nw):
            cp = pltpu.make_async_remote_copy(
                src_ref=x_refs[w], dst_ref=out_refs[w], send_sem=send_sems.at[w], recv_sem=recv_sems.at[w],
                device_id=(x, y, 1 - c), device_id_type=MESH)
            cp.start()
            cps.append(cp)
        for cp in cps:
            cp.wait()

    return pl.pallas_call(
        body, name=name, out_shape=[jax.ShapeDtypeStruct(a.shape, a.dtype) for a in arrs],
        in_specs=[_ANY] * nw, out_specs=[_ANY] * nw,
        scratch_shapes=[pltpu.SemaphoreType.DMA((nw,)), pltpu.SemaphoreType.DMA((nw,))],
        compiler_params=_params(),
    )(*arrs)


def _adamw_halves(w, own, sib, m, v, c_arr, *, name):
    r, cols = w.shape
    h = r // 2
    tm = _tile(h, max(SUBLANES, (192 * 1024) // cols // SUBLANES * SUBLANES), SUBLANES)
    nb = h // tm

    def body(c_ref, w_ref, own_ref, sib_ref, m_ref, v_ref, g_out, d_out, m_out, v_out):
        g = jnp.where(pl.program_id(0) == c_ref[0], own_ref[...], sib_ref[...])
        g_out[...] = g
        d_out[...], m_out[...], v_out[...] = _adamw_math(w_ref[...], g, m_ref[...], v_ref[...])

    full = pl.BlockSpec((tm, cols), lambda hh, i, c_ref: (hh * nb + i, 0))
    half = pl.BlockSpec((tm, cols), lambda hh, i, c_ref: (i, 0))
    return pl.pallas_call(
        body, name=name,
        grid_spec=pltpu.PrefetchScalarGridSpec(num_scalar_prefetch=1, grid=(2, nb),
                                               in_specs=[full, half, half, full, full], out_specs=[full] * 4),
        out_shape=[jax.ShapeDtypeStruct((r, cols), F32)] * 4,
        compiler_params=_params(("parallel", "parallel")),
    )(c_arr, w, own, sib, m, v)


def _chip_exchange(arrs, *, name):
    nw = len(arrs)

    def body(*refs):
        x_refs, out_refs = refs[:nw], refs[nw:2 * nw]
        send_sems, recv_sems = refs[2 * nw:]
        x, y, c = _me()
        s_me = 2 * x + y
        chips = [(1 - x, y), (x, 1 - y), (1 - x, 1 - y)]
        started = []
        for w in range(nw):
            for k, (px, py) in enumerate(chips):
                cp = pltpu.make_async_remote_copy(
                    src_ref=x_refs[w].at[2 * px + py], dst_ref=out_refs[w].at[s_me], send_sem=send_sems.at[w, k],
                    recv_sem=recv_sems.at[w, k], device_id=(px, py, c), device_id_type=MESH)
                cp.start()
                started.append(cp)
        for w in range(nw):
            for k, (px, py) in enumerate(chips):
                pltpu.make_async_remote_copy(
                    src_ref=x_refs[w].at[s_me], dst_ref=out_refs[w].at[2 * px + py], send_sem=send_sems.at[w, k],
                    recv_sem=recv_sems.at[w, k], device_id=(px, py, c), device_id_type=MESH).wait_recv()
        for cp in started:
            cp.wait_send()

    return pl.pallas_call(
        body, name=name, out_shape=[jax.ShapeDtypeStruct(a.shape, a.dtype) for a in arrs],
        in_specs=[_ANY] * nw, out_specs=[_ANY] * nw,
        scratch_shapes=[pltpu.SemaphoreType.DMA((nw, 3)), pltpu.SemaphoreType.DMA((nw, 3))],
        compiler_params=_params(),
    )(*arrs)


_HBM = pl.BlockSpec(memory_space=pltpu.HBM)
_SEM = pl.BlockSpec(memory_space=pltpu.SEMAPHORE)
_DATAFLOW = pltpu.SideEffectType.DATAFLOW_SIDE_EFFECTING


def _chip_exchange_start(arrs, *, name):
    nw = len(arrs)

    def body(*refs):
        x_refs, land_refs, send_sems, recv_sems = refs[:nw], refs[nw:2 * nw], refs[2 * nw], refs[2 * nw + 1]
        token = refs[-1]
        x, y, c = _me()
        s_me = 2 * x + y
        for w in range(nw):
            for k, (px, py) in enumerate([(1 - x, y), (x, 1 - y), (1 - x, 1 - y)]):
                pltpu.make_async_remote_copy(
                    src_ref=x_refs[w].at[2 * px + py], dst_ref=land_refs[w].at[s_me], send_sem=send_sems.at[3 * w + k],
                    recv_sem=recv_sems.at[3 * w + k], device_id=(px, py, c), device_id_type=MESH).start()
        token[...] = jnp.zeros_like(token)

    hbm = [pltpu.HBM(a.shape, a.dtype) for a in arrs]
    outs = pl.pallas_call(
        body, name=name,
        out_shape=(pltpu.SemaphoreType.DMA((3 * nw,)), pltpu.SemaphoreType.DMA((3 * nw,)), *hbm, *hbm,
                   jax.ShapeDtypeStruct((SUBLANES, LANES), F32)),
        in_specs=[_HBM] * (2 * nw), out_specs=(_SEM, _SEM, *([_HBM] * (2 * nw)), pl.BlockSpec(memory_space=pltpu.VMEM)),
        input_output_aliases={i: 2 + i for i in range(2 * nw)},
        compiler_params=pltpu.CompilerParams(has_side_effects=_DATAFLOW),
    )(*[pltpu.with_memory_space_constraint(a, pltpu.HBM) for a in arrs],
      *[pltpu.with_memory_space_constraint(lax.empty(a.shape, a.dtype), pltpu.HBM) for a in arrs])
    return outs[0], outs[1], list(outs[2:2 + nw]), list(outs[2 + nw:2 + 2 * nw]), outs[-1]


def _allgather_start(shards, *, name):
    nw = len(shards)

    def body(*refs):
        x_refs, land_refs, send_sems, recv_sems = refs[:nw], refs[nw:2 * nw], refs[2 * nw], refs[2 * nw + 1]
        token = refs[-1]
        x, y, c = _me()
        me = 4 * x + 2 * y + c
        for w in range(nw):
            half = shards[w].shape[0] // 2
            own = x_refs[w].at[pl.ds(c * half, half), :]
            for k, to in enumerate([(x, y, 1 - c), (1 - x, y, c), (x, 1 - y, c), (1 - x, 1 - y, c)]):
                pltpu.make_async_remote_copy(
                    src_ref=own, dst_ref=land_refs[w].at[me], send_sem=send_sems.at[4 * w + k],
                    recv_sem=recv_sems.at[4 * w + k], device_id=to, device_id_type=MESH).start()
        token[...] = jnp.zeros_like(token)

    lands = [pltpu.HBM((8, s.shape[0] // 2, s.shape[1]), s.dtype) for s in shards]
    outs = pl.pallas_call(
        body, name=name,
        out_shape=(pltpu.SemaphoreType.DMA((4 * nw,)), pltpu.SemaphoreType.DMA((4 * nw,)),
                   *[pltpu.HBM(s.shape, s.dtype) for s in shards], *lands, jax.ShapeDtypeStruct((SUBLANES, LANES), F32)),
        in_specs=[_HBM] * (2 * nw), out_specs=(_SEM, _SEM, *([_HBM] * (2 * nw)), pl.BlockSpec(memory_space=pltpu.VMEM)),
        input_output_aliases={i: 2 + i for i in range(2 * nw)},
        compiler_params=pltpu.CompilerParams(has_side_effects=_DATAFLOW),
    )(*[pltpu.with_memory_space_constraint(s, pltpu.HBM) for s in shards],
      *[pltpu.with_memory_space_constraint(lax.empty(l.shape, l.dtype), pltpu.HBM) for l in lands])
    return outs[0], outs[1], list(outs[2:2 + nw]), list(outs[2 + nw:2 + 2 * nw]), outs[-1]


def _allgather_wait(send_sems, recv_sems, srcs, lands, after, *, name):
    nw = len(srcs)

    def body(*refs):
        x_refs, land_refs, send_sems, recv_sems = refs[:nw], refs[nw:2 * nw], refs[2 * nw], refs[2 * nw + 1]
        x, y, c = _me()
        for w in range(nw):
            half = srcs[w].shape[0] // 2
            own = x_refs[w].at[pl.ds(c * half, half), :]
            for k, (px, py, pc) in enumerate([(x, y, 1 - c), (1 - x, y, c), (x, 1 - y, c), (1 - x, 1 - y, c)]):
                cp = pltpu.make_async_remote_copy(
                    src_ref=own, dst_ref=land_refs[w].at[4 * px + 2 * py + pc], send_sem=send_sems.at[4 * w + k],
                    recv_sem=recv_sems.at[4 * w + k], device_id=(px, py, pc), device_id_type=MESH)
                cp.wait_send()
                cp.wait_recv()

    outs = pl.pallas_call(
        body, name=name,
        out_shape=(*[pltpu.HBM(a.shape, a.dtype) for a in srcs], *[pltpu.HBM(a.shape, a.dtype) for a in lands]),
        in_specs=[_HBM] * (2 * nw) + [_SEM, _SEM, _ANY], out_specs=tuple([_HBM] * (2 * nw)),
        input_output_aliases={i: i for i in range(2 * nw)},
        compiler_params=pltpu.CompilerParams(has_side_effects=_DATAFLOW),
    )(*srcs, *lands, send_sems, recv_sems, after)
    return list(outs[:nw]), list(outs[nw:])


def _pass_to_sibling(lands, *, name):
    nw = len(lands)

    def body(*refs):
        x_refs, out_refs, send_sems, recv_sems = refs[:nw], refs[nw:2 * nw], refs[2 * nw], refs[2 * nw + 1]
        x, y, c = _me()
        chips = [(1 - x, y), (x, 1 - y), (1 - x, 1 - y)]
        cps = []
        for w in range(nw):
            for k, (px, py) in enumerate(chips):
                cp = pltpu.make_async_remote_copy(
                    src_ref=x_refs[w].at[4 * px + 2 * py + c], dst_ref=out_refs[w].at[4 * px + 2 * py + c],
                    send_sem=send_sems.at[3 * w + k], recv_sem=recv_sems.at[3 * w + k], device_id=(x, y, 1 - c),
                    device_id_type=MESH)
                cp.start()
                cps.append(cp)
        for w in range(nw):
            for k, (px, py) in enumerate(chips):
                pltpu.make_async_remote_copy(
                    src_ref=x_refs[w].at[4 * px + 2 * py + c], dst_ref=out_refs[w].at[4 * px + 2 * py + 1 - c],
                    send_sem=send_sems.at[3 * w + k], recv_sem=recv_sems.at[3 * w + k], device_id=(x, y, 1 - c),
                    device_id_type=MESH).wait_recv()
        for cp in cps:
            cp.wait_send()

    return pl.pallas_call(
        body, name=name, out_shape=[jax.ShapeDtypeStruct(a.shape, a.dtype) for a in lands],
        in_specs=[_ANY] * nw, out_specs=[_ANY] * nw, input_output_aliases={i: i for i in range(nw)},
        scratch_shapes=[pltpu.SemaphoreType.DMA((3 * nw,)), pltpu.SemaphoreType.DMA((3 * nw,))],
        compiler_params=_params(),
    )(*lands)


def _chip_exchange_wait(send_sems, recv_sems, srcs, lands, after, *, name):
    nw = len(srcs)

    def body(*refs):
        x_refs, land_refs, send_sems, recv_sems = refs[:nw], refs[nw:2 * nw], refs[2 * nw], refs[2 * nw + 1]
        x, y, c = _me()
        for w in range(nw):
            for k, (px, py) in enumerate([(1 - x, y), (x, 1 - y), (1 - x, 1 - y)]):
                cp = pltpu.make_async_remote_copy(
                    src_ref=x_refs[w].at[2 * px + py], dst_ref=land_refs[w].at[2 * px + py], send_sem=send_sems.at[3 * w + k],
                    recv_sem=recv_sems.at[3 * w + k], device_id=(px, py, c), device_id_type=MESH)
                cp.wait_send()
                cp.wait_recv()

    hbm = [pltpu.HBM(a.shape, a.dtype) for a in srcs]
    outs = pl.pallas_call(
        body, name=name, out_shape=(*hbm, *hbm),
        in_specs=[_HBM] * (2 * nw) + [_SEM, _SEM, _ANY], out_specs=tuple([_HBM] * (2 * nw)),
        input_output_aliases={i: i for i in range(2 * nw)},
        compiler_params=pltpu.CompilerParams(has_side_effects=_DATAFLOW),
    )(*srcs, *lands, send_sems, recv_sems, after)
    return list(outs[:nw]), list(outs[nw:])


def _sum_slabs(landed, own_src, s_arr, *, name, tm=512):
    S, h, w = landed.shape
    tm = _tile(h, tm, 16)

    def body(s_ref, *refs):
        own = refs[S][0].astype(F32)
        acc = None
        for s in range(S):
            term = jnp.where(s_ref[0] == s, own, refs[s][0].astype(F32))
            acc = term if acc is None else acc + term
        refs[S + 1][...] = acc

    def slab(s):
        return pl.BlockSpec((1, tm, w), lambda i, s_ref: (jnp.where(s_ref[0] == s, (s + 1) % S, s), i, 0))

    return pl.pallas_call(
        body, name=name,
        grid_spec=pltpu.PrefetchScalarGridSpec(
            num_scalar_prefetch=1, grid=(h // tm,),
            in_specs=[slab(s) for s in range(S)] + [pl.BlockSpec((1, tm, w), lambda i, s_ref: (s_ref[0], i, 0))],
            out_specs=pl.BlockSpec((tm, w), lambda i, s_ref: (i, 0))),
        out_shape=jax.ShapeDtypeStruct((h, w), F32),
        compiler_params=_params(("parallel",)),
    )(s_arr, *([landed] * S), own_src)


def _half_add(g, recv, c_arr, *, name):
    S, r, w = g.shape
    h = r // 2
    tm = _tile(h, 512, 16)
    nb = h // tm

    def body(c_ref, g_ref, r_ref, o_ref):
        o_ref[...] = (g_ref[...] + r_ref[...]).astype(BF16)

    return pl.pallas_call(
        body, name=name,
        grid_spec=pltpu.PrefetchScalarGridSpec(
            num_scalar_prefetch=1, grid=(S, nb),
            in_specs=[pl.BlockSpec((1, tm, w), lambda s, i, c_ref: (s, c_ref[0] * nb + i, 0)),
                      pl.BlockSpec((1, tm, w), lambda s, i, c_ref: (s, i, 0))],
            out_specs=pl.BlockSpec((1, tm, w), lambda s, i, c_ref: (s, i, 0))),
        out_shape=jax.ShapeDtypeStruct((S, h, w), BF16),
        compiler_params=_params(("parallel", "parallel")),
    )(c_arr, g, recv)


def _layout(sizes, width, part_mult, total_mult):
    offs, rows, r = [], [], 0
    for n in sizes:
        k = -(-n // width)
        offs.append(r)
        rows.append(k)
        r += -(-k // part_mult) * part_mult
    return offs, rows, -(-r // total_mult) * total_mult


def _pack(arrs, width, part_mult, total_mult, dtype, lead=()):
    nl = len(lead)
    sizes = [math.prod(a.shape[nl:]) for a in arrs]
    offs, rows, total = _layout(sizes, width, part_mult, total_mult)
    parts, r = [], 0
    for a, n, o, k in zip(arrs, sizes, offs, rows):
        kp = -(-k // part_mult) * part_mult
        flat = a.reshape(*lead, n).astype(dtype)
        if kp * width > n:
            flat = jnp.pad(flat, [(0, 0)] * nl + [(0, kp * width - n)])
        parts.append(flat.reshape(*lead, kp, width))
        r = o + kp
    if total > r:
        parts.append(jnp.zeros((*lead, total - r, width), dtype))
    return jnp.concatenate(parts, axis=nl)


def _unpack(pool, shapes, width, part_mult, total_mult):
    lead = pool.shape[:-2]
    sizes = [math.prod(s) for s in shapes]
    offs, rows, _ = _layout(sizes, width, part_mult, total_mult)
    out = []
    for s, n, o, k in zip(shapes, sizes, offs, rows):
        flat = lax.slice_in_dim(pool, o, o + k, axis=len(lead)).reshape(*lead, k * width)
        out.append(lax.slice_in_dim(flat, 0, n, axis=len(lead)).reshape(*lead, *s))
    return out


_WEIGHTS = ("c_ctx", "w_ada", "b_ada", "g_pre_mix", "g_post_mix", "g_pre_ffn", "g_post_ffn", "w_in", "b_merge",
            "dn_conv", "dn_a_log", "dn_dt_bias", "dn_onorm", "lru_conv", "lru_conv_b", "lru_w_rg", "lru_b_rg",
            "lru_w_ig", "lru_b_ig", "lru_lambda", "w_branch_dn", "w_branch_lru", "w_out", "w_up", "ffn_dw",
            "ffn_dw_b", "w_down")
_BIG = {"w_ada": True, "w_in": True, "w_branch_dn": False, "w_branch_lru": False, "w_out": False, "w_up": True,
        "w_down": False}
_SMALL_SHARDED = ("dn_conv", "lru_conv", "lru_b_rg", "lru_b_ig", "lru_lambda", "ffn_dw")
_NCHIP = 4
_FLAT_PART = 8
_FLAT_TOTAL = 256


def _to_chip_shards(g, by_cols):
    if by_cols:
        return g.reshape(g.shape[0], _NCHIP, g.shape[1] // _NCHIP).transpose(1, 0, 2)
    return g.reshape(_NCHIP, g.shape[0] // _NCHIP, g.shape[1])


def _from_chip_shards(s, by_cols):
    if by_cols:
        return s.transpose(1, 0, 2).reshape(s.shape[1], _NCHIP * s.shape[2])
    return s.reshape(_NCHIP * s.shape[1], s.shape[2])


def _dsilu(x):
    s = _sigmoid(x)
    return s * (1.0 + x * (1.0 - s))


def kernel(x, c, ctx, c_ctx, w_ada, b_ada, g_pre_mix, g_post_mix, g_pre_ffn, g_post_ffn, w_in, b_merge, dn_conv, dn_a_log, dn_dt_bias, dn_onorm, lru_conv, lru_conv_b, lru_w_rg, lru_b_rg, lru_w_ig, lru_b_ig, lru_lambda, w_branch_dn, w_branch_lru, w_out, w_up, ffn_dw, ffn_dw_b, w_down, loss_target, m_c_ctx, m_w_ada, m_b_ada, m_g_pre_mix, m_g_post_mix, m_g_pre_ffn, m_g_post_ffn, m_w_in, m_b_merge, m_dn_conv, m_dn_a_log, m_dn_dt_bias, m_dn_onorm, m_lru_conv, m_lru_conv_b, m_lru_w_rg, m_lru_b_rg, m_lru_w_ig, m_lru_b_ig, m_lru_lambda, m_w_branch_dn, m_w_branch_lru, m_w_out, m_w_up, m_ffn_dw, m_ffn_dw_b, m_w_down, v_c_ctx, v_w_ada, v_b_ada, v_g_pre_mix, v_g_post_mix, v_g_pre_ffn, v_g_post_ffn, v_w_in, v_b_merge, v_dn_conv, v_dn_a_log, v_dn_dt_bias, v_dn_onorm, v_lru_conv, v_lru_conv_b, v_lru_w_rg, v_lru_b_rg, v_lru_w_ig, v_lru_b_ig, v_lru_lambda, v_w_branch_dn, v_w_branch_lru, v_w_out, v_w_up, v_ffn_dw, v_ffn_dw_b, v_w_down):
    W = dict(zip(_WEIGHTS, (c_ctx, w_ada, b_ada, g_pre_mix, g_post_mix, g_pre_ffn, g_post_ffn, w_in, b_merge, dn_conv,
                            dn_a_log, dn_dt_bias, dn_onorm, lru_conv, lru_conv_b, lru_w_rg, lru_b_rg, lru_w_ig, lru_b_ig,
                            lru_lambda, w_branch_dn, w_branch_lru, w_out, w_up, ffn_dw, ffn_dw_b, w_down)))
    Mo = dict(zip(_WEIGHTS, (m_c_ctx, m_w_ada, m_b_ada, m_g_pre_mix, m_g_post_mix, m_g_pre_ffn, m_g_post_ffn, m_w_in,
                             m_b_merge, m_dn_conv, m_dn_a_log, m_dn_dt_bias, m_dn_onorm, m_lru_conv, m_lru_conv_b,
                             m_lru_w_rg, m_lru_b_rg, m_lru_w_ig, m_lru_b_ig, m_lru_lambda, m_w_branch_dn,
                             m_w_branch_lru, m_w_out, m_w_up, m_ffn_dw, m_ffn_dw_b, m_w_down)))
    Vo = dict(zip(_WEIGHTS, (v_c_ctx, v_w_ada, v_b_ada, v_g_pre_mix, v_g_post_mix, v_g_pre_ffn, v_g_post_ffn, v_w_in,
                             v_b_merge, v_dn_conv, v_dn_a_log, v_dn_dt_bias, v_dn_onorm, v_lru_conv, v_lru_conv_b,
                             v_lru_w_rg, v_lru_b_rg, v_lru_w_ig, v_lru_b_ig, v_lru_lambda, v_w_branch_dn,
                             v_w_branch_lru, v_w_out, v_w_up, v_ffn_dw, v_ffn_dw_b, v_w_down)))
    B, N, D = x.shape
    NC = ctx.shape[1]
    T = NC + N
    H, HD = dn_a_log.shape[-1], dn_onorm.shape[-1]
    DNW = H * HD
    LW, LBD = lru_conv_b.shape[-1], lru_w_rg.shape[-1]
    DFF = ffn_dw_b.shape[-1]
    LC = LANES
    x_i, y_i, c_i = _me()
    s_me = 2 * x_i + y_i
    tm = _tile(math.gcd(NC, N), 256, 16)

    def whole(n, g):
        r, w_ = W[n].shape[1:]
        return g.reshape(_NCHIP, r, w_) if _BIG[n] else g.reshape(_NCHIP * r, w_)

    first = ("w_ada", "w_in")
    later = tuple(n for n in _BIG if n not in first)
    shard16 = {n: W[n][0].astype(BF16) for n in _BIG}
    full = {n: whole(n, g) for n, g in zip(first, _allgather_halves([shard16[n] for n in first], name="allgather_first"))}

    small_local = [W[n][0].reshape(-1, W[n].shape[-1]) for n in _SMALL_SHARDED]
    small_shapes = [a.shape for a in small_local]
    spack = _pack(small_local, LANES, _FLAT_PART, _FLAT_PART, F32)
    sgath = _allgather_small(spack)[0::2]
    sfull = {n: _from_chip_shards(s, True)
             for n, s in zip(_SMALL_SHARDED, _unpack(sgath, small_shapes, LANES, _FLAT_PART, _FLAT_PART))}

    later16, sgath = lax.optimization_barrier(([shard16[n] for n in later], sgath))
    ag_send, ag_recv, ag_src, ag_land, ag_token = _allgather_start(later16, name="ag_start")

    o_a = 4 * DNW
    o_xl = o_a + 4 * H
    o_mg = o_xl + 2 * LW
    wi_ = _from_chip_shards(full["w_in"], True)
    nj = LW // LC
    lru_cols = jnp.stack([wi_[:, o_xl:o_xl + LW].reshape(D, nj, LC), wi_[:, o_xl + LW:o_mg].reshape(D, nj, LC)],
                         axis=2).reshape(D, 2 * LW)
    wp = jnp.concatenate([wi_[:, :o_a], lru_cols, wi_[:, o_mg:], wi_[:, o_a:o_xl],
                          jnp.zeros((D, LANES - 4 * H), BF16)], axis=1)
    p_lru, p_mg, p_ab = 4 * DNW, 4 * DNW + 2 * LW, 4 * DNW + 2 * LW + 2 * D
    PW = p_ab + LANES

    MR = LANES
    cond = jnp.concatenate([c, c_ctx[None], jnp.zeros((MR - B - 1, D), F32)], axis=0)
    silu_rows = _rowwise(lambda a: (_silu(a),), [cond], [F32], name="cond_silu")[0]
    mod = _matmul(silu_rows, full["w_ada"], b_shards=(0, _NCHIP), name="ada_fwd") + b_ada + ag_token[0, 0]
    mx = mod[:B].reshape(B, 6, D)
    mc = mod[B].reshape(6, D)
    zero = jnp.zeros((B, D), F32)
    tab = jnp.stack([jnp.stack([jnp.broadcast_to(mc[0], (B, D)), jnp.broadcast_to(mc[1], (B, D))] + [zero] * 6, axis=1),
                     jnp.stack([mx[:, 0], mx[:, 1]] + [zero] * 6, axis=1)], axis=1)
    vecs = jnp.stack([mx[:, 2], mx[:, 3], mx[:, 4], mx[:, 5]] + [zero] * 4, axis=1)
    gains = jnp.concatenate([g_post_mix, g_pre_ffn, g_post_ffn, jnp.zeros((5, D), F32)], axis=0)

    h = jnp.concatenate([ctx, x], axis=1)
    u = _premix_fwd(h, g_pre_mix, tab, nc=NC, tm=tm)
    p = _matmul(u, wp, name="in_fwd")
    dkw = dict(B=B, T=T, nc=NC, H=H, HD=HD)
    qkv = _dnprep_fwd(p, sfull["dn_conv"], **dkw)
    prm = jnp.concatenate([
        jnp.concatenate([dn_a_log.reshape(1, 2 * H), jnp.zeros((1, LANES - 2 * H), F32)], axis=1),
        jnp.concatenate([dn_dt_bias.reshape(1, 2 * H), jnp.zeros((1, LANES - 2 * H), F32)], axis=1),
        jnp.zeros((6, LANES), F32)], axis=0)
    gtm = _tile(B * T, 512, 16)
    gb = _gb_fwd(p, prm, rows=B * T, col0=p_ab, H=H, tm=gtm)
    y_dn, o_dn, *dn_res = _delta_fwd(qkv, gb, p, dn_onorm, **dkw)
    lv = jnp.concatenate([lru_conv_b, sfull["lru_b_rg"], sfull["lru_b_ig"], sfull["lru_lambda"], jnp.zeros((1, LW), F32)], axis=0)
    wr = _blockdiag(lru_w_rg[0], LC).astype(BF16)
    wi = _blockdiag(lru_w_ig[0], LC).astype(BF16)
    lkw = dict(B=B, T=T, nc=NC, LW=LW, col0=p_lru, C=LC)
    y_lru = _lru_fwd(p, sfull["lru_conv"], lv, wr, wi, **lkw)
    ag_src, ag_land = _allgather_wait(ag_send, ag_recv, ag_src, ag_land, y_lru, name="ag_wait")
    me_piece = 4 * x_i + 2 * y_i + c_i
    for n, src, land in zip(later, ag_src, _pass_to_sibling(ag_land, name="ag_pass")):
        own = lax.dynamic_slice_in_dim(src, c_i * (src.shape[0] // 2), src.shape[0] // 2, axis=0)
        full[n] = whole(n, lax.dynamic_update_index_in_dim(land, own, me_piece, axis=0))
    Ydn = _matmul(y_dn, full["w_branch_dn"], name="bdn_fwd")
    Ylru = _matmul(y_lru, full["w_branch_lru"], name="blru_fwd")
    mkw = dict(B=B, T=T, nc=NC, D=D, col0=p_mg, tm=tm)
    mixin = _merge_fwd(p, Ydn, Ylru, b_merge, **mkw)
    mix = _matmul(mixin, full["w_out"], name="out_fwd")
    h1, u2 = _post_fwd(x, mix, gains, vecs, tm=tm)
    F = _matmul(u2, full["w_up"], b_shards=(0, _NCHIP), name="up_fwd")
    w9 = sfull["ffn_dw"]
    ftc = _tile(DFF, 256)
    f = _ffn_act_fwd(F, w9, ffn_dw_b, B=B, N=N, DFF=DFF, tc=ftc)
    dn = _matmul(f, full["w_down"], name="down_fwd")
    ddn, dout, sums_f = _final(h1, dn, loss_target, gains, vecs, tm=tm)

    G = {}
    df = _matmul(ddn, full["w_down"], tb=True, name="down_bwd_x")
    G["w_down"] = _matmul(f, ddn, ta=True, name="down_bwd_w")
    dFg, dFv, dwb = _ffn_act_bwd(F, w9, ffn_dw_b, df, B=B, N=N, DFF=DFF, tc=ftc)
    hs = _NCHIP // 2
    du2 = _matmul(dFg, full["w_up"], tb=True, b_shards=(0, hs), name="up_bwd_xg")
    du2 = _matmul(dFv, full["w_up"], tb=True, b_shards=(hs, hs), add=du2, name="up_bwd_xv")
    G["w_up"] = jnp.concatenate([_matmul(u2, dFg, ta=True, out_shards=hs, name="up_bwd_wg"),
                                 _matmul(u2, dFv, ta=True, out_shards=hs, name="up_bwd_wv")], axis=0)
    dx1, dmix, sums_p = _post_bwd(x, mix, gains, vecs, dout, du2, tm=tm)
    dmixin = _matmul(dmix, full["w_out"], tb=True, name="out_bwd_x")
    G["w_out"] = _matmul(mixin, dmix, ta=True, name="out_bwd_w")
    dp = jnp.zeros((B * T, PW), BF16)
    dYdn, dYlru, dp, sums_m = _merge_bwd(p, Ydn, Ylru, b_merge, dmixin, dp, **mkw)
    dy_dn = _matmul(dYdn, full["w_branch_dn"], tb=True, name="bdn_bwd_x")
    G["w_branch_dn"] = _matmul(y_dn, dYdn, ta=True, name="bdn_bwd_w")
    dy_lru = _matmul(dYlru, full["w_branch_lru"], tb=True, name="blru_bwd_x")
    G["w_branch_lru"] = _matmul(y_lru, dYlru, ta=True, name="blru_bwd_w")

    c_arr = c_i.astype(jnp.int32).reshape(1)
    s_arr = s_me.astype(jnp.int32).reshape(1)

    def chip_sums(names, tag):
        slabs = [G[n] if _BIG[n] else G[n].reshape(_NCHIP, G[n].shape[0] // _NCHIP, G[n].shape[1]) for n in names]
        from_sibling = _sibling_send_halves(slabs, name="rs_sibling_" + tag)
        return [_half_add(g, r, c_arr, name="rs_add_" + n) for n, g, r in zip(names, slabs, from_sibling)]

    early = tuple(n for n in _BIG if n in G)
    late = tuple(n for n in _BIG if n not in G)
    cx_send, cx_recv, cx_src, cx_land, cx_token = _chip_exchange_start(chip_sums(early, "early"), name="cx_start")
    dp, dcw_l, dlv, dwr, dwi = _lru_bwd(p, sfull["lru_conv"], lv + cx_token[0, 0], wr, wi, dy_lru, dp, **lkw)
    dqkv, dgb, dp, don = _delta_bwd(qkv, gb, p, dn_onorm, o_dn, dn_res, dy_dn, dp, **dkw)
    dp, dprm = _gb_bwd(p, prm, dgb, dp, rows=B * T, col0=p_ab, H=H, tm=gtm)
    dp, dcw_d = _dnprep_bwd(p, sfull["dn_conv"], dqkv, dp, **dkw)
    dU = _matmul(dp, wp, tb=True, name="in_bwd_x")
    dwp = _matmul(u, dp, ta=True, name="in_bwd_w")
    grad_x, sums_pm = _premix_bwd(h, g_pre_mix, tab, dU, dx1, nc=NC, tm=tm)
    dlru = dwp[:, p_lru:p_mg].reshape(D, nj, 2, LC)
    G["w_in"] = _to_chip_shards(jnp.concatenate([dwp[:, :o_a], dwp[:, p_ab:p_ab + 4 * H], dlru[:, :, 0].reshape(D, LW),
                                                 dlru[:, :, 1].reshape(D, LW), dwp[:, p_mg:p_ab]], axis=1), True)

    dmod_x = jnp.stack([sums_pm[:, 1, 0], sums_pm[:, 1, 1], sums_p[:, 0], sums_p[:, 1], sums_p[:, 2], sums_f[:, 0]],
                       axis=1).reshape(B, 6 * D)
    dmod_c = jnp.concatenate([sums_pm[:, 0, 0].sum(0), sums_pm[:, 0, 1].sum(0), jnp.zeros((4 * D,), F32)])[None]
    dmod = jnp.concatenate([dmod_x, dmod_c, jnp.zeros((MR - B - 1, 6 * D), F32)], axis=0)
    G["w_ada"] = _matmul(silu_rows, dmod, ta=True, out_shards=_NCHIP, name="ada_bwd_w")
    dsilu = _matmul(dmod, full["w_ada"], tb=True, b_shards=(0, _NCHIP), name="ada_bwd_x")

    g_small = {
        "c_ctx": dsilu[B] * _dsilu(c_ctx),
        "b_ada": dmod[:B + 1].sum(0)[None],
        "g_pre_mix": sums_pm[:, :, 2].sum((0, 1))[None],
        "g_post_mix": sums_p[:, 3].sum(0)[None],
        "g_pre_ffn": sums_p[:, 4].sum(0)[None],
        "g_post_ffn": sums_f[:, 1].sum(0)[None],
        "b_merge": sums_m[0:1],
        "dn_conv": dcw_d[0:4][None],
        "dn_a_log": dprm[0, :2 * H].reshape(1, 2, H),
        "dn_dt_bias": dprm[1, :2 * H].reshape(1, 2, H),
        "dn_onorm": don[:, 0].sum(0)[None],
        "lru_conv": dcw_l[0:4][None],
        "lru_conv_b": dlv[0:1],
        "lru_w_rg": _blockdiag_extract(dwr, LBD)[None],
        "lru_b_rg": dlv[1:3][None],
        "lru_w_ig": _blockdiag_extract(dwi, LBD)[None],
        "lru_b_ig": dlv[3:5][None],
        "lru_lambda": dlv[5:7][None],
        "ffn_dw": dwb[0:9].reshape(1, 3, 3, DFF),
        "ffn_dw_b": dwb[9:10],
    }
    small_names = tuple(n for n in _WEIGHTS if n not in _BIG)
    loss_part = sums_f[:, 2].sum().reshape(1)
    gs_list = [g_small[n] for n in small_names] + [loss_part]
    gs_shapes = [a.shape for a in gs_list]
    gpack = _pack(gs_list, LANES, _FLAT_PART, _FLAT_TOTAL, F32)
    gsum = _sum_lead(_allgather_small(gpack), name="small_sum", tm=512, mult=SUBLANES)
    gs_red = dict(zip(small_names + ("loss",), _unpack(gsum, gs_shapes, LANES, _FLAT_PART, _FLAT_TOTAL)))
    loss = gs_red["loss"][0]

    cx_src, cx_land = _chip_exchange_wait(cx_send, cx_recv, cx_src, cx_land, dsilu, name="cx_wait")
    late_sums = chip_sums(late, "late")
    late_land = _chip_exchange(late_sums, name="chip_exchange")
    half = {n: _sum_slabs(l, src, s_arr, name="rs_sum_" + n)
            for n, l, src in zip(early + late, list(cx_land) + list(late_land), list(cx_src) + list(late_sums))}
    halves = [half[n] for n in _BIG]
    sib_halves = _sibling_swap(halves, name="rs_gather")

    grads, deltas, new_m, new_v = {}, {}, {}, {}
    for n, own, sib in zip(_BIG, halves, sib_halves):
        shp = W[n].shape
        outs = _adamw_halves(W[n][0], own, sib, Mo[n][0], Vo[n][0], c_arr, name="adamw_" + n)
        grads[n], deltas[n], new_m[n], new_v[n] = (o.reshape(shp) for o in outs)
    for n in small_names:
        g = gs_red[n]
        if n in _SMALL_SHARDED:
            k = W[n].shape[-1]
            g = lax.dynamic_slice_in_dim(g, s_me * k, k, axis=g.ndim - 1)
        grads[n] = g.reshape(W[n].shape)
    sm_shapes = [W[n].shape for n in small_names]
    pk = lambda d: _pack([d[n] for n in small_names], LANES, _FLAT_PART, _FLAT_TOTAL, F32)
    d_, m_, v_ = _adamw(pk(W), pk(grads), pk(Mo), pk(Vo), name="adamw_small")
    for dst, pool_ in ((deltas, d_), (new_m, m_), (new_v, v_)):
        dst.update(zip(small_names, _unpack(pool_, sm_shapes, LANES, _FLAT_PART, _FLAT_TOTAL)))
    return (loss, grad_x, *[grads[n] for n in _WEIGHTS], *[deltas[n] for n in _WEIGHTS],
            *[new_m[n] for n in _WEIGHTS], *[new_v[n] for n in _WEIGHTS])
```

```python
import functools
import math

import jax
import jax.numpy as jnp
from jax import lax
from jax.experimental import pallas as pl
from jax.experimental.pallas import tpu as pltpu

F32 = jnp.float32
BF16 = jnp.bfloat16
EPS = 1e-6
GRID_W = 64
CHUNK = 256
LRU_C = 8.0
LANES = 128
SUBLANES = 8
VMEM_LIMIT = 56 * 1024 * 1024
ADAM_LR, ADAM_B1, ADAM_B2, ADAM_EPS, ADAM_WD, ADAM_STEP = 0.001, 0.9, 0.999, 1e-08, 0.01, 10
MESH = pl.DeviceIdType.MESH


def _tile(n, target, mult=LANES):
    best = None
    for t in range(mult, min(n, target) + 1, mult):
        if n % t == 0:
            best = t
    return best if best is not None else n


def _params(sem=None, **kw):
    return pltpu.CompilerParams(dimension_semantics=sem, vmem_limit_bytes=VMEM_LIMIT, **kw)


def _sigmoid(x):
    return 1.0 / (1.0 + jnp.exp(-x))


def _silu(x):
    return x * _sigmoid(x)


def _softplus(x):
    return jnp.maximum(x, 0.0) + jnp.log(1.0 + jnp.exp(-jnp.abs(x)))


def _gelu(x):
    return 0.5 * x * (1.0 + jnp.tanh(math.sqrt(2.0 / math.pi) * (x + 0.044715 * x * x * x)))


def _rmsn(u, gain):
    return u * lax.rsqrt(jnp.mean(u * u, axis=-1, keepdims=True) + EPS) * gain


_MM_VMEM = 40 * 1024 * 1024


def _matmul(a, b, *, ta=False, tb=False, add=None, b_shards=None, out_shards=None, out_dtype=F32, name,
            tm=1024, tn=2048, tk=1024):
    (K, M) = a.shape if ta else a.shape[::-1]
    if b_shards is not None:
        s0, ns = b_shards
        bsh = (b.shape[1], ns * b.shape[2])
        nsh = b.shape[2]
    else:
        bsh = b.shape
    N = bsh[0] if tb else bsh[1]
    assert (bsh[1] if tb else bsh[0]) == K, (a.shape, b.shape, ta, tb)
    tm = _tile(M, tm)
    tk = _tile(nsh if (b_shards is not None and tb) else K, tk)
    nlim = nsh if (b_shards is not None and not tb) else (N // out_shards if out_shards else N)
    osz = jnp.dtype(out_dtype).itemsize + (4 if add is not None else 0)
    while True:
        tn_ = _tile(nlim, tn)
        need = 2 * (tm * tk * a.dtype.itemsize + tk * tn_ * b.dtype.itemsize + tm * tn_ * osz) + 4 * tm * tn_
        if need <= _MM_VMEM or tn <= LANES:
            break
        tn //= 2
    tn = tn_
    nk = K // tk
    dims = (((0 if ta else 1,), (1 if tb else 0,)), ((), ()))

    def body(a_ref, b_ref, *rest):
        (c_ref, o_ref, acc_ref) = rest if add is not None else (None, *rest)
        k = pl.program_id(2)

        @pl.when(k == 0)
        def _():
            acc_ref[...] = jnp.zeros_like(acc_ref) if c_ref is None else c_ref[...]

        bv = b_ref[0] if b_shards is not None else b_ref[...]
        acc_ref[...] += lax.dot_general(a_ref[...].astype(BF16), bv.astype(BF16), dims, preferred_element_type=F32)

        @pl.when(k == nk - 1)
        def _():
            if out_shards:
                o_ref[0] = acc_ref[...].astype(out_dtype)
            else:
                o_ref[...] = acc_ref[...].astype(out_dtype)

    a_spec = pl.BlockSpec((tk, tm), lambda i, j, k: (k, i)) if ta else pl.BlockSpec((tm, tk), lambda i, j, k: (i, k))
    if b_shards is None:
        b_spec = pl.BlockSpec((tn, tk), lambda i, j, k: (j, k)) if tb else pl.BlockSpec((tk, tn), lambda i, j, k: (k, j))
    elif tb:
        per = nsh // tk
        b_spec = pl.BlockSpec((1, tn, tk), lambda i, j, k: (s0 + k // per, j, k % per))
    else:
        per = nsh // tn
        b_spec = pl.BlockSpec((1, tk, tn), lambda i, j, k: (s0 + j // per, k, j % per))
    o_spec = pl.BlockSpec((tm, tn), lambda i, j, k: (i, j))
    if out_shards:
        oper = N // out_shards // tn
        out_spec = pl.BlockSpec((1, tm, tn), lambda i, j, k: (j // oper, i, j % oper))
        out_shape = jax.ShapeDtypeStruct((out_shards, M, N // out_shards), out_dtype)
    else:
        out_spec, out_shape = o_spec, jax.ShapeDtypeStruct((M, N), out_dtype)
    return pl.pallas_call(
        body, name=name, grid=(M // tm, N // tn, nk),
        in_specs=[a_spec, b_spec] + ([o_spec] if add is not None else []),
        out_specs=out_spec, out_shape=out_shape,
        scratch_shapes=[pltpu.VMEM((tm, tn), F32)],
        compiler_params=_params(("parallel", "parallel", "arbitrary")),
    )(*((a, b) + ((add,) if add is not None else ())))


def _premix_math(h, gain, shift, scale):
    return _rmsn(h, gain) * (1.0 + scale) + shift


def _premix_fwd(h, gain, tab, *, nc, tm):
    B, T, D = h.shape
    nt, nct = T // tm, nc // tm

    def body(h_ref, g_ref, tab_ref, u_ref):
        tabv = tab_ref[0, 0]
        u_ref[...] = _premix_math(h_ref[0], g_ref[...], tabv[0:1], tabv[1:2]).astype(BF16)

    return pl.pallas_call(
        body, name="premix_fwd", grid=(B, nt),
        in_specs=[pl.BlockSpec((1, tm, D), lambda b, t: (b, t, 0)),
                  pl.BlockSpec((1, D), lambda b, t: (0, 0)),
                  pl.BlockSpec((1, 1, 8, D), lambda b, t: (b, jnp.where(t < nct, 0, 1), 0, 0))],
        out_specs=pl.BlockSpec((tm, D), lambda b, t: (b * nt + t, 0)),
        out_shape=jax.ShapeDtypeStruct((B * T, D), BF16),
        compiler_params=_params(("parallel", "parallel")),
    )(h, gain, tab)


def _premix_bwd(h, gain, tab, du, dres, *, nc, tm):
    B, T, D = h.shape
    nt, nct = T // tm, nc // tm
    N = T - nc

    def body(h_ref, g_ref, tab_ref, du_ref, dres_ref, dx_ref, sums_ref):
        t = pl.program_id(1)
        tabv = tab_ref[0, 0]
        _, vjp = jax.vjp(_premix_math, h_ref[0], g_ref[...], tabv[0:1], tabv[1:2])
        dh, dgain, dshift, dscale = vjp(du_ref[...].astype(F32))

        @pl.when((t == 0) | (t == nct))
        def _():
            sums_ref[...] = jnp.zeros_like(sums_ref)

        sums_ref[0, 0, 0:1, :] += dshift
        sums_ref[0, 0, 1:2, :] += dscale
        sums_ref[0, 0, 2:3, :] += dgain

        @pl.when(t >= nct)
        def _():
            dx_ref[0] = dres_ref[...] + dh

    lat = lambda b, t: jnp.maximum(t - nct, 0)
    return pl.pallas_call(
        body, name="premix_bwd", grid=(B, nt),
        in_specs=[pl.BlockSpec((1, tm, D), lambda b, t: (b, t, 0)),
                  pl.BlockSpec((1, D), lambda b, t: (0, 0)),
                  pl.BlockSpec((1, 1, 8, D), lambda b, t: (b, jnp.where(t < nct, 0, 1), 0, 0)),
                  pl.BlockSpec((tm, D), lambda b, t: (b * nt + t, 0)),
                  pl.BlockSpec((tm, D), lambda b, t: (b * (nt - nct) + lat(b, t), 0))],
        out_specs=[pl.BlockSpec((1, tm, D), lambda b, t: (b, lat(b, t), 0)),
                   pl.BlockSpec((1, 1, 8, D), lambda b, t: (b, jnp.where(t < nct, 0, 1), 0, 0))],
        out_shape=[jax.ShapeDtypeStruct((B, N, D), F32), jax.ShapeDtypeStruct((B, 2, 8, D), F32)],
        compiler_params=_params(("parallel", "arbitrary")),
    )(h, gain, tab, du, dres)


def _merge_math(mgd, mgl, yd, yl, bd, bl):
    return _sigmoid(mgd + bd) * yd + _sigmoid(mgl + bl) * yl


def _merge_fwd(p, ydn, ylru, b_merge, *, B, T, nc, D, col0, tm):
    N = T - nc
    ntl, nt, nct, cb = N // tm, T // tm, nc // tm, col0 // D

    def body(mgd_ref, mgl_ref, yd_ref, yl_ref, bm_ref, o_ref):
        o_ref[...] = _merge_math(mgd_ref[...], mgl_ref[...], yd_ref[...], yl_ref[...],
                                 bm_ref[:, 0:D], bm_ref[:, D:2 * D]).astype(BF16)

    prow = lambda b, t: b * nt + nct + t
    return pl.pallas_call(
        body, name="merge_fwd", grid=(B, ntl),
        in_specs=[pl.BlockSpec((tm, D), lambda b, t: (prow(b, t), cb)),
                  pl.BlockSpec((tm, D), lambda b, t: (prow(b, t), cb + 1)),
                  pl.BlockSpec((tm, D), lambda b, t: (b * ntl + t, 0)),
                  pl.BlockSpec((tm, D), lambda b, t: (b * ntl + t, 0)),
                  pl.BlockSpec((1, 2 * D), lambda b, t: (0, 0))],
        out_specs=pl.BlockSpec((tm, D), lambda b, t: (b * ntl + t, 0)),
        out_shape=jax.ShapeDtypeStruct((B * N, D), BF16),
        compiler_params=_params(("parallel", "parallel")),
    )(p, p, ydn, ylru, b_merge)


def _merge_bwd(p, ydn, ylru, b_merge, dmix, dp, *, B, T, nc, D, col0, tm):
    N = T - nc
    ntl, nt, nct, cb = N // tm, T // tm, nc // tm, col0 // D
    assert col0 % (2 * D) == 0

    def body(mgd_ref, mgl_ref, yd_ref, yl_ref, bm_ref, dm_ref, dp_any, dyd_ref, dyl_ref, dp_ref, sums_ref):
        _, vjp = jax.vjp(_merge_math, mgd_ref[...], mgl_ref[...], yd_ref[...], yl_ref[...],
                         bm_ref[:, 0:D], bm_ref[:, D:2 * D])
        dmgd, dmgl, dyd, dyl, dbd, dbl = vjp(dm_ref[...])
        dyd_ref[...] = dyd.astype(BF16)
        dyl_ref[...] = dyl.astype(BF16)
        dp_ref[:, 0:D] = dmgd.astype(BF16)
        dp_ref[:, D:2 * D] = dmgl.astype(BF16)

        @pl.when((pl.program_id(0) == 0) & (pl.program_id(1) == 0))
        def _():
            sums_ref[...] = jnp.zeros_like(sums_ref)

        sums_ref[0:1, 0:D] += dbd
        sums_ref[0:1, D:2 * D] += dbl

    prow = lambda b, t: b * nt + nct + t
    row = pl.BlockSpec((tm, D), lambda b, t: (b * ntl + t, 0))
    return pl.pallas_call(
        body, name="merge_bwd", grid=(B, ntl),
        in_specs=[pl.BlockSpec((tm, D), lambda b, t: (prow(b, t), cb)),
                  pl.BlockSpec((tm, D), lambda b, t: (prow(b, t), cb + 1)),
                  row, row, pl.BlockSpec((1, 2 * D), lambda b, t: (0, 0)), row,
                  pl.BlockSpec(memory_space=pl.ANY)],
        out_specs=[row, row,
                   pl.BlockSpec((tm, 2 * D), lambda b, t: (prow(b, t), cb // 2)),
                   pl.BlockSpec((8, 2 * D), lambda b, t: (0, 0))],
        out_shape=[jax.ShapeDtypeStruct((B * N, D), BF16), jax.ShapeDtypeStruct((B * N, D), BF16),
                   jax.ShapeDtypeStruct(dp.shape, dp.dtype), jax.ShapeDtypeStruct((8, 2 * D), F32)],
        input_output_aliases={6: 2},
        compiler_params=_params(("arbitrary", "arbitrary")),
    )(p, p, ydn, ylru, b_merge, dmix, dp)


def _post_math(x, mix, g1, gate, g2, sh, sc):
    h1 = x + _rmsn(mix, g1) * gate
    return h1, _rmsn(h1, g2) * (1.0 + sc) + sh


def _post_fwd(x, mix, gains, vecs, *, tm):
    B, N, D = x.shape
    ntl = N // tm

    def body(x_ref, mix_ref, g_ref, v_ref, h1_ref, u2_ref):
        v = v_ref[0]
        h1, u2 = _post_math(x_ref[0], mix_ref[...], g_ref[0:1], v[0:1], g_ref[1:2], v[1:2], v[2:3])
        h1_ref[...] = h1
        u2_ref[...] = u2.astype(BF16)

    row = pl.BlockSpec((tm, D), lambda b, t: (b * ntl + t, 0))
    return pl.pallas_call(
        body, name="post_fwd", grid=(B, ntl),
        in_specs=[pl.BlockSpec((1, tm, D), lambda b, t: (b, t, 0)), row,
                  pl.BlockSpec((8, D), lambda b, t: (0, 0)), pl.BlockSpec((1, 8, D), lambda b, t: (b, 0, 0))],
        out_specs=[row, row],
        out_shape=[jax.ShapeDtypeStruct((B * N, D), F32), jax.ShapeDtypeStruct((B * N, D), BF16)],
        compiler_params=_params(("parallel", "parallel")),
    )(x, mix, gains, vecs)


def _post_bwd(x, mix, gains, vecs, dh1, du2, *, tm):
    B, N, D = x.shape
    ntl = N // tm

    def body(x_ref, mix_ref, g_ref, v_ref, dh1_ref, du2_ref, dx_ref, dmix_ref, sums_ref):
        v = v_ref[0]
        _, vjp = jax.vjp(_post_math, x_ref[0], mix_ref[...], g_ref[0:1], v[0:1], g_ref[1:2], v[1:2], v[2:3])
        dx, dmix, dg1, dgate, dg2, dsh, dsc = vjp((dh1_ref[...], du2_ref[...]))
        dx_ref[...] = dx
        dmix_ref[...] = dmix.astype(BF16)

        @pl.when(pl.program_id(1) == 0)
        def _():
            sums_ref[...] = jnp.zeros_like(sums_ref)

        sums_ref[0, 0:1, :] += dgate
        sums_ref[0, 1:2, :] += dsh
        sums_ref[0, 2:3, :] += dsc
        sums_ref[0, 3:4, :] += dg1
        sums_ref[0, 4:5, :] += dg2

    row = pl.BlockSpec((tm, D), lambda b, t: (b * ntl + t, 0))
    return pl.pallas_call(
        body, name="post_bwd", grid=(B, ntl),
        in_specs=[pl.BlockSpec((1, tm, D), lambda b, t: (b, t, 0)), row,
                  pl.BlockSpec((8, D), lambda b, t: (0, 0)), pl.BlockSpec((1, 8, D), lambda b, t: (b, 0, 0)), row, row],
        out_specs=[row, row, pl.BlockSpec((1, 8, D), lambda b, t: (b, 0, 0))],
        out_shape=[jax.ShapeDtypeStruct((B * N, D), F32), jax.ShapeDtypeStruct((B * N, D), BF16),
                   jax.ShapeDtypeStruct((B, 8, D), F32)],
        compiler_params=_params(("parallel", "arbitrary")),
    )(x, mix, gains, vecs, dh1, du2)


def _final_math(dn, g4, gate5):
    return _rmsn(dn, g4) * gate5


def _final(h1, dn, target, gains, vecs, *, tm):
    B, N, D = target.shape
    ntl = N // tm

    def body(h1_ref, dn_ref, t_ref, g_ref, v_ref, ddn_ref, dout_ref, sums_ref):
        v = v_ref[0]
        y, vjp = jax.vjp(_final_math, dn_ref[...], g_ref[2:3], v[3:4])
        err = h1_ref[...] + y - t_ref[0]
        dout = err * (1.0 / D)
        ddn, dg4, dgate5 = vjp(dout)
        ddn_ref[...] = ddn.astype(BF16)
        dout_ref[...] = dout

        @pl.when(pl.program_id(1) == 0)
        def _():
            sums_ref[...] = jnp.zeros_like(sums_ref)

        sums_ref[0, 0:1, :] += dgate5
        sums_ref[0, 1:2, :] += dg4
        sums_ref[0, 2:3, :] += jnp.sum(err * err, axis=0, keepdims=True) * (0.5 / D)

    row = pl.BlockSpec((tm, D), lambda b, t: (b * ntl + t, 0))
    return pl.pallas_call(
        body, name="final", grid=(B, ntl),
        in_specs=[row, row, pl.BlockSpec((1, tm, D), lambda b, t: (b, t, 0)),
                  pl.BlockSpec((8, D), lambda b, t: (0, 0)), pl.BlockSpec((1, 8, D), lambda b, t: (b, 0, 0))],
        out_specs=[row, row, pl.BlockSpec((1, 8, D), lambda b, t: (b, 0, 0))],
        out_shape=[jax.ShapeDtypeStruct((B * N, D), BF16), jax.ShapeDtypeStruct((B * N, D), F32),
                   jax.ShapeDtypeStruct((B, 8, D), F32)],
        compiler_params=_params(("parallel", "arbitrary")),
    )(h1, dn, target, gains, vecs)


def _shift(x, s):
    s = s % x.shape[0]
    return x if s == 0 else pltpu.roll(x, s, 0)


def _seg_taps(T, nc, width, pad_left):
    t = lax.broadcasted_iota(jnp.int32, (T, 1), 0)
    pos = jnp.where(t < nc, t, t - nc)
    seg = jnp.where(t < nc, nc, T - nc)
    taps = []
    for k in range(width):
        src = pos + (k - pad_left)
        taps.append((pad_left - k, (src >= 0) & (src < seg)))
    return taps


def _grid_taps(N):
    t = lax.broadcasted_iota(jnp.int32, (N, 1), 0)
    wcol = t % GRID_W
    taps = []
    for dr in (-1, 0, 1):
        for dw in (-1, 0, 1):
            off = dr * GRID_W + dw
            ok = (wcol + dw >= 0) & (wcol + dw < GRID_W) & (t + dr * GRID_W >= 0) & (t + dr * GRID_W < N)
            taps.append((-off, ok))
    return taps


def _conv_fwd(x, w, taps):
    y = jnp.zeros_like(x)
    for k, (s, m) in enumerate(taps):
        y = y + w[k:k + 1] * jnp.where(m, _shift(x, s), 0.0)
    return y


def _conv_bwd(x, w, taps, dy):
    dx = jnp.zeros_like(x)
    dws = []
    for k, (s, m) in enumerate(taps):
        dym = jnp.where(m, dy, 0.0)
        dx = dx + w[k:k + 1] * _shift(dym, -s)
        dws.append(jnp.sum(dym * _shift(x, s), axis=0, keepdims=True))
    return dx, jnp.concatenate(dws, axis=0)


def _ffn_act_fwd(F, w9, bias, *, B, N, DFF, tc):
    nj = DFF // tc

    def body(fg_ref, fv_ref, w_ref, b_ref, o_ref):
        fg = _conv_fwd(fg_ref[...], w_ref[...], _grid_taps(N)) + b_ref[...]
        o_ref[...] = (_gelu(fg) * fv_ref[...]).astype(BF16)

    return pl.pallas_call(
        body, name="ffn_act_fwd", grid=(B, nj),
        in_specs=[pl.BlockSpec((N, tc), lambda b, j: (b, j)), pl.BlockSpec((N, tc), lambda b, j: (b, nj + j)),
                  pl.BlockSpec((9, tc), lambda b, j: (0, j)), pl.BlockSpec((1, tc), lambda b, j: (0, j))],
        out_specs=pl.BlockSpec((N, tc), lambda b, j: (b, j)),
        out_shape=jax.ShapeDtypeStruct((B * N, DFF), BF16),
        compiler_params=_params(("parallel", "parallel")),
    )(F, F, w9, bias)


def _ffn_act_bwd(F, w9, bias, df, *, B, N, DFF, tc):
    nj = DFF // tc

    def body(fg_ref, fv_ref, w_ref, b_ref, df_ref, dfg_ref, dfv_ref, dwb_ref):
        taps = _grid_taps(N)
        x = fg_ref[...]
        fg, vjp = jax.vjp(lambda a: _gelu(a), _conv_fwd(x, w_ref[...], taps) + b_ref[...])
        dfl = df_ref[...]
        dfv_ref[...] = (dfl * fg).astype(BF16)
        (dpre,) = vjp(dfl * fv_ref[...])
        dx, dw = _conv_bwd(x, w_ref[...], taps, dpre)
        dfg_ref[...] = dx.astype(BF16)

        @pl.when(pl.program_id(1) == 0)
        def _():
            dwb_ref[...] = jnp.zeros_like(dwb_ref)

        dwb_ref[0:9, :] += dw
        dwb_ref[9:10, :] += jnp.sum(dpre, axis=0, keepdims=True)

    col = pl.BlockSpec((N, tc), lambda j, b: (b, j))
    return pl.pallas_call(
        body, name="ffn_act_bwd", grid=(nj, B),
        in_specs=[col, pl.BlockSpec((N, tc), lambda j, b: (b, nj + j)),
                  pl.BlockSpec((9, tc), lambda j, b: (0, j)), pl.BlockSpec((1, tc), lambda j, b: (0, j)), col],
        out_specs=[col, col, pl.BlockSpec((16, tc), lambda j, b: (0, j))],
        out_shape=[jax.ShapeDtypeStruct((B * N, DFF), BF16), jax.ShapeDtypeStruct((B * N, DFF), BF16),
                   jax.ShapeDtypeStruct((16, DFF), F32)],
        compiler_params=_params(("parallel", "arbitrary")),
    )(F, F, w9, bias, df)


def _dnprep_math(y, is_qk, scale):
    s = _silu(y)
    n = s * lax.rsqrt(jnp.sum(s * s, axis=-1, keepdims=True) + EPS) * scale
    return jnp.where(is_qk, n, s)


def _dnprep_fwd(p, cw, *, B, T, nc, H, HD):
    def body(x_ref, w_ref, o_ref):
        j = pl.program_id(1)
        y = _conv_fwd(x_ref[...], w_ref[...], _seg_taps(T, nc, 4, 2))
        o_ref[...] = _dnprep_math(y, j < 2 * H, jnp.where(j < H, HD ** -0.5, 1.0))

    return pl.pallas_call(
        body, name="dnprep_fwd", grid=(B, 3 * H),
        in_specs=[pl.BlockSpec((T, HD), lambda b, j: (b, j)), pl.BlockSpec((4, HD), lambda b, j: (0, j))],
        out_specs=pl.BlockSpec((T, HD), lambda b, j: (b, j)),
        out_shape=jax.ShapeDtypeStruct((B * T, 3 * H * HD), F32),
        compiler_params=_params(("parallel", "parallel")),
    )(p, cw)


def _dnprep_bwd(p, cw, dqkv, dp, *, B, T, nc, H, HD):
    def body(x_ref, w_ref, d_ref, dp_any, dp_ref, dcw_ref):
        j = pl.program_id(0)
        taps = _seg_taps(T, nc, 4, 2)
        x = x_ref[...]
        y = _conv_fwd(x, w_ref[...], taps)
        is_qk, scale = j < 2 * H, jnp.where(j < H, HD ** -0.5, 1.0)
        _, vjp = jax.vjp(lambda a: _dnprep_math(a, is_qk, scale), y)
        (dy,) = vjp(d_ref[0])
        dx, dw = _conv_bwd(x, w_ref[...], taps, dy)
        dp_ref[...] = dx.astype(BF16)

        @pl.when(pl.program_id(1) == 0)
        def _():
            dcw_ref[...] = jnp.zeros_like(dcw_ref)

        dcw_ref[0:4, :] += dw

    col = pl.BlockSpec((T, HD), lambda j, b: (b, j))
    return pl.pallas_call(
        body, name="dnprep_bwd", grid=(3 * H, B),
        in_specs=[col, pl.BlockSpec((4, HD), lambda j, b: (0, j)),
                  pl.BlockSpec((1, T, HD), lambda j, b: (j // H, b, j % H)), pl.BlockSpec(memory_space=pl.ANY)],
        out_specs=[col, pl.BlockSpec((8, HD), lambda j, b: (0, j))],
        out_shape=[jax.ShapeDtypeStruct(dp.shape, dp.dtype), jax.ShapeDtypeStruct((8, 3 * H * HD), F32)],
        input_output_aliases={3: 0},
        compiler_params=_params(("parallel", "arbitrary")),
    )(p, cw, dqkv, dp)


def _gb_math(ab, alog, dtb, H):
    lane = lax.broadcasted_iota(jnp.int32, ab.shape, 1)
    g = -jnp.exp(alog) * _softplus(ab + dtb)
    return jnp.where(lane < 2 * H, g, jnp.where(lane < 4 * H, _sigmoid(ab), 0.0))


def _gb_fwd(p, prm, *, rows, col0, H, tm):
    def body(x_ref, prm_ref, o_ref):
        o_ref[...] = _gb_math(x_ref[...], prm_ref[0:1], prm_ref[1:2], H)

    return pl.pallas_call(
        body, name="gb_fwd", grid=(rows // tm,),
        in_specs=[pl.BlockSpec((tm, LANES), lambda t: (t, col0 // LANES)), pl.BlockSpec((8, LANES), lambda t: (0, 0))],
        out_specs=pl.BlockSpec((tm, LANES), lambda t: (t, 0)),
        out_shape=jax.ShapeDtypeStruct((rows, LANES), F32),
        compiler_params=_params(("parallel",)),
    )(p, prm)


def _gb_bwd(p, prm, dgb, dp, *, rows, col0, H, tm):
    def body(x_ref, prm_ref, d_ref, dp_any, dp_ref, dprm_ref):
        _, vjp = jax.vjp(lambda a, b, c: _gb_math(a, b, c, H), x_ref[...], prm_ref[0:1], prm_ref[1:2])
        dab, dalog, ddtb = vjp(d_ref[...])
        dp_ref[...] = dab.astype(BF16)

        @pl.when(pl.program_id(0) == 0)
        def _():
            dprm_ref[...] = jnp.zeros_like(dprm_ref)

        dprm_ref[0:1, :] += dalog
        dprm_ref[1:2, :] += ddtb

    blk = pl.BlockSpec((tm, LANES), lambda t: (t, col0 // LANES))
    return pl.pallas_call(
        body, name="gb_bwd", grid=(rows // tm,),
        in_specs=[blk, pl.BlockSpec((8, LANES), lambda t: (0, 0)), pl.BlockSpec((tm, LANES), lambda t: (t, 0)),
                  pl.BlockSpec(memory_space=pl.ANY)],
        out_specs=[blk, pl.BlockSpec((8, LANES), lambda t: (0, 0))],
        out_shape=[jax.ShapeDtypeStruct(dp.shape, dp.dtype), jax.ShapeDtypeStruct((8, LANES), F32)],
        input_output_aliases={3: 0},
        compiler_params=_params(("arbitrary",)),
    )(p, prm, dgb, dp)


def _lru_scans(scans):
    C = scans[0][0].shape[1]
    row = lax.broadcasted_iota(jnp.int32, (SUBLANES, C), 0)
    carries = tuple(jnp.zeros((1, C), F32) for _ in scans)
    for si in range(len(scans[0][4])):
        nb = scans[0][4][si][1] // SUBLANES
        assert all(sc[4][si][1] // SUBLANES == nb for sc in scans)

        def blk(i, carries, si=si, nb=nb):
            out = []
            for (a_ref, b_ref, h_ref, hp_ref, segs), carry in zip(scans, carries):
                start, _, reverse = segs[si]
                r0 = pl.multiple_of(start + (nb - 1 - i if reverse else i) * SUBLANES, SUBLANES)
                A = a_ref[pl.ds(r0, SUBLANES), :]
                Bv = b_ref[pl.ds(r0, SUBLANES), :]
                for s in (1, 2, 4):
                    sh = SUBLANES - s if reverse else s
                    m = (row < SUBLANES - s) if reverse else (row >= s)
                    Bv = jnp.where(m, A * pltpu.roll(Bv, sh, 0) + Bv, Bv)
                    A = jnp.where(m, A * pltpu.roll(A, sh, 0), A)
                Hv = Bv + A * carry
                h_ref[pl.ds(r0, SUBLANES), :] = Hv
                if hp_ref is not None:
                    if reverse:
                        hp = jnp.where(row < SUBLANES - 1, pltpu.roll(Hv, SUBLANES - 1, 0), carry)
                    else:
                        hp = jnp.where(row >= 1, pltpu.roll(Hv, 1, 0), carry)
                    hp_ref[pl.ds(r0, SUBLANES), :] = hp
                out.append(Hv[0:1] if reverse else Hv[SUBLANES - 1:SUBLANES])
            return tuple(out)

        carries = lax.fori_loop(0, nb, blk, carries)


def _lru_orders(T, nc, d):
    N = T - nc
    if d == 0:
        return [(0, nc, False), (nc, N, False)], [(nc, N, True), (0, nc, True)]
    return [(0, nc, True), (nc, N, True)], [(nc, N, False), (0, nc, False)]


def _bdot(a, b, dims=(((1,), (0,)), ((), ()))):
    return lax.dot_general(a.astype(BF16), b.astype(BF16), dims, preferred_element_type=F32)


_NT = (((1,), (1,)), ((), ()))
_TN = (((0,), (0,)), ((), ()))


def _blockdiag(w, C):
    nd, nb, bd, _ = w.shape
    per = C // bd
    out = jnp.einsum('dnpij,pq->dnpiqj', w.reshape(nd, nb // per, per, bd, bd), jnp.eye(per, dtype=w.dtype))
    return out.reshape(nd, nb // per, C, C)


def _blockdiag_extract(dw, bd):
    nd, nj, C, _ = dw.shape
    per = C // bd
    out = jnp.einsum('dnpiqj,pq->dnpij', dw.reshape(nd, nj, per, bd, per, bd), jnp.eye(per, dtype=dw.dtype))
    return out.reshape(nd, nj * per, bd, bd)


def _lru_fwd(p, cw, lv, wr, wi, *, B, T, nc, LW, col0, C):
    N = T - nc
    nj = LW // C

    def body(x_ref, cw_ref, lv_ref, wr_ref, wi_ref, o_ref, a_s, b_s, h_s):
        lv_ = lv_ref[...]
        xc = _conv_fwd(x_ref[:, 0:C], cw_ref[...], _seg_taps(T, nc, 4, 2)) + lv_[0:1]
        for d in (0, 1):
            r = _sigmoid(_bdot(xc, wr_ref[d, 0]) + lv_[1 + d:2 + d])
            i = _sigmoid(_bdot(xc, wi_ref[d, 0]) + lv_[3 + d:4 + d])
            la = -LRU_C * r * _softplus(-lv_[5 + d:6 + d])
            a_s[d] = jnp.exp(la)
            b_s[d] = jnp.sqrt(1.0 - jnp.exp(2.0 * la)) * i * xc
        _lru_scans([(a_s.at[d], b_s.at[d], h_s.at[d], None, _lru_orders(T, nc, d)[0]) for d in (0, 1)])
        o_ref[...] = ((h_s[0, nc:, :] + h_s[1, nc:, :]) * _gelu(x_ref[nc:, C:2 * C])).astype(BF16)

    return pl.pallas_call(
        body, name="lru_fwd", grid=(B, nj),
        in_specs=[pl.BlockSpec((T, 2 * C), lambda b, j: (b, col0 // (2 * C) + j)),
                  pl.BlockSpec((4, C), lambda b, j: (0, j)), pl.BlockSpec((8, C), lambda b, j: (0, j)),
                  pl.BlockSpec((2, 1, C, C), lambda b, j: (0, j, 0, 0)), pl.BlockSpec((2, 1, C, C), lambda b, j: (0, j, 0, 0))],
        out_specs=pl.BlockSpec((N, C), lambda b, j: (b, j)),
        out_shape=jax.ShapeDtypeStruct((B * N, LW), BF16),
        scratch_shapes=[pltpu.VMEM((2, T, C), F32)] * 3,
        compiler_params=_params(("parallel", "parallel")),
    )(p, cw, lv, wr, wi)


def _lru_bwd(p, cw, lv, wr, wi, dy, dp, *, B, T, nc, LW, col0, C):
    N = T - nc
    nj = LW // C

    def body(x_ref, cw_ref, lv_ref, wr_ref, wi_ref, dy_ref, dp_any, dp_ref, dcw_ref, dlv_ref, dwr_ref, dwi_ref,
             a_s, b_s, h_s, hp_s, mu_s, mup_s, dh_s, dxc_s):
        taps = _seg_taps(T, nc, 4, 2)
        lv_ = lv_ref[...]
        xl = x_ref[:, 0:C]
        xc = _conv_fwd(xl, cw_ref[...], taps) + lv_[0:1]
        gel, gelu_vjp = jax.vjp(_gelu, x_ref[nc:, C:2 * C])
        dh_s[0:nc, :] = jnp.zeros((nc, C), F32)
        dh_s[nc:, :] = dy_ref[...] * gel
        dxc_s[...] = jnp.zeros_like(dxc_s)

        @pl.when(pl.program_id(1) == 0)
        def _():
            dcw_ref[...] = jnp.zeros_like(dcw_ref)
            dlv_ref[...] = jnp.zeros_like(dlv_ref)
            dwr_ref[...] = jnp.zeros_like(dwr_ref)
            dwi_ref[...] = jnp.zeros_like(dwi_ref)

        def gates(d):
            lam = lv_[5 + d:6 + d]
            r = _sigmoid(_bdot(xc, wr_ref[d, 0]) + lv_[1 + d:2 + d])
            i = _sigmoid(_bdot(xc, wi_ref[d, 0]) + lv_[3 + d:4 + d])
            sp = _softplus(-lam)
            la = -LRU_C * r * sp
            e2 = jnp.exp(2.0 * la)
            return lam, r, i, sp, la, e2, jnp.sqrt(1.0 - e2)

        for d in (0, 1):
            _, _, i, _, la, _, mult = gates(d)
            a_s[d] = jnp.exp(la)
            b_s[d] = mult * i * xc
        _lru_scans([(a_s.at[d], b_s.at[d], h_s.at[d], hp_s.at[d], _lru_orders(T, nc, d)[0]) for d in (0, 1)])
        for d in (0, 1):
            b_s[d] = a_s[d] * dh_s[...]
        _lru_scans([(a_s.at[d], b_s.at[d], mu_s.at[d], mup_s.at[d], _lru_orders(T, nc, d)[1]) for d in (0, 1)])

        for d in (0, 1):
            lam, r, i, sp, la, e2, mult = gates(d)
            a = a_s[d]
            dinp = dh_s[...] + mup_s[d]
            da = dinp * hp_s[d]
            dmult = dinp * i * xc
            di = dinp * mult * xc
            dla = da * a - dmult * e2 / mult
            dpre_r = (dla * (-LRU_C * sp)) * r * (1.0 - r)
            dpre_i = di * i * (1.0 - i)
            dsp = jnp.sum(dla * (-LRU_C * r), axis=0, keepdims=True)
            dxc_s[...] += dinp * mult * i + _bdot(dpre_r, wr_ref[d, 0], _NT) + _bdot(dpre_i, wi_ref[d, 0], _NT)
            dwr_ref[d, 0] += _bdot(xc, dpre_r, _TN)
            dwi_ref[d, 0] += _bdot(xc, dpre_i, _TN)
            dlv_ref[1 + d:2 + d, :] += jnp.sum(dpre_r, axis=0, keepdims=True)
            dlv_ref[3 + d:4 + d, :] += jnp.sum(dpre_i, axis=0, keepdims=True)
            dlv_ref[5 + d:6 + d, :] += -dsp * _sigmoid(-lam)

        dxc = dxc_s[...]
        dxl, dw = _conv_bwd(xl, cw_ref[...], taps, dxc)
        dcw_ref[0:4, :] += dw
        dlv_ref[0:1, :] += jnp.sum(dxc, axis=0, keepdims=True)
        dp_ref[:, 0:C] = dxl.astype(BF16)
        (dyl,) = gelu_vjp(dy_ref[...] * (h_s[0, nc:, :] + h_s[1, nc:, :]))
        dp_ref[0:nc, C:2 * C] = jnp.zeros((nc, C), BF16)
        dp_ref[nc:, C:2 * C] = dyl.astype(BF16)

    xblk = pl.BlockSpec((T, 2 * C), lambda j, b: (b, col0 // (2 * C) + j))
    wblk = pl.BlockSpec((2, 1, C, C), lambda j, b: (0, j, 0, 0))
    vblk = pl.BlockSpec((8, C), lambda j, b: (0, j))
    return pl.pallas_call(
        body, name="lru_bwd", grid=(nj, B),
        in_specs=[xblk, pl.BlockSpec((4, C), lambda j, b: (0, j)), vblk, wblk, wblk,
                  pl.BlockSpec((N, C), lambda j, b: (b, j)), pl.BlockSpec(memory_space=pl.ANY)],
        out_specs=[xblk, vblk, vblk, wblk, wblk],
        out_shape=[jax.ShapeDtypeStruct(dp.shape, dp.dtype), jax.ShapeDtypeStruct((8, LW), F32),
                   jax.ShapeDtypeStruct((8, LW), F32), jax.ShapeDtypeStruct((2, nj, C, C), F32),
                   jax.ShapeDtypeStruct((2, nj, C, C), F32)],
        scratch_shapes=[pltpu.VMEM((2, T, C), F32)] * 6 + [pltpu.VMEM((T, C), F32)] * 2,
        input_output_aliases={6: 0},
        compiler_params=_params(("parallel", "arbitrary")),
    )(p, cw, lv, wr, wi, dy, dp)


def _chunk_masks(upper):
    i = lax.broadcasted_iota(jnp.int32, (CHUNK, CHUNK), 0)
    j = lax.broadcasted_iota(jnp.int32, (CHUNK, CHUNK), 1)
    ahead = jnp.where(upper, j - i, i - j)
    return i == j, ahead >= 0, ahead > 0


def _col2row(c, eye):
    return jnp.sum(jnp.where(eye, c, 0.0), axis=0, keepdims=True)


def _row2col(r, eye):
    return jnp.sum(jnp.where(eye, r, 0.0), axis=1, keepdims=True)


def _rowsum(x):
    return jnp.sum(x, axis=1, keepdims=True)


_INV_BASE = 8


def _unit_tri_inverses(Ls):
    G = len(Ls)
    W = G * CHUNK
    blk = (lax.broadcasted_iota(jnp.int32, (W, W), 0) // CHUNK) == (lax.broadcasted_iota(jnp.int32, (W, W), 1) // CHUNK)
    ri = lax.broadcasted_iota(jnp.int32, (CHUNK, W), 0)
    ci = lax.broadcasted_iota(jnp.int32, (CHUNK, W), 1) % CHUNK

    def bd(b):
        return jnp.where(blk, jnp.tile(b, (G, 1)), jnp.zeros((), BF16))

    def pdot(a, b):
        return jnp.dot(a.astype(BF16), bd(b.astype(BF16)), preferred_element_type=F32)

    Lc = Ls[0] if G == 1 else jnp.concatenate(Ls, axis=1)
    s = _INV_BASE
    Xp = -jnp.where(ri // s == ci // s, Lc, 0.0)
    Rm = Xp
    for _ in range(int(math.log2(s)) - 1):
        Xp = pdot(Xp, Xp)
        Rm = Rm + Xp + pdot(Rm, Xp)
    while s < CHUNK:
        E = jnp.where((ri // (2 * s) == ci // (2 * s)) & (ri // s != ci // s), Lc, 0.0)
        DE = E + pdot(Rm, E)
        Rm = Rm - (DE + pdot(DE, Rm))
        s *= 2
    eye = _chunk_masks(False)[0]
    return [jnp.where(eye, 1.0, 0.0) + Rm[:, g * CHUNK:(g + 1) * CHUNK] for g in range(G)]


def _delta_chunk_common(q, k, v, gcol, bcol, upper):
    eye, incl, strict = _chunk_masks(upper)
    gc = _rowsum(jnp.where(incl, _col2row(gcol, eye), 0.0))
    D = jnp.where(incl, jnp.exp(jnp.minimum(gc - _col2row(gc, eye), 0.0)), 0.0)
    kb = k * bcol
    AP = _bdot(jnp.concatenate([kb, q], axis=0), k, _NT)
    A = AP[:CHUNK]
    L = jnp.where(strict, A * D, 0.0)
    eg = jnp.exp(gc)
    gl = jnp.sum(gcol, axis=0, keepdims=True)
    attn = jnp.where(incl, AP[CHUNK:] * D, 0.0)
    return dict(eye=eye, incl=incl, strict=strict, gc=gc, D=D, kb=kb, A=A, L=L, eg=eg, gl=gl, egl=jnp.exp(gl),
                attn=attn, kbe=kb * eg, vb=v * bcol, qe=q * eg, kd=k * jnp.exp(gl - gc))


def _delta_group_pre(chunks, upper):
    cs = [_delta_chunk_common(*ch, upper) for ch in chunks]
    out = []
    for c, Tm in zip(cs, _unit_tri_inverses([c["L"] for c in cs])):
        dk = c["kbe"].shape[1]
        wu = _bdot(Tm, jnp.concatenate([c["kbe"], c["vb"]], axis=1))
        KN = _bdot(c["kd"], wu, _TN)
        QO = _bdot(c["attn"], wu)
        out.append((Tm, KN[:, :dk], KN[:, dk:], c["qe"] - QO[:, :dk], QO[:, dk:], c["egl"]))
    return out


def _delta_chunk_bwd(q, k, v, gcol, bcol, S, Tm, do, dS2, upper):
    c = _delta_chunk_common(q, k, v, gcol, bcol, upper)
    eye, incl, strict, D, eg, egl = c["eye"], c["incl"], c["strict"], c["D"], c["eg"], c["egl"]
    kb, kbe, vb, qe, kd, attn = c["kb"], c["kbe"], c["vb"], c["qe"], c["kd"], c["attn"]
    dkk = kbe.shape[1]
    wu = _bdot(Tm, jnp.concatenate([kbe, vb], axis=1))
    w = wu[:, :dkk]
    vn = wu[:, dkk:] - _bdot(w, S)
    dvn = _bdot(kd, dS2) + _bdot(attn, do, _TN)
    dkd = _bdot(vn, dS2, _NT)
    dgl = jnp.sum(_rowsum(dS2 * S), axis=0, keepdims=True) * egl
    dqa = _bdot(do, jnp.concatenate([S, vn], axis=0), _NT)
    dqe = dqa[:, :dkk]
    dattn = jnp.where(incl, dqa[:, dkk:], 0.0)
    dw = -_bdot(dvn, S, _NT)
    r = _rowsum(dkd * kd)
    dk = dkd * jnp.exp(c["gl"] - c["gc"])
    dgl = dgl + jnp.sum(r, axis=0, keepdims=True)
    dgc = _rowsum(dqe * qe) - r
    E = dattn * attn
    dvw = jnp.concatenate([dvn, dw], axis=1)
    dTm = _bdot(dvw, jnp.concatenate([vb, kbe], axis=1), _NT)
    dvk = _bdot(Tm, dvw, _TN)
    dvb = dvk[:, :dvn.shape[1]]
    dv = dvb * bcol
    dbeta = _rowsum(dvb * v)
    dkbe = dvk[:, dvn.shape[1]:]
    dkb = dkbe * eg
    dgc = dgc + _rowsum(dkbe * kbe)
    dL = jnp.where(strict, -_bdot(Tm, _bdot(dTm, Tm, _NT), _TN), 0.0)
    dA = dL * D
    E = E + dL * c["L"]
    PA = jnp.concatenate([dattn * D, dA], axis=0)
    PAk = _bdot(PA, k)
    dq = dqe * eg + PAk[:CHUNK]
    dkb = dkb + PAk[CHUNK:]
    dk = dk + _bdot(PA, jnp.concatenate([q, kb], axis=0), _TN) + dkb * bcol
    dbeta = dbeta + _rowsum(dkb * k)
    dgc = dgc + _rowsum(E) - _row2col(jnp.sum(E, axis=0, keepdims=True), eye)
    dg = _row2col(jnp.sum(jnp.where(incl, dgc, 0.0), axis=0, keepdims=True), eye) + dgl
    return dq, dk, dv, dg, dbeta


def _delta_unroll(trips):
    return max(u for u in (3, 2, 1) if trips % u == 0)


def _delta_group(n):
    return max(g for g in range(1, 2 * LANES // CHUNK + 1) if n % g == 0)


def _delta_chunk_at(T, nc, d, i):
    n, ncc = T // CHUNK, nc // CHUNK
    desc = jnp.where(i < ncc, ncc - 1 - i, n - 1 - (i - ncc))
    if isinstance(d, int):
        return i if d == 0 else desc
    return jnp.where(d == 0, i, desc)


def _dn_out_math(o, onorm, z):
    return _rmsn(o, onorm) * _silu(z)


def _delta_fwd(qkv, gb, p, onorm, *, B, T, nc, H, HD):
    N = T - nc
    n = T // CHUNK
    G = _delta_group(n)

    def body(q_ref, k_ref, v_ref, gb_ref, z_ref, on_ref, y_ref, o_ref, Tm_ref, K_ref, S_ref, Qp_ref, eg_ref,
             N_s, O0_s, o_s):
        h = pl.program_id(1)
        lane = lax.broadcasted_iota(jnp.int32, (CHUNK, LANES), 1)

        def pre(g, carry):
            cs = [g * G + i for i in range(G)]
            rows = [pl.ds(pl.multiple_of(c * CHUNK, CHUNK), CHUNK) for c in cs]
            for d in (0, 1):
                chunks = []
                for r in rows:
                    gbb = gb_ref[r, :]
                    chunks.append((q_ref[r, :], k_ref[r, :], v_ref[r, :],
                                   _rowsum(jnp.where(lane == d * H + h, gbb, 0.0)),
                                   _rowsum(jnp.where(lane == 2 * H + d * H + h, gbb, 0.0))))
                for c, r, (Tm, K, Nn, Qp, O0, egl) in zip(cs, rows, _delta_group_pre(chunks, d == 1)):
                    Tm_ref[0, d * n + c] = Tm
                    K_ref[0, d * n + c] = K.astype(BF16)
                    N_s[d * n + c] = Nn
                    Qp_ref[0, d, r, :] = Qp.astype(BF16)
                    O0_s[d, r, :] = O0
                    eg_ref[0, d * n + c] = jnp.broadcast_to(egl, (SUBLANES, HD))
            return carry

        lax.fori_loop(0, n // G, pre, 0)

        def step(i, Ss):
            out = []
            for d in (0, 1):
                c = _delta_chunk_at(T, nc, d, i)
                rows = pl.ds(pl.multiple_of(c * CHUNK, CHUNK), CHUNK)
                S_ref[0, d * n + c] = Ss[d]
                Sb = Ss[d].astype(BF16)
                o_s[d, rows, :] = jnp.dot(Qp_ref[0, d, rows, :], Sb, preferred_element_type=F32) + O0_s[d, rows, :]
                out.append(eg_ref[0, d * n + c][0:1] * Ss[d] + N_s[d * n + c]
                           - jnp.dot(K_ref[0, d * n + c], Sb, preferred_element_type=F32))
            return tuple(out)

        lax.fori_loop(0, n, step, (jnp.zeros((HD, HD), F32), jnp.zeros((HD, HD), F32)))
        o = o_s[0, nc:, :] + o_s[1, nc:, :]
        o_ref[...] = o
        y_ref[...] = _dn_out_math(o, on_ref[...], z_ref[nc:, :]).astype(BF16)

    col = lambda off: pl.BlockSpec((T, HD), lambda b, h: (b, off + h))
    lat = pl.BlockSpec((N, HD), lambda b, h: (b, h))
    per = lambda *blk: pl.BlockSpec((1, *blk), lambda b, h: (b * H + h, 0, 0, 0))
    return pl.pallas_call(
        body, name="delta_fwd", grid=(B, H),
        in_specs=[col(0), col(H), col(2 * H), pl.BlockSpec((T, LANES), lambda b, h: (b, 0)), col(3 * H),
                  pl.BlockSpec((1, HD), lambda b, h: (0, 0))],
        out_specs=[lat, lat, per(2 * n, CHUNK, CHUNK), per(2 * n, HD, HD), per(2 * n, HD, HD), per(2, T, HD),
                   per(2 * n, SUBLANES, HD)],
        out_shape=[jax.ShapeDtypeStruct((B * N, H * HD), BF16), jax.ShapeDtypeStruct((B * N, H * HD), F32),
                   jax.ShapeDtypeStruct((B * H, 2 * n, CHUNK, CHUNK), F32),
                   jax.ShapeDtypeStruct((B * H, 2 * n, HD, HD), BF16), jax.ShapeDtypeStruct((B * H, 2 * n, HD, HD), F32),
                   jax.ShapeDtypeStruct((B * H, 2, T, HD), BF16), jax.ShapeDtypeStruct((B * H, 2 * n, SUBLANES, HD), F32)],
        scratch_shapes=[pltpu.VMEM((2 * n, HD, HD), F32), pltpu.VMEM((2, T, HD), F32), pltpu.VMEM((2, T, HD), F32)],
        compiler_params=_params(("parallel", "parallel")),
    )(qkv, qkv, qkv, gb, p, onorm)


def _delta_bwd(qkv, gb, p, onorm, o, res, dy, dp, *, B, T, nc, H, HD):
    N = T - nc
    n = T // CHUNK

    def body(q_ref, k_ref, v_ref, gb_ref, z_ref, on_ref, o_ref, dy_ref, Tm_ref, K_ref, S_ref, Qp_ref, eg_ref, dp_any,
             dqkv_ref, dgb_ref, dp_ref, don_ref, do_s, R_s, dS_s):
        h, d = pl.program_id(1), pl.program_id(2)
        lane = lax.broadcasted_iota(jnp.int32, (CHUNK, LANES), 1)

        @pl.when(d == 0)
        def _():
            _, vjp = jax.vjp(_dn_out_math, o_ref[...], on_ref[...], z_ref[nc:, :])
            do, don, dz = vjp(dy_ref[...])
            do_s[0:nc, :] = jnp.zeros((nc, HD), F32)
            do_s[nc:, :] = do
            dp_ref[0:nc, :] = jnp.zeros((nc, HD), BF16)
            dp_ref[nc:, :] = dz.astype(BF16)
            dqkv_ref[...] = jnp.zeros_like(dqkv_ref)

            @pl.when(h == 0)
            def _():
                don_ref[...] = jnp.zeros_like(don_ref)
                dgb_ref[...] = jnp.zeros_like(dgb_ref)

            don_ref[0, 0:1, :] += don

        def r_of(c, carry):
            rows = pl.ds(pl.multiple_of(c * CHUNK, CHUNK), CHUNK)
            R_s[c] = lax.dot_general(Qp_ref[0, 0, rows, :], do_s[rows, :].astype(BF16), _TN, preferred_element_type=F32)
            return carry

        lax.fori_loop(0, n, r_of, 0)

        def bwd_step(i, dS):
            c = _delta_chunk_at(T, nc, d, n - 1 - i)
            dS_s[c] = dS
            return (eg_ref[0, c][0:1] * dS + R_s[c]
                    - lax.dot_general(K_ref[0, c], dS.astype(BF16), _TN, preferred_element_type=F32))

        lax.fori_loop(0, n, bwd_step, jnp.zeros((HD, HD), F32))

        def grads(c, carry):
            rows = pl.ds(pl.multiple_of(c * CHUNK, CHUNK), CHUNK)
            gbb = gb_ref[rows, :]
            gcol = _rowsum(jnp.where(lane == d * H + h, gbb, 0.0))
            bcol = _rowsum(jnp.where(lane == 2 * H + d * H + h, gbb, 0.0))
            dq, dk, dv, dg, dbeta = _delta_chunk_bwd(q_ref[rows, :], k_ref[rows, :], v_ref[rows, :], gcol, bcol,
                                                     S_ref[0, c], Tm_ref[0, c], do_s[rows, :], dS_s[c], d == 1)
            dqkv_ref[0, rows, :] += dq
            dqkv_ref[1, rows, :] += dk
            dqkv_ref[2, rows, :] += dv
            dgb_ref[rows, :] += (jnp.where(lane == d * H + h, dg, 0.0)
                                 + jnp.where(lane == 2 * H + d * H + h, dbeta, 0.0))
            return carry

        lax.fori_loop(0, n, grads, 0, unroll=_delta_unroll(n))

    col = lambda off: pl.BlockSpec((T, HD), lambda b, h, d: (b, off + h))
    lat = pl.BlockSpec((N, HD), lambda b, h, d: (b, h))
    per = lambda *blk: pl.BlockSpec((1, *blk), lambda b, h, d: (b * H + h, d, 0, 0))
    return pl.pallas_call(
        body, name="delta_bwd", grid=(B, H, 2),
        in_specs=[col(0), col(H), col(2 * H), pl.BlockSpec((T, LANES), lambda b, h, d: (b, 0)), col(3 * H),
                  pl.BlockSpec((1, HD), lambda b, h, d: (0, 0)), lat, lat,
                  per(n, CHUNK, CHUNK), per(n, HD, HD), per(n, HD, HD), per(1, T, HD), per(n, SUBLANES, HD),
                  pl.BlockSpec(memory_space=pl.ANY)],
        out_specs=[pl.BlockSpec((3, T, HD), lambda b, h, d: (0, b, h)), pl.BlockSpec((T, LANES), lambda b, h, d: (b, 0)),
                   col(3 * H), pl.BlockSpec((1, 8, HD), lambda b, h, d: (b, 0, 0))],
        out_shape=[jax.ShapeDtypeStruct((3, B * T, H * HD), F32), jax.ShapeDtypeStruct((B * T, LANES), F32),
                   jax.ShapeDtypeStruct(dp.shape, dp.dtype), jax.ShapeDtypeStruct((B, 8, HD), F32)],
        scratch_shapes=[pltpu.VMEM((T, HD), F32), pltpu.VMEM((n, HD, HD), F32), pltpu.VMEM((n, HD, HD), F32)],
        input_output_aliases={13: 2},
        compiler_params=_params(("parallel", "arbitrary", "arbitrary")),
    )(qkv, qkv, qkv, gb, p, onorm, o, dy, *res, dp)


def _rowwise(fn, ins, out_dtypes, *, name, tm=256, mult=16):
    R, W = ins[0].shape
    tm = _tile(R, tm, mult)

    def body(*refs):
        outs = fn(*[r[...] for r in refs[:len(ins)]])
        for o_ref, o in zip(refs[len(ins):], outs):
            o_ref[...] = o.astype(o_ref.dtype)

    spec = pl.BlockSpec((tm, W), lambda i: (i, 0))
    return pl.pallas_call(
        body, name=name, grid=(R // tm,), in_specs=[spec] * len(ins), out_specs=[spec] * len(out_dtypes),
        out_shape=[jax.ShapeDtypeStruct((R, W), dt) for dt in out_dtypes],
        compiler_params=_params(("parallel",)),
    )(*ins)


def _sum_lead(x, *, name, tm=256, mult=16):
    S, R, W = x.shape
    tm = _tile(R, tm, mult)

    def body(*refs):
        acc = refs[0][0].astype(F32)
        for r in refs[1:S]:
            acc = acc + r[0].astype(F32)
        refs[S][...] = acc

    return pl.pallas_call(
        body, name=name, grid=(R // tm,),
        in_specs=[pl.BlockSpec((1, tm, W), functools.partial(lambda s, i: (s, i, 0), s)) for s in range(S)],
        out_specs=pl.BlockSpec((tm, W), lambda i: (i, 0)),
        out_shape=jax.ShapeDtypeStruct((R, W), F32),
        compiler_params=_params(("parallel",)),
    )(*([x] * S))


def _adamw_math(w, g, m, v):
    m = ADAM_B1 * m + (1.0 - ADAM_B1) * g
    v = ADAM_B2 * v + (1.0 - ADAM_B2) * (g * g)
    m_hat = m / (1.0 - ADAM_B1 ** ADAM_STEP)
    v_hat = v / (1.0 - ADAM_B2 ** ADAM_STEP)
    return -ADAM_LR * (m_hat / (jnp.sqrt(v_hat) + ADAM_EPS) + ADAM_WD * w), m, v


def _adamw(w, g, m, v, *, name):
    tm = max(SUBLANES, (256 * 1024) // w.shape[1] // SUBLANES * SUBLANES)
    return _rowwise(_adamw_math, [w, g, m, v], [F32, F32, F32], name=name, tm=tm, mult=SUBLANES)


def _me():
    return lax.axis_index("x"), lax.axis_index("y"), lax.axis_index("c")


def _allgather_small(v):
    R, W = v.shape

    def body(x_ref, out_ref, send_sems, recv_sems, local_sem):
        x, y, c = _me()
        me, sibling = (x, y, c), (x, y, 1 - c)
        chips = [(1 - x, y), (x, 1 - y), (1 - x, 1 - y)]

        def slot(px, py, pc):
            return out_ref.at[4 * px + 2 * py + pc]

        def copy(k, block, to, src=None):
            return pltpu.make_async_remote_copy(
                src_ref=slot(*block) if src is None else src, dst_ref=slot(*block),
                send_sem=send_sems.at[k], recv_sem=recv_sems.at[k], device_id=to, device_id_type=MESH)

        mine = pltpu.make_async_copy(x_ref, slot(*me), local_sem)
        mine.start()
        first = [copy(0, me, sibling, src=x_ref)]
        first += [copy(1 + j, me, (*chip, c), src=x_ref) for j, chip in enumerate(chips)]
        for cp in first:
            cp.start()
        passed = [copy(4 + j, (*chip, c), sibling) for j, chip in enumerate(chips)]
        for j, chip in enumerate(chips):
            copy(1 + j, (*chip, c), me).wait_recv()
            passed[j].start()
        copy(0, sibling, me).wait_recv()
        for j, chip in enumerate(chips):
            copy(4 + j, (*chip, 1 - c), me).wait_recv()
        for cp in first + passed:
            cp.wait_send()
        mine.wait()

    return pl.pallas_call(
        body, name="allgather_small", out_shape=jax.ShapeDtypeStruct((8, R, W), v.dtype),
        in_specs=[pl.BlockSpec(memory_space=pltpu.VMEM)], out_specs=pl.BlockSpec(memory_space=pltpu.VMEM),
        scratch_shapes=[pltpu.SemaphoreType.DMA((7,)), pltpu.SemaphoreType.DMA((7,)), pltpu.SemaphoreType.DMA],
        compiler_params=_params(),
    )(v)


_ANY = pl.BlockSpec(memory_space=pl.ANY)


def _allgather_halves(shards, *, name):
    nw = len(shards)

    def body(*refs):
        x_refs, out_refs = refs[:nw], refs[nw:2 * nw]
        send_sems, recv_sems, local_sems = refs[2 * nw:]
        x, y, c = _me()
        me, sibling = (x, y, c), (x, y, 1 - c)
        chips = [(1 - x, y), (x, 1 - y), (1 - x, 1 - y)]

        def slot(w, px, py, pc):
            return out_refs[w].at[4 * px + 2 * py + pc]

        def copy(w, k, block, to, src=None):
            return pltpu.make_async_remote_copy(
                src_ref=slot(w, *block) if src is None else src, dst_ref=slot(w, *block),
                send_sem=send_sems.at[w, k], recv_sem=recv_sems.at[w, k], device_id=to, device_id_type=MESH)

        started, local = [], []
        for w in range(nw):
            half = shards[w].shape[0] // 2
            own = x_refs[w].at[pl.ds(c * half, half), :]
            mine = pltpu.make_async_copy(own, slot(w, *me), local_sems.at[w])
            mine.start()
            first = [copy(w, 0, me, sibling, src=own)]
            first += [copy(w, 1 + j, me, (*chip, c), src=own) for j, chip in enumerate(chips)]
            for cp in first:
                cp.start()
            started += first
            local.append(mine)
        for w in range(nw):
            for j, chip in enumerate(chips):
                copy(w, 1 + j, (*chip, c), me).wait_recv()
                fwd = copy(w, 4 + j, (*chip, c), sibling)
                fwd.start()
                started.append(fwd)
        for w in range(nw):
            copy(w, 0, sibling, me).wait_recv()
            for j, chip in enumerate(chips):
                copy(w, 4 + j, (*chip, 1 - c), me).wait_recv()
        for cp in started:
            cp.wait_send()
        for cp in local:
            cp.wait()

    return pl.pallas_call(
        body, name=name,
        out_shape=[jax.ShapeDtypeStruct((8, s.shape[0] // 2, s.shape[1]), s.dtype) for s in shards],
        in_specs=[_ANY] * nw, out_specs=[_ANY] * nw,
        scratch_shapes=[pltpu.SemaphoreType.DMA((nw, 7)), pltpu.SemaphoreType.DMA((nw, 7)), pltpu.SemaphoreType.DMA((nw,))],
        compiler_params=_params(),
    )(*shards)


def _sibling_send_halves(arrs, *, name):
    nw = len(arrs)

    def body(*refs):
        x_refs, out_refs, send_sems, recv_sems = refs[:nw], refs[nw:2 * nw], refs[2 * nw], refs[2 * nw + 1]
        x, y, c = _me()
        cps = []
        for w in range(nw):
            half = arrs[w].shape[1] // 2
            cp = pltpu.make_async_remote_copy(
                src_ref=x_refs[w].at[:, pl.ds((1 - c) * half, half), :], dst_ref=out_refs[w],
                send_sem=send_sems.at[w], recv_sem=recv_sems.at[w], device_id=(x, y, 1 - c), device_id_type=MESH)
            cp.start()
            cps.append(cp)
        for cp in cps:
            cp.wait()

    return pl.pallas_call(
        body, name=name,
        out_shape=[jax.ShapeDtypeStruct((a.shape[0], a.shape[1] // 2, a.shape[2]), a.dtype) for a in arrs],
        in_specs=[_ANY] * nw, out_specs=[_ANY] * nw,
        scratch_shapes=[pltpu.SemaphoreType.DMA((nw,)), pltpu.SemaphoreType.DMA((nw,))],
        compiler_params=_params(),
    )(*arrs)


def _sibling_swap(arrs, *, name):
    nw = len(arrs)

    def body(*refs):
        x_refs, out_refs, send_sems, recv_sems = refs[:nw], refs[nw:2 * nw], refs[2 * nw], refs[2 * nw + 1]
        x, y, c = _me()
        cps = []
        for w in range(nw):
            cp = pltpu.make_async_remote_copy(
                src_ref=x_refs[w], dst_ref=out_refs[w], send_sem=send_sems.at[w], recv_sem=recv_sems.at[w],
                device_id=(x, y, 1 - c), device_id_type=MESH)
            cp.start()
            cps.append(cp)
        for cp in cps:
            cp.wait()

    return pl.pallas_call(
        body, name=name, out_shape=[jax.ShapeDtypeStruct(a.shape, a.dtype) for a in arrs],
        in_specs=[_ANY] * nw, out_specs=[_ANY] * nw,
        scratch_shapes=[pltpu.SemaphoreType.DMA((nw,)), pltpu.SemaphoreType.DMA((nw,))],
        compiler_params=_params(),
    )(*arrs)


def _adamw_halves(w, own, sib, m, v, c_arr, *, name):
    r, cols = w.shape
    h = r // 2
    tm = _tile(h, max(SUBLANES, (192 * 1024) // cols // SUBLANES * SUBLANES), SUBLANES)
    nb = h // tm

    def body(c_ref, w_ref, own_ref, sib_ref, m_ref, v_ref, g_out, d_out, m_out, v_out):
        g = jnp.where(pl.program_id(0) == c_ref[0], own_ref[...], sib_ref[...])
        g_out[...] = g
        d_out[...], m_out[...], v_out[...] = _adamw_math(w_ref[...], g, m_ref[...], v_ref[...])

    full = pl.BlockSpec((tm, cols), lambda hh, i, c_ref: (hh * nb + i, 0))
    half = pl.BlockSpec((tm, cols), lambda hh, i, c_ref: (i, 0))
    return pl.pallas_call(
        body, name=name,
        grid_spec=pltpu.PrefetchScalarGridSpec(num_scalar_prefetch=1, grid=(2, nb),
                                               in_specs=[full, half, half, full, full], out_specs=[full] * 4),
        out_shape=[jax.ShapeDtypeStruct((r, cols), F32)] * 4,
        compiler_params=_params(("parallel", "parallel")),
    )(c_arr, w, own, sib, m, v)


_HBM = pl.BlockSpec(memory_space=pltpu.HBM)
_SEM = pl.BlockSpec(memory_space=pltpu.SEMAPHORE)
_DATAFLOW = pltpu.SideEffectType.DATAFLOW_SIDE_EFFECTING


def _chip_exchange_start(arrs, *, name):
    nw = len(arrs)

    def body(*refs):
        x_refs, land_refs, send_sems, recv_sems = refs[:nw], refs[nw:2 * nw], refs[2 * nw], refs[2 * nw + 1]
        token = refs[-1]
        x, y, c = _me()
        s_me = 2 * x + y
        for w in range(nw):
            for k, (px, py) in enumerate([(1 - x, y), (x, 1 - y), (1 - x, 1 - y)]):
                pltpu.make_async_remote_copy(
                    src_ref=x_refs[w].at[2 * px + py], dst_ref=land_refs[w].at[s_me], send_sem=send_sems.at[3 * w + k],
                    recv_sem=recv_sems.at[3 * w + k], device_id=(px, py, c), device_id_type=MESH).start()
        token[...] = jnp.zeros_like(token)

    hbm = [pltpu.HBM(a.shape, a.dtype) for a in arrs]
    outs = pl.pallas_call(
        body, name=name,
        out_shape=(pltpu.SemaphoreType.DMA((3 * nw,)), pltpu.SemaphoreType.DMA((3 * nw,)), *hbm, *hbm,
                   jax.ShapeDtypeStruct((SUBLANES, LANES), F32)),
        in_specs=[_HBM] * (2 * nw), out_specs=(_SEM, _SEM, *([_HBM] * (2 * nw)), pl.BlockSpec(memory_space=pltpu.VMEM)),
        input_output_aliases={i: 2 + i for i in range(2 * nw)},
        compiler_params=pltpu.CompilerParams(has_side_effects=_DATAFLOW),
    )(*[pltpu.with_memory_space_constraint(a, pltpu.HBM) for a in arrs],
      *[pltpu.with_memory_space_constraint(lax.empty(a.shape, a.dtype), pltpu.HBM) for a in arrs])
    return outs[0], outs[1], list(outs[2:2 + nw]), list(outs[2 + nw:2 + 2 * nw]), outs[-1]


def _allgather_start(shards, *, name):
    nw = len(shards)

    def body(*refs):
        x_refs, land_refs, send_sems, recv_sems = refs[:nw], refs[nw:2 * nw], refs[2 * nw], refs[2 * nw + 1]
        token = refs[-1]
        x, y, c = _me()
        me = 4 * x + 2 * y + c
        for w in range(nw):
            half = shards[w].shape[0] // 2
            own = x_refs[w].at[pl.ds(c * half, half), :]
            for k, to in enumerate([(x, y, 1 - c), (1 - x, y, c), (x, 1 - y, c), (1 - x, 1 - y, c)]):
                pltpu.make_async_remote_copy(
                    src_ref=own, dst_ref=land_refs[w].at[me], send_sem=send_sems.at[4 * w + k],
                    recv_sem=recv_sems.at[4 * w + k], device_id=to, device_id_type=MESH).start()
        token[...] = jnp.zeros_like(token)

    lands = [pltpu.HBM((8, s.shape[0] // 2, s.shape[1]), s.dtype) for s in shards]
    outs = pl.pallas_call(
        body, name=name,
        out_shape=(pltpu.SemaphoreType.DMA((4 * nw,)), pltpu.SemaphoreType.DMA((4 * nw,)),
                   *[pltpu.HBM(s.shape, s.dtype) for s in shards], *lands, jax.ShapeDtypeStruct((SUBLANES, LANES), F32)),
        in_specs=[_HBM] * (2 * nw), out_specs=(_SEM, _SEM, *([_HBM] * (2 * nw)), pl.BlockSpec(memory_space=pltpu.VMEM)),
        input_output_aliases={i: 2 + i for i in range(2 * nw)},
        compiler_params=pltpu.CompilerParams(has_side_effects=_DATAFLOW),
    )(*[pltpu.with_memory_space_constraint(s, pltpu.HBM) for s in shards],
      *[pltpu.with_memory_space_constraint(lax.empty(l.shape, l.dtype), pltpu.HBM) for l in lands])
    return outs[0], outs[1], list(outs[2:2 + nw]), list(outs[2 + nw:2 + 2 * nw]), outs[-1]


def _allgather_wait(send_sems, recv_sems, srcs, lands, after, *, name):
    nw = len(srcs)

    def body(*refs):
        x_refs, land_refs, send_sems, recv_sems = refs[:nw], refs[nw:2 * nw], refs[2 * nw], refs[2 * nw + 1]
        x, y, c = _me()
        for w in range(nw):
            half = srcs[w].shape[0] // 2
            own = x_refs[w].at[pl.ds(c * half, half), :]
            for k, (px, py, pc) in enumerate([(x, y, 1 - c), (1 - x, y, c), (x, 1 - y, c), (1 - x, 1 - y, c)]):
                cp = pltpu.make_async_remote_copy(
                    src_ref=own, dst_ref=land_refs[w].at[4 * px + 2 * py + pc], send_sem=send_sems.at[4 * w + k],
                    recv_sem=recv_sems.at[4 * w + k], device_id=(px, py, pc), device_id_type=MESH)
                cp.wait_send()
                cp.wait_recv()

    outs = pl.pallas_call(
        body, name=name,
        out_shape=(*[pltpu.HBM(a.shape, a.dtype) for a in srcs], *[pltpu.HBM(a.shape, a.dtype) for a in lands]),
        in_specs=[_HBM] * (2 * nw) + [_SEM, _SEM, _ANY], out_specs=tuple([_HBM] * (2 * nw)),
        input_output_aliases={i: i for i in range(2 * nw)},
        compiler_params=pltpu.CompilerParams(has_side_effects=_DATAFLOW),
    )(*srcs, *lands, send_sems, recv_sems, after)
    return list(outs[:nw]), list(outs[nw:])


def _pass_to_sibling(lands, *, name):
    nw = len(lands)

    def body(*refs):
        x_refs, out_refs, send_sems, recv_sems = refs[:nw], refs[nw:2 * nw], refs[2 * nw], refs[2 * nw + 1]
        x, y, c = _me()
        chips = [(1 - x, y), (x, 1 - y), (1 - x, 1 - y)]
        cps = []
        for w in range(nw):
            for k, (px, py) in enumerate(chips):
                cp = pltpu.make_async_remote_copy(
                    src_ref=x_refs[w].at[4 * px + 2 * py + c], dst_ref=out_refs[w].at[4 * px + 2 * py + c],
                    send_sem=send_sems.at[3 * w + k], recv_sem=recv_sems.at[3 * w + k], device_id=(x, y, 1 - c),
                    device_id_type=MESH)
                cp.start()
                cps.append(cp)
        for w in range(nw):
            for k, (px, py) in enumerate(chips):
                pltpu.make_async_remote_copy(
                    src_ref=x_refs[w].at[4 * px + 2 * py + c], dst_ref=out_refs[w].at[4 * px + 2 * py + 1 - c],
                    send_sem=send_sems.at[3 * w + k], recv_sem=recv_sems.at[3 * w + k], device_id=(x, y, 1 - c),
                    device_id_type=MESH).wait_recv()
        for cp in cps:
            cp.wait_send()

    return pl.pallas_call(
        body, name=name, out_shape=[jax.ShapeDtypeStruct(a.shape, a.dtype) for a in lands],
        in_specs=[_ANY] * nw, out_specs=[_ANY] * nw, input_output_aliases={i: i for i in range(nw)},
        scratch_shapes=[pltpu.SemaphoreType.DMA((3 * nw,)), pltpu.SemaphoreType.DMA((3 * nw,))],
        compiler_params=_params(),
    )(*lands)


def _chip_exchange_wait(send_sems, recv_sems, srcs, lands, after, *, name):
    nw = len(srcs)

    def body(*refs):
        x_refs, land_refs, send_sems, recv_sems = refs[:nw], refs[nw:2 * nw], refs[2 * nw], refs[2 * nw + 1]
        x, y, c = _me()
        for w in range(nw):
            for k, (px, py) in enumerate([(1 - x, y), (x, 1 - y), (1 - x, 1 - y)]):
                cp = pltpu.make_async_remote_copy(
                    src_ref=x_refs[w].at[2 * px + py], dst_ref=land_refs[w].at[2 * px + py], send_sem=send_sems.at[3 * w + k],
                    recv_sem=recv_sems.at[3 * w + k], device_id=(px, py, c), device_id_type=MESH)
                cp.wait_send()
                cp.wait_recv()

    hbm = [pltpu.HBM(a.shape, a.dtype) for a in srcs]
    outs = pl.pallas_call(
        body, name=name, out_shape=(*hbm, *hbm),
        in_specs=[_HBM] * (2 * nw) + [_SEM, _SEM, _ANY], out_specs=tuple([_HBM] * (2 * nw)),
        input_output_aliases={i: i for i in range(2 * nw)},
        compiler_params=pltpu.CompilerParams(has_side_effects=_DATAFLOW),
    )(*srcs, *lands, send_sems, recv_sems, after)
    return list(outs[:nw]), list(outs[nw:])


def _sum_slabs(landed, own_src, s_arr, after, *, name, tm=512):
    S, h, w = landed.shape
    tm = _tile(h, tm, 16)

    def body(s_ref, *refs):
        own = refs[S][0].astype(F32)
        acc = None
        for s in range(S):
            term = jnp.where(s_ref[0] == s, own, refs[s][0].astype(F32))
            acc = term if acc is None else acc + term
        refs[S + 2][...] = acc

    def slab(s):
        return pl.BlockSpec((1, tm, w), lambda i, s_ref: (jnp.where(s_ref[0] == s, (s + 1) % S, s), i, 0))

    return pl.pallas_call(
        body, name=name,
        grid_spec=pltpu.PrefetchScalarGridSpec(
            num_scalar_prefetch=1, grid=(h // tm,),
            in_specs=[slab(s) for s in range(S)] + [pl.BlockSpec((1, tm, w), lambda i, s_ref: (s_ref[0], i, 0)), _ANY],
            out_specs=pl.BlockSpec((tm, w), lambda i, s_ref: (i, 0))),
        out_shape=jax.ShapeDtypeStruct((h, w), F32),
        compiler_params=_params(("parallel",)),
    )(s_arr, *([landed] * S), own_src, after)


def _half_add(g, recv, c_arr, *, name):
    S, r, w = g.shape
    h = r // 2
    tm = _tile(h, 512, 16)
    nb = h // tm

    def body(c_ref, g_ref, r_ref, o_ref):
        o_ref[...] = (g_ref[...] + r_ref[...]).astype(BF16)

    return pl.pallas_call(
        body, name=name,
        grid_spec=pltpu.PrefetchScalarGridSpec(
            num_scalar_prefetch=1, grid=(S, nb),
            in_specs=[pl.BlockSpec((1, tm, w), lambda s, i, c_ref: (s, c_ref[0] * nb + i, 0)),
                      pl.BlockSpec((1, tm, w), lambda s, i, c_ref: (s, i, 0))],
            out_specs=pl.BlockSpec((1, tm, w), lambda s, i, c_ref: (s, i, 0))),
        out_shape=jax.ShapeDtypeStruct((S, h, w), BF16),
        compiler_params=_params(("parallel", "parallel")),
    )(c_arr, g, recv)


def _layout(sizes, width, part_mult, total_mult):
    offs, rows, r = [], [], 0
    for n in sizes:
        k = -(-n // width)
        offs.append(r)
        rows.append(k)
        r += -(-k // part_mult) * part_mult
    return offs, rows, -(-r // total_mult) * total_mult


def _pack(arrs, width, part_mult, total_mult, dtype, lead=()):
    nl = len(lead)
    sizes = [math.prod(a.shape[nl:]) for a in arrs]
    offs, rows, total = _layout(sizes, width, part_mult, total_mult)
    parts, r = [], 0
    for a, n, o, k in zip(arrs, sizes, offs, rows):
        kp = -(-k // part_mult) * part_mult
        flat = a.reshape(*lead, n).astype(dtype)
        if kp * width > n:
            flat = jnp.pad(flat, [(0, 0)] * nl + [(0, kp * width - n)])
        parts.append(flat.reshape(*lead, kp, width))
        r = o + kp
    if total > r:
        parts.append(jnp.zeros((*lead, total - r, width), dtype))
    return jnp.concatenate(parts, axis=nl)


def _unpack(pool, shapes, width, part_mult, total_mult):
    lead = pool.shape[:-2]
    sizes = [math.prod(s) for s in shapes]
    offs, rows, _ = _layout(sizes, width, part_mult, total_mult)
    out = []
    for s, n, o, k in zip(shapes, sizes, offs, rows):
        flat = lax.slice_in_dim(pool, o, o + k, axis=len(lead)).reshape(*lead, k * width)
        out.append(lax.slice_in_dim(flat, 0, n, axis=len(lead)).reshape(*lead, *s))
    return out


_WEIGHTS = ("c_ctx", "w_ada", "b_ada", "g_pre_mix", "g_post_mix", "g_pre_ffn", "g_post_ffn", "w_in", "b_merge",
            "dn_conv", "dn_a_log", "dn_dt_bias", "dn_onorm", "lru_conv", "lru_conv_b", "lru_w_rg", "lru_b_rg",
            "lru_w_ig", "lru_b_ig", "lru_lambda", "w_branch_dn", "w_branch_lru", "w_out", "w_up", "ffn_dw",
            "ffn_dw_b", "w_down")
_BIG = {"w_ada": True, "w_in": True, "w_branch_dn": False, "w_branch_lru": False, "w_out": False, "w_up": True,
        "w_down": False}
_SMALL_SHARDED = ("dn_conv", "lru_conv", "lru_b_rg", "lru_b_ig", "lru_lambda", "ffn_dw")
_NCHIP = 4
_FLAT_PART = 8
_FLAT_TOTAL = 256


def _to_chip_shards(g, by_cols):
    if by_cols:
        return g.reshape(g.shape[0], _NCHIP, g.shape[1] // _NCHIP).transpose(1, 0, 2)
    return g.reshape(_NCHIP, g.shape[0] // _NCHIP, g.shape[1])


def _from_chip_shards(s, by_cols):
    if by_cols:
        return s.transpose(1, 0, 2).reshape(s.shape[1], _NCHIP * s.shape[2])
    return s.reshape(_NCHIP * s.shape[1], s.shape[2])


def _dsilu(x):
    s = _sigmoid(x)
    return s * (1.0 + x * (1.0 - s))


def kernel(x, c, ctx, c_ctx, w_ada, b_ada, g_pre_mix, g_post_mix, g_pre_ffn, g_post_ffn, w_in, b_merge, dn_conv, dn_a_log, dn_dt_bias, dn_onorm, lru_conv, lru_conv_b, lru_w_rg, lru_b_rg, lru_w_ig, lru_b_ig, lru_lambda, w_branch_dn, w_branch_lru, w_out, w_up, ffn_dw, ffn_dw_b, w_down, loss_target, m_c_ctx, m_w_ada, m_b_ada, m_g_pre_mix, m_g_post_mix, m_g_pre_ffn, m_g_post_ffn, m_w_in, m_b_merge, m_dn_conv, m_dn_a_log, m_dn_dt_bias, m_dn_onorm, m_lru_conv, m_lru_conv_b, m_lru_w_rg, m_lru_b_rg, m_lru_w_ig, m_lru_b_ig, m_lru_lambda, m_w_branch_dn, m_w_branch_lru, m_w_out, m_w_up, m_ffn_dw, m_ffn_dw_b, m_w_down, v_c_ctx, v_w_ada, v_b_ada, v_g_pre_mix, v_g_post_mix, v_g_pre_ffn, v_g_post_ffn, v_w_in, v_b_merge, v_dn_conv, v_dn_a_log, v_dn_dt_bias, v_dn_onorm, v_lru_conv, v_lru_conv_b, v_lru_w_rg, v_lru_b_rg, v_lru_w_ig, v_lru_b_ig, v_lru_lambda, v_w_branch_dn, v_w_branch_lru, v_w_out, v_w_up, v_ffn_dw, v_ffn_dw_b, v_w_down):
    W = dict(zip(_WEIGHTS, (c_ctx, w_ada, b_ada, g_pre_mix, g_post_mix, g_pre_ffn, g_post_ffn, w_in, b_merge, dn_conv,
                            dn_a_log, dn_dt_bias, dn_onorm, lru_conv, lru_conv_b, lru_w_rg, lru_b_rg, lru_w_ig, lru_b_ig,
                            lru_lambda, w_branch_dn, w_branch_lru, w_out, w_up, ffn_dw, ffn_dw_b, w_down)))
    Mo = dict(zip(_WEIGHTS, (m_c_ctx, m_w_ada, m_b_ada, m_g_pre_mix, m_g_post_mix, m_g_pre_ffn, m_g_post_ffn, m_w_in,
                             m_b_merge, m_dn_conv, m_dn_a_log, m_dn_dt_bias, m_dn_onorm, m_lru_conv, m_lru_conv_b,
                             m_lru_w_rg, m_lru_b_rg, m_lru_w_ig, m_lru_b_ig, m_lru_lambda, m_w_branch_dn,
                             m_w_branch_lru, m_w_out, m_w_up, m_ffn_dw, m_ffn_dw_b, m_w_down)))
    Vo = dict(zip(_WEIGHTS, (v_c_ctx, v_w_ada, v_b_ada, v_g_pre_mix, v_g_post_mix, v_g_pre_ffn, v_g_post_ffn, v_w_in,
                             v_b_merge, v_dn_conv, v_dn_a_log, v_dn_dt_bias, v_dn_onorm, v_lru_conv, v_lru_conv_b,
                             v_lru_w_rg, v_lru_b_rg, v_lru_w_ig, v_lru_b_ig, v_lru_lambda, v_w_branch_dn,
                             v_w_branch_lru, v_w_out, v_w_up, v_ffn_dw, v_ffn_dw_b, v_w_down)))
    B, N, D = x.shape
    NC = ctx.shape[1]
    T = NC + N
    H, HD = dn_a_log.shape[-1], dn_onorm.shape[-1]
    DNW = H * HD
    LW, LBD = lru_conv_b.shape[-1], lru_w_rg.shape[-1]
    DFF = ffn_dw_b.shape[-1]
    LC = LANES
    x_i, y_i, c_i = _me()
    s_me = 2 * x_i + y_i
    tm = _tile(math.gcd(NC, N), 256, 16)

    def whole(n, g):
        r, w_ = W[n].shape[1:]
        return g.reshape(_NCHIP, r, w_) if _BIG[n] else g.reshape(_NCHIP * r, w_)

    first = ("w_ada", "w_in")
    later = tuple(n for n in _BIG if n not in first)
    shard16 = {n: W[n][0].astype(BF16) for n in _BIG}
    full = {n: whole(n, g) for n, g in zip(first, _allgather_halves([shard16[n] for n in first], name="allgather_first"))}

    small_local = [W[n][0].reshape(-1, W[n].shape[-1]) for n in _SMALL_SHARDED]
    small_shapes = [a.shape for a in small_local]
    spack = _pack(small_local, LANES, _FLAT_PART, _FLAT_PART, F32)
    sgath = _allgather_small(spack)[0::2]
    sfull = {n: _from_chip_shards(s, True)
             for n, s in zip(_SMALL_SHARDED, _unpack(sgath, small_shapes, LANES, _FLAT_PART, _FLAT_PART))}

    later16, sgath = lax.optimization_barrier(([shard16[n] for n in later], sgath))
    ag_send, ag_recv, ag_src, ag_land, ag_token = _allgather_start(later16, name="ag_start")

    o_a = 4 * DNW
    o_xl = o_a + 4 * H
    o_mg = o_xl + 2 * LW
    wi_ = _from_chip_shards(full["w_in"], True)
    nj = LW // LC
    lru_cols = jnp.stack([wi_[:, o_xl:o_xl + LW].reshape(D, nj, LC), wi_[:, o_xl + LW:o_mg].reshape(D, nj, LC)],
                         axis=2).reshape(D, 2 * LW)
    wp = jnp.concatenate([wi_[:, :o_a], lru_cols, wi_[:, o_mg:], wi_[:, o_a:o_xl],
                          jnp.zeros((D, LANES - 4 * H), BF16)], axis=1)
    p_lru, p_mg, p_ab = 4 * DNW, 4 * DNW + 2 * LW, 4 * DNW + 2 * LW + 2 * D
    PW = p_ab + LANES

    MR = LANES
    cond = jnp.concatenate([c, c_ctx[None], jnp.zeros((MR - B - 1, D), F32)], axis=0)
    silu_rows = _rowwise(lambda a: (_silu(a),), [cond], [F32], name="cond_silu")[0]
    mod = _matmul(silu_rows, full["w_ada"], b_shards=(0, _NCHIP), name="ada_fwd") + b_ada + ag_token[0, 0]
    mx = mod[:B].reshape(B, 6, D)
    mc = mod[B].reshape(6, D)
    zero = jnp.zeros((B, D), F32)
    tab = jnp.stack([jnp.stack([jnp.broadcast_to(mc[0], (B, D)), jnp.broadcast_to(mc[1], (B, D))] + [zero] * 6, axis=1),
                     jnp.stack([mx[:, 0], mx[:, 1]] + [zero] * 6, axis=1)], axis=1)
    vecs = jnp.stack([mx[:, 2], mx[:, 3], mx[:, 4], mx[:, 5]] + [zero] * 4, axis=1)
    gains = jnp.concatenate([g_post_mix, g_pre_ffn, g_post_ffn, jnp.zeros((5, D), F32)], axis=0)

    h = jnp.concatenate([ctx, x], axis=1)
    u = _premix_fwd(h, g_pre_mix, tab, nc=NC, tm=tm)
    p = _matmul(u, wp, name="in_fwd")
    dkw = dict(B=B, T=T, nc=NC, H=H, HD=HD)
    qkv = _dnprep_fwd(p, sfull["dn_conv"], **dkw)
    prm = jnp.concatenate([
        jnp.concatenate([dn_a_log.reshape(1, 2 * H), jnp.zeros((1, LANES - 2 * H), F32)], axis=1),
        jnp.concatenate([dn_dt_bias.reshape(1, 2 * H), jnp.zeros((1, LANES - 2 * H), F32)], axis=1),
        jnp.zeros((6, LANES), F32)], axis=0)
    gtm = _tile(B * T, 512, 16)
    gb = _gb_fwd(p, prm, rows=B * T, col0=p_ab, H=H, tm=gtm)
    y_dn, o_dn, *dn_res = _delta_fwd(qkv, gb, p, dn_onorm, **dkw)
    lv = jnp.concatenate([lru_conv_b, sfull["lru_b_rg"], sfull["lru_b_ig"], sfull["lru_lambda"], jnp.zeros((1, LW), F32)], axis=0)
    wr = _blockdiag(lru_w_rg[0], LC).astype(BF16)
    wi = _blockdiag(lru_w_ig[0], LC).astype(BF16)
    lkw = dict(B=B, T=T, nc=NC, LW=LW, col0=p_lru, C=LC)
    y_lru = _lru_fwd(p, sfull["lru_conv"], lv, wr, wi, **lkw)
    ag_src, ag_land = _allgather_wait(ag_send, ag_recv, ag_src, ag_land, y_lru, name="ag_wait")
    me_piece = 4 * x_i + 2 * y_i + c_i
    for n, src, land in zip(later, ag_src, _pass_to_sibling(ag_land, name="ag_pass")):
        own = lax.dynamic_slice_in_dim(src, c_i * (src.shape[0] // 2), src.shape[0] // 2, axis=0)
        full[n] = whole(n, lax.dynamic_update_index_in_dim(land, own, me_piece, axis=0))
    Ydn = _matmul(y_dn, full["w_branch_dn"], name="bdn_fwd")
    Ylru = _matmul(y_lru, full["w_branch_lru"], name="blru_fwd")
    mkw = dict(B=B, T=T, nc=NC, D=D, col0=p_mg, tm=tm)
    mixin = _merge_fwd(p, Ydn, Ylru, b_merge, **mkw)
    mix = _matmul(mixin, full["w_out"], name="out_fwd")
    h1, u2 = _post_fwd(x, mix, gains, vecs, tm=tm)
    F = _matmul(u2, full["w_up"], b_shards=(0, _NCHIP), name="up_fwd")
    w9 = sfull["ffn_dw"]
    ftc = _tile(DFF, 256)
    f = _ffn_act_fwd(F, w9, ffn_dw_b, B=B, N=N, DFF=DFF, tc=ftc)
    dn = _matmul(f, full["w_down"], name="down_fwd")
    ddn, dout, sums_f = _final(h1, dn, loss_target, gains, vecs, tm=tm)

    G = {}
    df = _matmul(ddn, full["w_down"], tb=True, name="down_bwd_x")
    G["w_down"] = _matmul(f, ddn, ta=True, name="down_bwd_w")
    dFg, dFv, dwb = _ffn_act_bwd(F, w9, ffn_dw_b, df, B=B, N=N, DFF=DFF, tc=ftc)
    hs = _NCHIP // 2
    du2 = _matmul(dFg, full["w_up"], tb=True, b_shards=(0, hs), name="up_bwd_xg")
    du2 = _matmul(dFv, full["w_up"], tb=True, b_shards=(hs, hs), add=du2, name="up_bwd_xv")
    G["w_up"] = jnp.concatenate([_matmul(u2, dFg, ta=True, out_shards=hs, name="up_bwd_wg"),
                                 _matmul(u2, dFv, ta=True, out_shards=hs, name="up_bwd_wv")], axis=0)
    dx1, dmix, sums_p = _post_bwd(x, mix, gains, vecs, dout, du2, tm=tm)
    dmixin = _matmul(dmix, full["w_out"], tb=True, name="out_bwd_x")
    G["w_out"] = _matmul(mixin, dmix, ta=True, name="out_bwd_w")
    dp = jnp.zeros((B * T, PW), BF16)
    dYdn, dYlru, dp, sums_m = _merge_bwd(p, Ydn, Ylru, b_merge, dmixin, dp, **mkw)
    dy_dn = _matmul(dYdn, full["w_branch_dn"], tb=True, name="bdn_bwd_x")
    G["w_branch_dn"] = _matmul(y_dn, dYdn, ta=True, name="bdn_bwd_w")
    dy_lru = _matmul(dYlru, full["w_branch_lru"], tb=True, name="blru_bwd_x")
    G["w_branch_lru"] = _matmul(y_lru, dYlru, ta=True, name="blru_bwd_w")

    c_arr = c_i.astype(jnp.int32).reshape(1)
    s_arr = s_me.astype(jnp.int32).reshape(1)

    def chip_sums(names, tag):
        slabs = [G[n] if _BIG[n] else G[n].reshape(_NCHIP, G[n].shape[0] // _NCHIP, G[n].shape[1]) for n in names]
        from_sibling = _sibling_send_halves(slabs, name="rs_sibling_" + tag)
        return [_half_add(g, r, c_arr, name="rs_add_" + n) for n, g, r in zip(names, slabs, from_sibling)]

    early = tuple(n for n in _BIG if n in G)
    late = tuple(n for n in _BIG if n not in G)
    cx_send, cx_recv, cx_src, cx_land, cx_token = _chip_exchange_start(chip_sums(early, "early"), name="cx_start")
    dp, dcw_l, dlv, dwr, dwi = _lru_bwd(p, sfull["lru_conv"], lv + cx_token[0, 0], wr, wi, dy_lru, dp, **lkw)
    dqkv, dgb, dp, don = _delta_bwd(qkv, gb, p, dn_onorm, o_dn, dn_res, dy_dn, dp, **dkw)
    dp, dprm = _gb_bwd(p, prm, dgb, dp, rows=B * T, col0=p_ab, H=H, tm=gtm)
    dp, dcw_d = _dnprep_bwd(p, sfull["dn_conv"], dqkv, dp, **dkw)
    dU = _matmul(dp, wp, tb=True, name="in_bwd_x")
    dwp = _matmul(u, dp, ta=True, name="in_bwd_w")
    grad_x, sums_pm = _premix_bwd(h, g_pre_mix, tab, dU, dx1, nc=NC, tm=tm)
    dlru = dwp[:, p_lru:p_mg].reshape(D, nj, 2, LC)
    G["w_in"] = _to_chip_shards(jnp.concatenate([dwp[:, :o_a], dwp[:, p_ab:p_ab + 4 * H], dlru[:, :, 0].reshape(D, LW),
                                                 dlru[:, :, 1].reshape(D, LW), dwp[:, p_mg:p_ab]], axis=1), True)

    dmod_x = jnp.stack([sums_pm[:, 1, 0], sums_pm[:, 1, 1], sums_p[:, 0], sums_p[:, 1], sums_p[:, 2], sums_f[:, 0]],
                       axis=1).reshape(B, 6 * D)
    dmod_c = jnp.concatenate([sums_pm[:, 0, 0].sum(0), sums_pm[:, 0, 1].sum(0), jnp.zeros((4 * D,), F32)])[None]
    dmod = jnp.concatenate([dmod_x, dmod_c, jnp.zeros((MR - B - 1, 6 * D), F32)], axis=0)
    G["w_ada"] = _matmul(silu_rows, dmod, ta=True, out_shards=_NCHIP, name="ada_bwd_w")
    dsilu = _matmul(dmod, full["w_ada"], tb=True, b_shards=(0, _NCHIP), name="ada_bwd_x")

    g_small = {
        "c_ctx": dsilu[B] * _dsilu(c_ctx),
        "b_ada": dmod[:B + 1].sum(0)[None],
        "g_pre_mix": sums_pm[:, :, 2].sum((0, 1))[None],
        "g_post_mix": sums_p[:, 3].sum(0)[None],
        "g_pre_ffn": sums_p[:, 4].sum(0)[None],
        "g_post_ffn": sums_f[:, 1].sum(0)[None],
        "b_merge": sums_m[0:1],
        "dn_conv": dcw_d[0:4][None],
        "dn_a_log": dprm[0, :2 * H].reshape(1, 2, H),
        "dn_dt_bias": dprm[1, :2 * H].reshape(1, 2, H),
        "dn_onorm": don[:, 0].sum(0)[None],
        "lru_conv": dcw_l[0:4][None],
        "lru_conv_b": dlv[0:1],
        "lru_w_rg": _blockdiag_extract(dwr, LBD)[None],
        "lru_b_rg": dlv[1:3][None],
        "lru_w_ig": _blockdiag_extract(dwi, LBD)[None],
        "lru_b_ig": dlv[3:5][None],
        "lru_lambda": dlv[5:7][None],
        "ffn_dw": dwb[0:9].reshape(1, 3, 3, DFF),
        "ffn_dw_b": dwb[9:10],
    }
    small_names = tuple(n for n in _WEIGHTS if n not in _BIG)
    loss_part = sums_f[:, 2].sum().reshape(1)
    gs_list = [g_small[n] for n in small_names] + [loss_part]
    gs_shapes = [a.shape for a in gs_list]
    gpack = _pack(gs_list, LANES, _FLAT_PART, _FLAT_TOTAL, F32)
    gsum = _sum_lead(_allgather_small(gpack), name="small_sum", tm=512, mult=SUBLANES)
    gs_red = dict(zip(small_names + ("loss",), _unpack(gsum, gs_shapes, LANES, _FLAT_PART, _FLAT_TOTAL)))
    loss = gs_red["loss"][0]

    grads, deltas, new_m, new_v = {}, {}, {}, {}

    def finish(names, lands, srcs, after, tag):
        halves = [_sum_slabs(l, src, s_arr, after, name="rs_sum_" + n) for n, l, src in zip(names, lands, srcs)]
        outs = None
        for n, own, sib in zip(names, halves, _sibling_swap(halves, name="rs_gather_" + tag)):
            shp = W[n].shape
            outs = _adamw_halves(W[n][0], own, sib, Mo[n][0], Vo[n][0], c_arr, name="adamw_" + n)
            grads[n], deltas[n], new_m[n], new_v[n] = (o.reshape(shp) for o in outs)
        return outs[1]

    cx_src, cx_land = _chip_exchange_wait(cx_send, cx_recv, cx_src, cx_land, dsilu, name="cx_wait")
    late_sums, gsum = lax.optimization_barrier((chip_sums(late, "late"), gsum))
    lx_send, lx_recv, lx_src, lx_land, lx_token = _chip_exchange_start(late_sums, name="cx_late_start")
    last_early = finish(early, cx_land, cx_src, lx_token, "early")
    lx_src, lx_land = _chip_exchange_wait(lx_send, lx_recv, lx_src, lx_land, last_early, name="cx_late_wait")
    finish(late, lx_land, lx_src, last_early, "late")
    for n in small_names:
        g = gs_red[n]
        if n in _SMALL_SHARDED:
            k = W[n].shape[-1]
            g = lax.dynamic_slice_in_dim(g, s_me * k, k, axis=g.ndim - 1)
        grads[n] = g.reshape(W[n].shape)
    sm_shapes = [W[n].shape for n in small_names]
    pk = lambda d: _pack([d[n] for n in small_names], LANES, _FLAT_PART, _FLAT_TOTAL, F32)
    d_, m_, v_ = _adamw(pk(W), pk(grads), pk(Mo), pk(Vo), name="adamw_small")
    for dst, pool_ in ((deltas, d_), (new_m, m_), (new_v, v_)):
        dst.update(zip(small_names, _unpack(pool_, sm_shapes, LANES, _FLAT_PART, _FLAT_TOTAL)))
    return (loss, grad_x, *[grads[n] for n in _WEIGHTS], *[deltas[n] for n in _WEIGHTS],
            *[new_m[n] for n in _WEIGHTS], *[new_v[n] for n in _WEIGHTS])
```

```python
import functools
import math

import jax
import jax.numpy as jnp
from jax import lax
from jax.experimental import pallas as pl
from jax.experimental.pallas import tpu as pltpu

F32 = jnp.float32
BF16 = jnp.bfloat16
EPS = 1e-6
GRID_W = 64
CHUNK = 256
LRU_C = 8.0
LANES = 128
SUBLANES = 8
VMEM_LIMIT = 56 * 1024 * 1024
ADAM_LR, ADAM_B1, ADAM_B2, ADAM_EPS, ADAM_WD, ADAM_STEP = 0.001, 0.9, 0.999, 1e-08, 0.01, 10
MESH = pl.DeviceIdType.MESH


def _tile(n, target, mult=LANES):
    best = None
    for t in range(mult, min(n, target) + 1, mult):
        if n % t == 0:
            best = t
    return best if best is not None else n


def _params(sem=None, **kw):
    return pltpu.CompilerParams(dimension_semantics=sem, vmem_limit_bytes=VMEM_LIMIT, **kw)


def _sigmoid(x):
    return 1.0 / (1.0 + jnp.exp(-x))


def _silu(x):
    return x * _sigmoid(x)


def _softplus(x):
    return jnp.maximum(x, 0.0) + jnp.log(1.0 + jnp.exp(-jnp.abs(x)))


def _gelu(x):
    return 0.5 * x * (1.0 + jnp.tanh(math.sqrt(2.0 / math.pi) * (x + 0.044715 * x * x * x)))


def _rmsn(u, gain):
    return u * lax.rsqrt(jnp.mean(u * u, axis=-1, keepdims=True) + EPS) * gain


_MM_VMEM = 40 * 1024 * 1024


_ANY_SPEC = pl.BlockSpec(memory_space=pl.ANY)


def _matmul(a, b, *, ta=False, tb=False, add=None, b_shards=None, out_shards=None, into=None, out_dtype=F32, name,
            tm=1024, tn=2048, tk=1024):
    (K, M) = a.shape if ta else a.shape[::-1]
    if b_shards is not None:
        s0, ns = b_shards
        bsh = (b.shape[1], ns * b.shape[2])
        nsh = b.shape[2]
    else:
        bsh = b.shape
    N = bsh[0] if tb else bsh[1]
    assert (bsh[1] if tb else bsh[0]) == K, (a.shape, b.shape, ta, tb)
    tm = _tile(M, tm)
    tk = _tile(nsh if (b_shards is not None and tb) else K, tk)
    nlim = nsh if (b_shards is not None and not tb) else (N // out_shards if out_shards else N)
    osz = jnp.dtype(out_dtype).itemsize + (4 if add is not None else 0)
    while True:
        tn_ = _tile(nlim, tn)
        need = 2 * (tm * tk * a.dtype.itemsize + tk * tn_ * b.dtype.itemsize + tm * tn_ * osz) + 4 * tm * tn_
        if need <= _MM_VMEM or tn <= LANES:
            break
        tn //= 2
    tn = tn_
    nk = K // tk
    dims = (((0 if ta else 1,), (1 if tb else 0,)), ((), ()))

    def body(a_ref, b_ref, *rest):
        c_ref = rest[0] if add is not None else None
        o_ref, acc_ref = rest[-2:]
        k = pl.program_id(2)

        @pl.when(k == 0)
        def _():
            acc_ref[...] = jnp.zeros_like(acc_ref) if c_ref is None else c_ref[...]

        bv = b_ref[0] if b_shards is not None else b_ref[...]
        acc_ref[...] += lax.dot_general(a_ref[...].astype(BF16), bv.astype(BF16), dims, preferred_element_type=F32)

        @pl.when(k == nk - 1)
        def _():
            if out_shards:
                o_ref[0] = acc_ref[...].astype(out_dtype)
            else:
                o_ref[...] = acc_ref[...].astype(out_dtype)

    a_spec = pl.BlockSpec((tk, tm), lambda i, j, k: (k, i)) if ta else pl.BlockSpec((tm, tk), lambda i, j, k: (i, k))
    if b_shards is None:
        b_spec = pl.BlockSpec((tn, tk), lambda i, j, k: (j, k)) if tb else pl.BlockSpec((tk, tn), lambda i, j, k: (k, j))
    elif tb:
        per = nsh // tk
        b_spec = pl.BlockSpec((1, tn, tk), lambda i, j, k: (s0 + k // per, j, k % per))
    else:
        per = nsh // tn
        b_spec = pl.BlockSpec((1, tk, tn), lambda i, j, k: (s0 + j // per, k, j % per))
    o_spec = pl.BlockSpec((tm, tn), lambda i, j, k: (i, j))
    extra, alias = (), {}
    if out_shards:
        oper = N // out_shards // tn
        o0 = 0
        out_shape = jax.ShapeDtypeStruct((out_shards, M, N // out_shards), out_dtype)
        if into is not None:
            buf, o0 = into
            out_shape = jax.ShapeDtypeStruct(buf.shape, buf.dtype)
            extra, alias = (buf,), {2 + (add is not None): 0}
        out_spec = pl.BlockSpec((1, tm, tn), lambda i, j, k: (o0 + j // oper, i, j % oper))
    else:
        out_spec, out_shape = o_spec, jax.ShapeDtypeStruct((M, N), out_dtype)
    return pl.pallas_call(
        body, name=name, grid=(M // tm, N // tn, nk),
        in_specs=[a_spec, b_spec] + ([o_spec] if add is not None else []) + [_ANY_SPEC] * len(extra),
        out_specs=out_spec, out_shape=out_shape, input_output_aliases=alias,
        scratch_shapes=[pltpu.VMEM((tm, tn), F32)],
        compiler_params=_params(("parallel", "parallel", "arbitrary")),
    )(*((a, b) + ((add,) if add is not None else ()) + extra))


def _premix_math(h, gain, shift, scale):
    return _rmsn(h, gain) * (1.0 + scale) + shift


def _premix_fwd(h, gain, tab, *, nc, tm):
    B, T, D = h.shape
    nt, nct = T // tm, nc // tm

    def body(h_ref, g_ref, tab_ref, u_ref):
        tabv = tab_ref[0, 0]
        u_ref[...] = _premix_math(h_ref[0], g_ref[...], tabv[0:1], tabv[1:2]).astype(BF16)

    return pl.pallas_call(
        body, name="premix_fwd", grid=(B, nt),
        in_specs=[pl.BlockSpec((1, tm, D), lambda b, t: (b, t, 0)),
                  pl.BlockSpec((1, D), lambda b, t: (0, 0)),
                  pl.BlockSpec((1, 1, 8, D), lambda b, t: (b, jnp.where(t < nct, 0, 1), 0, 0))],
        out_specs=pl.BlockSpec((tm, D), lambda b, t: (b * nt + t, 0)),
        out_shape=jax.ShapeDtypeStruct((B * T, D), BF16),
        compiler_params=_params(("parallel", "parallel")),
    )(h, gain, tab)


def _premix_bwd(h, gain, tab, du, dres, *, nc, tm):
    B, T, D = h.shape
    nt, nct = T // tm, nc // tm
    N = T - nc

    def body(h_ref, g_ref, tab_ref, du_ref, dres_ref, dx_ref, sums_ref):
        t = pl.program_id(1)
        tabv = tab_ref[0, 0]
        _, vjp = jax.vjp(_premix_math, h_ref[0], g_ref[...], tabv[0:1], tabv[1:2])
        dh, dgain, dshift, dscale = vjp(du_ref[...].astype(F32))

        @pl.when((t == 0) | (t == nct))
        def _():
            sums_ref[...] = jnp.zeros_like(sums_ref)

        sums_ref[0, 0, 0:1, :] += dshift
        sums_ref[0, 0, 1:2, :] += dscale
        sums_ref[0, 0, 2:3, :] += dgain

        @pl.when(t >= nct)
        def _():
            dx_ref[0] = dres_ref[...] + dh

    lat = lambda b, t: jnp.maximum(t - nct, 0)
    return pl.pallas_call(
        body, name="premix_bwd", grid=(B, nt),
        in_specs=[pl.BlockSpec((1, tm, D), lambda b, t: (b, t, 0)),
                  pl.BlockSpec((1, D), lambda b, t: (0, 0)),
                  pl.BlockSpec((1, 1, 8, D), lambda b, t: (b, jnp.where(t < nct, 0, 1), 0, 0)),
                  pl.BlockSpec((tm, D), lambda b, t: (b * nt + t, 0)),
                  pl.BlockSpec((tm, D), lambda b, t: (b * (nt - nct) + lat(b, t), 0))],
        out_specs=[pl.BlockSpec((1, tm, D), lambda b, t: (b, lat(b, t), 0)),
                   pl.BlockSpec((1, 1, 8, D), lambda b, t: (b, jnp.where(t < nct, 0, 1), 0, 0))],
        out_shape=[jax.ShapeDtypeStruct((B, N, D), F32), jax.ShapeDtypeStruct((B, 2, 8, D), F32)],
        compiler_params=_params(("parallel", "arbitrary")),
    )(h, gain, tab, du, dres)


def _merge_math(mgd, mgl, yd, yl, bd, bl):
    return _sigmoid(mgd + bd) * yd + _sigmoid(mgl + bl) * yl


def _merge_fwd(p, ydn, ylru, b_merge, *, B, T, nc, D, col0, tm):
    N = T - nc
    ntl, nt, nct, cb = N // tm, T // tm, nc // tm, col0 // D

    def body(mgd_ref, mgl_ref, yd_ref, yl_ref, bm_ref, o_ref):
        o_ref[...] = _merge_math(mgd_ref[...], mgl_ref[...], yd_ref[...], yl_ref[...],
                                 bm_ref[:, 0:D], bm_ref[:, D:2 * D]).astype(BF16)

    prow = lambda b, t: b * nt + nct + t
    return pl.pallas_call(
        body, name="merge_fwd", grid=(B, ntl),
        in_specs=[pl.BlockSpec((tm, D), lambda b, t: (prow(b, t), cb)),
                  pl.BlockSpec((tm, D), lambda b, t: (prow(b, t), cb + 1)),
                  pl.BlockSpec((tm, D), lambda b, t: (b * ntl + t, 0)),
                  pl.BlockSpec((tm, D), lambda b, t: (b * ntl + t, 0)),
                  pl.BlockSpec((1, 2 * D), lambda b, t: (0, 0))],
        out_specs=pl.BlockSpec((tm, D), lambda b, t: (b * ntl + t, 0)),
        out_shape=jax.ShapeDtypeStruct((B * N, D), BF16),
        compiler_params=_params(("parallel", "parallel")),
    )(p, p, ydn, ylru, b_merge)


def _merge_bwd(p, ydn, ylru, b_merge, dmix, dp, *, B, T, nc, D, col0, tm):
    N = T - nc
    ntl, nt, nct, cb = N // tm, T // tm, nc // tm, col0 // D
    assert col0 % (2 * D) == 0

    def body(mgd_ref, mgl_ref, yd_ref, yl_ref, bm_ref, dm_ref, dp_any, dyd_ref, dyl_ref, dp_ref, sums_ref):
        _, vjp = jax.vjp(_merge_math, mgd_ref[...], mgl_ref[...], yd_ref[...], yl_ref[...],
                         bm_ref[:, 0:D], bm_ref[:, D:2 * D])
        dmgd, dmgl, dyd, dyl, dbd, dbl = vjp(dm_ref[...])
        dyd_ref[...] = dyd.astype(BF16)
        dyl_ref[...] = dyl.astype(BF16)
        dp_ref[:, 0:D] = dmgd.astype(BF16)
        dp_ref[:, D:2 * D] = dmgl.astype(BF16)

        @pl.when((pl.program_id(0) == 0) & (pl.program_id(1) == 0))
        def _():
            sums_ref[...] = jnp.zeros_like(sums_ref)

        sums_ref[0:1, 0:D] += dbd
        sums_ref[0:1, D:2 * D] += dbl

    prow = lambda b, t: b * nt + nct + t
    row = pl.BlockSpec((tm, D), lambda b, t: (b * ntl + t, 0))
    return pl.pallas_call(
        body, name="merge_bwd", grid=(B, ntl),
        in_specs=[pl.BlockSpec((tm, D), lambda b, t: (prow(b, t), cb)),
                  pl.BlockSpec((tm, D), lambda b, t: (prow(b, t), cb + 1)),
                  row, row, pl.BlockSpec((1, 2 * D), lambda b, t: (0, 0)), row,
                  pl.BlockSpec(memory_space=pl.ANY)],
        out_specs=[row, row,
                   pl.BlockSpec((tm, 2 * D), lambda b, t: (prow(b, t), cb // 2)),
                   pl.BlockSpec((8, 2 * D), lambda b, t: (0, 0))],
        out_shape=[jax.ShapeDtypeStruct((B * N, D), BF16), jax.ShapeDtypeStruct((B * N, D), BF16),
                   jax.ShapeDtypeStruct(dp.shape, dp.dtype), jax.ShapeDtypeStruct((8, 2 * D), F32)],
        input_output_aliases={6: 2},
        compiler_params=_params(("arbitrary", "arbitrary")),
    )(p, p, ydn, ylru, b_merge, dmix, dp)


def _post_math(x, mix, g1, gate, g2, sh, sc):
    h1 = x + _rmsn(mix, g1) * gate
    return h1, _rmsn(h1, g2) * (1.0 + sc) + sh


def _post_fwd(x, mix, gains, vecs, *, tm):
    B, N, D = x.shape
    ntl = N // tm

    def body(x_ref, mix_ref, g_ref, v_ref, h1_ref, u2_ref):
        v = v_ref[0]
        h1, u2 = _post_math(x_ref[0], mix_ref[...], g_ref[0:1], v[0:1], g_ref[1:2], v[1:2], v[2:3])
        h1_ref[...] = h1
        u2_ref[...] = u2.astype(BF16)

    row = pl.BlockSpec((tm, D), lambda b, t: (b * ntl + t, 0))
    return pl.pallas_call(
        body, name="post_fwd", grid=(B, ntl),
        in_specs=[pl.BlockSpec((1, tm, D), lambda b, t: (b, t, 0)), row,
                  pl.BlockSpec((8, D), lambda b, t: (0, 0)), pl.BlockSpec((1, 8, D), lambda b, t: (b, 0, 0))],
        out_specs=[row, row],
        out_shape=[jax.ShapeDtypeStruct((B * N, D), F32), jax.ShapeDtypeStruct((B * N, D), BF16)],
        compiler_params=_params(("parallel", "parallel")),
    )(x, mix, gains, vecs)


def _post_bwd(x, mix, gains, vecs, dh1, du2, *, tm):
    B, N, D = x.shape
    ntl = N // tm

    def body(x_ref, mix_ref, g_ref, v_ref, dh1_ref, du2_ref, dx_ref, dmix_ref, sums_ref):
        v = v_ref[0]
        _, vjp = jax.vjp(_post_math, x_ref[0], mix_ref[...], g_ref[0:1], v[0:1], g_ref[1:2], v[1:2], v[2:3])
        dx, dmix, dg1, dgate, dg2, dsh, dsc = vjp((dh1_ref[...], du2_ref[...]))
        dx_ref[...] = dx
        dmix_ref[...] = dmix.astype(BF16)

        @pl.when(pl.program_id(1) == 0)
        def _():
            sums_ref[...] = jnp.zeros_like(sums_ref)

        sums_ref[0, 0:1, :] += dgate
        sums_ref[0, 1:2, :] += dsh
        sums_ref[0, 2:3, :] += dsc
        sums_ref[0, 3:4, :] += dg1
        sums_ref[0, 4:5, :] += dg2

    row = pl.BlockSpec((tm, D), lambda b, t: (b * ntl + t, 0))
    return pl.pallas_call(
        body, name="post_bwd", grid=(B, ntl),
        in_specs=[pl.BlockSpec((1, tm, D), lambda b, t: (b, t, 0)), row,
                  pl.BlockSpec((8, D), lambda b, t: (0, 0)), pl.BlockSpec((1, 8, D), lambda b, t: (b, 0, 0)), row, row],
        out_specs=[row, row, pl.BlockSpec((1, 8, D), lambda b, t: (b, 0, 0))],
        out_shape=[jax.ShapeDtypeStruct((B * N, D), F32), jax.ShapeDtypeStruct((B * N, D), BF16),
                   jax.ShapeDtypeStruct((B, 8, D), F32)],
        compiler_params=_params(("parallel", "arbitrary")),
    )(x, mix, gains, vecs, dh1, du2)


def _final_math(dn, g4, gate5):
    return _rmsn(dn, g4) * gate5


def _final(h1, dn, target, gains, vecs, *, tm):
    B, N, D = target.shape
    ntl = N // tm

    def body(h1_ref, dn_ref, t_ref, g_ref, v_ref, ddn_ref, dout_ref, sums_ref):
        v = v_ref[0]
        y, vjp = jax.vjp(_final_math, dn_ref[...], g_ref[2:3], v[3:4])
        err = h1_ref[...] + y - t_ref[0]
        dout = err * (1.0 / D)
        ddn, dg4, dgate5 = vjp(dout)
        ddn_ref[...] = ddn.astype(BF16)
        dout_ref[...] = dout

        @pl.when(pl.program_id(1) == 0)
        def _():
            sums_ref[...] = jnp.zeros_like(sums_ref)

        sums_ref[0, 0:1, :] += dgate5
        sums_ref[0, 1:2, :] += dg4
        sums_ref[0, 2:3, :] += jnp.sum(err * err, axis=0, keepdims=True) * (0.5 / D)

    row = pl.BlockSpec((tm, D), lambda b, t: (b * ntl + t, 0))
    return pl.pallas_call(
        body, name="final", grid=(B, ntl),
        in_specs=[row, row, pl.BlockSpec((1, tm, D), lambda b, t: (b, t, 0)),
                  pl.BlockSpec((8, D), lambda b, t: (0, 0)), pl.BlockSpec((1, 8, D), lambda b, t: (b, 0, 0))],
        out_specs=[row, row, pl.BlockSpec((1, 8, D), lambda b, t: (b, 0, 0))],
        out_shape=[jax.ShapeDtypeStruct((B * N, D), BF16), jax.ShapeDtypeStruct((B * N, D), F32),
                   jax.ShapeDtypeStruct((B, 8, D), F32)],
        compiler_params=_params(("parallel", "arbitrary")),
    )(h1, dn, target, gains, vecs)


def _shift(x, s):
    s = s % x.shape[0]
    return x if s == 0 else pltpu.roll(x, s, 0)


def _seg_taps(T, nc, width, pad_left):
    t = lax.broadcasted_iota(jnp.int32, (T, 1), 0)
    pos = jnp.where(t < nc, t, t - nc)
    seg = jnp.where(t < nc, nc, T - nc)
    taps = []
    for k in range(width):
        src = pos + (k - pad_left)
        taps.append((pad_left - k, (src >= 0) & (src < seg)))
    return taps


def _grid_taps(N):
    t = lax.broadcasted_iota(jnp.int32, (N, 1), 0)
    wcol = t % GRID_W
    taps = []
    for dr in (-1, 0, 1):
        for dw in (-1, 0, 1):
            off = dr * GRID_W + dw
            ok = (wcol + dw >= 0) & (wcol + dw < GRID_W) & (t + dr * GRID_W >= 0) & (t + dr * GRID_W < N)
            taps.append((-off, ok))
    return taps


def _conv_fwd(x, w, taps):
    y = jnp.zeros_like(x)
    for k, (s, m) in enumerate(taps):
        y = y + w[k:k + 1] * jnp.where(m, _shift(x, s), 0.0)
    return y


def _conv_bwd(x, w, taps, dy):
    dx = jnp.zeros_like(x)
    dws = []
    for k, (s, m) in enumerate(taps):
        dym = jnp.where(m, dy, 0.0)
        dx = dx + w[k:k + 1] * _shift(dym, -s)
        dws.append(jnp.sum(dym * _shift(x, s), axis=0, keepdims=True))
    return dx, jnp.concatenate(dws, axis=0)


def _ffn_act_fwd(F, w9, bias, *, B, N, DFF, tc):
    nj = DFF // tc

    def body(fg_ref, fv_ref, w_ref, b_ref, o_ref, pre_ref):
        fg = _conv_fwd(fg_ref[...], w_ref[...], _grid_taps(N)) + b_ref[...]
        pre_ref[...] = fg
        o_ref[...] = (_gelu(fg) * fv_ref[...]).astype(BF16)

    col = pl.BlockSpec((N, tc), lambda b, j: (b, j))
    return pl.pallas_call(
        body, name="ffn_act_fwd", grid=(B, nj),
        in_specs=[col, pl.BlockSpec((N, tc), lambda b, j: (b, nj + j)),
                  pl.BlockSpec((9, tc), lambda b, j: (0, j)), pl.BlockSpec((1, tc), lambda b, j: (0, j))],
        out_specs=[col, col],
        out_shape=[jax.ShapeDtypeStruct((B * N, DFF), BF16), jax.ShapeDtypeStruct((B * N, DFF), F32)],
        compiler_params=_params(("parallel", "parallel")),
    )(F, F, w9, bias)


def _ffn_act_bwd(F, pre, w9, df, *, B, N, DFF, tc):
    nj = DFF // tc

    def body(fg_ref, fv_ref, w_ref, pre_ref, df_ref, dfg_ref, dfv_ref, dwb_ref):
        taps = _grid_taps(N)
        x = fg_ref[...]
        fg, vjp = jax.vjp(lambda a: _gelu(a), pre_ref[...])
        dfl = df_ref[...]
        dfv_ref[...] = (dfl * fg).astype(BF16)
        (dpre,) = vjp(dfl * fv_ref[...])
        dx, dw = _conv_bwd(x, w_ref[...], taps, dpre)
        dfg_ref[...] = dx.astype(BF16)

        @pl.when(pl.program_id(1) == 0)
        def _():
            dwb_ref[...] = jnp.zeros_like(dwb_ref)

        dwb_ref[0:9, :] += dw
        dwb_ref[9:10, :] += jnp.sum(dpre, axis=0, keepdims=True)

    col = pl.BlockSpec((N, tc), lambda j, b: (b, j))
    return pl.pallas_call(
        body, name="ffn_act_bwd", grid=(nj, B),
        in_specs=[col, pl.BlockSpec((N, tc), lambda j, b: (b, nj + j)), pl.BlockSpec((9, tc), lambda j, b: (0, j)), col, col],
        out_specs=[col, col, pl.BlockSpec((16, tc), lambda j, b: (0, j))],
        out_shape=[jax.ShapeDtypeStruct((B * N, DFF), BF16), jax.ShapeDtypeStruct((B * N, DFF), BF16),
                   jax.ShapeDtypeStruct((16, DFF), F32)],
        compiler_params=_params(("parallel", "arbitrary")),
    )(F, F, w9, pre, df)


def _dnprep_math(y, is_qk, scale):
    s = _silu(y)
    n = s * lax.rsqrt(jnp.sum(s * s, axis=-1, keepdims=True) + EPS) * scale
    return jnp.where(is_qk, n, s)


def _dnprep_fwd(p, cw, *, B, T, nc, H, HD):
    def body(x_ref, w_ref, o_ref):
        j = pl.program_id(1)
        y = _conv_fwd(x_ref[...], w_ref[...], _seg_taps(T, nc, 4, 2))
        o_ref[...] = _dnprep_math(y, j < 2 * H, jnp.where(j < H, HD ** -0.5, 1.0))

    return pl.pallas_call(
        body, name="dnprep_fwd", grid=(B, 3 * H),
        in_specs=[pl.BlockSpec((T, HD), lambda b, j: (b, j)), pl.BlockSpec((4, HD), lambda b, j: (0, j))],
        out_specs=pl.BlockSpec((T, HD), lambda b, j: (b, j)),
        out_shape=jax.ShapeDtypeStruct((B * T, 3 * H * HD), F32),
        compiler_params=_params(("parallel", "parallel")),
    )(p, cw)


def _dnprep_bwd(p, cw, dqkv, dp, *, B, T, nc, H, HD):
    def body(x_ref, w_ref, d_ref, dp_any, dp_ref, dcw_ref):
        j = pl.program_id(0)
        taps = _seg_taps(T, nc, 4, 2)
        x = x_ref[...]
        y = _conv_fwd(x, w_ref[...], taps)
        is_qk, scale = j < 2 * H, jnp.where(j < H, HD ** -0.5, 1.0)
        _, vjp = jax.vjp(lambda a: _dnprep_math(a, is_qk, scale), y)
        (dy,) = vjp(d_ref[0])
        dx, dw = _conv_bwd(x, w_ref[...], taps, dy)
        dp_ref[...] = dx.astype(BF16)

        @pl.when(pl.program_id(1) == 0)
        def _():
            dcw_ref[...] = jnp.zeros_like(dcw_ref)

        dcw_ref[0:4, :] += dw

    col = pl.BlockSpec((T, HD), lambda j, b: (b, j))
    return pl.pallas_call(
        body, name="dnprep_bwd", grid=(3 * H, B),
        in_specs=[col, pl.BlockSpec((4, HD), lambda j, b: (0, j)),
                  pl.BlockSpec((1, T, HD), lambda j, b: (j // H, b, j % H)), pl.BlockSpec(memory_space=pl.ANY)],
        out_specs=[col, pl.BlockSpec((8, HD), lambda j, b: (0, j))],
        out_shape=[jax.ShapeDtypeStruct(dp.shape, dp.dtype), jax.ShapeDtypeStruct((8, 3 * H * HD), F32)],
        input_output_aliases={3: 0},
        compiler_params=_params(("parallel", "arbitrary")),
    )(p, cw, dqkv, dp)


def _gb_math(ab, alog, dtb, H):
    lane = lax.broadcasted_iota(jnp.int32, ab.shape, 1)
    g = -jnp.exp(alog) * _softplus(ab + dtb)
    return jnp.where(lane < 2 * H, g, jnp.where(lane < 4 * H, _sigmoid(ab), 0.0))


def _gb_fwd(p, prm, *, rows, col0, H, tm):
    def body(x_ref, prm_ref, o_ref):
        o_ref[...] = _gb_math(x_ref[...], prm_ref[0:1], prm_ref[1:2], H)

    return pl.pallas_call(
        body, name="gb_fwd", grid=(rows // tm,),
        in_specs=[pl.BlockSpec((tm, LANES), lambda t: (t, col0 // LANES)), pl.BlockSpec((8, LANES), lambda t: (0, 0))],
        out_specs=pl.BlockSpec((tm, LANES), lambda t: (t, 0)),
        out_shape=jax.ShapeDtypeStruct((rows, LANES), F32),
        compiler_params=_params(("parallel",)),
    )(p, prm)


def _gb_bwd(p, prm, dgb, dp, *, rows, col0, H, tm):
    def body(x_ref, prm_ref, d_ref, dp_any, dp_ref, dprm_ref):
        _, vjp = jax.vjp(lambda a, b, c: _gb_math(a, b, c, H), x_ref[...], prm_ref[0:1], prm_ref[1:2])
        dab, dalog, ddtb = vjp(d_ref[...])
        dp_ref[...] = dab.astype(BF16)

        @pl.when(pl.program_id(0) == 0)
        def _():
            dprm_ref[...] = jnp.zeros_like(dprm_ref)

        dprm_ref[0:1, :] += dalog
        dprm_ref[1:2, :] += ddtb

    blk = pl.BlockSpec((tm, LANES), lambda t: (t, col0 // LANES))
    return pl.pallas_call(
        body, name="gb_bwd", grid=(rows // tm,),
        in_specs=[blk, pl.BlockSpec((8, LANES), lambda t: (0, 0)), pl.BlockSpec((tm, LANES), lambda t: (t, 0)),
                  pl.BlockSpec(memory_space=pl.ANY)],
        out_specs=[blk, pl.BlockSpec((8, LANES), lambda t: (0, 0))],
        out_shape=[jax.ShapeDtypeStruct(dp.shape, dp.dtype), jax.ShapeDtypeStruct((8, LANES), F32)],
        input_output_aliases={3: 0},
        compiler_params=_params(("arbitrary",)),
    )(p, prm, dgb, dp)


def _lru_scans(scans):
    C = scans[0][0].shape[1]
    row = lax.broadcasted_iota(jnp.int32, (SUBLANES, C), 0)
    carries = tuple(jnp.zeros((1, C), F32) for _ in scans)
    for si in range(len(scans[0][4])):
        nb = scans[0][4][si][1] // SUBLANES
        assert all(sc[4][si][1] // SUBLANES == nb for sc in scans)

        def blk(i, carries, si=si, nb=nb):
            out = []
            for (a_ref, b_ref, h_ref, hp_ref, segs), carry in zip(scans, carries):
                start, _, reverse = segs[si]
                r0 = pl.multiple_of(start + (nb - 1 - i if reverse else i) * SUBLANES, SUBLANES)
                A = a_ref[pl.ds(r0, SUBLANES), :]
                Bv = b_ref[pl.ds(r0, SUBLANES), :]
                for s in (1, 2, 4):
                    sh = SUBLANES - s if reverse else s
                    m = (row < SUBLANES - s) if reverse else (row >= s)
                    Bv = jnp.where(m, A * pltpu.roll(Bv, sh, 0) + Bv, Bv)
                    A = jnp.where(m, A * pltpu.roll(A, sh, 0), A)
                Hv = Bv + A * carry
                h_ref[pl.ds(r0, SUBLANES), :] = Hv
                if hp_ref is not None:
                    if reverse:
                        hp = jnp.where(row < SUBLANES - 1, pltpu.roll(Hv, SUBLANES - 1, 0), carry)
                    else:
                        hp = jnp.where(row >= 1, pltpu.roll(Hv, 1, 0), carry)
                    hp_ref[pl.ds(r0, SUBLANES), :] = hp
                out.append(Hv[0:1] if reverse else Hv[SUBLANES - 1:SUBLANES])
            return tuple(out)

        carries = lax.fori_loop(0, nb, blk, carries)


def _lru_orders(T, nc, d):
    N = T - nc
    if d == 0:
        return [(0, nc, False), (nc, N, False)], [(nc, N, True), (0, nc, True)]
    return [(0, nc, True), (nc, N, True)], [(nc, N, False), (0, nc, False)]


def _bdot(a, b, dims=(((1,), (0,)), ((), ()))):
    return lax.dot_general(a.astype(BF16), b.astype(BF16), dims, preferred_element_type=F32)


_NT = (((1,), (1,)), ((), ()))
_TN = (((0,), (0,)), ((), ()))


def _blockdiag(w, C):
    nd, nb, bd, _ = w.shape
    per = C // bd
    out = jnp.einsum('dnpij,pq->dnpiqj', w.reshape(nd, nb // per, per, bd, bd), jnp.eye(per, dtype=w.dtype))
    return out.reshape(nd, nb // per, C, C)


def _blockdiag_extract(dw, bd):
    nd, nj, C, _ = dw.shape
    per = C // bd
    out = jnp.einsum('dnpiqj,pq->dnpij', dw.reshape(nd, nj, per, bd, per, bd), jnp.eye(per, dtype=dw.dtype))
    return out.reshape(nd, nj * per, bd, bd)


def _lru_fwd(p, cw, lv, wr, wi, *, B, T, nc, LW, col0, C):
    N = T - nc
    nj = LW // C

    def body(x_ref, cw_ref, lv_ref, wr_ref, wi_ref, o_ref, a_s, b_s, h_s):
        lv_ = lv_ref[...]
        xc = _conv_fwd(x_ref[:, 0:C], cw_ref[...], _seg_taps(T, nc, 4, 2)) + lv_[0:1]
        for d in (0, 1):
            r = _sigmoid(_bdot(xc, wr_ref[d, 0]) + lv_[1 + d:2 + d])
            i = _sigmoid(_bdot(xc, wi_ref[d, 0]) + lv_[3 + d:4 + d])
            la = -LRU_C * r * _softplus(-lv_[5 + d:6 + d])
            a_s[d] = jnp.exp(la)
            b_s[d] = jnp.sqrt(1.0 - jnp.exp(2.0 * la)) * i * xc
        _lru_scans([(a_s.at[d], b_s.at[d], h_s.at[d], None, _lru_orders(T, nc, d)[0]) for d in (0, 1)])
        o_ref[...] = ((h_s[0, nc:, :] + h_s[1, nc:, :]) * _gelu(x_ref[nc:, C:2 * C])).astype(BF16)

    return pl.pallas_call(
        body, name="lru_fwd", grid=(B, nj),
        in_specs=[pl.BlockSpec((T, 2 * C), lambda b, j: (b, col0 // (2 * C) + j)),
                  pl.BlockSpec((4, C), lambda b, j: (0, j)), pl.BlockSpec((8, C), lambda b, j: (0, j)),
                  pl.BlockSpec((2, 1, C, C), lambda b, j: (0, j, 0, 0)), pl.BlockSpec((2, 1, C, C), lambda b, j: (0, j, 0, 0))],
        out_specs=pl.BlockSpec((N, C), lambda b, j: (b, j)),
        out_shape=jax.ShapeDtypeStruct((B * N, LW), BF16),
        scratch_shapes=[pltpu.VMEM((2, T, C), F32)] * 3,
        compiler_params=_params(("parallel", "parallel")),
    )(p, cw, lv, wr, wi)


def _lru_bwd(p, cw, lv, wr, wi, dy, dp, *, B, T, nc, LW, col0, C):
    N = T - nc
    nj = LW // C

    def body(x_ref, cw_ref, lv_ref, wr_ref, wi_ref, dy_ref, dp_any, dp_ref, dcw_ref, dlv_ref, dwr_ref, dwi_ref,
             a_s, b_s, h_s, hp_s, mu_s, mup_s, dh_s, dxc_s):
        taps = _seg_taps(T, nc, 4, 2)
        lv_ = lv_ref[...]
        xl = x_ref[:, 0:C]
        xc = _conv_fwd(xl, cw_ref[...], taps) + lv_[0:1]
        gel, gelu_vjp = jax.vjp(_gelu, x_ref[nc:, C:2 * C])
        dh_s[0:nc, :] = jnp.zeros((nc, C), F32)
        dh_s[nc:, :] = dy_ref[...] * gel
        dxc_s[...] = jnp.zeros_like(dxc_s)

        @pl.when(pl.program_id(1) == 0)
        def _():
            dcw_ref[...] = jnp.zeros_like(dcw_ref)
            dlv_ref[...] = jnp.zeros_like(dlv_ref)
            dwr_ref[...] = jnp.zeros_like(dwr_ref)
            dwi_ref[...] = jnp.zeros_like(dwi_ref)

        def gates(d):
            lam = lv_[5 + d:6 + d]
            r = _sigmoid(_bdot(xc, wr_ref[d, 0]) + lv_[1 + d:2 + d])
            i = _sigmoid(_bdot(xc, wi_ref[d, 0]) + lv_[3 + d:4 + d])
            sp = _softplus(-lam)
            la = -LRU_C * r * sp
            e2 = jnp.exp(2.0 * la)
            return lam, r, i, sp, la, e2, jnp.sqrt(1.0 - e2)

        for d in (0, 1):
            _, _, i, _, la, _, mult = gates(d)
            a_s[d] = jnp.exp(la)
            b_s[d] = mult * i * xc
        _lru_scans([(a_s.at[d], b_s.at[d], h_s.at[d], hp_s.at[d], _lru_orders(T, nc, d)[0]) for d in (0, 1)])
        for d in (0, 1):
            b_s[d] = a_s[d] * dh_s[...]
        _lru_scans([(a_s.at[d], b_s.at[d], mu_s.at[d], mup_s.at[d], _lru_orders(T, nc, d)[1]) for d in (0, 1)])

        for d in (0, 1):
            lam, r, i, sp, la, e2, mult = gates(d)
            a = a_s[d]
            dinp = dh_s[...] + mup_s[d]
            da = dinp * hp_s[d]
            dmult = dinp * i * xc
            di = dinp * mult * xc
            dla = da * a - dmult * e2 / mult
            dpre_r = (dla * (-LRU_C * sp)) * r * (1.0 - r)
            dpre_i = di * i * (1.0 - i)
            dsp = jnp.sum(dla * (-LRU_C * r), axis=0, keepdims=True)
            dxc_s[...] += dinp * mult * i + _bdot(dpre_r, wr_ref[d, 0], _NT) + _bdot(dpre_i, wi_ref[d, 0], _NT)
            dwr_ref[d, 0] += _bdot(xc, dpre_r, _TN)
            dwi_ref[d, 0] += _bdot(xc, dpre_i, _TN)
            dlv_ref[1 + d:2 + d, :] += jnp.sum(dpre_r, axis=0, keepdims=True)
            dlv_ref[3 + d:4 + d, :] += jnp.sum(dpre_i, axis=0, keepdims=True)
            dlv_ref[5 + d:6 + d, :] += -dsp * _sigmoid(-lam)

        dxc = dxc_s[...]
        dxl, dw = _conv_bwd(xl, cw_ref[...], taps, dxc)
        dcw_ref[0:4, :] += dw
        dlv_ref[0:1, :] += jnp.sum(dxc, axis=0, keepdims=True)
        dp_ref[:, 0:C] = dxl.astype(BF16)
        (dyl,) = gelu_vjp(dy_ref[...] * (h_s[0, nc:, :] + h_s[1, nc:, :]))
        dp_ref[0:nc, C:2 * C] = jnp.zeros((nc, C), BF16)
        dp_ref[nc:, C:2 * C] = dyl.astype(BF16)

    xblk = pl.BlockSpec((T, 2 * C), lambda j, b: (b, col0 // (2 * C) + j))
    wblk = pl.BlockSpec((2, 1, C, C), lambda j, b: (0, j, 0, 0))
    vblk = pl.BlockSpec((8, C), lambda j, b: (0, j))
    return pl.pallas_call(
        body, name="lru_bwd", grid=(nj, B),
        in_specs=[xblk, pl.BlockSpec((4, C), lambda j, b: (0, j)), vblk, wblk, wblk,
                  pl.BlockSpec((N, C), lambda j, b: (b, j)), pl.BlockSpec(memory_space=pl.ANY)],
        out_specs=[xblk, vblk, vblk, wblk, wblk],
        out_shape=[jax.ShapeDtypeStruct(dp.shape, dp.dtype), jax.ShapeDtypeStruct((8, LW), F32),
                   jax.ShapeDtypeStruct((8, LW), F32), jax.ShapeDtypeStruct((2, nj, C, C), F32),
                   jax.ShapeDtypeStruct((2, nj, C, C), F32)],
        scratch_shapes=[pltpu.VMEM((2, T, C), F32)] * 6 + [pltpu.VMEM((T, C), F32)] * 2,
        input_output_aliases={6: 0},
        compiler_params=_params(("parallel", "arbitrary")),
    )(p, cw, lv, wr, wi, dy, dp)


def _chunk_masks(upper):
    i = lax.broadcasted_iota(jnp.int32, (CHUNK, CHUNK), 0)
    j = lax.broadcasted_iota(jnp.int32, (CHUNK, CHUNK), 1)
    ahead = jnp.where(upper, j - i, i - j)
    return i == j, ahead >= 0, ahead > 0


def _col2row(c, eye):
    return jnp.sum(jnp.where(eye, c, 0.0), axis=0, keepdims=True)


def _row2col(r, eye):
    return jnp.sum(jnp.where(eye, r, 0.0), axis=1, keepdims=True)


def _rowsum(x):
    return jnp.sum(x, axis=1, keepdims=True)


_INV_BASE = 8


def _unit_tri_inverses(Ls):
    G = len(Ls)
    W = G * CHUNK
    blk = (lax.broadcasted_iota(jnp.int32, (W, W), 0) // CHUNK) == (lax.broadcasted_iota(jnp.int32, (W, W), 1) // CHUNK)
    ri = lax.broadcasted_iota(jnp.int32, (CHUNK, W), 0)
    ci = lax.broadcasted_iota(jnp.int32, (CHUNK, W), 1) % CHUNK

    def bd(b):
        return jnp.where(blk, jnp.tile(b, (G, 1)), jnp.zeros((), BF16))

    def pdot(a, b):
        return jnp.dot(a.astype(BF16), bd(b.astype(BF16)), preferred_element_type=F32)

    Lc = Ls[0] if G == 1 else jnp.concatenate(Ls, axis=1)
    s = _INV_BASE
    Xp = -jnp.where(ri // s == ci // s, Lc, 0.0)
    Rm = Xp
    for _ in range(int(math.log2(s)) - 1):
        Xp = pdot(Xp, Xp)
        Rm = Rm + Xp + pdot(Rm, Xp)
    while s < CHUNK:
        E = jnp.where((ri // (2 * s) == ci // (2 * s)) & (ri // s != ci // s), Lc, 0.0)
        DE = E + pdot(Rm, E)
        Rm = Rm - (DE + pdot(DE, Rm))
        s *= 2
    eye = _chunk_masks(False)[0]
    return [jnp.where(eye, 1.0, 0.0) + Rm[:, g * CHUNK:(g + 1) * CHUNK] for g in range(G)]


def _delta_chunk_common(q, k, v, gcol, bcol, upper):
    eye, incl, strict = _chunk_masks(upper)
    gc = _rowsum(jnp.where(incl, _col2row(gcol, eye), 0.0))
    D = jnp.where(incl, jnp.exp(jnp.minimum(gc - _col2row(gc, eye), 0.0)), 0.0)
    kb = k * bcol
    AP = _bdot(jnp.concatenate([kb, q], axis=0), k, _NT)
    A = AP[:CHUNK]
    L = jnp.where(strict, A * D, 0.0)
    eg = jnp.exp(gc)
    gl = jnp.sum(gcol, axis=0, keepdims=True)
    attn = jnp.where(incl, AP[CHUNK:] * D, 0.0)
    return dict(eye=eye, incl=incl, strict=strict, gc=gc, D=D, kb=kb, A=A, L=L, eg=eg, gl=gl, egl=jnp.exp(gl),
                attn=attn, kbe=kb * eg, vb=v * bcol, qe=q * eg, kd=k * jnp.exp(gl - gc))


def _delta_group_pre(chunks, upper):
    cs = [_delta_chunk_common(*ch, upper) for ch in chunks]
    out = []
    for c, Tm in zip(cs, _unit_tri_inverses([c["L"] for c in cs])):
        dk = c["kbe"].shape[1]
        wu = _bdot(Tm, jnp.concatenate([c["kbe"], c["vb"]], axis=1))
        KN = _bdot(c["kd"], wu, _TN)
        QO = _bdot(c["attn"], wu)
        out.append((Tm, KN[:, :dk], KN[:, dk:], c["qe"] - QO[:, :dk], QO[:, dk:], c["egl"]))
    return out


def _delta_chunk_bwd(q, k, v, gcol, bcol, S, Tm, do, dS2, upper):
    c = _delta_chunk_common(q, k, v, gcol, bcol, upper)
    eye, incl, strict, D, eg, egl = c["eye"], c["incl"], c["strict"], c["D"], c["eg"], c["egl"]
    kb, kbe, vb, qe, kd, attn = c["kb"], c["kbe"], c["vb"], c["qe"], c["kd"], c["attn"]
    dkk = kbe.shape[1]
    wu = _bdot(Tm, jnp.concatenate([kbe, vb], axis=1))
    w = wu[:, :dkk]
    vn = wu[:, dkk:] - _bdot(w, S)
    dvn = _bdot(kd, dS2) + _bdot(attn, do, _TN)
    dkd = _bdot(vn, dS2, _NT)
    dgl = jnp.sum(_rowsum(dS2 * S), axis=0, keepdims=True) * egl
    dqa = _bdot(do, jnp.concatenate([S, vn], axis=0), _NT)
    dqe = dqa[:, :dkk]
    dattn = jnp.where(incl, dqa[:, dkk:], 0.0)
    dw = -_bdot(dvn, S, _NT)
    r = _rowsum(dkd * kd)
    dk = dkd * jnp.exp(c["gl"] - c["gc"])
    dgl = dgl + jnp.sum(r, axis=0, keepdims=True)
    dgc = _rowsum(dqe * qe) - r
    E = dattn * attn
    dvw = jnp.concatenate([dvn, dw], axis=1)
    dTm = _bdot(dvw, jnp.concatenate([vb, kbe], axis=1), _NT)
    dvk = _bdot(Tm, dvw, _TN)
    dvb = dvk[:, :dvn.shape[1]]
    dv = dvb * bcol
    dbeta = _rowsum(dvb * v)
    dkbe = dvk[:, dvn.shape[1]:]
    dkb = dkbe * eg
    dgc = dgc + _rowsum(dkbe * kbe)
    dL = jnp.where(strict, -_bdot(Tm, _bdot(dTm, Tm, _NT), _TN), 0.0)
    dA = dL * D
    E = E + dL * c["L"]
    PA = jnp.concatenate([dattn * D, dA], axis=0)
    PAk = _bdot(PA, k)
    dq = dqe * eg + PAk[:CHUNK]
    dkb = dkb + PAk[CHUNK:]
    dk = dk + _bdot(PA, jnp.concatenate([q, kb], axis=0), _TN) + dkb * bcol
    dbeta = dbeta + _rowsum(dkb * k)
    dgc = dgc + _rowsum(E) - _row2col(jnp.sum(E, axis=0, keepdims=True), eye)
    dg = _row2col(jnp.sum(jnp.where(incl, dgc, 0.0), axis=0, keepdims=True), eye) + dgl
    return dq, dk, dv, dg, dbeta


def _delta_unroll(trips):
    return max(u for u in (3, 2, 1) if trips % u == 0)


def _delta_group(n):
    return max(g for g in range(1, 2 * LANES // CHUNK + 1) if n % g == 0)


def _delta_chunk_at(T, nc, d, i):
    n, ncc = T // CHUNK, nc // CHUNK
    desc = jnp.where(i < ncc, ncc - 1 - i, n - 1 - (i - ncc))
    if isinstance(d, int):
        return i if d == 0 else desc
    return jnp.where(d == 0, i, desc)


def _dn_out_math(o, onorm, z):
    return _rmsn(o, onorm) * _silu(z)


def _delta_fwd(qkv, gb, p, onorm, *, B, T, nc, H, HD):
    N = T - nc
    n = T // CHUNK
    G = _delta_group(n)

    def body(q_ref, k_ref, v_ref, gb_ref, z_ref, on_ref, y_ref, o_ref, Tm_ref, K_ref, S_ref, Qp_ref, eg_ref,
             N_s, O0_s, o_s):
        h = pl.program_id(1)
        lane = lax.broadcasted_iota(jnp.int32, (CHUNK, LANES), 1)

        def pre(g, carry):
            cs = [g * G + i for i in range(G)]
            rows = [pl.ds(pl.multiple_of(c * CHUNK, CHUNK), CHUNK) for c in cs]
            for d in (0, 1):
                chunks = []
                for r in rows:
                    gbb = gb_ref[r, :]
                    chunks.append((q_ref[r, :], k_ref[r, :], v_ref[r, :],
                                   _rowsum(jnp.where(lane == d * H + h, gbb, 0.0)),
                                   _rowsum(jnp.where(lane == 2 * H + d * H + h, gbb, 0.0))))
                for c, r, (Tm, K, Nn, Qp, O0, egl) in zip(cs, rows, _delta_group_pre(chunks, d == 1)):
                    Tm_ref[0, d * n + c] = Tm
                    K_ref[0, d * n + c] = K.astype(BF16)
                    N_s[d * n + c] = Nn
                    Qp_ref[0, d, r, :] = Qp.astype(BF16)
                    O0_s[d, r, :] = O0
                    eg_ref[0, d * n + c] = jnp.broadcast_to(egl, (SUBLANES, HD))
            return carry

        lax.fori_loop(0, n // G, pre, 0)

        def step(i, Ss):
            out = []
            for d in (0, 1):
                c = _delta_chunk_at(T, nc, d, i)
                rows = pl.ds(pl.multiple_of(c * CHUNK, CHUNK), CHUNK)
                S_ref[0, d * n + c] = Ss[d]
                Sb = Ss[d].astype(BF16)
                o_s[d, rows, :] = jnp.dot(Qp_ref[0, d, rows, :], Sb, preferred_element_type=F32) + O0_s[d, rows, :]
                out.append(eg_ref[0, d * n + c][0:1] * Ss[d] + N_s[d * n + c]
                           - jnp.dot(K_ref[0, d * n + c], Sb, preferred_element_type=F32))
            return tuple(out)

        lax.fori_loop(0, n, step, (jnp.zeros((HD, HD), F32), jnp.zeros((HD, HD), F32)))
        o = o_s[0, nc:, :] + o_s[1, nc:, :]
        o_ref[...] = o
        y_ref[...] = _dn_out_math(o, on_ref[...], z_ref[nc:, :]).astype(BF16)

    col = lambda off: pl.BlockSpec((T, HD), lambda b, h: (b, off + h))
    lat = pl.BlockSpec((N, HD), lambda b, h: (b, h))
    per = lambda *blk: pl.BlockSpec((1, *blk), lambda b, h: (b * H + h, 0, 0, 0))
    return pl.pallas_call(
        body, name="delta_fwd", grid=(B, H),
        in_specs=[col(0), col(H), col(2 * H), pl.BlockSpec((T, LANES), lambda b, h: (b, 0)), col(3 * H),
                  pl.BlockSpec((1, HD), lambda b, h: (0, 0))],
        out_specs=[lat, lat, per(2 * n, CHUNK, CHUNK), per(2 * n, HD, HD), per(2 * n, HD, HD), per(2, T, HD),
                   per(2 * n, SUBLANES, HD)],
        out_shape=[jax.ShapeDtypeStruct((B * N, H * HD), BF16), jax.ShapeDtypeStruct((B * N, H * HD), F32),
                   jax.ShapeDtypeStruct((B * H, 2 * n, CHUNK, CHUNK), F32),
                   jax.ShapeDtypeStruct((B * H, 2 * n, HD, HD), BF16), jax.ShapeDtypeStruct((B * H, 2 * n, HD, HD), F32),
                   jax.ShapeDtypeStruct((B * H, 2, T, HD), BF16), jax.ShapeDtypeStruct((B * H, 2 * n, SUBLANES, HD), F32)],
        scratch_shapes=[pltpu.VMEM((2 * n, HD, HD), F32), pltpu.VMEM((2, T, HD), F32), pltpu.VMEM((2, T, HD), F32)],
        compiler_params=_params(("parallel", "parallel")),
    )(qkv, qkv, qkv, gb, p, onorm)


def _delta_bwd(qkv, gb, p, onorm, o, res, dy, dp, *, B, T, nc, H, HD):
    N = T - nc
    n = T // CHUNK

    def body(q_ref, k_ref, v_ref, gb_ref, z_ref, on_ref, o_ref, dy_ref, Tm_ref, K_ref, S_ref, Qp_ref, eg_ref, dp_any,
             dqkv_ref, dgb_ref, dp_ref, don_ref, do_s, R_s, dS_s):
        h, d = pl.program_id(1), pl.program_id(2)
        lane = lax.broadcasted_iota(jnp.int32, (CHUNK, LANES), 1)

        @pl.when(d == 0)
        def _():
            _, vjp = jax.vjp(_dn_out_math, o_ref[...], on_ref[...], z_ref[nc:, :])
            do, don, dz = vjp(dy_ref[...])
            do_s[0:nc, :] = jnp.zeros((nc, HD), F32)
            do_s[nc:, :] = do
            dp_ref[0:nc, :] = jnp.zeros((nc, HD), BF16)
            dp_ref[nc:, :] = dz.astype(BF16)
            dqkv_ref[...] = jnp.zeros_like(dqkv_ref)

            @pl.when(h == 0)
            def _():
                don_ref[...] = jnp.zeros_like(don_ref)
                dgb_ref[...] = jnp.zeros_like(dgb_ref)

            don_ref[0, 0:1, :] += don

        def r_of(c, carry):
            rows = pl.ds(pl.multiple_of(c * CHUNK, CHUNK), CHUNK)
            R_s[c] = lax.dot_general(Qp_ref[0, 0, rows, :], do_s[rows, :].astype(BF16), _TN, preferred_element_type=F32)
            return carry

        lax.fori_loop(0, n, r_of, 0)

        def bwd_step(i, dS):
            c = _delta_chunk_at(T, nc, d, n - 1 - i)
            dS_s[c] = dS
            return (eg_ref[0, c][0:1] * dS + R_s[c]
                    - lax.dot_general(K_ref[0, c], dS.astype(BF16), _TN, preferred_element_type=F32))

        lax.fori_loop(0, n, bwd_step, jnp.zeros((HD, HD), F32))

        def grads(c, carry):
            rows = pl.ds(pl.multiple_of(c * CHUNK, CHUNK), CHUNK)
            gbb = gb_ref[rows, :]
            gcol = _rowsum(jnp.where(lane == d * H + h, gbb, 0.0))
            bcol = _rowsum(jnp.where(lane == 2 * H + d * H + h, gbb, 0.0))
            dq, dk, dv, dg, dbeta = _delta_chunk_bwd(q_ref[rows, :], k_ref[rows, :], v_ref[rows, :], gcol, bcol,
                                                     S_ref[0, c], Tm_ref[0, c], do_s[rows, :], dS_s[c], d == 1)
            dqkv_ref[0, rows, :] += dq
            dqkv_ref[1, rows, :] += dk
            dqkv_ref[2, rows, :] += dv
            dgb_ref[rows, :] += (jnp.where(lane == d * H + h, dg, 0.0)
                                 + jnp.where(lane == 2 * H + d * H + h, dbeta, 0.0))
            return carry

        lax.fori_loop(0, n, grads, 0, unroll=_delta_unroll(n))

    col = lambda off: pl.BlockSpec((T, HD), lambda b, h, d: (b, off + h))
    lat = pl.BlockSpec((N, HD), lambda b, h, d: (b, h))
    per = lambda *blk: pl.BlockSpec((1, *blk), lambda b, h, d: (b * H + h, d, 0, 0))
    return pl.pallas_call(
        body, name="delta_bwd", grid=(B, H, 2),
        in_specs=[col(0), col(H), col(2 * H), pl.BlockSpec((T, LANES), lambda b, h, d: (b, 0)), col(3 * H),
                  pl.BlockSpec((1, HD), lambda b, h, d: (0, 0)), lat, lat,
                  per(n, CHUNK, CHUNK), per(n, HD, HD), per(n, HD, HD), per(1, T, HD), per(n, SUBLANES, HD),
                  pl.BlockSpec(memory_space=pl.ANY)],
        out_specs=[pl.BlockSpec((3, T, HD), lambda b, h, d: (0, b, h)), pl.BlockSpec((T, LANES), lambda b, h, d: (b, 0)),
                   col(3 * H), pl.BlockSpec((1, 8, HD), lambda b, h, d: (b, 0, 0))],
        out_shape=[jax.ShapeDtypeStruct((3, B * T, H * HD), F32), jax.ShapeDtypeStruct((B * T, LANES), F32),
                   jax.ShapeDtypeStruct(dp.shape, dp.dtype), jax.ShapeDtypeStruct((B, 8, HD), F32)],
        scratch_shapes=[pltpu.VMEM((T, HD), F32), pltpu.VMEM((n, HD, HD), F32), pltpu.VMEM((n, HD, HD), F32)],
        input_output_aliases={13: 2},
        compiler_params=_params(("parallel", "arbitrary", "arbitrary")),
    )(qkv, qkv, qkv, gb, p, onorm, o, dy, *res, dp)


def _rowwise(fn, ins, out_dtypes, *, name, tm=256, mult=16):
    R, W = ins[0].shape
    tm = _tile(R, tm, mult)

    def body(*refs):
        outs = fn(*[r[...] for r in refs[:len(ins)]])
        for o_ref, o in zip(refs[len(ins):], outs):
            o_ref[...] = o.astype(o_ref.dtype)

    spec = pl.BlockSpec((tm, W), lambda i: (i, 0))
    return pl.pallas_call(
        body, name=name, grid=(R // tm,), in_specs=[spec] * len(ins), out_specs=[spec] * len(out_dtypes),
        out_shape=[jax.ShapeDtypeStruct((R, W), dt) for dt in out_dtypes],
        compiler_params=_params(("parallel",)),
    )(*ins)


def _sum_lead(x, *, name, tm=256, mult=16):
    S, R, W = x.shape
    tm = _tile(R, tm, mult)

    def body(*refs):
        acc = refs[0][0].astype(F32)
        for r in refs[1:S]:
            acc = acc + r[0].astype(F32)
        refs[S][...] = acc

    return pl.pallas_call(
        body, name=name, grid=(R // tm,),
        in_specs=[pl.BlockSpec((1, tm, W), functools.partial(lambda s, i: (s, i, 0), s)) for s in range(S)],
        out_specs=pl.BlockSpec((tm, W), lambda i: (i, 0)),
        out_shape=jax.ShapeDtypeStruct((R, W), F32),
        compiler_params=_params(("parallel",)),
    )(*([x] * S))


def _adamw_math(w, g, m, v):
    m = ADAM_B1 * m + (1.0 - ADAM_B1) * g
    v = ADAM_B2 * v + (1.0 - ADAM_B2) * (g * g)
    m_hat = m / (1.0 - ADAM_B1 ** ADAM_STEP)
    v_hat = v / (1.0 - ADAM_B2 ** ADAM_STEP)
    return -ADAM_LR * (m_hat / (jnp.sqrt(v_hat) + ADAM_EPS) + ADAM_WD * w), m, v


def _adamw(w, g, m, v, *, name):
    tm = max(SUBLANES, (256 * 1024) // w.shape[1] // SUBLANES * SUBLANES)
    return _rowwise(_adamw_math, [w, g, m, v], [F32, F32, F32], name=name, tm=tm, mult=SUBLANES)


def _me():
    return lax.axis_index("x"), lax.axis_index("y"), lax.axis_index("c")


def _allgather_small(v):
    R, W = v.shape

    def body(x_ref, out_ref, send_sems, recv_sems, local_sem):
        x, y, c = _me()
        me, sibling = (x, y, c), (x, y, 1 - c)
        chips = [(1 - x, y), (x, 1 - y), (1 - x, 1 - y)]

        def slot(px, py, pc):
            return out_ref.at[4 * px + 2 * py + pc]

        def copy(k, block, to, src=None):
            return pltpu.make_async_remote_copy(
                src_ref=slot(*block) if src is None else src, dst_ref=slot(*block),
                send_sem=send_sems.at[k], recv_sem=recv_sems.at[k], device_id=to, device_id_type=MESH)

        mine = pltpu.make_async_copy(x_ref, slot(*me), local_sem)
        mine.start()
        first = [copy(0, me, sibling, src=x_ref)]
        first += [copy(1 + j, me, (*chip, c), src=x_ref) for j, chip in enumerate(chips)]
        for cp in first:
            cp.start()
        passed = [copy(4 + j, (*chip, c), sibling) for j, chip in enumerate(chips)]
        for j, chip in enumerate(chips):
            copy(1 + j, (*chip, c), me).wait_recv()
            passed[j].start()
        copy(0, sibling, me).wait_recv()
        for j, chip in enumerate(chips):
            copy(4 + j, (*chip, 1 - c), me).wait_recv()
        for cp in first + passed:
            cp.wait_send()
        mine.wait()

    return pl.pallas_call(
        body, name="allgather_small", out_shape=jax.ShapeDtypeStruct((8, R, W), v.dtype),
        in_specs=[pl.BlockSpec(memory_space=pltpu.VMEM)], out_specs=pl.BlockSpec(memory_space=pltpu.VMEM),
        scratch_shapes=[pltpu.SemaphoreType.DMA((7,)), pltpu.SemaphoreType.DMA((7,)), pltpu.SemaphoreType.DMA],
        compiler_params=_params(),
    )(v)


_ANY = pl.BlockSpec(memory_space=pl.ANY)


def _allgather_halves(shards, *, name):
    nw = len(shards)

    def body(*refs):
        x_refs, out_refs = refs[:nw], refs[nw:2 * nw]
        send_sems, recv_sems, local_sems = refs[2 * nw:]
        x, y, c = _me()
        me, sibling = (x, y, c), (x, y, 1 - c)
        chips = [(1 - x, y), (x, 1 - y), (1 - x, 1 - y)]

        def slot(w, px, py, pc):
            return out_refs[w].at[4 * px + 2 * py + pc]

        def copy(w, k, block, to, src=None):
            return pltpu.make_async_remote_copy(
                src_ref=slot(w, *block) if src is None else src, dst_ref=slot(w, *block),
                send_sem=send_sems.at[w, k], recv_sem=recv_sems.at[w, k], device_id=to, device_id_type=MESH)

        started, local = [], []
        for w in range(nw):
            half = shards[w].shape[0] // 2
            own = x_refs[w].at[pl.ds(c * half, half), :]
            mine = pltpu.make_async_copy(own, slot(w, *me), local_sems.at[w])
            mine.start()
            first = [copy(w, 0, me, sibling, src=own)]
            first += [copy(w, 1 + j, me, (*chip, c), src=own) for j, chip in enumerate(chips)]
            for cp in first:
                cp.start()
            started += first
            local.append(mine)
        for w in range(nw):
            for j, chip in enumerate(chips):
                copy(w, 1 + j, (*chip, c), me).wait_recv()
                fwd = copy(w, 4 + j, (*chip, c), sibling)
                fwd.start()
                started.append(fwd)
        for w in range(nw):
            copy(w, 0, sibling, me).wait_recv()
            for j, chip in enumerate(chips):
                copy(w, 4 + j, (*chip, 1 - c), me).wait_recv()
        for cp in started:
            cp.wait_send()
        for cp in local:
            cp.wait()

    return pl.pallas_call(
        body, name=name,
        out_shape=[jax.ShapeDtypeStruct((8, s.shape[0] // 2, s.shape[1]), s.dtype) for s in shards],
        in_specs=[_ANY] * nw, out_specs=[_ANY] * nw,
        scratch_shapes=[pltpu.SemaphoreType.DMA((nw, 7)), pltpu.SemaphoreType.DMA((nw, 7)), pltpu.SemaphoreType.DMA((nw,))],
        compiler_params=_params(),
    )(*shards)


def _sibling_send_halves(arrs, *, name):
    nw = len(arrs)

    def body(*refs):
        x_refs, out_refs, send_sems, recv_sems = refs[:nw], refs[nw:2 * nw], refs[2 * nw], refs[2 * nw + 1]
        x, y, c = _me()
        cps = []
        for w in range(nw):
            half = arrs[w].shape[1] // 2
            cp = pltpu.make_async_remote_copy(
                src_ref=x_refs[w].at[:, pl.ds((1 - c) * half, half), :], dst_ref=out_refs[w],
                send_sem=send_sems.at[w], recv_sem=recv_sems.at[w], device_id=(x, y, 1 - c), device_id_type=MESH)
            cp.start()
            cps.append(cp)
        for cp in cps:
            cp.wait()

    return pl.pallas_call(
        body, name=name,
        out_shape=[jax.ShapeDtypeStruct((a.shape[0], a.shape[1] // 2, a.shape[2]), a.dtype) for a in arrs],
        in_specs=[_ANY] * nw, out_specs=[_ANY] * nw,
        scratch_shapes=[pltpu.SemaphoreType.DMA((nw,)), pltpu.SemaphoreType.DMA((nw,))],
        compiler_params=_params(),
    )(*arrs)


def _sibling_swap(arrs, *, name):
    nw = len(arrs)

    def body(*refs):
        x_refs, out_refs, send_sems, recv_sems = refs[:nw], refs[nw:2 * nw], refs[2 * nw], refs[2 * nw + 1]
        x, y, c = _me()
        cps = []
        for w in range(nw):
            cp = pltpu.make_async_remote_copy(
                src_ref=x_refs[w], dst_ref=out_refs[w], send_sem=send_sems.at[w], recv_sem=recv_sems.at[w],
                device_id=(x, y, 1 - c), device_id_type=MESH)
            cp.start()
            cps.append(cp)
        for cp in cps:
            cp.wait()

    return pl.pallas_call(
        body, name=name, out_shape=[jax.ShapeDtypeStruct(a.shape, a.dtype) for a in arrs],
        in_specs=[_ANY] * nw, out_specs=[_ANY] * nw,
        scratch_shapes=[pltpu.SemaphoreType.DMA((nw,)), pltpu.SemaphoreType.DMA((nw,))],
        compiler_params=_params(),
    )(*arrs)


def _adamw_halves(w, own, sib, m, v, c_arr, *, name):
    r, cols = w.shape
    h = r // 2
    tm = _tile(h, max(SUBLANES, (192 * 1024) // cols // SUBLANES * SUBLANES), SUBLANES)
    nb = h // tm

    def body(c_ref, w_ref, own_ref, sib_ref, m_ref, v_ref, g_out, d_out, m_out, v_out):
        g = jnp.where(pl.program_id(0) == c_ref[0], own_ref[...], sib_ref[...])
        g_out[...] = g
        d_out[...], m_out[...], v_out[...] = _adamw_math(w_ref[...], g, m_ref[...], v_ref[...])

    full = pl.BlockSpec((tm, cols), lambda hh, i, c_ref: (hh * nb + i, 0))
    half = pl.BlockSpec((tm, cols), lambda hh, i, c_ref: (i, 0))
    return pl.pallas_call(
        body, name=name,
        grid_spec=pltpu.PrefetchScalarGridSpec(num_scalar_prefetch=1, grid=(2, nb),
                                               in_specs=[full, half, half, full, full], out_specs=[full] * 4),
        out_shape=[jax.ShapeDtypeStruct((r, cols), F32)] * 4,
        compiler_params=_params(("parallel", "parallel")),
    )(c_arr, w, own, sib, m, v)


_HBM = pl.BlockSpec(memory_space=pltpu.HBM)
_SEM = pl.BlockSpec(memory_space=pltpu.SEMAPHORE)
_DATAFLOW = pltpu.SideEffectType.DATAFLOW_SIDE_EFFECTING


def _chip_exchange_start(arrs, *, name):
    nw = len(arrs)

    def body(*refs):
        x_refs, land_refs, send_sems, recv_sems = refs[:nw], refs[nw:2 * nw], refs[2 * nw], refs[2 * nw + 1]
        token = refs[-1]
        x, y, c = _me()
        s_me = 2 * x + y
        for w in range(nw):
            for k, (px, py) in enumerate([(1 - x, y), (x, 1 - y), (1 - x, 1 - y)]):
                pltpu.make_async_remote_copy(
                    src_ref=x_refs[w].at[2 * px + py], dst_ref=land_refs[w].at[s_me], send_sem=send_sems.at[3 * w + k],
                    recv_sem=recv_sems.at[3 * w + k], device_id=(px, py, c), device_id_type=MESH).start()
        token[...] = jnp.zeros_like(token)

    hbm = [pltpu.HBM(a.shape, a.dtype) for a in arrs]
    outs = pl.pallas_call(
        body, name=name,
        out_shape=(pltpu.SemaphoreType.DMA((3 * nw,)), pltpu.SemaphoreType.DMA((3 * nw,)), *hbm, *hbm,
                   jax.ShapeDtypeStruct((SUBLANES, LANES), F32)),
        in_specs=[_HBM] * (2 * nw), out_specs=(_SEM, _SEM, *([_HBM] * (2 * nw)), pl.BlockSpec(memory_space=pltpu.VMEM)),
        input_output_aliases={i: 2 + i for i in range(2 * nw)},
        compiler_params=pltpu.CompilerParams(has_side_effects=_DATAFLOW),
    )(*[pltpu.with_memory_space_constraint(a, pltpu.HBM) for a in arrs],
      *[pltpu.with_memory_space_constraint(lax.empty(a.shape, a.dtype), pltpu.HBM) for a in arrs])
    return outs[0], outs[1], list(outs[2:2 + nw]), list(outs[2 + nw:2 + 2 * nw]), outs[-1]


def _allgather_start(shards, *, name):
    nw = len(shards)

    def body(*refs):
        x_refs, land_refs, send_sems, recv_sems = refs[:nw], refs[nw:2 * nw], refs[2 * nw], refs[2 * nw + 1]
        token = refs[-1]
        x, y, c = _me()
        me = 4 * x + 2 * y + c
        for w in range(nw):
            half = shards[w].shape[0] // 2
            own = x_refs[w].at[pl.ds(c * half, half), :]
            for k, to in enumerate([(x, y, 1 - c), (1 - x, y, c), (x, 1 - y, c), (1 - x, 1 - y, c)]):
                pltpu.make_async_remote_copy(
                    src_ref=own, dst_ref=land_refs[w].at[me], send_sem=send_sems.at[4 * w + k],
                    recv_sem=recv_sems.at[4 * w + k], device_id=to, device_id_type=MESH).start()
        token[...] = jnp.zeros_like(token)

    lands = [pltpu.HBM((8, s.shape[0] // 2, s.shape[1]), s.dtype) for s in shards]
    outs = pl.pallas_call(
        body, name=name,
        out_shape=(pltpu.SemaphoreType.DMA((4 * nw,)), pltpu.SemaphoreType.DMA((4 * nw,)),
                   *[pltpu.HBM(s.shape, s.dtype) for s in shards], *lands, jax.ShapeDtypeStruct((SUBLANES, LANES), F32)),
        in_specs=[_HBM] * (2 * nw), out_specs=(_SEM, _SEM, *([_HBM] * (2 * nw)), pl.BlockSpec(memory_space=pltpu.VMEM)),
        input_output_aliases={i: 2 + i for i in range(2 * nw)},
        compiler_params=pltpu.CompilerParams(has_side_effects=_DATAFLOW),
    )(*[pltpu.with_memory_space_constraint(s, pltpu.HBM) for s in shards],
      *[pltpu.with_memory_space_constraint(lax.empty(l.shape, l.dtype), pltpu.HBM) for l in lands])
    return outs[0], outs[1], list(outs[2:2 + nw]), list(outs[2 + nw:2 + 2 * nw]), outs[-1]


def _allgather_wait(send_sems, recv_sems, srcs, lands, after, *, name):
    nw = len(srcs)

    def body(*refs):
        x_refs, land_refs, send_sems, recv_sems = refs[:nw], refs[nw:2 * nw], refs[2 * nw], refs[2 * nw + 1]
        x, y, c = _me()
        for w in range(nw):
            half = srcs[w].shape[0] // 2
            own = x_refs[w].at[pl.ds(c * half, half), :]
            for k, (px, py, pc) in enumerate([(x, y, 1 - c), (1 - x, y, c), (x, 1 - y, c), (1 - x, 1 - y, c)]):
                cp = pltpu.make_async_remote_copy(
                    src_ref=own, dst_ref=land_refs[w].at[4 * px + 2 * py + pc], send_sem=send_sems.at[4 * w + k],
                    recv_sem=recv_sems.at[4 * w + k], device_id=(px, py, pc), device_id_type=MESH)
                cp.wait_send()
                cp.wait_recv()

    outs = pl.pallas_call(
        body, name=name,
        out_shape=(*[pltpu.HBM(a.shape, a.dtype) for a in srcs], *[pltpu.HBM(a.shape, a.dtype) for a in lands]),
        in_specs=[_HBM] * (2 * nw) + [_SEM, _SEM, _ANY], out_specs=tuple([_HBM] * (2 * nw)),
        input_output_aliases={i: i for i in range(2 * nw)},
        compiler_params=pltpu.CompilerParams(has_side_effects=_DATAFLOW),
    )(*srcs, *lands, send_sems, recv_sems, after)
    return list(outs[:nw]), list(outs[nw:])


def _pass_to_sibling(lands, *, name):
    nw = len(lands)

    def body(*refs):
        x_refs, out_refs, send_sems, recv_sems = refs[:nw], refs[nw:2 * nw], refs[2 * nw], refs[2 * nw + 1]
        x, y, c = _me()
        chips = [(1 - x, y), (x, 1 - y), (1 - x, 1 - y)]
        cps = []
        for w in range(nw):
            for k, (px, py) in enumerate(chips):
                cp = pltpu.make_async_remote_copy(
                    src_ref=x_refs[w].at[4 * px + 2 * py + c], dst_ref=out_refs[w].at[4 * px + 2 * py + c],
                    send_sem=send_sems.at[3 * w + k], recv_sem=recv_sems.at[3 * w + k], device_id=(x, y, 1 - c),
                    device_id_type=MESH)
                cp.start()
                cps.append(cp)
        for w in range(nw):
            for k, (px, py) in enumerate(chips):
                pltpu.make_async_remote_copy(
                    src_ref=x_refs[w].at[4 * px + 2 * py + c], dst_ref=out_refs[w].at[4 * px + 2 * py + 1 - c],
                    send_sem=send_sems.at[3 * w + k], recv_sem=recv_sems.at[3 * w + k], device_id=(x, y, 1 - c),
                    device_id_type=MESH).wait_recv()
        for cp in cps:
            cp.wait_send()

    return pl.pallas_call(
        body, name=name, out_shape=[jax.ShapeDtypeStruct(a.shape, a.dtype) for a in lands],
        in_specs=[_ANY] * nw, out_specs=[_ANY] * nw, input_output_aliases={i: i for i in range(nw)},
        scratch_shapes=[pltpu.SemaphoreType.DMA((3 * nw,)), pltpu.SemaphoreType.DMA((3 * nw,))],
        compiler_params=_params(),
    )(*lands)


def _chip_exchange_wait(send_sems, recv_sems, srcs, lands, after, *, name):
    nw = len(srcs)

    def body(*refs):
        x_refs, land_refs, send_sems, recv_sems = refs[:nw], refs[nw:2 * nw], refs[2 * nw], refs[2 * nw + 1]
        x, y, c = _me()
        for w in range(nw):
            for k, (px, py) in enumerate([(1 - x, y), (x, 1 - y), (1 - x, 1 - y)]):
                cp = pltpu.make_async_remote_copy(
                    src_ref=x_refs[w].at[2 * px + py], dst_ref=land_refs[w].at[2 * px + py], send_sem=send_sems.at[3 * w + k],
                    recv_sem=recv_sems.at[3 * w + k], device_id=(px, py, c), device_id_type=MESH)
                cp.wait_send()
                cp.wait_recv()

    hbm = [pltpu.HBM(a.shape, a.dtype) for a in srcs]
    outs = pl.pallas_call(
        body, name=name, out_shape=(*hbm, *hbm),
        in_specs=[_HBM] * (2 * nw) + [_SEM, _SEM, _ANY], out_specs=tuple([_HBM] * (2 * nw)),
        input_output_aliases={i: i for i in range(2 * nw)},
        compiler_params=pltpu.CompilerParams(has_side_effects=_DATAFLOW),
    )(*srcs, *lands, send_sems, recv_sems, after)
    return list(outs[:nw]), list(outs[nw:])


def _sum_slabs(landed, own_src, s_arr, after, *, name, tm=512):
    S, h, w = landed.shape
    tm = _tile(h, tm, 16)

    def body(s_ref, *refs):
        own = refs[S][0].astype(F32)
        acc = None
        for s in range(S):
            term = jnp.where(s_ref[0] == s, own, refs[s][0].astype(F32))
            acc = term if acc is None else acc + term
        refs[S + 2][...] = acc

    def slab(s):
        return pl.BlockSpec((1, tm, w), lambda i, s_ref: (jnp.where(s_ref[0] == s, (s + 1) % S, s), i, 0))

    return pl.pallas_call(
        body, name=name,
        grid_spec=pltpu.PrefetchScalarGridSpec(
            num_scalar_prefetch=1, grid=(h // tm,),
            in_specs=[slab(s) for s in range(S)] + [pl.BlockSpec((1, tm, w), lambda i, s_ref: (s_ref[0], i, 0)), _ANY],
            out_specs=pl.BlockSpec((tm, w), lambda i, s_ref: (i, 0))),
        out_shape=jax.ShapeDtypeStruct((h, w), F32),
        compiler_params=_params(("parallel",)),
    )(s_arr, *([landed] * S), own_src, after)


def _half_add(g, recv, c_arr, *, name):
    S, r, w = g.shape
    h = r // 2
    tm = _tile(h, 512, 16)
    nb = h // tm

    def body(c_ref, g_ref, r_ref, o_ref):
        o_ref[...] = (g_ref[...] + r_ref[...]).astype(BF16)

    return pl.pallas_call(
        body, name=name,
        grid_spec=pltpu.PrefetchScalarGridSpec(
            num_scalar_prefetch=1, grid=(S, nb),
            in_specs=[pl.BlockSpec((1, tm, w), lambda s, i, c_ref: (s, c_ref[0] * nb + i, 0)),
                      pl.BlockSpec((1, tm, w), lambda s, i, c_ref: (s, i, 0))],
            out_specs=pl.BlockSpec((1, tm, w), lambda s, i, c_ref: (s, i, 0))),
        out_shape=jax.ShapeDtypeStruct((S, h, w), BF16),
        compiler_params=_params(("parallel", "parallel")),
    )(c_arr, g, recv)


def _layout(sizes, width, part_mult, total_mult):
    offs, rows, r = [], [], 0
    for n in sizes:
        k = -(-n // width)
        offs.append(r)
        rows.append(k)
        r += -(-k // part_mult) * part_mult
    return offs, rows, -(-r // total_mult) * total_mult


def _pack(arrs, width, part_mult, total_mult, dtype, lead=()):
    nl = len(lead)
    sizes = [math.prod(a.shape[nl:]) for a in arrs]
    offs, rows, total = _layout(sizes, width, part_mult, total_mult)
    parts, r = [], 0
    for a, n, o, k in zip(arrs, sizes, offs, rows):
        kp = -(-k // part_mult) * part_mult
        flat = a.reshape(*lead, n).astype(dtype)
        if kp * width > n:
            flat = jnp.pad(flat, [(0, 0)] * nl + [(0, kp * width - n)])
        parts.append(flat.reshape(*lead, kp, width))
        r = o + kp
    if total > r:
        parts.append(jnp.zeros((*lead, total - r, width), dtype))
    return jnp.concatenate(parts, axis=nl)


def _unpack(pool, shapes, width, part_mult, total_mult):
    lead = pool.shape[:-2]
    sizes = [math.prod(s) for s in shapes]
    offs, rows, _ = _layout(sizes, width, part_mult, total_mult)
    out = []
    for s, n, o, k in zip(shapes, sizes, offs, rows):
        flat = lax.slice_in_dim(pool, o, o + k, axis=len(lead)).reshape(*lead, k * width)
        out.append(lax.slice_in_dim(flat, 0, n, axis=len(lead)).reshape(*lead, *s))
    return out


_WEIGHTS = ("c_ctx", "w_ada", "b_ada", "g_pre_mix", "g_post_mix", "g_pre_ffn", "g_post_ffn", "w_in", "b_merge",
            "dn_conv", "dn_a_log", "dn_dt_bias", "dn_onorm", "lru_conv", "lru_conv_b", "lru_w_rg", "lru_b_rg",
            "lru_w_ig", "lru_b_ig", "lru_lambda", "w_branch_dn", "w_branch_lru", "w_out", "w_up", "ffn_dw",
            "ffn_dw_b", "w_down")
_BIG = {"w_ada": True, "w_in": True, "w_branch_dn": False, "w_branch_lru": False, "w_out": False, "w_up": True,
        "w_down": False}
_SMALL_SHARDED = ("dn_conv", "lru_conv", "lru_b_rg", "lru_b_ig", "lru_lambda", "ffn_dw")
_NCHIP = 4
_FLAT_PART = 8
_FLAT_TOTAL = 256


def _to_chip_shards(g, by_cols):
    if by_cols:
        return g.reshape(g.shape[0], _NCHIP, g.shape[1] // _NCHIP).transpose(1, 0, 2)
    return g.reshape(_NCHIP, g.shape[0] // _NCHIP, g.shape[1])


def _from_chip_shards(s, by_cols):
    if by_cols:
        return s.transpose(1, 0, 2).reshape(s.shape[1], _NCHIP * s.shape[2])
    return s.reshape(_NCHIP * s.shape[1], s.shape[2])


def _dsilu(x):
    s = _sigmoid(x)
    return s * (1.0 + x * (1.0 - s))


def kernel(x, c, ctx, c_ctx, w_ada, b_ada, g_pre_mix, g_post_mix, g_pre_ffn, g_post_ffn, w_in, b_merge, dn_conv, dn_a_log, dn_dt_bias, dn_onorm, lru_conv, lru_conv_b, lru_w_rg, lru_b_rg, lru_w_ig, lru_b_ig, lru_lambda, w_branch_dn, w_branch_lru, w_out, w_up, ffn_dw, ffn_dw_b, w_down, loss_target, m_c_ctx, m_w_ada, m_b_ada, m_g_pre_mix, m_g_post_mix, m_g_pre_ffn, m_g_post_ffn, m_w_in, m_b_merge, m_dn_conv, m_dn_a_log, m_dn_dt_bias, m_dn_onorm, m_lru_conv, m_lru_conv_b, m_lru_w_rg, m_lru_b_rg, m_lru_w_ig, m_lru_b_ig, m_lru_lambda, m_w_branch_dn, m_w_branch_lru, m_w_out, m_w_up, m_ffn_dw, m_ffn_dw_b, m_w_down, v_c_ctx, v_w_ada, v_b_ada, v_g_pre_mix, v_g_post_mix, v_g_pre_ffn, v_g_post_ffn, v_w_in, v_b_merge, v_dn_conv, v_dn_a_log, v_dn_dt_bias, v_dn_onorm, v_lru_conv, v_lru_conv_b, v_lru_w_rg, v_lru_b_rg, v_lru_w_ig, v_lru_b_ig, v_lru_lambda, v_w_branch_dn, v_w_branch_lru, v_w_out, v_w_up, v_ffn_dw, v_ffn_dw_b, v_w_down):
    W = dict(zip(_WEIGHTS, (c_ctx, w_ada, b_ada, g_pre_mix, g_post_mix, g_pre_ffn, g_post_ffn, w_in, b_merge, dn_conv,
                            dn_a_log, dn_dt_bias, dn_onorm, lru_conv, lru_conv_b, lru_w_rg, lru_b_rg, lru_w_ig, lru_b_ig,
                            lru_lambda, w_branch_dn, w_branch_lru, w_out, w_up, ffn_dw, ffn_dw_b, w_down)))
    Mo = dict(zip(_WEIGHTS, (m_c_ctx, m_w_ada, m_b_ada, m_g_pre_mix, m_g_post_mix, m_g_pre_ffn, m_g_post_ffn, m_w_in,
                             m_b_merge, m_dn_conv, m_dn_a_log, m_dn_dt_bias, m_dn_onorm, m_lru_conv, m_lru_conv_b,
                             m_lru_w_rg, m_lru_b_rg, m_lru_w_ig, m_lru_b_ig, m_lru_lambda, m_w_branch_dn,
                             m_w_branch_lru, m_w_out, m_w_up, m_ffn_dw, m_ffn_dw_b, m_w_down)))
    Vo = dict(zip(_WEIGHTS, (v_c_ctx, v_w_ada, v_b_ada, v_g_pre_mix, v_g_post_mix, v_g_pre_ffn, v_g_post_ffn, v_w_in,
                             v_b_merge, v_dn_conv, v_dn_a_log, v_dn_dt_bias, v_dn_onorm, v_lru_conv, v_lru_conv_b,
                             v_lru_w_rg, v_lru_b_rg, v_lru_w_ig, v_lru_b_ig, v_lru_lambda, v_w_branch_dn,
                             v_w_branch_lru, v_w_out, v_w_up, v_ffn_dw, v_ffn_dw_b, v_w_down)))
    B, N, D = x.shape
    NC = ctx.shape[1]
    T = NC + N
    H, HD = dn_a_log.shape[-1], dn_onorm.shape[-1]
    DNW = H * HD
    LW, LBD = lru_conv_b.shape[-1], lru_w_rg.shape[-1]
    DFF = ffn_dw_b.shape[-1]
    LC = LANES
    x_i, y_i, c_i = _me()
    s_me = 2 * x_i + y_i
    tm = _tile(math.gcd(NC, N), 256, 16)

    def whole(n, g):
        r, w_ = W[n].shape[1:]
        return g.reshape(_NCHIP, r, w_) if _BIG[n] else g.reshape(_NCHIP * r, w_)

    first = ("w_ada", "w_in")
    later = tuple(n for n in _BIG if n not in first)
    shard16 = {n: W[n][0].astype(BF16) for n in _BIG}
    full = {n: whole(n, g) for n, g in zip(first, _allgather_halves([shard16[n] for n in first], name="allgather_first"))}

    small_local = [W[n][0].reshape(-1, W[n].shape[-1]) for n in _SMALL_SHARDED]
    small_shapes = [a.shape for a in small_local]
    spack = _pack(small_local, LANES, _FLAT_PART, _FLAT_PART, F32)
    sgath = _allgather_small(spack)[0::2]
    sfull = {n: _from_chip_shards(s, True)
             for n, s in zip(_SMALL_SHARDED, _unpack(sgath, small_shapes, LANES, _FLAT_PART, _FLAT_PART))}

    later16, sgath = lax.optimization_barrier(([shard16[n] for n in later], sgath))
    ag_send, ag_recv, ag_src, ag_land, ag_token = _allgather_start(later16, name="ag_start")

    o_a = 4 * DNW
    o_xl = o_a + 4 * H
    o_mg = o_xl + 2 * LW
    wi_ = _from_chip_shards(full["w_in"], True)
    nj = LW // LC
    lru_cols = jnp.stack([wi_[:, o_xl:o_xl + LW].reshape(D, nj, LC), wi_[:, o_xl + LW:o_mg].reshape(D, nj, LC)],
                         axis=2).reshape(D, 2 * LW)
    wp = jnp.concatenate([wi_[:, :o_a], lru_cols, wi_[:, o_mg:], wi_[:, o_a:o_xl],
                          jnp.zeros((D, LANES - 4 * H), BF16)], axis=1)
    p_lru, p_mg, p_ab = 4 * DNW, 4 * DNW + 2 * LW, 4 * DNW + 2 * LW + 2 * D
    PW = p_ab + LANES

    MR = LANES
    cond = jnp.concatenate([c, c_ctx[None], jnp.zeros((MR - B - 1, D), F32)], axis=0)
    silu_rows = _rowwise(lambda a: (_silu(a),), [cond], [F32], name="cond_silu")[0]
    mod = _matmul(silu_rows, full["w_ada"], b_shards=(0, _NCHIP), name="ada_fwd") + b_ada + ag_token[0, 0]
    mx = mod[:B].reshape(B, 6, D)
    mc = mod[B].reshape(6, D)
    zero = jnp.zeros((B, D), F32)
    tab = jnp.stack([jnp.stack([jnp.broadcast_to(mc[0], (B, D)), jnp.broadcast_to(mc[1], (B, D))] + [zero] * 6, axis=1),
                     jnp.stack([mx[:, 0], mx[:, 1]] + [zero] * 6, axis=1)], axis=1)
    vecs = jnp.stack([mx[:, 2], mx[:, 3], mx[:, 4], mx[:, 5]] + [zero] * 4, axis=1)
    gains = jnp.concatenate([g_post_mix, g_pre_ffn, g_post_ffn, jnp.zeros((5, D), F32)], axis=0)

    h = jnp.concatenate([ctx, x], axis=1)
    u = _premix_fwd(h, g_pre_mix, tab, nc=NC, tm=tm)
    p = _matmul(u, wp, name="in_fwd")
    dkw = dict(B=B, T=T, nc=NC, H=H, HD=HD)
    qkv = _dnprep_fwd(p, sfull["dn_conv"], **dkw)
    prm = jnp.concatenate([
        jnp.concatenate([dn_a_log.reshape(1, 2 * H), jnp.zeros((1, LANES - 2 * H), F32)], axis=1),
        jnp.concatenate([dn_dt_bias.reshape(1, 2 * H), jnp.zeros((1, LANES - 2 * H), F32)], axis=1),
        jnp.zeros((6, LANES), F32)], axis=0)
    gtm = _tile(B * T, 512, 16)
    gb = _gb_fwd(p, prm, rows=B * T, col0=p_ab, H=H, tm=gtm)
    y_dn, o_dn, *dn_res = _delta_fwd(qkv, gb, p, dn_onorm, **dkw)
    lv = jnp.concatenate([lru_conv_b, sfull["lru_b_rg"], sfull["lru_b_ig"], sfull["lru_lambda"], jnp.zeros((1, LW), F32)], axis=0)
    wr = _blockdiag(lru_w_rg[0], LC).astype(BF16)
    wi = _blockdiag(lru_w_ig[0], LC).astype(BF16)
    lkw = dict(B=B, T=T, nc=NC, LW=LW, col0=p_lru, C=LC)
    y_lru = _lru_fwd(p, sfull["lru_conv"], lv, wr, wi, **lkw)
    ag_src, ag_land = _allgather_wait(ag_send, ag_recv, ag_src, ag_land, y_lru, name="ag_wait")
    me_piece = 4 * x_i + 2 * y_i + c_i
    for n, src, land in zip(later, ag_src, _pass_to_sibling(ag_land, name="ag_pass")):
        own = lax.dynamic_slice_in_dim(src, c_i * (src.shape[0] // 2), src.shape[0] // 2, axis=0)
        full[n] = whole(n, lax.dynamic_update_index_in_dim(land, own, me_piece, axis=0))
    Ydn = _matmul(y_dn, full["w_branch_dn"], name="bdn_fwd")
    Ylru = _matmul(y_lru, full["w_branch_lru"], name="blru_fwd")
    mkw = dict(B=B, T=T, nc=NC, D=D, col0=p_mg, tm=tm)
    mixin = _merge_fwd(p, Ydn, Ylru, b_merge, **mkw)
    mix = _matmul(mixin, full["w_out"], name="out_fwd")
    h1, u2 = _post_fwd(x, mix, gains, vecs, tm=tm)
    F = _matmul(u2, full["w_up"], b_shards=(0, _NCHIP), name="up_fwd")
    w9 = sfull["ffn_dw"]
    ftc = _tile(DFF, 256)
    f, f_pre = _ffn_act_fwd(F, w9, ffn_dw_b, B=B, N=N, DFF=DFF, tc=ftc)
    dn = _matmul(f, full["w_down"], name="down_fwd")
    ddn, dout, sums_f = _final(h1, dn, loss_target, gains, vecs, tm=tm)

    G = {}
    df = _matmul(ddn, full["w_down"], tb=True, name="down_bwd_x")
    G["w_down"] = _matmul(f, ddn, ta=True, name="down_bwd_w")
    dFg, dFv, dwb = _ffn_act_bwd(F, f_pre, w9, df, B=B, N=N, DFF=DFF, tc=ftc)
    hs = _NCHIP // 2
    du2 = _matmul(dFg, full["w_up"], tb=True, b_shards=(0, hs), name="up_bwd_xg")
    du2 = _matmul(dFv, full["w_up"], tb=True, b_shards=(hs, hs), add=du2, name="up_bwd_xv")
    gup = _matmul(u2, dFg, ta=True, out_shards=hs, into=(lax.empty((_NCHIP, D, DFF // hs), F32), 0), name="up_bwd_wg")
    G["w_up"] = _matmul(u2, dFv, ta=True, out_shards=hs, into=(gup, hs), name="up_bwd_wv")
    dx1, dmix, sums_p = _post_bwd(x, mix, gains, vecs, dout, du2, tm=tm)
    dmixin = _matmul(dmix, full["w_out"], tb=True, name="out_bwd_x")
    G["w_out"] = _matmul(mixin, dmix, ta=True, name="out_bwd_w")
    dp = jnp.zeros((B * T, PW), BF16)
    dYdn, dYlru, dp, sums_m = _merge_bwd(p, Ydn, Ylru, b_merge, dmixin, dp, **mkw)
    dy_dn = _matmul(dYdn, full["w_branch_dn"], tb=True, name="bdn_bwd_x")
    G["w_branch_dn"] = _matmul(y_dn, dYdn, ta=True, name="bdn_bwd_w")
    dy_lru = _matmul(dYlru, full["w_branch_lru"], tb=True, name="blru_bwd_x")
    G["w_branch_lru"] = _matmul(y_lru, dYlru, ta=True, name="blru_bwd_w")

    c_arr = c_i.astype(jnp.int32).reshape(1)
    s_arr = s_me.astype(jnp.int32).reshape(1)

    def chip_sums(names, tag):
        slabs = [G[n] if _BIG[n] else G[n].reshape(_NCHIP, G[n].shape[0] // _NCHIP, G[n].shape[1]) for n in names]
        from_sibling = _sibling_send_halves(slabs, name="rs_sibling_" + tag)
        return [_half_add(g, r, c_arr, name="rs_add_" + n) for n, g, r in zip(names, slabs, from_sibling)]

    early = tuple(n for n in _BIG if n in G)
    late = tuple(n for n in _BIG if n not in G)
    cx_send, cx_recv, cx_src, cx_land, cx_token = _chip_exchange_start(chip_sums(early, "early"), name="cx_start")
    dp, dcw_l, dlv, dwr, dwi = _lru_bwd(p, sfull["lru_conv"], lv + cx_token[0, 0], wr, wi, dy_lru, dp, **lkw)
    dqkv, dgb, dp, don = _delta_bwd(qkv, gb, p, dn_onorm, o_dn, dn_res, dy_dn, dp, **dkw)
    dp, dprm = _gb_bwd(p, prm, dgb, dp, rows=B * T, col0=p_ab, H=H, tm=gtm)
    dp, dcw_d = _dnprep_bwd(p, sfull["dn_conv"], dqkv, dp, **dkw)
    dU = _matmul(dp, wp, tb=True, name="in_bwd_x")
    dwp = _matmul(u, dp, ta=True, name="in_bwd_w")
    grad_x, sums_pm = _premix_bwd(h, g_pre_mix, tab, dU, dx1, nc=NC, tm=tm)
    dlru = dwp[:, p_lru:p_mg].reshape(D, nj, 2, LC)
    G["w_in"] = _to_chip_shards(jnp.concatenate([dwp[:, :o_a], dwp[:, p_ab:p_ab + 4 * H], dlru[:, :, 0].reshape(D, LW),
                                                 dlru[:, :, 1].reshape(D, LW), dwp[:, p_mg:p_ab]], axis=1), True)

    dmod_x = jnp.stack([sums_pm[:, 1, 0], sums_pm[:, 1, 1], sums_p[:, 0], sums_p[:, 1], sums_p[:, 2], sums_f[:, 0]],
                       axis=1).reshape(B, 6 * D)
    dmod_c = jnp.concatenate([sums_pm[:, 0, 0].sum(0), sums_pm[:, 0, 1].sum(0), jnp.zeros((4 * D,), F32)])[None]
    dmod = jnp.concatenate([dmod_x, dmod_c, jnp.zeros((MR - B - 1, 6 * D), F32)], axis=0)
    G["w_ada"] = _matmul(silu_rows, dmod, ta=True, out_shards=_NCHIP, name="ada_bwd_w")
    dsilu = _matmul(dmod, full["w_ada"], tb=True, b_shards=(0, _NCHIP), name="ada_bwd_x")

    g_small = {
        "c_ctx": dsilu[B] * _dsilu(c_ctx),
        "b_ada": dmod[:B + 1].sum(0)[None],
        "g_pre_mix": sums_pm[:, :, 2].sum((0, 1))[None],
        "g_post_mix": sums_p[:, 3].sum(0)[None],
        "g_pre_ffn": sums_p[:, 4].sum(0)[None],
        "g_post_ffn": sums_f[:, 1].sum(0)[None],
        "b_merge": sums_m[0:1],
        "dn_conv": dcw_d[0:4][None],
        "dn_a_log": dprm[0, :2 * H].reshape(1, 2, H),
        "dn_dt_bias": dprm[1, :2 * H].reshape(1, 2, H),
        "dn_onorm": don[:, 0].sum(0)[None],
        "lru_conv": dcw_l[0:4][None],
        "lru_conv_b": dlv[0:1],
        "lru_w_rg": _blockdiag_extract(dwr, LBD)[None],
        "lru_b_rg": dlv[1:3][None],
        "lru_w_ig": _blockdiag_extract(dwi, LBD)[None],
        "lru_b_ig": dlv[3:5][None],
        "lru_lambda": dlv[5:7][None],
        "ffn_dw": dwb[0:9].reshape(1, 3, 3, DFF),
        "ffn_dw_b": dwb[9:10],
    }
    small_names = tuple(n for n in _WEIGHTS if n not in _BIG)
    loss_part = sums_f[:, 2].sum().reshape(1)
    gs_list = [g_small[n] for n in small_names] + [loss_part]
    gs_shapes = [a.shape for a in gs_list]
    gpack = _pack(gs_list, LANES, _FLAT_PART, _FLAT_TOTAL, F32)
    gsum = _sum_lead(_allgather_small(gpack), name="small_sum", tm=512, mult=SUBLANES)
    gs_red = dict(zip(small_names + ("loss",), _unpack(gsum, gs_shapes, LANES, _FLAT_PART, _FLAT_TOTAL)))
    loss = gs_red["loss"][0]

    grads, deltas, new_m, new_v = {}, {}, {}, {}

    def finish(names, lands, srcs, after, tag):
        halves = [_sum_slabs(l, src, s_arr, after, name="rs_sum_" + n) for n, l, src in zip(names, lands, srcs)]
        outs = None
        for n, own, sib in zip(names, halves, _sibling_swap(halves, name="rs_gather_" + tag)):
            shp = W[n].shape
            outs = _adamw_halves(W[n][0], own, sib, Mo[n][0], Vo[n][0], c_arr, name="adamw_" + n)
            grads[n], deltas[n], new_m[n], new_v[n] = (o.reshape(shp) for o in outs)
        return outs[1]

    cx_src, cx_land = _chip_exchange_wait(cx_send, cx_recv, cx_src, cx_land, dsilu, name="cx_wait")
    late_sums, gsum = lax.optimization_barrier((chip_sums(late, "late"), gsum))
    lx_send, lx_recv, lx_src, lx_land, lx_token = _chip_exchange_start(late_sums, name="cx_late_start")
    last_early = finish(early, cx_land, cx_src, lx_token, "early")
    lx_src, lx_land = _chip_exchange_wait(lx_send, lx_recv, lx_src, lx_land, last_early, name="cx_late_wait")
    finish(late, lx_land, lx_src, last_early, "late")
    for n in small_names:
        g = gs_red[n]
        if n in _SMALL_SHARDED:
            k = W[n].shape[-1]
            g = lax.dynamic_slice_in_dim(g, s_me * k, k, axis=g.ndim - 1)
        grads[n] = g.reshape(W[n].shape)
    sm_shapes = [W[n].shape for n in small_names]
    pk = lambda d: _pack([d[n] for n in small_names], LANES, _FLAT_PART, _FLAT_TOTAL, F32)
    d_, m_, v_ = _adamw(pk(W), pk(grads), pk(Mo), pk(Vo), name="adamw_small")
    for dst, pool_ in ((deltas, d_), (new_m, m_), (new_v, v_)):
        dst.update(zip(small_names, _unpack(pool_, sm_shapes, LANES, _FLAT_PART, _FLAT_TOTAL)))
    return (loss, grad_x, *[grads[n] for n in _WEIGHTS], *[deltas[n] for n in _WEIGHTS],
            *[new_m[n] for n in _WEIGHTS], *[new_v[n] for n in _WEIGHTS])
```

```python
import functools
import math

import jax
import jax.numpy as jnp
from jax import lax
from jax.experimental import pallas as pl
from jax.experimental.pallas import tpu as pltpu

F32 = jnp.float32
BF16 = jnp.bfloat16
EPS = 1e-6
GRID_W = 64
CHUNK = 256
LRU_C = 8.0
LANES = 128
SUBLANES = 8
VMEM_LIMIT = 56 * 1024 * 1024
ADAM_LR, ADAM_B1, ADAM_B2, ADAM_EPS, ADAM_WD, ADAM_STEP = 0.001, 0.9, 0.999, 1e-08, 0.01, 10
MESH = pl.DeviceIdType.MESH


def _tile(n, target, mult=LANES):
    best = None
    for t in range(mult, min(n, target) + 1, mult):
        if n % t == 0:
            best = t
    return best if best is not None else n


def _params(sem=None, **kw):
    return pltpu.CompilerParams(dimension_semantics=sem, vmem_limit_bytes=VMEM_LIMIT, **kw)


def _sigmoid(x):
    return 1.0 / (1.0 + jnp.exp(-x))


def _silu(x):
    return x * _sigmoid(x)


def _softplus(x):
    return jnp.maximum(x, 0.0) + jnp.log(1.0 + jnp.exp(-jnp.abs(x)))


def _gelu(x):
    return 0.5 * x * (1.0 + jnp.tanh(math.sqrt(2.0 / math.pi) * (x + 0.044715 * x * x * x)))


def _rmsn(u, gain):
    return u * lax.rsqrt(jnp.mean(u * u, axis=-1, keepdims=True) + EPS) * gain


_MM_VMEM = 40 * 1024 * 1024


_ANY_SPEC = pl.BlockSpec(memory_space=pl.ANY)


def _matmul(a, b, *, ta=False, tb=False, add=None, b_shards=None, out_shards=None, into=None, after=None,
            out_dtype=F32, name, tm=1024, tn=2048, tk=2048):
    (K, M) = a.shape if ta else a.shape[::-1]
    if b_shards is not None:
        s0, ns = b_shards
        bsh = (b.shape[1], ns * b.shape[2])
        nsh = b.shape[2]
    else:
        bsh = b.shape
    N = bsh[0] if tb else bsh[1]
    assert (bsh[1] if tb else bsh[0]) == K, (a.shape, b.shape, ta, tb)
    tm = _tile(M, tm)
    tk = _tile(nsh if (b_shards is not None and tb) else K, tk)
    nlim = nsh if (b_shards is not None and not tb) else (N // out_shards if out_shards else N)
    osz = jnp.dtype(out_dtype).itemsize + (4 if add is not None else 0)
    while True:
        tn_ = _tile(nlim, tn)
        need = 2 * (tm * tk * a.dtype.itemsize + tk * tn_ * b.dtype.itemsize + tm * tn_ * osz) + 4 * tm * tn_
        if need <= _MM_VMEM or tn <= LANES:
            break
        tn //= 2
    tn = tn_
    nk = K // tk
    dims = (((0 if ta else 1,), (1 if tb else 0,)), ((), ()))

    def body(a_ref, b_ref, *rest):
        c_ref = rest[0] if add is not None else None
        o_ref, acc_ref = rest[-2:]
        k = pl.program_id(2)

        @pl.when(k == 0)
        def _():
            acc_ref[...] = jnp.zeros_like(acc_ref) if c_ref is None else c_ref[...]

        bv = b_ref[0] if b_shards is not None else b_ref[...]
        acc_ref[...] += lax.dot_general(a_ref[...].astype(BF16), bv.astype(BF16), dims, preferred_element_type=F32)

        @pl.when(k == nk - 1)
        def _():
            if out_shards:
                o_ref[0] = acc_ref[...].astype(out_dtype)
            else:
                o_ref[...] = acc_ref[...].astype(out_dtype)

    a_spec = pl.BlockSpec((tk, tm), lambda i, j, k: (k, i)) if ta else pl.BlockSpec((tm, tk), lambda i, j, k: (i, k))
    if b_shards is None:
        b_spec = pl.BlockSpec((tn, tk), lambda i, j, k: (j, k)) if tb else pl.BlockSpec((tk, tn), lambda i, j, k: (k, j))
    elif tb:
        per = nsh // tk
        b_spec = pl.BlockSpec((1, tn, tk), lambda i, j, k: (s0 + k // per, j, k % per))
    else:
        per = nsh // tn
        b_spec = pl.BlockSpec((1, tk, tn), lambda i, j, k: (s0 + j // per, k, j % per))
    o_spec = pl.BlockSpec((tm, tn), lambda i, j, k: (i, j))
    extra, alias = (), {}
    if out_shards:
        oper = N // out_shards // tn
        o0 = 0
        out_shape = jax.ShapeDtypeStruct((out_shards, M, N // out_shards), out_dtype)
        if into is not None:
            buf, o0 = into
            out_shape = jax.ShapeDtypeStruct(buf.shape, buf.dtype)
            extra, alias = (buf,), {2 + (add is not None): 0}
        out_spec = pl.BlockSpec((1, tm, tn), lambda i, j, k: (o0 + j // oper, i, j % oper))
    else:
        out_spec, out_shape = o_spec, jax.ShapeDtypeStruct((M, N), out_dtype)
    if after is not None:
        extra = extra + (after,)
    return pl.pallas_call(
        body, name=name, grid=(M // tm, N // tn, nk),
        in_specs=[a_spec, b_spec] + ([o_spec] if add is not None else []) + [_ANY_SPEC] * len(extra),
        out_specs=out_spec, out_shape=out_shape, input_output_aliases=alias,
        scratch_shapes=[pltpu.VMEM((tm, tn), F32)],
        compiler_params=_params(("parallel", "parallel", "arbitrary")),
    )(*((a, b) + ((add,) if add is not None else ()) + extra))


def _premix_math(h, gain, shift, scale):
    return _rmsn(h, gain) * (1.0 + scale) + shift


def _premix_fwd(h, gain, tab, *, nc, tm):
    B, T, D = h.shape
    nt, nct = T // tm, nc // tm

    def body(h_ref, g_ref, tab_ref, u_ref):
        tabv = tab_ref[0, 0]
        u_ref[...] = _premix_math(h_ref[0], g_ref[...], tabv[0:1], tabv[1:2]).astype(BF16)

    return pl.pallas_call(
        body, name="premix_fwd", grid=(B, nt),
        in_specs=[pl.BlockSpec((1, tm, D), lambda b, t: (b, t, 0)),
                  pl.BlockSpec((1, D), lambda b, t: (0, 0)),
                  pl.BlockSpec((1, 1, 8, D), lambda b, t: (b, jnp.where(t < nct, 0, 1), 0, 0))],
        out_specs=pl.BlockSpec((tm, D), lambda b, t: (b * nt + t, 0)),
        out_shape=jax.ShapeDtypeStruct((B * T, D), BF16),
        compiler_params=_params(("parallel", "parallel")),
    )(h, gain, tab)


def _premix_bwd(h, gain, tab, du, dres, *, nc, tm):
    B, T, D = h.shape
    nt, nct = T // tm, nc // tm
    N = T - nc

    def body(h_ref, g_ref, tab_ref, du_ref, dres_ref, dx_ref, sums_ref):
        t = pl.program_id(1)
        tabv = tab_ref[0, 0]
        _, vjp = jax.vjp(_premix_math, h_ref[0], g_ref[...], tabv[0:1], tabv[1:2])
        dh, dgain, dshift, dscale = vjp(du_ref[...].astype(F32))

        @pl.when((t == 0) | (t == nct))
        def _():
            sums_ref[...] = jnp.zeros_like(sums_ref)

        sums_ref[0, 0, 0:1, :] += dshift
        sums_ref[0, 0, 1:2, :] += dscale
        sums_ref[0, 0, 2:3, :] += dgain

        @pl.when(t >= nct)
        def _():
            dx_ref[0] = dres_ref[...] + dh

    lat = lambda b, t: jnp.maximum(t - nct, 0)
    return pl.pallas_call(
        body, name="premix_bwd", grid=(B, nt),
        in_specs=[pl.BlockSpec((1, tm, D), lambda b, t: (b, t, 0)),
                  pl.BlockSpec((1, D), lambda b, t: (0, 0)),
                  pl.BlockSpec((1, 1, 8, D), lambda b, t: (b, jnp.where(t < nct, 0, 1), 0, 0)),
                  pl.BlockSpec((tm, D), lambda b, t: (b * nt + t, 0)),
                  pl.BlockSpec((tm, D), lambda b, t: (b * (nt - nct) + lat(b, t), 0))],
        out_specs=[pl.BlockSpec((1, tm, D), lambda b, t: (b, lat(b, t), 0)),
                   pl.BlockSpec((1, 1, 8, D), lambda b, t: (b, jnp.where(t < nct, 0, 1), 0, 0))],
        out_shape=[jax.ShapeDtypeStruct((B, N, D), F32), jax.ShapeDtypeStruct((B, 2, 8, D), F32)],
        compiler_params=_params(("parallel", "arbitrary")),
    )(h, gain, tab, du, dres)


def _merge_math(mgd, mgl, yd, yl, bd, bl):
    return _sigmoid(mgd + bd) * yd + _sigmoid(mgl + bl) * yl


def _merge_fwd(p, ydn, ylru, b_merge, *, B, T, nc, D, col0, tm):
    N = T - nc
    ntl, nt, nct, cb = N // tm, T // tm, nc // tm, col0 // D

    def body(mgd_ref, mgl_ref, yd_ref, yl_ref, bm_ref, o_ref):
        o_ref[...] = _merge_math(mgd_ref[...], mgl_ref[...], yd_ref[...], yl_ref[...],
                                 bm_ref[:, 0:D], bm_ref[:, D:2 * D]).astype(BF16)

    prow = lambda b, t: b * nt + nct + t
    return pl.pallas_call(
        body, name="merge_fwd", grid=(B, ntl),
        in_specs=[pl.BlockSpec((tm, D), lambda b, t: (prow(b, t), cb)),
                  pl.BlockSpec((tm, D), lambda b, t: (prow(b, t), cb + 1)),
                  pl.BlockSpec((tm, D), lambda b, t: (b * ntl + t, 0)),
                  pl.BlockSpec((tm, D), lambda b, t: (b * ntl + t, 0)),
                  pl.BlockSpec((1, 2 * D), lambda b, t: (0, 0))],
        out_specs=pl.BlockSpec((tm, D), lambda b, t: (b * ntl + t, 0)),
        out_shape=jax.ShapeDtypeStruct((B * N, D), BF16),
        compiler_params=_params(("parallel", "parallel")),
    )(p, p, ydn, ylru, b_merge)


def _merge_bwd(p, ydn, ylru, b_merge, dmix, dp, *, B, T, nc, D, col0, tm):
    N = T - nc
    ntl, nt, nct, cb = N // tm, T // tm, nc // tm, col0 // D
    assert col0 % (2 * D) == 0

    def body(mgd_ref, mgl_ref, yd_ref, yl_ref, bm_ref, dm_ref, dp_any, dyd_ref, dyl_ref, dp_ref, sums_ref):
        _, vjp = jax.vjp(_merge_math, mgd_ref[...], mgl_ref[...], yd_ref[...], yl_ref[...],
                         bm_ref[:, 0:D], bm_ref[:, D:2 * D])
        dmgd, dmgl, dyd, dyl, dbd, dbl = vjp(dm_ref[...])
        dyd_ref[...] = dyd.astype(BF16)
        dyl_ref[...] = dyl.astype(BF16)
        dp_ref[:, 0:D] = dmgd.astype(BF16)
        dp_ref[:, D:2 * D] = dmgl.astype(BF16)

        @pl.when((pl.program_id(0) == 0) & (pl.program_id(1) == 0))
        def _():
            sums_ref[...] = jnp.zeros_like(sums_ref)

        sums_ref[0:1, 0:D] += dbd
        sums_ref[0:1, D:2 * D] += dbl

    prow = lambda b, t: b * nt + nct + t
    row = pl.BlockSpec((tm, D), lambda b, t: (b * ntl + t, 0))
    return pl.pallas_call(
        body, name="merge_bwd", grid=(B, ntl),
        in_specs=[pl.BlockSpec((tm, D), lambda b, t: (prow(b, t), cb)),
                  pl.BlockSpec((tm, D), lambda b, t: (prow(b, t), cb + 1)),
                  row, row, pl.BlockSpec((1, 2 * D), lambda b, t: (0, 0)), row,
                  pl.BlockSpec(memory_space=pl.ANY)],
        out_specs=[row, row,
                   pl.BlockSpec((tm, 2 * D), lambda b, t: (prow(b, t), cb // 2)),
                   pl.BlockSpec((8, 2 * D), lambda b, t: (0, 0))],
        out_shape=[jax.ShapeDtypeStruct((B * N, D), BF16), jax.ShapeDtypeStruct((B * N, D), BF16),
                   jax.ShapeDtypeStruct(dp.shape, dp.dtype), jax.ShapeDtypeStruct((8, 2 * D), F32)],
        input_output_aliases={6: 2},
        compiler_params=_params(("arbitrary", "arbitrary")),
    )(p, p, ydn, ylru, b_merge, dmix, dp)


def _post_math(x, mix, g1, gate, g2, sh, sc):
    h1 = x + _rmsn(mix, g1) * gate
    return h1, _rmsn(h1, g2) * (1.0 + sc) + sh


def _post_fwd(x, mix, gains, vecs, *, tm):
    B, N, D = x.shape
    ntl = N // tm

    def body(x_ref, mix_ref, g_ref, v_ref, h1_ref, u2_ref):
        v = v_ref[0]
        h1, u2 = _post_math(x_ref[0], mix_ref[...], g_ref[0:1], v[0:1], g_ref[1:2], v[1:2], v[2:3])
        h1_ref[...] = h1
        u2_ref[...] = u2.astype(BF16)

    row = pl.BlockSpec((tm, D), lambda b, t: (b * ntl + t, 0))
    return pl.pallas_call(
        body, name="post_fwd", grid=(B, ntl),
        in_specs=[pl.BlockSpec((1, tm, D), lambda b, t: (b, t, 0)), row,
                  pl.BlockSpec((8, D), lambda b, t: (0, 0)), pl.BlockSpec((1, 8, D), lambda b, t: (b, 0, 0))],
        out_specs=[row, row],
        out_shape=[jax.ShapeDtypeStruct((B * N, D), F32), jax.ShapeDtypeStruct((B * N, D), BF16)],
        compiler_params=_params(("parallel", "parallel")),
    )(x, mix, gains, vecs)


def _post_bwd(x, mix, gains, vecs, dh1, du2, *, tm):
    B, N, D = x.shape
    ntl = N // tm

    def body(x_ref, mix_ref, g_ref, v_ref, dh1_ref, du2_ref, dx_ref, dmix_ref, sums_ref):
        v = v_ref[0]
        _, vjp = jax.vjp(_post_math, x_ref[0], mix_ref[...], g_ref[0:1], v[0:1], g_ref[1:2], v[1:2], v[2:3])
        dx, dmix, dg1, dgate, dg2, dsh, dsc = vjp((dh1_ref[...], du2_ref[...]))
        dx_ref[...] = dx
        dmix_ref[...] = dmix.astype(BF16)

        @pl.when(pl.program_id(1) == 0)
        def _():
            sums_ref[...] = jnp.zeros_like(sums_ref)

        sums_ref[0, 0:1, :] += dgate
        sums_ref[0, 1:2, :] += dsh
        sums_ref[0, 2:3, :] += dsc
        sums_ref[0, 3:4, :] += dg1
        sums_ref[0, 4:5, :] += dg2

    row = pl.BlockSpec((tm, D), lambda b, t: (b * ntl + t, 0))
    return pl.pallas_call(
        body, name="post_bwd", grid=(B, ntl),
        in_specs=[pl.BlockSpec((1, tm, D), lambda b, t: (b, t, 0)), row,
                  pl.BlockSpec((8, D), lambda b, t: (0, 0)), pl.BlockSpec((1, 8, D), lambda b, t: (b, 0, 0)), row, row],
        out_specs=[row, row, pl.BlockSpec((1, 8, D), lambda b, t: (b, 0, 0))],
        out_shape=[jax.ShapeDtypeStruct((B * N, D), F32), jax.ShapeDtypeStruct((B * N, D), BF16),
                   jax.ShapeDtypeStruct((B, 8, D), F32)],
        compiler_params=_params(("parallel", "arbitrary")),
    )(x, mix, gains, vecs, dh1, du2)


def _final_math(dn, g4, gate5):
    return _rmsn(dn, g4) * gate5


def _final(h1, dn, target, gains, vecs, *, tm):
    B, N, D = target.shape
    ntl = N // tm

    def body(h1_ref, dn_ref, t_ref, g_ref, v_ref, ddn_ref, dout_ref, sums_ref):
        v = v_ref[0]
        y, vjp = jax.vjp(_final_math, dn_ref[...], g_ref[2:3], v[3:4])
        err = h1_ref[...] + y - t_ref[0]
        dout = err * (1.0 / D)
        ddn, dg4, dgate5 = vjp(dout)
        ddn_ref[...] = ddn.astype(BF16)
        dout_ref[...] = dout

        @pl.when(pl.program_id(1) == 0)
        def _():
            sums_ref[...] = jnp.zeros_like(sums_ref)

        sums_ref[0, 0:1, :] += dgate5
        sums_ref[0, 1:2, :] += dg4
        sums_ref[0, 2:3, :] += jnp.sum(err * err, axis=0, keepdims=True) * (0.5 / D)

    row = pl.BlockSpec((tm, D), lambda b, t: (b * ntl + t, 0))
    return pl.pallas_call(
        body, name="final", grid=(B, ntl),
        in_specs=[row, row, pl.BlockSpec((1, tm, D), lambda b, t: (b, t, 0)),
                  pl.BlockSpec((8, D), lambda b, t: (0, 0)), pl.BlockSpec((1, 8, D), lambda b, t: (b, 0, 0))],
        out_specs=[row, row, pl.BlockSpec((1, 8, D), lambda b, t: (b, 0, 0))],
        out_shape=[jax.ShapeDtypeStruct((B * N, D), BF16), jax.ShapeDtypeStruct((B * N, D), F32),
                   jax.ShapeDtypeStruct((B, 8, D), F32)],
        compiler_params=_params(("parallel", "arbitrary")),
    )(h1, dn, target, gains, vecs)


def _shift(x, s):
    s = s % x.shape[0]
    return x if s == 0 else pltpu.roll(x, s, 0)


def _seg_taps(T, nc, width, pad_left):
    t = lax.broadcasted_iota(jnp.int32, (T, 1), 0)
    pos = jnp.where(t < nc, t, t - nc)
    seg = jnp.where(t < nc, nc, T - nc)
    taps = []
    for k in range(width):
        src = pos + (k - pad_left)
        taps.append((pad_left - k, (src >= 0) & (src < seg)))
    return taps


def _grid_taps(N):
    t = lax.broadcasted_iota(jnp.int32, (N, 1), 0)
    wcol = t % GRID_W
    taps = []
    for dr in (-1, 0, 1):
        for dw in (-1, 0, 1):
            off = dr * GRID_W + dw
            ok = (wcol + dw >= 0) & (wcol + dw < GRID_W) & (t + dr * GRID_W >= 0) & (t + dr * GRID_W < N)
            taps.append((-off, ok))
    return taps


def _conv_fwd(x, w, taps):
    y = jnp.zeros_like(x)
    for k, (s, m) in enumerate(taps):
        y = y + w[k:k + 1] * jnp.where(m, _shift(x, s), 0.0)
    return y


def _conv_bwd(x, w, taps, dy):
    dx = jnp.zeros_like(x)
    dws = []
    for k, (s, m) in enumerate(taps):
        dym = jnp.where(m, dy, 0.0)
        dx = dx + w[k:k + 1] * _shift(dym, -s)
        dws.append(jnp.sum(dym * _shift(x, s), axis=0, keepdims=True))
    return dx, jnp.concatenate(dws, axis=0)


def _ffn_act_fwd(F, w9, bias, *, B, N, DFF, tc):
    nj = DFF // tc

    def body(fg_ref, fv_ref, w_ref, b_ref, o_ref, pre_ref):
        fg = _conv_fwd(fg_ref[...], w_ref[...], _grid_taps(N)) + b_ref[...]
        pre_ref[...] = fg
        o_ref[...] = (_gelu(fg) * fv_ref[...]).astype(BF16)

    col = pl.BlockSpec((N, tc), lambda b, j: (b, j))
    return pl.pallas_call(
        body, name="ffn_act_fwd", grid=(B, nj),
        in_specs=[col, pl.BlockSpec((N, tc), lambda b, j: (b, nj + j)),
                  pl.BlockSpec((9, tc), lambda b, j: (0, j)), pl.BlockSpec((1, tc), lambda b, j: (0, j))],
        out_specs=[col, col],
        out_shape=[jax.ShapeDtypeStruct((B * N, DFF), BF16), jax.ShapeDtypeStruct((B * N, DFF), F32)],
        compiler_params=_params(("parallel", "parallel")),
    )(F, F, w9, bias)


def _ffn_act_bwd(F, pre, w9, df, *, B, N, DFF, tc):
    nj = DFF // tc

    def body(fg_ref, fv_ref, w_ref, pre_ref, df_ref, dfg_ref, dfv_ref, dwb_ref):
        taps = _grid_taps(N)
        x = fg_ref[...]
        fg, vjp = jax.vjp(lambda a: _gelu(a), pre_ref[...])
        dfl = df_ref[...]
        dfv_ref[...] = (dfl * fg).astype(BF16)
        (dpre,) = vjp(dfl * fv_ref[...])
        dx, dw = _conv_bwd(x, w_ref[...], taps, dpre)
        dfg_ref[...] = dx.astype(BF16)

        @pl.when(pl.program_id(1) == 0)
        def _():
            dwb_ref[...] = jnp.zeros_like(dwb_ref)

        dwb_ref[0:9, :] += dw
        dwb_ref[9:10, :] += jnp.sum(dpre, axis=0, keepdims=True)

    col = pl.BlockSpec((N, tc), lambda j, b: (b, j))
    return pl.pallas_call(
        body, name="ffn_act_bwd", grid=(nj, B),
        in_specs=[col, pl.BlockSpec((N, tc), lambda j, b: (b, nj + j)), pl.BlockSpec((9, tc), lambda j, b: (0, j)), col, col],
        out_specs=[col, col, pl.BlockSpec((16, tc), lambda j, b: (0, j))],
        out_shape=[jax.ShapeDtypeStruct((B * N, DFF), BF16), jax.ShapeDtypeStruct((B * N, DFF), BF16),
                   jax.ShapeDtypeStruct((16, DFF), F32)],
        compiler_params=_params(("parallel", "arbitrary")),
    )(F, F, w9, pre, df)


def _dnprep_math(y, is_qk, scale):
    s = _silu(y)
    n = s * lax.rsqrt(jnp.sum(s * s, axis=-1, keepdims=True) + EPS) * scale
    return jnp.where(is_qk, n, s)


def _dnprep_fwd(p, cw, *, B, T, nc, H, HD):
    def body(x_ref, w_ref, o_ref):
        j = pl.program_id(1)
        y = _conv_fwd(x_ref[...], w_ref[...], _seg_taps(T, nc, 4, 2))
        o_ref[...] = _dnprep_math(y, j < 2 * H, jnp.where(j < H, HD ** -0.5, 1.0))

    return pl.pallas_call(
        body, name="dnprep_fwd", grid=(B, 3 * H),
        in_specs=[pl.BlockSpec((T, HD), lambda b, j: (b, j)), pl.BlockSpec((4, HD), lambda b, j: (0, j))],
        out_specs=pl.BlockSpec((T, HD), lambda b, j: (b, j)),
        out_shape=jax.ShapeDtypeStruct((B * T, 3 * H * HD), F32),
        compiler_params=_params(("parallel", "parallel")),
    )(p, cw)


def _dnprep_bwd(p, cw, dqkv, dp, *, B, T, nc, H, HD):
    def body(x_ref, w_ref, d_ref, dp_any, dp_ref, dcw_ref):
        j = pl.program_id(0)
        taps = _seg_taps(T, nc, 4, 2)
        x = x_ref[...]
        y = _conv_fwd(x, w_ref[...], taps)
        is_qk, scale = j < 2 * H, jnp.where(j < H, HD ** -0.5, 1.0)
        _, vjp = jax.vjp(lambda a: _dnprep_math(a, is_qk, scale), y)
        (dy,) = vjp(d_ref[0])
        dx, dw = _conv_bwd(x, w_ref[...], taps, dy)
        dp_ref[...] = dx.astype(BF16)

        @pl.when(pl.program_id(1) == 0)
        def _():
            dcw_ref[...] = jnp.zeros_like(dcw_ref)

        dcw_ref[0:4, :] += dw

    col = pl.BlockSpec((T, HD), lambda j, b: (b, j))
    return pl.pallas_call(
        body, name="dnprep_bwd", grid=(3 * H, B),
        in_specs=[col, pl.BlockSpec((4, HD), lambda j, b: (0, j)),
                  pl.BlockSpec((1, T, HD), lambda j, b: (j // H, b, j % H)), pl.BlockSpec(memory_space=pl.ANY)],
        out_specs=[col, pl.BlockSpec((8, HD), lambda j, b: (0, j))],
        out_shape=[jax.ShapeDtypeStruct(dp.shape, dp.dtype), jax.ShapeDtypeStruct((8, 3 * H * HD), F32)],
        input_output_aliases={3: 0},
        compiler_params=_params(("parallel", "arbitrary")),
    )(p, cw, dqkv, dp)


def _gb_math(ab, alog, dtb, H):
    lane = lax.broadcasted_iota(jnp.int32, ab.shape, 1)
    g = -jnp.exp(alog) * _softplus(ab + dtb)
    return jnp.where(lane < 2 * H, g, jnp.where(lane < 4 * H, _sigmoid(ab), 0.0))


def _gb_fwd(p, prm, *, rows, col0, H, tm):
    def body(x_ref, prm_ref, o_ref):
        o_ref[...] = _gb_math(x_ref[...], prm_ref[0:1], prm_ref[1:2], H)

    return pl.pallas_call(
        body, name="gb_fwd", grid=(rows // tm,),
        in_specs=[pl.BlockSpec((tm, LANES), lambda t: (t, col0 // LANES)), pl.BlockSpec((8, LANES), lambda t: (0, 0))],
        out_specs=pl.BlockSpec((tm, LANES), lambda t: (t, 0)),
        out_shape=jax.ShapeDtypeStruct((rows, LANES), F32),
        compiler_params=_params(("parallel",)),
    )(p, prm)


def _gb_bwd(p, prm, dgb, dp, *, rows, col0, H, tm):
    def body(x_ref, prm_ref, d_ref, dp_any, dp_ref, dprm_ref):
        _, vjp = jax.vjp(lambda a, b, c: _gb_math(a, b, c, H), x_ref[...], prm_ref[0:1], prm_ref[1:2])
        dab, dalog, ddtb = vjp(d_ref[...])
        dp_ref[...] = dab.astype(BF16)

        @pl.when(pl.program_id(0) == 0)
        def _():
            dprm_ref[...] = jnp.zeros_like(dprm_ref)

        dprm_ref[0:1, :] += dalog
        dprm_ref[1:2, :] += ddtb

    blk = pl.BlockSpec((tm, LANES), lambda t: (t, col0 // LANES))
    return pl.pallas_call(
        body, name="gb_bwd", grid=(rows // tm,),
        in_specs=[blk, pl.BlockSpec((8, LANES), lambda t: (0, 0)), pl.BlockSpec((tm, LANES), lambda t: (t, 0)),
                  pl.BlockSpec(memory_space=pl.ANY)],
        out_specs=[blk, pl.BlockSpec((8, LANES), lambda t: (0, 0))],
        out_shape=[jax.ShapeDtypeStruct(dp.shape, dp.dtype), jax.ShapeDtypeStruct((8, LANES), F32)],
        input_output_aliases={3: 0},
        compiler_params=_params(("arbitrary",)),
    )(p, prm, dgb, dp)


def _lru_scans(scans):
    C = scans[0][0].shape[1]
    row = lax.broadcasted_iota(jnp.int32, (SUBLANES, C), 0)
    carries = tuple(jnp.zeros((1, C), F32) for _ in scans)
    for si in range(len(scans[0][4])):
        nb = scans[0][4][si][1] // SUBLANES
        assert all(sc[4][si][1] // SUBLANES == nb for sc in scans)

        def blk(i, carries, si=si, nb=nb):
            out = []
            for (a_ref, b_ref, h_ref, hp_ref, segs), carry in zip(scans, carries):
                start, _, reverse = segs[si]
                r0 = pl.multiple_of(start + (nb - 1 - i if reverse else i) * SUBLANES, SUBLANES)
                A = a_ref[pl.ds(r0, SUBLANES), :]
                Bv = b_ref[pl.ds(r0, SUBLANES), :]
                for s in (1, 2, 4):
                    sh = SUBLANES - s if reverse else s
                    m = (row < SUBLANES - s) if reverse else (row >= s)
                    Bv = jnp.where(m, A * pltpu.roll(Bv, sh, 0) + Bv, Bv)
                    A = jnp.where(m, A * pltpu.roll(A, sh, 0), A)
                Hv = Bv + A * carry
                h_ref[pl.ds(r0, SUBLANES), :] = Hv
                if hp_ref is not None:
                    if reverse:
                        hp = jnp.where(row < SUBLANES - 1, pltpu.roll(Hv, SUBLANES - 1, 0), carry)
                    else:
                        hp = jnp.where(row >= 1, pltpu.roll(Hv, 1, 0), carry)
                    hp_ref[pl.ds(r0, SUBLANES), :] = hp
                out.append(Hv[0:1] if reverse else Hv[SUBLANES - 1:SUBLANES])
            return tuple(out)

        carries = lax.fori_loop(0, nb, blk, carries)


def _lru_orders(T, nc, d):
    N = T - nc
    if d == 0:
        return [(0, nc, False), (nc, N, False)], [(nc, N, True), (0, nc, True)]
    return [(0, nc, True), (nc, N, True)], [(nc, N, False), (0, nc, False)]


def _bdot(a, b, dims=(((1,), (0,)), ((), ()))):
    return lax.dot_general(a.astype(BF16), b.astype(BF16), dims, preferred_element_type=F32)


_NT = (((1,), (1,)), ((), ()))
_TN = (((0,), (0,)), ((), ()))


def _blockdiag(w, C):
    nd, nb, bd, _ = w.shape
    per = C // bd
    out = jnp.einsum('dnpij,pq->dnpiqj', w.reshape(nd, nb // per, per, bd, bd), jnp.eye(per, dtype=w.dtype))
    return out.reshape(nd, nb // per, C, C)


def _blockdiag_extract(dw, bd):
    nd, nj, C, _ = dw.shape
    per = C // bd
    out = jnp.einsum('dnpiqj,pq->dnpij', dw.reshape(nd, nj, per, bd, per, bd), jnp.eye(per, dtype=dw.dtype))
    return out.reshape(nd, nj * per, bd, bd)


def _lru_fwd(p, cw, lv, wr, wi, *, B, T, nc, LW, col0, C):
    N = T - nc
    nj = LW // C

    def body(x_ref, cw_ref, lv_ref, wr_ref, wi_ref, o_ref, a_s, b_s, h_s):
        lv_ = lv_ref[...]
        xc = _conv_fwd(x_ref[:, 0:C], cw_ref[...], _seg_taps(T, nc, 4, 2)) + lv_[0:1]
        for d in (0, 1):
            r = _sigmoid(_bdot(xc, wr_ref[d, 0]) + lv_[1 + d:2 + d])
            i = _sigmoid(_bdot(xc, wi_ref[d, 0]) + lv_[3 + d:4 + d])
            la = -LRU_C * r * _softplus(-lv_[5 + d:6 + d])
            a_s[d] = jnp.exp(la)
            b_s[d] = jnp.sqrt(1.0 - jnp.exp(2.0 * la)) * i * xc
        _lru_scans([(a_s.at[d], b_s.at[d], h_s.at[d], None, _lru_orders(T, nc, d)[0]) for d in (0, 1)])
        o_ref[...] = ((h_s[0, nc:, :] + h_s[1, nc:, :]) * _gelu(x_ref[nc:, C:2 * C])).astype(BF16)

    return pl.pallas_call(
        body, name="lru_fwd", grid=(B, nj),
        in_specs=[pl.BlockSpec((T, 2 * C), lambda b, j: (b, col0 // (2 * C) + j)),
                  pl.BlockSpec((4, C), lambda b, j: (0, j)), pl.BlockSpec((8, C), lambda b, j: (0, j)),
                  pl.BlockSpec((2, 1, C, C), lambda b, j: (0, j, 0, 0)), pl.BlockSpec((2, 1, C, C), lambda b, j: (0, j, 0, 0))],
        out_specs=pl.BlockSpec((N, C), lambda b, j: (b, j)),
        out_shape=jax.ShapeDtypeStruct((B * N, LW), BF16),
        scratch_shapes=[pltpu.VMEM((2, T, C), F32)] * 3,
        compiler_params=_params(("parallel", "parallel")),
    )(p, cw, lv, wr, wi)


def _lru_bwd(p, cw, lv, wr, wi, dy, dp, *, B, T, nc, LW, col0, C):
    N = T - nc
    nj = LW // C

    def body(x_ref, cw_ref, lv_ref, wr_ref, wi_ref, dy_ref, dp_any, dp_ref, dcw_ref, dlv_ref, dwr_ref, dwi_ref,
             a_s, b_s, h_s, hp_s, mu_s, mup_s, dh_s, dxc_s):
        taps = _seg_taps(T, nc, 4, 2)
        lv_ = lv_ref[...]
        xl = x_ref[:, 0:C]
        xc = _conv_fwd(xl, cw_ref[...], taps) + lv_[0:1]
        gel, gelu_vjp = jax.vjp(_gelu, x_ref[nc:, C:2 * C])
        dh_s[0:nc, :] = jnp.zeros((nc, C), F32)
        dh_s[nc:, :] = dy_ref[...] * gel
        dxc_s[...] = jnp.zeros_like(dxc_s)

        @pl.when(pl.program_id(1) == 0)
        def _():
            dcw_ref[...] = jnp.zeros_like(dcw_ref)
            dlv_ref[...] = jnp.zeros_like(dlv_ref)
            dwr_ref[...] = jnp.zeros_like(dwr_ref)
            dwi_ref[...] = jnp.zeros_like(dwi_ref)

        def gates(d):
            lam = lv_[5 + d:6 + d]
            r = _sigmoid(_bdot(xc, wr_ref[d, 0]) + lv_[1 + d:2 + d])
            i = _sigmoid(_bdot(xc, wi_ref[d, 0]) + lv_[3 + d:4 + d])
            sp = _softplus(-lam)
            la = -LRU_C * r * sp
            e2 = jnp.exp(2.0 * la)
            return lam, r, i, sp, la, e2, jnp.sqrt(1.0 - e2)

        for d in (0, 1):
            _, _, i, _, la, _, mult = gates(d)
            a_s[d] = jnp.exp(la)
            b_s[d] = mult * i * xc
        _lru_scans([(a_s.at[d], b_s.at[d], h_s.at[d], hp_s.at[d], _lru_orders(T, nc, d)[0]) for d in (0, 1)])
        for d in (0, 1):
            b_s[d] = a_s[d] * dh_s[...]
        _lru_scans([(a_s.at[d], b_s.at[d], mu_s.at[d], mup_s.at[d], _lru_orders(T, nc, d)[1]) for d in (0, 1)])

        for d in (0, 1):
            lam, r, i, sp, la, e2, mult = gates(d)
            a = a_s[d]
            dinp = dh_s[...] + mup_s[d]
            da = dinp * hp_s[d]
            dmult = dinp * i * xc
            di = dinp * mult * xc
            dla = da * a - dmult * e2 / mult
            dpre_r = (dla * (-LRU_C * sp)) * r * (1.0 - r)
            dpre_i = di * i * (1.0 - i)
            dsp = jnp.sum(dla * (-LRU_C * r), axis=0, keepdims=True)
            dxc_s[...] += dinp * mult * i + _bdot(dpre_r, wr_ref[d, 0], _NT) + _bdot(dpre_i, wi_ref[d, 0], _NT)
            dwr_ref[d, 0] += _bdot(xc, dpre_r, _TN)
            dwi_ref[d, 0] += _bdot(xc, dpre_i, _TN)
            dlv_ref[1 + d:2 + d, :] += jnp.sum(dpre_r, axis=0, keepdims=True)
            dlv_ref[3 + d:4 + d, :] += jnp.sum(dpre_i, axis=0, keepdims=True)
            dlv_ref[5 + d:6 + d, :] += -dsp * _sigmoid(-lam)

        dxc = dxc_s[...]
        dxl, dw = _conv_bwd(xl, cw_ref[...], taps, dxc)
        dcw_ref[0:4, :] += dw
        dlv_ref[0:1, :] += jnp.sum(dxc, axis=0, keepdims=True)
        dp_ref[:, 0:C] = dxl.astype(BF16)
        (dyl,) = gelu_vjp(dy_ref[...] * (h_s[0, nc:, :] + h_s[1, nc:, :]))
        dp_ref[0:nc, C:2 * C] = jnp.zeros((nc, C), BF16)
        dp_ref[nc:, C:2 * C] = dyl.astype(BF16)

    xblk = pl.BlockSpec((T, 2 * C), lambda j, b: (b, col0 // (2 * C) + j))
    wblk = pl.BlockSpec((2, 1, C, C), lambda j, b: (0, j, 0, 0))
    vblk = pl.BlockSpec((8, C), lambda j, b: (0, j))
    return pl.pallas_call(
        body, name="lru_bwd", grid=(nj, B),
        in_specs=[xblk, pl.BlockSpec((4, C), lambda j, b: (0, j)), vblk, wblk, wblk,
                  pl.BlockSpec((N, C), lambda j, b: (b, j)), pl.BlockSpec(memory_space=pl.ANY)],
        out_specs=[xblk, vblk, vblk, wblk, wblk],
        out_shape=[jax.ShapeDtypeStruct(dp.shape, dp.dtype), jax.ShapeDtypeStruct((8, LW), F32),
                   jax.ShapeDtypeStruct((8, LW), F32), jax.ShapeDtypeStruct((2, nj, C, C), F32),
                   jax.ShapeDtypeStruct((2, nj, C, C), F32)],
        scratch_shapes=[pltpu.VMEM((2, T, C), F32)] * 6 + [pltpu.VMEM((T, C), F32)] * 2,
        input_output_aliases={6: 0},
        compiler_params=_params(("parallel", "arbitrary")),
    )(p, cw, lv, wr, wi, dy, dp)


def _chunk_masks(upper):
    i = lax.broadcasted_iota(jnp.int32, (CHUNK, CHUNK), 0)
    j = lax.broadcasted_iota(jnp.int32, (CHUNK, CHUNK), 1)
    ahead = jnp.where(upper, j - i, i - j)
    return i == j, ahead >= 0, ahead > 0


def _col2row(c, eye):
    return jnp.sum(jnp.where(eye, c, 0.0), axis=0, keepdims=True)


def _row2col(r, eye):
    return jnp.sum(jnp.where(eye, r, 0.0), axis=1, keepdims=True)


def _rowsum(x):
    return jnp.sum(x, axis=1, keepdims=True)


_INV_BASE = 8


def _unit_tri_inverses(Ls):
    G = len(Ls)
    W = G * CHUNK
    blk = (lax.broadcasted_iota(jnp.int32, (W, W), 0) // CHUNK) == (lax.broadcasted_iota(jnp.int32, (W, W), 1) // CHUNK)
    ri = lax.broadcasted_iota(jnp.int32, (CHUNK, W), 0)
    ci = lax.broadcasted_iota(jnp.int32, (CHUNK, W), 1) % CHUNK

    def bd(b):
        return jnp.where(blk, jnp.tile(b, (G, 1)), jnp.zeros((), BF16))

    def pdot(a, b):
        return jnp.dot(a.astype(BF16), bd(b.astype(BF16)), preferred_element_type=F32)

    Lc = Ls[0] if G == 1 else jnp.concatenate(Ls, axis=1)
    s = _INV_BASE
    Xp = -jnp.where(ri // s == ci // s, Lc, 0.0)
    Rm = Xp
    for _ in range(int(math.log2(s)) - 1):
        Xp = pdot(Xp, Xp)
        Rm = Rm + Xp + pdot(Rm, Xp)
    while s < CHUNK:
        E = jnp.where((ri // (2 * s) == ci // (2 * s)) & (ri // s != ci // s), Lc, 0.0)
        DE = E + pdot(Rm, E)
        Rm = Rm - (DE + pdot(DE, Rm))
        s *= 2
    eye = _chunk_masks(False)[0]
    return [jnp.where(eye, 1.0, 0.0) + Rm[:, g * CHUNK:(g + 1) * CHUNK] for g in range(G)]


def _delta_chunk_common(q, k, v, gcol, bcol, upper):
    eye, incl, strict = _chunk_masks(upper)
    gc = _rowsum(jnp.where(incl, _col2row(gcol, eye), 0.0))
    D = jnp.where(incl, jnp.exp(jnp.minimum(gc - _col2row(gc, eye), 0.0)), 0.0)
    kb = k * bcol
    AP = _bdot(jnp.concatenate([kb, q], axis=0), k, _NT)
    A = AP[:CHUNK]
    L = jnp.where(strict, A * D, 0.0)
    eg = jnp.exp(gc)
    gl = jnp.sum(gcol, axis=0, keepdims=True)
    attn = jnp.where(incl, AP[CHUNK:] * D, 0.0)
    return dict(eye=eye, incl=incl, strict=strict, gc=gc, D=D, kb=kb, A=A, L=L, eg=eg, gl=gl, egl=jnp.exp(gl),
                attn=attn, kbe=kb * eg, vb=v * bcol, qe=q * eg, kd=k * jnp.exp(gl - gc))


def _delta_group_pre(chunks, upper):
    cs = [_delta_chunk_common(*ch, upper) for ch in chunks]
    out = []
    for c, Tm in zip(cs, _unit_tri_inverses([c["L"] for c in cs])):
        dk = c["kbe"].shape[1]
        wu = _bdot(Tm, jnp.concatenate([c["kbe"], c["vb"]], axis=1))
        KN = _bdot(c["kd"], wu, _TN)
        QO = _bdot(c["attn"], wu)
        out.append((Tm, KN[:, :dk], KN[:, dk:], c["qe"] - QO[:, :dk], QO[:, dk:], c["egl"]))
    return out


def _delta_chunk_bwd(q, k, v, gcol, bcol, S, Tm, do, dS2, upper):
    c = _delta_chunk_common(q, k, v, gcol, bcol, upper)
    eye, incl, strict, D, eg, egl = c["eye"], c["incl"], c["strict"], c["D"], c["eg"], c["egl"]
    kb, kbe, vb, qe, kd, attn = c["kb"], c["kbe"], c["vb"], c["qe"], c["kd"], c["attn"]
    dkk = kbe.shape[1]
    wu = _bdot(Tm, jnp.concatenate([kbe, vb], axis=1))
    w = wu[:, :dkk]
    vn = wu[:, dkk:] - _bdot(w, S)
    dvn = _bdot(kd, dS2) + _bdot(attn, do, _TN)
    dkd = _bdot(vn, dS2, _NT)
    dgl = jnp.sum(_rowsum(dS2 * S), axis=0, keepdims=True) * egl
    dqa = _bdot(do, jnp.concatenate([S, vn], axis=0), _NT)
    dqe = dqa[:, :dkk]
    dattn = jnp.where(incl, dqa[:, dkk:], 0.0)
    dw = -_bdot(dvn, S, _NT)
    r = _rowsum(dkd * kd)
    dk = dkd * jnp.exp(c["gl"] - c["gc"])
    dgl = dgl + jnp.sum(r, axis=0, keepdims=True)
    dgc = _rowsum(dqe * qe) - r
    E = dattn * attn
    dvw = jnp.concatenate([dvn, dw], axis=1)
    dTm = _bdot(dvw, jnp.concatenate([vb, kbe], axis=1), _NT)
    dvk = _bdot(Tm, dvw, _TN)
    dvb = dvk[:, :dvn.shape[1]]
    dv = dvb * bcol
    dbeta = _rowsum(dvb * v)
    dkbe = dvk[:, dvn.shape[1]:]
    dkb = dkbe * eg
    dgc = dgc + _rowsum(dkbe * kbe)
    dL = jnp.where(strict, -_bdot(Tm, _bdot(dTm, Tm, _NT), _TN), 0.0)
    dA = dL * D
    E = E + dL * c["L"]
    PA = jnp.concatenate([dattn * D, dA], axis=0)
    PAk = _bdot(PA, k)
    dq = dqe * eg + PAk[:CHUNK]
    dkb = dkb + PAk[CHUNK:]
    dk = dk + _bdot(PA, jnp.concatenate([q, kb], axis=0), _TN) + dkb * bcol
    dbeta = dbeta + _rowsum(dkb * k)
    dgc = dgc + _rowsum(E) - _row2col(jnp.sum(E, axis=0, keepdims=True), eye)
    dg = _row2col(jnp.sum(jnp.where(incl, dgc, 0.0), axis=0, keepdims=True), eye) + dgl
    return dq, dk, dv, dg, dbeta


def _delta_unroll(trips):
    return max(u for u in (3, 2, 1) if trips % u == 0)


def _delta_group(n):
    return max(g for g in range(1, 2 * LANES // CHUNK + 1) if n % g == 0)


def _delta_chunk_at(T, nc, d, i):
    n, ncc = T // CHUNK, nc // CHUNK
    desc = jnp.where(i < ncc, ncc - 1 - i, n - 1 - (i - ncc))
    if isinstance(d, int):
        return i if d == 0 else desc
    return jnp.where(d == 0, i, desc)


def _dn_out_math(o, onorm, z):
    return _rmsn(o, onorm) * _silu(z)


def _delta_fwd(qkv, gb, p, onorm, *, B, T, nc, H, HD):
    N = T - nc
    n = T // CHUNK
    G = _delta_group(n)

    def body(q_ref, k_ref, v_ref, gb_ref, z_ref, on_ref, y_ref, o_ref, Tm_ref, K_ref, S_ref, Qp_ref, eg_ref,
             N_s, O0_s, o_s):
        h = pl.program_id(1)
        lane = lax.broadcasted_iota(jnp.int32, (CHUNK, LANES), 1)

        def pre(g, carry):
            cs = [g * G + i for i in range(G)]
            rows = [pl.ds(pl.multiple_of(c * CHUNK, CHUNK), CHUNK) for c in cs]
            for d in (0, 1):
                chunks = []
                for r in rows:
                    gbb = gb_ref[r, :]
                    chunks.append((q_ref[r, :], k_ref[r, :], v_ref[r, :],
                                   _rowsum(jnp.where(lane == d * H + h, gbb, 0.0)),
                                   _rowsum(jnp.where(lane == 2 * H + d * H + h, gbb, 0.0))))
                for c, r, (Tm, K, Nn, Qp, O0, egl) in zip(cs, rows, _delta_group_pre(chunks, d == 1)):
                    Tm_ref[0, d * n + c] = Tm
                    K_ref[0, d * n + c] = K.astype(BF16)
                    N_s[d * n + c] = Nn
                    Qp_ref[0, d, r, :] = Qp.astype(BF16)
                    O0_s[d, r, :] = O0
                    eg_ref[0, d * n + c] = jnp.broadcast_to(egl, (SUBLANES, HD))
            return carry

        lax.fori_loop(0, n // G, pre, 0)

        def step(i, Ss):
            out = []
            for d in (0, 1):
                c = _delta_chunk_at(T, nc, d, i)
                rows = pl.ds(pl.multiple_of(c * CHUNK, CHUNK), CHUNK)
                S_ref[0, d * n + c] = Ss[d]
                Sb = Ss[d].astype(BF16)
                o_s[d, rows, :] = jnp.dot(Qp_ref[0, d, rows, :], Sb, preferred_element_type=F32) + O0_s[d, rows, :]
                out.append(eg_ref[0, d * n + c][0:1] * Ss[d] + N_s[d * n + c]
                           - jnp.dot(K_ref[0, d * n + c], Sb, preferred_element_type=F32))
            return tuple(out)

        lax.fori_loop(0, n, step, (jnp.zeros((HD, HD), F32), jnp.zeros((HD, HD), F32)))
        o = o_s[0, nc:, :] + o_s[1, nc:, :]
        o_ref[...] = o
        y_ref[...] = _dn_out_math(o, on_ref[...], z_ref[nc:, :]).astype(BF16)

    col = lambda off: pl.BlockSpec((T, HD), lambda b, h: (b, off + h))
    lat = pl.BlockSpec((N, HD), lambda b, h: (b, h))
    per = lambda *blk: pl.BlockSpec((1, *blk), lambda b, h: (b * H + h, 0, 0, 0))
    return pl.pallas_call(
        body, name="delta_fwd", grid=(B, H),
        in_specs=[col(0), col(H), col(2 * H), pl.BlockSpec((T, LANES), lambda b, h: (b, 0)), col(3 * H),
                  pl.BlockSpec((1, HD), lambda b, h: (0, 0))],
        out_specs=[lat, lat, per(2 * n, CHUNK, CHUNK), per(2 * n, HD, HD), per(2 * n, HD, HD), per(2, T, HD),
                   per(2 * n, SUBLANES, HD)],
        out_shape=[jax.ShapeDtypeStruct((B * N, H * HD), BF16), jax.ShapeDtypeStruct((B * N, H * HD), F32),
                   jax.ShapeDtypeStruct((B * H, 2 * n, CHUNK, CHUNK), F32),
                   jax.ShapeDtypeStruct((B * H, 2 * n, HD, HD), BF16), jax.ShapeDtypeStruct((B * H, 2 * n, HD, HD), F32),
                   jax.ShapeDtypeStruct((B * H, 2, T, HD), BF16), jax.ShapeDtypeStruct((B * H, 2 * n, SUBLANES, HD), F32)],
        scratch_shapes=[pltpu.VMEM((2 * n, HD, HD), F32), pltpu.VMEM((2, T, HD), F32), pltpu.VMEM((2, T, HD), F32)],
        compiler_params=_params(("parallel", "parallel")),
    )(qkv, qkv, qkv, gb, p, onorm)


def _delta_bwd(qkv, gb, p, onorm, o, res, dy, dp, *, B, T, nc, H, HD):
    N = T - nc
    n = T // CHUNK

    def body(q_ref, k_ref, v_ref, gb_ref, z_ref, on_ref, o_ref, dy_ref, Tm_ref, K_ref, S_ref, Qp_ref, eg_ref, dp_any,
             dqkv_ref, dgb_ref, dp_ref, don_ref, do_s, R_s, dS_s):
        h, d = pl.program_id(1), pl.program_id(2)
        lane = lax.broadcasted_iota(jnp.int32, (CHUNK, LANES), 1)

        @pl.when(d == 0)
        def _():
            _, vjp = jax.vjp(_dn_out_math, o_ref[...], on_ref[...], z_ref[nc:, :])
            do, don, dz = vjp(dy_ref[...])
            do_s[0:nc, :] = jnp.zeros((nc, HD), F32)
            do_s[nc:, :] = do
            dp_ref[0:nc, :] = jnp.zeros((nc, HD), BF16)
            dp_ref[nc:, :] = dz.astype(BF16)
            dqkv_ref[...] = jnp.zeros_like(dqkv_ref)

            @pl.when(h == 0)
            def _():
                don_ref[...] = jnp.zeros_like(don_ref)
                dgb_ref[...] = jnp.zeros_like(dgb_ref)

            don_ref[0, 0:1, :] += don

        def r_of(c, carry):
            rows = pl.ds(pl.multiple_of(c * CHUNK, CHUNK), CHUNK)
            R_s[c] = lax.dot_general(Qp_ref[0, 0, rows, :], do_s[rows, :].astype(BF16), _TN, preferred_element_type=F32)
            return carry

        lax.fori_loop(0, n, r_of, 0)

        def bwd_step(i, dS):
            c = _delta_chunk_at(T, nc, d, n - 1 - i)
            dS_s[c] = dS
            return (eg_ref[0, c][0:1] * dS + R_s[c]
                    - lax.dot_general(K_ref[0, c], dS.astype(BF16), _TN, preferred_element_type=F32))

        lax.fori_loop(0, n, bwd_step, jnp.zeros((HD, HD), F32))

        def grads(c, carry):
            rows = pl.ds(pl.multiple_of(c * CHUNK, CHUNK), CHUNK)
            gbb = gb_ref[rows, :]
            gcol = _rowsum(jnp.where(lane == d * H + h, gbb, 0.0))
            bcol = _rowsum(jnp.where(lane == 2 * H + d * H + h, gbb, 0.0))
            dq, dk, dv, dg, dbeta = _delta_chunk_bwd(q_ref[rows, :], k_ref[rows, :], v_ref[rows, :], gcol, bcol,
                                                     S_ref[0, c], Tm_ref[0, c], do_s[rows, :], dS_s[c], d == 1)
            dqkv_ref[0, rows, :] += dq
            dqkv_ref[1, rows, :] += dk
            dqkv_ref[2, rows, :] += dv
            dgb_ref[rows, :] += (jnp.where(lane == d * H + h, dg, 0.0)
                                 + jnp.where(lane == 2 * H + d * H + h, dbeta, 0.0))
            return carry

        lax.fori_loop(0, n, grads, 0, unroll=_delta_unroll(n))

    col = lambda off: pl.BlockSpec((T, HD), lambda b, h, d: (b, off + h))
    lat = pl.BlockSpec((N, HD), lambda b, h, d: (b, h))
    per = lambda *blk: pl.BlockSpec((1, *blk), lambda b, h, d: (b * H + h, d, 0, 0))
    return pl.pallas_call(
        body, name="delta_bwd", grid=(B, H, 2),
        in_specs=[col(0), col(H), col(2 * H), pl.BlockSpec((T, LANES), lambda b, h, d: (b, 0)), col(3 * H),
                  pl.BlockSpec((1, HD), lambda b, h, d: (0, 0)), lat, lat,
                  per(n, CHUNK, CHUNK), per(n, HD, HD), per(n, HD, HD), per(1, T, HD), per(n, SUBLANES, HD),
                  pl.BlockSpec(memory_space=pl.ANY)],
        out_specs=[pl.BlockSpec((3, T, HD), lambda b, h, d: (0, b, h)), pl.BlockSpec((T, LANES), lambda b, h, d: (b, 0)),
                   col(3 * H), pl.BlockSpec((1, 8, HD), lambda b, h, d: (b, 0, 0))],
        out_shape=[jax.ShapeDtypeStruct((3, B * T, H * HD), F32), jax.ShapeDtypeStruct((B * T, LANES), F32),
                   jax.ShapeDtypeStruct(dp.shape, dp.dtype), jax.ShapeDtypeStruct((B, 8, HD), F32)],
        scratch_shapes=[pltpu.VMEM((T, HD), F32), pltpu.VMEM((n, HD, HD), F32), pltpu.VMEM((n, HD, HD), F32)],
        input_output_aliases={13: 2},
        compiler_params=_params(("parallel", "arbitrary", "arbitrary")),
    )(qkv, qkv, qkv, gb, p, onorm, o, dy, *res, dp)


def _rowwise(fn, ins, out_dtypes, *, name, tm=256, mult=16):
    R, W = ins[0].shape
    tm = _tile(R, tm, mult)

    def body(*refs):
        outs = fn(*[r[...] for r in refs[:len(ins)]])
        for o_ref, o in zip(refs[len(ins):], outs):
            o_ref[...] = o.astype(o_ref.dtype)

    spec = pl.BlockSpec((tm, W), lambda i: (i, 0))
    return pl.pallas_call(
        body, name=name, grid=(R // tm,), in_specs=[spec] * len(ins), out_specs=[spec] * len(out_dtypes),
        out_shape=[jax.ShapeDtypeStruct((R, W), dt) for dt in out_dtypes],
        compiler_params=_params(("parallel",)),
    )(*ins)


def _sum_lead(x, *, name, tm=256, mult=16):
    S, R, W = x.shape
    tm = _tile(R, tm, mult)

    def body(*refs):
        acc = refs[0][0].astype(F32)
        for r in refs[1:S]:
            acc = acc + r[0].astype(F32)
        refs[S][...] = acc

    return pl.pallas_call(
        body, name=name, grid=(R // tm,),
        in_specs=[pl.BlockSpec((1, tm, W), functools.partial(lambda s, i: (s, i, 0), s)) for s in range(S)],
        out_specs=pl.BlockSpec((tm, W), lambda i: (i, 0)),
        out_shape=jax.ShapeDtypeStruct((R, W), F32),
        compiler_params=_params(("parallel",)),
    )(*([x] * S))


def _adamw_math(w, g, m, v):
    m = ADAM_B1 * m + (1.0 - ADAM_B1) * g
    v = ADAM_B2 * v + (1.0 - ADAM_B2) * (g * g)
    m_hat = m / (1.0 - ADAM_B1 ** ADAM_STEP)
    v_hat = v / (1.0 - ADAM_B2 ** ADAM_STEP)
    return -ADAM_LR * (m_hat / (jnp.sqrt(v_hat) + ADAM_EPS) + ADAM_WD * w), m, v


def _adamw(w, g, m, v, *, name):
    tm = max(SUBLANES, (256 * 1024) // w.shape[1] // SUBLANES * SUBLANES)
    return _rowwise(_adamw_math, [w, g, m, v], [F32, F32, F32], name=name, tm=tm, mult=SUBLANES)


def _me():
    return lax.axis_index("x"), lax.axis_index("y"), lax.axis_index("c")


def _allgather_small(v):
    R, W = v.shape

    def body(x_ref, out_ref, send_sems, recv_sems, local_sem):
        x, y, c = _me()
        me, sibling = (x, y, c), (x, y, 1 - c)
        chips = [(1 - x, y), (x, 1 - y), (1 - x, 1 - y)]

        def slot(px, py, pc):
            return out_ref.at[4 * px + 2 * py + pc]

        def copy(k, block, to, src=None):
            return pltpu.make_async_remote_copy(
                src_ref=slot(*block) if src is None else src, dst_ref=slot(*block),
                send_sem=send_sems.at[k], recv_sem=recv_sems.at[k], device_id=to, device_id_type=MESH)

        mine = pltpu.make_async_copy(x_ref, slot(*me), local_sem)
        mine.start()
        first = [copy(0, me, sibling, src=x_ref)]
        first += [copy(1 + j, me, (*chip, c), src=x_ref) for j, chip in enumerate(chips)]
        for cp in first:
            cp.start()
        passed = [copy(4 + j, (*chip, c), sibling) for j, chip in enumerate(chips)]
        for j, chip in enumerate(chips):
            copy(1 + j, (*chip, c), me).wait_recv()
            passed[j].start()
        copy(0, sibling, me).wait_recv()
        for j, chip in enumerate(chips):
            copy(4 + j, (*chip, 1 - c), me).wait_recv()
        for cp in first + passed:
            cp.wait_send()
        mine.wait()

    return pl.pallas_call(
        body, name="allgather_small", out_shape=jax.ShapeDtypeStruct((8, R, W), v.dtype),
        in_specs=[pl.BlockSpec(memory_space=pltpu.VMEM)], out_specs=pl.BlockSpec(memory_space=pltpu.VMEM),
        scratch_shapes=[pltpu.SemaphoreType.DMA((7,)), pltpu.SemaphoreType.DMA((7,)), pltpu.SemaphoreType.DMA],
        compiler_params=_params(),
    )(v)


_ANY = pl.BlockSpec(memory_space=pl.ANY)


def _allgather_halves(shards, *, name):
    nw = len(shards)

    def body(*refs):
        x_refs, out_refs = refs[:nw], refs[nw:2 * nw]
        send_sems, recv_sems, local_sems = refs[2 * nw:]
        x, y, c = _me()
        me, sibling = (x, y, c), (x, y, 1 - c)
        chips = [(1 - x, y), (x, 1 - y), (1 - x, 1 - y)]

        def slot(w, px, py, pc):
            return out_refs[w].at[4 * px + 2 * py + pc]

        def copy(w, k, block, to, src=None):
            return pltpu.make_async_remote_copy(
                src_ref=slot(w, *block) if src is None else src, dst_ref=slot(w, *block),
                send_sem=send_sems.at[w, k], recv_sem=recv_sems.at[w, k], device_id=to, device_id_type=MESH)

        started, local = [], []
        for w in range(nw):
            half = shards[w].shape[0] // 2
            own = x_refs[w].at[pl.ds(c * half, half), :]
            mine = pltpu.make_async_copy(own, slot(w, *me), local_sems.at[w])
            mine.start()
            first = [copy(w, 0, me, sibling, src=own)]
            first += [copy(w, 1 + j, me, (*chip, c), src=own) for j, chip in enumerate(chips)]
            for cp in first:
                cp.start()
            started += first
            local.append(mine)
        for w in range(nw):
            for j, chip in enumerate(chips):
                copy(w, 1 + j, (*chip, c), me).wait_recv()
                fwd = copy(w, 4 + j, (*chip, c), sibling)
                fwd.start()
                started.append(fwd)
        for w in range(nw):
            copy(w, 0, sibling, me).wait_recv()
            for j, chip in enumerate(chips):
                copy(w, 4 + j, (*chip, 1 - c), me).wait_recv()
        for cp in started:
            cp.wait_send()
        for cp in local:
            cp.wait()

    return pl.pallas_call(
        body, name=name,
        out_shape=[jax.ShapeDtypeStruct((8, s.shape[0] // 2, s.shape[1]), s.dtype) for s in shards],
        in_specs=[_ANY] * nw, out_specs=[_ANY] * nw,
        scratch_shapes=[pltpu.SemaphoreType.DMA((nw, 7)), pltpu.SemaphoreType.DMA((nw, 7)), pltpu.SemaphoreType.DMA((nw,))],
        compiler_params=_params(),
    )(*shards)


def _sibling_send_halves(arrs, *, name):
    nw = len(arrs)

    def body(*refs):
        x_refs, out_refs, send_sems, recv_sems = refs[:nw], refs[nw:2 * nw], refs[2 * nw], refs[2 * nw + 1]
        x, y, c = _me()
        cps = []
        for w in range(nw):
            half = arrs[w].shape[1] // 2
            cp = pltpu.make_async_remote_copy(
                src_ref=x_refs[w].at[:, pl.ds((1 - c) * half, half), :], dst_ref=out_refs[w],
                send_sem=send_sems.at[w], recv_sem=recv_sems.at[w], device_id=(x, y, 1 - c), device_id_type=MESH)
            cp.start()
            cps.append(cp)
        for cp in cps:
            cp.wait()

    return pl.pallas_call(
        body, name=name,
        out_shape=[jax.ShapeDtypeStruct((a.shape[0], a.shape[1] // 2, a.shape[2]), a.dtype) for a in arrs],
        in_specs=[_ANY] * nw, out_specs=[_ANY] * nw,
        scratch_shapes=[pltpu.SemaphoreType.DMA((nw,)), pltpu.SemaphoreType.DMA((nw,))],
        compiler_params=_params(),
    )(*arrs)


def _sibling_swap(arrs, *, name):
    nw = len(arrs)

    def body(*refs):
        x_refs, out_refs, send_sems, recv_sems = refs[:nw], refs[nw:2 * nw], refs[2 * nw], refs[2 * nw + 1]
        x, y, c = _me()
        cps = []
        for w in range(nw):
            cp = pltpu.make_async_remote_copy(
                src_ref=x_refs[w], dst_ref=out_refs[w], send_sem=send_sems.at[w], recv_sem=recv_sems.at[w],
                device_id=(x, y, 1 - c), device_id_type=MESH)
            cp.start()
            cps.append(cp)
        for cp in cps:
            cp.wait()

    return pl.pallas_call(
        body, name=name, out_shape=[jax.ShapeDtypeStruct(a.shape, a.dtype) for a in arrs],
        in_specs=[_ANY] * nw, out_specs=[_ANY] * nw,
        scratch_shapes=[pltpu.SemaphoreType.DMA((nw,)), pltpu.SemaphoreType.DMA((nw,))],
        compiler_params=_params(),
    )(*arrs)


def _adamw_halves(w, own, sib, m, v, c_arr, *, name):
    r, cols = w.shape
    h = r // 2
    tm = _tile(h, max(SUBLANES, (192 * 1024) // cols // SUBLANES * SUBLANES), SUBLANES)
    nb = h // tm

    def body(c_ref, w_ref, own_ref, sib_ref, m_ref, v_ref, g_out, d_out, m_out, v_out):
        g = jnp.where(pl.program_id(0) == c_ref[0], own_ref[...], sib_ref[...])
        g_out[...] = g
        d_out[...], m_out[...], v_out[...] = _adamw_math(w_ref[...], g, m_ref[...], v_ref[...])

    full = pl.BlockSpec((tm, cols), lambda hh, i, c_ref: (hh * nb + i, 0))
    half = pl.BlockSpec((tm, cols), lambda hh, i, c_ref: (i, 0))
    return pl.pallas_call(
        body, name=name,
        grid_spec=pltpu.PrefetchScalarGridSpec(num_scalar_prefetch=1, grid=(2, nb),
                                               in_specs=[full, half, half, full, full], out_specs=[full] * 4),
        out_shape=[jax.ShapeDtypeStruct((r, cols), F32)] * 4,
        compiler_params=_params(("parallel", "parallel")),
    )(c_arr, w, own, sib, m, v)


_HBM = pl.BlockSpec(memory_space=pltpu.HBM)
_SEM = pl.BlockSpec(memory_space=pltpu.SEMAPHORE)
_DATAFLOW = pltpu.SideEffectType.DATAFLOW_SIDE_EFFECTING


def _chip_exchange_start(arrs, *, name):
    nw = len(arrs)

    def body(*refs):
        x_refs, land_refs, send_sems, recv_sems = refs[:nw], refs[nw:2 * nw], refs[2 * nw], refs[2 * nw + 1]
        token = refs[-1]
        x, y, c = _me()
        s_me = 2 * x + y
        for w in range(nw):
            for k, (px, py) in enumerate([(1 - x, y), (x, 1 - y), (1 - x, 1 - y)]):
                pltpu.make_async_remote_copy(
                    src_ref=x_refs[w].at[2 * px + py], dst_ref=land_refs[w].at[s_me], send_sem=send_sems.at[3 * w + k],
                    recv_sem=recv_sems.at[3 * w + k], device_id=(px, py, c), device_id_type=MESH).start()
        token[...] = jnp.zeros_like(token)

    hbm = [pltpu.HBM(a.shape, a.dtype) for a in arrs]
    outs = pl.pallas_call(
        body, name=name,
        out_shape=(pltpu.SemaphoreType.DMA((3 * nw,)), pltpu.SemaphoreType.DMA((3 * nw,)), *hbm, *hbm,
                   jax.ShapeDtypeStruct((SUBLANES, LANES), F32)),
        in_specs=[_HBM] * (2 * nw), out_specs=(_SEM, _SEM, *([_HBM] * (2 * nw)), pl.BlockSpec(memory_space=pltpu.VMEM)),
        input_output_aliases={i: 2 + i for i in range(2 * nw)},
        compiler_params=pltpu.CompilerParams(has_side_effects=_DATAFLOW),
    )(*[pltpu.with_memory_space_constraint(a, pltpu.HBM) for a in arrs],
      *[pltpu.with_memory_space_constraint(lax.empty(a.shape, a.dtype), pltpu.HBM) for a in arrs])
    return outs[0], outs[1], list(outs[2:2 + nw]), list(outs[2 + nw:2 + 2 * nw]), outs[-1]


def _allgather_start(shards, *, name):
    nw = len(shards)

    def body(*refs):
        x_refs, land_refs, send_sems, recv_sems = refs[:nw], refs[nw:2 * nw], refs[2 * nw], refs[2 * nw + 1]
        token = refs[-1]
        x, y, c = _me()
        me = 4 * x + 2 * y + c
        for w in range(nw):
            half = shards[w].shape[0] // 2
            own = x_refs[w].at[pl.ds(c * half, half), :]
            for k, to in enumerate([(x, y, 1 - c), (1 - x, y, c), (x, 1 - y, c), (1 - x, 1 - y, c)]):
                pltpu.make_async_remote_copy(
                    src_ref=own, dst_ref=land_refs[w].at[me], send_sem=send_sems.at[4 * w + k],
                    recv_sem=recv_sems.at[4 * w + k], device_id=to, device_id_type=MESH).start()
        token[...] = jnp.zeros_like(token)

    lands = [pltpu.HBM((8, s.shape[0] // 2, s.shape[1]), s.dtype) for s in shards]
    outs = pl.pallas_call(
        body, name=name,
        out_shape=(pltpu.SemaphoreType.DMA((4 * nw,)), pltpu.SemaphoreType.DMA((4 * nw,)),
                   *[pltpu.HBM(s.shape, s.dtype) for s in shards], *lands, jax.ShapeDtypeStruct((SUBLANES, LANES), F32)),
        in_specs=[_HBM] * (2 * nw), out_specs=(_SEM, _SEM, *([_HBM] * (2 * nw)), pl.BlockSpec(memory_space=pltpu.VMEM)),
        input_output_aliases={i: 2 + i for i in range(2 * nw)},
        compiler_params=pltpu.CompilerParams(has_side_effects=_DATAFLOW),
    )(*[pltpu.with_memory_space_constraint(s, pltpu.HBM) for s in shards],
      *[pltpu.with_memory_space_constraint(lax.empty(l.shape, l.dtype), pltpu.HBM) for l in lands])
    return outs[0], outs[1], list(outs[2:2 + nw]), list(outs[2 + nw:2 + 2 * nw]), outs[-1]


def _allgather_wait(send_sems, recv_sems, srcs, lands, after, *, name):
    nw = len(srcs)

    def body(*refs):
        x_refs, land_refs, send_sems, recv_sems = refs[:nw], refs[nw:2 * nw], refs[2 * nw], refs[2 * nw + 1]
        x, y, c = _me()
        for w in range(nw):
            half = srcs[w].shape[0] // 2
            own = x_refs[w].at[pl.ds(c * half, half), :]
            for k, (px, py, pc) in enumerate([(x, y, 1 - c), (1 - x, y, c), (x, 1 - y, c), (1 - x, 1 - y, c)]):
                cp = pltpu.make_async_remote_copy(
                    src_ref=own, dst_ref=land_refs[w].at[4 * px + 2 * py + pc], send_sem=send_sems.at[4 * w + k],
                    recv_sem=recv_sems.at[4 * w + k], device_id=(px, py, pc), device_id_type=MESH)
                cp.wait_send()
                cp.wait_recv()

    outs = pl.pallas_call(
        body, name=name,
        out_shape=(*[pltpu.HBM(a.shape, a.dtype) for a in srcs], *[pltpu.HBM(a.shape, a.dtype) for a in lands]),
        in_specs=[_HBM] * (2 * nw) + [_SEM, _SEM, _ANY], out_specs=tuple([_HBM] * (2 * nw)),
        input_output_aliases={i: i for i in range(2 * nw)},
        compiler_params=pltpu.CompilerParams(has_side_effects=_DATAFLOW),
    )(*srcs, *lands, send_sems, recv_sems, after)
    return list(outs[:nw]), list(outs[nw:])


def _pass_to_sibling(lands, *, name):
    nw = len(lands)

    def body(*refs):
        x_refs, out_refs, send_sems, recv_sems = refs[:nw], refs[nw:2 * nw], refs[2 * nw], refs[2 * nw + 1]
        x, y, c = _me()
        chips = [(1 - x, y), (x, 1 - y), (1 - x, 1 - y)]
        cps = []
        for w in range(nw):
            for k, (px, py) in enumerate(chips):
                cp = pltpu.make_async_remote_copy(
                    src_ref=x_refs[w].at[4 * px + 2 * py + c], dst_ref=out_refs[w].at[4 * px + 2 * py + c],
                    send_sem=send_sems.at[3 * w + k], recv_sem=recv_sems.at[3 * w + k], device_id=(x, y, 1 - c),
                    device_id_type=MESH)
                cp.start()
                cps.append(cp)
        for w in range(nw):
            for k, (px, py) in enumerate(chips):
                pltpu.make_async_remote_copy(
                    src_ref=x_refs[w].at[4 * px + 2 * py + c], dst_ref=out_refs[w].at[4 * px + 2 * py + 1 - c],
                    send_sem=send_sems.at[3 * w + k], recv_sem=recv_sems.at[3 * w + k], device_id=(x, y, 1 - c),
                    device_id_type=MESH).wait_recv()
        for cp in cps:
            cp.wait_send()

    return pl.pallas_call(
        body, name=name, out_shape=[jax.ShapeDtypeStruct(a.shape, a.dtype) for a in lands],
        in_specs=[_ANY] * nw, out_specs=[_ANY] * nw, input_output_aliases={i: i for i in range(nw)},
        scratch_shapes=[pltpu.SemaphoreType.DMA((3 * nw,)), pltpu.SemaphoreType.DMA((3 * nw,))],
        compiler_params=_params(),
    )(*lands)


def _chip_exchange_wait(send_sems, recv_sems, srcs, lands, after, *, name):
    nw = len(srcs)

    def body(*refs):
        x_refs, land_refs, send_sems, recv_sems = refs[:nw], refs[nw:2 * nw], refs[2 * nw], refs[2 * nw + 1]
        x, y, c = _me()
        for w in range(nw):
            for k, (px, py) in enumerate([(1 - x, y), (x, 1 - y), (1 - x, 1 - y)]):
                cp = pltpu.make_async_remote_copy(
                    src_ref=x_refs[w].at[2 * px + py], dst_ref=land_refs[w].at[2 * px + py], send_sem=send_sems.at[3 * w + k],
                    recv_sem=recv_sems.at[3 * w + k], device_id=(px, py, c), device_id_type=MESH)
                cp.wait_send()
                cp.wait_recv()

    hbm = [pltpu.HBM(a.shape, a.dtype) for a in srcs]
    outs = pl.pallas_call(
        body, name=name, out_shape=(*hbm, *hbm),
        in_specs=[_HBM] * (2 * nw) + [_SEM, _SEM, _ANY], out_specs=tuple([_HBM] * (2 * nw)),
        input_output_aliases={i: i for i in range(2 * nw)},
        compiler_params=pltpu.CompilerParams(has_side_effects=_DATAFLOW),
    )(*srcs, *lands, send_sems, recv_sems, after)
    return list(outs[:nw]), list(outs[nw:])


def _sum_slabs(landed, own_src, s_arr, after, *, name, tm=512):
    S, h, w = landed.shape
    tm = _tile(h, tm, 16)

    def body(s_ref, *refs):
        own = refs[S][0].astype(F32)
        acc = None
        for s in range(S):
            term = jnp.where(s_ref[0] == s, own, refs[s][0].astype(F32))
            acc = term if acc is None else acc + term
        refs[S + 2][...] = acc

    def slab(s):
        return pl.BlockSpec((1, tm, w), lambda i, s_ref: (jnp.where(s_ref[0] == s, (s + 1) % S, s), i, 0))

    return pl.pallas_call(
        body, name=name,
        grid_spec=pltpu.PrefetchScalarGridSpec(
            num_scalar_prefetch=1, grid=(h // tm,),
            in_specs=[slab(s) for s in range(S)] + [pl.BlockSpec((1, tm, w), lambda i, s_ref: (s_ref[0], i, 0)), _ANY],
            out_specs=pl.BlockSpec((tm, w), lambda i, s_ref: (i, 0))),
        out_shape=jax.ShapeDtypeStruct((h, w), F32),
        compiler_params=_params(("parallel",)),
    )(s_arr, *([landed] * S), own_src, after)


def _half_add(g, recv, c_arr, *, name):
    S, r, w = g.shape
    h = r // 2
    tm = _tile(h, 512, 16)
    nb = h // tm

    def body(c_ref, g_ref, r_ref, o_ref):
        o_ref[...] = (g_ref[...] + r_ref[...]).astype(BF16)

    return pl.pallas_call(
        body, name=name,
        grid_spec=pltpu.PrefetchScalarGridSpec(
            num_scalar_prefetch=1, grid=(S, nb),
            in_specs=[pl.BlockSpec((1, tm, w), lambda s, i, c_ref: (s, c_ref[0] * nb + i, 0)),
                      pl.BlockSpec((1, tm, w), lambda s, i, c_ref: (s, i, 0))],
            out_specs=pl.BlockSpec((1, tm, w), lambda s, i, c_ref: (s, i, 0))),
        out_shape=jax.ShapeDtypeStruct((S, h, w), BF16),
        compiler_params=_params(("parallel", "parallel")),
    )(c_arr, g, recv)


def _layout(sizes, width, part_mult, total_mult):
    offs, rows, r = [], [], 0
    for n in sizes:
        k = -(-n // width)
        offs.append(r)
        rows.append(k)
        r += -(-k // part_mult) * part_mult
    return offs, rows, -(-r // total_mult) * total_mult


def _pack(arrs, width, part_mult, total_mult, dtype, lead=()):
    nl = len(lead)
    sizes = [math.prod(a.shape[nl:]) for a in arrs]
    offs, rows, total = _layout(sizes, width, part_mult, total_mult)
    parts, r = [], 0
    for a, n, o, k in zip(arrs, sizes, offs, rows):
        kp = -(-k // part_mult) * part_mult
        flat = a.reshape(*lead, n).astype(dtype)
        if kp * width > n:
            flat = jnp.pad(flat, [(0, 0)] * nl + [(0, kp * width - n)])
        parts.append(flat.reshape(*lead, kp, width))
        r = o + kp
    if total > r:
        parts.append(jnp.zeros((*lead, total - r, width), dtype))
    return jnp.concatenate(parts, axis=nl)


def _unpack(pool, shapes, width, part_mult, total_mult):
    lead = pool.shape[:-2]
    sizes = [math.prod(s) for s in shapes]
    offs, rows, _ = _layout(sizes, width, part_mult, total_mult)
    out = []
    for s, n, o, k in zip(shapes, sizes, offs, rows):
        flat = lax.slice_in_dim(pool, o, o + k, axis=len(lead)).reshape(*lead, k * width)
        out.append(lax.slice_in_dim(flat, 0, n, axis=len(lead)).reshape(*lead, *s))
    return out


_WEIGHTS = ("c_ctx", "w_ada", "b_ada", "g_pre_mix", "g_post_mix", "g_pre_ffn", "g_post_ffn", "w_in", "b_merge",
            "dn_conv", "dn_a_log", "dn_dt_bias", "dn_onorm", "lru_conv", "lru_conv_b", "lru_w_rg", "lru_b_rg",
            "lru_w_ig", "lru_b_ig", "lru_lambda", "w_branch_dn", "w_branch_lru", "w_out", "w_up", "ffn_dw",
            "ffn_dw_b", "w_down")
_BIG = {"w_ada": True, "w_in": True, "w_branch_dn": False, "w_branch_lru": False, "w_out": False, "w_up": True,
        "w_down": False}
_SMALL_SHARDED = ("dn_conv", "lru_conv", "lru_b_rg", "lru_b_ig", "lru_lambda", "ffn_dw")
_NCHIP = 4
_FLAT_PART = 8
_FLAT_TOTAL = 256


def _to_chip_shards(g, by_cols):
    if by_cols:
        return g.reshape(g.shape[0], _NCHIP, g.shape[1] // _NCHIP).transpose(1, 0, 2)
    return g.reshape(_NCHIP, g.shape[0] // _NCHIP, g.shape[1])


def _from_chip_shards(s, by_cols):
    if by_cols:
        return s.transpose(1, 0, 2).reshape(s.shape[1], _NCHIP * s.shape[2])
    return s.reshape(_NCHIP * s.shape[1], s.shape[2])


def _dsilu(x):
    s = _sigmoid(x)
    return s * (1.0 + x * (1.0 - s))


def kernel(x, c, ctx, c_ctx, w_ada, b_ada, g_pre_mix, g_post_mix, g_pre_ffn, g_post_ffn, w_in, b_merge, dn_conv, dn_a_log, dn_dt_bias, dn_onorm, lru_conv, lru_conv_b, lru_w_rg, lru_b_rg, lru_w_ig, lru_b_ig, lru_lambda, w_branch_dn, w_branch_lru, w_out, w_up, ffn_dw, ffn_dw_b, w_down, loss_target, m_c_ctx, m_w_ada, m_b_ada, m_g_pre_mix, m_g_post_mix, m_g_pre_ffn, m_g_post_ffn, m_w_in, m_b_merge, m_dn_conv, m_dn_a_log, m_dn_dt_bias, m_dn_onorm, m_lru_conv, m_lru_conv_b, m_lru_w_rg, m_lru_b_rg, m_lru_w_ig, m_lru_b_ig, m_lru_lambda, m_w_branch_dn, m_w_branch_lru, m_w_out, m_w_up, m_ffn_dw, m_ffn_dw_b, m_w_down, v_c_ctx, v_w_ada, v_b_ada, v_g_pre_mix, v_g_post_mix, v_g_pre_ffn, v_g_post_ffn, v_w_in, v_b_merge, v_dn_conv, v_dn_a_log, v_dn_dt_bias, v_dn_onorm, v_lru_conv, v_lru_conv_b, v_lru_w_rg, v_lru_b_rg, v_lru_w_ig, v_lru_b_ig, v_lru_lambda, v_w_branch_dn, v_w_branch_lru, v_w_out, v_w_up, v_ffn_dw, v_ffn_dw_b, v_w_down):
    W = dict(zip(_WEIGHTS, (c_ctx, w_ada, b_ada, g_pre_mix, g_post_mix, g_pre_ffn, g_post_ffn, w_in, b_merge, dn_conv,
                            dn_a_log, dn_dt_bias, dn_onorm, lru_conv, lru_conv_b, lru_w_rg, lru_b_rg, lru_w_ig, lru_b_ig,
                            lru_lambda, w_branch_dn, w_branch_lru, w_out, w_up, ffn_dw, ffn_dw_b, w_down)))
    Mo = dict(zip(_WEIGHTS, (m_c_ctx, m_w_ada, m_b_ada, m_g_pre_mix, m_g_post_mix, m_g_pre_ffn, m_g_post_ffn, m_w_in,
                             m_b_merge, m_dn_conv, m_dn_a_log, m_dn_dt_bias, m_dn_onorm, m_lru_conv, m_lru_conv_b,
                             m_lru_w_rg, m_lru_b_rg, m_lru_w_ig, m_lru_b_ig, m_lru_lambda, m_w_branch_dn,
                             m_w_branch_lru, m_w_out, m_w_up, m_ffn_dw, m_ffn_dw_b, m_w_down)))
    Vo = dict(zip(_WEIGHTS, (v_c_ctx, v_w_ada, v_b_ada, v_g_pre_mix, v_g_post_mix, v_g_pre_ffn, v_g_post_ffn, v_w_in,
                             v_b_merge, v_dn_conv, v_dn_a_log, v_dn_dt_bias, v_dn_onorm, v_lru_conv, v_lru_conv_b,
                             v_lru_w_rg, v_lru_b_rg, v_lru_w_ig, v_lru_b_ig, v_lru_lambda, v_w_branch_dn,
                             v_w_branch_lru, v_w_out, v_w_up, v_ffn_dw, v_ffn_dw_b, v_w_down)))
    B, N, D = x.shape
    NC = ctx.shape[1]
    T = NC + N
    H, HD = dn_a_log.shape[-1], dn_onorm.shape[-1]
    DNW = H * HD
    LW, LBD = lru_conv_b.shape[-1], lru_w_rg.shape[-1]
    DFF = ffn_dw_b.shape[-1]
    LC = LANES
    x_i, y_i, c_i = _me()
    s_me = 2 * x_i + y_i
    tm = _tile(math.gcd(NC, N), 256, 16)

    def whole(n, g):
        r, w_ = W[n].shape[1:]
        return g.reshape(_NCHIP, r, w_) if _BIG[n] else g.reshape(_NCHIP * r, w_)

    first = ("w_ada", "w_in")
    later = tuple(n for n in _BIG if n not in first)
    shard16 = {n: W[n][0].astype(BF16) for n in _BIG}
    full = {n: whole(n, g) for n, g in zip(first, _allgather_halves([shard16[n] for n in first], name="allgather_first"))}

    small_local = [W[n][0].reshape(-1, W[n].shape[-1]) for n in _SMALL_SHARDED]
    small_shapes = [a.shape for a in small_local]
    spack = _pack(small_local, LANES, _FLAT_PART, _FLAT_PART, F32)
    sgath = _allgather_small(spack)[0::2]
    sfull = {n: _from_chip_shards(s, True)
             for n, s in zip(_SMALL_SHARDED, _unpack(sgath, small_shapes, LANES, _FLAT_PART, _FLAT_PART))}

    later16, sgath = lax.optimization_barrier(([shard16[n] for n in later], sgath))
    ag_send, ag_recv, ag_src, ag_land, ag_token = _allgather_start(later16, name="ag_start")

    o_a = 4 * DNW
    o_xl = o_a + 4 * H
    o_mg = o_xl + 2 * LW
    wi_ = _from_chip_shards(full["w_in"], True)
    nj = LW // LC
    lru_cols = jnp.stack([wi_[:, o_xl:o_xl + LW].reshape(D, nj, LC), wi_[:, o_xl + LW:o_mg].reshape(D, nj, LC)],
                         axis=2).reshape(D, 2 * LW)
    wp = jnp.concatenate([wi_[:, :o_a], lru_cols, wi_[:, o_mg:], wi_[:, o_a:o_xl],
                          jnp.zeros((D, LANES - 4 * H), BF16)], axis=1)
    p_lru, p_mg, p_ab = 4 * DNW, 4 * DNW + 2 * LW, 4 * DNW + 2 * LW + 2 * D
    PW = p_ab + LANES

    MR = LANES
    cond = jnp.concatenate([c, c_ctx[None], jnp.zeros((MR - B - 1, D), F32)], axis=0)
    silu_rows = _rowwise(lambda a: (_silu(a),), [cond], [F32], name="cond_silu")[0]
    mod = _matmul(silu_rows, full["w_ada"], b_shards=(0, _NCHIP), name="ada_fwd") + b_ada + ag_token[0, 0]
    mx = mod[:B].reshape(B, 6, D)
    mc = mod[B].reshape(6, D)
    zero = jnp.zeros((B, D), F32)
    tab = jnp.stack([jnp.stack([jnp.broadcast_to(mc[0], (B, D)), jnp.broadcast_to(mc[1], (B, D))] + [zero] * 6, axis=1),
                     jnp.stack([mx[:, 0], mx[:, 1]] + [zero] * 6, axis=1)], axis=1)
    vecs = jnp.stack([mx[:, 2], mx[:, 3], mx[:, 4], mx[:, 5]] + [zero] * 4, axis=1)
    gains = jnp.concatenate([g_post_mix, g_pre_ffn, g_post_ffn, jnp.zeros((5, D), F32)], axis=0)

    h = jnp.concatenate([ctx, x], axis=1)
    u = _premix_fwd(h, g_pre_mix, tab, nc=NC, tm=tm)
    p = _matmul(u, wp, name="in_fwd")
    dkw = dict(B=B, T=T, nc=NC, H=H, HD=HD)
    qkv = _dnprep_fwd(p, sfull["dn_conv"], **dkw)
    prm = jnp.concatenate([
        jnp.concatenate([dn_a_log.reshape(1, 2 * H), jnp.zeros((1, LANES - 2 * H), F32)], axis=1),
        jnp.concatenate([dn_dt_bias.reshape(1, 2 * H), jnp.zeros((1, LANES - 2 * H), F32)], axis=1),
        jnp.zeros((6, LANES), F32)], axis=0)
    gtm = _tile(B * T, 512, 16)
    gb = _gb_fwd(p, prm, rows=B * T, col0=p_ab, H=H, tm=gtm)
    y_dn, o_dn, *dn_res = _delta_fwd(qkv, gb, p, dn_onorm, **dkw)
    lv = jnp.concatenate([lru_conv_b, sfull["lru_b_rg"], sfull["lru_b_ig"], sfull["lru_lambda"], jnp.zeros((1, LW), F32)], axis=0)
    wr = _blockdiag(lru_w_rg[0], LC).astype(BF16)
    wi = _blockdiag(lru_w_ig[0], LC).astype(BF16)
    lkw = dict(B=B, T=T, nc=NC, LW=LW, col0=p_lru, C=LC)
    y_lru = _lru_fwd(p, sfull["lru_conv"], lv, wr, wi, **lkw)
    ag_src, ag_land = _allgather_wait(ag_send, ag_recv, ag_src, ag_land, y_lru, name="ag_wait")
    me_piece = 4 * x_i + 2 * y_i + c_i
    for n, src, land in zip(later, ag_src, _pass_to_sibling(ag_land, name="ag_pass")):
        own = lax.dynamic_slice_in_dim(src, c_i * (src.shape[0] // 2), src.shape[0] // 2, axis=0)
        full[n] = whole(n, lax.dynamic_update_index_in_dim(land, own, me_piece, axis=0))
    Ydn = _matmul(y_dn, full["w_branch_dn"], name="bdn_fwd")
    Ylru = _matmul(y_lru, full["w_branch_lru"], name="blru_fwd")
    mkw = dict(B=B, T=T, nc=NC, D=D, col0=p_mg, tm=tm)
    mixin = _merge_fwd(p, Ydn, Ylru, b_merge, **mkw)
    mix = _matmul(mixin, full["w_out"], name="out_fwd")
    h1, u2 = _post_fwd(x, mix, gains, vecs, tm=tm)
    F = _matmul(u2, full["w_up"], b_shards=(0, _NCHIP), name="up_fwd")
    w9 = sfull["ffn_dw"]
    ftc = _tile(DFF, 256)
    f, f_pre = _ffn_act_fwd(F, w9, ffn_dw_b, B=B, N=N, DFF=DFF, tc=ftc)
    dn = _matmul(f, full["w_down"], name="down_fwd")
    ddn, dout, sums_f = _final(h1, dn, loss_target, gains, vecs, tm=tm)

    G = {}
    df = _matmul(ddn, full["w_down"], tb=True, name="down_bwd_x")
    G["w_down"] = _matmul(f, ddn, ta=True, name="down_bwd_w")
    dFg, dFv, dwb = _ffn_act_bwd(F, f_pre, w9, df, B=B, N=N, DFF=DFF, tc=ftc)
    hs = _NCHIP // 2
    du2 = _matmul(dFg, full["w_up"], tb=True, b_shards=(0, hs), name="up_bwd_xg")
    du2 = _matmul(dFv, full["w_up"], tb=True, b_shards=(hs, hs), add=du2, name="up_bwd_xv")
    gup = _matmul(u2, dFg, ta=True, out_shards=hs, into=(lax.empty((_NCHIP, D, DFF // hs), F32), 0), name="up_bwd_wg")
    G["w_up"] = _matmul(u2, dFv, ta=True, out_shards=hs, into=(gup, hs), name="up_bwd_wv")
    dx1, dmix, sums_p = _post_bwd(x, mix, gains, vecs, dout, du2, tm=tm)
    dmixin = _matmul(dmix, full["w_out"], tb=True, name="out_bwd_x")
    G["w_out"] = _matmul(mixin, dmix, ta=True, name="out_bwd_w")
    dp = jnp.zeros((B * T, PW), BF16)
    dYdn, dYlru, dp, sums_m = _merge_bwd(p, Ydn, Ylru, b_merge, dmixin, dp, **mkw)
    dy_dn = _matmul(dYdn, full["w_branch_dn"], tb=True, name="bdn_bwd_x")
    G["w_branch_dn"] = _matmul(y_dn, dYdn, ta=True, name="bdn_bwd_w")
    dy_lru = _matmul(dYlru, full["w_branch_lru"], tb=True, name="blru_bwd_x")
    G["w_branch_lru"] = _matmul(y_lru, dYlru, ta=True, name="blru_bwd_w")

    c_arr = c_i.astype(jnp.int32).reshape(1)
    s_arr = s_me.astype(jnp.int32).reshape(1)

    def chip_sums(names, tag):
        slabs = [G[n] if _BIG[n] else G[n].reshape(_NCHIP, G[n].shape[0] // _NCHIP, G[n].shape[1]) for n in names]
        from_sibling = _sibling_send_halves(slabs, name="rs_sibling_" + tag)
        return [_half_add(g, r, c_arr, name="rs_add_" + n) for n, g, r in zip(names, slabs, from_sibling)]

    early = tuple(n for n in _BIG if n in G)
    cx_send, cx_recv, cx_src, cx_land, cx_token = _chip_exchange_start(chip_sums(early, "early"), name="cx_start")
    dp, dcw_l, dlv, dwr, dwi = _lru_bwd(p, sfull["lru_conv"], lv + cx_token[0, 0], wr, wi, dy_lru, dp, **lkw)
    dqkv, dgb, dp, don = _delta_bwd(qkv, gb, p, dn_onorm, o_dn, dn_res, dy_dn, dp, **dkw)
    dp, dprm = _gb_bwd(p, prm, dgb, dp, rows=B * T, col0=p_ab, H=H, tm=gtm)
    dp, dcw_d = _dnprep_bwd(p, sfull["dn_conv"], dqkv, dp, **dkw)
    dwp = _matmul(u, dp, ta=True, name="in_bwd_w")
    dlru = dwp[:, p_lru:p_mg].reshape(D, nj, 2, LC)
    G["w_in"] = _to_chip_shards(jnp.concatenate([dwp[:, :o_a], dwp[:, p_ab:p_ab + 4 * H], dlru[:, :, 0].reshape(D, LW),
                                                 dlru[:, :, 1].reshape(D, LW), dwp[:, p_mg:p_ab]], axis=1), True)
    wi_send, wi_recv, wi_src, wi_land, wi_token = _chip_exchange_start(chip_sums(("w_in",), "w_in"), name="cx_in_start")
    dU = _matmul(dp, wp, tb=True, after=wi_token, name="in_bwd_x")
    grad_x, sums_pm = _premix_bwd(h, g_pre_mix, tab, dU, dx1, nc=NC, tm=tm)

    dmod_x = jnp.stack([sums_pm[:, 1, 0], sums_pm[:, 1, 1], sums_p[:, 0], sums_p[:, 1], sums_p[:, 2], sums_f[:, 0]],
                       axis=1).reshape(B, 6 * D)
    dmod_c = jnp.concatenate([sums_pm[:, 0, 0].sum(0), sums_pm[:, 0, 1].sum(0), jnp.zeros((4 * D,), F32)])[None]
    dmod = jnp.concatenate([dmod_x, dmod_c, jnp.zeros((MR - B - 1, 6 * D), F32)], axis=0)
    G["w_ada"] = _matmul(silu_rows, dmod, ta=True, out_shards=_NCHIP, name="ada_bwd_w")
    dsilu = _matmul(dmod, full["w_ada"], tb=True, b_shards=(0, _NCHIP), name="ada_bwd_x")

    g_small = {
        "c_ctx": dsilu[B] * _dsilu(c_ctx),
        "b_ada": dmod[:B + 1].sum(0)[None],
        "g_pre_mix": sums_pm[:, :, 2].sum((0, 1))[None],
        "g_post_mix": sums_p[:, 3].sum(0)[None],
        "g_pre_ffn": sums_p[:, 4].sum(0)[None],
        "g_post_ffn": sums_f[:, 1].sum(0)[None],
        "b_merge": sums_m[0:1],
        "dn_conv": dcw_d[0:4][None],
        "dn_a_log": dprm[0, :2 * H].reshape(1, 2, H),
        "dn_dt_bias": dprm[1, :2 * H].reshape(1, 2, H),
        "dn_onorm": don[:, 0].sum(0)[None],
        "lru_conv": dcw_l[0:4][None],
        "lru_conv_b": dlv[0:1],
        "lru_w_rg": _blockdiag_extract(dwr, LBD)[None],
        "lru_b_rg": dlv[1:3][None],
        "lru_w_ig": _blockdiag_extract(dwi, LBD)[None],
        "lru_b_ig": dlv[3:5][None],
        "lru_lambda": dlv[5:7][None],
        "ffn_dw": dwb[0:9].reshape(1, 3, 3, DFF),
        "ffn_dw_b": dwb[9:10],
    }
    small_names = tuple(n for n in _WEIGHTS if n not in _BIG)
    loss_part = sums_f[:, 2].sum().reshape(1)
    gs_list = [g_small[n] for n in small_names] + [loss_part]
    gs_shapes = [a.shape for a in gs_list]
    gpack = _pack(gs_list, LANES, _FLAT_PART, _FLAT_TOTAL, F32)
    gsum = _sum_lead(_allgather_small(gpack), name="small_sum", tm=512, mult=SUBLANES)
    gs_red = dict(zip(small_names + ("loss",), _unpack(gsum, gs_shapes, LANES, _FLAT_PART, _FLAT_TOTAL)))
    loss = gs_red["loss"][0]

    grads, deltas, new_m, new_v = {}, {}, {}, {}

    def finish(names, lands, srcs, after, tag):
        halves = [_sum_slabs(l, src, s_arr, after, name="rs_sum_" + n) for n, l, src in zip(names, lands, srcs)]
        outs = None
        for n, own, sib in zip(names, halves, _sibling_swap(halves, name="rs_gather_" + tag)):
            shp = W[n].shape
            outs = _adamw_halves(W[n][0], own, sib, Mo[n][0], Vo[n][0], c_arr, name="adamw_" + n)
            grads[n], deltas[n], new_m[n], new_v[n] = (o.reshape(shp) for o in outs)
        return outs[1]

    cx_src, cx_land = _chip_exchange_wait(cx_send, cx_recv, cx_src, cx_land, dsilu, name="cx_wait")
    ada_sums, gsum = lax.optimization_barrier((chip_sums(("w_ada",), "w_ada"), gsum))
    ad_send, ad_recv, ad_src, ad_land, ad_token = _chip_exchange_start(ada_sums, name="cx_ada_start")
    last_early = finish(early, cx_land, cx_src, ad_token, "early")
    wi_src, wi_land = _chip_exchange_wait(wi_send, wi_recv, wi_src, wi_land, last_early, name="cx_in_wait")
    last_in = finish(("w_in",), wi_land, wi_src, last_early, "w_in")
    ad_src, ad_land = _chip_exchange_wait(ad_send, ad_recv, ad_src, ad_land, last_in, name="cx_ada_wait")
    finish(("w_ada",), ad_land, ad_src, last_in, "w_ada")
    for n in small_names:
        g = gs_red[n]
        if n in _SMALL_SHARDED:
            k = W[n].shape[-1]
            g = lax.dynamic_slice_in_dim(g, s_me * k, k, axis=g.ndim - 1)
        grads[n] = g.reshape(W[n].shape)
    sm_shapes = [W[n].shape for n in small_names]
    pk = lambda d: _pack([d[n] for n in small_names], LANES, _FLAT_PART, _FLAT_TOTAL, F32)
    d_, m_, v_ = _adamw(pk(W), pk(grads), pk(Mo), pk(Vo), name="adamw_small")
    for dst, pool_ in ((deltas, d_), (new_m, m_), (new_v, v_)):
        dst.update(zip(small_names, _unpack(pool_, sm_shapes, LANES, _FLAT_PART, _FLAT_TOTAL)))
    return (loss, grad_x, *[grads[n] for n in _WEIGHTS], *[deltas[n] for n in _WEIGHTS],
            *[new_m[n] for n in _WEIGHTS], *[new_v[n] for n in _WEIGHTS])
```

```python
import functools
import math

import jax
import jax.numpy as jnp
from jax import lax
from jax.experimental import pallas as pl
from jax.experimental.pallas import tpu as pltpu

F32 = jnp.float32
BF16 = jnp.bfloat16
EPS = 1e-6
GRID_W = 64
CHUNK = 256
LRU_C = 8.0
LANES = 128
SUBLANES = 8
VMEM_LIMIT = 56 * 1024 * 1024
ADAM_LR, ADAM_B1, ADAM_B2, ADAM_EPS, ADAM_WD, ADAM_STEP = 0.001, 0.9, 0.999, 1e-08, 0.01, 10
MESH = pl.DeviceIdType.MESH


def _tile(n, target, mult=LANES):
    best = None
    for t in range(mult, min(n, target) + 1, mult):
        if n % t == 0:
            best = t
    return best if best is not None else n


def _params(sem=None, **kw):
    return pltpu.CompilerParams(dimension_semantics=sem, vmem_limit_bytes=VMEM_LIMIT, **kw)


def _sigmoid(x):
    return 1.0 / (1.0 + jnp.exp(-x))


def _silu(x):
    return x * _sigmoid(x)


def _softplus(x):
    return jnp.maximum(x, 0.0) + jnp.log(1.0 + jnp.exp(-jnp.abs(x)))


def _gelu(x):
    return 0.5 * x * (1.0 + jnp.tanh(math.sqrt(2.0 / math.pi) * (x + 0.044715 * x * x * x)))


def _rmsn(u, gain):
    return u * lax.rsqrt(jnp.mean(u * u, axis=-1, keepdims=True) + EPS) * gain


_MM_VMEM = 40 * 1024 * 1024


_ANY_SPEC = pl.BlockSpec(memory_space=pl.ANY)


def _matmul(a, b, *, ta=False, tb=False, add=None, b_shards=None, out_shards=None, into=None, after=None,
            out_dtype=F32, name, tm=1024, tn=2048, tk=2048):
    (K, M) = a.shape if ta else a.shape[::-1]
    if b_shards is not None:
        s0, ns = b_shards
        bsh = (b.shape[1], ns * b.shape[2])
        nsh = b.shape[2]
    else:
        bsh = b.shape
    N = bsh[0] if tb else bsh[1]
    assert (bsh[1] if tb else bsh[0]) == K, (a.shape, b.shape, ta, tb)
    tm = _tile(M, tm)
    tk = _tile(nsh if (b_shards is not None and tb) else K, tk)
    nlim = nsh if (b_shards is not None and not tb) else (N // out_shards if out_shards else N)
    osz = jnp.dtype(out_dtype).itemsize + (4 if add is not None else 0)
    while True:
        tn_ = _tile(nlim, tn)
        need = 2 * (tm * tk * a.dtype.itemsize + tk * tn_ * b.dtype.itemsize + tm * tn_ * osz) + 4 * tm * tn_
        if need <= _MM_VMEM or tn <= LANES:
            break
        tn //= 2
    tn = tn_
    nk = K // tk
    dims = (((0 if ta else 1,), (1 if tb else 0,)), ((), ()))

    def body(a_ref, b_ref, *rest):
        c_ref = rest[0] if add is not None else None
        o_ref, acc_ref = rest[-2:]
        k = pl.program_id(2)

        @pl.when(k == 0)
        def _():
            acc_ref[...] = jnp.zeros_like(acc_ref) if c_ref is None else c_ref[...]

        bv = b_ref[0] if b_shards is not None else b_ref[...]
        acc_ref[...] += lax.dot_general(a_ref[...].astype(BF16), bv.astype(BF16), dims, preferred_element_type=F32)

        @pl.when(k == nk - 1)
        def _():
            if out_shards:
                o_ref[0] = acc_ref[...].astype(out_dtype)
            else:
                o_ref[...] = acc_ref[...].astype(out_dtype)

    a_spec = pl.BlockSpec((tk, tm), lambda i, j, k: (k, i)) if ta else pl.BlockSpec((tm, tk), lambda i, j, k: (i, k))
    if b_shards is None:
        b_spec = pl.BlockSpec((tn, tk), lambda i, j, k: (j, k)) if tb else pl.BlockSpec((tk, tn), lambda i, j, k: (k, j))
    elif tb:
        per = nsh // tk
        b_spec = pl.BlockSpec((1, tn, tk), lambda i, j, k: (s0 + k // per, j, k % per))
    else:
        per = nsh // tn
        b_spec = pl.BlockSpec((1, tk, tn), lambda i, j, k: (s0 + j // per, k, j % per))
    o_spec = pl.BlockSpec((tm, tn), lambda i, j, k: (i, j))
    extra, alias = (), {}
    if out_shards:
        oper = N // out_shards // tn
        o0 = 0
        out_shape = jax.ShapeDtypeStruct((out_shards, M, N // out_shards), out_dtype)
        if into is not None:
            buf, o0 = into
            out_shape = jax.ShapeDtypeStruct(buf.shape, buf.dtype)
            extra, alias = (buf,), {2 + (add is not None): 0}
        out_spec = pl.BlockSpec((1, tm, tn), lambda i, j, k: (o0 + j // oper, i, j % oper))
    else:
        out_spec, out_shape = o_spec, jax.ShapeDtypeStruct((M, N), out_dtype)
    if after is not None:
        extra = extra + (after,)
    return pl.pallas_call(
        body, name=name, grid=(M // tm, N // tn, nk),
        in_specs=[a_spec, b_spec] + ([o_spec] if add is not None else []) + [_ANY_SPEC] * len(extra),
        out_specs=out_spec, out_shape=out_shape, input_output_aliases=alias,
        scratch_shapes=[pltpu.VMEM((tm, tn), F32)],
        compiler_params=_params(("parallel", "parallel", "arbitrary")),
    )(*((a, b) + ((add,) if add is not None else ()) + extra))


def _premix_math(h, gain, shift, scale):
    return _rmsn(h, gain) * (1.0 + scale) + shift


def _premix_fwd(h, gain, tab, *, nc, tm):
    B, T, D = h.shape
    nt, nct = T // tm, nc // tm

    def body(h_ref, g_ref, tab_ref, u_ref):
        tabv = tab_ref[0, 0]
        u_ref[...] = _premix_math(h_ref[0], g_ref[...], tabv[0:1], tabv[1:2]).astype(BF16)

    return pl.pallas_call(
        body, name="premix_fwd", grid=(B, nt),
        in_specs=[pl.BlockSpec((1, tm, D), lambda b, t: (b, t, 0)),
                  pl.BlockSpec((1, D), lambda b, t: (0, 0)),
                  pl.BlockSpec((1, 1, 8, D), lambda b, t: (b, jnp.where(t < nct, 0, 1), 0, 0))],
        out_specs=pl.BlockSpec((tm, D), lambda b, t: (b * nt + t, 0)),
        out_shape=jax.ShapeDtypeStruct((B * T, D), BF16),
        compiler_params=_params(("parallel", "parallel")),
    )(h, gain, tab)


def _premix_bwd(h, gain, tab, du, dres, *, nc, tm):
    B, T, D = h.shape
    nt, nct = T // tm, nc // tm
    N = T - nc

    def body(h_ref, g_ref, tab_ref, du_ref, dres_ref, dx_ref, sums_ref):
        t = pl.program_id(1)
        tabv = tab_ref[0, 0]
        _, vjp = jax.vjp(_premix_math, h_ref[0], g_ref[...], tabv[0:1], tabv[1:2])
        dh, dgain, dshift, dscale = vjp(du_ref[...].astype(F32))

        @pl.when((t == 0) | (t == nct))
        def _():
            sums_ref[...] = jnp.zeros_like(sums_ref)

        sums_ref[0, 0, 0:1, :] += dshift
        sums_ref[0, 0, 1:2, :] += dscale
        sums_ref[0, 0, 2:3, :] += dgain

        @pl.when(t >= nct)
        def _():
            dx_ref[0] = dres_ref[...] + dh

    lat = lambda b, t: jnp.maximum(t - nct, 0)
    return pl.pallas_call(
        body, name="premix_bwd", grid=(B, nt),
        in_specs=[pl.BlockSpec((1, tm, D), lambda b, t: (b, t, 0)),
                  pl.BlockSpec((1, D), lambda b, t: (0, 0)),
                  pl.BlockSpec((1, 1, 8, D), lambda b, t: (b, jnp.where(t < nct, 0, 1), 0, 0)),
                  pl.BlockSpec((tm, D), lambda b, t: (b * nt + t, 0)),
                  pl.BlockSpec((tm, D), lambda b, t: (b * (nt - nct) + lat(b, t), 0))],
        out_specs=[pl.BlockSpec((1, tm, D), lambda b, t: (b, lat(b, t), 0)),
                   pl.BlockSpec((1, 1, 8, D), lambda b, t: (b, jnp.where(t < nct, 0, 1), 0, 0))],
        out_shape=[jax.ShapeDtypeStruct((B, N, D), F32), jax.ShapeDtypeStruct((B, 2, 8, D), F32)],
        compiler_params=_params(("parallel", "arbitrary")),
    )(h, gain, tab, du, dres)


def _merge_math(mgd, mgl, yd, yl, bd, bl):
    return _sigmoid(mgd + bd) * yd + _sigmoid(mgl + bl) * yl


def _merge_fwd(p, ydn, ylru, b_merge, *, B, T, nc, D, col0, tm):
    N = T - nc
    ntl, nt, nct, cb = N // tm, T // tm, nc // tm, col0 // D

    def body(mgd_ref, mgl_ref, yd_ref, yl_ref, bm_ref, o_ref):
        o_ref[...] = _merge_math(mgd_ref[...], mgl_ref[...], yd_ref[...], yl_ref[...],
                                 bm_ref[:, 0:D], bm_ref[:, D:2 * D]).astype(BF16)

    prow = lambda b, t: b * nt + nct + t
    return pl.pallas_call(
        body, name="merge_fwd", grid=(B, ntl),
        in_specs=[pl.BlockSpec((tm, D), lambda b, t: (prow(b, t), cb)),
                  pl.BlockSpec((tm, D), lambda b, t: (prow(b, t), cb + 1)),
                  pl.BlockSpec((tm, D), lambda b, t: (b * ntl + t, 0)),
                  pl.BlockSpec((tm, D), lambda b, t: (b * ntl + t, 0)),
                  pl.BlockSpec((1, 2 * D), lambda b, t: (0, 0))],
        out_specs=pl.BlockSpec((tm, D), lambda b, t: (b * ntl + t, 0)),
        out_shape=jax.ShapeDtypeStruct((B * N, D), BF16),
        compiler_params=_params(("parallel", "parallel")),
    )(p, p, ydn, ylru, b_merge)


def _merge_bwd(p, ydn, ylru, b_merge, dmix, dp, *, B, T, nc, D, col0, tm):
    N = T - nc
    ntl, nt, nct, cb = N // tm, T // tm, nc // tm, col0 // D
    assert col0 % (2 * D) == 0

    def body(mgd_ref, mgl_ref, yd_ref, yl_ref, bm_ref, dm_ref, dp_any, dyd_ref, dyl_ref, dp_ref, sums_ref):
        _, vjp = jax.vjp(_merge_math, mgd_ref[...], mgl_ref[...], yd_ref[...], yl_ref[...],
                         bm_ref[:, 0:D], bm_ref[:, D:2 * D])
        dmgd, dmgl, dyd, dyl, dbd, dbl = vjp(dm_ref[...])
        dyd_ref[...] = dyd.astype(BF16)
        dyl_ref[...] = dyl.astype(BF16)
        dp_ref[:, 0:D] = dmgd.astype(BF16)
        dp_ref[:, D:2 * D] = dmgl.astype(BF16)

        @pl.when((pl.program_id(0) == 0) & (pl.program_id(1) == 0))
        def _():
            sums_ref[...] = jnp.zeros_like(sums_ref)

        sums_ref[0:1, 0:D] += dbd
        sums_ref[0:1, D:2 * D] += dbl

    prow = lambda b, t: b * nt + nct + t
    row = pl.BlockSpec((tm, D), lambda b, t: (b * ntl + t, 0))
    return pl.pallas_call(
        body, name="merge_bwd", grid=(B, ntl),
        in_specs=[pl.BlockSpec((tm, D), lambda b, t: (prow(b, t), cb)),
                  pl.BlockSpec((tm, D), lambda b, t: (prow(b, t), cb + 1)),
                  row, row, pl.BlockSpec((1, 2 * D), lambda b, t: (0, 0)), row,
                  pl.BlockSpec(memory_space=pl.ANY)],
        out_specs=[row, row,
                   pl.BlockSpec((tm, 2 * D), lambda b, t: (prow(b, t), cb // 2)),
                   pl.BlockSpec((8, 2 * D), lambda b, t: (0, 0))],
        out_shape=[jax.ShapeDtypeStruct((B * N, D), BF16), jax.ShapeDtypeStruct((B * N, D), BF16),
                   jax.ShapeDtypeStruct(dp.shape, dp.dtype), jax.ShapeDtypeStruct((8, 2 * D), F32)],
        input_output_aliases={6: 2},
        compiler_params=_params(("arbitrary", "arbitrary")),
    )(p, p, ydn, ylru, b_merge, dmix, dp)


def _post_math(x, mix, g1, gate, g2, sh, sc):
    h1 = x + _rmsn(mix, g1) * gate
    return h1, _rmsn(h1, g2) * (1.0 + sc) + sh


def _post_fwd(x, mix, gains, vecs, *, tm):
    B, N, D = x.shape
    ntl = N // tm

    def body(x_ref, mix_ref, g_ref, v_ref, h1_ref, u2_ref):
        v = v_ref[0]
        h1, u2 = _post_math(x_ref[0], mix_ref[...], g_ref[0:1], v[0:1], g_ref[1:2], v[1:2], v[2:3])
        h1_ref[...] = h1
        u2_ref[...] = u2.astype(BF16)

    row = pl.BlockSpec((tm, D), lambda b, t: (b * ntl + t, 0))
    return pl.pallas_call(
        body, name="post_fwd", grid=(B, ntl),
        in_specs=[pl.BlockSpec((1, tm, D), lambda b, t: (b, t, 0)), row,
                  pl.BlockSpec((8, D), lambda b, t: (0, 0)), pl.BlockSpec((1, 8, D), lambda b, t: (b, 0, 0))],
        out_specs=[row, row],
        out_shape=[jax.ShapeDtypeStruct((B * N, D), F32), jax.ShapeDtypeStruct((B * N, D), BF16)],
        compiler_params=_params(("parallel", "parallel")),
    )(x, mix, gains, vecs)


def _post_bwd(x, mix, gains, vecs, dh1, du2, *, tm):
    B, N, D = x.shape
    ntl = N // tm

    def body(x_ref, mix_ref, g_ref, v_ref, dh1_ref, du2_ref, dx_ref, dmix_ref, sums_ref):
        v = v_ref[0]
        _, vjp = jax.vjp(_post_math, x_ref[0], mix_ref[...], g_ref[0:1], v[0:1], g_ref[1:2], v[1:2], v[2:3])
        dx, dmix, dg1, dgate, dg2, dsh, dsc = vjp((dh1_ref[...], du2_ref[...]))
        dx_ref[...] = dx
        dmix_ref[...] = dmix.astype(BF16)

        @pl.when(pl.program_id(1) == 0)
        def _():
            sums_ref[...] = jnp.zeros_like(sums_ref)

        sums_ref[0, 0:1, :] += dgate
        sums_ref[0, 1:2, :] += dsh
        sums_ref[0, 2:3, :] += dsc
        sums_ref[0, 3:4, :] += dg1
        sums_ref[0, 4:5, :] += dg2

    row = pl.BlockSpec((tm, D), lambda b, t: (b * ntl + t, 0))
    return pl.pallas_call(
        body, name="post_bwd", grid=(B, ntl),
        in_specs=[pl.BlockSpec((1, tm, D), lambda b, t: (b, t, 0)), row,
                  pl.BlockSpec((8, D), lambda b, t: (0, 0)), pl.BlockSpec((1, 8, D), lambda b, t: (b, 0, 0)), row, row],
        out_specs=[row, row, pl.BlockSpec((1, 8, D), lambda b, t: (b, 0, 0))],
        out_shape=[jax.ShapeDtypeStruct((B * N, D), F32), jax.ShapeDtypeStruct((B * N, D), BF16),
                   jax.ShapeDtypeStruct((B, 8, D), F32)],
        compiler_params=_params(("parallel", "arbitrary")),
    )(x, mix, gains, vecs, dh1, du2)


def _final_math(dn, g4, gate5):
    return _rmsn(dn, g4) * gate5


def _final(h1, dn, target, gains, vecs, *, tm):
    B, N, D = target.shape
    ntl = N // tm

    def body(h1_ref, dn_ref, t_ref, g_ref, v_ref, ddn_ref, dout_ref, sums_ref):
        v = v_ref[0]
        y, vjp = jax.vjp(_final_math, dn_ref[...], g_ref[2:3], v[3:4])
        err = h1_ref[...] + y - t_ref[0]
        dout = err * (1.0 / D)
        ddn, dg4, dgate5 = vjp(dout)
        ddn_ref[...] = ddn.astype(BF16)
        dout_ref[...] = dout

        @pl.when(pl.program_id(1) == 0)
        def _():
            sums_ref[...] = jnp.zeros_like(sums_ref)

        sums_ref[0, 0:1, :] += dgate5
        sums_ref[0, 1:2, :] += dg4
        sums_ref[0, 2:3, :] += jnp.sum(err * err, axis=0, keepdims=True) * (0.5 / D)

    row = pl.BlockSpec((tm, D), lambda b, t: (b * ntl + t, 0))
    return pl.pallas_call(
        body, name="final", grid=(B, ntl),
        in_specs=[row, row, pl.BlockSpec((1, tm, D), lambda b, t: (b, t, 0)),
                  pl.BlockSpec((8, D), lambda b, t: (0, 0)), pl.BlockSpec((1, 8, D), lambda b, t: (b, 0, 0))],
        out_specs=[row, row, pl.BlockSpec((1, 8, D), lambda b, t: (b, 0, 0))],
        out_shape=[jax.ShapeDtypeStruct((B * N, D), BF16), jax.ShapeDtypeStruct((B * N, D), F32),
                   jax.ShapeDtypeStruct((B, 8, D), F32)],
        compiler_params=_params(("parallel", "arbitrary")),
    )(h1, dn, target, gains, vecs)


def _shift(x, s):
    s = s % x.shape[0]
    return x if s == 0 else pltpu.roll(x, s, 0)


def _seg_taps(T, nc, width, pad_left):
    t = lax.broadcasted_iota(jnp.int32, (T, 1), 0)
    pos = jnp.where(t < nc, t, t - nc)
    seg = jnp.where(t < nc, nc, T - nc)
    taps = []
    for k in range(width):
        src = pos + (k - pad_left)
        taps.append((pad_left - k, (src >= 0) & (src < seg)))
    return taps


def _grid_taps(N):
    t = lax.broadcasted_iota(jnp.int32, (N, 1), 0)
    wcol = t % GRID_W
    taps = []
    for dr in (-1, 0, 1):
        for dw in (-1, 0, 1):
            off = dr * GRID_W + dw
            ok = (wcol + dw >= 0) & (wcol + dw < GRID_W) & (t + dr * GRID_W >= 0) & (t + dr * GRID_W < N)
            taps.append((-off, ok))
    return taps


def _conv_fwd(x, w, taps):
    y = jnp.zeros_like(x)
    for k, (s, m) in enumerate(taps):
        y = y + w[k:k + 1] * jnp.where(m, _shift(x, s), 0.0)
    return y


def _conv_bwd(x, w, taps, dy):
    dx = jnp.zeros_like(x)
    dws = []
    for k, (s, m) in enumerate(taps):
        dym = jnp.where(m, dy, 0.0)
        dx = dx + w[k:k + 1] * _shift(dym, -s)
        dws.append(jnp.sum(dym * _shift(x, s), axis=0, keepdims=True))
    return dx, jnp.concatenate(dws, axis=0)


def _ffn_act_fwd(F, w9, bias, *, B, N, DFF, tc):
    nj = DFF // tc

    def body(fg_ref, fv_ref, w_ref, b_ref, o_ref, pre_ref):
        fg = _conv_fwd(fg_ref[...], w_ref[...], _grid_taps(N)) + b_ref[...]
        pre_ref[...] = fg
        o_ref[...] = (_gelu(fg) * fv_ref[...]).astype(BF16)

    col = pl.BlockSpec((N, tc), lambda b, j: (b, j))
    return pl.pallas_call(
        body, name="ffn_act_fwd", grid=(B, nj),
        in_specs=[col, pl.BlockSpec((N, tc), lambda b, j: (b, nj + j)),
                  pl.BlockSpec((9, tc), lambda b, j: (0, j)), pl.BlockSpec((1, tc), lambda b, j: (0, j))],
        out_specs=[col, col],
        out_shape=[jax.ShapeDtypeStruct((B * N, DFF), BF16), jax.ShapeDtypeStruct((B * N, DFF), F32)],
        compiler_params=_params(("parallel", "parallel")),
    )(F, F, w9, bias)


def _ffn_act_bwd(F, pre, w9, df, *, B, N, DFF, tc):
    nj = DFF // tc

    def body(fg_ref, fv_ref, w_ref, pre_ref, df_ref, dfg_ref, dfv_ref, dwb_ref):
        taps = _grid_taps(N)
        x = fg_ref[...]
        fg, vjp = jax.vjp(lambda a: _gelu(a), pre_ref[...])
        dfl = df_ref[...]
        dfv_ref[...] = (dfl * fg).astype(BF16)
        (dpre,) = vjp(dfl * fv_ref[...])
        dx, dw = _conv_bwd(x, w_ref[...], taps, dpre)
        dfg_ref[...] = dx.astype(BF16)

        @pl.when(pl.program_id(1) == 0)
        def _():
            dwb_ref[...] = jnp.zeros_like(dwb_ref)

        dwb_ref[0:9, :] += dw
        dwb_ref[9:10, :] += jnp.sum(dpre, axis=0, keepdims=True)

    col = pl.BlockSpec((N, tc), lambda j, b: (b, j))
    return pl.pallas_call(
        body, name="ffn_act_bwd", grid=(nj, B),
        in_specs=[col, pl.BlockSpec((N, tc), lambda j, b: (b, nj + j)), pl.BlockSpec((9, tc), lambda j, b: (0, j)), col, col],
        out_specs=[col, col, pl.BlockSpec((16, tc), lambda j, b: (0, j))],
        out_shape=[jax.ShapeDtypeStruct((B * N, DFF), BF16), jax.ShapeDtypeStruct((B * N, DFF), BF16),
                   jax.ShapeDtypeStruct((16, DFF), F32)],
        compiler_params=_params(("parallel", "arbitrary")),
    )(F, F, w9, pre, df)


def _dnprep_math(y, is_qk, scale):
    s = _silu(y)
    n = s * lax.rsqrt(jnp.sum(s * s, axis=-1, keepdims=True) + EPS) * scale
    return jnp.where(is_qk, n, s)


def _dnprep_fwd(p, cw, *, B, T, nc, H, HD):
    def body(x_ref, w_ref, o_ref):
        j = pl.program_id(1)
        y = _conv_fwd(x_ref[...], w_ref[...], _seg_taps(T, nc, 4, 2))
        o_ref[...] = _dnprep_math(y, j < 2 * H, jnp.where(j < H, HD ** -0.5, 1.0))

    return pl.pallas_call(
        body, name="dnprep_fwd", grid=(B, 3 * H),
        in_specs=[pl.BlockSpec((T, HD), lambda b, j: (b, j)), pl.BlockSpec((4, HD), lambda b, j: (0, j))],
        out_specs=pl.BlockSpec((T, HD), lambda b, j: (b, j)),
        out_shape=jax.ShapeDtypeStruct((B * T, 3 * H * HD), F32),
        compiler_params=_params(("parallel", "parallel")),
    )(p, cw)


def _dnprep_bwd(p, cw, dqkv, dp, *, B, T, nc, H, HD):
    def body(x_ref, w_ref, d_ref, dp_any, dp_ref, dcw_ref):
        j = pl.program_id(0)
        taps = _seg_taps(T, nc, 4, 2)
        x = x_ref[...]
        y = _conv_fwd(x, w_ref[...], taps)
        is_qk, scale = j < 2 * H, jnp.where(j < H, HD ** -0.5, 1.0)
        _, vjp = jax.vjp(lambda a: _dnprep_math(a, is_qk, scale), y)
        (dy,) = vjp(d_ref[0])
        dx, dw = _conv_bwd(x, w_ref[...], taps, dy)
        dp_ref[...] = dx.astype(BF16)

        @pl.when(pl.program_id(1) == 0)
        def _():
            dcw_ref[...] = jnp.zeros_like(dcw_ref)

        dcw_ref[0:4, :] += dw

    col = pl.BlockSpec((T, HD), lambda j, b: (b, j))
    return pl.pallas_call(
        body, name="dnprep_bwd", grid=(3 * H, B),
        in_specs=[col, pl.BlockSpec((4, HD), lambda j, b: (0, j)),
                  pl.BlockSpec((1, T, HD), lambda j, b: (j // H, b, j % H)), pl.BlockSpec(memory_space=pl.ANY)],
        out_specs=[col, pl.BlockSpec((8, HD), lambda j, b: (0, j))],
        out_shape=[jax.ShapeDtypeStruct(dp.shape, dp.dtype), jax.ShapeDtypeStruct((8, 3 * H * HD), F32)],
        input_output_aliases={3: 0},
        compiler_params=_params(("parallel", "arbitrary")),
    )(p, cw, dqkv, dp)


def _gb_math(ab, alog, dtb, H):
    lane = lax.broadcasted_iota(jnp.int32, ab.shape, 1)
    g = -jnp.exp(alog) * _softplus(ab + dtb)
    return jnp.where(lane < 2 * H, g, jnp.where(lane < 4 * H, _sigmoid(ab), 0.0))


def _gb_fwd(p, prm, *, rows, col0, H, tm):
    def body(x_ref, prm_ref, o_ref):
        o_ref[...] = _gb_math(x_ref[...], prm_ref[0:1], prm_ref[1:2], H)

    return pl.pallas_call(
        body, name="gb_fwd", grid=(rows // tm,),
        in_specs=[pl.BlockSpec((tm, LANES), lambda t: (t, col0 // LANES)), pl.BlockSpec((8, LANES), lambda t: (0, 0))],
        out_specs=pl.BlockSpec((tm, LANES), lambda t: (t, 0)),
        out_shape=jax.ShapeDtypeStruct((rows, LANES), F32),
        compiler_params=_params(("parallel",)),
    )(p, prm)


def _gb_bwd(p, prm, dgb, dp, *, rows, col0, H, tm):
    def body(x_ref, prm_ref, d_ref, dp_any, dp_ref, dprm_ref):
        _, vjp = jax.vjp(lambda a, b, c: _gb_math(a, b, c, H), x_ref[...], prm_ref[0:1], prm_ref[1:2])
        dab, dalog, ddtb = vjp(d_ref[...])
        dp_ref[...] = dab.astype(BF16)

        @pl.when(pl.program_id(0) == 0)
        def _():
            dprm_ref[...] = jnp.zeros_like(dprm_ref)

        dprm_ref[0:1, :] += dalog
        dprm_ref[1:2, :] += ddtb

    blk = pl.BlockSpec((tm, LANES), lambda t: (t, col0 // LANES))
    return pl.pallas_call(
        body, name="gb_bwd", grid=(rows // tm,),
        in_specs=[blk, pl.BlockSpec((8, LANES), lambda t: (0, 0)), pl.BlockSpec((tm, LANES), lambda t: (t, 0)),
                  pl.BlockSpec(memory_space=pl.ANY)],
        out_specs=[blk, pl.BlockSpec((8, LANES), lambda t: (0, 0))],
        out_shape=[jax.ShapeDtypeStruct(dp.shape, dp.dtype), jax.ShapeDtypeStruct((8, LANES), F32)],
        input_output_aliases={3: 0},
        compiler_params=_params(("arbitrary",)),
    )(p, prm, dgb, dp)


def _lru_scans(scans):
    C = scans[0][0].shape[1]
    row = lax.broadcasted_iota(jnp.int32, (SUBLANES, C), 0)
    carries = tuple(jnp.zeros((1, C), F32) for _ in scans)
    for si in range(len(scans[0][4])):
        rows = scans[0][4][si][1]
        assert all(sc[4][si][1] == rows for sc in scans)
        sub = max(s for s in (4, 2, 1) if rows % (s * SUBLANES) == 0)
        span = sub * SUBLANES
        nb = rows // span

        def blk(i, carries, si=si, nb=nb, sub=sub, span=span):
            out = []
            for (a_ref, b_ref, h_ref, hp_ref, segs), carry in zip(scans, carries):
                start, _, reverse = segs[si]
                r0 = pl.multiple_of(start + (nb - 1 - i if reverse else i) * span, span)
                local = []
                for j in range(sub):
                    A = a_ref[pl.ds(r0 + j * SUBLANES, SUBLANES), :]
                    Bv = b_ref[pl.ds(r0 + j * SUBLANES, SUBLANES), :]
                    for s in (1, 2, 4):
                        sh = SUBLANES - s if reverse else s
                        m = (row < SUBLANES - s) if reverse else (row >= s)
                        Bv = jnp.where(m, A * pltpu.roll(Bv, sh, 0) + Bv, Bv)
                        A = jnp.where(m, A * pltpu.roll(A, sh, 0), A)
                    local.append((A, Bv))
                for j in (reversed(range(sub)) if reverse else range(sub)):
                    A, Bv = local[j]
                    Hv = Bv + A * carry
                    h_ref[pl.ds(r0 + j * SUBLANES, SUBLANES), :] = Hv
                    if hp_ref is not None:
                        if reverse:
                            hp = jnp.where(row < SUBLANES - 1, pltpu.roll(Hv, SUBLANES - 1, 0), carry)
                        else:
                            hp = jnp.where(row >= 1, pltpu.roll(Hv, 1, 0), carry)
                        hp_ref[pl.ds(r0 + j * SUBLANES, SUBLANES), :] = hp
                    carry = Hv[0:1] if reverse else Hv[SUBLANES - 1:SUBLANES]
                out.append(carry)
            return tuple(out)

        carries = lax.fori_loop(0, nb, blk, carries)


def _lru_orders(T, nc, d):
    N = T - nc
    if d == 0:
        return [(0, nc, False), (nc, N, False)], [(nc, N, True), (0, nc, True)]
    return [(0, nc, True), (nc, N, True)], [(nc, N, False), (0, nc, False)]


def _bdot(a, b, dims=(((1,), (0,)), ((), ()))):
    return lax.dot_general(a.astype(BF16), b.astype(BF16), dims, preferred_element_type=F32)


_NT = (((1,), (1,)), ((), ()))
_TN = (((0,), (0,)), ((), ()))


def _blockdiag(w, C):
    nd, nb, bd, _ = w.shape
    per = C // bd
    out = jnp.einsum('dnpij,pq->dnpiqj', w.reshape(nd, nb // per, per, bd, bd), jnp.eye(per, dtype=w.dtype))
    return out.reshape(nd, nb // per, C, C)


def _blockdiag_extract(dw, bd):
    nd, nj, C, _ = dw.shape
    per = C // bd
    out = jnp.einsum('dnpiqj,pq->dnpij', dw.reshape(nd, nj, per, bd, per, bd), jnp.eye(per, dtype=dw.dtype))
    return out.reshape(nd, nj * per, bd, bd)


def _lru_fwd(p, cw, lv, wr, wi, *, B, T, nc, LW, col0, C):
    N = T - nc
    nj = LW // C

    def body(x_ref, cw_ref, lv_ref, wr_ref, wi_ref, o_ref, a_s, b_s, h_s):
        lv_ = lv_ref[...]
        xc = _conv_fwd(x_ref[:, 0:C], cw_ref[...], _seg_taps(T, nc, 4, 2)) + lv_[0:1]
        for d in (0, 1):
            r = _sigmoid(_bdot(xc, wr_ref[d, 0]) + lv_[1 + d:2 + d])
            i = _sigmoid(_bdot(xc, wi_ref[d, 0]) + lv_[3 + d:4 + d])
            la = -LRU_C * r * _softplus(-lv_[5 + d:6 + d])
            a_s[d] = jnp.exp(la)
            b_s[d] = jnp.sqrt(1.0 - jnp.exp(2.0 * la)) * i * xc
        _lru_scans([(a_s.at[d], b_s.at[d], h_s.at[d], None, _lru_orders(T, nc, d)[0]) for d in (0, 1)])
        o_ref[...] = ((h_s[0, nc:, :] + h_s[1, nc:, :]) * _gelu(x_ref[nc:, C:2 * C])).astype(BF16)

    return pl.pallas_call(
        body, name="lru_fwd", grid=(B, nj),
        in_specs=[pl.BlockSpec((T, 2 * C), lambda b, j: (b, col0 // (2 * C) + j)),
                  pl.BlockSpec((4, C), lambda b, j: (0, j)), pl.BlockSpec((8, C), lambda b, j: (0, j)),
                  pl.BlockSpec((2, 1, C, C), lambda b, j: (0, j, 0, 0)), pl.BlockSpec((2, 1, C, C), lambda b, j: (0, j, 0, 0))],
        out_specs=pl.BlockSpec((N, C), lambda b, j: (b, j)),
        out_shape=jax.ShapeDtypeStruct((B * N, LW), BF16),
        scratch_shapes=[pltpu.VMEM((2, T, C), F32)] * 3,
        compiler_params=_params(("parallel", "parallel")),
    )(p, cw, lv, wr, wi)


def _lru_bwd(p, cw, lv, wr, wi, dy, dp, *, B, T, nc, LW, col0, C):
    N = T - nc
    nj = LW // C

    def body(x_ref, cw_ref, lv_ref, wr_ref, wi_ref, dy_ref, dp_any, dp_ref, dcw_ref, dlv_ref, dwr_ref, dwi_ref,
             a_s, b_s, h_s, hp_s, mu_s, mup_s, dh_s, dxc_s):
        taps = _seg_taps(T, nc, 4, 2)
        lv_ = lv_ref[...]
        xl = x_ref[:, 0:C]
        xc = _conv_fwd(xl, cw_ref[...], taps) + lv_[0:1]
        gel, gelu_vjp = jax.vjp(_gelu, x_ref[nc:, C:2 * C])
        dh_s[0:nc, :] = jnp.zeros((nc, C), F32)
        dh_s[nc:, :] = dy_ref[...] * gel
        dxc_s[...] = jnp.zeros_like(dxc_s)

        @pl.when(pl.program_id(1) == 0)
        def _():
            dcw_ref[...] = jnp.zeros_like(dcw_ref)
            dlv_ref[...] = jnp.zeros_like(dlv_ref)
            dwr_ref[...] = jnp.zeros_like(dwr_ref)
            dwi_ref[...] = jnp.zeros_like(dwi_ref)

        def gates(d):
            lam = lv_[5 + d:6 + d]
            r = _sigmoid(_bdot(xc, wr_ref[d, 0]) + lv_[1 + d:2 + d])
            i = _sigmoid(_bdot(xc, wi_ref[d, 0]) + lv_[3 + d:4 + d])
            sp = _softplus(-lam)
            la = -LRU_C * r * sp
            e2 = jnp.exp(2.0 * la)
            return lam, r, i, sp, la, e2, jnp.sqrt(1.0 - e2)

        for d in (0, 1):
            _, _, i, _, la, _, mult = gates(d)
            a_s[d] = jnp.exp(la)
            b_s[d] = mult * i * xc
        _lru_scans([(a_s.at[d], b_s.at[d], h_s.at[d], hp_s.at[d], _lru_orders(T, nc, d)[0]) for d in (0, 1)])
        for d in (0, 1):
            b_s[d] = a_s[d] * dh_s[...]
        _lru_scans([(a_s.at[d], b_s.at[d], mu_s.at[d], mup_s.at[d], _lru_orders(T, nc, d)[1]) for d in (0, 1)])

        for d in (0, 1):
            lam, r, i, sp, la, e2, mult = gates(d)
            a = a_s[d]
            dinp = dh_s[...] + mup_s[d]
            da = dinp * hp_s[d]
            dmult = dinp * i * xc
            di = dinp * mult * xc
            dla = da * a - dmult * e2 / mult
            dpre_r = (dla * (-LRU_C * sp)) * r * (1.0 - r)
            dpre_i = di * i * (1.0 - i)
            dsp = jnp.sum(dla * (-LRU_C * r), axis=0, keepdims=True)
            dxc_s[...] += dinp * mult * i + _bdot(dpre_r, wr_ref[d, 0], _NT) + _bdot(dpre_i, wi_ref[d, 0], _NT)
            dwr_ref[d, 0] += _bdot(xc, dpre_r, _TN)
            dwi_ref[d, 0] += _bdot(xc, dpre_i, _TN)
            dlv_ref[1 + d:2 + d, :] += jnp.sum(dpre_r, axis=0, keepdims=True)
            dlv_ref[3 + d:4 + d, :] += jnp.sum(dpre_i, axis=0, keepdims=True)
            dlv_ref[5 + d:6 + d, :] += -dsp * _sigmoid(-lam)

        dxc = dxc_s[...]
        dxl, dw = _conv_bwd(xl, cw_ref[...], taps, dxc)
        dcw_ref[0:4, :] += dw
        dlv_ref[0:1, :] += jnp.sum(dxc, axis=0, keepdims=True)
        dp_ref[:, 0:C] = dxl.astype(BF16)
        (dyl,) = gelu_vjp(dy_ref[...] * (h_s[0, nc:, :] + h_s[1, nc:, :]))
        dp_ref[0:nc, C:2 * C] = jnp.zeros((nc, C), BF16)
        dp_ref[nc:, C:2 * C] = dyl.astype(BF16)

    xblk = pl.BlockSpec((T, 2 * C), lambda j, b: (b, col0 // (2 * C) + j))
    wblk = pl.BlockSpec((2, 1, C, C), lambda j, b: (0, j, 0, 0))
    vblk = pl.BlockSpec((8, C), lambda j, b: (0, j))
    return pl.pallas_call(
        body, name="lru_bwd", grid=(nj, B),
        in_specs=[xblk, pl.BlockSpec((4, C), lambda j, b: (0, j)), vblk, wblk, wblk,
                  pl.BlockSpec((N, C), lambda j, b: (b, j)), pl.BlockSpec(memory_space=pl.ANY)],
        out_specs=[xblk, vblk, vblk, wblk, wblk],
        out_shape=[jax.ShapeDtypeStruct(dp.shape, dp.dtype), jax.ShapeDtypeStruct((8, LW), F32),
                   jax.ShapeDtypeStruct((8, LW), F32), jax.ShapeDtypeStruct((2, nj, C, C), F32),
                   jax.ShapeDtypeStruct((2, nj, C, C), F32)],
        scratch_shapes=[pltpu.VMEM((2, T, C), F32)] * 6 + [pltpu.VMEM((T, C), F32)] * 2,
        input_output_aliases={6: 0},
        compiler_params=_params(("parallel", "arbitrary")),
    )(p, cw, lv, wr, wi, dy, dp)


def _chunk_masks(upper):
    i = lax.broadcasted_iota(jnp.int32, (CHUNK, CHUNK), 0)
    j = lax.broadcasted_iota(jnp.int32, (CHUNK, CHUNK), 1)
    ahead = jnp.where(upper, j - i, i - j)
    return i == j, ahead >= 0, ahead > 0


def _col2row(c, eye):
    return jnp.sum(jnp.where(eye, c, 0.0), axis=0, keepdims=True)


def _row2col(r, eye):
    return jnp.sum(jnp.where(eye, r, 0.0), axis=1, keepdims=True)


def _rowsum(x):
    return jnp.sum(x, axis=1, keepdims=True)


_INV_BASE = 8


def _unit_tri_inverses(Ls):
    G = len(Ls)
    W = G * CHUNK
    blk = (lax.broadcasted_iota(jnp.int32, (W, W), 0) // CHUNK) == (lax.broadcasted_iota(jnp.int32, (W, W), 1) // CHUNK)
    ri = lax.broadcasted_iota(jnp.int32, (CHUNK, W), 0)
    ci = lax.broadcasted_iota(jnp.int32, (CHUNK, W), 1) % CHUNK

    def bd(b):
        return jnp.where(blk, jnp.tile(b, (G, 1)), jnp.zeros((), BF16))

    def pdot(a, b):
        return jnp.dot(a.astype(BF16), bd(b.astype(BF16)), preferred_element_type=F32)

    Lc = Ls[0] if G == 1 else jnp.concatenate(Ls, axis=1)
    s = _INV_BASE
    Xp = -jnp.where(ri // s == ci // s, Lc, 0.0)
    Rm = Xp
    for _ in range(int(math.log2(s)) - 1):
        Xp = pdot(Xp, Xp)
        Rm = Rm + Xp + pdot(Rm, Xp)
    while s < CHUNK:
        E = jnp.where((ri // (2 * s) == ci // (2 * s)) & (ri // s != ci // s), Lc, 0.0)
        DE = E + pdot(Rm, E)
        Rm = Rm - (DE + pdot(DE, Rm))
        s *= 2
    eye = _chunk_masks(False)[0]
    return [jnp.where(eye, 1.0, 0.0) + Rm[:, g * CHUNK:(g + 1) * CHUNK] for g in range(G)]


def _delta_chunk_common(q, k, v, gcol, bcol, upper):
    eye, incl, strict = _chunk_masks(upper)
    gc = _rowsum(jnp.where(incl, _col2row(gcol, eye), 0.0))
    D = jnp.where(incl, jnp.exp(jnp.minimum(gc - _col2row(gc, eye), 0.0)), 0.0)
    kb = k * bcol
    AP = _bdot(jnp.concatenate([kb, q], axis=0), k, _NT)
    A = AP[:CHUNK]
    L = jnp.where(strict, A * D, 0.0)
    eg = jnp.exp(gc)
    gl = jnp.sum(gcol, axis=0, keepdims=True)
    attn = jnp.where(incl, AP[CHUNK:] * D, 0.0)
    return dict(eye=eye, incl=incl, strict=strict, gc=gc, D=D, kb=kb, A=A, L=L, eg=eg, gl=gl, egl=jnp.exp(gl),
                attn=attn, kbe=kb * eg, vb=v * bcol, qe=q * eg, kd=k * jnp.exp(gl - gc))


def _delta_group_pre(chunks, upper):
    cs = [_delta_chunk_common(*ch, upper) for ch in chunks]
    out = []
    for c, Tm in zip(cs, _unit_tri_inverses([c["L"] for c in cs])):
        dk = c["kbe"].shape[1]
        wu = _bdot(Tm, jnp.concatenate([c["kbe"], c["vb"]], axis=1))
        KN = _bdot(c["kd"], wu, _TN)
        QO = _bdot(c["attn"], wu)
        out.append((Tm, KN[:, :dk], KN[:, dk:], c["qe"] - QO[:, :dk], QO[:, dk:], c["egl"]))
    return out


def _delta_chunk_bwd(q, k, v, gcol, bcol, S, Tm, do, dS2, upper):
    c = _delta_chunk_common(q, k, v, gcol, bcol, upper)
    eye, incl, strict, D, eg, egl = c["eye"], c["incl"], c["strict"], c["D"], c["eg"], c["egl"]
    kb, kbe, vb, qe, kd, attn = c["kb"], c["kbe"], c["vb"], c["qe"], c["kd"], c["attn"]
    dkk = kbe.shape[1]
    wu = _bdot(Tm, jnp.concatenate([kbe, vb], axis=1))
    w = wu[:, :dkk]
    vn = wu[:, dkk:] - _bdot(w, S)
    dvn = _bdot(kd, dS2) + _bdot(attn, do, _TN)
    dkd = _bdot(vn, dS2, _NT)
    dgl = jnp.sum(_rowsum(dS2 * S), axis=0, keepdims=True) * egl
    dqa = _bdot(do, jnp.concatenate([S, vn], axis=0), _NT)
    dqe = dqa[:, :dkk]
    dattn = jnp.where(incl, dqa[:, dkk:], 0.0)
    dw = -_bdot(dvn, S, _NT)
    r = _rowsum(dkd * kd)
    dk = dkd * jnp.exp(c["gl"] - c["gc"])
    dgl = dgl + jnp.sum(r, axis=0, keepdims=True)
    dgc = _rowsum(dqe * qe) - r
    E = dattn * attn
    dvw = jnp.concatenate([dvn, dw], axis=1)
    dTm = _bdot(dvw, jnp.concatenate([vb, kbe], axis=1), _NT)
    dvk = _bdot(Tm, dvw, _TN)
    dvb = dvk[:, :dvn.shape[1]]
    dv = dvb * bcol
    dbeta = _rowsum(dvb * v)
    dkbe = dvk[:, dvn.shape[1]:]
    dkb = dkbe * eg
    dgc = dgc + _rowsum(dkbe * kbe)
    dL = jnp.where(strict, -_bdot(Tm, _bdot(dTm, Tm, _NT), _TN), 0.0)
    dA = dL * D
    E = E + dL * c["L"]
    PA = jnp.concatenate([dattn * D, dA], axis=0)
    PAk = _bdot(PA, k)
    dq = dqe * eg + PAk[:CHUNK]
    dkb = dkb + PAk[CHUNK:]
    dk = dk + _bdot(PA, jnp.concatenate([q, kb], axis=0), _TN) + dkb * bcol
    dbeta = dbeta + _rowsum(dkb * k)
    dgc = dgc + _rowsum(E) - _row2col(jnp.sum(E, axis=0, keepdims=True), eye)
    dg = _row2col(jnp.sum(jnp.where(incl, dgc, 0.0), axis=0, keepdims=True), eye) + dgl
    return dq, dk, dv, dg, dbeta


def _delta_unroll(trips):
    return max(u for u in (3, 2, 1) if trips % u == 0)


def _delta_group(n):
    return max(g for g in range(1, 2 * LANES // CHUNK + 1) if n % g == 0)


def _delta_chunk_at(T, nc, d, i):
    n, ncc = T // CHUNK, nc // CHUNK
    desc = jnp.where(i < ncc, ncc - 1 - i, n - 1 - (i - ncc))
    if isinstance(d, int):
        return i if d == 0 else desc
    return jnp.where(d == 0, i, desc)


def _dn_out_math(o, onorm, z):
    return _rmsn(o, onorm) * _silu(z)


def _delta_fwd(qkv, gb, p, onorm, *, B, T, nc, H, HD):
    N = T - nc
    n = T // CHUNK
    G = _delta_group(n)

    def body(q_ref, k_ref, v_ref, gb_ref, z_ref, on_ref, y_ref, o_ref, Tm_ref, K_ref, S_ref, Qp_ref, eg_ref,
             N_s, O0_s, o_s):
        h = pl.program_id(1)
        lane = lax.broadcasted_iota(jnp.int32, (CHUNK, LANES), 1)

        def pre(g, carry):
            cs = [g * G + i for i in range(G)]
            rows = [pl.ds(pl.multiple_of(c * CHUNK, CHUNK), CHUNK) for c in cs]
            for d in (0, 1):
                chunks = []
                for r in rows:
                    gbb = gb_ref[r, :]
                    chunks.append((q_ref[r, :], k_ref[r, :], v_ref[r, :],
                                   _rowsum(jnp.where(lane == d * H + h, gbb, 0.0)),
                                   _rowsum(jnp.where(lane == 2 * H + d * H + h, gbb, 0.0))))
                for c, r, (Tm, K, Nn, Qp, O0, egl) in zip(cs, rows, _delta_group_pre(chunks, d == 1)):
                    Tm_ref[0, d * n + c] = Tm
                    K_ref[0, d * n + c] = K.astype(BF16)
                    N_s[d * n + c] = Nn
                    Qp_ref[0, d, r, :] = Qp.astype(BF16)
                    O0_s[d, r, :] = O0
                    eg_ref[0, d * n + c] = jnp.broadcast_to(egl, (SUBLANES, HD))
            return carry

        lax.fori_loop(0, n // G, pre, 0)

        def step(i, Ss):
            out = []
            for d in (0, 1):
                c = _delta_chunk_at(T, nc, d, i)
                rows = pl.ds(pl.multiple_of(c * CHUNK, CHUNK), CHUNK)
                S_ref[0, d * n + c] = Ss[d]
                Sb = Ss[d].astype(BF16)
                o_s[d, rows, :] = jnp.dot(Qp_ref[0, d, rows, :], Sb, preferred_element_type=F32) + O0_s[d, rows, :]
                out.append(eg_ref[0, d * n + c][0:1] * Ss[d] + N_s[d * n + c]
                           - jnp.dot(K_ref[0, d * n + c], Sb, preferred_element_type=F32))
            return tuple(out)

        lax.fori_loop(0, n, step, (jnp.zeros((HD, HD), F32), jnp.zeros((HD, HD), F32)))
        o = o_s[0, nc:, :] + o_s[1, nc:, :]
        o_ref[...] = o
        y_ref[...] = _dn_out_math(o, on_ref[...], z_ref[nc:, :]).astype(BF16)

    col = lambda off: pl.BlockSpec((T, HD), lambda b, h: (b, off + h))
    lat = pl.BlockSpec((N, HD), lambda b, h: (b, h))
    per = lambda *blk: pl.BlockSpec((1, *blk), lambda b, h: (b * H + h, 0, 0, 0))
    return pl.pallas_call(
        body, name="delta_fwd", grid=(B, H),
        in_specs=[col(0), col(H), col(2 * H), pl.BlockSpec((T, LANES), lambda b, h: (b, 0)), col(3 * H),
                  pl.BlockSpec((1, HD), lambda b, h: (0, 0))],
        out_specs=[lat, lat, per(2 * n, CHUNK, CHUNK), per(2 * n, HD, HD), per(2 * n, HD, HD), per(2, T, HD),
                   per(2 * n, SUBLANES, HD)],
        out_shape=[jax.ShapeDtypeStruct((B * N, H * HD), BF16), jax.ShapeDtypeStruct((B * N, H * HD), F32),
                   jax.ShapeDtypeStruct((B * H, 2 * n, CHUNK, CHUNK), F32),
                   jax.ShapeDtypeStruct((B * H, 2 * n, HD, HD), BF16), jax.ShapeDtypeStruct((B * H, 2 * n, HD, HD), F32),
                   jax.ShapeDtypeStruct((B * H, 2, T, HD), BF16), jax.ShapeDtypeStruct((B * H, 2 * n, SUBLANES, HD), F32)],
        scratch_shapes=[pltpu.VMEM((2 * n, HD, HD), F32), pltpu.VMEM((2, T, HD), F32), pltpu.VMEM((2, T, HD), F32)],
        compiler_params=_params(("parallel", "parallel")),
    )(qkv, qkv, qkv, gb, p, onorm)


def _delta_bwd(qkv, gb, p, onorm, o, res, dy, dp, *, B, T, nc, H, HD):
    N = T - nc
    n = T // CHUNK

    def body(q_ref, k_ref, v_ref, gb_ref, z_ref, on_ref, o_ref, dy_ref, Tm_ref, K_ref, S_ref, Qp_ref, eg_ref, dp_any,
             dqkv_ref, dgb_ref, dp_ref, don_ref, do_s, R_s, dS_s):
        h, d = pl.program_id(1), pl.program_id(2)
        lane = lax.broadcasted_iota(jnp.int32, (CHUNK, LANES), 1)

        @pl.when(d == 0)
        def _():
            _, vjp = jax.vjp(_dn_out_math, o_ref[...], on_ref[...], z_ref[nc:, :])
            do, don, dz = vjp(dy_ref[...])
            do_s[0:nc, :] = jnp.zeros((nc, HD), F32)
            do_s[nc:, :] = do
            dp_ref[0:nc, :] = jnp.zeros((nc, HD), BF16)
            dp_ref[nc:, :] = dz.astype(BF16)
            dqkv_ref[...] = jnp.zeros_like(dqkv_ref)

            @pl.when(h == 0)
            def _():
                don_ref[...] = jnp.zeros_like(don_ref)
                dgb_ref[...] = jnp.zeros_like(dgb_ref)

            don_ref[0, 0:1, :] += don

        def r_of(c, carry):
            rows = pl.ds(pl.multiple_of(c * CHUNK, CHUNK), CHUNK)
            R_s[c] = lax.dot_general(Qp_ref[0, 0, rows, :], do_s[rows, :].astype(BF16), _TN, preferred_element_type=F32)
            return carry

        lax.fori_loop(0, n, r_of, 0)

        def bwd_step(i, dS):
            c = _delta_chunk_at(T, nc, d, n - 1 - i)
            dS_s[c] = dS
            return (eg_ref[0, c][0:1] * dS + R_s[c]
                    - lax.dot_general(K_ref[0, c], dS.astype(BF16), _TN, preferred_element_type=F32))

        lax.fori_loop(0, n, bwd_step, jnp.zeros((HD, HD), F32))

        def grads(c, carry):
            rows = pl.ds(pl.multiple_of(c * CHUNK, CHUNK), CHUNK)
            gbb = gb_ref[rows, :]
            gcol = _rowsum(jnp.where(lane == d * H + h, gbb, 0.0))
            bcol = _rowsum(jnp.where(lane == 2 * H + d * H + h, gbb, 0.0))
            dq, dk, dv, dg, dbeta = _delta_chunk_bwd(q_ref[rows, :], k_ref[rows, :], v_ref[rows, :], gcol, bcol,
                                                     S_ref[0, c], Tm_ref[0, c], do_s[rows, :], dS_s[c], d == 1)
            dqkv_ref[0, rows, :] += dq
            dqkv_ref[1, rows, :] += dk
            dqkv_ref[2, rows, :] += dv
            dgb_ref[rows, :] += (jnp.where(lane == d * H + h, dg, 0.0)
                                 + jnp.where(lane == 2 * H + d * H + h, dbeta, 0.0))
            return carry

        lax.fori_loop(0, n, grads, 0, unroll=_delta_unroll(n))

    col = lambda off: pl.BlockSpec((T, HD), lambda b, h, d: (b, off + h))
    lat = pl.BlockSpec((N, HD), lambda b, h, d: (b, h))
    per = lambda *blk: pl.BlockSpec((1, *blk), lambda b, h, d: (b * H + h, d, 0, 0))
    return pl.pallas_call(
        body, name="delta_bwd", grid=(B, H, 2),
        in_specs=[col(0), col(H), col(2 * H), pl.BlockSpec((T, LANES), lambda b, h, d: (b, 0)), col(3 * H),
                  pl.BlockSpec((1, HD), lambda b, h, d: (0, 0)), lat, lat,
                  per(n, CHUNK, CHUNK), per(n, HD, HD), per(n, HD, HD), per(1, T, HD), per(n, SUBLANES, HD),
                  pl.BlockSpec(memory_space=pl.ANY)],
        out_specs=[pl.BlockSpec((3, T, HD), lambda b, h, d: (0, b, h)), pl.BlockSpec((T, LANES), lambda b, h, d: (b, 0)),
                   col(3 * H), pl.BlockSpec((1, 8, HD), lambda b, h, d: (b, 0, 0))],
        out_shape=[jax.ShapeDtypeStruct((3, B * T, H * HD), F32), jax.ShapeDtypeStruct((B * T, LANES), F32),
                   jax.ShapeDtypeStruct(dp.shape, dp.dtype), jax.ShapeDtypeStruct((B, 8, HD), F32)],
        scratch_shapes=[pltpu.VMEM((T, HD), F32), pltpu.VMEM((n, HD, HD), F32), pltpu.VMEM((n, HD, HD), F32)],
        input_output_aliases={13: 2},
        compiler_params=_params(("parallel", "arbitrary", "arbitrary")),
    )(qkv, qkv, qkv, gb, p, onorm, o, dy, *res, dp)


def _rowwise(fn, ins, out_dtypes, *, name, tm=256, mult=16):
    R, W = ins[0].shape
    tm = _tile(R, tm, mult)

    def body(*refs):
        outs = fn(*[r[...] for r in refs[:len(ins)]])
        for o_ref, o in zip(refs[len(ins):], outs):
            o_ref[...] = o.astype(o_ref.dtype)

    spec = pl.BlockSpec((tm, W), lambda i: (i, 0))
    return pl.pallas_call(
        body, name=name, grid=(R // tm,), in_specs=[spec] * len(ins), out_specs=[spec] * len(out_dtypes),
        out_shape=[jax.ShapeDtypeStruct((R, W), dt) for dt in out_dtypes],
        compiler_params=_params(("parallel",)),
    )(*ins)


def _sum_lead(x, *, name, tm=256, mult=16):
    S, R, W = x.shape
    tm = _tile(R, tm, mult)

    def body(*refs):
        acc = refs[0][0].astype(F32)
        for r in refs[1:S]:
            acc = acc + r[0].astype(F32)
        refs[S][...] = acc

    return pl.pallas_call(
        body, name=name, grid=(R // tm,),
        in_specs=[pl.BlockSpec((1, tm, W), functools.partial(lambda s, i: (s, i, 0), s)) for s in range(S)],
        out_specs=pl.BlockSpec((tm, W), lambda i: (i, 0)),
        out_shape=jax.ShapeDtypeStruct((R, W), F32),
        compiler_params=_params(("parallel",)),
    )(*([x] * S))


def _adamw_math(w, g, m, v):
    m = ADAM_B1 * m + (1.0 - ADAM_B1) * g
    v = ADAM_B2 * v + (1.0 - ADAM_B2) * (g * g)
    m_hat = m / (1.0 - ADAM_B1 ** ADAM_STEP)
    v_hat = v / (1.0 - ADAM_B2 ** ADAM_STEP)
    return -ADAM_LR * (m_hat / (jnp.sqrt(v_hat) + ADAM_EPS) + ADAM_WD * w), m, v


def _adamw(w, g, m, v, *, name):
    tm = max(SUBLANES, (256 * 1024) // w.shape[1] // SUBLANES * SUBLANES)
    return _rowwise(_adamw_math, [w, g, m, v], [F32, F32, F32], name=name, tm=tm, mult=SUBLANES)


def _me():
    return lax.axis_index("x"), lax.axis_index("y"), lax.axis_index("c")


def _allgather_small(v):
    R, W = v.shape

    def body(x_ref, out_ref, send_sems, recv_sems, local_sem):
        x, y, c = _me()
        me, sibling = (x, y, c), (x, y, 1 - c)
        chips = [(1 - x, y), (x, 1 - y), (1 - x, 1 - y)]

        def slot(px, py, pc):
            return out_ref.at[4 * px + 2 * py + pc]

        def copy(k, block, to, src=None):
            return pltpu.make_async_remote_copy(
                src_ref=slot(*block) if src is None else src, dst_ref=slot(*block),
                send_sem=send_sems.at[k], recv_sem=recv_sems.at[k], device_id=to, device_id_type=MESH)

        mine = pltpu.make_async_copy(x_ref, slot(*me), local_sem)
        mine.start()
        first = [copy(0, me, sibling, src=x_ref)]
        first += [copy(1 + j, me, (*chip, c), src=x_ref) for j, chip in enumerate(chips)]
        for cp in first:
            cp.start()
        passed = [copy(4 + j, (*chip, c), sibling) for j, chip in enumerate(chips)]
        for j, chip in enumerate(chips):
            copy(1 + j, (*chip, c), me).wait_recv()
            passed[j].start()
        copy(0, sibling, me).wait_recv()
        for j, chip in enumerate(chips):
            copy(4 + j, (*chip, 1 - c), me).wait_recv()
        for cp in first + passed:
            cp.wait_send()
        mine.wait()

    return pl.pallas_call(
        body, name="allgather_small", out_shape=jax.ShapeDtypeStruct((8, R, W), v.dtype),
        in_specs=[pl.BlockSpec(memory_space=pltpu.VMEM)], out_specs=pl.BlockSpec(memory_space=pltpu.VMEM),
        scratch_shapes=[pltpu.SemaphoreType.DMA((7,)), pltpu.SemaphoreType.DMA((7,)), pltpu.SemaphoreType.DMA],
        compiler_params=_params(),
    )(v)


_ANY = pl.BlockSpec(memory_space=pl.ANY)


def _allgather_halves(shards, *, name):
    nw = len(shards)

    def body(*refs):
        x_refs, out_refs = refs[:nw], refs[nw:2 * nw]
        send_sems, recv_sems, local_sems = refs[2 * nw:]
        x, y, c = _me()
        me, sibling = (x, y, c), (x, y, 1 - c)
        chips = [(1 - x, y), (x, 1 - y), (1 - x, 1 - y)]

        def slot(w, px, py, pc):
            return out_refs[w].at[4 * px + 2 * py + pc]

        def copy(w, k, block, to, src=None):
            return pltpu.make_async_remote_copy(
                src_ref=slot(w, *block) if src is None else src, dst_ref=slot(w, *block),
                send_sem=send_sems.at[w, k], recv_sem=recv_sems.at[w, k], device_id=to, device_id_type=MESH)

        started, local = [], []
        for w in range(nw):
            half = shards[w].shape[0] // 2
            own = x_refs[w].at[pl.ds(c * half, half), :]
            mine = pltpu.make_async_copy(own, slot(w, *me), local_sems.at[w])
            mine.start()
            first = [copy(w, 0, me, sibling, src=own)]
            first += [copy(w, 1 + j, me, (*chip, c), src=own) for j, chip in enumerate(chips)]
            for cp in first:
                cp.start()
            started += first
            local.append(mine)
        for w in range(nw):
            for j, chip in enumerate(chips):
                copy(w, 1 + j, (*chip, c), me).wait_recv()
                fwd = copy(w, 4 + j, (*chip, c), sibling)
                fwd.start()
                started.append(fwd)
        for w in range(nw):
            copy(w, 0, sibling, me).wait_recv()
            for j, chip in enumerate(chips):
                copy(w, 4 + j, (*chip, 1 - c), me).wait_recv()
        for cp in started:
            cp.wait_send()
        for cp in local:
            cp.wait()

    return pl.pallas_call(
        body, name=name,
        out_shape=[jax.ShapeDtypeStruct((8, s.shape[0] // 2, s.shape[1]), s.dtype) for s in shards],
        in_specs=[_ANY] * nw, out_specs=[_ANY] * nw,
        scratch_shapes=[pltpu.SemaphoreType.DMA((nw, 7)), pltpu.SemaphoreType.DMA((nw, 7)), pltpu.SemaphoreType.DMA((nw,))],
        compiler_params=_params(),
    )(*shards)


def _sibling_send_halves(arrs, *, name):
    nw = len(arrs)

    def body(*refs):
        x_refs, out_refs, send_sems, recv_sems = refs[:nw], refs[nw:2 * nw], refs[2 * nw], refs[2 * nw + 1]
        x, y, c = _me()
        cps = []
        for w in range(nw):
            half = arrs[w].shape[1] // 2
            cp = pltpu.make_async_remote_copy(
                src_ref=x_refs[w].at[:, pl.ds((1 - c) * half, half), :], dst_ref=out_refs[w],
                send_sem=send_sems.at[w], recv_sem=recv_sems.at[w], device_id=(x, y, 1 - c), device_id_type=MESH)
            cp.start()
            cps.append(cp)
        for cp in cps:
            cp.wait()

    return pl.pallas_call(
        body, name=name,
        out_shape=[jax.ShapeDtypeStruct((a.shape[0], a.shape[1] // 2, a.shape[2]), a.dtype) for a in arrs],
        in_specs=[_ANY] * nw, out_specs=[_ANY] * nw,
        scratch_shapes=[pltpu.SemaphoreType.DMA((nw,)), pltpu.SemaphoreType.DMA((nw,))],
        compiler_params=_params(),
    )(*arrs)


def _sibling_swap(arrs, *, name):
    nw = len(arrs)

    def body(*refs):
        x_refs, out_refs, send_sems, recv_sems = refs[:nw], refs[nw:2 * nw], refs[2 * nw], refs[2 * nw + 1]
        x, y, c = _me()
        cps = []
        for w in range(nw):
            cp = pltpu.make_async_remote_copy(
                src_ref=x_refs[w], dst_ref=out_refs[w], send_sem=send_sems.at[w], recv_sem=recv_sems.at[w],
                device_id=(x, y, 1 - c), device_id_type=MESH)
            cp.start()
            cps.append(cp)
        for cp in cps:
            cp.wait()

    return pl.pallas_call(
        body, name=name, out_shape=[jax.ShapeDtypeStruct(a.shape, a.dtype) for a in arrs],
        in_specs=[_ANY] * nw, out_specs=[_ANY] * nw,
        scratch_shapes=[pltpu.SemaphoreType.DMA((nw,)), pltpu.SemaphoreType.DMA((nw,))],
        compiler_params=_params(),
    )(*arrs)


def _adamw_halves(w, own, sib, m, v, c_arr, *, name):
    r, cols = w.shape
    h = r // 2
    tm = _tile(h, max(SUBLANES, (192 * 1024) // cols // SUBLANES * SUBLANES), SUBLANES)
    nb = h // tm

    def body(c_ref, w_ref, own_ref, sib_ref, m_ref, v_ref, g_out, d_out, m_out, v_out):
        g = jnp.where(pl.program_id(0) == c_ref[0], own_ref[...], sib_ref[...])
        g_out[...] = g
        d_out[...], m_out[...], v_out[...] = _adamw_math(w_ref[...], g, m_ref[...], v_ref[...])

    full = pl.BlockSpec((tm, cols), lambda hh, i, c_ref: (hh * nb + i, 0))
    half = pl.BlockSpec((tm, cols), lambda hh, i, c_ref: (i, 0))
    return pl.pallas_call(
        body, name=name,
        grid_spec=pltpu.PrefetchScalarGridSpec(num_scalar_prefetch=1, grid=(2, nb),
                                               in_specs=[full, half, half, full, full], out_specs=[full] * 4),
        out_shape=[jax.ShapeDtypeStruct((r, cols), F32)] * 4,
        compiler_params=_params(("parallel", "parallel")),
    )(c_arr, w, own, sib, m, v)


_HBM = pl.BlockSpec(memory_space=pltpu.HBM)
_SEM = pl.BlockSpec(memory_space=pltpu.SEMAPHORE)
_DATAFLOW = pltpu.SideEffectType.DATAFLOW_SIDE_EFFECTING


def _chip_exchange_start(arrs, *, name):
    nw = len(arrs)

    def body(*refs):
        x_refs, land_refs, send_sems, recv_sems = refs[:nw], refs[nw:2 * nw], refs[2 * nw], refs[2 * nw + 1]
        token = refs[-1]
        x, y, c = _me()
        s_me = 2 * x + y
        for w in range(nw):
            for k, (px, py) in enumerate([(1 - x, y), (x, 1 - y), (1 - x, 1 - y)]):
                pltpu.make_async_remote_copy(
                    src_ref=x_refs[w].at[2 * px + py], dst_ref=land_refs[w].at[s_me], send_sem=send_sems.at[3 * w + k],
                    recv_sem=recv_sems.at[3 * w + k], device_id=(px, py, c), device_id_type=MESH).start()
        token[...] = jnp.zeros_like(token)

    hbm = [pltpu.HBM(a.shape, a.dtype) for a in arrs]
    outs = pl.pallas_call(
        body, name=name,
        out_shape=(pltpu.SemaphoreType.DMA((3 * nw,)), pltpu.SemaphoreType.DMA((3 * nw,)), *hbm, *hbm,
                   jax.ShapeDtypeStruct((SUBLANES, LANES), F32)),
        in_specs=[_HBM] * (2 * nw), out_specs=(_SEM, _SEM, *([_HBM] * (2 * nw)), pl.BlockSpec(memory_space=pltpu.VMEM)),
        input_output_aliases={i: 2 + i for i in range(2 * nw)},
        compiler_params=pltpu.CompilerParams(has_side_effects=_DATAFLOW),
    )(*[pltpu.with_memory_space_constraint(a, pltpu.HBM) for a in arrs],
      *[pltpu.with_memory_space_constraint(lax.empty(a.shape, a.dtype), pltpu.HBM) for a in arrs])
    return outs[0], outs[1], list(outs[2:2 + nw]), list(outs[2 + nw:2 + 2 * nw]), outs[-1]


def _allgather_start(shards, *, name):
    nw = len(shards)

    def body(*refs):
        x_refs, land_refs, send_sems, recv_sems = refs[:nw], refs[nw:2 * nw], refs[2 * nw], refs[2 * nw + 1]
        token = refs[-1]
        x, y, c = _me()
        me = 4 * x + 2 * y + c
        for w in range(nw):
            half = shards[w].shape[0] // 2
            own = x_refs[w].at[pl.ds(c * half, half), :]
            for k, to in enumerate([(x, y, 1 - c), (1 - x, y, c), (x, 1 - y, c), (1 - x, 1 - y, c)]):
                pltpu.make_async_remote_copy(
                    src_ref=own, dst_ref=land_refs[w].at[me], send_sem=send_sems.at[4 * w + k],
                    recv_sem=recv_sems.at[4 * w + k], device_id=to, device_id_type=MESH).start()
        token[...] = jnp.zeros_like(token)

    lands = [pltpu.HBM((8, s.shape[0] // 2, s.shape[1]), s.dtype) for s in shards]
    outs = pl.pallas_call(
        body, name=name,
        out_shape=(pltpu.SemaphoreType.DMA((4 * nw,)), pltpu.SemaphoreType.DMA((4 * nw,)),
                   *[pltpu.HBM(s.shape, s.dtype) for s in shards], *lands, jax.ShapeDtypeStruct((SUBLANES, LANES), F32)),
        in_specs=[_HBM] * (2 * nw), out_specs=(_SEM, _SEM, *([_HBM] * (2 * nw)), pl.BlockSpec(memory_space=pltpu.VMEM)),
        input_output_aliases={i: 2 + i for i in range(2 * nw)},
        compiler_params=pltpu.CompilerParams(has_side_effects=_DATAFLOW),
    )(*[pltpu.with_memory_space_constraint(s, pltpu.HBM) for s in shards],
      *[pltpu.with_memory_space_constraint(lax.empty(l.shape, l.dtype), pltpu.HBM) for l in lands])
    return outs[0], outs[1], list(outs[2:2 + nw]), list(outs[2 + nw:2 + 2 * nw]), outs[-1]


def _allgather_wait(send_sems, recv_sems, srcs, lands, after, *, name):
    nw = len(srcs)

    def body(*refs):
        x_refs, land_refs, send_sems, recv_sems = refs[:nw], refs[nw:2 * nw], refs[2 * nw], refs[2 * nw + 1]
        x, y, c = _me()
        for w in range(nw):
            half = srcs[w].shape[0] // 2
            own = x_refs[w].at[pl.ds(c * half, half), :]
            for k, (px, py, pc) in enumerate([(x, y, 1 - c), (1 - x, y, c), (x, 1 - y, c), (1 - x, 1 - y, c)]):
                cp = pltpu.make_async_remote_copy(
                    src_ref=own, dst_ref=land_refs[w].at[4 * px + 2 * py + pc], send_sem=send_sems.at[4 * w + k],
                    recv_sem=recv_sems.at[4 * w + k], device_id=(px, py, pc), device_id_type=MESH)
                cp.wait_send()
                cp.wait_recv()

    outs = pl.pallas_call(
        body, name=name,
        out_shape=(*[pltpu.HBM(a.shape, a.dtype) for a in srcs], *[pltpu.HBM(a.shape, a.dtype) for a in lands]),
        in_specs=[_HBM] * (2 * nw) + [_SEM, _SEM, _ANY], out_specs=tuple([_HBM] * (2 * nw)),
        input_output_aliases={i: i for i in range(2 * nw)},
        compiler_params=pltpu.CompilerParams(has_side_effects=_DATAFLOW),
    )(*srcs, *lands, send_sems, recv_sems, after)
    return list(outs[:nw]), list(outs[nw:])


def _pass_to_sibling(lands, *, name):
    nw = len(lands)

    def body(*refs):
        x_refs, out_refs, send_sems, recv_sems = refs[:nw], refs[nw:2 * nw], refs[2 * nw], refs[2 * nw + 1]
        x, y, c = _me()
        chips = [(1 - x, y), (x, 1 - y), (1 - x, 1 - y)]
        cps = []
        for w in range(nw):
            for k, (px, py) in enumerate(chips):
                cp = pltpu.make_async_remote_copy(
                    src_ref=x_refs[w].at[4 * px + 2 * py + c], dst_ref=out_refs[w].at[4 * px + 2 * py + c],
                    send_sem=send_sems.at[3 * w + k], recv_sem=recv_sems.at[3 * w + k], device_id=(x, y, 1 - c),
                    device_id_type=MESH)
                cp.start()
                cps.append(cp)
        for w in range(nw):
            for k, (px, py) in enumerate(chips):
                pltpu.make_async_remote_copy(
                    src_ref=x_refs[w].at[4 * px + 2 * py + c], dst_ref=out_refs[w].at[4 * px + 2 * py + 1 - c],
                    send_sem=send_sems.at[3 * w + k], recv_sem=recv_sems.at[3 * w + k], device_id=(x, y, 1 - c),
                    device_id_type=MESH).wait_recv()
        for cp in cps:
            cp.wait_send()

    return pl.pallas_call(
        body, name=name, out_shape=[jax.ShapeDtypeStruct(a.shape, a.dtype) for a in lands],
        in_specs=[_ANY] * nw, out_specs=[_ANY] * nw, input_output_aliases={i: i for i in range(nw)},
        scratch_shapes=[pltpu.SemaphoreType.DMA((3 * nw,)), pltpu.SemaphoreType.DMA((3 * nw,))],
        compiler_params=_params(),
    )(*lands)


def _chip_exchange_wait(send_sems, recv_sems, srcs, lands, after, *, name):
    nw = len(srcs)

    def body(*refs):
        x_refs, land_refs, send_sems, recv_sems = refs[:nw], refs[nw:2 * nw], refs[2 * nw], refs[2 * nw + 1]
        x, y, c = _me()
        for w in range(nw):
            for k, (px, py) in enumerate([(1 - x, y), (x, 1 - y), (1 - x, 1 - y)]):
                cp = pltpu.make_async_remote_copy(
                    src_ref=x_refs[w].at[2 * px + py], dst_ref=land_refs[w].at[2 * px + py], send_sem=send_sems.at[3 * w + k],
                    recv_sem=recv_sems.at[3 * w + k], device_id=(px, py, c), device_id_type=MESH)
                cp.wait_send()
                cp.wait_recv()

    hbm = [pltpu.HBM(a.shape, a.dtype) for a in srcs]
    outs = pl.pallas_call(
        body, name=name, out_shape=(*hbm, *hbm),
        in_specs=[_HBM] * (2 * nw) + [_SEM, _SEM, _ANY], out_specs=tuple([_HBM] * (2 * nw)),
        input_output_aliases={i: i for i in range(2 * nw)},
        compiler_params=pltpu.CompilerParams(has_side_effects=_DATAFLOW),
    )(*srcs, *lands, send_sems, recv_sems, after)
    return list(outs[:nw]), list(outs[nw:])


def _sum_slabs(landed, own_src, s_arr, after, *, name, tm=512):
    S, h, w = landed.shape
    tm = _tile(h, tm, 16)

    def body(s_ref, *refs):
        own = refs[S][0].astype(F32)
        acc = None
        for s in range(S):
            term = jnp.where(s_ref[0] == s, own, refs[s][0].astype(F32))
            acc = term if acc is None else acc + term
        refs[S + 2][...] = acc

    def slab(s):
        return pl.BlockSpec((1, tm, w), lambda i, s_ref: (jnp.where(s_ref[0] == s, (s + 1) % S, s), i, 0))

    return pl.pallas_call(
        body, name=name,
        grid_spec=pltpu.PrefetchScalarGridSpec(
            num_scalar_prefetch=1, grid=(h // tm,),
            in_specs=[slab(s) for s in range(S)] + [pl.BlockSpec((1, tm, w), lambda i, s_ref: (s_ref[0], i, 0)), _ANY],
            out_specs=pl.BlockSpec((tm, w), lambda i, s_ref: (i, 0))),
        out_shape=jax.ShapeDtypeStruct((h, w), F32),
        compiler_params=_params(("parallel",)),
    )(s_arr, *([landed] * S), own_src, after)


def _half_add(g, recv, c_arr, *, name):
    S, r, w = g.shape
    h = r // 2
    tm = _tile(h, 512, 16)
    nb = h // tm

    def body(c_ref, g_ref, r_ref, o_ref):
        o_ref[...] = (g_ref[...] + r_ref[...]).astype(BF16)

    return pl.pallas_call(
        body, name=name,
        grid_spec=pltpu.PrefetchScalarGridSpec(
            num_scalar_prefetch=1, grid=(S, nb),
            in_specs=[pl.BlockSpec((1, tm, w), lambda s, i, c_ref: (s, c_ref[0] * nb + i, 0)),
                      pl.BlockSpec((1, tm, w), lambda s, i, c_ref: (s, i, 0))],
            out_specs=pl.BlockSpec((1, tm, w), lambda s, i, c_ref: (s, i, 0))),
        out_shape=jax.ShapeDtypeStruct((S, h, w), BF16),
        compiler_params=_params(("parallel", "parallel")),
    )(c_arr, g, recv)


def _layout(sizes, width, part_mult, total_mult):
    offs, rows, r = [], [], 0
    for n in sizes:
        k = -(-n // width)
        offs.append(r)
        rows.append(k)
        r += -(-k // part_mult) * part_mult
    return offs, rows, -(-r // total_mult) * total_mult


def _pack(arrs, width, part_mult, total_mult, dtype, lead=()):
    nl = len(lead)
    sizes = [math.prod(a.shape[nl:]) for a in arrs]
    offs, rows, total = _layout(sizes, width, part_mult, total_mult)
    parts, r = [], 0
    for a, n, o, k in zip(arrs, sizes, offs, rows):
        kp = -(-k // part_mult) * part_mult
        flat = a.reshape(*lead, n).astype(dtype)
        if kp * width > n:
            flat = jnp.pad(flat, [(0, 0)] * nl + [(0, kp * width - n)])
        parts.append(flat.reshape(*lead, kp, width))
        r = o + kp
    if total > r:
        parts.append(jnp.zeros((*lead, total - r, width), dtype))
    return jnp.concatenate(parts, axis=nl)


def _unpack(pool, shapes, width, part_mult, total_mult):
    lead = pool.shape[:-2]
    sizes = [math.prod(s) for s in shapes]
    offs, rows, _ = _layout(sizes, width, part_mult, total_mult)
    out = []
    for s, n, o, k in zip(shapes, sizes, offs, rows):
        flat = lax.slice_in_dim(pool, o, o + k, axis=len(lead)).reshape(*lead, k * width)
        out.append(lax.slice_in_dim(flat, 0, n, axis=len(lead)).reshape(*lead, *s))
    return out


_WEIGHTS = ("c_ctx", "w_ada", "b_ada", "g_pre_mix", "g_post_mix", "g_pre_ffn", "g_post_ffn", "w_in", "b_merge",
            "dn_conv", "dn_a_log", "dn_dt_bias", "dn_onorm", "lru_conv", "lru_conv_b", "lru_w_rg", "lru_b_rg",
            "lru_w_ig", "lru_b_ig", "lru_lambda", "w_branch_dn", "w_branch_lru", "w_out", "w_up", "ffn_dw",
            "ffn_dw_b", "w_down")
_BIG = {"w_ada": True, "w_in": True, "w_branch_dn": False, "w_branch_lru": False, "w_out": False, "w_up": True,
        "w_down": False}
_SMALL_SHARDED = ("dn_conv", "lru_conv", "lru_b_rg", "lru_b_ig", "lru_lambda", "ffn_dw")
_NCHIP = 4
_FLAT_PART = 8
_FLAT_TOTAL = 256


def _to_chip_shards(g, by_cols):
    if by_cols:
        return g.reshape(g.shape[0], _NCHIP, g.shape[1] // _NCHIP).transpose(1, 0, 2)
    return g.reshape(_NCHIP, g.shape[0] // _NCHIP, g.shape[1])


def _from_chip_shards(s, by_cols):
    if by_cols:
        return s.transpose(1, 0, 2).reshape(s.shape[1], _NCHIP * s.shape[2])
    return s.reshape(_NCHIP * s.shape[1], s.shape[2])


def _dsilu(x):
    s = _sigmoid(x)
    return s * (1.0 + x * (1.0 - s))


def kernel(x, c, ctx, c_ctx, w_ada, b_ada, g_pre_mix, g_post_mix, g_pre_ffn, g_post_ffn, w_in, b_merge, dn_conv, dn_a_log, dn_dt_bias, dn_onorm, lru_conv, lru_conv_b, lru_w_rg, lru_b_rg, lru_w_ig, lru_b_ig, lru_lambda, w_branch_dn, w_branch_lru, w_out, w_up, ffn_dw, ffn_dw_b, w_down, loss_target, m_c_ctx, m_w_ada, m_b_ada, m_g_pre_mix, m_g_post_mix, m_g_pre_ffn, m_g_post_ffn, m_w_in, m_b_merge, m_dn_conv, m_dn_a_log, m_dn_dt_bias, m_dn_onorm, m_lru_conv, m_lru_conv_b, m_lru_w_rg, m_lru_b_rg, m_lru_w_ig, m_lru_b_ig, m_lru_lambda, m_w_branch_dn, m_w_branch_lru, m_w_out, m_w_up, m_ffn_dw, m_ffn_dw_b, m_w_down, v_c_ctx, v_w_ada, v_b_ada, v_g_pre_mix, v_g_post_mix, v_g_pre_ffn, v_g_post_ffn, v_w_in, v_b_merge, v_dn_conv, v_dn_a_log, v_dn_dt_bias, v_dn_onorm, v_lru_conv, v_lru_conv_b, v_lru_w_rg, v_lru_b_rg, v_lru_w_ig, v_lru_b_ig, v_lru_lambda, v_w_branch_dn, v_w_branch_lru, v_w_out, v_w_up, v_ffn_dw, v_ffn_dw_b, v_w_down):
    W = dict(zip(_WEIGHTS, (c_ctx, w_ada, b_ada, g_pre_mix, g_post_mix, g_pre_ffn, g_post_ffn, w_in, b_merge, dn_conv,
                            dn_a_log, dn_dt_bias, dn_onorm, lru_conv, lru_conv_b, lru_w_rg, lru_b_rg, lru_w_ig, lru_b_ig,
                            lru_lambda, w_branch_dn, w_branch_lru, w_out, w_up, ffn_dw, ffn_dw_b, w_down)))
    Mo = dict(zip(_WEIGHTS, (m_c_ctx, m_w_ada, m_b_ada, m_g_pre_mix, m_g_post_mix, m_g_pre_ffn, m_g_post_ffn, m_w_in,
                             m_b_merge, m_dn_conv, m_dn_a_log, m_dn_dt_bias, m_dn_onorm, m_lru_conv, m_lru_conv_b,
                             m_lru_w_rg, m_lru_b_rg, m_lru_w_ig, m_lru_b_ig, m_lru_lambda, m_w_branch_dn,
                             m_w_branch_lru, m_w_out, m_w_up, m_ffn_dw, m_ffn_dw_b, m_w_down)))
    Vo = dict(zip(_WEIGHTS, (v_c_ctx, v_w_ada, v_b_ada, v_g_pre_mix, v_g_post_mix, v_g_pre_ffn, v_g_post_ffn, v_w_in,
                             v_b_merge, v_dn_conv, v_dn_a_log, v_dn_dt_bias, v_dn_onorm, v_lru_conv, v_lru_conv_b,
                             v_lru_w_rg, v_lru_b_rg, v_lru_w_ig, v_lru_b_ig, v_lru_lambda, v_w_branch_dn,
                             v_w_branch_lru, v_w_out, v_w_up, v_ffn_dw, v_ffn_dw_b, v_w_down)))
    B, N, D = x.shape
    NC = ctx.shape[1]
    T = NC + N
    H, HD = dn_a_log.shape[-1], dn_onorm.shape[-1]
    DNW = H * HD
    LW, LBD = lru_conv_b.shape[-1], lru_w_rg.shape[-1]
    DFF = ffn_dw_b.shape[-1]
    LC = LANES
    x_i, y_i, c_i = _me()
    s_me = 2 * x_i + y_i
    tm = _tile(math.gcd(NC, N), 256, 16)

    def whole(n, g):
        r, w_ = W[n].shape[1:]
        return g.reshape(_NCHIP, r, w_) if _BIG[n] else g.reshape(_NCHIP * r, w_)

    first = ("w_ada", "w_in")
    later = tuple(n for n in _BIG if n not in first)
    shard16 = {n: W[n][0].astype(BF16) for n in _BIG}
    full = {n: whole(n, g) for n, g in zip(first, _allgather_halves([shard16[n] for n in first], name="allgather_first"))}

    small_local = [W[n][0].reshape(-1, W[n].shape[-1]) for n in _SMALL_SHARDED]
    small_shapes = [a.shape for a in small_local]
    spack = _pack(small_local, LANES, _FLAT_PART, _FLAT_PART, F32)
    sgath = _allgather_small(spack)[0::2]
    sfull = {n: _from_chip_shards(s, True)
             for n, s in zip(_SMALL_SHARDED, _unpack(sgath, small_shapes, LANES, _FLAT_PART, _FLAT_PART))}

    later16, sgath = lax.optimization_barrier(([shard16[n] for n in later], sgath))
    ag_send, ag_recv, ag_src, ag_land, ag_token = _allgather_start(later16, name="ag_start")

    o_a = 4 * DNW
    o_xl = o_a + 4 * H
    o_mg = o_xl + 2 * LW
    wi_ = _from_chip_shards(full["w_in"], True)
    nj = LW // LC
    lru_cols = jnp.stack([wi_[:, o_xl:o_xl + LW].reshape(D, nj, LC), wi_[:, o_xl + LW:o_mg].reshape(D, nj, LC)],
                         axis=2).reshape(D, 2 * LW)
    wp = jnp.concatenate([wi_[:, :o_a], lru_cols, wi_[:, o_mg:], wi_[:, o_a:o_xl],
                          jnp.zeros((D, LANES - 4 * H), BF16)], axis=1)
    p_lru, p_mg, p_ab = 4 * DNW, 4 * DNW + 2 * LW, 4 * DNW + 2 * LW + 2 * D
    PW = p_ab + LANES

    MR = LANES
    cond = jnp.concatenate([c, c_ctx[None], jnp.zeros((MR - B - 1, D), F32)], axis=0)
    silu_rows = _rowwise(lambda a: (_silu(a),), [cond], [F32], name="cond_silu")[0]
    mod = _matmul(silu_rows, full["w_ada"], b_shards=(0, _NCHIP), name="ada_fwd") + b_ada + ag_token[0, 0]
    mx = mod[:B].reshape(B, 6, D)
    mc = mod[B].reshape(6, D)
    zero = jnp.zeros((B, D), F32)
    tab = jnp.stack([jnp.stack([jnp.broadcast_to(mc[0], (B, D)), jnp.broadcast_to(mc[1], (B, D))] + [zero] * 6, axis=1),
                     jnp.stack([mx[:, 0], mx[:, 1]] + [zero] * 6, axis=1)], axis=1)
    vecs = jnp.stack([mx[:, 2], mx[:, 3], mx[:, 4], mx[:, 5]] + [zero] * 4, axis=1)
    gains = jnp.concatenate([g_post_mix, g_pre_ffn, g_post_ffn, jnp.zeros((5, D), F32)], axis=0)

    h = jnp.concatenate([ctx, x], axis=1)
    u = _premix_fwd(h, g_pre_mix, tab, nc=NC, tm=tm)
    p = _matmul(u, wp, name="in_fwd")
    dkw = dict(B=B, T=T, nc=NC, H=H, HD=HD)
    qkv = _dnprep_fwd(p, sfull["dn_conv"], **dkw)
    prm = jnp.concatenate([
        jnp.concatenate([dn_a_log.reshape(1, 2 * H), jnp.zeros((1, LANES - 2 * H), F32)], axis=1),
        jnp.concatenate([dn_dt_bias.reshape(1, 2 * H), jnp.zeros((1, LANES - 2 * H), F32)], axis=1),
        jnp.zeros((6, LANES), F32)], axis=0)
    gtm = _tile(B * T, 512, 16)
    gb = _gb_fwd(p, prm, rows=B * T, col0=p_ab, H=H, tm=gtm)
    y_dn, o_dn, *dn_res = _delta_fwd(qkv, gb, p, dn_onorm, **dkw)
    lv = jnp.concatenate([lru_conv_b, sfull["lru_b_rg"], sfull["lru_b_ig"], sfull["lru_lambda"], jnp.zeros((1, LW), F32)], axis=0)
    wr = _blockdiag(lru_w_rg[0], LC).astype(BF16)
    wi = _blockdiag(lru_w_ig[0], LC).astype(BF16)
    lkw = dict(B=B, T=T, nc=NC, LW=LW, col0=p_lru, C=LC)
    y_lru = _lru_fwd(p, sfull["lru_conv"], lv, wr, wi, **lkw)
    ag_src, ag_land = _allgather_wait(ag_send, ag_recv, ag_src, ag_land, y_lru, name="ag_wait")
    me_piece = 4 * x_i + 2 * y_i + c_i
    for n, src, land in zip(later, ag_src, _pass_to_sibling(ag_land, name="ag_pass")):
        own = lax.dynamic_slice_in_dim(src, c_i * (src.shape[0] // 2), src.shape[0] // 2, axis=0)
        full[n] = whole(n, lax.dynamic_update_index_in_dim(land, own, me_piece, axis=0))
    Ydn = _matmul(y_dn, full["w_branch_dn"], name="bdn_fwd")
    Ylru = _matmul(y_lru, full["w_branch_lru"], name="blru_fwd")
    mkw = dict(B=B, T=T, nc=NC, D=D, col0=p_mg, tm=tm)
    mixin = _merge_fwd(p, Ydn, Ylru, b_merge, **mkw)
    mix = _matmul(mixin, full["w_out"], name="out_fwd")
    h1, u2 = _post_fwd(x, mix, gains, vecs, tm=tm)
    F = _matmul(u2, full["w_up"], b_shards=(0, _NCHIP), name="up_fwd")
    w9 = sfull["ffn_dw"]
    ftc = _tile(DFF, 256)
    f, f_pre = _ffn_act_fwd(F, w9, ffn_dw_b, B=B, N=N, DFF=DFF, tc=ftc)
    dn = _matmul(f, full["w_down"], name="down_fwd")
    ddn, dout, sums_f = _final(h1, dn, loss_target, gains, vecs, tm=tm)

    G = {}
    df = _matmul(ddn, full["w_down"], tb=True, name="down_bwd_x")
    G["w_down"] = _matmul(f, ddn, ta=True, name="down_bwd_w")
    dFg, dFv, dwb = _ffn_act_bwd(F, f_pre, w9, df, B=B, N=N, DFF=DFF, tc=ftc)
    hs = _NCHIP // 2
    du2 = _matmul(dFg, full["w_up"], tb=True, b_shards=(0, hs), name="up_bwd_xg")
    du2 = _matmul(dFv, full["w_up"], tb=True, b_shards=(hs, hs), add=du2, name="up_bwd_xv")
    gup = _matmul(u2, dFg, ta=True, out_shards=hs, into=(lax.empty((_NCHIP, D, DFF // hs), F32), 0), name="up_bwd_wg")
    G["w_up"] = _matmul(u2, dFv, ta=True, out_shards=hs, into=(gup, hs), name="up_bwd_wv")
    dx1, dmix, sums_p = _post_bwd(x, mix, gains, vecs, dout, du2, tm=tm)
    dmixin = _matmul(dmix, full["w_out"], tb=True, name="out_bwd_x")
    G["w_out"] = _matmul(mixin, dmix, ta=True, name="out_bwd_w")
    dp = jnp.zeros((B * T, PW), BF16)
    dYdn, dYlru, dp, sums_m = _merge_bwd(p, Ydn, Ylru, b_merge, dmixin, dp, **mkw)
    dy_dn = _matmul(dYdn, full["w_branch_dn"], tb=True, name="bdn_bwd_x")
    G["w_branch_dn"] = _matmul(y_dn, dYdn, ta=True, name="bdn_bwd_w")
    dy_lru = _matmul(dYlru, full["w_branch_lru"], tb=True, name="blru_bwd_x")
    G["w_branch_lru"] = _matmul(y_lru, dYlru, ta=True, name="blru_bwd_w")

    c_arr = c_i.astype(jnp.int32).reshape(1)
    s_arr = s_me.astype(jnp.int32).reshape(1)

    def chip_sums(names, tag):
        slabs = [G[n] if _BIG[n] else G[n].reshape(_NCHIP, G[n].shape[0] // _NCHIP, G[n].shape[1]) for n in names]
        from_sibling = _sibling_send_halves(slabs, name="rs_sibling_" + tag)
        return [_half_add(g, r, c_arr, name="rs_add_" + n) for n, g, r in zip(names, slabs, from_sibling)]

    early = tuple(n for n in _BIG if n in G)
    cx_send, cx_recv, cx_src, cx_land, cx_token = _chip_exchange_start(chip_sums(early, "early"), name="cx_start")
    dp, dcw_l, dlv, dwr, dwi = _lru_bwd(p, sfull["lru_conv"], lv + cx_token[0, 0], wr, wi, dy_lru, dp, **lkw)
    dqkv, dgb, dp, don = _delta_bwd(qkv, gb, p, dn_onorm, o_dn, dn_res, dy_dn, dp, **dkw)
    dp, dprm = _gb_bwd(p, prm, dgb, dp, rows=B * T, col0=p_ab, H=H, tm=gtm)
    dp, dcw_d = _dnprep_bwd(p, sfull["dn_conv"], dqkv, dp, **dkw)
    dwp = _matmul(u, dp, ta=True, name="in_bwd_w")
    dlru = dwp[:, p_lru:p_mg].reshape(D, nj, 2, LC)
    G["w_in"] = _to_chip_shards(jnp.concatenate([dwp[:, :o_a], dwp[:, p_ab:p_ab + 4 * H], dlru[:, :, 0].reshape(D, LW),
                                                 dlru[:, :, 1].reshape(D, LW), dwp[:, p_mg:p_ab]], axis=1), True)
    wi_send, wi_recv, wi_src, wi_land, wi_token = _chip_exchange_start(chip_sums(("w_in",), "w_in"), name="cx_in_start")
    dU = _matmul(dp, wp, tb=True, after=wi_token, name="in_bwd_x")
    grad_x, sums_pm = _premix_bwd(h, g_pre_mix, tab, dU, dx1, nc=NC, tm=tm)

    dmod_x = jnp.stack([sums_pm[:, 1, 0], sums_pm[:, 1, 1], sums_p[:, 0], sums_p[:, 1], sums_p[:, 2], sums_f[:, 0]],
                       axis=1).reshape(B, 6 * D)
    dmod_c = jnp.concatenate([sums_pm[:, 0, 0].sum(0), sums_pm[:, 0, 1].sum(0), jnp.zeros((4 * D,), F32)])[None]
    dmod = jnp.concatenate([dmod_x, dmod_c, jnp.zeros((MR - B - 1, 6 * D), F32)], axis=0)
    G["w_ada"] = _matmul(silu_rows, dmod, ta=True, out_shards=_NCHIP, name="ada_bwd_w")
    dsilu = _matmul(dmod, full["w_ada"], tb=True, b_shards=(0, _NCHIP), name="ada_bwd_x")

    g_small = {
        "c_ctx": dsilu[B] * _dsilu(c_ctx),
        "b_ada": dmod[:B + 1].sum(0)[None],
        "g_pre_mix": sums_pm[:, :, 2].sum((0, 1))[None],
        "g_post_mix": sums_p[:, 3].sum(0)[None],
        "g_pre_ffn": sums_p[:, 4].sum(0)[None],
        "g_post_ffn": sums_f[:, 1].sum(0)[None],
        "b_merge": sums_m[0:1],
        "dn_conv": dcw_d[0:4][None],
        "dn_a_log": dprm[0, :2 * H].reshape(1, 2, H),
        "dn_dt_bias": dprm[1, :2 * H].reshape(1, 2, H),
        "dn_onorm": don[:, 0].sum(0)[None],
        "lru_conv": dcw_l[0:4][None],
        "lru_conv_b": dlv[0:1],
        "lru_w_rg": _blockdiag_extract(dwr, LBD)[None],
        "lru_b_rg": dlv[1:3][None],
        "lru_w_ig": _blockdiag_extract(dwi, LBD)[None],
        "lru_b_ig": dlv[3:5][None],
        "lru_lambda": dlv[5:7][None],
        "ffn_dw": dwb[0:9].reshape(1, 3, 3, DFF),
        "ffn_dw_b": dwb[9:10],
    }
    small_names = tuple(n for n in _WEIGHTS if n not in _BIG)
    loss_part = sums_f[:, 2].sum().reshape(1)
    gs_list = [g_small[n] for n in small_names] + [loss_part]
    gs_shapes = [a.shape for a in gs_list]
    gpack = _pack(gs_list, LANES, _FLAT_PART, _FLAT_TOTAL, F32)
    gsum = _sum_lead(_allgather_small(gpack), name="small_sum", tm=512, mult=SUBLANES)
    gs_red = dict(zip(small_names + ("loss",), _unpack(gsum, gs_shapes, LANES, _FLAT_PART, _FLAT_TOTAL)))
    loss = gs_red["loss"][0]

    grads, deltas, new_m, new_v = {}, {}, {}, {}

    def finish(names, lands, srcs, after, tag):
        halves = [_sum_slabs(l, src, s_arr, after, name="rs_sum_" + n) for n, l, src in zip(names, lands, srcs)]
        outs = None
        for n, own, sib in zip(names, halves, _sibling_swap(halves, name="rs_gather_" + tag)):
            shp = W[n].shape
            outs = _adamw_halves(W[n][0], own, sib, Mo[n][0], Vo[n][0], c_arr, name="adamw_" + n)
            grads[n], deltas[n], new_m[n], new_v[n] = (o.reshape(shp) for o in outs)
        return outs[1]

    cx_src, cx_land = _chip_exchange_wait(cx_send, cx_recv, cx_src, cx_land, dsilu, name="cx_wait")
    ada_sums, gsum = lax.optimization_barrier((chip_sums(("w_ada",), "w_ada"), gsum))
    ad_send, ad_recv, ad_src, ad_land, ad_token = _chip_exchange_start(ada_sums, name="cx_ada_start")
    last_early = finish(early, cx_land, cx_src, ad_token, "early")
    wi_src, wi_land = _chip_exchange_wait(wi_send, wi_recv, wi_src, wi_land, last_early, name="cx_in_wait")
    last_in = finish(("w_in",), wi_land, wi_src, last_early, "w_in")
    ad_src, ad_land = _chip_exchange_wait(ad_send, ad_recv, ad_src, ad_land, last_in, name="cx_ada_wait")
    finish(("w_ada",), ad_land, ad_src, last_in, "w_ada")
    for n in small_names:
        g = gs_red[n]
        if n in _SMALL_SHARDED:
            k = W[n].shape[-1]
            g = lax.dynamic_slice_in_dim(g, s_me * k, k, axis=g.ndim - 1)
        grads[n] = g.reshape(W[n].shape)
    sm_shapes = [W[n].shape for n in small_names]
    pk = lambda d: _pack([d[n] for n in small_names], LANES, _FLAT_PART, _FLAT_TOTAL, F32)
    d_, m_, v_ = _adamw(pk(W), pk(grads), pk(Mo), pk(Vo), name="adamw_small")
    for dst, pool_ in ((deltas, d_), (new_m, m_), (new_v, v_)):
        dst.update(zip(small_names, _unpack(pool_, sm_shapes, LANES, _FLAT_PART, _FLAT_TOTAL)))
    return (loss, grad_x, *[grads[n] for n in _WEIGHTS], *[deltas[n] for n in _WEIGHTS],
            *[new_m[n] for n in _WEIGHTS], *[new_v[n] for n in _WEIGHTS])
```

```python
import functools
import math

import jax
import jax.numpy as jnp
from jax import lax
from jax.experimental import pallas as pl
from jax.experimental.pallas import tpu as pltpu

F32 = jnp.float32
BF16 = jnp.bfloat16
EPS = 1e-6
GRID_W = 64
CHUNK = 256
LRU_C = 8.0
LANES = 128
SUBLANES = 8
VMEM_LIMIT = 56 * 1024 * 1024
ADAM_LR, ADAM_B1, ADAM_B2, ADAM_EPS, ADAM_WD, ADAM_STEP = 0.001, 0.9, 0.999, 1e-08, 0.01, 10
MESH = pl.DeviceIdType.MESH


def _tile(n, target, mult=LANES):
    best = None
    for t in range(mult, min(n, target) + 1, mult):
        if n % t == 0:
            best = t
    return best if best is not None else n


def _params(sem=None, **kw):
    return pltpu.CompilerParams(dimension_semantics=sem, vmem_limit_bytes=VMEM_LIMIT, **kw)


def _sigmoid(x):
    return 1.0 / (1.0 + jnp.exp(-x))


def _silu(x):
    return x * _sigmoid(x)


def _softplus(x):
    return jnp.maximum(x, 0.0) + jnp.log(1.0 + jnp.exp(-jnp.abs(x)))


def _gelu(x):
    return 0.5 * x * (1.0 + jnp.tanh(math.sqrt(2.0 / math.pi) * (x + 0.044715 * x * x * x)))


def _rmsn(u, gain):
    return u * lax.rsqrt(jnp.mean(u * u, axis=-1, keepdims=True) + EPS) * gain


_MM_VMEM = 40 * 1024 * 1024


_ANY_SPEC = pl.BlockSpec(memory_space=pl.ANY)


def _matmul(a, b, *, ta=False, tb=False, add=None, b_shards=None, out_shards=None, into=None, after=None,
            out_dtype=F32, name, tm=1024, tn=2048, tk=2048):
    (K, M) = a.shape if ta else a.shape[::-1]
    if b_shards is not None:
        s0, ns = b_shards
        bsh = (b.shape[1], ns * b.shape[2])
        nsh = b.shape[2]
    else:
        bsh = b.shape
    N = bsh[0] if tb else bsh[1]
    assert (bsh[1] if tb else bsh[0]) == K, (a.shape, b.shape, ta, tb)
    tm = _tile(M, tm)
    tk = _tile(nsh if (b_shards is not None and tb) else K, tk)
    nlim = nsh if (b_shards is not None and not tb) else (N // out_shards if out_shards else N)
    osz = jnp.dtype(out_dtype).itemsize + (4 if add is not None else 0)
    while True:
        tn_ = _tile(nlim, tn)
        need = 2 * (tm * tk * a.dtype.itemsize + tk * tn_ * b.dtype.itemsize + tm * tn_ * osz) + 4 * tm * tn_
        if need <= _MM_VMEM or tn <= LANES:
            break
        tn //= 2
    tn = tn_
    nk = K // tk
    dims = (((0 if ta else 1,), (1 if tb else 0,)), ((), ()))

    def body(a_ref, b_ref, *rest):
        c_ref = rest[0] if add is not None else None
        o_ref, acc_ref = rest[-2:]
        k = pl.program_id(2)

        @pl.when(k == 0)
        def _():
            acc_ref[...] = jnp.zeros_like(acc_ref) if c_ref is None else c_ref[...]

        bv = b_ref[0] if b_shards is not None else b_ref[...]
        acc_ref[...] += lax.dot_general(a_ref[...].astype(BF16), bv.astype(BF16), dims, preferred_element_type=F32)

        @pl.when(k == nk - 1)
        def _():
            if out_shards:
                o_ref[0] = acc_ref[...].astype(out_dtype)
            else:
                o_ref[...] = acc_ref[...].astype(out_dtype)

    a_spec = pl.BlockSpec((tk, tm), lambda i, j, k: (k, i)) if ta else pl.BlockSpec((tm, tk), lambda i, j, k: (i, k))
    if b_shards is None:
        b_spec = pl.BlockSpec((tn, tk), lambda i, j, k: (j, k)) if tb else pl.BlockSpec((tk, tn), lambda i, j, k: (k, j))
    elif tb:
        per = nsh // tk
        b_spec = pl.BlockSpec((1, tn, tk), lambda i, j, k: (s0 + k // per, j, k % per))
    else:
        per = nsh // tn
        b_spec = pl.BlockSpec((1, tk, tn), lambda i, j, k: (s0 + j // per, k, j % per))
    o_spec = pl.BlockSpec((tm, tn), lambda i, j, k: (i, j))
    extra, alias = (), {}
    if out_shards:
        oper = N // out_shards // tn
        o0 = 0
        out_shape = jax.ShapeDtypeStruct((out_shards, M, N // out_shards), out_dtype)
        if into is not None:
            buf, o0 = into
            out_shape = jax.ShapeDtypeStruct(buf.shape, buf.dtype)
            extra, alias = (buf,), {2 + (add is not None): 0}
        out_spec = pl.BlockSpec((1, tm, tn), lambda i, j, k: (o0 + j // oper, i, j % oper))
    else:
        out_spec, out_shape = o_spec, jax.ShapeDtypeStruct((M, N), out_dtype)
    if after is not None:
        extra = extra + (after,)
    return pl.pallas_call(
        body, name=name, grid=(M // tm, N // tn, nk),
        in_specs=[a_spec, b_spec] + ([o_spec] if add is not None else []) + [_ANY_SPEC] * len(extra),
        out_specs=out_spec, out_shape=out_shape, input_output_aliases=alias,
        scratch_shapes=[pltpu.VMEM((tm, tn), F32)],
        compiler_params=_params(("parallel", "parallel", "arbitrary")),
    )(*((a, b) + ((add,) if add is not None else ()) + extra))


def _premix_math(h, gain, shift, scale):
    return _rmsn(h, gain) * (1.0 + scale) + shift


def _stream_specs(nct, tm, D):
    return [pl.BlockSpec((1, tm, D), lambda b, t: (b, jnp.minimum(t, nct - 1), 0)),
            pl.BlockSpec((1, tm, D), lambda b, t: (b, jnp.maximum(t - nct, 0), 0))]


def _premix_fwd(ctx, x, gain, tab, *, tm):
    B, nc, D = ctx.shape
    T = nc + x.shape[1]
    nt, nct = T // tm, nc // tm

    def body(c_ref, x_ref, g_ref, tab_ref, u_ref):
        tabv = tab_ref[0, 0]
        h = jnp.where(pl.program_id(1) < nct, c_ref[0], x_ref[0])
        u_ref[...] = _premix_math(h, g_ref[...], tabv[0:1], tabv[1:2]).astype(BF16)

    return pl.pallas_call(
        body, name="premix_fwd", grid=(B, nt),
        in_specs=_stream_specs(nct, tm, D) + [
            pl.BlockSpec((1, D), lambda b, t: (0, 0)),
            pl.BlockSpec((1, 1, 8, D), lambda b, t: (b, jnp.where(t < nct, 0, 1), 0, 0))],
        out_specs=pl.BlockSpec((tm, D), lambda b, t: (b * nt + t, 0)),
        out_shape=jax.ShapeDtypeStruct((B * T, D), BF16),
        compiler_params=_params(("parallel", "parallel")),
    )(ctx, x, gain, tab)


def _premix_bwd(ctx, x, gain, tab, du, dres, *, tm):
    B, nc, D = ctx.shape
    N = x.shape[1]
    T = nc + N
    nt, nct = T // tm, nc // tm

    def body(c_ref, x_ref, g_ref, tab_ref, du_ref, dres_ref, dx_ref, sums_ref):
        t = pl.program_id(1)
        tabv = tab_ref[0, 0]
        h = jnp.where(t < nct, c_ref[0], x_ref[0])
        _, vjp = jax.vjp(_premix_math, h, g_ref[...], tabv[0:1], tabv[1:2])
        dh, dgain, dshift, dscale = vjp(du_ref[...].astype(F32))

        @pl.when((t == 0) | (t == nct))
        def _():
            sums_ref[...] = jnp.zeros_like(sums_ref)

        sums_ref[0, 0, 0:1, :] += dshift
        sums_ref[0, 0, 1:2, :] += dscale
        sums_ref[0, 0, 2:3, :] += dgain

        @pl.when(t >= nct)
        def _():
            dx_ref[0] = dres_ref[...] + dh

    lat = lambda b, t: jnp.maximum(t - nct, 0)
    return pl.pallas_call(
        body, name="premix_bwd", grid=(B, nt),
        in_specs=_stream_specs(nct, tm, D) + [
            pl.BlockSpec((1, D), lambda b, t: (0, 0)),
            pl.BlockSpec((1, 1, 8, D), lambda b, t: (b, jnp.where(t < nct, 0, 1), 0, 0)),
            pl.BlockSpec((tm, D), lambda b, t: (b * nt + t, 0)),
            pl.BlockSpec((tm, D), lambda b, t: (b * (nt - nct) + lat(b, t), 0))],
        out_specs=[pl.BlockSpec((1, tm, D), lambda b, t: (b, lat(b, t), 0)),
                   pl.BlockSpec((1, 1, 8, D), lambda b, t: (b, jnp.where(t < nct, 0, 1), 0, 0))],
        out_shape=[jax.ShapeDtypeStruct((B, N, D), F32), jax.ShapeDtypeStruct((B, 2, 8, D), F32)],
        compiler_params=_params(("parallel", "arbitrary")),
    )(ctx, x, gain, tab, du, dres)


def _merge_math(mgd, mgl, yd, yl, bd, bl):
    return _sigmoid(mgd + bd) * yd + _sigmoid(mgl + bl) * yl


def _merge_fwd(p, ydn, ylru, b_merge, *, B, T, nc, D, col0, tm):
    N = T - nc
    ntl, nt, nct, cb = N // tm, T // tm, nc // tm, col0 // D

    def body(mgd_ref, mgl_ref, yd_ref, yl_ref, bm_ref, o_ref):
        o_ref[...] = _merge_math(mgd_ref[...], mgl_ref[...], yd_ref[...], yl_ref[...],
                                 bm_ref[:, 0:D], bm_ref[:, D:2 * D]).astype(BF16)

    prow = lambda b, t: b * nt + nct + t
    return pl.pallas_call(
        body, name="merge_fwd", grid=(B, ntl),
        in_specs=[pl.BlockSpec((tm, D), lambda b, t: (prow(b, t), cb)),
                  pl.BlockSpec((tm, D), lambda b, t: (prow(b, t), cb + 1)),
                  pl.BlockSpec((tm, D), lambda b, t: (b * ntl + t, 0)),
                  pl.BlockSpec((tm, D), lambda b, t: (b * ntl + t, 0)),
                  pl.BlockSpec((1, 2 * D), lambda b, t: (0, 0))],
        out_specs=pl.BlockSpec((tm, D), lambda b, t: (b * ntl + t, 0)),
        out_shape=jax.ShapeDtypeStruct((B * N, D), BF16),
        compiler_params=_params(("parallel", "parallel")),
    )(p, p, ydn, ylru, b_merge)


def _merge_bwd(p, ydn, ylru, b_merge, dmix, dp, *, B, T, nc, D, col0, tm):
    N = T - nc
    ntl, nt, nct, cb = N // tm, T // tm, nc // tm, col0 // D
    assert col0 % (2 * D) == 0

    def body(mgd_ref, mgl_ref, yd_ref, yl_ref, bm_ref, dm_ref, dp_any, dyd_ref, dyl_ref, dp_ref, sums_ref):
        _, vjp = jax.vjp(_merge_math, mgd_ref[...], mgl_ref[...], yd_ref[...], yl_ref[...],
                         bm_ref[:, 0:D], bm_ref[:, D:2 * D])
        dmgd, dmgl, dyd, dyl, dbd, dbl = vjp(dm_ref[...])
        dyd_ref[...] = dyd.astype(BF16)
        dyl_ref[...] = dyl.astype(BF16)
        dp_ref[:, 0:D] = dmgd.astype(BF16)
        dp_ref[:, D:2 * D] = dmgl.astype(BF16)

        @pl.when((pl.program_id(0) == 0) & (pl.program_id(1) == 0))
        def _():
            sums_ref[...] = jnp.zeros_like(sums_ref)

        sums_ref[0:1, 0:D] += dbd
        sums_ref[0:1, D:2 * D] += dbl

    prow = lambda b, t: b * nt + nct + t
    row = pl.BlockSpec((tm, D), lambda b, t: (b * ntl + t, 0))
    return pl.pallas_call(
        body, name="merge_bwd", grid=(B, ntl),
        in_specs=[pl.BlockSpec((tm, D), lambda b, t: (prow(b, t), cb)),
                  pl.BlockSpec((tm, D), lambda b, t: (prow(b, t), cb + 1)),
                  row, row, pl.BlockSpec((1, 2 * D), lambda b, t: (0, 0)), row,
                  pl.BlockSpec(memory_space=pl.ANY)],
        out_specs=[row, row,
                   pl.BlockSpec((tm, 2 * D), lambda b, t: (prow(b, t), cb // 2)),
                   pl.BlockSpec((8, 2 * D), lambda b, t: (0, 0))],
        out_shape=[jax.ShapeDtypeStruct((B * N, D), BF16), jax.ShapeDtypeStruct((B * N, D), BF16),
                   jax.ShapeDtypeStruct(dp.shape, dp.dtype), jax.ShapeDtypeStruct((8, 2 * D), F32)],
        input_output_aliases={6: 2},
        compiler_params=_params(("arbitrary", "arbitrary")),
    )(p, p, ydn, ylru, b_merge, dmix, dp)


def _post_math(x, mix, g1, gate, g2, sh, sc):
    h1 = x + _rmsn(mix, g1) * gate
    return h1, _rmsn(h1, g2) * (1.0 + sc) + sh


def _post_fwd(x, mix, gains, vecs, *, tm):
    B, N, D = x.shape
    ntl = N // tm

    def body(x_ref, mix_ref, g_ref, v_ref, h1_ref, u2_ref):
        v = v_ref[0]
        h1, u2 = _post_math(x_ref[0], mix_ref[...], g_ref[0:1], v[0:1], g_ref[1:2], v[1:2], v[2:3])
        h1_ref[...] = h1
        u2_ref[...] = u2.astype(BF16)

    row = pl.BlockSpec((tm, D), lambda b, t: (b * ntl + t, 0))
    return pl.pallas_call(
        body, name="post_fwd", grid=(B, ntl),
        in_specs=[pl.BlockSpec((1, tm, D), lambda b, t: (b, t, 0)), row,
                  pl.BlockSpec((8, D), lambda b, t: (0, 0)), pl.BlockSpec((1, 8, D), lambda b, t: (b, 0, 0))],
        out_specs=[row, row],
        out_shape=[jax.ShapeDtypeStruct((B * N, D), F32), jax.ShapeDtypeStruct((B * N, D), BF16)],
        compiler_params=_params(("parallel", "parallel")),
    )(x, mix, gains, vecs)


def _post_bwd(x, mix, gains, vecs, dh1, du2, *, tm):
    B, N, D = x.shape
    ntl = N // tm

    def body(x_ref, mix_ref, g_ref, v_ref, dh1_ref, du2_ref, dx_ref, dmix_ref, sums_ref):
        v = v_ref[0]
        _, vjp = jax.vjp(_post_math, x_ref[0], mix_ref[...], g_ref[0:1], v[0:1], g_ref[1:2], v[1:2], v[2:3])
        dx, dmix, dg1, dgate, dg2, dsh, dsc = vjp((dh1_ref[...], du2_ref[...]))
        dx_ref[...] = dx
        dmix_ref[...] = dmix.astype(BF16)

        @pl.when(pl.program_id(1) == 0)
        def _():
            sums_ref[...] = jnp.zeros_like(sums_ref)

        sums_ref[0, 0:1, :] += dgate
        sums_ref[0, 1:2, :] += dsh
        sums_ref[0, 2:3, :] += dsc
        sums_ref[0, 3:4, :] += dg1
        sums_ref[0, 4:5, :] += dg2

    row = pl.BlockSpec((tm, D), lambda b, t: (b * ntl + t, 0))
    return pl.pallas_call(
        body, name="post_bwd", grid=(B, ntl),
        in_specs=[pl.BlockSpec((1, tm, D), lambda b, t: (b, t, 0)), row,
                  pl.BlockSpec((8, D), lambda b, t: (0, 0)), pl.BlockSpec((1, 8, D), lambda b, t: (b, 0, 0)), row, row],
        out_specs=[row, row, pl.BlockSpec((1, 8, D), lambda b, t: (b, 0, 0))],
        out_shape=[jax.ShapeDtypeStruct((B * N, D), F32), jax.ShapeDtypeStruct((B * N, D), BF16),
                   jax.ShapeDtypeStruct((B, 8, D), F32)],
        compiler_params=_params(("parallel", "arbitrary")),
    )(x, mix, gains, vecs, dh1, du2)


def _final_math(dn, g4, gate5):
    return _rmsn(dn, g4) * gate5


def _final(h1, dn, target, gains, vecs, *, tm):
    B, N, D = target.shape
    ntl = N // tm

    def body(h1_ref, dn_ref, t_ref, g_ref, v_ref, ddn_ref, dout_ref, sums_ref):
        v = v_ref[0]
        y, vjp = jax.vjp(_final_math, dn_ref[...], g_ref[2:3], v[3:4])
        err = h1_ref[...] + y - t_ref[0]
        dout = err * (1.0 / D)
        ddn, dg4, dgate5 = vjp(dout)
        ddn_ref[...] = ddn.astype(BF16)
        dout_ref[...] = dout

        @pl.when(pl.program_id(1) == 0)
        def _():
            sums_ref[...] = jnp.zeros_like(sums_ref)

        sums_ref[0, 0:1, :] += dgate5
        sums_ref[0, 1:2, :] += dg4
        sums_ref[0, 2:3, :] += jnp.sum(err * err, axis=0, keepdims=True) * (0.5 / D)

    row = pl.BlockSpec((tm, D), lambda b, t: (b * ntl + t, 0))
    return pl.pallas_call(
        body, name="final", grid=(B, ntl),
        in_specs=[row, row, pl.BlockSpec((1, tm, D), lambda b, t: (b, t, 0)),
                  pl.BlockSpec((8, D), lambda b, t: (0, 0)), pl.BlockSpec((1, 8, D), lambda b, t: (b, 0, 0))],
        out_specs=[row, row, pl.BlockSpec((1, 8, D), lambda b, t: (b, 0, 0))],
        out_shape=[jax.ShapeDtypeStruct((B * N, D), BF16), jax.ShapeDtypeStruct((B * N, D), F32),
                   jax.ShapeDtypeStruct((B, 8, D), F32)],
        compiler_params=_params(("parallel", "arbitrary")),
    )(h1, dn, target, gains, vecs)


def _shift(x, s):
    s = s % x.shape[0]
    return x if s == 0 else pltpu.roll(x, s, 0)


def _seg_taps(T, nc, width, pad_left):
    t = lax.broadcasted_iota(jnp.int32, (T, 1), 0)
    pos = jnp.where(t < nc, t, t - nc)
    seg = jnp.where(t < nc, nc, T - nc)
    taps = []
    for k in range(width):
        src = pos + (k - pad_left)
        taps.append((pad_left - k, (src >= 0) & (src < seg)))
    return taps


def _grid_taps(N):
    t = lax.broadcasted_iota(jnp.int32, (N, 1), 0)
    wcol = t % GRID_W
    taps = []
    for dr in (-1, 0, 1):
        for dw in (-1, 0, 1):
            off = dr * GRID_W + dw
            ok = (wcol + dw >= 0) & (wcol + dw < GRID_W) & (t + dr * GRID_W >= 0) & (t + dr * GRID_W < N)
            taps.append((-off, ok))
    return taps


def _conv_fwd(x, w, taps):
    y = jnp.zeros_like(x)
    for k, (s, m) in enumerate(taps):
        y = y + w[k:k + 1] * jnp.where(m, _shift(x, s), 0.0)
    return y


def _conv_bwd(x, w, taps, dy):
    dx = jnp.zeros_like(x)
    dws = []
    for k, (s, m) in enumerate(taps):
        dym = jnp.where(m, dy, 0.0)
        dx = dx + w[k:k + 1] * _shift(dym, -s)
        dws.append(jnp.sum(dym * _shift(x, s), axis=0, keepdims=True))
    return dx, jnp.concatenate(dws, axis=0)


def _ffn_act_fwd(F, w9, bias, *, B, N, DFF, tc):
    nj = DFF // tc

    def body(fg_ref, fv_ref, w_ref, b_ref, o_ref, pre_ref):
        fg = _conv_fwd(fg_ref[...], w_ref[...], _grid_taps(N)) + b_ref[...]
        pre_ref[...] = fg
        o_ref[...] = (_gelu(fg) * fv_ref[...]).astype(BF16)

    col = pl.BlockSpec((N, tc), lambda b, j: (b, j))
    return pl.pallas_call(
        body, name="ffn_act_fwd", grid=(B, nj),
        in_specs=[col, pl.BlockSpec((N, tc), lambda b, j: (b, nj + j)),
                  pl.BlockSpec((9, tc), lambda b, j: (0, j)), pl.BlockSpec((1, tc), lambda b, j: (0, j))],
        out_specs=[col, col],
        out_shape=[jax.ShapeDtypeStruct((B * N, DFF), BF16), jax.ShapeDtypeStruct((B * N, DFF), F32)],
        compiler_params=_params(("parallel", "parallel")),
    )(F, F, w9, bias)


def _ffn_act_bwd(F, pre, w9, df, *, B, N, DFF, tc):
    nj = DFF // tc

    def body(fg_ref, fv_ref, w_ref, pre_ref, df_ref, dfg_ref, dfv_ref, dwb_ref):
        taps = _grid_taps(N)
        x = fg_ref[...]
        fg, vjp = jax.vjp(lambda a: _gelu(a), pre_ref[...])
        dfl = df_ref[...]
        dfv_ref[...] = (dfl * fg).astype(BF16)
        (dpre,) = vjp(dfl * fv_ref[...])
        dx, dw = _conv_bwd(x, w_ref[...], taps, dpre)
        dfg_ref[...] = dx.astype(BF16)

        @pl.when(pl.program_id(1) == 0)
        def _():
            dwb_ref[...] = jnp.zeros_like(dwb_ref)

        dwb_ref[0:9, :] += dw
        dwb_ref[9:10, :] += jnp.sum(dpre, axis=0, keepdims=True)

    col = pl.BlockSpec((N, tc), lambda j, b: (b, j))
    return pl.pallas_call(
        body, name="ffn_act_bwd", grid=(nj, B),
        in_specs=[col, pl.BlockSpec((N, tc), lambda j, b: (b, nj + j)), pl.BlockSpec((9, tc), lambda j, b: (0, j)), col, col],
        out_specs=[col, col, pl.BlockSpec((16, tc), lambda j, b: (0, j))],
        out_shape=[jax.ShapeDtypeStruct((B * N, DFF), BF16), jax.ShapeDtypeStruct((B * N, DFF), BF16),
                   jax.ShapeDtypeStruct((16, DFF), F32)],
        compiler_params=_params(("parallel", "arbitrary")),
    )(F, F, w9, pre, df)


def _dnprep_math(y, is_qk, scale):
    s = _silu(y)
    n = s * lax.rsqrt(jnp.sum(s * s, axis=-1, keepdims=True) + EPS) * scale
    return jnp.where(is_qk, n, s)


def _dnprep_fwd(p, cw, *, B, T, nc, H, HD):
    def body(x_ref, w_ref, o_ref):
        j = pl.program_id(1)
        y = _conv_fwd(x_ref[...], w_ref[...], _seg_taps(T, nc, 4, 2))
        o_ref[...] = _dnprep_math(y, j < 2 * H, jnp.where(j < H, HD ** -0.5, 1.0))

    return pl.pallas_call(
        body, name="dnprep_fwd", grid=(B, 3 * H),
        in_specs=[pl.BlockSpec((T, HD), lambda b, j: (b, j)), pl.BlockSpec((4, HD), lambda b, j: (0, j))],
        out_specs=pl.BlockSpec((T, HD), lambda b, j: (b, j)),
        out_shape=jax.ShapeDtypeStruct((B * T, 3 * H * HD), F32),
        compiler_params=_params(("parallel", "parallel")),
    )(p, cw)


def _dnprep_bwd(p, cw, dqkv, dp, *, B, T, nc, H, HD):
    def body(x_ref, w_ref, d_ref, dp_any, dp_ref, dcw_ref):
        j = pl.program_id(0)
        taps = _seg_taps(T, nc, 4, 2)
        x = x_ref[...]
        y = _conv_fwd(x, w_ref[...], taps)
        is_qk, scale = j < 2 * H, jnp.where(j < H, HD ** -0.5, 1.0)
        _, vjp = jax.vjp(lambda a: _dnprep_math(a, is_qk, scale), y)
        (dy,) = vjp(d_ref[0])
        dx, dw = _conv_bwd(x, w_ref[...], taps, dy)
        dp_ref[...] = dx.astype(BF16)

        @pl.when(pl.program_id(1) == 0)
        def _():
            dcw_ref[...] = jnp.zeros_like(dcw_ref)

        dcw_ref[0:4, :] += dw

    col = pl.BlockSpec((T, HD), lambda j, b: (b, j))
    return pl.pallas_call(
        body, name="dnprep_bwd", grid=(3 * H, B),
        in_specs=[col, pl.BlockSpec((4, HD), lambda j, b: (0, j)),
                  pl.BlockSpec((1, T, HD), lambda j, b: (j // H, b, j % H)), pl.BlockSpec(memory_space=pl.ANY)],
        out_specs=[col, pl.BlockSpec((8, HD), lambda j, b: (0, j))],
        out_shape=[jax.ShapeDtypeStruct(dp.shape, dp.dtype), jax.ShapeDtypeStruct((8, 3 * H * HD), F32)],
        input_output_aliases={3: 0},
        compiler_params=_params(("parallel", "arbitrary")),
    )(p, cw, dqkv, dp)


def _gb_math(ab, alog, dtb, H):
    lane = lax.broadcasted_iota(jnp.int32, ab.shape, 1)
    g = -jnp.exp(alog) * _softplus(ab + dtb)
    return jnp.where(lane < 2 * H, g, jnp.where(lane < 4 * H, _sigmoid(ab), 0.0))


def _gb_fwd(p, prm, *, rows, col0, H, tm):
    def body(x_ref, prm_ref, o_ref):
        o_ref[...] = _gb_math(x_ref[...], prm_ref[0:1], prm_ref[1:2], H)

    return pl.pallas_call(
        body, name="gb_fwd", grid=(rows // tm,),
        in_specs=[pl.BlockSpec((tm, LANES), lambda t: (t, col0 // LANES)), pl.BlockSpec((8, LANES), lambda t: (0, 0))],
        out_specs=pl.BlockSpec((tm, LANES), lambda t: (t, 0)),
        out_shape=jax.ShapeDtypeStruct((rows, LANES), F32),
        compiler_params=_params(("parallel",)),
    )(p, prm)


def _gb_bwd(p, prm, dgb, dp, *, rows, col0, H, tm):
    def body(x_ref, prm_ref, d_ref, dp_any, dp_ref, dprm_ref):
        _, vjp = jax.vjp(lambda a, b, c: _gb_math(a, b, c, H), x_ref[...], prm_ref[0:1], prm_ref[1:2])
        dab, dalog, ddtb = vjp(d_ref[...])
        dp_ref[...] = dab.astype(BF16)

        @pl.when(pl.program_id(0) == 0)
        def _():
            dprm_ref[...] = jnp.zeros_like(dprm_ref)

        dprm_ref[0:1, :] += dalog
        dprm_ref[1:2, :] += ddtb

    blk = pl.BlockSpec((tm, LANES), lambda t: (t, col0 // LANES))
    return pl.pallas_call(
        body, name="gb_bwd", grid=(rows // tm,),
        in_specs=[blk, pl.BlockSpec((8, LANES), lambda t: (0, 0)), pl.BlockSpec((tm, LANES), lambda t: (t, 0)),
                  pl.BlockSpec(memory_space=pl.ANY)],
        out_specs=[blk, pl.BlockSpec((8, LANES), lambda t: (0, 0))],
        out_shape=[jax.ShapeDtypeStruct(dp.shape, dp.dtype), jax.ShapeDtypeStruct((8, LANES), F32)],
        input_output_aliases={3: 0},
        compiler_params=_params(("arbitrary",)),
    )(p, prm, dgb, dp)


def _lru_scans(scans):
    C = scans[0][0].shape[1]
    row = lax.broadcasted_iota(jnp.int32, (SUBLANES, C), 0)
    carries = tuple(jnp.zeros((1, C), F32) for _ in scans)
    for si in range(len(scans[0][4])):
        rows = scans[0][4][si][1]
        assert all(sc[4][si][1] == rows for sc in scans)
        sub = max(s for s in (4, 2, 1) if rows % (s * SUBLANES) == 0)
        span = sub * SUBLANES
        nb = rows // span

        def blk(i, carries, si=si, nb=nb, sub=sub, span=span):
            out = []
            for (a_ref, b_ref, h_ref, hp_ref, segs), carry in zip(scans, carries):
                start, _, reverse = segs[si]
                r0 = pl.multiple_of(start + (nb - 1 - i if reverse else i) * span, span)
                local = []
                for j in range(sub):
                    A = a_ref[pl.ds(r0 + j * SUBLANES, SUBLANES), :]
                    Bv = b_ref[pl.ds(r0 + j * SUBLANES, SUBLANES), :]
                    for s in (1, 2, 4):
                        sh = SUBLANES - s if reverse else s
                        m = (row < SUBLANES - s) if reverse else (row >= s)
                        Bv = jnp.where(m, A * pltpu.roll(Bv, sh, 0) + Bv, Bv)
                        A = jnp.where(m, A * pltpu.roll(A, sh, 0), A)
                    local.append((A, Bv))
                for j in (reversed(range(sub)) if reverse else range(sub)):
                    A, Bv = local[j]
                    Hv = Bv + A * carry
                    h_ref[pl.ds(r0 + j * SUBLANES, SUBLANES), :] = Hv
                    if hp_ref is not None:
                        if reverse:
                            hp = jnp.where(row < SUBLANES - 1, pltpu.roll(Hv, SUBLANES - 1, 0), carry)
                        else:
                            hp = jnp.where(row >= 1, pltpu.roll(Hv, 1, 0), carry)
                        hp_ref[pl.ds(r0 + j * SUBLANES, SUBLANES), :] = hp
                    carry = Hv[0:1] if reverse else Hv[SUBLANES - 1:SUBLANES]
                out.append(carry)
            return tuple(out)

        carries = lax.fori_loop(0, nb, blk, carries)


def _lru_orders(T, nc, d):
    N = T - nc
    if d == 0:
        return [(0, nc, False), (nc, N, False)], [(nc, N, True), (0, nc, True)]
    return [(0, nc, True), (nc, N, True)], [(nc, N, False), (0, nc, False)]


def _bdot(a, b, dims=(((1,), (0,)), ((), ()))):
    return lax.dot_general(a.astype(BF16), b.astype(BF16), dims, preferred_element_type=F32)


_NT = (((1,), (1,)), ((), ()))
_TN = (((0,), (0,)), ((), ()))


def _blockdiag(w, C):
    nd, nb, bd, _ = w.shape
    per = C // bd
    out = jnp.einsum('dnpij,pq->dnpiqj', w.reshape(nd, nb // per, per, bd, bd), jnp.eye(per, dtype=w.dtype))
    return out.reshape(nd, nb // per, C, C)


def _blockdiag_extract(dw, bd):
    nd, nj, C, _ = dw.shape
    per = C // bd
    out = jnp.einsum('dnpiqj,pq->dnpij', dw.reshape(nd, nj, per, bd, per, bd), jnp.eye(per, dtype=dw.dtype))
    return out.reshape(nd, nj * per, bd, bd)


def _lru_fwd(p, cw, lv, wr, wi, *, B, T, nc, LW, col0, C):
    N = T - nc
    nj = LW // C

    def body(x_ref, cw_ref, lv_ref, wr_ref, wi_ref, o_ref, a_s, b_s, h_s):
        lv_ = lv_ref[...]
        xc = _conv_fwd(x_ref[:, 0:C], cw_ref[...], _seg_taps(T, nc, 4, 2)) + lv_[0:1]
        for d in (0, 1):
            r = _sigmoid(_bdot(xc, wr_ref[d, 0]) + lv_[1 + d:2 + d])
            i = _sigmoid(_bdot(xc, wi_ref[d, 0]) + lv_[3 + d:4 + d])
            la = -LRU_C * r * _softplus(-lv_[5 + d:6 + d])
            a_s[d] = jnp.exp(la)
            b_s[d] = jnp.sqrt(1.0 - jnp.exp(2.0 * la)) * i * xc
        _lru_scans([(a_s.at[d], b_s.at[d], h_s.at[d], None, _lru_orders(T, nc, d)[0]) for d in (0, 1)])
        o_ref[...] = ((h_s[0, nc:, :] + h_s[1, nc:, :]) * _gelu(x_ref[nc:, C:2 * C])).astype(BF16)

    return pl.pallas_call(
        body, name="lru_fwd", grid=(B, nj),
        in_specs=[pl.BlockSpec((T, 2 * C), lambda b, j: (b, col0 // (2 * C) + j)),
                  pl.BlockSpec((4, C), lambda b, j: (0, j)), pl.BlockSpec((8, C), lambda b, j: (0, j)),
                  pl.BlockSpec((2, 1, C, C), lambda b, j: (0, j, 0, 0)), pl.BlockSpec((2, 1, C, C), lambda b, j: (0, j, 0, 0))],
        out_specs=pl.BlockSpec((N, C), lambda b, j: (b, j)),
        out_shape=jax.ShapeDtypeStruct((B * N, LW), BF16),
        scratch_shapes=[pltpu.VMEM((2, T, C), F32)] * 3,
        compiler_params=_params(("parallel", "parallel")),
    )(p, cw, lv, wr, wi)


def _lru_bwd(p, cw, lv, wr, wi, dy, dp, *, B, T, nc, LW, col0, C):
    N = T - nc
    nj = LW // C

    def body(x_ref, cw_ref, lv_ref, wr_ref, wi_ref, dy_ref, dp_any, dp_ref, dcw_ref, dlv_ref, dwr_ref, dwi_ref,
             a_s, b_s, h_s, hp_s, mu_s, mup_s, dh_s, dxc_s):
        taps = _seg_taps(T, nc, 4, 2)
        lv_ = lv_ref[...]
        xl = x_ref[:, 0:C]
        xc = _conv_fwd(xl, cw_ref[...], taps) + lv_[0:1]
        gel, gelu_vjp = jax.vjp(_gelu, x_ref[nc:, C:2 * C])
        dh_s[0:nc, :] = jnp.zeros((nc, C), F32)
        dh_s[nc:, :] = dy_ref[...] * gel
        dxc_s[...] = jnp.zeros_like(dxc_s)

        @pl.when(pl.program_id(1) == 0)
        def _():
            dcw_ref[...] = jnp.zeros_like(dcw_ref)
            dlv_ref[...] = jnp.zeros_like(dlv_ref)
            dwr_ref[...] = jnp.zeros_like(dwr_ref)
            dwi_ref[...] = jnp.zeros_like(dwi_ref)

        def gates(d):
            lam = lv_[5 + d:6 + d]
            r = _sigmoid(_bdot(xc, wr_ref[d, 0]) + lv_[1 + d:2 + d])
            i = _sigmoid(_bdot(xc, wi_ref[d, 0]) + lv_[3 + d:4 + d])
            sp = _softplus(-lam)
            la = -LRU_C * r * sp
            e2 = jnp.exp(2.0 * la)
            return lam, r, i, sp, la, e2, jnp.sqrt(1.0 - e2)

        for d in (0, 1):
            _, _, i, _, la, _, mult = gates(d)
            a_s[d] = jnp.exp(la)
            b_s[d] = mult * i * xc
        _lru_scans([(a_s.at[d], b_s.at[d], h_s.at[d], hp_s.at[d], _lru_orders(T, nc, d)[0]) for d in (0, 1)])
        for d in (0, 1):
            b_s[d] = a_s[d] * dh_s[...]
        _lru_scans([(a_s.at[d], b_s.at[d], mu_s.at[d], mup_s.at[d], _lru_orders(T, nc, d)[1]) for d in (0, 1)])

        for d in (0, 1):
            lam, r, i, sp, la, e2, mult = gates(d)
            a = a_s[d]
            dinp = dh_s[...] + mup_s[d]
            da = dinp * hp_s[d]
            dmult = dinp * i * xc
            di = dinp * mult * xc
            dla = da * a - dmult * e2 / mult
            dpre_r = (dla * (-LRU_C * sp)) * r * (1.0 - r)
            dpre_i = di * i * (1.0 - i)
            dsp = jnp.sum(dla * (-LRU_C * r), axis=0, keepdims=True)
            dxc_s[...] += dinp * mult * i + _bdot(dpre_r, wr_ref[d, 0], _NT) + _bdot(dpre_i, wi_ref[d, 0], _NT)
            dwr_ref[d, 0] += _bdot(xc, dpre_r, _TN)
            dwi_ref[d, 0] += _bdot(xc, dpre_i, _TN)
            dlv_ref[1 + d:2 + d, :] += jnp.sum(dpre_r, axis=0, keepdims=True)
            dlv_ref[3 + d:4 + d, :] += jnp.sum(dpre_i, axis=0, keepdims=True)
            dlv_ref[5 + d:6 + d, :] += -dsp * _sigmoid(-lam)

        dxc = dxc_s[...]
        dxl, dw = _conv_bwd(xl, cw_ref[...], taps, dxc)
        dcw_ref[0:4, :] += dw
        dlv_ref[0:1, :] += jnp.sum(dxc, axis=0, keepdims=True)
        dp_ref[:, 0:C] = dxl.astype(BF16)
        (dyl,) = gelu_vjp(dy_ref[...] * (h_s[0, nc:, :] + h_s[1, nc:, :]))
        dp_ref[0:nc, C:2 * C] = jnp.zeros((nc, C), BF16)
        dp_ref[nc:, C:2 * C] = dyl.astype(BF16)

    xblk = pl.BlockSpec((T, 2 * C), lambda j, b: (b, col0 // (2 * C) + j))
    wblk = pl.BlockSpec((2, 1, C, C), lambda j, b: (0, j, 0, 0))
    vblk = pl.BlockSpec((8, C), lambda j, b: (0, j))
    return pl.pallas_call(
        body, name="lru_bwd", grid=(nj, B),
        in_specs=[xblk, pl.BlockSpec((4, C), lambda j, b: (0, j)), vblk, wblk, wblk,
                  pl.BlockSpec((N, C), lambda j, b: (b, j)), pl.BlockSpec(memory_space=pl.ANY)],
        out_specs=[xblk, vblk, vblk, wblk, wblk],
        out_shape=[jax.ShapeDtypeStruct(dp.shape, dp.dtype), jax.ShapeDtypeStruct((8, LW), F32),
                   jax.ShapeDtypeStruct((8, LW), F32), jax.ShapeDtypeStruct((2, nj, C, C), F32),
                   jax.ShapeDtypeStruct((2, nj, C, C), F32)],
        scratch_shapes=[pltpu.VMEM((2, T, C), F32)] * 6 + [pltpu.VMEM((T, C), F32)] * 2,
        input_output_aliases={6: 0},
        compiler_params=_params(("parallel", "arbitrary")),
    )(p, cw, lv, wr, wi, dy, dp)


def _chunk_masks(upper):
    i = lax.broadcasted_iota(jnp.int32, (CHUNK, CHUNK), 0)
    j = lax.broadcasted_iota(jnp.int32, (CHUNK, CHUNK), 1)
    ahead = jnp.where(upper, j - i, i - j)
    return i == j, ahead >= 0, ahead > 0


def _col2row(c, eye):
    return jnp.sum(jnp.where(eye, c, 0.0), axis=0, keepdims=True)


def _row2col(r, eye):
    return jnp.sum(jnp.where(eye, r, 0.0), axis=1, keepdims=True)


def _rowsum(x):
    return jnp.sum(x, axis=1, keepdims=True)


_INV_BASE = 8


def _unit_tri_inverses(Ls):
    G = len(Ls)
    W = G * CHUNK
    blk = (lax.broadcasted_iota(jnp.int32, (W, W), 0) // CHUNK) == (lax.broadcasted_iota(jnp.int32, (W, W), 1) // CHUNK)
    ri = lax.broadcasted_iota(jnp.int32, (CHUNK, W), 0)
    ci = lax.broadcasted_iota(jnp.int32, (CHUNK, W), 1) % CHUNK

    def bd(b):
        return jnp.where(blk, jnp.tile(b, (G, 1)), jnp.zeros((), BF16))

    def pdot(a, b):
        return jnp.dot(a.astype(BF16), bd(b.astype(BF16)), preferred_element_type=F32)

    Lc = Ls[0] if G == 1 else jnp.concatenate(Ls, axis=1)
    s = _INV_BASE
    Xp = -jnp.where(ri // s == ci // s, Lc, 0.0)
    Rm = Xp
    for _ in range(int(math.log2(s)) - 1):
        Xp = pdot(Xp, Xp)
        Rm = Rm + Xp + pdot(Rm, Xp)
    while s < CHUNK:
        E = jnp.where((ri // (2 * s) == ci // (2 * s)) & (ri // s != ci // s), Lc, 0.0)
        DE = E + pdot(Rm, E)
        Rm = Rm - (DE + pdot(DE, Rm))
        s *= 2
    eye = _chunk_masks(False)[0]
    return [jnp.where(eye, 1.0, 0.0) + Rm[:, g * CHUNK:(g + 1) * CHUNK] for g in range(G)]


def _delta_chunk_common(q, k, v, gcol, bcol, upper):
    eye, incl, strict = _chunk_masks(upper)
    gc = _rowsum(jnp.where(incl, _col2row(gcol, eye), 0.0))
    D = jnp.where(incl, jnp.exp(jnp.minimum(gc - _col2row(gc, eye), 0.0)), 0.0)
    kb = k * bcol
    AP = _bdot(jnp.concatenate([kb, q], axis=0), k, _NT)
    A = AP[:CHUNK]
    L = jnp.where(strict, A * D, 0.0)
    eg = jnp.exp(gc)
    gl = jnp.sum(gcol, axis=0, keepdims=True)
    attn = jnp.where(incl, AP[CHUNK:] * D, 0.0)
    return dict(eye=eye, incl=incl, strict=strict, gc=gc, D=D, kb=kb, A=A, L=L, eg=eg, gl=gl, egl=jnp.exp(gl),
                attn=attn, kbe=kb * eg, vb=v * bcol, qe=q * eg, kd=k * jnp.exp(gl - gc))


def _delta_group_pre(chunks, upper):
    cs = [_delta_chunk_common(*ch, upper) for ch in chunks]
    out = []
    for c, Tm in zip(cs, _unit_tri_inverses([c["L"] for c in cs])):
        dk = c["kbe"].shape[1]
        wu = _bdot(Tm, jnp.concatenate([c["kbe"], c["vb"]], axis=1))
        KN = _bdot(c["kd"], wu, _TN)
        QO = _bdot(c["attn"], wu)
        out.append((Tm, KN[:, :dk], KN[:, dk:], c["qe"] - QO[:, :dk], QO[:, dk:], c["egl"]))
    return out


def _delta_chunk_bwd(q, k, v, gcol, bcol, S, Tm, do, dS2, upper):
    c = _delta_chunk_common(q, k, v, gcol, bcol, upper)
    eye, incl, strict, D, eg, egl = c["eye"], c["incl"], c["strict"], c["D"], c["eg"], c["egl"]
    kb, kbe, vb, qe, kd, attn = c["kb"], c["kbe"], c["vb"], c["qe"], c["kd"], c["attn"]
    dkk = kbe.shape[1]
    wu = _bdot(Tm, jnp.concatenate([kbe, vb], axis=1))
    w = wu[:, :dkk]
    vn = wu[:, dkk:] - _bdot(w, S)
    dvn = _bdot(kd, dS2) + _bdot(attn, do, _TN)
    dkd = _bdot(vn, dS2, _NT)
    dgl = jnp.sum(_rowsum(dS2 * S), axis=0, keepdims=True) * egl
    dqa = _bdot(do, jnp.concatenate([S, vn], axis=0), _NT)
    dqe = dqa[:, :dkk]
    dattn = jnp.where(incl, dqa[:, dkk:], 0.0)
    dw = -_bdot(dvn, S, _NT)
    r = _rowsum(dkd * kd)
    dk = dkd * jnp.exp(c["gl"] - c["gc"])
    dgl = dgl + jnp.sum(r, axis=0, keepdims=True)
    dgc = _rowsum(dqe * qe) - r
    E = dattn * attn
    dvw = jnp.concatenate([dvn, dw], axis=1)
    dTm = _bdot(dvw, jnp.concatenate([vb, kbe], axis=1), _NT)
    dvk = _bdot(Tm, dvw, _TN)
    dvb = dvk[:, :dvn.shape[1]]
    dv = dvb * bcol
    dbeta = _rowsum(dvb * v)
    dkbe = dvk[:, dvn.shape[1]:]
    dkb = dkbe * eg
    dgc = dgc + _rowsum(dkbe * kbe)
    dL = jnp.where(strict, -_bdot(Tm, _bdot(dTm, Tm, _NT), _TN), 0.0)
    dA = dL * D
    E = E + dL * c["L"]
    PA = jnp.concatenate([dattn * D, dA], axis=0)
    PAk = _bdot(PA, k)
    dq = dqe * eg + PAk[:CHUNK]
    dkb = dkb + PAk[CHUNK:]
    dk = dk + _bdot(PA, jnp.concatenate([q, kb], axis=0), _TN) + dkb * bcol
    dbeta = dbeta + _rowsum(dkb * k)
    dgc = dgc + _rowsum(E) - _row2col(jnp.sum(E, axis=0, keepdims=True), eye)
    dg = _row2col(jnp.sum(jnp.where(incl, dgc, 0.0), axis=0, keepdims=True), eye) + dgl
    return dq, dk, dv, dg, dbeta


def _delta_unroll(trips):
    return max(u for u in (3, 2, 1) if trips % u == 0)


def _delta_group(n):
    return max(g for g in range(1, 2 * LANES // CHUNK + 1) if n % g == 0)


def _delta_chunk_at(T, nc, d, i):
    n, ncc = T // CHUNK, nc // CHUNK
    desc = jnp.where(i < ncc, ncc - 1 - i, n - 1 - (i - ncc))
    if isinstance(d, int):
        return i if d == 0 else desc
    return jnp.where(d == 0, i, desc)


def _dn_out_math(o, onorm, z):
    return _rmsn(o, onorm) * _silu(z)


def _delta_fwd(qkv, gb, p, onorm, *, B, T, nc, H, HD):
    N = T - nc
    n = T // CHUNK
    G = _delta_group(n)

    def body(q_ref, k_ref, v_ref, gb_ref, z_ref, on_ref, y_ref, o_ref, Tm_ref, K_ref, S_ref, Qp_ref, eg_ref,
             N_s, O0_s, o_s):
        h = pl.program_id(1)
        lane = lax.broadcasted_iota(jnp.int32, (CHUNK, LANES), 1)

        def pre(g, carry):
            cs = [g * G + i for i in range(G)]
            rows = [pl.ds(pl.multiple_of(c * CHUNK, CHUNK), CHUNK) for c in cs]
            for d in (0, 1):
                chunks = []
                for r in rows:
                    gbb = gb_ref[r, :]
                    chunks.append((q_ref[r, :], k_ref[r, :], v_ref[r, :],
                                   _rowsum(jnp.where(lane == d * H + h, gbb, 0.0)),
                                   _rowsum(jnp.where(lane == 2 * H + d * H + h, gbb, 0.0))))
                for c, r, (Tm, K, Nn, Qp, O0, egl) in zip(cs, rows, _delta_group_pre(chunks, d == 1)):
                    Tm_ref[0, d * n + c] = Tm
                    K_ref[0, d * n + c] = K.astype(BF16)
                    N_s[d * n + c] = Nn
                    Qp_ref[0, d, r, :] = Qp.astype(BF16)
                    O0_s[d, r, :] = O0
                    eg_ref[0, d * n + c] = jnp.broadcast_to(egl, (SUBLANES, HD))
            return carry

        lax.fori_loop(0, n // G, pre, 0)

        def step(i, Ss):
            out = []
            for d in (0, 1):
                c = _delta_chunk_at(T, nc, d, i)
                rows = pl.ds(pl.multiple_of(c * CHUNK, CHUNK), CHUNK)
                S_ref[0, d * n + c] = Ss[d]
                Sb = Ss[d].astype(BF16)
                o_s[d, rows, :] = jnp.dot(Qp_ref[0, d, rows, :], Sb, preferred_element_type=F32) + O0_s[d, rows, :]
                out.append(eg_ref[0, d * n + c][0:1] * Ss[d] + N_s[d * n + c]
                           - jnp.dot(K_ref[0, d * n + c], Sb, preferred_element_type=F32))
            return tuple(out)

        lax.fori_loop(0, n, step, (jnp.zeros((HD, HD), F32), jnp.zeros((HD, HD), F32)))
        o = o_s[0, nc:, :] + o_s[1, nc:, :]
        o_ref[...] = o
        y_ref[...] = _dn_out_math(o, on_ref[...], z_ref[nc:, :]).astype(BF16)

    col = lambda off: pl.BlockSpec((T, HD), lambda b, h: (b, off + h))
    lat = pl.BlockSpec((N, HD), lambda b, h: (b, h))
    per = lambda *blk: pl.BlockSpec((1, *blk), lambda b, h: (b * H + h, 0, 0, 0))
    return pl.pallas_call(
        body, name="delta_fwd", grid=(B, H),
        in_specs=[col(0), col(H), col(2 * H), pl.BlockSpec((T, LANES), lambda b, h: (b, 0)), col(3 * H),
                  pl.BlockSpec((1, HD), lambda b, h: (0, 0))],
        out_specs=[lat, lat, per(2 * n, CHUNK, CHUNK), per(2 * n, HD, HD), per(2 * n, HD, HD), per(2, T, HD),
                   per(2 * n, SUBLANES, HD)],
        out_shape=[jax.ShapeDtypeStruct((B * N, H * HD), BF16), jax.ShapeDtypeStruct((B * N, H * HD), F32),
                   jax.ShapeDtypeStruct((B * H, 2 * n, CHUNK, CHUNK), F32),
                   jax.ShapeDtypeStruct((B * H, 2 * n, HD, HD), BF16), jax.ShapeDtypeStruct((B * H, 2 * n, HD, HD), F32),
                   jax.ShapeDtypeStruct((B * H, 2, T, HD), BF16), jax.ShapeDtypeStruct((B * H, 2 * n, SUBLANES, HD), F32)],
        scratch_shapes=[pltpu.VMEM((2 * n, HD, HD), F32), pltpu.VMEM((2, T, HD), F32), pltpu.VMEM((2, T, HD), F32)],
        compiler_params=_params(("parallel", "parallel")),
    )(qkv, qkv, qkv, gb, p, onorm)


def _delta_bwd(qkv, gb, p, onorm, o, res, dy, dp, *, B, T, nc, H, HD):
    N = T - nc
    n = T // CHUNK

    def body(q_ref, k_ref, v_ref, gb_ref, z_ref, on_ref, o_ref, dy_ref, Tm_ref, K_ref, S_ref, Qp_ref, eg_ref, dp_any,
             dqkv_ref, dgb_ref, dp_ref, don_ref, do_s, R_s, dS_s):
        h, d = pl.program_id(1), pl.program_id(2)
        lane = lax.broadcasted_iota(jnp.int32, (CHUNK, LANES), 1)

        @pl.when(d == 0)
        def _():
            _, vjp = jax.vjp(_dn_out_math, o_ref[...], on_ref[...], z_ref[nc:, :])
            do, don, dz = vjp(dy_ref[...])
            do_s[0:nc, :] = jnp.zeros((nc, HD), F32)
            do_s[nc:, :] = do
            dp_ref[0:nc, :] = jnp.zeros((nc, HD), BF16)
            dp_ref[nc:, :] = dz.astype(BF16)
            dqkv_ref[...] = jnp.zeros_like(dqkv_ref)

            @pl.when(h == 0)
            def _():
                don_ref[...] = jnp.zeros_like(don_ref)
                dgb_ref[...] = jnp.zeros_like(dgb_ref)

            don_ref[0, 0:1, :] += don

        def r_of(c, carry):
            rows = pl.ds(pl.multiple_of(c * CHUNK, CHUNK), CHUNK)
            R_s[c] = lax.dot_general(Qp_ref[0, 0, rows, :], do_s[rows, :].astype(BF16), _TN, preferred_element_type=F32)
            return carry

        lax.fori_loop(0, n, r_of, 0)

        def bwd_step(i, dS):
            c = _delta_chunk_at(T, nc, d, n - 1 - i)
            dS_s[c] = dS
            return (eg_ref[0, c][0:1] * dS + R_s[c]
                    - lax.dot_general(K_ref[0, c], dS.astype(BF16), _TN, preferred_element_type=F32))

        lax.fori_loop(0, n, bwd_step, jnp.zeros((HD, HD), F32))

        def grads(c, carry):
            rows = pl.ds(pl.multiple_of(c * CHUNK, CHUNK), CHUNK)
            gbb = gb_ref[rows, :]
            gcol = _rowsum(jnp.where(lane == d * H + h, gbb, 0.0))
            bcol = _rowsum(jnp.where(lane == 2 * H + d * H + h, gbb, 0.0))
            dq, dk, dv, dg, dbeta = _delta_chunk_bwd(q_ref[rows, :], k_ref[rows, :], v_ref[rows, :], gcol, bcol,
                                                     S_ref[0, c], Tm_ref[0, c], do_s[rows, :], dS_s[c], d == 1)
            dqkv_ref[0, rows, :] += dq
            dqkv_ref[1, rows, :] += dk
            dqkv_ref[2, rows, :] += dv
            dgb_ref[rows, :] += (jnp.where(lane == d * H + h, dg, 0.0)
                                 + jnp.where(lane == 2 * H + d * H + h, dbeta, 0.0))
            return carry

        lax.fori_loop(0, n, grads, 0, unroll=_delta_unroll(n))

    col = lambda off: pl.BlockSpec((T, HD), lambda b, h, d: (b, off + h))
    lat = pl.BlockSpec((N, HD), lambda b, h, d: (b, h))
    per = lambda *blk: pl.BlockSpec((1, *blk), lambda b, h, d: (b * H + h, d, 0, 0))
    return pl.pallas_call(
        body, name="delta_bwd", grid=(B, H, 2),
        in_specs=[col(0), col(H), col(2 * H), pl.BlockSpec((T, LANES), lambda b, h, d: (b, 0)), col(3 * H),
                  pl.BlockSpec((1, HD), lambda b, h, d: (0, 0)), lat, lat,
                  per(n, CHUNK, CHUNK), per(n, HD, HD), per(n, HD, HD), per(1, T, HD), per(n, SUBLANES, HD),
                  pl.BlockSpec(memory_space=pl.ANY)],
        out_specs=[pl.BlockSpec((3, T, HD), lambda b, h, d: (0, b, h)), pl.BlockSpec((T, LANES), lambda b, h, d: (b, 0)),
                   col(3 * H), pl.BlockSpec((1, 8, HD), lambda b, h, d: (b, 0, 0))],
        out_shape=[jax.ShapeDtypeStruct((3, B * T, H * HD), F32), jax.ShapeDtypeStruct((B * T, LANES), F32),
                   jax.ShapeDtypeStruct(dp.shape, dp.dtype), jax.ShapeDtypeStruct((B, 8, HD), F32)],
        scratch_shapes=[pltpu.VMEM((T, HD), F32), pltpu.VMEM((n, HD, HD), F32), pltpu.VMEM((n, HD, HD), F32)],
        input_output_aliases={13: 2},
        compiler_params=_params(("parallel", "arbitrary", "arbitrary")),
    )(qkv, qkv, qkv, gb, p, onorm, o, dy, *res, dp)


def _rowwise(fn, ins, out_dtypes, *, name, tm=256, mult=16):
    R, W = ins[0].shape
    tm = _tile(R, tm, mult)

    def body(*refs):
        outs = fn(*[r[...] for r in refs[:len(ins)]])
        for o_ref, o in zip(refs[len(ins):], outs):
            o_ref[...] = o.astype(o_ref.dtype)

    spec = pl.BlockSpec((tm, W), lambda i: (i, 0))
    return pl.pallas_call(
        body, name=name, grid=(R // tm,), in_specs=[spec] * len(ins), out_specs=[spec] * len(out_dtypes),
        out_shape=[jax.ShapeDtypeStruct((R, W), dt) for dt in out_dtypes],
        compiler_params=_params(("parallel",)),
    )(*ins)


def _sum_lead(x, *, name, tm=256, mult=16):
    S, R, W = x.shape
    tm = _tile(R, tm, mult)

    def body(*refs):
        acc = refs[0][0].astype(F32)
        for r in refs[1:S]:
            acc = acc + r[0].astype(F32)
        refs[S][...] = acc

    return pl.pallas_call(
        body, name=name, grid=(R // tm,),
        in_specs=[pl.BlockSpec((1, tm, W), functools.partial(lambda s, i: (s, i, 0), s)) for s in range(S)],
        out_specs=pl.BlockSpec((tm, W), lambda i: (i, 0)),
        out_shape=jax.ShapeDtypeStruct((R, W), F32),
        compiler_params=_params(("parallel",)),
    )(*([x] * S))


def _adamw_math(w, g, m, v):
    m = ADAM_B1 * m + (1.0 - ADAM_B1) * g
    v = ADAM_B2 * v + (1.0 - ADAM_B2) * (g * g)
    m_hat = m / (1.0 - ADAM_B1 ** ADAM_STEP)
    v_hat = v / (1.0 - ADAM_B2 ** ADAM_STEP)
    return -ADAM_LR * (m_hat / (jnp.sqrt(v_hat) + ADAM_EPS) + ADAM_WD * w), m, v


def _adamw(w, g, m, v, *, name):
    tm = max(SUBLANES, (256 * 1024) // w.shape[1] // SUBLANES * SUBLANES)
    return _rowwise(_adamw_math, [w, g, m, v], [F32, F32, F32], name=name, tm=tm, mult=SUBLANES)


def _me():
    return lax.axis_index("x"), lax.axis_index("y"), lax.axis_index("c")


def _allgather_small(v):
    R, W = v.shape

    def body(x_ref, out_ref, send_sems, recv_sems, local_sem):
        x, y, c = _me()
        me, sibling = (x, y, c), (x, y, 1 - c)
        chips = [(1 - x, y), (x, 1 - y), (1 - x, 1 - y)]

        def slot(px, py, pc):
            return out_ref.at[4 * px + 2 * py + pc]

        def copy(k, block, to, src=None):
            return pltpu.make_async_remote_copy(
                src_ref=slot(*block) if src is None else src, dst_ref=slot(*block),
                send_sem=send_sems.at[k], recv_sem=recv_sems.at[k], device_id=to, device_id_type=MESH)

        mine = pltpu.make_async_copy(x_ref, slot(*me), local_sem)
        mine.start()
        first = [copy(0, me, sibling, src=x_ref)]
        first += [copy(1 + j, me, (*chip, c), src=x_ref) for j, chip in enumerate(chips)]
        for cp in first:
            cp.start()
        passed = [copy(4 + j, (*chip, c), sibling) for j, chip in enumerate(chips)]
        for j, chip in enumerate(chips):
            copy(1 + j, (*chip, c), me).wait_recv()
            passed[j].start()
        copy(0, sibling, me).wait_recv()
        for j, chip in enumerate(chips):
            copy(4 + j, (*chip, 1 - c), me).wait_recv()
        for cp in first + passed:
            cp.wait_send()
        mine.wait()

    return pl.pallas_call(
        body, name="allgather_small", out_shape=jax.ShapeDtypeStruct((8, R, W), v.dtype),
        in_specs=[pl.BlockSpec(memory_space=pltpu.VMEM)], out_specs=pl.BlockSpec(memory_space=pltpu.VMEM),
        scratch_shapes=[pltpu.SemaphoreType.DMA((7,)), pltpu.SemaphoreType.DMA((7,)), pltpu.SemaphoreType.DMA],
        compiler_params=_params(),
    )(v)


_ANY = pl.BlockSpec(memory_space=pl.ANY)


def _allgather_halves(shards, *, name):
    nw = len(shards)

    def body(*refs):
        x_refs, out_refs = refs[:nw], refs[nw:2 * nw]
        send_sems, recv_sems, local_sems = refs[2 * nw:]
        x, y, c = _me()
        me, sibling = (x, y, c), (x, y, 1 - c)
        chips = [(1 - x, y), (x, 1 - y), (1 - x, 1 - y)]

        def slot(w, px, py, pc):
            return out_refs[w].at[4 * px + 2 * py + pc]

        def copy(w, k, block, to, src=None):
            return pltpu.make_async_remote_copy(
                src_ref=slot(w, *block) if src is None else src, dst_ref=slot(w, *block),
                send_sem=send_sems.at[w, k], recv_sem=recv_sems.at[w, k], device_id=to, device_id_type=MESH)

        started, local = [], []
        for w in range(nw):
            half = shards[w].shape[0] // 2
            own = x_refs[w].at[pl.ds(c * half, half), :]
            mine = pltpu.make_async_copy(own, slot(w, *me), local_sems.at[w])
            mine.start()
            first = [copy(w, 0, me, sibling, src=own)]
            first += [copy(w, 1 + j, me, (*chip, c), src=own) for j, chip in enumerate(chips)]
            for cp in first:
                cp.start()
            started += first
            local.append(mine)
        for w in range(nw):
            for j, chip in enumerate(chips):
                copy(w, 1 + j, (*chip, c), me).wait_recv()
                fwd = copy(w, 4 + j, (*chip, c), sibling)
                fwd.start()
                started.append(fwd)
        for w in range(nw):
            copy(w, 0, sibling, me).wait_recv()
            for j, chip in enumerate(chips):
                copy(w, 4 + j, (*chip, 1 - c), me).wait_recv()
        for cp in started:
            cp.wait_send()
        for cp in local:
            cp.wait()

    return pl.pallas_call(
        body, name=name,
        out_shape=[jax.ShapeDtypeStruct((8, s.shape[0] // 2, s.shape[1]), s.dtype) for s in shards],
        in_specs=[_ANY] * nw, out_specs=[_ANY] * nw,
        scratch_shapes=[pltpu.SemaphoreType.DMA((nw, 7)), pltpu.SemaphoreType.DMA((nw, 7)), pltpu.SemaphoreType.DMA((nw,))],
        compiler_params=_params(),
    )(*shards)


def _sibling_send_halves(arrs, *, name):
    nw = len(arrs)

    def body(*refs):
        x_refs, out_refs, send_sems, recv_sems = refs[:nw], refs[nw:2 * nw], refs[2 * nw], refs[2 * nw + 1]
        x, y, c = _me()
        cps = []
        for w in range(nw):
            half = arrs[w].shape[1] // 2
            cp = pltpu.make_async_remote_copy(
                src_ref=x_refs[w].at[:, pl.ds((1 - c) * half, half), :], dst_ref=out_refs[w],
                send_sem=send_sems.at[w], recv_sem=recv_sems.at[w], device_id=(x, y, 1 - c), device_id_type=MESH)
            cp.start()
            cps.append(cp)
        for cp in cps:
            cp.wait()

    return pl.pallas_call(
        body, name=name,
        out_shape=[jax.ShapeDtypeStruct((a.shape[0], a.shape[1] // 2, a.shape[2]), a.dtype) for a in arrs],
        in_specs=[_ANY] * nw, out_specs=[_ANY] * nw,
        scratch_shapes=[pltpu.SemaphoreType.DMA((nw,)), pltpu.SemaphoreType.DMA((nw,))],
        compiler_params=_params(),
    )(*arrs)


def _sibling_swap(arrs, *, name):
    nw = len(arrs)

    def body(*refs):
        x_refs, out_refs, send_sems, recv_sems = refs[:nw], refs[nw:2 * nw], refs[2 * nw], refs[2 * nw + 1]
        x, y, c = _me()
        cps = []
        for w in range(nw):
            cp = pltpu.make_async_remote_copy(
                src_ref=x_refs[w], dst_ref=out_refs[w], send_sem=send_sems.at[w], recv_sem=recv_sems.at[w],
                device_id=(x, y, 1 - c), device_id_type=MESH)
            cp.start()
            cps.append(cp)
        for cp in cps:
            cp.wait()

    return pl.pallas_call(
        body, name=name, out_shape=[jax.ShapeDtypeStruct(a.shape, a.dtype) for a in arrs],
        in_specs=[_ANY] * nw, out_specs=[_ANY] * nw,
        scratch_shapes=[pltpu.SemaphoreType.DMA((nw,)), pltpu.SemaphoreType.DMA((nw,))],
        compiler_params=_params(),
    )(*arrs)


def _adamw_halves(w, own, sib, m, v, c_arr, *, name):
    r, cols = w.shape
    h = r // 2
    tm = _tile(h, max(SUBLANES, (192 * 1024) // cols // SUBLANES * SUBLANES), SUBLANES)
    nb = h // tm

    def body(c_ref, w_ref, own_ref, sib_ref, m_ref, v_ref, g_out, d_out, m_out, v_out):
        g = jnp.where(pl.program_id(0) == c_ref[0], own_ref[...], sib_ref[...])
        g_out[...] = g
        d_out[...], m_out[...], v_out[...] = _adamw_math(w_ref[...], g, m_ref[...], v_ref[...])

    full = pl.BlockSpec((tm, cols), lambda hh, i, c_ref: (hh * nb + i, 0))
    half = pl.BlockSpec((tm, cols), lambda hh, i, c_ref: (i, 0))
    return pl.pallas_call(
        body, name=name,
        grid_spec=pltpu.PrefetchScalarGridSpec(num_scalar_prefetch=1, grid=(2, nb),
                                               in_specs=[full, half, half, full, full], out_specs=[full] * 4),
        out_shape=[jax.ShapeDtypeStruct((r, cols), F32)] * 4,
        compiler_params=_params(("parallel", "parallel")),
    )(c_arr, w, own, sib, m, v)


_HBM = pl.BlockSpec(memory_space=pltpu.HBM)
_SEM = pl.BlockSpec(memory_space=pltpu.SEMAPHORE)
_DATAFLOW = pltpu.SideEffectType.DATAFLOW_SIDE_EFFECTING


def _chip_exchange_start(arrs, *, name):
    nw = len(arrs)

    def body(*refs):
        x_refs, land_refs, send_sems, recv_sems = refs[:nw], refs[nw:2 * nw], refs[2 * nw], refs[2 * nw + 1]
        token = refs[-1]
        x, y, c = _me()
        s_me = 2 * x + y
        for w in range(nw):
            for k, (px, py) in enumerate([(1 - x, y), (x, 1 - y), (1 - x, 1 - y)]):
                pltpu.make_async_remote_copy(
                    src_ref=x_refs[w].at[2 * px + py], dst_ref=land_refs[w].at[s_me], send_sem=send_sems.at[3 * w + k],
                    recv_sem=recv_sems.at[3 * w + k], device_id=(px, py, c), device_id_type=MESH).start()
        token[...] = jnp.zeros_like(token)

    hbm = [pltpu.HBM(a.shape, a.dtype) for a in arrs]
    outs = pl.pallas_call(
        body, name=name,
        out_shape=(pltpu.SemaphoreType.DMA((3 * nw,)), pltpu.SemaphoreType.DMA((3 * nw,)), *hbm, *hbm,
                   jax.ShapeDtypeStruct((SUBLANES, LANES), F32)),
        in_specs=[_HBM] * (2 * nw), out_specs=(_SEM, _SEM, *([_HBM] * (2 * nw)), pl.BlockSpec(memory_space=pltpu.VMEM)),
        input_output_aliases={i: 2 + i for i in range(2 * nw)},
        compiler_params=pltpu.CompilerParams(has_side_effects=_DATAFLOW),
    )(*[pltpu.with_memory_space_constraint(a, pltpu.HBM) for a in arrs],
      *[pltpu.with_memory_space_constraint(lax.empty(a.shape, a.dtype), pltpu.HBM) for a in arrs])
    return outs[0], outs[1], list(outs[2:2 + nw]), list(outs[2 + nw:2 + 2 * nw]), outs[-1]


def _allgather_start(shards, *, name):
    nw = len(shards)

    def body(*refs):
        x_refs, land_refs, send_sems, recv_sems = refs[:nw], refs[nw:2 * nw], refs[2 * nw], refs[2 * nw + 1]
        token = refs[-1]
        x, y, c = _me()
        me = 4 * x + 2 * y + c
        for w in range(nw):
            half = shards[w].shape[0] // 2
            own = x_refs[w].at[pl.ds(c * half, half), :]
            for k, to in enumerate([(x, y, 1 - c), (1 - x, y, c), (x, 1 - y, c), (1 - x, 1 - y, c)]):
                pltpu.make_async_remote_copy(
                    src_ref=own, dst_ref=land_refs[w].at[me], send_sem=send_sems.at[4 * w + k],
                    recv_sem=recv_sems.at[4 * w + k], device_id=to, device_id_type=MESH).start()
        token[...] = jnp.zeros_like(token)

    lands = [pltpu.HBM((8, s.shape[0] // 2, s.shape[1]), s.dtype) for s in shards]
    outs = pl.pallas_call(
        body, name=name,
        out_shape=(pltpu.SemaphoreType.DMA((4 * nw,)), pltpu.SemaphoreType.DMA((4 * nw,)),
                   *[pltpu.HBM(s.shape, s.dtype) for s in shards], *lands, jax.ShapeDtypeStruct((SUBLANES, LANES), F32)),
        in_specs=[_HBM] * (2 * nw), out_specs=(_SEM, _SEM, *([_HBM] * (2 * nw)), pl.BlockSpec(memory_space=pltpu.VMEM)),
        input_output_aliases={i: 2 + i for i in range(2 * nw)},
        compiler_params=pltpu.CompilerParams(has_side_effects=_DATAFLOW),
    )(*[pltpu.with_memory_space_constraint(s, pltpu.HBM) for s in shards],
      *[pltpu.with_memory_space_constraint(lax.empty(l.shape, l.dtype), pltpu.HBM) for l in lands])
    return outs[0], outs[1], list(outs[2:2 + nw]), list(outs[2 + nw:2 + 2 * nw]), outs[-1]


def _allgather_wait(send_sems, recv_sems, srcs, lands, after, *, name):
    nw = len(srcs)

    def body(*refs):
        x_refs, land_refs, send_sems, recv_sems = refs[:nw], refs[nw:2 * nw], refs[2 * nw], refs[2 * nw + 1]
        x, y, c = _me()
        for w in range(nw):
            half = srcs[w].shape[0] // 2
            own = x_refs[w].at[pl.ds(c * half, half), :]
            for k, (px, py, pc) in enumerate([(x, y, 1 - c), (1 - x, y, c), (x, 1 - y, c), (1 - x, 1 - y, c)]):
                cp = pltpu.make_async_remote_copy(
                    src_ref=own, dst_ref=land_refs[w].at[4 * px + 2 * py + pc], send_sem=send_sems.at[4 * w + k],
                    recv_sem=recv_sems.at[4 * w + k], device_id=(px, py, pc), device_id_type=MESH)
                cp.wait_send()
                cp.wait_recv()

    outs = pl.pallas_call(
        body, name=name,
        out_shape=(*[pltpu.HBM(a.shape, a.dtype) for a in srcs], *[pltpu.HBM(a.shape, a.dtype) for a in lands]),
        in_specs=[_HBM] * (2 * nw) + [_SEM, _SEM, _ANY], out_specs=tuple([_HBM] * (2 * nw)),
        input_output_aliases={i: i for i in range(2 * nw)},
        compiler_params=pltpu.CompilerParams(has_side_effects=_DATAFLOW),
    )(*srcs, *lands, send_sems, recv_sems, after)
    return list(outs[:nw]), list(outs[nw:])


def _pass_to_sibling(lands, *, name):
    nw = len(lands)

    def body(*refs):
        x_refs, out_refs, send_sems, recv_sems = refs[:nw], refs[nw:2 * nw], refs[2 * nw], refs[2 * nw + 1]
        x, y, c = _me()
        chips = [(1 - x, y), (x, 1 - y), (1 - x, 1 - y)]
        cps = []
        for w in range(nw):
            for k, (px, py) in enumerate(chips):
                cp = pltpu.make_async_remote_copy(
                    src_ref=x_refs[w].at[4 * px + 2 * py + c], dst_ref=out_refs[w].at[4 * px + 2 * py + c],
                    send_sem=send_sems.at[3 * w + k], recv_sem=recv_sems.at[3 * w + k], device_id=(x, y, 1 - c),
                    device_id_type=MESH)
                cp.start()
                cps.append(cp)
        for w in range(nw):
            for k, (px, py) in enumerate(chips):
                pltpu.make_async_remote_copy(
                    src_ref=x_refs[w].at[4 * px + 2 * py + c], dst_ref=out_refs[w].at[4 * px + 2 * py + 1 - c],
                    send_sem=send_sems.at[3 * w + k], recv_sem=recv_sems.at[3 * w + k], device_id=(x, y, 1 - c),
                    device_id_type=MESH).wait_recv()
        for cp in cps:
            cp.wait_send()

    return pl.pallas_call(
        body, name=name, out_shape=[jax.ShapeDtypeStruct(a.shape, a.dtype) for a in lands],
        in_specs=[_ANY] * nw, out_specs=[_ANY] * nw, input_output_aliases={i: i for i in range(nw)},
        scratch_shapes=[pltpu.SemaphoreType.DMA((3 * nw,)), pltpu.SemaphoreType.DMA((3 * nw,))],
        compiler_params=_params(),
    )(*lands)


def _chip_exchange_wait(send_sems, recv_sems, srcs, lands, after, *, name):
    nw = len(srcs)

    def body(*refs):
        x_refs, land_refs, send_sems, recv_sems = refs[:nw], refs[nw:2 * nw], refs[2 * nw], refs[2 * nw + 1]
        x, y, c = _me()
        for w in range(nw):
            for k, (px, py) in enumerate([(1 - x, y), (x, 1 - y), (1 - x, 1 - y)]):
                cp = pltpu.make_async_remote_copy(
                    src_ref=x_refs[w].at[2 * px + py], dst_ref=land_refs[w].at[2 * px + py], send_sem=send_sems.at[3 * w + k],
                    recv_sem=recv_sems.at[3 * w + k], device_id=(px, py, c), device_id_type=MESH)
                cp.wait_send()
                cp.wait_recv()

    hbm = [pltpu.HBM(a.shape, a.dtype) for a in srcs]
    outs = pl.pallas_call(
        body, name=name, out_shape=(*hbm, *hbm),
        in_specs=[_HBM] * (2 * nw) + [_SEM, _SEM, _ANY], out_specs=tuple([_HBM] * (2 * nw)),
        input_output_aliases={i: i for i in range(2 * nw)},
        compiler_params=pltpu.CompilerParams(has_side_effects=_DATAFLOW),
    )(*srcs, *lands, send_sems, recv_sems, after)
    return list(outs[:nw]), list(outs[nw:])


def _sum_slabs(landed, own_src, s_arr, after, *, name, tm=512):
    S, h, w = landed.shape
    tm = _tile(h, tm, 16)

    def body(s_ref, *refs):
        own = refs[S][0].astype(F32)
        acc = None
        for s in range(S):
            term = jnp.where(s_ref[0] == s, own, refs[s][0].astype(F32))
            acc = term if acc is None else acc + term
        refs[S + 2][...] = acc

    def slab(s):
        return pl.BlockSpec((1, tm, w), lambda i, s_ref: (jnp.where(s_ref[0] == s, (s + 1) % S, s), i, 0))

    return pl.pallas_call(
        body, name=name,
        grid_spec=pltpu.PrefetchScalarGridSpec(
            num_scalar_prefetch=1, grid=(h // tm,),
            in_specs=[slab(s) for s in range(S)] + [pl.BlockSpec((1, tm, w), lambda i, s_ref: (s_ref[0], i, 0)), _ANY],
            out_specs=pl.BlockSpec((tm, w), lambda i, s_ref: (i, 0))),
        out_shape=jax.ShapeDtypeStruct((h, w), F32),
        compiler_params=_params(("parallel",)),
    )(s_arr, *([landed] * S), own_src, after)


def _half_add(g, recv, c_arr, *, name):
    S, r, w = g.shape
    h = r // 2
    tm = _tile(h, 512, 16)
    nb = h // tm

    def body(c_ref, g_ref, r_ref, o_ref):
        o_ref[...] = (g_ref[...] + r_ref[...]).astype(BF16)

    return pl.pallas_call(
        body, name=name,
        grid_spec=pltpu.PrefetchScalarGridSpec(
            num_scalar_prefetch=1, grid=(S, nb),
            in_specs=[pl.BlockSpec((1, tm, w), lambda s, i, c_ref: (s, c_ref[0] * nb + i, 0)),
                      pl.BlockSpec((1, tm, w), lambda s, i, c_ref: (s, i, 0))],
            out_specs=pl.BlockSpec((1, tm, w), lambda s, i, c_ref: (s, i, 0))),
        out_shape=jax.ShapeDtypeStruct((S, h, w), BF16),
        compiler_params=_params(("parallel", "parallel")),
    )(c_arr, g, recv)


def _layout(sizes, width, part_mult, total_mult):
    offs, rows, r = [], [], 0
    for n in sizes:
        k = -(-n // width)
        offs.append(r)
        rows.append(k)
        r += -(-k // part_mult) * part_mult
    return offs, rows, -(-r // total_mult) * total_mult


def _pack(arrs, width, part_mult, total_mult, dtype, lead=()):
    nl = len(lead)
    sizes = [math.prod(a.shape[nl:]) for a in arrs]
    offs, rows, total = _layout(sizes, width, part_mult, total_mult)
    parts, r = [], 0
    for a, n, o, k in zip(arrs, sizes, offs, rows):
        kp = -(-k // part_mult) * part_mult
        flat = a.reshape(*lead, n).astype(dtype)
        if kp * width > n:
            flat = jnp.pad(flat, [(0, 0)] * nl + [(0, kp * width - n)])
        parts.append(flat.reshape(*lead, kp, width))
        r = o + kp
    if total > r:
        parts.append(jnp.zeros((*lead, total - r, width), dtype))
    return jnp.concatenate(parts, axis=nl)


def _unpack(pool, shapes, width, part_mult, total_mult):
    lead = pool.shape[:-2]
    sizes = [math.prod(s) for s in shapes]
    offs, rows, _ = _layout(sizes, width, part_mult, total_mult)
    out = []
    for s, n, o, k in zip(shapes, sizes, offs, rows):
        flat = lax.slice_in_dim(pool, o, o + k, axis=len(lead)).reshape(*lead, k * width)
        out.append(lax.slice_in_dim(flat, 0, n, axis=len(lead)).reshape(*lead, *s))
    return out


_WEIGHTS = ("c_ctx", "w_ada", "b_ada", "g_pre_mix", "g_post_mix", "g_pre_ffn", "g_post_ffn", "w_in", "b_merge",
            "dn_conv", "dn_a_log", "dn_dt_bias", "dn_onorm", "lru_conv", "lru_conv_b", "lru_w_rg", "lru_b_rg",
            "lru_w_ig", "lru_b_ig", "lru_lambda", "w_branch_dn", "w_branch_lru", "w_out", "w_up", "ffn_dw",
            "ffn_dw_b", "w_down")
_BIG = {"w_ada": True, "w_in": True, "w_branch_dn": False, "w_branch_lru": False, "w_out": False, "w_up": True,
        "w_down": False}
_SMALL_SHARDED = ("dn_conv", "lru_conv", "lru_b_rg", "lru_b_ig", "lru_lambda", "ffn_dw")
_NCHIP = 4
_FLAT_PART = 8
_FLAT_TOTAL = 256


def _to_chip_shards(g, by_cols):
    if by_cols:
        return g.reshape(g.shape[0], _NCHIP, g.shape[1] // _NCHIP).transpose(1, 0, 2)
    return g.reshape(_NCHIP, g.shape[0] // _NCHIP, g.shape[1])


def _from_chip_shards(s, by_cols):
    if by_cols:
        return s.transpose(1, 0, 2).reshape(s.shape[1], _NCHIP * s.shape[2])
    return s.reshape(_NCHIP * s.shape[1], s.shape[2])


def _dsilu(x):
    s = _sigmoid(x)
    return s * (1.0 + x * (1.0 - s))


def kernel(x, c, ctx, c_ctx, w_ada, b_ada, g_pre_mix, g_post_mix, g_pre_ffn, g_post_ffn, w_in, b_merge, dn_conv, dn_a_log, dn_dt_bias, dn_onorm, lru_conv, lru_conv_b, lru_w_rg, lru_b_rg, lru_w_ig, lru_b_ig, lru_lambda, w_branch_dn, w_branch_lru, w_out, w_up, ffn_dw, ffn_dw_b, w_down, loss_target, m_c_ctx, m_w_ada, m_b_ada, m_g_pre_mix, m_g_post_mix, m_g_pre_ffn, m_g_post_ffn, m_w_in, m_b_merge, m_dn_conv, m_dn_a_log, m_dn_dt_bias, m_dn_onorm, m_lru_conv, m_lru_conv_b, m_lru_w_rg, m_lru_b_rg, m_lru_w_ig, m_lru_b_ig, m_lru_lambda, m_w_branch_dn, m_w_branch_lru, m_w_out, m_w_up, m_ffn_dw, m_ffn_dw_b, m_w_down, v_c_ctx, v_w_ada, v_b_ada, v_g_pre_mix, v_g_post_mix, v_g_pre_ffn, v_g_post_ffn, v_w_in, v_b_merge, v_dn_conv, v_dn_a_log, v_dn_dt_bias, v_dn_onorm, v_lru_conv, v_lru_conv_b, v_lru_w_rg, v_lru_b_rg, v_lru_w_ig, v_lru_b_ig, v_lru_lambda, v_w_branch_dn, v_w_branch_lru, v_w_out, v_w_up, v_ffn_dw, v_ffn_dw_b, v_w_down):
    W = dict(zip(_WEIGHTS, (c_ctx, w_ada, b_ada, g_pre_mix, g_post_mix, g_pre_ffn, g_post_ffn, w_in, b_merge, dn_conv,
                            dn_a_log, dn_dt_bias, dn_onorm, lru_conv, lru_conv_b, lru_w_rg, lru_b_rg, lru_w_ig, lru_b_ig,
                            lru_lambda, w_branch_dn, w_branch_lru, w_out, w_up, ffn_dw, ffn_dw_b, w_down)))
    Mo = dict(zip(_WEIGHTS, (m_c_ctx, m_w_ada, m_b_ada, m_g_pre_mix, m_g_post_mix, m_g_pre_ffn, m_g_post_ffn, m_w_in,
                             m_b_merge, m_dn_conv, m_dn_a_log, m_dn_dt_bias, m_dn_onorm, m_lru_conv, m_lru_conv_b,
                             m_lru_w_rg, m_lru_b_rg, m_lru_w_ig, m_lru_b_ig, m_lru_lambda, m_w_branch_dn,
                             m_w_branch_lru, m_w_out, m_w_up, m_ffn_dw, m_ffn_dw_b, m_w_down)))
    Vo = dict(zip(_WEIGHTS, (v_c_ctx, v_w_ada, v_b_ada, v_g_pre_mix, v_g_post_mix, v_g_pre_ffn, v_g_post_ffn, v_w_in,
                             v_b_merge, v_dn_conv, v_dn_a_log, v_dn_dt_bias, v_dn_onorm, v_lru_conv, v_lru_conv_b,
                             v_lru_w_rg, v_lru_b_rg, v_lru_w_ig, v_lru_b_ig, v_lru_lambda, v_w_branch_dn,
                             v_w_branch_lru, v_w_out, v_w_up, v_ffn_dw, v_ffn_dw_b, v_w_down)))
    B, N, D = x.shape
    NC = ctx.shape[1]
    T = NC + N
    H, HD = dn_a_log.shape[-1], dn_onorm.shape[-1]
    DNW = H * HD
    LW, LBD = lru_conv_b.shape[-1], lru_w_rg.shape[-1]
    DFF = ffn_dw_b.shape[-1]
    LC = LANES
    x_i, y_i, c_i = _me()
    s_me = 2 * x_i + y_i
    tm = _tile(math.gcd(NC, N), 256, 16)

    def whole(n, g):
        r, w_ = W[n].shape[1:]
        return g.reshape(_NCHIP, r, w_) if _BIG[n] else g.reshape(_NCHIP * r, w_)

    first = ("w_ada", "w_in")
    later = tuple(n for n in _BIG if n not in first)
    shard16 = {n: W[n][0].astype(BF16) for n in _BIG}
    full = {n: whole(n, g) for n, g in zip(first, _allgather_halves([shard16[n] for n in first], name="allgather_first"))}

    small_local = [W[n][0].reshape(-1, W[n].shape[-1]) for n in _SMALL_SHARDED]
    small_shapes = [a.shape for a in small_local]
    spack = _pack(small_local, LANES, _FLAT_PART, _FLAT_PART, F32)
    sgath = _allgather_small(spack)[0::2]
    sfull = {n: _from_chip_shards(s, True)
             for n, s in zip(_SMALL_SHARDED, _unpack(sgath, small_shapes, LANES, _FLAT_PART, _FLAT_PART))}

    later16, sgath = lax.optimization_barrier(([shard16[n] for n in later], sgath))
    ag_send, ag_recv, ag_src, ag_land, ag_token = _allgather_start(later16, name="ag_start")

    o_a = 4 * DNW
    o_xl = o_a + 4 * H
    o_mg = o_xl + 2 * LW
    wi_ = _from_chip_shards(full["w_in"], True)
    nj = LW // LC
    lru_cols = jnp.stack([wi_[:, o_xl:o_xl + LW].reshape(D, nj, LC), wi_[:, o_xl + LW:o_mg].reshape(D, nj, LC)],
                         axis=2).reshape(D, 2 * LW)
    wp = jnp.concatenate([wi_[:, :o_a], lru_cols, wi_[:, o_mg:], wi_[:, o_a:o_xl],
                          jnp.zeros((D, LANES - 4 * H), BF16)], axis=1)
    p_lru, p_mg, p_ab = 4 * DNW, 4 * DNW + 2 * LW, 4 * DNW + 2 * LW + 2 * D
    PW = p_ab + LANES

    MR = LANES
    cond = jnp.concatenate([c, c_ctx[None], jnp.zeros((MR - B - 1, D), F32)], axis=0)
    silu_rows = _rowwise(lambda a: (_silu(a),), [cond], [F32], name="cond_silu")[0]
    mod = _matmul(silu_rows, full["w_ada"], b_shards=(0, _NCHIP), name="ada_fwd") + b_ada + ag_token[0, 0]
    mx = mod[:B].reshape(B, 6, D)
    mc = mod[B].reshape(6, D)
    zero = jnp.zeros((B, D), F32)
    tab = jnp.stack([jnp.stack([jnp.broadcast_to(mc[0], (B, D)), jnp.broadcast_to(mc[1], (B, D))] + [zero] * 6, axis=1),
                     jnp.stack([mx[:, 0], mx[:, 1]] + [zero] * 6, axis=1)], axis=1)
    vecs = jnp.stack([mx[:, 2], mx[:, 3], mx[:, 4], mx[:, 5]] + [zero] * 4, axis=1)
    gains = jnp.concatenate([g_post_mix, g_pre_ffn, g_post_ffn, jnp.zeros((5, D), F32)], axis=0)

    u = _premix_fwd(ctx, x, g_pre_mix, tab, tm=tm)
    p = _matmul(u, wp, name="in_fwd")
    dkw = dict(B=B, T=T, nc=NC, H=H, HD=HD)
    qkv = _dnprep_fwd(p, sfull["dn_conv"], **dkw)
    prm = jnp.concatenate([
        jnp.concatenate([dn_a_log.reshape(1, 2 * H), jnp.zeros((1, LANES - 2 * H), F32)], axis=1),
        jnp.concatenate([dn_dt_bias.reshape(1, 2 * H), jnp.zeros((1, LANES - 2 * H), F32)], axis=1),
        jnp.zeros((6, LANES), F32)], axis=0)
    gtm = _tile(B * T, 512, 16)
    gb = _gb_fwd(p, prm, rows=B * T, col0=p_ab, H=H, tm=gtm)
    y_dn, o_dn, *dn_res = _delta_fwd(qkv, gb, p, dn_onorm, **dkw)
    lv = jnp.concatenate([lru_conv_b, sfull["lru_b_rg"], sfull["lru_b_ig"], sfull["lru_lambda"], jnp.zeros((1, LW), F32)], axis=0)
    wr = _blockdiag(lru_w_rg[0], LC).astype(BF16)
    wi = _blockdiag(lru_w_ig[0], LC).astype(BF16)
    lkw = dict(B=B, T=T, nc=NC, LW=LW, col0=p_lru, C=LC)
    y_lru = _lru_fwd(p, sfull["lru_conv"], lv, wr, wi, **lkw)
    ag_src, ag_land = _allgather_wait(ag_send, ag_recv, ag_src, ag_land, y_lru, name="ag_wait")
    me_piece = 4 * x_i + 2 * y_i + c_i
    for n, src, land in zip(later, ag_src, _pass_to_sibling(ag_land, name="ag_pass")):
        own = lax.dynamic_slice_in_dim(src, c_i * (src.shape[0] // 2), src.shape[0] // 2, axis=0)
        full[n] = whole(n, lax.dynamic_update_index_in_dim(land, own, me_piece, axis=0))
    Ydn = _matmul(y_dn, full["w_branch_dn"], name="bdn_fwd")
    Ylru = _matmul(y_lru, full["w_branch_lru"], name="blru_fwd")
    mkw = dict(B=B, T=T, nc=NC, D=D, col0=p_mg, tm=tm)
    mixin = _merge_fwd(p, Ydn, Ylru, b_merge, **mkw)
    mix = _matmul(mixin, full["w_out"], name="out_fwd")
    h1, u2 = _post_fwd(x, mix, gains, vecs, tm=tm)
    F = _matmul(u2, full["w_up"], b_shards=(0, _NCHIP), name="up_fwd")
    w9 = sfull["ffn_dw"]
    ftc = _tile(DFF, 256)
    f, f_pre = _ffn_act_fwd(F, w9, ffn_dw_b, B=B, N=N, DFF=DFF, tc=ftc)
    dn = _matmul(f, full["w_down"], name="down_fwd")
    ddn, dout, sums_f = _final(h1, dn, loss_target, gains, vecs, tm=tm)

    G = {}
    df = _matmul(ddn, full["w_down"], tb=True, name="down_bwd_x")
    G["w_down"] = _matmul(f, ddn, ta=True, name="down_bwd_w")
    dFg, dFv, dwb = _ffn_act_bwd(F, f_pre, w9, df, B=B, N=N, DFF=DFF, tc=ftc)
    hs = _NCHIP // 2
    du2 = _matmul(dFg, full["w_up"], tb=True, b_shards=(0, hs), name="up_bwd_xg")
    du2 = _matmul(dFv, full["w_up"], tb=True, b_shards=(hs, hs), add=du2, name="up_bwd_xv")
    gup = _matmul(u2, dFg, ta=True, out_shards=hs, into=(lax.empty((_NCHIP, D, DFF // hs), F32), 0), name="up_bwd_wg")
    G["w_up"] = _matmul(u2, dFv, ta=True, out_shards=hs, into=(gup, hs), name="up_bwd_wv")
    dx1, dmix, sums_p = _post_bwd(x, mix, gains, vecs, dout, du2, tm=tm)
    dmixin = _matmul(dmix, full["w_out"], tb=True, name="out_bwd_x")
    G["w_out"] = _matmul(mixin, dmix, ta=True, name="out_bwd_w")
    dp = jnp.zeros((B * T, PW), BF16)
    dYdn, dYlru, dp, sums_m = _merge_bwd(p, Ydn, Ylru, b_merge, dmixin, dp, **mkw)
    dy_dn = _matmul(dYdn, full["w_branch_dn"], tb=True, name="bdn_bwd_x")
    G["w_branch_dn"] = _matmul(y_dn, dYdn, ta=True, name="bdn_bwd_w")
    dy_lru = _matmul(dYlru, full["w_branch_lru"], tb=True, name="blru_bwd_x")
    G["w_branch_lru"] = _matmul(y_lru, dYlru, ta=True, name="blru_bwd_w")

    c_arr = c_i.astype(jnp.int32).reshape(1)
    s_arr = s_me.astype(jnp.int32).reshape(1)

    def chip_sums(names, tag):
        slabs = [G[n] if _BIG[n] else G[n].reshape(_NCHIP, G[n].shape[0] // _NCHIP, G[n].shape[1]) for n in names]
        from_sibling = _sibling_send_halves(slabs, name="rs_sibling_" + tag)
        return [_half_add(g, r, c_arr, name="rs_add_" + n) for n, g, r in zip(names, slabs, from_sibling)]

    early = tuple(n for n in _BIG if n in G)
    cx_send, cx_recv, cx_src, cx_land, cx_token = _chip_exchange_start(chip_sums(early, "early"), name="cx_start")
    dp, dcw_l, dlv, dwr, dwi = _lru_bwd(p, sfull["lru_conv"], lv + cx_token[0, 0], wr, wi, dy_lru, dp, **lkw)
    dqkv, dgb, dp, don = _delta_bwd(qkv, gb, p, dn_onorm, o_dn, dn_res, dy_dn, dp, **dkw)
    dp, dprm = _gb_bwd(p, prm, dgb, dp, rows=B * T, col0=p_ab, H=H, tm=gtm)
    dp, dcw_d = _dnprep_bwd(p, sfull["dn_conv"], dqkv, dp, **dkw)
    dwp = _matmul(u, dp, ta=True, name="in_bwd_w")
    segs = [(0, o_a, 0), (o_a, 4 * H, p_ab)]
    segs += [(o_xl + i * LC, LC, p_lru + 2 * i * LC) for i in range(nj)]
    segs += [(o_xl + LW + i * LC, LC, p_lru + (2 * i + 1) * LC) for i in range(nj)]
    segs.append((o_mg, 2 * D, p_mg))
    n_in = W["w_in"].shape[-1]
    slabs_in = []
    for s in range(_NCHIP):
        lo, hi = s * n_in, (s + 1) * n_in
        parts = [dwp[:, q0 + max(lo, c0) - c0:q0 + min(hi, c0 + ln) - c0] for c0, ln, q0 in segs if max(lo, c0) < min(hi, c0 + ln)]
        slabs_in.append(jnp.concatenate(parts, axis=1))
    G["w_in"] = jnp.stack(slabs_in)
    wi_send, wi_recv, wi_src, wi_land, wi_token = _chip_exchange_start(chip_sums(("w_in",), "w_in"), name="cx_in_start")
    dU = _matmul(dp, wp, tb=True, after=wi_token, name="in_bwd_x")
    grad_x, sums_pm = _premix_bwd(ctx, x, g_pre_mix, tab, dU, dx1, tm=tm)

    dmod_x = jnp.stack([sums_pm[:, 1, 0], sums_pm[:, 1, 1], sums_p[:, 0], sums_p[:, 1], sums_p[:, 2], sums_f[:, 0]],
                       axis=1).reshape(B, 6 * D)
    dmod_c = jnp.concatenate([sums_pm[:, 0, 0].sum(0), sums_pm[:, 0, 1].sum(0), jnp.zeros((4 * D,), F32)])[None]
    dmod = jnp.concatenate([dmod_x, dmod_c, jnp.zeros((MR - B - 1, 6 * D), F32)], axis=0)
    G["w_ada"] = _matmul(silu_rows, dmod, ta=True, out_shards=_NCHIP, name="ada_bwd_w")
    dsilu = _matmul(dmod, full["w_ada"], tb=True, b_shards=(0, _NCHIP), name="ada_bwd_x")

    g_small = {
        "c_ctx": dsilu[B] * _dsilu(c_ctx),
        "b_ada": dmod[:B + 1].sum(0)[None],
        "g_pre_mix": sums_pm[:, :, 2].sum((0, 1))[None],
        "g_post_mix": sums_p[:, 3].sum(0)[None],
        "g_pre_ffn": sums_p[:, 4].sum(0)[None],
        "g_post_ffn": sums_f[:, 1].sum(0)[None],
        "b_merge": sums_m[0:1],
        "dn_conv": dcw_d[0:4][None],
        "dn_a_log": dprm[0, :2 * H].reshape(1, 2, H),
        "dn_dt_bias": dprm[1, :2 * H].reshape(1, 2, H),
        "dn_onorm": don[:, 0].sum(0)[None],
        "lru_conv": dcw_l[0:4][None],
        "lru_conv_b": dlv[0:1],
        "lru_w_rg": _blockdiag_extract(dwr, LBD)[None],
        "lru_b_rg": dlv[1:3][None],
        "lru_w_ig": _blockdiag_extract(dwi, LBD)[None],
        "lru_b_ig": dlv[3:5][None],
        "lru_lambda": dlv[5:7][None],
        "ffn_dw": dwb[0:9].reshape(1, 3, 3, DFF),
        "ffn_dw_b": dwb[9:10],
    }
    small_names = tuple(n for n in _WEIGHTS if n not in _BIG)
    loss_part = sums_f[:, 2].sum().reshape(1)
    gs_list = [g_small[n] for n in small_names] + [loss_part]
    gs_shapes = [a.shape for a in gs_list]
    gpack = _pack(gs_list, LANES, _FLAT_PART, _FLAT_TOTAL, F32)
    gsum = _sum_lead(_allgather_small(gpack), name="small_sum", tm=512, mult=SUBLANES)
    gs_red = dict(zip(small_names + ("loss",), _unpack(gsum, gs_shapes, LANES, _FLAT_PART, _FLAT_TOTAL)))
    loss = gs_red["loss"][0]

    grads, deltas, new_m, new_v = {}, {}, {}, {}

    def finish(names, lands, srcs, after, tag):
        halves = [_sum_slabs(l, src, s_arr, after, name="rs_sum_" + n) for n, l, src in zip(names, lands, srcs)]
        outs = None
        for n, own, sib in zip(names, halves, _sibling_swap(halves, name="rs_gather_" + tag)):
            shp = W[n].shape
            outs = _adamw_halves(W[n][0], own, sib, Mo[n][0], Vo[n][0], c_arr, name="adamw_" + n)
            grads[n], deltas[n], new_m[n], new_v[n] = (o.reshape(shp) for o in outs)
        return outs[1]

    cx_src, cx_land = _chip_exchange_wait(cx_send, cx_recv, cx_src, cx_land, dsilu, name="cx_wait")
    ada_sums, gsum = lax.optimization_barrier((chip_sums(("w_ada",), "w_ada"), gsum))
    ad_send, ad_recv, ad_src, ad_land, ad_token = _chip_exchange_start(ada_sums, name="cx_ada_start")
    last_early = finish(early, cx_land, cx_src, ad_token, "early")
    wi_src, wi_land = _chip_exchange_wait(wi_send, wi_recv, wi_src, wi_land, last_early, name="cx_in_wait")
    last_in = finish(("w_in",), wi_land, wi_src, last_early, "w_in")
    ad_src, ad_land = _chip_exchange_wait(ad_send, ad_recv, ad_src, ad_land, last_in, name="cx_ada_wait")
    finish(("w_ada",), ad_land, ad_src, last_in, "w_ada")
    for n in small_names:
        g = gs_red[n]
        if n in _SMALL_SHARDED:
            k = W[n].shape[-1]
            g = lax.dynamic_slice_in_dim(g, s_me * k, k, axis=g.ndim - 1)
        grads[n] = g.reshape(W[n].shape)
    sm_shapes = [W[n].shape for n in small_names]
    pk = lambda d: _pack([d[n] for n in small_names], LANES, _FLAT_PART, _FLAT_TOTAL, F32)
    d_, m_, v_ = _adamw(pk(W), pk(grads), pk(Mo), pk(Vo), name="adamw_small")
    for dst, pool_ in ((deltas, d_), (new_m, m_), (new_v, v_)):
        dst.update(zip(small_names, _unpack(pool_, sm_shapes, LANES, _FLAT_PART, _FLAT_TOTAL)))
    return (loss, grad_x, *[grads[n] for n in _WEIGHTS], *[deltas[n] for n in _WEIGHTS],
            *[new_m[n] for n in _WEIGHTS], *[new_v[n] for n in _WEIGHTS])
```

```python
import functools
import math

import jax
import jax.numpy as jnp
from jax import lax
from jax.experimental import pallas as pl
from jax.experimental.pallas import tpu as pltpu

F32 = jnp.float32
BF16 = jnp.bfloat16
EPS = 1e-6
GRID_W = 64
CHUNK = 256
LRU_C = 8.0
LANES = 128
SUBLANES = 8
VMEM_LIMIT = 56 * 1024 * 1024
ADAM_LR, ADAM_B1, ADAM_B2, ADAM_EPS, ADAM_WD, ADAM_STEP = 0.001, 0.9, 0.999, 1e-08, 0.01, 10
MESH = pl.DeviceIdType.MESH


def _tile(n, target, mult=LANES):
    best = None
    for t in range(mult, min(n, target) + 1, mult):
        if n % t == 0:
            best = t
    return best if best is not None else n


def _params(sem=None, **kw):
    return pltpu.CompilerParams(dimension_semantics=sem, vmem_limit_bytes=VMEM_LIMIT, **kw)


def _sigmoid(x):
    return 1.0 / (1.0 + jnp.exp(-x))


def _silu(x):
    return x * _sigmoid(x)


def _softplus(x):
    return jnp.maximum(x, 0.0) + jnp.log(1.0 + jnp.exp(-jnp.abs(x)))


def _gelu(x):
    return 0.5 * x * (1.0 + jnp.tanh(math.sqrt(2.0 / math.pi) * (x + 0.044715 * x * x * x)))


def _rmsn(u, gain):
    return u * lax.rsqrt(jnp.mean(u * u, axis=-1, keepdims=True) + EPS) * gain


_MM_VMEM = 40 * 1024 * 1024


_ANY_SPEC = pl.BlockSpec(memory_space=pl.ANY)


def _matmul(a, b, *, ta=False, tb=False, add=None, b_shards=None, out_shards=None, into=None, after=None,
            out_dtype=F32, name, tm=1024, tn=2048, tk=2048):
    (K, M) = a.shape if ta else a.shape[::-1]
    if b_shards is not None:
        s0, ns = b_shards
        bsh = (b.shape[1], ns * b.shape[2])
        nsh = b.shape[2]
    else:
        bsh = b.shape
    N = bsh[0] if tb else bsh[1]
    assert (bsh[1] if tb else bsh[0]) == K, (a.shape, b.shape, ta, tb)
    tm = _tile(M, tm)
    tk = _tile(nsh if (b_shards is not None and tb) else K, tk)
    nlim = nsh if (b_shards is not None and not tb) else (N // out_shards if out_shards else N)
    osz = jnp.dtype(out_dtype).itemsize + (4 if add is not None else 0)
    while True:
        tn_ = _tile(nlim, tn)
        need = 2 * (tm * tk * a.dtype.itemsize + tk * tn_ * b.dtype.itemsize + tm * tn_ * osz) + 4 * tm * tn_
        if need <= _MM_VMEM or tn <= LANES:
            break
        tn //= 2
    tn = tn_
    nk = K // tk
    dims = (((0 if ta else 1,), (1 if tb else 0,)), ((), ()))

    def body(a_ref, b_ref, *rest):
        c_ref = rest[0] if add is not None else None
        o_ref, acc_ref = rest[-2:]
        k = pl.program_id(2)

        @pl.when(k == 0)
        def _():
            acc_ref[...] = jnp.zeros_like(acc_ref) if c_ref is None else c_ref[...]

        bv = b_ref[0] if b_shards is not None else b_ref[...]
        acc_ref[...] += lax.dot_general(a_ref[...].astype(BF16), bv.astype(BF16), dims, preferred_element_type=F32)

        @pl.when(k == nk - 1)
        def _():
            if out_shards:
                o_ref[0] = acc_ref[...].astype(out_dtype)
            else:
                o_ref[...] = acc_ref[...].astype(out_dtype)

    a_spec = pl.BlockSpec((tk, tm), lambda i, j, k: (k, i)) if ta else pl.BlockSpec((tm, tk), lambda i, j, k: (i, k))
    if b_shards is None:
        b_spec = pl.BlockSpec((tn, tk), lambda i, j, k: (j, k)) if tb else pl.BlockSpec((tk, tn), lambda i, j, k: (k, j))
    elif tb:
        per = nsh // tk
        b_spec = pl.BlockSpec((1, tn, tk), lambda i, j, k: (s0 + k // per, j, k % per))
    else:
        per = nsh // tn
        b_spec = pl.BlockSpec((1, tk, tn), lambda i, j, k: (s0 + j // per, k, j % per))
    o_spec = pl.BlockSpec((tm, tn), lambda i, j, k: (i, j))
    extra, alias = (), {}
    if out_shards:
        oper = N // out_shards // tn
        o0 = 0
        out_shape = jax.ShapeDtypeStruct((out_shards, M, N // out_shards), out_dtype)
        if into is not None:
            buf, o0 = into
            out_shape = jax.ShapeDtypeStruct(buf.shape, buf.dtype)
            extra, alias = (buf,), {2 + (add is not None): 0}
        out_spec = pl.BlockSpec((1, tm, tn), lambda i, j, k: (o0 + j // oper, i, j % oper))
    else:
        out_spec, out_shape = o_spec, jax.ShapeDtypeStruct((M, N), out_dtype)
    if after is not None:
        extra = extra + (after,)
    return pl.pallas_call(
        body, name=name, grid=(M // tm, N // tn, nk),
        in_specs=[a_spec, b_spec] + ([o_spec] if add is not None else []) + [_ANY_SPEC] * len(extra),
        out_specs=out_spec, out_shape=out_shape, input_output_aliases=alias,
        scratch_shapes=[pltpu.VMEM((tm, tn), F32)],
        compiler_params=_params(("parallel", "parallel", "arbitrary")),
    )(*((a, b) + ((add,) if add is not None else ()) + extra))


def _premix_math(h, gain, shift, scale):
    return _rmsn(h, gain) * (1.0 + scale) + shift


def _stream_specs(nct, tm, D):
    return [pl.BlockSpec((1, tm, D), lambda b, t: (b, jnp.minimum(t, nct - 1), 0)),
            pl.BlockSpec((1, tm, D), lambda b, t: (b, jnp.maximum(t - nct, 0), 0))]


def _premix_fwd(ctx, x, gain, tab, *, tm):
    B, nc, D = ctx.shape
    T = nc + x.shape[1]
    nt, nct = T // tm, nc // tm

    def body(c_ref, x_ref, g_ref, tab_ref, u_ref):
        tabv = tab_ref[0, 0]
        h = jnp.where(pl.program_id(1) < nct, c_ref[0], x_ref[0])
        u_ref[...] = _premix_math(h, g_ref[...], tabv[0:1], tabv[1:2]).astype(BF16)

    return pl.pallas_call(
        body, name="premix_fwd", grid=(B, nt),
        in_specs=_stream_specs(nct, tm, D) + [
            pl.BlockSpec((1, D), lambda b, t: (0, 0)),
            pl.BlockSpec((1, 1, 8, D), lambda b, t: (b, jnp.where(t < nct, 0, 1), 0, 0))],
        out_specs=pl.BlockSpec((tm, D), lambda b, t: (b * nt + t, 0)),
        out_shape=jax.ShapeDtypeStruct((B * T, D), BF16),
        compiler_params=_params(("parallel", "parallel")),
    )(ctx, x, gain, tab)


def _premix_bwd(ctx, x, gain, tab, du, dres, *, tm):
    B, nc, D = ctx.shape
    N = x.shape[1]
    T = nc + N
    nt, nct = T // tm, nc // tm

    def body(c_ref, x_ref, g_ref, tab_ref, du_ref, dres_ref, dx_ref, sums_ref):
        t = pl.program_id(1)
        tabv = tab_ref[0, 0]
        h = jnp.where(t < nct, c_ref[0], x_ref[0])
        _, vjp = jax.vjp(_premix_math, h, g_ref[...], tabv[0:1], tabv[1:2])
        dh, dgain, dshift, dscale = vjp(du_ref[...].astype(F32))

        @pl.when((t == 0) | (t == nct))
        def _():
            sums_ref[...] = jnp.zeros_like(sums_ref)

        sums_ref[0, 0, 0:1, :] += dshift
        sums_ref[0, 0, 1:2, :] += dscale
        sums_ref[0, 0, 2:3, :] += dgain

        @pl.when(t >= nct)
        def _():
            dx_ref[0] = dres_ref[...] + dh

    lat = lambda b, t: jnp.maximum(t - nct, 0)
    return pl.pallas_call(
        body, name="premix_bwd", grid=(B, nt),
        in_specs=_stream_specs(nct, tm, D) + [
            pl.BlockSpec((1, D), lambda b, t: (0, 0)),
            pl.BlockSpec((1, 1, 8, D), lambda b, t: (b, jnp.where(t < nct, 0, 1), 0, 0)),
            pl.BlockSpec((tm, D), lambda b, t: (b * nt + t, 0)),
            pl.BlockSpec((tm, D), lambda b, t: (b * (nt - nct) + lat(b, t), 0))],
        out_specs=[pl.BlockSpec((1, tm, D), lambda b, t: (b, lat(b, t), 0)),
                   pl.BlockSpec((1, 1, 8, D), lambda b, t: (b, jnp.where(t < nct, 0, 1), 0, 0))],
        out_shape=[jax.ShapeDtypeStruct((B, N, D), F32), jax.ShapeDtypeStruct((B, 2, 8, D), F32)],
        compiler_params=_params(("parallel", "arbitrary")),
    )(ctx, x, gain, tab, du, dres)


def _merge_math(mgd, mgl, yd, yl, bd, bl):
    return _sigmoid(mgd + bd) * yd + _sigmoid(mgl + bl) * yl


def _merge_fwd(p, ydn, ylru, b_merge, *, B, T, nc, D, col0, tm):
    N = T - nc
    ntl, nt, nct, cb = N // tm, T // tm, nc // tm, col0 // D

    def body(mgd_ref, mgl_ref, yd_ref, yl_ref, bm_ref, o_ref):
        o_ref[...] = _merge_math(mgd_ref[...], mgl_ref[...], yd_ref[...], yl_ref[...],
                                 bm_ref[:, 0:D], bm_ref[:, D:2 * D]).astype(BF16)

    prow = lambda b, t: b * nt + nct + t
    return pl.pallas_call(
        body, name="merge_fwd", grid=(B, ntl),
        in_specs=[pl.BlockSpec((tm, D), lambda b, t: (prow(b, t), cb)),
                  pl.BlockSpec((tm, D), lambda b, t: (prow(b, t), cb + 1)),
                  pl.BlockSpec((tm, D), lambda b, t: (b * ntl + t, 0)),
                  pl.BlockSpec((tm, D), lambda b, t: (b * ntl + t, 0)),
                  pl.BlockSpec((1, 2 * D), lambda b, t: (0, 0))],
        out_specs=pl.BlockSpec((tm, D), lambda b, t: (b * ntl + t, 0)),
        out_shape=jax.ShapeDtypeStruct((B * N, D), BF16),
        compiler_params=_params(("parallel", "parallel")),
    )(p, p, ydn, ylru, b_merge)


def _merge_bwd(p, ydn, ylru, b_merge, dmix, dp, *, B, T, nc, D, col0, tm):
    N = T - nc
    ntl, nt, nct, cb = N // tm, T // tm, nc // tm, col0 // D
    assert col0 % (2 * D) == 0

    def body(mgd_ref, mgl_ref, yd_ref, yl_ref, bm_ref, dm_ref, dp_any, dyd_ref, dyl_ref, dp_ref, sums_ref):
        _, vjp = jax.vjp(_merge_math, mgd_ref[...], mgl_ref[...], yd_ref[...], yl_ref[...],
                         bm_ref[:, 0:D], bm_ref[:, D:2 * D])
        dmgd, dmgl, dyd, dyl, dbd, dbl = vjp(dm_ref[...])
        dyd_ref[...] = dyd.astype(BF16)
        dyl_ref[...] = dyl.astype(BF16)
        dp_ref[:, 0:D] = dmgd.astype(BF16)
        dp_ref[:, D:2 * D] = dmgl.astype(BF16)

        @pl.when((pl.program_id(0) == 0) & (pl.program_id(1) == 0))
        def _():
            sums_ref[...] = jnp.zeros_like(sums_ref)

        sums_ref[0:1, 0:D] += dbd
        sums_ref[0:1, D:2 * D] += dbl

    prow = lambda b, t: b * nt + nct + t
    row = pl.BlockSpec((tm, D), lambda b, t: (b * ntl + t, 0))
    return pl.pallas_call(
        body, name="merge_bwd", grid=(B, ntl),
        in_specs=[pl.BlockSpec((tm, D), lambda b, t: (prow(b, t), cb)),
                  pl.BlockSpec((tm, D), lambda b, t: (prow(b, t), cb + 1)),
                  row, row, pl.BlockSpec((1, 2 * D), lambda b, t: (0, 0)), row,
                  pl.BlockSpec(memory_space=pl.ANY)],
        out_specs=[row, row,
                   pl.BlockSpec((tm, 2 * D), lambda b, t: (prow(b, t), cb // 2)),
                   pl.BlockSpec((8, 2 * D), lambda b, t: (0, 0))],
        out_shape=[jax.ShapeDtypeStruct((B * N, D), BF16), jax.ShapeDtypeStruct((B * N, D), BF16),
                   jax.ShapeDtypeStruct(dp.shape, dp.dtype), jax.ShapeDtypeStruct((8, 2 * D), F32)],
        input_output_aliases={6: 2},
        compiler_params=_params(("arbitrary", "arbitrary")),
    )(p, p, ydn, ylru, b_merge, dmix, dp)


def _zero_context_cols(dp, *, B, T, nc, col0, width, tm):
    nt, nct = T // tm, nc // tm

    def body(dp_any, o_ref):
        o_ref[...] = jnp.zeros_like(o_ref)

    return pl.pallas_call(
        body, name="dp_zero_ctx", grid=(B, nct), in_specs=[pl.BlockSpec(memory_space=pl.ANY)],
        out_specs=pl.BlockSpec((tm, width), lambda b, t: (b * nt + t, col0 // width)),
        out_shape=jax.ShapeDtypeStruct(dp.shape, dp.dtype), input_output_aliases={0: 0},
        compiler_params=_params(("parallel", "parallel")),
    )(dp)


def _post_math(x, mix, g1, gate, g2, sh, sc):
    h1 = x + _rmsn(mix, g1) * gate
    return h1, _rmsn(h1, g2) * (1.0 + sc) + sh


def _post_fwd(x, mix, gains, vecs, *, tm):
    B, N, D = x.shape
    ntl = N // tm

    def body(x_ref, mix_ref, g_ref, v_ref, h1_ref, u2_ref):
        v = v_ref[0]
        h1, u2 = _post_math(x_ref[0], mix_ref[...], g_ref[0:1], v[0:1], g_ref[1:2], v[1:2], v[2:3])
        h1_ref[...] = h1
        u2_ref[...] = u2.astype(BF16)

    row = pl.BlockSpec((tm, D), lambda b, t: (b * ntl + t, 0))
    return pl.pallas_call(
        body, name="post_fwd", grid=(B, ntl),
        in_specs=[pl.BlockSpec((1, tm, D), lambda b, t: (b, t, 0)), row,
                  pl.BlockSpec((8, D), lambda b, t: (0, 0)), pl.BlockSpec((1, 8, D), lambda b, t: (b, 0, 0))],
        out_specs=[row, row],
        out_shape=[jax.ShapeDtypeStruct((B * N, D), F32), jax.ShapeDtypeStruct((B * N, D), BF16)],
        compiler_params=_params(("parallel", "parallel")),
    )(x, mix, gains, vecs)


def _post_bwd(x, mix, gains, vecs, dh1, du2, *, tm):
    B, N, D = x.shape
    ntl = N // tm

    def body(x_ref, mix_ref, g_ref, v_ref, dh1_ref, du2_ref, dx_ref, dmix_ref, sums_ref):
        v = v_ref[0]
        _, vjp = jax.vjp(_post_math, x_ref[0], mix_ref[...], g_ref[0:1], v[0:1], g_ref[1:2], v[1:2], v[2:3])
        dx, dmix, dg1, dgate, dg2, dsh, dsc = vjp((dh1_ref[...], du2_ref[...]))
        dx_ref[...] = dx
        dmix_ref[...] = dmix.astype(BF16)

        @pl.when(pl.program_id(1) == 0)
        def _():
            sums_ref[...] = jnp.zeros_like(sums_ref)

        sums_ref[0, 0:1, :] += dgate
        sums_ref[0, 1:2, :] += dsh
        sums_ref[0, 2:3, :] += dsc
        sums_ref[0, 3:4, :] += dg1
        sums_ref[0, 4:5, :] += dg2

    row = pl.BlockSpec((tm, D), lambda b, t: (b * ntl + t, 0))
    return pl.pallas_call(
        body, name="post_bwd", grid=(B, ntl),
        in_specs=[pl.BlockSpec((1, tm, D), lambda b, t: (b, t, 0)), row,
                  pl.BlockSpec((8, D), lambda b, t: (0, 0)), pl.BlockSpec((1, 8, D), lambda b, t: (b, 0, 0)), row, row],
        out_specs=[row, row, pl.BlockSpec((1, 8, D), lambda b, t: (b, 0, 0))],
        out_shape=[jax.ShapeDtypeStruct((B * N, D), F32), jax.ShapeDtypeStruct((B * N, D), BF16),
                   jax.ShapeDtypeStruct((B, 8, D), F32)],
        compiler_params=_params(("parallel", "arbitrary")),
    )(x, mix, gains, vecs, dh1, du2)


def _final_math(dn, g4, gate5):
    return _rmsn(dn, g4) * gate5


def _final(h1, dn, target, gains, vecs, *, tm):
    B, N, D = target.shape
    ntl = N // tm

    def body(h1_ref, dn_ref, t_ref, g_ref, v_ref, ddn_ref, dout_ref, sums_ref):
        v = v_ref[0]
        y, vjp = jax.vjp(_final_math, dn_ref[...], g_ref[2:3], v[3:4])
        err = h1_ref[...] + y - t_ref[0]
        dout = err * (1.0 / D)
        ddn, dg4, dgate5 = vjp(dout)
        ddn_ref[...] = ddn.astype(BF16)
        dout_ref[...] = dout

        @pl.when(pl.program_id(1) == 0)
        def _():
            sums_ref[...] = jnp.zeros_like(sums_ref)

        sums_ref[0, 0:1, :] += dgate5
        sums_ref[0, 1:2, :] += dg4
        sums_ref[0, 2:3, :] += jnp.sum(err * err, axis=0, keepdims=True) * (0.5 / D)

    row = pl.BlockSpec((tm, D), lambda b, t: (b * ntl + t, 0))
    return pl.pallas_call(
        body, name="final", grid=(B, ntl),
        in_specs=[row, row, pl.BlockSpec((1, tm, D), lambda b, t: (b, t, 0)),
                  pl.BlockSpec((8, D), lambda b, t: (0, 0)), pl.BlockSpec((1, 8, D), lambda b, t: (b, 0, 0))],
        out_specs=[row, row, pl.BlockSpec((1, 8, D), lambda b, t: (b, 0, 0))],
        out_shape=[jax.ShapeDtypeStruct((B * N, D), BF16), jax.ShapeDtypeStruct((B * N, D), F32),
                   jax.ShapeDtypeStruct((B, 8, D), F32)],
        compiler_params=_params(("parallel", "arbitrary")),
    )(h1, dn, target, gains, vecs)


def _shift(x, s):
    s = s % x.shape[0]
    return x if s == 0 else pltpu.roll(x, s, 0)


def _seg_taps(T, nc, width, pad_left):
    t = lax.broadcasted_iota(jnp.int32, (T, 1), 0)
    pos = jnp.where(t < nc, t, t - nc)
    seg = jnp.where(t < nc, nc, T - nc)
    taps = []
    for k in range(width):
        src = pos + (k - pad_left)
        taps.append((pad_left - k, (src >= 0) & (src < seg)))
    return taps


def _grid_taps(N):
    t = lax.broadcasted_iota(jnp.int32, (N, 1), 0)
    wcol = t % GRID_W
    taps = []
    for dr in (-1, 0, 1):
        for dw in (-1, 0, 1):
            off = dr * GRID_W + dw
            ok = (wcol + dw >= 0) & (wcol + dw < GRID_W) & (t + dr * GRID_W >= 0) & (t + dr * GRID_W < N)
            taps.append((-off, ok))
    return taps


def _conv_fwd(x, w, taps):
    y = jnp.zeros_like(x)
    for k, (s, m) in enumerate(taps):
        y = y + w[k:k + 1] * jnp.where(m, _shift(x, s), 0.0)
    return y


def _conv_bwd(x, w, taps, dy):
    dx = jnp.zeros_like(x)
    dws = []
    for k, (s, m) in enumerate(taps):
        dym = jnp.where(m, dy, 0.0)
        dx = dx + w[k:k + 1] * _shift(dym, -s)
        dws.append(jnp.sum(dym * _shift(x, s), axis=0, keepdims=True))
    return dx, jnp.concatenate(dws, axis=0)


def _ffn_act_fwd(F, w9, bias, *, B, N, DFF, tc):
    nj = DFF // tc

    def body(fg_ref, fv_ref, w_ref, b_ref, o_ref, pre_ref):
        fg = _conv_fwd(fg_ref[...], w_ref[...], _grid_taps(N)) + b_ref[...]
        pre_ref[...] = fg
        o_ref[...] = (_gelu(fg) * fv_ref[...]).astype(BF16)

    col = pl.BlockSpec((N, tc), lambda b, j: (b, j))
    return pl.pallas_call(
        body, name="ffn_act_fwd", grid=(B, nj),
        in_specs=[col, pl.BlockSpec((N, tc), lambda b, j: (b, nj + j)),
                  pl.BlockSpec((9, tc), lambda b, j: (0, j)), pl.BlockSpec((1, tc), lambda b, j: (0, j))],
        out_specs=[col, col],
        out_shape=[jax.ShapeDtypeStruct((B * N, DFF), BF16), jax.ShapeDtypeStruct((B * N, DFF), F32)],
        compiler_params=_params(("parallel", "parallel")),
    )(F, F, w9, bias)


def _ffn_act_bwd(F, pre, w9, df, *, B, N, DFF, tc):
    nj = DFF // tc

    def body(fg_ref, fv_ref, w_ref, pre_ref, df_ref, dfg_ref, dfv_ref, dwb_ref):
        taps = _grid_taps(N)
        x = fg_ref[...]
        fg, vjp = jax.vjp(lambda a: _gelu(a), pre_ref[...])
        dfl = df_ref[...]
        dfv_ref[...] = (dfl * fg).astype(BF16)
        (dpre,) = vjp(dfl * fv_ref[...])
        dx, dw = _conv_bwd(x, w_ref[...], taps, dpre)
        dfg_ref[...] = dx.astype(BF16)

        @pl.when(pl.program_id(1) == 0)
        def _():
            dwb_ref[...] = jnp.zeros_like(dwb_ref)

        dwb_ref[0:9, :] += dw
        dwb_ref[9:10, :] += jnp.sum(dpre, axis=0, keepdims=True)

    col = pl.BlockSpec((N, tc), lambda j, b: (b, j))
    return pl.pallas_call(
        body, name="ffn_act_bwd", grid=(nj, B),
        in_specs=[col, pl.BlockSpec((N, tc), lambda j, b: (b, nj + j)), pl.BlockSpec((9, tc), lambda j, b: (0, j)), col, col],
        out_specs=[col, col, pl.BlockSpec((16, tc), lambda j, b: (0, j))],
        out_shape=[jax.ShapeDtypeStruct((B * N, DFF), BF16), jax.ShapeDtypeStruct((B * N, DFF), BF16),
                   jax.ShapeDtypeStruct((16, DFF), F32)],
        compiler_params=_params(("parallel", "arbitrary")),
    )(F, F, w9, pre, df)


def _dnprep_math(y, is_qk, scale):
    s = _silu(y)
    n = s * lax.rsqrt(jnp.sum(s * s, axis=-1, keepdims=True) + EPS) * scale
    return jnp.where(is_qk, n, s)


def _dnprep_fwd(p, cw, *, B, T, nc, H, HD):
    def body(x_ref, w_ref, o_ref):
        j = pl.program_id(1)
        y = _conv_fwd(x_ref[...], w_ref[...], _seg_taps(T, nc, 4, 2))
        o_ref[...] = _dnprep_math(y, j < 2 * H, jnp.where(j < H, HD ** -0.5, 1.0))

    return pl.pallas_call(
        body, name="dnprep_fwd", grid=(B, 3 * H),
        in_specs=[pl.BlockSpec((T, HD), lambda b, j: (b, j)), pl.BlockSpec((4, HD), lambda b, j: (0, j))],
        out_specs=pl.BlockSpec((T, HD), lambda b, j: (b, j)),
        out_shape=jax.ShapeDtypeStruct((B * T, 3 * H * HD), F32),
        compiler_params=_params(("parallel", "parallel")),
    )(p, cw)


def _dnprep_bwd(p, cw, dqkv, dp, *, B, T, nc, H, HD):
    def body(x_ref, w_ref, d_ref, dp_any, dp_ref, dcw_ref):
        j = pl.program_id(0)
        taps = _seg_taps(T, nc, 4, 2)
        x = x_ref[...]
        y = _conv_fwd(x, w_ref[...], taps)
        is_qk, scale = j < 2 * H, jnp.where(j < H, HD ** -0.5, 1.0)
        _, vjp = jax.vjp(lambda a: _dnprep_math(a, is_qk, scale), y)
        (dy,) = vjp(d_ref[0])
        dx, dw = _conv_bwd(x, w_ref[...], taps, dy)
        dp_ref[...] = dx.astype(BF16)

        @pl.when(pl.program_id(1) == 0)
        def _():
            dcw_ref[...] = jnp.zeros_like(dcw_ref)

        dcw_ref[0:4, :] += dw

    col = pl.BlockSpec((T, HD), lambda j, b: (b, j))
    return pl.pallas_call(
        body, name="dnprep_bwd", grid=(3 * H, B),
        in_specs=[col, pl.BlockSpec((4, HD), lambda j, b: (0, j)),
                  pl.BlockSpec((1, T, HD), lambda j, b: (j // H, b, j % H)), pl.BlockSpec(memory_space=pl.ANY)],
        out_specs=[col, pl.BlockSpec((8, HD), lambda j, b: (0, j))],
        out_shape=[jax.ShapeDtypeStruct(dp.shape, dp.dtype), jax.ShapeDtypeStruct((8, 3 * H * HD), F32)],
        input_output_aliases={3: 0},
        compiler_params=_params(("parallel", "arbitrary")),
    )(p, cw, dqkv, dp)


def _gb_math(ab, alog, dtb, H):
    lane = lax.broadcasted_iota(jnp.int32, ab.shape, 1)
    g = -jnp.exp(alog) * _softplus(ab + dtb)
    return jnp.where(lane < 2 * H, g, jnp.where(lane < 4 * H, _sigmoid(ab), 0.0))


def _gb_fwd(p, prm, *, rows, col0, H, tm):
    def body(x_ref, prm_ref, o_ref):
        o_ref[...] = _gb_math(x_ref[...], prm_ref[0:1], prm_ref[1:2], H)

    return pl.pallas_call(
        body, name="gb_fwd", grid=(rows // tm,),
        in_specs=[pl.BlockSpec((tm, LANES), lambda t: (t, col0 // LANES)), pl.BlockSpec((8, LANES), lambda t: (0, 0))],
        out_specs=pl.BlockSpec((tm, LANES), lambda t: (t, 0)),
        out_shape=jax.ShapeDtypeStruct((rows, LANES), F32),
        compiler_params=_params(("parallel",)),
    )(p, prm)


def _gb_bwd(p, prm, dgb, dp, *, rows, col0, H, tm):
    def body(x_ref, prm_ref, d_ref, dp_any, dp_ref, dprm_ref):
        _, vjp = jax.vjp(lambda a, b, c: _gb_math(a, b, c, H), x_ref[...], prm_ref[0:1], prm_ref[1:2])
        dab, dalog, ddtb = vjp(d_ref[...])
        dp_ref[...] = dab.astype(BF16)

        @pl.when(pl.program_id(0) == 0)
        def _():
            dprm_ref[...] = jnp.zeros_like(dprm_ref)

        dprm_ref[0:1, :] += dalog
        dprm_ref[1:2, :] += ddtb

    blk = pl.BlockSpec((tm, LANES), lambda t: (t, col0 // LANES))
    return pl.pallas_call(
        body, name="gb_bwd", grid=(rows // tm,),
        in_specs=[blk, pl.BlockSpec((8, LANES), lambda t: (0, 0)), pl.BlockSpec((tm, LANES), lambda t: (t, 0)),
                  pl.BlockSpec(memory_space=pl.ANY)],
        out_specs=[blk, pl.BlockSpec((8, LANES), lambda t: (0, 0))],
        out_shape=[jax.ShapeDtypeStruct(dp.shape, dp.dtype), jax.ShapeDtypeStruct((8, LANES), F32)],
        input_output_aliases={3: 0},
        compiler_params=_params(("arbitrary",)),
    )(p, prm, dgb, dp)


def _lru_scans(scans):
    C = scans[0][0].shape[1]
    row = lax.broadcasted_iota(jnp.int32, (SUBLANES, C), 0)
    carries = tuple(jnp.zeros((1, C), F32) for _ in scans)
    for si in range(len(scans[0][4])):
        rows = scans[0][4][si][1]
        assert all(sc[4][si][1] == rows for sc in scans)
        sub = max(s for s in (4, 2, 1) if rows % (s * SUBLANES) == 0)
        span = sub * SUBLANES
        nb = rows // span

        def blk(i, carries, si=si, nb=nb, sub=sub, span=span):
            out = []
            for (a_ref, b_ref, h_ref, hp_ref, segs), carry in zip(scans, carries):
                start, _, reverse = segs[si]
                r0 = pl.multiple_of(start + (nb - 1 - i if reverse else i) * span, span)
                local = []
                for j in range(sub):
                    A = a_ref[pl.ds(r0 + j * SUBLANES, SUBLANES), :]
                    Bv = b_ref[pl.ds(r0 + j * SUBLANES, SUBLANES), :]
                    for s in (1, 2, 4):
                        sh = SUBLANES - s if reverse else s
                        m = (row < SUBLANES - s) if reverse else (row >= s)
                        Bv = jnp.where(m, A * pltpu.roll(Bv, sh, 0) + Bv, Bv)
                        A = jnp.where(m, A * pltpu.roll(A, sh, 0), A)
                    local.append((A, Bv))
                for j in (reversed(range(sub)) if reverse else range(sub)):
                    A, Bv = local[j]
                    Hv = Bv + A * carry
                    h_ref[pl.ds(r0 + j * SUBLANES, SUBLANES), :] = Hv
                    if hp_ref is not None:
                        if reverse:
                            hp = jnp.where(row < SUBLANES - 1, pltpu.roll(Hv, SUBLANES - 1, 0), carry)
                        else:
                            hp = jnp.where(row >= 1, pltpu.roll(Hv, 1, 0), carry)
                        hp_ref[pl.ds(r0 + j * SUBLANES, SUBLANES), :] = hp
                    carry = Hv[0:1] if reverse else Hv[SUBLANES - 1:SUBLANES]
                out.append(carry)
            return tuple(out)

        carries = lax.fori_loop(0, nb, blk, carries)


def _lru_orders(T, nc, d):
    N = T - nc
    if d == 0:
        return [(0, nc, False), (nc, N, False)], [(nc, N, True), (0, nc, True)]
    return [(0, nc, True), (nc, N, True)], [(nc, N, False), (0, nc, False)]


def _bdot(a, b, dims=(((1,), (0,)), ((), ()))):
    return lax.dot_general(a.astype(BF16), b.astype(BF16), dims, preferred_element_type=F32)


_NT = (((1,), (1,)), ((), ()))
_TN = (((0,), (0,)), ((), ()))


def _blockdiag(w, C):
    nd, nb, bd, _ = w.shape
    per = C // bd
    out = jnp.einsum('dnpij,pq->dnpiqj', w.reshape(nd, nb // per, per, bd, bd), jnp.eye(per, dtype=w.dtype))
    return out.reshape(nd, nb // per, C, C)


def _blockdiag_extract(dw, bd):
    nd, nj, C, _ = dw.shape
    per = C // bd
    out = jnp.einsum('dnpiqj,pq->dnpij', dw.reshape(nd, nj, per, bd, per, bd), jnp.eye(per, dtype=dw.dtype))
    return out.reshape(nd, nj * per, bd, bd)


def _lru_fwd(p, cw, lv, wr, wi, *, B, T, nc, LW, col0, C):
    N = T - nc
    nj = LW // C

    def body(x_ref, cw_ref, lv_ref, wr_ref, wi_ref, o_ref, a_s, b_s, h_s):
        lv_ = lv_ref[...]
        xc = _conv_fwd(x_ref[:, 0:C], cw_ref[...], _seg_taps(T, nc, 4, 2)) + lv_[0:1]
        for d in (0, 1):
            r = _sigmoid(_bdot(xc, wr_ref[d, 0]) + lv_[1 + d:2 + d])
            i = _sigmoid(_bdot(xc, wi_ref[d, 0]) + lv_[3 + d:4 + d])
            la = -LRU_C * r * _softplus(-lv_[5 + d:6 + d])
            a_s[d] = jnp.exp(la)
            b_s[d] = jnp.sqrt(1.0 - jnp.exp(2.0 * la)) * i * xc
        _lru_scans([(a_s.at[d], b_s.at[d], h_s.at[d], None, _lru_orders(T, nc, d)[0]) for d in (0, 1)])
        o_ref[...] = ((h_s[0, nc:, :] + h_s[1, nc:, :]) * _gelu(x_ref[nc:, C:2 * C])).astype(BF16)

    return pl.pallas_call(
        body, name="lru_fwd", grid=(B, nj),
        in_specs=[pl.BlockSpec((T, 2 * C), lambda b, j: (b, col0 // (2 * C) + j)),
                  pl.BlockSpec((4, C), lambda b, j: (0, j)), pl.BlockSpec((8, C), lambda b, j: (0, j)),
                  pl.BlockSpec((2, 1, C, C), lambda b, j: (0, j, 0, 0)), pl.BlockSpec((2, 1, C, C), lambda b, j: (0, j, 0, 0))],
        out_specs=pl.BlockSpec((N, C), lambda b, j: (b, j)),
        out_shape=jax.ShapeDtypeStruct((B * N, LW), BF16),
        scratch_shapes=[pltpu.VMEM((2, T, C), F32)] * 3,
        compiler_params=_params(("parallel", "parallel")),
    )(p, cw, lv, wr, wi)


def _lru_bwd(p, cw, lv, wr, wi, dy, dp, *, B, T, nc, LW, col0, C):
    N = T - nc
    nj = LW // C

    def body(x_ref, cw_ref, lv_ref, wr_ref, wi_ref, dy_ref, dp_any, dp_ref, dcw_ref, dlv_ref, dwr_ref, dwi_ref,
             a_s, b_s, h_s, hp_s, mu_s, mup_s, dh_s, dxc_s):
        taps = _seg_taps(T, nc, 4, 2)
        lv_ = lv_ref[...]
        xl = x_ref[:, 0:C]
        xc = _conv_fwd(xl, cw_ref[...], taps) + lv_[0:1]
        gel, gelu_vjp = jax.vjp(_gelu, x_ref[nc:, C:2 * C])
        dh_s[0:nc, :] = jnp.zeros((nc, C), F32)
        dh_s[nc:, :] = dy_ref[...] * gel
        dxc_s[...] = jnp.zeros_like(dxc_s)

        @pl.when(pl.program_id(1) == 0)
        def _():
            dcw_ref[...] = jnp.zeros_like(dcw_ref)
            dlv_ref[...] = jnp.zeros_like(dlv_ref)
            dwr_ref[...] = jnp.zeros_like(dwr_ref)
            dwi_ref[...] = jnp.zeros_like(dwi_ref)

        def gates(d):
            lam = lv_[5 + d:6 + d]
            r = _sigmoid(_bdot(xc, wr_ref[d, 0]) + lv_[1 + d:2 + d])
            i = _sigmoid(_bdot(xc, wi_ref[d, 0]) + lv_[3 + d:4 + d])
            sp = _softplus(-lam)
            la = -LRU_C * r * sp
            e2 = jnp.exp(2.0 * la)
            return lam, r, i, sp, la, e2, jnp.sqrt(1.0 - e2)

        for d in (0, 1):
            _, _, i, _, la, _, mult = gates(d)
            a_s[d] = jnp.exp(la)
            b_s[d] = mult * i * xc
        _lru_scans([(a_s.at[d], b_s.at[d], h_s.at[d], hp_s.at[d], _lru_orders(T, nc, d)[0]) for d in (0, 1)])
        for d in (0, 1):
            b_s[d] = a_s[d] * dh_s[...]
        _lru_scans([(a_s.at[d], b_s.at[d], mu_s.at[d], mup_s.at[d], _lru_orders(T, nc, d)[1]) for d in (0, 1)])

        for d in (0, 1):
            lam, r, i, sp, la, e2, mult = gates(d)
            a = a_s[d]
            dinp = dh_s[...] + mup_s[d]
            da = dinp * hp_s[d]
            dmult = dinp * i * xc
            di = dinp * mult * xc
            dla = da * a - dmult * e2 / mult
            dpre_r = (dla * (-LRU_C * sp)) * r * (1.0 - r)
            dpre_i = di * i * (1.0 - i)
            dsp = jnp.sum(dla * (-LRU_C * r), axis=0, keepdims=True)
            dxc_s[...] += dinp * mult * i + _bdot(dpre_r, wr_ref[d, 0], _NT) + _bdot(dpre_i, wi_ref[d, 0], _NT)
            dwr_ref[d, 0] += _bdot(xc, dpre_r, _TN)
            dwi_ref[d, 0] += _bdot(xc, dpre_i, _TN)
            dlv_ref[1 + d:2 + d, :] += jnp.sum(dpre_r, axis=0, keepdims=True)
            dlv_ref[3 + d:4 + d, :] += jnp.sum(dpre_i, axis=0, keepdims=True)
            dlv_ref[5 + d:6 + d, :] += -dsp * _sigmoid(-lam)

        dxc = dxc_s[...]
        dxl, dw = _conv_bwd(xl, cw_ref[...], taps, dxc)
        dcw_ref[0:4, :] += dw
        dlv_ref[0:1, :] += jnp.sum(dxc, axis=0, keepdims=True)
        dp_ref[:, 0:C] = dxl.astype(BF16)
        (dyl,) = gelu_vjp(dy_ref[...] * (h_s[0, nc:, :] + h_s[1, nc:, :]))
        dp_ref[0:nc, C:2 * C] = jnp.zeros((nc, C), BF16)
        dp_ref[nc:, C:2 * C] = dyl.astype(BF16)

    xblk = pl.BlockSpec((T, 2 * C), lambda j, b: (b, col0 // (2 * C) + j))
    wblk = pl.BlockSpec((2, 1, C, C), lambda j, b: (0, j, 0, 0))
    vblk = pl.BlockSpec((8, C), lambda j, b: (0, j))
    return pl.pallas_call(
        body, name="lru_bwd", grid=(nj, B),
        in_specs=[xblk, pl.BlockSpec((4, C), lambda j, b: (0, j)), vblk, wblk, wblk,
                  pl.BlockSpec((N, C), lambda j, b: (b, j)), pl.BlockSpec(memory_space=pl.ANY)],
        out_specs=[xblk, vblk, vblk, wblk, wblk],
        out_shape=[jax.ShapeDtypeStruct(dp.shape, dp.dtype), jax.ShapeDtypeStruct((8, LW), F32),
                   jax.ShapeDtypeStruct((8, LW), F32), jax.ShapeDtypeStruct((2, nj, C, C), F32),
                   jax.ShapeDtypeStruct((2, nj, C, C), F32)],
        scratch_shapes=[pltpu.VMEM((2, T, C), F32)] * 6 + [pltpu.VMEM((T, C), F32)] * 2,
        input_output_aliases={6: 0},
        compiler_params=_params(("parallel", "arbitrary")),
    )(p, cw, lv, wr, wi, dy, dp)


def _chunk_masks(upper):
    i = lax.broadcasted_iota(jnp.int32, (CHUNK, CHUNK), 0)
    j = lax.broadcasted_iota(jnp.int32, (CHUNK, CHUNK), 1)
    ahead = jnp.where(upper, j - i, i - j)
    return i == j, ahead >= 0, ahead > 0


def _col2row(c, eye):
    return jnp.sum(jnp.where(eye, c, 0.0), axis=0, keepdims=True)


def _row2col(r, eye):
    return jnp.sum(jnp.where(eye, r, 0.0), axis=1, keepdims=True)


def _rowsum(x):
    return jnp.sum(x, axis=1, keepdims=True)


_INV_BASE = 8


def _unit_tri_inverses(Ls):
    G = len(Ls)
    W = G * CHUNK
    blk = (lax.broadcasted_iota(jnp.int32, (W, W), 0) // CHUNK) == (lax.broadcasted_iota(jnp.int32, (W, W), 1) // CHUNK)
    ri = lax.broadcasted_iota(jnp.int32, (CHUNK, W), 0)
    ci = lax.broadcasted_iota(jnp.int32, (CHUNK, W), 1) % CHUNK

    def bd(b):
        return jnp.where(blk, jnp.tile(b, (G, 1)), jnp.zeros((), BF16))

    def pdot(a, b):
        return jnp.dot(a.astype(BF16), bd(b.astype(BF16)), preferred_element_type=F32)

    Lc = Ls[0] if G == 1 else jnp.concatenate(Ls, axis=1)
    s = _INV_BASE
    Xp = -jnp.where(ri // s == ci // s, Lc, 0.0)
    Rm = Xp
    for _ in range(int(math.log2(s)) - 1):
        Xp = pdot(Xp, Xp)
        Rm = Rm + Xp + pdot(Rm, Xp)
    while s < CHUNK:
        E = jnp.where((ri // (2 * s) == ci // (2 * s)) & (ri // s != ci // s), Lc, 0.0)
        DE = E + pdot(Rm, E)
        Rm = Rm - (DE + pdot(DE, Rm))
        s *= 2
    eye = _chunk_masks(False)[0]
    return [jnp.where(eye, 1.0, 0.0) + Rm[:, g * CHUNK:(g + 1) * CHUNK] for g in range(G)]


def _delta_chunk_common(q, k, v, gcol, bcol, upper):
    eye, incl, strict = _chunk_masks(upper)
    gc = _rowsum(jnp.where(incl, _col2row(gcol, eye), 0.0))
    D = jnp.where(incl, jnp.exp(jnp.minimum(gc - _col2row(gc, eye), 0.0)), 0.0)
    kb = k * bcol
    AP = _bdot(jnp.concatenate([kb, q], axis=0), k, _NT)
    A = AP[:CHUNK]
    L = jnp.where(strict, A * D, 0.0)
    eg = jnp.exp(gc)
    gl = jnp.sum(gcol, axis=0, keepdims=True)
    attn = jnp.where(incl, AP[CHUNK:] * D, 0.0)
    return dict(eye=eye, incl=incl, strict=strict, gc=gc, D=D, kb=kb, A=A, L=L, eg=eg, gl=gl, egl=jnp.exp(gl),
                attn=attn, kbe=kb * eg, vb=v * bcol, qe=q * eg, kd=k * jnp.exp(gl - gc))


def _delta_group_pre(chunks, upper):
    cs = [_delta_chunk_common(*ch, upper) for ch in chunks]
    out = []
    for c, Tm in zip(cs, _unit_tri_inverses([c["L"] for c in cs])):
        dk = c["kbe"].shape[1]
        wu = _bdot(Tm, jnp.concatenate([c["kbe"], c["vb"]], axis=1))
        KN = _bdot(c["kd"], wu, _TN)
        QO = _bdot(c["attn"], wu)
        out.append((Tm, KN[:, :dk], KN[:, dk:], c["qe"] - QO[:, :dk], QO[:, dk:], c["egl"]))
    return out


def _delta_chunk_bwd(q, k, v, gcol, bcol, S, Tm, do, dS2, upper):
    c = _delta_chunk_common(q, k, v, gcol, bcol, upper)
    eye, incl, strict, D, eg, egl = c["eye"], c["incl"], c["strict"], c["D"], c["eg"], c["egl"]
    kb, kbe, vb, qe, kd, attn = c["kb"], c["kbe"], c["vb"], c["qe"], c["kd"], c["attn"]
    dkk = kbe.shape[1]
    wu = _bdot(Tm, jnp.concatenate([kbe, vb], axis=1))
    w = wu[:, :dkk]
    vn = wu[:, dkk:] - _bdot(w, S)
    dvn = _bdot(kd, dS2) + _bdot(attn, do, _TN)
    dkd = _bdot(vn, dS2, _NT)
    dgl = jnp.sum(_rowsum(dS2 * S), axis=0, keepdims=True) * egl
    dqa = _bdot(do, jnp.concatenate([S, vn], axis=0), _NT)
    dqe = dqa[:, :dkk]
    dattn = jnp.where(incl, dqa[:, dkk:], 0.0)
    dw = -_bdot(dvn, S, _NT)
    r = _rowsum(dkd * kd)
    dk = dkd * jnp.exp(c["gl"] - c["gc"])
    dgl = dgl + jnp.sum(r, axis=0, keepdims=True)
    dgc = _rowsum(dqe * qe) - r
    E = dattn * attn
    dvw = jnp.concatenate([dvn, dw], axis=1)
    dTm = _bdot(dvw, jnp.concatenate([vb, kbe], axis=1), _NT)
    dvk = _bdot(Tm, dvw, _TN)
    dvb = dvk[:, :dvn.shape[1]]
    dv = dvb * bcol
    dbeta = _rowsum(dvb * v)
    dkbe = dvk[:, dvn.shape[1]:]
    dkb = dkbe * eg
    dgc = dgc + _rowsum(dkbe * kbe)
    dL = jnp.where(strict, -_bdot(Tm, _bdot(dTm, Tm, _NT), _TN), 0.0)
    dA = dL * D
    E = E + dL * c["L"]
    PA = jnp.concatenate([dattn * D, dA], axis=0)
    PAk = _bdot(PA, k)
    dq = dqe * eg + PAk[:CHUNK]
    dkb = dkb + PAk[CHUNK:]
    dk = dk + _bdot(PA, jnp.concatenate([q, kb], axis=0), _TN) + dkb * bcol
    dbeta = dbeta + _rowsum(dkb * k)
    dgc = dgc + _rowsum(E) - _row2col(jnp.sum(E, axis=0, keepdims=True), eye)
    dg = _row2col(jnp.sum(jnp.where(incl, dgc, 0.0), axis=0, keepdims=True), eye) + dgl
    return dq, dk, dv, dg, dbeta


def _delta_unroll(trips):
    return max(u for u in (3, 2, 1) if trips % u == 0)


def _delta_group(n):
    return max(g for g in range(1, 2 * LANES // CHUNK + 1) if n % g == 0)


def _delta_chunk_at(T, nc, d, i):
    n, ncc = T // CHUNK, nc // CHUNK
    desc = jnp.where(i < ncc, ncc - 1 - i, n - 1 - (i - ncc))
    if isinstance(d, int):
        return i if d == 0 else desc
    return jnp.where(d == 0, i, desc)


def _dn_out_math(o, onorm, z):
    return _rmsn(o, onorm) * _silu(z)


def _delta_fwd(qkv, gb, p, onorm, *, B, T, nc, H, HD):
    N = T - nc
    n = T // CHUNK
    G = _delta_group(n)

    def body(q_ref, k_ref, v_ref, gb_ref, z_ref, on_ref, y_ref, o_ref, Tm_ref, K_ref, S_ref, Qp_ref, eg_ref,
             N_s, O0_s, o_s):
        h = pl.program_id(1)
        lane = lax.broadcasted_iota(jnp.int32, (CHUNK, LANES), 1)

        def pre(g, carry):
            cs = [g * G + i for i in range(G)]
            rows = [pl.ds(pl.multiple_of(c * CHUNK, CHUNK), CHUNK) for c in cs]
            for d in (0, 1):
                chunks = []
                for r in rows:
                    gbb = gb_ref[r, :]
                    chunks.append((q_ref[r, :], k_ref[r, :], v_ref[r, :],
                                   _rowsum(jnp.where(lane == d * H + h, gbb, 0.0)),
                                   _rowsum(jnp.where(lane == 2 * H + d * H + h, gbb, 0.0))))
                for c, r, (Tm, K, Nn, Qp, O0, egl) in zip(cs, rows, _delta_group_pre(chunks, d == 1)):
                    Tm_ref[0, d * n + c] = Tm
                    K_ref[0, d * n + c] = K.astype(BF16)
                    N_s[d * n + c] = Nn
                    Qp_ref[0, d, r, :] = Qp.astype(BF16)
                    O0_s[d, r, :] = O0
                    eg_ref[0, d * n + c] = jnp.broadcast_to(egl, (SUBLANES, HD))
            return carry

        lax.fori_loop(0, n // G, pre, 0)

        def step(i, Ss):
            out = []
            for d in (0, 1):
                c = _delta_chunk_at(T, nc, d, i)
                rows = pl.ds(pl.multiple_of(c * CHUNK, CHUNK), CHUNK)
                S_ref[0, d * n + c] = Ss[d]
                Sb = Ss[d].astype(BF16)
                o_s[d, rows, :] = jnp.dot(Qp_ref[0, d, rows, :], Sb, preferred_element_type=F32) + O0_s[d, rows, :]
                out.append(eg_ref[0, d * n + c][0:1] * Ss[d] + N_s[d * n + c]
                           - jnp.dot(K_ref[0, d * n + c], Sb, preferred_element_type=F32))
            return tuple(out)

        lax.fori_loop(0, n, step, (jnp.zeros((HD, HD), F32), jnp.zeros((HD, HD), F32)))
        o = o_s[0, nc:, :] + o_s[1, nc:, :]
        o_ref[...] = o
        y_ref[...] = _dn_out_math(o, on_ref[...], z_ref[nc:, :]).astype(BF16)

    col = lambda off: pl.BlockSpec((T, HD), lambda b, h: (b, off + h))
    lat = pl.BlockSpec((N, HD), lambda b, h: (b, h))
    per = lambda *blk: pl.BlockSpec((1, *blk), lambda b, h: (b * H + h, 0, 0, 0))
    return pl.pallas_call(
        body, name="delta_fwd", grid=(B, H),
        in_specs=[col(0), col(H), col(2 * H), pl.BlockSpec((T, LANES), lambda b, h: (b, 0)), col(3 * H),
                  pl.BlockSpec((1, HD), lambda b, h: (0, 0))],
        out_specs=[lat, lat, per(2 * n, CHUNK, CHUNK), per(2 * n, HD, HD), per(2 * n, HD, HD), per(2, T, HD),
                   per(2 * n, SUBLANES, HD)],
        out_shape=[jax.ShapeDtypeStruct((B * N, H * HD), BF16), jax.ShapeDtypeStruct((B * N, H * HD), F32),
                   jax.ShapeDtypeStruct((B * H, 2 * n, CHUNK, CHUNK), F32),
                   jax.ShapeDtypeStruct((B * H, 2 * n, HD, HD), BF16), jax.ShapeDtypeStruct((B * H, 2 * n, HD, HD), F32),
                   jax.ShapeDtypeStruct((B * H, 2, T, HD), BF16), jax.ShapeDtypeStruct((B * H, 2 * n, SUBLANES, HD), F32)],
        scratch_shapes=[pltpu.VMEM((2 * n, HD, HD), F32), pltpu.VMEM((2, T, HD), F32), pltpu.VMEM((2, T, HD), F32)],
        compiler_params=_params(("parallel", "parallel")),
    )(qkv, qkv, qkv, gb, p, onorm)


def _delta_bwd(qkv, gb, p, onorm, o, res, dy, dp, *, B, T, nc, H, HD):
    N = T - nc
    n = T // CHUNK

    def body(q_ref, k_ref, v_ref, gb_ref, z_ref, on_ref, o_ref, dy_ref, Tm_ref, K_ref, S_ref, Qp_ref, eg_ref, dp_any,
             dqkv_ref, dgb_ref, dp_ref, don_ref, do_s, R_s, dS_s):
        h, d = pl.program_id(1), pl.program_id(2)
        lane = lax.broadcasted_iota(jnp.int32, (CHUNK, LANES), 1)

        @pl.when(d == 0)
        def _():
            _, vjp = jax.vjp(_dn_out_math, o_ref[...], on_ref[...], z_ref[nc:, :])
            do, don, dz = vjp(dy_ref[...])
            do_s[0:nc, :] = jnp.zeros((nc, HD), F32)
            do_s[nc:, :] = do
            dp_ref[0:nc, :] = jnp.zeros((nc, HD), BF16)
            dp_ref[nc:, :] = dz.astype(BF16)
            dqkv_ref[...] = jnp.zeros_like(dqkv_ref)

            @pl.when(h == 0)
            def _():
                don_ref[...] = jnp.zeros_like(don_ref)
                dgb_ref[...] = jnp.zeros_like(dgb_ref)

            don_ref[0, 0:1, :] += don

        def r_of(c, carry):
            rows = pl.ds(pl.multiple_of(c * CHUNK, CHUNK), CHUNK)
            R_s[c] = lax.dot_general(Qp_ref[0, 0, rows, :], do_s[rows, :].astype(BF16), _TN, preferred_element_type=F32)
            return carry

        lax.fori_loop(0, n, r_of, 0)

        def bwd_step(i, dS):
            c = _delta_chunk_at(T, nc, d, n - 1 - i)
            dS_s[c] = dS
            return (eg_ref[0, c][0:1] * dS + R_s[c]
                    - lax.dot_general(K_ref[0, c], dS.astype(BF16), _TN, preferred_element_type=F32))

        lax.fori_loop(0, n, bwd_step, jnp.zeros((HD, HD), F32))

        def grads(c, carry):
            rows = pl.ds(pl.multiple_of(c * CHUNK, CHUNK), CHUNK)
            gbb = gb_ref[rows, :]
            gcol = _rowsum(jnp.where(lane == d * H + h, gbb, 0.0))
            bcol = _rowsum(jnp.where(lane == 2 * H + d * H + h, gbb, 0.0))
            dq, dk, dv, dg, dbeta = _delta_chunk_bwd(q_ref[rows, :], k_ref[rows, :], v_ref[rows, :], gcol, bcol,
                                                     S_ref[0, c], Tm_ref[0, c], do_s[rows, :], dS_s[c], d == 1)
            dqkv_ref[0, rows, :] += dq
            dqkv_ref[1, rows, :] += dk
            dqkv_ref[2, rows, :] += dv
            dgb_ref[rows, :] += (jnp.where(lane == d * H + h, dg, 0.0)
                                 + jnp.where(lane == 2 * H + d * H + h, dbeta, 0.0))
            return carry

        lax.fori_loop(0, n, grads, 0, unroll=_delta_unroll(n))

    col = lambda off: pl.BlockSpec((T, HD), lambda b, h, d: (b, off + h))
    lat = pl.BlockSpec((N, HD), lambda b, h, d: (b, h))
    per = lambda *blk: pl.BlockSpec((1, *blk), lambda b, h, d: (b * H + h, d, 0, 0))
    return pl.pallas_call(
        body, name="delta_bwd", grid=(B, H, 2),
        in_specs=[col(0), col(H), col(2 * H), pl.BlockSpec((T, LANES), lambda b, h, d: (b, 0)), col(3 * H),
                  pl.BlockSpec((1, HD), lambda b, h, d: (0, 0)), lat, lat,
                  per(n, CHUNK, CHUNK), per(n, HD, HD), per(n, HD, HD), per(1, T, HD), per(n, SUBLANES, HD),
                  pl.BlockSpec(memory_space=pl.ANY)],
        out_specs=[pl.BlockSpec((3, T, HD), lambda b, h, d: (0, b, h)), pl.BlockSpec((T, LANES), lambda b, h, d: (b, 0)),
                   col(3 * H), pl.BlockSpec((1, 8, HD), lambda b, h, d: (b, 0, 0))],
        out_shape=[jax.ShapeDtypeStruct((3, B * T, H * HD), F32), jax.ShapeDtypeStruct((B * T, LANES), F32),
                   jax.ShapeDtypeStruct(dp.shape, dp.dtype), jax.ShapeDtypeStruct((B, 8, HD), F32)],
        scratch_shapes=[pltpu.VMEM((T, HD), F32), pltpu.VMEM((n, HD, HD), F32), pltpu.VMEM((n, HD, HD), F32)],
        input_output_aliases={13: 2},
        compiler_params=_params(("parallel", "arbitrary", "arbitrary")),
    )(qkv, qkv, qkv, gb, p, onorm, o, dy, *res, dp)


def _rowwise(fn, ins, out_dtypes, *, name, tm=256, mult=16):
    R, W = ins[0].shape
    tm = _tile(R, tm, mult)

    def body(*refs):
        outs = fn(*[r[...] for r in refs[:len(ins)]])
        for o_ref, o in zip(refs[len(ins):], outs):
            o_ref[...] = o.astype(o_ref.dtype)

    spec = pl.BlockSpec((tm, W), lambda i: (i, 0))
    return pl.pallas_call(
        body, name=name, grid=(R // tm,), in_specs=[spec] * len(ins), out_specs=[spec] * len(out_dtypes),
        out_shape=[jax.ShapeDtypeStruct((R, W), dt) for dt in out_dtypes],
        compiler_params=_params(("parallel",)),
    )(*ins)


def _sum_lead(x, *, name, tm=256, mult=16):
    S, R, W = x.shape
    tm = _tile(R, tm, mult)

    def body(*refs):
        acc = refs[0][0].astype(F32)
        for r in refs[1:S]:
            acc = acc + r[0].astype(F32)
        refs[S][...] = acc

    return pl.pallas_call(
        body, name=name, grid=(R // tm,),
        in_specs=[pl.BlockSpec((1, tm, W), functools.partial(lambda s, i: (s, i, 0), s)) for s in range(S)],
        out_specs=pl.BlockSpec((tm, W), lambda i: (i, 0)),
        out_shape=jax.ShapeDtypeStruct((R, W), F32),
        compiler_params=_params(("parallel",)),
    )(*([x] * S))


def _adamw_math(w, g, m, v):
    m = ADAM_B1 * m + (1.0 - ADAM_B1) * g
    v = ADAM_B2 * v + (1.0 - ADAM_B2) * (g * g)
    m_hat = m / (1.0 - ADAM_B1 ** ADAM_STEP)
    v_hat = v / (1.0 - ADAM_B2 ** ADAM_STEP)
    return -ADAM_LR * (m_hat / (jnp.sqrt(v_hat) + ADAM_EPS) + ADAM_WD * w), m, v


def _adamw(w, g, m, v, *, name):
    tm = max(SUBLANES, (256 * 1024) // w.shape[1] // SUBLANES * SUBLANES)
    return _rowwise(_adamw_math, [w, g, m, v], [F32, F32, F32], name=name, tm=tm, mult=SUBLANES)


def _me():
    return lax.axis_index("x"), lax.axis_index("y"), lax.axis_index("c")


def _allgather_small(v):
    R, W = v.shape

    def body(x_ref, out_ref, send_sems, recv_sems, local_sem):
        x, y, c = _me()
        me, sibling = (x, y, c), (x, y, 1 - c)
        chips = [(1 - x, y), (x, 1 - y), (1 - x, 1 - y)]

        def slot(px, py, pc):
            return out_ref.at[4 * px + 2 * py + pc]

        def copy(k, block, to, src=None):
            return pltpu.make_async_remote_copy(
                src_ref=slot(*block) if src is None else src, dst_ref=slot(*block),
                send_sem=send_sems.at[k], recv_sem=recv_sems.at[k], device_id=to, device_id_type=MESH)

        mine = pltpu.make_async_copy(x_ref, slot(*me), local_sem)
        mine.start()
        first = [copy(0, me, sibling, src=x_ref)]
        first += [copy(1 + j, me, (*chip, c), src=x_ref) for j, chip in enumerate(chips)]
        for cp in first:
            cp.start()
        passed = [copy(4 + j, (*chip, c), sibling) for j, chip in enumerate(chips)]
        for j, chip in enumerate(chips):
            copy(1 + j, (*chip, c), me).wait_recv()
            passed[j].start()
        copy(0, sibling, me).wait_recv()
        for j, chip in enumerate(chips):
            copy(4 + j, (*chip, 1 - c), me).wait_recv()
        for cp in first + passed:
            cp.wait_send()
        mine.wait()

    return pl.pallas_call(
        body, name="allgather_small", out_shape=jax.ShapeDtypeStruct((8, R, W), v.dtype),
        in_specs=[pl.BlockSpec(memory_space=pltpu.VMEM)], out_specs=pl.BlockSpec(memory_space=pltpu.VMEM),
        scratch_shapes=[pltpu.SemaphoreType.DMA((7,)), pltpu.SemaphoreType.DMA((7,)), pltpu.SemaphoreType.DMA],
        compiler_params=_params(),
    )(v)


_ANY = pl.BlockSpec(memory_space=pl.ANY)


def _allgather_halves(shards, *, name):
    nw = len(shards)

    def body(*refs):
        x_refs, out_refs = refs[:nw], refs[nw:2 * nw]
        send_sems, recv_sems, local_sems = refs[2 * nw:]
        x, y, c = _me()
        me, sibling = (x, y, c), (x, y, 1 - c)
        chips = [(1 - x, y), (x, 1 - y), (1 - x, 1 - y)]

        def slot(w, px, py, pc):
            return out_refs[w].at[4 * px + 2 * py + pc]

        def copy(w, k, block, to, src=None):
            return pltpu.make_async_remote_copy(
                src_ref=slot(w, *block) if src is None else src, dst_ref=slot(w, *block),
                send_sem=send_sems.at[w, k], recv_sem=recv_sems.at[w, k], device_id=to, device_id_type=MESH)

        started, local = [], []
        for w in range(nw):
            half = shards[w].shape[0] // 2
            own = x_refs[w].at[pl.ds(c * half, half), :]
            mine = pltpu.make_async_copy(own, slot(w, *me), local_sems.at[w])
            mine.start()
            first = [copy(w, 0, me, sibling, src=own)]
            first += [copy(w, 1 + j, me, (*chip, c), src=own) for j, chip in enumerate(chips)]
            for cp in first:
                cp.start()
            started += first
            local.append(mine)
        for w in range(nw):
            for j, chip in enumerate(chips):
                copy(w, 1 + j, (*chip, c), me).wait_recv()
                fwd = copy(w, 4 + j, (*chip, c), sibling)
                fwd.start()
                started.append(fwd)
        for w in range(nw):
            copy(w, 0, sibling, me).wait_recv()
            for j, chip in enumerate(chips):
                copy(w, 4 + j, (*chip, 1 - c), me).wait_recv()
        for cp in started:
            cp.wait_send()
        for cp in local:
            cp.wait()

    return pl.pallas_call(
        body, name=name,
        out_shape=[jax.ShapeDtypeStruct((8, s.shape[0] // 2, s.shape[1]), s.dtype) for s in shards],
        in_specs=[_ANY] * nw, out_specs=[_ANY] * nw,
        scratch_shapes=[pltpu.SemaphoreType.DMA((nw, 7)), pltpu.SemaphoreType.DMA((nw, 7)), pltpu.SemaphoreType.DMA((nw,))],
        compiler_params=_params(),
    )(*shards)


def _sibling_send_halves(arrs, *, name):
    nw = len(arrs)

    def body(*refs):
        x_refs, out_refs, send_sems, recv_sems = refs[:nw], refs[nw:2 * nw], refs[2 * nw], refs[2 * nw + 1]
        x, y, c = _me()
        cps = []
        for w in range(nw):
            half = arrs[w].shape[1] // 2
            cp = pltpu.make_async_remote_copy(
                src_ref=x_refs[w].at[:, pl.ds((1 - c) * half, half), :], dst_ref=out_refs[w],
                send_sem=send_sems.at[w], recv_sem=recv_sems.at[w], device_id=(x, y, 1 - c), device_id_type=MESH)
            cp.start()
            cps.append(cp)
        for cp in cps:
            cp.wait()

    return pl.pallas_call(
        body, name=name,
        out_shape=[jax.ShapeDtypeStruct((a.shape[0], a.shape[1] // 2, a.shape[2]), a.dtype) for a in arrs],
        in_specs=[_ANY] * nw, out_specs=[_ANY] * nw,
        scratch_shapes=[pltpu.SemaphoreType.DMA((nw,)), pltpu.SemaphoreType.DMA((nw,))],
        compiler_params=_params(),
    )(*arrs)


def _sibling_swap(arrs, *, name):
    nw = len(arrs)

    def body(*refs):
        x_refs, out_refs, send_sems, recv_sems = refs[:nw], refs[nw:2 * nw], refs[2 * nw], refs[2 * nw + 1]
        x, y, c = _me()
        cps = []
        for w in range(nw):
            cp = pltpu.make_async_remote_copy(
                src_ref=x_refs[w], dst_ref=out_refs[w], send_sem=send_sems.at[w], recv_sem=recv_sems.at[w],
                device_id=(x, y, 1 - c), device_id_type=MESH)
            cp.start()
            cps.append(cp)
        for cp in cps:
            cp.wait()

    return pl.pallas_call(
        body, name=name, out_shape=[jax.ShapeDtypeStruct(a.shape, a.dtype) for a in arrs],
        in_specs=[_ANY] * nw, out_specs=[_ANY] * nw,
        scratch_shapes=[pltpu.SemaphoreType.DMA((nw,)), pltpu.SemaphoreType.DMA((nw,))],
        compiler_params=_params(),
    )(*arrs)


def _adamw_halves(w, own, sib, m, v, c_arr, *, name):
    r, cols = w.shape
    h = r // 2
    tm = _tile(h, max(SUBLANES, (192 * 1024) // cols // SUBLANES * SUBLANES), SUBLANES)
    nb = h // tm

    def body(c_ref, w_ref, own_ref, sib_ref, m_ref, v_ref, g_out, d_out, m_out, v_out):
        g = jnp.where(pl.program_id(0) == c_ref[0], own_ref[...], sib_ref[...])
        g_out[...] = g
        d_out[...], m_out[...], v_out[...] = _adamw_math(w_ref[...], g, m_ref[...], v_ref[...])

    full = pl.BlockSpec((tm, cols), lambda hh, i, c_ref: (hh * nb + i, 0))
    half = pl.BlockSpec((tm, cols), lambda hh, i, c_ref: (i, 0))
    return pl.pallas_call(
        body, name=name,
        grid_spec=pltpu.PrefetchScalarGridSpec(num_scalar_prefetch=1, grid=(2, nb),
                                               in_specs=[full, half, half, full, full], out_specs=[full] * 4),
        out_shape=[jax.ShapeDtypeStruct((r, cols), F32)] * 4,
        compiler_params=_params(("parallel", "parallel")),
    )(c_arr, w, own, sib, m, v)


_HBM = pl.BlockSpec(memory_space=pltpu.HBM)
_SEM = pl.BlockSpec(memory_space=pltpu.SEMAPHORE)
_DATAFLOW = pltpu.SideEffectType.DATAFLOW_SIDE_EFFECTING


def _chip_exchange_start(arrs, *, name):
    nw = len(arrs)

    def body(*refs):
        x_refs, land_refs, send_sems, recv_sems = refs[:nw], refs[nw:2 * nw], refs[2 * nw], refs[2 * nw + 1]
        token = refs[-1]
        x, y, c = _me()
        s_me = 2 * x + y
        for w in range(nw):
            for k, (px, py) in enumerate([(1 - x, y), (x, 1 - y), (1 - x, 1 - y)]):
                pltpu.make_async_remote_copy(
                    src_ref=x_refs[w].at[2 * px + py], dst_ref=land_refs[w].at[s_me], send_sem=send_sems.at[3 * w + k],
                    recv_sem=recv_sems.at[3 * w + k], device_id=(px, py, c), device_id_type=MESH).start()
        token[...] = jnp.zeros_like(token)

    hbm = [pltpu.HBM(a.shape, a.dtype) for a in arrs]
    outs = pl.pallas_call(
        body, name=name,
        out_shape=(pltpu.SemaphoreType.DMA((3 * nw,)), pltpu.SemaphoreType.DMA((3 * nw,)), *hbm, *hbm,
                   jax.ShapeDtypeStruct((SUBLANES, LANES), F32)),
        in_specs=[_HBM] * (2 * nw), out_specs=(_SEM, _SEM, *([_HBM] * (2 * nw)), pl.BlockSpec(memory_space=pltpu.VMEM)),
        input_output_aliases={i: 2 + i for i in range(2 * nw)},
        compiler_params=pltpu.CompilerParams(has_side_effects=_DATAFLOW),
    )(*[pltpu.with_memory_space_constraint(a, pltpu.HBM) for a in arrs],
      *[pltpu.with_memory_space_constraint(lax.empty(a.shape, a.dtype), pltpu.HBM) for a in arrs])
    return outs[0], outs[1], list(outs[2:2 + nw]), list(outs[2 + nw:2 + 2 * nw]), outs[-1]


def _allgather_start(shards, *, name):
    nw = len(shards)

    def body(*refs):
        x_refs, land_refs, send_sems, recv_sems = refs[:nw], refs[nw:2 * nw], refs[2 * nw], refs[2 * nw + 1]
        token = refs[-1]
        x, y, c = _me()
        me = 4 * x + 2 * y + c
        for w in range(nw):
            half = shards[w].shape[0] // 2
            own = x_refs[w].at[pl.ds(c * half, half), :]
            for k, to in enumerate([(x, y, 1 - c), (1 - x, y, c), (x, 1 - y, c), (1 - x, 1 - y, c)]):
                pltpu.make_async_remote_copy(
                    src_ref=own, dst_ref=land_refs[w].at[me], send_sem=send_sems.at[4 * w + k],
                    recv_sem=recv_sems.at[4 * w + k], device_id=to, device_id_type=MESH).start()
        token[...] = jnp.zeros_like(token)

    lands = [pltpu.HBM((8, s.shape[0] // 2, s.shape[1]), s.dtype) for s in shards]
    outs = pl.pallas_call(
        body, name=name,
        out_shape=(pltpu.SemaphoreType.DMA((4 * nw,)), pltpu.SemaphoreType.DMA((4 * nw,)),
                   *[pltpu.HBM(s.shape, s.dtype) for s in shards], *lands, jax.ShapeDtypeStruct((SUBLANES, LANES), F32)),
        in_specs=[_HBM] * (2 * nw), out_specs=(_SEM, _SEM, *([_HBM] * (2 * nw)), pl.BlockSpec(memory_space=pltpu.VMEM)),
        input_output_aliases={i: 2 + i for i in range(2 * nw)},
        compiler_params=pltpu.CompilerParams(has_side_effects=_DATAFLOW),
    )(*[pltpu.with_memory_space_constraint(s, pltpu.HBM) for s in shards],
      *[pltpu.with_memory_space_constraint(lax.empty(l.shape, l.dtype), pltpu.HBM) for l in lands])
    return outs[0], outs[1], list(outs[2:2 + nw]), list(outs[2 + nw:2 + 2 * nw]), outs[-1]


def _allgather_wait(send_sems, recv_sems, srcs, lands, after, *, name):
    nw = len(srcs)

    def body(*refs):
        x_refs, land_refs, send_sems, recv_sems = refs[:nw], refs[nw:2 * nw], refs[2 * nw], refs[2 * nw + 1]
        x, y, c = _me()
        for w in range(nw):
            half = srcs[w].shape[0] // 2
            own = x_refs[w].at[pl.ds(c * half, half), :]
            for k, (px, py, pc) in enumerate([(x, y, 1 - c), (1 - x, y, c), (x, 1 - y, c), (1 - x, 1 - y, c)]):
                cp = pltpu.make_async_remote_copy(
                    src_ref=own, dst_ref=land_refs[w].at[4 * px + 2 * py + pc], send_sem=send_sems.at[4 * w + k],
                    recv_sem=recv_sems.at[4 * w + k], device_id=(px, py, pc), device_id_type=MESH)
                cp.wait_send()
                cp.wait_recv()

    outs = pl.pallas_call(
        body, name=name,
        out_shape=(*[pltpu.HBM(a.shape, a.dtype) for a in srcs], *[pltpu.HBM(a.shape, a.dtype) for a in lands]),
        in_specs=[_HBM] * (2 * nw) + [_SEM, _SEM, _ANY], out_specs=tuple([_HBM] * (2 * nw)),
        input_output_aliases={i: i for i in range(2 * nw)},
        compiler_params=pltpu.CompilerParams(has_side_effects=_DATAFLOW),
    )(*srcs, *lands, send_sems, recv_sems, after)
    return list(outs[:nw]), list(outs[nw:])


def _pass_to_sibling(lands, *, name):
    nw = len(lands)

    def body(*refs):
        x_refs, out_refs, send_sems, recv_sems = refs[:nw], refs[nw:2 * nw], refs[2 * nw], refs[2 * nw + 1]
        x, y, c = _me()
        chips = [(1 - x, y), (x, 1 - y), (1 - x, 1 - y)]
        cps = []
        for w in range(nw):
            for k, (px, py) in enumerate(chips):
                cp = pltpu.make_async_remote_copy(
                    src_ref=x_refs[w].at[4 * px + 2 * py + c], dst_ref=out_refs[w].at[4 * px + 2 * py + c],
                    send_sem=send_sems.at[3 * w + k], recv_sem=recv_sems.at[3 * w + k], device_id=(x, y, 1 - c),
                    device_id_type=MESH)
                cp.start()
                cps.append(cp)
        for w in range(nw):
            for k, (px, py) in enumerate(chips):
                pltpu.make_async_remote_copy(
                    src_ref=x_refs[w].at[4 * px + 2 * py + c], dst_ref=out_refs[w].at[4 * px + 2 * py + 1 - c],
                    send_sem=send_sems.at[3 * w + k], recv_sem=recv_sems.at[3 * w + k], device_id=(x, y, 1 - c),
                    device_id_type=MESH).wait_recv()
        for cp in cps:
            cp.wait_send()

    return pl.pallas_call(
        body, name=name, out_shape=[jax.ShapeDtypeStruct(a.shape, a.dtype) for a in lands],
        in_specs=[_ANY] * nw, out_specs=[_ANY] * nw, input_output_aliases={i: i for i in range(nw)},
        scratch_shapes=[pltpu.SemaphoreType.DMA((3 * nw,)), pltpu.SemaphoreType.DMA((3 * nw,))],
        compiler_params=_params(),
    )(*lands)


def _chip_exchange_wait(send_sems, recv_sems, srcs, lands, after, *, name):
    nw = len(srcs)

    def body(*refs):
        x_refs, land_refs, send_sems, recv_sems = refs[:nw], refs[nw:2 * nw], refs[2 * nw], refs[2 * nw + 1]
        x, y, c = _me()
        for w in range(nw):
            for k, (px, py) in enumerate([(1 - x, y), (x, 1 - y), (1 - x, 1 - y)]):
                cp = pltpu.make_async_remote_copy(
                    src_ref=x_refs[w].at[2 * px + py], dst_ref=land_refs[w].at[2 * px + py], send_sem=send_sems.at[3 * w + k],
                    recv_sem=recv_sems.at[3 * w + k], device_id=(px, py, c), device_id_type=MESH)
                cp.wait_send()
                cp.wait_recv()

    hbm = [pltpu.HBM(a.shape, a.dtype) for a in srcs]
    outs = pl.pallas_call(
        body, name=name, out_shape=(*hbm, *hbm),
        in_specs=[_HBM] * (2 * nw) + [_SEM, _SEM, _ANY], out_specs=tuple([_HBM] * (2 * nw)),
        input_output_aliases={i: i for i in range(2 * nw)},
        compiler_params=pltpu.CompilerParams(has_side_effects=_DATAFLOW),
    )(*srcs, *lands, send_sems, recv_sems, after)
    return list(outs[:nw]), list(outs[nw:])


def _sum_slabs(landed, own_src, s_arr, after, *, name, tm=512):
    S, h, w = landed.shape
    tm = _tile(h, tm, 16)

    def body(s_ref, *refs):
        own = refs[S][0].astype(F32)
        acc = None
        for s in range(S):
            term = jnp.where(s_ref[0] == s, own, refs[s][0].astype(F32))
            acc = term if acc is None else acc + term
        refs[S + 2][...] = acc

    def slab(s):
        return pl.BlockSpec((1, tm, w), lambda i, s_ref: (jnp.where(s_ref[0] == s, (s + 1) % S, s), i, 0))

    return pl.pallas_call(
        body, name=name,
        grid_spec=pltpu.PrefetchScalarGridSpec(
            num_scalar_prefetch=1, grid=(h // tm,),
            in_specs=[slab(s) for s in range(S)] + [pl.BlockSpec((1, tm, w), lambda i, s_ref: (s_ref[0], i, 0)), _ANY],
            out_specs=pl.BlockSpec((tm, w), lambda i, s_ref: (i, 0))),
        out_shape=jax.ShapeDtypeStruct((h, w), F32),
        compiler_params=_params(("parallel",)),
    )(s_arr, *([landed] * S), own_src, after)


def _half_add(g, recv, c_arr, *, name):
    S, r, w = g.shape
    h = r // 2
    tm = _tile(h, 512, 16)
    nb = h // tm

    def body(c_ref, g_ref, r_ref, o_ref):
        o_ref[...] = (g_ref[...] + r_ref[...]).astype(BF16)

    return pl.pallas_call(
        body, name=name,
        grid_spec=pltpu.PrefetchScalarGridSpec(
            num_scalar_prefetch=1, grid=(S, nb),
            in_specs=[pl.BlockSpec((1, tm, w), lambda s, i, c_ref: (s, c_ref[0] * nb + i, 0)),
                      pl.BlockSpec((1, tm, w), lambda s, i, c_ref: (s, i, 0))],
            out_specs=pl.BlockSpec((1, tm, w), lambda s, i, c_ref: (s, i, 0))),
        out_shape=jax.ShapeDtypeStruct((S, h, w), BF16),
        compiler_params=_params(("parallel", "parallel")),
    )(c_arr, g, recv)


def _layout(sizes, width, part_mult, total_mult):
    offs, rows, r = [], [], 0
    for n in sizes:
        k = -(-n // width)
        offs.append(r)
        rows.append(k)
        r += -(-k // part_mult) * part_mult
    return offs, rows, -(-r // total_mult) * total_mult


def _pack(arrs, width, part_mult, total_mult, dtype, lead=()):
    nl = len(lead)
    sizes = [math.prod(a.shape[nl:]) for a in arrs]
    offs, rows, total = _layout(sizes, width, part_mult, total_mult)
    parts, r = [], 0
    for a, n, o, k in zip(arrs, sizes, offs, rows):
        kp = -(-k // part_mult) * part_mult
        flat = a.reshape(*lead, n).astype(dtype)
        if kp * width > n:
            flat = jnp.pad(flat, [(0, 0)] * nl + [(0, kp * width - n)])
        parts.append(flat.reshape(*lead, kp, width))
        r = o + kp
    if total > r:
        parts.append(jnp.zeros((*lead, total - r, width), dtype))
    return jnp.concatenate(parts, axis=nl)


def _unpack(pool, shapes, width, part_mult, total_mult):
    lead = pool.shape[:-2]
    sizes = [math.prod(s) for s in shapes]
    offs, rows, _ = _layout(sizes, width, part_mult, total_mult)
    out = []
    for s, n, o, k in zip(shapes, sizes, offs, rows):
        flat = lax.slice_in_dim(pool, o, o + k, axis=len(lead)).reshape(*lead, k * width)
        out.append(lax.slice_in_dim(flat, 0, n, axis=len(lead)).reshape(*lead, *s))
    return out


_WEIGHTS = ("c_ctx", "w_ada", "b_ada", "g_pre_mix", "g_post_mix", "g_pre_ffn", "g_post_ffn", "w_in", "b_merge",
            "dn_conv", "dn_a_log", "dn_dt_bias", "dn_onorm", "lru_conv", "lru_conv_b", "lru_w_rg", "lru_b_rg",
            "lru_w_ig", "lru_b_ig", "lru_lambda", "w_branch_dn", "w_branch_lru", "w_out", "w_up", "ffn_dw",
            "ffn_dw_b", "w_down")
_BIG = {"w_ada": True, "w_in": True, "w_branch_dn": False, "w_branch_lru": False, "w_out": False, "w_up": True,
        "w_down": False}
_SMALL_SHARDED = ("dn_conv", "lru_conv", "lru_b_rg", "lru_b_ig", "lru_lambda", "ffn_dw")
_NCHIP = 4
_FLAT_PART = 8
_FLAT_TOTAL = 256


def _to_chip_shards(g, by_cols):
    if by_cols:
        return g.reshape(g.shape[0], _NCHIP, g.shape[1] // _NCHIP).transpose(1, 0, 2)
    return g.reshape(_NCHIP, g.shape[0] // _NCHIP, g.shape[1])


def _from_chip_shards(s, by_cols):
    if by_cols:
        return s.transpose(1, 0, 2).reshape(s.shape[1], _NCHIP * s.shape[2])
    return s.reshape(_NCHIP * s.shape[1], s.shape[2])


def _dsilu(x):
    s = _sigmoid(x)
    return s * (1.0 + x * (1.0 - s))


def kernel(x, c, ctx, c_ctx, w_ada, b_ada, g_pre_mix, g_post_mix, g_pre_ffn, g_post_ffn, w_in, b_merge, dn_conv, dn_a_log, dn_dt_bias, dn_onorm, lru_conv, lru_conv_b, lru_w_rg, lru_b_rg, lru_w_ig, lru_b_ig, lru_lambda, w_branch_dn, w_branch_lru, w_out, w_up, ffn_dw, ffn_dw_b, w_down, loss_target, m_c_ctx, m_w_ada, m_b_ada, m_g_pre_mix, m_g_post_mix, m_g_pre_ffn, m_g_post_ffn, m_w_in, m_b_merge, m_dn_conv, m_dn_a_log, m_dn_dt_bias, m_dn_onorm, m_lru_conv, m_lru_conv_b, m_lru_w_rg, m_lru_b_rg, m_lru_w_ig, m_lru_b_ig, m_lru_lambda, m_w_branch_dn, m_w_branch_lru, m_w_out, m_w_up, m_ffn_dw, m_ffn_dw_b, m_w_down, v_c_ctx, v_w_ada, v_b_ada, v_g_pre_mix, v_g_post_mix, v_g_pre_ffn, v_g_post_ffn, v_w_in, v_b_merge, v_dn_conv, v_dn_a_log, v_dn_dt_bias, v_dn_onorm, v_lru_conv, v_lru_conv_b, v_lru_w_rg, v_lru_b_rg, v_lru_w_ig, v_lru_b_ig, v_lru_lambda, v_w_branch_dn, v_w_branch_lru, v_w_out, v_w_up, v_ffn_dw, v_ffn_dw_b, v_w_down):
    W = dict(zip(_WEIGHTS, (c_ctx, w_ada, b_ada, g_pre_mix, g_post_mix, g_pre_ffn, g_post_ffn, w_in, b_merge, dn_conv,
                            dn_a_log, dn_dt_bias, dn_onorm, lru_conv, lru_conv_b, lru_w_rg, lru_b_rg, lru_w_ig, lru_b_ig,
                            lru_lambda, w_branch_dn, w_branch_lru, w_out, w_up, ffn_dw, ffn_dw_b, w_down)))
    Mo = dict(zip(_WEIGHTS, (m_c_ctx, m_w_ada, m_b_ada, m_g_pre_mix, m_g_post_mix, m_g_pre_ffn, m_g_post_ffn, m_w_in,
                             m_b_merge, m_dn_conv, m_dn_a_log, m_dn_dt_bias, m_dn_onorm, m_lru_conv, m_lru_conv_b,
                             m_lru_w_rg, m_lru_b_rg, m_lru_w_ig, m_lru_b_ig, m_lru_lambda, m_w_branch_dn,
                             m_w_branch_lru, m_w_out, m_w_up, m_ffn_dw, m_ffn_dw_b, m_w_down)))
    Vo = dict(zip(_WEIGHTS, (v_c_ctx, v_w_ada, v_b_ada, v_g_pre_mix, v_g_post_mix, v_g_pre_ffn, v_g_post_ffn, v_w_in,
                             v_b_merge, v_dn_conv, v_dn_a_log, v_dn_dt_bias, v_dn_onorm, v_lru_conv, v_lru_conv_b,
                             v_lru_w_rg, v_lru_b_rg, v_lru_w_ig, v_lru_b_ig, v_lru_lambda, v_w_branch_dn,
                             v_w_branch_lru, v_w_out, v_w_up, v_ffn_dw, v_ffn_dw_b, v_w_down)))
    B, N, D = x.shape
    NC = ctx.shape[1]
    T = NC + N
    H, HD = dn_a_log.shape[-1], dn_onorm.shape[-1]
    DNW = H * HD
    LW, LBD = lru_conv_b.shape[-1], lru_w_rg.shape[-1]
    DFF = ffn_dw_b.shape[-1]
    LC = LANES
    x_i, y_i, c_i = _me()
    s_me = 2 * x_i + y_i
    tm = _tile(math.gcd(NC, N), 256, 16)

    def whole(n, g):
        r, w_ = W[n].shape[1:]
        return g.reshape(_NCHIP, r, w_) if _BIG[n] else g.reshape(_NCHIP * r, w_)

    first = ("w_ada", "w_in")
    later = tuple(n for n in _BIG if n not in first)
    shard16 = {n: W[n][0].astype(BF16) for n in _BIG}
    full = {n: whole(n, g) for n, g in zip(first, _allgather_halves([shard16[n] for n in first], name="allgather_first"))}

    small_local = [W[n][0].reshape(-1, W[n].shape[-1]) for n in _SMALL_SHARDED]
    small_shapes = [a.shape for a in small_local]
    spack = _pack(small_local, LANES, _FLAT_PART, _FLAT_PART, F32)
    sgath = _allgather_small(spack)[0::2]
    sfull = {n: _from_chip_shards(s, True)
             for n, s in zip(_SMALL_SHARDED, _unpack(sgath, small_shapes, LANES, _FLAT_PART, _FLAT_PART))}

    later16, sgath = lax.optimization_barrier(([shard16[n] for n in later], sgath))
    ag_send, ag_recv, ag_src, ag_land, ag_token = _allgather_start(later16, name="ag_start")

    o_a = 4 * DNW
    o_xl = o_a + 4 * H
    o_mg = o_xl + 2 * LW
    wi_ = _from_chip_shards(full["w_in"], True)
    nj = LW // LC
    lru_cols = jnp.stack([wi_[:, o_xl:o_xl + LW].reshape(D, nj, LC), wi_[:, o_xl + LW:o_mg].reshape(D, nj, LC)],
                         axis=2).reshape(D, 2 * LW)
    wp = jnp.concatenate([wi_[:, :o_a], lru_cols, wi_[:, o_mg:], wi_[:, o_a:o_xl],
                          jnp.zeros((D, LANES - 4 * H), BF16)], axis=1)
    p_lru, p_mg, p_ab = 4 * DNW, 4 * DNW + 2 * LW, 4 * DNW + 2 * LW + 2 * D
    PW = p_ab + LANES

    MR = LANES
    cond = jnp.concatenate([c, c_ctx[None], jnp.zeros((MR - B - 1, D), F32)], axis=0)
    silu_rows = _rowwise(lambda a: (_silu(a),), [cond], [F32], name="cond_silu")[0]
    mod = _matmul(silu_rows, full["w_ada"], b_shards=(0, _NCHIP), name="ada_fwd") + b_ada + ag_token[0, 0]
    mx = mod[:B].reshape(B, 6, D)
    mc = mod[B].reshape(6, D)
    zero = jnp.zeros((B, D), F32)
    tab = jnp.stack([jnp.stack([jnp.broadcast_to(mc[0], (B, D)), jnp.broadcast_to(mc[1], (B, D))] + [zero] * 6, axis=1),
                     jnp.stack([mx[:, 0], mx[:, 1]] + [zero] * 6, axis=1)], axis=1)
    vecs = jnp.stack([mx[:, 2], mx[:, 3], mx[:, 4], mx[:, 5]] + [zero] * 4, axis=1)
    gains = jnp.concatenate([g_post_mix, g_pre_ffn, g_post_ffn, jnp.zeros((5, D), F32)], axis=0)

    u = _premix_fwd(ctx, x, g_pre_mix, tab, tm=tm)
    p = _matmul(u, wp, name="in_fwd")
    dkw = dict(B=B, T=T, nc=NC, H=H, HD=HD)
    qkv = _dnprep_fwd(p, sfull["dn_conv"], **dkw)
    prm = jnp.concatenate([
        jnp.concatenate([dn_a_log.reshape(1, 2 * H), jnp.zeros((1, LANES - 2 * H), F32)], axis=1),
        jnp.concatenate([dn_dt_bias.reshape(1, 2 * H), jnp.zeros((1, LANES - 2 * H), F32)], axis=1),
        jnp.zeros((6, LANES), F32)], axis=0)
    gtm = _tile(B * T, 512, 16)
    gb = _gb_fwd(p, prm, rows=B * T, col0=p_ab, H=H, tm=gtm)
    y_dn, o_dn, *dn_res = _delta_fwd(qkv, gb, p, dn_onorm, **dkw)
    lv = jnp.concatenate([lru_conv_b, sfull["lru_b_rg"], sfull["lru_b_ig"], sfull["lru_lambda"], jnp.zeros((1, LW), F32)], axis=0)
    wr = _blockdiag(lru_w_rg[0], LC).astype(BF16)
    wi = _blockdiag(lru_w_ig[0], LC).astype(BF16)
    lkw = dict(B=B, T=T, nc=NC, LW=LW, col0=p_lru, C=LC)
    y_lru = _lru_fwd(p, sfull["lru_conv"], lv, wr, wi, **lkw)
    ag_src, ag_land = _allgather_wait(ag_send, ag_recv, ag_src, ag_land, y_lru, name="ag_wait")
    me_piece = 4 * x_i + 2 * y_i + c_i
    for n, src, land in zip(later, ag_src, _pass_to_sibling(ag_land, name="ag_pass")):
        own = lax.dynamic_slice_in_dim(src, c_i * (src.shape[0] // 2), src.shape[0] // 2, axis=0)
        full[n] = whole(n, lax.dynamic_update_index_in_dim(land, own, me_piece, axis=0))
    Ydn = _matmul(y_dn, full["w_branch_dn"], name="bdn_fwd")
    Ylru = _matmul(y_lru, full["w_branch_lru"], name="blru_fwd")
    mkw = dict(B=B, T=T, nc=NC, D=D, col0=p_mg, tm=tm)
    mixin = _merge_fwd(p, Ydn, Ylru, b_merge, **mkw)
    mix = _matmul(mixin, full["w_out"], name="out_fwd")
    h1, u2 = _post_fwd(x, mix, gains, vecs, tm=tm)
    F = _matmul(u2, full["w_up"], b_shards=(0, _NCHIP), name="up_fwd")
    w9 = sfull["ffn_dw"]
    ftc = _tile(DFF, 256)
    f, f_pre = _ffn_act_fwd(F, w9, ffn_dw_b, B=B, N=N, DFF=DFF, tc=ftc)
    dn = _matmul(f, full["w_down"], name="down_fwd")
    ddn, dout, sums_f = _final(h1, dn, loss_target, gains, vecs, tm=tm)

    G = {}
    df = _matmul(ddn, full["w_down"], tb=True, name="down_bwd_x")
    G["w_down"] = _matmul(f, ddn, ta=True, name="down_bwd_w")
    dFg, dFv, dwb = _ffn_act_bwd(F, f_pre, w9, df, B=B, N=N, DFF=DFF, tc=ftc)
    hs = _NCHIP // 2
    du2 = _matmul(dFg, full["w_up"], tb=True, b_shards=(0, hs), name="up_bwd_xg")
    du2 = _matmul(dFv, full["w_up"], tb=True, b_shards=(hs, hs), add=du2, name="up_bwd_xv")
    gup = _matmul(u2, dFg, ta=True, out_shards=hs, into=(lax.empty((_NCHIP, D, DFF // hs), F32), 0), name="up_bwd_wg")
    G["w_up"] = _matmul(u2, dFv, ta=True, out_shards=hs, into=(gup, hs), name="up_bwd_wv")
    dx1, dmix, sums_p = _post_bwd(x, mix, gains, vecs, dout, du2, tm=tm)
    dmixin = _matmul(dmix, full["w_out"], tb=True, name="out_bwd_x")
    G["w_out"] = _matmul(mixin, dmix, ta=True, name="out_bwd_w")
    dp = _zero_context_cols(lax.empty((B * T, PW), BF16), B=B, T=T, nc=NC, col0=p_mg, width=2 * D, tm=tm)
    dYdn, dYlru, dp, sums_m = _merge_bwd(p, Ydn, Ylru, b_merge, dmixin, dp, **mkw)
    dy_dn = _matmul(dYdn, full["w_branch_dn"], tb=True, name="bdn_bwd_x")
    G["w_branch_dn"] = _matmul(y_dn, dYdn, ta=True, name="bdn_bwd_w")
    dy_lru = _matmul(dYlru, full["w_branch_lru"], tb=True, name="blru_bwd_x")
    G["w_branch_lru"] = _matmul(y_lru, dYlru, ta=True, name="blru_bwd_w")

    c_arr = c_i.astype(jnp.int32).reshape(1)
    s_arr = s_me.astype(jnp.int32).reshape(1)

    def chip_sums(names, tag):
        slabs = [G[n] if _BIG[n] else G[n].reshape(_NCHIP, G[n].shape[0] // _NCHIP, G[n].shape[1]) for n in names]
        from_sibling = _sibling_send_halves(slabs, name="rs_sibling_" + tag)
        return [_half_add(g, r, c_arr, name="rs_add_" + n) for n, g, r in zip(names, slabs, from_sibling)]

    early = tuple(n for n in _BIG if n in G)
    cx_send, cx_recv, cx_src, cx_land, cx_token = _chip_exchange_start(chip_sums(early, "early"), name="cx_start")
    dp, dcw_l, dlv, dwr, dwi = _lru_bwd(p, sfull["lru_conv"], lv + cx_token[0, 0], wr, wi, dy_lru, dp, **lkw)
    dqkv, dgb, dp, don = _delta_bwd(qkv, gb, p, dn_onorm, o_dn, dn_res, dy_dn, dp, **dkw)
    dp, dprm = _gb_bwd(p, prm, dgb, dp, rows=B * T, col0=p_ab, H=H, tm=gtm)
    dp, dcw_d = _dnprep_bwd(p, sfull["dn_conv"], dqkv, dp, **dkw)
    dwp = _matmul(u, dp, ta=True, name="in_bwd_w")
    segs = [(0, o_a, 0), (o_a, 4 * H, p_ab)]
    segs += [(o_xl + i * LC, LC, p_lru + 2 * i * LC) for i in range(nj)]
    segs += [(o_xl + LW + i * LC, LC, p_lru + (2 * i + 1) * LC) for i in range(nj)]
    segs.append((o_mg, 2 * D, p_mg))
    n_in = W["w_in"].shape[-1]
    slabs_in = []
    for s in range(_NCHIP):
        lo, hi = s * n_in, (s + 1) * n_in
        parts = [dwp[:, q0 + max(lo, c0) - c0:q0 + min(hi, c0 + ln) - c0] for c0, ln, q0 in segs if max(lo, c0) < min(hi, c0 + ln)]
        slabs_in.append(jnp.concatenate(parts, axis=1))
    G["w_in"] = jnp.stack(slabs_in)
    wi_send, wi_recv, wi_src, wi_land, wi_token = _chip_exchange_start(chip_sums(("w_in",), "w_in"), name="cx_in_start")
    dU = _matmul(dp, wp, tb=True, after=wi_token, name="in_bwd_x")
    grad_x, sums_pm = _premix_bwd(ctx, x, g_pre_mix, tab, dU, dx1, tm=tm)

    dmod_x = jnp.stack([sums_pm[:, 1, 0], sums_pm[:, 1, 1], sums_p[:, 0], sums_p[:, 1], sums_p[:, 2], sums_f[:, 0]],
                       axis=1).reshape(B, 6 * D)
    dmod_c = jnp.concatenate([sums_pm[:, 0, 0].sum(0), sums_pm[:, 0, 1].sum(0), jnp.zeros((4 * D,), F32)])[None]
    dmod = jnp.concatenate([dmod_x, dmod_c, jnp.zeros((MR - B - 1, 6 * D), F32)], axis=0)
    G["w_ada"] = _matmul(silu_rows, dmod, ta=True, out_shards=_NCHIP, name="ada_bwd_w")
    dsilu = _matmul(dmod, full["w_ada"], tb=True, b_shards=(0, _NCHIP), name="ada_bwd_x")

    g_small = {
        "c_ctx": dsilu[B] * _dsilu(c_ctx),
        "b_ada": dmod[:B + 1].sum(0)[None],
        "g_pre_mix": sums_pm[:, :, 2].sum((0, 1))[None],
        "g_post_mix": sums_p[:, 3].sum(0)[None],
        "g_pre_ffn": sums_p[:, 4].sum(0)[None],
        "g_post_ffn": sums_f[:, 1].sum(0)[None],
        "b_merge": sums_m[0:1],
        "dn_conv": dcw_d[0:4][None],
        "dn_a_log": dprm[0, :2 * H].reshape(1, 2, H),
        "dn_dt_bias": dprm[1, :2 * H].reshape(1, 2, H),
        "dn_onorm": don[:, 0].sum(0)[None],
        "lru_conv": dcw_l[0:4][None],
        "lru_conv_b": dlv[0:1],
        "lru_w_rg": _blockdiag_extract(dwr, LBD)[None],
        "lru_b_rg": dlv[1:3][None],
        "lru_w_ig": _blockdiag_extract(dwi, LBD)[None],
        "lru_b_ig": dlv[3:5][None],
        "lru_lambda": dlv[5:7][None],
        "ffn_dw": dwb[0:9].reshape(1, 3, 3, DFF),
        "ffn_dw_b": dwb[9:10],
    }
    small_names = tuple(n for n in _WEIGHTS if n not in _BIG)
    loss_part = sums_f[:, 2].sum().reshape(1)
    gs_list = [g_small[n] for n in small_names] + [loss_part]
    gs_shapes = [a.shape for a in gs_list]
    gpack = _pack(gs_list, LANES, _FLAT_PART, _FLAT_TOTAL, F32)
    gsum = _sum_lead(_allgather_small(gpack), name="small_sum", tm=512, mult=SUBLANES)
    gs_red = dict(zip(small_names + ("loss",), _unpack(gsum, gs_shapes, LANES, _FLAT_PART, _FLAT_TOTAL)))
    loss = gs_red["loss"][0]

    grads, deltas, new_m, new_v = {}, {}, {}, {}

    def finish(names, lands, srcs, after, tag):
        halves = [_sum_slabs(l, src, s_arr, after, name="rs_sum_" + n) for n, l, src in zip(names, lands, srcs)]
        outs = None
        for n, own, sib in zip(names, halves, _sibling_swap(halves, name="rs_gather_" + tag)):
            shp = W[n].shape
            outs = _adamw_halves(W[n][0], own, sib, Mo[n][0], Vo[n][0], c_arr, name="adamw_" + n)
            grads[n], deltas[n], new_m[n], new_v[n] = (o.reshape(shp) for o in outs)
        return outs[1]

    cx_src, cx_land = _chip_exchange_wait(cx_send, cx_recv, cx_src, cx_land, dsilu, name="cx_wait")
    ada_sums, gsum = lax.optimization_barrier((chip_sums(("w_ada",), "w_ada"), gsum))
    ad_send, ad_recv, ad_src, ad_land, ad_token = _chip_exchange_start(ada_sums, name="cx_ada_start")
    last_early = finish(early, cx_land, cx_src, ad_token, "early")
    wi_src, wi_land = _chip_exchange_wait(wi_send, wi_recv, wi_src, wi_land, last_early, name="cx_in_wait")
    last_in = finish(("w_in",), wi_land, wi_src, last_early, "w_in")
    ad_src, ad_land = _chip_exchange_wait(ad_send, ad_recv, ad_src, ad_land, last_in, name="cx_ada_wait")
    finish(("w_ada",), ad_land, ad_src, last_in, "w_ada")
    for n in small_names:
        g = gs_red[n]
        if n in _SMALL_SHARDED:
            k = W[n].shape[-1]
            g = lax.dynamic_slice_in_dim(g, s_me * k, k, axis=g.ndim - 1)
        grads[n] = g.reshape(W[n].shape)
    sm_shapes = [W[n].shape for n in small_names]
    pk = lambda d: _pack([d[n] for n in small_names], LANES, _FLAT_PART, _FLAT_TOTAL, F32)
    d_, m_, v_ = _adamw(pk(W), pk(grads), pk(Mo), pk(Vo), name="adamw_small")
    for dst, pool_ in ((deltas, d_), (new_m, m_), (new_v, v_)):
        dst.update(zip(small_names, _unpack(pool_, sm_shapes, LANES, _FLAT_PART, _FLAT_TOTAL)))
    return (loss, grad_x, *[grads[n] for n in _WEIGHTS], *[deltas[n] for n in _WEIGHTS],
            *[new_m[n] for n in _WEIGHTS], *[new_v[n] for n in _WEIGHTS])
```

```python
import functools
import math

import jax
import jax.numpy as jnp
from jax import lax
from jax.experimental import pallas as pl
from jax.experimental.pallas import tpu as pltpu

F32 = jnp.float32
BF16 = jnp.bfloat16
EPS = 1e-6
GRID_W = 64
CHUNK = 256
LRU_C = 8.0
LANES = 128
SUBLANES = 8
VMEM_LIMIT = 56 * 1024 * 1024
ADAM_LR, ADAM_B1, ADAM_B2, ADAM_EPS, ADAM_WD, ADAM_STEP = 0.001, 0.9, 0.999, 1e-08, 0.01, 10
MESH = pl.DeviceIdType.MESH


def _tile(n, target, mult=LANES):
    best = None
    for t in range(mult, min(n, target) + 1, mult):
        if n % t == 0:
            best = t
    return best if best is not None else n


def _params(sem=None, **kw):
    return pltpu.CompilerParams(dimension_semantics=sem, vmem_limit_bytes=VMEM_LIMIT, **kw)


def _sigmoid(x):
    return 1.0 / (1.0 + jnp.exp(-x))


def _silu(x):
    return x * _sigmoid(x)


def _softplus(x):
    return jnp.maximum(x, 0.0) + jnp.log(1.0 + jnp.exp(-jnp.abs(x)))


def _gelu(x):
    return 0.5 * x * (1.0 + jnp.tanh(math.sqrt(2.0 / math.pi) * (x + 0.044715 * x * x * x)))


def _rmsn(u, gain):
    return u * lax.rsqrt(jnp.mean(u * u, axis=-1, keepdims=True) + EPS) * gain


_MM_VMEM = 40 * 1024 * 1024


_ANY_SPEC = pl.BlockSpec(memory_space=pl.ANY)


def _matmul(a, b, *, ta=False, tb=False, add=None, b_shards=None, out_shards=None, into=None, after=None,
            out_dtype=F32, name, tm=1024, tn=2048, tk=2048):
    (K, M) = a.shape if ta else a.shape[::-1]
    if b_shards is not None:
        s0, ns = b_shards
        bsh = (b.shape[1], ns * b.shape[2])
        nsh = b.shape[2]
    else:
        bsh = b.shape
    N = bsh[0] if tb else bsh[1]
    assert (bsh[1] if tb else bsh[0]) == K, (a.shape, b.shape, ta, tb)
    tm = _tile(M, tm)
    tk = _tile(nsh if (b_shards is not None and tb) else K, tk)
    nlim = nsh if (b_shards is not None and not tb) else (N // out_shards if out_shards else N)
    osz = jnp.dtype(out_dtype).itemsize + (4 if add is not None else 0)
    while True:
        tn_ = _tile(nlim, tn)
        need = 2 * (tm * tk * a.dtype.itemsize + tk * tn_ * b.dtype.itemsize + tm * tn_ * osz) + 4 * tm * tn_
        if need <= _MM_VMEM or tn <= LANES:
            break
        tn //= 2
    tn = tn_
    nk = K // tk
    dims = (((0 if ta else 1,), (1 if tb else 0,)), ((), ()))

    def body(a_ref, b_ref, *rest):
        c_ref = rest[0] if add is not None else None
        o_ref, acc_ref = rest[-2:]
        k = pl.program_id(2)

        @pl.when(k == 0)
        def _():
            acc_ref[...] = jnp.zeros_like(acc_ref) if c_ref is None else c_ref[...]

        bv = b_ref[0] if b_shards is not None else b_ref[...]
        acc_ref[...] += lax.dot_general(a_ref[...].astype(BF16), bv.astype(BF16), dims, preferred_element_type=F32)

        @pl.when(k == nk - 1)
        def _():
            if out_shards:
                o_ref[0] = acc_ref[...].astype(out_dtype)
            else:
                o_ref[...] = acc_ref[...].astype(out_dtype)

    a_spec = pl.BlockSpec((tk, tm), lambda i, j, k: (k, i)) if ta else pl.BlockSpec((tm, tk), lambda i, j, k: (i, k))
    if b_shards is None:
        b_spec = pl.BlockSpec((tn, tk), lambda i, j, k: (j, k)) if tb else pl.BlockSpec((tk, tn), lambda i, j, k: (k, j))
    elif tb:
        per = nsh // tk
        b_spec = pl.BlockSpec((1, tn, tk), lambda i, j, k: (s0 + k // per, j, k % per))
    else:
        per = nsh // tn
        b_spec = pl.BlockSpec((1, tk, tn), lambda i, j, k: (s0 + j // per, k, j % per))
    o_spec = pl.BlockSpec((tm, tn), lambda i, j, k: (i, j))
    extra, alias = (), {}
    if out_shards:
        oper = N // out_shards // tn
        o0 = 0
        out_shape = jax.ShapeDtypeStruct((out_shards, M, N // out_shards), out_dtype)
        if into is not None:
            buf, o0 = into
            out_shape = jax.ShapeDtypeStruct(buf.shape, buf.dtype)
            extra, alias = (buf,), {2 + (add is not None): 0}
        out_spec = pl.BlockSpec((1, tm, tn), lambda i, j, k: (o0 + j // oper, i, j % oper))
    else:
        out_spec, out_shape = o_spec, jax.ShapeDtypeStruct((M, N), out_dtype)
    if after is not None:
        extra = extra + (after,)
    return pl.pallas_call(
        body, name=name, grid=(M // tm, N // tn, nk),
        in_specs=[a_spec, b_spec] + ([o_spec] if add is not None else []) + [_ANY_SPEC] * len(extra),
        out_specs=out_spec, out_shape=out_shape, input_output_aliases=alias,
        scratch_shapes=[pltpu.VMEM((tm, tn), F32)],
        compiler_params=_params(("parallel", "parallel", "arbitrary")),
    )(*((a, b) + ((add,) if add is not None else ()) + extra))


def _premix_math(h, gain, shift, scale):
    return _rmsn(h, gain) * (1.0 + scale) + shift


def _stream_specs(nct, tm, D):
    return [pl.BlockSpec((1, tm, D), lambda b, t: (b, jnp.minimum(t, nct - 1), 0)),
            pl.BlockSpec((1, tm, D), lambda b, t: (b, jnp.maximum(t - nct, 0), 0))]


def _premix_fwd(ctx, x, gain, tab, *, tm):
    B, nc, D = ctx.shape
    T = nc + x.shape[1]
    nt, nct = T // tm, nc // tm

    def body(c_ref, x_ref, g_ref, tab_ref, u_ref):
        tabv = tab_ref[0, 0]
        h = jnp.where(pl.program_id(1) < nct, c_ref[0], x_ref[0])
        u_ref[...] = _premix_math(h, g_ref[...], tabv[0:1], tabv[1:2]).astype(BF16)

    return pl.pallas_call(
        body, name="premix_fwd", grid=(B, nt),
        in_specs=_stream_specs(nct, tm, D) + [
            pl.BlockSpec((1, D), lambda b, t: (0, 0)),
            pl.BlockSpec((1, 1, 8, D), lambda b, t: (b, jnp.where(t < nct, 0, 1), 0, 0))],
        out_specs=pl.BlockSpec((tm, D), lambda b, t: (b * nt + t, 0)),
        out_shape=jax.ShapeDtypeStruct((B * T, D), BF16),
        compiler_params=_params(("parallel", "parallel")),
    )(ctx, x, gain, tab)


def _premix_bwd(ctx, x, gain, tab, du, dres, *, tm):
    B, nc, D = ctx.shape
    N = x.shape[1]
    T = nc + N
    nt, nct = T // tm, nc // tm

    def body(c_ref, x_ref, g_ref, tab_ref, du_ref, dres_ref, dx_ref, sums_ref):
        t = pl.program_id(1)
        tabv = tab_ref[0, 0]
        h = jnp.where(t < nct, c_ref[0], x_ref[0])
        _, vjp = jax.vjp(_premix_math, h, g_ref[...], tabv[0:1], tabv[1:2])
        dh, dgain, dshift, dscale = vjp(du_ref[...].astype(F32))

        @pl.when((t == 0) | (t == nct))
        def _():
            sums_ref[...] = jnp.zeros_like(sums_ref)

        sums_ref[0, 0, 0:1, :] += dshift
        sums_ref[0, 0, 1:2, :] += dscale
        sums_ref[0, 0, 2:3, :] += dgain

        @pl.when(t >= nct)
        def _():
            dx_ref[0] = dres_ref[...] + dh

    lat = lambda b, t: jnp.maximum(t - nct, 0)
    return pl.pallas_call(
        body, name="premix_bwd", grid=(B, nt),
        in_specs=_stream_specs(nct, tm, D) + [
            pl.BlockSpec((1, D), lambda b, t: (0, 0)),
            pl.BlockSpec((1, 1, 8, D), lambda b, t: (b, jnp.where(t < nct, 0, 1), 0, 0)),
            pl.BlockSpec((tm, D), lambda b, t: (b * nt + t, 0)),
            pl.BlockSpec((tm, D), lambda b, t: (b * (nt - nct) + lat(b, t), 0))],
        out_specs=[pl.BlockSpec((1, tm, D), lambda b, t: (b, lat(b, t), 0)),
                   pl.BlockSpec((1, 1, 8, D), lambda b, t: (b, jnp.where(t < nct, 0, 1), 0, 0))],
        out_shape=[jax.ShapeDtypeStruct((B, N, D), F32), jax.ShapeDtypeStruct((B, 2, 8, D), F32)],
        compiler_params=_params(("parallel", "arbitrary")),
    )(ctx, x, gain, tab, du, dres)


def _merge_math(mgd, mgl, yd, yl, bd, bl):
    return _sigmoid(mgd + bd) * yd + _sigmoid(mgl + bl) * yl


def _merge_fwd(p, ydn, ylru, b_merge, *, B, T, nc, D, col0, tm):
    N = T - nc
    ntl, nt, nct, cb = N // tm, T // tm, nc // tm, col0 // D

    def body(mgd_ref, mgl_ref, yd_ref, yl_ref, bm_ref, o_ref):
        o_ref[...] = _merge_math(mgd_ref[...], mgl_ref[...], yd_ref[...], yl_ref[...],
                                 bm_ref[:, 0:D], bm_ref[:, D:2 * D]).astype(BF16)

    prow = lambda b, t: b * nt + nct + t
    return pl.pallas_call(
        body, name="merge_fwd", grid=(B, ntl),
        in_specs=[pl.BlockSpec((tm, D), lambda b, t: (prow(b, t), cb)),
                  pl.BlockSpec((tm, D), lambda b, t: (prow(b, t), cb + 1)),
                  pl.BlockSpec((tm, D), lambda b, t: (b * ntl + t, 0)),
                  pl.BlockSpec((tm, D), lambda b, t: (b * ntl + t, 0)),
                  pl.BlockSpec((1, 2 * D), lambda b, t: (0, 0))],
        out_specs=pl.BlockSpec((tm, D), lambda b, t: (b * ntl + t, 0)),
        out_shape=jax.ShapeDtypeStruct((B * N, D), BF16),
        compiler_params=_params(("parallel", "parallel")),
    )(p, p, ydn, ylru, b_merge)


def _merge_bwd(p, ydn, ylru, b_merge, dmix, dp, *, B, T, nc, D, col0, tm):
    N = T - nc
    ntl, nt, nct, cb = N // tm, T // tm, nc // tm, col0 // D
    assert col0 % (2 * D) == 0

    def body(mgd_ref, mgl_ref, yd_ref, yl_ref, bm_ref, dm_ref, dp_any, dyd_ref, dyl_ref, dp_ref, sums_ref):
        _, vjp = jax.vjp(_merge_math, mgd_ref[...], mgl_ref[...], yd_ref[...], yl_ref[...],
                         bm_ref[:, 0:D], bm_ref[:, D:2 * D])
        dmgd, dmgl, dyd, dyl, dbd, dbl = vjp(dm_ref[...])
        dyd_ref[...] = dyd.astype(BF16)
        dyl_ref[...] = dyl.astype(BF16)
        dp_ref[:, 0:D] = dmgd.astype(BF16)
        dp_ref[:, D:2 * D] = dmgl.astype(BF16)

        @pl.when((pl.program_id(0) == 0) & (pl.program_id(1) == 0))
        def _():
            sums_ref[...] = jnp.zeros_like(sums_ref)

        sums_ref[0:1, 0:D] += dbd
        sums_ref[0:1, D:2 * D] += dbl

    prow = lambda b, t: b * nt + nct + t
    row = pl.BlockSpec((tm, D), lambda b, t: (b * ntl + t, 0))
    return pl.pallas_call(
        body, name="merge_bwd", grid=(B, ntl),
        in_specs=[pl.BlockSpec((tm, D), lambda b, t: (prow(b, t), cb)),
                  pl.BlockSpec((tm, D), lambda b, t: (prow(b, t), cb + 1)),
                  row, row, pl.BlockSpec((1, 2 * D), lambda b, t: (0, 0)), row,
                  pl.BlockSpec(memory_space=pl.ANY)],
        out_specs=[row, row,
                   pl.BlockSpec((tm, 2 * D), lambda b, t: (prow(b, t), cb // 2)),
                   pl.BlockSpec((8, 2 * D), lambda b, t: (0, 0))],
        out_shape=[jax.ShapeDtypeStruct((B * N, D), BF16), jax.ShapeDtypeStruct((B * N, D), BF16),
                   jax.ShapeDtypeStruct(dp.shape, dp.dtype), jax.ShapeDtypeStruct((8, 2 * D), F32)],
        input_output_aliases={6: 2},
        compiler_params=_params(("arbitrary", "arbitrary")),
    )(p, p, ydn, ylru, b_merge, dmix, dp)


def _zero_context_cols(dp, *, B, T, nc, col0, width, tm):
    nt, nct = T // tm, nc // tm

    def body(dp_any, o_ref):
        o_ref[...] = jnp.zeros_like(o_ref)

    return pl.pallas_call(
        body, name="dp_zero_ctx", grid=(B, nct), in_specs=[pl.BlockSpec(memory_space=pl.ANY)],
        out_specs=pl.BlockSpec((tm, width), lambda b, t: (b * nt + t, col0 // width)),
        out_shape=jax.ShapeDtypeStruct(dp.shape, dp.dtype), input_output_aliases={0: 0},
        compiler_params=_params(("parallel", "parallel")),
    )(dp)


def _post_math(x, mix, g1, gate, g2, sh, sc):
    h1 = x + _rmsn(mix, g1) * gate
    return h1, _rmsn(h1, g2) * (1.0 + sc) + sh


def _post_fwd(x, mix, gains, vecs, *, tm):
    B, N, D = x.shape
    ntl = N // tm

    def body(x_ref, mix_ref, g_ref, v_ref, h1_ref, u2_ref):
        v = v_ref[0]
        h1, u2 = _post_math(x_ref[0], mix_ref[...], g_ref[0:1], v[0:1], g_ref[1:2], v[1:2], v[2:3])
        h1_ref[...] = h1
        u2_ref[...] = u2.astype(BF16)

    row = pl.BlockSpec((tm, D), lambda b, t: (b * ntl + t, 0))
    return pl.pallas_call(
        body, name="post_fwd", grid=(B, ntl),
        in_specs=[pl.BlockSpec((1, tm, D), lambda b, t: (b, t, 0)), row,
                  pl.BlockSpec((8, D), lambda b, t: (0, 0)), pl.BlockSpec((1, 8, D), lambda b, t: (b, 0, 0))],
        out_specs=[row, row],
        out_shape=[jax.ShapeDtypeStruct((B * N, D), F32), jax.ShapeDtypeStruct((B * N, D), BF16)],
        compiler_params=_params(("parallel", "parallel")),
    )(x, mix, gains, vecs)


def _post_bwd(x, mix, gains, vecs, dh1, du2, *, tm):
    B, N, D = x.shape
    ntl = N // tm

    def body(x_ref, mix_ref, g_ref, v_ref, dh1_ref, du2_ref, dx_ref, dmix_ref, sums_ref):
        v = v_ref[0]
        _, vjp = jax.vjp(_post_math, x_ref[0], mix_ref[...], g_ref[0:1], v[0:1], g_ref[1:2], v[1:2], v[2:3])
        dx, dmix, dg1, dgate, dg2, dsh, dsc = vjp((dh1_ref[...], du2_ref[...]))
        dx_ref[...] = dx
        dmix_ref[...] = dmix.astype(BF16)

        @pl.when(pl.program_id(1) == 0)
        def _():
            sums_ref[...] = jnp.zeros_like(sums_ref)

        sums_ref[0, 0:1, :] += dgate
        sums_ref[0, 1:2, :] += dsh
        sums_ref[0, 2:3, :] += dsc
        sums_ref[0, 3:4, :] += dg1
        sums_ref[0, 4:5, :] += dg2

    row = pl.BlockSpec((tm, D), lambda b, t: (b * ntl + t, 0))
    return pl.pallas_call(
        body, name="post_bwd", grid=(B, ntl),
        in_specs=[pl.BlockSpec((1, tm, D), lambda b, t: (b, t, 0)), row,
                  pl.BlockSpec((8, D), lambda b, t: (0, 0)), pl.BlockSpec((1, 8, D), lambda b, t: (b, 0, 0)), row, row],
        out_specs=[row, row, pl.BlockSpec((1, 8, D), lambda b, t: (b, 0, 0))],
        out_shape=[jax.ShapeDtypeStruct((B * N, D), F32), jax.ShapeDtypeStruct((B * N, D), BF16),
                   jax.ShapeDtypeStruct((B, 8, D), F32)],
        compiler_params=_params(("parallel", "arbitrary")),
    )(x, mix, gains, vecs, dh1, du2)


def _final_math(dn, g4, gate5):
    return _rmsn(dn, g4) * gate5


def _final(h1, dn, target, gains, vecs, *, tm):
    B, N, D = target.shape
    ntl = N // tm

    def body(h1_ref, dn_ref, t_ref, g_ref, v_ref, ddn_ref, dout_ref, sums_ref):
        v = v_ref[0]
        y, vjp = jax.vjp(_final_math, dn_ref[...], g_ref[2:3], v[3:4])
        err = h1_ref[...] + y - t_ref[0]
        dout = err * (1.0 / D)
        ddn, dg4, dgate5 = vjp(dout)
        ddn_ref[...] = ddn.astype(BF16)
        dout_ref[...] = dout

        @pl.when(pl.program_id(1) == 0)
        def _():
            sums_ref[...] = jnp.zeros_like(sums_ref)

        sums_ref[0, 0:1, :] += dgate5
        sums_ref[0, 1:2, :] += dg4
        sums_ref[0, 2:3, :] += jnp.sum(err * err, axis=0, keepdims=True) * (0.5 / D)

    row = pl.BlockSpec((tm, D), lambda b, t: (b * ntl + t, 0))
    return pl.pallas_call(
        body, name="final", grid=(B, ntl),
        in_specs=[row, row, pl.BlockSpec((1, tm, D), lambda b, t: (b, t, 0)),
                  pl.BlockSpec((8, D), lambda b, t: (0, 0)), pl.BlockSpec((1, 8, D), lambda b, t: (b, 0, 0))],
        out_specs=[row, row, pl.BlockSpec((1, 8, D), lambda b, t: (b, 0, 0))],
        out_shape=[jax.ShapeDtypeStruct((B * N, D), BF16), jax.ShapeDtypeStruct((B * N, D), F32),
                   jax.ShapeDtypeStruct((B, 8, D), F32)],
        compiler_params=_params(("parallel", "arbitrary")),
    )(h1, dn, target, gains, vecs)


def _shift(x, s):
    s = s % x.shape[0]
    return x if s == 0 else pltpu.roll(x, s, 0)


def _seg_taps(T, nc, width, pad_left):
    t = lax.broadcasted_iota(jnp.int32, (T, 1), 0)
    pos = jnp.where(t < nc, t, t - nc)
    seg = jnp.where(t < nc, nc, T - nc)
    taps = []
    for k in range(width):
        src = pos + (k - pad_left)
        taps.append((pad_left - k, (src >= 0) & (src < seg)))
    return taps


def _grid_taps(N):
    t = lax.broadcasted_iota(jnp.int32, (N, 1), 0)
    wcol = t % GRID_W
    taps = []
    for dr in (-1, 0, 1):
        for dw in (-1, 0, 1):
            off = dr * GRID_W + dw
            ok = (wcol + dw >= 0) & (wcol + dw < GRID_W) & (t + dr * GRID_W >= 0) & (t + dr * GRID_W < N)
            taps.append((-off, ok))
    return taps


def _conv_fwd(x, w, taps):
    y = jnp.zeros_like(x)
    for k, (s, m) in enumerate(taps):
        y = y + w[k:k + 1] * jnp.where(m, _shift(x, s), 0.0)
    return y


def _conv_bwd(x, w, taps, dy):
    dx = jnp.zeros_like(x)
    dws = []
    for k, (s, m) in enumerate(taps):
        dym = jnp.where(m, dy, 0.0)
        dx = dx + w[k:k + 1] * _shift(dym, -s)
        dws.append(jnp.sum(dym * _shift(x, s), axis=0, keepdims=True))
    return dx, jnp.concatenate(dws, axis=0)


def _ffn_act_fwd(F, w9, bias, *, B, N, DFF, tc):
    nj = DFF // tc

    def body(fg_ref, fv_ref, w_ref, b_ref, o_ref, pre_ref):
        fg = _conv_fwd(fg_ref[...], w_ref[...], _grid_taps(N)) + b_ref[...]
        pre_ref[...] = fg
        o_ref[...] = (_gelu(fg) * fv_ref[...]).astype(BF16)

    col = pl.BlockSpec((N, tc), lambda b, j: (b, j))
    return pl.pallas_call(
        body, name="ffn_act_fwd", grid=(B, nj),
        in_specs=[col, pl.BlockSpec((N, tc), lambda b, j: (b, nj + j)),
                  pl.BlockSpec((9, tc), lambda b, j: (0, j)), pl.BlockSpec((1, tc), lambda b, j: (0, j))],
        out_specs=[col, col],
        out_shape=[jax.ShapeDtypeStruct((B * N, DFF), BF16), jax.ShapeDtypeStruct((B * N, DFF), F32)],
        compiler_params=_params(("parallel", "parallel")),
    )(F, F, w9, bias)


def _ffn_act_bwd(F, pre, w9, df, *, B, N, DFF, tc):
    nj = DFF // tc

    def body(fg_ref, fv_ref, w_ref, pre_ref, df_ref, dfg_ref, dfv_ref, dwb_ref):
        taps = _grid_taps(N)
        x = fg_ref[...]
        fg, vjp = jax.vjp(lambda a: _gelu(a), pre_ref[...])
        dfl = df_ref[...]
        dfv_ref[...] = (dfl * fg).astype(BF16)
        (dpre,) = vjp(dfl * fv_ref[...])
        dx, dw = _conv_bwd(x, w_ref[...], taps, dpre)
        dfg_ref[...] = dx.astype(BF16)

        @pl.when(pl.program_id(1) == 0)
        def _():
            dwb_ref[...] = jnp.zeros_like(dwb_ref)

        dwb_ref[0:9, :] += dw
        dwb_ref[9:10, :] += jnp.sum(dpre, axis=0, keepdims=True)

    col = pl.BlockSpec((N, tc), lambda j, b: (b, j))
    return pl.pallas_call(
        body, name="ffn_act_bwd", grid=(nj, B),
        in_specs=[col, pl.BlockSpec((N, tc), lambda j, b: (b, nj + j)), pl.BlockSpec((9, tc), lambda j, b: (0, j)), col, col],
        out_specs=[col, col, pl.BlockSpec((16, tc), lambda j, b: (0, j))],
        out_shape=[jax.ShapeDtypeStruct((B * N, DFF), BF16), jax.ShapeDtypeStruct((B * N, DFF), BF16),
                   jax.ShapeDtypeStruct((16, DFF), F32)],
        compiler_params=_params(("parallel", "arbitrary")),
    )(F, F, w9, pre, df)


def _dnprep_math(y, is_qk, scale):
    s = _silu(y)
    n = s * lax.rsqrt(jnp.sum(s * s, axis=-1, keepdims=True) + EPS) * scale
    return jnp.where(is_qk, n, s)


def _dnprep_fwd(p, cw, *, B, T, nc, H, HD):
    def body(x_ref, w_ref, o_ref):
        j = pl.program_id(1)
        y = _conv_fwd(x_ref[...], w_ref[...], _seg_taps(T, nc, 4, 2))
        o_ref[...] = _dnprep_math(y, j < 2 * H, jnp.where(j < H, HD ** -0.5, 1.0))

    return pl.pallas_call(
        body, name="dnprep_fwd", grid=(B, 3 * H),
        in_specs=[pl.BlockSpec((T, HD), lambda b, j: (b, j)), pl.BlockSpec((4, HD), lambda b, j: (0, j))],
        out_specs=pl.BlockSpec((T, HD), lambda b, j: (b, j)),
        out_shape=jax.ShapeDtypeStruct((B * T, 3 * H * HD), F32),
        compiler_params=_params(("parallel", "parallel")),
    )(p, cw)


def _dnprep_bwd(p, cw, dqkv, dp, *, B, T, nc, H, HD):
    def body(x_ref, w_ref, d_ref, dp_any, dp_ref, dcw_ref):
        j = pl.program_id(0)
        taps = _seg_taps(T, nc, 4, 2)
        x = x_ref[...]
        y = _conv_fwd(x, w_ref[...], taps)
        is_qk, scale = j < 2 * H, jnp.where(j < H, HD ** -0.5, 1.0)
        _, vjp = jax.vjp(lambda a: _dnprep_math(a, is_qk, scale), y)
        (dy,) = vjp(d_ref[0])
        dx, dw = _conv_bwd(x, w_ref[...], taps, dy)
        dp_ref[...] = dx.astype(BF16)

        @pl.when(pl.program_id(1) == 0)
        def _():
            dcw_ref[...] = jnp.zeros_like(dcw_ref)

        dcw_ref[0:4, :] += dw

    col = pl.BlockSpec((T, HD), lambda j, b: (b, j))
    return pl.pallas_call(
        body, name="dnprep_bwd", grid=(3 * H, B),
        in_specs=[col, pl.BlockSpec((4, HD), lambda j, b: (0, j)),
                  pl.BlockSpec((1, T, HD), lambda j, b: (j // H, b, j % H)), pl.BlockSpec(memory_space=pl.ANY)],
        out_specs=[col, pl.BlockSpec((8, HD), lambda j, b: (0, j))],
        out_shape=[jax.ShapeDtypeStruct(dp.shape, dp.dtype), jax.ShapeDtypeStruct((8, 3 * H * HD), F32)],
        input_output_aliases={3: 0},
        compiler_params=_params(("parallel", "arbitrary")),
    )(p, cw, dqkv, dp)


def _gb_math(ab, alog, dtb, H):
    lane = lax.broadcasted_iota(jnp.int32, ab.shape, 1)
    g = -jnp.exp(alog) * _softplus(ab + dtb)
    return jnp.where(lane < 2 * H, g, jnp.where(lane < 4 * H, _sigmoid(ab), 0.0))


def _gb_fwd(p, prm, *, rows, col0, H, tm):
    def body(x_ref, prm_ref, o_ref):
        o_ref[...] = _gb_math(x_ref[...], prm_ref[0:1], prm_ref[1:2], H)

    return pl.pallas_call(
        body, name="gb_fwd", grid=(rows // tm,),
        in_specs=[pl.BlockSpec((tm, LANES), lambda t: (t, col0 // LANES)), pl.BlockSpec((8, LANES), lambda t: (0, 0))],
        out_specs=pl.BlockSpec((tm, LANES), lambda t: (t, 0)),
        out_shape=jax.ShapeDtypeStruct((rows, LANES), F32),
        compiler_params=_params(("parallel",)),
    )(p, prm)


def _gb_bwd(p, prm, dgb, dp, *, rows, col0, H, tm):
    def body(x_ref, prm_ref, d_ref, dp_any, dp_ref, dprm_ref):
        _, vjp = jax.vjp(lambda a, b, c: _gb_math(a, b, c, H), x_ref[...], prm_ref[0:1], prm_ref[1:2])
        dab, dalog, ddtb = vjp(d_ref[...])
        dp_ref[...] = dab.astype(BF16)

        @pl.when(pl.program_id(0) == 0)
        def _():
            dprm_ref[...] = jnp.zeros_like(dprm_ref)

        dprm_ref[0:1, :] += dalog
        dprm_ref[1:2, :] += ddtb

    blk = pl.BlockSpec((tm, LANES), lambda t: (t, col0 // LANES))
    return pl.pallas_call(
        body, name="gb_bwd", grid=(rows // tm,),
        in_specs=[blk, pl.BlockSpec((8, LANES), lambda t: (0, 0)), pl.BlockSpec((tm, LANES), lambda t: (t, 0)),
                  pl.BlockSpec(memory_space=pl.ANY)],
        out_specs=[blk, pl.BlockSpec((8, LANES), lambda t: (0, 0))],
        out_shape=[jax.ShapeDtypeStruct(dp.shape, dp.dtype), jax.ShapeDtypeStruct((8, LANES), F32)],
        input_output_aliases={3: 0},
        compiler_params=_params(("arbitrary",)),
    )(p, prm, dgb, dp)


def _lru_scans(scans):
    C = scans[0][0].shape[1]
    row = lax.broadcasted_iota(jnp.int32, (SUBLANES, C), 0)
    carries = tuple(jnp.zeros((1, C), F32) for _ in scans)
    for si in range(len(scans[0][4])):
        rows = scans[0][4][si][1]
        assert all(sc[4][si][1] == rows for sc in scans)
        sub = max(s for s in (4, 2, 1) if rows % (s * SUBLANES) == 0)
        span = sub * SUBLANES
        nb = rows // span

        def blk(i, carries, si=si, nb=nb, sub=sub, span=span):
            out = []
            for (a_ref, b_ref, h_ref, hp_ref, segs), carry in zip(scans, carries):
                start, _, reverse = segs[si]
                r0 = pl.multiple_of(start + (nb - 1 - i if reverse else i) * span, span)
                local = []
                for j in range(sub):
                    A = a_ref[pl.ds(r0 + j * SUBLANES, SUBLANES), :]
                    Bv = b_ref[pl.ds(r0 + j * SUBLANES, SUBLANES), :]
                    for s in (1, 2, 4):
                        sh = SUBLANES - s if reverse else s
                        m = (row < SUBLANES - s) if reverse else (row >= s)
                        Bv = jnp.where(m, A * pltpu.roll(Bv, sh, 0) + Bv, Bv)
                        A = jnp.where(m, A * pltpu.roll(A, sh, 0), A)
                    local.append((A, Bv))
                for j in (reversed(range(sub)) if reverse else range(sub)):
                    A, Bv = local[j]
                    Hv = Bv + A * carry
                    h_ref[pl.ds(r0 + j * SUBLANES, SUBLANES), :] = Hv
                    if hp_ref is not None:
                        if reverse:
                            hp = jnp.where(row < SUBLANES - 1, pltpu.roll(Hv, SUBLANES - 1, 0), carry)
                        else:
                            hp = jnp.where(row >= 1, pltpu.roll(Hv, 1, 0), carry)
                        hp_ref[pl.ds(r0 + j * SUBLANES, SUBLANES), :] = hp
                    carry = Hv[0:1] if reverse else Hv[SUBLANES - 1:SUBLANES]
                out.append(carry)
            return tuple(out)

        carries = lax.fori_loop(0, nb, blk, carries)


def _lru_orders(T, nc, d):
    N = T - nc
    if d == 0:
        return [(0, nc, False), (nc, N, False)], [(nc, N, True), (0, nc, True)]
    return [(0, nc, True), (nc, N, True)], [(nc, N, False), (0, nc, False)]


def _bdot(a, b, dims=(((1,), (0,)), ((), ()))):
    return lax.dot_general(a.astype(BF16), b.astype(BF16), dims, preferred_element_type=F32)


_NT = (((1,), (1,)), ((), ()))
_TN = (((0,), (0,)), ((), ()))


def _blockdiag(w, C):
    nd, nb, bd, _ = w.shape
    per = C // bd
    out = jnp.einsum('dnpij,pq->dnpiqj', w.reshape(nd, nb // per, per, bd, bd), jnp.eye(per, dtype=w.dtype))
    return out.reshape(nd, nb // per, C, C)


def _blockdiag_extract(dw, bd):
    nd, nj, C, _ = dw.shape
    per = C // bd
    out = jnp.einsum('dnpiqj,pq->dnpij', dw.reshape(nd, nj, per, bd, per, bd), jnp.eye(per, dtype=dw.dtype))
    return out.reshape(nd, nj * per, bd, bd)


def _lru_fwd(p, cw, lv, wr, wi, *, B, T, nc, LW, col0, C):
    N = T - nc
    nj = LW // C

    def body(x_ref, cw_ref, lv_ref, wr_ref, wi_ref, o_ref, a_s, b_s, h_s):
        lv_ = lv_ref[...]
        xc = _conv_fwd(x_ref[:, 0:C], cw_ref[...], _seg_taps(T, nc, 4, 2)) + lv_[0:1]
        for d in (0, 1):
            r = _sigmoid(_bdot(xc, wr_ref[d, 0]) + lv_[1 + d:2 + d])
            i = _sigmoid(_bdot(xc, wi_ref[d, 0]) + lv_[3 + d:4 + d])
            la = -LRU_C * r * _softplus(-lv_[5 + d:6 + d])
            a_s[d] = jnp.exp(la)
            b_s[d] = jnp.sqrt(1.0 - jnp.exp(2.0 * la)) * i * xc
        _lru_scans([(a_s.at[d], b_s.at[d], h_s.at[d], None, _lru_orders(T, nc, d)[0]) for d in (0, 1)])
        o_ref[...] = ((h_s[0, nc:, :] + h_s[1, nc:, :]) * _gelu(x_ref[nc:, C:2 * C])).astype(BF16)

    return pl.pallas_call(
        body, name="lru_fwd", grid=(B, nj),
        in_specs=[pl.BlockSpec((T, 2 * C), lambda b, j: (b, col0 // (2 * C) + j)),
                  pl.BlockSpec((4, C), lambda b, j: (0, j)), pl.BlockSpec((8, C), lambda b, j: (0, j)),
                  pl.BlockSpec((2, 1, C, C), lambda b, j: (0, j, 0, 0)), pl.BlockSpec((2, 1, C, C), lambda b, j: (0, j, 0, 0))],
        out_specs=pl.BlockSpec((N, C), lambda b, j: (b, j)),
        out_shape=jax.ShapeDtypeStruct((B * N, LW), BF16),
        scratch_shapes=[pltpu.VMEM((2, T, C), F32)] * 3,
        compiler_params=_params(("parallel", "parallel")),
    )(p, cw, lv, wr, wi)


def _lru_bwd(p, cw, lv, wr, wi, dy, dp, *, B, T, nc, LW, col0, C):
    N = T - nc
    nj = LW // C

    def body(x_ref, cw_ref, lv_ref, wr_ref, wi_ref, dy_ref, dp_any, dp_ref, dcw_ref, dlv_ref, dwr_ref, dwi_ref,
             a_s, b_s, h_s, hp_s, mu_s, mup_s, dh_s, dxc_s):
        taps = _seg_taps(T, nc, 4, 2)
        lv_ = lv_ref[...]
        xl = x_ref[:, 0:C]
        xc = _conv_fwd(xl, cw_ref[...], taps) + lv_[0:1]
        gel, gelu_vjp = jax.vjp(_gelu, x_ref[nc:, C:2 * C])
        dh_s[0:nc, :] = jnp.zeros((nc, C), F32)
        dh_s[nc:, :] = dy_ref[...] * gel
        dxc_s[...] = jnp.zeros_like(dxc_s)

        @pl.when(pl.program_id(1) == 0)
        def _():
            dcw_ref[...] = jnp.zeros_like(dcw_ref)
            dlv_ref[...] = jnp.zeros_like(dlv_ref)
            dwr_ref[...] = jnp.zeros_like(dwr_ref)
            dwi_ref[...] = jnp.zeros_like(dwi_ref)

        def gates(d):
            lam = lv_[5 + d:6 + d]
            r = _sigmoid(_bdot(xc, wr_ref[d, 0]) + lv_[1 + d:2 + d])
            i = _sigmoid(_bdot(xc, wi_ref[d, 0]) + lv_[3 + d:4 + d])
            sp = _softplus(-lam)
            la = -LRU_C * r * sp
            e2 = jnp.exp(2.0 * la)
            return lam, r, i, sp, la, e2, jnp.sqrt(1.0 - e2)

        for d in (0, 1):
            _, _, i, _, la, _, mult = gates(d)
            a_s[d] = jnp.exp(la)
            b_s[d] = mult * i * xc
        _lru_scans([(a_s.at[d], b_s.at[d], h_s.at[d], hp_s.at[d], _lru_orders(T, nc, d)[0]) for d in (0, 1)])
        for d in (0, 1):
            b_s[d] = a_s[d] * dh_s[...]
        _lru_scans([(a_s.at[d], b_s.at[d], mu_s.at[d], mup_s.at[d], _lru_orders(T, nc, d)[1]) for d in (0, 1)])

        for d in (0, 1):
            lam, r, i, sp, la, e2, mult = gates(d)
            a = a_s[d]
            dinp = dh_s[...] + mup_s[d]
            da = dinp * hp_s[d]
            dmult = dinp * i * xc
            di = dinp * mult * xc
            dla = da * a - dmult * e2 / mult
            dpre_r = (dla * (-LRU_C * sp)) * r * (1.0 - r)
            dpre_i = di * i * (1.0 - i)
            dsp = jnp.sum(dla * (-LRU_C * r), axis=0, keepdims=True)
            dxc_s[...] += dinp * mult * i + _bdot(dpre_r, wr_ref[d, 0], _NT) + _bdot(dpre_i, wi_ref[d, 0], _NT)
            dwr_ref[d, 0] += _bdot(xc, dpre_r, _TN)
            dwi_ref[d, 0] += _bdot(xc, dpre_i, _TN)
            dlv_ref[1 + d:2 + d, :] += jnp.sum(dpre_r, axis=0, keepdims=True)
            dlv_ref[3 + d:4 + d, :] += jnp.sum(dpre_i, axis=0, keepdims=True)
            dlv_ref[5 + d:6 + d, :] += -dsp * _sigmoid(-lam)

        dxc = dxc_s[...]
        dxl, dw = _conv_bwd(xl, cw_ref[...], taps, dxc)
        dcw_ref[0:4, :] += dw
        dlv_ref[0:1, :] += jnp.sum(dxc, axis=0, keepdims=True)
        dp_ref[:, 0:C] = dxl.astype(BF16)
        (dyl,) = gelu_vjp(dy_ref[...] * (h_s[0, nc:, :] + h_s[1, nc:, :]))
        dp_ref[0:nc, C:2 * C] = jnp.zeros((nc, C), BF16)
        dp_ref[nc:, C:2 * C] = dyl.astype(BF16)

    xblk = pl.BlockSpec((T, 2 * C), lambda j, b: (b, col0 // (2 * C) + j))
    wblk = pl.BlockSpec((2, 1, C, C), lambda j, b: (0, j, 0, 0))
    vblk = pl.BlockSpec((8, C), lambda j, b: (0, j))
    return pl.pallas_call(
        body, name="lru_bwd", grid=(nj, B),
        in_specs=[xblk, pl.BlockSpec((4, C), lambda j, b: (0, j)), vblk, wblk, wblk,
                  pl.BlockSpec((N, C), lambda j, b: (b, j)), pl.BlockSpec(memory_space=pl.ANY)],
        out_specs=[xblk, vblk, vblk, wblk, wblk],
        out_shape=[jax.ShapeDtypeStruct(dp.shape, dp.dtype), jax.ShapeDtypeStruct((8, LW), F32),
                   jax.ShapeDtypeStruct((8, LW), F32), jax.ShapeDtypeStruct((2, nj, C, C), F32),
                   jax.ShapeDtypeStruct((2, nj, C, C), F32)],
        scratch_shapes=[pltpu.VMEM((2, T, C), F32)] * 6 + [pltpu.VMEM((T, C), F32)] * 2,
        input_output_aliases={6: 0},
        compiler_params=_params(("parallel", "arbitrary")),
    )(p, cw, lv, wr, wi, dy, dp)


def _chunk_masks(upper):
    i = lax.broadcasted_iota(jnp.int32, (CHUNK, CHUNK), 0)
    j = lax.broadcasted_iota(jnp.int32, (CHUNK, CHUNK), 1)
    ahead = jnp.where(upper, j - i, i - j)
    return i == j, ahead >= 0, ahead > 0


def _col2row(c, eye):
    return jnp.sum(jnp.where(eye, c, 0.0), axis=0, keepdims=True)


def _row2col(r, eye):
    return jnp.sum(jnp.where(eye, r, 0.0), axis=1, keepdims=True)


def _rowsum(x):
    return jnp.sum(x, axis=1, keepdims=True)


_INV_BASE = 8


def _unit_tri_inverses(Ls):
    G = len(Ls)
    W = G * CHUNK
    blk = (lax.broadcasted_iota(jnp.int32, (W, W), 0) // CHUNK) == (lax.broadcasted_iota(jnp.int32, (W, W), 1) // CHUNK)
    ri = lax.broadcasted_iota(jnp.int32, (CHUNK, W), 0)
    ci = lax.broadcasted_iota(jnp.int32, (CHUNK, W), 1) % CHUNK

    def bd(b):
        return jnp.where(blk, jnp.tile(b, (G, 1)), jnp.zeros((), BF16))

    def pdot(a, b):
        return jnp.dot(a.astype(BF16), bd(b.astype(BF16)), preferred_element_type=F32)

    Lc = Ls[0] if G == 1 else jnp.concatenate(Ls, axis=1)
    s = _INV_BASE
    Xp = -jnp.where(ri // s == ci // s, Lc, 0.0)
    Rm = Xp
    for _ in range(int(math.log2(s)) - 1):
        Xp = pdot(Xp, Xp)
        Rm = Rm + Xp + pdot(Rm, Xp)
    while s < CHUNK:
        E = jnp.where((ri // (2 * s) == ci // (2 * s)) & (ri // s != ci // s), Lc, 0.0)
        DE = E + pdot(Rm, E)
        Rm = Rm - (DE + pdot(DE, Rm))
        s *= 2
    eye = _chunk_masks(False)[0]
    return [jnp.where(eye, 1.0, 0.0) + Rm[:, g * CHUNK:(g + 1) * CHUNK] for g in range(G)]


def _delta_chunk_common(q, k, v, gcol, bcol, upper):
    eye, incl, strict = _chunk_masks(upper)
    gc = _rowsum(jnp.where(incl, _col2row(gcol, eye), 0.0))
    D = jnp.where(incl, jnp.exp(jnp.minimum(gc - _col2row(gc, eye), 0.0)), 0.0)
    kb = k * bcol
    AP = _bdot(jnp.concatenate([kb, q], axis=0), k, _NT)
    A = AP[:CHUNK]
    L = jnp.where(strict, A * D, 0.0)
    eg = jnp.exp(gc)
    gl = jnp.sum(gcol, axis=0, keepdims=True)
    attn = jnp.where(incl, AP[CHUNK:] * D, 0.0)
    return dict(eye=eye, incl=incl, strict=strict, gc=gc, D=D, kb=kb, A=A, L=L, eg=eg, gl=gl, egl=jnp.exp(gl),
                attn=attn, kbe=kb * eg, vb=v * bcol, qe=q * eg, kd=k * jnp.exp(gl - gc))


def _delta_group_pre(chunks, upper):
    cs = [_delta_chunk_common(*ch, upper) for ch in chunks]
    out = []
    for c, Tm in zip(cs, _unit_tri_inverses([c["L"] for c in cs])):
        dk = c["kbe"].shape[1]
        wu = _bdot(Tm, jnp.concatenate([c["kbe"], c["vb"]], axis=1))
        KN = _bdot(c["kd"], wu, _TN)
        QO = _bdot(c["attn"], wu)
        out.append((Tm, KN[:, :dk], KN[:, dk:], c["qe"] - QO[:, :dk], QO[:, dk:], c["egl"]))
    return out


def _delta_chunk_bwd(q, k, v, gcol, bcol, S, Tm, do, dS2, upper):
    c = _delta_chunk_common(q, k, v, gcol, bcol, upper)
    eye, incl, strict, D, eg, egl = c["eye"], c["incl"], c["strict"], c["D"], c["eg"], c["egl"]
    kb, kbe, vb, qe, kd, attn = c["kb"], c["kbe"], c["vb"], c["qe"], c["kd"], c["attn"]
    dkk = kbe.shape[1]
    wu = _bdot(Tm, jnp.concatenate([kbe, vb], axis=1))
    w = wu[:, :dkk]
    vn = wu[:, dkk:] - _bdot(w, S)
    dvn = _bdot(kd, dS2) + _bdot(attn, do, _TN)
    dkd = _bdot(vn, dS2, _NT)
    dgl = jnp.sum(_rowsum(dS2 * S), axis=0, keepdims=True) * egl
    dqa = _bdot(do, jnp.concatenate([S, vn], axis=0), _NT)
    dqe = dqa[:, :dkk]
    dattn = jnp.where(incl, dqa[:, dkk:], 0.0)
    dw = -_bdot(dvn, S, _NT)
    r = _rowsum(dkd * kd)
    dk = dkd * jnp.exp(c["gl"] - c["gc"])
    dgl = dgl + jnp.sum(r, axis=0, keepdims=True)
    dgc = _rowsum(dqe * qe) - r
    E = dattn * attn
    dvw = jnp.concatenate([dvn, dw], axis=1)
    dTm = _bdot(dvw, jnp.concatenate([vb, kbe], axis=1), _NT)
    dvk = _bdot(Tm, dvw, _TN)
    dvb = dvk[:, :dvn.shape[1]]
    dv = dvb * bcol
    dbeta = _rowsum(dvb * v)
    dkbe = dvk[:, dvn.shape[1]:]
    dkb = dkbe * eg
    dgc = dgc + _rowsum(dkbe * kbe)
    dL = jnp.where(strict, -_bdot(Tm, _bdot(dTm, Tm, _NT), _TN), 0.0)
    dA = dL * D
    E = E + dL * c["L"]
    PA = jnp.concatenate([dattn * D, dA], axis=0)
    PAk = _bdot(PA, k)
    dq = dqe * eg + PAk[:CHUNK]
    dkb = dkb + PAk[CHUNK:]
    dk = dk + _bdot(PA, jnp.concatenate([q, kb], axis=0), _TN) + dkb * bcol
    dbeta = dbeta + _rowsum(dkb * k)
    dgc = dgc + _rowsum(E) - _row2col(jnp.sum(E, axis=0, keepdims=True), eye)
    dg = _row2col(jnp.sum(jnp.where(incl, dgc, 0.0), axis=0, keepdims=True), eye) + dgl
    return dq, dk, dv, dg, dbeta


def _delta_unroll(trips):
    return max(u for u in (3, 2, 1) if trips % u == 0)


def _delta_group(n):
    return max(g for g in range(1, 2 * LANES // CHUNK + 1) if n % g == 0)


def _delta_chunk_at(T, nc, d, i):
    n, ncc = T // CHUNK, nc // CHUNK
    desc = jnp.where(i < ncc, ncc - 1 - i, n - 1 - (i - ncc))
    if isinstance(d, int):
        return i if d == 0 else desc
    return jnp.where(d == 0, i, desc)


def _dn_out_math(o, onorm, z):
    return _rmsn(o, onorm) * _silu(z)


def _delta_fwd(qkv, gb, p, onorm, *, B, T, nc, H, HD):
    N = T - nc
    n = T // CHUNK
    G = _delta_group(n)

    def body(q_ref, k_ref, v_ref, gb_ref, z_ref, on_ref, y_ref, o_ref, Tm_ref, K_ref, S_ref, Qp_ref, eg_ref,
             N_s, O0_s, o_s):
        h = pl.program_id(1)
        lane = lax.broadcasted_iota(jnp.int32, (CHUNK, LANES), 1)

        def pre(g, carry):
            cs = [g * G + i for i in range(G)]
            rows = [pl.ds(pl.multiple_of(c * CHUNK, CHUNK), CHUNK) for c in cs]
            for d in (0, 1):
                chunks = []
                for r in rows:
                    gbb = gb_ref[r, :]
                    chunks.append((q_ref[r, :], k_ref[r, :], v_ref[r, :],
                                   _rowsum(jnp.where(lane == d * H + h, gbb, 0.0)),
                                   _rowsum(jnp.where(lane == 2 * H + d * H + h, gbb, 0.0))))
                for c, r, (Tm, K, Nn, Qp, O0, egl) in zip(cs, rows, _delta_group_pre(chunks, d == 1)):
                    Tm_ref[0, d * n + c] = Tm
                    K_ref[0, d * n + c] = K.astype(BF16)
                    N_s[d * n + c] = Nn
                    Qp_ref[0, d, r, :] = Qp.astype(BF16)
                    O0_s[d, r, :] = O0
                    eg_ref[0, d * n + c] = jnp.broadcast_to(egl, (SUBLANES, HD))
            return carry

        lax.fori_loop(0, n // G, pre, 0)

        def step(i, Ss):
            out = []
            for d in (0, 1):
                c = _delta_chunk_at(T, nc, d, i)
                rows = pl.ds(pl.multiple_of(c * CHUNK, CHUNK), CHUNK)
                S_ref[0, d * n + c] = Ss[d]
                Sb = Ss[d].astype(BF16)
                o_s[d, rows, :] = jnp.dot(Qp_ref[0, d, rows, :], Sb, preferred_element_type=F32) + O0_s[d, rows, :]
                out.append(eg_ref[0, d * n + c][0:1] * Ss[d] + N_s[d * n + c]
                           - jnp.dot(K_ref[0, d * n + c], Sb, preferred_element_type=F32))
            return tuple(out)

        lax.fori_loop(0, n, step, (jnp.zeros((HD, HD), F32), jnp.zeros((HD, HD), F32)))
        o = o_s[0, nc:, :] + o_s[1, nc:, :]
        o_ref[...] = o
        y_ref[...] = _dn_out_math(o, on_ref[...], z_ref[nc:, :]).astype(BF16)

    col = lambda off: pl.BlockSpec((T, HD), lambda b, h: (b, off + h))
    lat = pl.BlockSpec((N, HD), lambda b, h: (b, h))
    per = lambda *blk: pl.BlockSpec((1, *blk), lambda b, h: (b * H + h, 0, 0, 0))
    return pl.pallas_call(
        body, name="delta_fwd", grid=(B, H),
        in_specs=[col(0), col(H), col(2 * H), pl.BlockSpec((T, LANES), lambda b, h: (b, 0)), col(3 * H),
                  pl.BlockSpec((1, HD), lambda b, h: (0, 0))],
        out_specs=[lat, lat, per(2 * n, CHUNK, CHUNK), per(2 * n, HD, HD), per(2 * n, HD, HD), per(2, T, HD),
                   per(2 * n, SUBLANES, HD)],
        out_shape=[jax.ShapeDtypeStruct((B * N, H * HD), BF16), jax.ShapeDtypeStruct((B * N, H * HD), F32),
                   jax.ShapeDtypeStruct((B * H, 2 * n, CHUNK, CHUNK), F32),
                   jax.ShapeDtypeStruct((B * H, 2 * n, HD, HD), BF16), jax.ShapeDtypeStruct((B * H, 2 * n, HD, HD), F32),
                   jax.ShapeDtypeStruct((B * H, 2, T, HD), BF16), jax.ShapeDtypeStruct((B * H, 2 * n, SUBLANES, HD), F32)],
        scratch_shapes=[pltpu.VMEM((2 * n, HD, HD), F32), pltpu.VMEM((2, T, HD), F32), pltpu.VMEM((2, T, HD), F32)],
        compiler_params=_params(("parallel", "parallel")),
    )(qkv, qkv, qkv, gb, p, onorm)


def _delta_bwd(qkv, gb, p, onorm, o, res, dy, dp, *, B, T, nc, H, HD):
    N = T - nc
    n = T // CHUNK

    def body(q_ref, k_ref, v_ref, gb_ref, z_ref, on_ref, o_ref, dy_ref, Tm_ref, K_ref, S_ref, Qp_ref, eg_ref, dp_any,
             dqkv_ref, dgb_ref, dp_ref, don_ref, do_s, R_s, dS_s):
        h, d = pl.program_id(1), pl.program_id(2)
        lane = lax.broadcasted_iota(jnp.int32, (CHUNK, LANES), 1)

        @pl.when(d == 0)
        def _():
            _, vjp = jax.vjp(_dn_out_math, o_ref[...], on_ref[...], z_ref[nc:, :])
            do, don, dz = vjp(dy_ref[...])
            do_s[0:nc, :] = jnp.zeros((nc, HD), F32)
            do_s[nc:, :] = do
            dp_ref[0:nc, :] = jnp.zeros((nc, HD), BF16)
            dp_ref[nc:, :] = dz.astype(BF16)
            dqkv_ref[...] = jnp.zeros_like(dqkv_ref)

            @pl.when(h == 0)
            def _():
                don_ref[...] = jnp.zeros_like(don_ref)
                dgb_ref[...] = jnp.zeros_like(dgb_ref)

            don_ref[0, 0:1, :] += don

        def r_of(c, carry):
            rows = pl.ds(pl.multiple_of(c * CHUNK, CHUNK), CHUNK)
            R_s[c] = lax.dot_general(Qp_ref[0, 0, rows, :], do_s[rows, :].astype(BF16), _TN, preferred_element_type=F32)
            return carry

        lax.fori_loop(0, n, r_of, 0)

        def bwd_step(i, dS):
            c = _delta_chunk_at(T, nc, d, n - 1 - i)
            dS_s[c] = dS
            return (eg_ref[0, c][0:1] * dS + R_s[c]
                    - lax.dot_general(K_ref[0, c], dS.astype(BF16), _TN, preferred_element_type=F32))

        lax.fori_loop(0, n, bwd_step, jnp.zeros((HD, HD), F32))

        def grads(c, carry):
            rows = pl.ds(pl.multiple_of(c * CHUNK, CHUNK), CHUNK)
            gbb = gb_ref[rows, :]
            gcol = _rowsum(jnp.where(lane == d * H + h, gbb, 0.0))
            bcol = _rowsum(jnp.where(lane == 2 * H + d * H + h, gbb, 0.0))
            dq, dk, dv, dg, dbeta = _delta_chunk_bwd(q_ref[rows, :], k_ref[rows, :], v_ref[rows, :], gcol, bcol,
                                                     S_ref[0, c], Tm_ref[0, c], do_s[rows, :], dS_s[c], d == 1)
            dqkv_ref[0, rows, :] += dq
            dqkv_ref[1, rows, :] += dk
            dqkv_ref[2, rows, :] += dv
            dgb_ref[rows, :] += (jnp.where(lane == d * H + h, dg, 0.0)
                                 + jnp.where(lane == 2 * H + d * H + h, dbeta, 0.0))
            return carry

        lax.fori_loop(0, n, grads, 0, unroll=_delta_unroll(n))

    col = lambda off: pl.BlockSpec((T, HD), lambda b, h, d: (b, off + h))
    lat = pl.BlockSpec((N, HD), lambda b, h, d: (b, h))
    per = lambda *blk: pl.BlockSpec((1, *blk), lambda b, h, d: (b * H + h, d, 0, 0))
    return pl.pallas_call(
        body, name="delta_bwd", grid=(B, H, 2),
        in_specs=[col(0), col(H), col(2 * H), pl.BlockSpec((T, LANES), lambda b, h, d: (b, 0)), col(3 * H),
                  pl.BlockSpec((1, HD), lambda b, h, d: (0, 0)), lat, lat,
                  per(n, CHUNK, CHUNK), per(n, HD, HD), per(n, HD, HD), per(1, T, HD), per(n, SUBLANES, HD),
                  pl.BlockSpec(memory_space=pl.ANY)],
        out_specs=[pl.BlockSpec((3, T, HD), lambda b, h, d: (0, b, h)), pl.BlockSpec((T, LANES), lambda b, h, d: (b, 0)),
                   col(3 * H), pl.BlockSpec((1, 8, HD), lambda b, h, d: (b, 0, 0))],
        out_shape=[jax.ShapeDtypeStruct((3, B * T, H * HD), F32), jax.ShapeDtypeStruct((B * T, LANES), F32),
                   jax.ShapeDtypeStruct(dp.shape, dp.dtype), jax.ShapeDtypeStruct((B, 8, HD), F32)],
        scratch_shapes=[pltpu.VMEM((T, HD), F32), pltpu.VMEM((n, HD, HD), F32), pltpu.VMEM((n, HD, HD), F32)],
        input_output_aliases={13: 2},
        compiler_params=_params(("parallel", "arbitrary", "arbitrary")),
    )(qkv, qkv, qkv, gb, p, onorm, o, dy, *res, dp)


def _rowwise(fn, ins, out_dtypes, *, name, tm=256, mult=16):
    R, W = ins[0].shape
    tm = _tile(R, tm, mult)

    def body(*refs):
        outs = fn(*[r[...] for r in refs[:len(ins)]])
        for o_ref, o in zip(refs[len(ins):], outs):
            o_ref[...] = o.astype(o_ref.dtype)

    spec = pl.BlockSpec((tm, W), lambda i: (i, 0))
    return pl.pallas_call(
        body, name=name, grid=(R // tm,), in_specs=[spec] * len(ins), out_specs=[spec] * len(out_dtypes),
        out_shape=[jax.ShapeDtypeStruct((R, W), dt) for dt in out_dtypes],
        compiler_params=_params(("parallel",)),
    )(*ins)


def _sum_lead(x, *, name, tm=256, mult=16):
    S, R, W = x.shape
    tm = _tile(R, tm, mult)

    def body(*refs):
        acc = refs[0][0].astype(F32)
        for r in refs[1:S]:
            acc = acc + r[0].astype(F32)
        refs[S][...] = acc

    return pl.pallas_call(
        body, name=name, grid=(R // tm,),
        in_specs=[pl.BlockSpec((1, tm, W), functools.partial(lambda s, i: (s, i, 0), s)) for s in range(S)],
        out_specs=pl.BlockSpec((tm, W), lambda i: (i, 0)),
        out_shape=jax.ShapeDtypeStruct((R, W), F32),
        compiler_params=_params(("parallel",)),
    )(*([x] * S))


def _adamw_math(w, g, m, v):
    m = ADAM_B1 * m + (1.0 - ADAM_B1) * g
    v = ADAM_B2 * v + (1.0 - ADAM_B2) * (g * g)
    m_hat = m / (1.0 - ADAM_B1 ** ADAM_STEP)
    v_hat = v / (1.0 - ADAM_B2 ** ADAM_STEP)
    return -ADAM_LR * (m_hat / (jnp.sqrt(v_hat) + ADAM_EPS) + ADAM_WD * w), m, v


def _adamw(w, g, m, v, *, name):
    tm = max(SUBLANES, (256 * 1024) // w.shape[1] // SUBLANES * SUBLANES)
    return _rowwise(_adamw_math, [w, g, m, v], [F32, F32, F32], name=name, tm=tm, mult=SUBLANES)


def _me():
    return lax.axis_index("x"), lax.axis_index("y"), lax.axis_index("c")


def _allgather_small(v):
    R, W = v.shape

    def body(x_ref, out_ref, send_sems, recv_sems, local_sem):
        x, y, c = _me()
        me, sibling = (x, y, c), (x, y, 1 - c)
        chips = [(1 - x, y), (x, 1 - y), (1 - x, 1 - y)]

        def slot(px, py, pc):
            return out_ref.at[4 * px + 2 * py + pc]

        def copy(k, block, to, src=None):
            return pltpu.make_async_remote_copy(
                src_ref=slot(*block) if src is None else src, dst_ref=slot(*block),
                send_sem=send_sems.at[k], recv_sem=recv_sems.at[k], device_id=to, device_id_type=MESH)

        mine = pltpu.make_async_copy(x_ref, slot(*me), local_sem)
        mine.start()
        first = [copy(0, me, sibling, src=x_ref)]
        first += [copy(1 + j, me, (*chip, c), src=x_ref) for j, chip in enumerate(chips)]
        for cp in first:
            cp.start()
        passed = [copy(4 + j, (*chip, c), sibling) for j, chip in enumerate(chips)]
        for j, chip in enumerate(chips):
            copy(1 + j, (*chip, c), me).wait_recv()
            passed[j].start()
        copy(0, sibling, me).wait_recv()
        for j, chip in enumerate(chips):
            copy(4 + j, (*chip, 1 - c), me).wait_recv()
        for cp in first + passed:
            cp.wait_send()
        mine.wait()

    return pl.pallas_call(
        body, name="allgather_small", out_shape=jax.ShapeDtypeStruct((8, R, W), v.dtype),
        in_specs=[pl.BlockSpec(memory_space=pltpu.VMEM)], out_specs=pl.BlockSpec(memory_space=pltpu.VMEM),
        scratch_shapes=[pltpu.SemaphoreType.DMA((7,)), pltpu.SemaphoreType.DMA((7,)), pltpu.SemaphoreType.DMA],
        compiler_params=_params(),
    )(v)


_ANY = pl.BlockSpec(memory_space=pl.ANY)


def _allgather_halves(shards, *, name):
    nw = len(shards)

    def body(*refs):
        x_refs, out_refs = refs[:nw], refs[nw:2 * nw]
        send_sems, recv_sems = refs[2 * nw:]
        x, y, c = _me()
        me, sibling = (x, y, c), (x, y, 1 - c)
        chips = [(1 - x, y), (x, 1 - y), (1 - x, 1 - y)]

        def slot(w, px, py, pc):
            return out_refs[w].at[4 * px + 2 * py + pc]

        def copy(w, k, block, to, src=None):
            return pltpu.make_async_remote_copy(
                src_ref=slot(w, *block) if src is None else src, dst_ref=slot(w, *block),
                send_sem=send_sems.at[w, k], recv_sem=recv_sems.at[w, k], device_id=to, device_id_type=MESH)

        started = []
        for w in range(nw):
            half = shards[w].shape[0] // 2
            own = x_refs[w].at[pl.ds(c * half, half), :]
            first = [copy(w, 0, me, sibling, src=own)]
            first += [copy(w, 1 + j, me, (*chip, c), src=own) for j, chip in enumerate(chips)]
            for cp in first:
                cp.start()
            started += first
        for w in range(nw):
            for j, chip in enumerate(chips):
                copy(w, 1 + j, (*chip, c), me).wait_recv()
                fwd = copy(w, 4 + j, (*chip, c), sibling)
                fwd.start()
                started.append(fwd)
        for w in range(nw):
            copy(w, 0, sibling, me).wait_recv()
            for j, chip in enumerate(chips):
                copy(w, 4 + j, (*chip, 1 - c), me).wait_recv()
        for cp in started:
            cp.wait_send()

    return pl.pallas_call(
        body, name=name,
        out_shape=[jax.ShapeDtypeStruct((8, s.shape[0] // 2, s.shape[1]), s.dtype) for s in shards],
        in_specs=[_ANY] * nw, out_specs=[_ANY] * nw,
        scratch_shapes=[pltpu.SemaphoreType.DMA((nw, 7)), pltpu.SemaphoreType.DMA((nw, 7))],
        compiler_params=_params(),
    )(*shards)


def _sibling_send_halves(arrs, *, name):
    nw = len(arrs)

    def body(*refs):
        x_refs, out_refs, send_sems, recv_sems = refs[:nw], refs[nw:2 * nw], refs[2 * nw], refs[2 * nw + 1]
        x, y, c = _me()
        cps = []
        for w in range(nw):
            half = arrs[w].shape[1] // 2
            cp = pltpu.make_async_remote_copy(
                src_ref=x_refs[w].at[:, pl.ds((1 - c) * half, half), :], dst_ref=out_refs[w],
                send_sem=send_sems.at[w], recv_sem=recv_sems.at[w], device_id=(x, y, 1 - c), device_id_type=MESH)
            cp.start()
            cps.append(cp)
        for cp in cps:
            cp.wait()

    return pl.pallas_call(
        body, name=name,
        out_shape=[jax.ShapeDtypeStruct((a.shape[0], a.shape[1] // 2, a.shape[2]), a.dtype) for a in arrs],
        in_specs=[_ANY] * nw, out_specs=[_ANY] * nw,
        scratch_shapes=[pltpu.SemaphoreType.DMA((nw,)), pltpu.SemaphoreType.DMA((nw,))],
        compiler_params=_params(),
    )(*arrs)


def _sibling_swap(arrs, *, name):
    nw = len(arrs)

    def body(*refs):
        x_refs, out_refs, send_sems, recv_sems = refs[:nw], refs[nw:2 * nw], refs[2 * nw], refs[2 * nw + 1]
        x, y, c = _me()
        cps = []
        for w in range(nw):
            cp = pltpu.make_async_remote_copy(
                src_ref=x_refs[w], dst_ref=out_refs[w], send_sem=send_sems.at[w], recv_sem=recv_sems.at[w],
                device_id=(x, y, 1 - c), device_id_type=MESH)
            cp.start()
            cps.append(cp)
        for cp in cps:
            cp.wait()

    return pl.pallas_call(
        body, name=name, out_shape=[jax.ShapeDtypeStruct(a.shape, a.dtype) for a in arrs],
        in_specs=[_ANY] * nw, out_specs=[_ANY] * nw,
        scratch_shapes=[pltpu.SemaphoreType.DMA((nw,)), pltpu.SemaphoreType.DMA((nw,))],
        compiler_params=_params(),
    )(*arrs)


def _adamw_halves(w, own, sib, m, v, c_arr, *, name):
    r, cols = w.shape
    h = r // 2
    tm = _tile(h, max(SUBLANES, (192 * 1024) // cols // SUBLANES * SUBLANES), SUBLANES)
    nb = h // tm

    def body(c_ref, w_ref, own_ref, sib_ref, m_ref, v_ref, g_out, d_out, m_out, v_out):
        g = jnp.where(pl.program_id(0) == c_ref[0], own_ref[...], sib_ref[...])
        g_out[...] = g
        d_out[...], m_out[...], v_out[...] = _adamw_math(w_ref[...], g, m_ref[...], v_ref[...])

    full = pl.BlockSpec((tm, cols), lambda hh, i, c_ref: (hh * nb + i, 0))
    half = pl.BlockSpec((tm, cols), lambda hh, i, c_ref: (i, 0))
    return pl.pallas_call(
        body, name=name,
        grid_spec=pltpu.PrefetchScalarGridSpec(num_scalar_prefetch=1, grid=(2, nb),
                                               in_specs=[full, half, half, full, full], out_specs=[full] * 4),
        out_shape=[jax.ShapeDtypeStruct((r, cols), F32)] * 4,
        compiler_params=_params(("parallel", "parallel")),
    )(c_arr, w, own, sib, m, v)


_HBM = pl.BlockSpec(memory_space=pltpu.HBM)
_SEM = pl.BlockSpec(memory_space=pltpu.SEMAPHORE)
_DATAFLOW = pltpu.SideEffectType.DATAFLOW_SIDE_EFFECTING


def _chip_exchange_start(arrs, *, name):
    nw = len(arrs)

    def body(*refs):
        x_refs, land_refs, send_sems, recv_sems = refs[:nw], refs[nw:2 * nw], refs[2 * nw], refs[2 * nw + 1]
        token = refs[-1]
        x, y, c = _me()
        s_me = 2 * x + y
        for w in range(nw):
            for k, (px, py) in enumerate([(1 - x, y), (x, 1 - y), (1 - x, 1 - y)]):
                pltpu.make_async_remote_copy(
                    src_ref=x_refs[w].at[2 * px + py], dst_ref=land_refs[w].at[s_me], send_sem=send_sems.at[3 * w + k],
                    recv_sem=recv_sems.at[3 * w + k], device_id=(px, py, c), device_id_type=MESH).start()
        token[...] = jnp.zeros_like(token)

    hbm = [pltpu.HBM(a.shape, a.dtype) for a in arrs]
    outs = pl.pallas_call(
        body, name=name,
        out_shape=(pltpu.SemaphoreType.DMA((3 * nw,)), pltpu.SemaphoreType.DMA((3 * nw,)), *hbm, *hbm,
                   jax.ShapeDtypeStruct((SUBLANES, LANES), F32)),
        in_specs=[_HBM] * (2 * nw), out_specs=(_SEM, _SEM, *([_HBM] * (2 * nw)), pl.BlockSpec(memory_space=pltpu.VMEM)),
        input_output_aliases={i: 2 + i for i in range(2 * nw)},
        compiler_params=pltpu.CompilerParams(has_side_effects=_DATAFLOW),
    )(*[pltpu.with_memory_space_constraint(a, pltpu.HBM) for a in arrs],
      *[pltpu.with_memory_space_constraint(lax.empty(a.shape, a.dtype), pltpu.HBM) for a in arrs])
    return outs[0], outs[1], list(outs[2:2 + nw]), list(outs[2 + nw:2 + 2 * nw]), outs[-1]


def _allgather_start(shards, *, name):
    nw = len(shards)

    def body(*refs):
        x_refs, land_refs, send_sems, recv_sems = refs[:nw], refs[nw:2 * nw], refs[2 * nw], refs[2 * nw + 1]
        token = refs[-1]
        x, y, c = _me()
        me = 4 * x + 2 * y + c
        for w in range(nw):
            half = shards[w].shape[0] // 2
            own = x_refs[w].at[pl.ds(c * half, half), :]
            for k, to in enumerate([(x, y, 1 - c), (1 - x, y, c), (x, 1 - y, c), (1 - x, 1 - y, c)]):
                pltpu.make_async_remote_copy(
                    src_ref=own, dst_ref=land_refs[w].at[me], send_sem=send_sems.at[4 * w + k],
                    recv_sem=recv_sems.at[4 * w + k], device_id=to, device_id_type=MESH).start()
        token[...] = jnp.zeros_like(token)

    lands = [pltpu.HBM((8, s.shape[0] // 2, s.shape[1]), s.dtype) for s in shards]
    outs = pl.pallas_call(
        body, name=name,
        out_shape=(pltpu.SemaphoreType.DMA((4 * nw,)), pltpu.SemaphoreType.DMA((4 * nw,)),
                   *[pltpu.HBM(s.shape, s.dtype) for s in shards], *lands, jax.ShapeDtypeStruct((SUBLANES, LANES), F32)),
        in_specs=[_HBM] * (2 * nw), out_specs=(_SEM, _SEM, *([_HBM] * (2 * nw)), pl.BlockSpec(memory_space=pltpu.VMEM)),
        input_output_aliases={i: 2 + i for i in range(2 * nw)},
        compiler_params=pltpu.CompilerParams(has_side_effects=_DATAFLOW),
    )(*[pltpu.with_memory_space_constraint(s, pltpu.HBM) for s in shards],
      *[pltpu.with_memory_space_constraint(lax.empty(l.shape, l.dtype), pltpu.HBM) for l in lands])
    return outs[0], outs[1], list(outs[2:2 + nw]), list(outs[2 + nw:2 + 2 * nw]), outs[-1]


def _allgather_wait(send_sems, recv_sems, srcs, lands, after, *, name):
    nw = len(srcs)

    def body(*refs):
        x_refs, land_refs, send_sems, recv_sems = refs[:nw], refs[nw:2 * nw], refs[2 * nw], refs[2 * nw + 1]
        x, y, c = _me()
        for w in range(nw):
            half = srcs[w].shape[0] // 2
            own = x_refs[w].at[pl.ds(c * half, half), :]
            for k, (px, py, pc) in enumerate([(x, y, 1 - c), (1 - x, y, c), (x, 1 - y, c), (1 - x, 1 - y, c)]):
                cp = pltpu.make_async_remote_copy(
                    src_ref=own, dst_ref=land_refs[w].at[4 * px + 2 * py + pc], send_sem=send_sems.at[4 * w + k],
                    recv_sem=recv_sems.at[4 * w + k], device_id=(px, py, pc), device_id_type=MESH)
                cp.wait_send()
                cp.wait_recv()

    outs = pl.pallas_call(
        body, name=name,
        out_shape=(*[pltpu.HBM(a.shape, a.dtype) for a in srcs], *[pltpu.HBM(a.shape, a.dtype) for a in lands]),
        in_specs=[_HBM] * (2 * nw) + [_SEM, _SEM, _ANY], out_specs=tuple([_HBM] * (2 * nw)),
        input_output_aliases={i: i for i in range(2 * nw)},
        compiler_params=pltpu.CompilerParams(has_side_effects=_DATAFLOW),
    )(*srcs, *lands, send_sems, recv_sems, after)
    return list(outs[:nw]), list(outs[nw:])


def _pass_to_sibling(lands, *, name):
    nw = len(lands)

    def body(*refs):
        x_refs, out_refs, send_sems, recv_sems = refs[:nw], refs[nw:2 * nw], refs[2 * nw], refs[2 * nw + 1]
        x, y, c = _me()
        chips = [(1 - x, y), (x, 1 - y), (1 - x, 1 - y)]
        cps = []
        for w in range(nw):
            for k, (px, py) in enumerate(chips):
                cp = pltpu.make_async_remote_copy(
                    src_ref=x_refs[w].at[4 * px + 2 * py + c], dst_ref=out_refs[w].at[4 * px + 2 * py + c],
                    send_sem=send_sems.at[3 * w + k], recv_sem=recv_sems.at[3 * w + k], device_id=(x, y, 1 - c),
                    device_id_type=MESH)
                cp.start()
                cps.append(cp)
        for w in range(nw):
            for k, (px, py) in enumerate(chips):
                pltpu.make_async_remote_copy(
                    src_ref=x_refs[w].at[4 * px + 2 * py + c], dst_ref=out_refs[w].at[4 * px + 2 * py + 1 - c],
                    send_sem=send_sems.at[3 * w + k], recv_sem=recv_sems.at[3 * w + k], device_id=(x, y, 1 - c),
                    device_id_type=MESH).wait_recv()
        for cp in cps:
            cp.wait_send()

    return pl.pallas_call(
        body, name=name, out_shape=[jax.ShapeDtypeStruct(a.shape, a.dtype) for a in lands],
        in_specs=[_ANY] * nw, out_specs=[_ANY] * nw, input_output_aliases={i: i for i in range(nw)},
        scratch_shapes=[pltpu.SemaphoreType.DMA((3 * nw,)), pltpu.SemaphoreType.DMA((3 * nw,))],
        compiler_params=_params(),
    )(*lands)


def _chip_exchange_wait(send_sems, recv_sems, srcs, lands, after, *, name):
    nw = len(srcs)

    def body(*refs):
        x_refs, land_refs, send_sems, recv_sems = refs[:nw], refs[nw:2 * nw], refs[2 * nw], refs[2 * nw + 1]
        x, y, c = _me()
        for w in range(nw):
            for k, (px, py) in enumerate([(1 - x, y), (x, 1 - y), (1 - x, 1 - y)]):
                cp = pltpu.make_async_remote_copy(
                    src_ref=x_refs[w].at[2 * px + py], dst_ref=land_refs[w].at[2 * px + py], send_sem=send_sems.at[3 * w + k],
                    recv_sem=recv_sems.at[3 * w + k], device_id=(px, py, c), device_id_type=MESH)
                cp.wait_send()
                cp.wait_recv()

    hbm = [pltpu.HBM(a.shape, a.dtype) for a in srcs]
    outs = pl.pallas_call(
        body, name=name, out_shape=(*hbm, *hbm),
        in_specs=[_HBM] * (2 * nw) + [_SEM, _SEM, _ANY], out_specs=tuple([_HBM] * (2 * nw)),
        input_output_aliases={i: i for i in range(2 * nw)},
        compiler_params=pltpu.CompilerParams(has_side_effects=_DATAFLOW),
    )(*srcs, *lands, send_sems, recv_sems, after)
    return list(outs[:nw]), list(outs[nw:])


def _sum_slabs(landed, own_src, s_arr, after, *, name, tm=512):
    S, h, w = landed.shape
    tm = _tile(h, tm, 16)

    def body(s_ref, *refs):
        own = refs[S][0].astype(F32)
        acc = None
        for s in range(S):
            term = jnp.where(s_ref[0] == s, own, refs[s][0].astype(F32))
            acc = term if acc is None else acc + term
        refs[S + 2][...] = acc

    def slab(s):
        return pl.BlockSpec((1, tm, w), lambda i, s_ref: (jnp.where(s_ref[0] == s, (s + 1) % S, s), i, 0))

    return pl.pallas_call(
        body, name=name,
        grid_spec=pltpu.PrefetchScalarGridSpec(
            num_scalar_prefetch=1, grid=(h // tm,),
            in_specs=[slab(s) for s in range(S)] + [pl.BlockSpec((1, tm, w), lambda i, s_ref: (s_ref[0], i, 0)), _ANY],
            out_specs=pl.BlockSpec((tm, w), lambda i, s_ref: (i, 0))),
        out_shape=jax.ShapeDtypeStruct((h, w), F32),
        compiler_params=_params(("parallel",)),
    )(s_arr, *([landed] * S), own_src, after)


def _half_add(g, recv, c_arr, *, name):
    S, r, w = g.shape
    h = r // 2
    tm = _tile(h, 512, 16)
    nb = h // tm

    def body(c_ref, g_ref, r_ref, o_ref):
        o_ref[...] = (g_ref[...] + r_ref[...]).astype(BF16)

    return pl.pallas_call(
        body, name=name,
        grid_spec=pltpu.PrefetchScalarGridSpec(
            num_scalar_prefetch=1, grid=(S, nb),
            in_specs=[pl.BlockSpec((1, tm, w), lambda s, i, c_ref: (s, c_ref[0] * nb + i, 0)),
                      pl.BlockSpec((1, tm, w), lambda s, i, c_ref: (s, i, 0))],
            out_specs=pl.BlockSpec((1, tm, w), lambda s, i, c_ref: (s, i, 0))),
        out_shape=jax.ShapeDtypeStruct((S, h, w), BF16),
        compiler_params=_params(("parallel", "parallel")),
    )(c_arr, g, recv)


def _layout(sizes, width, part_mult, total_mult):
    offs, rows, r = [], [], 0
    for n in sizes:
        k = -(-n // width)
        offs.append(r)
        rows.append(k)
        r += -(-k // part_mult) * part_mult
    return offs, rows, -(-r // total_mult) * total_mult


def _pack(arrs, width, part_mult, total_mult, dtype, lead=()):
    nl = len(lead)
    sizes = [math.prod(a.shape[nl:]) for a in arrs]
    offs, rows, total = _layout(sizes, width, part_mult, total_mult)
    parts, r = [], 0
    for a, n, o, k in zip(arrs, sizes, offs, rows):
        kp = -(-k // part_mult) * part_mult
        flat = a.reshape(*lead, n).astype(dtype)
        if kp * width > n:
            flat = jnp.pad(flat, [(0, 0)] * nl + [(0, kp * width - n)])
        parts.append(flat.reshape(*lead, kp, width))
        r = o + kp
    if total > r:
        parts.append(jnp.zeros((*lead, total - r, width), dtype))
    return jnp.concatenate(parts, axis=nl)


def _unpack(pool, shapes, width, part_mult, total_mult):
    lead = pool.shape[:-2]
    sizes = [math.prod(s) for s in shapes]
    offs, rows, _ = _layout(sizes, width, part_mult, total_mult)
    out = []
    for s, n, o, k in zip(shapes, sizes, offs, rows):
        flat = lax.slice_in_dim(pool, o, o + k, axis=len(lead)).reshape(*lead, k * width)
        out.append(lax.slice_in_dim(flat, 0, n, axis=len(lead)).reshape(*lead, *s))
    return out


_WEIGHTS = ("c_ctx", "w_ada", "b_ada", "g_pre_mix", "g_post_mix", "g_pre_ffn", "g_post_ffn", "w_in", "b_merge",
            "dn_conv", "dn_a_log", "dn_dt_bias", "dn_onorm", "lru_conv", "lru_conv_b", "lru_w_rg", "lru_b_rg",
            "lru_w_ig", "lru_b_ig", "lru_lambda", "w_branch_dn", "w_branch_lru", "w_out", "w_up", "ffn_dw",
            "ffn_dw_b", "w_down")
_BIG = {"w_ada": True, "w_in": True, "w_branch_dn": False, "w_branch_lru": False, "w_out": False, "w_up": True,
        "w_down": False}
_SMALL_SHARDED = ("dn_conv", "lru_conv", "lru_b_rg", "lru_b_ig", "lru_lambda", "ffn_dw")
_NCHIP = 4
_FLAT_PART = 8
_FLAT_TOTAL = 256


def _from_chip_shards(s, by_cols):
    if by_cols:
        return s.transpose(1, 0, 2).reshape(s.shape[1], _NCHIP * s.shape[2])
    return s.reshape(_NCHIP * s.shape[1], s.shape[2])


def _dsilu(x):
    s = _sigmoid(x)
    return s * (1.0 + x * (1.0 - s))


def kernel(x, c, ctx, c_ctx, w_ada, b_ada, g_pre_mix, g_post_mix, g_pre_ffn, g_post_ffn, w_in, b_merge, dn_conv, dn_a_log, dn_dt_bias, dn_onorm, lru_conv, lru_conv_b, lru_w_rg, lru_b_rg, lru_w_ig, lru_b_ig, lru_lambda, w_branch_dn, w_branch_lru, w_out, w_up, ffn_dw, ffn_dw_b, w_down, loss_target, m_c_ctx, m_w_ada, m_b_ada, m_g_pre_mix, m_g_post_mix, m_g_pre_ffn, m_g_post_ffn, m_w_in, m_b_merge, m_dn_conv, m_dn_a_log, m_dn_dt_bias, m_dn_onorm, m_lru_conv, m_lru_conv_b, m_lru_w_rg, m_lru_b_rg, m_lru_w_ig, m_lru_b_ig, m_lru_lambda, m_w_branch_dn, m_w_branch_lru, m_w_out, m_w_up, m_ffn_dw, m_ffn_dw_b, m_w_down, v_c_ctx, v_w_ada, v_b_ada, v_g_pre_mix, v_g_post_mix, v_g_pre_ffn, v_g_post_ffn, v_w_in, v_b_merge, v_dn_conv, v_dn_a_log, v_dn_dt_bias, v_dn_onorm, v_lru_conv, v_lru_conv_b, v_lru_w_rg, v_lru_b_rg, v_lru_w_ig, v_lru_b_ig, v_lru_lambda, v_w_branch_dn, v_w_branch_lru, v_w_out, v_w_up, v_ffn_dw, v_ffn_dw_b, v_w_down):
    W = dict(zip(_WEIGHTS, (c_ctx, w_ada, b_ada, g_pre_mix, g_post_mix, g_pre_ffn, g_post_ffn, w_in, b_merge, dn_conv,
                            dn_a_log, dn_dt_bias, dn_onorm, lru_conv, lru_conv_b, lru_w_rg, lru_b_rg, lru_w_ig, lru_b_ig,
                            lru_lambda, w_branch_dn, w_branch_lru, w_out, w_up, ffn_dw, ffn_dw_b, w_down)))
    Mo = dict(zip(_WEIGHTS, (m_c_ctx, m_w_ada, m_b_ada, m_g_pre_mix, m_g_post_mix, m_g_pre_ffn, m_g_post_ffn, m_w_in,
                             m_b_merge, m_dn_conv, m_dn_a_log, m_dn_dt_bias, m_dn_onorm, m_lru_conv, m_lru_conv_b,
                             m_lru_w_rg, m_lru_b_rg, m_lru_w_ig, m_lru_b_ig, m_lru_lambda, m_w_branch_dn,
                             m_w_branch_lru, m_w_out, m_w_up, m_ffn_dw, m_ffn_dw_b, m_w_down)))
    Vo = dict(zip(_WEIGHTS, (v_c_ctx, v_w_ada, v_b_ada, v_g_pre_mix, v_g_post_mix, v_g_pre_ffn, v_g_post_ffn, v_w_in,
                             v_b_merge, v_dn_conv, v_dn_a_log, v_dn_dt_bias, v_dn_onorm, v_lru_conv, v_lru_conv_b,
                             v_lru_w_rg, v_lru_b_rg, v_lru_w_ig, v_lru_b_ig, v_lru_lambda, v_w_branch_dn,
                             v_w_branch_lru, v_w_out, v_w_up, v_ffn_dw, v_ffn_dw_b, v_w_down)))
    B, N, D = x.shape
    NC = ctx.shape[1]
    T = NC + N
    H, HD = dn_a_log.shape[-1], dn_onorm.shape[-1]
    DNW = H * HD
    LW, LBD = lru_conv_b.shape[-1], lru_w_rg.shape[-1]
    DFF = ffn_dw_b.shape[-1]
    LC = LANES
    x_i, y_i, c_i = _me()
    s_me = 2 * x_i + y_i
    tm = _tile(math.gcd(NC, N), 256, 16)

    def whole(n, g):
        r, w_ = W[n].shape[1:]
        return g.reshape(_NCHIP, r, w_) if _BIG[n] else g.reshape(_NCHIP * r, w_)

    first = ("w_ada", "w_in")
    later = tuple(n for n in _BIG if n not in first)
    shard16 = {n: W[n][0].astype(BF16) for n in _BIG}
    me_piece = 4 * x_i + 2 * y_i + c_i

    def with_own(n, gathered):
        src = shard16[n]
        own = lax.dynamic_slice_in_dim(src, c_i * (src.shape[0] // 2), src.shape[0] // 2, axis=0)
        return whole(n, lax.dynamic_update_index_in_dim(gathered, own, me_piece, axis=0))

    full = {n: with_own(n, g) for n, g in zip(first, _allgather_halves([shard16[n] for n in first], name="allgather_first"))}

    small_local = [W[n][0].reshape(-1, W[n].shape[-1]) for n in _SMALL_SHARDED]
    small_shapes = [a.shape for a in small_local]
    spack = _pack(small_local, LANES, _FLAT_PART, _FLAT_PART, F32)
    sgath = _allgather_small(spack)[0::2]
    sfull = {n: _from_chip_shards(s, True)
             for n, s in zip(_SMALL_SHARDED, _unpack(sgath, small_shapes, LANES, _FLAT_PART, _FLAT_PART))}

    later16, sgath = lax.optimization_barrier(([shard16[n] for n in later], sgath))
    ag_send, ag_recv, ag_src, ag_land, ag_token = _allgather_start(later16, name="ag_start")

    o_a = 4 * DNW
    o_xl = o_a + 4 * H
    o_mg = o_xl + 2 * LW
    wi_ = _from_chip_shards(full["w_in"], True)
    nj = LW // LC
    lru_cols = jnp.stack([wi_[:, o_xl:o_xl + LW].reshape(D, nj, LC), wi_[:, o_xl + LW:o_mg].reshape(D, nj, LC)],
                         axis=2).reshape(D, 2 * LW)
    wp = jnp.concatenate([wi_[:, :o_a], lru_cols, wi_[:, o_mg:], wi_[:, o_a:o_xl],
                          jnp.zeros((D, LANES - 4 * H), BF16)], axis=1)
    p_lru, p_mg, p_ab = 4 * DNW, 4 * DNW + 2 * LW, 4 * DNW + 2 * LW + 2 * D
    PW = p_ab + LANES

    MR = LANES
    cond = jnp.concatenate([c, c_ctx[None], jnp.zeros((MR - B - 1, D), F32)], axis=0)
    silu_rows = _rowwise(lambda a: (_silu(a),), [cond], [F32], name="cond_silu")[0]
    mod = _matmul(silu_rows, full["w_ada"], b_shards=(0, _NCHIP), name="ada_fwd") + b_ada + ag_token[0, 0]
    mx = mod[:B].reshape(B, 6, D)
    mc = mod[B].reshape(6, D)
    zero = jnp.zeros((B, D), F32)
    tab = jnp.stack([jnp.stack([jnp.broadcast_to(mc[0], (B, D)), jnp.broadcast_to(mc[1], (B, D))] + [zero] * 6, axis=1),
                     jnp.stack([mx[:, 0], mx[:, 1]] + [zero] * 6, axis=1)], axis=1)
    vecs = jnp.stack([mx[:, 2], mx[:, 3], mx[:, 4], mx[:, 5]] + [zero] * 4, axis=1)
    gains = jnp.concatenate([g_post_mix, g_pre_ffn, g_post_ffn, jnp.zeros((5, D), F32)], axis=0)

    u = _premix_fwd(ctx, x, g_pre_mix, tab, tm=tm)
    p = _matmul(u, wp, name="in_fwd")
    dkw = dict(B=B, T=T, nc=NC, H=H, HD=HD)
    qkv = _dnprep_fwd(p, sfull["dn_conv"], **dkw)
    prm = jnp.concatenate([
        jnp.concatenate([dn_a_log.reshape(1, 2 * H), jnp.zeros((1, LANES - 2 * H), F32)], axis=1),
        jnp.concatenate([dn_dt_bias.reshape(1, 2 * H), jnp.zeros((1, LANES - 2 * H), F32)], axis=1),
        jnp.zeros((6, LANES), F32)], axis=0)
    gtm = _tile(B * T, 512, 16)
    gb = _gb_fwd(p, prm, rows=B * T, col0=p_ab, H=H, tm=gtm)
    y_dn, o_dn, *dn_res = _delta_fwd(qkv, gb, p, dn_onorm, **dkw)
    lv = jnp.concatenate([lru_conv_b, sfull["lru_b_rg"], sfull["lru_b_ig"], sfull["lru_lambda"], jnp.zeros((1, LW), F32)], axis=0)
    wr = _blockdiag(lru_w_rg[0], LC).astype(BF16)
    wi = _blockdiag(lru_w_ig[0], LC).astype(BF16)
    lkw = dict(B=B, T=T, nc=NC, LW=LW, col0=p_lru, C=LC)
    y_lru = _lru_fwd(p, sfull["lru_conv"], lv, wr, wi, **lkw)
    ag_src, ag_land = _allgather_wait(ag_send, ag_recv, ag_src, ag_land, y_lru, name="ag_wait")
    for n, land in zip(later, _pass_to_sibling(ag_land, name="ag_pass")):
        full[n] = with_own(n, land)
    Ydn = _matmul(y_dn, full["w_branch_dn"], name="bdn_fwd")
    Ylru = _matmul(y_lru, full["w_branch_lru"], name="blru_fwd")
    mkw = dict(B=B, T=T, nc=NC, D=D, col0=p_mg, tm=tm)
    mixin = _merge_fwd(p, Ydn, Ylru, b_merge, **mkw)
    mix = _matmul(mixin, full["w_out"], name="out_fwd")
    h1, u2 = _post_fwd(x, mix, gains, vecs, tm=tm)
    F = _matmul(u2, full["w_up"], b_shards=(0, _NCHIP), name="up_fwd")
    w9 = sfull["ffn_dw"]
    ftc = _tile(DFF, 256)
    f, f_pre = _ffn_act_fwd(F, w9, ffn_dw_b, B=B, N=N, DFF=DFF, tc=ftc)
    dn = _matmul(f, full["w_down"], name="down_fwd")
    ddn, dout, sums_f = _final(h1, dn, loss_target, gains, vecs, tm=tm)

    G = {}
    df = _matmul(ddn, full["w_down"], tb=True, name="down_bwd_x")
    G["w_down"] = _matmul(f, ddn, ta=True, name="down_bwd_w")
    dFg, dFv, dwb = _ffn_act_bwd(F, f_pre, w9, df, B=B, N=N, DFF=DFF, tc=ftc)
    hs = _NCHIP // 2
    du2 = _matmul(dFg, full["w_up"], tb=True, b_shards=(0, hs), name="up_bwd_xg")
    du2 = _matmul(dFv, full["w_up"], tb=True, b_shards=(hs, hs), add=du2, name="up_bwd_xv")
    gup = _matmul(u2, dFg, ta=True, out_shards=hs, into=(lax.empty((_NCHIP, D, DFF // hs), F32), 0), name="up_bwd_wg")
    G["w_up"] = _matmul(u2, dFv, ta=True, out_shards=hs, into=(gup, hs), name="up_bwd_wv")
    dx1, dmix, sums_p = _post_bwd(x, mix, gains, vecs, dout, du2, tm=tm)
    dmixin = _matmul(dmix, full["w_out"], tb=True, name="out_bwd_x")
    G["w_out"] = _matmul(mixin, dmix, ta=True, name="out_bwd_w")
    dp = _zero_context_cols(lax.empty((B * T, PW), BF16), B=B, T=T, nc=NC, col0=p_mg, width=2 * D, tm=tm)
    dYdn, dYlru, dp, sums_m = _merge_bwd(p, Ydn, Ylru, b_merge, dmixin, dp, **mkw)
    dy_dn = _matmul(dYdn, full["w_branch_dn"], tb=True, name="bdn_bwd_x")
    G["w_branch_dn"] = _matmul(y_dn, dYdn, ta=True, name="bdn_bwd_w")
    dy_lru = _matmul(dYlru, full["w_branch_lru"], tb=True, name="blru_bwd_x")
    G["w_branch_lru"] = _matmul(y_lru, dYlru, ta=True, name="blru_bwd_w")

    c_arr = c_i.astype(jnp.int32).reshape(1)
    s_arr = s_me.astype(jnp.int32).reshape(1)

    def chip_sums(names, tag):
        slabs = [G[n] if _BIG[n] else G[n].reshape(_NCHIP, G[n].shape[0] // _NCHIP, G[n].shape[1]) for n in names]
        from_sibling = _sibling_send_halves(slabs, name="rs_sibling_" + tag)
        return [_half_add(g, r, c_arr, name="rs_add_" + n) for n, g, r in zip(names, slabs, from_sibling)]

    early = tuple(n for n in _BIG if n in G)
    cx_send, cx_recv, cx_src, cx_land, cx_token = _chip_exchange_start(chip_sums(early, "early"), name="cx_start")
    dp, dcw_l, dlv, dwr, dwi = _lru_bwd(p, sfull["lru_conv"], lv + cx_token[0, 0], wr, wi, dy_lru, dp, **lkw)
    dqkv, dgb, dp, don = _delta_bwd(qkv, gb, p, dn_onorm, o_dn, dn_res, dy_dn, dp, **dkw)
    dp, dprm = _gb_bwd(p, prm, dgb, dp, rows=B * T, col0=p_ab, H=H, tm=gtm)
    dp, dcw_d = _dnprep_bwd(p, sfull["dn_conv"], dqkv, dp, **dkw)
    dwp = _matmul(u, dp, ta=True, name="in_bwd_w")
    segs = [(0, o_a, 0), (o_a, 4 * H, p_ab)]
    segs += [(o_xl + i * LC, LC, p_lru + 2 * i * LC) for i in range(nj)]
    segs += [(o_xl + LW + i * LC, LC, p_lru + (2 * i + 1) * LC) for i in range(nj)]
    segs.append((o_mg, 2 * D, p_mg))
    n_in = W["w_in"].shape[-1]
    slabs_in = []
    for s in range(_NCHIP):
        lo, hi = s * n_in, (s + 1) * n_in
        parts = [dwp[:, q0 + max(lo, c0) - c0:q0 + min(hi, c0 + ln) - c0] for c0, ln, q0 in segs if max(lo, c0) < min(hi, c0 + ln)]
        slabs_in.append(jnp.concatenate(parts, axis=1))
    G["w_in"] = jnp.stack(slabs_in)
    wi_send, wi_recv, wi_src, wi_land, wi_token = _chip_exchange_start(chip_sums(("w_in",), "w_in"), name="cx_in_start")
    dU = _matmul(dp, wp, tb=True, after=wi_token, name="in_bwd_x")
    grad_x, sums_pm = _premix_bwd(ctx, x, g_pre_mix, tab, dU, dx1, tm=tm)

    dmod_x = jnp.stack([sums_pm[:, 1, 0], sums_pm[:, 1, 1], sums_p[:, 0], sums_p[:, 1], sums_p[:, 2], sums_f[:, 0]],
                       axis=1).reshape(B, 6 * D)
    dmod_c = jnp.concatenate([sums_pm[:, 0, 0].sum(0), sums_pm[:, 0, 1].sum(0), jnp.zeros((4 * D,), F32)])[None]
    dmod = jnp.concatenate([dmod_x, dmod_c, jnp.zeros((MR - B - 1, 6 * D), F32)], axis=0)
    G["w_ada"] = _matmul(silu_rows, dmod, ta=True, out_shards=_NCHIP, name="ada_bwd_w")
    dsilu = _matmul(dmod, full["w_ada"], tb=True, b_shards=(0, _NCHIP), name="ada_bwd_x")

    g_small = {
        "c_ctx": dsilu[B] * _dsilu(c_ctx),
        "b_ada": dmod[:B + 1].sum(0)[None],
        "g_pre_mix": sums_pm[:, :, 2].sum((0, 1))[None],
        "g_post_mix": sums_p[:, 3].sum(0)[None],
        "g_pre_ffn": sums_p[:, 4].sum(0)[None],
        "g_post_ffn": sums_f[:, 1].sum(0)[None],
        "b_merge": sums_m[0:1],
        "dn_conv": dcw_d[0:4][None],
        "dn_a_log": dprm[0, :2 * H].reshape(1, 2, H),
        "dn_dt_bias": dprm[1, :2 * H].reshape(1, 2, H),
        "dn_onorm": don[:, 0].sum(0)[None],
        "lru_conv": dcw_l[0:4][None],
        "lru_conv_b": dlv[0:1],
        "lru_w_rg": _blockdiag_extract(dwr, LBD)[None],
        "lru_b_rg": dlv[1:3][None],
        "lru_w_ig": _blockdiag_extract(dwi, LBD)[None],
        "lru_b_ig": dlv[3:5][None],
        "lru_lambda": dlv[5:7][None],
        "ffn_dw": dwb[0:9].reshape(1, 3, 3, DFF),
        "ffn_dw_b": dwb[9:10],
    }
    small_names = tuple(n for n in _WEIGHTS if n not in _BIG)
    loss_part = sums_f[:, 2].sum().reshape(1)
    gs_list = [g_small[n] for n in small_names] + [loss_part]
    gs_shapes = [a.shape for a in gs_list]
    gpack = _pack(gs_list, LANES, _FLAT_PART, _FLAT_TOTAL, F32)
    gsum = _sum_lead(_allgather_small(gpack), name="small_sum", tm=512, mult=SUBLANES)
    gs_red = dict(zip(small_names + ("loss",), _unpack(gsum, gs_shapes, LANES, _FLAT_PART, _FLAT_TOTAL)))
    loss = gs_red["loss"][0]

    grads, deltas, new_m, new_v = {}, {}, {}, {}

    def finish(names, lands, srcs, after, tag):
        halves = [_sum_slabs(l, src, s_arr, after, name="rs_sum_" + n) for n, l, src in zip(names, lands, srcs)]
        outs = None
        for n, own, sib in zip(names, halves, _sibling_swap(halves, name="rs_gather_" + tag)):
            shp = W[n].shape
            outs = _adamw_halves(W[n][0], own, sib, Mo[n][0], Vo[n][0], c_arr, name="adamw_" + n)
            grads[n], deltas[n], new_m[n], new_v[n] = (o.reshape(shp) for o in outs)
        return outs[1]

    cx_src, cx_land = _chip_exchange_wait(cx_send, cx_recv, cx_src, cx_land, dsilu, name="cx_wait")
    ada_sums, gsum = lax.optimization_barrier((chip_sums(("w_ada",), "w_ada"), gsum))
    ad_send, ad_recv, ad_src, ad_land, ad_token = _chip_exchange_start(ada_sums, name="cx_ada_start")
    last_early = finish(early, cx_land, cx_src, ad_token, "early")
    wi_src, wi_land = _chip_exchange_wait(wi_send, wi_recv, wi_src, wi_land, last_early, name="cx_in_wait")
    last_in = finish(("w_in",), wi_land, wi_src, last_early, "w_in")
    ad_src, ad_land = _chip_exchange_wait(ad_send, ad_recv, ad_src, ad_land, last_in, name="cx_ada_wait")
    finish(("w_ada",), ad_land, ad_src, last_in, "w_ada")
    for n in small_names:
        g = gs_red[n]
        if n in _SMALL_SHARDED:
            k = W[n].shape[-1]
            g = lax.dynamic_slice_in_dim(g, s_me * k, k, axis=g.ndim - 1)
        grads[n] = g.reshape(W[n].shape)
    sm_shapes = [W[n].shape for n in small_names]
    pk = lambda d: _pack([d[n] for n in small_names], LANES, _FLAT_PART, _FLAT_TOTAL, F32)
    d_, m_, v_ = _adamw(pk(W), pk(grads), pk(Mo), pk(Vo), name="adamw_small")
    for dst, pool_ in ((deltas, d_), (new_m, m_), (new_v, v_)):
        dst.update(zip(small_names, _unpack(pool_, sm_shapes, LANES, _FLAT_PART, _FLAT_TOTAL)))
    return (loss, grad_x, *[grads[n] for n in _WEIGHTS], *[deltas[n] for n in _WEIGHTS],
            *[new_m[n] for n in _WEIGHTS], *[new_v[n] for n in _WEIGHTS])
```

```python
import functools
import math

import jax
import jax.numpy as jnp
from jax import lax
from jax.experimental import pallas as pl
from jax.experimental.pallas import tpu as pltpu

F32 = jnp.float32
BF16 = jnp.bfloat16
EPS = 1e-6
GRID_W = 64
CHUNK = 256
LRU_C = 8.0
LANES = 128
SUBLANES = 8
VMEM_LIMIT = 56 * 1024 * 1024
ADAM_LR, ADAM_B1, ADAM_B2, ADAM_EPS, ADAM_WD, ADAM_STEP = 0.001, 0.9, 0.999, 1e-08, 0.01, 10
MESH = pl.DeviceIdType.MESH


def _tile(n, target, mult=LANES):
    best = None
    for t in range(mult, min(n, target) + 1, mult):
        if n % t == 0:
            best = t
    return best if best is not None else n


def _params(sem=None, **kw):
    return pltpu.CompilerParams(dimension_semantics=sem, vmem_limit_bytes=VMEM_LIMIT, **kw)


def _sigmoid(x):
    return 1.0 / (1.0 + jnp.exp(-x))


def _silu(x):
    return x * _sigmoid(x)


def _softplus(x):
    return jnp.maximum(x, 0.0) + jnp.log(1.0 + jnp.exp(-jnp.abs(x)))


def _gelu(x):
    return 0.5 * x * (1.0 + jnp.tanh(math.sqrt(2.0 / math.pi) * (x + 0.044715 * x * x * x)))


def _rmsn(u, gain):
    return u * lax.rsqrt(jnp.mean(u * u, axis=-1, keepdims=True) + EPS) * gain


_MM_VMEM = 40 * 1024 * 1024


_ANY_SPEC = pl.BlockSpec(memory_space=pl.ANY)


def _matmul(a, b, *, ta=False, tb=False, add=None, b_shards=None, out_shards=None, into=None, after=None,
            out_dtype=F32, name, tm=1024, tn=2048, tk=2048):
    (K, M) = a.shape if ta else a.shape[::-1]
    if b_shards is not None:
        s0, ns = b_shards
        bsh = (b.shape[1], ns * b.shape[2])
        nsh = b.shape[2]
    else:
        bsh = b.shape
    N = bsh[0] if tb else bsh[1]
    assert (bsh[1] if tb else bsh[0]) == K, (a.shape, b.shape, ta, tb)
    tm = _tile(M, tm)
    tk = _tile(nsh if (b_shards is not None and tb) else K, tk)
    nlim = nsh if (b_shards is not None and not tb) else (N // out_shards if out_shards else N)
    osz = jnp.dtype(out_dtype).itemsize + (4 if add is not None else 0)
    while True:
        tn_ = _tile(nlim, tn)
        need = 2 * (tm * tk * a.dtype.itemsize + tk * tn_ * b.dtype.itemsize + tm * tn_ * osz) + 4 * tm * tn_
        if need <= _MM_VMEM or tn <= LANES:
            break
        tn //= 2
    tn = tn_
    nk = K // tk
    dims = (((0 if ta else 1,), (1 if tb else 0,)), ((), ()))

    def body(a_ref, b_ref, *rest):
        c_ref = rest[0] if add is not None else None
        o_ref, acc_ref = rest[-2:]
        k = pl.program_id(2)

        @pl.when(k == 0)
        def _():
            acc_ref[...] = jnp.zeros_like(acc_ref) if c_ref is None else c_ref[...]

        bv = b_ref[0] if b_shards is not None else b_ref[...]
        acc_ref[...] += lax.dot_general(a_ref[...].astype(BF16), bv.astype(BF16), dims, preferred_element_type=F32)

        @pl.when(k == nk - 1)
        def _():
            if out_shards:
                o_ref[0] = acc_ref[...].astype(out_dtype)
            else:
                o_ref[...] = acc_ref[...].astype(out_dtype)

    a_spec = pl.BlockSpec((tk, tm), lambda i, j, k: (k, i)) if ta else pl.BlockSpec((tm, tk), lambda i, j, k: (i, k))
    if b_shards is None:
        b_spec = pl.BlockSpec((tn, tk), lambda i, j, k: (j, k)) if tb else pl.BlockSpec((tk, tn), lambda i, j, k: (k, j))
    elif tb:
        per = nsh // tk
        b_spec = pl.BlockSpec((1, tn, tk), lambda i, j, k: (s0 + k // per, j, k % per))
    else:
        per = nsh // tn
        b_spec = pl.BlockSpec((1, tk, tn), lambda i, j, k: (s0 + j // per, k, j % per))
    o_spec = pl.BlockSpec((tm, tn), lambda i, j, k: (i, j))
    extra, alias = (), {}
    if out_shards:
        oper = N // out_shards // tn
        o0 = 0
        out_shape = jax.ShapeDtypeStruct((out_shards, M, N // out_shards), out_dtype)
        if into is not None:
            buf, o0 = into
            out_shape = jax.ShapeDtypeStruct(buf.shape, buf.dtype)
            extra, alias = (buf,), {2 + (add is not None): 0}
        out_spec = pl.BlockSpec((1, tm, tn), lambda i, j, k: (o0 + j // oper, i, j % oper))
    else:
        out_spec, out_shape = o_spec, jax.ShapeDtypeStruct((M, N), out_dtype)
    if after is not None:
        extra = extra + (after,)
    return pl.pallas_call(
        body, name=name, grid=(M // tm, N // tn, nk),
        in_specs=[a_spec, b_spec] + ([o_spec] if add is not None else []) + [_ANY_SPEC] * len(extra),
        out_specs=out_spec, out_shape=out_shape, input_output_aliases=alias,
        scratch_shapes=[pltpu.VMEM((tm, tn), F32)],
        compiler_params=_params(("parallel", "parallel", "arbitrary")),
    )(*((a, b) + ((add,) if add is not None else ()) + extra))


def _premix_math(h, gain, shift, scale):
    return _rmsn(h, gain) * (1.0 + scale) + shift


def _stream_specs(nct, tm, D):
    return [pl.BlockSpec((1, tm, D), lambda b, t: (b, jnp.minimum(t, nct - 1), 0)),
            pl.BlockSpec((1, tm, D), lambda b, t: (b, jnp.maximum(t - nct, 0), 0))]


def _premix_fwd(ctx, x, gain, tab, *, tm):
    B, nc, D = ctx.shape
    T = nc + x.shape[1]
    nt, nct = T // tm, nc // tm

    def body(c_ref, x_ref, g_ref, tab_ref, u_ref):
        tabv = tab_ref[0, 0]
        h = jnp.where(pl.program_id(1) < nct, c_ref[0], x_ref[0])
        u_ref[...] = _premix_math(h, g_ref[...], tabv[0:1], tabv[1:2]).astype(BF16)

    return pl.pallas_call(
        body, name="premix_fwd", grid=(B, nt),
        in_specs=_stream_specs(nct, tm, D) + [
            pl.BlockSpec((1, D), lambda b, t: (0, 0)),
            pl.BlockSpec((1, 1, 8, D), lambda b, t: (b, jnp.where(t < nct, 0, 1), 0, 0))],
        out_specs=pl.BlockSpec((tm, D), lambda b, t: (b * nt + t, 0)),
        out_shape=jax.ShapeDtypeStruct((B * T, D), BF16),
        compiler_params=_params(("parallel", "parallel")),
    )(ctx, x, gain, tab)


def _premix_bwd(ctx, x, gain, tab, du, dres, *, tm):
    B, nc, D = ctx.shape
    N = x.shape[1]
    T = nc + N
    nt, nct = T // tm, nc // tm

    def body(c_ref, x_ref, g_ref, tab_ref, du_ref, dres_ref, dx_ref, sums_ref):
        t = pl.program_id(1)
        tabv = tab_ref[0, 0]
        h = jnp.where(t < nct, c_ref[0], x_ref[0])
        _, vjp = jax.vjp(_premix_math, h, g_ref[...], tabv[0:1], tabv[1:2])
        dh, dgain, dshift, dscale = vjp(du_ref[...].astype(F32))

        @pl.when((t == 0) | (t == nct))
        def _():
            sums_ref[...] = jnp.zeros_like(sums_ref)

        sums_ref[0, 0, 0:1, :] += dshift
        sums_ref[0, 0, 1:2, :] += dscale
        sums_ref[0, 0, 2:3, :] += dgain

        @pl.when(t >= nct)
        def _():
            dx_ref[0] = dres_ref[...] + dh

    lat = lambda b, t: jnp.maximum(t - nct, 0)
    return pl.pallas_call(
        body, name="premix_bwd", grid=(B, nt),
        in_specs=_stream_specs(nct, tm, D) + [
            pl.BlockSpec((1, D), lambda b, t: (0, 0)),
            pl.BlockSpec((1, 1, 8, D), lambda b, t: (b, jnp.where(t < nct, 0, 1), 0, 0)),
            pl.BlockSpec((tm, D), lambda b, t: (b * nt + t, 0)),
            pl.BlockSpec((tm, D), lambda b, t: (b * (nt - nct) + lat(b, t), 0))],
        out_specs=[pl.BlockSpec((1, tm, D), lambda b, t: (b, lat(b, t), 0)),
                   pl.BlockSpec((1, 1, 8, D), lambda b, t: (b, jnp.where(t < nct, 0, 1), 0, 0))],
        out_shape=[jax.ShapeDtypeStruct((B, N, D), F32), jax.ShapeDtypeStruct((B, 2, 8, D), F32)],
        compiler_params=_params(("parallel", "arbitrary")),
    )(ctx, x, gain, tab, du, dres)


def _merge_math(mgd, mgl, yd, yl, bd, bl):
    return _sigmoid(mgd + bd) * yd + _sigmoid(mgl + bl) * yl


def _merge_fwd(p, ydn, ylru, b_merge, *, B, T, nc, D, col0, tm):
    N = T - nc
    ntl, nt, nct, cb = N // tm, T // tm, nc // tm, col0 // D

    def body(mgd_ref, mgl_ref, yd_ref, yl_ref, bm_ref, o_ref):
        o_ref[...] = _merge_math(mgd_ref[...], mgl_ref[...], yd_ref[...], yl_ref[...],
                                 bm_ref[:, 0:D], bm_ref[:, D:2 * D]).astype(BF16)

    prow = lambda b, t: b * nt + nct + t
    return pl.pallas_call(
        body, name="merge_fwd", grid=(B, ntl),
        in_specs=[pl.BlockSpec((tm, D), lambda b, t: (prow(b, t), cb)),
                  pl.BlockSpec((tm, D), lambda b, t: (prow(b, t), cb + 1)),
                  pl.BlockSpec((tm, D), lambda b, t: (b * ntl + t, 0)),
                  pl.BlockSpec((tm, D), lambda b, t: (b * ntl + t, 0)),
                  pl.BlockSpec((1, 2 * D), lambda b, t: (0, 0))],
        out_specs=pl.BlockSpec((tm, D), lambda b, t: (b * ntl + t, 0)),
        out_shape=jax.ShapeDtypeStruct((B * N, D), BF16),
        compiler_params=_params(("parallel", "parallel")),
    )(p, p, ydn, ylru, b_merge)


def _merge_bwd(p, ydn, ylru, b_merge, dmix, dp, *, B, T, nc, D, col0, tm):
    N = T - nc
    ntl, nt, nct, cb = N // tm, T // tm, nc // tm, col0 // D
    assert col0 % (2 * D) == 0

    def body(mgd_ref, mgl_ref, yd_ref, yl_ref, bm_ref, dm_ref, dp_any, dyd_ref, dyl_ref, dp_ref, sums_ref):
        _, vjp = jax.vjp(_merge_math, mgd_ref[...], mgl_ref[...], yd_ref[...], yl_ref[...],
                         bm_ref[:, 0:D], bm_ref[:, D:2 * D])
        dmgd, dmgl, dyd, dyl, dbd, dbl = vjp(dm_ref[...])
        dyd_ref[...] = dyd.astype(BF16)
        dyl_ref[...] = dyl.astype(BF16)
        dp_ref[:, 0:D] = dmgd.astype(BF16)
        dp_ref[:, D:2 * D] = dmgl.astype(BF16)

        @pl.when((pl.program_id(0) == 0) & (pl.program_id(1) == 0))
        def _():
            sums_ref[...] = jnp.zeros_like(sums_ref)

        sums_ref[0:1, 0:D] += dbd
        sums_ref[0:1, D:2 * D] += dbl

    prow = lambda b, t: b * nt + nct + t
    row = pl.BlockSpec((tm, D), lambda b, t: (b * ntl + t, 0))
    return pl.pallas_call(
        body, name="merge_bwd", grid=(B, ntl),
        in_specs=[pl.BlockSpec((tm, D), lambda b, t: (prow(b, t), cb)),
                  pl.BlockSpec((tm, D), lambda b, t: (prow(b, t), cb + 1)),
                  row, row, pl.BlockSpec((1, 2 * D), lambda b, t: (0, 0)), row,
                  pl.BlockSpec(memory_space=pl.ANY)],
        out_specs=[row, row,
                   pl.BlockSpec((tm, 2 * D), lambda b, t: (prow(b, t), cb // 2)),
                   pl.BlockSpec((8, 2 * D), lambda b, t: (0, 0))],
        out_shape=[jax.ShapeDtypeStruct((B * N, D), BF16), jax.ShapeDtypeStruct((B * N, D), BF16),
                   jax.ShapeDtypeStruct(dp.shape, dp.dtype), jax.ShapeDtypeStruct((8, 2 * D), F32)],
        input_output_aliases={6: 2},
        compiler_params=_params(("arbitrary", "arbitrary")),
    )(p, p, ydn, ylru, b_merge, dmix, dp)


def _zero_context_cols(dp, *, B, T, nc, col0, width, tm):
    nt, nct = T // tm, nc // tm

    def body(dp_any, o_ref):
        o_ref[...] = jnp.zeros_like(o_ref)

    return pl.pallas_call(
        body, name="dp_zero_ctx", grid=(B, nct), in_specs=[pl.BlockSpec(memory_space=pl.ANY)],
        out_specs=pl.BlockSpec((tm, width), lambda b, t: (b * nt + t, col0 // width)),
        out_shape=jax.ShapeDtypeStruct(dp.shape, dp.dtype), input_output_aliases={0: 0},
        compiler_params=_params(("parallel", "parallel")),
    )(dp)


def _post_math(x, mix, g1, gate, g2, sh, sc):
    h1 = x + _rmsn(mix, g1) * gate
    return h1, _rmsn(h1, g2) * (1.0 + sc) + sh


def _post_fwd(x, mix, gains, vecs, *, tm):
    B, N, D = x.shape
    ntl = N // tm

    def body(x_ref, mix_ref, g_ref, v_ref, h1_ref, u2_ref):
        v = v_ref[0]
        h1, u2 = _post_math(x_ref[0], mix_ref[...], g_ref[0:1], v[0:1], g_ref[1:2], v[1:2], v[2:3])
        h1_ref[...] = h1
        u2_ref[...] = u2.astype(BF16)

    row = pl.BlockSpec((tm, D), lambda b, t: (b * ntl + t, 0))
    return pl.pallas_call(
        body, name="post_fwd", grid=(B, ntl),
        in_specs=[pl.BlockSpec((1, tm, D), lambda b, t: (b, t, 0)), row,
                  pl.BlockSpec((8, D), lambda b, t: (0, 0)), pl.BlockSpec((1, 8, D), lambda b, t: (b, 0, 0))],
        out_specs=[row, row],
        out_shape=[jax.ShapeDtypeStruct((B * N, D), F32), jax.ShapeDtypeStruct((B * N, D), BF16)],
        compiler_params=_params(("parallel", "parallel")),
    )(x, mix, gains, vecs)


def _post_bwd(x, mix, gains, vecs, dh1, du2, *, tm):
    B, N, D = x.shape
    ntl = N // tm

    def body(x_ref, mix_ref, g_ref, v_ref, dh1_ref, du2_ref, dx_ref, dmix_ref, sums_ref):
        v = v_ref[0]
        _, vjp = jax.vjp(_post_math, x_ref[0], mix_ref[...], g_ref[0:1], v[0:1], g_ref[1:2], v[1:2], v[2:3])
        dx, dmix, dg1, dgate, dg2, dsh, dsc = vjp((dh1_ref[...], du2_ref[...]))
        dx_ref[...] = dx
        dmix_ref[...] = dmix.astype(BF16)

        @pl.when(pl.program_id(1) == 0)
        def _():
            sums_ref[...] = jnp.zeros_like(sums_ref)

        sums_ref[0, 0:1, :] += dgate
        sums_ref[0, 1:2, :] += dsh
        sums_ref[0, 2:3, :] += dsc
        sums_ref[0, 3:4, :] += dg1
        sums_ref[0, 4:5, :] += dg2

    row = pl.BlockSpec((tm, D), lambda b, t: (b * ntl + t, 0))
    return pl.pallas_call(
        body, name="post_bwd", grid=(B, ntl),
        in_specs=[pl.BlockSpec((1, tm, D), lambda b, t: (b, t, 0)), row,
                  pl.BlockSpec((8, D), lambda b, t: (0, 0)), pl.BlockSpec((1, 8, D), lambda b, t: (b, 0, 0)), row, row],
        out_specs=[row, row, pl.BlockSpec((1, 8, D), lambda b, t: (b, 0, 0))],
        out_shape=[jax.ShapeDtypeStruct((B * N, D), F32), jax.ShapeDtypeStruct((B * N, D), BF16),
                   jax.ShapeDtypeStruct((B, 8, D), F32)],
        compiler_params=_params(("parallel", "arbitrary")),
    )(x, mix, gains, vecs, dh1, du2)


def _final_math(dn, g4, gate5):
    return _rmsn(dn, g4) * gate5


def _final(h1, dn, target, gains, vecs, *, tm):
    B, N, D = target.shape
    ntl = N // tm

    def body(h1_ref, dn_ref, t_ref, g_ref, v_ref, ddn_ref, dout_ref, sums_ref):
        v = v_ref[0]
        y, vjp = jax.vjp(_final_math, dn_ref[...], g_ref[2:3], v[3:4])
        err = h1_ref[...] + y - t_ref[0]
        dout = err * (1.0 / D)
        ddn, dg4, dgate5 = vjp(dout)
        ddn_ref[...] = ddn.astype(BF16)
        dout_ref[...] = dout

        @pl.when(pl.program_id(1) == 0)
        def _():
            sums_ref[...] = jnp.zeros_like(sums_ref)

        sums_ref[0, 0:1, :] += dgate5
        sums_ref[0, 1:2, :] += dg4
        sums_ref[0, 2:3, :] += jnp.sum(err * err, axis=0, keepdims=True) * (0.5 / D)

    row = pl.BlockSpec((tm, D), lambda b, t: (b * ntl + t, 0))
    return pl.pallas_call(
        body, name="final", grid=(B, ntl),
        in_specs=[row, row, pl.BlockSpec((1, tm, D), lambda b, t: (b, t, 0)),
                  pl.BlockSpec((8, D), lambda b, t: (0, 0)), pl.BlockSpec((1, 8, D), lambda b, t: (b, 0, 0))],
        out_specs=[row, row, pl.BlockSpec((1, 8, D), lambda b, t: (b, 0, 0))],
        out_shape=[jax.ShapeDtypeStruct((B * N, D), BF16), jax.ShapeDtypeStruct((B * N, D), F32),
                   jax.ShapeDtypeStruct((B, 8, D), F32)],
        compiler_params=_params(("parallel", "arbitrary")),
    )(h1, dn, target, gains, vecs)


def _shift(x, s):
    s = s % x.shape[0]
    return x if s == 0 else pltpu.roll(x, s, 0)


def _seg_taps(T, nc, width, pad_left):
    t = lax.broadcasted_iota(jnp.int32, (T, 1), 0)
    pos = jnp.where(t < nc, t, t - nc)
    seg = jnp.where(t < nc, nc, T - nc)
    taps = []
    for k in range(width):
        src = pos + (k - pad_left)
        taps.append((pad_left - k, (src >= 0) & (src < seg)))
    return taps


def _grid_taps(N):
    t = lax.broadcasted_iota(jnp.int32, (N, 1), 0)
    wcol = t % GRID_W
    taps = []
    for dr in (-1, 0, 1):
        for dw in (-1, 0, 1):
            off = dr * GRID_W + dw
            ok = (wcol + dw >= 0) & (wcol + dw < GRID_W) & (t + dr * GRID_W >= 0) & (t + dr * GRID_W < N)
            taps.append((-off, ok))
    return taps


def _conv_fwd(x, w, taps):
    y = jnp.zeros_like(x)
    for k, (s, m) in enumerate(taps):
        y = y + w[k:k + 1] * jnp.where(m, _shift(x, s), 0.0)
    return y


def _conv_bwd(x, w, taps, dy):
    dx = jnp.zeros_like(x)
    dws = []
    for k, (s, m) in enumerate(taps):
        dym = jnp.where(m, dy, 0.0)
        dx = dx + w[k:k + 1] * _shift(dym, -s)
        dws.append(jnp.sum(dym * _shift(x, s), axis=0, keepdims=True))
    return dx, jnp.concatenate(dws, axis=0)


def _ffn_act_fwd(F, w9, bias, *, B, N, DFF, tc):
    nj = DFF // tc

    def body(fg_ref, fv_ref, w_ref, b_ref, o_ref, pre_ref):
        fg = _conv_fwd(fg_ref[...], w_ref[...], _grid_taps(N)) + b_ref[...]
        pre_ref[...] = fg
        o_ref[...] = (_gelu(fg) * fv_ref[...]).astype(BF16)

    col = pl.BlockSpec((N, tc), lambda b, j: (b, j))
    return pl.pallas_call(
        body, name="ffn_act_fwd", grid=(B, nj),
        in_specs=[col, pl.BlockSpec((N, tc), lambda b, j: (b, nj + j)),
                  pl.BlockSpec((9, tc), lambda b, j: (0, j)), pl.BlockSpec((1, tc), lambda b, j: (0, j))],
        out_specs=[col, col],
        out_shape=[jax.ShapeDtypeStruct((B * N, DFF), BF16), jax.ShapeDtypeStruct((B * N, DFF), F32)],
        compiler_params=_params(("parallel", "parallel")),
    )(F, F, w9, bias)


def _ffn_act_bwd(F, pre, w9, df, *, B, N, DFF, tc):
    nj = DFF // tc

    def body(fg_ref, fv_ref, w_ref, pre_ref, df_ref, dfg_ref, dfv_ref, dwb_ref):
        taps = _grid_taps(N)
        x = fg_ref[...]
        fg, vjp = jax.vjp(lambda a: _gelu(a), pre_ref[...])
        dfl = df_ref[...]
        dfv_ref[...] = (dfl * fg).astype(BF16)
        (dpre,) = vjp(dfl * fv_ref[...])
        dx, dw = _conv_bwd(x, w_ref[...], taps, dpre)
        dfg_ref[...] = dx.astype(BF16)

        @pl.when(pl.program_id(1) == 0)
        def _():
            dwb_ref[...] = jnp.zeros_like(dwb_ref)

        dwb_ref[0:9, :] += dw
        dwb_ref[9:10, :] += jnp.sum(dpre, axis=0, keepdims=True)

    col = pl.BlockSpec((N, tc), lambda j, b: (b, j))
    return pl.pallas_call(
        body, name="ffn_act_bwd", grid=(nj, B),
        in_specs=[col, pl.BlockSpec((N, tc), lambda j, b: (b, nj + j)), pl.BlockSpec((9, tc), lambda j, b: (0, j)), col, col],
        out_specs=[col, col, pl.BlockSpec((16, tc), lambda j, b: (0, j))],
        out_shape=[jax.ShapeDtypeStruct((B * N, DFF), BF16), jax.ShapeDtypeStruct((B * N, DFF), BF16),
                   jax.ShapeDtypeStruct((16, DFF), F32)],
        compiler_params=_params(("parallel", "arbitrary")),
    )(F, F, w9, pre, df)


def _dnprep_math(y, is_qk, scale):
    s = _silu(y)
    n = s * lax.rsqrt(jnp.sum(s * s, axis=-1, keepdims=True) + EPS) * scale
    return jnp.where(is_qk, n, s)


def _dnprep_fwd(p, cw, *, B, T, nc, H, HD):
    def body(x_ref, w_ref, o_ref):
        j = pl.program_id(1)
        y = _conv_fwd(x_ref[...], w_ref[...], _seg_taps(T, nc, 4, 2))
        o_ref[...] = _dnprep_math(y, j < 2 * H, jnp.where(j < H, HD ** -0.5, 1.0))

    return pl.pallas_call(
        body, name="dnprep_fwd", grid=(B, 3 * H),
        in_specs=[pl.BlockSpec((T, HD), lambda b, j: (b, j)), pl.BlockSpec((4, HD), lambda b, j: (0, j))],
        out_specs=pl.BlockSpec((T, HD), lambda b, j: (b, j)),
        out_shape=jax.ShapeDtypeStruct((B * T, 3 * H * HD), F32),
        compiler_params=_params(("parallel", "parallel")),
    )(p, cw)


def _dnprep_bwd(p, cw, dqkv, dp, *, B, T, nc, H, HD):
    def body(x_ref, w_ref, d_ref, dp_any, dp_ref, dcw_ref):
        j = pl.program_id(0)
        taps = _seg_taps(T, nc, 4, 2)
        x = x_ref[...]
        y = _conv_fwd(x, w_ref[...], taps)
        is_qk, scale = j < 2 * H, jnp.where(j < H, HD ** -0.5, 1.0)
        _, vjp = jax.vjp(lambda a: _dnprep_math(a, is_qk, scale), y)
        (dy,) = vjp(d_ref[0])
        dx, dw = _conv_bwd(x, w_ref[...], taps, dy)
        dp_ref[...] = dx.astype(BF16)

        @pl.when(pl.program_id(1) == 0)
        def _():
            dcw_ref[...] = jnp.zeros_like(dcw_ref)

        dcw_ref[0:4, :] += dw

    col = pl.BlockSpec((T, HD), lambda j, b: (b, j))
    return pl.pallas_call(
        body, name="dnprep_bwd", grid=(3 * H, B),
        in_specs=[col, pl.BlockSpec((4, HD), lambda j, b: (0, j)),
                  pl.BlockSpec((1, T, HD), lambda j, b: (j // H, b, j % H)), pl.BlockSpec(memory_space=pl.ANY)],
        out_specs=[col, pl.BlockSpec((8, HD), lambda j, b: (0, j))],
        out_shape=[jax.ShapeDtypeStruct(dp.shape, dp.dtype), jax.ShapeDtypeStruct((8, 3 * H * HD), F32)],
        input_output_aliases={3: 0},
        compiler_params=_params(("parallel", "arbitrary")),
    )(p, cw, dqkv, dp)


def _gb_math(ab, alog, dtb, H):
    lane = lax.broadcasted_iota(jnp.int32, ab.shape, 1)
    g = -jnp.exp(alog) * _softplus(ab + dtb)
    return jnp.where(lane < 2 * H, g, jnp.where(lane < 4 * H, _sigmoid(ab), 0.0))


def _gb_fwd(p, prm, *, rows, col0, H, tm):
    def body(x_ref, prm_ref, o_ref):
        o_ref[...] = _gb_math(x_ref[...], prm_ref[0:1], prm_ref[1:2], H)

    return pl.pallas_call(
        body, name="gb_fwd", grid=(rows // tm,),
        in_specs=[pl.BlockSpec((tm, LANES), lambda t: (t, col0 // LANES)), pl.BlockSpec((8, LANES), lambda t: (0, 0))],
        out_specs=pl.BlockSpec((tm, LANES), lambda t: (t, 0)),
        out_shape=jax.ShapeDtypeStruct((rows, LANES), F32),
        compiler_params=_params(("parallel",)),
    )(p, prm)


def _gb_bwd(p, prm, dgb, dp, *, rows, col0, H, tm):
    def body(x_ref, prm_ref, d_ref, dp_any, dp_ref, dprm_ref):
        _, vjp = jax.vjp(lambda a, b, c: _gb_math(a, b, c, H), x_ref[...], prm_ref[0:1], prm_ref[1:2])
        dab, dalog, ddtb = vjp(d_ref[...])
        dp_ref[...] = dab.astype(BF16)

        @pl.when(pl.program_id(0) == 0)
        def _():
            dprm_ref[...] = jnp.zeros_like(dprm_ref)

        dprm_ref[0:1, :] += dalog
        dprm_ref[1:2, :] += ddtb

    blk = pl.BlockSpec((tm, LANES), lambda t: (t, col0 // LANES))
    return pl.pallas_call(
        body, name="gb_bwd", grid=(rows // tm,),
        in_specs=[blk, pl.BlockSpec((8, LANES), lambda t: (0, 0)), pl.BlockSpec((tm, LANES), lambda t: (t, 0)),
                  pl.BlockSpec(memory_space=pl.ANY)],
        out_specs=[blk, pl.BlockSpec((8, LANES), lambda t: (0, 0))],
        out_shape=[jax.ShapeDtypeStruct(dp.shape, dp.dtype), jax.ShapeDtypeStruct((8, LANES), F32)],
        input_output_aliases={3: 0},
        compiler_params=_params(("arbitrary",)),
    )(p, prm, dgb, dp)


def _lru_scans(scans):
    C = scans[0][0].shape[1]
    row = lax.broadcasted_iota(jnp.int32, (SUBLANES, C), 0)
    carries = tuple(jnp.zeros((1, C), F32) for _ in scans)
    for si in range(len(scans[0][4])):
        rows = scans[0][4][si][1]
        assert all(sc[4][si][1] == rows for sc in scans)
        sub = max(s for s in (4, 2, 1) if rows % (s * SUBLANES) == 0)
        span = sub * SUBLANES
        nb = rows // span

        def blk(i, carries, si=si, nb=nb, sub=sub, span=span):
            out = []
            for (a_ref, b_ref, h_ref, hp_ref, segs), carry in zip(scans, carries):
                start, _, reverse = segs[si]
                r0 = pl.multiple_of(start + (nb - 1 - i if reverse else i) * span, span)
                local = []
                for j in range(sub):
                    A = a_ref[pl.ds(r0 + j * SUBLANES, SUBLANES), :]
                    Bv = b_ref[pl.ds(r0 + j * SUBLANES, SUBLANES), :]
                    for s in (1, 2, 4):
                        sh = SUBLANES - s if reverse else s
                        m = (row < SUBLANES - s) if reverse else (row >= s)
                        Bv = jnp.where(m, A * pltpu.roll(Bv, sh, 0) + Bv, Bv)
                        A = jnp.where(m, A * pltpu.roll(A, sh, 0), A)
                    local.append((A, Bv))
                for j in (reversed(range(sub)) if reverse else range(sub)):
                    A, Bv = local[j]
                    Hv = Bv + A * carry
                    h_ref[pl.ds(r0 + j * SUBLANES, SUBLANES), :] = Hv
                    if hp_ref is not None:
                        if reverse:
                            hp = jnp.where(row < SUBLANES - 1, pltpu.roll(Hv, SUBLANES - 1, 0), carry)
                        else:
                            hp = jnp.where(row >= 1, pltpu.roll(Hv, 1, 0), carry)
                        hp_ref[pl.ds(r0 + j * SUBLANES, SUBLANES), :] = hp
                    carry = Hv[0:1] if reverse else Hv[SUBLANES - 1:SUBLANES]
                out.append(carry)
            return tuple(out)

        carries = lax.fori_loop(0, nb, blk, carries)


def _lru_orders(T, nc, d):
    N = T - nc
    if d == 0:
        return [(0, nc, False), (nc, N, False)], [(nc, N, True), (0, nc, True)]
    return [(0, nc, True), (nc, N, True)], [(nc, N, False), (0, nc, False)]


def _bdot(a, b, dims=(((1,), (0,)), ((), ()))):
    return lax.dot_general(a.astype(BF16), b.astype(BF16), dims, preferred_element_type=F32)


_NT = (((1,), (1,)), ((), ()))
_TN = (((0,), (0,)), ((), ()))


def _blockdiag(w, C):
    nd, nb, bd, _ = w.shape
    per = C // bd
    out = jnp.einsum('dnpij,pq->dnpiqj', w.reshape(nd, nb // per, per, bd, bd), jnp.eye(per, dtype=w.dtype))
    return out.reshape(nd, nb // per, C, C)


def _blockdiag_extract(dw, bd):
    nd, nj, C, _ = dw.shape
    per = C // bd
    out = jnp.einsum('dnpiqj,pq->dnpij', dw.reshape(nd, nj, per, bd, per, bd), jnp.eye(per, dtype=dw.dtype))
    return out.reshape(nd, nj * per, bd, bd)


def _lru_fwd(p, cw, lv, wr, wi, *, B, T, nc, LW, col0, C):
    N = T - nc
    nj = LW // C

    def body(x_ref, cw_ref, lv_ref, wr_ref, wi_ref, o_ref, a_s, b_s, h_s):
        lv_ = lv_ref[...]
        xc = _conv_fwd(x_ref[:, 0:C], cw_ref[...], _seg_taps(T, nc, 4, 2)) + lv_[0:1]
        for d in (0, 1):
            r = _sigmoid(_bdot(xc, wr_ref[d, 0]) + lv_[1 + d:2 + d])
            i = _sigmoid(_bdot(xc, wi_ref[d, 0]) + lv_[3 + d:4 + d])
            la = -LRU_C * r * _softplus(-lv_[5 + d:6 + d])
            a_s[d] = jnp.exp(la)
            b_s[d] = jnp.sqrt(1.0 - jnp.exp(2.0 * la)) * i * xc
        _lru_scans([(a_s.at[d], b_s.at[d], h_s.at[d], None, _lru_orders(T, nc, d)[0]) for d in (0, 1)])
        o_ref[...] = ((h_s[0, nc:, :] + h_s[1, nc:, :]) * _gelu(x_ref[nc:, C:2 * C])).astype(BF16)

    return pl.pallas_call(
        body, name="lru_fwd", grid=(B, nj),
        in_specs=[pl.BlockSpec((T, 2 * C), lambda b, j: (b, col0 // (2 * C) + j)),
                  pl.BlockSpec((4, C), lambda b, j: (0, j)), pl.BlockSpec((8, C), lambda b, j: (0, j)),
                  pl.BlockSpec((2, 1, C, C), lambda b, j: (0, j, 0, 0)), pl.BlockSpec((2, 1, C, C), lambda b, j: (0, j, 0, 0))],
        out_specs=pl.BlockSpec((N, C), lambda b, j: (b, j)),
        out_shape=jax.ShapeDtypeStruct((B * N, LW), BF16),
        scratch_shapes=[pltpu.VMEM((2, T, C), F32)] * 3,
        compiler_params=_params(("parallel", "parallel")),
    )(p, cw, lv, wr, wi)


def _lru_bwd(p, cw, lv, wr, wi, dy, dp, *, B, T, nc, LW, col0, C):
    N = T - nc
    nj = LW // C

    def body(x_ref, cw_ref, lv_ref, wr_ref, wi_ref, dy_ref, dp_any, dp_ref, dcw_ref, dlv_ref, dwr_ref, dwi_ref,
             a_s, b_s, h_s, hp_s, mu_s, mup_s, r_s, i_s, dh_s, dxc_s):
        taps = _seg_taps(T, nc, 4, 2)
        lv_ = lv_ref[...]
        xl = x_ref[:, 0:C]
        xc = _conv_fwd(xl, cw_ref[...], taps) + lv_[0:1]
        gel, gelu_vjp = jax.vjp(_gelu, x_ref[nc:, C:2 * C])
        dh_s[0:nc, :] = jnp.zeros((nc, C), F32)
        dh_s[nc:, :] = dy_ref[...] * gel
        dxc_s[...] = jnp.zeros_like(dxc_s)

        @pl.when(pl.program_id(1) == 0)
        def _():
            dcw_ref[...] = jnp.zeros_like(dcw_ref)
            dlv_ref[...] = jnp.zeros_like(dlv_ref)
            dwr_ref[...] = jnp.zeros_like(dwr_ref)
            dwi_ref[...] = jnp.zeros_like(dwi_ref)

        for d in (0, 1):
            r = _sigmoid(_bdot(xc, wr_ref[d, 0]) + lv_[1 + d:2 + d])
            i = _sigmoid(_bdot(xc, wi_ref[d, 0]) + lv_[3 + d:4 + d])
            a = jnp.exp(-LRU_C * r * _softplus(-lv_[5 + d:6 + d]))
            r_s[d] = r
            i_s[d] = i
            a_s[d] = a
            b_s[d] = jnp.sqrt(1.0 - a * a) * i * xc
        _lru_scans([(a_s.at[d], b_s.at[d], h_s.at[d], hp_s.at[d], _lru_orders(T, nc, d)[0]) for d in (0, 1)])
        for d in (0, 1):
            b_s[d] = a_s[d] * dh_s[...]
        _lru_scans([(a_s.at[d], b_s.at[d], mu_s.at[d], mup_s.at[d], _lru_orders(T, nc, d)[1]) for d in (0, 1)])

        for d in (0, 1):
            lam = lv_[5 + d:6 + d]
            sp = _softplus(-lam)
            r, i, a = r_s[d], i_s[d], a_s[d]
            e2 = a * a
            mult = jnp.sqrt(1.0 - e2)
            dinp = dh_s[...] + mup_s[d]
            da = dinp * hp_s[d]
            dmult = dinp * i * xc
            di = dinp * mult * xc
            dla = da * a - dmult * e2 / mult
            dpre_r = (dla * (-LRU_C * sp)) * r * (1.0 - r)
            dpre_i = di * i * (1.0 - i)
            dsp = jnp.sum(dla * (-LRU_C * r), axis=0, keepdims=True)
            dxc_s[...] += dinp * mult * i + _bdot(dpre_r, wr_ref[d, 0], _NT) + _bdot(dpre_i, wi_ref[d, 0], _NT)
            dwr_ref[d, 0] += _bdot(xc, dpre_r, _TN)
            dwi_ref[d, 0] += _bdot(xc, dpre_i, _TN)
            dlv_ref[1 + d:2 + d, :] += jnp.sum(dpre_r, axis=0, keepdims=True)
            dlv_ref[3 + d:4 + d, :] += jnp.sum(dpre_i, axis=0, keepdims=True)
            dlv_ref[5 + d:6 + d, :] += -dsp * _sigmoid(-lam)

        dxc = dxc_s[...]
        dxl, dw = _conv_bwd(xl, cw_ref[...], taps, dxc)
        dcw_ref[0:4, :] += dw
        dlv_ref[0:1, :] += jnp.sum(dxc, axis=0, keepdims=True)
        dp_ref[:, 0:C] = dxl.astype(BF16)
        (dyl,) = gelu_vjp(dy_ref[...] * (h_s[0, nc:, :] + h_s[1, nc:, :]))
        dp_ref[0:nc, C:2 * C] = jnp.zeros((nc, C), BF16)
        dp_ref[nc:, C:2 * C] = dyl.astype(BF16)

    xblk = pl.BlockSpec((T, 2 * C), lambda j, b: (b, col0 // (2 * C) + j))
    wblk = pl.BlockSpec((2, 1, C, C), lambda j, b: (0, j, 0, 0))
    vblk = pl.BlockSpec((8, C), lambda j, b: (0, j))
    return pl.pallas_call(
        body, name="lru_bwd", grid=(nj, B),
        in_specs=[xblk, pl.BlockSpec((4, C), lambda j, b: (0, j)), vblk, wblk, wblk,
                  pl.BlockSpec((N, C), lambda j, b: (b, j)), pl.BlockSpec(memory_space=pl.ANY)],
        out_specs=[xblk, vblk, vblk, wblk, wblk],
        out_shape=[jax.ShapeDtypeStruct(dp.shape, dp.dtype), jax.ShapeDtypeStruct((8, LW), F32),
                   jax.ShapeDtypeStruct((8, LW), F32), jax.ShapeDtypeStruct((2, nj, C, C), F32),
                   jax.ShapeDtypeStruct((2, nj, C, C), F32)],
        scratch_shapes=[pltpu.VMEM((2, T, C), F32)] * 8 + [pltpu.VMEM((T, C), F32)] * 2,
        input_output_aliases={6: 0},
        compiler_params=_params(("parallel", "arbitrary")),
    )(p, cw, lv, wr, wi, dy, dp)


def _chunk_masks(upper):
    i = lax.broadcasted_iota(jnp.int32, (CHUNK, CHUNK), 0)
    j = lax.broadcasted_iota(jnp.int32, (CHUNK, CHUNK), 1)
    ahead = jnp.where(upper, j - i, i - j)
    return i == j, ahead >= 0, ahead > 0


def _col2row(c, eye):
    return jnp.sum(jnp.where(eye, c, 0.0), axis=0, keepdims=True)


def _row2col(r, eye):
    return jnp.sum(jnp.where(eye, r, 0.0), axis=1, keepdims=True)


def _rowsum(x):
    return jnp.sum(x, axis=1, keepdims=True)


_INV_BASE = 8


def _unit_tri_inverses(Ls):
    G = len(Ls)
    W = G * CHUNK
    blk = (lax.broadcasted_iota(jnp.int32, (W, W), 0) // CHUNK) == (lax.broadcasted_iota(jnp.int32, (W, W), 1) // CHUNK)
    ri = lax.broadcasted_iota(jnp.int32, (CHUNK, W), 0)
    ci = lax.broadcasted_iota(jnp.int32, (CHUNK, W), 1) % CHUNK

    def bd(b):
        return jnp.where(blk, jnp.tile(b, (G, 1)), jnp.zeros((), BF16))

    def pdot(a, b):
        return jnp.dot(a.astype(BF16), bd(b.astype(BF16)), preferred_element_type=F32)

    Lc = Ls[0] if G == 1 else jnp.concatenate(Ls, axis=1)
    s = _INV_BASE
    Xp = -jnp.where(ri // s == ci // s, Lc, 0.0)
    Rm = Xp
    for _ in range(int(math.log2(s)) - 1):
        Xp = pdot(Xp, Xp)
        Rm = Rm + Xp + pdot(Rm, Xp)
    while s < CHUNK:
        E = jnp.where((ri // (2 * s) == ci // (2 * s)) & (ri // s != ci // s), Lc, 0.0)
        DE = E + pdot(Rm, E)
        Rm = Rm - (DE + pdot(DE, Rm))
        s *= 2
    eye = _chunk_masks(False)[0]
    return [jnp.where(eye, 1.0, 0.0) + Rm[:, g * CHUNK:(g + 1) * CHUNK] for g in range(G)]


def _delta_chunk_common(q, k, v, gcol, bcol, upper):
    eye, incl, strict = _chunk_masks(upper)
    gc = _rowsum(jnp.where(incl, _col2row(gcol, eye), 0.0))
    D = jnp.where(incl, jnp.exp(jnp.minimum(gc - _col2row(gc, eye), 0.0)), 0.0)
    kb = k * bcol
    AP = _bdot(jnp.concatenate([kb, q], axis=0), k, _NT)
    A = AP[:CHUNK]
    L = jnp.where(strict, A * D, 0.0)
    eg = jnp.exp(gc)
    gl = jnp.sum(gcol, axis=0, keepdims=True)
    attn = jnp.where(incl, AP[CHUNK:] * D, 0.0)
    return dict(eye=eye, incl=incl, strict=strict, gc=gc, D=D, kb=kb, A=A, L=L, eg=eg, gl=gl, egl=jnp.exp(gl),
                attn=attn, kbe=kb * eg, vb=v * bcol, qe=q * eg, kd=k * jnp.exp(gl - gc))


def _delta_group_pre(chunks, upper):
    cs = [_delta_chunk_common(*ch, upper) for ch in chunks]
    out = []
    for c, Tm in zip(cs, _unit_tri_inverses([c["L"] for c in cs])):
        dk = c["kbe"].shape[1]
        wu = _bdot(Tm, jnp.concatenate([c["kbe"], c["vb"]], axis=1))
        KN = _bdot(c["kd"], wu, _TN)
        QO = _bdot(c["attn"], wu)
        out.append((Tm, KN[:, :dk], KN[:, dk:], c["qe"] - QO[:, :dk], QO[:, dk:], c["egl"]))
    return out


def _delta_chunk_bwd(q, k, v, gcol, bcol, S, Tm, do, dS2, upper):
    c = _delta_chunk_common(q, k, v, gcol, bcol, upper)
    eye, incl, strict, D, eg, egl = c["eye"], c["incl"], c["strict"], c["D"], c["eg"], c["egl"]
    kb, kbe, vb, qe, kd, attn = c["kb"], c["kbe"], c["vb"], c["qe"], c["kd"], c["attn"]
    dkk = kbe.shape[1]
    wu = _bdot(Tm, jnp.concatenate([kbe, vb], axis=1))
    w = wu[:, :dkk]
    vn = wu[:, dkk:] - _bdot(w, S)
    dvn = _bdot(kd, dS2) + _bdot(attn, do, _TN)
    dkd = _bdot(vn, dS2, _NT)
    dgl = jnp.sum(_rowsum(dS2 * S), axis=0, keepdims=True) * egl
    dqa = _bdot(do, jnp.concatenate([S, vn], axis=0), _NT)
    dqe = dqa[:, :dkk]
    dattn = jnp.where(incl, dqa[:, dkk:], 0.0)
    dw = -_bdot(dvn, S, _NT)
    r = _rowsum(dkd * kd)
    dk = dkd * jnp.exp(c["gl"] - c["gc"])
    dgl = dgl + jnp.sum(r, axis=0, keepdims=True)
    dgc = _rowsum(dqe * qe) - r
    E = dattn * attn
    dvw = jnp.concatenate([dvn, dw], axis=1)
    dTm = _bdot(dvw, jnp.concatenate([vb, kbe], axis=1), _NT)
    dvk = _bdot(Tm, dvw, _TN)
    dvb = dvk[:, :dvn.shape[1]]
    dv = dvb * bcol
    dbeta = _rowsum(dvb * v)
    dkbe = dvk[:, dvn.shape[1]:]
    dkb = dkbe * eg
    dgc = dgc + _rowsum(dkbe * kbe)
    dL = jnp.where(strict, -_bdot(Tm, _bdot(dTm, Tm, _NT), _TN), 0.0)
    dA = dL * D
    E = E + dL * c["L"]
    PA = jnp.concatenate([dattn * D, dA], axis=0)
    PAk = _bdot(PA, k)
    dq = dqe * eg + PAk[:CHUNK]
    dkb = dkb + PAk[CHUNK:]
    dk = dk + _bdot(PA, jnp.concatenate([q, kb], axis=0), _TN) + dkb * bcol
    dbeta = dbeta + _rowsum(dkb * k)
    dgc = dgc + _rowsum(E) - _row2col(jnp.sum(E, axis=0, keepdims=True), eye)
    dg = _row2col(jnp.sum(jnp.where(incl, dgc, 0.0), axis=0, keepdims=True), eye) + dgl
    return dq, dk, dv, dg, dbeta


def _delta_unroll(trips):
    return max(u for u in (3, 2, 1) if trips % u == 0)


def _delta_group(n):
    return max(g for g in range(1, 2 * LANES // CHUNK + 1) if n % g == 0)


def _delta_chunk_at(T, nc, d, i):
    n, ncc = T // CHUNK, nc // CHUNK
    desc = jnp.where(i < ncc, ncc - 1 - i, n - 1 - (i - ncc))
    if isinstance(d, int):
        return i if d == 0 else desc
    return jnp.where(d == 0, i, desc)


def _dn_out_math(o, onorm, z):
    return _rmsn(o, onorm) * _silu(z)


def _delta_fwd(qkv, gb, p, onorm, *, B, T, nc, H, HD):
    N = T - nc
    n = T // CHUNK
    G = _delta_group(n)

    def body(q_ref, k_ref, v_ref, gb_ref, z_ref, on_ref, y_ref, o_ref, Tm_ref, K_ref, S_ref, Qp_ref, eg_ref,
             N_s, O0_s, o_s):
        h = pl.program_id(1)
        lane = lax.broadcasted_iota(jnp.int32, (CHUNK, LANES), 1)

        def pre(g, carry):
            cs = [g * G + i for i in range(G)]
            rows = [pl.ds(pl.multiple_of(c * CHUNK, CHUNK), CHUNK) for c in cs]
            for d in (0, 1):
                chunks = []
                for r in rows:
                    gbb = gb_ref[r, :]
                    chunks.append((q_ref[r, :], k_ref[r, :], v_ref[r, :],
                                   _rowsum(jnp.where(lane == d * H + h, gbb, 0.0)),
                                   _rowsum(jnp.where(lane == 2 * H + d * H + h, gbb, 0.0))))
                for c, r, (Tm, K, Nn, Qp, O0, egl) in zip(cs, rows, _delta_group_pre(chunks, d == 1)):
                    Tm_ref[0, d * n + c] = Tm
                    K_ref[0, d * n + c] = K.astype(BF16)
                    N_s[d * n + c] = Nn
                    Qp_ref[0, d, r, :] = Qp.astype(BF16)
                    O0_s[d, r, :] = O0
                    eg_ref[0, d * n + c] = jnp.broadcast_to(egl, (SUBLANES, HD))
            return carry

        lax.fori_loop(0, n // G, pre, 0)

        def step(i, Ss):
            out = []
            for d in (0, 1):
                c = _delta_chunk_at(T, nc, d, i)
                rows = pl.ds(pl.multiple_of(c * CHUNK, CHUNK), CHUNK)
                S_ref[0, d * n + c] = Ss[d]
                Sb = Ss[d].astype(BF16)
                o_s[d, rows, :] = jnp.dot(Qp_ref[0, d, rows, :], Sb, preferred_element_type=F32) + O0_s[d, rows, :]
                out.append(eg_ref[0, d * n + c][0:1] * Ss[d] + N_s[d * n + c]
                           - jnp.dot(K_ref[0, d * n + c], Sb, preferred_element_type=F32))
            return tuple(out)

        lax.fori_loop(0, n, step, (jnp.zeros((HD, HD), F32), jnp.zeros((HD, HD), F32)))
        o = o_s[0, nc:, :] + o_s[1, nc:, :]
        o_ref[...] = o
        y_ref[...] = _dn_out_math(o, on_ref[...], z_ref[nc:, :]).astype(BF16)

    col = lambda off: pl.BlockSpec((T, HD), lambda b, h: (b, off + h))
    lat = pl.BlockSpec((N, HD), lambda b, h: (b, h))
    per = lambda *blk: pl.BlockSpec((1, *blk), lambda b, h: (b * H + h, 0, 0, 0))
    return pl.pallas_call(
        body, name="delta_fwd", grid=(B, H),
        in_specs=[col(0), col(H), col(2 * H), pl.BlockSpec((T, LANES), lambda b, h: (b, 0)), col(3 * H),
                  pl.BlockSpec((1, HD), lambda b, h: (0, 0))],
        out_specs=[lat, lat, per(2 * n, CHUNK, CHUNK), per(2 * n, HD, HD), per(2 * n, HD, HD), per(2, T, HD),
                   per(2 * n, SUBLANES, HD)],
        out_shape=[jax.ShapeDtypeStruct((B * N, H * HD), BF16), jax.ShapeDtypeStruct((B * N, H * HD), F32),
                   jax.ShapeDtypeStruct((B * H, 2 * n, CHUNK, CHUNK), F32),
                   jax.ShapeDtypeStruct((B * H, 2 * n, HD, HD), BF16), jax.ShapeDtypeStruct((B * H, 2 * n, HD, HD), F32),
                   jax.ShapeDtypeStruct((B * H, 2, T, HD), BF16), jax.ShapeDtypeStruct((B * H, 2 * n, SUBLANES, HD), F32)],
        scratch_shapes=[pltpu.VMEM((2 * n, HD, HD), F32), pltpu.VMEM((2, T, HD), F32), pltpu.VMEM((2, T, HD), F32)],
        compiler_params=_params(("parallel", "parallel")),
    )(qkv, qkv, qkv, gb, p, onorm)


def _delta_bwd(qkv, gb, p, onorm, o, res, dy, dp, *, B, T, nc, H, HD):
    N = T - nc
    n = T // CHUNK

    def body(q_ref, k_ref, v_ref, gb_ref, z_ref, on_ref, o_ref, dy_ref, Tm_ref, K_ref, S_ref, Qp_ref, eg_ref, dp_any,
             dqkv_ref, dgb_ref, dp_ref, don_ref, do_s, R_s, dS_s):
        h, d = pl.program_id(1), pl.program_id(2)
        lane = lax.broadcasted_iota(jnp.int32, (CHUNK, LANES), 1)

        @pl.when(d == 0)
        def _():
            _, vjp = jax.vjp(_dn_out_math, o_ref[...], on_ref[...], z_ref[nc:, :])
            do, don, dz = vjp(dy_ref[...])
            do_s[0:nc, :] = jnp.zeros((nc, HD), F32)
            do_s[nc:, :] = do
            dp_ref[0:nc, :] = jnp.zeros((nc, HD), BF16)
            dp_ref[nc:, :] = dz.astype(BF16)
            dqkv_ref[...] = jnp.zeros_like(dqkv_ref)

            @pl.when(h == 0)
            def _():
                don_ref[...] = jnp.zeros_like(don_ref)
                dgb_ref[...] = jnp.zeros_like(dgb_ref)

            don_ref[0, 0:1, :] += don

        def r_of(c, carry):
            rows = pl.ds(pl.multiple_of(c * CHUNK, CHUNK), CHUNK)
            R_s[c] = lax.dot_general(Qp_ref[0, 0, rows, :], do_s[rows, :].astype(BF16), _TN, preferred_element_type=F32)
            return carry

        lax.fori_loop(0, n, r_of, 0)

        def bwd_step(i, dS):
            c = _delta_chunk_at(T, nc, d, n - 1 - i)
            dS_s[c] = dS
            return (eg_ref[0, c][0:1] * dS + R_s[c]
                    - lax.dot_general(K_ref[0, c], dS.astype(BF16), _TN, preferred_element_type=F32))

        lax.fori_loop(0, n, bwd_step, jnp.zeros((HD, HD), F32))

        def grads(c, carry):
            rows = pl.ds(pl.multiple_of(c * CHUNK, CHUNK), CHUNK)
            gbb = gb_ref[rows, :]
            gcol = _rowsum(jnp.where(lane == d * H + h, gbb, 0.0))
            bcol = _rowsum(jnp.where(lane == 2 * H + d * H + h, gbb, 0.0))
            dq, dk, dv, dg, dbeta = _delta_chunk_bwd(q_ref[rows, :], k_ref[rows, :], v_ref[rows, :], gcol, bcol,
                                                     S_ref[0, c], Tm_ref[0, c], do_s[rows, :], dS_s[c], d == 1)
            dqkv_ref[0, rows, :] += dq
            dqkv_ref[1, rows, :] += dk
            dqkv_ref[2, rows, :] += dv
            dgb_ref[rows, :] += (jnp.where(lane == d * H + h, dg, 0.0)
                                 + jnp.where(lane == 2 * H + d * H + h, dbeta, 0.0))
            return carry

        lax.fori_loop(0, n, grads, 0, unroll=_delta_unroll(n))

    col = lambda off: pl.BlockSpec((T, HD), lambda b, h, d: (b, off + h))
    lat = pl.BlockSpec((N, HD), lambda b, h, d: (b, h))
    per = lambda *blk: pl.BlockSpec((1, *blk), lambda b, h, d: (b * H + h, d, 0, 0))
    return pl.pallas_call(
        body, name="delta_bwd", grid=(B, H, 2),
        in_specs=[col(0), col(H), col(2 * H), pl.BlockSpec((T, LANES), lambda b, h, d: (b, 0)), col(3 * H),
                  pl.BlockSpec((1, HD), lambda b, h, d: (0, 0)), lat, lat,
                  per(n, CHUNK, CHUNK), per(n, HD, HD), per(n, HD, HD), per(1, T, HD), per(n, SUBLANES, HD),
                  pl.BlockSpec(memory_space=pl.ANY)],
        out_specs=[pl.BlockSpec((3, T, HD), lambda b, h, d: (0, b, h)), pl.BlockSpec((T, LANES), lambda b, h, d: (b, 0)),
                   col(3 * H), pl.BlockSpec((1, 8, HD), lambda b, h, d: (b, 0, 0))],
        out_shape=[jax.ShapeDtypeStruct((3, B * T, H * HD), F32), jax.ShapeDtypeStruct((B * T, LANES), F32),
                   jax.ShapeDtypeStruct(dp.shape, dp.dtype), jax.ShapeDtypeStruct((B, 8, HD), F32)],
        scratch_shapes=[pltpu.VMEM((T, HD), F32), pltpu.VMEM((n, HD, HD), F32), pltpu.VMEM((n, HD, HD), F32)],
        input_output_aliases={13: 2},
        compiler_params=_params(("parallel", "arbitrary", "arbitrary")),
    )(qkv, qkv, qkv, gb, p, onorm, o, dy, *res, dp)


def _rowwise(fn, ins, out_dtypes, *, name, tm=256, mult=16):
    R, W = ins[0].shape
    tm = _tile(R, tm, mult)

    def body(*refs):
        outs = fn(*[r[...] for r in refs[:len(ins)]])
        for o_ref, o in zip(refs[len(ins):], outs):
            o_ref[...] = o.astype(o_ref.dtype)

    spec = pl.BlockSpec((tm, W), lambda i: (i, 0))
    return pl.pallas_call(
        body, name=name, grid=(R // tm,), in_specs=[spec] * len(ins), out_specs=[spec] * len(out_dtypes),
        out_shape=[jax.ShapeDtypeStruct((R, W), dt) for dt in out_dtypes],
        compiler_params=_params(("parallel",)),
    )(*ins)


def _sum_lead(x, *, name, tm=256, mult=16):
    S, R, W = x.shape
    tm = _tile(R, tm, mult)

    def body(*refs):
        acc = refs[0][0].astype(F32)
        for r in refs[1:S]:
            acc = acc + r[0].astype(F32)
        refs[S][...] = acc

    return pl.pallas_call(
        body, name=name, grid=(R // tm,),
        in_specs=[pl.BlockSpec((1, tm, W), functools.partial(lambda s, i: (s, i, 0), s)) for s in range(S)],
        out_specs=pl.BlockSpec((tm, W), lambda i: (i, 0)),
        out_shape=jax.ShapeDtypeStruct((R, W), F32),
        compiler_params=_params(("parallel",)),
    )(*([x] * S))


def _adamw_math(w, g, m, v):
    m = ADAM_B1 * m + (1.0 - ADAM_B1) * g
    v = ADAM_B2 * v + (1.0 - ADAM_B2) * (g * g)
    m_hat = m / (1.0 - ADAM_B1 ** ADAM_STEP)
    v_hat = v / (1.0 - ADAM_B2 ** ADAM_STEP)
    return -ADAM_LR * (m_hat / (jnp.sqrt(v_hat) + ADAM_EPS) + ADAM_WD * w), m, v


def _adamw(w, g, m, v, *, name):
    tm = max(SUBLANES, (256 * 1024) // w.shape[1] // SUBLANES * SUBLANES)
    return _rowwise(_adamw_math, [w, g, m, v], [F32, F32, F32], name=name, tm=tm, mult=SUBLANES)


def _me():
    return lax.axis_index("x"), lax.axis_index("y"), lax.axis_index("c")


def _allgather_small(v):
    R, W = v.shape

    def body(x_ref, out_ref, send_sems, recv_sems, local_sem):
        x, y, c = _me()
        me, sibling = (x, y, c), (x, y, 1 - c)
        chips = [(1 - x, y), (x, 1 - y), (1 - x, 1 - y)]

        def slot(px, py, pc):
            return out_ref.at[4 * px + 2 * py + pc]

        def copy(k, block, to, src=None):
            return pltpu.make_async_remote_copy(
                src_ref=slot(*block) if src is None else src, dst_ref=slot(*block),
                send_sem=send_sems.at[k], recv_sem=recv_sems.at[k], device_id=to, device_id_type=MESH)

        mine = pltpu.make_async_copy(x_ref, slot(*me), local_sem)
        mine.start()
        first = [copy(0, me, sibling, src=x_ref)]
        first += [copy(1 + j, me, (*chip, c), src=x_ref) for j, chip in enumerate(chips)]
        for cp in first:
            cp.start()
        passed = [copy(4 + j, (*chip, c), sibling) for j, chip in enumerate(chips)]
        for j, chip in enumerate(chips):
            copy(1 + j, (*chip, c), me).wait_recv()
            passed[j].start()
        copy(0, sibling, me).wait_recv()
        for j, chip in enumerate(chips):
            copy(4 + j, (*chip, 1 - c), me).wait_recv()
        for cp in first + passed:
            cp.wait_send()
        mine.wait()

    return pl.pallas_call(
        body, name="allgather_small", out_shape=jax.ShapeDtypeStruct((8, R, W), v.dtype),
        in_specs=[pl.BlockSpec(memory_space=pltpu.VMEM)], out_specs=pl.BlockSpec(memory_space=pltpu.VMEM),
        scratch_shapes=[pltpu.SemaphoreType.DMA((7,)), pltpu.SemaphoreType.DMA((7,)), pltpu.SemaphoreType.DMA],
        compiler_params=_params(),
    )(v)


_ANY = pl.BlockSpec(memory_space=pl.ANY)


def _allgather_halves(shards, *, name):
    nw = len(shards)

    def body(*refs):
        x_refs, out_refs = refs[:nw], refs[nw:2 * nw]
        send_sems, recv_sems, local_sems = refs[2 * nw:]
        x, y, c = _me()
        me, sibling = (x, y, c), (x, y, 1 - c)
        chips = [(1 - x, y), (x, 1 - y), (1 - x, 1 - y)]

        def slot(w, px, py, pc):
            return out_refs[w].at[4 * px + 2 * py + pc]

        def copy(w, k, block, to, src=None):
            return pltpu.make_async_remote_copy(
                src_ref=slot(w, *block) if src is None else src, dst_ref=slot(w, *block),
                send_sem=send_sems.at[w, k], recv_sem=recv_sems.at[w, k], device_id=to, device_id_type=MESH)

        started, local = [], []
        for w in range(nw):
            half = shards[w].shape[0] // 2
            own = x_refs[w].at[pl.ds(c * half, half), :]
            mine = pltpu.make_async_copy(own, slot(w, *me), local_sems.at[w])
            mine.start()
            first = [copy(w, 0, me, sibling, src=own)]
            first += [copy(w, 1 + j, me, (*chip, c), src=own) for j, chip in enumerate(chips)]
            for cp in first:
                cp.start()
            started += first
            local.append(mine)
        for w in range(nw):
            for j, chip in enumerate(chips):
                copy(w, 1 + j, (*chip, c), me).wait_recv()
                fwd = copy(w, 4 + j, (*chip, c), sibling)
                fwd.start()
                started.append(fwd)
        for w in range(nw):
            copy(w, 0, sibling, me).wait_recv()
            for j, chip in enumerate(chips):
                copy(w, 4 + j, (*chip, 1 - c), me).wait_recv()
        for cp in started:
            cp.wait_send()
        for cp in local:
            cp.wait()

    return pl.pallas_call(
        body, name=name,
        out_shape=[jax.ShapeDtypeStruct((8, s.shape[0] // 2, s.shape[1]), s.dtype) for s in shards],
        in_specs=[_ANY] * nw, out_specs=[_ANY] * nw,
        scratch_shapes=[pltpu.SemaphoreType.DMA((nw, 7)), pltpu.SemaphoreType.DMA((nw, 7)), pltpu.SemaphoreType.DMA((nw,))],
        compiler_params=_params(),
    )(*shards)


def _sibling_send_halves(arrs, *, name):
    nw = len(arrs)

    def body(*refs):
        x_refs, out_refs, send_sems, recv_sems = refs[:nw], refs[nw:2 * nw], refs[2 * nw], refs[2 * nw + 1]
        x, y, c = _me()
        cps = []
        for w in range(nw):
            half = arrs[w].shape[1] // 2
            cp = pltpu.make_async_remote_copy(
                src_ref=x_refs[w].at[:, pl.ds((1 - c) * half, half), :], dst_ref=out_refs[w],
                send_sem=send_sems.at[w], recv_sem=recv_sems.at[w], device_id=(x, y, 1 - c), device_id_type=MESH)
            cp.start()
            cps.append(cp)
        for cp in cps:
            cp.wait()

    return pl.pallas_call(
        body, name=name,
        out_shape=[jax.ShapeDtypeStruct((a.shape[0], a.shape[1] // 2, a.shape[2]), a.dtype) for a in arrs],
        in_specs=[_ANY] * nw, out_specs=[_ANY] * nw,
        scratch_shapes=[pltpu.SemaphoreType.DMA((nw,)), pltpu.SemaphoreType.DMA((nw,))],
        compiler_params=_params(),
    )(*arrs)


def _sibling_swap(arrs, *, name):
    nw = len(arrs)

    def body(*refs):
        x_refs, out_refs, send_sems, recv_sems = refs[:nw], refs[nw:2 * nw], refs[2 * nw], refs[2 * nw + 1]
        x, y, c = _me()
        cps = []
        for w in range(nw):
            cp = pltpu.make_async_remote_copy(
                src_ref=x_refs[w], dst_ref=out_refs[w], send_sem=send_sems.at[w], recv_sem=recv_sems.at[w],
                device_id=(x, y, 1 - c), device_id_type=MESH)
            cp.start()
            cps.append(cp)
        for cp in cps:
            cp.wait()

    return pl.pallas_call(
        body, name=name, out_shape=[jax.ShapeDtypeStruct(a.shape, a.dtype) for a in arrs],
        in_specs=[_ANY] * nw, out_specs=[_ANY] * nw,
        scratch_shapes=[pltpu.SemaphoreType.DMA((nw,)), pltpu.SemaphoreType.DMA((nw,))],
        compiler_params=_params(),
    )(*arrs)


def _adamw_halves(w, own, sib, m, v, c_arr, *, name):
    r, cols = w.shape
    h = r // 2
    tm = _tile(h, max(SUBLANES, (192 * 1024) // cols // SUBLANES * SUBLANES), SUBLANES)
    nb = h // tm

    def body(c_ref, w_ref, own_ref, sib_ref, m_ref, v_ref, g_out, d_out, m_out, v_out):
        g = jnp.where(pl.program_id(0) == c_ref[0], own_ref[...], sib_ref[...])
        g_out[...] = g
        d_out[...], m_out[...], v_out[...] = _adamw_math(w_ref[...], g, m_ref[...], v_ref[...])

    full = pl.BlockSpec((tm, cols), lambda hh, i, c_ref: (hh * nb + i, 0))
    half = pl.BlockSpec((tm, cols), lambda hh, i, c_ref: (i, 0))
    return pl.pallas_call(
        body, name=name,
        grid_spec=pltpu.PrefetchScalarGridSpec(num_scalar_prefetch=1, grid=(2, nb),
                                               in_specs=[full, half, half, full, full], out_specs=[full] * 4),
        out_shape=[jax.ShapeDtypeStruct((r, cols), F32)] * 4,
        compiler_params=_params(("parallel", "parallel")),
    )(c_arr, w, own, sib, m, v)


_HBM = pl.BlockSpec(memory_space=pltpu.HBM)
_SEM = pl.BlockSpec(memory_space=pltpu.SEMAPHORE)
_DATAFLOW = pltpu.SideEffectType.DATAFLOW_SIDE_EFFECTING


def _chip_exchange_start(arrs, *, name):
    nw = len(arrs)

    def body(*refs):
        x_refs, land_refs, send_sems, recv_sems = refs[:nw], refs[nw:2 * nw], refs[2 * nw], refs[2 * nw + 1]
        token = refs[-1]
        x, y, c = _me()
        s_me = 2 * x + y
        for w in range(nw):
            for k, (px, py) in enumerate([(1 - x, y), (x, 1 - y), (1 - x, 1 - y)]):
                pltpu.make_async_remote_copy(
                    src_ref=x_refs[w].at[2 * px + py], dst_ref=land_refs[w].at[s_me], send_sem=send_sems.at[3 * w + k],
                    recv_sem=recv_sems.at[3 * w + k], device_id=(px, py, c), device_id_type=MESH).start()
        token[...] = jnp.zeros_like(token)

    hbm = [pltpu.HBM(a.shape, a.dtype) for a in arrs]
    outs = pl.pallas_call(
        body, name=name,
        out_shape=(pltpu.SemaphoreType.DMA((3 * nw,)), pltpu.SemaphoreType.DMA((3 * nw,)), *hbm, *hbm,
                   jax.ShapeDtypeStruct((SUBLANES, LANES), F32)),
        in_specs=[_HBM] * (2 * nw), out_specs=(_SEM, _SEM, *([_HBM] * (2 * nw)), pl.BlockSpec(memory_space=pltpu.VMEM)),
        input_output_aliases={i: 2 + i for i in range(2 * nw)},
        compiler_params=pltpu.CompilerParams(has_side_effects=_DATAFLOW),
    )(*[pltpu.with_memory_space_constraint(a, pltpu.HBM) for a in arrs],
      *[pltpu.with_memory_space_constraint(lax.empty(a.shape, a.dtype), pltpu.HBM) for a in arrs])
    return outs[0], outs[1], list(outs[2:2 + nw]), list(outs[2 + nw:2 + 2 * nw]), outs[-1]


def _allgather_start(shards, *, name):
    nw = len(shards)

    def body(*refs):
        x_refs, land_refs, send_sems, recv_sems = refs[:nw], refs[nw:2 * nw], refs[2 * nw], refs[2 * nw + 1]
        token = refs[-1]
        x, y, c = _me()
        me = 4 * x + 2 * y + c
        for w in range(nw):
            half = shards[w].shape[0] // 2
            own = x_refs[w].at[pl.ds(c * half, half), :]
            for k, to in enumerate([(x, y, 1 - c), (1 - x, y, c), (x, 1 - y, c), (1 - x, 1 - y, c)]):
                pltpu.make_async_remote_copy(
                    src_ref=own, dst_ref=land_refs[w].at[me], send_sem=send_sems.at[4 * w + k],
                    recv_sem=recv_sems.at[4 * w + k], device_id=to, device_id_type=MESH).start()
        token[...] = jnp.zeros_like(token)

    lands = [pltpu.HBM((8, s.shape[0] // 2, s.shape[1]), s.dtype) for s in shards]
    outs = pl.pallas_call(
        body, name=name,
        out_shape=(pltpu.SemaphoreType.DMA((4 * nw,)), pltpu.SemaphoreType.DMA((4 * nw,)),
                   *[pltpu.HBM(s.shape, s.dtype) for s in shards], *lands, jax.ShapeDtypeStruct((SUBLANES, LANES), F32)),
        in_specs=[_HBM] * (2 * nw), out_specs=(_SEM, _SEM, *([_HBM] * (2 * nw)), pl.BlockSpec(memory_space=pltpu.VMEM)),
        input_output_aliases={i: 2 + i for i in range(2 * nw)},
        compiler_params=pltpu.CompilerParams(has_side_effects=_DATAFLOW),
    )(*[pltpu.with_memory_space_constraint(s, pltpu.HBM) for s in shards],
      *[pltpu.with_memory_space_constraint(lax.empty(l.shape, l.dtype), pltpu.HBM) for l in lands])
    return outs[0], outs[1], list(outs[2:2 + nw]), list(outs[2 + nw:2 + 2 * nw]), outs[-1]


def _allgather_wait(send_sems, recv_sems, srcs, lands, after, *, name):
    nw = len(srcs)

    def body(*refs):
        x_refs, land_refs, send_sems, recv_sems = refs[:nw], refs[nw:2 * nw], refs[2 * nw], refs[2 * nw + 1]
        x, y, c = _me()
        for w in range(nw):
            half = srcs[w].shape[0] // 2
            own = x_refs[w].at[pl.ds(c * half, half), :]
            for k, (px, py, pc) in enumerate([(x, y, 1 - c), (1 - x, y, c), (x, 1 - y, c), (1 - x, 1 - y, c)]):
                cp = pltpu.make_async_remote_copy(
                    src_ref=own, dst_ref=land_refs[w].at[4 * px + 2 * py + pc], send_sem=send_sems.at[4 * w + k],
                    recv_sem=recv_sems.at[4 * w + k], device_id=(px, py, pc), device_id_type=MESH)
                cp.wait_send()
                cp.wait_recv()

    outs = pl.pallas_call(
        body, name=name,
        out_shape=(*[pltpu.HBM(a.shape, a.dtype) for a in srcs], *[pltpu.HBM(a.shape, a.dtype) for a in lands]),
        in_specs=[_HBM] * (2 * nw) + [_SEM, _SEM, _ANY], out_specs=tuple([_HBM] * (2 * nw)),
        input_output_aliases={i: i for i in range(2 * nw)},
        compiler_params=pltpu.CompilerParams(has_side_effects=_DATAFLOW),
    )(*srcs, *lands, send_sems, recv_sems, after)
    return list(outs[:nw]), list(outs[nw:])


def _pass_to_sibling(lands, *, name):
    nw = len(lands)

    def body(*refs):
        x_refs, out_refs, send_sems, recv_sems = refs[:nw], refs[nw:2 * nw], refs[2 * nw], refs[2 * nw + 1]
        x, y, c = _me()
        chips = [(1 - x, y), (x, 1 - y), (1 - x, 1 - y)]
        cps = []
        for w in range(nw):
            for k, (px, py) in enumerate(chips):
                cp = pltpu.make_async_remote_copy(
                    src_ref=x_refs[w].at[4 * px + 2 * py + c], dst_ref=out_refs[w].at[4 * px + 2 * py + c],
                    send_sem=send_sems.at[3 * w + k], recv_sem=recv_sems.at[3 * w + k], device_id=(x, y, 1 - c),
                    device_id_type=MESH)
                cp.start()
                cps.append(cp)
        for w in range(nw):
            for k, (px, py) in enumerate(chips):
                pltpu.make_async_remote_copy(
                    src_ref=x_refs[w].at[4 * px + 2 * py + c], dst_ref=out_refs[w].at[4 * px + 2 * py + 1 - c],
                    send_sem=send_sems.at[3 * w + k], recv_sem=recv_sems.at[3 * w + k], device_id=(x, y, 1 - c),
                    device_id_type=MESH).wait_recv()
        for cp in cps:
            cp.wait_send()

    return pl.pallas_call(
        body, name=name, out_shape=[jax.ShapeDtypeStruct(a.shape, a.dtype) for a in lands],
        in_specs=[_ANY] * nw, out_specs=[_ANY] * nw, input_output_aliases={i: i for i in range(nw)},
        scratch_shapes=[pltpu.SemaphoreType.DMA((3 * nw,)), pltpu.SemaphoreType.DMA((3 * nw,))],
        compiler_params=_params(),
    )(*lands)


def _chip_exchange_wait(send_sems, recv_sems, srcs, lands, after, *, name):
    nw = len(srcs)

    def body(*refs):
        x_refs, land_refs, send_sems, recv_sems = refs[:nw], refs[nw:2 * nw], refs[2 * nw], refs[2 * nw + 1]
        x, y, c = _me()
        for w in range(nw):
            for k, (px, py) in enumerate([(1 - x, y), (x, 1 - y), (1 - x, 1 - y)]):
                cp = pltpu.make_async_remote_copy(
                    src_ref=x_refs[w].at[2 * px + py], dst_ref=land_refs[w].at[2 * px + py], send_sem=send_sems.at[3 * w + k],
                    recv_sem=recv_sems.at[3 * w + k], device_id=(px, py, c), device_id_type=MESH)
                cp.wait_send()
                cp.wait_recv()

    hbm = [pltpu.HBM(a.shape, a.dtype) for a in srcs]
    outs = pl.pallas_call(
        body, name=name, out_shape=(*hbm, *hbm),
        in_specs=[_HBM] * (2 * nw) + [_SEM, _SEM, _ANY], out_specs=tuple([_HBM] * (2 * nw)),
        input_output_aliases={i: i for i in range(2 * nw)},
        compiler_params=pltpu.CompilerParams(has_side_effects=_DATAFLOW),
    )(*srcs, *lands, send_sems, recv_sems, after)
    return list(outs[:nw]), list(outs[nw:])


def _sum_slabs(landed, own_src, s_arr, after, *, name, tm=512):
    S, h, w = landed.shape
    tm = _tile(h, tm, 16)

    def body(s_ref, *refs):
        own = refs[S][0].astype(F32)
        acc = None
        for s in range(S):
            term = jnp.where(s_ref[0] == s, own, refs[s][0].astype(F32))
            acc = term if acc is None else acc + term
        refs[S + 2][...] = acc

    def slab(s):
        return pl.BlockSpec((1, tm, w), lambda i, s_ref: (jnp.where(s_ref[0] == s, (s + 1) % S, s), i, 0))

    return pl.pallas_call(
        body, name=name,
        grid_spec=pltpu.PrefetchScalarGridSpec(
            num_scalar_prefetch=1, grid=(h // tm,),
            in_specs=[slab(s) for s in range(S)] + [pl.BlockSpec((1, tm, w), lambda i, s_ref: (s_ref[0], i, 0)), _ANY],
            out_specs=pl.BlockSpec((tm, w), lambda i, s_ref: (i, 0))),
        out_shape=jax.ShapeDtypeStruct((h, w), F32),
        compiler_params=_params(("parallel",)),
    )(s_arr, *([landed] * S), own_src, after)


def _half_add(g, recv, c_arr, *, name):
    S, r, w = g.shape
    h = r // 2
    tm = _tile(h, 512, 16)
    nb = h // tm

    def body(c_ref, g_ref, r_ref, o_ref):
        o_ref[...] = (g_ref[...] + r_ref[...]).astype(BF16)

    return pl.pallas_call(
        body, name=name,
        grid_spec=pltpu.PrefetchScalarGridSpec(
            num_scalar_prefetch=1, grid=(S, nb),
            in_specs=[pl.BlockSpec((1, tm, w), lambda s, i, c_ref: (s, c_ref[0] * nb + i, 0)),
                      pl.BlockSpec((1, tm, w), lambda s, i, c_ref: (s, i, 0))],
            out_specs=pl.BlockSpec((1, tm, w), lambda s, i, c_ref: (s, i, 0))),
        out_shape=jax.ShapeDtypeStruct((S, h, w), BF16),
        compiler_params=_params(("parallel", "parallel")),
    )(c_arr, g, recv)


def _layout(sizes, width, part_mult, total_mult):
    offs, rows, r = [], [], 0
    for n in sizes:
        k = -(-n // width)
        offs.append(r)
        rows.append(k)
        r += -(-k // part_mult) * part_mult
    return offs, rows, -(-r // total_mult) * total_mult


def _pack(arrs, width, part_mult, total_mult, dtype, lead=()):
    nl = len(lead)
    sizes = [math.prod(a.shape[nl:]) for a in arrs]
    offs, rows, total = _layout(sizes, width, part_mult, total_mult)
    parts, r = [], 0
    for a, n, o, k in zip(arrs, sizes, offs, rows):
        kp = -(-k // part_mult) * part_mult
        flat = a.reshape(*lead, n).astype(dtype)
        if kp * width > n:
            flat = jnp.pad(flat, [(0, 0)] * nl + [(0, kp * width - n)])
        parts.append(flat.reshape(*lead, kp, width))
        r = o + kp
    if total > r:
        parts.append(jnp.zeros((*lead, total - r, width), dtype))
    return jnp.concatenate(parts, axis=nl)


def _unpack(pool, shapes, width, part_mult, total_mult):
    lead = pool.shape[:-2]
    sizes = [math.prod(s) for s in shapes]
    offs, rows, _ = _layout(sizes, width, part_mult, total_mult)
    out = []
    for s, n, o, k in zip(shapes, sizes, offs, rows):
        flat = lax.slice_in_dim(pool, o, o + k, axis=len(lead)).reshape(*lead, k * width)
        out.append(lax.slice_in_dim(flat, 0, n, axis=len(lead)).reshape(*lead, *s))
    return out


_WEIGHTS = ("c_ctx", "w_ada", "b_ada", "g_pre_mix", "g_post_mix", "g_pre_ffn", "g_post_ffn", "w_in", "b_merge",
            "dn_conv", "dn_a_log", "dn_dt_bias", "dn_onorm", "lru_conv", "lru_conv_b", "lru_w_rg", "lru_b_rg",
            "lru_w_ig", "lru_b_ig", "lru_lambda", "w_branch_dn", "w_branch_lru", "w_out", "w_up", "ffn_dw",
            "ffn_dw_b", "w_down")
_BIG = {"w_ada": True, "w_in": True, "w_branch_dn": False, "w_branch_lru": False, "w_out": False, "w_up": True,
        "w_down": False}
_SMALL_SHARDED = ("dn_conv", "lru_conv", "lru_b_rg", "lru_b_ig", "lru_lambda", "ffn_dw")
_NCHIP = 4
_FLAT_PART = 8
_FLAT_TOTAL = 256


def _from_chip_shards(s, by_cols):
    if by_cols:
        return s.transpose(1, 0, 2).reshape(s.shape[1], _NCHIP * s.shape[2])
    return s.reshape(_NCHIP * s.shape[1], s.shape[2])


def _dsilu(x):
    s = _sigmoid(x)
    return s * (1.0 + x * (1.0 - s))


def kernel(x, c, ctx, c_ctx, w_ada, b_ada, g_pre_mix, g_post_mix, g_pre_ffn, g_post_ffn, w_in, b_merge, dn_conv, dn_a_log, dn_dt_bias, dn_onorm, lru_conv, lru_conv_b, lru_w_rg, lru_b_rg, lru_w_ig, lru_b_ig, lru_lambda, w_branch_dn, w_branch_lru, w_out, w_up, ffn_dw, ffn_dw_b, w_down, loss_target, m_c_ctx, m_w_ada, m_b_ada, m_g_pre_mix, m_g_post_mix, m_g_pre_ffn, m_g_post_ffn, m_w_in, m_b_merge, m_dn_conv, m_dn_a_log, m_dn_dt_bias, m_dn_onorm, m_lru_conv, m_lru_conv_b, m_lru_w_rg, m_lru_b_rg, m_lru_w_ig, m_lru_b_ig, m_lru_lambda, m_w_branch_dn, m_w_branch_lru, m_w_out, m_w_up, m_ffn_dw, m_ffn_dw_b, m_w_down, v_c_ctx, v_w_ada, v_b_ada, v_g_pre_mix, v_g_post_mix, v_g_pre_ffn, v_g_post_ffn, v_w_in, v_b_merge, v_dn_conv, v_dn_a_log, v_dn_dt_bias, v_dn_onorm, v_lru_conv, v_lru_conv_b, v_lru_w_rg, v_lru_b_rg, v_lru_w_ig, v_lru_b_ig, v_lru_lambda, v_w_branch_dn, v_w_branch_lru, v_w_out, v_w_up, v_ffn_dw, v_ffn_dw_b, v_w_down):
    W = dict(zip(_WEIGHTS, (c_ctx, w_ada, b_ada, g_pre_mix, g_post_mix, g_pre_ffn, g_post_ffn, w_in, b_merge, dn_conv,
                            dn_a_log, dn_dt_bias, dn_onorm, lru_conv, lru_conv_b, lru_w_rg, lru_b_rg, lru_w_ig, lru_b_ig,
                            lru_lambda, w_branch_dn, w_branch_lru, w_out, w_up, ffn_dw, ffn_dw_b, w_down)))
    Mo = dict(zip(_WEIGHTS, (m_c_ctx, m_w_ada, m_b_ada, m_g_pre_mix, m_g_post_mix, m_g_pre_ffn, m_g_post_ffn, m_w_in,
                             m_b_merge, m_dn_conv, m_dn_a_log, m_dn_dt_bias, m_dn_onorm, m_lru_conv, m_lru_conv_b,
                             m_lru_w_rg, m_lru_b_rg, m_lru_w_ig, m_lru_b_ig, m_lru_lambda, m_w_branch_dn,
                             m_w_branch_lru, m_w_out, m_w_up, m_ffn_dw, m_ffn_dw_b, m_w_down)))
    Vo = dict(zip(_WEIGHTS, (v_c_ctx, v_w_ada, v_b_ada, v_g_pre_mix, v_g_post_mix, v_g_pre_ffn, v_g_post_ffn, v_w_in,
                             v_b_merge, v_dn_conv, v_dn_a_log, v_dn_dt_bias, v_dn_onorm, v_lru_conv, v_lru_conv_b,
                             v_lru_w_rg, v_lru_b_rg, v_lru_w_ig, v_lru_b_ig, v_lru_lambda, v_w_branch_dn,
                             v_w_branch_lru, v_w_out, v_w_up, v_ffn_dw, v_ffn_dw_b, v_w_down)))
    B, N, D = x.shape
    NC = ctx.shape[1]
    T = NC + N
    H, HD = dn_a_log.shape[-1], dn_onorm.shape[-1]
    DNW = H * HD
    LW, LBD = lru_conv_b.shape[-1], lru_w_rg.shape[-1]
    DFF = ffn_dw_b.shape[-1]
    LC = LANES
    x_i, y_i, c_i = _me()
    s_me = 2 * x_i + y_i
    tm = _tile(math.gcd(NC, N), 256, 16)

    def whole(n, g):
        r, w_ = W[n].shape[1:]
        return g.reshape(_NCHIP, r, w_) if _BIG[n] else g.reshape(_NCHIP * r, w_)

    first = ("w_ada", "w_in")
    later = tuple(n for n in _BIG if n not in first)
    shard16 = {n: W[n][0].astype(BF16) for n in _BIG}
    full = {n: whole(n, g) for n, g in zip(first, _allgather_halves([shard16[n] for n in first], name="allgather_first"))}

    small_local = [W[n][0].reshape(-1, W[n].shape[-1]) for n in _SMALL_SHARDED]
    small_shapes = [a.shape for a in small_local]
    spack = _pack(small_local, LANES, _FLAT_PART, _FLAT_PART, F32)
    sgath = _allgather_small(spack)[0::2]
    sfull = {n: _from_chip_shards(s, True)
             for n, s in zip(_SMALL_SHARDED, _unpack(sgath, small_shapes, LANES, _FLAT_PART, _FLAT_PART))}

    later16, sgath = lax.optimization_barrier(([shard16[n] for n in later], sgath))
    ag_send, ag_recv, ag_src, ag_land, ag_token = _allgather_start(later16, name="ag_start")

    o_a = 4 * DNW
    o_xl = o_a + 4 * H
    o_mg = o_xl + 2 * LW
    wi_ = _from_chip_shards(full["w_in"], True)
    nj = LW // LC
    lru_cols = jnp.stack([wi_[:, o_xl:o_xl + LW].reshape(D, nj, LC), wi_[:, o_xl + LW:o_mg].reshape(D, nj, LC)],
                         axis=2).reshape(D, 2 * LW)
    wp = jnp.concatenate([wi_[:, :o_a], lru_cols, wi_[:, o_mg:], wi_[:, o_a:o_xl],
                          jnp.zeros((D, LANES - 4 * H), BF16)], axis=1)
    p_lru, p_mg, p_ab = 4 * DNW, 4 * DNW + 2 * LW, 4 * DNW + 2 * LW + 2 * D
    PW = p_ab + LANES

    MR = LANES
    cond = jnp.concatenate([c, c_ctx[None], jnp.zeros((MR - B - 1, D), F32)], axis=0)
    silu_rows = _rowwise(lambda a: (_silu(a),), [cond], [F32], name="cond_silu")[0]
    mod = _matmul(silu_rows, full["w_ada"], b_shards=(0, _NCHIP), name="ada_fwd") + b_ada + ag_token[0, 0]
    mx = mod[:B].reshape(B, 6, D)
    mc = mod[B].reshape(6, D)
    zero = jnp.zeros((B, D), F32)
    tab = jnp.stack([jnp.stack([jnp.broadcast_to(mc[0], (B, D)), jnp.broadcast_to(mc[1], (B, D))] + [zero] * 6, axis=1),
                     jnp.stack([mx[:, 0], mx[:, 1]] + [zero] * 6, axis=1)], axis=1)
    vecs = jnp.stack([mx[:, 2], mx[:, 3], mx[:, 4], mx[:, 5]] + [zero] * 4, axis=1)
    gains = jnp.concatenate([g_post_mix, g_pre_ffn, g_post_ffn, jnp.zeros((5, D), F32)], axis=0)

    u = _premix_fwd(ctx, x, g_pre_mix, tab, tm=tm)
    p = _matmul(u, wp, name="in_fwd")
    dkw = dict(B=B, T=T, nc=NC, H=H, HD=HD)
    qkv = _dnprep_fwd(p, sfull["dn_conv"], **dkw)
    prm = jnp.concatenate([
        jnp.concatenate([dn_a_log.reshape(1, 2 * H), jnp.zeros((1, LANES - 2 * H), F32)], axis=1),
        jnp.concatenate([dn_dt_bias.reshape(1, 2 * H), jnp.zeros((1, LANES - 2 * H), F32)], axis=1),
        jnp.zeros((6, LANES), F32)], axis=0)
    gtm = _tile(B * T, 512, 16)
    gb = _gb_fwd(p, prm, rows=B * T, col0=p_ab, H=H, tm=gtm)
    y_dn, o_dn, *dn_res = _delta_fwd(qkv, gb, p, dn_onorm, **dkw)
    lv = jnp.concatenate([lru_conv_b, sfull["lru_b_rg"], sfull["lru_b_ig"], sfull["lru_lambda"], jnp.zeros((1, LW), F32)], axis=0)
    wr = _blockdiag(lru_w_rg[0], LC).astype(BF16)
    wi = _blockdiag(lru_w_ig[0], LC).astype(BF16)
    lkw = dict(B=B, T=T, nc=NC, LW=LW, col0=p_lru, C=LC)
    y_lru = _lru_fwd(p, sfull["lru_conv"], lv, wr, wi, **lkw)
    ag_src, ag_land = _allgather_wait(ag_send, ag_recv, ag_src, ag_land, y_lru, name="ag_wait")
    me_piece = 4 * x_i + 2 * y_i + c_i
    for n, src, land in zip(later, ag_src, _pass_to_sibling(ag_land, name="ag_pass")):
        own = lax.dynamic_slice_in_dim(src, c_i * (src.shape[0] // 2), src.shape[0] // 2, axis=0)
        full[n] = whole(n, lax.dynamic_update_index_in_dim(land, own, me_piece, axis=0))
    Ydn = _matmul(y_dn, full["w_branch_dn"], name="bdn_fwd")
    Ylru = _matmul(y_lru, full["w_branch_lru"], name="blru_fwd")
    mkw = dict(B=B, T=T, nc=NC, D=D, col0=p_mg, tm=tm)
    mixin = _merge_fwd(p, Ydn, Ylru, b_merge, **mkw)
    mix = _matmul(mixin, full["w_out"], name="out_fwd")
    h1, u2 = _post_fwd(x, mix, gains, vecs, tm=tm)
    F = _matmul(u2, full["w_up"], b_shards=(0, _NCHIP), name="up_fwd")
    w9 = sfull["ffn_dw"]
    ftc = _tile(DFF, 256)
    f, f_pre = _ffn_act_fwd(F, w9, ffn_dw_b, B=B, N=N, DFF=DFF, tc=ftc)
    dn = _matmul(f, full["w_down"], name="down_fwd")
    ddn, dout, sums_f = _final(h1, dn, loss_target, gains, vecs, tm=tm)

    G = {}
    df = _matmul(ddn, full["w_down"], tb=True, name="down_bwd_x")
    G["w_down"] = _matmul(f, ddn, ta=True, name="down_bwd_w")
    dFg, dFv, dwb = _ffn_act_bwd(F, f_pre, w9, df, B=B, N=N, DFF=DFF, tc=ftc)
    hs = _NCHIP // 2
    du2 = _matmul(dFg, full["w_up"], tb=True, b_shards=(0, hs), name="up_bwd_xg")
    du2 = _matmul(dFv, full["w_up"], tb=True, b_shards=(hs, hs), add=du2, name="up_bwd_xv")
    gup = _matmul(u2, dFg, ta=True, out_shards=hs, into=(lax.empty((_NCHIP, D, DFF // hs), F32), 0), name="up_bwd_wg")
    G["w_up"] = _matmul(u2, dFv, ta=True, out_shards=hs, into=(gup, hs), name="up_bwd_wv")
    dx1, dmix, sums_p = _post_bwd(x, mix, gains, vecs, dout, du2, tm=tm)
    dmixin = _matmul(dmix, full["w_out"], tb=True, name="out_bwd_x")
    G["w_out"] = _matmul(mixin, dmix, ta=True, name="out_bwd_w")
    dp = _zero_context_cols(lax.empty((B * T, PW), BF16), B=B, T=T, nc=NC, col0=p_mg, width=2 * D, tm=tm)
    dYdn, dYlru, dp, sums_m = _merge_bwd(p, Ydn, Ylru, b_merge, dmixin, dp, **mkw)
    dy_dn = _matmul(dYdn, full["w_branch_dn"], tb=True, name="bdn_bwd_x")
    G["w_branch_dn"] = _matmul(y_dn, dYdn, ta=True, name="bdn_bwd_w")
    dy_lru = _matmul(dYlru, full["w_branch_lru"], tb=True, name="blru_bwd_x")
    G["w_branch_lru"] = _matmul(y_lru, dYlru, ta=True, name="blru_bwd_w")

    c_arr = c_i.astype(jnp.int32).reshape(1)
    s_arr = s_me.astype(jnp.int32).reshape(1)

    def chip_sums(names, tag):
        slabs = [G[n] if _BIG[n] else G[n].reshape(_NCHIP, G[n].shape[0] // _NCHIP, G[n].shape[1]) for n in names]
        from_sibling = _sibling_send_halves(slabs, name="rs_sibling_" + tag)
        return [_half_add(g, r, c_arr, name="rs_add_" + n) for n, g, r in zip(names, slabs, from_sibling)]

    early = tuple(n for n in _BIG if n in G)
    cx_send, cx_recv, cx_src, cx_land, cx_token = _chip_exchange_start(chip_sums(early, "early"), name="cx_start")
    dp, dcw_l, dlv, dwr, dwi = _lru_bwd(p, sfull["lru_conv"], lv + cx_token[0, 0], wr, wi, dy_lru, dp, **lkw)
    dqkv, dgb, dp, don = _delta_bwd(qkv, gb, p, dn_onorm, o_dn, dn_res, dy_dn, dp, **dkw)
    dp, dprm = _gb_bwd(p, prm, dgb, dp, rows=B * T, col0=p_ab, H=H, tm=gtm)
    dp, dcw_d = _dnprep_bwd(p, sfull["dn_conv"], dqkv, dp, **dkw)
    dwp = _matmul(u, dp, ta=True, name="in_bwd_w")
    segs = [(0, o_a, 0), (o_a, 4 * H, p_ab)]
    segs += [(o_xl + i * LC, LC, p_lru + 2 * i * LC) for i in range(nj)]
    segs += [(o_xl + LW + i * LC, LC, p_lru + (2 * i + 1) * LC) for i in range(nj)]
    segs.append((o_mg, 2 * D, p_mg))
    n_in = W["w_in"].shape[-1]
    slabs_in = []
    for s in range(_NCHIP):
        lo, hi = s * n_in, (s + 1) * n_in
        parts = [dwp[:, q0 + max(lo, c0) - c0:q0 + min(hi, c0 + ln) - c0] for c0, ln, q0 in segs if max(lo, c0) < min(hi, c0 + ln)]
        slabs_in.append(jnp.concatenate(parts, axis=1))
    G["w_in"] = jnp.stack(slabs_in)
    wi_send, wi_recv, wi_src, wi_land, wi_token = _chip_exchange_start(chip_sums(("w_in",), "w_in"), name="cx_in_start")
    dU = _matmul(dp, wp, tb=True, after=wi_token, name="in_bwd_x")
    grad_x, sums_pm = _premix_bwd(ctx, x, g_pre_mix, tab, dU, dx1, tm=tm)

    dmod_x = jnp.stack([sums_pm[:, 1, 0], sums_pm[:, 1, 1], sums_p[:, 0], sums_p[:, 1], sums_p[:, 2], sums_f[:, 0]],
                       axis=1).reshape(B, 6 * D)
    dmod_c = jnp.concatenate([sums_pm[:, 0, 0].sum(0), sums_pm[:, 0, 1].sum(0), jnp.zeros((4 * D,), F32)])[None]
    dmod = jnp.concatenate([dmod_x, dmod_c, jnp.zeros((MR - B - 1, 6 * D), F32)], axis=0)
    G["w_ada"] = _matmul(silu_rows, dmod, ta=True, out_shards=_NCHIP, name="ada_bwd_w")
    dsilu = _matmul(dmod, full["w_ada"], tb=True, b_shards=(0, _NCHIP), name="ada_bwd_x")

    g_small = {
        "c_ctx": dsilu[B] * _dsilu(c_ctx),
        "b_ada": dmod[:B + 1].sum(0)[None],
        "g_pre_mix": sums_pm[:, :, 2].sum((0, 1))[None],
        "g_post_mix": sums_p[:, 3].sum(0)[None],
        "g_pre_ffn": sums_p[:, 4].sum(0)[None],
        "g_post_ffn": sums_f[:, 1].sum(0)[None],
        "b_merge": sums_m[0:1],
        "dn_conv": dcw_d[0:4][None],
        "dn_a_log": dprm[0, :2 * H].reshape(1, 2, H),
        "dn_dt_bias": dprm[1, :2 * H].reshape(1, 2, H),
        "dn_onorm": don[:, 0].sum(0)[None],
        "lru_conv": dcw_l[0:4][None],
        "lru_conv_b": dlv[0:1],
        "lru_w_rg": _blockdiag_extract(dwr, LBD)[None],
        "lru_b_rg": dlv[1:3][None],
        "lru_w_ig": _blockdiag_extract(dwi, LBD)[None],
        "lru_b_ig": dlv[3:5][None],
        "lru_lambda": dlv[5:7][None],
        "ffn_dw": dwb[0:9].reshape(1, 3, 3, DFF),
        "ffn_dw_b": dwb[9:10],
    }
    small_names = tuple(n for n in _WEIGHTS if n not in _BIG)
    loss_part = sums_f[:, 2].sum().reshape(1)
    gs_list = [g_small[n] for n in small_names] + [loss_part]
    gs_shapes = [a.shape for a in gs_list]
    gpack = _pack(gs_list, LANES, _FLAT_PART, _FLAT_TOTAL, F32)
    gsum = _sum_lead(_allgather_small(gpack), name="small_sum", tm=512, mult=SUBLANES)
    gs_red = dict(zip(small_names + ("loss",), _unpack(gsum, gs_shapes, LANES, _FLAT_PART, _FLAT_TOTAL)))
    loss = gs_red["loss"][0]

    grads, deltas, new_m, new_v = {}, {}, {}, {}

    def finish(names, lands, srcs, after, tag):
        halves = [_sum_slabs(l, src, s_arr, after, name="rs_sum_" + n) for n, l, src in zip(names, lands, srcs)]
        outs = None
        for n, own, sib in zip(names, halves, _sibling_swap(halves, name="rs_gather_" + tag)):
            shp = W[n].shape
            outs = _adamw_halves(W[n][0], own, sib, Mo[n][0], Vo[n][0], c_arr, name="adamw_" + n)
            grads[n], deltas[n], new_m[n], new_v[n] = (o.reshape(shp) for o in outs)
        return outs[1]

    cx_src, cx_land = _chip_exchange_wait(cx_send, cx_recv, cx_src, cx_land, dsilu, name="cx_wait")
    ada_sums, gsum = lax.optimization_barrier((chip_sums(("w_ada",), "w_ada"), gsum))
    ad_send, ad_recv, ad_src, ad_land, ad_token = _chip_exchange_start(ada_sums, name="cx_ada_start")
    last_early = finish(early, cx_land, cx_src, ad_token, "early")
    wi_src, wi_land = _chip_exchange_wait(wi_send, wi_recv, wi_src, wi_land, last_early, name="cx_in_wait")
    last_in = finish(("w_in",), wi_land, wi_src, last_early, "w_in")
    ad_src, ad_land = _chip_exchange_wait(ad_send, ad_recv, ad_src, ad_land, last_in, name="cx_ada_wait")
    finish(("w_ada",), ad_land, ad_src, last_in, "w_ada")
    for n in small_names:
        g = gs_red[n]
        if n in _SMALL_SHARDED:
            k = W[n].shape[-1]
            g = lax.dynamic_slice_in_dim(g, s_me * k, k, axis=g.ndim - 1)
        grads[n] = g.reshape(W[n].shape)
    sm_shapes = [W[n].shape for n in small_names]
    pk = lambda d: _pack([d[n] for n in small_names], LANES, _FLAT_PART, _FLAT_TOTAL, F32)
    d_, m_, v_ = _adamw(pk(W), pk(grads), pk(Mo), pk(Vo), name="adamw_small")
    for dst, pool_ in ((deltas, d_), (new_m, m_), (new_v, v_)):
        dst.update(zip(small_names, _unpack(pool_, sm_shapes, LANES, _FLAT_PART, _FLAT_TOTAL)))
    return (loss, grad_x, *[grads[n] for n in _WEIGHTS], *[deltas[n] for n in _WEIGHTS],
            *[new_m[n] for n in _WEIGHTS], *[new_v[n] for n in _WEIGHTS])
```

```python
import functools
import math

import jax
import jax.numpy as jnp
from jax import lax
from jax.experimental import pallas as pl
from jax.experimental.pallas import tpu as pltpu

F32 = jnp.float32
BF16 = jnp.bfloat16
EPS = 1e-6
GRID_W = 64
CHUNK = 256
LRU_C = 8.0
LANES = 128
SUBLANES = 8
VMEM_LIMIT = 56 * 1024 * 1024
ADAM_LR, ADAM_B1, ADAM_B2, ADAM_EPS, ADAM_WD, ADAM_STEP = 0.001, 0.9, 0.999, 1e-08, 0.01, 10
MESH = pl.DeviceIdType.MESH


def _tile(n, target, mult=LANES):
    best = None
    for t in range(mult, min(n, target) + 1, mult):
        if n % t == 0:
            best = t
    return best if best is not None else n


def _params(sem=None, **kw):
    return pltpu.CompilerParams(dimension_semantics=sem, vmem_limit_bytes=VMEM_LIMIT, **kw)


def _sigmoid(x):
    return 1.0 / (1.0 + jnp.exp(-x))


def _silu(x):
    return x * _sigmoid(x)


def _softplus(x):
    return jnp.maximum(x, 0.0) + jnp.log(1.0 + jnp.exp(-jnp.abs(x)))


def _gelu(x):
    return 0.5 * x * (1.0 + jnp.tanh(math.sqrt(2.0 / math.pi) * (x + 0.044715 * x * x * x)))


def _rmsn(u, gain):
    return u * lax.rsqrt(jnp.mean(u * u, axis=-1, keepdims=True) + EPS) * gain


_MM_VMEM = 40 * 1024 * 1024


_ANY_SPEC = pl.BlockSpec(memory_space=pl.ANY)


def _matmul(a, b, *, ta=False, tb=False, add=None, b_shards=None, out_shards=None, into=None, after=None,
            out_dtype=F32, name, tm=1024, tn=2048, tk=2048):
    (K, M) = a.shape if ta else a.shape[::-1]
    if b_shards is not None:
        s0, ns = b_shards
        bsh = (b.shape[1], ns * b.shape[2])
        nsh = b.shape[2]
    else:
        bsh = b.shape
    N = bsh[0] if tb else bsh[1]
    assert (bsh[1] if tb else bsh[0]) == K, (a.shape, b.shape, ta, tb)
    tm = _tile(M, tm)
    tk = _tile(nsh if (b_shards is not None and tb) else K, tk)
    nlim = nsh if (b_shards is not None and not tb) else (N // out_shards if out_shards else N)
    osz = jnp.dtype(out_dtype).itemsize + (4 if add is not None else 0)
    while True:
        tn_ = _tile(nlim, tn)
        need = 2 * (tm * tk * a.dtype.itemsize + tk * tn_ * b.dtype.itemsize + tm * tn_ * osz) + 4 * tm * tn_
        if need <= _MM_VMEM or tn <= LANES:
            break
        tn //= 2
    tn = tn_
    nk = K // tk
    dims = (((0 if ta else 1,), (1 if tb else 0,)), ((), ()))

    def body(a_ref, b_ref, *rest):
        c_ref = rest[0] if add is not None else None
        o_ref, acc_ref = rest[-2:]
        k = pl.program_id(2)

        @pl.when(k == 0)
        def _():
            acc_ref[...] = jnp.zeros_like(acc_ref) if c_ref is None else c_ref[...]

        bv = b_ref[0] if b_shards is not None else b_ref[...]
        acc_ref[...] += lax.dot_general(a_ref[...].astype(BF16), bv.astype(BF16), dims, preferred_element_type=F32)

        @pl.when(k == nk - 1)
        def _():
            if out_shards:
                o_ref[0] = acc_ref[...].astype(out_dtype)
            else:
                o_ref[...] = acc_ref[...].astype(out_dtype)

    a_spec = pl.BlockSpec((tk, tm), lambda i, j, k: (k, i)) if ta else pl.BlockSpec((tm, tk), lambda i, j, k: (i, k))
    if b_shards is None:
        b_spec = pl.BlockSpec((tn, tk), lambda i, j, k: (j, k)) if tb else pl.BlockSpec((tk, tn), lambda i, j, k: (k, j))
    elif tb:
        per = nsh // tk
        b_spec = pl.BlockSpec((1, tn, tk), lambda i, j, k: (s0 + k // per, j, k % per))
    else:
        per = nsh // tn
        b_spec = pl.BlockSpec((1, tk, tn), lambda i, j, k: (s0 + j // per, k, j % per))
    o_spec = pl.BlockSpec((tm, tn), lambda i, j, k: (i, j))
    extra, alias = (), {}
    if out_shards:
        oper = N // out_shards // tn
        o0 = 0
        out_shape = jax.ShapeDtypeStruct((out_shards, M, N // out_shards), out_dtype)
        if into is not None:
            buf, o0 = into
            out_shape = jax.ShapeDtypeStruct(buf.shape, buf.dtype)
            extra, alias = (buf,), {2 + (add is not None): 0}
        out_spec = pl.BlockSpec((1, tm, tn), lambda i, j, k: (o0 + j // oper, i, j % oper))
    else:
        out_spec, out_shape = o_spec, jax.ShapeDtypeStruct((M, N), out_dtype)
    if after is not None:
        extra = extra + (after,)
    return pl.pallas_call(
        body, name=name, grid=(M // tm, N // tn, nk),
        in_specs=[a_spec, b_spec] + ([o_spec] if add is not None else []) + [_ANY_SPEC] * len(extra),
        out_specs=out_spec, out_shape=out_shape, input_output_aliases=alias,
        scratch_shapes=[pltpu.VMEM((tm, tn), F32)],
        compiler_params=_params(("parallel", "parallel", "arbitrary")),
    )(*((a, b) + ((add,) if add is not None else ()) + extra))


def _premix_math(h, gain, shift, scale):
    return _rmsn(h, gain) * (1.0 + scale) + shift


def _stream_specs(nct, tm, D):
    return [pl.BlockSpec((1, tm, D), lambda b, t: (b, jnp.minimum(t, nct - 1), 0)),
            pl.BlockSpec((1, tm, D), lambda b, t: (b, jnp.maximum(t - nct, 0), 0))]


def _premix_fwd(ctx, x, gain, tab, *, tm):
    B, nc, D = ctx.shape
    T = nc + x.shape[1]
    nt, nct = T // tm, nc // tm

    def body(c_ref, x_ref, g_ref, tab_ref, u_ref):
        tabv = tab_ref[0, 0]
        h = jnp.where(pl.program_id(1) < nct, c_ref[0], x_ref[0])
        u_ref[...] = _premix_math(h, g_ref[...], tabv[0:1], tabv[1:2]).astype(BF16)

    return pl.pallas_call(
        body, name="premix_fwd", grid=(B, nt),
        in_specs=_stream_specs(nct, tm, D) + [
            pl.BlockSpec((1, D), lambda b, t: (0, 0)),
            pl.BlockSpec((1, 1, 8, D), lambda b, t: (b, jnp.where(t < nct, 0, 1), 0, 0))],
        out_specs=pl.BlockSpec((tm, D), lambda b, t: (b * nt + t, 0)),
        out_shape=jax.ShapeDtypeStruct((B * T, D), BF16),
        compiler_params=_params(("parallel", "parallel")),
    )(ctx, x, gain, tab)


def _premix_bwd(ctx, x, gain, tab, du, dres, *, tm):
    B, nc, D = ctx.shape
    N = x.shape[1]
    T = nc + N
    nt, nct = T // tm, nc // tm

    def body(c_ref, x_ref, g_ref, tab_ref, du_ref, dres_ref, dx_ref, sums_ref):
        t = pl.program_id(1)
        tabv = tab_ref[0, 0]
        h = jnp.where(t < nct, c_ref[0], x_ref[0])
        _, vjp = jax.vjp(_premix_math, h, g_ref[...], tabv[0:1], tabv[1:2])
        dh, dgain, dshift, dscale = vjp(du_ref[...].astype(F32))

        @pl.when((t == 0) | (t == nct))
        def _():
            sums_ref[...] = jnp.zeros_like(sums_ref)

        sums_ref[0, 0, 0:1, :] += dshift
        sums_ref[0, 0, 1:2, :] += dscale
        sums_ref[0, 0, 2:3, :] += dgain

        @pl.when(t >= nct)
        def _():
            dx_ref[0] = dres_ref[...] + dh

    lat = lambda b, t: jnp.maximum(t - nct, 0)
    return pl.pallas_call(
        body, name="premix_bwd", grid=(B, nt),
        in_specs=_stream_specs(nct, tm, D) + [
            pl.BlockSpec((1, D), lambda b, t: (0, 0)),
            pl.BlockSpec((1, 1, 8, D), lambda b, t: (b, jnp.where(t < nct, 0, 1), 0, 0)),
            pl.BlockSpec((tm, D), lambda b, t: (b * nt + t, 0)),
            pl.BlockSpec((tm, D), lambda b, t: (b * (nt - nct) + lat(b, t), 0))],
        out_specs=[pl.BlockSpec((1, tm, D), lambda b, t: (b, lat(b, t), 0)),
                   pl.BlockSpec((1, 1, 8, D), lambda b, t: (b, jnp.where(t < nct, 0, 1), 0, 0))],
        out_shape=[jax.ShapeDtypeStruct((B, N, D), F32), jax.ShapeDtypeStruct((B, 2, 8, D), F32)],
        compiler_params=_params(("parallel", "arbitrary")),
    )(ctx, x, gain, tab, du, dres)


def _merge_math(mgd, mgl, yd, yl, bd, bl):
    return _sigmoid(mgd + bd) * yd + _sigmoid(mgl + bl) * yl


def _merge_fwd(p, ydn, ylru, b_merge, *, B, T, nc, D, col0, tm):
    N = T - nc
    ntl, nt, nct, cb = N // tm, T // tm, nc // tm, col0 // D

    def body(mgd_ref, mgl_ref, yd_ref, yl_ref, bm_ref, o_ref):
        o_ref[...] = _merge_math(mgd_ref[...], mgl_ref[...], yd_ref[...], yl_ref[...],
                                 bm_ref[:, 0:D], bm_ref[:, D:2 * D]).astype(BF16)

    prow = lambda b, t: b * nt + nct + t
    return pl.pallas_call(
        body, name="merge_fwd", grid=(B, ntl),
        in_specs=[pl.BlockSpec((tm, D), lambda b, t: (prow(b, t), cb)),
                  pl.BlockSpec((tm, D), lambda b, t: (prow(b, t), cb + 1)),
                  pl.BlockSpec((tm, D), lambda b, t: (b * ntl + t, 0)),
                  pl.BlockSpec((tm, D), lambda b, t: (b * ntl + t, 0)),
                  pl.BlockSpec((1, 2 * D), lambda b, t: (0, 0))],
        out_specs=pl.BlockSpec((tm, D), lambda b, t: (b * ntl + t, 0)),
        out_shape=jax.ShapeDtypeStruct((B * N, D), BF16),
        compiler_params=_params(("parallel", "parallel")),
    )(p, p, ydn, ylru, b_merge)


def _merge_bwd(p, ydn, ylru, b_merge, dmix, dp, *, B, T, nc, D, col0, tm):
    N = T - nc
    ntl, nt, nct, cb = N // tm, T // tm, nc // tm, col0 // D
    assert col0 % (2 * D) == 0

    def body(mgd_ref, mgl_ref, yd_ref, yl_ref, bm_ref, dm_ref, dp_any, dyd_ref, dyl_ref, dp_ref, sums_ref):
        _, vjp = jax.vjp(_merge_math, mgd_ref[...], mgl_ref[...], yd_ref[...], yl_ref[...],
                         bm_ref[:, 0:D], bm_ref[:, D:2 * D])
        dmgd, dmgl, dyd, dyl, dbd, dbl = vjp(dm_ref[...])
        dyd_ref[...] = dyd.astype(BF16)
        dyl_ref[...] = dyl.astype(BF16)
        dp_ref[:, 0:D] = dmgd.astype(BF16)
        dp_ref[:, D:2 * D] = dmgl.astype(BF16)

        @pl.when((pl.program_id(0) == 0) & (pl.program_id(1) == 0))
        def _():
            sums_ref[...] = jnp.zeros_like(sums_ref)

        sums_ref[0:1, 0:D] += dbd
        sums_ref[0:1, D:2 * D] += dbl

    prow = lambda b, t: b * nt + nct + t
    row = pl.BlockSpec((tm, D), lambda b, t: (b * ntl + t, 0))
    return pl.pallas_call(
        body, name="merge_bwd", grid=(B, ntl),
        in_specs=[pl.BlockSpec((tm, D), lambda b, t: (prow(b, t), cb)),
                  pl.BlockSpec((tm, D), lambda b, t: (prow(b, t), cb + 1)),
                  row, row, pl.BlockSpec((1, 2 * D), lambda b, t: (0, 0)), row,
                  pl.BlockSpec(memory_space=pl.ANY)],
        out_specs=[row, row,
                   pl.BlockSpec((tm, 2 * D), lambda b, t: (prow(b, t), cb // 2)),
                   pl.BlockSpec((8, 2 * D), lambda b, t: (0, 0))],
        out_shape=[jax.ShapeDtypeStruct((B * N, D), BF16), jax.ShapeDtypeStruct((B * N, D), BF16),
                   jax.ShapeDtypeStruct(dp.shape, dp.dtype), jax.ShapeDtypeStruct((8, 2 * D), F32)],
        input_output_aliases={6: 2},
        compiler_params=_params(("arbitrary", "arbitrary")),
    )(p, p, ydn, ylru, b_merge, dmix, dp)


def _zero_context_cols(dp, *, B, T, nc, col0, width, tm):
    nt, nct = T // tm, nc // tm

    def body(dp_any, o_ref):
        o_ref[...] = jnp.zeros_like(o_ref)

    return pl.pallas_call(
        body, name="dp_zero_ctx", grid=(B, nct), in_specs=[pl.BlockSpec(memory_space=pl.ANY)],
        out_specs=pl.BlockSpec((tm, width), lambda b, t: (b * nt + t, col0 // width)),
        out_shape=jax.ShapeDtypeStruct(dp.shape, dp.dtype), input_output_aliases={0: 0},
        compiler_params=_params(("parallel", "parallel")),
    )(dp)


def _post_math(x, mix, g1, gate, g2, sh, sc):
    h1 = x + _rmsn(mix, g1) * gate
    return h1, _rmsn(h1, g2) * (1.0 + sc) + sh


def _post_fwd(x, mix, gains, vecs, *, tm):
    B, N, D = x.shape
    ntl = N // tm

    def body(x_ref, mix_ref, g_ref, v_ref, h1_ref, u2_ref):
        v = v_ref[0]
        h1, u2 = _post_math(x_ref[0], mix_ref[...], g_ref[0:1], v[0:1], g_ref[1:2], v[1:2], v[2:3])
        h1_ref[...] = h1
        u2_ref[...] = u2.astype(BF16)

    row = pl.BlockSpec((tm, D), lambda b, t: (b * ntl + t, 0))
    return pl.pallas_call(
        body, name="post_fwd", grid=(B, ntl),
        in_specs=[pl.BlockSpec((1, tm, D), lambda b, t: (b, t, 0)), row,
                  pl.BlockSpec((8, D), lambda b, t: (0, 0)), pl.BlockSpec((1, 8, D), lambda b, t: (b, 0, 0))],
        out_specs=[row, row],
        out_shape=[jax.ShapeDtypeStruct((B * N, D), F32), jax.ShapeDtypeStruct((B * N, D), BF16)],
        compiler_params=_params(("parallel", "parallel")),
    )(x, mix, gains, vecs)


def _post_bwd(x, mix, gains, vecs, dh1, du2, *, tm):
    B, N, D = x.shape
    ntl = N // tm

    def body(x_ref, mix_ref, g_ref, v_ref, dh1_ref, du2_ref, dx_ref, dmix_ref, sums_ref):
        v = v_ref[0]
        _, vjp = jax.vjp(_post_math, x_ref[0], mix_ref[...], g_ref[0:1], v[0:1], g_ref[1:2], v[1:2], v[2:3])
        dx, dmix, dg1, dgate, dg2, dsh, dsc = vjp((dh1_ref[...], du2_ref[...]))
        dx_ref[...] = dx
        dmix_ref[...] = dmix.astype(BF16)

        @pl.when(pl.program_id(1) == 0)
        def _():
            sums_ref[...] = jnp.zeros_like(sums_ref)

        sums_ref[0, 0:1, :] += dgate
        sums_ref[0, 1:2, :] += dsh
        sums_ref[0, 2:3, :] += dsc
        sums_ref[0, 3:4, :] += dg1
        sums_ref[0, 4:5, :] += dg2

    row = pl.BlockSpec((tm, D), lambda b, t: (b * ntl + t, 0))
    return pl.pallas_call(
        body, name="post_bwd", grid=(B, ntl),
        in_specs=[pl.BlockSpec((1, tm, D), lambda b, t: (b, t, 0)), row,
                  pl.BlockSpec((8, D), lambda b, t: (0, 0)), pl.BlockSpec((1, 8, D), lambda b, t: (b, 0, 0)), row, row],
        out_specs=[row, row, pl.BlockSpec((1, 8, D), lambda b, t: (b, 0, 0))],
        out_shape=[jax.ShapeDtypeStruct((B * N, D), F32), jax.ShapeDtypeStruct((B * N, D), BF16),
                   jax.ShapeDtypeStruct((B, 8, D), F32)],
        compiler_params=_params(("parallel", "arbitrary")),
    )(x, mix, gains, vecs, dh1, du2)


def _final_math(dn, g4, gate5):
    return _rmsn(dn, g4) * gate5


def _final(h1, dn, target, gains, vecs, *, tm):
    B, N, D = target.shape
    ntl = N // tm

    def body(h1_ref, dn_ref, t_ref, g_ref, v_ref, ddn_ref, dout_ref, sums_ref):
        v = v_ref[0]
        y, vjp = jax.vjp(_final_math, dn_ref[...], g_ref[2:3], v[3:4])
        err = h1_ref[...] + y - t_ref[0]
        dout = err * (1.0 / D)
        ddn, dg4, dgate5 = vjp(dout)
        ddn_ref[...] = ddn.astype(BF16)
        dout_ref[...] = dout

        @pl.when(pl.program_id(1) == 0)
        def _():
            sums_ref[...] = jnp.zeros_like(sums_ref)

        sums_ref[0, 0:1, :] += dgate5
        sums_ref[0, 1:2, :] += dg4
        sums_ref[0, 2:3, :] += jnp.sum(err * err, axis=0, keepdims=True) * (0.5 / D)

    row = pl.BlockSpec((tm, D), lambda b, t: (b * ntl + t, 0))
    return pl.pallas_call(
        body, name="final", grid=(B, ntl),
        in_specs=[row, row, pl.BlockSpec((1, tm, D), lambda b, t: (b, t, 0)),
                  pl.BlockSpec((8, D), lambda b, t: (0, 0)), pl.BlockSpec((1, 8, D), lambda b, t: (b, 0, 0))],
        out_specs=[row, row, pl.BlockSpec((1, 8, D), lambda b, t: (b, 0, 0))],
        out_shape=[jax.ShapeDtypeStruct((B * N, D), BF16), jax.ShapeDtypeStruct((B * N, D), F32),
                   jax.ShapeDtypeStruct((B, 8, D), F32)],
        compiler_params=_params(("parallel", "arbitrary")),
    )(h1, dn, target, gains, vecs)


def _shift(x, s):
    s = s % x.shape[0]
    return x if s == 0 else pltpu.roll(x, s, 0)


def _seg_taps(T, nc, width, pad_left):
    t = lax.broadcasted_iota(jnp.int32, (T, 1), 0)
    pos = jnp.where(t < nc, t, t - nc)
    seg = jnp.where(t < nc, nc, T - nc)
    taps = []
    for k in range(width):
        src = pos + (k - pad_left)
        taps.append((pad_left - k, (src >= 0) & (src < seg)))
    return taps


def _grid_taps(N):
    t = lax.broadcasted_iota(jnp.int32, (N, 1), 0)
    wcol = t % GRID_W
    taps = []
    for dr in (-1, 0, 1):
        for dw in (-1, 0, 1):
            off = dr * GRID_W + dw
            ok = (wcol + dw >= 0) & (wcol + dw < GRID_W) & (t + dr * GRID_W >= 0) & (t + dr * GRID_W < N)
            taps.append((-off, ok))
    return taps


def _conv_fwd(x, w, taps):
    y = jnp.zeros_like(x)
    for k, (s, m) in enumerate(taps):
        y = y + w[k:k + 1] * jnp.where(m, _shift(x, s), 0.0)
    return y


def _conv_bwd(x, w, taps, dy):
    dx = jnp.zeros_like(x)
    dws = []
    for k, (s, m) in enumerate(taps):
        dym = jnp.where(m, dy, 0.0)
        dx = dx + w[k:k + 1] * _shift(dym, -s)
        dws.append(jnp.sum(dym * _shift(x, s), axis=0, keepdims=True))
    return dx, jnp.concatenate(dws, axis=0)


def _ffn_act_fwd(F, w9, bias, *, B, N, DFF, tc):
    nj = DFF // tc

    def body(fg_ref, fv_ref, w_ref, b_ref, o_ref, pre_ref):
        fg = _conv_fwd(fg_ref[...], w_ref[...], _grid_taps(N)) + b_ref[...]
        pre_ref[...] = fg
        o_ref[...] = (_gelu(fg) * fv_ref[...]).astype(BF16)

    col = pl.BlockSpec((N, tc), lambda b, j: (b, j))
    return pl.pallas_call(
        body, name="ffn_act_fwd", grid=(B, nj),
        in_specs=[col, pl.BlockSpec((N, tc), lambda b, j: (b, nj + j)),
                  pl.BlockSpec((9, tc), lambda b, j: (0, j)), pl.BlockSpec((1, tc), lambda b, j: (0, j))],
        out_specs=[col, col],
        out_shape=[jax.ShapeDtypeStruct((B * N, DFF), BF16), jax.ShapeDtypeStruct((B * N, DFF), F32)],
        compiler_params=_params(("parallel", "parallel")),
    )(F, F, w9, bias)


def _ffn_act_bwd(F, pre, w9, df, *, B, N, DFF, tc):
    nj = DFF // tc

    def body(fg_ref, fv_ref, w_ref, pre_ref, df_ref, dfg_ref, dfv_ref, dwb_ref):
        taps = _grid_taps(N)

        @pl.when(pl.program_id(1) == 0)
        def _():
            dwb_ref[...] = jnp.zeros_like(dwb_ref)

        for lo in range(0, tc, LANES):
            ln = slice(lo, lo + LANES)
            fg, vjp = jax.vjp(lambda a: _gelu(a), pre_ref[:, ln])
            dfl = df_ref[:, ln]
            dfv_ref[:, ln] = (dfl * fg).astype(BF16)
            (dpre,) = vjp(dfl * fv_ref[:, ln])
            dx, dw = _conv_bwd(fg_ref[:, ln], w_ref[:, ln], taps, dpre)
            dfg_ref[:, ln] = dx.astype(BF16)
            dwb_ref[0:9, ln] += dw
            dwb_ref[9:10, ln] += jnp.sum(dpre, axis=0, keepdims=True)

    col = pl.BlockSpec((N, tc), lambda j, b: (b, j))
    return pl.pallas_call(
        body, name="ffn_act_bwd", grid=(nj, B),
        in_specs=[col, pl.BlockSpec((N, tc), lambda j, b: (b, nj + j)), pl.BlockSpec((9, tc), lambda j, b: (0, j)), col, col],
        out_specs=[col, col, pl.BlockSpec((16, tc), lambda j, b: (0, j))],
        out_shape=[jax.ShapeDtypeStruct((B * N, DFF), BF16), jax.ShapeDtypeStruct((B * N, DFF), BF16),
                   jax.ShapeDtypeStruct((16, DFF), F32)],
        compiler_params=_params(("parallel", "arbitrary")),
    )(F, F, w9, pre, df)


def _dnprep_math(y, is_qk, scale):
    s = _silu(y)
    n = s * lax.rsqrt(jnp.sum(s * s, axis=-1, keepdims=True) + EPS) * scale
    return jnp.where(is_qk, n, s)


def _dnprep_fwd(p, cw, *, B, T, nc, H, HD):
    def body(x_ref, w_ref, o_ref):
        j = pl.program_id(1)
        y = _conv_fwd(x_ref[...], w_ref[...], _seg_taps(T, nc, 4, 2))
        o_ref[...] = _dnprep_math(y, j < 2 * H, jnp.where(j < H, HD ** -0.5, 1.0))

    return pl.pallas_call(
        body, name="dnprep_fwd", grid=(B, 3 * H),
        in_specs=[pl.BlockSpec((T, HD), lambda b, j: (b, j)), pl.BlockSpec((4, HD), lambda b, j: (0, j))],
        out_specs=pl.BlockSpec((T, HD), lambda b, j: (b, j)),
        out_shape=jax.ShapeDtypeStruct((B * T, 3 * H * HD), F32),
        compiler_params=_params(("parallel", "parallel")),
    )(p, cw)


def _dnprep_bwd(p, cw, dqkv, dp, *, B, T, nc, H, HD):
    def body(x_ref, w_ref, d_ref, dp_any, dp_ref, dcw_ref):
        j = pl.program_id(0)
        taps = _seg_taps(T, nc, 4, 2)
        x = x_ref[...]
        y = _conv_fwd(x, w_ref[...], taps)
        is_qk, scale = j < 2 * H, jnp.where(j < H, HD ** -0.5, 1.0)
        _, vjp = jax.vjp(lambda a: _dnprep_math(a, is_qk, scale), y)
        (dy,) = vjp(d_ref[0])
        dx, dw = _conv_bwd(x, w_ref[...], taps, dy)
        dp_ref[...] = dx.astype(BF16)

        @pl.when(pl.program_id(1) == 0)
        def _():
            dcw_ref[...] = jnp.zeros_like(dcw_ref)

        dcw_ref[0:4, :] += dw

    col = pl.BlockSpec((T, HD), lambda j, b: (b, j))
    return pl.pallas_call(
        body, name="dnprep_bwd", grid=(3 * H, B),
        in_specs=[col, pl.BlockSpec((4, HD), lambda j, b: (0, j)),
                  pl.BlockSpec((1, T, HD), lambda j, b: (j // H, b, j % H)), pl.BlockSpec(memory_space=pl.ANY)],
        out_specs=[col, pl.BlockSpec((8, HD), lambda j, b: (0, j))],
        out_shape=[jax.ShapeDtypeStruct(dp.shape, dp.dtype), jax.ShapeDtypeStruct((8, 3 * H * HD), F32)],
        input_output_aliases={3: 0},
        compiler_params=_params(("parallel", "arbitrary")),
    )(p, cw, dqkv, dp)


def _gb_math(ab, alog, dtb, H):
    lane = lax.broadcasted_iota(jnp.int32, ab.shape, 1)
    g = -jnp.exp(alog) * _softplus(ab + dtb)
    return jnp.where(lane < 2 * H, g, jnp.where(lane < 4 * H, _sigmoid(ab), 0.0))


def _gb_fwd(p, prm, *, rows, col0, H, tm):
    def body(x_ref, prm_ref, o_ref):
        o_ref[...] = _gb_math(x_ref[...], prm_ref[0:1], prm_ref[1:2], H)

    return pl.pallas_call(
        body, name="gb_fwd", grid=(rows // tm,),
        in_specs=[pl.BlockSpec((tm, LANES), lambda t: (t, col0 // LANES)), pl.BlockSpec((8, LANES), lambda t: (0, 0))],
        out_specs=pl.BlockSpec((tm, LANES), lambda t: (t, 0)),
        out_shape=jax.ShapeDtypeStruct((rows, LANES), F32),
        compiler_params=_params(("parallel",)),
    )(p, prm)


def _gb_bwd(p, prm, dgb, dp, *, rows, col0, H, tm):
    def body(x_ref, prm_ref, d_ref, dp_any, dp_ref, dprm_ref):
        _, vjp = jax.vjp(lambda a, b, c: _gb_math(a, b, c, H), x_ref[...], prm_ref[0:1], prm_ref[1:2])
        dab, dalog, ddtb = vjp(d_ref[...])
        dp_ref[...] = dab.astype(BF16)

        @pl.when(pl.program_id(0) == 0)
        def _():
            dprm_ref[...] = jnp.zeros_like(dprm_ref)

        dprm_ref[0:1, :] += dalog
        dprm_ref[1:2, :] += ddtb

    blk = pl.BlockSpec((tm, LANES), lambda t: (t, col0 // LANES))
    return pl.pallas_call(
        body, name="gb_bwd", grid=(rows // tm,),
        in_specs=[blk, pl.BlockSpec((8, LANES), lambda t: (0, 0)), pl.BlockSpec((tm, LANES), lambda t: (t, 0)),
                  pl.BlockSpec(memory_space=pl.ANY)],
        out_specs=[blk, pl.BlockSpec((8, LANES), lambda t: (0, 0))],
        out_shape=[jax.ShapeDtypeStruct(dp.shape, dp.dtype), jax.ShapeDtypeStruct((8, LANES), F32)],
        input_output_aliases={3: 0},
        compiler_params=_params(("arbitrary",)),
    )(p, prm, dgb, dp)


def _lru_scans(scans):
    C = scans[0][0].shape[1]
    row = lax.broadcasted_iota(jnp.int32, (SUBLANES, C), 0)
    carries = tuple(jnp.zeros((1, C), F32) for _ in scans)
    for si in range(len(scans[0][4])):
        rows = scans[0][4][si][1]
        assert all(sc[4][si][1] == rows for sc in scans)
        sub = max(s for s in (4, 2, 1) if rows % (s * SUBLANES) == 0)
        span = sub * SUBLANES
        nb = rows // span

        def blk(i, carries, si=si, nb=nb, sub=sub, span=span):
            out = []
            for (a_ref, b_ref, h_ref, hp_ref, segs), carry in zip(scans, carries):
                start, _, reverse = segs[si]
                r0 = pl.multiple_of(start + (nb - 1 - i if reverse else i) * span, span)
                local = []
                for j in range(sub):
                    A = a_ref[pl.ds(r0 + j * SUBLANES, SUBLANES), :]
                    Bv = b_ref[pl.ds(r0 + j * SUBLANES, SUBLANES), :]
                    for s in (1, 2, 4):
                        sh = SUBLANES - s if reverse else s
                        m = (row < SUBLANES - s) if reverse else (row >= s)
                        Bv = jnp.where(m, A * pltpu.roll(Bv, sh, 0) + Bv, Bv)
                        A = jnp.where(m, A * pltpu.roll(A, sh, 0), A)
                    local.append((A, Bv))
                for j in (reversed(range(sub)) if reverse else range(sub)):
                    A, Bv = local[j]
                    Hv = Bv + A * carry
                    h_ref[pl.ds(r0 + j * SUBLANES, SUBLANES), :] = Hv
                    if hp_ref is not None:
                        if reverse:
                            hp = jnp.where(row < SUBLANES - 1, pltpu.roll(Hv, SUBLANES - 1, 0), carry)
                        else:
                            hp = jnp.where(row >= 1, pltpu.roll(Hv, 1, 0), carry)
                        hp_ref[pl.ds(r0 + j * SUBLANES, SUBLANES), :] = hp
                    carry = Hv[0:1] if reverse else Hv[SUBLANES - 1:SUBLANES]
                out.append(carry)
            return tuple(out)

        carries = lax.fori_loop(0, nb, blk, carries)


def _lru_orders(T, nc, d):
    N = T - nc
    if d == 0:
        return [(0, nc, False), (nc, N, False)], [(nc, N, True), (0, nc, True)]
    return [(0, nc, True), (nc, N, True)], [(nc, N, False), (0, nc, False)]


def _bdot(a, b, dims=(((1,), (0,)), ((), ()))):
    return lax.dot_general(a.astype(BF16), b.astype(BF16), dims, preferred_element_type=F32)


_NT = (((1,), (1,)), ((), ()))
_TN = (((0,), (0,)), ((), ()))


def _blockdiag(w, C):
    nd, nb, bd, _ = w.shape
    per = C // bd
    out = jnp.einsum('dnpij,pq->dnpiqj', w.reshape(nd, nb // per, per, bd, bd), jnp.eye(per, dtype=w.dtype))
    return out.reshape(nd, nb // per, C, C)


def _blockdiag_extract(dw, bd):
    nd, nj, C, _ = dw.shape
    per = C // bd
    out = jnp.einsum('dnpiqj,pq->dnpij', dw.reshape(nd, nj, per, bd, per, bd), jnp.eye(per, dtype=dw.dtype))
    return out.reshape(nd, nj * per, bd, bd)


def _lru_fwd(p, cw, lv, wr, wi, *, B, T, nc, LW, col0, C):
    N = T - nc
    nj = LW // C

    def body(x_ref, cw_ref, lv_ref, wr_ref, wi_ref, o_ref, a_s, b_s, h_s):
        lv_ = lv_ref[...]
        xc = _conv_fwd(x_ref[:, 0:C], cw_ref[...], _seg_taps(T, nc, 4, 2)) + lv_[0:1]
        for d in (0, 1):
            r = _sigmoid(_bdot(xc, wr_ref[d, 0]) + lv_[1 + d:2 + d])
            i = _sigmoid(_bdot(xc, wi_ref[d, 0]) + lv_[3 + d:4 + d])
            la = -LRU_C * r * _softplus(-lv_[5 + d:6 + d])
            a_s[d] = jnp.exp(la)
            b_s[d] = jnp.sqrt(1.0 - jnp.exp(2.0 * la)) * i * xc
        _lru_scans([(a_s.at[d], b_s.at[d], h_s.at[d], None, _lru_orders(T, nc, d)[0]) for d in (0, 1)])
        o_ref[...] = ((h_s[0, nc:, :] + h_s[1, nc:, :]) * _gelu(x_ref[nc:, C:2 * C])).astype(BF16)

    return pl.pallas_call(
        body, name="lru_fwd", grid=(B, nj),
        in_specs=[pl.BlockSpec((T, 2 * C), lambda b, j: (b, col0 // (2 * C) + j)),
                  pl.BlockSpec((4, C), lambda b, j: (0, j)), pl.BlockSpec((8, C), lambda b, j: (0, j)),
                  pl.BlockSpec((2, 1, C, C), lambda b, j: (0, j, 0, 0)), pl.BlockSpec((2, 1, C, C), lambda b, j: (0, j, 0, 0))],
        out_specs=pl.BlockSpec((N, C), lambda b, j: (b, j)),
        out_shape=jax.ShapeDtypeStruct((B * N, LW), BF16),
        scratch_shapes=[pltpu.VMEM((2, T, C), F32)] * 3,
        compiler_params=_params(("parallel", "parallel")),
    )(p, cw, lv, wr, wi)


def _lru_bwd(p, cw, lv, wr, wi, dy, dp, *, B, T, nc, LW, col0, C):
    N = T - nc
    nj = LW // C

    def body(x_ref, cw_ref, lv_ref, wr_ref, wi_ref, dy_ref, dp_any, dp_ref, dcw_ref, dlv_ref, dwr_ref, dwi_ref,
             a_s, b_s, h_s, hp_s, mu_s, mup_s, r_s, i_s, dh_s, dxc_s):
        taps = _seg_taps(T, nc, 4, 2)
        lv_ = lv_ref[...]
        xl = x_ref[:, 0:C]
        xc = _conv_fwd(xl, cw_ref[...], taps) + lv_[0:1]
        gel, gelu_vjp = jax.vjp(_gelu, x_ref[nc:, C:2 * C])
        dh_s[0:nc, :] = jnp.zeros((nc, C), F32)
        dh_s[nc:, :] = dy_ref[...] * gel
        dxc_s[...] = jnp.zeros_like(dxc_s)

        @pl.when(pl.program_id(1) == 0)
        def _():
            dcw_ref[...] = jnp.zeros_like(dcw_ref)
            dlv_ref[...] = jnp.zeros_like(dlv_ref)
            dwr_ref[...] = jnp.zeros_like(dwr_ref)
            dwi_ref[...] = jnp.zeros_like(dwi_ref)

        for d in (0, 1):
            r = _sigmoid(_bdot(xc, wr_ref[d, 0]) + lv_[1 + d:2 + d])
            i = _sigmoid(_bdot(xc, wi_ref[d, 0]) + lv_[3 + d:4 + d])
            a = jnp.exp(-LRU_C * r * _softplus(-lv_[5 + d:6 + d]))
            r_s[d] = r
            i_s[d] = i
            a_s[d] = a
            b_s[d] = jnp.sqrt(1.0 - a * a) * i * xc
        _lru_scans([(a_s.at[d], b_s.at[d], h_s.at[d], hp_s.at[d], _lru_orders(T, nc, d)[0]) for d in (0, 1)])
        for d in (0, 1):
            b_s[d] = a_s[d] * dh_s[...]
        _lru_scans([(a_s.at[d], b_s.at[d], mu_s.at[d], mup_s.at[d], _lru_orders(T, nc, d)[1]) for d in (0, 1)])

        for d in (0, 1):
            lam = lv_[5 + d:6 + d]
            sp = _softplus(-lam)
            r, i, a = r_s[d], i_s[d], a_s[d]
            e2 = a * a
            mult = jnp.sqrt(1.0 - e2)
            dinp = dh_s[...] + mup_s[d]
            da = dinp * hp_s[d]
            dmult = dinp * i * xc
            di = dinp * mult * xc
            dla = da * a - dmult * e2 / mult
            dpre_r = (dla * (-LRU_C * sp)) * r * (1.0 - r)
            dpre_i = di * i * (1.0 - i)
            dsp = jnp.sum(dla * (-LRU_C * r), axis=0, keepdims=True)
            dxc_s[...] += dinp * mult * i + _bdot(dpre_r, wr_ref[d, 0], _NT) + _bdot(dpre_i, wi_ref[d, 0], _NT)
            dwr_ref[d, 0] += _bdot(xc, dpre_r, _TN)
            dwi_ref[d, 0] += _bdot(xc, dpre_i, _TN)
            dlv_ref[1 + d:2 + d, :] += jnp.sum(dpre_r, axis=0, keepdims=True)
            dlv_ref[3 + d:4 + d, :] += jnp.sum(dpre_i, axis=0, keepdims=True)
            dlv_ref[5 + d:6 + d, :] += -dsp * _sigmoid(-lam)

        dxc = dxc_s[...]
        dxl, dw = _conv_bwd(xl, cw_ref[...], taps, dxc)
        dcw_ref[0:4, :] += dw
        dlv_ref[0:1, :] += jnp.sum(dxc, axis=0, keepdims=True)
        dp_ref[:, 0:C] = dxl.astype(BF16)
        (dyl,) = gelu_vjp(dy_ref[...] * (h_s[0, nc:, :] + h_s[1, nc:, :]))
        dp_ref[0:nc, C:2 * C] = jnp.zeros((nc, C), BF16)
        dp_ref[nc:, C:2 * C] = dyl.astype(BF16)

    xblk = pl.BlockSpec((T, 2 * C), lambda j, b: (b, col0 // (2 * C) + j))
    wblk = pl.BlockSpec((2, 1, C, C), lambda j, b: (0, j, 0, 0))
    vblk = pl.BlockSpec((8, C), lambda j, b: (0, j))
    return pl.pallas_call(
        body, name="lru_bwd", grid=(nj, B),
        in_specs=[xblk, pl.BlockSpec((4, C), lambda j, b: (0, j)), vblk, wblk, wblk,
                  pl.BlockSpec((N, C), lambda j, b: (b, j)), pl.BlockSpec(memory_space=pl.ANY)],
        out_specs=[xblk, vblk, vblk, wblk, wblk],
        out_shape=[jax.ShapeDtypeStruct(dp.shape, dp.dtype), jax.ShapeDtypeStruct((8, LW), F32),
                   jax.ShapeDtypeStruct((8, LW), F32), jax.ShapeDtypeStruct((2, nj, C, C), F32),
                   jax.ShapeDtypeStruct((2, nj, C, C), F32)],
        scratch_shapes=[pltpu.VMEM((2, T, C), F32)] * 8 + [pltpu.VMEM((T, C), F32)] * 2,
        input_output_aliases={6: 0},
        compiler_params=_params(("parallel", "arbitrary")),
    )(p, cw, lv, wr, wi, dy, dp)


def _chunk_masks(upper):
    i = lax.broadcasted_iota(jnp.int32, (CHUNK, CHUNK), 0)
    j = lax.broadcasted_iota(jnp.int32, (CHUNK, CHUNK), 1)
    ahead = jnp.where(upper, j - i, i - j)
    return i == j, ahead >= 0, ahead > 0


def _col2row(c, eye):
    return jnp.sum(jnp.where(eye, c, 0.0), axis=0, keepdims=True)


def _row2col(r, eye):
    return jnp.sum(jnp.where(eye, r, 0.0), axis=1, keepdims=True)


def _rowsum(x):
    return jnp.sum(x, axis=1, keepdims=True)


_INV_BASE = 8


def _unit_tri_inverses(Ls):
    G = len(Ls)
    W = G * CHUNK
    blk = (lax.broadcasted_iota(jnp.int32, (W, W), 0) // CHUNK) == (lax.broadcasted_iota(jnp.int32, (W, W), 1) // CHUNK)
    ri = lax.broadcasted_iota(jnp.int32, (CHUNK, W), 0)
    ci = lax.broadcasted_iota(jnp.int32, (CHUNK, W), 1) % CHUNK

    def bd(b):
        return jnp.where(blk, jnp.tile(b, (G, 1)), jnp.zeros((), BF16))

    def pdot(a, b):
        return jnp.dot(a.astype(BF16), bd(b.astype(BF16)), preferred_element_type=F32)

    Lc = Ls[0] if G == 1 else jnp.concatenate(Ls, axis=1)
    s = _INV_BASE
    Xp = -jnp.where(ri // s == ci // s, Lc, 0.0)
    Rm = Xp
    for _ in range(int(math.log2(s)) - 1):
        Xp = pdot(Xp, Xp)
        Rm = Rm + Xp + pdot(Rm, Xp)
    while s < CHUNK:
        E = jnp.where((ri // (2 * s) == ci // (2 * s)) & (ri // s != ci // s), Lc, 0.0)
        DE = E + pdot(Rm, E)
        Rm = Rm - (DE + pdot(DE, Rm))
        s *= 2
    eye = _chunk_masks(False)[0]
    return [jnp.where(eye, 1.0, 0.0) + Rm[:, g * CHUNK:(g + 1) * CHUNK] for g in range(G)]


def _delta_chunk_common(q, k, v, gcol, bcol, upper):
    eye, incl, strict = _chunk_masks(upper)
    gc = _rowsum(jnp.where(incl, _col2row(gcol, eye), 0.0))
    D = jnp.where(incl, jnp.exp(jnp.minimum(gc - _col2row(gc, eye), 0.0)), 0.0)
    kb = k * bcol
    AP = _bdot(jnp.concatenate([kb, q], axis=0), k, _NT)
    A = AP[:CHUNK]
    L = jnp.where(strict, A * D, 0.0)
    eg = jnp.exp(gc)
    gl = jnp.sum(gcol, axis=0, keepdims=True)
    attn = jnp.where(incl, AP[CHUNK:] * D, 0.0)
    return dict(eye=eye, incl=incl, strict=strict, gc=gc, D=D, kb=kb, A=A, L=L, eg=eg, gl=gl, egl=jnp.exp(gl),
                attn=attn, kbe=kb * eg, vb=v * bcol, qe=q * eg, kd=k * jnp.exp(gl - gc))


def _delta_group_pre(chunks, upper):
    cs = [_delta_chunk_common(*ch, upper) for ch in chunks]
    out = []
    for c, Tm in zip(cs, _unit_tri_inverses([c["L"] for c in cs])):
        dk = c["kbe"].shape[1]
        wu = _bdot(Tm, jnp.concatenate([c["kbe"], c["vb"]], axis=1))
        KN = _bdot(c["kd"], wu, _TN)
        QO = _bdot(c["attn"], wu)
        out.append((Tm, KN[:, :dk], KN[:, dk:], c["qe"] - QO[:, :dk], QO[:, dk:], c["egl"]))
    return out


def _delta_chunk_bwd(q, k, v, gcol, bcol, S, Tm, do, dS2, upper):
    c = _delta_chunk_common(q, k, v, gcol, bcol, upper)
    eye, incl, strict, D, eg, egl = c["eye"], c["incl"], c["strict"], c["D"], c["eg"], c["egl"]
    kb, kbe, vb, qe, kd, attn = c["kb"], c["kbe"], c["vb"], c["qe"], c["kd"], c["attn"]
    dkk = kbe.shape[1]
    wu = _bdot(Tm, jnp.concatenate([kbe, vb], axis=1))
    w = wu[:, :dkk]
    vn = wu[:, dkk:] - _bdot(w, S)
    dvn = _bdot(kd, dS2) + _bdot(attn, do, _TN)
    dkd = _bdot(vn, dS2, _NT)
    dgl = jnp.sum(_rowsum(dS2 * S), axis=0, keepdims=True) * egl
    dqa = _bdot(do, jnp.concatenate([S, vn], axis=0), _NT)
    dqe = dqa[:, :dkk]
    dattn = jnp.where(incl, dqa[:, dkk:], 0.0)
    dw = -_bdot(dvn, S, _NT)
    r = _rowsum(dkd * kd)
    dk = dkd * jnp.exp(c["gl"] - c["gc"])
    dgl = dgl + jnp.sum(r, axis=0, keepdims=True)
    dgc = _rowsum(dqe * qe) - r
    E = dattn * attn
    dvw = jnp.concatenate([dvn, dw], axis=1)
    dTm = _bdot(dvw, jnp.concatenate([vb, kbe], axis=1), _NT)
    dvk = _bdot(Tm, dvw, _TN)
    dvb = dvk[:, :dvn.shape[1]]
    dv = dvb * bcol
    dbeta = _rowsum(dvb * v)
    dkbe = dvk[:, dvn.shape[1]:]
    dkb = dkbe * eg
    dgc = dgc + _rowsum(dkbe * kbe)
    dL = jnp.where(strict, -_bdot(Tm, _bdot(dTm, Tm, _NT), _TN), 0.0)
    dA = dL * D
    E = E + dL * c["L"]
    PA = jnp.concatenate([dattn * D, dA], axis=0)
    PAk = _bdot(PA, k)
    dq = dqe * eg + PAk[:CHUNK]
    dkb = dkb + PAk[CHUNK:]
    dk = dk + _bdot(PA, jnp.concatenate([q, kb], axis=0), _TN) + dkb * bcol
    dbeta = dbeta + _rowsum(dkb * k)
    dgc = dgc + _rowsum(E) - _row2col(jnp.sum(E, axis=0, keepdims=True), eye)
    dg = _row2col(jnp.sum(jnp.where(incl, dgc, 0.0), axis=0, keepdims=True), eye) + dgl
    return dq, dk, dv, dg, dbeta


def _delta_unroll(trips):
    return max(u for u in (3, 2, 1) if trips % u == 0)


def _delta_group(n):
    return max(g for g in range(1, 2 * LANES // CHUNK + 1) if n % g == 0)


def _delta_chunk_at(T, nc, d, i):
    n, ncc = T // CHUNK, nc // CHUNK
    desc = jnp.where(i < ncc, ncc - 1 - i, n - 1 - (i - ncc))
    if isinstance(d, int):
        return i if d == 0 else desc
    return jnp.where(d == 0, i, desc)


def _dn_out_math(o, onorm, z):
    return _rmsn(o, onorm) * _silu(z)


def _delta_fwd(qkv, gb, p, onorm, *, B, T, nc, H, HD):
    N = T - nc
    n = T // CHUNK
    G = _delta_group(n)

    def body(q_ref, k_ref, v_ref, gb_ref, z_ref, on_ref, y_ref, o_ref, Tm_ref, K_ref, S_ref, Qp_ref, eg_ref,
             N_s, O0_s, o_s):
        h = pl.program_id(1)
        lane = lax.broadcasted_iota(jnp.int32, (CHUNK, LANES), 1)

        def pre(g, carry):
            cs = [g * G + i for i in range(G)]
            rows = [pl.ds(pl.multiple_of(c * CHUNK, CHUNK), CHUNK) for c in cs]
            for d in (0, 1):
                chunks = []
                for r in rows:
                    gbb = gb_ref[r, :]
                    chunks.append((q_ref[r, :], k_ref[r, :], v_ref[r, :],
                                   _rowsum(jnp.where(lane == d * H + h, gbb, 0.0)),
                                   _rowsum(jnp.where(lane == 2 * H + d * H + h, gbb, 0.0))))
                for c, r, (Tm, K, Nn, Qp, O0, egl) in zip(cs, rows, _delta_group_pre(chunks, d == 1)):
                    Tm_ref[0, d * n + c] = Tm
                    K_ref[0, d * n + c] = K.astype(BF16)
                    N_s[d * n + c] = Nn
                    Qp_ref[0, d, r, :] = Qp.astype(BF16)
                    O0_s[d, r, :] = O0
                    eg_ref[0, d * n + c] = jnp.broadcast_to(egl, (SUBLANES, HD))
            return carry

        lax.fori_loop(0, n // G, pre, 0)

        def step(i, Ss):
            out = []
            for d in (0, 1):
                c = _delta_chunk_at(T, nc, d, i)
                rows = pl.ds(pl.multiple_of(c * CHUNK, CHUNK), CHUNK)
                S_ref[0, d * n + c] = Ss[d]
                Sb = Ss[d].astype(BF16)
                o_s[d, rows, :] = jnp.dot(Qp_ref[0, d, rows, :], Sb, preferred_element_type=F32) + O0_s[d, rows, :]
                out.append(eg_ref[0, d * n + c][0:1] * Ss[d] + N_s[d * n + c]
                           - jnp.dot(K_ref[0, d * n + c], Sb, preferred_element_type=F32))
            return tuple(out)

        lax.fori_loop(0, n, step, (jnp.zeros((HD, HD), F32), jnp.zeros((HD, HD), F32)))
        o = o_s[0, nc:, :] + o_s[1, nc:, :]
        o_ref[...] = o
        y_ref[...] = _dn_out_math(o, on_ref[...], z_ref[nc:, :]).astype(BF16)

    col = lambda off: pl.BlockSpec((T, HD), lambda b, h: (b, off + h))
    lat = pl.BlockSpec((N, HD), lambda b, h: (b, h))
    per = lambda *blk: pl.BlockSpec((1, *blk), lambda b, h: (b * H + h, 0, 0, 0))
    return pl.pallas_call(
        body, name="delta_fwd", grid=(B, H),
        in_specs=[col(0), col(H), col(2 * H), pl.BlockSpec((T, LANES), lambda b, h: (b, 0)), col(3 * H),
                  pl.BlockSpec((1, HD), lambda b, h: (0, 0))],
        out_specs=[lat, lat, per(2 * n, CHUNK, CHUNK), per(2 * n, HD, HD), per(2 * n, HD, HD), per(2, T, HD),
                   per(2 * n, SUBLANES, HD)],
        out_shape=[jax.ShapeDtypeStruct((B * N, H * HD), BF16), jax.ShapeDtypeStruct((B * N, H * HD), F32),
                   jax.ShapeDtypeStruct((B * H, 2 * n, CHUNK, CHUNK), F32),
                   jax.ShapeDtypeStruct((B * H, 2 * n, HD, HD), BF16), jax.ShapeDtypeStruct((B * H, 2 * n, HD, HD), F32),
                   jax.ShapeDtypeStruct((B * H, 2, T, HD), BF16), jax.ShapeDtypeStruct((B * H, 2 * n, SUBLANES, HD), F32)],
        scratch_shapes=[pltpu.VMEM((2 * n, HD, HD), F32), pltpu.VMEM((2, T, HD), F32), pltpu.VMEM((2, T, HD), F32)],
        compiler_params=_params(("parallel", "parallel")),
    )(qkv, qkv, qkv, gb, p, onorm)


def _delta_bwd(qkv, gb, p, onorm, o, res, dy, dp, *, B, T, nc, H, HD):
    N = T - nc
    n = T // CHUNK

    def body(q_ref, k_ref, v_ref, gb_ref, z_ref, on_ref, o_ref, dy_ref, Tm_ref, K_ref, S_ref, Qp_ref, eg_ref, dp_any,
             dqkv_ref, dgb_ref, dp_ref, don_ref, do_s, R_s, dS_s):
        h, d = pl.program_id(1), pl.program_id(2)
        lane = lax.broadcasted_iota(jnp.int32, (CHUNK, LANES), 1)

        @pl.when(d == 0)
        def _():
            _, vjp = jax.vjp(_dn_out_math, o_ref[...], on_ref[...], z_ref[nc:, :])
            do, don, dz = vjp(dy_ref[...])
            do_s[0:nc, :] = jnp.zeros((nc, HD), F32)
            do_s[nc:, :] = do
            dp_ref[0:nc, :] = jnp.zeros((nc, HD), BF16)
            dp_ref[nc:, :] = dz.astype(BF16)
            dqkv_ref[...] = jnp.zeros_like(dqkv_ref)

            @pl.when(h == 0)
            def _():
                don_ref[...] = jnp.zeros_like(don_ref)
                dgb_ref[...] = jnp.zeros_like(dgb_ref)

            don_ref[0, 0:1, :] += don

        def r_of(c, carry):
            rows = pl.ds(pl.multiple_of(c * CHUNK, CHUNK), CHUNK)
            R_s[c] = lax.dot_general(Qp_ref[0, 0, rows, :], do_s[rows, :].astype(BF16), _TN, preferred_element_type=F32)
            return carry

        lax.fori_loop(0, n, r_of, 0)

        def bwd_step(i, dS):
            c = _delta_chunk_at(T, nc, d, n - 1 - i)
            dS_s[c] = dS
            return (eg_ref[0, c][0:1] * dS + R_s[c]
                    - lax.dot_general(K_ref[0, c], dS.astype(BF16), _TN, preferred_element_type=F32))

        lax.fori_loop(0, n, bwd_step, jnp.zeros((HD, HD), F32))

        def grads(c, carry):
            rows = pl.ds(pl.multiple_of(c * CHUNK, CHUNK), CHUNK)
            gbb = gb_ref[rows, :]
            gcol = _rowsum(jnp.where(lane == d * H + h, gbb, 0.0))
            bcol = _rowsum(jnp.where(lane == 2 * H + d * H + h, gbb, 0.0))
            dq, dk, dv, dg, dbeta = _delta_chunk_bwd(q_ref[rows, :], k_ref[rows, :], v_ref[rows, :], gcol, bcol,
                                                     S_ref[0, c], Tm_ref[0, c], do_s[rows, :], dS_s[c], d == 1)
            dqkv_ref[0, rows, :] += dq
            dqkv_ref[1, rows, :] += dk
            dqkv_ref[2, rows, :] += dv
            dgb_ref[rows, :] += (jnp.where(lane == d * H + h, dg, 0.0)
                                 + jnp.where(lane == 2 * H + d * H + h, dbeta, 0.0))
            return carry

        lax.fori_loop(0, n, grads, 0, unroll=_delta_unroll(n))

    col = lambda off: pl.BlockSpec((T, HD), lambda b, h, d: (b, off + h))
    lat = pl.BlockSpec((N, HD), lambda b, h, d: (b, h))
    per = lambda *blk: pl.BlockSpec((1, *blk), lambda b, h, d: (b * H + h, d, 0, 0))
    return pl.pallas_call(
        body, name="delta_bwd", grid=(B, H, 2),
        in_specs=[col(0), col(H), col(2 * H), pl.BlockSpec((T, LANES), lambda b, h, d: (b, 0)), col(3 * H),
                  pl.BlockSpec((1, HD), lambda b, h, d: (0, 0)), lat, lat,
                  per(n, CHUNK, CHUNK), per(n, HD, HD), per(n, HD, HD), per(1, T, HD), per(n, SUBLANES, HD),
                  pl.BlockSpec(memory_space=pl.ANY)],
        out_specs=[pl.BlockSpec((3, T, HD), lambda b, h, d: (0, b, h)), pl.BlockSpec((T, LANES), lambda b, h, d: (b, 0)),
                   col(3 * H), pl.BlockSpec((1, 8, HD), lambda b, h, d: (b, 0, 0))],
        out_shape=[jax.ShapeDtypeStruct((3, B * T, H * HD), F32), jax.ShapeDtypeStruct((B * T, LANES), F32),
                   jax.ShapeDtypeStruct(dp.shape, dp.dtype), jax.ShapeDtypeStruct((B, 8, HD), F32)],
        scratch_shapes=[pltpu.VMEM((T, HD), F32), pltpu.VMEM((n, HD, HD), F32), pltpu.VMEM((n, HD, HD), F32)],
        input_output_aliases={13: 2},
        compiler_params=_params(("parallel", "arbitrary", "arbitrary")),
    )(qkv, qkv, qkv, gb, p, onorm, o, dy, *res, dp)


def _rowwise(fn, ins, out_dtypes, *, name, tm=256, mult=16):
    R, W = ins[0].shape
    tm = _tile(R, tm, mult)

    def body(*refs):
        outs = fn(*[r[...] for r in refs[:len(ins)]])
        for o_ref, o in zip(refs[len(ins):], outs):
            o_ref[...] = o.astype(o_ref.dtype)

    spec = pl.BlockSpec((tm, W), lambda i: (i, 0))
    return pl.pallas_call(
        body, name=name, grid=(R // tm,), in_specs=[spec] * len(ins), out_specs=[spec] * len(out_dtypes),
        out_shape=[jax.ShapeDtypeStruct((R, W), dt) for dt in out_dtypes],
        compiler_params=_params(("parallel",)),
    )(*ins)


def _sum_lead(x, *, name, tm=256, mult=16):
    S, R, W = x.shape
    tm = _tile(R, tm, mult)

    def body(*refs):
        acc = refs[0][0].astype(F32)
        for r in refs[1:S]:
            acc = acc + r[0].astype(F32)
        refs[S][...] = acc

    return pl.pallas_call(
        body, name=name, grid=(R // tm,),
        in_specs=[pl.BlockSpec((1, tm, W), functools.partial(lambda s, i: (s, i, 0), s)) for s in range(S)],
        out_specs=pl.BlockSpec((tm, W), lambda i: (i, 0)),
        out_shape=jax.ShapeDtypeStruct((R, W), F32),
        compiler_params=_params(("parallel",)),
    )(*([x] * S))


def _adamw_math(w, g, m, v):
    m = ADAM_B1 * m + (1.0 - ADAM_B1) * g
    v = ADAM_B2 * v + (1.0 - ADAM_B2) * (g * g)
    m_hat = m / (1.0 - ADAM_B1 ** ADAM_STEP)
    v_hat = v / (1.0 - ADAM_B2 ** ADAM_STEP)
    return -ADAM_LR * (m_hat / (jnp.sqrt(v_hat) + ADAM_EPS) + ADAM_WD * w), m, v


def _adamw(w, g, m, v, *, name):
    tm = max(SUBLANES, (256 * 1024) // w.shape[1] // SUBLANES * SUBLANES)
    return _rowwise(_adamw_math, [w, g, m, v], [F32, F32, F32], name=name, tm=tm, mult=SUBLANES)


def _me():
    return lax.axis_index("x"), lax.axis_index("y"), lax.axis_index("c")


def _allgather_small(v):
    R, W = v.shape

    def body(x_ref, out_ref, send_sems, recv_sems, local_sem):
        x, y, c = _me()
        me, sibling = (x, y, c), (x, y, 1 - c)
        chips = [(1 - x, y), (x, 1 - y), (1 - x, 1 - y)]

        def slot(px, py, pc):
            return out_ref.at[4 * px + 2 * py + pc]

        def copy(k, block, to, src=None):
            return pltpu.make_async_remote_copy(
                src_ref=slot(*block) if src is None else src, dst_ref=slot(*block),
                send_sem=send_sems.at[k], recv_sem=recv_sems.at[k], device_id=to, device_id_type=MESH)

        mine = pltpu.make_async_copy(x_ref, slot(*me), local_sem)
        mine.start()
        first = [copy(0, me, sibling, src=x_ref)]
        first += [copy(1 + j, me, (*chip, c), src=x_ref) for j, chip in enumerate(chips)]
        for cp in first:
            cp.start()
        passed = [copy(4 + j, (*chip, c), sibling) for j, chip in enumerate(chips)]
        for j, chip in enumerate(chips):
            copy(1 + j, (*chip, c), me).wait_recv()
            passed[j].start()
        copy(0, sibling, me).wait_recv()
        for j, chip in enumerate(chips):
            copy(4 + j, (*chip, 1 - c), me).wait_recv()
        for cp in first + passed:
            cp.wait_send()
        mine.wait()

    return pl.pallas_call(
        body, name="allgather_small", out_shape=jax.ShapeDtypeStruct((8, R, W), v.dtype),
        in_specs=[pl.BlockSpec(memory_space=pltpu.VMEM)], out_specs=pl.BlockSpec(memory_space=pltpu.VMEM),
        scratch_shapes=[pltpu.SemaphoreType.DMA((7,)), pltpu.SemaphoreType.DMA((7,)), pltpu.SemaphoreType.DMA],
        compiler_params=_params(),
    )(v)


_ANY = pl.BlockSpec(memory_space=pl.ANY)


def _allgather_halves(shards, *, name):
    nw = len(shards)

    def body(*refs):
        x_refs, out_refs = refs[:nw], refs[nw:2 * nw]
        send_sems, recv_sems, local_sems = refs[2 * nw:]
        x, y, c = _me()
        me, sibling = (x, y, c), (x, y, 1 - c)
        chips = [(1 - x, y), (x, 1 - y), (1 - x, 1 - y)]

        def slot(w, px, py, pc):
            return out_refs[w].at[4 * px + 2 * py + pc]

        def copy(w, k, block, to, src=None):
            return pltpu.make_async_remote_copy(
                src_ref=slot(w, *block) if src is None else src, dst_ref=slot(w, *block),
                send_sem=send_sems.at[w, k], recv_sem=recv_sems.at[w, k], device_id=to, device_id_type=MESH)

        started, local = [], []
        for w in range(nw):
            half = shards[w].shape[0] // 2
            own = x_refs[w].at[pl.ds(c * half, half), :]
            mine = pltpu.make_async_copy(own, slot(w, *me), local_sems.at[w])
            mine.start()
            first = [copy(w, 0, me, sibling, src=own)]
            first += [copy(w, 1 + j, me, (*chip, c), src=own) for j, chip in enumerate(chips)]
            for cp in first:
                cp.start()
            started += first
            local.append(mine)
        for w in range(nw):
            for j, chip in enumerate(chips):
                copy(w, 1 + j, (*chip, c), me).wait_recv()
                fwd = copy(w, 4 + j, (*chip, c), sibling)
                fwd.start()
                started.append(fwd)
        for w in range(nw):
            copy(w, 0, sibling, me).wait_recv()
            for j, chip in enumerate(chips):
                copy(w, 4 + j, (*chip, 1 - c), me).wait_recv()
        for cp in started:
            cp.wait_send()
        for cp in local:
            cp.wait()

    return pl.pallas_call(
        body, name=name,
        out_shape=[jax.ShapeDtypeStruct((8, s.shape[0] // 2, s.shape[1]), s.dtype) for s in shards],
        in_specs=[_ANY] * nw, out_specs=[_ANY] * nw,
        scratch_shapes=[pltpu.SemaphoreType.DMA((nw, 7)), pltpu.SemaphoreType.DMA((nw, 7)), pltpu.SemaphoreType.DMA((nw,))],
        compiler_params=_params(),
    )(*shards)


def _sibling_send_halves(arrs, *, name):
    nw = len(arrs)

    def body(*refs):
        x_refs, out_refs, send_sems, recv_sems = refs[:nw], refs[nw:2 * nw], refs[2 * nw], refs[2 * nw + 1]
        x, y, c = _me()
        cps = []
        for w in range(nw):
            half = arrs[w].shape[1] // 2
            cp = pltpu.make_async_remote_copy(
                src_ref=x_refs[w].at[:, pl.ds((1 - c) * half, half), :], dst_ref=out_refs[w],
                send_sem=send_sems.at[w], recv_sem=recv_sems.at[w], device_id=(x, y, 1 - c), device_id_type=MESH)
            cp.start()
            cps.append(cp)
        for cp in cps:
            cp.wait()

    return pl.pallas_call(
        body, name=name,
        out_shape=[jax.ShapeDtypeStruct((a.shape[0], a.shape[1] // 2, a.shape[2]), a.dtype) for a in arrs],
        in_specs=[_ANY] * nw, out_specs=[_ANY] * nw,
        scratch_shapes=[pltpu.SemaphoreType.DMA((nw,)), pltpu.SemaphoreType.DMA((nw,))],
        compiler_params=_params(),
    )(*arrs)


def _sibling_swap(arrs, *, name):
    nw = len(arrs)

    def body(*refs):
        x_refs, out_refs, send_sems, recv_sems = refs[:nw], refs[nw:2 * nw], refs[2 * nw], refs[2 * nw + 1]
        x, y, c = _me()
        cps = []
        for w in range(nw):
            cp = pltpu.make_async_remote_copy(
                src_ref=x_refs[w], dst_ref=out_refs[w], send_sem=send_sems.at[w], recv_sem=recv_sems.at[w],
                device_id=(x, y, 1 - c), device_id_type=MESH)
            cp.start()
            cps.append(cp)
        for cp in cps:
            cp.wait()

    return pl.pallas_call(
        body, name=name, out_shape=[jax.ShapeDtypeStruct(a.shape, a.dtype) for a in arrs],
        in_specs=[_ANY] * nw, out_specs=[_ANY] * nw,
        scratch_shapes=[pltpu.SemaphoreType.DMA((nw,)), pltpu.SemaphoreType.DMA((nw,))],
        compiler_params=_params(),
    )(*arrs)


def _adamw_halves(w, own, sib, m, v, c_arr, *, name):
    r, cols = w.shape
    h = r // 2
    tm = _tile(h, max(SUBLANES, (192 * 1024) // cols // SUBLANES * SUBLANES), SUBLANES)
    nb = h // tm

    def body(c_ref, w_ref, own_ref, sib_ref, m_ref, v_ref, g_out, d_out, m_out, v_out):
        g = jnp.where(pl.program_id(0) == c_ref[0], own_ref[...], sib_ref[...])
        g_out[...] = g
        d_out[...], m_out[...], v_out[...] = _adamw_math(w_ref[...], g, m_ref[...], v_ref[...])

    full = pl.BlockSpec((tm, cols), lambda hh, i, c_ref: (hh * nb + i, 0))
    half = pl.BlockSpec((tm, cols), lambda hh, i, c_ref: (i, 0))
    return pl.pallas_call(
        body, name=name,
        grid_spec=pltpu.PrefetchScalarGridSpec(num_scalar_prefetch=1, grid=(2, nb),
                                               in_specs=[full, half, half, full, full], out_specs=[full] * 4),
        out_shape=[jax.ShapeDtypeStruct((r, cols), F32)] * 4,
        compiler_params=_params(("parallel", "parallel")),
    )(c_arr, w, own, sib, m, v)


_HBM = pl.BlockSpec(memory_space=pltpu.HBM)
_SEM = pl.BlockSpec(memory_space=pltpu.SEMAPHORE)
_DATAFLOW = pltpu.SideEffectType.DATAFLOW_SIDE_EFFECTING


def _chip_exchange_start(arrs, *, name):
    nw = len(arrs)

    def body(*refs):
        x_refs, land_refs, send_sems, recv_sems = refs[:nw], refs[nw:2 * nw], refs[2 * nw], refs[2 * nw + 1]
        token = refs[-1]
        x, y, c = _me()
        s_me = 2 * x + y
        for w in range(nw):
            for k, (px, py) in enumerate([(1 - x, y), (x, 1 - y), (1 - x, 1 - y)]):
                pltpu.make_async_remote_copy(
                    src_ref=x_refs[w].at[2 * px + py], dst_ref=land_refs[w].at[s_me], send_sem=send_sems.at[3 * w + k],
                    recv_sem=recv_sems.at[3 * w + k], device_id=(px, py, c), device_id_type=MESH).start()
        token[...] = jnp.zeros_like(token)

    hbm = [pltpu.HBM(a.shape, a.dtype) for a in arrs]
    outs = pl.pallas_call(
        body, name=name,
        out_shape=(pltpu.SemaphoreType.DMA((3 * nw,)), pltpu.SemaphoreType.DMA((3 * nw,)), *hbm, *hbm,
                   jax.ShapeDtypeStruct((SUBLANES, LANES), F32)),
        in_specs=[_HBM] * (2 * nw), out_specs=(_SEM, _SEM, *([_HBM] * (2 * nw)), pl.BlockSpec(memory_space=pltpu.VMEM)),
        input_output_aliases={i: 2 + i for i in range(2 * nw)},
        compiler_params=pltpu.CompilerParams(has_side_effects=_DATAFLOW),
    )(*[pltpu.with_memory_space_constraint(a, pltpu.HBM) for a in arrs],
      *[pltpu.with_memory_space_constraint(lax.empty(a.shape, a.dtype), pltpu.HBM) for a in arrs])
    return outs[0], outs[1], list(outs[2:2 + nw]), list(outs[2 + nw:2 + 2 * nw]), outs[-1]


def _allgather_start(shards, *, name):
    nw = len(shards)

    def body(*refs):
        x_refs, land_refs, send_sems, recv_sems = refs[:nw], refs[nw:2 * nw], refs[2 * nw], refs[2 * nw + 1]
        token = refs[-1]
        x, y, c = _me()
        me = 4 * x + 2 * y + c
        for w in range(nw):
            half = shards[w].shape[0] // 2
            own = x_refs[w].at[pl.ds(c * half, half), :]
            for k, to in enumerate([(x, y, 1 - c), (1 - x, y, c), (x, 1 - y, c), (1 - x, 1 - y, c)]):
                pltpu.make_async_remote_copy(
                    src_ref=own, dst_ref=land_refs[w].at[me], send_sem=send_sems.at[4 * w + k],
                    recv_sem=recv_sems.at[4 * w + k], device_id=to, device_id_type=MESH).start()
        token[...] = jnp.zeros_like(token)

    lands = [pltpu.HBM((8, s.shape[0] // 2, s.shape[1]), s.dtype) for s in shards]
    outs = pl.pallas_call(
        body, name=name,
        out_shape=(pltpu.SemaphoreType.DMA((4 * nw,)), pltpu.SemaphoreType.DMA((4 * nw,)),
                   *[pltpu.HBM(s.shape, s.dtype) for s in shards], *lands, jax.ShapeDtypeStruct((SUBLANES, LANES), F32)),
        in_specs=[_HBM] * (2 * nw), out_specs=(_SEM, _SEM, *([_HBM] * (2 * nw)), pl.BlockSpec(memory_space=pltpu.VMEM)),
        input_output_aliases={i: 2 + i for i in range(2 * nw)},
        compiler_params=pltpu.CompilerParams(has_side_effects=_DATAFLOW),
    )(*[pltpu.with_memory_space_constraint(s, pltpu.HBM) for s in shards],
      *[pltpu.with_memory_space_constraint(lax.empty(l.shape, l.dtype), pltpu.HBM) for l in lands])
    return outs[0], outs[1], list(outs[2:2 + nw]), list(outs[2 + nw:2 + 2 * nw]), outs[-1]


def _allgather_wait(send_sems, recv_sems, srcs, lands, after, *, name):
    nw = len(srcs)

    def body(*refs):
        x_refs, land_refs, send_sems, recv_sems = refs[:nw], refs[nw:2 * nw], refs[2 * nw], refs[2 * nw + 1]
        x, y, c = _me()
        for w in range(nw):
            half = srcs[w].shape[0] // 2
            own = x_refs[w].at[pl.ds(c * half, half), :]
            for k, (px, py, pc) in enumerate([(x, y, 1 - c), (1 - x, y, c), (x, 1 - y, c), (1 - x, 1 - y, c)]):
                cp = pltpu.make_async_remote_copy(
                    src_ref=own, dst_ref=land_refs[w].at[4 * px + 2 * py + pc], send_sem=send_sems.at[4 * w + k],
                    recv_sem=recv_sems.at[4 * w + k], device_id=(px, py, pc), device_id_type=MESH)
                cp.wait_send()
                cp.wait_recv()

    outs = pl.pallas_call(
        body, name=name,
        out_shape=(*[pltpu.HBM(a.shape, a.dtype) for a in srcs], *[pltpu.HBM(a.shape, a.dtype) for a in lands]),
        in_specs=[_HBM] * (2 * nw) + [_SEM, _SEM, _ANY], out_specs=tuple([_HBM] * (2 * nw)),
        input_output_aliases={i: i for i in range(2 * nw)},
        compiler_params=pltpu.CompilerParams(has_side_effects=_DATAFLOW),
    )(*srcs, *lands, send_sems, recv_sems, after)
    return list(outs[:nw]), list(outs[nw:])


def _pass_to_sibling(lands, *, name):
    nw = len(lands)

    def body(*refs):
        x_refs, out_refs, send_sems, recv_sems = refs[:nw], refs[nw:2 * nw], refs[2 * nw], refs[2 * nw + 1]
        x, y, c = _me()
        chips = [(1 - x, y), (x, 1 - y), (1 - x, 1 - y)]
        cps = []
        for w in range(nw):
            for k, (px, py) in enumerate(chips):
                cp = pltpu.make_async_remote_copy(
                    src_ref=x_refs[w].at[4 * px + 2 * py + c], dst_ref=out_refs[w].at[4 * px + 2 * py + c],
                    send_sem=send_sems.at[3 * w + k], recv_sem=recv_sems.at[3 * w + k], device_id=(x, y, 1 - c),
                    device_id_type=MESH)
                cp.start()
                cps.append(cp)
        for w in range(nw):
            for k, (px, py) in enumerate(chips):
                pltpu.make_async_remote_copy(
                    src_ref=x_refs[w].at[4 * px + 2 * py + c], dst_ref=out_refs[w].at[4 * px + 2 * py + 1 - c],
                    send_sem=send_sems.at[3 * w + k], recv_sem=recv_sems.at[3 * w + k], device_id=(x, y, 1 - c),
                    device_id_type=MESH).wait_recv()
        for cp in cps:
            cp.wait_send()

    return pl.pallas_call(
        body, name=name, out_shape=[jax.ShapeDtypeStruct(a.shape, a.dtype) for a in lands],
        in_specs=[_ANY] * nw, out_specs=[_ANY] * nw, input_output_aliases={i: i for i in range(nw)},
        scratch_shapes=[pltpu.SemaphoreType.DMA((3 * nw,)), pltpu.SemaphoreType.DMA((3 * nw,))],
        compiler_params=_params(),
    )(*lands)


def _chip_exchange_wait(send_sems, recv_sems, srcs, lands, after, *, name):
    nw = len(srcs)

    def body(*refs):
        x_refs, land_refs, send_sems, recv_sems = refs[:nw], refs[nw:2 * nw], refs[2 * nw], refs[2 * nw + 1]
        x, y, c = _me()
        for w in range(nw):
            for k, (px, py) in enumerate([(1 - x, y), (x, 1 - y), (1 - x, 1 - y)]):
                cp = pltpu.make_async_remote_copy(
                    src_ref=x_refs[w].at[2 * px + py], dst_ref=land_refs[w].at[2 * px + py], send_sem=send_sems.at[3 * w + k],
                    recv_sem=recv_sems.at[3 * w + k], device_id=(px, py, c), device_id_type=MESH)
                cp.wait_send()
                cp.wait_recv()

    hbm = [pltpu.HBM(a.shape, a.dtype) for a in srcs]
    outs = pl.pallas_call(
        body, name=name, out_shape=(*hbm, *hbm),
        in_specs=[_HBM] * (2 * nw) + [_SEM, _SEM, _ANY], out_specs=tuple([_HBM] * (2 * nw)),
        input_output_aliases={i: i for i in range(2 * nw)},
        compiler_params=pltpu.CompilerParams(has_side_effects=_DATAFLOW),
    )(*srcs, *lands, send_sems, recv_sems, after)
    return list(outs[:nw]), list(outs[nw:])


def _sum_slabs(landed, own_src, s_arr, after, *, name, tm=512):
    S, h, w = landed.shape
    tm = _tile(h, tm, 16)

    def body(s_ref, *refs):
        own = refs[S][0].astype(F32)
        acc = None
        for s in range(S):
            term = jnp.where(s_ref[0] == s, own, refs[s][0].astype(F32))
            acc = term if acc is None else acc + term
        refs[S + 2][...] = acc

    def slab(s):
        return pl.BlockSpec((1, tm, w), lambda i, s_ref: (jnp.where(s_ref[0] == s, (s + 1) % S, s), i, 0))

    return pl.pallas_call(
        body, name=name,
        grid_spec=pltpu.PrefetchScalarGridSpec(
            num_scalar_prefetch=1, grid=(h // tm,),
            in_specs=[slab(s) for s in range(S)] + [pl.BlockSpec((1, tm, w), lambda i, s_ref: (s_ref[0], i, 0)), _ANY],
            out_specs=pl.BlockSpec((tm, w), lambda i, s_ref: (i, 0))),
        out_shape=jax.ShapeDtypeStruct((h, w), F32),
        compiler_params=_params(("parallel",)),
    )(s_arr, *([landed] * S), own_src, after)


def _half_add(g, recv, c_arr, *, name):
    S, r, w = g.shape
    h = r // 2
    tm = _tile(h, 512, 16)
    nb = h // tm

    def body(c_ref, g_ref, r_ref, o_ref):
        o_ref[...] = (g_ref[...] + r_ref[...]).astype(BF16)

    return pl.pallas_call(
        body, name=name,
        grid_spec=pltpu.PrefetchScalarGridSpec(
            num_scalar_prefetch=1, grid=(S, nb),
            in_specs=[pl.BlockSpec((1, tm, w), lambda s, i, c_ref: (s, c_ref[0] * nb + i, 0)),
                      pl.BlockSpec((1, tm, w), lambda s, i, c_ref: (s, i, 0))],
            out_specs=pl.BlockSpec((1, tm, w), lambda s, i, c_ref: (s, i, 0))),
        out_shape=jax.ShapeDtypeStruct((S, h, w), BF16),
        compiler_params=_params(("parallel", "parallel")),
    )(c_arr, g, recv)


def _layout(sizes, width, part_mult, total_mult):
    offs, rows, r = [], [], 0
    for n in sizes:
        k = -(-n // width)
        offs.append(r)
        rows.append(k)
        r += -(-k // part_mult) * part_mult
    return offs, rows, -(-r // total_mult) * total_mult


def _pack(arrs, width, part_mult, total_mult, dtype, lead=()):
    nl = len(lead)
    sizes = [math.prod(a.shape[nl:]) for a in arrs]
    offs, rows, total = _layout(sizes, width, part_mult, total_mult)
    parts, r = [], 0
    for a, n, o, k in zip(arrs, sizes, offs, rows):
        kp = -(-k // part_mult) * part_mult
        flat = a.reshape(*lead, n).astype(dtype)
        if kp * width > n:
            flat = jnp.pad(flat, [(0, 0)] * nl + [(0, kp * width - n)])
        parts.append(flat.reshape(*lead, kp, width))
        r = o + kp
    if total > r:
        parts.append(jnp.zeros((*lead, total - r, width), dtype))
    return jnp.concatenate(parts, axis=nl)


def _unpack(pool, shapes, width, part_mult, total_mult):
    lead = pool.shape[:-2]
    sizes = [math.prod(s) for s in shapes]
    offs, rows, _ = _layout(sizes, width, part_mult, total_mult)
    out = []
    for s, n, o, k in zip(shapes, sizes, offs, rows):
        flat = lax.slice_in_dim(pool, o, o + k, axis=len(lead)).reshape(*lead, k * width)
        out.append(lax.slice_in_dim(flat, 0, n, axis=len(lead)).reshape(*lead, *s))
    return out


_WEIGHTS = ("c_ctx", "w_ada", "b_ada", "g_pre_mix", "g_post_mix", "g_pre_ffn", "g_post_ffn", "w_in", "b_merge",
            "dn_conv", "dn_a_log", "dn_dt_bias", "dn_onorm", "lru_conv", "lru_conv_b", "lru_w_rg", "lru_b_rg",
            "lru_w_ig", "lru_b_ig", "lru_lambda", "w_branch_dn", "w_branch_lru", "w_out", "w_up", "ffn_dw",
            "ffn_dw_b", "w_down")
_BIG = {"w_ada": True, "w_in": True, "w_branch_dn": False, "w_branch_lru": False, "w_out": False, "w_up": True,
        "w_down": False}
_SMALL_SHARDED = ("dn_conv", "lru_conv", "lru_b_rg", "lru_b_ig", "lru_lambda", "ffn_dw")
_NCHIP = 4
_FLAT_PART = 8
_FLAT_TOTAL = 256


def _from_chip_shards(s, by_cols):
    if by_cols:
        return s.transpose(1, 0, 2).reshape(s.shape[1], _NCHIP * s.shape[2])
    return s.reshape(_NCHIP * s.shape[1], s.shape[2])


def _dsilu(x):
    s = _sigmoid(x)
    return s * (1.0 + x * (1.0 - s))


def kernel(x, c, ctx, c_ctx, w_ada, b_ada, g_pre_mix, g_post_mix, g_pre_ffn, g_post_ffn, w_in, b_merge, dn_conv, dn_a_log, dn_dt_bias, dn_onorm, lru_conv, lru_conv_b, lru_w_rg, lru_b_rg, lru_w_ig, lru_b_ig, lru_lambda, w_branch_dn, w_branch_lru, w_out, w_up, ffn_dw, ffn_dw_b, w_down, loss_target, m_c_ctx, m_w_ada, m_b_ada, m_g_pre_mix, m_g_post_mix, m_g_pre_ffn, m_g_post_ffn, m_w_in, m_b_merge, m_dn_conv, m_dn_a_log, m_dn_dt_bias, m_dn_onorm, m_lru_conv, m_lru_conv_b, m_lru_w_rg, m_lru_b_rg, m_lru_w_ig, m_lru_b_ig, m_lru_lambda, m_w_branch_dn, m_w_branch_lru, m_w_out, m_w_up, m_ffn_dw, m_ffn_dw_b, m_w_down, v_c_ctx, v_w_ada, v_b_ada, v_g_pre_mix, v_g_post_mix, v_g_pre_ffn, v_g_post_ffn, v_w_in, v_b_merge, v_dn_conv, v_dn_a_log, v_dn_dt_bias, v_dn_onorm, v_lru_conv, v_lru_conv_b, v_lru_w_rg, v_lru_b_rg, v_lru_w_ig, v_lru_b_ig, v_lru_lambda, v_w_branch_dn, v_w_branch_lru, v_w_out, v_w_up, v_ffn_dw, v_ffn_dw_b, v_w_down):
    W = dict(zip(_WEIGHTS, (c_ctx, w_ada, b_ada, g_pre_mix, g_post_mix, g_pre_ffn, g_post_ffn, w_in, b_merge, dn_conv,
                            dn_a_log, dn_dt_bias, dn_onorm, lru_conv, lru_conv_b, lru_w_rg, lru_b_rg, lru_w_ig, lru_b_ig,
                            lru_lambda, w_branch_dn, w_branch_lru, w_out, w_up, ffn_dw, ffn_dw_b, w_down)))
    Mo = dict(zip(_WEIGHTS, (m_c_ctx, m_w_ada, m_b_ada, m_g_pre_mix, m_g_post_mix, m_g_pre_ffn, m_g_post_ffn, m_w_in,
                             m_b_merge, m_dn_conv, m_dn_a_log, m_dn_dt_bias, m_dn_onorm, m_lru_conv, m_lru_conv_b,
                             m_lru_w_rg, m_lru_b_rg, m_lru_w_ig, m_lru_b_ig, m_lru_lambda, m_w_branch_dn,
                             m_w_branch_lru, m_w_out, m_w_up, m_ffn_dw, m_ffn_dw_b, m_w_down)))
    Vo = dict(zip(_WEIGHTS, (v_c_ctx, v_w_ada, v_b_ada, v_g_pre_mix, v_g_post_mix, v_g_pre_ffn, v_g_post_ffn, v_w_in,
                             v_b_merge, v_dn_conv, v_dn_a_log, v_dn_dt_bias, v_dn_onorm, v_lru_conv, v_lru_conv_b,
                             v_lru_w_rg, v_lru_b_rg, v_lru_w_ig, v_lru_b_ig, v_lru_lambda, v_w_branch_dn,
                             v_w_branch_lru, v_w_out, v_w_up, v_ffn_dw, v_ffn_dw_b, v_w_down)))
    B, N, D = x.shape
    NC = ctx.shape[1]
    T = NC + N
    H, HD = dn_a_log.shape[-1], dn_onorm.shape[-1]
    DNW = H * HD
    LW, LBD = lru_conv_b.shape[-1], lru_w_rg.shape[-1]
    DFF = ffn_dw_b.shape[-1]
    LC = LANES
    x_i, y_i, c_i = _me()
    s_me = 2 * x_i + y_i
    tm = _tile(math.gcd(NC, N), 256, 16)

    def whole(n, g):
        r, w_ = W[n].shape[1:]
        return g.reshape(_NCHIP, r, w_) if _BIG[n] else g.reshape(_NCHIP * r, w_)

    first = ("w_ada", "w_in")
    later = tuple(n for n in _BIG if n not in first)
    shard16 = {n: W[n][0].astype(BF16) for n in _BIG}
    full = {n: whole(n, g) for n, g in zip(first, _allgather_halves([shard16[n] for n in first], name="allgather_first"))}

    small_local = [W[n][0].reshape(-1, W[n].shape[-1]) for n in _SMALL_SHARDED]
    small_shapes = [a.shape for a in small_local]
    spack = _pack(small_local, LANES, _FLAT_PART, _FLAT_PART, F32)
    sgath = _allgather_small(spack)[0::2]
    sfull = {n: _from_chip_shards(s, True)
             for n, s in zip(_SMALL_SHARDED, _unpack(sgath, small_shapes, LANES, _FLAT_PART, _FLAT_PART))}

    later16, sgath = lax.optimization_barrier(([shard16[n] for n in later], sgath))
    ag_send, ag_recv, ag_src, ag_land, ag_token = _allgather_start(later16, name="ag_start")

    o_a = 4 * DNW
    o_xl = o_a + 4 * H
    o_mg = o_xl + 2 * LW
    wi_ = _from_chip_shards(full["w_in"], True)
    nj = LW // LC
    lru_cols = jnp.stack([wi_[:, o_xl:o_xl + LW].reshape(D, nj, LC), wi_[:, o_xl + LW:o_mg].reshape(D, nj, LC)],
                         axis=2).reshape(D, 2 * LW)
    wp = jnp.concatenate([wi_[:, :o_a], lru_cols, wi_[:, o_mg:], wi_[:, o_a:o_xl],
                          jnp.zeros((D, LANES - 4 * H), BF16)], axis=1)
    p_lru, p_mg, p_ab = 4 * DNW, 4 * DNW + 2 * LW, 4 * DNW + 2 * LW + 2 * D
    PW = p_ab + LANES

    MR = LANES
    cond = jnp.concatenate([c, c_ctx[None], jnp.zeros((MR - B - 1, D), F32)], axis=0)
    silu_rows = _rowwise(lambda a: (_silu(a),), [cond], [F32], name="cond_silu")[0]
    mod = _matmul(silu_rows, full["w_ada"], b_shards=(0, _NCHIP), name="ada_fwd") + b_ada + ag_token[0, 0]
    mx = mod[:B].reshape(B, 6, D)
    mc = mod[B].reshape(6, D)
    zero = jnp.zeros((B, D), F32)
    tab = jnp.stack([jnp.stack([jnp.broadcast_to(mc[0], (B, D)), jnp.broadcast_to(mc[1], (B, D))] + [zero] * 6, axis=1),
                     jnp.stack([mx[:, 0], mx[:, 1]] + [zero] * 6, axis=1)], axis=1)
    vecs = jnp.stack([mx[:, 2], mx[:, 3], mx[:, 4], mx[:, 5]] + [zero] * 4, axis=1)
    gains = jnp.concatenate([g_post_mix, g_pre_ffn, g_post_ffn, jnp.zeros((5, D), F32)], axis=0)

    u = _premix_fwd(ctx, x, g_pre_mix, tab, tm=tm)
    p = _matmul(u, wp, name="in_fwd")
    dkw = dict(B=B, T=T, nc=NC, H=H, HD=HD)
    qkv = _dnprep_fwd(p, sfull["dn_conv"], **dkw)
    prm = jnp.concatenate([
        jnp.concatenate([dn_a_log.reshape(1, 2 * H), jnp.zeros((1, LANES - 2 * H), F32)], axis=1),
        jnp.concatenate([dn_dt_bias.reshape(1, 2 * H), jnp.zeros((1, LANES - 2 * H), F32)], axis=1),
        jnp.zeros((6, LANES), F32)], axis=0)
    gtm = _tile(B * T, 512, 16)
    gb = _gb_fwd(p, prm, rows=B * T, col0=p_ab, H=H, tm=gtm)
    y_dn, o_dn, *dn_res = _delta_fwd(qkv, gb, p, dn_onorm, **dkw)
    lv = jnp.concatenate([lru_conv_b, sfull["lru_b_rg"], sfull["lru_b_ig"], sfull["lru_lambda"], jnp.zeros((1, LW), F32)], axis=0)
    wr = _blockdiag(lru_w_rg[0], LC).astype(BF16)
    wi = _blockdiag(lru_w_ig[0], LC).astype(BF16)
    lkw = dict(B=B, T=T, nc=NC, LW=LW, col0=p_lru, C=LC)
    y_lru = _lru_fwd(p, sfull["lru_conv"], lv, wr, wi, **lkw)
    ag_src, ag_land = _allgather_wait(ag_send, ag_recv, ag_src, ag_land, y_lru, name="ag_wait")
    me_piece = 4 * x_i + 2 * y_i + c_i
    for n, src, land in zip(later, ag_src, _pass_to_sibling(ag_land, name="ag_pass")):
        own = lax.dynamic_slice_in_dim(src, c_i * (src.shape[0] // 2), src.shape[0] // 2, axis=0)
        full[n] = whole(n, lax.dynamic_update_index_in_dim(land, own, me_piece, axis=0))
    Ydn = _matmul(y_dn, full["w_branch_dn"], name="bdn_fwd")
    Ylru = _matmul(y_lru, full["w_branch_lru"], name="blru_fwd")
    mkw = dict(B=B, T=T, nc=NC, D=D, col0=p_mg, tm=tm)
    mixin = _merge_fwd(p, Ydn, Ylru, b_merge, **mkw)
    mix = _matmul(mixin, full["w_out"], name="out_fwd")
    h1, u2 = _post_fwd(x, mix, gains, vecs, tm=tm)
    F = _matmul(u2, full["w_up"], b_shards=(0, _NCHIP), name="up_fwd")
    w9 = sfull["ffn_dw"]
    ftc = _tile(DFF, 256)
    f, f_pre = _ffn_act_fwd(F, w9, ffn_dw_b, B=B, N=N, DFF=DFF, tc=ftc)
    dn = _matmul(f, full["w_down"], name="down_fwd")
    ddn, dout, sums_f = _final(h1, dn, loss_target, gains, vecs, tm=tm)

    G = {}
    df = _matmul(ddn, full["w_down"], tb=True, name="down_bwd_x")
    G["w_down"] = _matmul(f, ddn, ta=True, name="down_bwd_w")
    dFg, dFv, dwb = _ffn_act_bwd(F, f_pre, w9, df, B=B, N=N, DFF=DFF, tc=ftc)
    hs = _NCHIP // 2
    du2 = _matmul(dFg, full["w_up"], tb=True, b_shards=(0, hs), name="up_bwd_xg")
    du2 = _matmul(dFv, full["w_up"], tb=True, b_shards=(hs, hs), add=du2, name="up_bwd_xv")
    gup = _matmul(u2, dFg, ta=True, out_shards=hs, into=(lax.empty((_NCHIP, D, DFF // hs), F32), 0), name="up_bwd_wg")
    G["w_up"] = _matmul(u2, dFv, ta=True, out_shards=hs, into=(gup, hs), name="up_bwd_wv")
    dx1, dmix, sums_p = _post_bwd(x, mix, gains, vecs, dout, du2, tm=tm)
    dmixin = _matmul(dmix, full["w_out"], tb=True, name="out_bwd_x")
    G["w_out"] = _matmul(mixin, dmix, ta=True, name="out_bwd_w")
    dp = _zero_context_cols(lax.empty((B * T, PW), BF16), B=B, T=T, nc=NC, col0=p_mg, width=2 * D, tm=tm)
    dYdn, dYlru, dp, sums_m = _merge_bwd(p, Ydn, Ylru, b_merge, dmixin, dp, **mkw)
    dy_dn = _matmul(dYdn, full["w_branch_dn"], tb=True, name="bdn_bwd_x")
    G["w_branch_dn"] = _matmul(y_dn, dYdn, ta=True, name="bdn_bwd_w")
    dy_lru = _matmul(dYlru, full["w_branch_lru"], tb=True, name="blru_bwd_x")
    G["w_branch_lru"] = _matmul(y_lru, dYlru, ta=True, name="blru_bwd_w")

    c_arr = c_i.astype(jnp.int32).reshape(1)
    s_arr = s_me.astype(jnp.int32).reshape(1)

    def chip_sums(names, tag):
        slabs = [G[n] if _BIG[n] else G[n].reshape(_NCHIP, G[n].shape[0] // _NCHIP, G[n].shape[1]) for n in names]
        from_sibling = _sibling_send_halves(slabs, name="rs_sibling_" + tag)
        return [_half_add(g, r, c_arr, name="rs_add_" + n) for n, g, r in zip(names, slabs, from_sibling)]

    early = tuple(n for n in _BIG if n in G)
    cx_send, cx_recv, cx_src, cx_land, cx_token = _chip_exchange_start(chip_sums(early, "early"), name="cx_start")
    dp, dcw_l, dlv, dwr, dwi = _lru_bwd(p, sfull["lru_conv"], lv + cx_token[0, 0], wr, wi, dy_lru, dp, **lkw)
    dqkv, dgb, dp, don = _delta_bwd(qkv, gb, p, dn_onorm, o_dn, dn_res, dy_dn, dp, **dkw)
    dp, dprm = _gb_bwd(p, prm, dgb, dp, rows=B * T, col0=p_ab, H=H, tm=gtm)
    dp, dcw_d = _dnprep_bwd(p, sfull["dn_conv"], dqkv, dp, **dkw)
    dwp = _matmul(u, dp, ta=True, name="in_bwd_w")
    segs = [(0, o_a, 0), (o_a, 4 * H, p_ab)]
    segs += [(o_xl + i * LC, LC, p_lru + 2 * i * LC) for i in range(nj)]
    segs += [(o_xl + LW + i * LC, LC, p_lru + (2 * i + 1) * LC) for i in range(nj)]
    segs.append((o_mg, 2 * D, p_mg))
    n_in = W["w_in"].shape[-1]
    slabs_in = []
    for s in range(_NCHIP):
        lo, hi = s * n_in, (s + 1) * n_in
        parts = [dwp[:, q0 + max(lo, c0) - c0:q0 + min(hi, c0 + ln) - c0] for c0, ln, q0 in segs if max(lo, c0) < min(hi, c0 + ln)]
        slabs_in.append(jnp.concatenate(parts, axis=1))
    G["w_in"] = jnp.stack(slabs_in)
    wi_send, wi_recv, wi_src, wi_land, wi_token = _chip_exchange_start(chip_sums(("w_in",), "w_in"), name="cx_in_start")
    dU = _matmul(dp, wp, tb=True, after=wi_token, name="in_bwd_x")
    grad_x, sums_pm = _premix_bwd(ctx, x, g_pre_mix, tab, dU, dx1, tm=tm)

    dmod_x = jnp.stack([sums_pm[:, 1, 0], sums_pm[:, 1, 1], sums_p[:, 0], sums_p[:, 1], sums_p[:, 2], sums_f[:, 0]],
                       axis=1).reshape(B, 6 * D)
    dmod_c = jnp.concatenate([sums_pm[:, 0, 0].sum(0), sums_pm[:, 0, 1].sum(0), jnp.zeros((4 * D,), F32)])[None]
    dmod = jnp.concatenate([dmod_x, dmod_c, jnp.zeros((MR - B - 1, 6 * D), F32)], axis=0)
    G["w_ada"] = _matmul(silu_rows, dmod, ta=True, out_shards=_NCHIP, name="ada_bwd_w")
    dsilu = _matmul(dmod, full["w_ada"], tb=True, b_shards=(0, _NCHIP), name="ada_bwd_x")

    g_small = {
        "c_ctx": dsilu[B] * _dsilu(c_ctx),
        "b_ada": dmod[:B + 1].sum(0)[None],
        "g_pre_mix": sums_pm[:, :, 2].sum((0, 1))[None],
        "g_post_mix": sums_p[:, 3].sum(0)[None],
        "g_pre_ffn": sums_p[:, 4].sum(0)[None],
        "g_post_ffn": sums_f[:, 1].sum(0)[None],
        "b_merge": sums_m[0:1],
        "dn_conv": dcw_d[0:4][None],
        "dn_a_log": dprm[0, :2 * H].reshape(1, 2, H),
        "dn_dt_bias": dprm[1, :2 * H].reshape(1, 2, H),
        "dn_onorm": don[:, 0].sum(0)[None],
        "lru_conv": dcw_l[0:4][None],
        "lru_conv_b": dlv[0:1],
        "lru_w_rg": _blockdiag_extract(dwr, LBD)[None],
        "lru_b_rg": dlv[1:3][None],
        "lru_w_ig": _blockdiag_extract(dwi, LBD)[None],
        "lru_b_ig": dlv[3:5][None],
        "lru_lambda": dlv[5:7][None],
        "ffn_dw": dwb[0:9].reshape(1, 3, 3, DFF),
        "ffn_dw_b": dwb[9:10],
    }
    small_names = tuple(n for n in _WEIGHTS if n not in _BIG)
    loss_part = sums_f[:, 2].sum().reshape(1)
    gs_list = [g_small[n] for n in small_names] + [loss_part]
    gs_shapes = [a.shape for a in gs_list]
    gpack = _pack(gs_list, LANES, _FLAT_PART, _FLAT_TOTAL, F32)
    gsum = _sum_lead(_allgather_small(gpack), name="small_sum", tm=512, mult=SUBLANES)
    gs_red = dict(zip(small_names + ("loss",), _unpack(gsum, gs_shapes, LANES, _FLAT_PART, _FLAT_TOTAL)))
    loss = gs_red["loss"][0]

    grads, deltas, new_m, new_v = {}, {}, {}, {}

    def finish(names, lands, srcs, after, tag):
        halves = [_sum_slabs(l, src, s_arr, after, name="rs_sum_" + n) for n, l, src in zip(names, lands, srcs)]
        outs = None
        for n, own, sib in zip(names, halves, _sibling_swap(halves, name="rs_gather_" + tag)):
            shp = W[n].shape
            outs = _adamw_halves(W[n][0], own, sib, Mo[n][0], Vo[n][0], c_arr, name="adamw_" + n)
            grads[n], deltas[n], new_m[n], new_v[n] = (o.reshape(shp) for o in outs)
        return outs[1]

    cx_src, cx_land = _chip_exchange_wait(cx_send, cx_recv, cx_src, cx_land, dsilu, name="cx_wait")
    ada_sums, gsum = lax.optimization_barrier((chip_sums(("w_ada",), "w_ada"), gsum))
    ad_send, ad_recv, ad_src, ad_land, ad_token = _chip_exchange_start(ada_sums, name="cx_ada_start")
    last_early = finish(early, cx_land, cx_src, ad_token, "early")
    wi_src, wi_land = _chip_exchange_wait(wi_send, wi_recv, wi_src, wi_land, last_early, name="cx_in_wait")
    last_in = finish(("w_in",), wi_land, wi_src, last_early, "w_in")
    ad_src, ad_land = _chip_exchange_wait(ad_send, ad_recv, ad_src, ad_land, last_in, name="cx_ada_wait")
    finish(("w_ada",), ad_land, ad_src, last_in, "w_ada")
    for n in small_names:
        g = gs_red[n]
        if n in _SMALL_SHARDED:
            k = W[n].shape[-1]
            g = lax.dynamic_slice_in_dim(g, s_me * k, k, axis=g.ndim - 1)
        grads[n] = g.reshape(W[n].shape)
    sm_shapes = [W[n].shape for n in small_names]
    pk = lambda d: _pack([d[n] for n in small_names], LANES, _FLAT_PART, _FLAT_TOTAL, F32)
    d_, m_, v_ = _adamw(pk(W), pk(grads), pk(Mo), pk(Vo), name="adamw_small")
    for dst, pool_ in ((deltas, d_), (new_m, m_), (new_v, v_)):
        dst.update(zip(small_names, _unpack(pool_, sm_shapes, LANES, _FLAT_PART, _FLAT_TOTAL)))
    return (loss, grad_x, *[grads[n] for n in _WEIGHTS], *[deltas[n] for n in _WEIGHTS],
            *[new_m[n] for n in _WEIGHTS], *[new_v[n] for n in _WEIGHTS])
```
